```python
import math
import jax, jax.numpy as jnp
from jax import lax
import numpy as np

D_MODEL = 1024
BATCH = 16
SEQ = 2048
DEPTH = 4

D_CONV = D_MODEL
CONV_K = 3
D_SSM = D_MODEL
SSM_HEAD_DIM = 64
SSM_HEADS = D_SSM // SSM_HEAD_DIM
SSM_GROUPS = 2
HEADS_PER_GROUP = SSM_HEADS // SSM_GROUPS
D_STATE = 128
SSM_CONV_K = 4
CHUNK = 128
SSM_CONV_DIM = D_SSM + 2 * SSM_GROUPS * D_STATE
D_MIX = D_CONV + D_SSM
IN_COLS = 3 * D_CONV + D_SSM + SSM_CONV_DIM + SSM_HEADS
D_FF = 4 * D_MODEL
EPS = 1e-6

kernel_name = "hybrid_shortconv_ssd_parallel_groups"


def _rmsnorm(x, g):
    xf = x.astype(jnp.float32)
    y = xf * lax.rsqrt(jnp.mean(xf * xf, axis=-1, keepdims=True) + EPS)
    return (y * g.astype(jnp.float32)).astype(x.dtype)


def _causal_depthwise_conv(u, w):
    k, ch = w.shape
    return lax.conv_general_dilated(
        u, w[:, None, :].astype(u.dtype), window_strides=(1,), padding=[(k - 1, 0)],
        dimension_numbers=("NWC", "WIO", "NWC"), feature_group_count=ch)


def _ssd_chunked(x, dt, a, b, c):
    bsz, t = x.shape[:2]
    nc = t // CHUNK
    g, e, p, n = SSM_GROUPS, HEADS_PER_GROUP, SSM_HEAD_DIM, D_STATE
    xdt = (x * dt[..., None]).reshape(bsz, nc, CHUNK, g, e, p)
    adt = (dt * a).reshape(bsz, nc, CHUNK, g, e).transpose(0, 1, 3, 4, 2)
    bc = b.reshape(bsz, nc, CHUNK, g, n)
    cc = c.reshape(bsz, nc, CHUNK, g, n)
    cs = jnp.cumsum(adt, axis=-1)
    mask = jnp.tril(jnp.ones((CHUNK, CHUNK), dtype=bool))
    seg = jnp.where(mask, cs[..., :, None] - cs[..., None, :], -jnp.inf)
    decay_ls = jnp.exp(seg)
    scores = jnp.einsum("bclgn,bcsgn->bcgls", cc, bc)
    m = scores[:, :, :, None] * decay_ls
    y_diag = jnp.einsum("bcgels,bcsgep->bclgep", m, xdt)
    decay_to_end = jnp.exp(cs[..., -1:] - cs)
    states = jnp.einsum("bclgn,bcgel,bclgep->bcgepn", bc, decay_to_end, xdt)
    chunk_decay = jnp.exp(cs[..., -1])

    def step(carry, inp):
        s_c, d_c = inp
        return carry * d_c[..., None, None] + s_c, carry

    init = jnp.zeros((bsz, g, e, p, n), jnp.float32)
    _, prev = lax.scan(step, init, (jnp.moveaxis(states, 1, 0), jnp.moveaxis(chunk_decay, 1, 0)))
    prev = jnp.moveaxis(prev, 0, 1)
    y_off = jnp.einsum("bclgn,bcgepn,bcgel->bclgep", cc, prev, jnp.exp(cs))
    return (y_diag + y_off).reshape(bsz, t, SSM_HEADS, p)


def _fwd_setup_inputs(seed: int = 0) -> dict:
    key = jax.random.key(seed)
    ks = jax.random.split(key, 20)
    f32 = jnp.float32
    nrm = lambda k, s, scale: jax.random.normal(k, s, f32) * scale
    gain = lambda k, s: 1.0 + 0.02 * jax.random.normal(k, s, f32)
    x = jax.random.normal(ks[0], (BATCH, SEQ, D_MODEL), f32)
    dt_min, dt_max = 1e-3, 1e-1
    u = jax.random.uniform(ks[6], (DEPTH, SSM_HEADS), f32)
    dt0 = jnp.exp(u * (math.log(dt_max) - math.log(dt_min)) + math.log(dt_min))
    dt_bias = dt0 + jnp.log(-jnp.expm1(-dt0))
    a_log = jnp.log(jax.random.uniform(ks[7], (DEPTH, SSM_HEADS), f32, 1.0, 16.0))
    return {
        "x": x,
        "norm_mix_pre": gain(ks[1], (DEPTH, D_MODEL)),
        "w_in": nrm(ks[2], (DEPTH, D_MODEL, IN_COLS), D_MODEL ** -0.5),
        "conv_a_w": nrm(ks[3], (DEPTH, CONV_K, D_CONV), CONV_K ** -0.5),
        "ssm_conv_w": nrm(ks[4], (DEPTH, SSM_CONV_K, SSM_CONV_DIM), SSM_CONV_K ** -0.5),
        "ssm_conv_b": nrm(ks[5], (DEPTH, SSM_CONV_DIM), 0.02),
        "dt_bias": dt_bias,
        "a_log": a_log,
        "d_skip": gain(ks[8], (DEPTH, SSM_HEADS)),
        "conv_out_norm": gain(ks[9], (DEPTH, D_CONV)),
        "ssm_out_norm": gain(ks[10], (DEPTH, D_SSM)),
        "w_out": nrm(ks[11], (DEPTH, D_MIX, D_MODEL), D_MIX ** -0.5),
        "norm_mix_post": gain(ks[12], (DEPTH, D_MODEL)),
        "norm_mlp_pre": gain(ks[13], (DEPTH, D_MODEL)),
        "w_up": nrm(ks[14], (DEPTH, D_MODEL, D_FF), D_MODEL ** -0.5),
        "w_down": nrm(ks[15], (DEPTH, D_FF, D_MODEL), D_FF ** -0.5),
        "norm_mlp_post": gain(ks[16], (DEPTH, D_MODEL)),
    }


def _fwd_reference(x, norm_mix_pre, w_in, conv_a_w, ssm_conv_w, ssm_conv_b, dt_bias, a_log, d_skip,
              conv_out_norm, ssm_out_norm, w_out, norm_mix_post, norm_mlp_pre, w_up, w_down,
              norm_mlp_post):
    bsz, t, _ = x.shape
    split_at = [D_CONV, 2 * D_CONV, 3 * D_CONV, 3 * D_CONV + D_SSM,
                3 * D_CONV + D_SSM + SSM_CONV_DIM]
    for i in range(DEPTH):
        h = _rmsnorm(x, norm_mix_pre[i])
        proj = jnp.einsum("btd,de->bte", h, w_in[i])
        x_a, c_a, b_a, z, xbc, dt_raw = jnp.split(proj, split_at, axis=-1)
        y_a = b_a * _causal_depthwise_conv(c_a * x_a, conv_a_w[i])
        y_a = _rmsnorm(y_a, conv_out_norm[i])
        xbc = _causal_depthwise_conv(xbc, ssm_conv_w[i]) + ssm_conv_b[i].astype(xbc.dtype)
        xbc = jax.nn.silu(xbc)
        xs, bs, cs_ = jnp.split(xbc, [D_SSM, D_SSM + SSM_GROUPS * D_STATE], axis=-1)
        xs = xs.reshape(bsz, t, SSM_HEADS, SSM_HEAD_DIM).astype(jnp.float32)
        bs = bs.reshape(bsz, t, SSM_GROUPS, D_STATE).astype(jnp.float32)
        cs_ = cs_.reshape(bsz, t, SSM_GROUPS, D_STATE).astype(jnp.float32)
        dt = jax.nn.softplus(dt_raw.astype(jnp.float32) + dt_bias[i].astype(jnp.float32))
        a = -jnp.exp(a_log[i].astype(jnp.float32))
        y_s = _ssd_chunked(xs, dt, a, bs, cs_) + d_skip[i].astype(jnp.float32)[:, None] * xs
        y_s = y_s.reshape(bsz, t, D_SSM) * jax.nn.silu(z.astype(jnp.float32))
        y_s = _rmsnorm(y_s.reshape(bsz, t, SSM_GROUPS, D_SSM // SSM_GROUPS),
                       ssm_out_norm[i].reshape(SSM_GROUPS, D_SSM // SSM_GROUPS))
        y_s = y_s.reshape(bsz, t, D_SSM).astype(x.dtype)
        mix = jnp.einsum("bte,ed->btd", jnp.concatenate([y_a, y_s], axis=-1), w_out[i])
        x = x + _rmsnorm(mix, norm_mix_post[i])
        h = _rmsnorm(x, norm_mlp_pre[i])
        f = jnp.square(jax.nn.relu(jnp.einsum("btd,df->btf", h, w_up[i])))
        f = jnp.einsum("btf,fd->btd", f, w_down[i])
        x = x + _rmsnorm(f, norm_mlp_post[i])
    return x


import jax as _jax
import jax.numpy as _jnp

TWIN_FORMAT = 'train_step'
FWD_PARAMS = ['x', 'norm_mix_pre', 'w_in', 'conv_a_w', 'ssm_conv_w', 'ssm_conv_b', 'dt_bias', 'a_log', 'd_skip', 'conv_out_norm', 'ssm_out_norm', 'w_out', 'norm_mix_post', 'norm_mlp_pre', 'w_up', 'w_down', 'norm_mlp_post']
TWIN_WEIGHTS = ['norm_mix_pre', 'w_in', 'conv_a_w', 'ssm_conv_w', 'ssm_conv_b', 'dt_bias', 'a_log', 'd_skip', 'conv_out_norm', 'ssm_out_norm', 'w_out', 'norm_mix_post', 'norm_mlp_pre', 'w_up', 'w_down', 'norm_mlp_post']
TWIN_DIFF_INPUT = 'x'
TWIN_INPUTS = ['x', 'norm_mix_pre', 'w_in', 'conv_a_w', 'ssm_conv_w', 'ssm_conv_b', 'dt_bias', 'a_log', 'd_skip', 'conv_out_norm', 'ssm_out_norm', 'w_out', 'norm_mix_post', 'norm_mlp_pre', 'w_up', 'w_down', 'norm_mlp_post', 'loss_target', 'm_norm_mix_pre', 'm_w_in', 'm_conv_a_w', 'm_ssm_conv_w', 'm_ssm_conv_b', 'm_dt_bias', 'm_a_log', 'm_d_skip', 'm_conv_out_norm', 'm_ssm_out_norm', 'm_w_out', 'm_norm_mix_post', 'm_norm_mlp_pre', 'm_w_up', 'm_w_down', 'm_norm_mlp_post', 'v_norm_mix_pre', 'v_w_in', 'v_conv_a_w', 'v_ssm_conv_w', 'v_ssm_conv_b', 'v_dt_bias', 'v_a_log', 'v_d_skip', 'v_conv_out_norm', 'v_ssm_out_norm', 'v_w_out', 'v_norm_mix_post', 'v_norm_mlp_pre', 'v_w_up', 'v_w_down', 'v_norm_mlp_post']
TWIN_OUTPUTS = ['loss', 'grad_x', 'grad_norm_mix_pre', 'grad_w_in', 'grad_conv_a_w', 'grad_ssm_conv_w', 'grad_ssm_conv_b', 'grad_dt_bias', 'grad_a_log', 'grad_d_skip', 'grad_conv_out_norm', 'grad_ssm_out_norm', 'grad_w_out', 'grad_norm_mix_post', 'grad_norm_mlp_pre', 'grad_w_up', 'grad_w_down', 'grad_norm_mlp_post', 'delta_norm_mix_pre', 'delta_w_in', 'delta_conv_a_w', 'delta_ssm_conv_w', 'delta_ssm_conv_b', 'delta_dt_bias', 'delta_a_log', 'delta_d_skip', 'delta_conv_out_norm', 'delta_ssm_out_norm', 'delta_w_out', 'delta_norm_mix_post', 'delta_norm_mlp_pre', 'delta_w_up', 'delta_w_down', 'delta_norm_mlp_post', 'new_m_norm_mix_pre', 'new_m_w_in', 'new_m_conv_a_w', 'new_m_ssm_conv_w', 'new_m_ssm_conv_b', 'new_m_dt_bias', 'new_m_a_log', 'new_m_d_skip', 'new_m_conv_out_norm', 'new_m_ssm_out_norm', 'new_m_w_out', 'new_m_norm_mix_post', 'new_m_norm_mlp_pre', 'new_m_w_up', 'new_m_w_down', 'new_m_norm_mlp_post', 'new_v_norm_mix_pre', 'new_v_w_in', 'new_v_conv_a_w', 'new_v_ssm_conv_w', 'new_v_ssm_conv_b', 'new_v_dt_bias', 'new_v_a_log', 'new_v_d_skip', 'new_v_conv_out_norm', 'new_v_ssm_out_norm', 'new_v_w_out', 'new_v_norm_mix_post', 'new_v_norm_mlp_pre', 'new_v_w_up', 'new_v_w_down', 'new_v_norm_mlp_post']
TWIN_LEAF_KINDS = {'loss': 'loss', 'grad_x': 'grad_x', 'grad_norm_mix_pre': 'grad_w', 'grad_w_in': 'grad_w', 'grad_conv_a_w': 'grad_w', 'grad_ssm_conv_w': 'grad_w', 'grad_ssm_conv_b': 'grad_w', 'grad_dt_bias': 'grad_w', 'grad_a_log': 'grad_w', 'grad_d_skip': 'grad_w', 'grad_conv_out_norm': 'grad_w', 'grad_ssm_out_norm': 'grad_w', 'grad_w_out': 'grad_w', 'grad_norm_mix_post': 'grad_w', 'grad_norm_mlp_pre': 'grad_w', 'grad_w_up': 'grad_w', 'grad_w_down': 'grad_w', 'grad_norm_mlp_post': 'grad_w', 'delta_norm_mix_pre': 'delta_w', 'delta_w_in': 'delta_w', 'delta_conv_a_w': 'delta_w', 'delta_ssm_conv_w': 'delta_w', 'delta_ssm_conv_b': 'delta_w', 'delta_dt_bias': 'delta_w', 'delta_a_log': 'delta_w', 'delta_d_skip': 'delta_w', 'delta_conv_out_norm': 'delta_w', 'delta_ssm_out_norm': 'delta_w', 'delta_w_out': 'delta_w', 'delta_norm_mix_post': 'delta_w', 'delta_norm_mlp_pre': 'delta_w', 'delta_w_up': 'delta_w', 'delta_w_down': 'delta_w', 'delta_norm_mlp_post': 'delta_w', 'new_m_norm_mix_pre': 'new_m', 'new_m_w_in': 'new_m', 'new_m_conv_a_w': 'new_m', 'new_m_ssm_conv_w': 'new_m', 'new_m_ssm_conv_b': 'new_m', 'new_m_dt_bias': 'new_m', 'new_m_a_log': 'new_m', 'new_m_d_skip': 'new_m', 'new_m_conv_out_norm': 'new_m', 'new_m_ssm_out_norm': 'new_m', 'new_m_w_out': 'new_m', 'new_m_norm_mix_post': 'new_m', 'new_m_norm_mlp_pre': 'new_m', 'new_m_w_up': 'new_m', 'new_m_w_down': 'new_m', 'new_m_norm_mlp_post': 'new_m', 'new_v_norm_mix_pre': 'new_v', 'new_v_w_in': 'new_v', 'new_v_conv_a_w': 'new_v', 'new_v_ssm_conv_w': 'new_v', 'new_v_ssm_conv_b': 'new_v', 'new_v_dt_bias': 'new_v', 'new_v_a_log': 'new_v', 'new_v_d_skip': 'new_v', 'new_v_conv_out_norm': 'new_v', 'new_v_ssm_out_norm': 'new_v', 'new_v_w_out': 'new_v', 'new_v_norm_mix_post': 'new_v', 'new_v_norm_mlp_pre': 'new_v', 'new_v_w_up': 'new_v', 'new_v_w_down': 'new_v', 'new_v_norm_mlp_post': 'new_v'}


def _forward(args):
    return _fwd_reference(*[args[k] for k in FWD_PARAMS])


def _output_shape():
    out = _jax.eval_shape(lambda: _forward(_fwd_setup_inputs(0)))
    return out.shape, out.dtype

N_MICROBATCH = 1
ADAM_LR = 0.001
ADAM_B1 = 0.9
ADAM_B2 = 0.999
ADAM_EPS = 1e-08
ADAM_WD = 0.01
ADAM_STEP = 10
PER_EXAMPLE_BATCH_AXIS = {'x': 0, 'loss_target': 0}
SHARED_INPUTS = []
_WEIGHT_DTYPES = {'norm_mix_pre': _jnp.float32, 'w_in': _jnp.float32, 'conv_a_w': _jnp.float32, 'ssm_conv_w': _jnp.float32, 'ssm_conv_b': _jnp.float32, 'dt_bias': _jnp.float32, 'a_log': _jnp.float32, 'd_skip': _jnp.float32, 'conv_out_norm': _jnp.float32, 'ssm_out_norm': _jnp.float32, 'w_out': _jnp.float32, 'norm_mix_post': _jnp.float32, 'norm_mlp_pre': _jnp.float32, 'w_up': _jnp.float32, 'w_down': _jnp.float32, 'norm_mlp_post': _jnp.float32}
MOMENT_SCALE = {'norm_mix_pre': 3.120787e+00, 'w_in': 1.372841e+00, 'conv_a_w': 9.138962e-01, 'ssm_conv_w': 2.948003e+00, 'ssm_conv_b': 8.503325e+00, 'dt_bias': 2.644504e+00, 'a_log': 2.446995e+01, 'd_skip': 1.246287e+01, 'conv_out_norm': 9.123463e-01, 'ssm_out_norm': 4.918017e+00, 'w_out': 4.850920e+00, 'norm_mix_post': 3.207872e+01, 'norm_mlp_pre': 4.052157e+00, 'w_up': 1.933182e+00, 'w_down': 1.230251e+01, 'norm_mlp_post': 3.583668e+01}


def _to_microbatches(a, axis):
    t = _jnp.moveaxis(a, axis, 0)
    t = t.reshape((N_MICROBATCH, t.shape[0] // N_MICROBATCH) + t.shape[1:])
    return _jnp.moveaxis(t, 1, axis + 1)


def setup_inputs(seed: int = 0) -> dict:
    inp = _fwd_setup_inputs(seed)
    key = _jax.random.fold_in(_jax.random.key(seed), 7919)
    shape, _ = _output_shape()
    out = dict(inp)
    out["loss_target"] = _jax.random.normal(_jax.random.fold_in(key, 0), shape, _jnp.float32)
    for i, name in enumerate(TWIN_WEIGHTS):
        w = inp[name].astype(_jnp.float32)
        if MOMENT_SCALE is None:
            s = _jnp.sqrt(_jnp.mean(_jnp.square(w)) + 1e-30)
        else:
            s = MOMENT_SCALE[name]
        km, kv = _jax.random.split(_jax.random.fold_in(key, i + 1))
        out[name] = w
        out["m_" + name] = s * _jax.random.normal(km, w.shape, _jnp.float32)
        out["v_" + name] = (s * s) * _jax.random.uniform(kv, w.shape, _jnp.float32, 0.5, 1.5)
    if N_MICROBATCH > 1:
        for name, axis in PER_EXAMPLE_BATCH_AXIS.items():
            out[name] = _to_microbatches(out[name], axis)
    return {'x': out['x'], 'norm_mix_pre': out['norm_mix_pre'], 'w_in': out['w_in'], 'conv_a_w': out['conv_a_w'], 'ssm_conv_w': out['ssm_conv_w'], 'ssm_conv_b': out['ssm_conv_b'], 'dt_bias': out['dt_bias'], 'a_log': out['a_log'], 'd_skip': out['d_skip'], 'conv_out_norm': out['conv_out_norm'], 'ssm_out_norm': out['ssm_out_norm'], 'w_out': out['w_out'], 'norm_mix_post': out['norm_mix_post'], 'norm_mlp_pre': out['norm_mlp_pre'], 'w_up': out['w_up'], 'w_down': out['w_down'], 'norm_mlp_post': out['norm_mlp_post'], 'loss_target': out['loss_target'], 'm_norm_mix_pre': out['m_norm_mix_pre'], 'm_w_in': out['m_w_in'], 'm_conv_a_w': out['m_conv_a_w'], 'm_ssm_conv_w': out['m_ssm_conv_w'], 'm_ssm_conv_b': out['m_ssm_conv_b'], 'm_dt_bias': out['m_dt_bias'], 'm_a_log': out['m_a_log'], 'm_d_skip': out['m_d_skip'], 'm_conv_out_norm': out['m_conv_out_norm'], 'm_ssm_out_norm': out['m_ssm_out_norm'], 'm_w_out': out['m_w_out'], 'm_norm_mix_post': out['m_norm_mix_post'], 'm_norm_mlp_pre': out['m_norm_mlp_pre'], 'm_w_up': out['m_w_up'], 'm_w_down': out['m_w_down'], 'm_norm_mlp_post': out['m_norm_mlp_post'], 'v_norm_mix_pre': out['v_norm_mix_pre'], 'v_w_in': out['v_w_in'], 'v_conv_a_w': out['v_conv_a_w'], 'v_ssm_conv_w': out['v_ssm_conv_w'], 'v_ssm_conv_b': out['v_ssm_conv_b'], 'v_dt_bias': out['v_dt_bias'], 'v_a_log': out['v_a_log'], 'v_d_skip': out['v_d_skip'], 'v_conv_out_norm': out['v_conv_out_norm'], 'v_ssm_out_norm': out['v_ssm_out_norm'], 'v_w_out': out['v_w_out'], 'v_norm_mix_post': out['v_norm_mix_post'], 'v_norm_mlp_pre': out['v_norm_mlp_pre'], 'v_w_up': out['v_w_up'], 'v_w_down': out['v_w_down'], 'v_norm_mlp_post': out['v_norm_mlp_post']}


def _loss(weights, diff, rest, loss_target):
    with _jax.named_scope("forward"):
        args = {**rest, TWIN_DIFF_INPUT: diff, **{k: w.astype(_WEIGHT_DTYPES[k]) for k, w in weights.items()}}
        y = _forward(args)
    with _jax.named_scope("loss_head"):
        err = _jnp.square(y.astype(_jnp.float32) - loss_target)
        return 0.5 * _jnp.sum(_jnp.mean(err, axis=-1)) if err.ndim else 0.5 * err


def _adamw(w, g, m, v):
    m = ADAM_B1 * m + (1.0 - ADAM_B1) * g
    v = ADAM_B2 * v + (1.0 - ADAM_B2) * _jnp.square(g)
    m_hat = m / (1.0 - ADAM_B1 ** ADAM_STEP)
    v_hat = v / (1.0 - ADAM_B2 ** ADAM_STEP)
    delta = -ADAM_LR * (m_hat / (_jnp.sqrt(v_hat) + ADAM_EPS) + ADAM_WD * w)
    return delta, m, v


def reference(x, norm_mix_pre, w_in, conv_a_w, ssm_conv_w, ssm_conv_b, dt_bias, a_log, d_skip, conv_out_norm, ssm_out_norm, w_out, norm_mix_post, norm_mlp_pre, w_up, w_down, norm_mlp_post, loss_target, m_norm_mix_pre, m_w_in, m_conv_a_w, m_ssm_conv_w, m_ssm_conv_b, m_dt_bias, m_a_log, m_d_skip, m_conv_out_norm, m_ssm_out_norm, m_w_out, m_norm_mix_post, m_norm_mlp_pre, m_w_up, m_w_down, m_norm_mlp_post, v_norm_mix_pre, v_w_in, v_conv_a_w, v_ssm_conv_w, v_ssm_conv_b, v_dt_bias, v_a_log, v_d_skip, v_conv_out_norm, v_ssm_out_norm, v_w_out, v_norm_mix_post, v_norm_mlp_pre, v_w_up, v_w_down, v_norm_mlp_post):
    given = dict(x=x, norm_mix_pre=norm_mix_pre, w_in=w_in, conv_a_w=conv_a_w, ssm_conv_w=ssm_conv_w, ssm_conv_b=ssm_conv_b, dt_bias=dt_bias, a_log=a_log, d_skip=d_skip, conv_out_norm=conv_out_norm, ssm_out_norm=ssm_out_norm, w_out=w_out, norm_mix_post=norm_mix_post, norm_mlp_pre=norm_mlp_pre, w_up=w_up, w_down=w_down, norm_mlp_post=norm_mlp_post, loss_target=loss_target, m_norm_mix_pre=m_norm_mix_pre, m_w_in=m_w_in, m_conv_a_w=m_conv_a_w, m_ssm_conv_w=m_ssm_conv_w, m_ssm_conv_b=m_ssm_conv_b, m_dt_bias=m_dt_bias, m_a_log=m_a_log, m_d_skip=m_d_skip, m_conv_out_norm=m_conv_out_norm, m_ssm_out_norm=m_ssm_out_norm, m_w_out=m_w_out, m_norm_mix_post=m_norm_mix_post, m_norm_mlp_pre=m_norm_mlp_pre, m_w_up=m_w_up, m_w_down=m_w_down, m_norm_mlp_post=m_norm_mlp_post, v_norm_mix_pre=v_norm_mix_pre, v_w_in=v_w_in, v_conv_a_w=v_conv_a_w, v_ssm_conv_w=v_ssm_conv_w, v_ssm_conv_b=v_ssm_conv_b, v_dt_bias=v_dt_bias, v_a_log=v_a_log, v_d_skip=v_d_skip, v_conv_out_norm=v_conv_out_norm, v_ssm_out_norm=v_ssm_out_norm, v_w_out=v_w_out, v_norm_mix_post=v_norm_mix_post, v_norm_mlp_pre=v_norm_mlp_pre, v_w_up=v_w_up, v_w_down=v_w_down, v_norm_mlp_post=v_norm_mlp_post)
    weights = {n: given[n] for n in TWIN_WEIGHTS}
    shared = {n: given[n] for n in SHARED_INPUTS}
    per_example = {n: given[n] for n in ['x']}
    grad_fn = _jax.value_and_grad(_loss, argnums=(0, 1))

    def one_microbatch(ex, loss_target):
        ex = dict(ex)
        diff = ex.pop(TWIN_DIFF_INPUT)
        return grad_fn(weights, diff, {**shared, **ex}, loss_target)

    if N_MICROBATCH == 1:
        loss, (grad_w, grad_x) = one_microbatch(per_example, given["loss_target"])
    else:
        def body(carry, xs):
            loss_sum, grad_sum = carry
            l_k, (gw_k, gx_k) = one_microbatch(xs[0], xs[1])
            with _jax.named_scope("update"):
                return (loss_sum + l_k, _jax.tree.map(_jnp.add, grad_sum, gw_k)), gx_k

        init = (_jnp.zeros((), _jnp.float32), _jax.tree.map(_jnp.zeros_like, weights))
        (loss, grad_w), grad_x = _jax.lax.scan(body, init, (per_example, given["loss_target"]))
    with _jax.named_scope("update"):
        delta_w, new_m, new_v = {}, {}, {}
        for n in TWIN_WEIGHTS:
            delta_w[n], new_m[n], new_v[n] = _adamw(weights[n], grad_w[n], given["m_" + n], given["v_" + n])
    return (loss, grad_x, *[grad_w[n] for n in TWIN_WEIGHTS], *[delta_w[n] for n in TWIN_WEIGHTS],
            *[new_m[n] for n in TWIN_WEIGHTS], *[new_v[n] for n in TWIN_WEIGHTS])
```

```python
import functools

import jax
import jax.numpy as jnp
from jax import lax
from jax.experimental import pallas as pl
from jax.experimental.pallas import tpu as pltpu

f32, bf16 = jnp.float32, jnp.bfloat16

D = 1024
NH, HP = 16, 64
NG, NS = 2, 128
CH = 128
XBC = D + 2 * NG * NS
DFF = 4 * D
IN_COLS = 3 * D + D + XBC + NH
PROJ = 5760
COL_Z, COL_XBC, COL_DT = 3 * D, 4 * D, 4 * D + XBC
EPS = 1e-6
HALO = 8
VMEM_LIMIT = 56 * 2**20
MESH = pl.DeviceIdType.MESH

LR, B1, B2, AEPS, WD, STEP = 0.001, 0.9, 0.999, 1e-08, 0.01, 10


def _cparams(n_axes):
    return pltpu.CompilerParams(dimension_semantics=("arbitrary",) * n_axes, vmem_limit_bytes=VMEM_LIMIT)


def _sds(shape, dtype):
    return jax.ShapeDtypeStruct(tuple(shape), dtype)


def _rms_fwd(x, g):
    r = lax.rsqrt(jnp.mean(x * x, axis=-1, keepdims=True) + EPS)
    return x * r * g


def _rms_bwd(x, g, dy):
    r = lax.rsqrt(jnp.mean(x * x, axis=-1, keepdims=True) + EPS)
    xh = x * r
    gdy = dy * g
    dx = r * (gdy - xh * jnp.mean(xh * gdy, axis=-1, keepdims=True))
    return dx, dy * xh


def _accum(ref, part, first):
    @pl.when(first)
    def _():
        ref[...] = part

    @pl.when(jnp.logical_not(first))
    def _():
        ref[...] += part


def _dot_nt(a, b):
    return lax.dot_general(a, b, (((1,), (1,)), ((), ())), preferred_element_type=f32)


def _dot_tn(a, b):
    return lax.dot_general(a, b, (((0,), (0,)), ((), ())), preferred_element_type=f32)


def _dot(a, b):
    return jnp.dot(a, b, preferred_element_type=f32)


def _split_dot(x, e_bf, n_split, nt=False):
    acc = None
    rem = x
    for s in range(n_split):
        hi = rem.astype(bf16)
        term = _dot_nt(hi, e_bf) if nt else _dot(hi, e_bf)
        acc = term if acc is None else acc + term
        if s + 1 < n_split:
            rem = rem - hi.astype(f32)
    return acc


def _sigmoid(x):
    return 1.0 / (1.0 + jnp.exp(-x))


def norm_matmul(x, g, w, tm, tn, out_dtype, name):
    t = x.shape[0]
    if w.ndim == 3:
        assert w.shape[2] == tn
        n = w.shape[0] * tn
        w_spec = pl.BlockSpec((None, D, tn), lambda i, j: (j, 0, 0))
    else:
        n = w.shape[1]
        w_spec = pl.BlockSpec((D, tn), lambda i, j: (0, j))

    def body(x_ref, g_ref, w_ref, o_ref, h_ref):
        @pl.when(pl.program_id(1) == 0)
        def _():
            h_ref[...] = _rms_fwd(x_ref[...], g_ref[...]).astype(bf16)

        o_ref[...] = _dot(h_ref[...], w_ref[...]).astype(out_dtype)

    return pl.pallas_call(
        body, name=name, grid=(t // tm, n // tn),
        in_specs=[pl.BlockSpec((tm, D), lambda i, j: (i, 0)), pl.BlockSpec((1, D), lambda i, j: (0, 0)), w_spec],
        out_specs=[pl.BlockSpec((tm, tn), lambda i, j: (i, j)), pl.BlockSpec((tm, D), lambda i, j: (i, 0))],
        out_shape=[_sds((t, n), out_dtype), _sds((t, D), bf16)],
        compiler_params=_cparams(2))(x, g, w)


def matmul_postnorm(a, w, xres, g, tm, relu2, name):
    t, k = a.shape

    def body(a_ref, w_ref, xr_ref, g_ref, y_ref, xo_ref):
        av = a_ref[...]
        if relu2:
            af = jnp.maximum(av.astype(f32), 0.0)
            av = (af * af).astype(bf16)
        y = _dot(av, w_ref[...])
        y_ref[...] = y
        xo_ref[...] = xr_ref[...] + _rms_fwd(y, g_ref[...])

    return pl.pallas_call(
        body, name=name, grid=(t // tm,),
        in_specs=[pl.BlockSpec((tm, k), lambda i: (i, 0)), pl.BlockSpec((k, D), lambda i: (0, 0)),
                  pl.BlockSpec((tm, D), lambda i: (i, 0)), pl.BlockSpec((1, D), lambda i: (0, 0))],
        out_specs=[pl.BlockSpec((tm, D), lambda i: (i, 0)), pl.BlockSpec((tm, D), lambda i: (i, 0))],
        out_shape=[_sds((t, D), f32), _sds((t, D), f32)],
        compiler_params=_cparams(1))(a, w, xres, g)


def postnorm_bwd_matmul(y, g, dxo, w, fp, tm, tn, out_dtype, name):
    t, n = y.shape[0], w.shape[0]
    relu = fp is not None

    def body(*refs):
        if relu:
            y_ref, g_ref, dxo_ref, w_ref, fp_ref, dy_ref, dg_ref, da_ref = refs
        else:
            y_ref, g_ref, dxo_ref, w_ref, dy_ref, dg_ref, da_ref = refs
        i, j = pl.program_id(0), pl.program_id(1)

        @pl.when(j == 0)
        def _():
            dx, dgc = _rms_bwd(y_ref[...], g_ref[...], dxo_ref[...])
            dy_ref[...] = dx.astype(bf16)
            _accum(dg_ref, jnp.sum(dgc, axis=0, keepdims=True), i == 0)

        da = _dot_nt(dy_ref[...], w_ref[...])
        if relu:
            da = da * (2.0 * jnp.maximum(fp_ref[...].astype(f32), 0.0))
        da_ref[...] = da.astype(out_dtype)

    in_specs = [pl.BlockSpec((tm, D), lambda i, j: (i, 0)), pl.BlockSpec((1, D), lambda i, j: (0, 0)),
                pl.BlockSpec((tm, D), lambda i, j: (i, 0)), pl.BlockSpec((tn, D), lambda i, j: (j, 0))]
    args = [y, g, dxo, w]
    if relu:
        in_specs.append(pl.BlockSpec((tm, tn), lambda i, j: (i, j)))
        args.append(fp)
    return pl.pallas_call(
        body, name=name, grid=(t // tm, n // tn), in_specs=in_specs,
        out_specs=[pl.BlockSpec((tm, D), lambda i, j: (i, 0)), pl.BlockSpec((1, D), lambda i, j: (0, 0)),
                   pl.BlockSpec((tm, tn), lambda i, j: (i, j))],
        out_shape=[_sds((t, D), bf16), _sds((1, D), f32), _sds((t, n), out_dtype)],
        compiler_params=_cparams(2))(*args)


def matmul_prenorm_bwd(da, w, x, g, dxo, tm, name):
    t, k = da.shape
    blocked = w.ndim == 3

    def body(da_ref, w_ref, x_ref, g_ref, dxo_ref, dx_ref, dg_ref):
        if blocked:
            kc = w.shape[2]
            dh = _dot_nt(da_ref[:, 0:kc], w_ref[0])
            for q in range(1, w.shape[0]):
                dh = dh + _dot_nt(da_ref[:, q * kc:(q + 1) * kc], w_ref[q])
        else:
            dh = _dot_nt(da_ref[...], w_ref[...])
        dxn, dgc = _rms_bwd(x_ref[...], g_ref[...], dh)
        dx_ref[...] = dxo_ref[...] + dxn
        _accum(dg_ref, jnp.sum(dgc, axis=0, keepdims=True), pl.program_id(0) == 0)

    w_spec = pl.BlockSpec(w.shape, (lambda i: (0, 0, 0)) if blocked else (lambda i: (0, 0)))
    return pl.pallas_call(
        body, name=name, grid=(t // tm,),
        in_specs=[pl.BlockSpec((tm, k), lambda i: (i, 0)), w_spec,
                  pl.BlockSpec((tm, D), lambda i: (i, 0)), pl.BlockSpec((1, D), lambda i: (0, 0)),
                  pl.BlockSpec((tm, D), lambda i: (i, 0))],
        out_specs=[pl.BlockSpec((tm, D), lambda i: (i, 0)), pl.BlockSpec((1, D), lambda i: (0, 0))],
        out_shape=[_sds((t, D), f32), _sds((1, D), f32)],
        compiler_params=_cparams(1))(da, w, x, g, dxo)


def matmul_tn(a, b, tm, tn, relu2, name, col_blocks=False):
    t, m = a.shape
    n = b.shape[1]
    if col_blocks:
        out_spec, out_shape = pl.BlockSpec((None, tm, tn), lambda i, j: (j, i, 0)), _sds((n // tn, m, tn), f32)
    else:
        out_spec, out_shape = pl.BlockSpec((tm, tn), lambda i, j: (i, j)), _sds((m, n), f32)

    def body(a_ref, b_ref, o_ref, at_ref):
        @pl.when(pl.program_id(1) == 0)
        def _():
            av = a_ref[...]
            if relu2:
                af = jnp.maximum(av.astype(f32), 0.0)
                av = (af * af).astype(bf16)
            at_ref[...] = av.T

        o_ref[...] = _dot(at_ref[...], b_ref[...])

    return pl.pallas_call(
        body, name=name, grid=(m // tm, n // tn),
        in_specs=[pl.BlockSpec((t, tm), lambda i, j: (0, i)), pl.BlockSpec((t, tn), lambda i, j: (0, j))],
        out_specs=out_spec, out_shape=out_shape,
        scratch_shapes=[pltpu.VMEM((tm, t), bf16)],
        compiler_params=_cparams(2))(a, b)


def _halo_prev(tb, col):
    return lambda i: (jnp.maximum(i * (tb // HALO) - 1, 0), col)


def _halo_next(tb, col, t):
    return lambda i: (jnp.minimum((i + 1) * (tb // HALO), t // HALO - 1), col)


def group_a_fwd(proj, wa, g, seq, tb, name):
    t = proj.shape[0]
    bps = seq // tb

    def body(xa_ref, ca_ref, ba_ref, xah_ref, cah_ref, wa_ref, g_ref, o_ref, u_scr):
        first = (pl.program_id(0) % bps) == 0
        u = ca_ref[...] * xa_ref[...]
        u_scr[0:HALO, :] = jnp.where(first, 0.0, cah_ref[...] * xah_ref[...])
        u_scr[HALO:HALO + tb, :] = u
        w = wa_ref[...]
        cv = w[2:3] * u + w[1:2] * u_scr[pl.ds(HALO - 1, tb), :] + w[0:1] * u_scr[pl.ds(HALO - 2, tb), :]
        o_ref[...] = _rms_fwd(ba_ref[...] * cv, g_ref[...]).astype(bf16)

    blk = lambda c: pl.BlockSpec((tb, D), lambda i: (i, c))
    return pl.pallas_call(
        body, name=name, grid=(t // tb,),
        in_specs=[blk(0), blk(1), blk(2),
                  pl.BlockSpec((HALO, D), _halo_prev(tb, 0)), pl.BlockSpec((HALO, D), _halo_prev(tb, 1)),
                  pl.BlockSpec((8, D), lambda i: (0, 0)), pl.BlockSpec((1, D), lambda i: (0, 0))],
        out_specs=pl.BlockSpec((tb, D), lambda i: (i, 0)),
        out_shape=_sds((t, D), bf16),
        scratch_shapes=[pltpu.VMEM((tb + HALO, D), f32)],
        compiler_params=_cparams(1))(proj, proj, proj, proj, proj, wa, g)


def group_a_bwd(proj, dcat, wa, g, seq, tb, name):
    t = proj.shape[0]
    bps = seq // tb

    def body(xa_ref, ca_ref, ba_ref, dy_ref, xap_ref, cap_ref, xan_ref, can_ref, ban_ref, dyn_ref, wa_ref, g_ref,
             dp_ref, dwa_ref, dg_ref, u_scr, d_scr):
        i = pl.program_id(0)
        first = (i % bps) == 0
        last = (i % bps) == bps - 1
        w = wa_ref[...]
        gv = g_ref[...]
        xa, ca, ba = xa_ref[...], ca_ref[...], ba_ref[...]
        u_scr[0:HALO, :] = jnp.where(first, 0.0, cap_ref[...] * xap_ref[...])
        u_scr[HALO:HALO + tb, :] = ca * xa
        u_scr[HALO + tb:2 * HALO + tb, :] = can_ref[...] * xan_ref[...]

        def conv(start, rows):
            return (w[2:3] * u_scr[pl.ds(start, rows), :] + w[1:2] * u_scr[pl.ds(start - 1, rows), :]
                    + w[0:1] * u_scr[pl.ds(start - 2, rows), :])

        cv = conv(HALO, tb)
        dya, dgc = _rms_bwd(ba * cv, gv, dy_ref[...])
        ban = ban_ref[...]
        dyan, _ = _rms_bwd(ban * conv(HALO + tb, HALO), gv, dyn_ref[...])
        dcv = dya * ba
        d_scr[0:tb, :] = dcv
        d_scr[tb:tb + HALO, :] = jnp.where(last, 0.0, dyan * ban)
        du = w[2:3] * dcv + w[1:2] * d_scr[pl.ds(1, tb), :] + w[0:1] * d_scr[pl.ds(2, tb), :]
        dp_ref[:, 0:D] = (du * ca).astype(bf16)
        dp_ref[:, D:2 * D] = (du * xa).astype(bf16)
        dp_ref[:, 2 * D:3 * D] = (dya * cv).astype(bf16)
        row = lax.broadcasted_iota(jnp.int32, (8, D), 0)
        dw = jnp.zeros((8, D), f32)
        for k in range(3):
            s = jnp.sum(dcv * u_scr[pl.ds(HALO - 2 + k, tb), :], axis=0, keepdims=True)
            dw = jnp.where(row == k, s, dw)
        _accum(dwa_ref, dw, i == 0)
        _accum(dg_ref, jnp.sum(dgc, axis=0, keepdims=True), i == 0)

    blk = lambda c: pl.BlockSpec((tb, D), lambda i: (i, c))
    prv = lambda c: pl.BlockSpec((HALO, D), _halo_prev(tb, c))
    nxt = lambda c: pl.BlockSpec((HALO, D), _halo_next(tb, c, t))
    return pl.pallas_call(
        body, name=name, grid=(t // tb,),
        in_specs=[blk(0), blk(1), blk(2), blk(0), prv(0), prv(1), nxt(0), nxt(1), nxt(2), nxt(0),
                  pl.BlockSpec((8, D), lambda i: (0, 0)), pl.BlockSpec((1, D), lambda i: (0, 0))],
        out_specs=[pl.BlockSpec((tb, 3 * D), lambda i: (i, 0)), pl.BlockSpec((8, D), lambda i: (0, 0)),
                   pl.BlockSpec((1, D), lambda i: (0, 0))],
        out_shape=[_sds((t, 3 * D), bf16), _sds((8, D), f32), _sds((1, D), f32)],
        scratch_shapes=[pltpu.VMEM((tb + 2 * HALO, D), f32), pltpu.VMEM((tb + HALO, D), f32)],
        compiler_params=_cparams(1))(proj, proj, proj, dcat, proj, proj, proj, proj, proj, dcat, wa, g)


CB = 512
XBC_BLK0 = COL_XBC // CB


def conv_b_fwd(proj, ws, bs, seq, tb, name):
    t = proj.shape[0]
    bps = seq // tb

    def body(x_ref, xp_ref, w_ref, b_ref, o_ref, x_scr):
        first = (pl.program_id(1) % bps) == 0
        x = x_ref[...]
        x_scr[0:HALO, :] = jnp.where(first, 0.0, xp_ref[...])
        x_scr[HALO:HALO + tb, :] = x
        w = w_ref[...]
        xc = w[3:4] * x + b_ref[...]
        for k in range(3):
            xc = xc + w[k:k + 1] * x_scr[pl.ds(HALO - 3 + k, tb), :]
        o_ref[...] = xc * _sigmoid(xc)

    return pl.pallas_call(
        body, name=name, grid=(XBC // CB, t // tb),
        in_specs=[pl.BlockSpec((tb, CB), lambda j, i: (i, XBC_BLK0 + j)),
                  pl.BlockSpec((HALO, CB), lambda j, i: (jnp.maximum(i * (tb // HALO) - 1, 0), XBC_BLK0 + j)),
                  pl.BlockSpec((8, CB), lambda j, i: (0, j)), pl.BlockSpec((1, CB), lambda j, i: (0, j))],
        out_specs=pl.BlockSpec((tb, CB), lambda j, i: (i, j)),
        out_shape=_sds((t, XBC), f32),
        scratch_shapes=[pltpu.VMEM((tb + HALO, CB), f32)],
        compiler_params=_cparams(2))(proj, proj, ws, bs)


def conv_b_bwd(proj, dxs, ws, bs, seq, tb, name):
    t = proj.shape[0]
    bps = seq // tb

    def body(x_ref, xp_ref, xn_ref, d_ref, dn_ref, w_ref, b_ref, dx_ref, dw_ref, db_ref, x_scr, d_scr):
        i = pl.program_id(1)
        first = (i % bps) == 0
        last = (i % bps) == bps - 1
        w = w_ref[...]
        bias = b_ref[...]
        x_scr[0:HALO, :] = jnp.where(first, 0.0, xp_ref[...])
        x_scr[HALO:HALO + tb, :] = x_ref[...]
        x_scr[HALO + tb:2 * HALO + tb, :] = xn_ref[...]

        def dsilu_at(start, rows, d):
            xc = bias + w[3:4] * x_scr[pl.ds(start, rows), :]
            for k in range(3):
                xc = xc + w[k:k + 1] * x_scr[pl.ds(start - 3 + k, rows), :]
            sg = _sigmoid(xc)
            return d * (sg * (1.0 + xc * (1.0 - sg)))

        dxc = dsilu_at(HALO, tb, d_ref[...])
        d_scr[0:tb, :] = dxc
        d_scr[tb:tb + HALO, :] = jnp.where(last, 0.0, dsilu_at(HALO + tb, HALO, dn_ref[...]))
        dx = w[3:4] * dxc
        for k in range(3):
            dx = dx + w[k:k + 1] * d_scr[pl.ds(3 - k, tb), :]
        dx_ref[...] = dx.astype(bf16)
        row = lax.broadcasted_iota(jnp.int32, (8, CB), 0)
        dw = jnp.zeros((8, CB), f32)
        for k in range(4):
            s = jnp.sum(dxc * x_scr[pl.ds(HALO - 3 + k, tb), :], axis=0, keepdims=True)
            dw = jnp.where(row == k, s, dw)
        _accum(dw_ref, dw, i == 0)
        _accum(db_ref, jnp.sum(dxc, axis=0, keepdims=True), i == 0)

    nh = t // HALO
    return pl.pallas_call(
        body, name=name, grid=(XBC // CB, t // tb),
        in_specs=[pl.BlockSpec((tb, CB), lambda j, i: (i, XBC_BLK0 + j)),
                  pl.BlockSpec((HALO, CB), lambda j, i: (jnp.maximum(i * (tb // HALO) - 1, 0), XBC_BLK0 + j)),
                  pl.BlockSpec((HALO, CB), lambda j, i: (jnp.minimum((i + 1) * (tb // HALO), nh - 1), XBC_BLK0 + j)),
                  pl.BlockSpec((tb, CB), lambda j, i: (i, j)),
                  pl.BlockSpec((HALO, CB), lambda j, i: (jnp.minimum((i + 1) * (tb // HALO), nh - 1), j)),
                  pl.BlockSpec((8, CB), lambda j, i: (0, j)), pl.BlockSpec((1, CB), lambda j, i: (0, j))],
        out_specs=[pl.BlockSpec((tb, CB), lambda j, i: (i, j)), pl.BlockSpec((8, CB), lambda j, i: (0, j)),
                   pl.BlockSpec((1, CB), lambda j, i: (0, j))],
        out_shape=[_sds((t, XBC), bf16), _sds((8, XBC), f32), _sds((1, XBC), f32)],
        scratch_shapes=[pltpu.VMEM((tb + 2 * HALO, CB), f32), pltpu.VMEM((tb + HALO, CB), f32)],
        compiler_params=_cparams(2))(proj, proj, proj, dxs, dxs, ws, bs)


GW = D // NG


def _ssd_consts():
    head_of_lane = jnp.arange(D) // HP
    expand = (jnp.arange(CH)[:, None] == head_of_lane[None, :]).astype(bf16)
    tri = (jnp.arange(CH)[:, None] >= jnp.arange(CH)[None, :]).astype(f32)
    return expand, tri


def _ssd_common(par_ref, dtr_ref, e_ref, tri_ref):
    par = par_ref[...]
    dtb, alog, dsk = par[0:1], par[1:2], par[2:3]
    lane = lax.broadcasted_iota(jnp.int32, (CH, CH), 1)
    a = -jnp.exp(alog)
    dtr = dtr_ref[...] + dtb
    sp = jnp.maximum(dtr, 0.0) + jnp.log(1.0 + jnp.exp(-jnp.abs(dtr)))
    dt = jnp.where(lane < NH, sp, 0.0)
    cs = jnp.dot(tri_ref[...], dt * a, precision=lax.Precision.HIGHEST, preferred_element_type=f32)
    cs_last = cs[CH - 1:CH, :]
    dte = jnp.exp(cs_last - cs)
    ecs = jnp.exp(cs)
    ecl = jnp.exp(cs_last)
    e = e_ref[...]
    row8 = lax.broadcasted_iota(jnp.int32, (8, CH), 0)
    r8 = _split_dot(jnp.where(row8 == 0, ecl, jnp.where(row8 == 1, dsk, 0.0)), e, 3)
    return dict(a=a, dtr=dtr, dt=dt, cs=cs, cst=cs.T, dte=dte, ecs=ecs, ecl=ecl, e=e, lane=lane,
                dt_x=_split_dot(dt, e, 3), dte_x=_split_dot(dte, e, 3), ecs_x=_split_dot(ecs, e, 3),
                ecl_x=r8[0:1], dsk_x=r8[1:2])


def _decay_matrix(c, h):
    li = lax.broadcasted_iota(jnp.int32, (CH, CH), 0)
    seg = c["cs"][:, h:h + 1] - c["cst"][h:h + 1, :]
    return jnp.exp(jnp.where(li >= c["lane"], seg, -jnp.inf))


def _gate_norm_fwd(y, z, gs):
    zg = z * _sigmoid(z)
    yg = y * zg
    return jnp.concatenate([_rms_fwd(yg[:, k * GW:(k + 1) * GW], gs[:, k * GW:(k + 1) * GW]) for k in range(NG)], axis=1)


def ssd_fwd(xbcs, proj, par, gs, seq, name):
    t = xbcs.shape[0]
    nc = seq // CH
    expand, tri = _ssd_consts()

    def body(xs_ref, b_ref, c_ref, dtr_ref, z_ref, par_ref, e_ref, tri_ref, gs_ref, yn_ref, y_ref, st_ref, p_scr, yd_scr):
        @pl.when(pl.program_id(0) % nc == 0)
        def _():
            p_scr[...] = jnp.zeros_like(p_scr)

        c = _ssd_common(par_ref, dtr_ref, e_ref, tri_ref)
        xs = xs_ref[...]
        xdt = xs * c["dt_x"]
        xdt_b = xdt.astype(bf16)
        xdte_b = (xdt * c["dte_x"]).astype(bf16)
        p = p_scr[...]
        st_ref[0] = p
        p_b = p.astype(bf16)
        lo = c["lane"] < HP
        for g in range(NG):
            bg = b_ref[:, g * NS:(g + 1) * NS].astype(bf16)
            cg = c_ref[:, g * NS:(g + 1) * NS].astype(bf16)
            gmat = _dot_nt(cg, bg)
            for q in range(GW // CH):
                col = g * GW + q * CH
                xp = xdt_b[:, col:col + CH]
                h0 = col // HP
                m0 = (gmat * _decay_matrix(c, h0)).astype(bf16)
                m1 = (gmat * _decay_matrix(c, h0 + 1)).astype(bf16)
                yd_scr[:, col:col + CH] = (_dot(m0, jnp.where(lo, xp, jnp.zeros_like(xp)))
                                           + _dot(m1, jnp.where(lo, jnp.zeros_like(xp), xp)))
            gsl = slice(g * GW, (g + 1) * GW)
            yoff = _dot(cg, p_b[:, gsl]) * c["ecs_x"][:, gsl]
            yd_scr[:, gsl] = yd_scr[:, gsl] + yoff
            p_scr[:, gsl] = p[:, gsl] * c["ecl_x"][:, gsl] + _dot_tn(bg, xdte_b[:, gsl])
        y = yd_scr[...] + c["dsk_x"] * xs
        y_ref[...] = y
        yn_ref[...] = _gate_norm_fwd(y, z_ref[...], gs_ref[...]).astype(bf16)

    nb = t // CH
    return pl.pallas_call(
        body, name=name, grid=(nb,),
        in_specs=[pl.BlockSpec((CH, D), lambda i: (i, 0)),
                  pl.BlockSpec((CH, NG * NS), lambda i: (i, D // (NG * NS))),
                  pl.BlockSpec((CH, NG * NS), lambda i: (i, D // (NG * NS) + 1)),
                  pl.BlockSpec((CH, CH), lambda i: (i, COL_DT // CH)),
                  pl.BlockSpec((CH, D), lambda i: (i, COL_Z // D)),
                  pl.BlockSpec((8, CH), lambda i: (0, 0)), pl.BlockSpec((CH, D), lambda i: (0, 0)),
                  pl.BlockSpec((CH, CH), lambda i: (0, 0)), pl.BlockSpec((1, D), lambda i: (0, 0))],
        out_specs=[pl.BlockSpec((CH, D), lambda i: (i, 0)), pl.BlockSpec((CH, D), lambda i: (i, 0)),
                   pl.BlockSpec((1, NS, D), lambda i: (i, 0, 0))],
        out_shape=[_sds((t, D), bf16), _sds((t, D), f32), _sds((nb, NS, D), f32)],
        scratch_shapes=[pltpu.VMEM((NS, D), f32), pltpu.VMEM((CH, D), f32)],
        compiler_params=_cparams(1))(xbcs, xbcs, xbcs, proj, proj, par, expand, tri, gs)


def ssd_bwd(xbcs, proj, ypre, states, dcat, par, gs, seq, name):
    t = xbcs.shape[0]
    nc = seq // CH
    expand, tri = _ssd_consts()

    def body(xs_ref, b_ref, c_ref, dtr_ref, z_ref, y_ref, st_ref, dyn_ref, par_ref, e_ref, tri_ref, gs_ref,
             dx_ref, dz_ref, ddt_ref, dpar_ref, dgs_ref, dp_scr, dxdt_scr):
        i = pl.program_id(0)

        @pl.when(i % nc == 0)
        def _():
            dp_scr[...] = jnp.zeros_like(dp_scr)

        c = _ssd_common(par_ref, dtr_ref, e_ref, tri_ref)
        e = c["e"]
        lane = c["lane"]
        sub = lax.broadcasted_iota(jnp.int32, (CH, CH), 0)
        xs = xs_ref[...]
        xdt = xs * c["dt_x"]
        xdt_b = xdt.astype(bf16)
        xdte_b = (xdt * c["dte_x"]).astype(bf16)
        p = st_ref[0]
        p_b = p.astype(bf16)
        dpn = dp_scr[...]
        dpn_b = dpn.astype(bf16)

        y, z, gs_v = y_ref[...], z_ref[...], gs_ref[...]
        zs = _sigmoid(z)
        zg = z * zs
        yg = y * zg
        parts, gparts = [], []
        for k in range(NG):
            sl = slice(k * GW, (k + 1) * GW)
            dxk, dgk = _rms_bwd(yg[:, sl], gs_v[:, sl], dyn_ref[:, sl])
            parts.append(dxk)
            gparts.append(dgk)
        dyg = jnp.concatenate(parts, axis=1)
        dgs_rows = jnp.concatenate(gparts, axis=1)
        dy = dyg * zg
        dz_ref[...] = (dyg * y * (zs * (1.0 + z * (1.0 - zs)))).astype(bf16)
        dy_b = dy.astype(bf16)
        dq_b = (dy * c["ecs_x"]).astype(bf16)

        lo = lane < HP
        dcs = jnp.zeros((CH, CH), f32)
        dcst = jnp.zeros((CH, CH), f32)
        for g in range(NG):
            gsl = slice(g * GW, (g + 1) * GW)
            bg = b_ref[:, g * NS:(g + 1) * NS].astype(bf16)
            cg = c_ref[:, g * NS:(g + 1) * NS].astype(bf16)
            gmat = _dot_nt(cg, bg)
            dgm = jnp.zeros((CH, CH), f32)
            for q in range(GW // CH):
                col = g * GW + q * CH
                xp = xdt_b[:, col:col + CH]
                dyp = dy_b[:, col:col + CH]
                acc = None
                for hh in range(2):
                    h = col // HP + hh
                    keep = lo if hh == 0 else jnp.logical_not(lo)
                    dyh = jnp.where(keep, dyp, jnp.zeros_like(dyp))
                    dec = _decay_matrix(c, h)
                    m = gmat * dec
                    dm = _dot_nt(dyh, xp)
                    dseg = dm * m
                    dcs = dcs + jnp.where(lane == h, jnp.sum(dseg, axis=1, keepdims=True), 0.0)
                    dcst = dcst + jnp.where(sub == h, jnp.sum(dseg, axis=0, keepdims=True), 0.0)
                    dgm = dgm + dm * dec
                    term = _dot_tn(m.astype(bf16), dyh)
                    acc = term if acc is None else acc + term
                dxdt_scr[:, col:col + CH] = acc
            dgm_b = dgm.astype(bf16)
            bds = _dot(bg, dpn_b[:, gsl])
            dxdt_scr[:, gsl] = dxdt_scr[:, gsl] + c["dte_x"][:, gsl] * bds
            dc_g = _dot(dgm_b, bg) + _dot_nt(dq_b[:, gsl], p_b[:, gsl])
            db_g = _dot_tn(dgm_b, cg) + _dot_nt(xdte_b[:, gsl], dpn_b[:, gsl])
            dx_ref[:, D + g * NS:D + (g + 1) * NS] = db_g
            dx_ref[:, D + NG * NS + g * NS:D + NG * NS + (g + 1) * NS] = dc_g
            dp_scr[:, gsl] = dpn[:, gsl] * c["ecl_x"][:, gsl] + _dot_tn(cg, dq_b[:, gsl])
            q_g = _dot(cg, p_b[:, gsl])
            e_g = e[:, gsl]
            dcs = dcs + c["ecs"] * _split_dot(dy[:, gsl] * q_g, e_g, 2, nt=True)
            ddte = _split_dot(xdt[:, gsl] * bds, e_g, 2, nt=True) * c["dte"]
            dcs = dcs - ddte
            dcs = dcs + jnp.where(sub == CH - 1, jnp.sum(ddte, axis=0, keepdims=True), 0.0)

        decl = _split_dot(jnp.broadcast_to(jnp.sum(dpn * p, axis=0, keepdims=True), (8, D)), e, 2, nt=True)[0:1]
        dcs = dcs + jnp.where(sub == CH - 1, c["ecl"] * decl, 0.0)
        dcs = dcs - dcst.T
        dadt = lax.dot_general(tri_ref[...], dcs, (((0,), (0,)), ((), ())), precision=lax.Precision.HIGHEST,
                               preferred_element_type=f32)
        dxdt = dxdt_scr[...]
        ddt = dadt * c["a"] + _split_dot(dxdt * xs, e, 2, nt=True)
        ddtr = jnp.where(lane < NH, ddt * _sigmoid(c["dtr"]), 0.0)
        ddt_ref[...] = ddtr.astype(bf16)
        dx_ref[:, 0:D] = dxdt * c["dt_x"] + c["dsk_x"] * dy
        dsk = _split_dot(jnp.broadcast_to(jnp.sum(dy * xs, axis=0, keepdims=True), (8, D)), e, 2, nt=True)[0:1]
        dalog = jnp.sum(dadt * c["dt"], axis=0, keepdims=True) * c["a"]
        row8 = lax.broadcasted_iota(jnp.int32, (8, CH), 0)
        dpar = jnp.where(row8 == 0, jnp.sum(ddtr, axis=0, keepdims=True),
                         jnp.where(row8 == 1, dalog, jnp.where(row8 == 2, dsk, 0.0)))
        dpar = jnp.where(lax.broadcasted_iota(jnp.int32, (8, CH), 1) < NH, dpar, 0.0)
        _accum(dpar_ref, dpar, i == 0)
        _accum(dgs_ref, jnp.sum(dgs_rows, axis=0, keepdims=True), i == 0)

    nb = t // CH
    rev = lambda i: (i // nc) * nc + (nc - 1 - i % nc)
    return pl.pallas_call(
        body, name=name, grid=(nb,),
        in_specs=[pl.BlockSpec((CH, D), lambda i: (rev(i), 0)),
                  pl.BlockSpec((CH, NG * NS), lambda i: (rev(i), D // (NG * NS))),
                  pl.BlockSpec((CH, NG * NS), lambda i: (rev(i), D // (NG * NS) + 1)),
                  pl.BlockSpec((CH, CH), lambda i: (rev(i), COL_DT // CH)),
                  pl.BlockSpec((CH, D), lambda i: (rev(i), COL_Z // D)),
                  pl.BlockSpec((CH, D), lambda i: (rev(i), 0)),
                  pl.BlockSpec((1, NS, D), lambda i: (rev(i), 0, 0)),
                  pl.BlockSpec((CH, D), lambda i: (rev(i), 1)),
                  pl.BlockSpec((8, CH), lambda i: (0, 0)), pl.BlockSpec((CH, D), lambda i: (0, 0)),
                  pl.BlockSpec((CH, CH), lambda i: (0, 0)), pl.BlockSpec((1, D), lambda i: (0, 0))],
        out_specs=[pl.BlockSpec((CH, XBC), lambda i: (rev(i), 0)), pl.BlockSpec((CH, D), lambda i: (rev(i), 0)),
                   pl.BlockSpec((CH, CH), lambda i: (rev(i), 0)),
                   pl.BlockSpec((8, CH), lambda i: (0, 0)), pl.BlockSpec((1, D), lambda i: (0, 0))],
        out_shape=[_sds((t, XBC), f32), _sds((t, D), bf16), _sds((t, CH), bf16), _sds((8, CH), f32), _sds((1, D), f32)],
        scratch_shapes=[pltpu.VMEM((NS, D), f32), pltpu.VMEM((CH, D), f32)],
        compiler_params=_cparams(1))(xbcs, xbcs, xbcs, proj, proj, ypre, states, dcat, par, expand, tri, gs)


def loss_head(y, target, tb, name):
    t = y.shape[0]

    def body(y_ref, t_ref, s_ref, dy_ref):
        err = y_ref[...] - t_ref[...]
        dy_ref[...] = err * (1.0 / D)
        _accum(s_ref, jnp.zeros((8, CH), f32) + jnp.sum(err * err), pl.program_id(0) == 0)

    return pl.pallas_call(
        body, name=name, grid=(t // tb,),
        in_specs=[pl.BlockSpec((tb, D), lambda i: (i, 0)), pl.BlockSpec((tb, D), lambda i: (i, 0))],
        out_specs=[pl.BlockSpec((8, CH), lambda i: (0, 0)), pl.BlockSpec((tb, D), lambda i: (i, 0))],
        out_shape=[_sds((8, CH), f32), _sds((t, D), f32)],
        compiler_params=_cparams(1))(y, target)


def _tiles(t, seq):
    tm = min(512, t)
    return dict(tm=tm, tm_small=min(256, t), tb=min(512, seq), tb_conv=min(512, seq))


def local_step(x, target, w, seq):
    t = x.shape[0]
    ts = _tiles(t, seq)
    tm, tb = ts["tm"], ts["tb"]
    depth = len(w["win"])
    saved = []
    for l in range(depth):
        proj, h1 = norm_matmul(x, w["g1"][l], w["win"][l], tm, 1152, f32, "in_proj")
        ya = group_a_fwd(proj, w["wa"][l], w["ga"][l], seq, tb, "group_a_fwd")
        xbcs = conv_b_fwd(proj, w["ws"][l], w["bs"][l], seq, tb, "conv_b_fwd")
        ys, ypre, states = ssd_fwd(xbcs, proj, w["par"][l], w["gs"][l], seq, "ssd_fwd")
        cat = jnp.concatenate([ya, ys], axis=1)
        mix, x2 = matmul_postnorm(cat, w["wo"][l], x, w["g2"][l], tm, False, "out_proj")
        fp, h2 = norm_matmul(x2, w["g3"][l], w["wu"][l], tm, 1024, bf16, "mlp_up")
        o, x3 = matmul_postnorm(fp, w["wd"][l], x2, w["g4"][l], tm, True, "mlp_down")
        saved.append(dict(x=x, proj=proj, h1=h1, xbcs=xbcs, ypre=ypre, states=states, cat=cat, mix=mix, x2=x2,
                          fp=fp, h2=h2, o=o))
        x = x3
    sse, dx = loss_head(x, target, tm, "loss_head")
    grads = [None] * depth
    for l in reversed(range(depth)):
        s = saved[l]
        do, dg4, dfp = postnorm_bwd_matmul(s["o"], w["g4"][l], dx, w["wd"][l], s["fp"], tm, 1024, bf16, "mlp_down_bwd")
        dwd = matmul_tn(s["fp"], do, 512, 1024, True, "mlp_down_dw")
        dx2, dg3 = matmul_prenorm_bwd(dfp, w["wu"][l], s["x2"], w["g3"][l], dx, tm, "mlp_up_bwd")
        dwu = matmul_tn(s["h2"], dfp, 512, 1024, False, "mlp_up_dw", col_blocks=True)
        dmix, dg2, dcat = postnorm_bwd_matmul(s["mix"], w["g2"][l], dx2, w["wo"][l], None, tm, 1024, f32, "out_proj_bwd")
        dwo = matmul_tn(s["cat"], dmix, 512, 1024, False, "out_proj_dw")
        dpa, dwa, dga = group_a_bwd(s["proj"], dcat, w["wa"][l], w["ga"][l], seq, tb, "group_a_bwd")
        dxbcs, dz, ddt, dpar, dgs = ssd_bwd(s["xbcs"], s["proj"], s["ypre"], s["states"], dcat, w["par"][l],
                                            w["gs"][l], seq, "ssd_bwd")
        dxbc, dws, dbs = conv_b_bwd(s["proj"], dxbcs, w["ws"][l], w["bs"][l], seq, tb, "conv_b_bwd")
        dproj = jnp.concatenate([dpa, dz, dxbc, ddt], axis=1)
        dx, dg1 = matmul_prenorm_bwd(dproj, w["win"][l], s["x"], w["g1"][l], dx2, ts["tm_small"], "in_proj_bwd")
        dwin = matmul_tn(s["h1"], dproj, 512, 1152, False, "in_proj_dw")
        grads[l] = dict(win=dwin, wo=dwo, wu=dwu, wd=dwd, wa=dwa, ws=dws, bs=dbs, par=dpar,
                        g1=dg1, ga=dga, gs=dgs, g2=dg2, g3=dg3, g4=dg4)
    return sse, dx, grads


GROUPS = {
    "chips": [(1, 0, 0), (0, 1, 0), (1, 1, 0)],
    "pair": [(0, 0, 1)],
    "all": [(1, 0, 0), (0, 1, 0), (1, 1, 0), (0, 0, 1), (1, 0, 1), (0, 1, 1), (1, 1, 1)],
}


def _group_index(group, x, y, c):
    return {"chips": 2 * x + y, "pair": c, "all": 4 * x + 2 * y + c}[group]


def _exchange(arrays, out_shapes, group, src_view, dst_view, name):
    masks = GROUPS[group]
    na, nm = len(arrays), len(masks)

    def body(*refs):
        ins, outs = refs[:na], refs[na:2 * na]
        send_sems, recv_sems, local_sems = refs[2 * na:]
        x, y, c = lax.axis_index("x"), lax.axis_index("y"), lax.axis_index("c")
        me = _group_index(group, x, y, c)
        peers = []
        for mx, my, mc in masks:
            px, py, pc = (1 - x if mx else x), (1 - y if my else y), (1 - c if mc else c)
            peers.append(((px, py, pc), _group_index(group, px, py, pc)))
        local = [pltpu.make_async_copy(src_view(ins[a], a, me), dst_view(outs[a], a, me), local_sems.at[a])
                 for a in range(na)]
        for cp in local:
            cp.start()
        sends = []
        for a in range(na):
            for j, (dev, pidx) in enumerate(peers):
                sends.append(pltpu.make_async_remote_copy(
                    src_ref=src_view(ins[a], a, pidx), dst_ref=dst_view(outs[a], a, me),
                    send_sem=send_sems.at[a * nm + j], recv_sem=recv_sems.at[a * nm + j],
                    device_id=dev, device_id_type=MESH))
        for cp in sends:
            cp.start()
        for a in range(na):
            for j, (dev, pidx) in enumerate(peers):
                pltpu.make_async_remote_copy(
                    src_ref=src_view(ins[a], a, pidx), dst_ref=dst_view(outs[a], a, pidx),
                    send_sem=send_sems.at[a * nm + j], recv_sem=recv_sems.at[a * nm + j],
                    device_id=dev, device_id_type=MESH).wait_recv()
        for cp in sends:
            cp.wait_send()
        for cp in local:
            cp.wait()

    hbm = pl.BlockSpec(memory_space=pltpu.HBM)
    return pl.pallas_call(
        body, name=name, in_specs=[hbm] * na, out_specs=[hbm] * na,
        out_shape=[_sds(s, a.dtype) for s, a in zip(out_shapes, arrays)],
        scratch_shapes=[pltpu.SemaphoreType.DMA((na * nm,)), pltpu.SemaphoreType.DMA((na * nm,)),
                        pltpu.SemaphoreType.DMA((na,))])(*arrays)


def all_gather(arrays, group, name, slot_axis=0):
    n = len(GROUPS[group]) + 1
    shapes = [a.shape[:slot_axis] + (n,) + a.shape[slot_axis:] for a in arrays]
    lead = (slice(None),) * slot_axis
    return _exchange(arrays, shapes, group, lambda r, a, i: r, lambda r, a, i: r.at[lead + (i,)], name)


def all_to_all(arrays, group, name):
    return _exchange(arrays, [a.shape for a in arrays], group, lambda r, a, i: r.at[i], lambda r, a, i: r.at[i], name)


def sum_slots(y, out_dtype, name, tb=256):
    n, r, c = y.shape
    tb = min(tb, r)

    def body(y_ref, o_ref):
        acc = y_ref[0].astype(f32)
        for i in range(1, n):
            acc = acc + y_ref[i].astype(f32)
        o_ref[...] = acc.astype(out_dtype)

    return pl.pallas_call(
        body, name=name, grid=(r // tb,),
        in_specs=[pl.BlockSpec((n, tb, c), lambda i: (0, i, 0))], out_specs=pl.BlockSpec((tb, c), lambda i: (i, 0)),
        out_shape=_sds((r, c), out_dtype), compiler_params=_cparams(1))(y)


def adamw(w, g, m, v, name, tb=256):
    r, c = w.shape
    tb = min(tb, r)

    def body(w_ref, g_ref, m_ref, v_ref, d_ref, mo_ref, vo_ref):
        gv = g_ref[...]
        m2 = B1 * m_ref[...] + (1.0 - B1) * gv
        v2 = B2 * v_ref[...] + (1.0 - B2) * (gv * gv)
        m_hat = m2 / (1.0 - B1 ** STEP)
        v_hat = v2 / (1.0 - B2 ** STEP)
        d_ref[...] = -LR * (m_hat / (jnp.sqrt(v_hat) + AEPS) + WD * w_ref[...])
        mo_ref[...] = m2
        vo_ref[...] = v2

    spec = pl.BlockSpec((tb, c), lambda i: (i, 0))
    return pl.pallas_call(
        body, name=name, grid=(r // tb,), in_specs=[spec] * 4, out_specs=[spec] * 3,
        out_shape=[_sds((r, c), f32)] * 3, compiler_params=_cparams(1))(w, g, m, v)


N_CHIPS = 4
SMALL_ROW = 1024
SMALL_PIECES = (("g1", D), ("ga", D), ("gs", D), ("g2", D), ("g3", D), ("g4", D), ("wa", 3 * D), ("ws", 4 * XBC),
                ("bs", XBC), ("par", 3 * CH))
SMALL_LAYER = 17 * SMALL_ROW


def _pack_small(grads):
    rows = []
    for g in grads:
        flat = [g["g1"], g["ga"], g["gs"], g["g2"], g["g3"], g["g4"], g["wa"][:3], g["ws"][:4], g["bs"], g["par"][:3]]
        v = jnp.concatenate([p.reshape(-1) for p in flat])
        rows.append(jnp.pad(v, (0, SMALL_LAYER - v.shape[0])))
    packed = jnp.concatenate(rows).reshape(-1, SMALL_ROW)
    return jnp.pad(packed, ((0, -packed.shape[0] % 8), (0, 0)))


def _unpack_small(packed, depth):
    flat = packed.reshape(-1)[:depth * SMALL_LAYER].reshape(depth, SMALL_LAYER)
    out, off = {}, 0
    for key, size in SMALL_PIECES:
        out[key] = flat[:, off:off + size]
        off += size
    return out


def kernel(x, norm_mix_pre, w_in, conv_a_w, ssm_conv_w, ssm_conv_b, dt_bias, a_log, d_skip, conv_out_norm, ssm_out_norm, w_out, norm_mix_post, norm_mlp_pre, w_up, w_down, norm_mlp_post, loss_target, m_norm_mix_pre, m_w_in, m_conv_a_w, m_ssm_conv_w, m_ssm_conv_b, m_dt_bias, m_a_log, m_d_skip, m_conv_out_norm, m_ssm_out_norm, m_w_out, m_norm_mix_post, m_norm_mlp_pre, m_w_up, m_w_down, m_norm_mlp_post, v_norm_mix_pre, v_w_in, v_conv_a_w, v_ssm_conv_w, v_ssm_conv_b, v_dt_bias, v_a_log, v_d_skip, v_conv_out_norm, v_ssm_out_norm, v_w_out, v_norm_mix_post, v_norm_mlp_pre, v_w_up, v_w_down, v_norm_mlp_post):
    nb, seq, _ = x.shape
    t = nb * seq
    depth = w_in.shape[0]
    half = depth // 2
    ncol = w_in.shape[2]
    chip = 2 * lax.axis_index("x") + lax.axis_index("y")

    win_g, wo_g, wu_g, wd_g, wa_g, ws_g = all_gather(
        [w_in.astype(bf16), w_out.astype(bf16), w_up.astype(bf16), w_down.astype(bf16), conv_a_w, ssm_conv_w],
        "chips", "gather_weights", slot_axis=1)
    win_full = jnp.pad(jnp.transpose(win_g, (0, 2, 1, 3)).reshape(depth, D, N_CHIPS * ncol),
                       ((0, 0), (0, 0), (0, PROJ - N_CHIPS * ncol)))
    wa_full = jnp.transpose(wa_g, (0, 2, 1, 3)).reshape(depth, 3, D)
    ws_full = jnp.transpose(ws_g, (0, 2, 1, 3)).reshape(depth, 4, XBC)
    lane_pad = lambda a: jnp.pad(a, ((0, 0), (0, CH - a.shape[1])))
    par = jnp.stack([lane_pad(dt_bias), lane_pad(a_log), lane_pad(d_skip)], axis=1)
    par = jnp.pad(par, ((0, 0), (0, 5), (0, 0)))
    w = dict(
        win=[win_full[l] for l in range(depth)],
        wo=[wo_g[l].reshape(2 * D, D) for l in range(depth)],
        wu=[wu_g[l] for l in range(depth)],
        wd=[wd_g[l].reshape(DFF, D) for l in range(depth)],
        wa=[jnp.pad(wa_full[l], ((0, 5), (0, 0))) for l in range(depth)],
        ws=[jnp.pad(ws_full[l], ((0, 4), (0, 0))) for l in range(depth)],
        bs=[ssm_conv_b[l][None] for l in range(depth)],
        par=[par[l] for l in range(depth)],
        g1=[norm_mix_pre[l][None] for l in range(depth)], ga=[conv_out_norm[l][None] for l in range(depth)],
        gs=[ssm_out_norm[l][None] for l in range(depth)], g2=[norm_mix_post[l][None] for l in range(depth)],
        g3=[norm_mlp_pre[l][None] for l in range(depth)], g4=[norm_mlp_post[l][None] for l in range(depth)])

    sse, dx, grads = local_step(x.reshape(t, D), loss_target.reshape(t, D), w, seq)
    loss = lax.psum(0.5 / D * sse[0, 0], ("x", "y", "c"))

    def slots(per_layer, to_chip_major):
        g = jnp.stack([to_chip_major(a) for a in per_layer])
        g = g.reshape((2, half) + g.shape[1:])
        return jnp.transpose(g, (0, 2, 1, 3, 4))

    big = [
        slots([g["win"] for g in grads],
              lambda a: jnp.transpose(a[:, :N_CHIPS * ncol].reshape(D, N_CHIPS, ncol), (1, 0, 2))),
        slots([g["wo"] for g in grads], lambda a: a.reshape(N_CHIPS, 2 * D // N_CHIPS, D)),
        slots([g["wu"] for g in grads], lambda a: a),
        slots([g["wd"] for g in grads], lambda a: a.reshape(N_CHIPS, DFF // N_CHIPS, D)),
    ]
    pair = all_to_all(big, "pair", "grads_to_pair")
    pair_sum = [sum_slots(p.reshape(2, -1, p.shape[-1]), bf16, "pair_sum").reshape(p.shape[1:]) for p in pair]
    chips = all_to_all(pair_sum, "chips", "grads_to_chips")
    half_sum = [sum_slots(q.reshape(N_CHIPS, -1, q.shape[-1]), f32, "chip_sum").reshape(q.shape[1:]) for q in chips]
    full = all_gather(half_sum, "pair", "grads_from_pair")
    g_win, g_wo, g_wu, g_wd = [f.reshape((depth,) + f.shape[2:]) for f in full]

    small_all = all_gather([_pack_small(grads)], "all", "gather_small")[0]
    small = _unpack_small(sum_slots(small_all, f32, "small_sum", tb=8), depth)
    wa_cols, ws_cols = conv_a_w.shape[2], ssm_conv_w.shape[2]
    par_g = small["par"].reshape(depth, 3, CH)
    g_small = dict(
        norm_mix_pre=small["g1"], conv_out_norm=small["ga"], ssm_out_norm=small["gs"], norm_mix_post=small["g2"],
        norm_mlp_pre=small["g3"], norm_mlp_post=small["g4"], ssm_conv_b=small["bs"],
        conv_a_w=lax.dynamic_slice_in_dim(small["wa"].reshape(depth, 3, D), chip * wa_cols, wa_cols, axis=2),
        ssm_conv_w=lax.dynamic_slice_in_dim(small["ws"].reshape(depth, 4, XBC), chip * ws_cols, ws_cols, axis=2),
        dt_bias=par_g[:, 0, :NH], a_log=par_g[:, 1, :NH], d_skip=par_g[:, 2, :NH])

    given = dict(norm_mix_pre=(norm_mix_pre, m_norm_mix_pre, v_norm_mix_pre), w_in=(w_in, m_w_in, v_w_in),
                 conv_a_w=(conv_a_w, m_conv_a_w, v_conv_a_w), ssm_conv_w=(ssm_conv_w, m_ssm_conv_w, v_ssm_conv_w),
                 ssm_conv_b=(ssm_conv_b, m_ssm_conv_b, v_ssm_conv_b), dt_bias=(dt_bias, m_dt_bias, v_dt_bias),
                 a_log=(a_log, m_a_log, v_a_log), d_skip=(d_skip, m_d_skip, v_d_skip),
                 conv_out_norm=(conv_out_norm, m_conv_out_norm, v_conv_out_norm),
                 ssm_out_norm=(ssm_out_norm, m_ssm_out_norm, v_ssm_out_norm), w_out=(w_out, m_w_out, v_w_out),
                 norm_mix_post=(norm_mix_post, m_norm_mix_post, v_norm_mix_post),
                 norm_mlp_pre=(norm_mlp_pre, m_norm_mlp_pre, v_norm_mlp_pre), w_up=(w_up, m_w_up, v_w_up),
                 w_down=(w_down, m_w_down, v_w_down), norm_mlp_post=(norm_mlp_post, m_norm_mlp_post, v_norm_mlp_post))
    grad = dict(g_small, w_in=g_win, w_out=g_wo, w_up=g_wu, w_down=g_wd)
    order = ["norm_mix_pre", "w_in", "conv_a_w", "ssm_conv_w", "ssm_conv_b", "dt_bias", "a_log", "d_skip",
             "conv_out_norm", "ssm_out_norm", "w_out", "norm_mix_post", "norm_mlp_pre", "w_up", "w_down",
             "norm_mlp_post"]
    g_out, d_out, m_out, v_out = [], [], [], []
    for n in order:
        wv, mv, vv = given[n]
        gv = grad[n].reshape(wv.shape)
        two_d = lambda a: a.reshape(-1, a.shape[-1])
        dlt, m2, v2 = adamw(two_d(wv), two_d(gv), two_d(mv), two_d(vv), "adamw")
        g_out.append(gv)
        d_out.append(dlt.reshape(wv.shape))
        m_out.append(m2.reshape(wv.shape))
        v_out.append(v2.reshape(wv.shape))
    return (loss, dx.reshape(nb, seq, D), *g_out, *d_out, *m_out, *v_out)
```

```python
import functools

import jax
import jax.numpy as jnp
from jax import lax
from jax.experimental import pallas as pl
from jax.experimental.pallas import tpu as pltpu

f32, bf16 = jnp.float32, jnp.bfloat16

D = 1024
NH, HP = 16, 64
NG, NS = 2, 128
CH = 128
XBC = D + 2 * NG * NS
DFF = 4 * D
IN_COLS = 3 * D + D + XBC + NH
PROJ = 5760
COL_Z, COL_XBC, COL_DT = 3 * D, 4 * D, 4 * D + XBC
EPS = 1e-6
HALO = 8
VMEM_LIMIT = 56 * 2**20
MESH = pl.DeviceIdType.MESH

LR, B1, B2, AEPS, WD, STEP = 0.001, 0.9, 0.999, 1e-08, 0.01, 10


def _cparams(n_axes):
    return pltpu.CompilerParams(dimension_semantics=("arbitrary",) * n_axes, vmem_limit_bytes=VMEM_LIMIT)


def _sds(shape, dtype):
    return jax.ShapeDtypeStruct(tuple(shape), dtype)


def _rms_fwd(x, g):
    r = lax.rsqrt(jnp.mean(x * x, axis=-1, keepdims=True) + EPS)
    return x * r * g


def _rms_bwd(x, g, dy):
    r = lax.rsqrt(jnp.mean(x * x, axis=-1, keepdims=True) + EPS)
    xh = x * r
    gdy = dy * g
    dx = r * (gdy - xh * jnp.mean(xh * gdy, axis=-1, keepdims=True))
    return dx, dy * xh


def _accum(ref, part, first):
    @pl.when(first)
    def _():
        ref[...] = part

    @pl.when(jnp.logical_not(first))
    def _():
        ref[...] += part


def _dot_nt(a, b):
    return lax.dot_general(a, b, (((1,), (1,)), ((), ())), preferred_element_type=f32)


def _dot_tn(a, b):
    return lax.dot_general(a, b, (((0,), (0,)), ((), ())), preferred_element_type=f32)


def _dot(a, b):
    return jnp.dot(a, b, preferred_element_type=f32)


def _split_dot(x, e_bf, n_split, nt=False):
    acc = None
    rem = x
    for s in range(n_split):
        hi = rem.astype(bf16)
        term = _dot_nt(hi, e_bf) if nt else _dot(hi, e_bf)
        acc = term if acc is None else acc + term
        if s + 1 < n_split:
            rem = rem - hi.astype(f32)
    return acc


def _sigmoid(x):
    return 1.0 / (1.0 + jnp.exp(-x))


def norm_matmul(x, g, w, tm, tn, out_dtype, name):
    t = x.shape[0]
    if w.ndim == 3:
        assert w.shape[2] == tn
        n = w.shape[0] * tn
        w_spec = pl.BlockSpec((None, D, tn), lambda i, j: (j, 0, 0))
    else:
        n = w.shape[1]
        w_spec = pl.BlockSpec((D, tn), lambda i, j: (0, j))

    def body(x_ref, g_ref, w_ref, o_ref, h_ref):
        @pl.when(pl.program_id(1) == 0)
        def _():
            h_ref[...] = _rms_fwd(x_ref[...], g_ref[...]).astype(bf16)

        o_ref[...] = _dot(h_ref[...], w_ref[...]).astype(out_dtype)

    return pl.pallas_call(
        body, name=name, grid=(t // tm, n // tn),
        in_specs=[pl.BlockSpec((tm, D), lambda i, j: (i, 0)), pl.BlockSpec((1, D), lambda i, j: (0, 0)), w_spec],
        out_specs=[pl.BlockSpec((tm, tn), lambda i, j: (i, j)), pl.BlockSpec((tm, D), lambda i, j: (i, 0))],
        out_shape=[_sds((t, n), out_dtype), _sds((t, D), bf16)],
        compiler_params=_cparams(2))(x, g, w)


def matmul_postnorm(a, w, xres, g, tm, relu2, name):
    t, k = a.shape

    def body(a_ref, w_ref, xr_ref, g_ref, y_ref, xo_ref):
        av = a_ref[...]
        if relu2:
            af = jnp.maximum(av.astype(f32), 0.0)
            av = (af * af).astype(bf16)
        y = _dot(av, w_ref[...])
        y_ref[...] = y
        xo_ref[...] = xr_ref[...] + _rms_fwd(y, g_ref[...])

    return pl.pallas_call(
        body, name=name, grid=(t // tm,),
        in_specs=[pl.BlockSpec((tm, k), lambda i: (i, 0)), pl.BlockSpec((k, D), lambda i: (0, 0)),
                  pl.BlockSpec((tm, D), lambda i: (i, 0)), pl.BlockSpec((1, D), lambda i: (0, 0))],
        out_specs=[pl.BlockSpec((tm, D), lambda i: (i, 0)), pl.BlockSpec((tm, D), lambda i: (i, 0))],
        out_shape=[_sds((t, D), f32), _sds((t, D), f32)],
        compiler_params=_cparams(1))(a, w, xres, g)


def postnorm_bwd_matmul(y, g, dxo, w, fp, tm, tn, out_dtype, name):
    t, n = y.shape[0], w.shape[0]
    relu = fp is not None

    def body(*refs):
        if relu:
            y_ref, g_ref, dxo_ref, w_ref, fp_ref, dy_ref, dg_ref, da_ref = refs
        else:
            y_ref, g_ref, dxo_ref, w_ref, dy_ref, dg_ref, da_ref = refs
        i, j = pl.program_id(0), pl.program_id(1)

        @pl.when(j == 0)
        def _():
            dx, dgc = _rms_bwd(y_ref[...], g_ref[...], dxo_ref[...])
            dy_ref[...] = dx.astype(bf16)
            _accum(dg_ref, jnp.sum(dgc, axis=0, keepdims=True), i == 0)

        da = _dot_nt(dy_ref[...], w_ref[...])
        if relu:
            da = da * (2.0 * jnp.maximum(fp_ref[...].astype(f32), 0.0))
        da_ref[...] = da.astype(out_dtype)

    in_specs = [pl.BlockSpec((tm, D), lambda i, j: (i, 0)), pl.BlockSpec((1, D), lambda i, j: (0, 0)),
                pl.BlockSpec((tm, D), lambda i, j: (i, 0)), pl.BlockSpec((tn, D), lambda i, j: (j, 0))]
    args = [y, g, dxo, w]
    if relu:
        in_specs.append(pl.BlockSpec((tm, tn), lambda i, j: (i, j)))
        args.append(fp)
    return pl.pallas_call(
        body, name=name, grid=(t // tm, n // tn), in_specs=in_specs,
        out_specs=[pl.BlockSpec((tm, D), lambda i, j: (i, 0)), pl.BlockSpec((1, D), lambda i, j: (0, 0)),
                   pl.BlockSpec((tm, tn), lambda i, j: (i, j))],
        out_shape=[_sds((t, D), bf16), _sds((1, D), f32), _sds((t, n), out_dtype)],
        compiler_params=_cparams(2))(*args)


def matmul_prenorm_bwd(da, w, x, g, dxo, tm, name):
    t, k = da.shape
    blocked = w.ndim == 3

    def body(da_ref, w_ref, x_ref, g_ref, dxo_ref, dx_ref, dg_ref):
        if blocked:
            kc = w.shape[2]
            dh = _dot_nt(da_ref[:, 0:kc], w_ref[0])
            for q in range(1, w.shape[0]):
                dh = dh + _dot_nt(da_ref[:, q * kc:(q + 1) * kc], w_ref[q])
        else:
            dh = _dot_nt(da_ref[...], w_ref[...])
        dxn, dgc = _rms_bwd(x_ref[...], g_ref[...], dh)
        dx_ref[...] = dxo_ref[...] + dxn
        _accum(dg_ref, jnp.sum(dgc, axis=0, keepdims=True), pl.program_id(0) == 0)

    w_spec = pl.BlockSpec(w.shape, (lambda i: (0, 0, 0)) if blocked else (lambda i: (0, 0)))
    return pl.pallas_call(
        body, name=name, grid=(t // tm,),
        in_specs=[pl.BlockSpec((tm, k), lambda i: (i, 0)), w_spec,
                  pl.BlockSpec((tm, D), lambda i: (i, 0)), pl.BlockSpec((1, D), lambda i: (0, 0)),
                  pl.BlockSpec((tm, D), lambda i: (i, 0))],
        out_specs=[pl.BlockSpec((tm, D), lambda i: (i, 0)), pl.BlockSpec((1, D), lambda i: (0, 0))],
        out_shape=[_sds((t, D), f32), _sds((1, D), f32)],
        compiler_params=_cparams(1))(da, w, x, g, dxo)


def matmul_tn(a, b, tm, tn, relu2, name, col_blocks=False):
    t, m = a.shape
    n = b.shape[1]
    if col_blocks:
        out_spec, out_shape = pl.BlockSpec((None, tm, tn), lambda i, j: (j, i, 0)), _sds((n // tn, m, tn), f32)
    else:
        out_spec, out_shape = pl.BlockSpec((tm, tn), lambda i, j: (i, j)), _sds((m, n), f32)

    def body(a_ref, b_ref, o_ref, at_ref):
        @pl.when(pl.program_id(1) == 0)
        def _():
            av = a_ref[...]
            if relu2:
                af = jnp.maximum(av.astype(f32), 0.0)
                av = (af * af).astype(bf16)
            at_ref[...] = av.T

        o_ref[...] = _dot(at_ref[...], b_ref[...])

    return pl.pallas_call(
        body, name=name, grid=(m // tm, n // tn),
        in_specs=[pl.BlockSpec((t, tm), lambda i, j: (0, i)), pl.BlockSpec((t, tn), lambda i, j: (0, j))],
        out_specs=out_spec, out_shape=out_shape,
        scratch_shapes=[pltpu.VMEM((tm, t), bf16)],
        compiler_params=_cparams(2))(a, b)


def _halo_prev(tb, col):
    return lambda i: (jnp.maximum(i * (tb // HALO) - 1, 0), col)


def _halo_next(tb, col, t):
    return lambda i: (jnp.minimum((i + 1) * (tb // HALO), t // HALO - 1), col)


def group_a_fwd(proj, wa, g, seq, tb, name):
    t = proj.shape[0]
    bps = seq // tb

    def body(xa_ref, ca_ref, ba_ref, xah_ref, cah_ref, wa_ref, g_ref, o_ref, u_scr):
        first = (pl.program_id(0) % bps) == 0
        u = ca_ref[...] * xa_ref[...]
        u_scr[0:HALO, :] = jnp.where(first, 0.0, cah_ref[...] * xah_ref[...])
        u_scr[HALO:HALO + tb, :] = u
        w = wa_ref[...]
        cv = w[2:3] * u + w[1:2] * u_scr[pl.ds(HALO - 1, tb), :] + w[0:1] * u_scr[pl.ds(HALO - 2, tb), :]
        o_ref[...] = _rms_fwd(ba_ref[...] * cv, g_ref[...]).astype(bf16)

    blk = lambda c: pl.BlockSpec((tb, D), lambda i: (i, c))
    return pl.pallas_call(
        body, name=name, grid=(t // tb,),
        in_specs=[blk(0), blk(1), blk(2),
                  pl.BlockSpec((HALO, D), _halo_prev(tb, 0)), pl.BlockSpec((HALO, D), _halo_prev(tb, 1)),
                  pl.BlockSpec((8, D), lambda i: (0, 0)), pl.BlockSpec((1, D), lambda i: (0, 0))],
        out_specs=pl.BlockSpec((tb, D), lambda i: (i, 0)),
        out_shape=_sds((t, D), bf16),
        scratch_shapes=[pltpu.VMEM((tb + HALO, D), f32)],
        compiler_params=_cparams(1))(proj, proj, proj, proj, proj, wa, g)


def group_a_bwd(proj, dcat, wa, g, seq, tb, name):
    t = proj.shape[0]
    bps = seq // tb

    def body(xa_ref, ca_ref, ba_ref, dy_ref, xap_ref, cap_ref, xan_ref, can_ref, ban_ref, dyn_ref, wa_ref, g_ref,
             dp_ref, dwa_ref, dg_ref, u_scr, d_scr):
        i = pl.program_id(0)
        first = (i % bps) == 0
        last = (i % bps) == bps - 1
        w = wa_ref[...]
        gv = g_ref[...]
        xa, ca, ba = xa_ref[...], ca_ref[...], ba_ref[...]
        u_scr[0:HALO, :] = jnp.where(first, 0.0, cap_ref[...] * xap_ref[...])
        u_scr[HALO:HALO + tb, :] = ca * xa
        u_scr[HALO + tb:2 * HALO + tb, :] = can_ref[...] * xan_ref[...]

        def conv(start, rows):
            return (w[2:3] * u_scr[pl.ds(start, rows), :] + w[1:2] * u_scr[pl.ds(start - 1, rows), :]
                    + w[0:1] * u_scr[pl.ds(start - 2, rows), :])

        cv = conv(HALO, tb)
        dya, dgc = _rms_bwd(ba * cv, gv, dy_ref[...])
        ban = ban_ref[...]
        dyan, _ = _rms_bwd(ban * conv(HALO + tb, HALO), gv, dyn_ref[...])
        dcv = dya * ba
        d_scr[0:tb, :] = dcv
        d_scr[tb:tb + HALO, :] = jnp.where(last, 0.0, dyan * ban)
        du = w[2:3] * dcv + w[1:2] * d_scr[pl.ds(1, tb), :] + w[0:1] * d_scr[pl.ds(2, tb), :]
        dp_ref[:, 0:D] = (du * ca).astype(bf16)
        dp_ref[:, D:2 * D] = (du * xa).astype(bf16)
        dp_ref[:, 2 * D:3 * D] = (dya * cv).astype(bf16)
        row = lax.broadcasted_iota(jnp.int32, (8, D), 0)
        dw = jnp.zeros((8, D), f32)
        for k in range(3):
            s = jnp.sum(dcv * u_scr[pl.ds(HALO - 2 + k, tb), :], axis=0, keepdims=True)
            dw = jnp.where(row == k, s, dw)
        _accum(dwa_ref, dw, i == 0)
        _accum(dg_ref, jnp.sum(dgc, axis=0, keepdims=True), i == 0)

    blk = lambda c: pl.BlockSpec((tb, D), lambda i: (i, c))
    prv = lambda c: pl.BlockSpec((HALO, D), _halo_prev(tb, c))
    nxt = lambda c: pl.BlockSpec((HALO, D), _halo_next(tb, c, t))
    return pl.pallas_call(
        body, name=name, grid=(t // tb,),
        in_specs=[blk(0), blk(1), blk(2), blk(0), prv(0), prv(1), nxt(0), nxt(1), nxt(2), nxt(0),
                  pl.BlockSpec((8, D), lambda i: (0, 0)), pl.BlockSpec((1, D), lambda i: (0, 0))],
        out_specs=[pl.BlockSpec((tb, 3 * D), lambda i: (i, 0)), pl.BlockSpec((8, D), lambda i: (0, 0)),
                   pl.BlockSpec((1, D), lambda i: (0, 0))],
        out_shape=[_sds((t, 3 * D), bf16), _sds((8, D), f32), _sds((1, D), f32)],
        scratch_shapes=[pltpu.VMEM((tb + 2 * HALO, D), f32), pltpu.VMEM((tb + HALO, D), f32)],
        compiler_params=_cparams(1))(proj, proj, proj, dcat, proj, proj, proj, proj, proj, dcat, wa, g)


CB = 512
XBC_BLK0 = COL_XBC // CB


def conv_b_fwd(proj, ws, bs, seq, tb, name):
    t = proj.shape[0]
    bps = seq // tb

    def body(x_ref, xp_ref, w_ref, b_ref, o_ref, x_scr):
        first = (pl.program_id(1) % bps) == 0
        x = x_ref[...]
        x_scr[0:HALO, :] = jnp.where(first, 0.0, xp_ref[...])
        x_scr[HALO:HALO + tb, :] = x
        w = w_ref[...]
        xc = w[3:4] * x + b_ref[...]
        for k in range(3):
            xc = xc + w[k:k + 1] * x_scr[pl.ds(HALO - 3 + k, tb), :]
        o_ref[...] = xc * _sigmoid(xc)

    return pl.pallas_call(
        body, name=name, grid=(XBC // CB, t // tb),
        in_specs=[pl.BlockSpec((tb, CB), lambda j, i: (i, XBC_BLK0 + j)),
                  pl.BlockSpec((HALO, CB), lambda j, i: (jnp.maximum(i * (tb // HALO) - 1, 0), XBC_BLK0 + j)),
                  pl.BlockSpec((8, CB), lambda j, i: (0, j)), pl.BlockSpec((1, CB), lambda j, i: (0, j))],
        out_specs=pl.BlockSpec((tb, CB), lambda j, i: (i, j)),
        out_shape=_sds((t, XBC), f32),
        scratch_shapes=[pltpu.VMEM((tb + HALO, CB), f32)],
        compiler_params=_cparams(2))(proj, proj, ws, bs)


def conv_b_bwd(proj, dxs, ws, bs, seq, tb, name):
    t = proj.shape[0]
    bps = seq // tb

    def body(x_ref, xp_ref, xn_ref, d_ref, dn_ref, w_ref, b_ref, dx_ref, dw_ref, db_ref, x_scr, d_scr):
        i = pl.program_id(1)
        first = (i % bps) == 0
        last = (i % bps) == bps - 1
        w = w_ref[...]
        bias = b_ref[...]
        x_scr[0:HALO, :] = jnp.where(first, 0.0, xp_ref[...])
        x_scr[HALO:HALO + tb, :] = x_ref[...]
        x_scr[HALO + tb:2 * HALO + tb, :] = xn_ref[...]

        def dsilu_at(start, rows, d):
            xc = bias + w[3:4] * x_scr[pl.ds(start, rows), :]
            for k in range(3):
                xc = xc + w[k:k + 1] * x_scr[pl.ds(start - 3 + k, rows), :]
            sg = _sigmoid(xc)
            return d * (sg * (1.0 + xc * (1.0 - sg)))

        dxc = dsilu_at(HALO, tb, d_ref[...])
        d_scr[0:tb, :] = dxc
        d_scr[tb:tb + HALO, :] = jnp.where(last, 0.0, dsilu_at(HALO + tb, HALO, dn_ref[...]))
        dx = w[3:4] * dxc
        for k in range(3):
            dx = dx + w[k:k + 1] * d_scr[pl.ds(3 - k, tb), :]
        dx_ref[...] = dx.astype(bf16)
        row = lax.broadcasted_iota(jnp.int32, (8, CB), 0)
        dw = jnp.zeros((8, CB), f32)
        for k in range(4):
            s = jnp.sum(dxc * x_scr[pl.ds(HALO - 3 + k, tb), :], axis=0, keepdims=True)
            dw = jnp.where(row == k, s, dw)
        _accum(dw_ref, dw, i == 0)
        _accum(db_ref, jnp.sum(dxc, axis=0, keepdims=True), i == 0)

    nh = t // HALO
    return pl.pallas_call(
        body, name=name, grid=(XBC // CB, t // tb),
        in_specs=[pl.BlockSpec((tb, CB), lambda j, i: (i, XBC_BLK0 + j)),
                  pl.BlockSpec((HALO, CB), lambda j, i: (jnp.maximum(i * (tb // HALO) - 1, 0), XBC_BLK0 + j)),
                  pl.BlockSpec((HALO, CB), lambda j, i: (jnp.minimum((i + 1) * (tb // HALO), nh - 1), XBC_BLK0 + j)),
                  pl.BlockSpec((tb, CB), lambda j, i: (i, j)),
                  pl.BlockSpec((HALO, CB), lambda j, i: (jnp.minimum((i + 1) * (tb // HALO), nh - 1), j)),
                  pl.BlockSpec((8, CB), lambda j, i: (0, j)), pl.BlockSpec((1, CB), lambda j, i: (0, j))],
        out_specs=[pl.BlockSpec((tb, CB), lambda j, i: (i, j)), pl.BlockSpec((8, CB), lambda j, i: (0, j)),
                   pl.BlockSpec((1, CB), lambda j, i: (0, j))],
        out_shape=[_sds((t, XBC), bf16), _sds((8, XBC), f32), _sds((1, XBC), f32)],
        scratch_shapes=[pltpu.VMEM((tb + 2 * HALO, CB), f32), pltpu.VMEM((tb + HALO, CB), f32)],
        compiler_params=_cparams(2))(proj, proj, proj, dxs, dxs, ws, bs)


GW = D // NG


def _ssd_consts():
    head_of_lane = jnp.arange(D) // HP
    expand = (jnp.arange(CH)[:, None] == head_of_lane[None, :]).astype(bf16)
    tri = (jnp.arange(CH)[:, None] >= jnp.arange(CH)[None, :]).astype(f32)
    return expand, tri


def _ssd_common(par_ref, dtr_ref, e_ref, tri_ref):
    par = par_ref[...]
    dtb, alog, dsk = par[0:1], par[1:2], par[2:3]
    lane = lax.broadcasted_iota(jnp.int32, (CH, CH), 1)
    a = -jnp.exp(alog)
    dtr = dtr_ref[...] + dtb
    sp = jnp.maximum(dtr, 0.0) + jnp.log(1.0 + jnp.exp(-jnp.abs(dtr)))
    dt = jnp.where(lane < NH, sp, 0.0)
    cs = jnp.dot(tri_ref[...], dt * a, precision=lax.Precision.HIGHEST, preferred_element_type=f32)
    cs_last = cs[CH - 1:CH, :]
    dte = jnp.exp(cs_last - cs)
    ecs = jnp.exp(cs)
    ecl = jnp.exp(cs_last)
    e = e_ref[...]
    row8 = lax.broadcasted_iota(jnp.int32, (8, CH), 0)
    r8 = _split_dot(jnp.where(row8 == 0, ecl, jnp.where(row8 == 1, dsk, 0.0)), e, 3)
    return dict(a=a, dtr=dtr, dt=dt, cs=cs, cst=cs.T, dte=dte, ecs=ecs, ecl=ecl, e=e, lane=lane,
                dt_x=_split_dot(dt, e, 3), dte_x=_split_dot(dte, e, 3), ecs_x=_split_dot(ecs, e, 3),
                ecl_x=r8[0:1], dsk_x=r8[1:2])


def _decay_matrix(c, h):
    li = lax.broadcasted_iota(jnp.int32, (CH, CH), 0)
    seg = c["cs"][:, h:h + 1] - c["cst"][h:h + 1, :]
    return jnp.exp(jnp.where(li >= c["lane"], seg, -jnp.inf))


def _gate_norm_fwd(y, z, gs):
    zg = z * _sigmoid(z)
    yg = y * zg
    return jnp.concatenate([_rms_fwd(yg[:, k * GW:(k + 1) * GW], gs[:, k * GW:(k + 1) * GW]) for k in range(NG)], axis=1)


def ssd_fwd(xbcs, proj, par, gs, seq, name):
    t = xbcs.shape[0]
    nc = seq // CH
    expand, tri = _ssd_consts()

    def body(xs_ref, b_ref, c_ref, dtr_ref, z_ref, par_ref, e_ref, tri_ref, gs_ref, yn_ref, y_ref, st_ref, p_scr, yd_scr):
        @pl.when(pl.program_id(0) % nc == 0)
        def _():
            p_scr[...] = jnp.zeros_like(p_scr)

        c = _ssd_common(par_ref, dtr_ref, e_ref, tri_ref)
        xs = xs_ref[...]
        xdt = xs * c["dt_x"]
        xdt_b = xdt.astype(bf16)
        xdte_b = (xdt * c["dte_x"]).astype(bf16)
        p = p_scr[...]
        st_ref[0] = p
        p_b = p.astype(bf16)
        lo = c["lane"] < HP
        for g in range(NG):
            bg = b_ref[:, g * NS:(g + 1) * NS].astype(bf16)
            cg = c_ref[:, g * NS:(g + 1) * NS].astype(bf16)
            gmat = _dot_nt(cg, bg)
            for q in range(GW // CH):
                col = g * GW + q * CH
                xp = xdt_b[:, col:col + CH]
                h0 = col // HP
                m0 = (gmat * _decay_matrix(c, h0)).astype(bf16)
                m1 = (gmat * _decay_matrix(c, h0 + 1)).astype(bf16)
                yd_scr[:, col:col + CH] = (_dot(m0, jnp.where(lo, xp, jnp.zeros_like(xp)))
                                           + _dot(m1, jnp.where(lo, jnp.zeros_like(xp), xp)))
            gsl = slice(g * GW, (g + 1) * GW)
            yoff = _dot(cg, p_b[:, gsl]) * c["ecs_x"][:, gsl]
            yd_scr[:, gsl] = yd_scr[:, gsl] + yoff
            p_scr[:, gsl] = p[:, gsl] * c["ecl_x"][:, gsl] + _dot_tn(bg, xdte_b[:, gsl])
        y = yd_scr[...] + c["dsk_x"] * xs
        y_ref[...] = y
        yn_ref[...] = _gate_norm_fwd(y, z_ref[...], gs_ref[...]).astype(bf16)

    nb = t // CH
    return pl.pallas_call(
        body, name=name, grid=(nb,),
        in_specs=[pl.BlockSpec((CH, D), lambda i: (i, 0)),
                  pl.BlockSpec((CH, NG * NS), lambda i: (i, D // (NG * NS))),
                  pl.BlockSpec((CH, NG * NS), lambda i: (i, D // (NG * NS) + 1)),
                  pl.BlockSpec((CH, CH), lambda i: (i, COL_DT // CH)),
                  pl.BlockSpec((CH, D), lambda i: (i, COL_Z // D)),
                  pl.BlockSpec((8, CH), lambda i: (0, 0)), pl.BlockSpec((CH, D), lambda i: (0, 0)),
                  pl.BlockSpec((CH, CH), lambda i: (0, 0)), pl.BlockSpec((1, D), lambda i: (0, 0))],
        out_specs=[pl.BlockSpec((CH, D), lambda i: (i, 0)), pl.BlockSpec((CH, D), lambda i: (i, 0)),
                   pl.BlockSpec((1, NS, D), lambda i: (i, 0, 0))],
        out_shape=[_sds((t, D), bf16), _sds((t, D), f32), _sds((nb, NS, D), f32)],
        scratch_shapes=[pltpu.VMEM((NS, D), f32), pltpu.VMEM((CH, D), f32)],
        compiler_params=_cparams(1))(xbcs, xbcs, xbcs, proj, proj, par, expand, tri, gs)


def ssd_bwd(xbcs, proj, ypre, states, dcat, par, gs, seq, name):
    t = xbcs.shape[0]
    nc = seq // CH
    expand, tri = _ssd_consts()

    def body(xs_ref, b_ref, c_ref, dtr_ref, z_ref, y_ref, st_ref, dyn_ref, par_ref, e_ref, tri_ref, gs_ref,
             dx_ref, dz_ref, ddt_ref, dpar_ref, dgs_ref, dp_scr, dxdt_scr):
        i = pl.program_id(0)

        @pl.when(i % nc == 0)
        def _():
            dp_scr[...] = jnp.zeros_like(dp_scr)

        c = _ssd_common(par_ref, dtr_ref, e_ref, tri_ref)
        e = c["e"]
        lane = c["lane"]
        sub = lax.broadcasted_iota(jnp.int32, (CH, CH), 0)
        xs = xs_ref[...]
        xdt = xs * c["dt_x"]
        xdt_b = xdt.astype(bf16)
        xdte_b = (xdt * c["dte_x"]).astype(bf16)
        p = st_ref[0]
        p_b = p.astype(bf16)
        dpn = dp_scr[...]
        dpn_b = dpn.astype(bf16)

        y, z, gs_v = y_ref[...], z_ref[...], gs_ref[...]
        zs = _sigmoid(z)
        zg = z * zs
        yg = y * zg
        parts, gparts = [], []
        for k in range(NG):
            sl = slice(k * GW, (k + 1) * GW)
            dxk, dgk = _rms_bwd(yg[:, sl], gs_v[:, sl], dyn_ref[:, sl])
            parts.append(dxk)
            gparts.append(dgk)
        dyg = jnp.concatenate(parts, axis=1)
        dgs_rows = jnp.concatenate(gparts, axis=1)
        dy = dyg * zg
        dz_ref[...] = (dyg * y * (zs * (1.0 + z * (1.0 - zs)))).astype(bf16)
        dy_b = dy.astype(bf16)
        dq_b = (dy * c["ecs_x"]).astype(bf16)

        lo = lane < HP
        dcs = jnp.zeros((CH, CH), f32)
        dcst = jnp.zeros((CH, CH), f32)
        for g in range(NG):
            gsl = slice(g * GW, (g + 1) * GW)
            bg = b_ref[:, g * NS:(g + 1) * NS].astype(bf16)
            cg = c_ref[:, g * NS:(g + 1) * NS].astype(bf16)
            gmat = _dot_nt(cg, bg)
            dgm = jnp.zeros((CH, CH), f32)
            for q in range(GW // CH):
                col = g * GW + q * CH
                xp = xdt_b[:, col:col + CH]
                dyp = dy_b[:, col:col + CH]
                acc = None
                for hh in range(2):
                    h = col // HP + hh
                    keep = lo if hh == 0 else jnp.logical_not(lo)
                    dyh = jnp.where(keep, dyp, jnp.zeros_like(dyp))
                    dec = _decay_matrix(c, h)
                    m = gmat * dec
                    dm = _dot_nt(dyh, xp)
                    dseg = dm * m
                    dcs = dcs + jnp.where(lane == h, jnp.sum(dseg, axis=1, keepdims=True), 0.0)
                    dcst = dcst + jnp.where(sub == h, jnp.sum(dseg, axis=0, keepdims=True), 0.0)
                    dgm = dgm + dm * dec
                    term = _dot_tn(m.astype(bf16), dyh)
                    acc = term if acc is None else acc + term
                dxdt_scr[:, col:col + CH] = acc
            dgm_b = dgm.astype(bf16)
            bds = _dot(bg, dpn_b[:, gsl])
            dxdt_scr[:, gsl] = dxdt_scr[:, gsl] + c["dte_x"][:, gsl] * bds
            dc_g = _dot(dgm_b, bg) + _dot_nt(dq_b[:, gsl], p_b[:, gsl])
            db_g = _dot_tn(dgm_b, cg) + _dot_nt(xdte_b[:, gsl], dpn_b[:, gsl])
            dx_ref[:, D + g * NS:D + (g + 1) * NS] = db_g
            dx_ref[:, D + NG * NS + g * NS:D + NG * NS + (g + 1) * NS] = dc_g
            dp_scr[:, gsl] = dpn[:, gsl] * c["ecl_x"][:, gsl] + _dot_tn(cg, dq_b[:, gsl])
            q_g = _dot(cg, p_b[:, gsl])
            e_g = e[:, gsl]
            dcs = dcs + c["ecs"] * _split_dot(dy[:, gsl] * q_g, e_g, 2, nt=True)
            ddte = _split_dot(xdt[:, gsl] * bds, e_g, 2, nt=True) * c["dte"]
            dcs = dcs - ddte
            dcs = dcs + jnp.where(sub == CH - 1, jnp.sum(ddte, axis=0, keepdims=True), 0.0)

        decl = _split_dot(jnp.broadcast_to(jnp.sum(dpn * p, axis=0, keepdims=True), (8, D)), e, 2, nt=True)[0:1]
        dcs = dcs + jnp.where(sub == CH - 1, c["ecl"] * decl, 0.0)
        dcs = dcs - dcst.T
        dadt = lax.dot_general(tri_ref[...], dcs, (((0,), (0,)), ((), ())), precision=lax.Precision.HIGHEST,
                               preferred_element_type=f32)
        dxdt = dxdt_scr[...]
        ddt = dadt * c["a"] + _split_dot(dxdt * xs, e, 2, nt=True)
        ddtr = jnp.where(lane < NH, ddt * _sigmoid(c["dtr"]), 0.0)
        ddt_ref[...] = ddtr.astype(bf16)
        dx_ref[:, 0:D] = dxdt * c["dt_x"] + c["dsk_x"] * dy
        dsk = _split_dot(jnp.broadcast_to(jnp.sum(dy * xs, axis=0, keepdims=True), (8, D)), e, 2, nt=True)[0:1]
        dalog = jnp.sum(dadt * c["dt"], axis=0, keepdims=True) * c["a"]
        row8 = lax.broadcasted_iota(jnp.int32, (8, CH), 0)
        dpar = jnp.where(row8 == 0, jnp.sum(ddtr, axis=0, keepdims=True),
                         jnp.where(row8 == 1, dalog, jnp.where(row8 == 2, dsk, 0.0)))
        dpar = jnp.where(lax.broadcasted_iota(jnp.int32, (8, CH), 1) < NH, dpar, 0.0)
        _accum(dpar_ref, dpar, i == 0)
        _accum(dgs_ref, jnp.sum(dgs_rows, axis=0, keepdims=True), i == 0)

    nb = t // CH
    rev = lambda i: (i // nc) * nc + (nc - 1 - i % nc)
    return pl.pallas_call(
        body, name=name, grid=(nb,),
        in_specs=[pl.BlockSpec((CH, D), lambda i: (rev(i), 0)),
                  pl.BlockSpec((CH, NG * NS), lambda i: (rev(i), D // (NG * NS))),
                  pl.BlockSpec((CH, NG * NS), lambda i: (rev(i), D // (NG * NS) + 1)),
                  pl.BlockSpec((CH, CH), lambda i: (rev(i), COL_DT // CH)),
                  pl.BlockSpec((CH, D), lambda i: (rev(i), COL_Z // D)),
                  pl.BlockSpec((CH, D), lambda i: (rev(i), 0)),
                  pl.BlockSpec((1, NS, D), lambda i: (rev(i), 0, 0)),
                  pl.BlockSpec((CH, D), lambda i: (rev(i), 1)),
                  pl.BlockSpec((8, CH), lambda i: (0, 0)), pl.BlockSpec((CH, D), lambda i: (0, 0)),
                  pl.BlockSpec((CH, CH), lambda i: (0, 0)), pl.BlockSpec((1, D), lambda i: (0, 0))],
        out_specs=[pl.BlockSpec((CH, XBC), lambda i: (rev(i), 0)), pl.BlockSpec((CH, D), lambda i: (rev(i), 0)),
                   pl.BlockSpec((CH, CH), lambda i: (rev(i), 0)),
                   pl.BlockSpec((8, CH), lambda i: (0, 0)), pl.BlockSpec((1, D), lambda i: (0, 0))],
        out_shape=[_sds((t, XBC), f32), _sds((t, D), bf16), _sds((t, CH), bf16), _sds((8, CH), f32), _sds((1, D), f32)],
        scratch_shapes=[pltpu.VMEM((NS, D), f32), pltpu.VMEM((CH, D), f32)],
        compiler_params=_cparams(1))(xbcs, xbcs, xbcs, proj, proj, ypre, states, dcat, par, expand, tri, gs)


def loss_head(y, target, tb, name):
    t = y.shape[0]

    def body(y_ref, t_ref, s_ref, dy_ref):
        err = y_ref[...] - t_ref[...]
        dy_ref[...] = err * (1.0 / D)
        _accum(s_ref, jnp.zeros((8, CH), f32) + jnp.sum(err * err), pl.program_id(0) == 0)

    return pl.pallas_call(
        body, name=name, grid=(t // tb,),
        in_specs=[pl.BlockSpec((tb, D), lambda i: (i, 0)), pl.BlockSpec((tb, D), lambda i: (i, 0))],
        out_specs=[pl.BlockSpec((8, CH), lambda i: (0, 0)), pl.BlockSpec((tb, D), lambda i: (i, 0))],
        out_shape=[_sds((8, CH), f32), _sds((t, D), f32)],
        compiler_params=_cparams(1))(y, target)


def _tiles(t, seq):
    tm = min(512, t)
    return dict(tm=tm, tm_small=min(256, t), tb=min(512, seq), tb_conv=min(512, seq))


def local_step(x, target, w, seq):
    t = x.shape[0]
    ts = _tiles(t, seq)
    tm, tb = ts["tm"], ts["tb"]
    depth = len(w["win"])
    saved = []
    for l in range(depth):
        proj, h1 = norm_matmul(x, w["g1"][l], w["win"][l], tm, 1152, f32, "in_proj")
        ya = group_a_fwd(proj, w["wa"][l], w["ga"][l], seq, tb, "group_a_fwd")
        xbcs = conv_b_fwd(proj, w["ws"][l], w["bs"][l], seq, tb, "conv_b_fwd")
        ys, ypre, states = ssd_fwd(xbcs, proj, w["par"][l], w["gs"][l], seq, "ssd_fwd")
        cat = jnp.concatenate([ya, ys], axis=1)
        mix, x2 = matmul_postnorm(cat, w["wo"][l], x, w["g2"][l], tm, False, "out_proj")
        fp, h2 = norm_matmul(x2, w["g3"][l], w["wu"][l], tm, 1024, bf16, "mlp_up")
        o, x3 = matmul_postnorm(fp, w["wd"][l], x2, w["g4"][l], tm, True, "mlp_down")
        saved.append(dict(x=x, proj=proj, h1=h1, xbcs=xbcs, ypre=ypre, states=states, cat=cat, mix=mix, x2=x2,
                          fp=fp, h2=h2, o=o))
        x = x3
    sse, dx = loss_head(x, target, tm, "loss_head")
    grads = [None] * depth
    for l in reversed(range(depth)):
        s = saved[l]
        do, dg4, dfp = postnorm_bwd_matmul(s["o"], w["g4"][l], dx, w["wd"][l], s["fp"], tm, 1024, bf16, "mlp_down_bwd")
        dwd = matmul_tn(s["fp"], do, 512, 1024, True, "mlp_down_dw")
        dx2, dg3 = matmul_prenorm_bwd(dfp, w["wu"][l], s["x2"], w["g3"][l], dx, tm, "mlp_up_bwd")
        dwu = matmul_tn(s["h2"], dfp, 512, 1024, False, "mlp_up_dw", col_blocks=True)
        dmix, dg2, dcat = postnorm_bwd_matmul(s["mix"], w["g2"][l], dx2, w["wo"][l], None, tm, 1024, f32, "out_proj_bwd")
        dwo = matmul_tn(s["cat"], dmix, 512, 1024, False, "out_proj_dw")
        dpa, dwa, dga = group_a_bwd(s["proj"], dcat, w["wa"][l], w["ga"][l], seq, tb, "group_a_bwd")
        dxbcs, dz, ddt, dpar, dgs = ssd_bwd(s["xbcs"], s["proj"], s["ypre"], s["states"], dcat, w["par"][l],
                                            w["gs"][l], seq, "ssd_bwd")
        dxbc, dws, dbs = conv_b_bwd(s["proj"], dxbcs, w["ws"][l], w["bs"][l], seq, tb, "conv_b_bwd")
        dproj = jnp.concatenate([dpa, dz, dxbc, ddt], axis=1)
        dx, dg1 = matmul_prenorm_bwd(dproj, w["win"][l], s["x"], w["g1"][l], dx2, ts["tm_small"], "in_proj_bwd")
        dwin = matmul_tn(s["h1"], dproj, 512, 1152, False, "in_proj_dw")
        grads[l] = dict(win=dwin, wo=dwo, wu=dwu, wd=dwd, wa=dwa, ws=dws, bs=dbs, par=dpar,
                        g1=dg1, ga=dga, gs=dgs, g2=dg2, g3=dg3, g4=dg4)
    return sse, dx, grads


GROUPS = {
    "chips": [(1, 0, 0), (0, 1, 0), (1, 1, 0)],
    "pair": [(0, 0, 1)],
    "all": [(1, 0, 0), (0, 1, 0), (1, 1, 0), (0, 0, 1), (1, 0, 1), (0, 1, 1), (1, 1, 1)],
}


def _group_index(group, x, y, c):
    return {"chips": 2 * x + y, "pair": c, "all": 4 * x + 2 * y + c}[group]


def _chunk_indices(shape, pieces):
    if len(shape) < 3:
        return [()]
    lead = [()]
    for n in shape[:-2]:
        lead = [i + (k,) for i in lead for k in range(n)]
    rows = shape[-2]
    split = max(1, pieces // len(lead))
    while split > 1 and (rows % split or (rows // split) % 16):
        split -= 1
    step = rows // split
    return [i + (pl.ds(s * step, step),) for i in lead for s in range(split)]


def _exchange(arrays, out_shapes, group, src_view, dst_view, view_shape, name, pieces=16):
    masks = GROUPS[group]
    na, nm = len(arrays), len(masks)
    cuts = [_chunk_indices(view_shape(a), pieces) for a in range(na)]

    def body(*refs):
        ins, outs = refs[:na], refs[na:2 * na]
        send_sems, recv_sems, local_sems = refs[2 * na:]
        x, y, c = lax.axis_index("x"), lax.axis_index("y"), lax.axis_index("c")
        me = _group_index(group, x, y, c)
        peers = []
        for mx, my, mc in masks:
            px, py, pc = (1 - x if mx else x), (1 - y if my else y), (1 - c if mc else c)
            peers.append(((px, py, pc), _group_index(group, px, py, pc)))

        def part(ref, idx):
            return ref.at[idx] if idx else ref

        for a in range(na):
            for idx in cuts[a]:
                pltpu.make_async_copy(part(src_view(ins[a], a, me), idx), part(dst_view(outs[a], a, me), idx),
                                      local_sems.at[a]).start()
        for a in range(na):
            for j, (dev, pidx) in enumerate(peers):
                for idx in cuts[a]:
                    pltpu.make_async_remote_copy(
                        src_ref=part(src_view(ins[a], a, pidx), idx), dst_ref=part(dst_view(outs[a], a, me), idx),
                        send_sem=send_sems.at[a * nm + j], recv_sem=recv_sems.at[a * nm + j],
                        device_id=dev, device_id_type=MESH).start()
        whole = []
        for a in range(na):
            for j, (dev, pidx) in enumerate(peers):
                whole.append(pltpu.make_async_remote_copy(
                    src_ref=src_view(ins[a], a, pidx), dst_ref=dst_view(outs[a], a, pidx),
                    send_sem=send_sems.at[a * nm + j], recv_sem=recv_sems.at[a * nm + j],
                    device_id=dev, device_id_type=MESH))
        for cp in whole:
            cp.wait_recv()
        for cp in whole:
            cp.wait_send()
        for a in range(na):
            pltpu.make_async_copy(src_view(ins[a], a, me), dst_view(outs[a], a, me), local_sems.at[a]).wait()

    hbm = pl.BlockSpec(memory_space=pltpu.HBM)
    return pl.pallas_call(
        body, name=name, in_specs=[hbm] * na, out_specs=[hbm] * na,
        out_shape=[_sds(s, a.dtype) for s, a in zip(out_shapes, arrays)],
        scratch_shapes=[pltpu.SemaphoreType.DMA((na * nm,)), pltpu.SemaphoreType.DMA((na * nm,)),
                        pltpu.SemaphoreType.DMA((na,))])(*arrays)


def all_gather(arrays, group, name, slot_axis=0):
    n = len(GROUPS[group]) + 1
    shapes = [a.shape[:slot_axis] + (n,) + a.shape[slot_axis:] for a in arrays]
    lead = (slice(None),) * slot_axis
    return _exchange(arrays, shapes, group, lambda r, a, i: r, lambda r, a, i: r.at[lead + (i,)],
                     lambda a: arrays[a].shape, name)


def all_to_all(arrays, group, name):
    return _exchange(arrays, [a.shape for a in arrays], group, lambda r, a, i: r.at[i], lambda r, a, i: r.at[i],
                     lambda a: arrays[a].shape[1:], name)


def sum_slots(y, out_dtype, name, tb=256):
    n, r, c = y.shape
    tb = min(tb, r)

    def body(y_ref, o_ref):
        acc = y_ref[0].astype(f32)
        for i in range(1, n):
            acc = acc + y_ref[i].astype(f32)
        o_ref[...] = acc.astype(out_dtype)

    return pl.pallas_call(
        body, name=name, grid=(r // tb,),
        in_specs=[pl.BlockSpec((n, tb, c), lambda i: (0, i, 0))], out_specs=pl.BlockSpec((tb, c), lambda i: (i, 0)),
        out_shape=_sds((r, c), out_dtype), compiler_params=_cparams(1))(y)


def adamw(w, g, m, v, name, tb=256):
    r, c = w.shape
    tb = min(tb, r)

    def body(w_ref, g_ref, m_ref, v_ref, d_ref, mo_ref, vo_ref):
        gv = g_ref[...]
        m2 = B1 * m_ref[...] + (1.0 - B1) * gv
        v2 = B2 * v_ref[...] + (1.0 - B2) * (gv * gv)
        m_hat = m2 / (1.0 - B1 ** STEP)
        v_hat = v2 / (1.0 - B2 ** STEP)
        d_ref[...] = -LR * (m_hat / (jnp.sqrt(v_hat) + AEPS) + WD * w_ref[...])
        mo_ref[...] = m2
        vo_ref[...] = v2

    spec = pl.BlockSpec((tb, c), lambda i: (i, 0))
    return pl.pallas_call(
        body, name=name, grid=(r // tb,), in_specs=[spec] * 4, out_specs=[spec] * 3,
        out_shape=[_sds((r, c), f32)] * 3, compiler_params=_cparams(1))(w, g, m, v)


N_CHIPS = 4
SMALL_ROW = 1024
SMALL_PIECES = (("g1", D), ("ga", D), ("gs", D), ("g2", D), ("g3", D), ("g4", D), ("wa", 3 * D), ("ws", 4 * XBC),
                ("bs", XBC), ("par", 3 * CH))
SMALL_LAYER = 17 * SMALL_ROW


def _pack_small(grads):
    rows = []
    for g in grads:
        flat = [g["g1"], g["ga"], g["gs"], g["g2"], g["g3"], g["g4"], g["wa"][:3], g["ws"][:4], g["bs"], g["par"][:3]]
        v = jnp.concatenate([p.reshape(-1) for p in flat])
        rows.append(jnp.pad(v, (0, SMALL_LAYER - v.shape[0])))
    packed = jnp.concatenate(rows).reshape(-1, SMALL_ROW)
    return jnp.pad(packed, ((0, -packed.shape[0] % 8), (0, 0)))


def _unpack_small(packed, depth):
    flat = packed.reshape(-1)[:depth * SMALL_LAYER].reshape(depth, SMALL_LAYER)
    out, off = {}, 0
    for key, size in SMALL_PIECES:
        out[key] = flat[:, off:off + size]
        off += size
    return out


def kernel(x, norm_mix_pre, w_in, conv_a_w, ssm_conv_w, ssm_conv_b, dt_bias, a_log, d_skip, conv_out_norm, ssm_out_norm, w_out, norm_mix_post, norm_mlp_pre, w_up, w_down, norm_mlp_post, loss_target, m_norm_mix_pre, m_w_in, m_conv_a_w, m_ssm_conv_w, m_ssm_conv_b, m_dt_bias, m_a_log, m_d_skip, m_conv_out_norm, m_ssm_out_norm, m_w_out, m_norm_mix_post, m_norm_mlp_pre, m_w_up, m_w_down, m_norm_mlp_post, v_norm_mix_pre, v_w_in, v_conv_a_w, v_ssm_conv_w, v_ssm_conv_b, v_dt_bias, v_a_log, v_d_skip, v_conv_out_norm, v_ssm_out_norm, v_w_out, v_norm_mix_post, v_norm_mlp_pre, v_w_up, v_w_down, v_norm_mlp_post):
    nb, seq, _ = x.shape
    t = nb * seq
    depth = w_in.shape[0]
    half = depth // 2
    ncol = w_in.shape[2]
    chip = 2 * lax.axis_index("x") + lax.axis_index("y")

    win_g, wo_g, wu_g, wd_g, wa_g, ws_g = all_gather(
        [w_in.astype(bf16), w_out.astype(bf16), w_up.astype(bf16), w_down.astype(bf16), conv_a_w, ssm_conv_w],
        "chips", "gather_weights", slot_axis=1)
    win_full = jnp.pad(jnp.transpose(win_g, (0, 2, 1, 3)).reshape(depth, D, N_CHIPS * ncol),
                       ((0, 0), (0, 0), (0, PROJ - N_CHIPS * ncol)))
    wa_full = jnp.transpose(wa_g, (0, 2, 1, 3)).reshape(depth, 3, D)
    ws_full = jnp.transpose(ws_g, (0, 2, 1, 3)).reshape(depth, 4, XBC)
    lane_pad = lambda a: jnp.pad(a, ((0, 0), (0, CH - a.shape[1])))
    par = jnp.stack([lane_pad(dt_bias), lane_pad(a_log), lane_pad(d_skip)], axis=1)
    par = jnp.pad(par, ((0, 0), (0, 5), (0, 0)))
    w = dict(
        win=[win_full[l] for l in range(depth)],
        wo=[wo_g[l].reshape(2 * D, D) for l in range(depth)],
        wu=[wu_g[l] for l in range(depth)],
        wd=[wd_g[l].reshape(DFF, D) for l in range(depth)],
        wa=[jnp.pad(wa_full[l], ((0, 5), (0, 0))) for l in range(depth)],
        ws=[jnp.pad(ws_full[l], ((0, 4), (0, 0))) for l in range(depth)],
        bs=[ssm_conv_b[l][None] for l in range(depth)],
        par=[par[l] for l in range(depth)],
        g1=[norm_mix_pre[l][None] for l in range(depth)], ga=[conv_out_norm[l][None] for l in range(depth)],
        gs=[ssm_out_norm[l][None] for l in range(depth)], g2=[norm_mix_post[l][None] for l in range(depth)],
        g3=[norm_mlp_pre[l][None] for l in range(depth)], g4=[norm_mlp_post[l][None] for l in range(depth)])

    sse, dx, grads = local_step(x.reshape(t, D), loss_target.reshape(t, D), w, seq)
    loss = lax.psum(0.5 / D * sse[0, 0], ("x", "y", "c"))

    def slots(per_layer, to_chip_major):
        g = jnp.stack([to_chip_major(a) for a in per_layer])
        g = g.reshape((2, half) + g.shape[1:])
        return jnp.transpose(g, (0, 2, 1, 3, 4))

    big = [
        slots([g["win"] for g in grads],
              lambda a: jnp.transpose(a[:, :N_CHIPS * ncol].reshape(D, N_CHIPS, ncol), (1, 0, 2))),
        slots([g["wo"] for g in grads], lambda a: a.reshape(N_CHIPS, 2 * D // N_CHIPS, D)),
        slots([g["wu"] for g in grads], lambda a: a),
        slots([g["wd"] for g in grads], lambda a: a.reshape(N_CHIPS, DFF // N_CHIPS, D)),
    ]
    pair = all_to_all(big, "pair", "grads_to_pair")
    pair_sum = [sum_slots(p.reshape(2, -1, p.shape[-1]), bf16, "pair_sum").reshape(p.shape[1:]) for p in pair]
    chips = all_to_all(pair_sum, "chips", "grads_to_chips")
    half_sum = [sum_slots(q.reshape(N_CHIPS, -1, q.shape[-1]), f32, "chip_sum").reshape(q.shape[1:]) for q in chips]
    full = all_gather(half_sum, "pair", "grads_from_pair")
    g_win, g_wo, g_wu, g_wd = [f.reshape((depth,) + f.shape[2:]) for f in full]

    small_all = all_gather([_pack_small(grads)], "all", "gather_small")[0]
    small = _unpack_small(sum_slots(small_all, f32, "small_sum", tb=8), depth)
    wa_cols, ws_cols = conv_a_w.shape[2], ssm_conv_w.shape[2]
    par_g = small["par"].reshape(depth, 3, CH)
    g_small = dict(
        norm_mix_pre=small["g1"], conv_out_norm=small["ga"], ssm_out_norm=small["gs"], norm_mix_post=small["g2"],
        norm_mlp_pre=small["g3"], norm_mlp_post=small["g4"], ssm_conv_b=small["bs"],
        conv_a_w=lax.dynamic_slice_in_dim(small["wa"].reshape(depth, 3, D), chip * wa_cols, wa_cols, axis=2),
        ssm_conv_w=lax.dynamic_slice_in_dim(small["ws"].reshape(depth, 4, XBC), chip * ws_cols, ws_cols, axis=2),
        dt_bias=par_g[:, 0, :NH], a_log=par_g[:, 1, :NH], d_skip=par_g[:, 2, :NH])

    given = dict(norm_mix_pre=(norm_mix_pre, m_norm_mix_pre, v_norm_mix_pre), w_in=(w_in, m_w_in, v_w_in),
                 conv_a_w=(conv_a_w, m_conv_a_w, v_conv_a_w), ssm_conv_w=(ssm_conv_w, m_ssm_conv_w, v_ssm_conv_w),
                 ssm_conv_b=(ssm_conv_b, m_ssm_conv_b, v_ssm_conv_b), dt_bias=(dt_bias, m_dt_bias, v_dt_bias),
                 a_log=(a_log, m_a_log, v_a_log), d_skip=(d_skip, m_d_skip, v_d_skip),
                 conv_out_norm=(conv_out_norm, m_conv_out_norm, v_conv_out_norm),
                 ssm_out_norm=(ssm_out_norm, m_ssm_out_norm, v_ssm_out_norm), w_out=(w_out, m_w_out, v_w_out),
                 norm_mix_post=(norm_mix_post, m_norm_mix_post, v_norm_mix_post),
                 norm_mlp_pre=(norm_mlp_pre, m_norm_mlp_pre, v_norm_mlp_pre), w_up=(w_up, m_w_up, v_w_up),
                 w_down=(w_down, m_w_down, v_w_down), norm_mlp_post=(norm_mlp_post, m_norm_mlp_post, v_norm_mlp_post))
    grad = dict(g_small, w_in=g_win, w_out=g_wo, w_up=g_wu, w_down=g_wd)
    order = ["norm_mix_pre", "w_in", "conv_a_w", "ssm_conv_w", "ssm_conv_b", "dt_bias", "a_log", "d_skip",
             "conv_out_norm", "ssm_out_norm", "w_out", "norm_mix_post", "norm_mlp_pre", "w_up", "w_down",
             "norm_mlp_post"]
    g_out, d_out, m_out, v_out = [], [], [], []
    for n in order:
        wv, mv, vv = given[n]
        gv = grad[n].reshape(wv.shape)
        two_d = lambda a: a.reshape(-1, a.shape[-1])
        dlt, m2, v2 = adamw(two_d(wv), two_d(gv), two_d(mv), two_d(vv), "adamw")
        g_out.append(gv)
        d_out.append(dlt.reshape(wv.shape))
        m_out.append(m2.reshape(wv.shape))
        v_out.append(v2.reshape(wv.shape))
    return (loss, dx.reshape(nb, seq, D), *g_out, *d_out, *m_out, *v_out)
```

```python
import functools

import jax
import jax.numpy as jnp
from jax import lax
from jax.experimental import pallas as pl
from jax.experimental.pallas import tpu as pltpu

f32, bf16 = jnp.float32, jnp.bfloat16

D = 1024
NH, HP = 16, 64
NG, NS = 2, 128
CH = 128
XBC = D + 2 * NG * NS
DFF = 4 * D
IN_COLS = 3 * D + D + XBC + NH
PROJ = 5760
COL_Z, COL_XBC, COL_DT = 3 * D, 4 * D, 4 * D + XBC
EPS = 1e-6
HALO = 8
VMEM_LIMIT = 56 * 2**20
MESH = pl.DeviceIdType.MESH

LR, B1, B2, AEPS, WD, STEP = 0.001, 0.9, 0.999, 1e-08, 0.01, 10


def _cparams(n_axes):
    return pltpu.CompilerParams(dimension_semantics=("arbitrary",) * n_axes, vmem_limit_bytes=VMEM_LIMIT)


def _sds(shape, dtype):
    return jax.ShapeDtypeStruct(tuple(shape), dtype)


def _rms_fwd(x, g):
    r = lax.rsqrt(jnp.mean(x * x, axis=-1, keepdims=True) + EPS)
    return x * r * g


def _rms_bwd(x, g, dy):
    r = lax.rsqrt(jnp.mean(x * x, axis=-1, keepdims=True) + EPS)
    xh = x * r
    gdy = dy * g
    dx = r * (gdy - xh * jnp.mean(xh * gdy, axis=-1, keepdims=True))
    return dx, dy * xh


def _accum(ref, part, first):
    @pl.when(first)
    def _():
        ref[...] = part

    @pl.when(jnp.logical_not(first))
    def _():
        ref[...] += part


def _dot_nt(a, b):
    return lax.dot_general(a, b, (((1,), (1,)), ((), ())), preferred_element_type=f32)


def _dot_tn(a, b):
    return lax.dot_general(a, b, (((0,), (0,)), ((), ())), preferred_element_type=f32)


def _dot(a, b):
    return jnp.dot(a, b, preferred_element_type=f32)


def _split_dot(x, e_bf, n_split, nt=False):
    acc = None
    rem = x
    for s in range(n_split):
        hi = rem.astype(bf16)
        term = _dot_nt(hi, e_bf) if nt else _dot(hi, e_bf)
        acc = term if acc is None else acc + term
        if s + 1 < n_split:
            rem = rem - hi.astype(f32)
    return acc


def _sigmoid(x):
    return 1.0 / (1.0 + jnp.exp(-x))


def norm_matmul(x, g, w, tm, tn, out_dtype, name):
    t = x.shape[0]
    if w.ndim == 3:
        assert w.shape[2] == tn
        n = w.shape[0] * tn
        w_spec = pl.BlockSpec((None, D, tn), lambda i, j: (j, 0, 0))
    else:
        n = w.shape[1]
        w_spec = pl.BlockSpec((D, tn), lambda i, j: (0, j))

    def body(x_ref, g_ref, w_ref, o_ref, h_ref):
        @pl.when(pl.program_id(1) == 0)
        def _():
            h_ref[...] = _rms_fwd(x_ref[...], g_ref[...]).astype(bf16)

        o_ref[...] = _dot(h_ref[...], w_ref[...]).astype(out_dtype)

    return pl.pallas_call(
        body, name=name, grid=(t // tm, n // tn),
        in_specs=[pl.BlockSpec((tm, D), lambda i, j: (i, 0)), pl.BlockSpec((1, D), lambda i, j: (0, 0)), w_spec],
        out_specs=[pl.BlockSpec((tm, tn), lambda i, j: (i, j)), pl.BlockSpec((tm, D), lambda i, j: (i, 0))],
        out_shape=[_sds((t, n), out_dtype), _sds((t, D), bf16)],
        compiler_params=_cparams(2))(x, g, w)


def matmul_postnorm(a, w, xres, g, tm, relu2, name):
    t, k = a.shape

    def body(a_ref, w_ref, xr_ref, g_ref, y_ref, xo_ref):
        av = a_ref[...]
        if relu2:
            af = jnp.maximum(av.astype(f32), 0.0)
            av = (af * af).astype(bf16)
        y = _dot(av, w_ref[...])
        y_ref[...] = y
        xo_ref[...] = xr_ref[...] + _rms_fwd(y, g_ref[...])

    return pl.pallas_call(
        body, name=name, grid=(t // tm,),
        in_specs=[pl.BlockSpec((tm, k), lambda i: (i, 0)), pl.BlockSpec((k, D), lambda i: (0, 0)),
                  pl.BlockSpec((tm, D), lambda i: (i, 0)), pl.BlockSpec((1, D), lambda i: (0, 0))],
        out_specs=[pl.BlockSpec((tm, D), lambda i: (i, 0)), pl.BlockSpec((tm, D), lambda i: (i, 0))],
        out_shape=[_sds((t, D), f32), _sds((t, D), f32)],
        compiler_params=_cparams(1))(a, w, xres, g)


def postnorm_bwd_matmul(y, g, dxo, w, fp, tm, tn, out_dtype, name):
    t, n = y.shape[0], w.shape[0]
    relu = fp is not None

    def body(*refs):
        if relu:
            y_ref, g_ref, dxo_ref, w_ref, fp_ref, dy_ref, dg_ref, da_ref = refs
        else:
            y_ref, g_ref, dxo_ref, w_ref, dy_ref, dg_ref, da_ref = refs
        i, j = pl.program_id(0), pl.program_id(1)

        @pl.when(j == 0)
        def _():
            dx, dgc = _rms_bwd(y_ref[...], g_ref[...], dxo_ref[...])
            dy_ref[...] = dx.astype(bf16)
            _accum(dg_ref, jnp.sum(dgc, axis=0, keepdims=True), i == 0)

        da = _dot_nt(dy_ref[...], w_ref[...])
        if relu:
            da = da * (2.0 * jnp.maximum(fp_ref[...].astype(f32), 0.0))
        da_ref[...] = da.astype(out_dtype)

    in_specs = [pl.BlockSpec((tm, D), lambda i, j: (i, 0)), pl.BlockSpec((1, D), lambda i, j: (0, 0)),
                pl.BlockSpec((tm, D), lambda i, j: (i, 0)), pl.BlockSpec((tn, D), lambda i, j: (j, 0))]
    args = [y, g, dxo, w]
    if relu:
        in_specs.append(pl.BlockSpec((tm, tn), lambda i, j: (i, j)))
        args.append(fp)
    return pl.pallas_call(
        body, name=name, grid=(t // tm, n // tn), in_specs=in_specs,
        out_specs=[pl.BlockSpec((tm, D), lambda i, j: (i, 0)), pl.BlockSpec((1, D), lambda i, j: (0, 0)),
                   pl.BlockSpec((tm, tn), lambda i, j: (i, j))],
        out_shape=[_sds((t, D), bf16), _sds((1, D), f32), _sds((t, n), out_dtype)],
        compiler_params=_cparams(2))(*args)


def matmul_prenorm_bwd(da, w, x, g, dxo, tm, name):
    t, k = da.shape
    blocked = w.ndim == 3

    def body(da_ref, w_ref, x_ref, g_ref, dxo_ref, dx_ref, dg_ref):
        if blocked:
            kc = w.shape[2]
            dh = _dot_nt(da_ref[:, 0:kc], w_ref[0])
            for q in range(1, w.shape[0]):
                dh = dh + _dot_nt(da_ref[:, q * kc:(q + 1) * kc], w_ref[q])
        else:
            dh = _dot_nt(da_ref[...], w_ref[...])
        dxn, dgc = _rms_bwd(x_ref[...], g_ref[...], dh)
        dx_ref[...] = dxo_ref[...] + dxn
        _accum(dg_ref, jnp.sum(dgc, axis=0, keepdims=True), pl.program_id(0) == 0)

    w_spec = pl.BlockSpec(w.shape, (lambda i: (0, 0, 0)) if blocked else (lambda i: (0, 0)))
    return pl.pallas_call(
        body, name=name, grid=(t // tm,),
        in_specs=[pl.BlockSpec((tm, k), lambda i: (i, 0)), w_spec,
                  pl.BlockSpec((tm, D), lambda i: (i, 0)), pl.BlockSpec((1, D), lambda i: (0, 0)),
                  pl.BlockSpec((tm, D), lambda i: (i, 0))],
        out_specs=[pl.BlockSpec((tm, D), lambda i: (i, 0)), pl.BlockSpec((1, D), lambda i: (0, 0))],
        out_shape=[_sds((t, D), f32), _sds((1, D), f32)],
        compiler_params=_cparams(1))(da, w, x, g, dxo)


def matmul_tn(a, b, tm, tn, relu2, name, col_blocks=False):
    t, m = a.shape
    n = b.shape[1]
    if col_blocks:
        out_spec, out_shape = pl.BlockSpec((None, tm, tn), lambda i, j: (j, i, 0)), _sds((n // tn, m, tn), f32)
    else:
        out_spec, out_shape = pl.BlockSpec((tm, tn), lambda i, j: (i, j)), _sds((m, n), f32)

    def body(a_ref, b_ref, o_ref, at_ref):
        @pl.when(pl.program_id(1) == 0)
        def _():
            av = a_ref[...]
            if relu2:
                af = jnp.maximum(av.astype(f32), 0.0)
                av = (af * af).astype(bf16)
            at_ref[...] = av.T

        o_ref[...] = _dot(at_ref[...], b_ref[...])

    return pl.pallas_call(
        body, name=name, grid=(m // tm, n // tn),
        in_specs=[pl.BlockSpec((t, tm), lambda i, j: (0, i)), pl.BlockSpec((t, tn), lambda i, j: (0, j))],
        out_specs=out_spec, out_shape=out_shape,
        scratch_shapes=[pltpu.VMEM((tm, t), bf16)],
        compiler_params=_cparams(2))(a, b)


def _halo_prev(tb, col):
    return lambda i: (jnp.maximum(i * (tb // HALO) - 1, 0), col)


def _halo_next(tb, col, t):
    return lambda i: (jnp.minimum((i + 1) * (tb // HALO), t // HALO - 1), col)


def group_a_fwd(proj, wa, g, seq, tb, name):
    t = proj.shape[0]
    bps = seq // tb

    def body(xa_ref, ca_ref, ba_ref, xah_ref, cah_ref, wa_ref, g_ref, o_ref, u_scr):
        first = (pl.program_id(0) % bps) == 0
        u = ca_ref[...] * xa_ref[...]
        u_scr[0:HALO, :] = jnp.where(first, 0.0, cah_ref[...] * xah_ref[...])
        u_scr[HALO:HALO + tb, :] = u
        w = wa_ref[...]
        cv = w[2:3] * u + w[1:2] * u_scr[pl.ds(HALO - 1, tb), :] + w[0:1] * u_scr[pl.ds(HALO - 2, tb), :]
        o_ref[...] = _rms_fwd(ba_ref[...] * cv, g_ref[...]).astype(bf16)

    blk = lambda c: pl.BlockSpec((tb, D), lambda i: (i, c))
    return pl.pallas_call(
        body, name=name, grid=(t // tb,),
        in_specs=[blk(0), blk(1), blk(2),
                  pl.BlockSpec((HALO, D), _halo_prev(tb, 0)), pl.BlockSpec((HALO, D), _halo_prev(tb, 1)),
                  pl.BlockSpec((8, D), lambda i: (0, 0)), pl.BlockSpec((1, D), lambda i: (0, 0))],
        out_specs=pl.BlockSpec((tb, D), lambda i: (i, 0)),
        out_shape=_sds((t, D), bf16),
        scratch_shapes=[pltpu.VMEM((tb + HALO, D), f32)],
        compiler_params=_cparams(1))(proj, proj, proj, proj, proj, wa, g)


def group_a_bwd(proj, dcat, wa, g, seq, tb, name):
    t = proj.shape[0]
    bps = seq // tb

    def body(xa_ref, ca_ref, ba_ref, dy_ref, xap_ref, cap_ref, xan_ref, can_ref, ban_ref, dyn_ref, wa_ref, g_ref,
             dp_ref, dwa_ref, dg_ref, u_scr, d_scr):
        i = pl.program_id(0)
        first = (i % bps) == 0
        last = (i % bps) == bps - 1
        w = wa_ref[...]
        gv = g_ref[...]
        xa, ca, ba = xa_ref[...], ca_ref[...], ba_ref[...]
        u_scr[0:HALO, :] = jnp.where(first, 0.0, cap_ref[...] * xap_ref[...])
        u_scr[HALO:HALO + tb, :] = ca * xa
        u_scr[HALO + tb:2 * HALO + tb, :] = can_ref[...] * xan_ref[...]

        def conv(start, rows):
            return (w[2:3] * u_scr[pl.ds(start, rows), :] + w[1:2] * u_scr[pl.ds(start - 1, rows), :]
                    + w[0:1] * u_scr[pl.ds(start - 2, rows), :])

        cv = conv(HALO, tb)
        dya, dgc = _rms_bwd(ba * cv, gv, dy_ref[...])
        ban = ban_ref[...]
        dyan, _ = _rms_bwd(ban * conv(HALO + tb, HALO), gv, dyn_ref[...])
        dcv = dya * ba
        d_scr[0:tb, :] = dcv
        d_scr[tb:tb + HALO, :] = jnp.where(last, 0.0, dyan * ban)
        du = w[2:3] * dcv + w[1:2] * d_scr[pl.ds(1, tb), :] + w[0:1] * d_scr[pl.ds(2, tb), :]
        dp_ref[:, 0:D] = (du * ca).astype(bf16)
        dp_ref[:, D:2 * D] = (du * xa).astype(bf16)
        dp_ref[:, 2 * D:3 * D] = (dya * cv).astype(bf16)
        row = lax.broadcasted_iota(jnp.int32, (8, D), 0)
        dw = jnp.zeros((8, D), f32)
        for k in range(3):
            s = jnp.sum(dcv * u_scr[pl.ds(HALO - 2 + k, tb), :], axis=0, keepdims=True)
            dw = jnp.where(row == k, s, dw)
        _accum(dwa_ref, dw, i == 0)
        _accum(dg_ref, jnp.sum(dgc, axis=0, keepdims=True), i == 0)

    blk = lambda c: pl.BlockSpec((tb, D), lambda i: (i, c))
    prv = lambda c: pl.BlockSpec((HALO, D), _halo_prev(tb, c))
    nxt = lambda c: pl.BlockSpec((HALO, D), _halo_next(tb, c, t))
    return pl.pallas_call(
        body, name=name, grid=(t // tb,),
        in_specs=[blk(0), blk(1), blk(2), blk(0), prv(0), prv(1), nxt(0), nxt(1), nxt(2), nxt(0),
                  pl.BlockSpec((8, D), lambda i: (0, 0)), pl.BlockSpec((1, D), lambda i: (0, 0))],
        out_specs=[pl.BlockSpec((tb, 3 * D), lambda i: (i, 0)), pl.BlockSpec((8, D), lambda i: (0, 0)),
                   pl.BlockSpec((1, D), lambda i: (0, 0))],
        out_shape=[_sds((t, 3 * D), bf16), _sds((8, D), f32), _sds((1, D), f32)],
        scratch_shapes=[pltpu.VMEM((tb + 2 * HALO, D), f32), pltpu.VMEM((tb + HALO, D), f32)],
        compiler_params=_cparams(1))(proj, proj, proj, dcat, proj, proj, proj, proj, proj, dcat, wa, g)


CB = 512
XBC_BLK0 = COL_XBC // CB


def conv_b_fwd(proj, ws, bs, seq, tb, name):
    t = proj.shape[0]
    bps = seq // tb

    def body(x_ref, xp_ref, w_ref, b_ref, o_ref, x_scr):
        first = (pl.program_id(1) % bps) == 0
        x = x_ref[...]
        x_scr[0:HALO, :] = jnp.where(first, 0.0, xp_ref[...])
        x_scr[HALO:HALO + tb, :] = x
        w = w_ref[...]
        xc = w[3:4] * x + b_ref[...]
        for k in range(3):
            xc = xc + w[k:k + 1] * x_scr[pl.ds(HALO - 3 + k, tb), :]
        o_ref[...] = xc * _sigmoid(xc)

    return pl.pallas_call(
        body, name=name, grid=(XBC // CB, t // tb),
        in_specs=[pl.BlockSpec((tb, CB), lambda j, i: (i, XBC_BLK0 + j)),
                  pl.BlockSpec((HALO, CB), lambda j, i: (jnp.maximum(i * (tb // HALO) - 1, 0), XBC_BLK0 + j)),
                  pl.BlockSpec((8, CB), lambda j, i: (0, j)), pl.BlockSpec((1, CB), lambda j, i: (0, j))],
        out_specs=pl.BlockSpec((tb, CB), lambda j, i: (i, j)),
        out_shape=_sds((t, XBC), f32),
        scratch_shapes=[pltpu.VMEM((tb + HALO, CB), f32)],
        compiler_params=_cparams(2))(proj, proj, ws, bs)


def conv_b_bwd(proj, dxs, ws, bs, seq, tb, name):
    t = proj.shape[0]
    bps = seq // tb

    def body(x_ref, xp_ref, xn_ref, d_ref, dn_ref, w_ref, b_ref, dx_ref, dw_ref, db_ref, x_scr, d_scr):
        i = pl.program_id(1)
        first = (i % bps) == 0
        last = (i % bps) == bps - 1
        w = w_ref[...]
        bias = b_ref[...]
        x_scr[0:HALO, :] = jnp.where(first, 0.0, xp_ref[...])
        x_scr[HALO:HALO + tb, :] = x_ref[...]
        x_scr[HALO + tb:2 * HALO + tb, :] = xn_ref[...]

        def dsilu_at(start, rows, d):
            xc = bias + w[3:4] * x_scr[pl.ds(start, rows), :]
            for k in range(3):
                xc = xc + w[k:k + 1] * x_scr[pl.ds(start - 3 + k, rows), :]
            sg = _sigmoid(xc)
            return d * (sg * (1.0 + xc * (1.0 - sg)))

        dxc = dsilu_at(HALO, tb, d_ref[...])
        d_scr[0:tb, :] = dxc
        d_scr[tb:tb + HALO, :] = jnp.where(last, 0.0, dsilu_at(HALO + tb, HALO, dn_ref[...]))
        dx = w[3:4] * dxc
        for k in range(3):
            dx = dx + w[k:k + 1] * d_scr[pl.ds(3 - k, tb), :]
        dx_ref[...] = dx.astype(bf16)
        row = lax.broadcasted_iota(jnp.int32, (8, CB), 0)
        dw = jnp.zeros((8, CB), f32)
        for k in range(4):
            s = jnp.sum(dxc * x_scr[pl.ds(HALO - 3 + k, tb), :], axis=0, keepdims=True)
            dw = jnp.where(row == k, s, dw)
        _accum(dw_ref, dw, i == 0)
        _accum(db_ref, jnp.sum(dxc, axis=0, keepdims=True), i == 0)

    nh = t // HALO
    return pl.pallas_call(
        body, name=name, grid=(XBC // CB, t // tb),
        in_specs=[pl.BlockSpec((tb, CB), lambda j, i: (i, XBC_BLK0 + j)),
                  pl.BlockSpec((HALO, CB), lambda j, i: (jnp.maximum(i * (tb // HALO) - 1, 0), XBC_BLK0 + j)),
                  pl.BlockSpec((HALO, CB), lambda j, i: (jnp.minimum((i + 1) * (tb // HALO), nh - 1), XBC_BLK0 + j)),
                  pl.BlockSpec((tb, CB), lambda j, i: (i, j)),
                  pl.BlockSpec((HALO, CB), lambda j, i: (jnp.minimum((i + 1) * (tb // HALO), nh - 1), j)),
                  pl.BlockSpec((8, CB), lambda j, i: (0, j)), pl.BlockSpec((1, CB), lambda j, i: (0, j))],
        out_specs=[pl.BlockSpec((tb, CB), lambda j, i: (i, j)), pl.BlockSpec((8, CB), lambda j, i: (0, j)),
                   pl.BlockSpec((1, CB), lambda j, i: (0, j))],
        out_shape=[_sds((t, XBC), bf16), _sds((8, XBC), f32), _sds((1, XBC), f32)],
        scratch_shapes=[pltpu.VMEM((tb + 2 * HALO, CB), f32), pltpu.VMEM((tb + HALO, CB), f32)],
        compiler_params=_cparams(2))(proj, proj, proj, dxs, dxs, ws, bs)


GW = D // NG


def _ssd_consts():
    head_of_lane = jnp.arange(D) // HP
    expand = (jnp.arange(CH)[:, None] == head_of_lane[None, :]).astype(bf16)
    tri = (jnp.arange(CH)[:, None] >= jnp.arange(CH)[None, :]).astype(f32)
    return expand, tri


def _ssd_common(par_ref, dtr_ref, e_ref, tri_ref):
    par = par_ref[...]
    dtb, alog, dsk = par[0:1], par[1:2], par[2:3]
    lane = lax.broadcasted_iota(jnp.int32, (CH, CH), 1)
    a = -jnp.exp(alog)
    dtr = dtr_ref[...] + dtb
    sp = jnp.maximum(dtr, 0.0) + jnp.log(1.0 + jnp.exp(-jnp.abs(dtr)))
    dt = jnp.where(lane < NH, sp, 0.0)
    cs = jnp.dot(tri_ref[...], dt * a, precision=lax.Precision.HIGHEST, preferred_element_type=f32)
    cs_last = cs[CH - 1:CH, :]
    dte = jnp.exp(cs_last - cs)
    ecs = jnp.exp(cs)
    ecl = jnp.exp(cs_last)
    e = e_ref[...]
    row8 = lax.broadcasted_iota(jnp.int32, (8, CH), 0)
    r8 = _split_dot(jnp.where(row8 == 0, ecl, jnp.where(row8 == 1, dsk, 0.0)), e, 3)
    return dict(a=a, dtr=dtr, dt=dt, cs=cs, cst=cs.T, dte=dte, ecs=ecs, ecl=ecl, e=e, lane=lane,
                dt_x=_split_dot(dt, e, 3), dte_x=_split_dot(dte, e, 3), ecs_x=_split_dot(ecs, e, 3),
                ecl_x=r8[0:1], dsk_x=r8[1:2])


def _decay_matrix(c, h):
    li = lax.broadcasted_iota(jnp.int32, (CH, CH), 0)
    seg = c["cs"][:, h:h + 1] - c["cst"][h:h + 1, :]
    return jnp.exp(jnp.where(li >= c["lane"], seg, -jnp.inf))


def _gate_norm_fwd(y, z, gs):
    zg = z * _sigmoid(z)
    yg = y * zg
    return jnp.concatenate([_rms_fwd(yg[:, k * GW:(k + 1) * GW], gs[:, k * GW:(k + 1) * GW]) for k in range(NG)], axis=1)


def ssd_fwd(xbcs, proj, par, gs, seq, name):
    t = xbcs.shape[0]
    nc = seq // CH
    expand, tri = _ssd_consts()

    def body(xs_ref, b_ref, c_ref, dtr_ref, z_ref, par_ref, e_ref, tri_ref, gs_ref, yn_ref, y_ref, st_ref, p_scr, yd_scr):
        @pl.when(pl.program_id(0) % nc == 0)
        def _():
            p_scr[...] = jnp.zeros_like(p_scr)

        c = _ssd_common(par_ref, dtr_ref, e_ref, tri_ref)
        xs = xs_ref[...]
        xdt = xs * c["dt_x"]
        xdt_b = xdt.astype(bf16)
        xdte_b = (xdt * c["dte_x"]).astype(bf16)
        p = p_scr[...]
        st_ref[0] = p
        p_b = p.astype(bf16)
        lo = c["lane"] < HP
        for g in range(NG):
            bg = b_ref[:, g * NS:(g + 1) * NS].astype(bf16)
            cg = c_ref[:, g * NS:(g + 1) * NS].astype(bf16)
            gmat = _dot_nt(cg, bg)
            for q in range(GW // CH):
                col = g * GW + q * CH
                xp = xdt_b[:, col:col + CH]
                h0 = col // HP
                m0 = (gmat * _decay_matrix(c, h0)).astype(bf16)
                m1 = (gmat * _decay_matrix(c, h0 + 1)).astype(bf16)
                yd_scr[:, col:col + CH] = (_dot(m0, jnp.where(lo, xp, jnp.zeros_like(xp)))
                                           + _dot(m1, jnp.where(lo, jnp.zeros_like(xp), xp)))
            gsl = slice(g * GW, (g + 1) * GW)
            yoff = _dot(cg, p_b[:, gsl]) * c["ecs_x"][:, gsl]
            yd_scr[:, gsl] = yd_scr[:, gsl] + yoff
            p_scr[:, gsl] = p[:, gsl] * c["ecl_x"][:, gsl] + _dot_tn(bg, xdte_b[:, gsl])
        y = yd_scr[...] + c["dsk_x"] * xs
        y_ref[...] = y
        yn_ref[...] = _gate_norm_fwd(y, z_ref[...], gs_ref[...]).astype(bf16)

    nb = t // CH
    return pl.pallas_call(
        body, name=name, grid=(nb,),
        in_specs=[pl.BlockSpec((CH, D), lambda i: (i, 0)),
                  pl.BlockSpec((CH, NG * NS), lambda i: (i, D // (NG * NS))),
                  pl.BlockSpec((CH, NG * NS), lambda i: (i, D // (NG * NS) + 1)),
                  pl.BlockSpec((CH, CH), lambda i: (i, COL_DT // CH)),
                  pl.BlockSpec((CH, D), lambda i: (i, COL_Z // D)),
                  pl.BlockSpec((8, CH), lambda i: (0, 0)), pl.BlockSpec((CH, D), lambda i: (0, 0)),
                  pl.BlockSpec((CH, CH), lambda i: (0, 0)), pl.BlockSpec((1, D), lambda i: (0, 0))],
        out_specs=[pl.BlockSpec((CH, D), lambda i: (i, 0)), pl.BlockSpec((CH, D), lambda i: (i, 0)),
                   pl.BlockSpec((1, NS, D), lambda i: (i, 0, 0))],
        out_shape=[_sds((t, D), bf16), _sds((t, D), f32), _sds((nb, NS, D), f32)],
        scratch_shapes=[pltpu.VMEM((NS, D), f32), pltpu.VMEM((CH, D), f32)],
        compiler_params=_cparams(1))(xbcs, xbcs, xbcs, proj, proj, par, expand, tri, gs)


def ssd_bwd(xbcs, proj, ypre, states, dcat, par, gs, seq, name):
    t = xbcs.shape[0]
    nc = seq // CH
    expand, tri = _ssd_consts()

    def body(xs_ref, b_ref, c_ref, dtr_ref, z_ref, y_ref, st_ref, dyn_ref, par_ref, e_ref, tri_ref, gs_ref,
             dx_ref, dz_ref, ddt_ref, dpar_ref, dgs_ref, dp_scr, dxdt_scr):
        i = pl.program_id(0)

        @pl.when(i % nc == 0)
        def _():
            dp_scr[...] = jnp.zeros_like(dp_scr)

        c = _ssd_common(par_ref, dtr_ref, e_ref, tri_ref)
        e = c["e"]
        lane = c["lane"]
        sub = lax.broadcasted_iota(jnp.int32, (CH, CH), 0)
        xs = xs_ref[...]
        xdt = xs * c["dt_x"]
        xdt_b = xdt.astype(bf16)
        xdte_b = (xdt * c["dte_x"]).astype(bf16)
        p = st_ref[0]
        p_b = p.astype(bf16)
        dpn = dp_scr[...]
        dpn_b = dpn.astype(bf16)

        y, z, gs_v = y_ref[...], z_ref[...], gs_ref[...]
        zs = _sigmoid(z)
        zg = z * zs
        yg = y * zg
        parts, gparts = [], []
        for k in range(NG):
            sl = slice(k * GW, (k + 1) * GW)
            dxk, dgk = _rms_bwd(yg[:, sl], gs_v[:, sl], dyn_ref[:, sl])
            parts.append(dxk)
            gparts.append(dgk)
        dyg = jnp.concatenate(parts, axis=1)
        dgs_rows = jnp.concatenate(gparts, axis=1)
        dy = dyg * zg
        dz_ref[...] = (dyg * y * (zs * (1.0 + z * (1.0 - zs)))).astype(bf16)
        dy_b = dy.astype(bf16)
        dq_b = (dy * c["ecs_x"]).astype(bf16)

        lo = lane < HP
        dcs = jnp.zeros((CH, CH), f32)
        dcst = jnp.zeros((CH, CH), f32)
        for g in range(NG):
            gsl = slice(g * GW, (g + 1) * GW)
            bg = b_ref[:, g * NS:(g + 1) * NS].astype(bf16)
            cg = c_ref[:, g * NS:(g + 1) * NS].astype(bf16)
            gmat = _dot_nt(cg, bg)
            dgm = jnp.zeros((CH, CH), f32)
            for q in range(GW // CH):
                col = g * GW + q * CH
                xp = xdt_b[:, col:col + CH]
                dyp = dy_b[:, col:col + CH]
                acc = None
                for hh in range(2):
                    h = col // HP + hh
                    keep = lo if hh == 0 else jnp.logical_not(lo)
                    dyh = jnp.where(keep, dyp, jnp.zeros_like(dyp))
                    dec = _decay_matrix(c, h)
                    m = gmat * dec
                    dm = _dot_nt(dyh, xp)
                    dseg = dm * m
                    dcs = dcs + jnp.where(lane == h, jnp.sum(dseg, axis=1, keepdims=True), 0.0)
                    dcst = dcst + jnp.where(sub == h, jnp.sum(dseg, axis=0, keepdims=True), 0.0)
                    dgm = dgm + dm * dec
                    term = _dot_tn(m.astype(bf16), dyh)
                    acc = term if acc is None else acc + term
                dxdt_scr[:, col:col + CH] = acc
            dgm_b = dgm.astype(bf16)
            bds = _dot(bg, dpn_b[:, gsl])
            dxdt_scr[:, gsl] = dxdt_scr[:, gsl] + c["dte_x"][:, gsl] * bds
            dc_g = _dot(dgm_b, bg) + _dot_nt(dq_b[:, gsl], p_b[:, gsl])
            db_g = _dot_tn(dgm_b, cg) + _dot_nt(xdte_b[:, gsl], dpn_b[:, gsl])
            dx_ref[:, D + g * NS:D + (g + 1) * NS] = db_g
            dx_ref[:, D + NG * NS + g * NS:D + NG * NS + (g + 1) * NS] = dc_g
            dp_scr[:, gsl] = dpn[:, gsl] * c["ecl_x"][:, gsl] + _dot_tn(cg, dq_b[:, gsl])
            q_g = _dot(cg, p_b[:, gsl])
            e_g = e[:, gsl]
            dcs = dcs + c["ecs"] * _split_dot(dy[:, gsl] * q_g, e_g, 2, nt=True)
            ddte = _split_dot(xdt[:, gsl] * bds, e_g, 2, nt=True) * c["dte"]
            dcs = dcs - ddte
            dcs = dcs + jnp.where(sub == CH - 1, jnp.sum(ddte, axis=0, keepdims=True), 0.0)

        decl = _split_dot(jnp.broadcast_to(jnp.sum(dpn * p, axis=0, keepdims=True), (8, D)), e, 2, nt=True)[0:1]
        dcs = dcs + jnp.where(sub == CH - 1, c["ecl"] * decl, 0.0)
        dcs = dcs - dcst.T
        dadt = lax.dot_general(tri_ref[...], dcs, (((0,), (0,)), ((), ())), precision=lax.Precision.HIGHEST,
                               preferred_element_type=f32)
        dxdt = dxdt_scr[...]
        ddt = dadt * c["a"] + _split_dot(dxdt * xs, e, 2, nt=True)
        ddtr = jnp.where(lane < NH, ddt * _sigmoid(c["dtr"]), 0.0)
        ddt_ref[...] = ddtr.astype(bf16)
        dx_ref[:, 0:D] = dxdt * c["dt_x"] + c["dsk_x"] * dy
        dsk = _split_dot(jnp.broadcast_to(jnp.sum(dy * xs, axis=0, keepdims=True), (8, D)), e, 2, nt=True)[0:1]
        dalog = jnp.sum(dadt * c["dt"], axis=0, keepdims=True) * c["a"]
        row8 = lax.broadcasted_iota(jnp.int32, (8, CH), 0)
        dpar = jnp.where(row8 == 0, jnp.sum(ddtr, axis=0, keepdims=True),
                         jnp.where(row8 == 1, dalog, jnp.where(row8 == 2, dsk, 0.0)))
        dpar = jnp.where(lax.broadcasted_iota(jnp.int32, (8, CH), 1) < NH, dpar, 0.0)
        _accum(dpar_ref, dpar, i == 0)
        _accum(dgs_ref, jnp.sum(dgs_rows, axis=0, keepdims=True), i == 0)

    nb = t // CH
    rev = lambda i: (i // nc) * nc + (nc - 1 - i % nc)
    return pl.pallas_call(
        body, name=name, grid=(nb,),
        in_specs=[pl.BlockSpec((CH, D), lambda i: (rev(i), 0)),
                  pl.BlockSpec((CH, NG * NS), lambda i: (rev(i), D // (NG * NS))),
                  pl.BlockSpec((CH, NG * NS), lambda i: (rev(i), D // (NG * NS) + 1)),
                  pl.BlockSpec((CH, CH), lambda i: (rev(i), COL_DT // CH)),
                  pl.BlockSpec((CH, D), lambda i: (rev(i), COL_Z // D)),
                  pl.BlockSpec((CH, D), lambda i: (rev(i), 0)),
                  pl.BlockSpec((1, NS, D), lambda i: (rev(i), 0, 0)),
                  pl.BlockSpec((CH, D), lambda i: (rev(i), 1)),
                  pl.BlockSpec((8, CH), lambda i: (0, 0)), pl.BlockSpec((CH, D), lambda i: (0, 0)),
                  pl.BlockSpec((CH, CH), lambda i: (0, 0)), pl.BlockSpec((1, D), lambda i: (0, 0))],
        out_specs=[pl.BlockSpec((CH, XBC), lambda i: (rev(i), 0)), pl.BlockSpec((CH, D), lambda i: (rev(i), 0)),
                   pl.BlockSpec((CH, CH), lambda i: (rev(i), 0)),
                   pl.BlockSpec((8, CH), lambda i: (0, 0)), pl.BlockSpec((1, D), lambda i: (0, 0))],
        out_shape=[_sds((t, XBC), f32), _sds((t, D), bf16), _sds((t, CH), bf16), _sds((8, CH), f32), _sds((1, D), f32)],
        scratch_shapes=[pltpu.VMEM((NS, D), f32), pltpu.VMEM((CH, D), f32)],
        compiler_params=_cparams(1))(xbcs, xbcs, xbcs, proj, proj, ypre, states, dcat, par, expand, tri, gs)


def loss_head(y, target, tb, name):
    t = y.shape[0]

    def body(y_ref, t_ref, s_ref, dy_ref):
        err = y_ref[...] - t_ref[...]
        dy_ref[...] = err * (1.0 / D)
        _accum(s_ref, jnp.zeros((8, CH), f32) + jnp.sum(err * err), pl.program_id(0) == 0)

    return pl.pallas_call(
        body, name=name, grid=(t // tb,),
        in_specs=[pl.BlockSpec((tb, D), lambda i: (i, 0)), pl.BlockSpec((tb, D), lambda i: (i, 0))],
        out_specs=[pl.BlockSpec((8, CH), lambda i: (0, 0)), pl.BlockSpec((tb, D), lambda i: (i, 0))],
        out_shape=[_sds((8, CH), f32), _sds((t, D), f32)],
        compiler_params=_cparams(1))(y, target)


def _tiles(t, seq):
    tm = min(512, t)
    return dict(tm=tm, tm_small=min(256, t), tb=min(512, seq), tb_conv=min(512, seq))


def local_step(x, target, w, seq):
    t = x.shape[0]
    ts = _tiles(t, seq)
    tm, tb = ts["tm"], ts["tb"]
    depth = len(w["win"])
    saved = []
    for l in range(depth):
        proj, h1 = norm_matmul(x, w["g1"][l], w["win"][l], tm, 1152, f32, "in_proj")
        ya = group_a_fwd(proj, w["wa"][l], w["ga"][l], seq, tb, "group_a_fwd")
        xbcs = conv_b_fwd(proj, w["ws"][l], w["bs"][l], seq, tb, "conv_b_fwd")
        ys, ypre, states = ssd_fwd(xbcs, proj, w["par"][l], w["gs"][l], seq, "ssd_fwd")
        cat = jnp.concatenate([ya, ys], axis=1)
        mix, x2 = matmul_postnorm(cat, w["wo"][l], x, w["g2"][l], tm, False, "out_proj")
        fp, h2 = norm_matmul(x2, w["g3"][l], w["wu"][l], tm, 1024, bf16, "mlp_up")
        o, x3 = matmul_postnorm(fp, w["wd"][l], x2, w["g4"][l], tm, True, "mlp_down")
        saved.append(dict(x=x, proj=proj, h1=h1, xbcs=xbcs, ypre=ypre, states=states, cat=cat, mix=mix, x2=x2,
                          fp=fp, h2=h2, o=o))
        x = x3
    sse, dx = loss_head(x, target, tm, "loss_head")
    grads = [None] * depth
    for l in reversed(range(depth)):
        s = saved[l]
        do, dg4, dfp = postnorm_bwd_matmul(s["o"], w["g4"][l], dx, w["wd"][l], s["fp"], tm, 1024, bf16, "mlp_down_bwd")
        dwd = matmul_tn(s["fp"], do, 512, 1024, True, "mlp_down_dw")
        dx2, dg3 = matmul_prenorm_bwd(dfp, w["wu"][l], s["x2"], w["g3"][l], dx, tm, "mlp_up_bwd")
        dwu = matmul_tn(s["h2"], dfp, 512, 1024, False, "mlp_up_dw", col_blocks=True)
        dmix, dg2, dcat = postnorm_bwd_matmul(s["mix"], w["g2"][l], dx2, w["wo"][l], None, tm, 1024, f32, "out_proj_bwd")
        dwo = matmul_tn(s["cat"], dmix, 512, 1024, False, "out_proj_dw")
        dpa, dwa, dga = group_a_bwd(s["proj"], dcat, w["wa"][l], w["ga"][l], seq, tb, "group_a_bwd")
        dxbcs, dz, ddt, dpar, dgs = ssd_bwd(s["xbcs"], s["proj"], s["ypre"], s["states"], dcat, w["par"][l],
                                            w["gs"][l], seq, "ssd_bwd")
        dxbc, dws, dbs = conv_b_bwd(s["proj"], dxbcs, w["ws"][l], w["bs"][l], seq, tb, "conv_b_bwd")
        dproj = jnp.concatenate([dpa, dz, dxbc, ddt], axis=1)
        dx, dg1 = matmul_prenorm_bwd(dproj, w["win"][l], s["x"], w["g1"][l], dx2, ts["tm_small"], "in_proj_bwd")
        dwin = matmul_tn(s["h1"], dproj, 512, 1152, False, "in_proj_dw")
        grads[l] = dict(win=dwin, wo=dwo, wu=dwu, wd=dwd, wa=dwa, ws=dws, bs=dbs, par=dpar,
                        g1=dg1, ga=dga, gs=dgs, g2=dg2, g3=dg3, g4=dg4)
    return sse, dx, grads


GROUPS = {
    "chips": [(1, 0, 0), (0, 1, 0), (1, 1, 0)],
    "pair": [(0, 0, 1)],
    "all": [(1, 0, 0), (0, 1, 0), (1, 1, 0), (0, 0, 1), (1, 0, 1), (0, 1, 1), (1, 1, 1)],
}


def _group_index(group, x, y, c):
    return {"chips": 2 * x + y, "pair": c, "all": 4 * x + 2 * y + c}[group]


def _chunk_indices(shape, pieces):
    if len(shape) < 3:
        return [()]
    lead = [()]
    for n in shape[:-2]:
        lead = [i + (k,) for i in lead for k in range(n)]
    rows = shape[-2]
    split = max(1, pieces // len(lead))
    while split > 1 and (rows % split or (rows // split) % 16):
        split -= 1
    step = rows // split
    return [i + (pl.ds(s * step, step),) for i in lead for s in range(split)]


def _exchange(arrays, out_shapes, group, src_view, dst_view, view_shape, name, own, pieces=16):
    masks = GROUPS[group]
    na, nm = len(arrays), len(masks)
    cuts = [_chunk_indices(view_shape(a), pieces) for a in range(na)]

    def body(*refs):
        ins, outs = refs[:na], refs[na:2 * na]
        send_sems, recv_sems = refs[2 * na:2 * na + 2]
        local_sems = refs[2 * na + 2] if own else None
        x, y, c = lax.axis_index("x"), lax.axis_index("y"), lax.axis_index("c")
        me = _group_index(group, x, y, c)
        peers = []
        for mx, my, mc in masks:
            px, py, pc = (1 - x if mx else x), (1 - y if my else y), (1 - c if mc else c)
            peers.append(((px, py, pc), _group_index(group, px, py, pc)))

        def part(ref, idx):
            return ref.at[idx] if idx else ref

        if own:
            for a in range(na):
                for idx in cuts[a]:
                    pltpu.make_async_copy(part(src_view(ins[a], a, me), idx), part(dst_view(outs[a], a, me), idx),
                                          local_sems.at[a]).start()
        for a in range(na):
            for j, (dev, pidx) in enumerate(peers):
                for idx in cuts[a]:
                    pltpu.make_async_remote_copy(
                        src_ref=part(src_view(ins[a], a, pidx), idx), dst_ref=part(dst_view(outs[a], a, me), idx),
                        send_sem=send_sems.at[a * nm + j], recv_sem=recv_sems.at[a * nm + j],
                        device_id=dev, device_id_type=MESH).start()
        whole = []
        for a in range(na):
            for j, (dev, pidx) in enumerate(peers):
                whole.append(pltpu.make_async_remote_copy(
                    src_ref=src_view(ins[a], a, pidx), dst_ref=dst_view(outs[a], a, pidx),
                    send_sem=send_sems.at[a * nm + j], recv_sem=recv_sems.at[a * nm + j],
                    device_id=dev, device_id_type=MESH))
        for cp in whole:
            cp.wait_recv()
        for cp in whole:
            cp.wait_send()
        if own:
            for a in range(na):
                pltpu.make_async_copy(src_view(ins[a], a, me), dst_view(outs[a], a, me), local_sems.at[a]).wait()

    hbm = pl.BlockSpec(memory_space=pltpu.HBM)
    sems = [pltpu.SemaphoreType.DMA((na * nm,)), pltpu.SemaphoreType.DMA((na * nm,))]
    return pl.pallas_call(
        body, name=name, in_specs=[hbm] * na, out_specs=[hbm] * na,
        out_shape=[_sds(s, a.dtype) for s, a in zip(out_shapes, arrays)],
        scratch_shapes=sems + ([pltpu.SemaphoreType.DMA((na,))] if own else []))(*arrays)


def all_gather(arrays, group, name, slot_axis=0, own=True):
    n = len(GROUPS[group]) + 1
    shapes = [a.shape[:slot_axis] + (n,) + a.shape[slot_axis:] for a in arrays]
    lead = (slice(None),) * slot_axis
    return _exchange(arrays, shapes, group, lambda r, a, i: r, lambda r, a, i: r.at[lead + (i,)],
                     lambda a: arrays[a].shape, name, own)


def all_to_all(arrays, group, name):
    return _exchange(arrays, [a.shape for a in arrays], group, lambda r, a, i: r.at[i], lambda r, a, i: r.at[i],
                     lambda a: arrays[a].shape[1:], name, False)


def sum_exchanged(x, y, me, out_dtype, name, tb=256):
    n, r, c = x.shape
    tb = min(tb, r)

    def body(me_ref, x_ref, *rest):
        o_ref = rest[-1]
        acc = x_ref[...].astype(f32)
        for y_ref in rest[:-1]:
            acc = acc + y_ref[...].astype(f32)
        o_ref[...] = acc.astype(out_dtype)

    def slot(k):
        return pl.BlockSpec((None, tb, c), lambda i, me_ref: (me_ref[0] ^ k, i, 0))

    return pl.pallas_call(
        body, name=name,
        grid_spec=pltpu.PrefetchScalarGridSpec(
            num_scalar_prefetch=1, grid=(r // tb,), in_specs=[slot(k) for k in range(n)],
            out_specs=pl.BlockSpec((tb, c), lambda i, me_ref: (i, 0))),
        out_shape=_sds((r, c), out_dtype), compiler_params=_cparams(1))(
            jnp.reshape(me, (1,)).astype(jnp.int32), x, *([y] * (n - 1)))


def sum_slots(y, out_dtype, name, tb=256):
    n, r, c = y.shape
    tb = min(tb, r)

    def body(y_ref, o_ref):
        acc = y_ref[0].astype(f32)
        for i in range(1, n):
            acc = acc + y_ref[i].astype(f32)
        o_ref[...] = acc.astype(out_dtype)

    return pl.pallas_call(
        body, name=name, grid=(r // tb,),
        in_specs=[pl.BlockSpec((n, tb, c), lambda i: (0, i, 0))], out_specs=pl.BlockSpec((tb, c), lambda i: (i, 0)),
        out_shape=_sds((r, c), out_dtype), compiler_params=_cparams(1))(y)


def adamw(w, g, m, v, name, tb=256):
    r, c = w.shape
    tb = min(tb, r)

    def body(w_ref, g_ref, m_ref, v_ref, d_ref, mo_ref, vo_ref):
        gv = g_ref[...]
        m2 = B1 * m_ref[...] + (1.0 - B1) * gv
        v2 = B2 * v_ref[...] + (1.0 - B2) * (gv * gv)
        m_hat = m2 / (1.0 - B1 ** STEP)
        v_hat = v2 / (1.0 - B2 ** STEP)
        d_ref[...] = -LR * (m_hat / (jnp.sqrt(v_hat) + AEPS) + WD * w_ref[...])
        mo_ref[...] = m2
        vo_ref[...] = v2

    spec = pl.BlockSpec((tb, c), lambda i: (i, 0))
    return pl.pallas_call(
        body, name=name, grid=(r // tb,), in_specs=[spec] * 4, out_specs=[spec] * 3,
        out_shape=[_sds((r, c), f32)] * 3, compiler_params=_cparams(1))(w, g, m, v)


N_CHIPS = 4
SMALL_ROW = 1024
SMALL_PIECES = (("g1", D), ("ga", D), ("gs", D), ("g2", D), ("g3", D), ("g4", D), ("wa", 3 * D), ("ws", 4 * XBC),
                ("bs", XBC), ("par", 3 * CH))
SMALL_LAYER = 17 * SMALL_ROW


def _pack_small(grads):
    rows = []
    for g in grads:
        flat = [g["g1"], g["ga"], g["gs"], g["g2"], g["g3"], g["g4"], g["wa"][:3], g["ws"][:4], g["bs"], g["par"][:3]]
        v = jnp.concatenate([p.reshape(-1) for p in flat])
        rows.append(jnp.pad(v, (0, SMALL_LAYER - v.shape[0])))
    packed = jnp.concatenate(rows).reshape(-1, SMALL_ROW)
    return jnp.pad(packed, ((0, -packed.shape[0] % 8), (0, 0)))


def _unpack_small(packed, depth):
    flat = packed.reshape(-1)[:depth * SMALL_LAYER].reshape(depth, SMALL_LAYER)
    out, off = {}, 0
    for key, size in SMALL_PIECES:
        out[key] = flat[:, off:off + size]
        off += size
    return out


def kernel(x, norm_mix_pre, w_in, conv_a_w, ssm_conv_w, ssm_conv_b, dt_bias, a_log, d_skip, conv_out_norm, ssm_out_norm, w_out, norm_mix_post, norm_mlp_pre, w_up, w_down, norm_mlp_post, loss_target, m_norm_mix_pre, m_w_in, m_conv_a_w, m_ssm_conv_w, m_ssm_conv_b, m_dt_bias, m_a_log, m_d_skip, m_conv_out_norm, m_ssm_out_norm, m_w_out, m_norm_mix_post, m_norm_mlp_pre, m_w_up, m_w_down, m_norm_mlp_post, v_norm_mix_pre, v_w_in, v_conv_a_w, v_ssm_conv_w, v_ssm_conv_b, v_dt_bias, v_a_log, v_d_skip, v_conv_out_norm, v_ssm_out_norm, v_w_out, v_norm_mix_post, v_norm_mlp_pre, v_w_up, v_w_down, v_norm_mlp_post):
    nb, seq, _ = x.shape
    t = nb * seq
    depth = w_in.shape[0]
    half = depth // 2
    ncol = w_in.shape[2]
    chip = 2 * lax.axis_index("x") + lax.axis_index("y")

    shards = [w_in.astype(bf16), w_out.astype(bf16), w_up.astype(bf16), w_down.astype(bf16), conv_a_w, ssm_conv_w]
    gathered = all_gather(shards, "chips", "gather_weights", slot_axis=1, own=False)
    win_g, wo_g, wu_g, wd_g, wa_g, ws_g = [lax.dynamic_update_index_in_dim(g, s, chip, 1)
                                           for g, s in zip(gathered, shards)]
    win_full = jnp.pad(jnp.transpose(win_g, (0, 2, 1, 3)).reshape(depth, D, N_CHIPS * ncol),
                       ((0, 0), (0, 0), (0, PROJ - N_CHIPS * ncol)))
    wa_full = jnp.transpose(wa_g, (0, 2, 1, 3)).reshape(depth, 3, D)
    ws_full = jnp.transpose(ws_g, (0, 2, 1, 3)).reshape(depth, 4, XBC)
    lane_pad = lambda a: jnp.pad(a, ((0, 0), (0, CH - a.shape[1])))
    par = jnp.stack([lane_pad(dt_bias), lane_pad(a_log), lane_pad(d_skip)], axis=1)
    par = jnp.pad(par, ((0, 0), (0, 5), (0, 0)))
    w = dict(
        win=[win_full[l] for l in range(depth)],
        wo=[wo_g[l].reshape(2 * D, D) for l in range(depth)],
        wu=[wu_g[l] for l in range(depth)],
        wd=[wd_g[l].reshape(DFF, D) for l in range(depth)],
        wa=[jnp.pad(wa_full[l], ((0, 5), (0, 0))) for l in range(depth)],
        ws=[jnp.pad(ws_full[l], ((0, 4), (0, 0))) for l in range(depth)],
        bs=[ssm_conv_b[l][None] for l in range(depth)],
        par=[par[l] for l in range(depth)],
        g1=[norm_mix_pre[l][None] for l in range(depth)], ga=[conv_out_norm[l][None] for l in range(depth)],
        gs=[ssm_out_norm[l][None] for l in range(depth)], g2=[norm_mix_post[l][None] for l in range(depth)],
        g3=[norm_mlp_pre[l][None] for l in range(depth)], g4=[norm_mlp_post[l][None] for l in range(depth)])

    sse, dx, grads = local_step(x.reshape(t, D), loss_target.reshape(t, D), w, seq)
    loss = lax.psum(0.5 / D * sse[0, 0], ("x", "y", "c"))

    def slots(per_layer, to_chip_major):
        g = jnp.stack([to_chip_major(a) for a in per_layer])
        g = g.reshape((2, half) + g.shape[1:])
        return jnp.transpose(g, (0, 2, 1, 3, 4))

    big = [
        slots([g["win"] for g in grads],
              lambda a: jnp.transpose(a[:, :N_CHIPS * ncol].reshape(D, N_CHIPS, ncol), (1, 0, 2))),
        slots([g["wo"] for g in grads], lambda a: a.reshape(N_CHIPS, 2 * D // N_CHIPS, D)),
        slots([g["wu"] for g in grads], lambda a: a),
        slots([g["wd"] for g in grads], lambda a: a.reshape(N_CHIPS, DFF // N_CHIPS, D)),
    ]
    core = lax.axis_index("c")
    rows3 = lambda a: a.reshape(a.shape[0], -1, a.shape[-1])
    pair = all_to_all(big, "pair", "grads_to_pair")
    pair_sum = [sum_exchanged(rows3(b), rows3(p), core, bf16, "pair_sum").reshape(b.shape[1:])
                for b, p in zip(big, pair)]
    chips = all_to_all(pair_sum, "chips", "grads_to_chips")
    half_sum = [sum_exchanged(rows3(s), rows3(q), chip, f32, "chip_sum").reshape(s.shape[1:])
                for s, q in zip(pair_sum, chips)]
    full = all_gather(half_sum, "pair", "grads_from_pair", own=False)
    full = [lax.dynamic_update_index_in_dim(f, h, core, 0) for f, h in zip(full, half_sum)]
    g_win, g_wo, g_wu, g_wd = [f.reshape((depth,) + f.shape[2:]) for f in full]

    small_all = all_gather([_pack_small(grads)], "all", "gather_small")[0]
    small = _unpack_small(sum_slots(small_all, f32, "small_sum", tb=8), depth)
    wa_cols, ws_cols = conv_a_w.shape[2], ssm_conv_w.shape[2]
    par_g = small["par"].reshape(depth, 3, CH)
    g_small = dict(
        norm_mix_pre=small["g1"], conv_out_norm=small["ga"], ssm_out_norm=small["gs"], norm_mix_post=small["g2"],
        norm_mlp_pre=small["g3"], norm_mlp_post=small["g4"], ssm_conv_b=small["bs"],
        conv_a_w=lax.dynamic_slice_in_dim(small["wa"].reshape(depth, 3, D), chip * wa_cols, wa_cols, axis=2),
        ssm_conv_w=lax.dynamic_slice_in_dim(small["ws"].reshape(depth, 4, XBC), chip * ws_cols, ws_cols, axis=2),
        dt_bias=par_g[:, 0, :NH], a_log=par_g[:, 1, :NH], d_skip=par_g[:, 2, :NH])

    given = dict(norm_mix_pre=(norm_mix_pre, m_norm_mix_pre, v_norm_mix_pre), w_in=(w_in, m_w_in, v_w_in),
                 conv_a_w=(conv_a_w, m_conv_a_w, v_conv_a_w), ssm_conv_w=(ssm_conv_w, m_ssm_conv_w, v_ssm_conv_w),
                 ssm_conv_b=(ssm_conv_b, m_ssm_conv_b, v_ssm_conv_b), dt_bias=(dt_bias, m_dt_bias, v_dt_bias),
                 a_log=(a_log, m_a_log, v_a_log), d_skip=(d_skip, m_d_skip, v_d_skip),
                 conv_out_norm=(conv_out_norm, m_conv_out_norm, v_conv_out_norm),
                 ssm_out_norm=(ssm_out_norm, m_ssm_out_norm, v_ssm_out_norm), w_out=(w_out, m_w_out, v_w_out),
                 norm_mix_post=(norm_mix_post, m_norm_mix_post, v_norm_mix_post),
                 norm_mlp_pre=(norm_mlp_pre, m_norm_mlp_pre, v_norm_mlp_pre), w_up=(w_up, m_w_up, v_w_up),
                 w_down=(w_down, m_w_down, v_w_down), norm_mlp_post=(norm_mlp_post, m_norm_mlp_post, v_norm_mlp_post))
    grad = dict(g_small, w_in=g_win, w_out=g_wo, w_up=g_wu, w_down=g_wd)
    order = ["norm_mix_pre", "w_in", "conv_a_w", "ssm_conv_w", "ssm_conv_b", "dt_bias", "a_log", "d_skip",
             "conv_out_norm", "ssm_out_norm", "w_out", "norm_mix_post", "norm_mlp_pre", "w_up", "w_down",
             "norm_mlp_post"]
    g_out, d_out, m_out, v_out = [], [], [], []
    for n in order:
        wv, mv, vv = given[n]
        gv = grad[n].reshape(wv.shape)
        two_d = lambda a: a.reshape(-1, a.shape[-1])
        dlt, m2, v2 = adamw(two_d(wv), two_d(gv), two_d(mv), two_d(vv), "adamw")
        g_out.append(gv)
        d_out.append(dlt.reshape(wv.shape))
        m_out.append(m2.reshape(wv.shape))
        v_out.append(v2.reshape(wv.shape))
    return (loss, dx.reshape(nb, seq, D), *g_out, *d_out, *m_out, *v_out)
```

```python
import functools

import jax
import jax.numpy as jnp
from jax import lax
from jax.experimental import pallas as pl
from jax.experimental.pallas import tpu as pltpu

f32, bf16 = jnp.float32, jnp.bfloat16

D = 1024
NH, HP = 16, 64
NG, NS = 2, 128
CH = 128
XBC = D + 2 * NG * NS
DFF = 4 * D
IN_COLS = 3 * D + D + XBC + NH
PROJ = 5760
COL_Z, COL_XBC, COL_DT = 3 * D, 4 * D, 4 * D + XBC
EPS = 1e-6
HALO = 8
VMEM_LIMIT = 56 * 2**20
MESH = pl.DeviceIdType.MESH

LR, B1, B2, AEPS, WD, STEP = 0.001, 0.9, 0.999, 1e-08, 0.01, 10


def _cparams(n_axes):
    return pltpu.CompilerParams(dimension_semantics=("arbitrary",) * n_axes, vmem_limit_bytes=VMEM_LIMIT)


def _sds(shape, dtype):
    return jax.ShapeDtypeStruct(tuple(shape), dtype)


def _token_spec(token):
    return [] if token is None else [pl.BlockSpec(memory_space=pl.ANY)]


def _token_arg(token):
    return [] if token is None else [token]


def _rms_fwd(x, g):
    r = lax.rsqrt(jnp.mean(x * x, axis=-1, keepdims=True) + EPS)
    return x * r * g


def _rms_bwd(x, g, dy):
    r = lax.rsqrt(jnp.mean(x * x, axis=-1, keepdims=True) + EPS)
    xh = x * r
    gdy = dy * g
    dx = r * (gdy - xh * jnp.mean(xh * gdy, axis=-1, keepdims=True))
    return dx, dy * xh


def _accum(ref, part, first):
    @pl.when(first)
    def _():
        ref[...] = part

    @pl.when(jnp.logical_not(first))
    def _():
        ref[...] += part


def _dot_nt(a, b):
    return lax.dot_general(a, b, (((1,), (1,)), ((), ())), preferred_element_type=f32)


def _dot_tn(a, b):
    return lax.dot_general(a, b, (((0,), (0,)), ((), ())), preferred_element_type=f32)


def _dot(a, b):
    return jnp.dot(a, b, preferred_element_type=f32)


def _split_dot(x, e_bf, n_split, nt=False):
    acc = None
    rem = x
    for s in range(n_split):
        hi = rem.astype(bf16)
        term = _dot_nt(hi, e_bf) if nt else _dot(hi, e_bf)
        acc = term if acc is None else acc + term
        if s + 1 < n_split:
            rem = rem - hi.astype(f32)
    return acc


def _sigmoid(x):
    return 1.0 / (1.0 + jnp.exp(-x))


def norm_matmul(x, g, w, tm, tn, out_dtype, name, token=None):
    t = x.shape[0]
    if w.ndim == 3:
        assert w.shape[2] == tn
        n = w.shape[0] * tn
        w_spec = pl.BlockSpec((None, D, tn), lambda i, j: (j, 0, 0))
    else:
        n = w.shape[1]
        w_spec = pl.BlockSpec((D, tn), lambda i, j: (0, j))

    def body(x_ref, g_ref, w_ref, *rest):
        o_ref, h_ref = rest[-2:]

        @pl.when(pl.program_id(1) == 0)
        def _():
            h_ref[...] = _rms_fwd(x_ref[...], g_ref[...]).astype(bf16)

        o_ref[...] = _dot(h_ref[...], w_ref[...]).astype(out_dtype)

    return pl.pallas_call(
        body, name=name, grid=(t // tm, n // tn),
        in_specs=[pl.BlockSpec((tm, D), lambda i, j: (i, 0)), pl.BlockSpec((1, D), lambda i, j: (0, 0)), w_spec]
        + _token_spec(token),
        out_specs=[pl.BlockSpec((tm, tn), lambda i, j: (i, j)), pl.BlockSpec((tm, D), lambda i, j: (i, 0))],
        out_shape=[_sds((t, n), out_dtype), _sds((t, D), bf16)],
        compiler_params=_cparams(2))(x, g, w, *_token_arg(token))


def matmul_postnorm(a, w, xres, g, tm, relu2, name):
    t, k = a.shape

    def body(a_ref, w_ref, xr_ref, g_ref, y_ref, xo_ref):
        av = a_ref[...]
        if relu2:
            af = jnp.maximum(av.astype(f32), 0.0)
            av = (af * af).astype(bf16)
        y = _dot(av, w_ref[...])
        y_ref[...] = y
        xo_ref[...] = xr_ref[...] + _rms_fwd(y, g_ref[...])

    return pl.pallas_call(
        body, name=name, grid=(t // tm,),
        in_specs=[pl.BlockSpec((tm, k), lambda i: (i, 0)), pl.BlockSpec((k, D), lambda i: (0, 0)),
                  pl.BlockSpec((tm, D), lambda i: (i, 0)), pl.BlockSpec((1, D), lambda i: (0, 0))],
        out_specs=[pl.BlockSpec((tm, D), lambda i: (i, 0)), pl.BlockSpec((tm, D), lambda i: (i, 0))],
        out_shape=[_sds((t, D), f32), _sds((t, D), f32)],
        compiler_params=_cparams(1))(a, w, xres, g)


def postnorm_bwd_matmul(y, g, dxo, w, fp, tm, tn, out_dtype, name, token=None):
    t, n = y.shape[0], w.shape[0]
    relu = fp is not None

    def body(*refs):
        y_ref, g_ref, dxo_ref, w_ref = refs[:4]
        fp_ref = refs[4] if relu else None
        dy_ref, dg_ref, da_ref = refs[-3:]
        i, j = pl.program_id(0), pl.program_id(1)

        @pl.when(j == 0)
        def _():
            dx, dgc = _rms_bwd(y_ref[...], g_ref[...], dxo_ref[...])
            dy_ref[...] = dx.astype(bf16)
            _accum(dg_ref, jnp.sum(dgc, axis=0, keepdims=True), i == 0)

        da = _dot_nt(dy_ref[...], w_ref[...])
        if relu:
            da = da * (2.0 * jnp.maximum(fp_ref[...].astype(f32), 0.0))
        da_ref[...] = da.astype(out_dtype)

    in_specs = [pl.BlockSpec((tm, D), lambda i, j: (i, 0)), pl.BlockSpec((1, D), lambda i, j: (0, 0)),
                pl.BlockSpec((tm, D), lambda i, j: (i, 0)), pl.BlockSpec((tn, D), lambda i, j: (j, 0))]
    args = [y, g, dxo, w]
    if relu:
        in_specs.append(pl.BlockSpec((tm, tn), lambda i, j: (i, j)))
        args.append(fp)
    in_specs += _token_spec(token)
    args += _token_arg(token)
    return pl.pallas_call(
        body, name=name, grid=(t // tm, n // tn), in_specs=in_specs,
        out_specs=[pl.BlockSpec((tm, D), lambda i, j: (i, 0)), pl.BlockSpec((1, D), lambda i, j: (0, 0)),
                   pl.BlockSpec((tm, tn), lambda i, j: (i, j))],
        out_shape=[_sds((t, D), bf16), _sds((1, D), f32), _sds((t, n), out_dtype)],
        compiler_params=_cparams(2))(*args)


def matmul_prenorm_bwd(da, w, x, g, dxo, tm, name):
    t, k = da.shape
    blocked = w.ndim == 3

    def body(da_ref, w_ref, x_ref, g_ref, dxo_ref, dx_ref, dg_ref):
        if blocked:
            kc = w.shape[2]
            dh = _dot_nt(da_ref[:, 0:kc], w_ref[0])
            for q in range(1, w.shape[0]):
                dh = dh + _dot_nt(da_ref[:, q * kc:(q + 1) * kc], w_ref[q])
        else:
            dh = _dot_nt(da_ref[...], w_ref[...])
        dxn, dgc = _rms_bwd(x_ref[...], g_ref[...], dh)
        dx_ref[...] = dxo_ref[...] + dxn
        _accum(dg_ref, jnp.sum(dgc, axis=0, keepdims=True), pl.program_id(0) == 0)

    w_spec = pl.BlockSpec(w.shape, (lambda i: (0, 0, 0)) if blocked else (lambda i: (0, 0)))
    return pl.pallas_call(
        body, name=name, grid=(t // tm,),
        in_specs=[pl.BlockSpec((tm, k), lambda i: (i, 0)), w_spec,
                  pl.BlockSpec((tm, D), lambda i: (i, 0)), pl.BlockSpec((1, D), lambda i: (0, 0)),
                  pl.BlockSpec((tm, D), lambda i: (i, 0))],
        out_specs=[pl.BlockSpec((tm, D), lambda i: (i, 0)), pl.BlockSpec((1, D), lambda i: (0, 0))],
        out_shape=[_sds((t, D), f32), _sds((1, D), f32)],
        compiler_params=_cparams(1))(da, w, x, g, dxo)


def matmul_tn(a, b, tm, tn, relu2, name, col_blocks=False):
    t, m = a.shape
    n = b.shape[1]
    if col_blocks:
        out_spec, out_shape = pl.BlockSpec((None, tm, tn), lambda i, j: (j, i, 0)), _sds((n // tn, m, tn), f32)
    else:
        out_spec, out_shape = pl.BlockSpec((tm, tn), lambda i, j: (i, j)), _sds((m, n), f32)

    def body(a_ref, b_ref, o_ref, at_ref):
        @pl.when(pl.program_id(1) == 0)
        def _():
            av = a_ref[...]
            if relu2:
                af = jnp.maximum(av.astype(f32), 0.0)
                av = (af * af).astype(bf16)
            at_ref[...] = av.T

        o_ref[...] = _dot(at_ref[...], b_ref[...])

    return pl.pallas_call(
        body, name=name, grid=(m // tm, n // tn),
        in_specs=[pl.BlockSpec((t, tm), lambda i, j: (0, i)), pl.BlockSpec((t, tn), lambda i, j: (0, j))],
        out_specs=out_spec, out_shape=out_shape,
        scratch_shapes=[pltpu.VMEM((tm, t), bf16)],
        compiler_params=_cparams(2))(a, b)


def _halo_prev(tb, col):
    return lambda i: (jnp.maximum(i * (tb // HALO) - 1, 0), col)


def _halo_next(tb, col, t):
    return lambda i: (jnp.minimum((i + 1) * (tb // HALO), t // HALO - 1), col)


def group_a_fwd(proj, wa, g, seq, tb, name):
    t = proj.shape[0]
    bps = seq // tb

    def body(xa_ref, ca_ref, ba_ref, xah_ref, cah_ref, wa_ref, g_ref, o_ref, u_scr):
        first = (pl.program_id(0) % bps) == 0
        u = ca_ref[...] * xa_ref[...]
        u_scr[0:HALO, :] = jnp.where(first, 0.0, cah_ref[...] * xah_ref[...])
        u_scr[HALO:HALO + tb, :] = u
        w = wa_ref[...]
        cv = w[2:3] * u + w[1:2] * u_scr[pl.ds(HALO - 1, tb), :] + w[0:1] * u_scr[pl.ds(HALO - 2, tb), :]
        o_ref[...] = _rms_fwd(ba_ref[...] * cv, g_ref[...]).astype(bf16)

    blk = lambda c: pl.BlockSpec((tb, D), lambda i: (i, c))
    return pl.pallas_call(
        body, name=name, grid=(t // tb,),
        in_specs=[blk(0), blk(1), blk(2),
                  pl.BlockSpec((HALO, D), _halo_prev(tb, 0)), pl.BlockSpec((HALO, D), _halo_prev(tb, 1)),
                  pl.BlockSpec((8, D), lambda i: (0, 0)), pl.BlockSpec((1, D), lambda i: (0, 0))],
        out_specs=pl.BlockSpec((tb, D), lambda i: (i, 0)),
        out_shape=_sds((t, D), bf16),
        scratch_shapes=[pltpu.VMEM((tb + HALO, D), f32)],
        compiler_params=_cparams(1))(proj, proj, proj, proj, proj, wa, g)


def group_a_bwd(proj, dcat, wa, g, seq, tb, name):
    t = proj.shape[0]
    bps = seq // tb

    def body(xa_ref, ca_ref, ba_ref, dy_ref, xap_ref, cap_ref, xan_ref, can_ref, ban_ref, dyn_ref, wa_ref, g_ref,
             dp_ref, dwa_ref, dg_ref, u_scr, d_scr):
        i = pl.program_id(0)
        first = (i % bps) == 0
        last = (i % bps) == bps - 1
        w = wa_ref[...]
        gv = g_ref[...]
        xa, ca, ba = xa_ref[...], ca_ref[...], ba_ref[...]
        u_scr[0:HALO, :] = jnp.where(first, 0.0, cap_ref[...] * xap_ref[...])
        u_scr[HALO:HALO + tb, :] = ca * xa
        u_scr[HALO + tb:2 * HALO + tb, :] = can_ref[...] * xan_ref[...]

        def conv(start, rows):
            return (w[2:3] * u_scr[pl.ds(start, rows), :] + w[1:2] * u_scr[pl.ds(start - 1, rows), :]
                    + w[0:1] * u_scr[pl.ds(start - 2, rows), :])

        cv = conv(HALO, tb)
        dya, dgc = _rms_bwd(ba * cv, gv, dy_ref[...])
        ban = ban_ref[...]
        dyan, _ = _rms_bwd(ban * conv(HALO + tb, HALO), gv, dyn_ref[...])
        dcv = dya * ba
        d_scr[0:tb, :] = dcv
        d_scr[tb:tb + HALO, :] = jnp.where(last, 0.0, dyan * ban)
        du = w[2:3] * dcv + w[1:2] * d_scr[pl.ds(1, tb), :] + w[0:1] * d_scr[pl.ds(2, tb), :]
        dp_ref[:, 0:D] = (du * ca).astype(bf16)
        dp_ref[:, D:2 * D] = (du * xa).astype(bf16)
        dp_ref[:, 2 * D:3 * D] = (dya * cv).astype(bf16)
        row = lax.broadcasted_iota(jnp.int32, (8, D), 0)
        dw = jnp.zeros((8, D), f32)
        for k in range(3):
            s = jnp.sum(dcv * u_scr[pl.ds(HALO - 2 + k, tb), :], axis=0, keepdims=True)
            dw = jnp.where(row == k, s, dw)
        _accum(dwa_ref, dw, i == 0)
        _accum(dg_ref, jnp.sum(dgc, axis=0, keepdims=True), i == 0)

    blk = lambda c: pl.BlockSpec((tb, D), lambda i: (i, c))
    prv = lambda c: pl.BlockSpec((HALO, D), _halo_prev(tb, c))
    nxt = lambda c: pl.BlockSpec((HALO, D), _halo_next(tb, c, t))
    return pl.pallas_call(
        body, name=name, grid=(t // tb,),
        in_specs=[blk(0), blk(1), blk(2), blk(0), prv(0), prv(1), nxt(0), nxt(1), nxt(2), nxt(0),
                  pl.BlockSpec((8, D), lambda i: (0, 0)), pl.BlockSpec((1, D), lambda i: (0, 0))],
        out_specs=[pl.BlockSpec((tb, 3 * D), lambda i: (i, 0)), pl.BlockSpec((8, D), lambda i: (0, 0)),
                   pl.BlockSpec((1, D), lambda i: (0, 0))],
        out_shape=[_sds((t, 3 * D), bf16), _sds((8, D), f32), _sds((1, D), f32)],
        scratch_shapes=[pltpu.VMEM((tb + 2 * HALO, D), f32), pltpu.VMEM((tb + HALO, D), f32)],
        compiler_params=_cparams(1))(proj, proj, proj, dcat, proj, proj, proj, proj, proj, dcat, wa, g)


CB = 512
XBC_BLK0 = COL_XBC // CB


def conv_b_fwd(proj, ws, bs, seq, tb, name):
    t = proj.shape[0]
    bps = seq // tb

    def body(x_ref, xp_ref, w_ref, b_ref, o_ref, x_scr):
        first = (pl.program_id(1) % bps) == 0
        x = x_ref[...]
        x_scr[0:HALO, :] = jnp.where(first, 0.0, xp_ref[...])
        x_scr[HALO:HALO + tb, :] = x
        w = w_ref[...]
        xc = w[3:4] * x + b_ref[...]
        for k in range(3):
            xc = xc + w[k:k + 1] * x_scr[pl.ds(HALO - 3 + k, tb), :]
        o_ref[...] = xc * _sigmoid(xc)

    return pl.pallas_call(
        body, name=name, grid=(XBC // CB, t // tb),
        in_specs=[pl.BlockSpec((tb, CB), lambda j, i: (i, XBC_BLK0 + j)),
                  pl.BlockSpec((HALO, CB), lambda j, i: (jnp.maximum(i * (tb // HALO) - 1, 0), XBC_BLK0 + j)),
                  pl.BlockSpec((8, CB), lambda j, i: (0, j)), pl.BlockSpec((1, CB), lambda j, i: (0, j))],
        out_specs=pl.BlockSpec((tb, CB), lambda j, i: (i, j)),
        out_shape=_sds((t, XBC), f32),
        scratch_shapes=[pltpu.VMEM((tb + HALO, CB), f32)],
        compiler_params=_cparams(2))(proj, proj, ws, bs)


def conv_b_bwd(proj, dxs, ws, bs, seq, tb, name):
    t = proj.shape[0]
    bps = seq // tb

    def body(x_ref, xp_ref, xn_ref, d_ref, dn_ref, w_ref, b_ref, dx_ref, dw_ref, db_ref, x_scr, d_scr):
        i = pl.program_id(1)
        first = (i % bps) == 0
        last = (i % bps) == bps - 1
        w = w_ref[...]
        bias = b_ref[...]
        x_scr[0:HALO, :] = jnp.where(first, 0.0, xp_ref[...])
        x_scr[HALO:HALO + tb, :] = x_ref[...]
        x_scr[HALO + tb:2 * HALO + tb, :] = xn_ref[...]

        def dsilu_at(start, rows, d):
            xc = bias + w[3:4] * x_scr[pl.ds(start, rows), :]
            for k in range(3):
                xc = xc + w[k:k + 1] * x_scr[pl.ds(start - 3 + k, rows), :]
            sg = _sigmoid(xc)
            return d * (sg * (1.0 + xc * (1.0 - sg)))

        dxc = dsilu_at(HALO, tb, d_ref[...])
        d_scr[0:tb, :] = dxc
        d_scr[tb:tb + HALO, :] = jnp.where(last, 0.0, dsilu_at(HALO + tb, HALO, dn_ref[...]))
        dx = w[3:4] * dxc
        for k in range(3):
            dx = dx + w[k:k + 1] * d_scr[pl.ds(3 - k, tb), :]
        dx_ref[...] = dx.astype(bf16)
        row = lax.broadcasted_iota(jnp.int32, (8, CB), 0)
        dw = jnp.zeros((8, CB), f32)
        for k in range(4):
            s = jnp.sum(dxc * x_scr[pl.ds(HALO - 3 + k, tb), :], axis=0, keepdims=True)
            dw = jnp.where(row == k, s, dw)
        _accum(dw_ref, dw, i == 0)
        _accum(db_ref, jnp.sum(dxc, axis=0, keepdims=True), i == 0)

    nh = t // HALO
    return pl.pallas_call(
        body, name=name, grid=(XBC // CB, t // tb),
        in_specs=[pl.BlockSpec((tb, CB), lambda j, i: (i, XBC_BLK0 + j)),
                  pl.BlockSpec((HALO, CB), lambda j, i: (jnp.maximum(i * (tb // HALO) - 1, 0), XBC_BLK0 + j)),
                  pl.BlockSpec((HALO, CB), lambda j, i: (jnp.minimum((i + 1) * (tb // HALO), nh - 1), XBC_BLK0 + j)),
                  pl.BlockSpec((tb, CB), lambda j, i: (i, j)),
                  pl.BlockSpec((HALO, CB), lambda j, i: (jnp.minimum((i + 1) * (tb // HALO), nh - 1), j)),
                  pl.BlockSpec((8, CB), lambda j, i: (0, j)), pl.BlockSpec((1, CB), lambda j, i: (0, j))],
        out_specs=[pl.BlockSpec((tb, CB), lambda j, i: (i, j)), pl.BlockSpec((8, CB), lambda j, i: (0, j)),
                   pl.BlockSpec((1, CB), lambda j, i: (0, j))],
        out_shape=[_sds((t, XBC), bf16), _sds((8, XBC), f32), _sds((1, XBC), f32)],
        scratch_shapes=[pltpu.VMEM((tb + 2 * HALO, CB), f32), pltpu.VMEM((tb + HALO, CB), f32)],
        compiler_params=_cparams(2))(proj, proj, proj, dxs, dxs, ws, bs)


GW = D // NG


def _ssd_consts():
    head_of_lane = jnp.arange(D) // HP
    expand = (jnp.arange(CH)[:, None] == head_of_lane[None, :]).astype(bf16)
    tri = (jnp.arange(CH)[:, None] >= jnp.arange(CH)[None, :]).astype(f32)
    return expand, tri


def _ssd_common(par_ref, dtr_ref, e_ref, tri_ref):
    par = par_ref[...]
    dtb, alog, dsk = par[0:1], par[1:2], par[2:3]
    lane = lax.broadcasted_iota(jnp.int32, (CH, CH), 1)
    a = -jnp.exp(alog)
    dtr = dtr_ref[...] + dtb
    sp = jnp.maximum(dtr, 0.0) + jnp.log(1.0 + jnp.exp(-jnp.abs(dtr)))
    dt = jnp.where(lane < NH, sp, 0.0)
    cs = jnp.dot(tri_ref[...], dt * a, precision=lax.Precision.HIGHEST, preferred_element_type=f32)
    cs_last = cs[CH - 1:CH, :]
    dte = jnp.exp(cs_last - cs)
    ecs = jnp.exp(cs)
    ecl = jnp.exp(cs_last)
    e = e_ref[...]
    row8 = lax.broadcasted_iota(jnp.int32, (8, CH), 0)
    r8 = _split_dot(jnp.where(row8 == 0, ecl, jnp.where(row8 == 1, dsk, 0.0)), e, 3)
    return dict(a=a, dtr=dtr, dt=dt, cs=cs, cst=cs.T, dte=dte, ecs=ecs, ecl=ecl, e=e, lane=lane,
                dt_x=_split_dot(dt, e, 3), dte_x=_split_dot(dte, e, 3), ecs_x=_split_dot(ecs, e, 3),
                ecl_x=r8[0:1], dsk_x=r8[1:2])


def _decay_matrix(c, h):
    li = lax.broadcasted_iota(jnp.int32, (CH, CH), 0)
    seg = c["cs"][:, h:h + 1] - c["cst"][h:h + 1, :]
    return jnp.exp(jnp.where(li >= c["lane"], seg, -jnp.inf))


def _gate_norm_fwd(y, z, gs):
    zg = z * _sigmoid(z)
    yg = y * zg
    return jnp.concatenate([_rms_fwd(yg[:, k * GW:(k + 1) * GW], gs[:, k * GW:(k + 1) * GW]) for k in range(NG)], axis=1)


def ssd_fwd(xbcs, proj, par, gs, seq, name):
    t = xbcs.shape[0]
    nc = seq // CH
    expand, tri = _ssd_consts()

    def body(xs_ref, b_ref, c_ref, dtr_ref, z_ref, par_ref, e_ref, tri_ref, gs_ref, yn_ref, y_ref, st_ref, p_scr, yd_scr):
        @pl.when(pl.program_id(0) % nc == 0)
        def _():
            p_scr[...] = jnp.zeros_like(p_scr)

        c = _ssd_common(par_ref, dtr_ref, e_ref, tri_ref)
        xs = xs_ref[...]
        xdt = xs * c["dt_x"]
        xdt_b = xdt.astype(bf16)
        xdte_b = (xdt * c["dte_x"]).astype(bf16)
        p = p_scr[...]
        st_ref[0] = p
        p_b = p.astype(bf16)
        lo = c["lane"] < HP
        for g in range(NG):
            bg = b_ref[:, g * NS:(g + 1) * NS].astype(bf16)
            cg = c_ref[:, g * NS:(g + 1) * NS].astype(bf16)
            gmat = _dot_nt(cg, bg)
            for q in range(GW // CH):
                col = g * GW + q * CH
                xp = xdt_b[:, col:col + CH]
                h0 = col // HP
                m0 = (gmat * _decay_matrix(c, h0)).astype(bf16)
                m1 = (gmat * _decay_matrix(c, h0 + 1)).astype(bf16)
                yd_scr[:, col:col + CH] = (_dot(m0, jnp.where(lo, xp, jnp.zeros_like(xp)))
                                           + _dot(m1, jnp.where(lo, jnp.zeros_like(xp), xp)))
            gsl = slice(g * GW, (g + 1) * GW)
            yoff = _dot(cg, p_b[:, gsl]) * c["ecs_x"][:, gsl]
            yd_scr[:, gsl] = yd_scr[:, gsl] + yoff
            p_scr[:, gsl] = p[:, gsl] * c["ecl_x"][:, gsl] + _dot_tn(bg, xdte_b[:, gsl])
        y = yd_scr[...] + c["dsk_x"] * xs
        y_ref[...] = y
        yn_ref[...] = _gate_norm_fwd(y, z_ref[...], gs_ref[...]).astype(bf16)

    nb = t // CH
    return pl.pallas_call(
        body, name=name, grid=(nb,),
        in_specs=[pl.BlockSpec((CH, D), lambda i: (i, 0)),
                  pl.BlockSpec((CH, NG * NS), lambda i: (i, D // (NG * NS))),
                  pl.BlockSpec((CH, NG * NS), lambda i: (i, D // (NG * NS) + 1)),
                  pl.BlockSpec((CH, CH), lambda i: (i, COL_DT // CH)),
                  pl.BlockSpec((CH, D), lambda i: (i, COL_Z // D)),
                  pl.BlockSpec((8, CH), lambda i: (0, 0)), pl.BlockSpec((CH, D), lambda i: (0, 0)),
                  pl.BlockSpec((CH, CH), lambda i: (0, 0)), pl.BlockSpec((1, D), lambda i: (0, 0))],
        out_specs=[pl.BlockSpec((CH, D), lambda i: (i, 0)), pl.BlockSpec((CH, D), lambda i: (i, 0)),
                   pl.BlockSpec((1, NS, D), lambda i: (i, 0, 0))],
        out_shape=[_sds((t, D), bf16), _sds((t, D), f32), _sds((nb, NS, D), f32)],
        scratch_shapes=[pltpu.VMEM((NS, D), f32), pltpu.VMEM((CH, D), f32)],
        compiler_params=_cparams(1))(xbcs, xbcs, xbcs, proj, proj, par, expand, tri, gs)


def ssd_bwd(xbcs, proj, ypre, states, dcat, par, gs, seq, name):
    t = xbcs.shape[0]
    nc = seq // CH
    expand, tri = _ssd_consts()

    def body(xs_ref, b_ref, c_ref, dtr_ref, z_ref, y_ref, st_ref, dyn_ref, par_ref, e_ref, tri_ref, gs_ref,
             dx_ref, dz_ref, ddt_ref, dpar_ref, dgs_ref, dp_scr, dxdt_scr):
        i = pl.program_id(0)

        @pl.when(i % nc == 0)
        def _():
            dp_scr[...] = jnp.zeros_like(dp_scr)

        c = _ssd_common(par_ref, dtr_ref, e_ref, tri_ref)
        e = c["e"]
        lane = c["lane"]
        sub = lax.broadcasted_iota(jnp.int32, (CH, CH), 0)
        xs = xs_ref[...]
        xdt = xs * c["dt_x"]
        xdt_b = xdt.astype(bf16)
        xdte_b = (xdt * c["dte_x"]).astype(bf16)
        p = st_ref[0]
        p_b = p.astype(bf16)
        dpn = dp_scr[...]
        dpn_b = dpn.astype(bf16)

        y, z, gs_v = y_ref[...], z_ref[...], gs_ref[...]
        zs = _sigmoid(z)
        zg = z * zs
        yg = y * zg
        parts, gparts = [], []
        for k in range(NG):
            sl = slice(k * GW, (k + 1) * GW)
            dxk, dgk = _rms_bwd(yg[:, sl], gs_v[:, sl], dyn_ref[:, sl])
            parts.append(dxk)
            gparts.append(dgk)
        dyg = jnp.concatenate(parts, axis=1)
        dgs_rows = jnp.concatenate(gparts, axis=1)
        dy = dyg * zg
        dz_ref[...] = (dyg * y * (zs * (1.0 + z * (1.0 - zs)))).astype(bf16)
        dy_b = dy.astype(bf16)
        dq_b = (dy * c["ecs_x"]).astype(bf16)

        lo = lane < HP
        dcs = jnp.zeros((CH, CH), f32)
        dcst = jnp.zeros((CH, CH), f32)
        for g in range(NG):
            gsl = slice(g * GW, (g + 1) * GW)
            bg = b_ref[:, g * NS:(g + 1) * NS].astype(bf16)
            cg = c_ref[:, g * NS:(g + 1) * NS].astype(bf16)
            gmat = _dot_nt(cg, bg)
            dgm = jnp.zeros((CH, CH), f32)
            for q in range(GW // CH):
                col = g * GW + q * CH
                xp = xdt_b[:, col:col + CH]
                dyp = dy_b[:, col:col + CH]
                acc = None
                for hh in range(2):
                    h = col // HP + hh
                    keep = lo if hh == 0 else jnp.logical_not(lo)
                    dyh = jnp.where(keep, dyp, jnp.zeros_like(dyp))
                    dec = _decay_matrix(c, h)
                    m = gmat * dec
                    dm = _dot_nt(dyh, xp)
                    dseg = dm * m
                    dcs = dcs + jnp.where(lane == h, jnp.sum(dseg, axis=1, keepdims=True), 0.0)
                    dcst = dcst + jnp.where(sub == h, jnp.sum(dseg, axis=0, keepdims=True), 0.0)
                    dgm = dgm + dm * dec
                    term = _dot_tn(m.astype(bf16), dyh)
                    acc = term if acc is None else acc + term
                dxdt_scr[:, col:col + CH] = acc
            dgm_b = dgm.astype(bf16)
            bds = _dot(bg, dpn_b[:, gsl])
            dxdt_scr[:, gsl] = dxdt_scr[:, gsl] + c["dte_x"][:, gsl] * bds
            dc_g = _dot(dgm_b, bg) + _dot_nt(dq_b[:, gsl], p_b[:, gsl])
            db_g = _dot_tn(dgm_b, cg) + _dot_nt(xdte_b[:, gsl], dpn_b[:, gsl])
            dx_ref[:, D + g * NS:D + (g + 1) * NS] = db_g
            dx_ref[:, D + NG * NS + g * NS:D + NG * NS + (g + 1) * NS] = dc_g
            dp_scr[:, gsl] = dpn[:, gsl] * c["ecl_x"][:, gsl] + _dot_tn(cg, dq_b[:, gsl])
            q_g = _dot(cg, p_b[:, gsl])
            e_g = e[:, gsl]
            dcs = dcs + c["ecs"] * _split_dot(dy[:, gsl] * q_g, e_g, 2, nt=True)
            ddte = _split_dot(xdt[:, gsl] * bds, e_g, 2, nt=True) * c["dte"]
            dcs = dcs - ddte
            dcs = dcs + jnp.where(sub == CH - 1, jnp.sum(ddte, axis=0, keepdims=True), 0.0)

        decl = _split_dot(jnp.broadcast_to(jnp.sum(dpn * p, axis=0, keepdims=True), (8, D)), e, 2, nt=True)[0:1]
        dcs = dcs + jnp.where(sub == CH - 1, c["ecl"] * decl, 0.0)
        dcs = dcs - dcst.T
        dadt = lax.dot_general(tri_ref[...], dcs, (((0,), (0,)), ((), ())), precision=lax.Precision.HIGHEST,
                               preferred_element_type=f32)
        dxdt = dxdt_scr[...]
        ddt = dadt * c["a"] + _split_dot(dxdt * xs, e, 2, nt=True)
        ddtr = jnp.where(lane < NH, ddt * _sigmoid(c["dtr"]), 0.0)
        ddt_ref[...] = ddtr.astype(bf16)
        dx_ref[:, 0:D] = dxdt * c["dt_x"] + c["dsk_x"] * dy
        dsk = _split_dot(jnp.broadcast_to(jnp.sum(dy * xs, axis=0, keepdims=True), (8, D)), e, 2, nt=True)[0:1]
        dalog = jnp.sum(dadt * c["dt"], axis=0, keepdims=True) * c["a"]
        row8 = lax.broadcasted_iota(jnp.int32, (8, CH), 0)
        dpar = jnp.where(row8 == 0, jnp.sum(ddtr, axis=0, keepdims=True),
                         jnp.where(row8 == 1, dalog, jnp.where(row8 == 2, dsk, 0.0)))
        dpar = jnp.where(lax.broadcasted_iota(jnp.int32, (8, CH), 1) < NH, dpar, 0.0)
        _accum(dpar_ref, dpar, i == 0)
        _accum(dgs_ref, jnp.sum(dgs_rows, axis=0, keepdims=True), i == 0)

    nb = t // CH
    rev = lambda i: (i // nc) * nc + (nc - 1 - i % nc)
    return pl.pallas_call(
        body, name=name, grid=(nb,),
        in_specs=[pl.BlockSpec((CH, D), lambda i: (rev(i), 0)),
                  pl.BlockSpec((CH, NG * NS), lambda i: (rev(i), D // (NG * NS))),
                  pl.BlockSpec((CH, NG * NS), lambda i: (rev(i), D // (NG * NS) + 1)),
                  pl.BlockSpec((CH, CH), lambda i: (rev(i), COL_DT // CH)),
                  pl.BlockSpec((CH, D), lambda i: (rev(i), COL_Z // D)),
                  pl.BlockSpec((CH, D), lambda i: (rev(i), 0)),
                  pl.BlockSpec((1, NS, D), lambda i: (rev(i), 0, 0)),
                  pl.BlockSpec((CH, D), lambda i: (rev(i), 1)),
                  pl.BlockSpec((8, CH), lambda i: (0, 0)), pl.BlockSpec((CH, D), lambda i: (0, 0)),
                  pl.BlockSpec((CH, CH), lambda i: (0, 0)), pl.BlockSpec((1, D), lambda i: (0, 0))],
        out_specs=[pl.BlockSpec((CH, XBC), lambda i: (rev(i), 0)), pl.BlockSpec((CH, D), lambda i: (rev(i), 0)),
                   pl.BlockSpec((CH, CH), lambda i: (rev(i), 0)),
                   pl.BlockSpec((8, CH), lambda i: (0, 0)), pl.BlockSpec((1, D), lambda i: (0, 0))],
        out_shape=[_sds((t, XBC), f32), _sds((t, D), bf16), _sds((t, CH), bf16), _sds((8, CH), f32), _sds((1, D), f32)],
        scratch_shapes=[pltpu.VMEM((NS, D), f32), pltpu.VMEM((CH, D), f32)],
        compiler_params=_cparams(1))(xbcs, xbcs, xbcs, proj, proj, ypre, states, dcat, par, expand, tri, gs)


def loss_head(y, target, tb, name):
    t = y.shape[0]

    def body(y_ref, t_ref, s_ref, dy_ref):
        err = y_ref[...] - t_ref[...]
        dy_ref[...] = err * (1.0 / D)
        _accum(s_ref, jnp.zeros((8, CH), f32) + jnp.sum(err * err), pl.program_id(0) == 0)

    return pl.pallas_call(
        body, name=name, grid=(t // tb,),
        in_specs=[pl.BlockSpec((tb, D), lambda i: (i, 0)), pl.BlockSpec((tb, D), lambda i: (i, 0))],
        out_specs=[pl.BlockSpec((8, CH), lambda i: (0, 0)), pl.BlockSpec((tb, D), lambda i: (i, 0))],
        out_shape=[_sds((8, CH), f32), _sds((t, D), f32)],
        compiler_params=_cparams(1))(y, target)


def _tiles(t, seq):
    tm = min(512, t)
    return dict(tm=tm, tm_small=min(256, t), tb=min(512, seq), tb_conv=min(512, seq))


def local_step(x, target, depth, weights_of, seq, grads_done=None):
    t = x.shape[0]
    ts = _tiles(t, seq)
    tm, tb = ts["tm"], ts["tb"]
    saved, ws = [], []
    for l in range(depth):
        w = weights_of(l, x)
        ws.append(w)
        proj, h1 = norm_matmul(x, w["g1"], w["win"], tm, 1152, f32, "in_proj", token=w.get("token"))
        ya = group_a_fwd(proj, w["wa"], w["ga"], seq, tb, "group_a_fwd")
        xbcs = conv_b_fwd(proj, w["ws"], w["bs"], seq, tb, "conv_b_fwd")
        ys, ypre, states = ssd_fwd(xbcs, proj, w["par"], w["gs"], seq, "ssd_fwd")
        cat = jnp.concatenate([ya, ys], axis=1)
        mix, x2 = matmul_postnorm(cat, w["wo"], x, w["g2"], tm, False, "out_proj")
        fp, h2 = norm_matmul(x2, w["g3"], w["wu"], tm, 1024, bf16, "mlp_up")
        o, x3 = matmul_postnorm(fp, w["wd"], x2, w["g4"], tm, True, "mlp_down")
        saved.append(dict(x=x, proj=proj, h1=h1, xbcs=xbcs, ypre=ypre, states=states, cat=cat, mix=mix, x2=x2,
                          fp=fp, h2=h2, o=o))
        x = x3
    sse, dx = loss_head(x, target, tm, "loss_head")
    grads = [None] * depth
    token = None
    for l in reversed(range(depth)):
        s, w = saved[l], ws[l]
        do, dg4, dfp = postnorm_bwd_matmul(s["o"], w["g4"], dx, w["wd"], s["fp"], tm, 1024, bf16, "mlp_down_bwd",
                                           token=token)
        dwd = matmul_tn(s["fp"], do, 512, 1024, True, "mlp_down_dw")
        dx2, dg3 = matmul_prenorm_bwd(dfp, w["wu"], s["x2"], w["g3"], dx, tm, "mlp_up_bwd")
        dwu = matmul_tn(s["h2"], dfp, 512, 1024, False, "mlp_up_dw", col_blocks=True)
        dmix, dg2, dcat = postnorm_bwd_matmul(s["mix"], w["g2"], dx2, w["wo"], None, tm, 1024, f32, "out_proj_bwd")
        dwo = matmul_tn(s["cat"], dmix, 512, 1024, False, "out_proj_dw")
        dpa, dwa, dga = group_a_bwd(s["proj"], dcat, w["wa"], w["ga"], seq, tb, "group_a_bwd")
        dxbcs, dz, ddt, dpar, dgs = ssd_bwd(s["xbcs"], s["proj"], s["ypre"], s["states"], dcat, w["par"], w["gs"], seq,
                                            "ssd_bwd")
        dxbc, dws, dbs = conv_b_bwd(s["proj"], dxbcs, w["ws"], w["bs"], seq, tb, "conv_b_bwd")
        dproj = jnp.concatenate([dpa, dz, dxbc, ddt], axis=1)
        dx, dg1 = matmul_prenorm_bwd(dproj, w["win"], s["x"], w["g1"], dx2, ts["tm_small"], "in_proj_bwd")
        dwin = matmul_tn(s["h1"], dproj, 512, 1152, False, "in_proj_dw")
        grads[l] = dict(win=dwin, wo=dwo, wu=dwu, wd=dwd, wa=dwa, ws=dws, bs=dbs, par=dpar,
                        g1=dg1, ga=dga, gs=dgs, g2=dg2, g3=dg3, g4=dg4)
        if grads_done is not None:
            token = grads_done(l, grads[l], dx)
    return sse, dx, grads


GROUPS = {
    "chips": [(1, 0, 0), (0, 1, 0), (1, 1, 0)],
    "pair": [(0, 0, 1)],
    "all": [(1, 0, 0), (0, 1, 0), (1, 1, 0), (0, 0, 1), (1, 0, 1), (0, 1, 1), (1, 1, 1)],
}


def _group_index(group, x, y, c):
    return {"chips": 2 * x + y, "pair": c, "all": 4 * x + 2 * y + c}[group]


def _chunk_indices(shape, pieces):
    if len(shape) < 3:
        return [()]
    lead = [()]
    for n in shape[:-2]:
        lead = [i + (k,) for i in lead for k in range(n)]
    rows = shape[-2]
    split = max(1, pieces // len(lead))
    while split > 1 and (rows % split or (rows // split) % 16):
        split -= 1
    step = rows // split
    return [i + (pl.ds(s * step, step),) for i in lead for s in range(split)]


def _exchange(arrays, out_shapes, group, src_view, dst_view, view_shape, name, own, pieces=16):
    masks = GROUPS[group]
    na, nm = len(arrays), len(masks)
    cuts = [_chunk_indices(view_shape(a), pieces) for a in range(na)]

    def body(*refs):
        ins, outs = refs[:na], refs[na:2 * na]
        send_sems, recv_sems = refs[2 * na:2 * na + 2]
        local_sems = refs[2 * na + 2] if own else None
        x, y, c = lax.axis_index("x"), lax.axis_index("y"), lax.axis_index("c")
        me = _group_index(group, x, y, c)
        peers = []
        for mx, my, mc in masks:
            px, py, pc = (1 - x if mx else x), (1 - y if my else y), (1 - c if mc else c)
            peers.append(((px, py, pc), _group_index(group, px, py, pc)))

        def part(ref, idx):
            return ref.at[idx] if idx else ref

        if own:
            for a in range(na):
                for idx in cuts[a]:
                    pltpu.make_async_copy(part(src_view(ins[a], a, me), idx), part(dst_view(outs[a], a, me), idx),
                                          local_sems.at[a]).start()
        for a in range(na):
            for j, (dev, pidx) in enumerate(peers):
                for idx in cuts[a]:
                    pltpu.make_async_remote_copy(
                        src_ref=part(src_view(ins[a], a, pidx), idx), dst_ref=part(dst_view(outs[a], a, me), idx),
                        send_sem=send_sems.at[a * nm + j], recv_sem=recv_sems.at[a * nm + j],
                        device_id=dev, device_id_type=MESH).start()
        whole = []
        for a in range(na):
            for j, (dev, pidx) in enumerate(peers):
                whole.append(pltpu.make_async_remote_copy(
                    src_ref=src_view(ins[a], a, pidx), dst_ref=dst_view(outs[a], a, pidx),
                    send_sem=send_sems.at[a * nm + j], recv_sem=recv_sems.at[a * nm + j],
                    device_id=dev, device_id_type=MESH))
        for cp in whole:
            cp.wait_recv()
        for cp in whole:
            cp.wait_send()
        if own:
            for a in range(na):
                pltpu.make_async_copy(src_view(ins[a], a, me), dst_view(outs[a], a, me), local_sems.at[a]).wait()

    hbm = pl.BlockSpec(memory_space=pltpu.HBM)
    sems = [pltpu.SemaphoreType.DMA((na * nm,)), pltpu.SemaphoreType.DMA((na * nm,))]
    return pl.pallas_call(
        body, name=name, in_specs=[hbm] * na, out_specs=[hbm] * na,
        out_shape=[_sds(s, a.dtype) for s, a in zip(out_shapes, arrays)],
        scratch_shapes=sems + ([pltpu.SemaphoreType.DMA((na,))] if own else []))(*arrays)


def all_gather(arrays, group, name, slot_axis=0, own=True):
    n = len(GROUPS[group]) + 1
    shapes = [a.shape[:slot_axis] + (n,) + a.shape[slot_axis:] for a in arrays]
    lead = (slice(None),) * slot_axis
    return _exchange(arrays, shapes, group, lambda r, a, i: r, lambda r, a, i: r.at[lead + (i,)],
                     lambda a: arrays[a].shape, name, own)


def all_to_all(arrays, group, name):
    return _exchange(arrays, [a.shape for a in arrays], group, lambda r, a, i: r.at[i], lambda r, a, i: r.at[i],
                     lambda a: arrays[a].shape[1:], name, False)


def sum_exchanged(x, y, me, out_dtype, name, tb=256):
    n, r, c = x.shape
    tb = min(tb, r)

    def body(me_ref, x_ref, *rest):
        o_ref = rest[-1]
        acc = x_ref[...].astype(f32)
        for y_ref in rest[:-1]:
            acc = acc + y_ref[...].astype(f32)
        o_ref[...] = acc.astype(out_dtype)

    def slot(k):
        return pl.BlockSpec((None, tb, c), lambda i, me_ref: (me_ref[0] ^ k, i, 0))

    return pl.pallas_call(
        body, name=name,
        grid_spec=pltpu.PrefetchScalarGridSpec(
            num_scalar_prefetch=1, grid=(r // tb,), in_specs=[slot(k) for k in range(n)],
            out_specs=pl.BlockSpec((tb, c), lambda i, me_ref: (i, 0))),
        out_shape=_sds((r, c), out_dtype), compiler_params=_cparams(1))(
            jnp.reshape(me, (1,)).astype(jnp.int32), x, *([y] * (n - 1)))


HBM_SPEC = pl.BlockSpec(memory_space=pltpu.HBM)
SEM_SPEC = pl.BlockSpec(memory_space=pltpu.SEMAPHORE)
DATAFLOW = pltpu.SideEffectType.DATAFLOW_SIDE_EFFECTING
N_CHIPS = 4


def _chip_peers(x, y, c):
    out = []
    for mx, my, _ in GROUPS["chips"]:
        px, py = (1 - x if mx else x), (1 - y if my else y)
        out.append(((px, py, c), 2 * px + py))
    return out


def weights_start(shards, name, pieces=4):
    na, nm = len(shards), N_CHIPS - 1
    zones = [lax.empty((N_CHIPS,) + s.shape, s.dtype) for s in shards]

    def body(*refs):
        ins, lands = refs[:na], refs[na:2 * na]
        send_sems, recv_sems, token = refs[2 * na], refs[2 * na + 1], refs[-1]
        x, y, c = lax.axis_index("x"), lax.axis_index("y"), lax.axis_index("c")
        chip = 2 * x + y
        for a in range(na):
            step = shards[a].shape[0] // 2 // pieces
            for j, (dev, _) in enumerate(_chip_peers(x, y, c)):
                for q in range(pieces):
                    rows = pl.ds((c * pieces + q) * step, step)
                    pltpu.make_async_remote_copy(
                        src_ref=ins[a].at[rows], dst_ref=lands[a].at[chip, rows],
                        send_sem=send_sems.at[a * nm + j], recv_sem=recv_sems.at[a * nm + j],
                        device_id=dev, device_id_type=MESH).start()
        token[...] = jnp.zeros_like(token)

    both = list(shards) + zones
    outs = pl.pallas_call(
        body, name=name,
        out_shape=(pltpu.SemaphoreType.DMA((na * nm,)), pltpu.SemaphoreType.DMA((na * nm,)),
                   *[pltpu.HBM(b.shape, b.dtype) for b in both], _sds((8, CH), f32)),
        in_specs=[HBM_SPEC] * (2 * na),
        out_specs=(SEM_SPEC, SEM_SPEC, *[HBM_SPEC] * (2 * na), pl.BlockSpec(memory_space=pltpu.VMEM)),
        input_output_aliases={i: 2 + i for i in range(2 * na)},
        compiler_params=pltpu.CompilerParams(has_side_effects=DATAFLOW))(
            *[pltpu.with_memory_space_constraint(b, pltpu.HBM) for b in both])
    return dict(send=outs[0], recv=outs[1], shards=list(outs[2:2 + na]), zones=list(outs[2 + na:2 + 2 * na]),
                token=outs[-1])


def weights_wait(started, after, name):
    shards, zones = started["shards"], started["zones"]
    na, nm = len(shards), N_CHIPS - 1

    def body(*refs):
        ins, lands = refs[:na], refs[na:2 * na]
        send_sems, recv_sems = refs[2 * na], refs[2 * na + 1]
        x, y, c = lax.axis_index("x"), lax.axis_index("y"), lax.axis_index("c")
        for a in range(na):
            half = shards[a].shape[0] // 2
            rows = pl.ds(c * half, half)
            for j, (dev, pchip) in enumerate(_chip_peers(x, y, c)):
                cp = pltpu.make_async_remote_copy(
                    src_ref=ins[a].at[rows], dst_ref=lands[a].at[pchip, rows],
                    send_sem=send_sems.at[a * nm + j], recv_sem=recv_sems.at[a * nm + j],
                    device_id=dev, device_id_type=MESH)
                cp.wait_send()
                cp.wait_recv()

    both = list(shards) + list(zones)
    outs = pl.pallas_call(
        body, name=name, out_shape=tuple(pltpu.HBM(b.shape, b.dtype) for b in both),
        in_specs=[HBM_SPEC] * (2 * na) + [SEM_SPEC, SEM_SPEC, pl.BlockSpec(memory_space=pl.ANY)],
        out_specs=tuple([HBM_SPEC] * (2 * na)), input_output_aliases={i: i for i in range(2 * na)},
        compiler_params=pltpu.CompilerParams(has_side_effects=DATAFLOW))(*both, started["send"], started["recv"], after)
    return list(outs[:na]), list(outs[na:])


def weights_share(zones, name):
    na, nm = len(zones), N_CHIPS - 1

    def body(*refs):
        lands = refs[na:2 * na]
        send_sems, recv_sems = refs[2 * na:]
        x, y, c = lax.axis_index("x"), lax.axis_index("y"), lax.axis_index("c")
        chip = 2 * x + y
        sibling = (x, y, 1 - c)
        sends = []
        for a in range(na):
            half = zones[a].shape[1] // 2
            for m in range(1, N_CHIPS):
                mine = lands[a].at[chip ^ m, pl.ds(c * half, half)]
                sends.append(pltpu.make_async_remote_copy(
                    src_ref=mine, dst_ref=mine, send_sem=send_sems.at[a * nm + m - 1],
                    recv_sem=recv_sems.at[a * nm + m - 1], device_id=sibling, device_id_type=MESH))
        for cp in sends:
            cp.start()
        for a in range(na):
            half = zones[a].shape[1] // 2
            for m in range(1, N_CHIPS):
                theirs = lands[a].at[chip ^ m, pl.ds((1 - c) * half, half)]
                pltpu.make_async_remote_copy(
                    src_ref=theirs, dst_ref=theirs, send_sem=send_sems.at[a * nm + m - 1],
                    recv_sem=recv_sems.at[a * nm + m - 1], device_id=sibling, device_id_type=MESH).wait_recv()
        for cp in sends:
            cp.wait_send()

    return pl.pallas_call(
        body, name=name, in_specs=[HBM_SPEC] * na, out_specs=[HBM_SPEC] * na,
        out_shape=[_sds(z.shape, z.dtype) for z in zones], input_output_aliases={i: i for i in range(na)},
        scratch_shapes=[pltpu.SemaphoreType.DMA((na * nm,)), pltpu.SemaphoreType.DMA((na * nm,))])(*zones)


def sum_slots(y, out_dtype, name, tb=256):
    n, r, c = y.shape
    tb = min(tb, r)

    def body(y_ref, o_ref):
        acc = y_ref[0].astype(f32)
        for i in range(1, n):
            acc = acc + y_ref[i].astype(f32)
        o_ref[...] = acc.astype(out_dtype)

    return pl.pallas_call(
        body, name=name, grid=(r // tb,),
        in_specs=[pl.BlockSpec((n, tb, c), lambda i: (0, i, 0))], out_specs=pl.BlockSpec((tb, c), lambda i: (i, 0)),
        out_shape=_sds((r, c), out_dtype), compiler_params=_cparams(1))(y)


def adamw(w, g, m, v, name, tb=256):
    r, c = w.shape
    tb = min(tb, r)

    def body(w_ref, g_ref, m_ref, v_ref, d_ref, mo_ref, vo_ref):
        gv = g_ref[...]
        m2 = B1 * m_ref[...] + (1.0 - B1) * gv
        v2 = B2 * v_ref[...] + (1.0 - B2) * (gv * gv)
        m_hat = m2 / (1.0 - B1 ** STEP)
        v_hat = v2 / (1.0 - B2 ** STEP)
        d_ref[...] = -LR * (m_hat / (jnp.sqrt(v_hat) + AEPS) + WD * w_ref[...])
        mo_ref[...] = m2
        vo_ref[...] = v2

    spec = pl.BlockSpec((tb, c), lambda i: (i, 0))
    return pl.pallas_call(
        body, name=name, grid=(r // tb,), in_specs=[spec] * 4, out_specs=[spec] * 3,
        out_shape=[_sds((r, c), f32)] * 3, compiler_params=_cparams(1))(w, g, m, v)


N_CHIPS = 4
SMALL_ROW = 1024
SMALL_PIECES = (("g1", D), ("ga", D), ("gs", D), ("g2", D), ("g3", D), ("g4", D), ("wa", 3 * D), ("ws", 4 * XBC),
                ("bs", XBC), ("par", 3 * CH))
SMALL_LAYER = 17 * SMALL_ROW


def _pack_small(grads):
    rows = []
    for g in grads:
        flat = [g["g1"], g["ga"], g["gs"], g["g2"], g["g3"], g["g4"], g["wa"][:3], g["ws"][:4], g["bs"], g["par"][:3]]
        v = jnp.concatenate([p.reshape(-1) for p in flat])
        rows.append(jnp.pad(v, (0, SMALL_LAYER - v.shape[0])))
    packed = jnp.concatenate(rows).reshape(-1, SMALL_ROW)
    return jnp.pad(packed, ((0, -packed.shape[0] % 8), (0, 0)))


def _unpack_small(packed, depth):
    flat = packed.reshape(-1)[:depth * SMALL_LAYER].reshape(depth, SMALL_LAYER)
    out, off = {}, 0
    for key, size in SMALL_PIECES:
        out[key] = flat[:, off:off + size]
        off += size
    return out


def kernel(x, norm_mix_pre, w_in, conv_a_w, ssm_conv_w, ssm_conv_b, dt_bias, a_log, d_skip, conv_out_norm, ssm_out_norm, w_out, norm_mix_post, norm_mlp_pre, w_up, w_down, norm_mlp_post, loss_target, m_norm_mix_pre, m_w_in, m_conv_a_w, m_ssm_conv_w, m_ssm_conv_b, m_dt_bias, m_a_log, m_d_skip, m_conv_out_norm, m_ssm_out_norm, m_w_out, m_norm_mix_post, m_norm_mlp_pre, m_w_up, m_w_down, m_norm_mlp_post, v_norm_mix_pre, v_w_in, v_conv_a_w, v_ssm_conv_w, v_ssm_conv_b, v_dt_bias, v_a_log, v_d_skip, v_conv_out_norm, v_ssm_out_norm, v_w_out, v_norm_mix_post, v_norm_mlp_pre, v_w_up, v_w_down, v_norm_mlp_post):
    nb, seq, _ = x.shape
    t = nb * seq
    depth = w_in.shape[0]
    half = depth // 2
    ncol = w_in.shape[2]
    chip = 2 * lax.axis_index("x") + lax.axis_index("y")

    taps = [conv_a_w, ssm_conv_w]
    taps_g = all_gather(taps, "chips", "gather_taps", slot_axis=1, own=False)
    wa_g, ws_g = [lax.dynamic_update_index_in_dim(g, s, chip, 1) for g, s in zip(taps_g, taps)]
    wa_full = jnp.transpose(wa_g, (0, 2, 1, 3)).reshape(depth, 3, D)
    ws_full = jnp.transpose(ws_g, (0, 2, 1, 3)).reshape(depth, 4, XBC)
    lane_pad = lambda a: jnp.pad(a, ((0, 0), (0, CH - a.shape[1])))
    par = jnp.stack([lane_pad(dt_bias), lane_pad(a_log), lane_pad(d_skip)], axis=1)
    par = jnp.pad(par, ((0, 0), (0, 5), (0, 0)))

    def start(l):
        return weights_start([w_in[l].astype(bf16), w_out[l].astype(bf16), w_up[l].astype(bf16),
                              w_down[l].astype(bf16)], f"weights_start_{l}")

    travelling = {0: start(0)}

    def weights_of(l, x_in):
        shards, zones = weights_wait(travelling.pop(l), x_in, f"weights_wait_{l}")
        zones = weights_share(zones, "weights_share")
        win_z, wo_z, wu_z, wd_z = [lax.dynamic_update_index_in_dim(z, s, chip, 0) for z, s in zip(zones, shards)]
        token = None
        if l + 1 < depth:
            travelling[l + 1] = start(l + 1)
            token = travelling[l + 1]["token"]
        win_full = jnp.pad(jnp.transpose(win_z, (1, 0, 2)).reshape(D, N_CHIPS * ncol),
                           ((0, 0), (0, PROJ - N_CHIPS * ncol)))
        return dict(win=win_full, wo=wo_z.reshape(2 * D, D), wu=wu_z, wd=wd_z.reshape(DFF, D),
                    wa=jnp.pad(wa_full[l], ((0, 5), (0, 0))), ws=jnp.pad(ws_full[l], ((0, 4), (0, 0))),
                    bs=ssm_conv_b[l][None], par=par[l], g1=norm_mix_pre[l][None], ga=conv_out_norm[l][None],
                    gs=ssm_out_norm[l][None], g2=norm_mix_post[l][None], g3=norm_mlp_pre[l][None],
                    g4=norm_mlp_post[l][None], token=token)

    sse, dx, grads = local_step(x.reshape(t, D), loss_target.reshape(t, D), depth, weights_of, seq)
    loss = lax.psum(0.5 / D * sse[0, 0], ("x", "y", "c"))

    def slots(per_layer, to_chip_major):
        g = jnp.stack([to_chip_major(a) for a in per_layer])
        g = g.reshape((2, half) + g.shape[1:])
        return jnp.transpose(g, (0, 2, 1, 3, 4))

    big = [
        slots([g["win"] for g in grads],
              lambda a: jnp.transpose(a[:, :N_CHIPS * ncol].reshape(D, N_CHIPS, ncol), (1, 0, 2))),
        slots([g["wo"] for g in grads], lambda a: a.reshape(N_CHIPS, 2 * D // N_CHIPS, D)),
        slots([g["wu"] for g in grads], lambda a: a),
        slots([g["wd"] for g in grads], lambda a: a.reshape(N_CHIPS, DFF // N_CHIPS, D)),
    ]
    core = lax.axis_index("c")
    rows3 = lambda a: a.reshape(a.shape[0], -1, a.shape[-1])
    pair = all_to_all(big, "pair", "grads_to_pair")
    pair_sum = [sum_exchanged(rows3(b), rows3(p), core, bf16, "pair_sum").reshape(b.shape[1:])
                for b, p in zip(big, pair)]
    chips = all_to_all(pair_sum, "chips", "grads_to_chips")
    half_sum = [sum_exchanged(rows3(s), rows3(q), chip, f32, "chip_sum").reshape(s.shape[1:])
                for s, q in zip(pair_sum, chips)]
    full = all_gather(half_sum, "pair", "grads_from_pair", own=False)
    full = [lax.dynamic_update_index_in_dim(f, h, core, 0) for f, h in zip(full, half_sum)]
    g_win, g_wo, g_wu, g_wd = [f.reshape((depth,) + f.shape[2:]) for f in full]

    small_all = all_gather([_pack_small(grads)], "all", "gather_small")[0]
    small = _unpack_small(sum_slots(small_all, f32, "small_sum", tb=8), depth)
    wa_cols, ws_cols = conv_a_w.shape[2], ssm_conv_w.shape[2]
    par_g = small["par"].reshape(depth, 3, CH)
    g_small = dict(
        norm_mix_pre=small["g1"], conv_out_norm=small["ga"], ssm_out_norm=small["gs"], norm_mix_post=small["g2"],
        norm_mlp_pre=small["g3"], norm_mlp_post=small["g4"], ssm_conv_b=small["bs"],
        conv_a_w=lax.dynamic_slice_in_dim(small["wa"].reshape(depth, 3, D), chip * wa_cols, wa_cols, axis=2),
        ssm_conv_w=lax.dynamic_slice_in_dim(small["ws"].reshape(depth, 4, XBC), chip * ws_cols, ws_cols, axis=2),
        dt_bias=par_g[:, 0, :NH], a_log=par_g[:, 1, :NH], d_skip=par_g[:, 2, :NH])

    given = dict(norm_mix_pre=(norm_mix_pre, m_norm_mix_pre, v_norm_mix_pre), w_in=(w_in, m_w_in, v_w_in),
                 conv_a_w=(conv_a_w, m_conv_a_w, v_conv_a_w), ssm_conv_w=(ssm_conv_w, m_ssm_conv_w, v_ssm_conv_w),
                 ssm_conv_b=(ssm_conv_b, m_ssm_conv_b, v_ssm_conv_b), dt_bias=(dt_bias, m_dt_bias, v_dt_bias),
                 a_log=(a_log, m_a_log, v_a_log), d_skip=(d_skip, m_d_skip, v_d_skip),
                 conv_out_norm=(conv_out_norm, m_conv_out_norm, v_conv_out_norm),
                 ssm_out_norm=(ssm_out_norm, m_ssm_out_norm, v_ssm_out_norm), w_out=(w_out, m_w_out, v_w_out),
                 norm_mix_post=(norm_mix_post, m_norm_mix_post, v_norm_mix_post),
                 norm_mlp_pre=(norm_mlp_pre, m_norm_mlp_pre, v_norm_mlp_pre), w_up=(w_up, m_w_up, v_w_up),
                 w_down=(w_down, m_w_down, v_w_down), norm_mlp_post=(norm_mlp_post, m_norm_mlp_post, v_norm_mlp_post))
    grad = dict(g_small, w_in=g_win, w_out=g_wo, w_up=g_wu, w_down=g_wd)
    order = ["norm_mix_pre", "w_in", "conv_a_w", "ssm_conv_w", "ssm_conv_b", "dt_bias", "a_log", "d_skip",
             "conv_out_norm", "ssm_out_norm", "w_out", "norm_mix_post", "norm_mlp_pre", "w_up", "w_down",
             "norm_mlp_post"]
    g_out, d_out, m_out, v_out = [], [], [], []
    for n in order:
        wv, mv, vv = given[n]
        gv = grad[n].reshape(wv.shape)
        two_d = lambda a: a.reshape(-1, a.shape[-1])
        dlt, m2, v2 = adamw(two_d(wv), two_d(gv), two_d(mv), two_d(vv), "adamw")
        g_out.append(gv)
        d_out.append(dlt.reshape(wv.shape))
        m_out.append(m2.reshape(wv.shape))
        v_out.append(v2.reshape(wv.shape))
    return (loss, dx.reshape(nb, seq, D), *g_out, *d_out, *m_out, *v_out)
```

```python
import functools

import jax
import jax.numpy as jnp
from jax import lax
from jax.experimental import pallas as pl
from jax.experimental.pallas import tpu as pltpu

f32, bf16 = jnp.float32, jnp.bfloat16

D = 1024
NH, HP = 16, 64
NG, NS = 2, 128
CH = 128
XBC = D + 2 * NG * NS
DFF = 4 * D
IN_COLS = 3 * D + D + XBC + NH
PROJ = 5760
COL_Z, COL_XBC, COL_DT = 3 * D, 4 * D, 4 * D + XBC
EPS = 1e-6
HALO = 8
VMEM_LIMIT = 56 * 2**20
MESH = pl.DeviceIdType.MESH

LR, B1, B2, AEPS, WD, STEP = 0.001, 0.9, 0.999, 1e-08, 0.01, 10


def _cparams(n_axes):
    return pltpu.CompilerParams(dimension_semantics=("arbitrary",) * n_axes, vmem_limit_bytes=VMEM_LIMIT)


def _sds(shape, dtype):
    return jax.ShapeDtypeStruct(tuple(shape), dtype)


def _token_spec(token):
    return [] if token is None else [pl.BlockSpec(memory_space=pl.ANY)]


def _token_arg(token):
    return [] if token is None else [token]


def _rms_fwd(x, g):
    r = lax.rsqrt(jnp.mean(x * x, axis=-1, keepdims=True) + EPS)
    return x * r * g


def _rms_bwd(x, g, dy):
    r = lax.rsqrt(jnp.mean(x * x, axis=-1, keepdims=True) + EPS)
    xh = x * r
    gdy = dy * g
    dx = r * (gdy - xh * jnp.mean(xh * gdy, axis=-1, keepdims=True))
    return dx, dy * xh


def _accum(ref, part, first):
    @pl.when(first)
    def _():
        ref[...] = part

    @pl.when(jnp.logical_not(first))
    def _():
        ref[...] += part


def _dot_nt(a, b):
    return lax.dot_general(a, b, (((1,), (1,)), ((), ())), preferred_element_type=f32)


def _dot_tn(a, b):
    return lax.dot_general(a, b, (((0,), (0,)), ((), ())), preferred_element_type=f32)


def _dot(a, b):
    return jnp.dot(a, b, preferred_element_type=f32)


def _split_dot(x, e_bf, n_split, nt=False):
    acc = None
    rem = x
    for s in range(n_split):
        hi = rem.astype(bf16)
        term = _dot_nt(hi, e_bf) if nt else _dot(hi, e_bf)
        acc = term if acc is None else acc + term
        if s + 1 < n_split:
            rem = rem - hi.astype(f32)
    return acc


def _sigmoid(x):
    return 1.0 / (1.0 + jnp.exp(-x))


def norm_matmul(x, g, w, tm, tn, out_dtype, name, token=None):
    t = x.shape[0]
    if w.ndim == 3:
        assert w.shape[2] == tn
        n = w.shape[0] * tn
        w_spec = pl.BlockSpec((None, D, tn), lambda i, j: (j, 0, 0))
    else:
        n = w.shape[1]
        w_spec = pl.BlockSpec((D, tn), lambda i, j: (0, j))

    def body(x_ref, g_ref, w_ref, *rest):
        o_ref, h_ref = rest[-2:]

        @pl.when(pl.program_id(1) == 0)
        def _():
            h_ref[...] = _rms_fwd(x_ref[...], g_ref[...]).astype(bf16)

        o_ref[...] = _dot(h_ref[...], w_ref[...]).astype(out_dtype)

    return pl.pallas_call(
        body, name=name, grid=(t // tm, n // tn),
        in_specs=[pl.BlockSpec((tm, D), lambda i, j: (i, 0)), pl.BlockSpec((1, D), lambda i, j: (0, 0)), w_spec]
        + _token_spec(token),
        out_specs=[pl.BlockSpec((tm, tn), lambda i, j: (i, j)), pl.BlockSpec((tm, D), lambda i, j: (i, 0))],
        out_shape=[_sds((t, n), out_dtype), _sds((t, D), bf16)],
        compiler_params=_cparams(2))(x, g, w, *_token_arg(token))


def matmul_postnorm(a, w, xres, g, tm, relu2, name):
    t, k = a.shape

    def body(a_ref, w_ref, xr_ref, g_ref, y_ref, xo_ref):
        av = a_ref[...]
        if relu2:
            af = jnp.maximum(av.astype(f32), 0.0)
            av = (af * af).astype(bf16)
        y = _dot(av, w_ref[...])
        y_ref[...] = y
        xo_ref[...] = xr_ref[...] + _rms_fwd(y, g_ref[...])

    return pl.pallas_call(
        body, name=name, grid=(t // tm,),
        in_specs=[pl.BlockSpec((tm, k), lambda i: (i, 0)), pl.BlockSpec((k, D), lambda i: (0, 0)),
                  pl.BlockSpec((tm, D), lambda i: (i, 0)), pl.BlockSpec((1, D), lambda i: (0, 0))],
        out_specs=[pl.BlockSpec((tm, D), lambda i: (i, 0)), pl.BlockSpec((tm, D), lambda i: (i, 0))],
        out_shape=[_sds((t, D), f32), _sds((t, D), f32)],
        compiler_params=_cparams(1))(a, w, xres, g)


def postnorm_bwd_matmul(y, g, dxo, w, fp, tm, tn, out_dtype, name, token=None):
    t, n = y.shape[0], w.shape[0]
    relu = fp is not None

    def body(*refs):
        y_ref, g_ref, dxo_ref, w_ref = refs[:4]
        fp_ref = refs[4] if relu else None
        dy_ref, dg_ref, da_ref = refs[-3:]
        i, j = pl.program_id(0), pl.program_id(1)

        @pl.when(j == 0)
        def _():
            dx, dgc = _rms_bwd(y_ref[...], g_ref[...], dxo_ref[...])
            dy_ref[...] = dx.astype(bf16)
            _accum(dg_ref, jnp.sum(dgc, axis=0, keepdims=True), i == 0)

        da = _dot_nt(dy_ref[...], w_ref[...])
        if relu:
            da = da * (2.0 * jnp.maximum(fp_ref[...].astype(f32), 0.0))
        da_ref[...] = da.astype(out_dtype)

    in_specs = [pl.BlockSpec((tm, D), lambda i, j: (i, 0)), pl.BlockSpec((1, D), lambda i, j: (0, 0)),
                pl.BlockSpec((tm, D), lambda i, j: (i, 0)), pl.BlockSpec((tn, D), lambda i, j: (j, 0))]
    args = [y, g, dxo, w]
    if relu:
        in_specs.append(pl.BlockSpec((tm, tn), lambda i, j: (i, j)))
        args.append(fp)
    in_specs += _token_spec(token)
    args += _token_arg(token)
    return pl.pallas_call(
        body, name=name, grid=(t // tm, n // tn), in_specs=in_specs,
        out_specs=[pl.BlockSpec((tm, D), lambda i, j: (i, 0)), pl.BlockSpec((1, D), lambda i, j: (0, 0)),
                   pl.BlockSpec((tm, tn), lambda i, j: (i, j))],
        out_shape=[_sds((t, D), bf16), _sds((1, D), f32), _sds((t, n), out_dtype)],
        compiler_params=_cparams(2))(*args)


def matmul_prenorm_bwd(da, w, x, g, dxo, tm, name):
    t, k = da.shape
    blocked = w.ndim == 3

    def body(da_ref, w_ref, x_ref, g_ref, dxo_ref, dx_ref, dg_ref):
        if blocked:
            kc = w.shape[2]
            dh = _dot_nt(da_ref[:, 0:kc], w_ref[0])
            for q in range(1, w.shape[0]):
                dh = dh + _dot_nt(da_ref[:, q * kc:(q + 1) * kc], w_ref[q])
        else:
            dh = _dot_nt(da_ref[...], w_ref[...])
        dxn, dgc = _rms_bwd(x_ref[...], g_ref[...], dh)
        dx_ref[...] = dxo_ref[...] + dxn
        _accum(dg_ref, jnp.sum(dgc, axis=0, keepdims=True), pl.program_id(0) == 0)

    w_spec = pl.BlockSpec(w.shape, (lambda i: (0, 0, 0)) if blocked else (lambda i: (0, 0)))
    return pl.pallas_call(
        body, name=name, grid=(t // tm,),
        in_specs=[pl.BlockSpec((tm, k), lambda i: (i, 0)), w_spec,
                  pl.BlockSpec((tm, D), lambda i: (i, 0)), pl.BlockSpec((1, D), lambda i: (0, 0)),
                  pl.BlockSpec((tm, D), lambda i: (i, 0))],
        out_specs=[pl.BlockSpec((tm, D), lambda i: (i, 0)), pl.BlockSpec((1, D), lambda i: (0, 0))],
        out_shape=[_sds((t, D), f32), _sds((1, D), f32)],
        compiler_params=_cparams(1))(da, w, x, g, dxo)


def matmul_tn(a, b, tm, tn, relu2, name, col_blocks=False):
    t, m = a.shape
    n = b.shape[1]
    if col_blocks:
        out_spec, out_shape = pl.BlockSpec((None, tm, tn), lambda i, j: (j, i, 0)), _sds((n // tn, m, tn), f32)
    else:
        out_spec, out_shape = pl.BlockSpec((tm, tn), lambda i, j: (i, j)), _sds((m, n), f32)

    def body(a_ref, b_ref, o_ref, at_ref):
        @pl.when(pl.program_id(1) == 0)
        def _():
            av = a_ref[...]
            if relu2:
                af = jnp.maximum(av.astype(f32), 0.0)
                av = (af * af).astype(bf16)
            at_ref[...] = av.T

        o_ref[...] = _dot(at_ref[...], b_ref[...])

    return pl.pallas_call(
        body, name=name, grid=(m // tm, n // tn),
        in_specs=[pl.BlockSpec((t, tm), lambda i, j: (0, i)), pl.BlockSpec((t, tn), lambda i, j: (0, j))],
        out_specs=out_spec, out_shape=out_shape,
        scratch_shapes=[pltpu.VMEM((tm, t), bf16)],
        compiler_params=_cparams(2))(a, b)


def _halo_prev(tb, col):
    return lambda i: (jnp.maximum(i * (tb // HALO) - 1, 0), col)


def _halo_next(tb, col, t):
    return lambda i: (jnp.minimum((i + 1) * (tb // HALO), t // HALO - 1), col)


def group_a_fwd(proj, wa, g, seq, tb, name):
    t = proj.shape[0]
    bps = seq // tb

    def body(xa_ref, ca_ref, ba_ref, xah_ref, cah_ref, wa_ref, g_ref, o_ref, u_scr):
        first = (pl.program_id(0) % bps) == 0
        u = ca_ref[...] * xa_ref[...]
        u_scr[0:HALO, :] = jnp.where(first, 0.0, cah_ref[...] * xah_ref[...])
        u_scr[HALO:HALO + tb, :] = u
        w = wa_ref[...]
        cv = w[2:3] * u + w[1:2] * u_scr[pl.ds(HALO - 1, tb), :] + w[0:1] * u_scr[pl.ds(HALO - 2, tb), :]
        o_ref[...] = _rms_fwd(ba_ref[...] * cv, g_ref[...]).astype(bf16)

    blk = lambda c: pl.BlockSpec((tb, D), lambda i: (i, c))
    return pl.pallas_call(
        body, name=name, grid=(t // tb,),
        in_specs=[blk(0), blk(1), blk(2),
                  pl.BlockSpec((HALO, D), _halo_prev(tb, 0)), pl.BlockSpec((HALO, D), _halo_prev(tb, 1)),
                  pl.BlockSpec((8, D), lambda i: (0, 0)), pl.BlockSpec((1, D), lambda i: (0, 0))],
        out_specs=pl.BlockSpec((tb, D), lambda i: (i, 0)),
        out_shape=_sds((t, D), bf16),
        scratch_shapes=[pltpu.VMEM((tb + HALO, D), f32)],
        compiler_params=_cparams(1))(proj, proj, proj, proj, proj, wa, g)


def group_a_bwd(proj, dcat, wa, g, seq, tb, name):
    t = proj.shape[0]
    bps = seq // tb

    def body(xa_ref, ca_ref, ba_ref, dy_ref, xap_ref, cap_ref, xan_ref, can_ref, ban_ref, dyn_ref, wa_ref, g_ref,
             dp_ref, dwa_ref, dg_ref, u_scr, d_scr):
        i = pl.program_id(0)
        first = (i % bps) == 0
        last = (i % bps) == bps - 1
        w = wa_ref[...]
        gv = g_ref[...]
        xa, ca, ba = xa_ref[...], ca_ref[...], ba_ref[...]
        u_scr[0:HALO, :] = jnp.where(first, 0.0, cap_ref[...] * xap_ref[...])
        u_scr[HALO:HALO + tb, :] = ca * xa
        u_scr[HALO + tb:2 * HALO + tb, :] = can_ref[...] * xan_ref[...]

        def conv(start, rows):
            return (w[2:3] * u_scr[pl.ds(start, rows), :] + w[1:2] * u_scr[pl.ds(start - 1, rows), :]
                    + w[0:1] * u_scr[pl.ds(start - 2, rows), :])

        cv = conv(HALO, tb)
        dya, dgc = _rms_bwd(ba * cv, gv, dy_ref[...])
        ban = ban_ref[...]
        dyan, _ = _rms_bwd(ban * conv(HALO + tb, HALO), gv, dyn_ref[...])
        dcv = dya * ba
        d_scr[0:tb, :] = dcv
        d_scr[tb:tb + HALO, :] = jnp.where(last, 0.0, dyan * ban)
        du = w[2:3] * dcv + w[1:2] * d_scr[pl.ds(1, tb), :] + w[0:1] * d_scr[pl.ds(2, tb), :]
        dp_ref[:, 0:D] = (du * ca).astype(bf16)
        dp_ref[:, D:2 * D] = (du * xa).astype(bf16)
        dp_ref[:, 2 * D:3 * D] = (dya * cv).astype(bf16)
        row = lax.broadcasted_iota(jnp.int32, (8, D), 0)
        dw = jnp.zeros((8, D), f32)
        for k in range(3):
            s = jnp.sum(dcv * u_scr[pl.ds(HALO - 2 + k, tb), :], axis=0, keepdims=True)
            dw = jnp.where(row == k, s, dw)
        _accum(dwa_ref, dw, i == 0)
        _accum(dg_ref, jnp.sum(dgc, axis=0, keepdims=True), i == 0)

    blk = lambda c: pl.BlockSpec((tb, D), lambda i: (i, c))
    prv = lambda c: pl.BlockSpec((HALO, D), _halo_prev(tb, c))
    nxt = lambda c: pl.BlockSpec((HALO, D), _halo_next(tb, c, t))
    return pl.pallas_call(
        body, name=name, grid=(t // tb,),
        in_specs=[blk(0), blk(1), blk(2), blk(0), prv(0), prv(1), nxt(0), nxt(1), nxt(2), nxt(0),
                  pl.BlockSpec((8, D), lambda i: (0, 0)), pl.BlockSpec((1, D), lambda i: (0, 0))],
        out_specs=[pl.BlockSpec((tb, 3 * D), lambda i: (i, 0)), pl.BlockSpec((8, D), lambda i: (0, 0)),
                   pl.BlockSpec((1, D), lambda i: (0, 0))],
        out_shape=[_sds((t, 3 * D), bf16), _sds((8, D), f32), _sds((1, D), f32)],
        scratch_shapes=[pltpu.VMEM((tb + 2 * HALO, D), f32), pltpu.VMEM((tb + HALO, D), f32)],
        compiler_params=_cparams(1))(proj, proj, proj, dcat, proj, proj, proj, proj, proj, dcat, wa, g)


CB = 512
XBC_BLK0 = COL_XBC // CB


def conv_b_fwd(proj, ws, bs, seq, tb, name):
    t = proj.shape[0]
    bps = seq // tb

    def body(x_ref, xp_ref, w_ref, b_ref, o_ref, x_scr):
        first = (pl.program_id(1) % bps) == 0
        x = x_ref[...]
        x_scr[0:HALO, :] = jnp.where(first, 0.0, xp_ref[...])
        x_scr[HALO:HALO + tb, :] = x
        w = w_ref[...]
        xc = w[3:4] * x + b_ref[...]
        for k in range(3):
            xc = xc + w[k:k + 1] * x_scr[pl.ds(HALO - 3 + k, tb), :]
        o_ref[...] = xc * _sigmoid(xc)

    return pl.pallas_call(
        body, name=name, grid=(XBC // CB, t // tb),
        in_specs=[pl.BlockSpec((tb, CB), lambda j, i: (i, XBC_BLK0 + j)),
                  pl.BlockSpec((HALO, CB), lambda j, i: (jnp.maximum(i * (tb // HALO) - 1, 0), XBC_BLK0 + j)),
                  pl.BlockSpec((8, CB), lambda j, i: (0, j)), pl.BlockSpec((1, CB), lambda j, i: (0, j))],
        out_specs=pl.BlockSpec((tb, CB), lambda j, i: (i, j)),
        out_shape=_sds((t, XBC), f32),
        scratch_shapes=[pltpu.VMEM((tb + HALO, CB), f32)],
        compiler_params=_cparams(2))(proj, proj, ws, bs)


def conv_b_bwd(proj, dxs, ws, bs, seq, tb, name):
    t = proj.shape[0]
    bps = seq // tb

    def body(x_ref, xp_ref, xn_ref, d_ref, dn_ref, w_ref, b_ref, dx_ref, dw_ref, db_ref, x_scr, d_scr):
        i = pl.program_id(1)
        first = (i % bps) == 0
        last = (i % bps) == bps - 1
        w = w_ref[...]
        bias = b_ref[...]
        x_scr[0:HALO, :] = jnp.where(first, 0.0, xp_ref[...])
        x_scr[HALO:HALO + tb, :] = x_ref[...]
        x_scr[HALO + tb:2 * HALO + tb, :] = xn_ref[...]

        def dsilu_at(start, rows, d):
            xc = bias + w[3:4] * x_scr[pl.ds(start, rows), :]
            for k in range(3):
                xc = xc + w[k:k + 1] * x_scr[pl.ds(start - 3 + k, rows), :]
            sg = _sigmoid(xc)
            return d * (sg * (1.0 + xc * (1.0 - sg)))

        dxc = dsilu_at(HALO, tb, d_ref[...])
        d_scr[0:tb, :] = dxc
        d_scr[tb:tb + HALO, :] = jnp.where(last, 0.0, dsilu_at(HALO + tb, HALO, dn_ref[...]))
        dx = w[3:4] * dxc
        for k in range(3):
            dx = dx + w[k:k + 1] * d_scr[pl.ds(3 - k, tb), :]
        dx_ref[...] = dx.astype(bf16)
        row = lax.broadcasted_iota(jnp.int32, (8, CB), 0)
        dw = jnp.zeros((8, CB), f32)
        for k in range(4):
            s = jnp.sum(dxc * x_scr[pl.ds(HALO - 3 + k, tb), :], axis=0, keepdims=True)
            dw = jnp.where(row == k, s, dw)
        _accum(dw_ref, dw, i == 0)
        _accum(db_ref, jnp.sum(dxc, axis=0, keepdims=True), i == 0)

    nh = t // HALO
    return pl.pallas_call(
        body, name=name, grid=(XBC // CB, t // tb),
        in_specs=[pl.BlockSpec((tb, CB), lambda j, i: (i, XBC_BLK0 + j)),
                  pl.BlockSpec((HALO, CB), lambda j, i: (jnp.maximum(i * (tb // HALO) - 1, 0), XBC_BLK0 + j)),
                  pl.BlockSpec((HALO, CB), lambda j, i: (jnp.minimum((i + 1) * (tb // HALO), nh - 1), XBC_BLK0 + j)),
                  pl.BlockSpec((tb, CB), lambda j, i: (i, j)),
                  pl.BlockSpec((HALO, CB), lambda j, i: (jnp.minimum((i + 1) * (tb // HALO), nh - 1), j)),
                  pl.BlockSpec((8, CB), lambda j, i: (0, j)), pl.BlockSpec((1, CB), lambda j, i: (0, j))],
        out_specs=[pl.BlockSpec((tb, CB), lambda j, i: (i, j)), pl.BlockSpec((8, CB), lambda j, i: (0, j)),
                   pl.BlockSpec((1, CB), lambda j, i: (0, j))],
        out_shape=[_sds((t, XBC), bf16), _sds((8, XBC), f32), _sds((1, XBC), f32)],
        scratch_shapes=[pltpu.VMEM((tb + 2 * HALO, CB), f32), pltpu.VMEM((tb + HALO, CB), f32)],
        compiler_params=_cparams(2))(proj, proj, proj, dxs, dxs, ws, bs)


GW = D // NG


def _ssd_consts():
    head_of_lane = jnp.arange(D) // HP
    expand = (jnp.arange(CH)[:, None] == head_of_lane[None, :]).astype(bf16)
    tri = (jnp.arange(CH)[:, None] >= jnp.arange(CH)[None, :]).astype(f32)
    return expand, tri


def _ssd_common(par_ref, dtr_ref, e_ref, tri_ref):
    par = par_ref[...]
    dtb, alog, dsk = par[0:1], par[1:2], par[2:3]
    lane = lax.broadcasted_iota(jnp.int32, (CH, CH), 1)
    a = -jnp.exp(alog)
    dtr = dtr_ref[...] + dtb
    sp = jnp.maximum(dtr, 0.0) + jnp.log(1.0 + jnp.exp(-jnp.abs(dtr)))
    dt = jnp.where(lane < NH, sp, 0.0)
    cs = jnp.dot(tri_ref[...], dt * a, precision=lax.Precision.HIGHEST, preferred_element_type=f32)
    cs_last = cs[CH - 1:CH, :]
    dte = jnp.exp(cs_last - cs)
    ecs = jnp.exp(cs)
    ecl = jnp.exp(cs_last)
    e = e_ref[...]
    row8 = lax.broadcasted_iota(jnp.int32, (8, CH), 0)
    r8 = _split_dot(jnp.where(row8 == 0, ecl, jnp.where(row8 == 1, dsk, 0.0)), e, 3)
    return dict(a=a, dtr=dtr, dt=dt, cs=cs, cst=cs.T, dte=dte, ecs=ecs, ecl=ecl, e=e, lane=lane,
                dt_x=_split_dot(dt, e, 3), dte_x=_split_dot(dte, e, 3), ecs_x=_split_dot(ecs, e, 3),
                ecl_x=r8[0:1], dsk_x=r8[1:2])


def _decay_matrix(c, h):
    li = lax.broadcasted_iota(jnp.int32, (CH, CH), 0)
    seg = c["cs"][:, h:h + 1] - c["cst"][h:h + 1, :]
    return jnp.exp(jnp.where(li >= c["lane"], seg, -jnp.inf))


def _gate_norm_fwd(y, z, gs):
    zg = z * _sigmoid(z)
    yg = y * zg
    return jnp.concatenate([_rms_fwd(yg[:, k * GW:(k + 1) * GW], gs[:, k * GW:(k + 1) * GW]) for k in range(NG)], axis=1)


def ssd_fwd(xbcs, proj, par, gs, seq, name):
    t = xbcs.shape[0]
    nc = seq // CH
    expand, tri = _ssd_consts()

    def body(xs_ref, b_ref, c_ref, dtr_ref, z_ref, par_ref, e_ref, tri_ref, gs_ref, yn_ref, y_ref, st_ref, p_scr, yd_scr):
        @pl.when(pl.program_id(0) % nc == 0)
        def _():
            p_scr[...] = jnp.zeros_like(p_scr)

        c = _ssd_common(par_ref, dtr_ref, e_ref, tri_ref)
        xs = xs_ref[...]
        xdt = xs * c["dt_x"]
        xdt_b = xdt.astype(bf16)
        xdte_b = (xdt * c["dte_x"]).astype(bf16)
        p = p_scr[...]
        st_ref[0] = p
        p_b = p.astype(bf16)
        lo = c["lane"] < HP
        for g in range(NG):
            bg = b_ref[:, g * NS:(g + 1) * NS].astype(bf16)
            cg = c_ref[:, g * NS:(g + 1) * NS].astype(bf16)
            gmat = _dot_nt(cg, bg)
            for q in range(GW // CH):
                col = g * GW + q * CH
                xp = xdt_b[:, col:col + CH]
                h0 = col // HP
                m0 = (gmat * _decay_matrix(c, h0)).astype(bf16)
                m1 = (gmat * _decay_matrix(c, h0 + 1)).astype(bf16)
                yd_scr[:, col:col + CH] = (_dot(m0, jnp.where(lo, xp, jnp.zeros_like(xp)))
                                           + _dot(m1, jnp.where(lo, jnp.zeros_like(xp), xp)))
            gsl = slice(g * GW, (g + 1) * GW)
            yoff = _dot(cg, p_b[:, gsl]) * c["ecs_x"][:, gsl]
            yd_scr[:, gsl] = yd_scr[:, gsl] + yoff
            p_scr[:, gsl] = p[:, gsl] * c["ecl_x"][:, gsl] + _dot_tn(bg, xdte_b[:, gsl])
        y = yd_scr[...] + c["dsk_x"] * xs
        y_ref[...] = y
        yn_ref[...] = _gate_norm_fwd(y, z_ref[...], gs_ref[...]).astype(bf16)

    nb = t // CH
    return pl.pallas_call(
        body, name=name, grid=(nb,),
        in_specs=[pl.BlockSpec((CH, D), lambda i: (i, 0)),
                  pl.BlockSpec((CH, NG * NS), lambda i: (i, D // (NG * NS))),
                  pl.BlockSpec((CH, NG * NS), lambda i: (i, D // (NG * NS) + 1)),
                  pl.BlockSpec((CH, CH), lambda i: (i, COL_DT // CH)),
                  pl.BlockSpec((CH, D), lambda i: (i, COL_Z // D)),
                  pl.BlockSpec((8, CH), lambda i: (0, 0)), pl.BlockSpec((CH, D), lambda i: (0, 0)),
                  pl.BlockSpec((CH, CH), lambda i: (0, 0)), pl.BlockSpec((1, D), lambda i: (0, 0))],
        out_specs=[pl.BlockSpec((CH, D), lambda i: (i, 0)), pl.BlockSpec((CH, D), lambda i: (i, 0)),
                   pl.BlockSpec((1, NS, D), lambda i: (i, 0, 0))],
        out_shape=[_sds((t, D), bf16), _sds((t, D), f32), _sds((nb, NS, D), f32)],
        scratch_shapes=[pltpu.VMEM((NS, D), f32), pltpu.VMEM((CH, D), f32)],
        compiler_params=_cparams(1))(xbcs, xbcs, xbcs, proj, proj, par, expand, tri, gs)


def ssd_bwd(xbcs, proj, ypre, states, dcat, par, gs, seq, name):
    t = xbcs.shape[0]
    nc = seq // CH
    expand, tri = _ssd_consts()

    def body(xs_ref, b_ref, c_ref, dtr_ref, z_ref, y_ref, st_ref, dyn_ref, par_ref, e_ref, tri_ref, gs_ref,
             dx_ref, dz_ref, ddt_ref, dpar_ref, dgs_ref, dp_scr, dxdt_scr):
        i = pl.program_id(0)

        @pl.when(i % nc == 0)
        def _():
            dp_scr[...] = jnp.zeros_like(dp_scr)

        c = _ssd_common(par_ref, dtr_ref, e_ref, tri_ref)
        e = c["e"]
        lane = c["lane"]
        sub = lax.broadcasted_iota(jnp.int32, (CH, CH), 0)
        xs = xs_ref[...]
        xdt = xs * c["dt_x"]
        xdt_b = xdt.astype(bf16)
        xdte_b = (xdt * c["dte_x"]).astype(bf16)
        p = st_ref[0]
        p_b = p.astype(bf16)
        dpn = dp_scr[...]
        dpn_b = dpn.astype(bf16)

        y, z, gs_v = y_ref[...], z_ref[...], gs_ref[...]
        zs = _sigmoid(z)
        zg = z * zs
        yg = y * zg
        parts, gparts = [], []
        for k in range(NG):
            sl = slice(k * GW, (k + 1) * GW)
            dxk, dgk = _rms_bwd(yg[:, sl], gs_v[:, sl], dyn_ref[:, sl])
            parts.append(dxk)
            gparts.append(dgk)
        dyg = jnp.concatenate(parts, axis=1)
        dgs_rows = jnp.concatenate(gparts, axis=1)
        dy = dyg * zg
        dz_ref[...] = (dyg * y * (zs * (1.0 + z * (1.0 - zs)))).astype(bf16)
        dy_b = dy.astype(bf16)
        dq_b = (dy * c["ecs_x"]).astype(bf16)

        lo = lane < HP
        dcs = jnp.zeros((CH, CH), f32)
        dcst = jnp.zeros((CH, CH), f32)
        for g in range(NG):
            gsl = slice(g * GW, (g + 1) * GW)
            bg = b_ref[:, g * NS:(g + 1) * NS].astype(bf16)
            cg = c_ref[:, g * NS:(g + 1) * NS].astype(bf16)
            gmat = _dot_nt(cg, bg)
            dgm = jnp.zeros((CH, CH), f32)
            for q in range(GW // CH):
                col = g * GW + q * CH
                xp = xdt_b[:, col:col + CH]
                dyp = dy_b[:, col:col + CH]
                acc = None
                for hh in range(2):
                    h = col // HP + hh
                    keep = lo if hh == 0 else jnp.logical_not(lo)
                    dyh = jnp.where(keep, dyp, jnp.zeros_like(dyp))
                    dec = _decay_matrix(c, h)
                    m = gmat * dec
                    dm = _dot_nt(dyh, xp)
                    dseg = dm * m
                    dcs = dcs + jnp.where(lane == h, jnp.sum(dseg, axis=1, keepdims=True), 0.0)
                    dcst = dcst + jnp.where(sub == h, jnp.sum(dseg, axis=0, keepdims=True), 0.0)
                    dgm = dgm + dm * dec
                    term = _dot_tn(m.astype(bf16), dyh)
                    acc = term if acc is None else acc + term
                dxdt_scr[:, col:col + CH] = acc
            dgm_b = dgm.astype(bf16)
            bds = _dot(bg, dpn_b[:, gsl])
            dxdt_scr[:, gsl] = dxdt_scr[:, gsl] + c["dte_x"][:, gsl] * bds
            dc_g = _dot(dgm_b, bg) + _dot_nt(dq_b[:, gsl], p_b[:, gsl])
            db_g = _dot_tn(dgm_b, cg) + _dot_nt(xdte_b[:, gsl], dpn_b[:, gsl])
            dx_ref[:, D + g * NS:D + (g + 1) * NS] = db_g
            dx_ref[:, D + NG * NS + g * NS:D + NG * NS + (g + 1) * NS] = dc_g
            dp_scr[:, gsl] = dpn[:, gsl] * c["ecl_x"][:, gsl] + _dot_tn(cg, dq_b[:, gsl])
            q_g = _dot(cg, p_b[:, gsl])
            e_g = e[:, gsl]
            dcs = dcs + c["ecs"] * _split_dot(dy[:, gsl] * q_g, e_g, 2, nt=True)
            ddte = _split_dot(xdt[:, gsl] * bds, e_g, 2, nt=True) * c["dte"]
            dcs = dcs - ddte
            dcs = dcs + jnp.where(sub == CH - 1, jnp.sum(ddte, axis=0, keepdims=True), 0.0)

        decl = _split_dot(jnp.broadcast_to(jnp.sum(dpn * p, axis=0, keepdims=True), (8, D)), e, 2, nt=True)[0:1]
        dcs = dcs + jnp.where(sub == CH - 1, c["ecl"] * decl, 0.0)
        dcs = dcs - dcst.T
        dadt = lax.dot_general(tri_ref[...], dcs, (((0,), (0,)), ((), ())), precision=lax.Precision.HIGHEST,
                               preferred_element_type=f32)
        dxdt = dxdt_scr[...]
        ddt = dadt * c["a"] + _split_dot(dxdt * xs, e, 2, nt=True)
        ddtr = jnp.where(lane < NH, ddt * _sigmoid(c["dtr"]), 0.0)
        ddt_ref[...] = ddtr.astype(bf16)
        dx_ref[:, 0:D] = dxdt * c["dt_x"] + c["dsk_x"] * dy
        dsk = _split_dot(jnp.broadcast_to(jnp.sum(dy * xs, axis=0, keepdims=True), (8, D)), e, 2, nt=True)[0:1]
        dalog = jnp.sum(dadt * c["dt"], axis=0, keepdims=True) * c["a"]
        row8 = lax.broadcasted_iota(jnp.int32, (8, CH), 0)
        dpar = jnp.where(row8 == 0, jnp.sum(ddtr, axis=0, keepdims=True),
                         jnp.where(row8 == 1, dalog, jnp.where(row8 == 2, dsk, 0.0)))
        dpar = jnp.where(lax.broadcasted_iota(jnp.int32, (8, CH), 1) < NH, dpar, 0.0)
        _accum(dpar_ref, dpar, i == 0)
        _accum(dgs_ref, jnp.sum(dgs_rows, axis=0, keepdims=True), i == 0)

    nb = t // CH
    rev = lambda i: (i // nc) * nc + (nc - 1 - i % nc)
    return pl.pallas_call(
        body, name=name, grid=(nb,),
        in_specs=[pl.BlockSpec((CH, D), lambda i: (rev(i), 0)),
                  pl.BlockSpec((CH, NG * NS), lambda i: (rev(i), D // (NG * NS))),
                  pl.BlockSpec((CH, NG * NS), lambda i: (rev(i), D // (NG * NS) + 1)),
                  pl.BlockSpec((CH, CH), lambda i: (rev(i), COL_DT // CH)),
                  pl.BlockSpec((CH, D), lambda i: (rev(i), COL_Z // D)),
                  pl.BlockSpec((CH, D), lambda i: (rev(i), 0)),
                  pl.BlockSpec((1, NS, D), lambda i: (rev(i), 0, 0)),
                  pl.BlockSpec((CH, D), lambda i: (rev(i), 1)),
                  pl.BlockSpec((8, CH), lambda i: (0, 0)), pl.BlockSpec((CH, D), lambda i: (0, 0)),
                  pl.BlockSpec((CH, CH), lambda i: (0, 0)), pl.BlockSpec((1, D), lambda i: (0, 0))],
        out_specs=[pl.BlockSpec((CH, XBC), lambda i: (rev(i), 0)), pl.BlockSpec((CH, D), lambda i: (rev(i), 0)),
                   pl.BlockSpec((CH, CH), lambda i: (rev(i), 0)),
                   pl.BlockSpec((8, CH), lambda i: (0, 0)), pl.BlockSpec((1, D), lambda i: (0, 0))],
        out_shape=[_sds((t, XBC), f32), _sds((t, D), bf16), _sds((t, CH), bf16), _sds((8, CH), f32), _sds((1, D), f32)],
        scratch_shapes=[pltpu.VMEM((NS, D), f32), pltpu.VMEM((CH, D), f32)],
        compiler_params=_cparams(1))(xbcs, xbcs, xbcs, proj, proj, ypre, states, dcat, par, expand, tri, gs)


def loss_head(y, target, tb, name):
    t = y.shape[0]

    def body(y_ref, t_ref, s_ref, dy_ref):
        err = y_ref[...] - t_ref[...]
        dy_ref[...] = err * (1.0 / D)
        _accum(s_ref, jnp.zeros((8, CH), f32) + jnp.sum(err * err), pl.program_id(0) == 0)

    return pl.pallas_call(
        body, name=name, grid=(t // tb,),
        in_specs=[pl.BlockSpec((tb, D), lambda i: (i, 0)), pl.BlockSpec((tb, D), lambda i: (i, 0))],
        out_specs=[pl.BlockSpec((8, CH), lambda i: (0, 0)), pl.BlockSpec((tb, D), lambda i: (i, 0))],
        out_shape=[_sds((8, CH), f32), _sds((t, D), f32)],
        compiler_params=_cparams(1))(y, target)


def _tiles(t, seq):
    tm = min(512, t)
    return dict(tm=tm, tm_small=min(256, t), tb=min(512, seq), tb_conv=min(512, seq))


def local_step(x, target, depth, weights_of, seq, grads_done=None):
    t = x.shape[0]
    ts = _tiles(t, seq)
    tm, tb = ts["tm"], ts["tb"]
    saved, ws = [], []
    for l in range(depth):
        w = weights_of(l, x)
        ws.append(w)
        proj, h1 = norm_matmul(x, w["g1"], w["win"], tm, 1152, f32, "in_proj", token=w.get("token"))
        ya = group_a_fwd(proj, w["wa"], w["ga"], seq, tb, "group_a_fwd")
        xbcs = conv_b_fwd(proj, w["ws"], w["bs"], seq, tb, "conv_b_fwd")
        ys, ypre, states = ssd_fwd(xbcs, proj, w["par"], w["gs"], seq, "ssd_fwd")
        cat = jnp.concatenate([ya, ys], axis=1)
        mix, x2 = matmul_postnorm(cat, w["wo"], x, w["g2"], tm, False, "out_proj")
        fp, h2 = norm_matmul(x2, w["g3"], w["wu"], tm, 1024, bf16, "mlp_up")
        o, x3 = matmul_postnorm(fp, w["wd"], x2, w["g4"], tm, True, "mlp_down")
        saved.append(dict(x=x, proj=proj, h1=h1, xbcs=xbcs, ypre=ypre, states=states, cat=cat, mix=mix, x2=x2,
                          fp=fp, h2=h2, o=o))
        x = x3
    sse, dx = loss_head(x, target, tm, "loss_head")
    grads = [None] * depth
    token = None
    for l in reversed(range(depth)):
        s, w = saved[l], ws[l]
        do, dg4, dfp = postnorm_bwd_matmul(s["o"], w["g4"], dx, w["wd"], s["fp"], tm, 1024, bf16, "mlp_down_bwd",
                                           token=token)
        dwd = matmul_tn(s["fp"], do, 512, 1024, True, "mlp_down_dw")
        dx2, dg3 = matmul_prenorm_bwd(dfp, w["wu"], s["x2"], w["g3"], dx, tm, "mlp_up_bwd")
        dwu = matmul_tn(s["h2"], dfp, 512, 1024, False, "mlp_up_dw", col_blocks=True)
        dmix, dg2, dcat = postnorm_bwd_matmul(s["mix"], w["g2"], dx2, w["wo"], None, tm, 1024, f32, "out_proj_bwd")
        dwo = matmul_tn(s["cat"], dmix, 512, 1024, False, "out_proj_dw")
        dpa, dwa, dga = group_a_bwd(s["proj"], dcat, w["wa"], w["ga"], seq, tb, "group_a_bwd")
        dxbcs, dz, ddt, dpar, dgs = ssd_bwd(s["xbcs"], s["proj"], s["ypre"], s["states"], dcat, w["par"], w["gs"], seq,
                                            "ssd_bwd")
        dxbc, dws, dbs = conv_b_bwd(s["proj"], dxbcs, w["ws"], w["bs"], seq, tb, "conv_b_bwd")
        dproj = jnp.concatenate([dpa, dz, dxbc, ddt], axis=1)
        dx, dg1 = matmul_prenorm_bwd(dproj, w["win"], s["x"], w["g1"], dx2, ts["tm_small"], "in_proj_bwd")
        dwin = matmul_tn(s["h1"], dproj, 512, 1152, False, "in_proj_dw")
        grads[l] = dict(win=dwin, wo=dwo, wu=dwu, wd=dwd, wa=dwa, ws=dws, bs=dbs, par=dpar,
                        g1=dg1, ga=dga, gs=dgs, g2=dg2, g3=dg3, g4=dg4)
        if grads_done is not None:
            token = grads_done(l, grads[l], dx)
    return sse, dx, grads


GROUPS = {
    "chips": [(1, 0, 0), (0, 1, 0), (1, 1, 0)],
    "pair": [(0, 0, 1)],
    "all": [(1, 0, 0), (0, 1, 0), (1, 1, 0), (0, 0, 1), (1, 0, 1), (0, 1, 1), (1, 1, 1)],
}


def _group_index(group, x, y, c):
    return {"chips": 2 * x + y, "pair": c, "all": 4 * x + 2 * y + c}[group]


def _chunk_indices(shape, pieces):
    if len(shape) < 3:
        return [()]
    lead = [()]
    for n in shape[:-2]:
        lead = [i + (k,) for i in lead for k in range(n)]
    rows = shape[-2]
    split = max(1, pieces // len(lead))
    while split > 1 and (rows % split or (rows // split) % 16):
        split -= 1
    step = rows // split
    return [i + (pl.ds(s * step, step),) for i in lead for s in range(split)]


def _exchange(arrays, out_shapes, group, src_view, dst_view, view_shape, name, own, pieces=16):
    masks = GROUPS[group]
    na, nm = len(arrays), len(masks)
    cuts = [_chunk_indices(view_shape(a), pieces) for a in range(na)]

    def body(*refs):
        ins, outs = refs[:na], refs[na:2 * na]
        send_sems, recv_sems = refs[2 * na:2 * na + 2]
        local_sems = refs[2 * na + 2] if own else None
        x, y, c = lax.axis_index("x"), lax.axis_index("y"), lax.axis_index("c")
        me = _group_index(group, x, y, c)
        peers = []
        for mx, my, mc in masks:
            px, py, pc = (1 - x if mx else x), (1 - y if my else y), (1 - c if mc else c)
            peers.append(((px, py, pc), _group_index(group, px, py, pc)))

        def part(ref, idx):
            return ref.at[idx] if idx else ref

        if own:
            for a in range(na):
                for idx in cuts[a]:
                    pltpu.make_async_copy(part(src_view(ins[a], a, me), idx), part(dst_view(outs[a], a, me), idx),
                                          local_sems.at[a]).start()
        for a in range(na):
            for j, (dev, pidx) in enumerate(peers):
                for idx in cuts[a]:
                    pltpu.make_async_remote_copy(
                        src_ref=part(src_view(ins[a], a, pidx), idx), dst_ref=part(dst_view(outs[a], a, me), idx),
                        send_sem=send_sems.at[a * nm + j], recv_sem=recv_sems.at[a * nm + j],
                        device_id=dev, device_id_type=MESH).start()
        whole = []
        for a in range(na):
            for j, (dev, pidx) in enumerate(peers):
                whole.append(pltpu.make_async_remote_copy(
                    src_ref=src_view(ins[a], a, pidx), dst_ref=dst_view(outs[a], a, pidx),
                    send_sem=send_sems.at[a * nm + j], recv_sem=recv_sems.at[a * nm + j],
                    device_id=dev, device_id_type=MESH))
        for cp in whole:
            cp.wait_recv()
        for cp in whole:
            cp.wait_send()
        if own:
            for a in range(na):
                pltpu.make_async_copy(src_view(ins[a], a, me), dst_view(outs[a], a, me), local_sems.at[a]).wait()

    hbm = pl.BlockSpec(memory_space=pltpu.HBM)
    sems = [pltpu.SemaphoreType.DMA((na * nm,)), pltpu.SemaphoreType.DMA((na * nm,))]
    return pl.pallas_call(
        body, name=name, in_specs=[hbm] * na, out_specs=[hbm] * na,
        out_shape=[_sds(s, a.dtype) for s, a in zip(out_shapes, arrays)],
        scratch_shapes=sems + ([pltpu.SemaphoreType.DMA((na,))] if own else []))(*arrays)


def all_gather(arrays, group, name, slot_axis=0, own=True):
    n = len(GROUPS[group]) + 1
    shapes = [a.shape[:slot_axis] + (n,) + a.shape[slot_axis:] for a in arrays]
    lead = (slice(None),) * slot_axis
    return _exchange(arrays, shapes, group, lambda r, a, i: r, lambda r, a, i: r.at[lead + (i,)],
                     lambda a: arrays[a].shape, name, own)


HBM_SPEC = pl.BlockSpec(memory_space=pltpu.HBM)
SEM_SPEC = pl.BlockSpec(memory_space=pltpu.SEMAPHORE)
DATAFLOW = pltpu.SideEffectType.DATAFLOW_SIDE_EFFECTING
N_CHIPS = 4


def _chip_peers(x, y, c):
    out = []
    for mx, my, _ in GROUPS["chips"]:
        px, py = (1 - x if mx else x), (1 - y if my else y)
        out.append(((px, py, c), 2 * px + py))
    return out


def _weight_views(shards):
    half = [s.shape[0] // 2 for s in shards]
    return dict(src=lambda ref, a, c, to_chip: ref.at[pl.ds(c * half[a], half[a])],
                dst=lambda ref, a, c, from_chip: ref.at[from_chip, pl.ds(c * half[a], half[a])],
                rows=lambda a: half[a])


def _grad_views(sums):
    return dict(src=lambda ref, a, c, to_chip: ref.at[to_chip], dst=lambda ref, a, c, from_chip: ref.at[from_chip],
                rows=lambda a: sums[a].shape[1])


def chips_start(sources, zones, views, name, pieces=4):
    na, nm = len(sources), N_CHIPS - 1

    def body(*refs):
        ins, lands = refs[:na], refs[na:2 * na]
        send_sems, recv_sems, token = refs[2 * na], refs[2 * na + 1], refs[-1]
        x, y, c = lax.axis_index("x"), lax.axis_index("y"), lax.axis_index("c")
        chip = 2 * x + y
        for a in range(na):
            step = views["rows"](a) // pieces
            for j, (dev, to_chip) in enumerate(_chip_peers(x, y, c)):
                for q in range(pieces):
                    rows = pl.ds(q * step, step)
                    pltpu.make_async_remote_copy(
                        src_ref=views["src"](ins[a], a, c, to_chip).at[rows],
                        dst_ref=views["dst"](lands[a], a, c, chip).at[rows],
                        send_sem=send_sems.at[a * nm + j], recv_sem=recv_sems.at[a * nm + j],
                        device_id=dev, device_id_type=MESH).start()
        token[...] = jnp.zeros_like(token)

    both = list(sources) + list(zones)
    outs = pl.pallas_call(
        body, name=name,
        out_shape=(pltpu.SemaphoreType.DMA((na * nm,)), pltpu.SemaphoreType.DMA((na * nm,)),
                   *[pltpu.HBM(b.shape, b.dtype) for b in both], _sds((8, CH), f32)),
        in_specs=[HBM_SPEC] * (2 * na),
        out_specs=(SEM_SPEC, SEM_SPEC, *[HBM_SPEC] * (2 * na), pl.BlockSpec(memory_space=pltpu.VMEM)),
        input_output_aliases={i: 2 + i for i in range(2 * na)},
        compiler_params=pltpu.CompilerParams(has_side_effects=DATAFLOW))(
            *[pltpu.with_memory_space_constraint(b, pltpu.HBM) for b in both])
    return dict(send=outs[0], recv=outs[1], sources=list(outs[2:2 + na]), zones=list(outs[2 + na:2 + 2 * na]),
                token=outs[-1], views=views)


def chips_wait(started, after, name):
    sources, zones, views = started["sources"], started["zones"], started["views"]
    na, nm = len(sources), N_CHIPS - 1

    def body(*refs):
        ins, lands = refs[:na], refs[na:2 * na]
        send_sems, recv_sems = refs[2 * na], refs[2 * na + 1]
        x, y, c = lax.axis_index("x"), lax.axis_index("y"), lax.axis_index("c")
        for a in range(na):
            for j, (dev, peer_chip) in enumerate(_chip_peers(x, y, c)):
                cp = pltpu.make_async_remote_copy(
                    src_ref=views["src"](ins[a], a, c, peer_chip), dst_ref=views["dst"](lands[a], a, c, peer_chip),
                    send_sem=send_sems.at[a * nm + j], recv_sem=recv_sems.at[a * nm + j],
                    device_id=dev, device_id_type=MESH)
                cp.wait_send()
                cp.wait_recv()

    both = list(sources) + list(zones)
    outs = pl.pallas_call(
        body, name=name, out_shape=tuple(pltpu.HBM(b.shape, b.dtype) for b in both),
        in_specs=[HBM_SPEC] * (2 * na) + [SEM_SPEC, SEM_SPEC, pl.BlockSpec(memory_space=pl.ANY)],
        out_specs=tuple([HBM_SPEC] * (2 * na)), input_output_aliases={i: i for i in range(2 * na)},
        compiler_params=pltpu.CompilerParams(has_side_effects=DATAFLOW))(*both, started["send"], started["recv"], after)
    return list(outs[:na]), list(outs[na:])


def weights_share(zones, name):
    na, nm = len(zones), N_CHIPS - 1

    def body(*refs):
        lands = refs[na:2 * na]
        send_sems, recv_sems = refs[2 * na:]
        x, y, c = lax.axis_index("x"), lax.axis_index("y"), lax.axis_index("c")
        chip = 2 * x + y
        sibling = (x, y, 1 - c)
        sends = []
        for a in range(na):
            half = zones[a].shape[1] // 2
            for m in range(1, N_CHIPS):
                mine = lands[a].at[chip ^ m, pl.ds(c * half, half)]
                sends.append(pltpu.make_async_remote_copy(
                    src_ref=mine, dst_ref=mine, send_sem=send_sems.at[a * nm + m - 1],
                    recv_sem=recv_sems.at[a * nm + m - 1], device_id=sibling, device_id_type=MESH))
        for cp in sends:
            cp.start()
        for a in range(na):
            half = zones[a].shape[1] // 2
            for m in range(1, N_CHIPS):
                theirs = lands[a].at[chip ^ m, pl.ds((1 - c) * half, half)]
                pltpu.make_async_remote_copy(
                    src_ref=theirs, dst_ref=theirs, send_sem=send_sems.at[a * nm + m - 1],
                    recv_sem=recv_sems.at[a * nm + m - 1], device_id=sibling, device_id_type=MESH).wait_recv()
        for cp in sends:
            cp.wait_send()

    return pl.pallas_call(
        body, name=name, in_specs=[HBM_SPEC] * na, out_specs=[HBM_SPEC] * na,
        out_shape=[_sds(z.shape, z.dtype) for z in zones], input_output_aliases={i: i for i in range(na)},
        scratch_shapes=[pltpu.SemaphoreType.DMA((na * nm,)), pltpu.SemaphoreType.DMA((na * nm,))])(*zones)


def pair_send_halves(grads, name):
    half = [g.shape[1] // 2 for g in grads]
    shapes = [(g.shape[0], h, g.shape[2]) for g, h in zip(grads, half)]
    return _exchange(grads, shapes, "pair", lambda r, a, i: r.at[:, pl.ds(i * half[a], half[a])],
                     lambda r, a, i: r, lambda a: shapes[a], name, False)


def sum_pair_half(g, recv, core, name, tb=256):
    nk, r, c = g.shape
    tb = min(tb, r // 2)
    nb = r // 2 // tb

    def body(core_ref, g_ref, r_ref, o_ref):
        o_ref[...] = (g_ref[...] + r_ref[...]).astype(bf16)

    return pl.pallas_call(
        body, name=name,
        grid_spec=pltpu.PrefetchScalarGridSpec(
            num_scalar_prefetch=1, grid=(nk, nb),
            in_specs=[pl.BlockSpec((None, tb, c), lambda k, i, core_ref: (k, core_ref[0] * nb + i, 0)),
                      pl.BlockSpec((None, tb, c), lambda k, i, core_ref: (k, i, 0))],
            out_specs=pl.BlockSpec((None, tb, c), lambda k, i, core_ref: (k, i, 0))),
        out_shape=_sds((nk, r // 2, c), bf16), compiler_params=_cparams(2))(
            jnp.reshape(core, (1,)).astype(jnp.int32), g, recv)


def chip_sum_into(acc, layer, own, others, chip, name, tb=256):
    n, r, c = own.shape
    tb = min(tb, r)

    def body(chip_ref, x_ref, y1_ref, y2_ref, y3_ref, acc_ref, o_ref):
        o_ref[...] = ((x_ref[...].astype(f32) + y1_ref[...].astype(f32)) + y2_ref[...].astype(f32)) + y3_ref[...].astype(f32)

    def slot(k):
        return pl.BlockSpec((None, tb, c), lambda i, chip_ref: (chip_ref[0] ^ k, i, 0))

    return pl.pallas_call(
        body, name=name,
        grid_spec=pltpu.PrefetchScalarGridSpec(
            num_scalar_prefetch=1, grid=(r // tb,),
            in_specs=[slot(k) for k in range(n)] + [pl.BlockSpec(memory_space=pl.ANY)],
            out_specs=pl.BlockSpec((None, tb, c), lambda i, chip_ref: (layer, i, 0))),
        out_shape=_sds(acc.shape, f32), input_output_aliases={n + 1: 0}, compiler_params=_cparams(1))(
            jnp.reshape(chip, (1,)).astype(jnp.int32), own, *([others] * (n - 1)), acc)


def adamw_halves(w, g_own, g_recv, m, v, core, name, tb=256):
    depth, r, c = w.shape
    tb = min(tb, r // 2)
    nb = r // 2 // tb

    def body(core_ref, w_ref, go_ref, gr_ref, m_ref, v_ref, g_ref, d_ref, mo_ref, vo_ref):
        gv = jnp.where(pl.program_id(1) == core_ref[0], go_ref[...], gr_ref[...])
        m2 = B1 * m_ref[...] + (1.0 - B1) * gv
        v2 = B2 * v_ref[...] + (1.0 - B2) * (gv * gv)
        m_hat = m2 / (1.0 - B1 ** STEP)
        v_hat = v2 / (1.0 - B2 ** STEP)
        g_ref[...] = gv
        d_ref[...] = -LR * (m_hat / (jnp.sqrt(v_hat) + AEPS) + WD * w_ref[...])
        mo_ref[...] = m2
        vo_ref[...] = v2

    whole = pl.BlockSpec((None, tb, c), lambda l, h, i, core_ref: (l, h * nb + i, 0))
    part = pl.BlockSpec((None, tb, c), lambda l, h, i, core_ref: (l, i, 0))
    return pl.pallas_call(
        body, name=name,
        grid_spec=pltpu.PrefetchScalarGridSpec(num_scalar_prefetch=1, grid=(depth, 2, nb),
                                               in_specs=[whole, part, part, whole, whole], out_specs=[whole] * 4),
        out_shape=[_sds(w.shape, f32)] * 4, compiler_params=_cparams(3))(
            jnp.reshape(core, (1,)).astype(jnp.int32), w, g_own, g_recv, m, v)


def sum_slots(y, out_dtype, name, tb=256):
    n, r, c = y.shape
    tb = min(tb, r)

    def body(y_ref, o_ref):
        acc = y_ref[0].astype(f32)
        for i in range(1, n):
            acc = acc + y_ref[i].astype(f32)
        o_ref[...] = acc.astype(out_dtype)

    return pl.pallas_call(
        body, name=name, grid=(r // tb,),
        in_specs=[pl.BlockSpec((n, tb, c), lambda i: (0, i, 0))], out_specs=pl.BlockSpec((tb, c), lambda i: (i, 0)),
        out_shape=_sds((r, c), out_dtype), compiler_params=_cparams(1))(y)


def adamw(w, g, m, v, name, tb=256):
    r, c = w.shape
    tb = min(tb, r)

    def body(w_ref, g_ref, m_ref, v_ref, d_ref, mo_ref, vo_ref):
        gv = g_ref[...]
        m2 = B1 * m_ref[...] + (1.0 - B1) * gv
        v2 = B2 * v_ref[...] + (1.0 - B2) * (gv * gv)
        m_hat = m2 / (1.0 - B1 ** STEP)
        v_hat = v2 / (1.0 - B2 ** STEP)
        d_ref[...] = -LR * (m_hat / (jnp.sqrt(v_hat) + AEPS) + WD * w_ref[...])
        mo_ref[...] = m2
        vo_ref[...] = v2

    spec = pl.BlockSpec((tb, c), lambda i: (i, 0))
    return pl.pallas_call(
        body, name=name, grid=(r // tb,), in_specs=[spec] * 4, out_specs=[spec] * 3,
        out_shape=[_sds((r, c), f32)] * 3, compiler_params=_cparams(1))(w, g, m, v)


SMALL_ROW = 1024
SMALL_PIECES = (("g1", D), ("ga", D), ("gs", D), ("g2", D), ("g3", D), ("g4", D), ("wa", 3 * D), ("ws", 4 * XBC),
                ("bs", XBC), ("par", 3 * CH))
SMALL_LAYER = 17 * SMALL_ROW


def _pack_small(grads):
    rows = []
    for g in grads:
        flat = [g["g1"], g["ga"], g["gs"], g["g2"], g["g3"], g["g4"], g["wa"][:3], g["ws"][:4], g["bs"], g["par"][:3]]
        v = jnp.concatenate([p.reshape(-1) for p in flat])
        rows.append(jnp.pad(v, (0, SMALL_LAYER - v.shape[0])))
    packed = jnp.concatenate(rows).reshape(-1, SMALL_ROW)
    return jnp.pad(packed, ((0, -packed.shape[0] % 8), (0, 0)))


def _unpack_small(packed, depth):
    flat = packed.reshape(-1)[:depth * SMALL_LAYER].reshape(depth, SMALL_LAYER)
    out, off = {}, 0
    for key, size in SMALL_PIECES:
        out[key] = flat[:, off:off + size]
        off += size
    return out


def kernel(x, norm_mix_pre, w_in, conv_a_w, ssm_conv_w, ssm_conv_b, dt_bias, a_log, d_skip, conv_out_norm, ssm_out_norm, w_out, norm_mix_post, norm_mlp_pre, w_up, w_down, norm_mlp_post, loss_target, m_norm_mix_pre, m_w_in, m_conv_a_w, m_ssm_conv_w, m_ssm_conv_b, m_dt_bias, m_a_log, m_d_skip, m_conv_out_norm, m_ssm_out_norm, m_w_out, m_norm_mix_post, m_norm_mlp_pre, m_w_up, m_w_down, m_norm_mlp_post, v_norm_mix_pre, v_w_in, v_conv_a_w, v_ssm_conv_w, v_ssm_conv_b, v_dt_bias, v_a_log, v_d_skip, v_conv_out_norm, v_ssm_out_norm, v_w_out, v_norm_mix_post, v_norm_mlp_pre, v_w_up, v_w_down, v_norm_mlp_post):
    nb, seq, _ = x.shape
    t = nb * seq
    depth = w_in.shape[0]
    ncol = w_in.shape[2]
    chip = 2 * lax.axis_index("x") + lax.axis_index("y")

    taps = [conv_a_w, ssm_conv_w]
    taps_g = all_gather(taps, "chips", "gather_taps", slot_axis=1, own=False)
    wa_g, ws_g = [lax.dynamic_update_index_in_dim(g, s, chip, 1) for g, s in zip(taps_g, taps)]
    wa_full = jnp.transpose(wa_g, (0, 2, 1, 3)).reshape(depth, 3, D)
    ws_full = jnp.transpose(ws_g, (0, 2, 1, 3)).reshape(depth, 4, XBC)
    lane_pad = lambda a: jnp.pad(a, ((0, 0), (0, CH - a.shape[1])))
    par = jnp.stack([lane_pad(dt_bias), lane_pad(a_log), lane_pad(d_skip)], axis=1)
    par = jnp.pad(par, ((0, 0), (0, 5), (0, 0)))

    def start(l):
        shards = [w_in[l].astype(bf16), w_out[l].astype(bf16), w_up[l].astype(bf16), w_down[l].astype(bf16)]
        zones = [lax.empty((N_CHIPS,) + s.shape, s.dtype) for s in shards]
        return chips_start(shards, zones, _weight_views(shards), f"weights_start_{l}")

    travelling = {0: start(0)}

    def weights_of(l, x_in):
        shards, zones = chips_wait(travelling.pop(l), x_in, f"weights_wait_{l}")
        zones = weights_share(zones, "weights_share")
        win_z, wo_z, wu_z, wd_z = [lax.dynamic_update_index_in_dim(z, s, chip, 0) for z, s in zip(zones, shards)]
        token = None
        if l + 1 < depth:
            travelling[l + 1] = start(l + 1)
            token = travelling[l + 1]["token"]
        win_full = jnp.pad(jnp.transpose(win_z, (1, 0, 2)).reshape(D, N_CHIPS * ncol),
                           ((0, 0), (0, PROJ - N_CHIPS * ncol)))
        return dict(win=win_full, wo=wo_z.reshape(2 * D, D), wu=wu_z, wd=wd_z.reshape(DFF, D),
                    wa=jnp.pad(wa_full[l], ((0, 5), (0, 0))), ws=jnp.pad(ws_full[l], ((0, 4), (0, 0))),
                    bs=ssm_conv_b[l][None], par=par[l], g1=norm_mix_pre[l][None], ga=conv_out_norm[l][None],
                    gs=ssm_out_norm[l][None], g2=norm_mix_post[l][None], g3=norm_mlp_pre[l][None],
                    g4=norm_mlp_post[l][None], token=token)

    core = lax.axis_index("c")
    grads_travelling = {}

    def grads_done(l, g, dx_l):
        mats = [jnp.transpose(g["win"][:, :N_CHIPS * ncol].reshape(D, N_CHIPS, ncol), (1, 0, 2)),
                g["wo"].reshape(N_CHIPS, 2 * D // N_CHIPS, D), g["wu"], g["wd"].reshape(N_CHIPS, DFF // N_CHIPS, D)]
        received = pair_send_halves(mats, "grads_to_pair")
        sums = [sum_pair_half(m_, r_, core, "pair_sum") for m_, r_ in zip(mats, received)]
        zones = [lax.empty(s.shape, s.dtype) for s in sums]
        grads_travelling[l] = chips_start(sums, zones, _grad_views(sums), f"grads_start_{l}")
        return grads_travelling[l]["token"]

    sse, dx, grads = local_step(x.reshape(t, D), loss_target.reshape(t, D), depth, weights_of, seq, grads_done)
    loss = lax.psum(0.5 / D * sse[0, 0], ("x", "y", "c"))

    big_w = [w_in, w_out, w_up, w_down]
    acc = [lax.empty((depth, bw.shape[1] // 2, bw.shape[2]), f32) for bw in big_w]
    for l in reversed(range(depth)):
        sums, zones = chips_wait(grads_travelling.pop(l), dx, f"grads_wait_{l}")
        acc = [chip_sum_into(acc_a, l, s, z, chip, "chip_sum") for acc_a, s, z in zip(acc, sums, zones)]
    from_sibling = _exchange(acc, [a.shape for a in acc], "pair", lambda r, a, i: r, lambda r, a, i: r,
                             lambda a: acc[a].shape, "grads_from_pair", False)

    small_all = all_gather([_pack_small(grads)], "all", "gather_small")[0]
    small = _unpack_small(sum_slots(small_all, f32, "small_sum", tb=8), depth)
    wa_cols, ws_cols = conv_a_w.shape[2], ssm_conv_w.shape[2]
    par_g = small["par"].reshape(depth, 3, CH)
    g_small = dict(
        norm_mix_pre=small["g1"], conv_out_norm=small["ga"], ssm_out_norm=small["gs"], norm_mix_post=small["g2"],
        norm_mlp_pre=small["g3"], norm_mlp_post=small["g4"], ssm_conv_b=small["bs"],
        conv_a_w=lax.dynamic_slice_in_dim(small["wa"].reshape(depth, 3, D), chip * wa_cols, wa_cols, axis=2),
        ssm_conv_w=lax.dynamic_slice_in_dim(small["ws"].reshape(depth, 4, XBC), chip * ws_cols, ws_cols, axis=2),
        dt_bias=par_g[:, 0, :NH], a_log=par_g[:, 1, :NH], d_skip=par_g[:, 2, :NH])

    given = dict(norm_mix_pre=(norm_mix_pre, m_norm_mix_pre, v_norm_mix_pre), w_in=(w_in, m_w_in, v_w_in),
                 conv_a_w=(conv_a_w, m_conv_a_w, v_conv_a_w), ssm_conv_w=(ssm_conv_w, m_ssm_conv_w, v_ssm_conv_w),
                 ssm_conv_b=(ssm_conv_b, m_ssm_conv_b, v_ssm_conv_b), dt_bias=(dt_bias, m_dt_bias, v_dt_bias),
                 a_log=(a_log, m_a_log, v_a_log), d_skip=(d_skip, m_d_skip, v_d_skip),
                 conv_out_norm=(conv_out_norm, m_conv_out_norm, v_conv_out_norm),
                 ssm_out_norm=(ssm_out_norm, m_ssm_out_norm, v_ssm_out_norm), w_out=(w_out, m_w_out, v_w_out),
                 norm_mix_post=(norm_mix_post, m_norm_mix_post, v_norm_mix_post),
                 norm_mlp_pre=(norm_mlp_pre, m_norm_mlp_pre, v_norm_mlp_pre), w_up=(w_up, m_w_up, v_w_up),
                 w_down=(w_down, m_w_down, v_w_down), norm_mlp_post=(norm_mlp_post, m_norm_mlp_post, v_norm_mlp_post))
    halves = dict(zip(["w_in", "w_out", "w_up", "w_down"], zip(acc, from_sibling)))
    order = ["norm_mix_pre", "w_in", "conv_a_w", "ssm_conv_w", "ssm_conv_b", "dt_bias", "a_log", "d_skip",
             "conv_out_norm", "ssm_out_norm", "w_out", "norm_mix_post", "norm_mlp_pre", "w_up", "w_down",
             "norm_mlp_post"]
    g_out, d_out, m_out, v_out = [], [], [], []
    for n in order:
        wv, mv, vv = given[n]
        if n in halves:
            gv, dlt, m2, v2 = adamw_halves(wv, *halves[n], mv, vv, core, "adamw_matrix")
        else:
            gv = g_small[n].reshape(wv.shape)
            two_d = lambda a: a.reshape(-1, a.shape[-1])
            dlt, m2, v2 = adamw(two_d(wv), two_d(gv), two_d(mv), two_d(vv), "adamw")
        g_out.append(gv)
        d_out.append(dlt.reshape(wv.shape))
        m_out.append(m2.reshape(wv.shape))
        v_out.append(v2.reshape(wv.shape))
    return (loss, dx.reshape(nb, seq, D), *g_out, *d_out, *m_out, *v_out)
```

```python
import functools

import jax
import jax.numpy as jnp
from jax import lax
from jax.experimental import pallas as pl
from jax.experimental.pallas import tpu as pltpu

f32, bf16 = jnp.float32, jnp.bfloat16

D = 1024
NH, HP = 16, 64
NG, NS = 2, 128
CH = 128
XBC = D + 2 * NG * NS
DFF = 4 * D
IN_COLS = 3 * D + D + XBC + NH
PROJ = 5760
COL_Z, COL_XBC, COL_DT = 3 * D, 4 * D, 4 * D + XBC
EPS = 1e-6
HALO = 8
VMEM_LIMIT = 56 * 2**20
MESH = pl.DeviceIdType.MESH

LR, B1, B2, AEPS, WD, STEP = 0.001, 0.9, 0.999, 1e-08, 0.01, 10


def _cparams(n_axes):
    return pltpu.CompilerParams(dimension_semantics=("arbitrary",) * n_axes, vmem_limit_bytes=VMEM_LIMIT)


def _sds(shape, dtype):
    return jax.ShapeDtypeStruct(tuple(shape), dtype)


def _token_spec(token):
    return [] if token is None else [pl.BlockSpec(memory_space=pl.ANY)]


def _token_arg(token):
    return [] if token is None else [token]


def _rms_fwd(x, g):
    r = lax.rsqrt(jnp.mean(x * x, axis=-1, keepdims=True) + EPS)
    return x * r * g


def _rms_bwd(x, g, dy):
    r = lax.rsqrt(jnp.mean(x * x, axis=-1, keepdims=True) + EPS)
    xh = x * r
    gdy = dy * g
    dx = r * (gdy - xh * jnp.mean(xh * gdy, axis=-1, keepdims=True))
    return dx, dy * xh


def _accum(ref, part, first):
    @pl.when(first)
    def _():
        ref[...] = part

    @pl.when(jnp.logical_not(first))
    def _():
        ref[...] += part


def _dot_nt(a, b):
    return lax.dot_general(a, b, (((1,), (1,)), ((), ())), preferred_element_type=f32)


def _dot_tn(a, b):
    return lax.dot_general(a, b, (((0,), (0,)), ((), ())), preferred_element_type=f32)


def _dot(a, b):
    return jnp.dot(a, b, preferred_element_type=f32)


def _split_dot(x, e_bf, n_split, nt=False):
    acc = None
    rem = x
    for s in range(n_split):
        hi = rem.astype(bf16)
        term = _dot_nt(hi, e_bf) if nt else _dot(hi, e_bf)
        acc = term if acc is None else acc + term
        if s + 1 < n_split:
            rem = rem - hi.astype(f32)
    return acc


def _sigmoid(x):
    return 1.0 / (1.0 + jnp.exp(-x))


def norm_matmul(x, g, w, tm, tn, out_dtype, name, token=None):
    t = x.shape[0]
    if w.ndim == 3:
        assert w.shape[2] == tn
        n = w.shape[0] * tn
        w_spec = pl.BlockSpec((None, D, tn), lambda i, j: (j, 0, 0))
    else:
        n = w.shape[1]
        w_spec = pl.BlockSpec((D, tn), lambda i, j: (0, j))

    def body(x_ref, g_ref, w_ref, *rest):
        o_ref, h_ref = rest[-2:]

        @pl.when(pl.program_id(1) == 0)
        def _():
            h_ref[...] = _rms_fwd(x_ref[...], g_ref[...]).astype(bf16)

        o_ref[...] = _dot(h_ref[...], w_ref[...]).astype(out_dtype)

    return pl.pallas_call(
        body, name=name, grid=(t // tm, n // tn),
        in_specs=[pl.BlockSpec((tm, D), lambda i, j: (i, 0)), pl.BlockSpec((1, D), lambda i, j: (0, 0)), w_spec]
        + _token_spec(token),
        out_specs=[pl.BlockSpec((tm, tn), lambda i, j: (i, j)), pl.BlockSpec((tm, D), lambda i, j: (i, 0))],
        out_shape=[_sds((t, n), out_dtype), _sds((t, D), bf16)],
        compiler_params=_cparams(2))(x, g, w, *_token_arg(token))


def matmul_postnorm(a, w, xres, g, tm, relu2, name):
    t, k = a.shape

    def body(a_ref, w_ref, xr_ref, g_ref, y_ref, xo_ref):
        av = a_ref[...]
        if relu2:
            af = jnp.maximum(av.astype(f32), 0.0)
            av = (af * af).astype(bf16)
        y = _dot(av, w_ref[...])
        y_ref[...] = y
        xo_ref[...] = xr_ref[...] + _rms_fwd(y, g_ref[...])

    return pl.pallas_call(
        body, name=name, grid=(t // tm,),
        in_specs=[pl.BlockSpec((tm, k), lambda i: (i, 0)), pl.BlockSpec((k, D), lambda i: (0, 0)),
                  pl.BlockSpec((tm, D), lambda i: (i, 0)), pl.BlockSpec((1, D), lambda i: (0, 0))],
        out_specs=[pl.BlockSpec((tm, D), lambda i: (i, 0)), pl.BlockSpec((tm, D), lambda i: (i, 0))],
        out_shape=[_sds((t, D), f32), _sds((t, D), f32)],
        compiler_params=_cparams(1))(a, w, xres, g)


def postnorm_bwd_matmul(y, g, dxo, w, fp, tm, tn, out_dtype, name, token=None):
    t, n = y.shape[0], w.shape[0]
    relu = fp is not None

    def body(*refs):
        y_ref, g_ref, dxo_ref, w_ref = refs[:4]
        fp_ref = refs[4] if relu else None
        dy_ref, dg_ref, da_ref = refs[-3:]
        i, j = pl.program_id(0), pl.program_id(1)

        @pl.when(j == 0)
        def _():
            dx, dgc = _rms_bwd(y_ref[...], g_ref[...], dxo_ref[...])
            dy_ref[...] = dx.astype(bf16)
            _accum(dg_ref, jnp.sum(dgc, axis=0, keepdims=True), i == 0)

        da = _dot_nt(dy_ref[...], w_ref[...])
        if relu:
            da = da * (2.0 * jnp.maximum(fp_ref[...].astype(f32), 0.0))
        da_ref[...] = da.astype(out_dtype)

    in_specs = [pl.BlockSpec((tm, D), lambda i, j: (i, 0)), pl.BlockSpec((1, D), lambda i, j: (0, 0)),
                pl.BlockSpec((tm, D), lambda i, j: (i, 0)), pl.BlockSpec((tn, D), lambda i, j: (j, 0))]
    args = [y, g, dxo, w]
    if relu:
        in_specs.append(pl.BlockSpec((tm, tn), lambda i, j: (i, j)))
        args.append(fp)
    in_specs += _token_spec(token)
    args += _token_arg(token)
    return pl.pallas_call(
        body, name=name, grid=(t // tm, n // tn), in_specs=in_specs,
        out_specs=[pl.BlockSpec((tm, D), lambda i, j: (i, 0)), pl.BlockSpec((1, D), lambda i, j: (0, 0)),
                   pl.BlockSpec((tm, tn), lambda i, j: (i, j))],
        out_shape=[_sds((t, D), bf16), _sds((1, D), f32), _sds((t, n), out_dtype)],
        compiler_params=_cparams(2))(*args)


def matmul_prenorm_bwd(da, w, x, g, dxo, tm, name):
    t, k = da.shape
    blocked = w.ndim == 3

    def body(da_ref, w_ref, x_ref, g_ref, dxo_ref, dx_ref, dg_ref):
        if blocked:
            kc = w.shape[2]
            dh = _dot_nt(da_ref[:, 0:kc], w_ref[0])
            for q in range(1, w.shape[0]):
                dh = dh + _dot_nt(da_ref[:, q * kc:(q + 1) * kc], w_ref[q])
        else:
            dh = _dot_nt(da_ref[...], w_ref[...])
        dxn, dgc = _rms_bwd(x_ref[...], g_ref[...], dh)
        dx_ref[...] = dxo_ref[...] + dxn
        _accum(dg_ref, jnp.sum(dgc, axis=0, keepdims=True), pl.program_id(0) == 0)

    w_spec = pl.BlockSpec(w.shape, (lambda i: (0, 0, 0)) if blocked else (lambda i: (0, 0)))
    return pl.pallas_call(
        body, name=name, grid=(t // tm,),
        in_specs=[pl.BlockSpec((tm, k), lambda i: (i, 0)), w_spec,
                  pl.BlockSpec((tm, D), lambda i: (i, 0)), pl.BlockSpec((1, D), lambda i: (0, 0)),
                  pl.BlockSpec((tm, D), lambda i: (i, 0))],
        out_specs=[pl.BlockSpec((tm, D), lambda i: (i, 0)), pl.BlockSpec((1, D), lambda i: (0, 0))],
        out_shape=[_sds((t, D), f32), _sds((1, D), f32)],
        compiler_params=_cparams(1))(da, w, x, g, dxo)


def matmul_tn(a, b, tm, tn, relu2, name, col_blocks=False):
    t, m = a.shape
    n = b.shape[1]
    if col_blocks:
        out_spec, out_shape = pl.BlockSpec((None, tm, tn), lambda i, j: (j, i, 0)), _sds((n // tn, m, tn), f32)
    else:
        out_spec, out_shape = pl.BlockSpec((tm, tn), lambda i, j: (i, j)), _sds((m, n), f32)

    def body(a_ref, b_ref, o_ref, at_ref):
        @pl.when(pl.program_id(1) == 0)
        def _():
            av = a_ref[...]
            if relu2:
                af = jnp.maximum(av.astype(f32), 0.0)
                av = (af * af).astype(bf16)
            at_ref[...] = av.T

        o_ref[...] = _dot(at_ref[...], b_ref[...])

    return pl.pallas_call(
        body, name=name, grid=(m // tm, n // tn),
        in_specs=[pl.BlockSpec((t, tm), lambda i, j: (0, i)), pl.BlockSpec((t, tn), lambda i, j: (0, j))],
        out_specs=out_spec, out_shape=out_shape,
        scratch_shapes=[pltpu.VMEM((tm, t), bf16)],
        compiler_params=_cparams(2))(a, b)


def _halo_prev(tb, col):
    return lambda i: (jnp.maximum(i * (tb // HALO) - 1, 0), col)


def _halo_next(tb, col, t):
    return lambda i: (jnp.minimum((i + 1) * (tb // HALO), t // HALO - 1), col)


def group_a_fwd(proj, wa, g, seq, tb, name):
    t = proj.shape[0]
    bps = seq // tb

    def body(xa_ref, ca_ref, ba_ref, xah_ref, cah_ref, wa_ref, g_ref, o_ref, u_scr):
        first = (pl.program_id(0) % bps) == 0
        u = ca_ref[...] * xa_ref[...]
        u_scr[0:HALO, :] = jnp.where(first, 0.0, cah_ref[...] * xah_ref[...])
        u_scr[HALO:HALO + tb, :] = u
        w = wa_ref[...]
        cv = w[2:3] * u + w[1:2] * u_scr[pl.ds(HALO - 1, tb), :] + w[0:1] * u_scr[pl.ds(HALO - 2, tb), :]
        o_ref[...] = _rms_fwd(ba_ref[...] * cv, g_ref[...]).astype(bf16)

    blk = lambda c: pl.BlockSpec((tb, D), lambda i: (i, c))
    return pl.pallas_call(
        body, name=name, grid=(t // tb,),
        in_specs=[blk(0), blk(1), blk(2),
                  pl.BlockSpec((HALO, D), _halo_prev(tb, 0)), pl.BlockSpec((HALO, D), _halo_prev(tb, 1)),
                  pl.BlockSpec((8, D), lambda i: (0, 0)), pl.BlockSpec((1, D), lambda i: (0, 0))],
        out_specs=pl.BlockSpec((tb, D), lambda i: (i, 0)),
        out_shape=_sds((t, 2 * D), bf16),
        scratch_shapes=[pltpu.VMEM((tb + HALO, D), f32)],
        compiler_params=_cparams(1))(proj, proj, proj, proj, proj, wa, g)


def group_a_bwd(proj, dcat, wa, g, seq, tb, name):
    t = proj.shape[0]
    bps = seq // tb

    def body(xa_ref, ca_ref, ba_ref, dy_ref, xap_ref, cap_ref, xan_ref, can_ref, ban_ref, dyn_ref, wa_ref, g_ref,
             dp_ref, dwa_ref, dg_ref, u_scr, d_scr):
        i = pl.program_id(0)
        first = (i % bps) == 0
        last = (i % bps) == bps - 1
        w = wa_ref[...]
        gv = g_ref[...]
        xa, ca, ba = xa_ref[...], ca_ref[...], ba_ref[...]
        u_scr[0:HALO, :] = jnp.where(first, 0.0, cap_ref[...] * xap_ref[...])
        u_scr[HALO:HALO + tb, :] = ca * xa
        u_scr[HALO + tb:2 * HALO + tb, :] = can_ref[...] * xan_ref[...]

        def conv(start, rows):
            return (w[2:3] * u_scr[pl.ds(start, rows), :] + w[1:2] * u_scr[pl.ds(start - 1, rows), :]
                    + w[0:1] * u_scr[pl.ds(start - 2, rows), :])

        cv = conv(HALO, tb)
        dya, dgc = _rms_bwd(ba * cv, gv, dy_ref[...])
        ban = ban_ref[...]
        dyan, _ = _rms_bwd(ban * conv(HALO + tb, HALO), gv, dyn_ref[...])
        dcv = dya * ba
        d_scr[0:tb, :] = dcv
        d_scr[tb:tb + HALO, :] = jnp.where(last, 0.0, dyan * ban)
        du = w[2:3] * dcv + w[1:2] * d_scr[pl.ds(1, tb), :] + w[0:1] * d_scr[pl.ds(2, tb), :]
        dp_ref[:, 0:D] = (du * ca).astype(bf16)
        dp_ref[:, D:2 * D] = (du * xa).astype(bf16)
        dp_ref[:, 2 * D:3 * D] = (dya * cv).astype(bf16)
        row = lax.broadcasted_iota(jnp.int32, (8, D), 0)
        dw = jnp.zeros((8, D), f32)
        for k in range(3):
            s = jnp.sum(dcv * u_scr[pl.ds(HALO - 2 + k, tb), :], axis=0, keepdims=True)
            dw = jnp.where(row == k, s, dw)
        _accum(dwa_ref, dw, i == 0)
        _accum(dg_ref, jnp.sum(dgc, axis=0, keepdims=True), i == 0)

    blk = lambda c: pl.BlockSpec((tb, D), lambda i: (i, c))
    prv = lambda c: pl.BlockSpec((HALO, D), _halo_prev(tb, c))
    nxt = lambda c: pl.BlockSpec((HALO, D), _halo_next(tb, c, t))
    return pl.pallas_call(
        body, name=name, grid=(t // tb,),
        in_specs=[blk(0), blk(1), blk(2), blk(0), prv(0), prv(1), nxt(0), nxt(1), nxt(2), nxt(0),
                  pl.BlockSpec((8, D), lambda i: (0, 0)), pl.BlockSpec((1, D), lambda i: (0, 0))],
        out_specs=[pl.BlockSpec((tb, 3 * D), lambda i: (i, 0)), pl.BlockSpec((8, D), lambda i: (0, 0)),
                   pl.BlockSpec((1, D), lambda i: (0, 0))],
        out_shape=[_sds((t, PROJ), bf16), _sds((8, D), f32), _sds((1, D), f32)],
        scratch_shapes=[pltpu.VMEM((tb + 2 * HALO, D), f32), pltpu.VMEM((tb + HALO, D), f32)],
        compiler_params=_cparams(1))(proj, proj, proj, dcat, proj, proj, proj, proj, proj, dcat, wa, g)


CB = 512
XBC_BLK0 = COL_XBC // CB


def conv_b_fwd(proj, ws, bs, seq, tb, name):
    t = proj.shape[0]
    bps = seq // tb

    def body(x_ref, xp_ref, w_ref, b_ref, o_ref, x_scr):
        first = (pl.program_id(1) % bps) == 0
        x = x_ref[...]
        x_scr[0:HALO, :] = jnp.where(first, 0.0, xp_ref[...])
        x_scr[HALO:HALO + tb, :] = x
        w = w_ref[...]
        xc = w[3:4] * x + b_ref[...]
        for k in range(3):
            xc = xc + w[k:k + 1] * x_scr[pl.ds(HALO - 3 + k, tb), :]
        o_ref[...] = xc * _sigmoid(xc)

    return pl.pallas_call(
        body, name=name, grid=(XBC // CB, t // tb),
        in_specs=[pl.BlockSpec((tb, CB), lambda j, i: (i, XBC_BLK0 + j)),
                  pl.BlockSpec((HALO, CB), lambda j, i: (jnp.maximum(i * (tb // HALO) - 1, 0), XBC_BLK0 + j)),
                  pl.BlockSpec((8, CB), lambda j, i: (0, j)), pl.BlockSpec((1, CB), lambda j, i: (0, j))],
        out_specs=pl.BlockSpec((tb, CB), lambda j, i: (i, j)),
        out_shape=_sds((t, XBC), f32),
        scratch_shapes=[pltpu.VMEM((tb + HALO, CB), f32)],
        compiler_params=_cparams(2))(proj, proj, ws, bs)


def conv_b_bwd(proj, dxs, ws, bs, dproj, seq, tb, name):
    t = proj.shape[0]
    bps = seq // tb

    def body(x_ref, xp_ref, xn_ref, d_ref, dn_ref, w_ref, b_ref, dproj_ref, dx_ref, dw_ref, db_ref, x_scr, d_scr):
        i = pl.program_id(1)
        first = (i % bps) == 0
        last = (i % bps) == bps - 1
        w = w_ref[...]
        bias = b_ref[...]
        x_scr[0:HALO, :] = jnp.where(first, 0.0, xp_ref[...])
        x_scr[HALO:HALO + tb, :] = x_ref[...]
        x_scr[HALO + tb:2 * HALO + tb, :] = xn_ref[...]

        def dsilu_at(start, rows, d):
            xc = bias + w[3:4] * x_scr[pl.ds(start, rows), :]
            for k in range(3):
                xc = xc + w[k:k + 1] * x_scr[pl.ds(start - 3 + k, rows), :]
            sg = _sigmoid(xc)
            return d * (sg * (1.0 + xc * (1.0 - sg)))

        dxc = dsilu_at(HALO, tb, d_ref[...])
        d_scr[0:tb, :] = dxc
        d_scr[tb:tb + HALO, :] = jnp.where(last, 0.0, dsilu_at(HALO + tb, HALO, dn_ref[...]))
        dx = w[3:4] * dxc
        for k in range(3):
            dx = dx + w[k:k + 1] * d_scr[pl.ds(3 - k, tb), :]
        dx_ref[...] = dx.astype(bf16)
        row = lax.broadcasted_iota(jnp.int32, (8, CB), 0)
        dw = jnp.zeros((8, CB), f32)
        for k in range(4):
            s = jnp.sum(dxc * x_scr[pl.ds(HALO - 3 + k, tb), :], axis=0, keepdims=True)
            dw = jnp.where(row == k, s, dw)
        _accum(dw_ref, dw, i == 0)
        _accum(db_ref, jnp.sum(dxc, axis=0, keepdims=True), i == 0)

    nh = t // HALO
    return pl.pallas_call(
        body, name=name, grid=(XBC // CB, t // tb),
        in_specs=[pl.BlockSpec((tb, CB), lambda j, i: (i, XBC_BLK0 + j)),
                  pl.BlockSpec((HALO, CB), lambda j, i: (jnp.maximum(i * (tb // HALO) - 1, 0), XBC_BLK0 + j)),
                  pl.BlockSpec((HALO, CB), lambda j, i: (jnp.minimum((i + 1) * (tb // HALO), nh - 1), XBC_BLK0 + j)),
                  pl.BlockSpec((tb, CB), lambda j, i: (i, j)),
                  pl.BlockSpec((HALO, CB), lambda j, i: (jnp.minimum((i + 1) * (tb // HALO), nh - 1), j)),
                  pl.BlockSpec((8, CB), lambda j, i: (0, j)), pl.BlockSpec((1, CB), lambda j, i: (0, j)),
                  pl.BlockSpec(memory_space=pl.ANY)],
        out_specs=[pl.BlockSpec((tb, CB), lambda j, i: (i, XBC_BLK0 + j)), pl.BlockSpec((8, CB), lambda j, i: (0, j)),
                   pl.BlockSpec((1, CB), lambda j, i: (0, j))],
        out_shape=[_sds((t, PROJ), bf16), _sds((8, XBC), f32), _sds((1, XBC), f32)],
        input_output_aliases={7: 0},
        scratch_shapes=[pltpu.VMEM((tb + 2 * HALO, CB), f32), pltpu.VMEM((tb + HALO, CB), f32)],
        compiler_params=_cparams(2))(proj, proj, proj, dxs, dxs, ws, bs, dproj)


def place_columns(buf, part, col_block, tb, name):
    t, wdt = part.shape

    def body(p_ref, buf_ref, o_ref):
        o_ref[...] = p_ref[...]

    return pl.pallas_call(
        body, name=name, grid=(t // tb,),
        in_specs=[pl.BlockSpec((tb, wdt), lambda i: (i, 0)), pl.BlockSpec(memory_space=pl.ANY)],
        out_specs=pl.BlockSpec((tb, wdt), lambda i: (i, col_block)), out_shape=_sds(buf.shape, buf.dtype),
        input_output_aliases={1: 0}, compiler_params=_cparams(1))(part, buf)


GW = D // NG


def _ssd_consts():
    head_of_lane = jnp.arange(D) // HP
    expand = (jnp.arange(CH)[:, None] == head_of_lane[None, :]).astype(bf16)
    tri = (jnp.arange(CH)[:, None] >= jnp.arange(CH)[None, :]).astype(f32)
    return expand, tri


def _ssd_common(par_ref, dtr_ref, e_ref, tri_ref):
    par = par_ref[...]
    dtb, alog, dsk = par[0:1], par[1:2], par[2:3]
    lane = lax.broadcasted_iota(jnp.int32, (CH, CH), 1)
    a = -jnp.exp(alog)
    dtr = dtr_ref[...] + dtb
    sp = jnp.maximum(dtr, 0.0) + jnp.log(1.0 + jnp.exp(-jnp.abs(dtr)))
    dt = jnp.where(lane < NH, sp, 0.0)
    cs = jnp.dot(tri_ref[...], dt * a, precision=lax.Precision.HIGHEST, preferred_element_type=f32)
    cs_last = cs[CH - 1:CH, :]
    dte = jnp.exp(cs_last - cs)
    ecs = jnp.exp(cs)
    ecl = jnp.exp(cs_last)
    e = e_ref[...]
    row8 = lax.broadcasted_iota(jnp.int32, (8, CH), 0)
    r8 = _split_dot(jnp.where(row8 == 0, ecl, jnp.where(row8 == 1, dsk, 0.0)), e, 3)
    return dict(a=a, dtr=dtr, dt=dt, cs=cs, cst=cs.T, dte=dte, ecs=ecs, ecl=ecl, e=e, lane=lane,
                dt_x=_split_dot(dt, e, 3), dte_x=_split_dot(dte, e, 3), ecs_x=_split_dot(ecs, e, 3),
                ecl_x=r8[0:1], dsk_x=r8[1:2])


def _decay_matrix(c, h):
    li = lax.broadcasted_iota(jnp.int32, (CH, CH), 0)
    seg = c["cs"][:, h:h + 1] - c["cst"][h:h + 1, :]
    return jnp.exp(jnp.where(li >= c["lane"], seg, -jnp.inf))


def _gate_norm_fwd(y, z, gs):
    zg = z * _sigmoid(z)
    yg = y * zg
    return jnp.concatenate([_rms_fwd(yg[:, k * GW:(k + 1) * GW], gs[:, k * GW:(k + 1) * GW]) for k in range(NG)], axis=1)


def ssd_fwd(xbcs, proj, par, gs, cat, seq, name):
    t = xbcs.shape[0]
    nc = seq // CH
    expand, tri = _ssd_consts()

    def body(xs_ref, b_ref, c_ref, dtr_ref, z_ref, par_ref, e_ref, tri_ref, gs_ref, cat_ref, yn_ref, y_ref, st_ref,
             p_scr, yd_scr):
        @pl.when(pl.program_id(0) % nc == 0)
        def _():
            p_scr[...] = jnp.zeros_like(p_scr)

        c = _ssd_common(par_ref, dtr_ref, e_ref, tri_ref)
        xs = xs_ref[...]
        xdt = xs * c["dt_x"]
        xdt_b = xdt.astype(bf16)
        xdte_b = (xdt * c["dte_x"]).astype(bf16)
        p = p_scr[...]
        st_ref[0] = p
        p_b = p.astype(bf16)
        lo = c["lane"] < HP
        for g in range(NG):
            bg = b_ref[:, g * NS:(g + 1) * NS].astype(bf16)
            cg = c_ref[:, g * NS:(g + 1) * NS].astype(bf16)
            gmat = _dot_nt(cg, bg)
            for q in range(GW // CH):
                col = g * GW + q * CH
                xp = xdt_b[:, col:col + CH]
                h0 = col // HP
                m0 = (gmat * _decay_matrix(c, h0)).astype(bf16)
                m1 = (gmat * _decay_matrix(c, h0 + 1)).astype(bf16)
                yd_scr[:, col:col + CH] = (_dot(m0, jnp.where(lo, xp, jnp.zeros_like(xp)))
                                           + _dot(m1, jnp.where(lo, jnp.zeros_like(xp), xp)))
            gsl = slice(g * GW, (g + 1) * GW)
            yoff = _dot(cg, p_b[:, gsl]) * c["ecs_x"][:, gsl]
            yd_scr[:, gsl] = yd_scr[:, gsl] + yoff
            p_scr[:, gsl] = p[:, gsl] * c["ecl_x"][:, gsl] + _dot_tn(bg, xdte_b[:, gsl])
        y = yd_scr[...] + c["dsk_x"] * xs
        y_ref[...] = y
        yn_ref[...] = _gate_norm_fwd(y, z_ref[...], gs_ref[...]).astype(bf16)

    nb = t // CH
    return pl.pallas_call(
        body, name=name, grid=(nb,),
        in_specs=[pl.BlockSpec((CH, D), lambda i: (i, 0)),
                  pl.BlockSpec((CH, NG * NS), lambda i: (i, D // (NG * NS))),
                  pl.BlockSpec((CH, NG * NS), lambda i: (i, D // (NG * NS) + 1)),
                  pl.BlockSpec((CH, CH), lambda i: (i, COL_DT // CH)),
                  pl.BlockSpec((CH, D), lambda i: (i, COL_Z // D)),
                  pl.BlockSpec((8, CH), lambda i: (0, 0)), pl.BlockSpec((CH, D), lambda i: (0, 0)),
                  pl.BlockSpec((CH, CH), lambda i: (0, 0)), pl.BlockSpec((1, D), lambda i: (0, 0)),
                  pl.BlockSpec(memory_space=pl.ANY)],
        out_specs=[pl.BlockSpec((CH, D), lambda i: (i, 1)), pl.BlockSpec((CH, D), lambda i: (i, 0)),
                   pl.BlockSpec((1, NS, D), lambda i: (i, 0, 0))],
        out_shape=[_sds((t, 2 * D), bf16), _sds((t, D), f32), _sds((nb, NS, D), f32)],
        input_output_aliases={9: 0},
        scratch_shapes=[pltpu.VMEM((NS, D), f32), pltpu.VMEM((CH, D), f32)],
        compiler_params=_cparams(1))(xbcs, xbcs, xbcs, proj, proj, par, expand, tri, gs, cat)


def ssd_bwd(xbcs, proj, ypre, states, dcat, par, gs, dproj, seq, name):
    t = xbcs.shape[0]
    nc = seq // CH
    expand, tri = _ssd_consts()

    def body(xs_ref, b_ref, c_ref, dtr_ref, z_ref, y_ref, st_ref, dyn_ref, par_ref, e_ref, tri_ref, gs_ref, dproj_ref,
             dx_ref, dz_ref, ddt_ref, dpar_ref, dgs_ref, dp_scr, dxdt_scr):
        i = pl.program_id(0)

        @pl.when(i % nc == 0)
        def _():
            dp_scr[...] = jnp.zeros_like(dp_scr)

        c = _ssd_common(par_ref, dtr_ref, e_ref, tri_ref)
        e = c["e"]
        lane = c["lane"]
        sub = lax.broadcasted_iota(jnp.int32, (CH, CH), 0)
        xs = xs_ref[...]
        xdt = xs * c["dt_x"]
        xdt_b = xdt.astype(bf16)
        xdte_b = (xdt * c["dte_x"]).astype(bf16)
        p = st_ref[0]
        p_b = p.astype(bf16)
        dpn = dp_scr[...]
        dpn_b = dpn.astype(bf16)

        y, z, gs_v = y_ref[...], z_ref[...], gs_ref[...]
        zs = _sigmoid(z)
        zg = z * zs
        yg = y * zg
        parts, gparts = [], []
        for k in range(NG):
            sl = slice(k * GW, (k + 1) * GW)
            dxk, dgk = _rms_bwd(yg[:, sl], gs_v[:, sl], dyn_ref[:, sl])
            parts.append(dxk)
            gparts.append(dgk)
        dyg = jnp.concatenate(parts, axis=1)
        dgs_rows = jnp.concatenate(gparts, axis=1)
        dy = dyg * zg
        dz_ref[...] = (dyg * y * (zs * (1.0 + z * (1.0 - zs)))).astype(bf16)
        dy_b = dy.astype(bf16)
        dq_b = (dy * c["ecs_x"]).astype(bf16)

        lo = lane < HP
        dcs = jnp.zeros((CH, CH), f32)
        dcst = jnp.zeros((CH, CH), f32)
        for g in range(NG):
            gsl = slice(g * GW, (g + 1) * GW)
            bg = b_ref[:, g * NS:(g + 1) * NS].astype(bf16)
            cg = c_ref[:, g * NS:(g + 1) * NS].astype(bf16)
            gmat = _dot_nt(cg, bg)
            dgm = jnp.zeros((CH, CH), f32)
            for q in range(GW // CH):
                col = g * GW + q * CH
                xp = xdt_b[:, col:col + CH]
                dyp = dy_b[:, col:col + CH]
                acc = None
                for hh in range(2):
                    h = col // HP + hh
                    keep = lo if hh == 0 else jnp.logical_not(lo)
                    dyh = jnp.where(keep, dyp, jnp.zeros_like(dyp))
                    dec = _decay_matrix(c, h)
                    m = gmat * dec
                    dm = _dot_nt(dyh, xp)
                    dseg = dm * m
                    dcs = dcs + jnp.where(lane == h, jnp.sum(dseg, axis=1, keepdims=True), 0.0)
                    dcst = dcst + jnp.where(sub == h, jnp.sum(dseg, axis=0, keepdims=True), 0.0)
                    dgm = dgm + dm * dec
                    term = _dot_tn(m.astype(bf16), dyh)
                    acc = term if acc is None else acc + term
                dxdt_scr[:, col:col + CH] = acc
            dgm_b = dgm.astype(bf16)
            bds = _dot(bg, dpn_b[:, gsl])
            dxdt_scr[:, gsl] = dxdt_scr[:, gsl] + c["dte_x"][:, gsl] * bds
            dc_g = _dot(dgm_b, bg) + _dot_nt(dq_b[:, gsl], p_b[:, gsl])
            db_g = _dot_tn(dgm_b, cg) + _dot_nt(xdte_b[:, gsl], dpn_b[:, gsl])
            dx_ref[:, D + g * NS:D + (g + 1) * NS] = db_g
            dx_ref[:, D + NG * NS + g * NS:D + NG * NS + (g + 1) * NS] = dc_g
            dp_scr[:, gsl] = dpn[:, gsl] * c["ecl_x"][:, gsl] + _dot_tn(cg, dq_b[:, gsl])
            q_g = _dot(cg, p_b[:, gsl])
            e_g = e[:, gsl]
            dcs = dcs + c["ecs"] * _split_dot(dy[:, gsl] * q_g, e_g, 2, nt=True)
            ddte = _split_dot(xdt[:, gsl] * bds, e_g, 2, nt=True) * c["dte"]
            dcs = dcs - ddte
            dcs = dcs + jnp.where(sub == CH - 1, jnp.sum(ddte, axis=0, keepdims=True), 0.0)

        decl = _split_dot(jnp.broadcast_to(jnp.sum(dpn * p, axis=0, keepdims=True), (8, D)), e, 2, nt=True)[0:1]
        dcs = dcs + jnp.where(sub == CH - 1, c["ecl"] * decl, 0.0)
        dcs = dcs - dcst.T
        dadt = lax.dot_general(tri_ref[...], dcs, (((0,), (0,)), ((), ())), precision=lax.Precision.HIGHEST,
                               preferred_element_type=f32)
        dxdt = dxdt_scr[...]
        ddt = dadt * c["a"] + _split_dot(dxdt * xs, e, 2, nt=True)
        ddtr = jnp.where(lane < NH, ddt * _sigmoid(c["dtr"]), 0.0)
        ddt_ref[...] = ddtr.astype(bf16)
        dx_ref[:, 0:D] = dxdt * c["dt_x"] + c["dsk_x"] * dy
        dsk = _split_dot(jnp.broadcast_to(jnp.sum(dy * xs, axis=0, keepdims=True), (8, D)), e, 2, nt=True)[0:1]
        dalog = jnp.sum(dadt * c["dt"], axis=0, keepdims=True) * c["a"]
        row8 = lax.broadcasted_iota(jnp.int32, (8, CH), 0)
        dpar = jnp.where(row8 == 0, jnp.sum(ddtr, axis=0, keepdims=True),
                         jnp.where(row8 == 1, dalog, jnp.where(row8 == 2, dsk, 0.0)))
        dpar = jnp.where(lax.broadcasted_iota(jnp.int32, (8, CH), 1) < NH, dpar, 0.0)
        _accum(dpar_ref, dpar, i == 0)
        _accum(dgs_ref, jnp.sum(dgs_rows, axis=0, keepdims=True), i == 0)

    nb = t // CH
    rev = lambda i: (i // nc) * nc + (nc - 1 - i % nc)
    return pl.pallas_call(
        body, name=name, grid=(nb,),
        in_specs=[pl.BlockSpec((CH, D), lambda i: (rev(i), 0)),
                  pl.BlockSpec((CH, NG * NS), lambda i: (rev(i), D // (NG * NS))),
                  pl.BlockSpec((CH, NG * NS), lambda i: (rev(i), D // (NG * NS) + 1)),
                  pl.BlockSpec((CH, CH), lambda i: (rev(i), COL_DT // CH)),
                  pl.BlockSpec((CH, D), lambda i: (rev(i), COL_Z // D)),
                  pl.BlockSpec((CH, D), lambda i: (rev(i), 0)),
                  pl.BlockSpec((1, NS, D), lambda i: (rev(i), 0, 0)),
                  pl.BlockSpec((CH, D), lambda i: (rev(i), 1)),
                  pl.BlockSpec((8, CH), lambda i: (0, 0)), pl.BlockSpec((CH, D), lambda i: (0, 0)),
                  pl.BlockSpec((CH, CH), lambda i: (0, 0)), pl.BlockSpec((1, D), lambda i: (0, 0)),
                  pl.BlockSpec(memory_space=pl.ANY)],
        out_specs=[pl.BlockSpec((CH, XBC), lambda i: (rev(i), 0)), pl.BlockSpec((CH, D), lambda i: (rev(i), COL_Z // D)),
                   pl.BlockSpec((CH, CH), lambda i: (rev(i), 0)),
                   pl.BlockSpec((8, CH), lambda i: (0, 0)), pl.BlockSpec((1, D), lambda i: (0, 0))],
        out_shape=[_sds((t, XBC), f32), _sds((t, PROJ), bf16), _sds((t, CH), bf16), _sds((8, CH), f32), _sds((1, D), f32)],
        input_output_aliases={12: 1},
        scratch_shapes=[pltpu.VMEM((NS, D), f32), pltpu.VMEM((CH, D), f32)],
        compiler_params=_cparams(1))(xbcs, xbcs, xbcs, proj, proj, ypre, states, dcat, par, expand, tri, gs, dproj)


def loss_head(y, target, tb, name):
    t = y.shape[0]

    def body(y_ref, t_ref, s_ref, dy_ref):
        err = y_ref[...] - t_ref[...]
        dy_ref[...] = err * (1.0 / D)
        _accum(s_ref, jnp.zeros((8, CH), f32) + jnp.sum(err * err), pl.program_id(0) == 0)

    return pl.pallas_call(
        body, name=name, grid=(t // tb,),
        in_specs=[pl.BlockSpec((tb, D), lambda i: (i, 0)), pl.BlockSpec((tb, D), lambda i: (i, 0))],
        out_specs=[pl.BlockSpec((8, CH), lambda i: (0, 0)), pl.BlockSpec((tb, D), lambda i: (i, 0))],
        out_shape=[_sds((8, CH), f32), _sds((t, D), f32)],
        compiler_params=_cparams(1))(y, target)


def _tiles(t, seq):
    tm = min(512, t)
    return dict(tm=tm, tm_small=min(256, t), tb=min(512, seq), tb_conv=min(512, seq))


def local_step(x, target, depth, weights_of, seq, grads_done=None):
    t = x.shape[0]
    ts = _tiles(t, seq)
    tm, tb = ts["tm"], ts["tb"]
    saved, ws = [], []
    for l in range(depth):
        w = weights_of(l, x)
        ws.append(w)
        proj, h1 = norm_matmul(x, w["g1"], w["win"], tm, 1152, f32, "in_proj", token=w.get("token"))
        cat = group_a_fwd(proj, w["wa"], w["ga"], seq, tb, "group_a_fwd")
        xbcs = conv_b_fwd(proj, w["ws"], w["bs"], seq, tb, "conv_b_fwd")
        cat, ypre, states = ssd_fwd(xbcs, proj, w["par"], w["gs"], cat, seq, "ssd_fwd")
        mix, x2 = matmul_postnorm(cat, w["wo"], x, w["g2"], tm, False, "out_proj")
        fp, h2 = norm_matmul(x2, w["g3"], w["wu"], tm, 1024, bf16, "mlp_up")
        o, x3 = matmul_postnorm(fp, w["wd"], x2, w["g4"], tm, True, "mlp_down")
        saved.append(dict(x=x, proj=proj, h1=h1, xbcs=xbcs, ypre=ypre, states=states, cat=cat, mix=mix, x2=x2,
                          fp=fp, h2=h2, o=o))
        x = x3
    sse, dx = loss_head(x, target, tm, "loss_head")
    grads = [None] * depth
    token = None
    for l in reversed(range(depth)):
        s, w = saved[l], ws[l]
        do, dg4, dfp = postnorm_bwd_matmul(s["o"], w["g4"], dx, w["wd"], s["fp"], tm, 1024, bf16, "mlp_down_bwd",
                                           token=token)
        dwd = matmul_tn(s["fp"], do, 512, 1024, True, "mlp_down_dw")
        dx2, dg3 = matmul_prenorm_bwd(dfp, w["wu"], s["x2"], w["g3"], dx, tm, "mlp_up_bwd")
        dwu = matmul_tn(s["h2"], dfp, 512, 1024, False, "mlp_up_dw", col_blocks=True)
        dmix, dg2, dcat = postnorm_bwd_matmul(s["mix"], w["g2"], dx2, w["wo"], None, tm, 1024, f32, "out_proj_bwd")
        dwo = matmul_tn(s["cat"], dmix, 512, 1024, False, "out_proj_dw")
        dproj, dwa, dga = group_a_bwd(s["proj"], dcat, w["wa"], w["ga"], seq, tb, "group_a_bwd")
        dxbcs, dproj, ddt, dpar, dgs = ssd_bwd(s["xbcs"], s["proj"], s["ypre"], s["states"], dcat, w["par"], w["gs"],
                                               dproj, seq, "ssd_bwd")
        dproj, dws, dbs = conv_b_bwd(s["proj"], dxbcs, w["ws"], w["bs"], dproj, seq, tb, "conv_b_bwd")
        dproj = place_columns(dproj, ddt, COL_DT // CH, tm, "place_ddt")
        dx, dg1 = matmul_prenorm_bwd(dproj, w["win"], s["x"], w["g1"], dx2, ts["tm_small"], "in_proj_bwd")
        dwin = matmul_tn(s["h1"], dproj, 512, 1152, False, "in_proj_dw")
        grads[l] = dict(win=dwin, wo=dwo, wu=dwu, wd=dwd, wa=dwa, ws=dws, bs=dbs, par=dpar,
                        g1=dg1, ga=dga, gs=dgs, g2=dg2, g3=dg3, g4=dg4)
        if grads_done is not None:
            token = grads_done(l, grads[l], dx)
    return sse, dx, grads


GROUPS = {
    "chips": [(1, 0, 0), (0, 1, 0), (1, 1, 0)],
    "pair": [(0, 0, 1)],
    "all": [(1, 0, 0), (0, 1, 0), (1, 1, 0), (0, 0, 1), (1, 0, 1), (0, 1, 1), (1, 1, 1)],
}


def _group_index(group, x, y, c):
    return {"chips": 2 * x + y, "pair": c, "all": 4 * x + 2 * y + c}[group]


def _chunk_indices(shape, pieces):
    if len(shape) < 3:
        return [()]
    lead = [()]
    for n in shape[:-2]:
        lead = [i + (k,) for i in lead for k in range(n)]
    rows = shape[-2]
    split = max(1, pieces // len(lead))
    while split > 1 and (rows % split or (rows // split) % 16):
        split -= 1
    step = rows // split
    return [i + (pl.ds(s * step, step),) for i in lead for s in range(split)]


def _exchange(arrays, out_shapes, group, src_view, dst_view, view_shape, name, own, pieces=16):
    masks = GROUPS[group]
    na, nm = len(arrays), len(masks)
    cuts = [_chunk_indices(view_shape(a), pieces) for a in range(na)]

    def body(*refs):
        ins, outs = refs[:na], refs[na:2 * na]
        send_sems, recv_sems = refs[2 * na:2 * na + 2]
        local_sems = refs[2 * na + 2] if own else None
        x, y, c = lax.axis_index("x"), lax.axis_index("y"), lax.axis_index("c")
        me = _group_index(group, x, y, c)
        peers = []
        for mx, my, mc in masks:
            px, py, pc = (1 - x if mx else x), (1 - y if my else y), (1 - c if mc else c)
            peers.append(((px, py, pc), _group_index(group, px, py, pc)))

        def part(ref, idx):
            return ref.at[idx] if idx else ref

        if own:
            for a in range(na):
                for idx in cuts[a]:
                    pltpu.make_async_copy(part(src_view(ins[a], a, me), idx), part(dst_view(outs[a], a, me), idx),
                                          local_sems.at[a]).start()
        for a in range(na):
            for j, (dev, pidx) in enumerate(peers):
                for idx in cuts[a]:
                    pltpu.make_async_remote_copy(
                        src_ref=part(src_view(ins[a], a, pidx), idx), dst_ref=part(dst_view(outs[a], a, me), idx),
                        send_sem=send_sems.at[a * nm + j], recv_sem=recv_sems.at[a * nm + j],
                        device_id=dev, device_id_type=MESH).start()
        whole = []
        for a in range(na):
            for j, (dev, pidx) in enumerate(peers):
                whole.append(pltpu.make_async_remote_copy(
                    src_ref=src_view(ins[a], a, pidx), dst_ref=dst_view(outs[a], a, pidx),
                    send_sem=send_sems.at[a * nm + j], recv_sem=recv_sems.at[a * nm + j],
                    device_id=dev, device_id_type=MESH))
        for cp in whole:
            cp.wait_recv()
        for cp in whole:
            cp.wait_send()
        if own:
            for a in range(na):
                pltpu.make_async_copy(src_view(ins[a], a, me), dst_view(outs[a], a, me), local_sems.at[a]).wait()

    hbm = pl.BlockSpec(memory_space=pltpu.HBM)
    sems = [pltpu.SemaphoreType.DMA((na * nm,)), pltpu.SemaphoreType.DMA((na * nm,))]
    return pl.pallas_call(
        body, name=name, in_specs=[hbm] * na, out_specs=[hbm] * na,
        out_shape=[_sds(s, a.dtype) for s, a in zip(out_shapes, arrays)],
        scratch_shapes=sems + ([pltpu.SemaphoreType.DMA((na,))] if own else []))(*arrays)


def all_gather(arrays, group, name, slot_axis=0, own=True):
    n = len(GROUPS[group]) + 1
    shapes = [a.shape[:slot_axis] + (n,) + a.shape[slot_axis:] for a in arrays]
    lead = (slice(None),) * slot_axis
    return _exchange(arrays, shapes, group, lambda r, a, i: r, lambda r, a, i: r.at[lead + (i,)],
                     lambda a: arrays[a].shape, name, own)


HBM_SPEC = pl.BlockSpec(memory_space=pltpu.HBM)
SEM_SPEC = pl.BlockSpec(memory_space=pltpu.SEMAPHORE)
DATAFLOW = pltpu.SideEffectType.DATAFLOW_SIDE_EFFECTING
N_CHIPS = 4


def _chip_peers(x, y, c):
    out = []
    for mx, my, _ in GROUPS["chips"]:
        px, py = (1 - x if mx else x), (1 - y if my else y)
        out.append(((px, py, c), 2 * px + py))
    return out


def _weight_views(shards):
    half = [s.shape[0] // 2 for s in shards]
    return dict(src=lambda ref, a, c, to_chip: ref.at[pl.ds(c * half[a], half[a])],
                dst=lambda ref, a, c, from_chip: ref.at[from_chip, pl.ds(c * half[a], half[a])],
                rows=lambda a: half[a])


def _grad_views(sums):
    return dict(src=lambda ref, a, c, to_chip: ref.at[to_chip], dst=lambda ref, a, c, from_chip: ref.at[from_chip],
                rows=lambda a: sums[a].shape[1])


def chips_start(sources, zones, views, name, pieces=4):
    na, nm = len(sources), N_CHIPS - 1

    def body(*refs):
        ins, lands = refs[:na], refs[na:2 * na]
        send_sems, recv_sems, token = refs[2 * na], refs[2 * na + 1], refs[-1]
        x, y, c = lax.axis_index("x"), lax.axis_index("y"), lax.axis_index("c")
        chip = 2 * x + y
        for a in range(na):
            step = views["rows"](a) // pieces
            for j, (dev, to_chip) in enumerate(_chip_peers(x, y, c)):
                for q in range(pieces):
                    rows = pl.ds(q * step, step)
                    pltpu.make_async_remote_copy(
                        src_ref=views["src"](ins[a], a, c, to_chip).at[rows],
                        dst_ref=views["dst"](lands[a], a, c, chip).at[rows],
                        send_sem=send_sems.at[a * nm + j], recv_sem=recv_sems.at[a * nm + j],
                        device_id=dev, device_id_type=MESH).start()
        token[...] = jnp.zeros_like(token)

    both = list(sources) + list(zones)
    outs = pl.pallas_call(
        body, name=name,
        out_shape=(pltpu.SemaphoreType.DMA((na * nm,)), pltpu.SemaphoreType.DMA((na * nm,)),
                   *[pltpu.HBM(b.shape, b.dtype) for b in both], _sds((8, CH), f32)),
        in_specs=[HBM_SPEC] * (2 * na),
        out_specs=(SEM_SPEC, SEM_SPEC, *[HBM_SPEC] * (2 * na), pl.BlockSpec(memory_space=pltpu.VMEM)),
        input_output_aliases={i: 2 + i for i in range(2 * na)},
        compiler_params=pltpu.CompilerParams(has_side_effects=DATAFLOW))(
            *[pltpu.with_memory_space_constraint(b, pltpu.HBM) for b in both])
    return dict(send=outs[0], recv=outs[1], sources=list(outs[2:2 + na]), zones=list(outs[2 + na:2 + 2 * na]),
                token=outs[-1], views=views)


def chips_wait(started, after, name):
    sources, zones, views = started["sources"], started["zones"], started["views"]
    na, nm = len(sources), N_CHIPS - 1

    def body(*refs):
        ins, lands = refs[:na], refs[na:2 * na]
        send_sems, recv_sems = refs[2 * na], refs[2 * na + 1]
        x, y, c = lax.axis_index("x"), lax.axis_index("y"), lax.axis_index("c")
        for a in range(na):
            for j, (dev, peer_chip) in enumerate(_chip_peers(x, y, c)):
                cp = pltpu.make_async_remote_copy(
                    src_ref=views["src"](ins[a], a, c, peer_chip), dst_ref=views["dst"](lands[a], a, c, peer_chip),
                    send_sem=send_sems.at[a * nm + j], recv_sem=recv_sems.at[a * nm + j],
                    device_id=dev, device_id_type=MESH)
                cp.wait_send()
                cp.wait_recv()

    both = list(sources) + list(zones)
    outs = pl.pallas_call(
        body, name=name, out_shape=tuple(pltpu.HBM(b.shape, b.dtype) for b in both),
        in_specs=[HBM_SPEC] * (2 * na) + [SEM_SPEC, SEM_SPEC, pl.BlockSpec(memory_space=pl.ANY)],
        out_specs=tuple([HBM_SPEC] * (2 * na)), input_output_aliases={i: i for i in range(2 * na)},
        compiler_params=pltpu.CompilerParams(has_side_effects=DATAFLOW))(*both, started["send"], started["recv"], after)
    return list(outs[:na]), list(outs[na:])


def weights_share(zones, name):
    na, nm = len(zones), N_CHIPS - 1

    def body(*refs):
        lands = refs[na:2 * na]
        send_sems, recv_sems = refs[2 * na:]
        x, y, c = lax.axis_index("x"), lax.axis_index("y"), lax.axis_index("c")
        chip = 2 * x + y
        sibling = (x, y, 1 - c)
        sends = []
        for a in range(na):
            half = zones[a].shape[1] // 2
            for m in range(1, N_CHIPS):
                mine = lands[a].at[chip ^ m, pl.ds(c * half, half)]
                sends.append(pltpu.make_async_remote_copy(
                    src_ref=mine, dst_ref=mine, send_sem=send_sems.at[a * nm + m - 1],
                    recv_sem=recv_sems.at[a * nm + m - 1], device_id=sibling, device_id_type=MESH))
        for cp in sends:
            cp.start()
        for a in range(na):
            half = zones[a].shape[1] // 2
            for m in range(1, N_CHIPS):
                theirs = lands[a].at[chip ^ m, pl.ds((1 - c) * half, half)]
                pltpu.make_async_remote_copy(
                    src_ref=theirs, dst_ref=theirs, send_sem=send_sems.at[a * nm + m - 1],
                    recv_sem=recv_sems.at[a * nm + m - 1], device_id=sibling, device_id_type=MESH).wait_recv()
        for cp in sends:
            cp.wait_send()

    return pl.pallas_call(
        body, name=name, in_specs=[HBM_SPEC] * na, out_specs=[HBM_SPEC] * na,
        out_shape=[_sds(z.shape, z.dtype) for z in zones], input_output_aliases={i: i for i in range(na)},
        scratch_shapes=[pltpu.SemaphoreType.DMA((na * nm,)), pltpu.SemaphoreType.DMA((na * nm,))])(*zones)


def pair_send_halves(grads, name):
    half = [g.shape[1] // 2 for g in grads]
    shapes = [(g.shape[0], h, g.shape[2]) for g, h in zip(grads, half)]
    return _exchange(grads, shapes, "pair", lambda r, a, i: r.at[:, pl.ds(i * half[a], half[a])],
                     lambda r, a, i: r, lambda a: shapes[a], name, False)


def sum_pair_half(g, recv, core, name, tb=256):
    nk, r, c = g.shape
    tb = min(tb, r // 2)
    nb = r // 2 // tb

    def body(core_ref, g_ref, r_ref, o_ref):
        o_ref[...] = (g_ref[...] + r_ref[...]).astype(bf16)

    return pl.pallas_call(
        body, name=name,
        grid_spec=pltpu.PrefetchScalarGridSpec(
            num_scalar_prefetch=1, grid=(nk, nb),
            in_specs=[pl.BlockSpec((None, tb, c), lambda k, i, core_ref: (k, core_ref[0] * nb + i, 0)),
                      pl.BlockSpec((None, tb, c), lambda k, i, core_ref: (k, i, 0))],
            out_specs=pl.BlockSpec((None, tb, c), lambda k, i, core_ref: (k, i, 0))),
        out_shape=_sds((nk, r // 2, c), bf16), compiler_params=_cparams(2))(
            jnp.reshape(core, (1,)).astype(jnp.int32), g, recv)


def chip_sum_into(acc, layer, own, others, chip, name, tb=256):
    n, r, c = own.shape
    tb = min(tb, r)

    def body(chip_ref, x_ref, y1_ref, y2_ref, y3_ref, acc_ref, o_ref):
        o_ref[...] = ((x_ref[...].astype(f32) + y1_ref[...].astype(f32)) + y2_ref[...].astype(f32)) + y3_ref[...].astype(f32)

    def slot(k):
        return pl.BlockSpec((None, tb, c), lambda i, chip_ref: (chip_ref[0] ^ k, i, 0))

    return pl.pallas_call(
        body, name=name,
        grid_spec=pltpu.PrefetchScalarGridSpec(
            num_scalar_prefetch=1, grid=(r // tb,),
            in_specs=[slot(k) for k in range(n)] + [pl.BlockSpec(memory_space=pl.ANY)],
            out_specs=pl.BlockSpec((None, tb, c), lambda i, chip_ref: (layer, i, 0))),
        out_shape=_sds(acc.shape, f32), input_output_aliases={n + 1: 0}, compiler_params=_cparams(1))(
            jnp.reshape(chip, (1,)).astype(jnp.int32), own, *([others] * (n - 1)), acc)


def adamw_halves(w, g_own, g_recv, m, v, core, name, tb=256):
    depth, r, c = w.shape
    tb = min(tb, r // 2)
    nb = r // 2 // tb

    def body(core_ref, w_ref, go_ref, gr_ref, m_ref, v_ref, g_ref, d_ref, mo_ref, vo_ref):
        gv = jnp.where(pl.program_id(1) == core_ref[0], go_ref[...], gr_ref[...])
        m2 = B1 * m_ref[...] + (1.0 - B1) * gv
        v2 = B2 * v_ref[...] + (1.0 - B2) * (gv * gv)
        m_hat = m2 / (1.0 - B1 ** STEP)
        v_hat = v2 / (1.0 - B2 ** STEP)
        g_ref[...] = gv
        d_ref[...] = -LR * (m_hat / (jnp.sqrt(v_hat) + AEPS) + WD * w_ref[...])
        mo_ref[...] = m2
        vo_ref[...] = v2

    whole = pl.BlockSpec((None, tb, c), lambda l, h, i, core_ref: (l, h * nb + i, 0))
    part = pl.BlockSpec((None, tb, c), lambda l, h, i, core_ref: (l, i, 0))
    return pl.pallas_call(
        body, name=name,
        grid_spec=pltpu.PrefetchScalarGridSpec(num_scalar_prefetch=1, grid=(depth, 2, nb),
                                               in_specs=[whole, part, part, whole, whole], out_specs=[whole] * 4),
        out_shape=[_sds(w.shape, f32)] * 4, compiler_params=_cparams(3))(
            jnp.reshape(core, (1,)).astype(jnp.int32), w, g_own, g_recv, m, v)


def sum_slots(y, out_dtype, name, tb=256):
    n, r, c = y.shape
    tb = min(tb, r)

    def body(y_ref, o_ref):
        acc = y_ref[0].astype(f32)
        for i in range(1, n):
            acc = acc + y_ref[i].astype(f32)
        o_ref[...] = acc.astype(out_dtype)

    return pl.pallas_call(
        body, name=name, grid=(r // tb,),
        in_specs=[pl.BlockSpec((n, tb, c), lambda i: (0, i, 0))], out_specs=pl.BlockSpec((tb, c), lambda i: (i, 0)),
        out_shape=_sds((r, c), out_dtype), compiler_params=_cparams(1))(y)


def adamw(w, g, m, v, name, tb=256):
    r, c = w.shape
    tb = min(tb, r)

    def body(w_ref, g_ref, m_ref, v_ref, d_ref, mo_ref, vo_ref):
        gv = g_ref[...]
        m2 = B1 * m_ref[...] + (1.0 - B1) * gv
        v2 = B2 * v_ref[...] + (1.0 - B2) * (gv * gv)
        m_hat = m2 / (1.0 - B1 ** STEP)
        v_hat = v2 / (1.0 - B2 ** STEP)
        d_ref[...] = -LR * (m_hat / (jnp.sqrt(v_hat) + AEPS) + WD * w_ref[...])
        mo_ref[...] = m2
        vo_ref[...] = v2

    spec = pl.BlockSpec((tb, c), lambda i: (i, 0))
    return pl.pallas_call(
        body, name=name, grid=(r // tb,), in_specs=[spec] * 4, out_specs=[spec] * 3,
        out_shape=[_sds((r, c), f32)] * 3, compiler_params=_cparams(1))(w, g, m, v)


def adamw_leading(w, g, m, v, name, tc=64):
    c, l, r = w.shape
    main = c // tc
    tail = c - main * tc

    def body(w_ref, g_ref, m_ref, v_ref, *rest):
        d_ref, mo_ref, vo_ref = rest[-3:]
        gv = g_ref[...]
        m2 = B1 * m_ref[...] + (1.0 - B1) * gv
        v2 = B2 * v_ref[...] + (1.0 - B2) * (gv * gv)
        m_hat = m2 / (1.0 - B1 ** STEP)
        v_hat = v2 / (1.0 - B2 ** STEP)
        d_ref[...] = -LR * (m_hat / (jnp.sqrt(v_hat) + AEPS) + WD * w_ref[...])
        mo_ref[...] = m2
        vo_ref[...] = v2

    spec = pl.BlockSpec((tc, l, r), lambda i: (i, 0, 0))
    outs = pl.pallas_call(
        functools.partial(body), name=name, grid=(main,), in_specs=[spec] * 4, out_specs=[spec] * 3,
        out_shape=[_sds(w.shape, f32)] * 3, compiler_params=_cparams(1))(w, g, m, v)
    if tail:
        assert (main * tc) % tail == 0
        last = pl.BlockSpec((tail, l, r), lambda i: (main * tc // tail, 0, 0))
        outs = pl.pallas_call(
            functools.partial(body), name=name + "_tail", grid=(1,),
            in_specs=[last] * 4 + [pl.BlockSpec(memory_space=pl.ANY)] * 3, out_specs=[last] * 3,
            out_shape=[_sds(w.shape, f32)] * 3, input_output_aliases={4: 0, 5: 1, 6: 2},
            compiler_params=_cparams(1))(w, g, m, v, *outs)
    return outs


SMALL_ROW = 1024
SMALL_GAINS = ("g1", "ga", "gs", "g2", "g3", "g4")
SMALL_LAYER_ROWS = 8 + 8 + 16 + 8


def _pack_small(grads):
    wide = lambda a: jnp.pad(a, ((0, 0), (0, 2 * SMALL_ROW - a.shape[1]))).reshape(-1, SMALL_ROW)
    row = lax.broadcasted_iota(jnp.int32, (8, SMALL_ROW), 0)
    parts = []
    for g in grads:
        singles = [g[k] for k in SMALL_GAINS] + [g["bs"][:, :SMALL_ROW],
                                                 jnp.pad(g["bs"][:, SMALL_ROW:], ((0, 0), (0, 2 * SMALL_ROW - XBC)))]
        first = sum(jnp.where(row == k, s, 0.0) for k, s in enumerate(singles))
        parts += [first, g["wa"], wide(g["ws"]), jnp.pad(g["par"], ((0, 0), (0, SMALL_ROW - CH)))]
    return jnp.concatenate(parts, axis=0)


def _unpack_small(packed, depth):
    rows = packed.reshape(depth, SMALL_LAYER_ROWS, SMALL_ROW)
    out = {k: rows[:, i] for i, k in enumerate(SMALL_GAINS)}
    out["bs"] = rows[:, 6:8].reshape(depth, 2 * SMALL_ROW)[:, :XBC]
    out["wa"] = rows[:, 8:11]
    out["ws"] = rows[:, 16:32].reshape(depth, 8, 2 * SMALL_ROW)[:, :4, :XBC]
    out["par"] = rows[:, 32:35, :CH]
    return out


def kernel(x, norm_mix_pre, w_in, conv_a_w, ssm_conv_w, ssm_conv_b, dt_bias, a_log, d_skip, conv_out_norm, ssm_out_norm, w_out, norm_mix_post, norm_mlp_pre, w_up, w_down, norm_mlp_post, loss_target, m_norm_mix_pre, m_w_in, m_conv_a_w, m_ssm_conv_w, m_ssm_conv_b, m_dt_bias, m_a_log, m_d_skip, m_conv_out_norm, m_ssm_out_norm, m_w_out, m_norm_mix_post, m_norm_mlp_pre, m_w_up, m_w_down, m_norm_mlp_post, v_norm_mix_pre, v_w_in, v_conv_a_w, v_ssm_conv_w, v_ssm_conv_b, v_dt_bias, v_a_log, v_d_skip, v_conv_out_norm, v_ssm_out_norm, v_w_out, v_norm_mix_post, v_norm_mlp_pre, v_w_up, v_w_down, v_norm_mlp_post):
    nb, seq, _ = x.shape
    t = nb * seq
    depth = w_in.shape[0]
    ncol = w_in.shape[2]
    chip = 2 * lax.axis_index("x") + lax.axis_index("y")

    taps = [conv_a_w, ssm_conv_w]
    taps_g = all_gather(taps, "chips", "gather_taps", slot_axis=1, own=False)
    wa_g, ws_g = [lax.dynamic_update_index_in_dim(g, s, chip, 1) for g, s in zip(taps_g, taps)]
    wa_full = jnp.transpose(wa_g, (0, 2, 1, 3)).reshape(depth, 3, D)
    ws_full = jnp.transpose(ws_g, (0, 2, 1, 3)).reshape(depth, 4, XBC)
    lane_pad = lambda a: jnp.pad(a, ((0, 0), (0, CH - a.shape[1])))
    par = jnp.stack([lane_pad(dt_bias), lane_pad(a_log), lane_pad(d_skip)], axis=1)
    par = jnp.pad(par, ((0, 0), (0, 5), (0, 0)))

    def start(l):
        shards = [w_in[l].astype(bf16), w_out[l].astype(bf16), w_up[l].astype(bf16), w_down[l].astype(bf16)]
        zones = [lax.empty((N_CHIPS,) + s.shape, s.dtype) for s in shards]
        return chips_start(shards, zones, _weight_views(shards), f"weights_start_{l}")

    travelling = {0: start(0)}

    def weights_of(l, x_in):
        shards, zones = chips_wait(travelling.pop(l), x_in, f"weights_wait_{l}")
        zones = weights_share(zones, "weights_share")
        win_z, wo_z, wu_z, wd_z = [lax.dynamic_update_index_in_dim(z, s, chip, 0) for z, s in zip(zones, shards)]
        token = None
        if l + 1 < depth:
            travelling[l + 1] = start(l + 1)
            token = travelling[l + 1]["token"]
        win_full = jnp.pad(jnp.transpose(win_z, (1, 0, 2)).reshape(D, N_CHIPS * ncol),
                           ((0, 0), (0, PROJ - N_CHIPS * ncol)))
        return dict(win=win_full, wo=wo_z.reshape(2 * D, D), wu=wu_z, wd=wd_z.reshape(DFF, D),
                    wa=jnp.pad(wa_full[l], ((0, 5), (0, 0))), ws=jnp.pad(ws_full[l], ((0, 4), (0, 0))),
                    bs=ssm_conv_b[l][None], par=par[l], g1=norm_mix_pre[l][None], ga=conv_out_norm[l][None],
                    gs=ssm_out_norm[l][None], g2=norm_mix_post[l][None], g3=norm_mlp_pre[l][None],
                    g4=norm_mlp_post[l][None], token=token)

    core = lax.axis_index("c")
    grads_travelling = {}

    def grads_done(l, g, dx_l):
        mats = [jnp.transpose(g["win"][:, :N_CHIPS * ncol].reshape(D, N_CHIPS, ncol), (1, 0, 2)),
                g["wo"].reshape(N_CHIPS, 2 * D // N_CHIPS, D), g["wu"], g["wd"].reshape(N_CHIPS, DFF // N_CHIPS, D)]
        received = pair_send_halves(mats, "grads_to_pair")
        sums = [sum_pair_half(m_, r_, core, "pair_sum") for m_, r_ in zip(mats, received)]
        zones = [lax.empty(s.shape, s.dtype) for s in sums]
        grads_travelling[l] = chips_start(sums, zones, _grad_views(sums), f"grads_start_{l}")
        return grads_travelling[l]["token"]

    sse, dx, grads = local_step(x.reshape(t, D), loss_target.reshape(t, D), depth, weights_of, seq, grads_done)
    loss = lax.psum(0.5 / D * sse[0, 0], ("x", "y", "c"))

    big_w = [w_in, w_out, w_up, w_down]
    acc = [lax.empty((depth, bw.shape[1] // 2, bw.shape[2]), f32) for bw in big_w]
    for l in reversed(range(depth)):
        sums, zones = chips_wait(grads_travelling.pop(l), dx, f"grads_wait_{l}")
        acc = [chip_sum_into(acc_a, l, s, z, chip, "chip_sum") for acc_a, s, z in zip(acc, sums, zones)]
    from_sibling = _exchange(acc, [a.shape for a in acc], "pair", lambda r, a, i: r, lambda r, a, i: r,
                             lambda a: acc[a].shape, "grads_from_pair", False)

    small_all = all_gather([_pack_small(grads)], "all", "gather_small")[0]
    small = _unpack_small(sum_slots(small_all, f32, "small_sum", tb=8), depth)
    wa_cols, ws_cols = conv_a_w.shape[2], ssm_conv_w.shape[2]
    par_g = small["par"].reshape(depth, 3, CH)
    g_small = dict(
        norm_mix_pre=small["g1"], conv_out_norm=small["ga"], ssm_out_norm=small["gs"], norm_mix_post=small["g2"],
        norm_mlp_pre=small["g3"], norm_mlp_post=small["g4"], ssm_conv_b=small["bs"],
        conv_a_w=lax.dynamic_slice_in_dim(small["wa"].reshape(depth, 3, D), chip * wa_cols, wa_cols, axis=2),
        ssm_conv_w=lax.dynamic_slice_in_dim(small["ws"].reshape(depth, 4, XBC), chip * ws_cols, ws_cols, axis=2),
        dt_bias=par_g[:, 0, :NH], a_log=par_g[:, 1, :NH], d_skip=par_g[:, 2, :NH])

    given = dict(norm_mix_pre=(norm_mix_pre, m_norm_mix_pre, v_norm_mix_pre), w_in=(w_in, m_w_in, v_w_in),
                 conv_a_w=(conv_a_w, m_conv_a_w, v_conv_a_w), ssm_conv_w=(ssm_conv_w, m_ssm_conv_w, v_ssm_conv_w),
                 ssm_conv_b=(ssm_conv_b, m_ssm_conv_b, v_ssm_conv_b), dt_bias=(dt_bias, m_dt_bias, v_dt_bias),
                 a_log=(a_log, m_a_log, v_a_log), d_skip=(d_skip, m_d_skip, v_d_skip),
                 conv_out_norm=(conv_out_norm, m_conv_out_norm, v_conv_out_norm),
                 ssm_out_norm=(ssm_out_norm, m_ssm_out_norm, v_ssm_out_norm), w_out=(w_out, m_w_out, v_w_out),
                 norm_mix_post=(norm_mix_post, m_norm_mix_post, v_norm_mix_post),
                 norm_mlp_pre=(norm_mlp_pre, m_norm_mlp_pre, v_norm_mlp_pre), w_up=(w_up, m_w_up, v_w_up),
                 w_down=(w_down, m_w_down, v_w_down), norm_mlp_post=(norm_mlp_post, m_norm_mlp_post, v_norm_mlp_post))
    halves = dict(zip(["w_in", "w_out", "w_up", "w_down"], zip(acc, from_sibling)))
    order = ["norm_mix_pre", "w_in", "conv_a_w", "ssm_conv_w", "ssm_conv_b", "dt_bias", "a_log", "d_skip",
             "conv_out_norm", "ssm_out_norm", "w_out", "norm_mix_post", "norm_mlp_pre", "w_up", "w_down",
             "norm_mlp_post"]
    g_out, d_out, m_out, v_out = [], [], [], []
    for n in order:
        wv, mv, vv = given[n]
        if n in halves and wv.shape[-1] % CH:
            own, recv = halves[n]
            gv = jnp.concatenate([jnp.where(core == 0, own, recv), jnp.where(core == 0, recv, own)], axis=1)
            to_cols, to_rows = (lambda a: jnp.transpose(a, (2, 0, 1))), (lambda a: jnp.transpose(a, (1, 2, 0)))
            dlt, m2, v2 = [to_rows(o) for o in adamw_leading(to_cols(wv), to_cols(gv), to_cols(mv), to_cols(vv),
                                                             "adamw_cols")]
        elif n in halves:
            gv, dlt, m2, v2 = adamw_halves(wv, *halves[n], mv, vv, core, "adamw_matrix")
        else:
            gv = g_small[n].reshape(wv.shape)
            two_d = lambda a: a.reshape(-1, a.shape[-1])
            dlt, m2, v2 = adamw(two_d(wv), two_d(gv), two_d(mv), two_d(vv), "adamw")
        g_out.append(gv)
        d_out.append(dlt.reshape(wv.shape))
        m_out.append(m2.reshape(wv.shape))
        v_out.append(v2.reshape(wv.shape))
    return (loss, dx.reshape(nb, seq, D), *g_out, *d_out, *m_out, *v_out)
```

```python
import functools

import jax
import jax.numpy as jnp
from jax import lax
from jax.experimental import pallas as pl
from jax.experimental.pallas import tpu as pltpu

f32, bf16 = jnp.float32, jnp.bfloat16

D = 1024
NH, HP = 16, 64
NG, NS = 2, 128
CH = 128
XBC = D + 2 * NG * NS
DFF = 4 * D
IN_COLS = 3 * D + D + XBC + NH
PROJ = 5760
COL_Z, COL_XBC, COL_DT = 3 * D, 4 * D, 4 * D + XBC
EPS = 1e-6
HALO = 8
VMEM_LIMIT = 56 * 2**20
MESH = pl.DeviceIdType.MESH

LR, B1, B2, AEPS, WD, STEP = 0.001, 0.9, 0.999, 1e-08, 0.01, 10


def _cparams(n_axes):
    return pltpu.CompilerParams(dimension_semantics=("arbitrary",) * n_axes, vmem_limit_bytes=VMEM_LIMIT)


def _sds(shape, dtype):
    return jax.ShapeDtypeStruct(tuple(shape), dtype)


def _token_spec(token):
    return [] if token is None else [pl.BlockSpec(memory_space=pl.ANY)]


def _token_arg(token):
    return [] if token is None else [token]


def _rms_fwd(x, g):
    r = lax.rsqrt(jnp.mean(x * x, axis=-1, keepdims=True) + EPS)
    return x * r * g


def _rms_bwd(x, g, dy):
    r = lax.rsqrt(jnp.mean(x * x, axis=-1, keepdims=True) + EPS)
    xh = x * r
    gdy = dy * g
    dx = r * (gdy - xh * jnp.mean(xh * gdy, axis=-1, keepdims=True))
    return dx, dy * xh


def _accum(ref, part, first):
    @pl.when(first)
    def _():
        ref[...] = part

    @pl.when(jnp.logical_not(first))
    def _():
        ref[...] += part


def _dot_nt(a, b):
    return lax.dot_general(a, b, (((1,), (1,)), ((), ())), preferred_element_type=f32)


def _dot_tn(a, b):
    return lax.dot_general(a, b, (((0,), (0,)), ((), ())), preferred_element_type=f32)


def _dot(a, b):
    return jnp.dot(a, b, preferred_element_type=f32)


def _split_dot(x, e_bf, n_split, nt=False):
    acc = None
    rem = x
    for s in range(n_split):
        hi = rem.astype(bf16)
        term = _dot_nt(hi, e_bf) if nt else _dot(hi, e_bf)
        acc = term if acc is None else acc + term
        if s + 1 < n_split:
            rem = rem - hi.astype(f32)
    return acc


def _sigmoid(x):
    return 0.5 * jnp.tanh(0.5 * x) + 0.5


def norm_matmul(x, g, w, tm, tn, out_dtype, name, token=None):
    t = x.shape[0]
    if w.ndim == 3:
        assert w.shape[2] == tn
        n = w.shape[0] * tn
        w_spec = pl.BlockSpec((None, D, tn), lambda i, j: (j, 0, 0))
    else:
        n = w.shape[1]
        w_spec = pl.BlockSpec((D, tn), lambda i, j: (0, j))

    def body(x_ref, g_ref, w_ref, *rest):
        o_ref, h_ref = rest[-2:]

        @pl.when(pl.program_id(1) == 0)
        def _():
            h_ref[...] = _rms_fwd(x_ref[...], g_ref[...]).astype(bf16)

        o_ref[...] = _dot(h_ref[...], w_ref[...]).astype(out_dtype)

    return pl.pallas_call(
        body, name=name, grid=(t // tm, n // tn),
        in_specs=[pl.BlockSpec((tm, D), lambda i, j: (i, 0)), pl.BlockSpec((1, D), lambda i, j: (0, 0)), w_spec]
        + _token_spec(token),
        out_specs=[pl.BlockSpec((tm, tn), lambda i, j: (i, j)), pl.BlockSpec((tm, D), lambda i, j: (i, 0))],
        out_shape=[_sds((t, n), out_dtype), _sds((t, D), bf16)],
        compiler_params=_cparams(2))(x, g, w, *_token_arg(token))


def matmul_postnorm(a, w, xres, g, tm, relu2, name):
    t, k = a.shape

    def body(a_ref, w_ref, xr_ref, g_ref, y_ref, xo_ref):
        av = a_ref[...]
        if relu2:
            af = jnp.maximum(av.astype(f32), 0.0)
            av = (af * af).astype(bf16)
        y = _dot(av, w_ref[...])
        y_ref[...] = y
        xo_ref[...] = xr_ref[...] + _rms_fwd(y, g_ref[...])

    return pl.pallas_call(
        body, name=name, grid=(t // tm,),
        in_specs=[pl.BlockSpec((tm, k), lambda i: (i, 0)), pl.BlockSpec((k, D), lambda i: (0, 0)),
                  pl.BlockSpec((tm, D), lambda i: (i, 0)), pl.BlockSpec((1, D), lambda i: (0, 0))],
        out_specs=[pl.BlockSpec((tm, D), lambda i: (i, 0)), pl.BlockSpec((tm, D), lambda i: (i, 0))],
        out_shape=[_sds((t, D), f32), _sds((t, D), f32)],
        compiler_params=_cparams(1))(a, w, xres, g)


def postnorm_bwd_matmul(y, g, dxo, w, fp, tm, tn, out_dtype, name, token=None):
    t, n = y.shape[0], w.shape[0]
    relu = fp is not None

    def body(*refs):
        y_ref, g_ref, dxo_ref, w_ref = refs[:4]
        fp_ref = refs[4] if relu else None
        dy_ref, dg_ref, da_ref = refs[-3:]
        i, j = pl.program_id(0), pl.program_id(1)

        @pl.when(j == 0)
        def _():
            dx, dgc = _rms_bwd(y_ref[...], g_ref[...], dxo_ref[...])
            dy_ref[...] = dx.astype(bf16)
            _accum(dg_ref, jnp.sum(dgc, axis=0, keepdims=True), i == 0)

        da = _dot_nt(dy_ref[...], w_ref[...])
        if relu:
            da = da * (2.0 * jnp.maximum(fp_ref[...].astype(f32), 0.0))
        da_ref[...] = da.astype(out_dtype)

    in_specs = [pl.BlockSpec((tm, D), lambda i, j: (i, 0)), pl.BlockSpec((1, D), lambda i, j: (0, 0)),
                pl.BlockSpec((tm, D), lambda i, j: (i, 0)), pl.BlockSpec((tn, D), lambda i, j: (j, 0))]
    args = [y, g, dxo, w]
    if relu:
        in_specs.append(pl.BlockSpec((tm, tn), lambda i, j: (i, j)))
        args.append(fp)
    in_specs += _token_spec(token)
    args += _token_arg(token)
    return pl.pallas_call(
        body, name=name, grid=(t // tm, n // tn), in_specs=in_specs,
        out_specs=[pl.BlockSpec((tm, D), lambda i, j: (i, 0)), pl.BlockSpec((1, D), lambda i, j: (0, 0)),
                   pl.BlockSpec((tm, tn), lambda i, j: (i, j))],
        out_shape=[_sds((t, D), bf16), _sds((1, D), f32), _sds((t, n), out_dtype)],
        compiler_params=_cparams(2))(*args)


def matmul_prenorm_bwd(da, w, x, g, dxo, tm, name):
    t, k = da.shape
    blocked = w.ndim == 3

    def body(da_ref, w_ref, x_ref, g_ref, dxo_ref, dx_ref, dg_ref):
        if blocked:
            kc = w.shape[2]
            dh = _dot_nt(da_ref[:, 0:kc], w_ref[0])
            for q in range(1, w.shape[0]):
                dh = dh + _dot_nt(da_ref[:, q * kc:(q + 1) * kc], w_ref[q])
        else:
            dh = _dot_nt(da_ref[...], w_ref[...])
        dxn, dgc = _rms_bwd(x_ref[...], g_ref[...], dh)
        dx_ref[...] = dxo_ref[...] + dxn
        _accum(dg_ref, jnp.sum(dgc, axis=0, keepdims=True), pl.program_id(0) == 0)

    w_spec = pl.BlockSpec(w.shape, (lambda i: (0, 0, 0)) if blocked else (lambda i: (0, 0)))
    return pl.pallas_call(
        body, name=name, grid=(t // tm,),
        in_specs=[pl.BlockSpec((tm, k), lambda i: (i, 0)), w_spec,
                  pl.BlockSpec((tm, D), lambda i: (i, 0)), pl.BlockSpec((1, D), lambda i: (0, 0)),
                  pl.BlockSpec((tm, D), lambda i: (i, 0))],
        out_specs=[pl.BlockSpec((tm, D), lambda i: (i, 0)), pl.BlockSpec((1, D), lambda i: (0, 0))],
        out_shape=[_sds((t, D), f32), _sds((1, D), f32)],
        compiler_params=_cparams(1))(da, w, x, g, dxo)


def matmul_tn(a, b, tm, tn, relu2, name, col_blocks=False):
    t, m = a.shape
    n = b.shape[1]
    if col_blocks:
        out_spec, out_shape = pl.BlockSpec((None, tm, tn), lambda i, j: (j, i, 0)), _sds((n // tn, m, tn), f32)
    else:
        out_spec, out_shape = pl.BlockSpec((tm, tn), lambda i, j: (i, j)), _sds((m, n), f32)

    def body(a_ref, b_ref, o_ref, at_ref):
        @pl.when(pl.program_id(1) == 0)
        def _():
            av = a_ref[...]
            if relu2:
                af = jnp.maximum(av.astype(f32), 0.0)
                av = (af * af).astype(bf16)
            at_ref[...] = av.T

        o_ref[...] = _dot(at_ref[...], b_ref[...])

    return pl.pallas_call(
        body, name=name, grid=(m // tm, n // tn),
        in_specs=[pl.BlockSpec((t, tm), lambda i, j: (0, i)), pl.BlockSpec((t, tn), lambda i, j: (0, j))],
        out_specs=out_spec, out_shape=out_shape,
        scratch_shapes=[pltpu.VMEM((tm, t), bf16)],
        compiler_params=_cparams(2))(a, b)


ROWS_A = 16
ROWS_B = 32
UNROLL = 4


def _past(win, s):
    return (win if s == 0 else pltpu.roll(win, s, 0))[HALO:]


def _future(win, s):
    n = win.shape[0]
    return (win if s == 0 else pltpu.roll(win, n - s, 0))[:n - HALO]


def _fold8(v):
    return v.reshape(v.shape[0] // 8, 8, v.shape[1]).sum(axis=0)


def _halo_prev(tb, col):
    return lambda i: (jnp.maximum(i * (tb // HALO) - 1, 0), col)


def _halo_next(tb, col, t):
    return lambda i: (jnp.minimum((i + 1) * (tb // HALO), t // HALO - 1), col)


def group_a_fwd(proj, wa, g, seq, tb, name):
    t = proj.shape[0]
    bps = seq // tb

    def body(xa_ref, ca_ref, ba_ref, xah_ref, cah_ref, wa_ref, g_ref, o_ref, u_scr):
        first = (pl.program_id(0) % bps) == 0
        u_scr[0:HALO, :] = jnp.where(first, 0.0, cah_ref[...] * xah_ref[...])
        w, gv = wa_ref[...], g_ref[...]

        def chunk(i, carry):
            r = pl.multiple_of(i * ROWS_A, ROWS_A)
            rows = pl.ds(r, ROWS_A)
            u_scr[pl.ds(pl.multiple_of(HALO + r, HALO), ROWS_A), :] = ca_ref[rows, :] * xa_ref[rows, :]
            win = u_scr[pl.ds(r, ROWS_A + HALO), :]
            cv = w[2:3] * _past(win, 0) + w[1:2] * _past(win, 1) + w[0:1] * _past(win, 2)
            o_ref[rows, :] = _rms_fwd(ba_ref[rows, :] * cv, gv).astype(bf16)
            return carry

        lax.fori_loop(0, tb // ROWS_A, chunk, 0, unroll=UNROLL)

    blk = lambda c: pl.BlockSpec((tb, D), lambda i: (i, c))
    return pl.pallas_call(
        body, name=name, grid=(t // tb,),
        in_specs=[blk(0), blk(1), blk(2),
                  pl.BlockSpec((HALO, D), _halo_prev(tb, 0)), pl.BlockSpec((HALO, D), _halo_prev(tb, 1)),
                  pl.BlockSpec((8, D), lambda i: (0, 0)), pl.BlockSpec((1, D), lambda i: (0, 0))],
        out_specs=pl.BlockSpec((tb, D), lambda i: (i, 0)),
        out_shape=_sds((t, 2 * D), bf16),
        scratch_shapes=[pltpu.VMEM((tb + HALO, D), f32)],
        compiler_params=_cparams(1))(proj, proj, proj, proj, proj, wa, g)


def group_a_bwd(proj, dcat, wa, g, seq, tb, name):
    t = proj.shape[0]
    bps = seq // tb

    def body(xa_ref, ca_ref, ba_ref, dy_ref, xap_ref, cap_ref, xan_ref, can_ref, ban_ref, dyn_ref, wa_ref, g_ref,
             dp_ref, dwa_ref, dg_ref, u_scr, d_scr, acc_scr):
        i = pl.program_id(0)
        first = (i % bps) == 0
        last = (i % bps) == bps - 1
        w = wa_ref[...]
        gv = g_ref[...]
        u_scr[0:HALO, :] = jnp.where(first, 0.0, cap_ref[...] * xap_ref[...])
        u_scr[HALO + tb:2 * HALO + tb, :] = can_ref[...] * xan_ref[...]
        acc_scr[...] = jnp.zeros_like(acc_scr)

        def forward_part(n, carry):
            r = pl.multiple_of(n * ROWS_A, ROWS_A)
            rows = pl.ds(r, ROWS_A)
            ba = ba_ref[rows, :]
            u_scr[pl.ds(pl.multiple_of(HALO + r, HALO), ROWS_A), :] = ca_ref[rows, :] * xa_ref[rows, :]
            win = u_scr[pl.ds(r, ROWS_A + HALO), :]
            u = [_past(win, s) for s in range(3)]
            cv = w[2:3] * u[0] + w[1:2] * u[1] + w[0:1] * u[2]
            dya, dgc = _rms_bwd(ba * cv, gv, dy_ref[rows, :])
            dcv = dya * ba
            d_scr[rows, :] = dcv
            dp_ref[rows, 2 * D:3 * D] = (dya * cv).astype(bf16)
            acc_scr[0:8, :] += _fold8(dgc)
            for k in range(3):
                acc_scr[8 + 8 * k:16 + 8 * k, :] += _fold8(dcv * u[2 - k])
            return carry

        lax.fori_loop(0, tb // ROWS_A, forward_part, 0, unroll=UNROLL)

        start = HALO + tb
        cvn = (w[2:3] * u_scr[pl.ds(start, HALO), :] + w[1:2] * u_scr[pl.ds(start - 1, HALO), :]
               + w[0:1] * u_scr[pl.ds(start - 2, HALO), :])
        ban = ban_ref[...]
        dyan, _ = _rms_bwd(ban * cvn, gv, dyn_ref[...])
        d_scr[tb:tb + HALO, :] = jnp.where(last, 0.0, dyan * ban)

        def backward_part(n, carry):
            r = pl.multiple_of(n * ROWS_A, ROWS_A)
            rows = pl.ds(r, ROWS_A)
            win = d_scr[pl.ds(r, ROWS_A + HALO), :]
            du = w[2:3] * _future(win, 0) + w[1:2] * _future(win, 1) + w[0:1] * _future(win, 2)
            dp_ref[rows, 0:D] = (du * ca_ref[rows, :]).astype(bf16)
            dp_ref[rows, D:2 * D] = (du * xa_ref[rows, :]).astype(bf16)
            return carry

        lax.fori_loop(0, tb // ROWS_A, backward_part, 0, unroll=UNROLL)

        row = lax.broadcasted_iota(jnp.int32, (8, D), 0)
        dw = jnp.zeros((8, D), f32)
        for k in range(3):
            dw = jnp.where(row == k, jnp.sum(acc_scr[8 + 8 * k:16 + 8 * k, :], axis=0, keepdims=True), dw)
        _accum(dwa_ref, dw, i == 0)
        _accum(dg_ref, jnp.sum(acc_scr[0:8, :], axis=0, keepdims=True), i == 0)

    blk = lambda c: pl.BlockSpec((tb, D), lambda i: (i, c))
    prv = lambda c: pl.BlockSpec((HALO, D), _halo_prev(tb, c))
    nxt = lambda c: pl.BlockSpec((HALO, D), _halo_next(tb, c, t))
    return pl.pallas_call(
        body, name=name, grid=(t // tb,),
        in_specs=[blk(0), blk(1), blk(2), blk(0), prv(0), prv(1), nxt(0), nxt(1), nxt(2), nxt(0),
                  pl.BlockSpec((8, D), lambda i: (0, 0)), pl.BlockSpec((1, D), lambda i: (0, 0))],
        out_specs=[pl.BlockSpec((tb, 3 * D), lambda i: (i, 0)), pl.BlockSpec((8, D), lambda i: (0, 0)),
                   pl.BlockSpec((1, D), lambda i: (0, 0))],
        out_shape=[_sds((t, PROJ), bf16), _sds((8, D), f32), _sds((1, D), f32)],
        scratch_shapes=[pltpu.VMEM((tb + 2 * HALO, D), f32), pltpu.VMEM((tb + HALO, D), f32), pltpu.VMEM((32, D), f32)],
        compiler_params=_cparams(1))(proj, proj, proj, dcat, proj, proj, proj, proj, proj, dcat, wa, g)


CB = 512
XBC_BLK0 = COL_XBC // CB


def conv_b_fwd(proj, ws, bs, seq, tb, name):
    t = proj.shape[0]
    bps = seq // tb

    def body(x_ref, xp_ref, w_ref, b_ref, o_ref, x_scr):
        first = (pl.program_id(1) % bps) == 0
        x_scr[0:HALO, :] = jnp.where(first, 0.0, xp_ref[...])
        w, bias = w_ref[...], b_ref[...]

        def chunk(n, carry):
            r = pl.multiple_of(n * ROWS_B, ROWS_B)
            rows = pl.ds(r, ROWS_B)
            x_scr[pl.ds(pl.multiple_of(HALO + r, HALO), ROWS_B), :] = x_ref[rows, :]
            win = x_scr[pl.ds(r, ROWS_B + HALO), :]
            xc = bias + w[3:4] * _past(win, 0)
            for k in range(3):
                xc = xc + w[k:k + 1] * _past(win, 3 - k)
            o_ref[rows, :] = xc * _sigmoid(xc)
            return carry

        lax.fori_loop(0, tb // ROWS_B, chunk, 0, unroll=UNROLL)

    return pl.pallas_call(
        body, name=name, grid=(XBC // CB, t // tb),
        in_specs=[pl.BlockSpec((tb, CB), lambda j, i: (i, XBC_BLK0 + j)),
                  pl.BlockSpec((HALO, CB), lambda j, i: (jnp.maximum(i * (tb // HALO) - 1, 0), XBC_BLK0 + j)),
                  pl.BlockSpec((8, CB), lambda j, i: (0, j)), pl.BlockSpec((1, CB), lambda j, i: (0, j))],
        out_specs=pl.BlockSpec((tb, CB), lambda j, i: (i, j)),
        out_shape=_sds((t, XBC), f32),
        scratch_shapes=[pltpu.VMEM((tb + HALO, CB), f32)],
        compiler_params=_cparams(2))(proj, proj, ws, bs)


def conv_b_bwd(proj, dxs, ws, bs, dproj, seq, tb, name):
    t = proj.shape[0]
    bps = seq // tb

    def body(x_ref, xp_ref, xn_ref, d_ref, dn_ref, w_ref, b_ref, dproj_ref, dx_ref, dw_ref, db_ref, x_scr, d_scr,
             acc_scr):
        i = pl.program_id(1)
        first = (i % bps) == 0
        last = (i % bps) == bps - 1
        w = w_ref[...]
        bias = b_ref[...]
        x_scr[0:HALO, :] = jnp.where(first, 0.0, xp_ref[...])
        x_scr[HALO + tb:2 * HALO + tb, :] = xn_ref[...]
        acc_scr[...] = jnp.zeros_like(acc_scr)

        def dsilu(xc, d):
            sg = _sigmoid(xc)
            return d * (sg * (1.0 + xc * (1.0 - sg)))

        def forward_part(n, carry):
            r = pl.multiple_of(n * ROWS_B, ROWS_B)
            rows = pl.ds(r, ROWS_B)
            x_scr[pl.ds(pl.multiple_of(HALO + r, HALO), ROWS_B), :] = x_ref[rows, :]
            win = x_scr[pl.ds(r, ROWS_B + HALO), :]
            xs = [_past(win, s) for s in range(4)]
            xc = bias + w[3:4] * xs[0]
            for k in range(3):
                xc = xc + w[k:k + 1] * xs[3 - k]
            dxc = dsilu(xc, d_ref[rows, :])
            d_scr[rows, :] = dxc
            acc_scr[0:8, :] += _fold8(dxc)
            for k in range(4):
                acc_scr[8 + 8 * k:16 + 8 * k, :] += _fold8(dxc * xs[3 - k])
            return carry

        lax.fori_loop(0, tb // ROWS_B, forward_part, 0, unroll=UNROLL)

        start = HALO + tb
        xcn = bias + w[3:4] * x_scr[pl.ds(start, HALO), :]
        for k in range(3):
            xcn = xcn + w[k:k + 1] * x_scr[pl.ds(start - 3 + k, HALO), :]
        d_scr[tb:tb + HALO, :] = jnp.where(last, 0.0, dsilu(xcn, dn_ref[...]))

        def backward_part(n, carry):
            r = pl.multiple_of(n * ROWS_B, ROWS_B)
            win = d_scr[pl.ds(r, ROWS_B + HALO), :]
            dx = w[3:4] * _future(win, 0)
            for k in range(3):
                dx = dx + w[k:k + 1] * _future(win, 3 - k)
            dx_ref[pl.ds(r, ROWS_B), :] = dx.astype(bf16)
            return carry

        lax.fori_loop(0, tb // ROWS_B, backward_part, 0, unroll=UNROLL)

        row = lax.broadcasted_iota(jnp.int32, (8, CB), 0)
        dw = jnp.zeros((8, CB), f32)
        for k in range(4):
            dw = jnp.where(row == k, jnp.sum(acc_scr[8 + 8 * k:16 + 8 * k, :], axis=0, keepdims=True), dw)
        _accum(dw_ref, dw, i == 0)
        _accum(db_ref, jnp.sum(acc_scr[0:8, :], axis=0, keepdims=True), i == 0)

    nh = t // HALO
    return pl.pallas_call(
        body, name=name, grid=(XBC // CB, t // tb),
        in_specs=[pl.BlockSpec((tb, CB), lambda j, i: (i, XBC_BLK0 + j)),
                  pl.BlockSpec((HALO, CB), lambda j, i: (jnp.maximum(i * (tb // HALO) - 1, 0), XBC_BLK0 + j)),
                  pl.BlockSpec((HALO, CB), lambda j, i: (jnp.minimum((i + 1) * (tb // HALO), nh - 1), XBC_BLK0 + j)),
                  pl.BlockSpec((tb, CB), lambda j, i: (i, j)),
                  pl.BlockSpec((HALO, CB), lambda j, i: (jnp.minimum((i + 1) * (tb // HALO), nh - 1), j)),
                  pl.BlockSpec((8, CB), lambda j, i: (0, j)), pl.BlockSpec((1, CB), lambda j, i: (0, j)),
                  pl.BlockSpec(memory_space=pl.ANY)],
        out_specs=[pl.BlockSpec((tb, CB), lambda j, i: (i, XBC_BLK0 + j)), pl.BlockSpec((8, CB), lambda j, i: (0, j)),
                   pl.BlockSpec((1, CB), lambda j, i: (0, j))],
        out_shape=[_sds((t, PROJ), bf16), _sds((8, XBC), f32), _sds((1, XBC), f32)],
        input_output_aliases={7: 0},
        scratch_shapes=[pltpu.VMEM((tb + 2 * HALO, CB), f32), pltpu.VMEM((tb + HALO, CB), f32),
                        pltpu.VMEM((40, CB), f32)],
        compiler_params=_cparams(2))(proj, proj, proj, dxs, dxs, ws, bs, dproj)


def place_columns(buf, part, col_block, tb, name):
    t, wdt = part.shape

    def body(p_ref, buf_ref, o_ref):
        o_ref[...] = p_ref[...]

    return pl.pallas_call(
        body, name=name, grid=(t // tb,),
        in_specs=[pl.BlockSpec((tb, wdt), lambda i: (i, 0)), pl.BlockSpec(memory_space=pl.ANY)],
        out_specs=pl.BlockSpec((tb, wdt), lambda i: (i, col_block)), out_shape=_sds(buf.shape, buf.dtype),
        input_output_aliases={1: 0}, compiler_params=_cparams(1))(part, buf)


GW = D // NG


def _ssd_consts():
    head_of_lane = jnp.arange(D) // HP
    expand = (jnp.arange(CH)[:, None] == head_of_lane[None, :]).astype(bf16)
    tri = (jnp.arange(CH)[:, None] >= jnp.arange(CH)[None, :]).astype(f32)
    return expand, tri


def _ssd_common(par_ref, dtr_ref, e_ref, tri_ref):
    par = par_ref[...]
    dtb, alog, dsk = par[0:1], par[1:2], par[2:3]
    lane = lax.broadcasted_iota(jnp.int32, (CH, CH), 1)
    a = -jnp.exp(alog)
    dtr = dtr_ref[...] + dtb
    sp = jnp.maximum(dtr, 0.0) + jnp.log(1.0 + jnp.exp(-jnp.abs(dtr)))
    dt = jnp.where(lane < NH, sp, 0.0)
    cs = jnp.dot(tri_ref[...], dt * a, precision=lax.Precision.HIGHEST, preferred_element_type=f32)
    cs_last = cs[CH - 1:CH, :]
    dte = jnp.exp(cs_last - cs)
    ecs = jnp.exp(cs)
    ecl = jnp.exp(cs_last)
    e = e_ref[...]
    row8 = lax.broadcasted_iota(jnp.int32, (8, CH), 0)
    r8 = _split_dot(jnp.where(row8 == 0, ecl, jnp.where(row8 == 1, dsk, 0.0)), e, 3)
    return dict(a=a, dtr=dtr, dt=dt, cs=cs, cst=cs.T, dte=dte, ecs=ecs, ecl=ecl, e=e, lane=lane,
                dt_x=_split_dot(dt, e, 3), dte_x=_split_dot(dte, e, 3), ecs_x=_split_dot(ecs, e, 3),
                ecl_x=r8[0:1], dsk_x=r8[1:2])


def _decay_matrix(c, h):
    li = lax.broadcasted_iota(jnp.int32, (CH, CH), 0)
    seg = c["cs"][:, h:h + 1] - c["cst"][h:h + 1, :]
    return jnp.exp(jnp.where(li >= c["lane"], seg, -jnp.inf))


def _gate_norm_fwd(y, z, gs):
    zg = z * _sigmoid(z)
    yg = y * zg
    return jnp.concatenate([_rms_fwd(yg[:, k * GW:(k + 1) * GW], gs[:, k * GW:(k + 1) * GW]) for k in range(NG)], axis=1)


def ssd_fwd(xbcs, proj, par, gs, cat, seq, name):
    t = xbcs.shape[0]
    nc = seq // CH
    expand, tri = _ssd_consts()

    def body(xs_ref, b_ref, c_ref, dtr_ref, z_ref, par_ref, e_ref, tri_ref, gs_ref, cat_ref, yn_ref, y_ref, st_ref,
             p_scr, yd_scr):
        @pl.when(pl.program_id(0) % nc == 0)
        def _():
            p_scr[...] = jnp.zeros_like(p_scr)

        c = _ssd_common(par_ref, dtr_ref, e_ref, tri_ref)
        xs = xs_ref[...]
        xdt = xs * c["dt_x"]
        xdt_b = xdt.astype(bf16)
        xdte_b = (xdt * c["dte_x"]).astype(bf16)
        p = p_scr[...]
        st_ref[0] = p
        p_b = p.astype(bf16)
        lo = c["lane"] < HP
        for g in range(NG):
            bg = b_ref[:, g * NS:(g + 1) * NS].astype(bf16)
            cg = c_ref[:, g * NS:(g + 1) * NS].astype(bf16)
            gmat = _dot_nt(cg, bg)
            for q in range(GW // CH):
                col = g * GW + q * CH
                xp = xdt_b[:, col:col + CH]
                h0 = col // HP
                m0 = (gmat * _decay_matrix(c, h0)).astype(bf16)
                m1 = (gmat * _decay_matrix(c, h0 + 1)).astype(bf16)
                yd_scr[:, col:col + CH] = (_dot(m0, jnp.where(lo, xp, jnp.zeros_like(xp)))
                                           + _dot(m1, jnp.where(lo, jnp.zeros_like(xp), xp)))
            gsl = slice(g * GW, (g + 1) * GW)
            yoff = _dot(cg, p_b[:, gsl]) * c["ecs_x"][:, gsl]
            yd_scr[:, gsl] = yd_scr[:, gsl] + yoff
            p_scr[:, gsl] = p[:, gsl] * c["ecl_x"][:, gsl] + _dot_tn(bg, xdte_b[:, gsl])
        y = yd_scr[...] + c["dsk_x"] * xs
        y_ref[...] = y
        yn_ref[...] = _gate_norm_fwd(y, z_ref[...], gs_ref[...]).astype(bf16)

    nb = t // CH
    return pl.pallas_call(
        body, name=name, grid=(nb,),
        in_specs=[pl.BlockSpec((CH, D), lambda i: (i, 0)),
                  pl.BlockSpec((CH, NG * NS), lambda i: (i, D // (NG * NS))),
                  pl.BlockSpec((CH, NG * NS), lambda i: (i, D // (NG * NS) + 1)),
                  pl.BlockSpec((CH, CH), lambda i: (i, COL_DT // CH)),
                  pl.BlockSpec((CH, D), lambda i: (i, COL_Z // D)),
                  pl.BlockSpec((8, CH), lambda i: (0, 0)), pl.BlockSpec((CH, D), lambda i: (0, 0)),
                  pl.BlockSpec((CH, CH), lambda i: (0, 0)), pl.BlockSpec((1, D), lambda i: (0, 0)),
                  pl.BlockSpec(memory_space=pl.ANY)],
        out_specs=[pl.BlockSpec((CH, D), lambda i: (i, 1)), pl.BlockSpec((CH, D), lambda i: (i, 0)),
                   pl.BlockSpec((1, NS, D), lambda i: (i, 0, 0))],
        out_shape=[_sds((t, 2 * D), bf16), _sds((t, D), f32), _sds((nb, NS, D), f32)],
        input_output_aliases={9: 0},
        scratch_shapes=[pltpu.VMEM((NS, D), f32), pltpu.VMEM((CH, D), f32)],
        compiler_params=_cparams(1))(xbcs, xbcs, xbcs, proj, proj, par, expand, tri, gs, cat)


def ssd_bwd(xbcs, proj, ypre, states, dcat, par, gs, dproj, seq, name):
    t = xbcs.shape[0]
    nc = seq // CH
    expand, tri = _ssd_consts()

    def body(xs_ref, b_ref, c_ref, dtr_ref, z_ref, y_ref, st_ref, dyn_ref, par_ref, e_ref, tri_ref, gs_ref, dproj_ref,
             dx_ref, dz_ref, ddt_ref, dpar_ref, dgs_ref, dp_scr, dxdt_scr):
        i = pl.program_id(0)

        @pl.when(i % nc == 0)
        def _():
            dp_scr[...] = jnp.zeros_like(dp_scr)

        c = _ssd_common(par_ref, dtr_ref, e_ref, tri_ref)
        e = c["e"]
        lane = c["lane"]
        sub = lax.broadcasted_iota(jnp.int32, (CH, CH), 0)
        xs = xs_ref[...]
        xdt = xs * c["dt_x"]
        xdt_b = xdt.astype(bf16)
        xdte_b = (xdt * c["dte_x"]).astype(bf16)
        p = st_ref[0]
        p_b = p.astype(bf16)
        dpn = dp_scr[...]
        dpn_b = dpn.astype(bf16)

        y, z, gs_v = y_ref[...], z_ref[...], gs_ref[...]
        zs = _sigmoid(z)
        zg = z * zs
        yg = y * zg
        parts, gparts = [], []
        for k in range(NG):
            sl = slice(k * GW, (k + 1) * GW)
            dxk, dgk = _rms_bwd(yg[:, sl], gs_v[:, sl], dyn_ref[:, sl])
            parts.append(dxk)
            gparts.append(dgk)
        dyg = jnp.concatenate(parts, axis=1)
        dgs_rows = jnp.concatenate(gparts, axis=1)
        dy = dyg * zg
        dz_ref[...] = (dyg * y * (zs * (1.0 + z * (1.0 - zs)))).astype(bf16)
        dy_b = dy.astype(bf16)
        dq_b = (dy * c["ecs_x"]).astype(bf16)

        lo = lane < HP
        dcs = jnp.zeros((CH, CH), f32)
        dcst = jnp.zeros((CH, CH), f32)
        for g in range(NG):
            gsl = slice(g * GW, (g + 1) * GW)
            bg = b_ref[:, g * NS:(g + 1) * NS].astype(bf16)
            cg = c_ref[:, g * NS:(g + 1) * NS].astype(bf16)
            gmat = _dot_nt(cg, bg)
            dgm = jnp.zeros((CH, CH), f32)
            for q in range(GW // CH):
                col = g * GW + q * CH
                xp = xdt_b[:, col:col + CH]
                dyp = dy_b[:, col:col + CH]
                acc = None
                for hh in range(2):
                    h = col // HP + hh
                    keep = lo if hh == 0 else jnp.logical_not(lo)
                    dyh = jnp.where(keep, dyp, jnp.zeros_like(dyp))
                    dec = _decay_matrix(c, h)
                    m = gmat * dec
                    dm = _dot_nt(dyh, xp)
                    dseg = dm * m
                    dcs = dcs + jnp.where(lane == h, jnp.sum(dseg, axis=1, keepdims=True), 0.0)
                    dcst = dcst + jnp.where(sub == h, jnp.sum(dseg, axis=0, keepdims=True), 0.0)
                    dgm = dgm + dm * dec
                    term = _dot_tn(m.astype(bf16), dyh)
                    acc = term if acc is None else acc + term
                dxdt_scr[:, col:col + CH] = acc
            dgm_b = dgm.astype(bf16)
            bds = _dot(bg, dpn_b[:, gsl])
            dxdt_scr[:, gsl] = dxdt_scr[:, gsl] + c["dte_x"][:, gsl] * bds
            dc_g = _dot(dgm_b, bg) + _dot_nt(dq_b[:, gsl], p_b[:, gsl])
            db_g = _dot_tn(dgm_b, cg) + _dot_nt(xdte_b[:, gsl], dpn_b[:, gsl])
            dx_ref[:, D + g * NS:D + (g + 1) * NS] = db_g
            dx_ref[:, D + NG * NS + g * NS:D + NG * NS + (g + 1) * NS] = dc_g
            dp_scr[:, gsl] = dpn[:, gsl] * c["ecl_x"][:, gsl] + _dot_tn(cg, dq_b[:, gsl])
            q_g = _dot(cg, p_b[:, gsl])
            e_g = e[:, gsl]
            dcs = dcs + c["ecs"] * _split_dot(dy[:, gsl] * q_g, e_g, 2, nt=True)
            ddte = _split_dot(xdt[:, gsl] * bds, e_g, 2, nt=True) * c["dte"]
            dcs = dcs - ddte
            dcs = dcs + jnp.where(sub == CH - 1, jnp.sum(ddte, axis=0, keepdims=True), 0.0)

        decl = _split_dot(jnp.broadcast_to(jnp.sum(dpn * p, axis=0, keepdims=True), (8, D)), e, 2, nt=True)[0:1]
        dcs = dcs + jnp.where(sub == CH - 1, c["ecl"] * decl, 0.0)
        dcs = dcs - dcst.T
        dadt = lax.dot_general(tri_ref[...], dcs, (((0,), (0,)), ((), ())), precision=lax.Precision.HIGHEST,
                               preferred_element_type=f32)
        dxdt = dxdt_scr[...]
        ddt = dadt * c["a"] + _split_dot(dxdt * xs, e, 2, nt=True)
        ddtr = jnp.where(lane < NH, ddt * _sigmoid(c["dtr"]), 0.0)
        ddt_ref[...] = ddtr.astype(bf16)
        dx_ref[:, 0:D] = dxdt * c["dt_x"] + c["dsk_x"] * dy
        dsk = _split_dot(jnp.broadcast_to(jnp.sum(dy * xs, axis=0, keepdims=True), (8, D)), e, 2, nt=True)[0:1]
        dalog = jnp.sum(dadt * c["dt"], axis=0, keepdims=True) * c["a"]
        row8 = lax.broadcasted_iota(jnp.int32, (8, CH), 0)
        dpar = jnp.where(row8 == 0, jnp.sum(ddtr, axis=0, keepdims=True),
                         jnp.where(row8 == 1, dalog, jnp.where(row8 == 2, dsk, 0.0)))
        dpar = jnp.where(lax.broadcasted_iota(jnp.int32, (8, CH), 1) < NH, dpar, 0.0)
        _accum(dpar_ref, dpar, i == 0)
        _accum(dgs_ref, jnp.sum(dgs_rows, axis=0, keepdims=True), i == 0)

    nb = t // CH
    rev = lambda i: (i // nc) * nc + (nc - 1 - i % nc)
    return pl.pallas_call(
        body, name=name, grid=(nb,),
        in_specs=[pl.BlockSpec((CH, D), lambda i: (rev(i), 0)),
                  pl.BlockSpec((CH, NG * NS), lambda i: (rev(i), D // (NG * NS))),
                  pl.BlockSpec((CH, NG * NS), lambda i: (rev(i), D // (NG * NS) + 1)),
                  pl.BlockSpec((CH, CH), lambda i: (rev(i), COL_DT // CH)),
                  pl.BlockSpec((CH, D), lambda i: (rev(i), COL_Z // D)),
                  pl.BlockSpec((CH, D), lambda i: (rev(i), 0)),
                  pl.BlockSpec((1, NS, D), lambda i: (rev(i), 0, 0)),
                  pl.BlockSpec((CH, D), lambda i: (rev(i), 1)),
                  pl.BlockSpec((8, CH), lambda i: (0, 0)), pl.BlockSpec((CH, D), lambda i: (0, 0)),
                  pl.BlockSpec((CH, CH), lambda i: (0, 0)), pl.BlockSpec((1, D), lambda i: (0, 0)),
                  pl.BlockSpec(memory_space=pl.ANY)],
        out_specs=[pl.BlockSpec((CH, XBC), lambda i: (rev(i), 0)), pl.BlockSpec((CH, D), lambda i: (rev(i), COL_Z // D)),
                   pl.BlockSpec((CH, CH), lambda i: (rev(i), 0)),
                   pl.BlockSpec((8, CH), lambda i: (0, 0)), pl.BlockSpec((1, D), lambda i: (0, 0))],
        out_shape=[_sds((t, XBC), f32), _sds((t, PROJ), bf16), _sds((t, CH), bf16), _sds((8, CH), f32), _sds((1, D), f32)],
        input_output_aliases={12: 1},
        scratch_shapes=[pltpu.VMEM((NS, D), f32), pltpu.VMEM((CH, D), f32)],
        compiler_params=_cparams(1))(xbcs, xbcs, xbcs, proj, proj, ypre, states, dcat, par, expand, tri, gs, dproj)


def loss_head(y, target, tb, name):
    t = y.shape[0]

    def body(y_ref, t_ref, s_ref, dy_ref):
        err = y_ref[...] - t_ref[...]
        dy_ref[...] = err * (1.0 / D)
        _accum(s_ref, jnp.zeros((8, CH), f32) + jnp.sum(err * err), pl.program_id(0) == 0)

    return pl.pallas_call(
        body, name=name, grid=(t // tb,),
        in_specs=[pl.BlockSpec((tb, D), lambda i: (i, 0)), pl.BlockSpec((tb, D), lambda i: (i, 0))],
        out_specs=[pl.BlockSpec((8, CH), lambda i: (0, 0)), pl.BlockSpec((tb, D), lambda i: (i, 0))],
        out_shape=[_sds((8, CH), f32), _sds((t, D), f32)],
        compiler_params=_cparams(1))(y, target)


def _tiles(t, seq):
    tm = min(512, t)
    return dict(tm=tm, tm_small=min(256, t), tm_large=min(1024, t), tb=min(512, seq))


def local_step(x, target, depth, weights_of, seq, grads_done=None):
    t = x.shape[0]
    ts = _tiles(t, seq)
    tm, tl, tb = ts["tm"], ts["tm_large"], ts["tb"]
    saved, ws = [], []
    for l in range(depth):
        w = weights_of(l, x)
        ws.append(w)
        proj, h1 = norm_matmul(x, w["g1"], w["win"], tl, 1152, f32, "in_proj", token=w.get("token"))
        cat = group_a_fwd(proj, w["wa"], w["ga"], seq, tb, "group_a_fwd")
        xbcs = conv_b_fwd(proj, w["ws"], w["bs"], seq, tb, "conv_b_fwd")
        cat, ypre, states = ssd_fwd(xbcs, proj, w["par"], w["gs"], cat, seq, "ssd_fwd")
        mix, x2 = matmul_postnorm(cat, w["wo"], x, w["g2"], tl, False, "out_proj")
        fp, h2 = norm_matmul(x2, w["g3"], w["wu"], tl, 1024, bf16, "mlp_up")
        o, x3 = matmul_postnorm(fp, w["wd"], x2, w["g4"], tm, True, "mlp_down")
        saved.append(dict(x=x, proj=proj, h1=h1, xbcs=xbcs, ypre=ypre, states=states, cat=cat, mix=mix, x2=x2,
                          fp=fp, h2=h2, o=o))
        x = x3
    sse, dx = loss_head(x, target, tm, "loss_head")
    grads = [None] * depth
    token = None
    for l in reversed(range(depth)):
        s, w = saved[l], ws[l]
        do, dg4, dfp = postnorm_bwd_matmul(s["o"], w["g4"], dx, w["wd"], s["fp"], tl, 1024, bf16, "mlp_down_bwd",
                                           token=token)
        dwd = matmul_tn(s["fp"], do, 512, 1024, True, "mlp_down_dw")
        dx2, dg3 = matmul_prenorm_bwd(dfp, w["wu"], s["x2"], w["g3"], dx, tm, "mlp_up_bwd")
        dwu = matmul_tn(s["h2"], dfp, 512, 1024, False, "mlp_up_dw", col_blocks=True)
        dmix, dg2, dcat = postnorm_bwd_matmul(s["mix"], w["g2"], dx2, w["wo"], None, tl, 1024, f32, "out_proj_bwd")
        dwo = matmul_tn(s["cat"], dmix, 512, 1024, False, "out_proj_dw")
        dproj, dwa, dga = group_a_bwd(s["proj"], dcat, w["wa"], w["ga"], seq, tb, "group_a_bwd")
        dxbcs, dproj, ddt, dpar, dgs = ssd_bwd(s["xbcs"], s["proj"], s["ypre"], s["states"], dcat, w["par"], w["gs"],
                                               dproj, seq, "ssd_bwd")
        dproj, dws, dbs = conv_b_bwd(s["proj"], dxbcs, w["ws"], w["bs"], dproj, seq, tb, "conv_b_bwd")
        dproj = place_columns(dproj, ddt, COL_DT // CH, tm, "place_ddt")
        dx, dg1 = matmul_prenorm_bwd(dproj, w["win"], s["x"], w["g1"], dx2, ts["tm_small"], "in_proj_bwd")
        dwin = matmul_tn(s["h1"], dproj, 512, 1152, False, "in_proj_dw")
        grads[l] = dict(win=dwin, wo=dwo, wu=dwu, wd=dwd, wa=dwa, ws=dws, bs=dbs, par=dpar,
                        g1=dg1, ga=dga, gs=dgs, g2=dg2, g3=dg3, g4=dg4)
        if grads_done is not None:
            token = grads_done(l, grads[l], dx)
    return sse, dx, grads


GROUPS = {
    "chips": [(1, 0, 0), (0, 1, 0), (1, 1, 0)],
    "pair": [(0, 0, 1)],
    "all": [(1, 0, 0), (0, 1, 0), (1, 1, 0), (0, 0, 1), (1, 0, 1), (0, 1, 1), (1, 1, 1)],
}


def _group_index(group, x, y, c):
    return {"chips": 2 * x + y, "pair": c, "all": 4 * x + 2 * y + c}[group]


def _chunk_indices(shape, pieces):
    if len(shape) < 3:
        return [()]
    lead = [()]
    for n in shape[:-2]:
        lead = [i + (k,) for i in lead for k in range(n)]
    rows = shape[-2]
    split = max(1, pieces // len(lead))
    while split > 1 and (rows % split or (rows // split) % 16):
        split -= 1
    step = rows // split
    return [i + (pl.ds(s * step, step),) for i in lead for s in range(split)]


def _exchange(arrays, out_shapes, group, src_view, dst_view, view_shape, name, own, pieces=16):
    masks = GROUPS[group]
    na, nm = len(arrays), len(masks)
    cuts = [_chunk_indices(view_shape(a), pieces) for a in range(na)]

    def body(*refs):
        ins, outs = refs[:na], refs[na:2 * na]
        send_sems, recv_sems = refs[2 * na:2 * na + 2]
        local_sems = refs[2 * na + 2] if own else None
        x, y, c = lax.axis_index("x"), lax.axis_index("y"), lax.axis_index("c")
        me = _group_index(group, x, y, c)
        peers = []
        for mx, my, mc in masks:
            px, py, pc = (1 - x if mx else x), (1 - y if my else y), (1 - c if mc else c)
            peers.append(((px, py, pc), _group_index(group, px, py, pc)))

        def part(ref, idx):
            return ref.at[idx] if idx else ref

        if own:
            for a in range(na):
                for idx in cuts[a]:
                    pltpu.make_async_copy(part(src_view(ins[a], a, me), idx), part(dst_view(outs[a], a, me), idx),
                                          local_sems.at[a]).start()
        for a in range(na):
            for j, (dev, pidx) in enumerate(peers):
                for idx in cuts[a]:
                    pltpu.make_async_remote_copy(
                        src_ref=part(src_view(ins[a], a, pidx), idx), dst_ref=part(dst_view(outs[a], a, me), idx),
                        send_sem=send_sems.at[a * nm + j], recv_sem=recv_sems.at[a * nm + j],
                        device_id=dev, device_id_type=MESH).start()
        whole = []
        for a in range(na):
            for j, (dev, pidx) in enumerate(peers):
                whole.append(pltpu.make_async_remote_copy(
                    src_ref=src_view(ins[a], a, pidx), dst_ref=dst_view(outs[a], a, pidx),
                    send_sem=send_sems.at[a * nm + j], recv_sem=recv_sems.at[a * nm + j],
                    device_id=dev, device_id_type=MESH))
        for cp in whole:
            cp.wait_recv()
        for cp in whole:
            cp.wait_send()
        if own:
            for a in range(na):
                pltpu.make_async_copy(src_view(ins[a], a, me), dst_view(outs[a], a, me), local_sems.at[a]).wait()

    hbm = pl.BlockSpec(memory_space=pltpu.HBM)
    sems = [pltpu.SemaphoreType.DMA((na * nm,)), pltpu.SemaphoreType.DMA((na * nm,))]
    return pl.pallas_call(
        body, name=name, in_specs=[hbm] * na, out_specs=[hbm] * na,
        out_shape=[_sds(s, a.dtype) for s, a in zip(out_shapes, arrays)],
        scratch_shapes=sems + ([pltpu.SemaphoreType.DMA((na,))] if own else []))(*arrays)


def all_gather(arrays, group, name, slot_axis=0, own=True):
    n = len(GROUPS[group]) + 1
    shapes = [a.shape[:slot_axis] + (n,) + a.shape[slot_axis:] for a in arrays]
    lead = (slice(None),) * slot_axis
    return _exchange(arrays, shapes, group, lambda r, a, i: r, lambda r, a, i: r.at[lead + (i,)],
                     lambda a: arrays[a].shape, name, own)


HBM_SPEC = pl.BlockSpec(memory_space=pltpu.HBM)
SEM_SPEC = pl.BlockSpec(memory_space=pltpu.SEMAPHORE)
DATAFLOW = pltpu.SideEffectType.DATAFLOW_SIDE_EFFECTING
N_CHIPS = 4


def _chip_peers(x, y, c):
    out = []
    for mx, my, _ in GROUPS["chips"]:
        px, py = (1 - x if mx else x), (1 - y if my else y)
        out.append(((px, py, c), 2 * px + py))
    return out


def _weight_views(shards):
    half = [s.shape[0] // 2 for s in shards]
    return dict(src=lambda ref, a, c, to_chip: ref.at[pl.ds(c * half[a], half[a])],
                dst=lambda ref, a, c, from_chip: ref.at[from_chip, pl.ds(c * half[a], half[a])],
                rows=lambda a: half[a])


def _grad_views(sums):
    return dict(src=lambda ref, a, c, to_chip: ref.at[to_chip], dst=lambda ref, a, c, from_chip: ref.at[from_chip],
                rows=lambda a: sums[a].shape[1])


def chips_start(sources, zones, views, name, pieces=4):
    na, nm = len(sources), N_CHIPS - 1

    def body(*refs):
        ins, lands = refs[:na], refs[na:2 * na]
        send_sems, recv_sems, token = refs[2 * na], refs[2 * na + 1], refs[-1]
        x, y, c = lax.axis_index("x"), lax.axis_index("y"), lax.axis_index("c")
        chip = 2 * x + y
        for a in range(na):
            step = views["rows"](a) // pieces
            for j, (dev, to_chip) in enumerate(_chip_peers(x, y, c)):
                for q in range(pieces):
                    rows = pl.ds(q * step, step)
                    pltpu.make_async_remote_copy(
                        src_ref=views["src"](ins[a], a, c, to_chip).at[rows],
                        dst_ref=views["dst"](lands[a], a, c, chip).at[rows],
                        send_sem=send_sems.at[a * nm + j], recv_sem=recv_sems.at[a * nm + j],
                        device_id=dev, device_id_type=MESH).start()
        token[...] = jnp.zeros_like(token)

    both = list(sources) + list(zones)
    outs = pl.pallas_call(
        body, name=name,
        out_shape=(pltpu.SemaphoreType.DMA((na * nm,)), pltpu.SemaphoreType.DMA((na * nm,)),
                   *[pltpu.HBM(b.shape, b.dtype) for b in both], _sds((8, CH), f32)),
        in_specs=[HBM_SPEC] * (2 * na),
        out_specs=(SEM_SPEC, SEM_SPEC, *[HBM_SPEC] * (2 * na), pl.BlockSpec(memory_space=pltpu.VMEM)),
        input_output_aliases={i: 2 + i for i in range(2 * na)},
        compiler_params=pltpu.CompilerParams(has_side_effects=DATAFLOW))(
            *[pltpu.with_memory_space_constraint(b, pltpu.HBM) for b in both])
    return dict(send=outs[0], recv=outs[1], sources=list(outs[2:2 + na]), zones=list(outs[2 + na:2 + 2 * na]),
                token=outs[-1], views=views)


def chips_wait(started, after, name):
    sources, zones, views = started["sources"], started["zones"], started["views"]
    na, nm = len(sources), N_CHIPS - 1

    def body(*refs):
        ins, lands = refs[:na], refs[na:2 * na]
        send_sems, recv_sems = refs[2 * na], refs[2 * na + 1]
        x, y, c = lax.axis_index("x"), lax.axis_index("y"), lax.axis_index("c")
        for a in range(na):
            for j, (dev, peer_chip) in enumerate(_chip_peers(x, y, c)):
                cp = pltpu.make_async_remote_copy(
                    src_ref=views["src"](ins[a], a, c, peer_chip), dst_ref=views["dst"](lands[a], a, c, peer_chip),
                    send_sem=send_sems.at[a * nm + j], recv_sem=recv_sems.at[a * nm + j],
                    device_id=dev, device_id_type=MESH)
                cp.wait_send()
                cp.wait_recv()

    both = list(sources) + list(zones)
    outs = pl.pallas_call(
        body, name=name, out_shape=tuple(pltpu.HBM(b.shape, b.dtype) for b in both),
        in_specs=[HBM_SPEC] * (2 * na) + [SEM_SPEC, SEM_SPEC, pl.BlockSpec(memory_space=pl.ANY)],
        out_specs=tuple([HBM_SPEC] * (2 * na)), input_output_aliases={i: i for i in range(2 * na)},
        compiler_params=pltpu.CompilerParams(has_side_effects=DATAFLOW))(*both, started["send"], started["recv"], after)
    return list(outs[:na]), list(outs[na:])


def weights_share(zones, name):
    na, nm = len(zones), N_CHIPS - 1

    def body(*refs):
        lands = refs[na:2 * na]
        send_sems, recv_sems = refs[2 * na:]
        x, y, c = lax.axis_index("x"), lax.axis_index("y"), lax.axis_index("c")
        chip = 2 * x + y
        sibling = (x, y, 1 - c)
        sends = []
        for a in range(na):
            half = zones[a].shape[1] // 2
            for m in range(1, N_CHIPS):
                mine = lands[a].at[chip ^ m, pl.ds(c * half, half)]
                sends.append(pltpu.make_async_remote_copy(
                    src_ref=mine, dst_ref=mine, send_sem=send_sems.at[a * nm + m - 1],
                    recv_sem=recv_sems.at[a * nm + m - 1], device_id=sibling, device_id_type=MESH))
        for cp in sends:
            cp.start()
        for a in range(na):
            half = zones[a].shape[1] // 2
            for m in range(1, N_CHIPS):
                theirs = lands[a].at[chip ^ m, pl.ds((1 - c) * half, half)]
                pltpu.make_async_remote_copy(
                    src_ref=theirs, dst_ref=theirs, send_sem=send_sems.at[a * nm + m - 1],
                    recv_sem=recv_sems.at[a * nm + m - 1], device_id=sibling, device_id_type=MESH).wait_recv()
        for cp in sends:
            cp.wait_send()

    return pl.pallas_call(
        body, name=name, in_specs=[HBM_SPEC] * na, out_specs=[HBM_SPEC] * na,
        out_shape=[_sds(z.shape, z.dtype) for z in zones], input_output_aliases={i: i for i in range(na)},
        scratch_shapes=[pltpu.SemaphoreType.DMA((na * nm,)), pltpu.SemaphoreType.DMA((na * nm,))])(*zones)


def pair_send_halves(grads, name):
    half = [g.shape[1] // 2 for g in grads]
    shapes = [(g.shape[0], h, g.shape[2]) for g, h in zip(grads, half)]
    return _exchange(grads, shapes, "pair", lambda r, a, i: r.at[:, pl.ds(i * half[a], half[a])],
                     lambda r, a, i: r, lambda a: shapes[a], name, False)


def sum_pair_half(g, recv, core, name, tb=256):
    nk, r, c = g.shape
    tb = min(tb, r // 2)
    nb = r // 2 // tb

    def body(core_ref, g_ref, r_ref, o_ref):
        o_ref[...] = (g_ref[...] + r_ref[...]).astype(bf16)

    return pl.pallas_call(
        body, name=name,
        grid_spec=pltpu.PrefetchScalarGridSpec(
            num_scalar_prefetch=1, grid=(nk, nb),
            in_specs=[pl.BlockSpec((None, tb, c), lambda k, i, core_ref: (k, core_ref[0] * nb + i, 0)),
                      pl.BlockSpec((None, tb, c), lambda k, i, core_ref: (k, i, 0))],
            out_specs=pl.BlockSpec((None, tb, c), lambda k, i, core_ref: (k, i, 0))),
        out_shape=_sds((nk, r // 2, c), bf16), compiler_params=_cparams(2))(
            jnp.reshape(core, (1,)).astype(jnp.int32), g, recv)


def chip_sum_into(acc, layer, own, others, chip, name, tb=256):
    n, r, c = own.shape
    tb = min(tb, r)

    def body(chip_ref, x_ref, y1_ref, y2_ref, y3_ref, acc_ref, o_ref):
        o_ref[...] = ((x_ref[...].astype(f32) + y1_ref[...].astype(f32)) + y2_ref[...].astype(f32)) + y3_ref[...].astype(f32)

    def slot(k):
        return pl.BlockSpec((None, tb, c), lambda i, chip_ref: (chip_ref[0] ^ k, i, 0))

    return pl.pallas_call(
        body, name=name,
        grid_spec=pltpu.PrefetchScalarGridSpec(
            num_scalar_prefetch=1, grid=(r // tb,),
            in_specs=[slot(k) for k in range(n)] + [pl.BlockSpec(memory_space=pl.ANY)],
            out_specs=pl.BlockSpec((None, tb, c), lambda i, chip_ref: (layer, i, 0))),
        out_shape=_sds(acc.shape, f32), input_output_aliases={n + 1: 0}, compiler_params=_cparams(1))(
            jnp.reshape(chip, (1,)).astype(jnp.int32), own, *([others] * (n - 1)), acc)


def adamw_halves(w, g_own, g_recv, m, v, core, name, tb=256):
    depth, r, c = w.shape
    tb = min(tb, r // 2)
    nb = r // 2 // tb

    def body(core_ref, w_ref, go_ref, gr_ref, m_ref, v_ref, g_ref, d_ref, mo_ref, vo_ref):
        gv = jnp.where(pl.program_id(1) == core_ref[0], go_ref[...], gr_ref[...])
        m2 = B1 * m_ref[...] + (1.0 - B1) * gv
        v2 = B2 * v_ref[...] + (1.0 - B2) * (gv * gv)
        m_hat = m2 / (1.0 - B1 ** STEP)
        v_hat = v2 / (1.0 - B2 ** STEP)
        g_ref[...] = gv
        d_ref[...] = -LR * (m_hat / (jnp.sqrt(v_hat) + AEPS) + WD * w_ref[...])
        mo_ref[...] = m2
        vo_ref[...] = v2

    whole = pl.BlockSpec((None, tb, c), lambda l, h, i, core_ref: (l, h * nb + i, 0))
    part = pl.BlockSpec((None, tb, c), lambda l, h, i, core_ref: (l, i, 0))
    return pl.pallas_call(
        body, name=name,
        grid_spec=pltpu.PrefetchScalarGridSpec(num_scalar_prefetch=1, grid=(depth, 2, nb),
                                               in_specs=[whole, part, part, whole, whole], out_specs=[whole] * 4),
        out_shape=[_sds(w.shape, f32)] * 4, compiler_params=_cparams(3))(
            jnp.reshape(core, (1,)).astype(jnp.int32), w, g_own, g_recv, m, v)


def sum_slots(y, out_dtype, name, tb=256):
    n, r, c = y.shape
    tb = min(tb, r)

    def body(y_ref, o_ref):
        acc = y_ref[0].astype(f32)
        for i in range(1, n):
            acc = acc + y_ref[i].astype(f32)
        o_ref[...] = acc.astype(out_dtype)

    return pl.pallas_call(
        body, name=name, grid=(r // tb,),
        in_specs=[pl.BlockSpec((n, tb, c), lambda i: (0, i, 0))], out_specs=pl.BlockSpec((tb, c), lambda i: (i, 0)),
        out_shape=_sds((r, c), out_dtype), compiler_params=_cparams(1))(y)


def adamw(w, g, m, v, name, tb=256):
    r, c = w.shape
    tb = min(tb, r)

    def body(w_ref, g_ref, m_ref, v_ref, d_ref, mo_ref, vo_ref):
        gv = g_ref[...]
        m2 = B1 * m_ref[...] + (1.0 - B1) * gv
        v2 = B2 * v_ref[...] + (1.0 - B2) * (gv * gv)
        m_hat = m2 / (1.0 - B1 ** STEP)
        v_hat = v2 / (1.0 - B2 ** STEP)
        d_ref[...] = -LR * (m_hat / (jnp.sqrt(v_hat) + AEPS) + WD * w_ref[...])
        mo_ref[...] = m2
        vo_ref[...] = v2

    spec = pl.BlockSpec((tb, c), lambda i: (i, 0))
    return pl.pallas_call(
        body, name=name, grid=(r // tb,), in_specs=[spec] * 4, out_specs=[spec] * 3,
        out_shape=[_sds((r, c), f32)] * 3, compiler_params=_cparams(1))(w, g, m, v)


def adamw_leading(w, g, m, v, name, tc=64):
    c, l, r = w.shape
    main = c // tc
    tail = c - main * tc

    def body(w_ref, g_ref, m_ref, v_ref, *rest):
        d_ref, mo_ref, vo_ref = rest[-3:]
        gv = g_ref[...]
        m2 = B1 * m_ref[...] + (1.0 - B1) * gv
        v2 = B2 * v_ref[...] + (1.0 - B2) * (gv * gv)
        m_hat = m2 / (1.0 - B1 ** STEP)
        v_hat = v2 / (1.0 - B2 ** STEP)
        d_ref[...] = -LR * (m_hat / (jnp.sqrt(v_hat) + AEPS) + WD * w_ref[...])
        mo_ref[...] = m2
        vo_ref[...] = v2

    spec = pl.BlockSpec((tc, l, r), lambda i: (i, 0, 0))
    outs = pl.pallas_call(
        functools.partial(body), name=name, grid=(main,), in_specs=[spec] * 4, out_specs=[spec] * 3,
        out_shape=[_sds(w.shape, f32)] * 3, compiler_params=_cparams(1))(w, g, m, v)
    if tail:
        assert (main * tc) % tail == 0
        last = pl.BlockSpec((tail, l, r), lambda i: (main * tc // tail, 0, 0))
        outs = pl.pallas_call(
            functools.partial(body), name=name + "_tail", grid=(1,),
            in_specs=[last] * 4 + [pl.BlockSpec(memory_space=pl.ANY)] * 3, out_specs=[last] * 3,
            out_shape=[_sds(w.shape, f32)] * 3, input_output_aliases={4: 0, 5: 1, 6: 2},
            compiler_params=_cparams(1))(w, g, m, v, *outs)
    return outs


SMALL_ROW = 1024
SMALL_GAINS = ("g1", "ga", "gs", "g2", "g3", "g4")
SMALL_LAYER_ROWS = 8 + 8 + 16 + 8


def _pack_small(grads):
    wide = lambda a: jnp.pad(a, ((0, 0), (0, 2 * SMALL_ROW - a.shape[1]))).reshape(-1, SMALL_ROW)
    row = lax.broadcasted_iota(jnp.int32, (8, SMALL_ROW), 0)
    parts = []
    for g in grads:
        singles = [g[k] for k in SMALL_GAINS] + [g["bs"][:, :SMALL_ROW],
                                                 jnp.pad(g["bs"][:, SMALL_ROW:], ((0, 0), (0, 2 * SMALL_ROW - XBC)))]
        first = sum(jnp.where(row == k, s, 0.0) for k, s in enumerate(singles))
        parts += [first, g["wa"], wide(g["ws"]), jnp.pad(g["par"], ((0, 0), (0, SMALL_ROW - CH)))]
    return jnp.concatenate(parts, axis=0)


def _unpack_small(packed, depth):
    rows = packed.reshape(depth, SMALL_LAYER_ROWS, SMALL_ROW)
    out = {k: rows[:, i] for i, k in enumerate(SMALL_GAINS)}
    out["bs"] = rows[:, 6:8].reshape(depth, 2 * SMALL_ROW)[:, :XBC]
    out["wa"] = rows[:, 8:11]
    out["ws"] = rows[:, 16:32].reshape(depth, 8, 2 * SMALL_ROW)[:, :4, :XBC]
    out["par"] = rows[:, 32:35, :CH]
    return out


def kernel(x, norm_mix_pre, w_in, conv_a_w, ssm_conv_w, ssm_conv_b, dt_bias, a_log, d_skip, conv_out_norm, ssm_out_norm, w_out, norm_mix_post, norm_mlp_pre, w_up, w_down, norm_mlp_post, loss_target, m_norm_mix_pre, m_w_in, m_conv_a_w, m_ssm_conv_w, m_ssm_conv_b, m_dt_bias, m_a_log, m_d_skip, m_conv_out_norm, m_ssm_out_norm, m_w_out, m_norm_mix_post, m_norm_mlp_pre, m_w_up, m_w_down, m_norm_mlp_post, v_norm_mix_pre, v_w_in, v_conv_a_w, v_ssm_conv_w, v_ssm_conv_b, v_dt_bias, v_a_log, v_d_skip, v_conv_out_norm, v_ssm_out_norm, v_w_out, v_norm_mix_post, v_norm_mlp_pre, v_w_up, v_w_down, v_norm_mlp_post):
    nb, seq, _ = x.shape
    t = nb * seq
    depth = w_in.shape[0]
    ncol = w_in.shape[2]
    chip = 2 * lax.axis_index("x") + lax.axis_index("y")

    taps = [conv_a_w, ssm_conv_w]
    taps_g = all_gather(taps, "chips", "gather_taps", slot_axis=1, own=False)
    wa_g, ws_g = [lax.dynamic_update_index_in_dim(g, s, chip, 1) for g, s in zip(taps_g, taps)]
    wa_full = jnp.transpose(wa_g, (0, 2, 1, 3)).reshape(depth, 3, D)
    ws_full = jnp.transpose(ws_g, (0, 2, 1, 3)).reshape(depth, 4, XBC)
    lane_pad = lambda a: jnp.pad(a, ((0, 0), (0, CH - a.shape[1])))
    par = jnp.stack([lane_pad(dt_bias), lane_pad(a_log), lane_pad(d_skip)], axis=1)
    par = jnp.pad(par, ((0, 0), (0, 5), (0, 0)))

    def start(l):
        shards = [w_in[l].astype(bf16), w_out[l].astype(bf16), w_up[l].astype(bf16), w_down[l].astype(bf16)]
        zones = [lax.empty((N_CHIPS,) + s.shape, s.dtype) for s in shards]
        return chips_start(shards, zones, _weight_views(shards), f"weights_start_{l}")

    travelling = {0: start(0)}

    def weights_of(l, x_in):
        shards, zones = chips_wait(travelling.pop(l), x_in, f"weights_wait_{l}")
        zones = weights_share(zones, "weights_share")
        win_z, wo_z, wu_z, wd_z = [lax.dynamic_update_index_in_dim(z, s, chip, 0) for z, s in zip(zones, shards)]
        token = None
        if l + 1 < depth:
            travelling[l + 1] = start(l + 1)
            token = travelling[l + 1]["token"]
        win_full = jnp.pad(jnp.transpose(win_z, (1, 0, 2)).reshape(D, N_CHIPS * ncol),
                           ((0, 0), (0, PROJ - N_CHIPS * ncol)))
        return dict(win=win_full, wo=wo_z.reshape(2 * D, D), wu=wu_z, wd=wd_z.reshape(DFF, D),
                    wa=jnp.pad(wa_full[l], ((0, 5), (0, 0))), ws=jnp.pad(ws_full[l], ((0, 4), (0, 0))),
                    bs=ssm_conv_b[l][None], par=par[l], g1=norm_mix_pre[l][None], ga=conv_out_norm[l][None],
                    gs=ssm_out_norm[l][None], g2=norm_mix_post[l][None], g3=norm_mlp_pre[l][None],
                    g4=norm_mlp_post[l][None], token=token)

    core = lax.axis_index("c")
    grads_travelling = {}

    def grads_done(l, g, dx_l):
        mats = [g["win"][None], g["wo"].reshape(N_CHIPS, 2 * D // N_CHIPS, D), g["wu"],
                g["wd"].reshape(N_CHIPS, DFF // N_CHIPS, D)]
        received = pair_send_halves(mats, "grads_to_pair")
        sums = [sum_pair_half(m_, r_, core, "pair_sum") for m_, r_ in zip(mats, received)]
        sums[0] = jnp.transpose(sums[0][0, :, :N_CHIPS * ncol].reshape(D // 2, N_CHIPS, ncol), (1, 0, 2))
        zones = [lax.empty(s.shape, s.dtype) for s in sums]
        grads_travelling[l] = chips_start(sums, zones, _grad_views(sums), f"grads_start_{l}")
        return grads_travelling[l]["token"]

    sse, dx, grads = local_step(x.reshape(t, D), loss_target.reshape(t, D), depth, weights_of, seq, grads_done)
    loss = lax.psum(0.5 / D * sse[0, 0], ("x", "y", "c"))

    big_w = [w_in, w_out, w_up, w_down]
    acc = [lax.empty((depth, bw.shape[1] // 2, bw.shape[2]), f32) for bw in big_w]
    for l in reversed(range(depth)):
        sums, zones = chips_wait(grads_travelling.pop(l), dx, f"grads_wait_{l}")
        acc = [chip_sum_into(acc_a, l, s, z, chip, "chip_sum") for acc_a, s, z in zip(acc, sums, zones)]
    from_sibling = _exchange(acc, [a.shape for a in acc], "pair", lambda r, a, i: r, lambda r, a, i: r,
                             lambda a: acc[a].shape, "grads_from_pair", False)

    small_all = all_gather([_pack_small(grads)], "all", "gather_small")[0]
    small = _unpack_small(sum_slots(small_all, f32, "small_sum", tb=8), depth)
    wa_cols, ws_cols = conv_a_w.shape[2], ssm_conv_w.shape[2]
    par_g = small["par"].reshape(depth, 3, CH)
    g_small = dict(
        norm_mix_pre=small["g1"], conv_out_norm=small["ga"], ssm_out_norm=small["gs"], norm_mix_post=small["g2"],
        norm_mlp_pre=small["g3"], norm_mlp_post=small["g4"], ssm_conv_b=small["bs"],
        conv_a_w=lax.dynamic_slice_in_dim(small["wa"].reshape(depth, 3, D), chip * wa_cols, wa_cols, axis=2),
        ssm_conv_w=lax.dynamic_slice_in_dim(small["ws"].reshape(depth, 4, XBC), chip * ws_cols, ws_cols, axis=2),
        dt_bias=par_g[:, 0, :NH], a_log=par_g[:, 1, :NH], d_skip=par_g[:, 2, :NH])

    given = dict(norm_mix_pre=(norm_mix_pre, m_norm_mix_pre, v_norm_mix_pre), w_in=(w_in, m_w_in, v_w_in),
                 conv_a_w=(conv_a_w, m_conv_a_w, v_conv_a_w), ssm_conv_w=(ssm_conv_w, m_ssm_conv_w, v_ssm_conv_w),
                 ssm_conv_b=(ssm_conv_b, m_ssm_conv_b, v_ssm_conv_b), dt_bias=(dt_bias, m_dt_bias, v_dt_bias),
                 a_log=(a_log, m_a_log, v_a_log), d_skip=(d_skip, m_d_skip, v_d_skip),
                 conv_out_norm=(conv_out_norm, m_conv_out_norm, v_conv_out_norm),
                 ssm_out_norm=(ssm_out_norm, m_ssm_out_norm, v_ssm_out_norm), w_out=(w_out, m_w_out, v_w_out),
                 norm_mix_post=(norm_mix_post, m_norm_mix_post, v_norm_mix_post),
                 norm_mlp_pre=(norm_mlp_pre, m_norm_mlp_pre, v_norm_mlp_pre), w_up=(w_up, m_w_up, v_w_up),
                 w_down=(w_down, m_w_down, v_w_down), norm_mlp_post=(norm_mlp_post, m_norm_mlp_post, v_norm_mlp_post))
    halves = dict(zip(["w_in", "w_out", "w_up", "w_down"], zip(acc, from_sibling)))
    order = ["norm_mix_pre", "w_in", "conv_a_w", "ssm_conv_w", "ssm_conv_b", "dt_bias", "a_log", "d_skip",
             "conv_out_norm", "ssm_out_norm", "w_out", "norm_mix_post", "norm_mlp_pre", "w_up", "w_down",
             "norm_mlp_post"]
    g_out, d_out, m_out, v_out = [], [], [], []
    for n in order:
        wv, mv, vv = given[n]
        if n in halves and wv.shape[-1] % CH:
            own, recv = halves[n]
            gv = jnp.concatenate([jnp.where(core == 0, own, recv), jnp.where(core == 0, recv, own)], axis=1)
            to_cols, to_rows = (lambda a: jnp.transpose(a, (2, 0, 1))), (lambda a: jnp.transpose(a, (1, 2, 0)))
            dlt, m2, v2 = [to_rows(o) for o in adamw_leading(to_cols(wv), to_cols(gv), to_cols(mv), to_cols(vv),
                                                             "adamw_cols")]
        elif n in halves:
            gv, dlt, m2, v2 = adamw_halves(wv, *halves[n], mv, vv, core, "adamw_matrix")
        else:
            gv = g_small[n].reshape(wv.shape)
            two_d = lambda a: a.reshape(-1, a.shape[-1])
            dlt, m2, v2 = adamw(two_d(wv), two_d(gv), two_d(mv), two_d(vv), "adamw")
        g_out.append(gv)
        d_out.append(dlt.reshape(wv.shape))
        m_out.append(m2.reshape(wv.shape))
        v_out.append(v2.reshape(wv.shape))
    return (loss, dx.reshape(nb, seq, D), *g_out, *d_out, *m_out, *v_out)
```

```python
import functools

import jax
import jax.numpy as jnp
from jax import lax
from jax.experimental import pallas as pl
from jax.experimental.pallas import tpu as pltpu

f32, bf16 = jnp.float32, jnp.bfloat16

D = 1024
NH, HP = 16, 64
NG, NS = 2, 128
CH = 128
XBC = D + 2 * NG * NS
DFF = 4 * D
IN_COLS = 3 * D + D + XBC + NH
PROJ = 5760
COL_Z, COL_XBC, COL_DT = 3 * D, 4 * D, 4 * D + XBC
EPS = 1e-6
HALO = 8
VMEM_LIMIT = 56 * 2**20
MESH = pl.DeviceIdType.MESH

LR, B1, B2, AEPS, WD, STEP = 0.001, 0.9, 0.999, 1e-08, 0.01, 10


def _cparams(n_axes):
    return pltpu.CompilerParams(dimension_semantics=("arbitrary",) * n_axes, vmem_limit_bytes=VMEM_LIMIT)


def _sds(shape, dtype):
    return jax.ShapeDtypeStruct(tuple(shape), dtype)


def _token_spec(token):
    return [] if token is None else [pl.BlockSpec(memory_space=pl.ANY)]


def _token_arg(token):
    return [] if token is None else [token]


def _rms_fwd(x, g):
    r = lax.rsqrt(jnp.mean(x * x, axis=-1, keepdims=True) + EPS)
    return x * r * g


def _rms_bwd(x, g, dy):
    r = lax.rsqrt(jnp.mean(x * x, axis=-1, keepdims=True) + EPS)
    xh = x * r
    gdy = dy * g
    dx = r * (gdy - xh * jnp.mean(xh * gdy, axis=-1, keepdims=True))
    return dx, dy * xh


def _accum(ref, part, first):
    @pl.when(first)
    def _():
        ref[...] = part

    @pl.when(jnp.logical_not(first))
    def _():
        ref[...] += part


def _dot_nt(a, b):
    return lax.dot_general(a, b, (((1,), (1,)), ((), ())), preferred_element_type=f32)


def _dot_tn(a, b):
    return lax.dot_general(a, b, (((0,), (0,)), ((), ())), preferred_element_type=f32)


def _dot(a, b):
    return jnp.dot(a, b, preferred_element_type=f32)


def _split_dot(x, e_bf, n_split, nt=False):
    acc = None
    rem = x
    for s in range(n_split):
        hi = rem.astype(bf16)
        term = _dot_nt(hi, e_bf) if nt else _dot(hi, e_bf)
        acc = term if acc is None else acc + term
        if s + 1 < n_split:
            rem = rem - hi.astype(f32)
    return acc


def _sigmoid(x):
    return 0.5 * jnp.tanh(0.5 * x) + 0.5


def norm_matmul(x, g, w, tm, tn, out_dtype, name, token=None):
    t = x.shape[0]
    if w.ndim == 3:
        assert w.shape[2] == tn
        n = w.shape[0] * tn
        w_spec = pl.BlockSpec((None, D, tn), lambda i, j: (j, 0, 0))
    else:
        n = w.shape[1]
        w_spec = pl.BlockSpec((D, tn), lambda i, j: (0, j))

    def body(x_ref, g_ref, w_ref, *rest):
        o_ref, h_ref = rest[-2:]

        @pl.when(pl.program_id(1) == 0)
        def _():
            h_ref[...] = _rms_fwd(x_ref[...], g_ref[...]).astype(bf16)

        o_ref[...] = _dot(h_ref[...], w_ref[...]).astype(out_dtype)

    return pl.pallas_call(
        body, name=name, grid=(t // tm, n // tn),
        in_specs=[pl.BlockSpec((tm, D), lambda i, j: (i, 0)), pl.BlockSpec((1, D), lambda i, j: (0, 0)), w_spec]
        + _token_spec(token),
        out_specs=[pl.BlockSpec((tm, tn), lambda i, j: (i, j)), pl.BlockSpec((tm, D), lambda i, j: (i, 0))],
        out_shape=[_sds((t, n), out_dtype), _sds((t, D), bf16)],
        compiler_params=_cparams(2))(x, g, w, *_token_arg(token))


def matmul_postnorm(a, w, xres, g, tm, relu2, name):
    t, k = a.shape

    def body(a_ref, w_ref, xr_ref, g_ref, y_ref, xo_ref):
        av = a_ref[...]
        if relu2:
            af = jnp.maximum(av.astype(f32), 0.0)
            av = (af * af).astype(bf16)
        y = _dot(av, w_ref[...])
        y_ref[...] = y
        xo_ref[...] = xr_ref[...] + _rms_fwd(y, g_ref[...])

    return pl.pallas_call(
        body, name=name, grid=(t // tm,),
        in_specs=[pl.BlockSpec((tm, k), lambda i: (i, 0)), pl.BlockSpec((k, D), lambda i: (0, 0)),
                  pl.BlockSpec((tm, D), lambda i: (i, 0)), pl.BlockSpec((1, D), lambda i: (0, 0))],
        out_specs=[pl.BlockSpec((tm, D), lambda i: (i, 0)), pl.BlockSpec((tm, D), lambda i: (i, 0))],
        out_shape=[_sds((t, D), f32), _sds((t, D), f32)],
        compiler_params=_cparams(1))(a, w, xres, g)


def postnorm_bwd_matmul(y, g, dxo, w, fp, tm, tn, out_dtype, name, token=None):
    t, n = y.shape[0], w.shape[0]
    relu = fp is not None

    def body(*refs):
        y_ref, g_ref, dxo_ref, w_ref = refs[:4]
        fp_ref = refs[4] if relu else None
        dy_ref, dg_ref, da_ref = refs[-3:]
        i, j = pl.program_id(0), pl.program_id(1)

        @pl.when(j == 0)
        def _():
            dx, dgc = _rms_bwd(y_ref[...], g_ref[...], dxo_ref[...])
            dy_ref[...] = dx.astype(bf16)
            _accum(dg_ref, jnp.sum(dgc, axis=0, keepdims=True), i == 0)

        da = _dot_nt(dy_ref[...], w_ref[...])
        if relu:
            da = da * (2.0 * jnp.maximum(fp_ref[...].astype(f32), 0.0))
        da_ref[...] = da.astype(out_dtype)

    in_specs = [pl.BlockSpec((tm, D), lambda i, j: (i, 0)), pl.BlockSpec((1, D), lambda i, j: (0, 0)),
                pl.BlockSpec((tm, D), lambda i, j: (i, 0)), pl.BlockSpec((tn, D), lambda i, j: (j, 0))]
    args = [y, g, dxo, w]
    if relu:
        in_specs.append(pl.BlockSpec((tm, tn), lambda i, j: (i, j)))
        args.append(fp)
    in_specs += _token_spec(token)
    args += _token_arg(token)
    return pl.pallas_call(
        body, name=name, grid=(t // tm, n // tn), in_specs=in_specs,
        out_specs=[pl.BlockSpec((tm, D), lambda i, j: (i, 0)), pl.BlockSpec((1, D), lambda i, j: (0, 0)),
                   pl.BlockSpec((tm, tn), lambda i, j: (i, j))],
        out_shape=[_sds((t, D), bf16), _sds((1, D), f32), _sds((t, n), out_dtype)],
        compiler_params=_cparams(2))(*args)


def matmul_prenorm_bwd(da, w, x, g, dxo, tm, name, token=None):
    t, k = da.shape
    blocked = w.ndim == 3

    def body(da_ref, w_ref, x_ref, g_ref, dxo_ref, *rest):
        dx_ref, dg_ref = rest[-2:]
        if blocked:
            kc = w.shape[2]
            dh = _dot_nt(da_ref[:, 0:kc], w_ref[0])
            for q in range(1, w.shape[0]):
                dh = dh + _dot_nt(da_ref[:, q * kc:(q + 1) * kc], w_ref[q])
        else:
            dh = _dot_nt(da_ref[...], w_ref[...])
        dxn, dgc = _rms_bwd(x_ref[...], g_ref[...], dh)
        dx_ref[...] = dxo_ref[...] + dxn
        _accum(dg_ref, jnp.sum(dgc, axis=0, keepdims=True), pl.program_id(0) == 0)

    w_spec = pl.BlockSpec(w.shape, (lambda i: (0, 0, 0)) if blocked else (lambda i: (0, 0)))
    return pl.pallas_call(
        body, name=name, grid=(t // tm,),
        in_specs=[pl.BlockSpec((tm, k), lambda i: (i, 0)), w_spec,
                  pl.BlockSpec((tm, D), lambda i: (i, 0)), pl.BlockSpec((1, D), lambda i: (0, 0)),
                  pl.BlockSpec((tm, D), lambda i: (i, 0))] + _token_spec(token),
        out_specs=[pl.BlockSpec((tm, D), lambda i: (i, 0)), pl.BlockSpec((1, D), lambda i: (0, 0))],
        out_shape=[_sds((t, D), f32), _sds((1, D), f32)],
        compiler_params=_cparams(1))(da, w, x, g, dxo, *_token_arg(token))


def matmul_tn(a, b, tm, tn, relu2, name, col_blocks=False):
    t, m = a.shape
    n = b.shape[1]
    if col_blocks:
        out_spec, out_shape = pl.BlockSpec((None, tm, tn), lambda i, j: (j, i, 0)), _sds((n // tn, m, tn), f32)
    else:
        out_spec, out_shape = pl.BlockSpec((tm, tn), lambda i, j: (i, j)), _sds((m, n), f32)

    def body(a_ref, b_ref, o_ref, at_ref):
        @pl.when(pl.program_id(1) == 0)
        def _():
            av = a_ref[...]
            if relu2:
                af = jnp.maximum(av.astype(f32), 0.0)
                av = (af * af).astype(bf16)
            at_ref[...] = av.T

        o_ref[...] = _dot(at_ref[...], b_ref[...])

    return pl.pallas_call(
        body, name=name, grid=(m // tm, n // tn),
        in_specs=[pl.BlockSpec((t, tm), lambda i, j: (0, i)), pl.BlockSpec((t, tn), lambda i, j: (0, j))],
        out_specs=out_spec, out_shape=out_shape,
        scratch_shapes=[pltpu.VMEM((tm, t), bf16)],
        compiler_params=_cparams(2))(a, b)


ROWS_A = 16
ROWS_B = 32
UNROLL = 4


def _past(win, s):
    return (win if s == 0 else pltpu.roll(win, s, 0))[HALO:]


def _future(win, s):
    n = win.shape[0]
    return (win if s == 0 else pltpu.roll(win, n - s, 0))[:n - HALO]


def _fold8(v):
    return v.reshape(v.shape[0] // 8, 8, v.shape[1]).sum(axis=0)


def _halo_prev(tb, col):
    return lambda i: (jnp.maximum(i * (tb // HALO) - 1, 0), col)


def _halo_next(tb, col, t):
    return lambda i: (jnp.minimum((i + 1) * (tb // HALO), t // HALO - 1), col)


def group_a_fwd(proj, wa, g, seq, tb, name):
    t = proj.shape[0]
    bps = seq // tb

    def body(xa_ref, ca_ref, ba_ref, xah_ref, cah_ref, wa_ref, g_ref, o_ref, u_scr):
        first = (pl.program_id(0) % bps) == 0
        u_scr[0:HALO, :] = jnp.where(first, 0.0, cah_ref[...] * xah_ref[...])
        w, gv = wa_ref[...], g_ref[...]

        def chunk(i, carry):
            r = pl.multiple_of(i * ROWS_A, ROWS_A)
            rows = pl.ds(r, ROWS_A)
            u_scr[pl.ds(pl.multiple_of(HALO + r, HALO), ROWS_A), :] = ca_ref[rows, :] * xa_ref[rows, :]
            win = u_scr[pl.ds(r, ROWS_A + HALO), :]
            cv = w[2:3] * _past(win, 0) + w[1:2] * _past(win, 1) + w[0:1] * _past(win, 2)
            o_ref[rows, :] = _rms_fwd(ba_ref[rows, :] * cv, gv).astype(bf16)
            return carry

        lax.fori_loop(0, tb // ROWS_A, chunk, 0, unroll=UNROLL)

    blk = lambda c: pl.BlockSpec((tb, D), lambda i: (i, c))
    return pl.pallas_call(
        body, name=name, grid=(t // tb,),
        in_specs=[blk(0), blk(1), blk(2),
                  pl.BlockSpec((HALO, D), _halo_prev(tb, 0)), pl.BlockSpec((HALO, D), _halo_prev(tb, 1)),
                  pl.BlockSpec((8, D), lambda i: (0, 0)), pl.BlockSpec((1, D), lambda i: (0, 0))],
        out_specs=pl.BlockSpec((tb, D), lambda i: (i, 0)),
        out_shape=_sds((t, 2 * D), bf16),
        scratch_shapes=[pltpu.VMEM((tb + HALO, D), f32)],
        compiler_params=_cparams(1))(proj, proj, proj, proj, proj, wa, g)


def group_a_bwd(proj, dcat, wa, g, seq, tb, name):
    t = proj.shape[0]
    bps = seq // tb

    def body(xa_ref, ca_ref, ba_ref, dy_ref, xap_ref, cap_ref, xan_ref, can_ref, ban_ref, dyn_ref, wa_ref, g_ref,
             dp_ref, dwa_ref, dg_ref, u_scr, d_scr, acc_scr):
        i = pl.program_id(0)
        first = (i % bps) == 0
        last = (i % bps) == bps - 1
        w = wa_ref[...]
        gv = g_ref[...]
        u_scr[0:HALO, :] = jnp.where(first, 0.0, cap_ref[...] * xap_ref[...])
        u_scr[HALO + tb:2 * HALO + tb, :] = can_ref[...] * xan_ref[...]
        acc_scr[...] = jnp.zeros_like(acc_scr)

        def forward_part(n, carry):
            r = pl.multiple_of(n * ROWS_A, ROWS_A)
            rows = pl.ds(r, ROWS_A)
            ba = ba_ref[rows, :]
            u_scr[pl.ds(pl.multiple_of(HALO + r, HALO), ROWS_A), :] = ca_ref[rows, :] * xa_ref[rows, :]
            win = u_scr[pl.ds(r, ROWS_A + HALO), :]
            u = [_past(win, s) for s in range(3)]
            cv = w[2:3] * u[0] + w[1:2] * u[1] + w[0:1] * u[2]
            dya, dgc = _rms_bwd(ba * cv, gv, dy_ref[rows, :])
            dcv = dya * ba
            d_scr[rows, :] = dcv
            dp_ref[rows, 2 * D:3 * D] = (dya * cv).astype(bf16)
            acc_scr[0:8, :] += _fold8(dgc)
            for k in range(3):
                acc_scr[8 + 8 * k:16 + 8 * k, :] += _fold8(dcv * u[2 - k])
            return carry

        lax.fori_loop(0, tb // ROWS_A, forward_part, 0, unroll=UNROLL)

        start = HALO + tb
        cvn = (w[2:3] * u_scr[pl.ds(start, HALO), :] + w[1:2] * u_scr[pl.ds(start - 1, HALO), :]
               + w[0:1] * u_scr[pl.ds(start - 2, HALO), :])
        ban = ban_ref[...]
        dyan, _ = _rms_bwd(ban * cvn, gv, dyn_ref[...])
        d_scr[tb:tb + HALO, :] = jnp.where(last, 0.0, dyan * ban)

        def backward_part(n, carry):
            r = pl.multiple_of(n * ROWS_A, ROWS_A)
            rows = pl.ds(r, ROWS_A)
            win = d_scr[pl.ds(r, ROWS_A + HALO), :]
            du = w[2:3] * _future(win, 0) + w[1:2] * _future(win, 1) + w[0:1] * _future(win, 2)
            dp_ref[rows, 0:D] = (du * ca_ref[rows, :]).astype(bf16)
            dp_ref[rows, D:2 * D] = (du * xa_ref[rows, :]).astype(bf16)
            return carry

        lax.fori_loop(0, tb // ROWS_A, backward_part, 0, unroll=UNROLL)

        row = lax.broadcasted_iota(jnp.int32, (8, D), 0)
        dw = jnp.zeros((8, D), f32)
        for k in range(3):
            dw = jnp.where(row == k, jnp.sum(acc_scr[8 + 8 * k:16 + 8 * k, :], axis=0, keepdims=True), dw)
        _accum(dwa_ref, dw, i == 0)
        _accum(dg_ref, jnp.sum(acc_scr[0:8, :], axis=0, keepdims=True), i == 0)

    blk = lambda c: pl.BlockSpec((tb, D), lambda i: (i, c))
    prv = lambda c: pl.BlockSpec((HALO, D), _halo_prev(tb, c))
    nxt = lambda c: pl.BlockSpec((HALO, D), _halo_next(tb, c, t))
    return pl.pallas_call(
        body, name=name, grid=(t // tb,),
        in_specs=[blk(0), blk(1), blk(2), blk(0), prv(0), prv(1), nxt(0), nxt(1), nxt(2), nxt(0),
                  pl.BlockSpec((8, D), lambda i: (0, 0)), pl.BlockSpec((1, D), lambda i: (0, 0))],
        out_specs=[pl.BlockSpec((tb, 3 * D), lambda i: (i, 0)), pl.BlockSpec((8, D), lambda i: (0, 0)),
                   pl.BlockSpec((1, D), lambda i: (0, 0))],
        out_shape=[_sds((t, PROJ), bf16), _sds((8, D), f32), _sds((1, D), f32)],
        scratch_shapes=[pltpu.VMEM((tb + 2 * HALO, D), f32), pltpu.VMEM((tb + HALO, D), f32), pltpu.VMEM((32, D), f32)],
        compiler_params=_cparams(1))(proj, proj, proj, dcat, proj, proj, proj, proj, proj, dcat, wa, g)


CB = 512
XBC_BLK0 = COL_XBC // CB


def conv_b_fwd(proj, ws, bs, seq, tb, name):
    t = proj.shape[0]
    bps = seq // tb

    def body(x_ref, xp_ref, w_ref, b_ref, o_ref, x_scr):
        first = (pl.program_id(1) % bps) == 0
        x_scr[0:HALO, :] = jnp.where(first, 0.0, xp_ref[...])
        w, bias = w_ref[...], b_ref[...]

        def chunk(n, carry):
            r = pl.multiple_of(n * ROWS_B, ROWS_B)
            rows = pl.ds(r, ROWS_B)
            x_scr[pl.ds(pl.multiple_of(HALO + r, HALO), ROWS_B), :] = x_ref[rows, :]
            win = x_scr[pl.ds(r, ROWS_B + HALO), :]
            xc = bias + w[3:4] * _past(win, 0)
            for k in range(3):
                xc = xc + w[k:k + 1] * _past(win, 3 - k)
            o_ref[rows, :] = xc * _sigmoid(xc)
            return carry

        lax.fori_loop(0, tb // ROWS_B, chunk, 0, unroll=UNROLL)

    return pl.pallas_call(
        body, name=name, grid=(XBC // CB, t // tb),
        in_specs=[pl.BlockSpec((tb, CB), lambda j, i: (i, XBC_BLK0 + j)),
                  pl.BlockSpec((HALO, CB), lambda j, i: (jnp.maximum(i * (tb // HALO) - 1, 0), XBC_BLK0 + j)),
                  pl.BlockSpec((8, CB), lambda j, i: (0, j)), pl.BlockSpec((1, CB), lambda j, i: (0, j))],
        out_specs=pl.BlockSpec((tb, CB), lambda j, i: (i, j)),
        out_shape=_sds((t, XBC), f32),
        scratch_shapes=[pltpu.VMEM((tb + HALO, CB), f32)],
        compiler_params=_cparams(2))(proj, proj, ws, bs)


def conv_b_bwd(proj, dxs, ws, bs, dproj, seq, tb, name):
    t = proj.shape[0]
    bps = seq // tb

    def body(x_ref, xp_ref, xn_ref, d_ref, dn_ref, w_ref, b_ref, dproj_ref, dx_ref, dw_ref, db_ref, x_scr, d_scr,
             acc_scr):
        i = pl.program_id(1)
        first = (i % bps) == 0
        last = (i % bps) == bps - 1
        w = w_ref[...]
        bias = b_ref[...]
        x_scr[0:HALO, :] = jnp.where(first, 0.0, xp_ref[...])
        x_scr[HALO + tb:2 * HALO + tb, :] = xn_ref[...]
        acc_scr[...] = jnp.zeros_like(acc_scr)

        def dsilu(xc, d):
            sg = _sigmoid(xc)
            return d * (sg * (1.0 + xc * (1.0 - sg)))

        def forward_part(n, carry):
            r = pl.multiple_of(n * ROWS_B, ROWS_B)
            rows = pl.ds(r, ROWS_B)
            x_scr[pl.ds(pl.multiple_of(HALO + r, HALO), ROWS_B), :] = x_ref[rows, :]
            win = x_scr[pl.ds(r, ROWS_B + HALO), :]
            xs = [_past(win, s) for s in range(4)]
            xc = bias + w[3:4] * xs[0]
            for k in range(3):
                xc = xc + w[k:k + 1] * xs[3 - k]
            dxc = dsilu(xc, d_ref[rows, :])
            d_scr[rows, :] = dxc
            acc_scr[0:8, :] += _fold8(dxc)
            for k in range(4):
                acc_scr[8 + 8 * k:16 + 8 * k, :] += _fold8(dxc * xs[3 - k])
            return carry

        lax.fori_loop(0, tb // ROWS_B, forward_part, 0, unroll=UNROLL)

        start = HALO + tb
        xcn = bias + w[3:4] * x_scr[pl.ds(start, HALO), :]
        for k in range(3):
            xcn = xcn + w[k:k + 1] * x_scr[pl.ds(start - 3 + k, HALO), :]
        d_scr[tb:tb + HALO, :] = jnp.where(last, 0.0, dsilu(xcn, dn_ref[...]))

        def backward_part(n, carry):
            r = pl.multiple_of(n * ROWS_B, ROWS_B)
            win = d_scr[pl.ds(r, ROWS_B + HALO), :]
            dx = w[3:4] * _future(win, 0)
            for k in range(3):
                dx = dx + w[k:k + 1] * _future(win, 3 - k)
            dx_ref[pl.ds(r, ROWS_B), :] = dx.astype(bf16)
            return carry

        lax.fori_loop(0, tb // ROWS_B, backward_part, 0, unroll=UNROLL)

        row = lax.broadcasted_iota(jnp.int32, (8, CB), 0)
        dw = jnp.zeros((8, CB), f32)
        for k in range(4):
            dw = jnp.where(row == k, jnp.sum(acc_scr[8 + 8 * k:16 + 8 * k, :], axis=0, keepdims=True), dw)
        _accum(dw_ref, dw, i == 0)
        _accum(db_ref, jnp.sum(acc_scr[0:8, :], axis=0, keepdims=True), i == 0)

    nh = t // HALO
    return pl.pallas_call(
        body, name=name, grid=(XBC // CB, t // tb),
        in_specs=[pl.BlockSpec((tb, CB), lambda j, i: (i, XBC_BLK0 + j)),
                  pl.BlockSpec((HALO, CB), lambda j, i: (jnp.maximum(i * (tb // HALO) - 1, 0), XBC_BLK0 + j)),
                  pl.BlockSpec((HALO, CB), lambda j, i: (jnp.minimum((i + 1) * (tb // HALO), nh - 1), XBC_BLK0 + j)),
                  pl.BlockSpec((tb, CB), lambda j, i: (i, j)),
                  pl.BlockSpec((HALO, CB), lambda j, i: (jnp.minimum((i + 1) * (tb // HALO), nh - 1), j)),
                  pl.BlockSpec((8, CB), lambda j, i: (0, j)), pl.BlockSpec((1, CB), lambda j, i: (0, j)),
                  pl.BlockSpec(memory_space=pl.ANY)],
        out_specs=[pl.BlockSpec((tb, CB), lambda j, i: (i, XBC_BLK0 + j)), pl.BlockSpec((8, CB), lambda j, i: (0, j)),
                   pl.BlockSpec((1, CB), lambda j, i: (0, j))],
        out_shape=[_sds((t, PROJ), bf16), _sds((8, XBC), f32), _sds((1, XBC), f32)],
        input_output_aliases={7: 0},
        scratch_shapes=[pltpu.VMEM((tb + 2 * HALO, CB), f32), pltpu.VMEM((tb + HALO, CB), f32),
                        pltpu.VMEM((40, CB), f32)],
        compiler_params=_cparams(2))(proj, proj, proj, dxs, dxs, ws, bs, dproj)


def place_columns(buf, part, col_block, tb, name):
    t, wdt = part.shape

    def body(p_ref, buf_ref, o_ref):
        o_ref[...] = p_ref[...]

    return pl.pallas_call(
        body, name=name, grid=(t // tb,),
        in_specs=[pl.BlockSpec((tb, wdt), lambda i: (i, 0)), pl.BlockSpec(memory_space=pl.ANY)],
        out_specs=pl.BlockSpec((tb, wdt), lambda i: (i, col_block)), out_shape=_sds(buf.shape, buf.dtype),
        input_output_aliases={1: 0}, compiler_params=_cparams(1))(part, buf)


GW = D // NG


def _ssd_consts():
    head_of_lane = jnp.arange(D) // HP
    expand = (jnp.arange(CH)[:, None] == head_of_lane[None, :]).astype(bf16)
    tri = (jnp.arange(CH)[:, None] >= jnp.arange(CH)[None, :]).astype(f32)
    return expand, tri


def _ssd_common(par_ref, dtr_ref, e_ref, tri_ref):
    par = par_ref[...]
    dtb, alog, dsk = par[0:1], par[1:2], par[2:3]
    lane = lax.broadcasted_iota(jnp.int32, (CH, CH), 1)
    a = -jnp.exp(alog)
    dtr = dtr_ref[...] + dtb
    sp = jnp.maximum(dtr, 0.0) + jnp.log(1.0 + jnp.exp(-jnp.abs(dtr)))
    dt = jnp.where(lane < NH, sp, 0.0)
    cs = jnp.dot(tri_ref[...], dt * a, precision=lax.Precision.HIGHEST, preferred_element_type=f32)
    cs_last = cs[CH - 1:CH, :]
    dte = jnp.exp(cs_last - cs)
    ecs = jnp.exp(cs)
    ecl = jnp.exp(cs_last)
    e = e_ref[...]
    row8 = lax.broadcasted_iota(jnp.int32, (8, CH), 0)
    r8 = _split_dot(jnp.where(row8 == 0, ecl, jnp.where(row8 == 1, dsk, 0.0)), e, 3)
    return dict(a=a, dtr=dtr, dt=dt, cs=cs, cst=cs.T, dte=dte, ecs=ecs, ecl=ecl, e=e, lane=lane,
                dt_x=_split_dot(dt, e, 3), dte_x=_split_dot(dte, e, 3), ecs_x=_split_dot(ecs, e, 3),
                ecl_x=r8[0:1], dsk_x=r8[1:2])


def _decay_matrix(c, h):
    li = lax.broadcasted_iota(jnp.int32, (CH, CH), 0)
    seg = c["cs"][:, h:h + 1] - c["cst"][h:h + 1, :]
    return jnp.exp(jnp.where(li >= c["lane"], seg, -jnp.inf))


def _gate_norm_fwd(y, z, gs):
    zg = z * _sigmoid(z)
    yg = y * zg
    return jnp.concatenate([_rms_fwd(yg[:, k * GW:(k + 1) * GW], gs[:, k * GW:(k + 1) * GW]) for k in range(NG)], axis=1)


def ssd_fwd(xbcs, proj, par, gs, cat, seq, name):
    t = xbcs.shape[0]
    nc = seq // CH
    expand, tri = _ssd_consts()

    def body(xs_ref, b_ref, c_ref, dtr_ref, z_ref, par_ref, e_ref, tri_ref, gs_ref, cat_ref, yn_ref, y_ref, st_ref,
             p_scr, yd_scr):
        @pl.when(pl.program_id(0) % nc == 0)
        def _():
            p_scr[...] = jnp.zeros_like(p_scr)

        c = _ssd_common(par_ref, dtr_ref, e_ref, tri_ref)
        xs = xs_ref[...]
        xdt = xs * c["dt_x"]
        xdt_b = xdt.astype(bf16)
        xdte_b = (xdt * c["dte_x"]).astype(bf16)
        p = p_scr[...]
        st_ref[0] = p
        p_b = p.astype(bf16)
        lo = c["lane"] < HP
        for g in range(NG):
            bg = b_ref[:, g * NS:(g + 1) * NS].astype(bf16)
            cg = c_ref[:, g * NS:(g + 1) * NS].astype(bf16)
            gmat = _dot_nt(cg, bg)
            for q in range(GW // CH):
                col = g * GW + q * CH
                xp = xdt_b[:, col:col + CH]
                h0 = col // HP
                m0 = (gmat * _decay_matrix(c, h0)).astype(bf16)
                m1 = (gmat * _decay_matrix(c, h0 + 1)).astype(bf16)
                yd_scr[:, col:col + CH] = (_dot(m0, jnp.where(lo, xp, jnp.zeros_like(xp)))
                                           + _dot(m1, jnp.where(lo, jnp.zeros_like(xp), xp)))
            gsl = slice(g * GW, (g + 1) * GW)
            yoff = _dot(cg, p_b[:, gsl]) * c["ecs_x"][:, gsl]
            yd_scr[:, gsl] = yd_scr[:, gsl] + yoff
            p_scr[:, gsl] = p[:, gsl] * c["ecl_x"][:, gsl] + _dot_tn(bg, xdte_b[:, gsl])
        y = yd_scr[...] + c["dsk_x"] * xs
        y_ref[...] = y
        yn_ref[...] = _gate_norm_fwd(y, z_ref[...], gs_ref[...]).astype(bf16)

    nb = t // CH
    return pl.pallas_call(
        body, name=name, grid=(nb,),
        in_specs=[pl.BlockSpec((CH, D), lambda i: (i, 0)),
                  pl.BlockSpec((CH, NG * NS), lambda i: (i, D // (NG * NS))),
                  pl.BlockSpec((CH, NG * NS), lambda i: (i, D // (NG * NS) + 1)),
                  pl.BlockSpec((CH, CH), lambda i: (i, COL_DT // CH)),
                  pl.BlockSpec((CH, D), lambda i: (i, COL_Z // D)),
                  pl.BlockSpec((8, CH), lambda i: (0, 0)), pl.BlockSpec((CH, D), lambda i: (0, 0)),
                  pl.BlockSpec((CH, CH), lambda i: (0, 0)), pl.BlockSpec((1, D), lambda i: (0, 0)),
                  pl.BlockSpec(memory_space=pl.ANY)],
        out_specs=[pl.BlockSpec((CH, D), lambda i: (i, 1)), pl.BlockSpec((CH, D), lambda i: (i, 0)),
                   pl.BlockSpec((1, NS, D), lambda i: (i, 0, 0))],
        out_shape=[_sds((t, 2 * D), bf16), _sds((t, D), f32), _sds((nb, NS, D), f32)],
        input_output_aliases={9: 0},
        scratch_shapes=[pltpu.VMEM((NS, D), f32), pltpu.VMEM((CH, D), f32)],
        compiler_params=_cparams(1))(xbcs, xbcs, xbcs, proj, proj, par, expand, tri, gs, cat)


def ssd_bwd(xbcs, proj, ypre, states, dcat, par, gs, dproj, seq, name):
    t = xbcs.shape[0]
    nc = seq // CH
    expand, tri = _ssd_consts()

    def body(xs_ref, b_ref, c_ref, dtr_ref, z_ref, y_ref, st_ref, dyn_ref, par_ref, e_ref, tri_ref, gs_ref, dproj_ref,
             dx_ref, dz_ref, ddt_ref, dpar_ref, dgs_ref, dp_scr, dxdt_scr):
        i = pl.program_id(0)

        @pl.when(i % nc == 0)
        def _():
            dp_scr[...] = jnp.zeros_like(dp_scr)

        c = _ssd_common(par_ref, dtr_ref, e_ref, tri_ref)
        e = c["e"]
        lane = c["lane"]
        sub = lax.broadcasted_iota(jnp.int32, (CH, CH), 0)
        xs = xs_ref[...]
        xdt = xs * c["dt_x"]
        xdt_b = xdt.astype(bf16)
        xdte_b = (xdt * c["dte_x"]).astype(bf16)
        p = st_ref[0]
        p_b = p.astype(bf16)
        dpn = dp_scr[...]
        dpn_b = dpn.astype(bf16)

        y, z, gs_v = y_ref[...], z_ref[...], gs_ref[...]
        zs = _sigmoid(z)
        zg = z * zs
        yg = y * zg
        parts, gparts = [], []
        for k in range(NG):
            sl = slice(k * GW, (k + 1) * GW)
            dxk, dgk = _rms_bwd(yg[:, sl], gs_v[:, sl], dyn_ref[:, sl])
            parts.append(dxk)
            gparts.append(dgk)
        dyg = jnp.concatenate(parts, axis=1)
        dgs_rows = jnp.concatenate(gparts, axis=1)
        dy = dyg * zg
        dz_ref[...] = (dyg * y * (zs * (1.0 + z * (1.0 - zs)))).astype(bf16)
        dy_b = dy.astype(bf16)
        dq_b = (dy * c["ecs_x"]).astype(bf16)

        lo = lane < HP
        dcs = jnp.zeros((CH, CH), f32)
        dcst = jnp.zeros((CH, CH), f32)
        for g in range(NG):
            gsl = slice(g * GW, (g + 1) * GW)
            bg = b_ref[:, g * NS:(g + 1) * NS].astype(bf16)
            cg = c_ref[:, g * NS:(g + 1) * NS].astype(bf16)
            gmat = _dot_nt(cg, bg)
            dgm = jnp.zeros((CH, CH), f32)
            for q in range(GW // CH):
                col = g * GW + q * CH
                xp = xdt_b[:, col:col + CH]
                dyp = dy_b[:, col:col + CH]
                acc = None
                for hh in range(2):
                    h = col // HP + hh
                    keep = lo if hh == 0 else jnp.logical_not(lo)
                    dyh = jnp.where(keep, dyp, jnp.zeros_like(dyp))
                    dec = _decay_matrix(c, h)
                    m = gmat * dec
                    dm = _dot_nt(dyh, xp)
                    dseg = dm * m
                    dcs = dcs + jnp.where(lane == h, jnp.sum(dseg, axis=1, keepdims=True), 0.0)
                    dcst = dcst + jnp.where(sub == h, jnp.sum(dseg, axis=0, keepdims=True), 0.0)
                    dgm = dgm + dm * dec
                    term = _dot_tn(m.astype(bf16), dyh)
                    acc = term if acc is None else acc + term
                dxdt_scr[:, col:col + CH] = acc
            dgm_b = dgm.astype(bf16)
            bds = _dot(bg, dpn_b[:, gsl])
            dxdt_scr[:, gsl] = dxdt_scr[:, gsl] + c["dte_x"][:, gsl] * bds
            dc_g = _dot(dgm_b, bg) + _dot_nt(dq_b[:, gsl], p_b[:, gsl])
            db_g = _dot_tn(dgm_b, cg) + _dot_nt(xdte_b[:, gsl], dpn_b[:, gsl])
            dx_ref[:, D + g * NS:D + (g + 1) * NS] = db_g
            dx_ref[:, D + NG * NS + g * NS:D + NG * NS + (g + 1) * NS] = dc_g
            dp_scr[:, gsl] = dpn[:, gsl] * c["ecl_x"][:, gsl] + _dot_tn(cg, dq_b[:, gsl])
            q_g = _dot(cg, p_b[:, gsl])
            e_g = e[:, gsl]
            dcs = dcs + c["ecs"] * _split_dot(dy[:, gsl] * q_g, e_g, 2, nt=True)
            ddte = _split_dot(xdt[:, gsl] * bds, e_g, 2, nt=True) * c["dte"]
            dcs = dcs - ddte
            dcs = dcs + jnp.where(sub == CH - 1, jnp.sum(ddte, axis=0, keepdims=True), 0.0)

        decl = _split_dot(jnp.broadcast_to(jnp.sum(dpn * p, axis=0, keepdims=True), (8, D)), e, 2, nt=True)[0:1]
        dcs = dcs + jnp.where(sub == CH - 1, c["ecl"] * decl, 0.0)
        dcs = dcs - dcst.T
        dadt = lax.dot_general(tri_ref[...], dcs, (((0,), (0,)), ((), ())), precision=lax.Precision.HIGHEST,
                               preferred_element_type=f32)
        dxdt = dxdt_scr[...]
        ddt = dadt * c["a"] + _split_dot(dxdt * xs, e, 2, nt=True)
        ddtr = jnp.where(lane < NH, ddt * _sigmoid(c["dtr"]), 0.0)
        ddt_ref[...] = ddtr.astype(bf16)
        dx_ref[:, 0:D] = dxdt * c["dt_x"] + c["dsk_x"] * dy
        dsk = _split_dot(jnp.broadcast_to(jnp.sum(dy * xs, axis=0, keepdims=True), (8, D)), e, 2, nt=True)[0:1]
        dalog = jnp.sum(dadt * c["dt"], axis=0, keepdims=True) * c["a"]
        row8 = lax.broadcasted_iota(jnp.int32, (8, CH), 0)
        dpar = jnp.where(row8 == 0, jnp.sum(ddtr, axis=0, keepdims=True),
                         jnp.where(row8 == 1, dalog, jnp.where(row8 == 2, dsk, 0.0)))
        dpar = jnp.where(lax.broadcasted_iota(jnp.int32, (8, CH), 1) < NH, dpar, 0.0)
        _accum(dpar_ref, dpar, i == 0)
        _accum(dgs_ref, jnp.sum(dgs_rows, axis=0, keepdims=True), i == 0)

    nb = t // CH
    rev = lambda i: (i // nc) * nc + (nc - 1 - i % nc)
    return pl.pallas_call(
        body, name=name, grid=(nb,),
        in_specs=[pl.BlockSpec((CH, D), lambda i: (rev(i), 0)),
                  pl.BlockSpec((CH, NG * NS), lambda i: (rev(i), D // (NG * NS))),
                  pl.BlockSpec((CH, NG * NS), lambda i: (rev(i), D // (NG * NS) + 1)),
                  pl.BlockSpec((CH, CH), lambda i: (rev(i), COL_DT // CH)),
                  pl.BlockSpec((CH, D), lambda i: (rev(i), COL_Z // D)),
                  pl.BlockSpec((CH, D), lambda i: (rev(i), 0)),
                  pl.BlockSpec((1, NS, D), lambda i: (rev(i), 0, 0)),
                  pl.BlockSpec((CH, D), lambda i: (rev(i), 1)),
                  pl.BlockSpec((8, CH), lambda i: (0, 0)), pl.BlockSpec((CH, D), lambda i: (0, 0)),
                  pl.BlockSpec((CH, CH), lambda i: (0, 0)), pl.BlockSpec((1, D), lambda i: (0, 0)),
                  pl.BlockSpec(memory_space=pl.ANY)],
        out_specs=[pl.BlockSpec((CH, XBC), lambda i: (rev(i), 0)), pl.BlockSpec((CH, D), lambda i: (rev(i), COL_Z // D)),
                   pl.BlockSpec((CH, CH), lambda i: (rev(i), 0)),
                   pl.BlockSpec((8, CH), lambda i: (0, 0)), pl.BlockSpec((1, D), lambda i: (0, 0))],
        out_shape=[_sds((t, XBC), f32), _sds((t, PROJ), bf16), _sds((t, CH), bf16), _sds((8, CH), f32), _sds((1, D), f32)],
        input_output_aliases={12: 1},
        scratch_shapes=[pltpu.VMEM((NS, D), f32), pltpu.VMEM((CH, D), f32)],
        compiler_params=_cparams(1))(xbcs, xbcs, xbcs, proj, proj, ypre, states, dcat, par, expand, tri, gs, dproj)


def loss_head(y, target, tb, name):
    t = y.shape[0]

    def body(y_ref, t_ref, s_ref, dy_ref):
        err = y_ref[...] - t_ref[...]
        dy_ref[...] = err * (1.0 / D)
        _accum(s_ref, jnp.zeros((8, CH), f32) + jnp.sum(err * err), pl.program_id(0) == 0)

    return pl.pallas_call(
        body, name=name, grid=(t // tb,),
        in_specs=[pl.BlockSpec((tb, D), lambda i: (i, 0)), pl.BlockSpec((tb, D), lambda i: (i, 0))],
        out_specs=[pl.BlockSpec((8, CH), lambda i: (0, 0)), pl.BlockSpec((tb, D), lambda i: (i, 0))],
        out_shape=[_sds((8, CH), f32), _sds((t, D), f32)],
        compiler_params=_cparams(1))(y, target)


def _tiles(t, seq):
    tm = min(512, t)
    return dict(tm=tm, tm_small=min(256, t), tm_large=min(1024, t), tb=min(512, seq))


def local_step(x, target, depth, weights_of, seq, grads_done=None):
    t = x.shape[0]
    ts = _tiles(t, seq)
    tm, tl, tb = ts["tm"], ts["tm_large"], ts["tb"]
    saved, ws = [], []
    for l in range(depth):
        w = weights_of(l, x)
        ws.append(w)
        proj, h1 = norm_matmul(x, w["g1"], w["win"], tl, 1152, f32, "in_proj", token=w.get("token"))
        cat = group_a_fwd(proj, w["wa"], w["ga"], seq, tb, "group_a_fwd")
        xbcs = conv_b_fwd(proj, w["ws"], w["bs"], seq, tb, "conv_b_fwd")
        cat, ypre, states = ssd_fwd(xbcs, proj, w["par"], w["gs"], cat, seq, "ssd_fwd")
        if "late" in w:
            w.update(w.pop("late")(cat))
        mix, x2 = matmul_postnorm(cat, w["wo"], x, w["g2"], tl, False, "out_proj")
        fp, h2 = norm_matmul(x2, w["g3"], w["wu"], tl, 1024, bf16, "mlp_up")
        o, x3 = matmul_postnorm(fp, w["wd"], x2, w["g4"], tm, True, "mlp_down")
        saved.append(dict(x=x, proj=proj, h1=h1, xbcs=xbcs, ypre=ypre, states=states, cat=cat, mix=mix, x2=x2,
                          fp=fp, h2=h2, o=o))
        x = x3
    sse, dx = loss_head(x, target, tm, "loss_head")
    grads = [None] * depth
    for l in reversed(range(depth)):
        s, w = saved[l], ws[l]
        do, dg4, dfp = postnorm_bwd_matmul(s["o"], w["g4"], dx, w["wd"], s["fp"], tl, 1024, bf16, "mlp_down_bwd")
        dwd = matmul_tn(s["fp"], do, 512, 1024, True, "mlp_down_dw")
        dx2, dg3 = matmul_prenorm_bwd(dfp, w["wu"], s["x2"], w["g3"], dx, tm, "mlp_up_bwd")
        dwu = matmul_tn(s["h2"], dfp, 512, 1024, False, "mlp_up_dw", col_blocks=True)
        dmix, dg2, dcat = postnorm_bwd_matmul(s["mix"], w["g2"], dx2, w["wo"], None, tl, 1024, f32, "out_proj_bwd")
        dwo = matmul_tn(s["cat"], dmix, 512, 1024, False, "out_proj_dw")
        dproj, dwa, dga = group_a_bwd(s["proj"], dcat, w["wa"], w["ga"], seq, tb, "group_a_bwd")
        dxbcs, dproj, ddt, dpar, dgs = ssd_bwd(s["xbcs"], s["proj"], s["ypre"], s["states"], dcat, w["par"], w["gs"],
                                               dproj, seq, "ssd_bwd")
        dproj, dws, dbs = conv_b_bwd(s["proj"], dxbcs, w["ws"], w["bs"], dproj, seq, tb, "conv_b_bwd")
        dproj = place_columns(dproj, ddt, COL_DT // CH, tm, "place_ddt")
        dwin = matmul_tn(s["h1"], dproj, 512, 1152, False, "in_proj_dw")
        token = None if grads_done is None else grads_done(l, dict(win=dwin, wo=dwo, wu=dwu, wd=dwd))
        dx, dg1 = matmul_prenorm_bwd(dproj, w["win"], s["x"], w["g1"], dx2, ts["tm_small"], "in_proj_bwd", token=token)
        grads[l] = dict(win=dwin, wo=dwo, wu=dwu, wd=dwd, wa=dwa, ws=dws, bs=dbs, par=dpar,
                        g1=dg1, ga=dga, gs=dgs, g2=dg2, g3=dg3, g4=dg4)
    return sse, dx, grads


GROUPS = {
    "chips": [(1, 0, 0), (0, 1, 0), (1, 1, 0)],
    "pair": [(0, 0, 1)],
    "all": [(1, 0, 0), (0, 1, 0), (1, 1, 0), (0, 0, 1), (1, 0, 1), (0, 1, 1), (1, 1, 1)],
}


def _group_index(group, x, y, c):
    return {"chips": 2 * x + y, "pair": c, "all": 4 * x + 2 * y + c}[group]


def _chunk_indices(shape, pieces):
    if len(shape) < 3:
        return [()]
    lead = [()]
    for n in shape[:-2]:
        lead = [i + (k,) for i in lead for k in range(n)]
    rows = shape[-2]
    split = max(1, pieces // len(lead))
    while split > 1 and (rows % split or (rows // split) % 16):
        split -= 1
    step = rows // split
    return [i + (pl.ds(s * step, step),) for i in lead for s in range(split)]


def _exchange(arrays, out_shapes, group, src_view, dst_view, view_shape, name, own, pieces=16):
    masks = GROUPS[group]
    na, nm = len(arrays), len(masks)
    cuts = [_chunk_indices(view_shape(a), pieces) for a in range(na)]

    def body(*refs):
        ins, outs = refs[:na], refs[na:2 * na]
        send_sems, recv_sems = refs[2 * na:2 * na + 2]
        local_sems = refs[2 * na + 2] if own else None
        x, y, c = lax.axis_index("x"), lax.axis_index("y"), lax.axis_index("c")
        me = _group_index(group, x, y, c)
        peers = []
        for mx, my, mc in masks:
            px, py, pc = (1 - x if mx else x), (1 - y if my else y), (1 - c if mc else c)
            peers.append(((px, py, pc), _group_index(group, px, py, pc)))

        def part(ref, idx):
            return ref.at[idx] if idx else ref

        if own:
            for a in range(na):
                for idx in cuts[a]:
                    pltpu.make_async_copy(part(src_view(ins[a], a, me), idx), part(dst_view(outs[a], a, me), idx),
                                          local_sems.at[a]).start()
        for a in range(na):
            for j, (dev, pidx) in enumerate(peers):
                for idx in cuts[a]:
                    pltpu.make_async_remote_copy(
                        src_ref=part(src_view(ins[a], a, pidx), idx), dst_ref=part(dst_view(outs[a], a, me), idx),
                        send_sem=send_sems.at[a * nm + j], recv_sem=recv_sems.at[a * nm + j],
                        device_id=dev, device_id_type=MESH).start()
        whole = []
        for a in range(na):
            for j, (dev, pidx) in enumerate(peers):
                whole.append(pltpu.make_async_remote_copy(
                    src_ref=src_view(ins[a], a, pidx), dst_ref=dst_view(outs[a], a, pidx),
                    send_sem=send_sems.at[a * nm + j], recv_sem=recv_sems.at[a * nm + j],
                    device_id=dev, device_id_type=MESH))
        for cp in whole:
            cp.wait_recv()
        for cp in whole:
            cp.wait_send()
        if own:
            for a in range(na):
                pltpu.make_async_copy(src_view(ins[a], a, me), dst_view(outs[a], a, me), local_sems.at[a]).wait()

    hbm = pl.BlockSpec(memory_space=pltpu.HBM)
    sems = [pltpu.SemaphoreType.DMA((na * nm,)), pltpu.SemaphoreType.DMA((na * nm,))]
    return pl.pallas_call(
        body, name=name, in_specs=[hbm] * na, out_specs=[hbm] * na,
        out_shape=[_sds(s, a.dtype) for s, a in zip(out_shapes, arrays)],
        scratch_shapes=sems + ([pltpu.SemaphoreType.DMA((na,))] if own else []))(*arrays)


def all_gather(arrays, group, name, slot_axis=0, own=True):
    n = len(GROUPS[group]) + 1
    shapes = [a.shape[:slot_axis] + (n,) + a.shape[slot_axis:] for a in arrays]
    lead = (slice(None),) * slot_axis
    return _exchange(arrays, shapes, group, lambda r, a, i: r, lambda r, a, i: r.at[lead + (i,)],
                     lambda a: arrays[a].shape, name, own)


HBM_SPEC = pl.BlockSpec(memory_space=pltpu.HBM)
SEM_SPEC = pl.BlockSpec(memory_space=pltpu.SEMAPHORE)
DATAFLOW = pltpu.SideEffectType.DATAFLOW_SIDE_EFFECTING
N_CHIPS = 4


def _chip_peers(x, y, c):
    out = []
    for mx, my, _ in GROUPS["chips"]:
        px, py = (1 - x if mx else x), (1 - y if my else y)
        out.append(((px, py, c), 2 * px + py))
    return out


def _weight_views(shards):
    half = [s.shape[0] // 2 for s in shards]
    return dict(src=lambda ref, a, c, to_chip: ref.at[pl.ds(c * half[a], half[a])],
                dst=lambda ref, a, c, from_chip: ref.at[from_chip, pl.ds(c * half[a], half[a])],
                rows=lambda a: half[a])


def _grad_views(sums):
    return dict(src=lambda ref, a, c, to_chip: ref.at[to_chip], dst=lambda ref, a, c, from_chip: ref.at[from_chip],
                rows=lambda a: sums[a].shape[1])


def chips_start(sources, zones, views, name, pieces=4, after=None):
    na, nm = len(sources), N_CHIPS - 1

    def body(*refs):
        ins, lands = refs[:na], refs[na:2 * na]
        n_in = 2 * na + len(_token_arg(after))
        send_sems, recv_sems, token = refs[n_in], refs[n_in + 1], refs[-1]
        x, y, c = lax.axis_index("x"), lax.axis_index("y"), lax.axis_index("c")
        chip = 2 * x + y
        for a in range(na):
            step = views["rows"](a) // pieces
            for j, (dev, to_chip) in enumerate(_chip_peers(x, y, c)):
                for q in range(pieces):
                    rows = pl.ds(q * step, step)
                    pltpu.make_async_remote_copy(
                        src_ref=views["src"](ins[a], a, c, to_chip).at[rows],
                        dst_ref=views["dst"](lands[a], a, c, chip).at[rows],
                        send_sem=send_sems.at[a * nm + j], recv_sem=recv_sems.at[a * nm + j],
                        device_id=dev, device_id_type=MESH).start()
        token[...] = jnp.zeros_like(token)

    both = list(sources) + list(zones)
    outs = pl.pallas_call(
        body, name=name,
        out_shape=(pltpu.SemaphoreType.DMA((na * nm,)), pltpu.SemaphoreType.DMA((na * nm,)),
                   *[pltpu.HBM(b.shape, b.dtype) for b in both], _sds((8, CH), f32)),
        in_specs=[HBM_SPEC] * (2 * na) + _token_spec(after),
        out_specs=(SEM_SPEC, SEM_SPEC, *[HBM_SPEC] * (2 * na), pl.BlockSpec(memory_space=pltpu.VMEM)),
        input_output_aliases={i: 2 + i for i in range(2 * na)},
        compiler_params=pltpu.CompilerParams(has_side_effects=DATAFLOW))(
            *[pltpu.with_memory_space_constraint(b, pltpu.HBM) for b in both], *_token_arg(after))
    return dict(send=outs[0], recv=outs[1], sources=list(outs[2:2 + na]), zones=list(outs[2 + na:2 + 2 * na]),
                token=outs[-1], views=views)


def chips_wait(started, after, name):
    sources, zones, views = started["sources"], started["zones"], started["views"]
    na, nm = len(sources), N_CHIPS - 1

    def body(*refs):
        ins, lands = refs[:na], refs[na:2 * na]
        send_sems, recv_sems = refs[2 * na], refs[2 * na + 1]
        x, y, c = lax.axis_index("x"), lax.axis_index("y"), lax.axis_index("c")
        for a in range(na):
            for j, (dev, peer_chip) in enumerate(_chip_peers(x, y, c)):
                cp = pltpu.make_async_remote_copy(
                    src_ref=views["src"](ins[a], a, c, peer_chip), dst_ref=views["dst"](lands[a], a, c, peer_chip),
                    send_sem=send_sems.at[a * nm + j], recv_sem=recv_sems.at[a * nm + j],
                    device_id=dev, device_id_type=MESH)
                cp.wait_send()
                cp.wait_recv()

    both = list(sources) + list(zones)
    outs = pl.pallas_call(
        body, name=name, out_shape=tuple(pltpu.HBM(b.shape, b.dtype) for b in both),
        in_specs=[HBM_SPEC] * (2 * na) + [SEM_SPEC, SEM_SPEC, pl.BlockSpec(memory_space=pl.ANY)],
        out_specs=tuple([HBM_SPEC] * (2 * na)), input_output_aliases={i: i for i in range(2 * na)},
        compiler_params=pltpu.CompilerParams(has_side_effects=DATAFLOW))(*both, started["send"], started["recv"], after)
    return list(outs[:na]), list(outs[na:])


def weights_share(zones, name):
    na, nm = len(zones), N_CHIPS - 1

    def body(*refs):
        lands = refs[na:2 * na]
        send_sems, recv_sems = refs[2 * na:]
        x, y, c = lax.axis_index("x"), lax.axis_index("y"), lax.axis_index("c")
        chip = 2 * x + y
        sibling = (x, y, 1 - c)
        sends = []
        for a in range(na):
            half = zones[a].shape[1] // 2
            for m in range(1, N_CHIPS):
                mine = lands[a].at[chip ^ m, pl.ds(c * half, half)]
                sends.append(pltpu.make_async_remote_copy(
                    src_ref=mine, dst_ref=mine, send_sem=send_sems.at[a * nm + m - 1],
                    recv_sem=recv_sems.at[a * nm + m - 1], device_id=sibling, device_id_type=MESH))
        for cp in sends:
            cp.start()
        for a in range(na):
            half = zones[a].shape[1] // 2
            for m in range(1, N_CHIPS):
                theirs = lands[a].at[chip ^ m, pl.ds((1 - c) * half, half)]
                pltpu.make_async_remote_copy(
                    src_ref=theirs, dst_ref=theirs, send_sem=send_sems.at[a * nm + m - 1],
                    recv_sem=recv_sems.at[a * nm + m - 1], device_id=sibling, device_id_type=MESH).wait_recv()
        for cp in sends:
            cp.wait_send()

    return pl.pallas_call(
        body, name=name, in_specs=[HBM_SPEC] * na, out_specs=[HBM_SPEC] * na,
        out_shape=[_sds(z.shape, z.dtype) for z in zones], input_output_aliases={i: i for i in range(na)},
        scratch_shapes=[pltpu.SemaphoreType.DMA((na * nm,)), pltpu.SemaphoreType.DMA((na * nm,))])(*zones)


def pair_send_halves(grads, name):
    half = [g.shape[1] // 2 for g in grads]
    shapes = [(g.shape[0], h, g.shape[2]) for g, h in zip(grads, half)]
    return _exchange(grads, shapes, "pair", lambda r, a, i: r.at[:, pl.ds(i * half[a], half[a])],
                     lambda r, a, i: r, lambda a: shapes[a], name, False)


def sum_pair_half(g, recv, core, name, tb=256):
    nk, r, c = g.shape
    tb = min(tb, r // 2)
    nb = r // 2 // tb

    def body(core_ref, g_ref, r_ref, o_ref):
        o_ref[...] = (g_ref[...] + r_ref[...]).astype(bf16)

    return pl.pallas_call(
        body, name=name,
        grid_spec=pltpu.PrefetchScalarGridSpec(
            num_scalar_prefetch=1, grid=(nk, nb),
            in_specs=[pl.BlockSpec((None, tb, c), lambda k, i, core_ref: (k, core_ref[0] * nb + i, 0)),
                      pl.BlockSpec((None, tb, c), lambda k, i, core_ref: (k, i, 0))],
            out_specs=pl.BlockSpec((None, tb, c), lambda k, i, core_ref: (k, i, 0))),
        out_shape=_sds((nk, r // 2, c), bf16), compiler_params=_cparams(2))(
            jnp.reshape(core, (1,)).astype(jnp.int32), g, recv)


def chip_sum_into(acc, layer, own, others, chip, name, tb=256):
    n, r, c = own.shape
    tb = min(tb, r)

    def body(chip_ref, x_ref, y1_ref, y2_ref, y3_ref, acc_ref, o_ref):
        o_ref[...] = ((x_ref[...].astype(f32) + y1_ref[...].astype(f32)) + y2_ref[...].astype(f32)) + y3_ref[...].astype(f32)

    def slot(k):
        return pl.BlockSpec((None, tb, c), lambda i, chip_ref: (chip_ref[0] ^ k, i, 0))

    return pl.pallas_call(
        body, name=name,
        grid_spec=pltpu.PrefetchScalarGridSpec(
            num_scalar_prefetch=1, grid=(r // tb,),
            in_specs=[slot(k) for k in range(n)] + [pl.BlockSpec(memory_space=pl.ANY)],
            out_specs=pl.BlockSpec((None, tb, c), lambda i, chip_ref: (layer, i, 0))),
        out_shape=_sds(acc.shape, f32), input_output_aliases={n + 1: 0}, compiler_params=_cparams(1))(
            jnp.reshape(chip, (1,)).astype(jnp.int32), own, *([others] * (n - 1)), acc)


def adamw_halves(w, g_own, g_recv, m, v, core, name, tb=256):
    depth, r, c = w.shape
    tb = min(tb, r // 2)
    nb = r // 2 // tb

    def body(core_ref, w_ref, go_ref, gr_ref, m_ref, v_ref, g_ref, d_ref, mo_ref, vo_ref):
        gv = jnp.where(pl.program_id(1) == core_ref[0], go_ref[...], gr_ref[...])
        m2 = B1 * m_ref[...] + (1.0 - B1) * gv
        v2 = B2 * v_ref[...] + (1.0 - B2) * (gv * gv)
        m_hat = m2 / (1.0 - B1 ** STEP)
        v_hat = v2 / (1.0 - B2 ** STEP)
        g_ref[...] = gv
        d_ref[...] = -LR * (m_hat / (jnp.sqrt(v_hat) + AEPS) + WD * w_ref[...])
        mo_ref[...] = m2
        vo_ref[...] = v2

    whole = pl.BlockSpec((None, tb, c), lambda l, h, i, core_ref: (l, h * nb + i, 0))
    part = pl.BlockSpec((None, tb, c), lambda l, h, i, core_ref: (l, i, 0))
    return pl.pallas_call(
        body, name=name,
        grid_spec=pltpu.PrefetchScalarGridSpec(num_scalar_prefetch=1, grid=(depth, 2, nb),
                                               in_specs=[whole, part, part, whole, whole], out_specs=[whole] * 4),
        out_shape=[_sds(w.shape, f32)] * 4, compiler_params=_cparams(3))(
            jnp.reshape(core, (1,)).astype(jnp.int32), w, g_own, g_recv, m, v)


def sum_slots(y, out_dtype, name, tb=256):
    n, r, c = y.shape
    tb = min(tb, r)

    def body(y_ref, o_ref):
        acc = y_ref[0].astype(f32)
        for i in range(1, n):
            acc = acc + y_ref[i].astype(f32)
        o_ref[...] = acc.astype(out_dtype)

    return pl.pallas_call(
        body, name=name, grid=(r // tb,),
        in_specs=[pl.BlockSpec((n, tb, c), lambda i: (0, i, 0))], out_specs=pl.BlockSpec((tb, c), lambda i: (i, 0)),
        out_shape=_sds((r, c), out_dtype), compiler_params=_cparams(1))(y)


def adamw(w, g, m, v, name, tb=256):
    r, c = w.shape
    tb = min(tb, r)

    def body(w_ref, g_ref, m_ref, v_ref, d_ref, mo_ref, vo_ref):
        gv = g_ref[...]
        m2 = B1 * m_ref[...] + (1.0 - B1) * gv
        v2 = B2 * v_ref[...] + (1.0 - B2) * (gv * gv)
        m_hat = m2 / (1.0 - B1 ** STEP)
        v_hat = v2 / (1.0 - B2 ** STEP)
        d_ref[...] = -LR * (m_hat / (jnp.sqrt(v_hat) + AEPS) + WD * w_ref[...])
        mo_ref[...] = m2
        vo_ref[...] = v2

    spec = pl.BlockSpec((tb, c), lambda i: (i, 0))
    return pl.pallas_call(
        body, name=name, grid=(r // tb,), in_specs=[spec] * 4, out_specs=[spec] * 3,
        out_shape=[_sds((r, c), f32)] * 3, compiler_params=_cparams(1))(w, g, m, v)


def adamw_leading(w, g, m, v, name, tc=64):
    c, l, r = w.shape
    main = c // tc
    tail = c - main * tc

    def body(w_ref, g_ref, m_ref, v_ref, *rest):
        d_ref, mo_ref, vo_ref = rest[-3:]
        gv = g_ref[...]
        m2 = B1 * m_ref[...] + (1.0 - B1) * gv
        v2 = B2 * v_ref[...] + (1.0 - B2) * (gv * gv)
        m_hat = m2 / (1.0 - B1 ** STEP)
        v_hat = v2 / (1.0 - B2 ** STEP)
        d_ref[...] = -LR * (m_hat / (jnp.sqrt(v_hat) + AEPS) + WD * w_ref[...])
        mo_ref[...] = m2
        vo_ref[...] = v2

    spec = pl.BlockSpec((tc, l, r), lambda i: (i, 0, 0))
    outs = pl.pallas_call(
        functools.partial(body), name=name, grid=(main,), in_specs=[spec] * 4, out_specs=[spec] * 3,
        out_shape=[_sds(w.shape, f32)] * 3, compiler_params=_cparams(1))(w, g, m, v)
    if tail:
        assert (main * tc) % tail == 0
        last = pl.BlockSpec((tail, l, r), lambda i: (main * tc // tail, 0, 0))
        outs = pl.pallas_call(
            functools.partial(body), name=name + "_tail", grid=(1,),
            in_specs=[last] * 4 + [pl.BlockSpec(memory_space=pl.ANY)] * 3, out_specs=[last] * 3,
            out_shape=[_sds(w.shape, f32)] * 3, input_output_aliases={4: 0, 5: 1, 6: 2},
            compiler_params=_cparams(1))(w, g, m, v, *outs)
    return outs


SMALL_ROW = 1024
SMALL_GAINS = ("g1", "ga", "gs", "g2", "g3", "g4")
SMALL_LAYER_ROWS = 8 + 8 + 16 + 8


def _pack_small(grads):
    wide = lambda a: jnp.pad(a, ((0, 0), (0, 2 * SMALL_ROW - a.shape[1]))).reshape(-1, SMALL_ROW)
    row = lax.broadcasted_iota(jnp.int32, (8, SMALL_ROW), 0)
    parts = []
    for g in grads:
        singles = [g[k] for k in SMALL_GAINS] + [g["bs"][:, :SMALL_ROW],
                                                 jnp.pad(g["bs"][:, SMALL_ROW:], ((0, 0), (0, 2 * SMALL_ROW - XBC)))]
        first = sum(jnp.where(row == k, s, 0.0) for k, s in enumerate(singles))
        parts += [first, g["wa"], wide(g["ws"]), jnp.pad(g["par"], ((0, 0), (0, SMALL_ROW - CH)))]
    return jnp.concatenate(parts, axis=0)


def _unpack_small(packed, depth):
    rows = packed.reshape(depth, SMALL_LAYER_ROWS, SMALL_ROW)
    out = {k: rows[:, i] for i, k in enumerate(SMALL_GAINS)}
    out["bs"] = rows[:, 6:8].reshape(depth, 2 * SMALL_ROW)[:, :XBC]
    out["wa"] = rows[:, 8:11]
    out["ws"] = rows[:, 16:32].reshape(depth, 8, 2 * SMALL_ROW)[:, :4, :XBC]
    out["par"] = rows[:, 32:35, :CH]
    return out


def kernel(x, norm_mix_pre, w_in, conv_a_w, ssm_conv_w, ssm_conv_b, dt_bias, a_log, d_skip, conv_out_norm, ssm_out_norm, w_out, norm_mix_post, norm_mlp_pre, w_up, w_down, norm_mlp_post, loss_target, m_norm_mix_pre, m_w_in, m_conv_a_w, m_ssm_conv_w, m_ssm_conv_b, m_dt_bias, m_a_log, m_d_skip, m_conv_out_norm, m_ssm_out_norm, m_w_out, m_norm_mix_post, m_norm_mlp_pre, m_w_up, m_w_down, m_norm_mlp_post, v_norm_mix_pre, v_w_in, v_conv_a_w, v_ssm_conv_w, v_ssm_conv_b, v_dt_bias, v_a_log, v_d_skip, v_conv_out_norm, v_ssm_out_norm, v_w_out, v_norm_mix_post, v_norm_mlp_pre, v_w_up, v_w_down, v_norm_mlp_post):
    nb, seq, _ = x.shape
    t = nb * seq
    depth = w_in.shape[0]
    ncol = w_in.shape[2]
    chip = 2 * lax.axis_index("x") + lax.axis_index("y")

    taps = [conv_a_w, ssm_conv_w]
    taps_g = all_gather(taps, "chips", "gather_taps", slot_axis=1, own=False)
    wa_g, ws_g = [lax.dynamic_update_index_in_dim(g, s, chip, 1) for g, s in zip(taps_g, taps)]
    wa_full = jnp.transpose(wa_g, (0, 2, 1, 3)).reshape(depth, 3, D)
    ws_full = jnp.transpose(ws_g, (0, 2, 1, 3)).reshape(depth, 4, XBC)
    lane_pad = lambda a: jnp.pad(a, ((0, 0), (0, CH - a.shape[1])))
    par = jnp.stack([lane_pad(dt_bias), lane_pad(a_log), lane_pad(d_skip)], axis=1)
    par = jnp.pad(par, ((0, 0), (0, 5), (0, 0)))

    layer_shards = lambda l: [w_in[l].astype(bf16), w_out[l].astype(bf16), w_up[l].astype(bf16), w_down[l].astype(bf16)]
    issued = []

    def start(shards, name):
        zones = [lax.empty((N_CHIPS,) + s.shape, s.dtype) for s in shards]
        issued.append(chips_start(shards, zones, _weight_views(shards), name,
                                  after=issued[-1]["token"] if issued else None))
        return issued[-1]

    def finish(started, after, name):
        shards, zones = chips_wait(started, after, name)
        zones = weights_share(zones, "weights_share")
        return [lax.dynamic_update_index_in_dim(z, s, chip, 0) for z, s in zip(zones, shards)]

    def shaped(mats):
        wo_z, wu_z, wd_z = mats
        return wo_z.reshape(2 * D, D), wu_z, wd_z.reshape(DFF, D)

    first = layer_shards(0)
    travelling = {0: start(first[:1], "weights_start_0")}
    rest = start(first[1:], "weights_start_0_rest")
    for l in range(1, depth):
        travelling[l] = start(layer_shards(l), f"weights_start_{l}")

    def weights_of(l, x_in):
        mats = finish(travelling.pop(l), x_in, f"weights_wait_{l}")
        win_full = jnp.pad(jnp.transpose(mats[0], (1, 0, 2)).reshape(D, N_CHIPS * ncol),
                           ((0, 0), (0, PROJ - N_CHIPS * ncol)))
        w = dict(win=win_full, wa=jnp.pad(wa_full[l], ((0, 5), (0, 0))), ws=jnp.pad(ws_full[l], ((0, 4), (0, 0))),
                 bs=ssm_conv_b[l][None], par=par[l], g1=norm_mix_pre[l][None], ga=conv_out_norm[l][None],
                 gs=ssm_out_norm[l][None], g2=norm_mix_post[l][None], g3=norm_mlp_pre[l][None],
                 g4=norm_mlp_post[l][None])
        if l == 0:
            w["late"] = lambda after: dict(zip(("wo", "wu", "wd"), shaped(finish(rest, after, "weights_wait_0_rest"))))
        else:
            w.update(zip(("wo", "wu", "wd"), shaped(mats[1:])))
        return w

    core = lax.axis_index("c")
    grads_travelling = {}

    def grads_done(l, g):
        mats = [g["win"][None], g["wo"].reshape(N_CHIPS, 2 * D // N_CHIPS, D), g["wu"],
                g["wd"].reshape(N_CHIPS, DFF // N_CHIPS, D)]
        received = pair_send_halves(mats, "grads_to_pair")
        sums = [sum_pair_half(m_, r_, core, "pair_sum") for m_, r_ in zip(mats, received)]
        sums[0] = jnp.transpose(sums[0][0, :, :N_CHIPS * ncol].reshape(D // 2, N_CHIPS, ncol), (1, 0, 2))
        zones = [lax.empty(s.shape, s.dtype) for s in sums]
        grads_travelling[l] = chips_start(sums, zones, _grad_views(sums), f"grads_start_{l}")
        return grads_travelling[l]["token"]

    sse, dx, grads = local_step(x.reshape(t, D), loss_target.reshape(t, D), depth, weights_of, seq, grads_done)
    loss = lax.psum(0.5 / D * sse[0, 0], ("x", "y", "c"))

    small_all = all_gather([_pack_small(grads)], "all", "gather_small")[0]
    small_sum = sum_slots(small_all, f32, "small_sum", tb=8)
    small = _unpack_small(small_sum, depth)

    big_w = [w_in, w_out, w_up, w_down]
    acc = [lax.empty((depth, bw.shape[1] // 2, bw.shape[2]), f32) for bw in big_w]
    for l in reversed(range(depth)):
        sums, zones = chips_wait(grads_travelling.pop(l), small_sum, f"grads_wait_{l}")
        acc = [chip_sum_into(acc_a, l, s, z, chip, "chip_sum") for acc_a, s, z in zip(acc, sums, zones)]
    from_sibling = _exchange(acc, [a.shape for a in acc], "pair", lambda r, a, i: r, lambda r, a, i: r,
                             lambda a: acc[a].shape, "grads_from_pair", False)

    wa_cols, ws_cols = conv_a_w.shape[2], ssm_conv_w.shape[2]
    par_g = small["par"].reshape(depth, 3, CH)
    g_small = dict(
        norm_mix_pre=small["g1"], conv_out_norm=small["ga"], ssm_out_norm=small["gs"], norm_mix_post=small["g2"],
        norm_mlp_pre=small["g3"], norm_mlp_post=small["g4"], ssm_conv_b=small["bs"],
        conv_a_w=lax.dynamic_slice_in_dim(small["wa"].reshape(depth, 3, D), chip * wa_cols, wa_cols, axis=2),
        ssm_conv_w=lax.dynamic_slice_in_dim(small["ws"].reshape(depth, 4, XBC), chip * ws_cols, ws_cols, axis=2),
        dt_bias=par_g[:, 0, :NH], a_log=par_g[:, 1, :NH], d_skip=par_g[:, 2, :NH])

    given = dict(norm_mix_pre=(norm_mix_pre, m_norm_mix_pre, v_norm_mix_pre), w_in=(w_in, m_w_in, v_w_in),
                 conv_a_w=(conv_a_w, m_conv_a_w, v_conv_a_w), ssm_conv_w=(ssm_conv_w, m_ssm_conv_w, v_ssm_conv_w),
                 ssm_conv_b=(ssm_conv_b, m_ssm_conv_b, v_ssm_conv_b), dt_bias=(dt_bias, m_dt_bias, v_dt_bias),
                 a_log=(a_log, m_a_log, v_a_log), d_skip=(d_skip, m_d_skip, v_d_skip),
                 conv_out_norm=(conv_out_norm, m_conv_out_norm, v_conv_out_norm),
                 ssm_out_norm=(ssm_out_norm, m_ssm_out_norm, v_ssm_out_norm), w_out=(w_out, m_w_out, v_w_out),
                 norm_mix_post=(norm_mix_post, m_norm_mix_post, v_norm_mix_post),
                 norm_mlp_pre=(norm_mlp_pre, m_norm_mlp_pre, v_norm_mlp_pre), w_up=(w_up, m_w_up, v_w_up),
                 w_down=(w_down, m_w_down, v_w_down), norm_mlp_post=(norm_mlp_post, m_norm_mlp_post, v_norm_mlp_post))
    halves = dict(zip(["w_in", "w_out", "w_up", "w_down"], zip(acc, from_sibling)))
    order = ["norm_mix_pre", "w_in", "conv_a_w", "ssm_conv_w", "ssm_conv_b", "dt_bias", "a_log", "d_skip",
             "conv_out_norm", "ssm_out_norm", "w_out", "norm_mix_post", "norm_mlp_pre", "w_up", "w_down",
             "norm_mlp_post"]
    g_out, d_out, m_out, v_out = [], [], [], []
    for n in order:
        wv, mv, vv = given[n]
        if n in halves and wv.shape[-1] % CH:
            own, recv = halves[n]
            gv = jnp.concatenate([jnp.where(core == 0, own, recv), jnp.where(core == 0, recv, own)], axis=1)
            to_cols, to_rows = (lambda a: jnp.transpose(a, (2, 0, 1))), (lambda a: jnp.transpose(a, (1, 2, 0)))
            dlt, m2, v2 = [to_rows(o) for o in adamw_leading(to_cols(wv), to_cols(gv), to_cols(mv), to_cols(vv),
                                                             "adamw_cols")]
        elif n in halves:
            gv, dlt, m2, v2 = adamw_halves(wv, *halves[n], mv, vv, core, "adamw_matrix")
        else:
            gv = g_small[n].reshape(wv.shape)
            two_d = lambda a: a.reshape(-1, a.shape[-1])
            dlt, m2, v2 = adamw(two_d(wv), two_d(gv), two_d(mv), two_d(vv), "adamw")
        g_out.append(gv)
        d_out.append(dlt.reshape(wv.shape))
        m_out.append(m2.reshape(wv.shape))
        v_out.append(v2.reshape(wv.shape))
    return (loss, dx.reshape(nb, seq, D), *g_out, *d_out, *m_out, *v_out)
```

```python
import functools

import jax
import jax.numpy as jnp
from jax import lax
from jax.experimental import pallas as pl
from jax.experimental.pallas import tpu as pltpu

f32, bf16 = jnp.float32, jnp.bfloat16

D = 1024
NH, HP = 16, 64
NG, NS = 2, 128
CH = 128
XBC = D + 2 * NG * NS
DFF = 4 * D
IN_COLS = 3 * D + D + XBC + NH
PROJ = 5760
COL_Z, COL_XBC, COL_DT = 3 * D, 4 * D, 4 * D + XBC
EPS = 1e-6
HALO = 8
VMEM_LIMIT = 56 * 2**20
MESH = pl.DeviceIdType.MESH

LR, B1, B2, AEPS, WD, STEP = 0.001, 0.9, 0.999, 1e-08, 0.01, 10


def _cparams(n_axes):
    return pltpu.CompilerParams(dimension_semantics=("arbitrary",) * n_axes, vmem_limit_bytes=VMEM_LIMIT)


def _sds(shape, dtype):
    return jax.ShapeDtypeStruct(tuple(shape), dtype)


def _token_spec(token):
    return [] if token is None else [pl.BlockSpec(memory_space=pl.ANY)]


def _token_arg(token):
    return [] if token is None else [token]


def _rms_fwd(x, g):
    r = lax.rsqrt(jnp.mean(x * x, axis=-1, keepdims=True) + EPS)
    return x * r * g


def _rms_bwd(x, g, dy):
    r = lax.rsqrt(jnp.mean(x * x, axis=-1, keepdims=True) + EPS)
    xh = x * r
    gdy = dy * g
    dx = r * (gdy - xh * jnp.mean(xh * gdy, axis=-1, keepdims=True))
    return dx, dy * xh


def _accum(ref, part, first):
    @pl.when(first)
    def _():
        ref[...] = part

    @pl.when(jnp.logical_not(first))
    def _():
        ref[...] += part


def _dot_nt(a, b):
    return lax.dot_general(a, b, (((1,), (1,)), ((), ())), preferred_element_type=f32)


def _dot_tn(a, b):
    return lax.dot_general(a, b, (((0,), (0,)), ((), ())), preferred_element_type=f32)


def _dot(a, b):
    return jnp.dot(a, b, preferred_element_type=f32)


def _split_dot(x, e_bf, n_split, nt=False):
    acc = None
    rem = x
    for s in range(n_split):
        hi = rem.astype(bf16)
        term = _dot_nt(hi, e_bf) if nt else _dot(hi, e_bf)
        acc = term if acc is None else acc + term
        if s + 1 < n_split:
            rem = rem - hi.astype(f32)
    return acc


def _sigmoid(x):
    return 0.5 * jnp.tanh(0.5 * x) + 0.5


def norm_matmul(x, g, w, tm, tn, out_dtype, name, token=None):
    t = x.shape[0]
    if w.ndim == 3:
        assert w.shape[2] == tn
        n = w.shape[0] * tn
        w_spec = pl.BlockSpec((None, D, tn), lambda i, j: (j, 0, 0))
    else:
        n = w.shape[1]
        w_spec = pl.BlockSpec((D, tn), lambda i, j: (0, j))

    def body(x_ref, g_ref, w_ref, *rest):
        o_ref, h_ref = rest[-2:]

        @pl.when(pl.program_id(1) == 0)
        def _():
            h_ref[...] = _rms_fwd(x_ref[...], g_ref[...]).astype(bf16)

        o_ref[...] = _dot(h_ref[...], w_ref[...]).astype(out_dtype)

    return pl.pallas_call(
        body, name=name, grid=(t // tm, n // tn),
        in_specs=[pl.BlockSpec((tm, D), lambda i, j: (i, 0)), pl.BlockSpec((1, D), lambda i, j: (0, 0)), w_spec]
        + _token_spec(token),
        out_specs=[pl.BlockSpec((tm, tn), lambda i, j: (i, j)), pl.BlockSpec((tm, D), lambda i, j: (i, 0))],
        out_shape=[_sds((t, n), out_dtype), _sds((t, D), bf16)],
        compiler_params=_cparams(2))(x, g, w, *_token_arg(token))


def matmul_postnorm(a, w, xres, g, tm, relu2, name):
    t, k = a.shape

    def body(a_ref, w_ref, xr_ref, g_ref, y_ref, xo_ref):
        av = a_ref[...]
        if relu2:
            af = jnp.maximum(av.astype(f32), 0.0)
            av = (af * af).astype(bf16)
        y = _dot(av, w_ref[...])
        y_ref[...] = y
        xo_ref[...] = xr_ref[...] + _rms_fwd(y, g_ref[...])

    return pl.pallas_call(
        body, name=name, grid=(t // tm,),
        in_specs=[pl.BlockSpec((tm, k), lambda i: (i, 0)), pl.BlockSpec((k, D), lambda i: (0, 0)),
                  pl.BlockSpec((tm, D), lambda i: (i, 0)), pl.BlockSpec((1, D), lambda i: (0, 0))],
        out_specs=[pl.BlockSpec((tm, D), lambda i: (i, 0)), pl.BlockSpec((tm, D), lambda i: (i, 0))],
        out_shape=[_sds((t, D), f32), _sds((t, D), f32)],
        compiler_params=_cparams(1))(a, w, xres, g)


def postnorm_bwd_matmul(y, g, dxo, w, fp, tm, tn, out_dtype, name, token=None):
    t, n = y.shape[0], w.shape[0]
    relu = fp is not None

    def body(*refs):
        y_ref, g_ref, dxo_ref, w_ref = refs[:4]
        fp_ref = refs[4] if relu else None
        dy_ref, dg_ref, da_ref = refs[-3:]
        i, j = pl.program_id(0), pl.program_id(1)

        @pl.when(j == 0)
        def _():
            dx, dgc = _rms_bwd(y_ref[...], g_ref[...], dxo_ref[...])
            dy_ref[...] = dx.astype(bf16)
            _accum(dg_ref, jnp.sum(dgc, axis=0, keepdims=True), i == 0)

        da = _dot_nt(dy_ref[...], w_ref[...])
        if relu:
            da = da * (2.0 * jnp.maximum(fp_ref[...].astype(f32), 0.0))
        da_ref[...] = da.astype(out_dtype)

    in_specs = [pl.BlockSpec((tm, D), lambda i, j: (i, 0)), pl.BlockSpec((1, D), lambda i, j: (0, 0)),
                pl.BlockSpec((tm, D), lambda i, j: (i, 0)), pl.BlockSpec((tn, D), lambda i, j: (j, 0))]
    args = [y, g, dxo, w]
    if relu:
        in_specs.append(pl.BlockSpec((tm, tn), lambda i, j: (i, j)))
        args.append(fp)
    in_specs += _token_spec(token)
    args += _token_arg(token)
    return pl.pallas_call(
        body, name=name, grid=(t // tm, n // tn), in_specs=in_specs,
        out_specs=[pl.BlockSpec((tm, D), lambda i, j: (i, 0)), pl.BlockSpec((1, D), lambda i, j: (0, 0)),
                   pl.BlockSpec((tm, tn), lambda i, j: (i, j))],
        out_shape=[_sds((t, D), bf16), _sds((1, D), f32), _sds((t, n), out_dtype)],
        compiler_params=_cparams(2))(*args)


def matmul_prenorm_bwd(da, w, x, g, dxo, tm, name, token=None):
    t, k = da.shape
    blocked = w.ndim == 3

    def body(da_ref, w_ref, x_ref, g_ref, dxo_ref, *rest):
        dx_ref, dg_ref = rest[-2:]
        if blocked:
            kc = w.shape[2]
            dh = _dot_nt(da_ref[:, 0:kc], w_ref[0])
            for q in range(1, w.shape[0]):
                dh = dh + _dot_nt(da_ref[:, q * kc:(q + 1) * kc], w_ref[q])
        else:
            dh = _dot_nt(da_ref[...], w_ref[...])
        dxn, dgc = _rms_bwd(x_ref[...], g_ref[...], dh)
        dx_ref[...] = dxo_ref[...] + dxn
        _accum(dg_ref, jnp.sum(dgc, axis=0, keepdims=True), pl.program_id(0) == 0)

    w_spec = pl.BlockSpec(w.shape, (lambda i: (0, 0, 0)) if blocked else (lambda i: (0, 0)))
    return pl.pallas_call(
        body, name=name, grid=(t // tm,),
        in_specs=[pl.BlockSpec((tm, k), lambda i: (i, 0)), w_spec,
                  pl.BlockSpec((tm, D), lambda i: (i, 0)), pl.BlockSpec((1, D), lambda i: (0, 0)),
                  pl.BlockSpec((tm, D), lambda i: (i, 0))] + _token_spec(token),
        out_specs=[pl.BlockSpec((tm, D), lambda i: (i, 0)), pl.BlockSpec((1, D), lambda i: (0, 0))],
        out_shape=[_sds((t, D), f32), _sds((1, D), f32)],
        compiler_params=_cparams(1))(da, w, x, g, dxo, *_token_arg(token))


def matmul_tn(a, b, tm, tn, relu2, name, col_blocks=False):
    t, m = a.shape
    n = b.shape[1]
    if col_blocks:
        out_spec, out_shape = pl.BlockSpec((None, tm, tn), lambda i, j: (j, i, 0)), _sds((n // tn, m, tn), f32)
    else:
        out_spec, out_shape = pl.BlockSpec((tm, tn), lambda i, j: (i, j)), _sds((m, n), f32)

    def body(a_ref, b_ref, o_ref, at_ref):
        @pl.when(pl.program_id(1) == 0)
        def _():
            av = a_ref[...]
            if relu2:
                af = jnp.maximum(av.astype(f32), 0.0)
                av = (af * af).astype(bf16)
            at_ref[...] = av.T

        o_ref[...] = _dot(at_ref[...], b_ref[...])

    return pl.pallas_call(
        body, name=name, grid=(m // tm, n // tn),
        in_specs=[pl.BlockSpec((t, tm), lambda i, j: (0, i)), pl.BlockSpec((t, tn), lambda i, j: (0, j))],
        out_specs=out_spec, out_shape=out_shape,
        scratch_shapes=[pltpu.VMEM((tm, t), bf16)],
        compiler_params=_cparams(2))(a, b)


ROWS_A = 16
ROWS_B = 32
UNROLL = 4


def _past(win, s):
    return (win if s == 0 else pltpu.roll(win, s, 0))[HALO:]


def _future(win, s):
    n = win.shape[0]
    return (win if s == 0 else pltpu.roll(win, n - s, 0))[:n - HALO]


def _fold8(v):
    return v.reshape(v.shape[0] // 8, 8, v.shape[1]).sum(axis=0)


def _halo_prev(tb, col):
    return lambda i: (jnp.maximum(i * (tb // HALO) - 1, 0), col)


def _halo_next(tb, col, t):
    return lambda i: (jnp.minimum((i + 1) * (tb // HALO), t // HALO - 1), col)


def group_a_fwd(proj, wa, g, seq, tb, name):
    t = proj.shape[0]
    bps = seq // tb

    def body(xa_ref, ca_ref, ba_ref, xah_ref, cah_ref, wa_ref, g_ref, o_ref, u_scr):
        first = (pl.program_id(0) % bps) == 0
        u_scr[0:HALO, :] = jnp.where(first, 0.0, cah_ref[...] * xah_ref[...])
        w, gv = wa_ref[...], g_ref[...]

        def chunk(i, carry):
            r = pl.multiple_of(i * ROWS_A, ROWS_A)
            rows = pl.ds(r, ROWS_A)
            u_scr[pl.ds(pl.multiple_of(HALO + r, HALO), ROWS_A), :] = ca_ref[rows, :] * xa_ref[rows, :]
            win = u_scr[pl.ds(r, ROWS_A + HALO), :]
            cv = w[2:3] * _past(win, 0) + w[1:2] * _past(win, 1) + w[0:1] * _past(win, 2)
            o_ref[rows, :] = _rms_fwd(ba_ref[rows, :] * cv, gv).astype(bf16)
            return carry

        lax.fori_loop(0, tb // ROWS_A, chunk, 0, unroll=UNROLL)

    blk = lambda c: pl.BlockSpec((tb, D), lambda i: (i, c))
    return pl.pallas_call(
        body, name=name, grid=(t // tb,),
        in_specs=[blk(0), blk(1), blk(2),
                  pl.BlockSpec((HALO, D), _halo_prev(tb, 0)), pl.BlockSpec((HALO, D), _halo_prev(tb, 1)),
                  pl.BlockSpec((8, D), lambda i: (0, 0)), pl.BlockSpec((1, D), lambda i: (0, 0))],
        out_specs=pl.BlockSpec((tb, D), lambda i: (i, 0)),
        out_shape=_sds((t, 2 * D), bf16),
        scratch_shapes=[pltpu.VMEM((tb + HALO, D), f32)],
        compiler_params=_cparams(1))(proj, proj, proj, proj, proj, wa, g)


def group_a_bwd(proj, dcat, wa, g, seq, tb, name, token=None):
    t = proj.shape[0]
    bps = seq // tb

    def body(xa_ref, ca_ref, ba_ref, dy_ref, xap_ref, cap_ref, xan_ref, can_ref, ban_ref, dyn_ref, wa_ref, g_ref,
             *rest):
        dp_ref, dwa_ref, dg_ref, u_scr, d_scr, acc_scr = rest[-6:]
        i = pl.program_id(0)
        first = (i % bps) == 0
        last = (i % bps) == bps - 1
        w = wa_ref[...]
        gv = g_ref[...]
        u_scr[0:HALO, :] = jnp.where(first, 0.0, cap_ref[...] * xap_ref[...])
        u_scr[HALO + tb:2 * HALO + tb, :] = can_ref[...] * xan_ref[...]
        acc_scr[...] = jnp.zeros_like(acc_scr)

        def forward_part(n, carry):
            r = pl.multiple_of(n * ROWS_A, ROWS_A)
            rows = pl.ds(r, ROWS_A)
            ba = ba_ref[rows, :]
            u_scr[pl.ds(pl.multiple_of(HALO + r, HALO), ROWS_A), :] = ca_ref[rows, :] * xa_ref[rows, :]
            win = u_scr[pl.ds(r, ROWS_A + HALO), :]
            u = [_past(win, s) for s in range(3)]
            cv = w[2:3] * u[0] + w[1:2] * u[1] + w[0:1] * u[2]
            dya, dgc = _rms_bwd(ba * cv, gv, dy_ref[rows, :])
            dcv = dya * ba
            d_scr[rows, :] = dcv
            dp_ref[rows, 2 * D:3 * D] = (dya * cv).astype(bf16)
            acc_scr[0:8, :] += _fold8(dgc)
            for k in range(3):
                acc_scr[8 + 8 * k:16 + 8 * k, :] += _fold8(dcv * u[2 - k])
            return carry

        lax.fori_loop(0, tb // ROWS_A, forward_part, 0, unroll=UNROLL)

        start = HALO + tb
        cvn = (w[2:3] * u_scr[pl.ds(start, HALO), :] + w[1:2] * u_scr[pl.ds(start - 1, HALO), :]
               + w[0:1] * u_scr[pl.ds(start - 2, HALO), :])
        ban = ban_ref[...]
        dyan, _ = _rms_bwd(ban * cvn, gv, dyn_ref[...])
        d_scr[tb:tb + HALO, :] = jnp.where(last, 0.0, dyan * ban)

        def backward_part(n, carry):
            r = pl.multiple_of(n * ROWS_A, ROWS_A)
            rows = pl.ds(r, ROWS_A)
            win = d_scr[pl.ds(r, ROWS_A + HALO), :]
            du = w[2:3] * _future(win, 0) + w[1:2] * _future(win, 1) + w[0:1] * _future(win, 2)
            dp_ref[rows, 0:D] = (du * ca_ref[rows, :]).astype(bf16)
            dp_ref[rows, D:2 * D] = (du * xa_ref[rows, :]).astype(bf16)
            return carry

        lax.fori_loop(0, tb // ROWS_A, backward_part, 0, unroll=UNROLL)

        row = lax.broadcasted_iota(jnp.int32, (8, D), 0)
        dw = jnp.zeros((8, D), f32)
        for k in range(3):
            dw = jnp.where(row == k, jnp.sum(acc_scr[8 + 8 * k:16 + 8 * k, :], axis=0, keepdims=True), dw)
        _accum(dwa_ref, dw, i == 0)
        _accum(dg_ref, jnp.sum(acc_scr[0:8, :], axis=0, keepdims=True), i == 0)

    blk = lambda c: pl.BlockSpec((tb, D), lambda i: (i, c))
    prv = lambda c: pl.BlockSpec((HALO, D), _halo_prev(tb, c))
    nxt = lambda c: pl.BlockSpec((HALO, D), _halo_next(tb, c, t))
    return pl.pallas_call(
        body, name=name, grid=(t // tb,),
        in_specs=[blk(0), blk(1), blk(2), blk(0), prv(0), prv(1), nxt(0), nxt(1), nxt(2), nxt(0),
                  pl.BlockSpec((8, D), lambda i: (0, 0)), pl.BlockSpec((1, D), lambda i: (0, 0))] + _token_spec(token),
        out_specs=[pl.BlockSpec((tb, 3 * D), lambda i: (i, 0)), pl.BlockSpec((8, D), lambda i: (0, 0)),
                   pl.BlockSpec((1, D), lambda i: (0, 0))],
        out_shape=[_sds((t, PROJ), bf16), _sds((8, D), f32), _sds((1, D), f32)],
        scratch_shapes=[pltpu.VMEM((tb + 2 * HALO, D), f32), pltpu.VMEM((tb + HALO, D), f32), pltpu.VMEM((32, D), f32)],
        compiler_params=_cparams(1))(proj, proj, proj, dcat, proj, proj, proj, proj, proj, dcat, wa, g,
                                     *_token_arg(token))


CB = 512
XBC_BLK0 = COL_XBC // CB


def conv_b_fwd(proj, ws, bs, seq, tb, name):
    t = proj.shape[0]
    bps = seq // tb

    def body(x_ref, xp_ref, w_ref, b_ref, o_ref, x_scr):
        first = (pl.program_id(1) % bps) == 0
        x_scr[0:HALO, :] = jnp.where(first, 0.0, xp_ref[...])
        w, bias = w_ref[...], b_ref[...]

        def chunk(n, carry):
            r = pl.multiple_of(n * ROWS_B, ROWS_B)
            rows = pl.ds(r, ROWS_B)
            x_scr[pl.ds(pl.multiple_of(HALO + r, HALO), ROWS_B), :] = x_ref[rows, :]
            win = x_scr[pl.ds(r, ROWS_B + HALO), :]
            xc = bias + w[3:4] * _past(win, 0)
            for k in range(3):
                xc = xc + w[k:k + 1] * _past(win, 3 - k)
            o_ref[rows, :] = xc * _sigmoid(xc)
            return carry

        lax.fori_loop(0, tb // ROWS_B, chunk, 0, unroll=UNROLL)

    return pl.pallas_call(
        body, name=name, grid=(XBC // CB, t // tb),
        in_specs=[pl.BlockSpec((tb, CB), lambda j, i: (i, XBC_BLK0 + j)),
                  pl.BlockSpec((HALO, CB), lambda j, i: (jnp.maximum(i * (tb // HALO) - 1, 0), XBC_BLK0 + j)),
                  pl.BlockSpec((8, CB), lambda j, i: (0, j)), pl.BlockSpec((1, CB), lambda j, i: (0, j))],
        out_specs=pl.BlockSpec((tb, CB), lambda j, i: (i, j)),
        out_shape=_sds((t, XBC), f32),
        scratch_shapes=[pltpu.VMEM((tb + HALO, CB), f32)],
        compiler_params=_cparams(2))(proj, proj, ws, bs)


def conv_b_bwd(proj, dxs, ws, bs, dproj, seq, tb, name):
    t = proj.shape[0]
    bps = seq // tb

    def body(x_ref, xp_ref, xn_ref, d_ref, dn_ref, w_ref, b_ref, dproj_ref, dx_ref, dw_ref, db_ref, x_scr, d_scr,
             acc_scr):
        i = pl.program_id(1)
        first = (i % bps) == 0
        last = (i % bps) == bps - 1
        w = w_ref[...]
        bias = b_ref[...]
        x_scr[0:HALO, :] = jnp.where(first, 0.0, xp_ref[...])
        x_scr[HALO + tb:2 * HALO + tb, :] = xn_ref[...]
        acc_scr[...] = jnp.zeros_like(acc_scr)

        def dsilu(xc, d):
            sg = _sigmoid(xc)
            return d * (sg * (1.0 + xc * (1.0 - sg)))

        def forward_part(n, carry):
            r = pl.multiple_of(n * ROWS_B, ROWS_B)
            rows = pl.ds(r, ROWS_B)
            x_scr[pl.ds(pl.multiple_of(HALO + r, HALO), ROWS_B), :] = x_ref[rows, :]
            win = x_scr[pl.ds(r, ROWS_B + HALO), :]
            xs = [_past(win, s) for s in range(4)]
            xc = bias + w[3:4] * xs[0]
            for k in range(3):
                xc = xc + w[k:k + 1] * xs[3 - k]
            dxc = dsilu(xc, d_ref[rows, :])
            d_scr[rows, :] = dxc
            acc_scr[0:8, :] += _fold8(dxc)
            for k in range(4):
                acc_scr[8 + 8 * k:16 + 8 * k, :] += _fold8(dxc * xs[3 - k])
            return carry

        lax.fori_loop(0, tb // ROWS_B, forward_part, 0, unroll=UNROLL)

        start = HALO + tb
        xcn = bias + w[3:4] * x_scr[pl.ds(start, HALO), :]
        for k in range(3):
            xcn = xcn + w[k:k + 1] * x_scr[pl.ds(start - 3 + k, HALO), :]
        d_scr[tb:tb + HALO, :] = jnp.where(last, 0.0, dsilu(xcn, dn_ref[...]))

        def backward_part(n, carry):
            r = pl.multiple_of(n * ROWS_B, ROWS_B)
            win = d_scr[pl.ds(r, ROWS_B + HALO), :]
            dx = w[3:4] * _future(win, 0)
            for k in range(3):
                dx = dx + w[k:k + 1] * _future(win, 3 - k)
            dx_ref[pl.ds(r, ROWS_B), :] = dx.astype(bf16)
            return carry

        lax.fori_loop(0, tb // ROWS_B, backward_part, 0, unroll=UNROLL)

        row = lax.broadcasted_iota(jnp.int32, (8, CB), 0)
        dw = jnp.zeros((8, CB), f32)
        for k in range(4):
            dw = jnp.where(row == k, jnp.sum(acc_scr[8 + 8 * k:16 + 8 * k, :], axis=0, keepdims=True), dw)
        _accum(dw_ref, dw, i == 0)
        _accum(db_ref, jnp.sum(acc_scr[0:8, :], axis=0, keepdims=True), i == 0)

    nh = t // HALO
    return pl.pallas_call(
        body, name=name, grid=(XBC // CB, t // tb),
        in_specs=[pl.BlockSpec((tb, CB), lambda j, i: (i, XBC_BLK0 + j)),
                  pl.BlockSpec((HALO, CB), lambda j, i: (jnp.maximum(i * (tb // HALO) - 1, 0), XBC_BLK0 + j)),
                  pl.BlockSpec((HALO, CB), lambda j, i: (jnp.minimum((i + 1) * (tb // HALO), nh - 1), XBC_BLK0 + j)),
                  pl.BlockSpec((tb, CB), lambda j, i: (i, j)),
                  pl.BlockSpec((HALO, CB), lambda j, i: (jnp.minimum((i + 1) * (tb // HALO), nh - 1), j)),
                  pl.BlockSpec((8, CB), lambda j, i: (0, j)), pl.BlockSpec((1, CB), lambda j, i: (0, j)),
                  pl.BlockSpec(memory_space=pl.ANY)],
        out_specs=[pl.BlockSpec((tb, CB), lambda j, i: (i, XBC_BLK0 + j)), pl.BlockSpec((8, CB), lambda j, i: (0, j)),
                   pl.BlockSpec((1, CB), lambda j, i: (0, j))],
        out_shape=[_sds((t, PROJ), bf16), _sds((8, XBC), f32), _sds((1, XBC), f32)],
        input_output_aliases={7: 0},
        scratch_shapes=[pltpu.VMEM((tb + 2 * HALO, CB), f32), pltpu.VMEM((tb + HALO, CB), f32),
                        pltpu.VMEM((40, CB), f32)],
        compiler_params=_cparams(2))(proj, proj, proj, dxs, dxs, ws, bs, dproj)


def place_columns(buf, part, col_block, tb, name):
    t, wdt = part.shape

    def body(p_ref, buf_ref, o_ref):
        o_ref[...] = p_ref[...]

    return pl.pallas_call(
        body, name=name, grid=(t // tb,),
        in_specs=[pl.BlockSpec((tb, wdt), lambda i: (i, 0)), pl.BlockSpec(memory_space=pl.ANY)],
        out_specs=pl.BlockSpec((tb, wdt), lambda i: (i, col_block)), out_shape=_sds(buf.shape, buf.dtype),
        input_output_aliases={1: 0}, compiler_params=_cparams(1))(part, buf)


GW = D // NG


def _ssd_consts():
    head_of_lane = jnp.arange(D) // HP
    expand = (jnp.arange(CH)[:, None] == head_of_lane[None, :]).astype(bf16)
    tri = (jnp.arange(CH)[:, None] >= jnp.arange(CH)[None, :]).astype(f32)
    return expand, tri


def _ssd_common(par_ref, dtr_ref, e_ref, tri_ref):
    par = par_ref[...]
    dtb, alog, dsk = par[0:1], par[1:2], par[2:3]
    lane = lax.broadcasted_iota(jnp.int32, (CH, CH), 1)
    a = -jnp.exp(alog)
    dtr = dtr_ref[...] + dtb
    sp = jnp.maximum(dtr, 0.0) + jnp.log(1.0 + jnp.exp(-jnp.abs(dtr)))
    dt = jnp.where(lane < NH, sp, 0.0)
    cs = jnp.dot(tri_ref[...], dt * a, precision=lax.Precision.HIGHEST, preferred_element_type=f32)
    cs_last = cs[CH - 1:CH, :]
    dte = jnp.exp(cs_last - cs)
    ecs = jnp.exp(cs)
    ecl = jnp.exp(cs_last)
    e = e_ref[...]
    row8 = lax.broadcasted_iota(jnp.int32, (8, CH), 0)
    r8 = _split_dot(jnp.where(row8 == 0, ecl, jnp.where(row8 == 1, dsk, 0.0)), e, 3)
    return dict(a=a, dtr=dtr, dt=dt, cs=cs, cst=cs.T, dte=dte, ecs=ecs, ecl=ecl, e=e, lane=lane,
                dt_x=_split_dot(dt, e, 3), dte_x=_split_dot(dte, e, 3), ecs_x=_split_dot(ecs, e, 3),
                ecl_x=r8[0:1], dsk_x=r8[1:2])


def _decay_matrix(c, h):
    li = lax.broadcasted_iota(jnp.int32, (CH, CH), 0)
    seg = c["cs"][:, h:h + 1] - c["cst"][h:h + 1, :]
    return jnp.exp(jnp.where(li >= c["lane"], seg, -jnp.inf))


def _gate_norm_fwd(y, z, gs):
    zg = z * _sigmoid(z)
    yg = y * zg
    return jnp.concatenate([_rms_fwd(yg[:, k * GW:(k + 1) * GW], gs[:, k * GW:(k + 1) * GW]) for k in range(NG)], axis=1)


def ssd_fwd(xbcs, proj, par, gs, cat, seq, name):
    t = xbcs.shape[0]
    nc = seq // CH
    expand, tri = _ssd_consts()

    def body(xs_ref, b_ref, c_ref, dtr_ref, z_ref, par_ref, e_ref, tri_ref, gs_ref, cat_ref, yn_ref, y_ref, st_ref,
             p_scr, yd_scr):
        @pl.when(pl.program_id(0) % nc == 0)
        def _():
            p_scr[...] = jnp.zeros_like(p_scr)

        c = _ssd_common(par_ref, dtr_ref, e_ref, tri_ref)
        xs = xs_ref[...]
        xdt = xs * c["dt_x"]
        xdt_b = xdt.astype(bf16)
        xdte_b = (xdt * c["dte_x"]).astype(bf16)
        p = p_scr[...]
        st_ref[0] = p
        p_b = p.astype(bf16)
        lo = c["lane"] < HP
        for g in range(NG):
            bg = b_ref[:, g * NS:(g + 1) * NS].astype(bf16)
            cg = c_ref[:, g * NS:(g + 1) * NS].astype(bf16)
            gmat = _dot_nt(cg, bg)
            for q in range(GW // CH):
                col = g * GW + q * CH
                xp = xdt_b[:, col:col + CH]
                h0 = col // HP
                m0 = (gmat * _decay_matrix(c, h0)).astype(bf16)
                m1 = (gmat * _decay_matrix(c, h0 + 1)).astype(bf16)
                yd_scr[:, col:col + CH] = (_dot(m0, jnp.where(lo, xp, jnp.zeros_like(xp)))
                                           + _dot(m1, jnp.where(lo, jnp.zeros_like(xp), xp)))
            gsl = slice(g * GW, (g + 1) * GW)
            yoff = _dot(cg, p_b[:, gsl]) * c["ecs_x"][:, gsl]
            yd_scr[:, gsl] = yd_scr[:, gsl] + yoff
            p_scr[:, gsl] = p[:, gsl] * c["ecl_x"][:, gsl] + _dot_tn(bg, xdte_b[:, gsl])
        y = yd_scr[...] + c["dsk_x"] * xs
        y_ref[...] = y
        yn_ref[...] = _gate_norm_fwd(y, z_ref[...], gs_ref[...]).astype(bf16)

    nb = t // CH
    return pl.pallas_call(
        body, name=name, grid=(nb,),
        in_specs=[pl.BlockSpec((CH, D), lambda i: (i, 0)),
                  pl.BlockSpec((CH, NG * NS), lambda i: (i, D // (NG * NS))),
                  pl.BlockSpec((CH, NG * NS), lambda i: (i, D // (NG * NS) + 1)),
                  pl.BlockSpec((CH, CH), lambda i: (i, COL_DT // CH)),
                  pl.BlockSpec((CH, D), lambda i: (i, COL_Z // D)),
                  pl.BlockSpec((8, CH), lambda i: (0, 0)), pl.BlockSpec((CH, D), lambda i: (0, 0)),
                  pl.BlockSpec((CH, CH), lambda i: (0, 0)), pl.BlockSpec((1, D), lambda i: (0, 0)),
                  pl.BlockSpec(memory_space=pl.ANY)],
        out_specs=[pl.BlockSpec((CH, D), lambda i: (i, 1)), pl.BlockSpec((CH, D), lambda i: (i, 0)),
                   pl.BlockSpec((1, NS, D), lambda i: (i, 0, 0))],
        out_shape=[_sds((t, 2 * D), bf16), _sds((t, D), f32), _sds((nb, NS, D), f32)],
        input_output_aliases={9: 0},
        scratch_shapes=[pltpu.VMEM((NS, D), f32), pltpu.VMEM((CH, D), f32)],
        compiler_params=_cparams(1))(xbcs, xbcs, xbcs, proj, proj, par, expand, tri, gs, cat)


def ssd_bwd(xbcs, proj, ypre, states, dcat, par, gs, dproj, seq, name):
    t = xbcs.shape[0]
    nc = seq // CH
    expand, tri = _ssd_consts()

    def body(xs_ref, b_ref, c_ref, dtr_ref, z_ref, y_ref, st_ref, dyn_ref, par_ref, e_ref, tri_ref, gs_ref, dproj_ref,
             dx_ref, dz_ref, ddt_ref, dpar_ref, dgs_ref, dp_scr, dxdt_scr):
        i = pl.program_id(0)

        @pl.when(i % nc == 0)
        def _():
            dp_scr[...] = jnp.zeros_like(dp_scr)

        c = _ssd_common(par_ref, dtr_ref, e_ref, tri_ref)
        e = c["e"]
        lane = c["lane"]
        sub = lax.broadcasted_iota(jnp.int32, (CH, CH), 0)
        xs = xs_ref[...]
        xdt = xs * c["dt_x"]
        xdt_b = xdt.astype(bf16)
        xdte_b = (xdt * c["dte_x"]).astype(bf16)
        p = st_ref[0]
        p_b = p.astype(bf16)
        dpn = dp_scr[...]
        dpn_b = dpn.astype(bf16)

        y, z, gs_v = y_ref[...], z_ref[...], gs_ref[...]
        zs = _sigmoid(z)
        zg = z * zs
        yg = y * zg
        parts, gparts = [], []
        for k in range(NG):
            sl = slice(k * GW, (k + 1) * GW)
            dxk, dgk = _rms_bwd(yg[:, sl], gs_v[:, sl], dyn_ref[:, sl])
            parts.append(dxk)
            gparts.append(dgk)
        dyg = jnp.concatenate(parts, axis=1)
        dgs_rows = jnp.concatenate(gparts, axis=1)
        dy = dyg * zg
        dz_ref[...] = (dyg * y * (zs * (1.0 + z * (1.0 - zs)))).astype(bf16)
        dy_b = dy.astype(bf16)
        dq_b = (dy * c["ecs_x"]).astype(bf16)

        lo = lane < HP
        dcs = jnp.zeros((CH, CH), f32)
        dcst = jnp.zeros((CH, CH), f32)
        for g in range(NG):
            gsl = slice(g * GW, (g + 1) * GW)
            bg = b_ref[:, g * NS:(g + 1) * NS].astype(bf16)
            cg = c_ref[:, g * NS:(g + 1) * NS].astype(bf16)
            gmat = _dot_nt(cg, bg)
            dgm = jnp.zeros((CH, CH), f32)
            for q in range(GW // CH):
                col = g * GW + q * CH
                xp = xdt_b[:, col:col + CH]
                dyp = dy_b[:, col:col + CH]
                acc = None
                for hh in range(2):
                    h = col // HP + hh
                    keep = lo if hh == 0 else jnp.logical_not(lo)
                    dyh = jnp.where(keep, dyp, jnp.zeros_like(dyp))
                    dec = _decay_matrix(c, h)
                    m = gmat * dec
                    dm = _dot_nt(dyh, xp)
                    dseg = dm * m
                    dcs = dcs + jnp.where(lane == h, jnp.sum(dseg, axis=1, keepdims=True), 0.0)
                    dcst = dcst + jnp.where(sub == h, jnp.sum(dseg, axis=0, keepdims=True), 0.0)
                    dgm = dgm + dm * dec
                    term = _dot_tn(m.astype(bf16), dyh)
                    acc = term if acc is None else acc + term
                dxdt_scr[:, col:col + CH] = acc
            dgm_b = dgm.astype(bf16)
            bds = _dot(bg, dpn_b[:, gsl])
            dxdt_scr[:, gsl] = dxdt_scr[:, gsl] + c["dte_x"][:, gsl] * bds
            dc_g = _dot(dgm_b, bg) + _dot_nt(dq_b[:, gsl], p_b[:, gsl])
            db_g = _dot_tn(dgm_b, cg) + _dot_nt(xdte_b[:, gsl], dpn_b[:, gsl])
            dx_ref[:, D + g * NS:D + (g + 1) * NS] = db_g
            dx_ref[:, D + NG * NS + g * NS:D + NG * NS + (g + 1) * NS] = dc_g
            dp_scr[:, gsl] = dpn[:, gsl] * c["ecl_x"][:, gsl] + _dot_tn(cg, dq_b[:, gsl])
            q_g = _dot(cg, p_b[:, gsl])
            e_g = e[:, gsl]
            dcs = dcs + c["ecs"] * _split_dot(dy[:, gsl] * q_g, e_g, 2, nt=True)
            ddte = _split_dot(xdt[:, gsl] * bds, e_g, 2, nt=True) * c["dte"]
            dcs = dcs - ddte
            dcs = dcs + jnp.where(sub == CH - 1, jnp.sum(ddte, axis=0, keepdims=True), 0.0)

        decl = _split_dot(jnp.broadcast_to(jnp.sum(dpn * p, axis=0, keepdims=True), (8, D)), e, 2, nt=True)[0:1]
        dcs = dcs + jnp.where(sub == CH - 1, c["ecl"] * decl, 0.0)
        dcs = dcs - dcst.T
        dadt = lax.dot_general(tri_ref[...], dcs, (((0,), (0,)), ((), ())), precision=lax.Precision.HIGHEST,
                               preferred_element_type=f32)
        dxdt = dxdt_scr[...]
        ddt = dadt * c["a"] + _split_dot(dxdt * xs, e, 2, nt=True)
        ddtr = jnp.where(lane < NH, ddt * _sigmoid(c["dtr"]), 0.0)
        ddt_ref[...] = ddtr.astype(bf16)
        dx_ref[:, 0:D] = dxdt * c["dt_x"] + c["dsk_x"] * dy
        dsk = _split_dot(jnp.broadcast_to(jnp.sum(dy * xs, axis=0, keepdims=True), (8, D)), e, 2, nt=True)[0:1]
        dalog = jnp.sum(dadt * c["dt"], axis=0, keepdims=True) * c["a"]
        row8 = lax.broadcasted_iota(jnp.int32, (8, CH), 0)
        dpar = jnp.where(row8 == 0, jnp.sum(ddtr, axis=0, keepdims=True),
                         jnp.where(row8 == 1, dalog, jnp.where(row8 == 2, dsk, 0.0)))
        dpar = jnp.where(lax.broadcasted_iota(jnp.int32, (8, CH), 1) < NH, dpar, 0.0)
        _accum(dpar_ref, dpar, i == 0)
        _accum(dgs_ref, jnp.sum(dgs_rows, axis=0, keepdims=True), i == 0)

    nb = t // CH
    rev = lambda i: (i // nc) * nc + (nc - 1 - i % nc)
    return pl.pallas_call(
        body, name=name, grid=(nb,),
        in_specs=[pl.BlockSpec((CH, D), lambda i: (rev(i), 0)),
                  pl.BlockSpec((CH, NG * NS), lambda i: (rev(i), D // (NG * NS))),
                  pl.BlockSpec((CH, NG * NS), lambda i: (rev(i), D // (NG * NS) + 1)),
                  pl.BlockSpec((CH, CH), lambda i: (rev(i), COL_DT // CH)),
                  pl.BlockSpec((CH, D), lambda i: (rev(i), COL_Z // D)),
                  pl.BlockSpec((CH, D), lambda i: (rev(i), 0)),
                  pl.BlockSpec((1, NS, D), lambda i: (rev(i), 0, 0)),
                  pl.BlockSpec((CH, D), lambda i: (rev(i), 1)),
                  pl.BlockSpec((8, CH), lambda i: (0, 0)), pl.BlockSpec((CH, D), lambda i: (0, 0)),
                  pl.BlockSpec((CH, CH), lambda i: (0, 0)), pl.BlockSpec((1, D), lambda i: (0, 0)),
                  pl.BlockSpec(memory_space=pl.ANY)],
        out_specs=[pl.BlockSpec((CH, XBC), lambda i: (rev(i), 0)), pl.BlockSpec((CH, D), lambda i: (rev(i), COL_Z // D)),
                   pl.BlockSpec((CH, CH), lambda i: (rev(i), 0)),
                   pl.BlockSpec((8, CH), lambda i: (0, 0)), pl.BlockSpec((1, D), lambda i: (0, 0))],
        out_shape=[_sds((t, XBC), f32), _sds((t, PROJ), bf16), _sds((t, CH), bf16), _sds((8, CH), f32), _sds((1, D), f32)],
        input_output_aliases={12: 1},
        scratch_shapes=[pltpu.VMEM((NS, D), f32), pltpu.VMEM((CH, D), f32)],
        compiler_params=_cparams(1))(xbcs, xbcs, xbcs, proj, proj, ypre, states, dcat, par, expand, tri, gs, dproj)


def loss_head(y, target, tb, name):
    t = y.shape[0]

    def body(y_ref, t_ref, s_ref, dy_ref):
        err = y_ref[...] - t_ref[...]
        dy_ref[...] = err * (1.0 / D)
        _accum(s_ref, jnp.zeros((8, CH), f32) + jnp.sum(err * err), pl.program_id(0) == 0)

    return pl.pallas_call(
        body, name=name, grid=(t // tb,),
        in_specs=[pl.BlockSpec((tb, D), lambda i: (i, 0)), pl.BlockSpec((tb, D), lambda i: (i, 0))],
        out_specs=[pl.BlockSpec((8, CH), lambda i: (0, 0)), pl.BlockSpec((tb, D), lambda i: (i, 0))],
        out_shape=[_sds((8, CH), f32), _sds((t, D), f32)],
        compiler_params=_cparams(1))(y, target)


def _tiles(t, seq):
    tm = min(512, t)
    return dict(tm=tm, tm_small=min(256, t), tm_large=min(1024, t), tb=min(512, seq))


def local_step(x, target, depth, weights_of, seq, grads_done=None):
    t = x.shape[0]
    ts = _tiles(t, seq)
    tm, tl, tb = ts["tm"], ts["tm_large"], ts["tb"]
    saved, ws = [], []
    for l in range(depth):
        w = weights_of(l, x)
        ws.append(w)
        proj, h1 = norm_matmul(x, w["g1"], w["win"], tl, 1152, f32, "in_proj", token=w.get("token"))
        cat = group_a_fwd(proj, w["wa"], w["ga"], seq, tb, "group_a_fwd")
        xbcs = conv_b_fwd(proj, w["ws"], w["bs"], seq, tb, "conv_b_fwd")
        cat, ypre, states = ssd_fwd(xbcs, proj, w["par"], w["gs"], cat, seq, "ssd_fwd")
        if "late" in w:
            w.update(w.pop("late")(cat))
        mix, x2 = matmul_postnorm(cat, w["wo"], x, w["g2"], tl, False, "out_proj")
        fp, h2 = norm_matmul(x2, w["g3"], w["wu"], tl, 1024, bf16, "mlp_up")
        o, x3 = matmul_postnorm(fp, w["wd"], x2, w["g4"], tm, True, "mlp_down")
        saved.append(dict(x=x, proj=proj, h1=h1, xbcs=xbcs, ypre=ypre, states=states, cat=cat, mix=mix, x2=x2,
                          fp=fp, h2=h2, o=o))
        x = x3
    sse, dx = loss_head(x, target, tm, "loss_head")
    grads = [None] * depth
    for l in reversed(range(depth)):
        s, w = saved[l], ws[l]
        do, dg4, dfp = postnorm_bwd_matmul(s["o"], w["g4"], dx, w["wd"], s["fp"], tl, 1024, bf16, "mlp_down_bwd")
        dwd = matmul_tn(s["fp"], do, 512, 1024, True, "mlp_down_dw")
        dx2, dg3 = matmul_prenorm_bwd(dfp, w["wu"], s["x2"], w["g3"], dx, tm, "mlp_up_bwd")
        dwu = matmul_tn(s["h2"], dfp, 512, 1024, False, "mlp_up_dw", col_blocks=True)
        dmix, dg2, dcat = postnorm_bwd_matmul(s["mix"], w["g2"], dx2, w["wo"], None, tl, 1024, f32, "out_proj_bwd")
        dwo = matmul_tn(s["cat"], dmix, 512, 1024, False, "out_proj_dw")
        token = None if grads_done is None else grads_done(l, dict(wo=dwo, wu=dwu, wd=dwd), False)
        dproj, dwa, dga = group_a_bwd(s["proj"], dcat, w["wa"], w["ga"], seq, tb, "group_a_bwd", token=token)
        dxbcs, dproj, ddt, dpar, dgs = ssd_bwd(s["xbcs"], s["proj"], s["ypre"], s["states"], dcat, w["par"], w["gs"],
                                               dproj, seq, "ssd_bwd")
        dproj, dws, dbs = conv_b_bwd(s["proj"], dxbcs, w["ws"], w["bs"], dproj, seq, tb, "conv_b_bwd")
        dproj = place_columns(dproj, ddt, COL_DT // CH, tm, "place_ddt")
        dwin = matmul_tn(s["h1"], dproj, 512, 1152, False, "in_proj_dw")
        token = None if grads_done is None else grads_done(l, dict(win=dwin), True)
        dx, dg1 = matmul_prenorm_bwd(dproj, w["win"], s["x"], w["g1"], dx2, ts["tm_small"], "in_proj_bwd", token=token)
        grads[l] = dict(win=dwin, wo=dwo, wu=dwu, wd=dwd, wa=dwa, ws=dws, bs=dbs, par=dpar,
                        g1=dg1, ga=dga, gs=dgs, g2=dg2, g3=dg3, g4=dg4)
    return sse, dx, grads


GROUPS = {
    "chips": [(1, 0, 0), (0, 1, 0), (1, 1, 0)],
    "pair": [(0, 0, 1)],
    "all": [(1, 0, 0), (0, 1, 0), (1, 1, 0), (0, 0, 1), (1, 0, 1), (0, 1, 1), (1, 1, 1)],
}


def _group_index(group, x, y, c):
    return {"chips": 2 * x + y, "pair": c, "all": 4 * x + 2 * y + c}[group]


def _chunk_indices(shape, pieces):
    if len(shape) < 3:
        return [()]
    lead = [()]
    for n in shape[:-2]:
        lead = [i + (k,) for i in lead for k in range(n)]
    rows = shape[-2]
    split = max(1, pieces // len(lead))
    while split > 1 and (rows % split or (rows // split) % 16):
        split -= 1
    step = rows // split
    return [i + (pl.ds(s * step, step),) for i in lead for s in range(split)]


def _exchange(arrays, out_shapes, group, src_view, dst_view, view_shape, name, own, pieces=16):
    masks = GROUPS[group]
    na, nm = len(arrays), len(masks)
    cuts = [_chunk_indices(view_shape(a), pieces) for a in range(na)]

    def body(*refs):
        ins, outs = refs[:na], refs[na:2 * na]
        send_sems, recv_sems = refs[2 * na:2 * na + 2]
        local_sems = refs[2 * na + 2] if own else None
        x, y, c = lax.axis_index("x"), lax.axis_index("y"), lax.axis_index("c")
        me = _group_index(group, x, y, c)
        peers = []
        for mx, my, mc in masks:
            px, py, pc = (1 - x if mx else x), (1 - y if my else y), (1 - c if mc else c)
            peers.append(((px, py, pc), _group_index(group, px, py, pc)))

        def part(ref, idx):
            return ref.at[idx] if idx else ref

        if own:
            for a in range(na):
                for idx in cuts[a]:
                    pltpu.make_async_copy(part(src_view(ins[a], a, me), idx), part(dst_view(outs[a], a, me), idx),
                                          local_sems.at[a]).start()
        for a in range(na):
            for j, (dev, pidx) in enumerate(peers):
                for idx in cuts[a]:
                    pltpu.make_async_remote_copy(
                        src_ref=part(src_view(ins[a], a, pidx), idx), dst_ref=part(dst_view(outs[a], a, me), idx),
                        send_sem=send_sems.at[a * nm + j], recv_sem=recv_sems.at[a * nm + j],
                        device_id=dev, device_id_type=MESH).start()
        whole = []
        for a in range(na):
            for j, (dev, pidx) in enumerate(peers):
                whole.append(pltpu.make_async_remote_copy(
                    src_ref=src_view(ins[a], a, pidx), dst_ref=dst_view(outs[a], a, pidx),
                    send_sem=send_sems.at[a * nm + j], recv_sem=recv_sems.at[a * nm + j],
                    device_id=dev, device_id_type=MESH))
        for cp in whole:
            cp.wait_recv()
        for cp in whole:
            cp.wait_send()
        if own:
            for a in range(na):
                pltpu.make_async_copy(src_view(ins[a], a, me), dst_view(outs[a], a, me), local_sems.at[a]).wait()

    hbm = pl.BlockSpec(memory_space=pltpu.HBM)
    sems = [pltpu.SemaphoreType.DMA((na * nm,)), pltpu.SemaphoreType.DMA((na * nm,))]
    return pl.pallas_call(
        body, name=name, in_specs=[hbm] * na, out_specs=[hbm] * na,
        out_shape=[_sds(s, a.dtype) for s, a in zip(out_shapes, arrays)],
        scratch_shapes=sems + ([pltpu.SemaphoreType.DMA((na,))] if own else []))(*arrays)


def all_gather(arrays, group, name, slot_axis=0, own=True):
    n = len(GROUPS[group]) + 1
    shapes = [a.shape[:slot_axis] + (n,) + a.shape[slot_axis:] for a in arrays]
    lead = (slice(None),) * slot_axis
    return _exchange(arrays, shapes, group, lambda r, a, i: r, lambda r, a, i: r.at[lead + (i,)],
                     lambda a: arrays[a].shape, name, own)


HBM_SPEC = pl.BlockSpec(memory_space=pltpu.HBM)
SEM_SPEC = pl.BlockSpec(memory_space=pltpu.SEMAPHORE)
DATAFLOW = pltpu.SideEffectType.DATAFLOW_SIDE_EFFECTING
N_CHIPS = 4


def _chip_peers(x, y, c):
    out = []
    for mx, my, _ in GROUPS["chips"]:
        px, py = (1 - x if mx else x), (1 - y if my else y)
        out.append(((px, py, c), 2 * px + py))
    return out


def _weight_views(shards):
    half = [s.shape[0] // 2 for s in shards]
    return dict(src=lambda ref, a, c, to_chip: ref.at[pl.ds(c * half[a], half[a])],
                dst=lambda ref, a, c, from_chip: ref.at[from_chip, pl.ds(c * half[a], half[a])],
                rows=lambda a: half[a])


def _grad_views(sums):
    return dict(src=lambda ref, a, c, to_chip: ref.at[to_chip], dst=lambda ref, a, c, from_chip: ref.at[from_chip],
                rows=lambda a: sums[a].shape[1])


def chips_start(sources, zones, views, name, pieces=4, after=None):
    na, nm = len(sources), N_CHIPS - 1

    def body(*refs):
        ins, lands = refs[:na], refs[na:2 * na]
        n_in = 2 * na + len(_token_arg(after))
        send_sems, recv_sems, token = refs[n_in], refs[n_in + 1], refs[-1]
        x, y, c = lax.axis_index("x"), lax.axis_index("y"), lax.axis_index("c")
        chip = 2 * x + y
        for a in range(na):
            step = views["rows"](a) // pieces
            for j, (dev, to_chip) in enumerate(_chip_peers(x, y, c)):
                for q in range(pieces):
                    rows = pl.ds(q * step, step)
                    pltpu.make_async_remote_copy(
                        src_ref=views["src"](ins[a], a, c, to_chip).at[rows],
                        dst_ref=views["dst"](lands[a], a, c, chip).at[rows],
                        send_sem=send_sems.at[a * nm + j], recv_sem=recv_sems.at[a * nm + j],
                        device_id=dev, device_id_type=MESH).start()
        token[...] = jnp.zeros_like(token)

    both = list(sources) + list(zones)
    outs = pl.pallas_call(
        body, name=name,
        out_shape=(pltpu.SemaphoreType.DMA((na * nm,)), pltpu.SemaphoreType.DMA((na * nm,)),
                   *[pltpu.HBM(b.shape, b.dtype) for b in both], _sds((8, CH), f32)),
        in_specs=[HBM_SPEC] * (2 * na) + _token_spec(after),
        out_specs=(SEM_SPEC, SEM_SPEC, *[HBM_SPEC] * (2 * na), pl.BlockSpec(memory_space=pltpu.VMEM)),
        input_output_aliases={i: 2 + i for i in range(2 * na)},
        compiler_params=pltpu.CompilerParams(has_side_effects=DATAFLOW))(
            *[pltpu.with_memory_space_constraint(b, pltpu.HBM) for b in both], *_token_arg(after))
    return dict(send=outs[0], recv=outs[1], sources=list(outs[2:2 + na]), zones=list(outs[2 + na:2 + 2 * na]),
                token=outs[-1], views=views)


def chips_wait(started, after, name):
    sources, zones, views = started["sources"], started["zones"], started["views"]
    na, nm = len(sources), N_CHIPS - 1

    def body(*refs):
        ins, lands = refs[:na], refs[na:2 * na]
        send_sems, recv_sems = refs[2 * na], refs[2 * na + 1]
        x, y, c = lax.axis_index("x"), lax.axis_index("y"), lax.axis_index("c")
        for a in range(na):
            for j, (dev, peer_chip) in enumerate(_chip_peers(x, y, c)):
                cp = pltpu.make_async_remote_copy(
                    src_ref=views["src"](ins[a], a, c, peer_chip), dst_ref=views["dst"](lands[a], a, c, peer_chip),
                    send_sem=send_sems.at[a * nm + j], recv_sem=recv_sems.at[a * nm + j],
                    device_id=dev, device_id_type=MESH)
                cp.wait_send()
                cp.wait_recv()

    both = list(sources) + list(zones)
    outs = pl.pallas_call(
        body, name=name, out_shape=tuple(pltpu.HBM(b.shape, b.dtype) for b in both),
        in_specs=[HBM_SPEC] * (2 * na) + [SEM_SPEC, SEM_SPEC, pl.BlockSpec(memory_space=pl.ANY)],
        out_specs=tuple([HBM_SPEC] * (2 * na)), input_output_aliases={i: i for i in range(2 * na)},
        compiler_params=pltpu.CompilerParams(has_side_effects=DATAFLOW))(*both, started["send"], started["recv"], after)
    return list(outs[:na]), list(outs[na:])


def weights_share(zones, name):
    na, nm = len(zones), N_CHIPS - 1

    def body(*refs):
        lands = refs[na:2 * na]
        send_sems, recv_sems = refs[2 * na:]
        x, y, c = lax.axis_index("x"), lax.axis_index("y"), lax.axis_index("c")
        chip = 2 * x + y
        sibling = (x, y, 1 - c)
        sends = []
        for a in range(na):
            half = zones[a].shape[1] // 2
            for m in range(1, N_CHIPS):
                mine = lands[a].at[chip ^ m, pl.ds(c * half, half)]
                sends.append(pltpu.make_async_remote_copy(
                    src_ref=mine, dst_ref=mine, send_sem=send_sems.at[a * nm + m - 1],
                    recv_sem=recv_sems.at[a * nm + m - 1], device_id=sibling, device_id_type=MESH))
        for cp in sends:
            cp.start()
        for a in range(na):
            half = zones[a].shape[1] // 2
            for m in range(1, N_CHIPS):
                theirs = lands[a].at[chip ^ m, pl.ds((1 - c) * half, half)]
                pltpu.make_async_remote_copy(
                    src_ref=theirs, dst_ref=theirs, send_sem=send_sems.at[a * nm + m - 1],
                    recv_sem=recv_sems.at[a * nm + m - 1], device_id=sibling, device_id_type=MESH).wait_recv()
        for cp in sends:
            cp.wait_send()

    return pl.pallas_call(
        body, name=name, in_specs=[HBM_SPEC] * na, out_specs=[HBM_SPEC] * na,
        out_shape=[_sds(z.shape, z.dtype) for z in zones], input_output_aliases={i: i for i in range(na)},
        scratch_shapes=[pltpu.SemaphoreType.DMA((na * nm,)), pltpu.SemaphoreType.DMA((na * nm,))])(*zones)


def pair_send_halves(grads, name):
    half = [g.shape[1] // 2 for g in grads]
    shapes = [(g.shape[0], h, g.shape[2]) for g, h in zip(grads, half)]
    return _exchange(grads, shapes, "pair", lambda r, a, i: r.at[:, pl.ds(i * half[a], half[a])],
                     lambda r, a, i: r, lambda a: shapes[a], name, False)


def sum_pair_half(g, recv, core, name, tb=256):
    nk, r, c = g.shape
    tb = min(tb, r // 2)
    nb = r // 2 // tb

    def body(core_ref, g_ref, r_ref, o_ref):
        o_ref[...] = (g_ref[...] + r_ref[...]).astype(bf16)

    return pl.pallas_call(
        body, name=name,
        grid_spec=pltpu.PrefetchScalarGridSpec(
            num_scalar_prefetch=1, grid=(nk, nb),
            in_specs=[pl.BlockSpec((None, tb, c), lambda k, i, core_ref: (k, core_ref[0] * nb + i, 0)),
                      pl.BlockSpec((None, tb, c), lambda k, i, core_ref: (k, i, 0))],
            out_specs=pl.BlockSpec((None, tb, c), lambda k, i, core_ref: (k, i, 0))),
        out_shape=_sds((nk, r // 2, c), bf16), compiler_params=_cparams(2))(
            jnp.reshape(core, (1,)).astype(jnp.int32), g, recv)


def chip_sum_into(acc, layer, own, others, chip, name, tb=256):
    n, r, c = own.shape
    tb = min(tb, r)

    def body(chip_ref, x_ref, y1_ref, y2_ref, y3_ref, acc_ref, o_ref):
        o_ref[...] = ((x_ref[...].astype(f32) + y1_ref[...].astype(f32)) + y2_ref[...].astype(f32)) + y3_ref[...].astype(f32)

    def slot(k):
        return pl.BlockSpec((None, tb, c), lambda i, chip_ref: (chip_ref[0] ^ k, i, 0))

    return pl.pallas_call(
        body, name=name,
        grid_spec=pltpu.PrefetchScalarGridSpec(
            num_scalar_prefetch=1, grid=(r // tb,),
            in_specs=[slot(k) for k in range(n)] + [pl.BlockSpec(memory_space=pl.ANY)],
            out_specs=pl.BlockSpec((None, tb, c), lambda i, chip_ref: (layer, i, 0))),
        out_shape=_sds(acc.shape, f32), input_output_aliases={n + 1: 0}, compiler_params=_cparams(1))(
            jnp.reshape(chip, (1,)).astype(jnp.int32), own, *([others] * (n - 1)), acc)


def adamw_halves(w, g_own, g_recv, m, v, core, name, tb=256):
    depth, r, c = w.shape
    tb = min(tb, r // 2)
    nb = r // 2 // tb

    def body(core_ref, w_ref, go_ref, gr_ref, m_ref, v_ref, g_ref, d_ref, mo_ref, vo_ref):
        gv = jnp.where(pl.program_id(1) == core_ref[0], go_ref[...], gr_ref[...])
        m2 = B1 * m_ref[...] + (1.0 - B1) * gv
        v2 = B2 * v_ref[...] + (1.0 - B2) * (gv * gv)
        m_hat = m2 / (1.0 - B1 ** STEP)
        v_hat = v2 / (1.0 - B2 ** STEP)
        g_ref[...] = gv
        d_ref[...] = -LR * (m_hat / (jnp.sqrt(v_hat) + AEPS) + WD * w_ref[...])
        mo_ref[...] = m2
        vo_ref[...] = v2

    whole = pl.BlockSpec((None, tb, c), lambda l, h, i, core_ref: (l, h * nb + i, 0))
    part = pl.BlockSpec((None, tb, c), lambda l, h, i, core_ref: (l, i, 0))
    return pl.pallas_call(
        body, name=name,
        grid_spec=pltpu.PrefetchScalarGridSpec(num_scalar_prefetch=1, grid=(depth, 2, nb),
                                               in_specs=[whole, part, part, whole, whole], out_specs=[whole] * 4),
        out_shape=[_sds(w.shape, f32)] * 4, compiler_params=_cparams(3))(
            jnp.reshape(core, (1,)).astype(jnp.int32), w, g_own, g_recv, m, v)


def sum_slots(y, out_dtype, name, tb=256):
    n, r, c = y.shape
    tb = min(tb, r)

    def body(y_ref, o_ref):
        acc = y_ref[0].astype(f32)
        for i in range(1, n):
            acc = acc + y_ref[i].astype(f32)
        o_ref[...] = acc.astype(out_dtype)

    return pl.pallas_call(
        body, name=name, grid=(r // tb,),
        in_specs=[pl.BlockSpec((n, tb, c), lambda i: (0, i, 0))], out_specs=pl.BlockSpec((tb, c), lambda i: (i, 0)),
        out_shape=_sds((r, c), out_dtype), compiler_params=_cparams(1))(y)


def adamw(w, g, m, v, name, tb=256):
    r, c = w.shape
    tb = min(tb, r)

    def body(w_ref, g_ref, m_ref, v_ref, d_ref, mo_ref, vo_ref):
        gv = g_ref[...]
        m2 = B1 * m_ref[...] + (1.0 - B1) * gv
        v2 = B2 * v_ref[...] + (1.0 - B2) * (gv * gv)
        m_hat = m2 / (1.0 - B1 ** STEP)
        v_hat = v2 / (1.0 - B2 ** STEP)
        d_ref[...] = -LR * (m_hat / (jnp.sqrt(v_hat) + AEPS) + WD * w_ref[...])
        mo_ref[...] = m2
        vo_ref[...] = v2

    spec = pl.BlockSpec((tb, c), lambda i: (i, 0))
    return pl.pallas_call(
        body, name=name, grid=(r // tb,), in_specs=[spec] * 4, out_specs=[spec] * 3,
        out_shape=[_sds((r, c), f32)] * 3, compiler_params=_cparams(1))(w, g, m, v)


def adamw_leading(w, g, m, v, name, tc=64):
    c, l, r = w.shape
    main = c // tc
    tail = c - main * tc

    def body(w_ref, g_ref, m_ref, v_ref, *rest):
        d_ref, mo_ref, vo_ref = rest[-3:]
        gv = g_ref[...]
        m2 = B1 * m_ref[...] + (1.0 - B1) * gv
        v2 = B2 * v_ref[...] + (1.0 - B2) * (gv * gv)
        m_hat = m2 / (1.0 - B1 ** STEP)
        v_hat = v2 / (1.0 - B2 ** STEP)
        d_ref[...] = -LR * (m_hat / (jnp.sqrt(v_hat) + AEPS) + WD * w_ref[...])
        mo_ref[...] = m2
        vo_ref[...] = v2

    spec = pl.BlockSpec((tc, l, r), lambda i: (i, 0, 0))
    outs = pl.pallas_call(
        functools.partial(body), name=name, grid=(main,), in_specs=[spec] * 4, out_specs=[spec] * 3,
        out_shape=[_sds(w.shape, f32)] * 3, compiler_params=_cparams(1))(w, g, m, v)
    if tail:
        assert (main * tc) % tail == 0
        last = pl.BlockSpec((tail, l, r), lambda i: (main * tc // tail, 0, 0))
        outs = pl.pallas_call(
            functools.partial(body), name=name + "_tail", grid=(1,),
            in_specs=[last] * 4 + [pl.BlockSpec(memory_space=pl.ANY)] * 3, out_specs=[last] * 3,
            out_shape=[_sds(w.shape, f32)] * 3, input_output_aliases={4: 0, 5: 1, 6: 2},
            compiler_params=_cparams(1))(w, g, m, v, *outs)
    return outs


SMALL_ROW = 1024
SMALL_GAINS = ("g1", "ga", "gs", "g2", "g3", "g4")
SMALL_LAYER_ROWS = 8 + 8 + 16 + 8


def _pack_small(grads):
    wide = lambda a: jnp.pad(a, ((0, 0), (0, 2 * SMALL_ROW - a.shape[1]))).reshape(-1, SMALL_ROW)
    row = lax.broadcasted_iota(jnp.int32, (8, SMALL_ROW), 0)
    parts = []
    for g in grads:
        singles = [g[k] for k in SMALL_GAINS] + [g["bs"][:, :SMALL_ROW],
                                                 jnp.pad(g["bs"][:, SMALL_ROW:], ((0, 0), (0, 2 * SMALL_ROW - XBC)))]
        first = sum(jnp.where(row == k, s, 0.0) for k, s in enumerate(singles))
        parts += [first, g["wa"], wide(g["ws"]), jnp.pad(g["par"], ((0, 0), (0, SMALL_ROW - CH)))]
    return jnp.concatenate(parts, axis=0)


def _unpack_small(packed, depth):
    rows = packed.reshape(depth, SMALL_LAYER_ROWS, SMALL_ROW)
    out = {k: rows[:, i] for i, k in enumerate(SMALL_GAINS)}
    out["bs"] = rows[:, 6:8].reshape(depth, 2 * SMALL_ROW)[:, :XBC]
    out["wa"] = rows[:, 8:11]
    out["ws"] = rows[:, 16:32].reshape(depth, 8, 2 * SMALL_ROW)[:, :4, :XBC]
    out["par"] = rows[:, 32:35, :CH]
    return out


def kernel(x, norm_mix_pre, w_in, conv_a_w, ssm_conv_w, ssm_conv_b, dt_bias, a_log, d_skip, conv_out_norm, ssm_out_norm, w_out, norm_mix_post, norm_mlp_pre, w_up, w_down, norm_mlp_post, loss_target, m_norm_mix_pre, m_w_in, m_conv_a_w, m_ssm_conv_w, m_ssm_conv_b, m_dt_bias, m_a_log, m_d_skip, m_conv_out_norm, m_ssm_out_norm, m_w_out, m_norm_mix_post, m_norm_mlp_pre, m_w_up, m_w_down, m_norm_mlp_post, v_norm_mix_pre, v_w_in, v_conv_a_w, v_ssm_conv_w, v_ssm_conv_b, v_dt_bias, v_a_log, v_d_skip, v_conv_out_norm, v_ssm_out_norm, v_w_out, v_norm_mix_post, v_norm_mlp_pre, v_w_up, v_w_down, v_norm_mlp_post):
    nb, seq, _ = x.shape
    t = nb * seq
    depth = w_in.shape[0]
    ncol = w_in.shape[2]
    chip = 2 * lax.axis_index("x") + lax.axis_index("y")

    taps = [conv_a_w, ssm_conv_w]
    taps_g = all_gather(taps, "chips", "gather_taps", slot_axis=1, own=False)
    wa_g, ws_g = [lax.dynamic_update_index_in_dim(g, s, chip, 1) for g, s in zip(taps_g, taps)]
    wa_full = jnp.transpose(wa_g, (0, 2, 1, 3)).reshape(depth, 3, D)
    ws_full = jnp.transpose(ws_g, (0, 2, 1, 3)).reshape(depth, 4, XBC)
    lane_pad = lambda a: jnp.pad(a, ((0, 0), (0, CH - a.shape[1])))
    par = jnp.stack([lane_pad(dt_bias), lane_pad(a_log), lane_pad(d_skip)], axis=1)
    par = jnp.pad(par, ((0, 0), (0, 5), (0, 0)))

    layer_shards = lambda l: [w_in[l].astype(bf16), w_out[l].astype(bf16), w_up[l].astype(bf16), w_down[l].astype(bf16)]
    issued = []

    def start(shards, name):
        zones = [lax.empty((N_CHIPS,) + s.shape, s.dtype) for s in shards]
        issued.append(chips_start(shards, zones, _weight_views(shards), name,
                                  after=issued[-1]["token"] if issued else None))
        return issued[-1]

    def finish(started, after, name):
        shards, zones = chips_wait(started, after, name)
        zones = weights_share(zones, "weights_share")
        return [lax.dynamic_update_index_in_dim(z, s, chip, 0) for z, s in zip(zones, shards)]

    def shaped(mats):
        wo_z, wu_z, wd_z = mats
        return wo_z.reshape(2 * D, D), wu_z, wd_z.reshape(DFF, D)

    first = layer_shards(0)
    travelling = {0: start(first[:1], "weights_start_0")}
    rest = start(first[1:], "weights_start_0_rest")
    for l in range(1, depth):
        travelling[l] = start(layer_shards(l), f"weights_start_{l}")

    def weights_of(l, x_in):
        mats = finish(travelling.pop(l), x_in, f"weights_wait_{l}")
        win_full = jnp.pad(jnp.transpose(mats[0], (1, 0, 2)).reshape(D, N_CHIPS * ncol),
                           ((0, 0), (0, PROJ - N_CHIPS * ncol)))
        w = dict(win=win_full, wa=jnp.pad(wa_full[l], ((0, 5), (0, 0))), ws=jnp.pad(ws_full[l], ((0, 4), (0, 0))),
                 bs=ssm_conv_b[l][None], par=par[l], g1=norm_mix_pre[l][None], ga=conv_out_norm[l][None],
                 gs=ssm_out_norm[l][None], g2=norm_mix_post[l][None], g3=norm_mlp_pre[l][None],
                 g4=norm_mlp_post[l][None])
        if l == 0:
            w["late"] = lambda after: dict(zip(("wo", "wu", "wd"), shaped(finish(rest, after, "weights_wait_0_rest"))))
        else:
            w.update(zip(("wo", "wu", "wd"), shaped(mats[1:])))
        return w

    core = lax.axis_index("c")
    grads_travelling = {}

    chip_major = dict(win=lambda a: a[None], wo=lambda a: a.reshape(N_CHIPS, 2 * D // N_CHIPS, D), wu=lambda a: a,
                      wd=lambda a: a.reshape(N_CHIPS, DFF // N_CHIPS, D))
    held = {}

    def grads_done(l, g, last):
        if l > 0 and not last:
            held[l] = g
            return None
        g = {**held.pop(l, {}), **g}
        keys = [k for k in ("win", "wo", "wu", "wd") if k in g]
        mats = [chip_major[k](g[k]) for k in keys]
        received = pair_send_halves(mats, "grads_to_pair")
        sums = [sum_pair_half(m_, r_, core, "pair_sum") for m_, r_ in zip(mats, received)]
        if "win" in keys:
            sums[0] = jnp.transpose(sums[0][0, :, :N_CHIPS * ncol].reshape(D // 2, N_CHIPS, ncol), (1, 0, 2))
        zones = [lax.empty(s.shape, s.dtype) for s in sums]
        started = chips_start(sums, zones, _grad_views(sums), f"grads_start_{l}_{len(grads_travelling)}")
        grads_travelling[(l, keys[0])] = (keys, started)
        return started["token"]

    sse, dx, grads = local_step(x.reshape(t, D), loss_target.reshape(t, D), depth, weights_of, seq, grads_done)
    loss = lax.psum(0.5 / D * sse[0, 0], ("x", "y", "c"))

    small_all = all_gather([_pack_small(grads)], "all", "gather_small")[0]
    small_sum = sum_slots(small_all, f32, "small_sum", tb=8)
    small = _unpack_small(small_sum, depth)

    big_w = dict(win=w_in, wo=w_out, wu=w_up, wd=w_down)
    acc = {k: lax.empty((depth, bw.shape[1] // 2, bw.shape[2]), f32) for k, bw in big_w.items()}
    for n, ((l, _), (keys, started)) in enumerate(grads_travelling.items()):
        sums, zones = chips_wait(started, small_sum, f"grads_wait_{l}_{n}")
        for k, s, z in zip(keys, sums, zones):
            acc[k] = chip_sum_into(acc[k], l, s, z, chip, "chip_sum")
    acc = [acc[k] for k in ("win", "wo", "wu", "wd")]
    from_sibling = _exchange(acc, [a.shape for a in acc], "pair", lambda r, a, i: r, lambda r, a, i: r,
                             lambda a: acc[a].shape, "grads_from_pair", False)

    wa_cols, ws_cols = conv_a_w.shape[2], ssm_conv_w.shape[2]
    par_g = small["par"].reshape(depth, 3, CH)
    g_small = dict(
        norm_mix_pre=small["g1"], conv_out_norm=small["ga"], ssm_out_norm=small["gs"], norm_mix_post=small["g2"],
        norm_mlp_pre=small["g3"], norm_mlp_post=small["g4"], ssm_conv_b=small["bs"],
        conv_a_w=lax.dynamic_slice_in_dim(small["wa"].reshape(depth, 3, D), chip * wa_cols, wa_cols, axis=2),
        ssm_conv_w=lax.dynamic_slice_in_dim(small["ws"].reshape(depth, 4, XBC), chip * ws_cols, ws_cols, axis=2),
        dt_bias=par_g[:, 0, :NH], a_log=par_g[:, 1, :NH], d_skip=par_g[:, 2, :NH])

    given = dict(norm_mix_pre=(norm_mix_pre, m_norm_mix_pre, v_norm_mix_pre), w_in=(w_in, m_w_in, v_w_in),
                 conv_a_w=(conv_a_w, m_conv_a_w, v_conv_a_w), ssm_conv_w=(ssm_conv_w, m_ssm_conv_w, v_ssm_conv_w),
                 ssm_conv_b=(ssm_conv_b, m_ssm_conv_b, v_ssm_conv_b), dt_bias=(dt_bias, m_dt_bias, v_dt_bias),
                 a_log=(a_log, m_a_log, v_a_log), d_skip=(d_skip, m_d_skip, v_d_skip),
                 conv_out_norm=(conv_out_norm, m_conv_out_norm, v_conv_out_norm),
                 ssm_out_norm=(ssm_out_norm, m_ssm_out_norm, v_ssm_out_norm), w_out=(w_out, m_w_out, v_w_out),
                 norm_mix_post=(norm_mix_post, m_norm_mix_post, v_norm_mix_post),
                 norm_mlp_pre=(norm_mlp_pre, m_norm_mlp_pre, v_norm_mlp_pre), w_up=(w_up, m_w_up, v_w_up),
                 w_down=(w_down, m_w_down, v_w_down), norm_mlp_post=(norm_mlp_post, m_norm_mlp_post, v_norm_mlp_post))
    halves = dict(zip(["w_in", "w_out", "w_up", "w_down"], zip(acc, from_sibling)))
    order = ["norm_mix_pre", "w_in", "conv_a_w", "ssm_conv_w", "ssm_conv_b", "dt_bias", "a_log", "d_skip",
             "conv_out_norm", "ssm_out_norm", "w_out", "norm_mix_post", "norm_mlp_pre", "w_up", "w_down",
             "norm_mlp_post"]
    g_out, d_out, m_out, v_out = [], [], [], []
    for n in order:
        wv, mv, vv = given[n]
        if n in halves and wv.shape[-1] % CH:
            own, recv = halves[n]
            gv = jnp.concatenate([jnp.where(core == 0, own, recv), jnp.where(core == 0, recv, own)], axis=1)
            to_cols, to_rows = (lambda a: jnp.transpose(a, (2, 0, 1))), (lambda a: jnp.transpose(a, (1, 2, 0)))
            dlt, m2, v2 = [to_rows(o) for o in adamw_leading(to_cols(wv), to_cols(gv), to_cols(mv), to_cols(vv),
                                                             "adamw_cols")]
        elif n in halves:
            gv, dlt, m2, v2 = adamw_halves(wv, *halves[n], mv, vv, core, "adamw_matrix")
        else:
            gv = g_small[n].reshape(wv.shape)
            two_d = lambda a: a.reshape(-1, a.shape[-1])
            dlt, m2, v2 = adamw(two_d(wv), two_d(gv), two_d(mv), two_d(vv), "adamw")
        g_out.append(gv)
        d_out.append(dlt.reshape(wv.shape))
        m_out.append(m2.reshape(wv.shape))
        v_out.append(v2.reshape(wv.shape))
    return (loss, dx.reshape(nb, seq, D), *g_out, *d_out, *m_out, *v_out)
```

```python
import functools

import jax
import jax.numpy as jnp
from jax import lax
from jax.experimental import pallas as pl
from jax.experimental.pallas import tpu as pltpu

f32, bf16 = jnp.float32, jnp.bfloat16

D = 1024
NH, HP = 16, 64
NG, NS = 2, 128
CH = 128
XBC = D + 2 * NG * NS
DFF = 4 * D
IN_COLS = 3 * D + D + XBC + NH
PROJ = 5760
COL_Z, COL_XBC, COL_DT = 3 * D, 4 * D, 4 * D + XBC
EPS = 1e-6
HALO = 8
VMEM_LIMIT = 56 * 2**20
MESH = pl.DeviceIdType.MESH

LR, B1, B2, AEPS, WD, STEP = 0.001, 0.9, 0.999, 1e-08, 0.01, 10


def _cparams(n_axes):
    return pltpu.CompilerParams(dimension_semantics=("arbitrary",) * n_axes, vmem_limit_bytes=VMEM_LIMIT)


def _sds(shape, dtype):
    return jax.ShapeDtypeStruct(tuple(shape), dtype)


def _token_spec(token):
    return [] if token is None else [pl.BlockSpec(memory_space=pl.ANY)]


def _token_arg(token):
    return [] if token is None else [token]


def _rms_fwd(x, g):
    r = lax.rsqrt(jnp.mean(x * x, axis=-1, keepdims=True) + EPS)
    return x * r * g


def _rms_bwd(x, g, dy):
    r = lax.rsqrt(jnp.mean(x * x, axis=-1, keepdims=True) + EPS)
    xh = x * r
    gdy = dy * g
    dx = r * (gdy - xh * jnp.mean(xh * gdy, axis=-1, keepdims=True))
    return dx, dy * xh


def _accum(ref, part, first):
    @pl.when(first)
    def _():
        ref[...] = part

    @pl.when(jnp.logical_not(first))
    def _():
        ref[...] += part


def _dot_nt(a, b):
    return lax.dot_general(a, b, (((1,), (1,)), ((), ())), preferred_element_type=f32)


def _dot_tn(a, b):
    return lax.dot_general(a, b, (((0,), (0,)), ((), ())), preferred_element_type=f32)


def _dot(a, b):
    return jnp.dot(a, b, preferred_element_type=f32)


def _split_dot(x, e_bf, n_split, nt=False):
    acc = None
    rem = x
    for s in range(n_split):
        hi = rem.astype(bf16)
        term = _dot_nt(hi, e_bf) if nt else _dot(hi, e_bf)
        acc = term if acc is None else acc + term
        if s + 1 < n_split:
            rem = rem - hi.astype(f32)
    return acc


def _sigmoid(x):
    return 0.5 * jnp.tanh(0.5 * x) + 0.5


def norm_matmul(x, g, w, tm, tn, out_dtype, name, token=None):
    t = x.shape[0]
    if w.ndim == 3:
        assert w.shape[2] == tn
        n = w.shape[0] * tn
        w_spec = pl.BlockSpec((None, D, tn), lambda i, j: (j, 0, 0))
    else:
        n = w.shape[1]
        w_spec = pl.BlockSpec((D, tn), lambda i, j: (0, j))

    def body(x_ref, g_ref, w_ref, *rest):
        o_ref, h_ref = rest[-2:]

        @pl.when(pl.program_id(1) == 0)
        def _():
            h_ref[...] = _rms_fwd(x_ref[...], g_ref[...]).astype(bf16)

        o_ref[...] = _dot(h_ref[...], w_ref[...]).astype(out_dtype)

    return pl.pallas_call(
        body, name=name, grid=(t // tm, n // tn),
        in_specs=[pl.BlockSpec((tm, D), lambda i, j: (i, 0)), pl.BlockSpec((1, D), lambda i, j: (0, 0)), w_spec]
        + _token_spec(token),
        out_specs=[pl.BlockSpec((tm, tn), lambda i, j: (i, j)), pl.BlockSpec((tm, D), lambda i, j: (i, 0))],
        out_shape=[_sds((t, n), out_dtype), _sds((t, D), bf16)],
        compiler_params=_cparams(2))(x, g, w, *_token_arg(token))


def matmul_postnorm(a, w, xres, g, tm, relu2, name):
    t, k = a.shape

    def body(a_ref, w_ref, xr_ref, g_ref, y_ref, xo_ref):
        av = a_ref[...]
        if relu2:
            af = jnp.maximum(av.astype(f32), 0.0)
            av = (af * af).astype(bf16)
        y = _dot(av, w_ref[...])
        y_ref[...] = y
        xo_ref[...] = xr_ref[...] + _rms_fwd(y, g_ref[...])

    return pl.pallas_call(
        body, name=name, grid=(t // tm,),
        in_specs=[pl.BlockSpec((tm, k), lambda i: (i, 0)), pl.BlockSpec((k, D), lambda i: (0, 0)),
                  pl.BlockSpec((tm, D), lambda i: (i, 0)), pl.BlockSpec((1, D), lambda i: (0, 0))],
        out_specs=[pl.BlockSpec((tm, D), lambda i: (i, 0)), pl.BlockSpec((tm, D), lambda i: (i, 0))],
        out_shape=[_sds((t, D), f32), _sds((t, D), f32)],
        compiler_params=_cparams(1))(a, w, xres, g)


def postnorm_bwd_matmul(y, g, dxo, w, fp, tm, tn, out_dtype, name, token=None):
    t, n = y.shape[0], w.shape[0]
    relu = fp is not None

    def body(*refs):
        y_ref, g_ref, dxo_ref, w_ref = refs[:4]
        fp_ref = refs[4] if relu else None
        dy_ref, dg_ref, da_ref = refs[-3:]
        i, j = pl.program_id(0), pl.program_id(1)

        @pl.when(j == 0)
        def _():
            dx, dgc = _rms_bwd(y_ref[...], g_ref[...], dxo_ref[...])
            dy_ref[...] = dx.astype(bf16)
            _accum(dg_ref, jnp.sum(dgc, axis=0, keepdims=True), i == 0)

        da = _dot_nt(dy_ref[...], w_ref[...])
        if relu:
            da = da * (2.0 * jnp.maximum(fp_ref[...].astype(f32), 0.0))
        da_ref[...] = da.astype(out_dtype)

    in_specs = [pl.BlockSpec((tm, D), lambda i, j: (i, 0)), pl.BlockSpec((1, D), lambda i, j: (0, 0)),
                pl.BlockSpec((tm, D), lambda i, j: (i, 0)), pl.BlockSpec((tn, D), lambda i, j: (j, 0))]
    args = [y, g, dxo, w]
    if relu:
        in_specs.append(pl.BlockSpec((tm, tn), lambda i, j: (i, j)))
        args.append(fp)
    in_specs += _token_spec(token)
    args += _token_arg(token)
    return pl.pallas_call(
        body, name=name, grid=(t // tm, n // tn), in_specs=in_specs,
        out_specs=[pl.BlockSpec((tm, D), lambda i, j: (i, 0)), pl.BlockSpec((1, D), lambda i, j: (0, 0)),
                   pl.BlockSpec((tm, tn), lambda i, j: (i, j))],
        out_shape=[_sds((t, D), bf16), _sds((1, D), f32), _sds((t, n), out_dtype)],
        compiler_params=_cparams(2))(*args)


def matmul_prenorm_bwd(da, w, x, g, dxo, tm, name, token=None):
    t, k = da.shape
    blocked = w.ndim == 3

    def body(da_ref, w_ref, x_ref, g_ref, dxo_ref, *rest):
        dx_ref, dg_ref = rest[-2:]
        if blocked:
            kc = w.shape[2]
            dh = _dot_nt(da_ref[:, 0:kc], w_ref[0])
            for q in range(1, w.shape[0]):
                dh = dh + _dot_nt(da_ref[:, q * kc:(q + 1) * kc], w_ref[q])
        else:
            dh = _dot_nt(da_ref[...], w_ref[...])
        dxn, dgc = _rms_bwd(x_ref[...], g_ref[...], dh)
        dx_ref[...] = dxo_ref[...] + dxn
        _accum(dg_ref, jnp.sum(dgc, axis=0, keepdims=True), pl.program_id(0) == 0)

    w_spec = pl.BlockSpec(w.shape, (lambda i: (0, 0, 0)) if blocked else (lambda i: (0, 0)))
    return pl.pallas_call(
        body, name=name, grid=(t // tm,),
        in_specs=[pl.BlockSpec((tm, k), lambda i: (i, 0)), w_spec,
                  pl.BlockSpec((tm, D), lambda i: (i, 0)), pl.BlockSpec((1, D), lambda i: (0, 0)),
                  pl.BlockSpec((tm, D), lambda i: (i, 0))] + _token_spec(token),
        out_specs=[pl.BlockSpec((tm, D), lambda i: (i, 0)), pl.BlockSpec((1, D), lambda i: (0, 0))],
        out_shape=[_sds((t, D), f32), _sds((1, D), f32)],
        compiler_params=_cparams(1))(da, w, x, g, dxo, *_token_arg(token))


def matmul_tn(a, b, tm, tn, relu2, name, col_blocks=False):
    t, m = a.shape
    n = b.shape[1]
    if col_blocks:
        out_spec, out_shape = pl.BlockSpec((None, tm, tn), lambda i, j: (j, i, 0)), _sds((n // tn, m, tn), f32)
    else:
        out_spec, out_shape = pl.BlockSpec((tm, tn), lambda i, j: (i, j)), _sds((m, n), f32)

    def body(a_ref, b_ref, o_ref, at_ref):
        @pl.when(pl.program_id(1) == 0)
        def _():
            av = a_ref[...]
            if relu2:
                af = jnp.maximum(av.astype(f32), 0.0)
                av = (af * af).astype(bf16)
            at_ref[...] = av.T

        o_ref[...] = _dot(at_ref[...], b_ref[...])

    return pl.pallas_call(
        body, name=name, grid=(m // tm, n // tn),
        in_specs=[pl.BlockSpec((t, tm), lambda i, j: (0, i)), pl.BlockSpec((t, tn), lambda i, j: (0, j))],
        out_specs=out_spec, out_shape=out_shape,
        scratch_shapes=[pltpu.VMEM((tm, t), bf16)],
        compiler_params=_cparams(2))(a, b)


ROWS_A = 16
ROWS_B = 32
UNROLL = 4


def _past(win, s):
    return (win if s == 0 else pltpu.roll(win, s, 0))[HALO:]


def _future(win, s):
    n = win.shape[0]
    return (win if s == 0 else pltpu.roll(win, n - s, 0))[:n - HALO]


def _fold8(v):
    return v.reshape(v.shape[0] // 8, 8, v.shape[1]).sum(axis=0)


def _halo_prev(tb, col):
    return lambda i: (jnp.maximum(i * (tb // HALO) - 1, 0), col)


def _halo_next(tb, col, t):
    return lambda i: (jnp.minimum((i + 1) * (tb // HALO), t // HALO - 1), col)


def group_a_fwd(proj, wa, g, seq, tb, name):
    t = proj.shape[0]
    bps = seq // tb

    def body(xa_ref, ca_ref, ba_ref, xah_ref, cah_ref, wa_ref, g_ref, o_ref, u_scr):
        first = (pl.program_id(0) % bps) == 0
        u_scr[0:HALO, :] = jnp.where(first, 0.0, cah_ref[...] * xah_ref[...])
        w, gv = wa_ref[...], g_ref[...]

        def chunk(i, carry):
            r = pl.multiple_of(i * ROWS_A, ROWS_A)
            rows = pl.ds(r, ROWS_A)
            u_scr[pl.ds(pl.multiple_of(HALO + r, HALO), ROWS_A), :] = ca_ref[rows, :] * xa_ref[rows, :]
            win = u_scr[pl.ds(r, ROWS_A + HALO), :]
            cv = w[2:3] * _past(win, 0) + w[1:2] * _past(win, 1) + w[0:1] * _past(win, 2)
            o_ref[rows, :] = _rms_fwd(ba_ref[rows, :] * cv, gv).astype(bf16)
            return carry

        lax.fori_loop(0, tb // ROWS_A, chunk, 0, unroll=UNROLL)

    blk = lambda c: pl.BlockSpec((tb, D), lambda i: (i, c))
    return pl.pallas_call(
        body, name=name, grid=(t // tb,),
        in_specs=[blk(0), blk(1), blk(2),
                  pl.BlockSpec((HALO, D), _halo_prev(tb, 0)), pl.BlockSpec((HALO, D), _halo_prev(tb, 1)),
                  pl.BlockSpec((8, D), lambda i: (0, 0)), pl.BlockSpec((1, D), lambda i: (0, 0))],
        out_specs=pl.BlockSpec((tb, D), lambda i: (i, 0)),
        out_shape=_sds((t, 2 * D), bf16),
        scratch_shapes=[pltpu.VMEM((tb + HALO, D), f32)],
        compiler_params=_cparams(1))(proj, proj, proj, proj, proj, wa, g)


def group_a_bwd(proj, dcat, wa, g, seq, tb, name, token=None):
    t = proj.shape[0]
    bps = seq // tb

    def body(xa_ref, ca_ref, ba_ref, dy_ref, xap_ref, cap_ref, xan_ref, can_ref, ban_ref, dyn_ref, wa_ref, g_ref,
             *rest):
        dp_ref, dwa_ref, dg_ref, u_scr, d_scr, acc_scr = rest[-6:]
        i = pl.program_id(0)
        first = (i % bps) == 0
        last = (i % bps) == bps - 1
        w = wa_ref[...]
        gv = g_ref[...]
        u_scr[0:HALO, :] = jnp.where(first, 0.0, cap_ref[...] * xap_ref[...])
        u_scr[HALO + tb:2 * HALO + tb, :] = can_ref[...] * xan_ref[...]
        acc_scr[...] = jnp.zeros_like(acc_scr)

        def forward_part(n, carry):
            r = pl.multiple_of(n * ROWS_A, ROWS_A)
            rows = pl.ds(r, ROWS_A)
            ba = ba_ref[rows, :]
            u_scr[pl.ds(pl.multiple_of(HALO + r, HALO), ROWS_A), :] = ca_ref[rows, :] * xa_ref[rows, :]
            win = u_scr[pl.ds(r, ROWS_A + HALO), :]
            u = [_past(win, s) for s in range(3)]
            cv = w[2:3] * u[0] + w[1:2] * u[1] + w[0:1] * u[2]
            dya, dgc = _rms_bwd(ba * cv, gv, dy_ref[rows, :])
            dcv = dya * ba
            d_scr[rows, :] = dcv
            dp_ref[rows, 2 * D:3 * D] = (dya * cv).astype(bf16)
            acc_scr[0:8, :] += _fold8(dgc)
            for k in range(3):
                acc_scr[8 + 8 * k:16 + 8 * k, :] += _fold8(dcv * u[2 - k])
            return carry

        lax.fori_loop(0, tb // ROWS_A, forward_part, 0, unroll=UNROLL)

        start = HALO + tb
        cvn = (w[2:3] * u_scr[pl.ds(start, HALO), :] + w[1:2] * u_scr[pl.ds(start - 1, HALO), :]
               + w[0:1] * u_scr[pl.ds(start - 2, HALO), :])
        ban = ban_ref[...]
        dyan, _ = _rms_bwd(ban * cvn, gv, dyn_ref[...])
        d_scr[tb:tb + HALO, :] = jnp.where(last, 0.0, dyan * ban)

        def backward_part(n, carry):
            r = pl.multiple_of(n * ROWS_A, ROWS_A)
            rows = pl.ds(r, ROWS_A)
            win = d_scr[pl.ds(r, ROWS_A + HALO), :]
            du = w[2:3] * _future(win, 0) + w[1:2] * _future(win, 1) + w[0:1] * _future(win, 2)
            dp_ref[rows, 0:D] = (du * ca_ref[rows, :]).astype(bf16)
            dp_ref[rows, D:2 * D] = (du * xa_ref[rows, :]).astype(bf16)
            return carry

        lax.fori_loop(0, tb // ROWS_A, backward_part, 0, unroll=UNROLL)

        row = lax.broadcasted_iota(jnp.int32, (8, D), 0)
        dw = jnp.zeros((8, D), f32)
        for k in range(3):
            dw = jnp.where(row == k, jnp.sum(acc_scr[8 + 8 * k:16 + 8 * k, :], axis=0, keepdims=True), dw)
        _accum(dwa_ref, dw, i == 0)
        _accum(dg_ref, jnp.sum(acc_scr[0:8, :], axis=0, keepdims=True), i == 0)

    blk = lambda c: pl.BlockSpec((tb, D), lambda i: (i, c))
    prv = lambda c: pl.BlockSpec((HALO, D), _halo_prev(tb, c))
    nxt = lambda c: pl.BlockSpec((HALO, D), _halo_next(tb, c, t))
    return pl.pallas_call(
        body, name=name, grid=(t // tb,),
        in_specs=[blk(0), blk(1), blk(2), blk(0), prv(0), prv(1), nxt(0), nxt(1), nxt(2), nxt(0),
                  pl.BlockSpec((8, D), lambda i: (0, 0)), pl.BlockSpec((1, D), lambda i: (0, 0))] + _token_spec(token),
        out_specs=[pl.BlockSpec((tb, 3 * D), lambda i: (i, 0)), pl.BlockSpec((8, D), lambda i: (0, 0)),
                   pl.BlockSpec((1, D), lambda i: (0, 0))],
        out_shape=[_sds((t, PROJ), bf16), _sds((8, D), f32), _sds((1, D), f32)],
        scratch_shapes=[pltpu.VMEM((tb + 2 * HALO, D), f32), pltpu.VMEM((tb + HALO, D), f32), pltpu.VMEM((32, D), f32)],
        compiler_params=_cparams(1))(proj, proj, proj, dcat, proj, proj, proj, proj, proj, dcat, wa, g,
                                     *_token_arg(token))


CB = 512
XBC_BLK0 = COL_XBC // CB


def conv_b_fwd(proj, ws, bs, seq, tb, name):
    t = proj.shape[0]
    bps = seq // tb

    def body(x_ref, xp_ref, w_ref, b_ref, o_ref, x_scr):
        first = (pl.program_id(1) % bps) == 0
        x_scr[0:HALO, :] = jnp.where(first, 0.0, xp_ref[...])
        w, bias = w_ref[...], b_ref[...]

        def chunk(n, carry):
            r = pl.multiple_of(n * ROWS_B, ROWS_B)
            rows = pl.ds(r, ROWS_B)
            x_scr[pl.ds(pl.multiple_of(HALO + r, HALO), ROWS_B), :] = x_ref[rows, :]
            win = x_scr[pl.ds(r, ROWS_B + HALO), :]
            xc = bias + w[3:4] * _past(win, 0)
            for k in range(3):
                xc = xc + w[k:k + 1] * _past(win, 3 - k)
            o_ref[rows, :] = xc * _sigmoid(xc)
            return carry

        lax.fori_loop(0, tb // ROWS_B, chunk, 0, unroll=UNROLL)

    return pl.pallas_call(
        body, name=name, grid=(XBC // CB, t // tb),
        in_specs=[pl.BlockSpec((tb, CB), lambda j, i: (i, XBC_BLK0 + j)),
                  pl.BlockSpec((HALO, CB), lambda j, i: (jnp.maximum(i * (tb // HALO) - 1, 0), XBC_BLK0 + j)),
                  pl.BlockSpec((8, CB), lambda j, i: (0, j)), pl.BlockSpec((1, CB), lambda j, i: (0, j))],
        out_specs=pl.BlockSpec((tb, CB), lambda j, i: (i, j)),
        out_shape=_sds((t, XBC), f32),
        scratch_shapes=[pltpu.VMEM((tb + HALO, CB), f32)],
        compiler_params=_cparams(2))(proj, proj, ws, bs)


def conv_b_bwd(proj, dxs, ws, bs, dproj, seq, tb, name):
    t = proj.shape[0]
    bps = seq // tb

    def body(x_ref, xp_ref, xn_ref, d_ref, dn_ref, w_ref, b_ref, dproj_ref, dx_ref, dw_ref, db_ref, x_scr, d_scr,
             acc_scr):
        i = pl.program_id(1)
        first = (i % bps) == 0
        last = (i % bps) == bps - 1
        w = w_ref[...]
        bias = b_ref[...]
        x_scr[0:HALO, :] = jnp.where(first, 0.0, xp_ref[...])
        x_scr[HALO + tb:2 * HALO + tb, :] = xn_ref[...]
        acc_scr[...] = jnp.zeros_like(acc_scr)

        def dsilu(xc, d):
            sg = _sigmoid(xc)
            return d * (sg * (1.0 + xc * (1.0 - sg)))

        def forward_part(n, carry):
            r = pl.multiple_of(n * ROWS_B, ROWS_B)
            rows = pl.ds(r, ROWS_B)
            x_scr[pl.ds(pl.multiple_of(HALO + r, HALO), ROWS_B), :] = x_ref[rows, :]
            win = x_scr[pl.ds(r, ROWS_B + HALO), :]
            xs = [_past(win, s) for s in range(4)]
            xc = bias + w[3:4] * xs[0]
            for k in range(3):
                xc = xc + w[k:k + 1] * xs[3 - k]
            dxc = dsilu(xc, d_ref[rows, :])
            d_scr[rows, :] = dxc
            acc_scr[0:8, :] += _fold8(dxc)
            for k in range(4):
                acc_scr[8 + 8 * k:16 + 8 * k, :] += _fold8(dxc * xs[3 - k])
            return carry

        lax.fori_loop(0, tb // ROWS_B, forward_part, 0, unroll=UNROLL)

        start = HALO + tb
        xcn = bias + w[3:4] * x_scr[pl.ds(start, HALO), :]
        for k in range(3):
            xcn = xcn + w[k:k + 1] * x_scr[pl.ds(start - 3 + k, HALO), :]
        d_scr[tb:tb + HALO, :] = jnp.where(last, 0.0, dsilu(xcn, dn_ref[...]))

        def backward_part(n, carry):
            r = pl.multiple_of(n * ROWS_B, ROWS_B)
            win = d_scr[pl.ds(r, ROWS_B + HALO), :]
            dx = w[3:4] * _future(win, 0)
            for k in range(3):
                dx = dx + w[k:k + 1] * _future(win, 3 - k)
            dx_ref[pl.ds(r, ROWS_B), :] = dx.astype(bf16)
            return carry

        lax.fori_loop(0, tb // ROWS_B, backward_part, 0, unroll=UNROLL)

        row = lax.broadcasted_iota(jnp.int32, (8, CB), 0)
        dw = jnp.zeros((8, CB), f32)
        for k in range(4):
            dw = jnp.where(row == k, jnp.sum(acc_scr[8 + 8 * k:16 + 8 * k, :], axis=0, keepdims=True), dw)
        _accum(dw_ref, dw, i == 0)
        _accum(db_ref, jnp.sum(acc_scr[0:8, :], axis=0, keepdims=True), i == 0)

    nh = t // HALO
    return pl.pallas_call(
        body, name=name, grid=(XBC // CB, t // tb),
        in_specs=[pl.BlockSpec((tb, CB), lambda j, i: (i, XBC_BLK0 + j)),
                  pl.BlockSpec((HALO, CB), lambda j, i: (jnp.maximum(i * (tb // HALO) - 1, 0), XBC_BLK0 + j)),
                  pl.BlockSpec((HALO, CB), lambda j, i: (jnp.minimum((i + 1) * (tb // HALO), nh - 1), XBC_BLK0 + j)),
                  pl.BlockSpec((tb, CB), lambda j, i: (i, j)),
                  pl.BlockSpec((HALO, CB), lambda j, i: (jnp.minimum((i + 1) * (tb // HALO), nh - 1), j)),
                  pl.BlockSpec((8, CB), lambda j, i: (0, j)), pl.BlockSpec((1, CB), lambda j, i: (0, j)),
                  pl.BlockSpec(memory_space=pl.ANY)],
        out_specs=[pl.BlockSpec((tb, CB), lambda j, i: (i, XBC_BLK0 + j)), pl.BlockSpec((8, CB), lambda j, i: (0, j)),
                   pl.BlockSpec((1, CB), lambda j, i: (0, j))],
        out_shape=[_sds((t, PROJ), bf16), _sds((8, XBC), f32), _sds((1, XBC), f32)],
        input_output_aliases={7: 0},
        scratch_shapes=[pltpu.VMEM((tb + 2 * HALO, CB), f32), pltpu.VMEM((tb + HALO, CB), f32),
                        pltpu.VMEM((40, CB), f32)],
        compiler_params=_cparams(2))(proj, proj, proj, dxs, dxs, ws, bs, dproj)


def place_columns(buf, part, col_block, tb, name):
    t, wdt = part.shape

    def body(p_ref, buf_ref, o_ref):
        o_ref[...] = p_ref[...]

    return pl.pallas_call(
        body, name=name, grid=(t // tb,),
        in_specs=[pl.BlockSpec((tb, wdt), lambda i: (i, 0)), pl.BlockSpec(memory_space=pl.ANY)],
        out_specs=pl.BlockSpec((tb, wdt), lambda i: (i, col_block)), out_shape=_sds(buf.shape, buf.dtype),
        input_output_aliases={1: 0}, compiler_params=_cparams(1))(part, buf)


GW = D // NG
EXPAND_TERMS = 2
REDUCE_TERMS = 1


def _ssd_consts():
    head_of_lane = jnp.arange(D) // HP
    expand = (jnp.arange(CH)[:, None] == head_of_lane[None, :]).astype(bf16)
    tri = (jnp.arange(CH)[:, None] >= jnp.arange(CH)[None, :]).astype(f32)
    return expand, tri


def _ssd_common(par_ref, dtr_ref, e_ref, tri_ref):
    par = par_ref[...]
    dtb, alog, dsk = par[0:1], par[1:2], par[2:3]
    lane = lax.broadcasted_iota(jnp.int32, (CH, CH), 1)
    a = -jnp.exp(alog)
    dtr = dtr_ref[...] + dtb
    sp = jnp.maximum(dtr, 0.0) + jnp.log(1.0 + jnp.exp(-jnp.abs(dtr)))
    dt = jnp.where(lane < NH, sp, 0.0)
    cs = jnp.dot(tri_ref[...], dt * a, precision=lax.Precision.HIGHEST, preferred_element_type=f32)
    cs_last = cs[CH - 1:CH, :]
    dte = jnp.exp(cs_last - cs)
    ecs = jnp.exp(cs)
    ecl = jnp.exp(cs_last)
    e = e_ref[...]
    row8 = lax.broadcasted_iota(jnp.int32, (8, CH), 0)
    r8 = _split_dot(jnp.where(row8 == 0, ecl, jnp.where(row8 == 1, dsk, 0.0)), e, 3)
    return dict(a=a, dtr=dtr, dt=dt, cs=cs, cst=cs.T, dte=dte, ecs=ecs, ecl=ecl, e=e, lane=lane,
                dt_x=_split_dot(dt, e, EXPAND_TERMS), dte_x=_split_dot(dte, e, EXPAND_TERMS),
                ecs_x=_split_dot(ecs, e, EXPAND_TERMS),
                ecl_x=r8[0:1], dsk_x=r8[1:2])


def _decay_matrix(c, h):
    li = lax.broadcasted_iota(jnp.int32, (CH, CH), 0)
    seg = c["cs"][:, h:h + 1] - c["cst"][h:h + 1, :]
    return jnp.exp(jnp.where(li >= c["lane"], seg, -jnp.inf))


def _gate_norm_fwd(y, z, gs):
    zg = z * _sigmoid(z)
    yg = y * zg
    return jnp.concatenate([_rms_fwd(yg[:, k * GW:(k + 1) * GW], gs[:, k * GW:(k + 1) * GW]) for k in range(NG)], axis=1)


def ssd_fwd(xbcs, proj, par, gs, cat, seq, name):
    t = xbcs.shape[0]
    nc = seq // CH
    expand, tri = _ssd_consts()

    def body(xs_ref, b_ref, c_ref, dtr_ref, z_ref, par_ref, e_ref, tri_ref, gs_ref, cat_ref, yn_ref, y_ref, st_ref,
             p_scr, yd_scr):
        @pl.when(pl.program_id(0) % nc == 0)
        def _():
            p_scr[...] = jnp.zeros_like(p_scr)

        c = _ssd_common(par_ref, dtr_ref, e_ref, tri_ref)
        xs = xs_ref[...]
        xdt = xs * c["dt_x"]
        xdt_b = xdt.astype(bf16)
        xdte_b = (xdt * c["dte_x"]).astype(bf16)
        p = p_scr[...]
        st_ref[0] = p
        p_b = p.astype(bf16)
        lo = c["lane"] < HP
        for g in range(NG):
            bg = b_ref[:, g * NS:(g + 1) * NS].astype(bf16)
            cg = c_ref[:, g * NS:(g + 1) * NS].astype(bf16)
            gmat = _dot_nt(cg, bg)
            for q in range(GW // CH):
                col = g * GW + q * CH
                xp = xdt_b[:, col:col + CH]
                h0 = col // HP
                m0 = (gmat * _decay_matrix(c, h0)).astype(bf16)
                m1 = (gmat * _decay_matrix(c, h0 + 1)).astype(bf16)
                stacked = jnp.concatenate([jnp.where(lo, xp, jnp.zeros_like(xp)),
                                           jnp.where(lo, jnp.zeros_like(xp), xp)], axis=0)
                yd_scr[:, col:col + CH] = _dot(jnp.concatenate([m0, m1], axis=1), stacked)
            gsl = slice(g * GW, (g + 1) * GW)
            yoff = _dot(cg, p_b[:, gsl]) * c["ecs_x"][:, gsl]
            yd_scr[:, gsl] = yd_scr[:, gsl] + yoff
            p_scr[:, gsl] = p[:, gsl] * c["ecl_x"][:, gsl] + _dot_tn(bg, xdte_b[:, gsl])
        y = yd_scr[...] + c["dsk_x"] * xs
        y_ref[...] = y
        yn_ref[...] = _gate_norm_fwd(y, z_ref[...], gs_ref[...]).astype(bf16)

    nb = t // CH
    return pl.pallas_call(
        body, name=name, grid=(nb,),
        in_specs=[pl.BlockSpec((CH, D), lambda i: (i, 0)),
                  pl.BlockSpec((CH, NG * NS), lambda i: (i, D // (NG * NS))),
                  pl.BlockSpec((CH, NG * NS), lambda i: (i, D // (NG * NS) + 1)),
                  pl.BlockSpec((CH, CH), lambda i: (i, COL_DT // CH)),
                  pl.BlockSpec((CH, D), lambda i: (i, COL_Z // D)),
                  pl.BlockSpec((8, CH), lambda i: (0, 0)), pl.BlockSpec((CH, D), lambda i: (0, 0)),
                  pl.BlockSpec((CH, CH), lambda i: (0, 0)), pl.BlockSpec((1, D), lambda i: (0, 0)),
                  pl.BlockSpec(memory_space=pl.ANY)],
        out_specs=[pl.BlockSpec((CH, D), lambda i: (i, 1)), pl.BlockSpec((CH, D), lambda i: (i, 0)),
                   pl.BlockSpec((1, NS, D), lambda i: (i, 0, 0))],
        out_shape=[_sds((t, 2 * D), bf16), _sds((t, D), f32), _sds((nb, NS, D), f32)],
        input_output_aliases={9: 0},
        scratch_shapes=[pltpu.VMEM((NS, D), f32), pltpu.VMEM((CH, D), f32)],
        compiler_params=_cparams(1))(xbcs, xbcs, xbcs, proj, proj, par, expand, tri, gs, cat)


def ssd_bwd(xbcs, proj, ypre, states, dcat, par, gs, dproj, seq, name):
    t = xbcs.shape[0]
    nc = seq // CH
    expand, tri = _ssd_consts()

    def body(xs_ref, b_ref, c_ref, dtr_ref, z_ref, y_ref, st_ref, dyn_ref, par_ref, e_ref, tri_ref, gs_ref, dproj_ref,
             dx_ref, dz_ref, ddt_ref, dpar_ref, dgs_ref, dp_scr, dxdt_scr):
        i = pl.program_id(0)

        @pl.when(i % nc == 0)
        def _():
            dp_scr[...] = jnp.zeros_like(dp_scr)

        c = _ssd_common(par_ref, dtr_ref, e_ref, tri_ref)
        e = c["e"]
        lane = c["lane"]
        sub = lax.broadcasted_iota(jnp.int32, (CH, CH), 0)
        xs = xs_ref[...]
        xdt = xs * c["dt_x"]
        xdt_b = xdt.astype(bf16)
        xdte_b = (xdt * c["dte_x"]).astype(bf16)
        p = st_ref[0]
        p_b = p.astype(bf16)
        dpn = dp_scr[...]
        dpn_b = dpn.astype(bf16)

        y, z, gs_v = y_ref[...], z_ref[...], gs_ref[...]
        zs = _sigmoid(z)
        zg = z * zs
        yg = y * zg
        parts, gparts = [], []
        for k in range(NG):
            sl = slice(k * GW, (k + 1) * GW)
            dxk, dgk = _rms_bwd(yg[:, sl], gs_v[:, sl], dyn_ref[:, sl])
            parts.append(dxk)
            gparts.append(dgk)
        dyg = jnp.concatenate(parts, axis=1)
        dgs_rows = jnp.concatenate(gparts, axis=1)
        dy = dyg * zg
        dz_ref[...] = (dyg * y * (zs * (1.0 + z * (1.0 - zs)))).astype(bf16)
        dy_b = dy.astype(bf16)
        dq_b = (dy * c["ecs_x"]).astype(bf16)

        lo = lane < HP
        dcs = jnp.zeros((CH, CH), f32)
        dcst = jnp.zeros((CH, CH), f32)
        for g in range(NG):
            gsl = slice(g * GW, (g + 1) * GW)
            bg = b_ref[:, g * NS:(g + 1) * NS].astype(bf16)
            cg = c_ref[:, g * NS:(g + 1) * NS].astype(bf16)
            gmat = _dot_nt(cg, bg)
            dgm = jnp.zeros((CH, CH), f32)
            for q in range(GW // CH):
                col = g * GW + q * CH
                xp = xdt_b[:, col:col + CH]
                dyp = dy_b[:, col:col + CH]
                zero = jnp.zeros_like(dyp)
                xp2 = jnp.concatenate([jnp.where(lo, xp, zero), jnp.where(lo, zero, xp)], axis=0)
                dy2 = jnp.concatenate([jnp.where(lo, dyp, zero), jnp.where(lo, zero, dyp)], axis=0)
                dm2 = _dot_nt(dyp, xp2)
                ms = []
                for hh in range(2):
                    h = col // HP + hh
                    dec = _decay_matrix(c, h)
                    m = gmat * dec
                    dm = dm2[:, hh * CH:(hh + 1) * CH]
                    dseg = dm * m
                    dcs = dcs + jnp.where(lane == h, jnp.sum(dseg, axis=1, keepdims=True), 0.0)
                    dcst = dcst + jnp.where(sub == h, jnp.sum(dseg, axis=0, keepdims=True), 0.0)
                    dgm = dgm + dm * dec
                    ms.append(m.astype(bf16))
                dxdt_scr[:, col:col + CH] = _dot_tn(jnp.concatenate(ms, axis=0), dy2)
            dgm_b = dgm.astype(bf16)
            bds = _dot(bg, dpn_b[:, gsl])
            dxdt_scr[:, gsl] = dxdt_scr[:, gsl] + c["dte_x"][:, gsl] * bds
            dc_g = _dot(dgm_b, bg) + _dot_nt(dq_b[:, gsl], p_b[:, gsl])
            db_g = _dot_tn(dgm_b, cg) + _dot_nt(xdte_b[:, gsl], dpn_b[:, gsl])
            dx_ref[:, D + g * NS:D + (g + 1) * NS] = db_g
            dx_ref[:, D + NG * NS + g * NS:D + NG * NS + (g + 1) * NS] = dc_g
            dp_scr[:, gsl] = dpn[:, gsl] * c["ecl_x"][:, gsl] + _dot_tn(cg, dq_b[:, gsl])
            q_g = _dot(cg, p_b[:, gsl])
            e_g = e[:, gsl]
            dcs = dcs + c["ecs"] * _split_dot(dy[:, gsl] * q_g, e_g, REDUCE_TERMS, nt=True)
            ddte = _split_dot(xdt[:, gsl] * bds, e_g, REDUCE_TERMS, nt=True) * c["dte"]
            dcs = dcs - ddte
            dcs = dcs + jnp.where(sub == CH - 1, jnp.sum(ddte, axis=0, keepdims=True), 0.0)

        decl = _split_dot(jnp.broadcast_to(jnp.sum(dpn * p, axis=0, keepdims=True), (8, D)), e, 2, nt=True)[0:1]
        dcs = dcs + jnp.where(sub == CH - 1, c["ecl"] * decl, 0.0)
        dcs = dcs - dcst.T
        dadt = lax.dot_general(tri_ref[...], dcs, (((0,), (0,)), ((), ())), precision=lax.Precision.HIGHEST,
                               preferred_element_type=f32)
        dxdt = dxdt_scr[...]
        ddt = dadt * c["a"] + _split_dot(dxdt * xs, e, REDUCE_TERMS, nt=True)
        ddtr = jnp.where(lane < NH, ddt * _sigmoid(c["dtr"]), 0.0)
        ddt_ref[...] = ddtr.astype(bf16)
        dx_ref[:, 0:D] = dxdt * c["dt_x"] + c["dsk_x"] * dy
        dsk = _split_dot(jnp.broadcast_to(jnp.sum(dy * xs, axis=0, keepdims=True), (8, D)), e, 2, nt=True)[0:1]
        dalog = jnp.sum(dadt * c["dt"], axis=0, keepdims=True) * c["a"]
        row8 = lax.broadcasted_iota(jnp.int32, (8, CH), 0)
        dpar = jnp.where(row8 == 0, jnp.sum(ddtr, axis=0, keepdims=True),
                         jnp.where(row8 == 1, dalog, jnp.where(row8 == 2, dsk, 0.0)))
        dpar = jnp.where(lax.broadcasted_iota(jnp.int32, (8, CH), 1) < NH, dpar, 0.0)
        _accum(dpar_ref, dpar, i == 0)
        _accum(dgs_ref, jnp.sum(dgs_rows, axis=0, keepdims=True), i == 0)

    nb = t // CH
    rev = lambda i: (i // nc) * nc + (nc - 1 - i % nc)
    return pl.pallas_call(
        body, name=name, grid=(nb,),
        in_specs=[pl.BlockSpec((CH, D), lambda i: (rev(i), 0)),
                  pl.BlockSpec((CH, NG * NS), lambda i: (rev(i), D // (NG * NS))),
                  pl.BlockSpec((CH, NG * NS), lambda i: (rev(i), D // (NG * NS) + 1)),
                  pl.BlockSpec((CH, CH), lambda i: (rev(i), COL_DT // CH)),
                  pl.BlockSpec((CH, D), lambda i: (rev(i), COL_Z // D)),
                  pl.BlockSpec((CH, D), lambda i: (rev(i), 0)),
                  pl.BlockSpec((1, NS, D), lambda i: (rev(i), 0, 0)),
                  pl.BlockSpec((CH, D), lambda i: (rev(i), 1)),
                  pl.BlockSpec((8, CH), lambda i: (0, 0)), pl.BlockSpec((CH, D), lambda i: (0, 0)),
                  pl.BlockSpec((CH, CH), lambda i: (0, 0)), pl.BlockSpec((1, D), lambda i: (0, 0)),
                  pl.BlockSpec(memory_space=pl.ANY)],
        out_specs=[pl.BlockSpec((CH, XBC), lambda i: (rev(i), 0)), pl.BlockSpec((CH, D), lambda i: (rev(i), COL_Z // D)),
                   pl.BlockSpec((CH, CH), lambda i: (rev(i), 0)),
                   pl.BlockSpec((8, CH), lambda i: (0, 0)), pl.BlockSpec((1, D), lambda i: (0, 0))],
        out_shape=[_sds((t, XBC), f32), _sds((t, PROJ), bf16), _sds((t, CH), bf16), _sds((8, CH), f32), _sds((1, D), f32)],
        input_output_aliases={12: 1},
        scratch_shapes=[pltpu.VMEM((NS, D), f32), pltpu.VMEM((CH, D), f32)],
        compiler_params=_cparams(1))(xbcs, xbcs, xbcs, proj, proj, ypre, states, dcat, par, expand, tri, gs, dproj)


def loss_head(y, target, tb, name):
    t = y.shape[0]

    def body(y_ref, t_ref, s_ref, dy_ref):
        err = y_ref[...] - t_ref[...]
        dy_ref[...] = err * (1.0 / D)
        _accum(s_ref, jnp.zeros((8, CH), f32) + jnp.sum(err * err), pl.program_id(0) == 0)

    return pl.pallas_call(
        body, name=name, grid=(t // tb,),
        in_specs=[pl.BlockSpec((tb, D), lambda i: (i, 0)), pl.BlockSpec((tb, D), lambda i: (i, 0))],
        out_specs=[pl.BlockSpec((8, CH), lambda i: (0, 0)), pl.BlockSpec((tb, D), lambda i: (i, 0))],
        out_shape=[_sds((8, CH), f32), _sds((t, D), f32)],
        compiler_params=_cparams(1))(y, target)


def _tiles(t, seq):
    tm = min(512, t)
    return dict(tm=tm, tm_small=min(256, t), tm_large=min(1024, t), tb=min(512, seq))


def local_step(x, target, depth, weights_of, seq, grads_done=None):
    t = x.shape[0]
    ts = _tiles(t, seq)
    tm, tl, tb = ts["tm"], ts["tm_large"], ts["tb"]
    saved, ws = [], []
    for l in range(depth):
        w = weights_of(l, x)
        ws.append(w)
        proj, h1 = norm_matmul(x, w["g1"], w["win"], tl, 1152, f32, "in_proj", token=w.get("token"))
        cat = group_a_fwd(proj, w["wa"], w["ga"], seq, tb, "group_a_fwd")
        xbcs = conv_b_fwd(proj, w["ws"], w["bs"], seq, tb, "conv_b_fwd")
        cat, ypre, states = ssd_fwd(xbcs, proj, w["par"], w["gs"], cat, seq, "ssd_fwd")
        if "late" in w:
            w.update(w.pop("late")(cat))
        mix, x2 = matmul_postnorm(cat, w["wo"], x, w["g2"], tl, False, "out_proj")
        fp, h2 = norm_matmul(x2, w["g3"], w["wu"], tl, 1024, bf16, "mlp_up")
        o, x3 = matmul_postnorm(fp, w["wd"], x2, w["g4"], tm, True, "mlp_down")
        saved.append(dict(x=x, proj=proj, h1=h1, xbcs=xbcs, ypre=ypre, states=states, cat=cat, mix=mix, x2=x2,
                          fp=fp, h2=h2, o=o))
        x = x3
    sse, dx = loss_head(x, target, tm, "loss_head")
    grads = [None] * depth
    for l in reversed(range(depth)):
        s, w = saved[l], ws[l]
        do, dg4, dfp = postnorm_bwd_matmul(s["o"], w["g4"], dx, w["wd"], s["fp"], tl, 1024, bf16, "mlp_down_bwd")
        dwd = matmul_tn(s["fp"], do, 512, 1024, True, "mlp_down_dw")
        dx2, dg3 = matmul_prenorm_bwd(dfp, w["wu"], s["x2"], w["g3"], dx, tm, "mlp_up_bwd")
        dwu = matmul_tn(s["h2"], dfp, 512, 1024, False, "mlp_up_dw", col_blocks=True)
        dmix, dg2, dcat = postnorm_bwd_matmul(s["mix"], w["g2"], dx2, w["wo"], None, tl, 1024, f32, "out_proj_bwd")
        dwo = matmul_tn(s["cat"], dmix, 512, 1024, False, "out_proj_dw")
        token = None if grads_done is None else grads_done(l, dict(wo=dwo, wu=dwu, wd=dwd), False)
        dproj, dwa, dga = group_a_bwd(s["proj"], dcat, w["wa"], w["ga"], seq, tb, "group_a_bwd", token=token)
        dxbcs, dproj, ddt, dpar, dgs = ssd_bwd(s["xbcs"], s["proj"], s["ypre"], s["states"], dcat, w["par"], w["gs"],
                                               dproj, seq, "ssd_bwd")
        dproj, dws, dbs = conv_b_bwd(s["proj"], dxbcs, w["ws"], w["bs"], dproj, seq, tb, "conv_b_bwd")
        dproj = place_columns(dproj, ddt, COL_DT // CH, tm, "place_ddt")
        dwin = matmul_tn(s["h1"], dproj, 512, 1152, False, "in_proj_dw")
        token = None if grads_done is None else grads_done(l, dict(win=dwin), True)
        dx, dg1 = matmul_prenorm_bwd(dproj, w["win"], s["x"], w["g1"], dx2, ts["tm_small"], "in_proj_bwd", token=token)
        grads[l] = dict(win=dwin, wo=dwo, wu=dwu, wd=dwd, wa=dwa, ws=dws, bs=dbs, par=dpar,
                        g1=dg1, ga=dga, gs=dgs, g2=dg2, g3=dg3, g4=dg4)
    return sse, dx, grads


GROUPS = {
    "chips": [(1, 0, 0), (0, 1, 0), (1, 1, 0)],
    "pair": [(0, 0, 1)],
    "all": [(1, 0, 0), (0, 1, 0), (1, 1, 0), (0, 0, 1), (1, 0, 1), (0, 1, 1), (1, 1, 1)],
}


def _group_index(group, x, y, c):
    return {"chips": 2 * x + y, "pair": c, "all": 4 * x + 2 * y + c}[group]


def _chunk_indices(shape, pieces):
    if len(shape) < 3:
        return [()]
    lead = [()]
    for n in shape[:-2]:
        lead = [i + (k,) for i in lead for k in range(n)]
    rows = shape[-2]
    split = max(1, pieces // len(lead))
    while split > 1 and (rows % split or (rows // split) % 16):
        split -= 1
    step = rows // split
    return [i + (pl.ds(s * step, step),) for i in lead for s in range(split)]


def _exchange(arrays, out_shapes, group, src_view, dst_view, view_shape, name, own, pieces=16):
    masks = GROUPS[group]
    na, nm = len(arrays), len(masks)
    cuts = [_chunk_indices(view_shape(a), pieces) for a in range(na)]

    def body(*refs):
        ins, outs = refs[:na], refs[na:2 * na]
        send_sems, recv_sems = refs[2 * na:2 * na + 2]
        local_sems = refs[2 * na + 2] if own else None
        x, y, c = lax.axis_index("x"), lax.axis_index("y"), lax.axis_index("c")
        me = _group_index(group, x, y, c)
        peers = []
        for mx, my, mc in masks:
            px, py, pc = (1 - x if mx else x), (1 - y if my else y), (1 - c if mc else c)
            peers.append(((px, py, pc), _group_index(group, px, py, pc)))

        def part(ref, idx):
            return ref.at[idx] if idx else ref

        if own:
            for a in range(na):
                for idx in cuts[a]:
                    pltpu.make_async_copy(part(src_view(ins[a], a, me), idx), part(dst_view(outs[a], a, me), idx),
                                          local_sems.at[a]).start()
        for a in range(na):
            for j, (dev, pidx) in enumerate(peers):
                for idx in cuts[a]:
                    pltpu.make_async_remote_copy(
                        src_ref=part(src_view(ins[a], a, pidx), idx), dst_ref=part(dst_view(outs[a], a, me), idx),
                        send_sem=send_sems.at[a * nm + j], recv_sem=recv_sems.at[a * nm + j],
                        device_id=dev, device_id_type=MESH).start()
        whole = []
        for a in range(na):
            for j, (dev, pidx) in enumerate(peers):
                whole.append(pltpu.make_async_remote_copy(
                    src_ref=src_view(ins[a], a, pidx), dst_ref=dst_view(outs[a], a, pidx),
                    send_sem=send_sems.at[a * nm + j], recv_sem=recv_sems.at[a * nm + j],
                    device_id=dev, device_id_type=MESH))
        for cp in whole:
            cp.wait_recv()
        for cp in whole:
            cp.wait_send()
        if own:
            for a in range(na):
                pltpu.make_async_copy(src_view(ins[a], a, me), dst_view(outs[a], a, me), local_sems.at[a]).wait()

    hbm = pl.BlockSpec(memory_space=pltpu.HBM)
    sems = [pltpu.SemaphoreType.DMA((na * nm,)), pltpu.SemaphoreType.DMA((na * nm,))]
    return pl.pallas_call(
        body, name=name, in_specs=[hbm] * na, out_specs=[hbm] * na,
        out_shape=[_sds(s, a.dtype) for s, a in zip(out_shapes, arrays)],
        scratch_shapes=sems + ([pltpu.SemaphoreType.DMA((na,))] if own else []))(*arrays)


def all_gather(arrays, group, name, slot_axis=0, own=True):
    n = len(GROUPS[group]) + 1
    shapes = [a.shape[:slot_axis] + (n,) + a.shape[slot_axis:] for a in arrays]
    lead = (slice(None),) * slot_axis
    return _exchange(arrays, shapes, group, lambda r, a, i: r, lambda r, a, i: r.at[lead + (i,)],
                     lambda a: arrays[a].shape, name, own)


HBM_SPEC = pl.BlockSpec(memory_space=pltpu.HBM)
SEM_SPEC = pl.BlockSpec(memory_space=pltpu.SEMAPHORE)
DATAFLOW = pltpu.SideEffectType.DATAFLOW_SIDE_EFFECTING
N_CHIPS = 4


def _chip_peers(x, y, c):
    out = []
    for mx, my, _ in GROUPS["chips"]:
        px, py = (1 - x if mx else x), (1 - y if my else y)
        out.append(((px, py, c), 2 * px + py))
    return out


def _weight_views(shards):
    half = [s.shape[0] // 2 for s in shards]
    return dict(src=lambda ref, a, c, to_chip: ref.at[pl.ds(c * half[a], half[a])],
                dst=lambda ref, a, c, from_chip: ref.at[from_chip, pl.ds(c * half[a], half[a])],
                rows=lambda a: half[a])


def _grad_views(sums):
    return dict(src=lambda ref, a, c, to_chip: ref.at[to_chip], dst=lambda ref, a, c, from_chip: ref.at[from_chip],
                rows=lambda a: sums[a].shape[1])


def chips_start(sources, zones, views, name, pieces=4, after=None):
    na, nm = len(sources), N_CHIPS - 1

    def body(*refs):
        ins, lands = refs[:na], refs[na:2 * na]
        n_in = 2 * na + len(_token_arg(after))
        send_sems, recv_sems, token = refs[n_in], refs[n_in + 1], refs[-1]
        x, y, c = lax.axis_index("x"), lax.axis_index("y"), lax.axis_index("c")
        chip = 2 * x + y
        for a in range(na):
            step = views["rows"](a) // pieces
            for j, (dev, to_chip) in enumerate(_chip_peers(x, y, c)):
                for q in range(pieces):
                    rows = pl.ds(q * step, step)
                    pltpu.make_async_remote_copy(
                        src_ref=views["src"](ins[a], a, c, to_chip).at[rows],
                        dst_ref=views["dst"](lands[a], a, c, chip).at[rows],
                        send_sem=send_sems.at[a * nm + j], recv_sem=recv_sems.at[a * nm + j],
                        device_id=dev, device_id_type=MESH).start()
        token[...] = jnp.zeros_like(token)

    both = list(sources) + list(zones)
    outs = pl.pallas_call(
        body, name=name,
        out_shape=(pltpu.SemaphoreType.DMA((na * nm,)), pltpu.SemaphoreType.DMA((na * nm,)),
                   *[pltpu.HBM(b.shape, b.dtype) for b in both], _sds((8, CH), f32)),
        in_specs=[HBM_SPEC] * (2 * na) + _token_spec(after),
        out_specs=(SEM_SPEC, SEM_SPEC, *[HBM_SPEC] * (2 * na), pl.BlockSpec(memory_space=pltpu.VMEM)),
        input_output_aliases={i: 2 + i for i in range(2 * na)},
        compiler_params=pltpu.CompilerParams(has_side_effects=DATAFLOW))(
            *[pltpu.with_memory_space_constraint(b, pltpu.HBM) for b in both], *_token_arg(after))
    return dict(send=outs[0], recv=outs[1], sources=list(outs[2:2 + na]), zones=list(outs[2 + na:2 + 2 * na]),
                token=outs[-1], views=views)


def chips_wait(started, after, name):
    sources, zones, views = started["sources"], started["zones"], started["views"]
    na, nm = len(sources), N_CHIPS - 1

    def body(*refs):
        ins, lands = refs[:na], refs[na:2 * na]
        send_sems, recv_sems = refs[2 * na], refs[2 * na + 1]
        x, y, c = lax.axis_index("x"), lax.axis_index("y"), lax.axis_index("c")
        for a in range(na):
            for j, (dev, peer_chip) in enumerate(_chip_peers(x, y, c)):
                cp = pltpu.make_async_remote_copy(
                    src_ref=views["src"](ins[a], a, c, peer_chip), dst_ref=views["dst"](lands[a], a, c, peer_chip),
                    send_sem=send_sems.at[a * nm + j], recv_sem=recv_sems.at[a * nm + j],
                    device_id=dev, device_id_type=MESH)
                cp.wait_send()
                cp.wait_recv()

    both = list(sources) + list(zones)
    outs = pl.pallas_call(
        body, name=name, out_shape=tuple(pltpu.HBM(b.shape, b.dtype) for b in both),
        in_specs=[HBM_SPEC] * (2 * na) + [SEM_SPEC, SEM_SPEC, pl.BlockSpec(memory_space=pl.ANY)],
        out_specs=tuple([HBM_SPEC] * (2 * na)), input_output_aliases={i: i for i in range(2 * na)},
        compiler_params=pltpu.CompilerParams(has_side_effects=DATAFLOW))(*both, started["send"], started["recv"], after)
    return list(outs[:na]), list(outs[na:])


def weights_share(zones, name):
    na, nm = len(zones), N_CHIPS - 1

    def body(*refs):
        lands = refs[na:2 * na]
        send_sems, recv_sems = refs[2 * na:]
        x, y, c = lax.axis_index("x"), lax.axis_index("y"), lax.axis_index("c")
        chip = 2 * x + y
        sibling = (x, y, 1 - c)
        sends = []
        for a in range(na):
            half = zones[a].shape[1] // 2
            for m in range(1, N_CHIPS):
                mine = lands[a].at[chip ^ m, pl.ds(c * half, half)]
                sends.append(pltpu.make_async_remote_copy(
                    src_ref=mine, dst_ref=mine, send_sem=send_sems.at[a * nm + m - 1],
                    recv_sem=recv_sems.at[a * nm + m - 1], device_id=sibling, device_id_type=MESH))
        for cp in sends:
            cp.start()
        for a in range(na):
            half = zones[a].shape[1] // 2
            for m in range(1, N_CHIPS):
                theirs = lands[a].at[chip ^ m, pl.ds((1 - c) * half, half)]
                pltpu.make_async_remote_copy(
                    src_ref=theirs, dst_ref=theirs, send_sem=send_sems.at[a * nm + m - 1],
                    recv_sem=recv_sems.at[a * nm + m - 1], device_id=sibling, device_id_type=MESH).wait_recv()
        for cp in sends:
            cp.wait_send()

    return pl.pallas_call(
        body, name=name, in_specs=[HBM_SPEC] * na, out_specs=[HBM_SPEC] * na,
        out_shape=[_sds(z.shape, z.dtype) for z in zones], input_output_aliases={i: i for i in range(na)},
        scratch_shapes=[pltpu.SemaphoreType.DMA((na * nm,)), pltpu.SemaphoreType.DMA((na * nm,))])(*zones)


def pair_send_halves(grads, name):
    half = [g.shape[1] // 2 for g in grads]
    shapes = [(g.shape[0], h, g.shape[2]) for g, h in zip(grads, half)]
    return _exchange(grads, shapes, "pair", lambda r, a, i: r.at[:, pl.ds(i * half[a], half[a])],
                     lambda r, a, i: r, lambda a: shapes[a], name, False)


def sum_pair_half(g, recv, core, name, tb=256):
    nk, r, c = g.shape
    tb = min(tb, r // 2)
    nb = r // 2 // tb

    def body(core_ref, g_ref, r_ref, o_ref):
        o_ref[...] = (g_ref[...] + r_ref[...]).astype(bf16)

    return pl.pallas_call(
        body, name=name,
        grid_spec=pltpu.PrefetchScalarGridSpec(
            num_scalar_prefetch=1, grid=(nk, nb),
            in_specs=[pl.BlockSpec((None, tb, c), lambda k, i, core_ref: (k, core_ref[0] * nb + i, 0)),
                      pl.BlockSpec((None, tb, c), lambda k, i, core_ref: (k, i, 0))],
            out_specs=pl.BlockSpec((None, tb, c), lambda k, i, core_ref: (k, i, 0))),
        out_shape=_sds((nk, r // 2, c), bf16), compiler_params=_cparams(2))(
            jnp.reshape(core, (1,)).astype(jnp.int32), g, recv)


def chip_sum_into(acc, layer, own, others, chip, name, tb=256):
    n, r, c = own.shape
    tb = min(tb, r)

    def body(chip_ref, x_ref, y1_ref, y2_ref, y3_ref, acc_ref, o_ref):
        o_ref[...] = ((x_ref[...].astype(f32) + y1_ref[...].astype(f32)) + y2_ref[...].astype(f32)) + y3_ref[...].astype(f32)

    def slot(k):
        return pl.BlockSpec((None, tb, c), lambda i, chip_ref: (chip_ref[0] ^ k, i, 0))

    return pl.pallas_call(
        body, name=name,
        grid_spec=pltpu.PrefetchScalarGridSpec(
            num_scalar_prefetch=1, grid=(r // tb,),
            in_specs=[slot(k) for k in range(n)] + [pl.BlockSpec(memory_space=pl.ANY)],
            out_specs=pl.BlockSpec((None, tb, c), lambda i, chip_ref: (layer, i, 0))),
        out_shape=_sds(acc.shape, f32), input_output_aliases={n + 1: 0}, compiler_params=_cparams(1))(
            jnp.reshape(chip, (1,)).astype(jnp.int32), own, *([others] * (n - 1)), acc)


def adamw_halves(w, g_own, g_recv, m, v, core, name, tb=256):
    depth, r, c = w.shape
    tb = min(tb, r // 2)
    nb = r // 2 // tb

    def body(core_ref, w_ref, go_ref, gr_ref, m_ref, v_ref, g_ref, d_ref, mo_ref, vo_ref):
        gv = jnp.where(pl.program_id(1) == core_ref[0], go_ref[...], gr_ref[...])
        m2 = B1 * m_ref[...] + (1.0 - B1) * gv
        v2 = B2 * v_ref[...] + (1.0 - B2) * (gv * gv)
        m_hat = m2 / (1.0 - B1 ** STEP)
        v_hat = v2 / (1.0 - B2 ** STEP)
        g_ref[...] = gv
        d_ref[...] = -LR * (m_hat / (jnp.sqrt(v_hat) + AEPS) + WD * w_ref[...])
        mo_ref[...] = m2
        vo_ref[...] = v2

    whole = pl.BlockSpec((None, tb, c), lambda l, h, i, core_ref: (l, h * nb + i, 0))
    part = pl.BlockSpec((None, tb, c), lambda l, h, i, core_ref: (l, i, 0))
    return pl.pallas_call(
        body, name=name,
        grid_spec=pltpu.PrefetchScalarGridSpec(num_scalar_prefetch=1, grid=(depth, 2, nb),
                                               in_specs=[whole, part, part, whole, whole], out_specs=[whole] * 4),
        out_shape=[_sds(w.shape, f32)] * 4, compiler_params=_cparams(3))(
            jnp.reshape(core, (1,)).astype(jnp.int32), w, g_own, g_recv, m, v)


def sum_slots(y, out_dtype, name, tb=256):
    n, r, c = y.shape
    tb = min(tb, r)

    def body(y_ref, o_ref):
        acc = y_ref[0].astype(f32)
        for i in range(1, n):
            acc = acc + y_ref[i].astype(f32)
        o_ref[...] = acc.astype(out_dtype)

    return pl.pallas_call(
        body, name=name, grid=(r // tb,),
        in_specs=[pl.BlockSpec((n, tb, c), lambda i: (0, i, 0))], out_specs=pl.BlockSpec((tb, c), lambda i: (i, 0)),
        out_shape=_sds((r, c), out_dtype), compiler_params=_cparams(1))(y)


def adamw(w, g, m, v, name, tb=256):
    r, c = w.shape
    tb = min(tb, r)

    def body(w_ref, g_ref, m_ref, v_ref, d_ref, mo_ref, vo_ref):
        gv = g_ref[...]
        m2 = B1 * m_ref[...] + (1.0 - B1) * gv
        v2 = B2 * v_ref[...] + (1.0 - B2) * (gv * gv)
        m_hat = m2 / (1.0 - B1 ** STEP)
        v_hat = v2 / (1.0 - B2 ** STEP)
        d_ref[...] = -LR * (m_hat / (jnp.sqrt(v_hat) + AEPS) + WD * w_ref[...])
        mo_ref[...] = m2
        vo_ref[...] = v2

    spec = pl.BlockSpec((tb, c), lambda i: (i, 0))
    return pl.pallas_call(
        body, name=name, grid=(r // tb,), in_specs=[spec] * 4, out_specs=[spec] * 3,
        out_shape=[_sds((r, c), f32)] * 3, compiler_params=_cparams(1))(w, g, m, v)


def adamw_leading(w, g, m, v, name, tc=64):
    c, l, r = w.shape
    main = c // tc
    tail = c - main * tc

    def body(w_ref, g_ref, m_ref, v_ref, *rest):
        d_ref, mo_ref, vo_ref = rest[-3:]
        gv = g_ref[...]
        m2 = B1 * m_ref[...] + (1.0 - B1) * gv
        v2 = B2 * v_ref[...] + (1.0 - B2) * (gv * gv)
        m_hat = m2 / (1.0 - B1 ** STEP)
        v_hat = v2 / (1.0 - B2 ** STEP)
        d_ref[...] = -LR * (m_hat / (jnp.sqrt(v_hat) + AEPS) + WD * w_ref[...])
        mo_ref[...] = m2
        vo_ref[...] = v2

    spec = pl.BlockSpec((tc, l, r), lambda i: (i, 0, 0))
    outs = pl.pallas_call(
        functools.partial(body), name=name, grid=(main,), in_specs=[spec] * 4, out_specs=[spec] * 3,
        out_shape=[_sds(w.shape, f32)] * 3, compiler_params=_cparams(1))(w, g, m, v)
    if tail:
        assert (main * tc) % tail == 0
        last = pl.BlockSpec((tail, l, r), lambda i: (main * tc // tail, 0, 0))
        outs = pl.pallas_call(
            functools.partial(body), name=name + "_tail", grid=(1,),
            in_specs=[last] * 4 + [pl.BlockSpec(memory_space=pl.ANY)] * 3, out_specs=[last] * 3,
            out_shape=[_sds(w.shape, f32)] * 3, input_output_aliases={4: 0, 5: 1, 6: 2},
            compiler_params=_cparams(1))(w, g, m, v, *outs)
    return outs


SMALL_ROW = 1024
SMALL_GAINS = ("g1", "ga", "gs", "g2", "g3", "g4")
SMALL_LAYER_ROWS = 8 + 8 + 16 + 8


def _pack_small(grads):
    wide = lambda a: jnp.pad(a, ((0, 0), (0, 2 * SMALL_ROW - a.shape[1]))).reshape(-1, SMALL_ROW)
    row = lax.broadcasted_iota(jnp.int32, (8, SMALL_ROW), 0)
    parts = []
    for g in grads:
        singles = [g[k] for k in SMALL_GAINS] + [g["bs"][:, :SMALL_ROW],
                                                 jnp.pad(g["bs"][:, SMALL_ROW:], ((0, 0), (0, 2 * SMALL_ROW - XBC)))]
        first = sum(jnp.where(row == k, s, 0.0) for k, s in enumerate(singles))
        parts += [first, g["wa"], wide(g["ws"]), jnp.pad(g["par"], ((0, 0), (0, SMALL_ROW - CH)))]
    return jnp.concatenate(parts, axis=0)


def _unpack_small(packed, depth):
    rows = packed.reshape(depth, SMALL_LAYER_ROWS, SMALL_ROW)
    out = {k: rows[:, i] for i, k in enumerate(SMALL_GAINS)}
    out["bs"] = rows[:, 6:8].reshape(depth, 2 * SMALL_ROW)[:, :XBC]
    out["wa"] = rows[:, 8:11]
    out["ws"] = rows[:, 16:32].reshape(depth, 8, 2 * SMALL_ROW)[:, :4, :XBC]
    out["par"] = rows[:, 32:35, :CH]
    return out


def kernel(x, norm_mix_pre, w_in, conv_a_w, ssm_conv_w, ssm_conv_b, dt_bias, a_log, d_skip, conv_out_norm, ssm_out_norm, w_out, norm_mix_post, norm_mlp_pre, w_up, w_down, norm_mlp_post, loss_target, m_norm_mix_pre, m_w_in, m_conv_a_w, m_ssm_conv_w, m_ssm_conv_b, m_dt_bias, m_a_log, m_d_skip, m_conv_out_norm, m_ssm_out_norm, m_w_out, m_norm_mix_post, m_norm_mlp_pre, m_w_up, m_w_down, m_norm_mlp_post, v_norm_mix_pre, v_w_in, v_conv_a_w, v_ssm_conv_w, v_ssm_conv_b, v_dt_bias, v_a_log, v_d_skip, v_conv_out_norm, v_ssm_out_norm, v_w_out, v_norm_mix_post, v_norm_mlp_pre, v_w_up, v_w_down, v_norm_mlp_post):
    nb, seq, _ = x.shape
    t = nb * seq
    depth = w_in.shape[0]
    ncol = w_in.shape[2]
    chip = 2 * lax.axis_index("x") + lax.axis_index("y")

    taps = [conv_a_w, ssm_conv_w]
    taps_g = all_gather(taps, "chips", "gather_taps", slot_axis=1, own=False)
    wa_g, ws_g = [lax.dynamic_update_index_in_dim(g, s, chip, 1) for g, s in zip(taps_g, taps)]
    wa_full = jnp.transpose(wa_g, (0, 2, 1, 3)).reshape(depth, 3, D)
    ws_full = jnp.transpose(ws_g, (0, 2, 1, 3)).reshape(depth, 4, XBC)
    lane_pad = lambda a: jnp.pad(a, ((0, 0), (0, CH - a.shape[1])))
    par = jnp.stack([lane_pad(dt_bias), lane_pad(a_log), lane_pad(d_skip)], axis=1)
    par = jnp.pad(par, ((0, 0), (0, 5), (0, 0)))

    layer_shards = lambda l: [w_in[l].astype(bf16), w_out[l].astype(bf16), w_up[l].astype(bf16), w_down[l].astype(bf16)]
    issued = []

    def start(shards, name):
        zones = [lax.empty((N_CHIPS,) + s.shape, s.dtype) for s in shards]
        issued.append(chips_start(shards, zones, _weight_views(shards), name,
                                  after=issued[-1]["token"] if issued else None))
        return issued[-1]

    def finish(started, after, name):
        shards, zones = chips_wait(started, after, name)
        zones = weights_share(zones, "weights_share")
        return [lax.dynamic_update_index_in_dim(z, s, chip, 0) for z, s in zip(zones, shards)]

    def shaped(mats):
        wo_z, wu_z, wd_z = mats
        return wo_z.reshape(2 * D, D), wu_z, wd_z.reshape(DFF, D)

    first = layer_shards(0)
    travelling = {0: start(first[:1], "weights_start_0")}
    rest = start(first[1:], "weights_start_0_rest")
    for l in range(1, depth):
        travelling[l] = start(layer_shards(l), f"weights_start_{l}")

    def weights_of(l, x_in):
        mats = finish(travelling.pop(l), x_in, f"weights_wait_{l}")
        win_full = jnp.pad(jnp.transpose(mats[0], (1, 0, 2)).reshape(D, N_CHIPS * ncol),
                           ((0, 0), (0, PROJ - N_CHIPS * ncol)))
        w = dict(win=win_full, wa=jnp.pad(wa_full[l], ((0, 5), (0, 0))), ws=jnp.pad(ws_full[l], ((0, 4), (0, 0))),
                 bs=ssm_conv_b[l][None], par=par[l], g1=norm_mix_pre[l][None], ga=conv_out_norm[l][None],
                 gs=ssm_out_norm[l][None], g2=norm_mix_post[l][None], g3=norm_mlp_pre[l][None],
                 g4=norm_mlp_post[l][None])
        if l == 0:
            w["late"] = lambda after: dict(zip(("wo", "wu", "wd"), shaped(finish(rest, after, "weights_wait_0_rest"))))
        else:
            w.update(zip(("wo", "wu", "wd"), shaped(mats[1:])))
        return w

    core = lax.axis_index("c")
    grads_travelling = {}

    chip_major = dict(win=lambda a: a[None], wo=lambda a: a.reshape(N_CHIPS, 2 * D // N_CHIPS, D), wu=lambda a: a,
                      wd=lambda a: a.reshape(N_CHIPS, DFF // N_CHIPS, D))
    held = {}

    def grads_done(l, g, last):
        if l > 0 and not last:
            held[l] = g
            return None
        g = {**held.pop(l, {}), **g}
        keys = [k for k in ("win", "wo", "wu", "wd") if k in g]
        mats = [chip_major[k](g[k]) for k in keys]
        received = pair_send_halves(mats, "grads_to_pair")
        sums = [sum_pair_half(m_, r_, core, "pair_sum") for m_, r_ in zip(mats, received)]
        if "win" in keys:
            sums[0] = jnp.transpose(sums[0][0, :, :N_CHIPS * ncol].reshape(D // 2, N_CHIPS, ncol), (1, 0, 2))
        zones = [lax.empty(s.shape, s.dtype) for s in sums]
        started = chips_start(sums, zones, _grad_views(sums), f"grads_start_{l}_{len(grads_travelling)}")
        grads_travelling[(l, keys[0])] = (keys, started)
        return started["token"]

    sse, dx, grads = local_step(x.reshape(t, D), loss_target.reshape(t, D), depth, weights_of, seq, grads_done)
    loss = lax.psum(0.5 / D * sse[0, 0], ("x", "y", "c"))

    small_all = all_gather([_pack_small(grads)], "all", "gather_small")[0]
    small_sum = sum_slots(small_all, f32, "small_sum", tb=8)
    small = _unpack_small(small_sum, depth)

    big_w = dict(win=w_in, wo=w_out, wu=w_up, wd=w_down)
    acc = {k: lax.empty((depth, bw.shape[1] // 2, bw.shape[2]), f32) for k, bw in big_w.items()}
    for n, ((l, _), (keys, started)) in enumerate(grads_travelling.items()):
        sums, zones = chips_wait(started, small_sum, f"grads_wait_{l}_{n}")
        for k, s, z in zip(keys, sums, zones):
            acc[k] = chip_sum_into(acc[k], l, s, z, chip, "chip_sum")
    acc = [acc[k] for k in ("win", "wo", "wu", "wd")]
    from_sibling = _exchange(acc, [a.shape for a in acc], "pair", lambda r, a, i: r, lambda r, a, i: r,
                             lambda a: acc[a].shape, "grads_from_pair", False)

    wa_cols, ws_cols = conv_a_w.shape[2], ssm_conv_w.shape[2]
    par_g = small["par"].reshape(depth, 3, CH)
    g_small = dict(
        norm_mix_pre=small["g1"], conv_out_norm=small["ga"], ssm_out_norm=small["gs"], norm_mix_post=small["g2"],
        norm_mlp_pre=small["g3"], norm_mlp_post=small["g4"], ssm_conv_b=small["bs"],
        conv_a_w=lax.dynamic_slice_in_dim(small["wa"].reshape(depth, 3, D), chip * wa_cols, wa_cols, axis=2),
        ssm_conv_w=lax.dynamic_slice_in_dim(small["ws"].reshape(depth, 4, XBC), chip * ws_cols, ws_cols, axis=2),
        dt_bias=par_g[:, 0, :NH], a_log=par_g[:, 1, :NH], d_skip=par_g[:, 2, :NH])

    given = dict(norm_mix_pre=(norm_mix_pre, m_norm_mix_pre, v_norm_mix_pre), w_in=(w_in, m_w_in, v_w_in),
                 conv_a_w=(conv_a_w, m_conv_a_w, v_conv_a_w), ssm_conv_w=(ssm_conv_w, m_ssm_conv_w, v_ssm_conv_w),
                 ssm_conv_b=(ssm_conv_b, m_ssm_conv_b, v_ssm_conv_b), dt_bias=(dt_bias, m_dt_bias, v_dt_bias),
                 a_log=(a_log, m_a_log, v_a_log), d_skip=(d_skip, m_d_skip, v_d_skip),
                 conv_out_norm=(conv_out_norm, m_conv_out_norm, v_conv_out_norm),
                 ssm_out_norm=(ssm_out_norm, m_ssm_out_norm, v_ssm_out_norm), w_out=(w_out, m_w_out, v_w_out),
                 norm_mix_post=(norm_mix_post, m_norm_mix_post, v_norm_mix_post),
                 norm_mlp_pre=(norm_mlp_pre, m_norm_mlp_pre, v_norm_mlp_pre), w_up=(w_up, m_w_up, v_w_up),
                 w_down=(w_down, m_w_down, v_w_down), norm_mlp_post=(norm_mlp_post, m_norm_mlp_post, v_norm_mlp_post))
    halves = dict(zip(["w_in", "w_out", "w_up", "w_down"], zip(acc, from_sibling)))
    order = ["norm_mix_pre", "w_in", "conv_a_w", "ssm_conv_w", "ssm_conv_b", "dt_bias", "a_log", "d_skip",
             "conv_out_norm", "ssm_out_norm", "w_out", "norm_mix_post", "norm_mlp_pre", "w_up", "w_down",
             "norm_mlp_post"]
    g_out, d_out, m_out, v_out = [], [], [], []
    for n in order:
        wv, mv, vv = given[n]
        if n in halves and wv.shape[-1] % CH:
            own, recv = halves[n]
            gv = jnp.concatenate([jnp.where(core == 0, own, recv), jnp.where(core == 0, recv, own)], axis=1)
            to_cols, to_rows = (lambda a: jnp.transpose(a, (2, 0, 1))), (lambda a: jnp.transpose(a, (1, 2, 0)))
            dlt, m2, v2 = [to_rows(o) for o in adamw_leading(to_cols(wv), to_cols(gv), to_cols(mv), to_cols(vv),
                                                             "adamw_cols")]
        elif n in halves:
            gv, dlt, m2, v2 = adamw_halves(wv, *halves[n], mv, vv, core, "adamw_matrix")
        else:
            gv = g_small[n].reshape(wv.shape)
            two_d = lambda a: a.reshape(-1, a.shape[-1])
            dlt, m2, v2 = adamw(two_d(wv), two_d(gv), two_d(mv), two_d(vv), "adamw")
        g_out.append(gv)
        d_out.append(dlt.reshape(wv.shape))
        m_out.append(m2.reshape(wv.shape))
        v_out.append(v2.reshape(wv.shape))
    return (loss, dx.reshape(nb, seq, D), *g_out, *d_out, *m_out, *v_out)
```

```python
import functools

import jax
import jax.numpy as jnp
from jax import lax
from jax.experimental import pallas as pl
from jax.experimental.pallas import tpu as pltpu

f32, bf16 = jnp.float32, jnp.bfloat16

D = 1024
NH, HP = 16, 64
NG, NS = 2, 128
CH = 128
XBC = D + 2 * NG * NS
DFF = 4 * D
IN_COLS = 3 * D + D + XBC + NH
PROJ = 5760
COL_Z, COL_XBC, COL_DT = 3 * D, 4 * D, 4 * D + XBC
EPS = 1e-6
HALO = 8
VMEM_LIMIT = 56 * 2**20
MESH = pl.DeviceIdType.MESH

LR, B1, B2, AEPS, WD, STEP = 0.001, 0.9, 0.999, 1e-08, 0.01, 10


def _cparams(n_axes):
    return pltpu.CompilerParams(dimension_semantics=("arbitrary",) * n_axes, vmem_limit_bytes=VMEM_LIMIT)


def _sds(shape, dtype):
    return jax.ShapeDtypeStruct(tuple(shape), dtype)


def _token_spec(token):
    return [] if token is None else [pl.BlockSpec(memory_space=pl.ANY)]


def _token_arg(token):
    return [] if token is None else [token]


def _rms_fwd(x, g):
    r = lax.rsqrt(jnp.mean(x * x, axis=-1, keepdims=True) + EPS)
    return x * r * g


def _rms_bwd(x, g, dy):
    r = lax.rsqrt(jnp.mean(x * x, axis=-1, keepdims=True) + EPS)
    xh = x * r
    gdy = dy * g
    dx = r * (gdy - xh * jnp.mean(xh * gdy, axis=-1, keepdims=True))
    return dx, dy * xh


def _accum(ref, part, first):
    @pl.when(first)
    def _():
        ref[...] = part

    @pl.when(jnp.logical_not(first))
    def _():
        ref[...] += part


def _dot_nt(a, b):
    return lax.dot_general(a, b, (((1,), (1,)), ((), ())), preferred_element_type=f32)


def _dot_tn(a, b):
    return lax.dot_general(a, b, (((0,), (0,)), ((), ())), preferred_element_type=f32)


def _dot(a, b):
    return jnp.dot(a, b, preferred_element_type=f32)


def _split_dot(x, e_bf, n_split, nt=False):
    acc = None
    rem = x
    for s in range(n_split):
        hi = rem.astype(bf16)
        term = _dot_nt(hi, e_bf) if nt else _dot(hi, e_bf)
        acc = term if acc is None else acc + term
        if s + 1 < n_split:
            rem = rem - hi.astype(f32)
    return acc


def _sigmoid(x):
    return 0.5 * jnp.tanh(0.5 * x) + 0.5


def norm_matmul(x, g, w, tm, tn, out_dtype, name, token=None):
    t = x.shape[0]
    if w.ndim == 3:
        assert w.shape[2] == tn
        n = w.shape[0] * tn
        w_spec = pl.BlockSpec((None, D, tn), lambda i, j: (j, 0, 0))
    else:
        n = w.shape[1]
        w_spec = pl.BlockSpec((D, tn), lambda i, j: (0, j))

    def body(x_ref, g_ref, w_ref, *rest):
        o_ref, h_ref = rest[-2:]

        @pl.when(pl.program_id(1) == 0)
        def _():
            h_ref[...] = _rms_fwd(x_ref[...], g_ref[...]).astype(bf16)

        o_ref[...] = _dot(h_ref[...], w_ref[...]).astype(out_dtype)

    return pl.pallas_call(
        body, name=name, grid=(t // tm, n // tn),
        in_specs=[pl.BlockSpec((tm, D), lambda i, j: (i, 0)), pl.BlockSpec((1, D), lambda i, j: (0, 0)), w_spec]
        + _token_spec(token),
        out_specs=[pl.BlockSpec((tm, tn), lambda i, j: (i, j)), pl.BlockSpec((tm, D), lambda i, j: (i, 0))],
        out_shape=[_sds((t, n), out_dtype), _sds((t, D), bf16)],
        compiler_params=_cparams(2))(x, g, w, *_token_arg(token))


def matmul_postnorm(a, w, xres, g, tm, relu2, name):
    t, k = a.shape

    def body(a_ref, w_ref, xr_ref, g_ref, y_ref, xo_ref):
        av = a_ref[...]
        if relu2:
            af = jnp.maximum(av.astype(f32), 0.0)
            av = (af * af).astype(bf16)
        y = _dot(av, w_ref[...])
        y_ref[...] = y
        xo_ref[...] = xr_ref[...] + _rms_fwd(y, g_ref[...])

    return pl.pallas_call(
        body, name=name, grid=(t // tm,),
        in_specs=[pl.BlockSpec((tm, k), lambda i: (i, 0)), pl.BlockSpec((k, D), lambda i: (0, 0)),
                  pl.BlockSpec((tm, D), lambda i: (i, 0)), pl.BlockSpec((1, D), lambda i: (0, 0))],
        out_specs=[pl.BlockSpec((tm, D), lambda i: (i, 0)), pl.BlockSpec((tm, D), lambda i: (i, 0))],
        out_shape=[_sds((t, D), f32), _sds((t, D), f32)],
        compiler_params=_cparams(1))(a, w, xres, g)


def postnorm_bwd_matmul(y, g, dxo, w, fp, tm, tn, out_dtype, name, token=None):
    t, n = y.shape[0], w.shape[0]
    relu = fp is not None

    def body(*refs):
        y_ref, g_ref, dxo_ref, w_ref = refs[:4]
        fp_ref = refs[4] if relu else None
        dy_ref, dg_ref, da_ref = refs[-3:]
        i, j = pl.program_id(0), pl.program_id(1)

        @pl.when(j == 0)
        def _():
            dx, dgc = _rms_bwd(y_ref[...], g_ref[...], dxo_ref[...])
            dy_ref[...] = dx.astype(bf16)
            _accum(dg_ref, jnp.sum(dgc, axis=0, keepdims=True), i == 0)

        da = _dot_nt(dy_ref[...], w_ref[...])
        if relu:
            da = da * (2.0 * jnp.maximum(fp_ref[...].astype(f32), 0.0))
        da_ref[...] = da.astype(out_dtype)

    in_specs = [pl.BlockSpec((tm, D), lambda i, j: (i, 0)), pl.BlockSpec((1, D), lambda i, j: (0, 0)),
                pl.BlockSpec((tm, D), lambda i, j: (i, 0)), pl.BlockSpec((tn, D), lambda i, j: (j, 0))]
    args = [y, g, dxo, w]
    if relu:
        in_specs.append(pl.BlockSpec((tm, tn), lambda i, j: (i, j)))
        args.append(fp)
    in_specs += _token_spec(token)
    args += _token_arg(token)
    return pl.pallas_call(
        body, name=name, grid=(t // tm, n // tn), in_specs=in_specs,
        out_specs=[pl.BlockSpec((tm, D), lambda i, j: (i, 0)), pl.BlockSpec((1, D), lambda i, j: (0, 0)),
                   pl.BlockSpec((tm, tn), lambda i, j: (i, j))],
        out_shape=[_sds((t, D), bf16), _sds((1, D), f32), _sds((t, n), out_dtype)],
        compiler_params=_cparams(2))(*args)


def matmul_prenorm_bwd(da, w, x, g, dxo, tm, name, token=None):
    t, k = da.shape
    blocked = w.ndim == 3

    def body(da_ref, w_ref, x_ref, g_ref, dxo_ref, *rest):
        dx_ref, dg_ref = rest[-2:]
        if blocked:
            kc = w.shape[2]
            dh = _dot_nt(da_ref[:, 0:kc], w_ref[0])
            for q in range(1, w.shape[0]):
                dh = dh + _dot_nt(da_ref[:, q * kc:(q + 1) * kc], w_ref[q])
        else:
            dh = _dot_nt(da_ref[...], w_ref[...])
        dxn, dgc = _rms_bwd(x_ref[...], g_ref[...], dh)
        dx_ref[...] = dxo_ref[...] + dxn
        _accum(dg_ref, jnp.sum(dgc, axis=0, keepdims=True), pl.program_id(0) == 0)

    w_spec = pl.BlockSpec(w.shape, (lambda i: (0, 0, 0)) if blocked else (lambda i: (0, 0)))
    return pl.pallas_call(
        body, name=name, grid=(t // tm,),
        in_specs=[pl.BlockSpec((tm, k), lambda i: (i, 0)), w_spec,
                  pl.BlockSpec((tm, D), lambda i: (i, 0)), pl.BlockSpec((1, D), lambda i: (0, 0)),
                  pl.BlockSpec((tm, D), lambda i: (i, 0))] + _token_spec(token),
        out_specs=[pl.BlockSpec((tm, D), lambda i: (i, 0)), pl.BlockSpec((1, D), lambda i: (0, 0))],
        out_shape=[_sds((t, D), f32), _sds((1, D), f32)],
        compiler_params=_cparams(1))(da, w, x, g, dxo, *_token_arg(token))


def matmul_tn(a, b, tm, tn, relu2, name, col_blocks=False):
    t, m = a.shape
    n = b.shape[1]
    if col_blocks:
        out_spec, out_shape = pl.BlockSpec((None, tm, tn), lambda i, j: (j, i, 0)), _sds((n // tn, m, tn), f32)
    else:
        out_spec, out_shape = pl.BlockSpec((tm, tn), lambda i, j: (i, j)), _sds((m, n), f32)

    def body(a_ref, b_ref, o_ref, at_ref):
        @pl.when(pl.program_id(1) == 0)
        def _():
            av = a_ref[...]
            if relu2:
                af = jnp.maximum(av.astype(f32), 0.0)
                av = (af * af).astype(bf16)
            at_ref[...] = av.T

        o_ref[...] = _dot(at_ref[...], b_ref[...])

    return pl.pallas_call(
        body, name=name, grid=(m // tm, n // tn),
        in_specs=[pl.BlockSpec((t, tm), lambda i, j: (0, i)), pl.BlockSpec((t, tn), lambda i, j: (0, j))],
        out_specs=out_spec, out_shape=out_shape,
        scratch_shapes=[pltpu.VMEM((tm, t), bf16)],
        compiler_params=_cparams(2))(a, b)


ROWS_A = 16
ROWS_B = 32
UNROLL = 4


def _past(win, s):
    return (win if s == 0 else pltpu.roll(win, s, 0))[HALO:]


def _future(win, s):
    n = win.shape[0]
    return (win if s == 0 else pltpu.roll(win, n - s, 0))[:n - HALO]


def _fold8(v):
    return v.reshape(v.shape[0] // 8, 8, v.shape[1]).sum(axis=0)


def _halo_prev(tb, col):
    return lambda i: (jnp.maximum(i * (tb // HALO) - 1, 0), col)


def _halo_next(tb, col, t):
    return lambda i: (jnp.minimum((i + 1) * (tb // HALO), t // HALO - 1), col)


def group_a_fwd(proj, wa, g, seq, tb, name):
    t = proj.shape[0]
    bps = seq // tb

    def body(xa_ref, ca_ref, ba_ref, xah_ref, cah_ref, wa_ref, g_ref, o_ref, u_scr):
        first = (pl.program_id(0) % bps) == 0
        u_scr[0:HALO, :] = jnp.where(first, 0.0, cah_ref[...] * xah_ref[...])
        w, gv = wa_ref[...], g_ref[...]

        def chunk(i, carry):
            r = pl.multiple_of(i * ROWS_A, ROWS_A)
            rows = pl.ds(r, ROWS_A)
            u_scr[pl.ds(pl.multiple_of(HALO + r, HALO), ROWS_A), :] = ca_ref[rows, :] * xa_ref[rows, :]
            win = u_scr[pl.ds(r, ROWS_A + HALO), :]
            cv = w[2:3] * _past(win, 0) + w[1:2] * _past(win, 1) + w[0:1] * _past(win, 2)
            o_ref[rows, :] = _rms_fwd(ba_ref[rows, :] * cv, gv).astype(bf16)
            return carry

        lax.fori_loop(0, tb // ROWS_A, chunk, 0, unroll=UNROLL)

    blk = lambda c: pl.BlockSpec((tb, D), lambda i: (i, c))
    return pl.pallas_call(
        body, name=name, grid=(t // tb,),
        in_specs=[blk(0), blk(1), blk(2),
                  pl.BlockSpec((HALO, D), _halo_prev(tb, 0)), pl.BlockSpec((HALO, D), _halo_prev(tb, 1)),
                  pl.BlockSpec((8, D), lambda i: (0, 0)), pl.BlockSpec((1, D), lambda i: (0, 0))],
        out_specs=pl.BlockSpec((tb, D), lambda i: (i, 0)),
        out_shape=_sds((t, 2 * D), bf16),
        scratch_shapes=[pltpu.VMEM((tb + HALO, D), f32)],
        compiler_params=_cparams(1))(proj, proj, proj, proj, proj, wa, g)


def group_a_bwd(proj, dcat, wa, g, seq, tb, name, token=None):
    t = proj.shape[0]
    bps = seq // tb

    def body(xa_ref, ca_ref, ba_ref, dy_ref, xap_ref, cap_ref, xan_ref, can_ref, ban_ref, dyn_ref, wa_ref, g_ref,
             *rest):
        dp_ref, dwa_ref, dg_ref, u_scr, d_scr, acc_scr = rest[-6:]
        i = pl.program_id(0)
        first = (i % bps) == 0
        last = (i % bps) == bps - 1
        w = wa_ref[...]
        gv = g_ref[...]
        u_scr[0:HALO, :] = jnp.where(first, 0.0, cap_ref[...] * xap_ref[...])
        u_scr[HALO + tb:2 * HALO + tb, :] = can_ref[...] * xan_ref[...]
        acc_scr[...] = jnp.zeros_like(acc_scr)

        def forward_part(n, carry):
            r = pl.multiple_of(n * ROWS_A, ROWS_A)
            rows = pl.ds(r, ROWS_A)
            ba = ba_ref[rows, :]
            u_scr[pl.ds(pl.multiple_of(HALO + r, HALO), ROWS_A), :] = ca_ref[rows, :] * xa_ref[rows, :]
            win = u_scr[pl.ds(r, ROWS_A + HALO), :]
            u = [_past(win, s) for s in range(3)]
            cv = w[2:3] * u[0] + w[1:2] * u[1] + w[0:1] * u[2]
            dya, dgc = _rms_bwd(ba * cv, gv, dy_ref[rows, :])
            dcv = dya * ba
            d_scr[rows, :] = dcv
            dp_ref[rows, 2 * D:3 * D] = (dya * cv).astype(bf16)
            acc_scr[0:8, :] += _fold8(dgc)
            for k in range(3):
                acc_scr[8 + 8 * k:16 + 8 * k, :] += _fold8(dcv * u[2 - k])
            return carry

        lax.fori_loop(0, tb // ROWS_A, forward_part, 0, unroll=UNROLL)

        start = HALO + tb
        cvn = (w[2:3] * u_scr[pl.ds(start, HALO), :] + w[1:2] * u_scr[pl.ds(start - 1, HALO), :]
               + w[0:1] * u_scr[pl.ds(start - 2, HALO), :])
        ban = ban_ref[...]
        dyan, _ = _rms_bwd(ban * cvn, gv, dyn_ref[...])
        d_scr[tb:tb + HALO, :] = jnp.where(last, 0.0, dyan * ban)

        def backward_part(n, carry):
            r = pl.multiple_of(n * ROWS_A, ROWS_A)
            rows = pl.ds(r, ROWS_A)
            win = d_scr[pl.ds(r, ROWS_A + HALO), :]
            du = w[2:3] * _future(win, 0) + w[1:2] * _future(win, 1) + w[0:1] * _future(win, 2)
            dp_ref[rows, 0:D] = (du * ca_ref[rows, :]).astype(bf16)
            dp_ref[rows, D:2 * D] = (du * xa_ref[rows, :]).astype(bf16)
            return carry

        lax.fori_loop(0, tb // ROWS_A, backward_part, 0, unroll=UNROLL)

        row = lax.broadcasted_iota(jnp.int32, (8, D), 0)
        dw = jnp.zeros((8, D), f32)
        for k in range(3):
            dw = jnp.where(row == k, jnp.sum(acc_scr[8 + 8 * k:16 + 8 * k, :], axis=0, keepdims=True), dw)
        _accum(dwa_ref, dw, i == 0)
        _accum(dg_ref, jnp.sum(acc_scr[0:8, :], axis=0, keepdims=True), i == 0)

    blk = lambda c: pl.BlockSpec((tb, D), lambda i: (i, c))
    prv = lambda c: pl.BlockSpec((HALO, D), _halo_prev(tb, c))
    nxt = lambda c: pl.BlockSpec((HALO, D), _halo_next(tb, c, t))
    return pl.pallas_call(
        body, name=name, grid=(t // tb,),
        in_specs=[blk(0), blk(1), blk(2), blk(0), prv(0), prv(1), nxt(0), nxt(1), nxt(2), nxt(0),
                  pl.BlockSpec((8, D), lambda i: (0, 0)), pl.BlockSpec((1, D), lambda i: (0, 0))] + _token_spec(token),
        out_specs=[pl.BlockSpec((tb, 3 * D), lambda i: (i, 0)), pl.BlockSpec((8, D), lambda i: (0, 0)),
                   pl.BlockSpec((1, D), lambda i: (0, 0))],
        out_shape=[_sds((t, PROJ), bf16), _sds((8, D), f32), _sds((1, D), f32)],
        scratch_shapes=[pltpu.VMEM((tb + 2 * HALO, D), f32), pltpu.VMEM((tb + HALO, D), f32), pltpu.VMEM((32, D), f32)],
        compiler_params=_cparams(1))(proj, proj, proj, dcat, proj, proj, proj, proj, proj, dcat, wa, g,
                                     *_token_arg(token))


CB = 512
XBC_BLK0 = COL_XBC // CB


def conv_b_fwd(proj, ws, bs, seq, tb, name):
    t = proj.shape[0]
    bps = seq // tb

    def body(x_ref, xp_ref, w_ref, b_ref, o_ref, x_scr):
        first = (pl.program_id(1) % bps) == 0
        x_scr[0:HALO, :] = jnp.where(first, 0.0, xp_ref[...])
        w, bias = w_ref[...], b_ref[...]

        def chunk(n, carry):
            r = pl.multiple_of(n * ROWS_B, ROWS_B)
            rows = pl.ds(r, ROWS_B)
            x_scr[pl.ds(pl.multiple_of(HALO + r, HALO), ROWS_B), :] = x_ref[rows, :]
            win = x_scr[pl.ds(r, ROWS_B + HALO), :]
            xc = bias + w[3:4] * _past(win, 0)
            for k in range(3):
                xc = xc + w[k:k + 1] * _past(win, 3 - k)
            o_ref[rows, :] = xc * _sigmoid(xc)
            return carry

        lax.fori_loop(0, tb // ROWS_B, chunk, 0, unroll=UNROLL)

    return pl.pallas_call(
        body, name=name, grid=(XBC // CB, t // tb),
        in_specs=[pl.BlockSpec((tb, CB), lambda j, i: (i, XBC_BLK0 + j)),
                  pl.BlockSpec((HALO, CB), lambda j, i: (jnp.maximum(i * (tb // HALO) - 1, 0), XBC_BLK0 + j)),
                  pl.BlockSpec((8, CB), lambda j, i: (0, j)), pl.BlockSpec((1, CB), lambda j, i: (0, j))],
        out_specs=pl.BlockSpec((tb, CB), lambda j, i: (i, j)),
        out_shape=_sds((t, XBC), f32),
        scratch_shapes=[pltpu.VMEM((tb + HALO, CB), f32)],
        compiler_params=_cparams(2))(proj, proj, ws, bs)


def conv_b_bwd(proj, dxs, ws, bs, dproj, seq, tb, name):
    t = proj.shape[0]
    bps = seq // tb

    def body(x_ref, xp_ref, xn_ref, d_ref, dn_ref, w_ref, b_ref, dproj_ref, dx_ref, dw_ref, db_ref, x_scr, d_scr,
             acc_scr):
        i = pl.program_id(1)
        first = (i % bps) == 0
        last = (i % bps) == bps - 1
        w = w_ref[...]
        bias = b_ref[...]
        x_scr[0:HALO, :] = jnp.where(first, 0.0, xp_ref[...])
        x_scr[HALO + tb:2 * HALO + tb, :] = xn_ref[...]
        acc_scr[...] = jnp.zeros_like(acc_scr)

        def dsilu(xc, d):
            sg = _sigmoid(xc)
            return d * (sg * (1.0 + xc * (1.0 - sg)))

        def forward_part(n, carry):
            r = pl.multiple_of(n * ROWS_B, ROWS_B)
            rows = pl.ds(r, ROWS_B)
            x_scr[pl.ds(pl.multiple_of(HALO + r, HALO), ROWS_B), :] = x_ref[rows, :]
            win = x_scr[pl.ds(r, ROWS_B + HALO), :]
            xs = [_past(win, s) for s in range(4)]
            xc = bias + w[3:4] * xs[0]
            for k in range(3):
                xc = xc + w[k:k + 1] * xs[3 - k]
            dxc = dsilu(xc, d_ref[rows, :])
            d_scr[rows, :] = dxc
            acc_scr[0:8, :] += _fold8(dxc)
            for k in range(4):
                acc_scr[8 + 8 * k:16 + 8 * k, :] += _fold8(dxc * xs[3 - k])
            return carry

        lax.fori_loop(0, tb // ROWS_B, forward_part, 0, unroll=UNROLL)

        start = HALO + tb
        xcn = bias + w[3:4] * x_scr[pl.ds(start, HALO), :]
        for k in range(3):
            xcn = xcn + w[k:k + 1] * x_scr[pl.ds(start - 3 + k, HALO), :]
        d_scr[tb:tb + HALO, :] = jnp.where(last, 0.0, dsilu(xcn, dn_ref[...]))

        def backward_part(n, carry):
            r = pl.multiple_of(n * ROWS_B, ROWS_B)
            win = d_scr[pl.ds(r, ROWS_B + HALO), :]
            dx = w[3:4] * _future(win, 0)
            for k in range(3):
                dx = dx + w[k:k + 1] * _future(win, 3 - k)
            dx_ref[pl.ds(r, ROWS_B), :] = dx.astype(bf16)
            return carry

        lax.fori_loop(0, tb // ROWS_B, backward_part, 0, unroll=UNROLL)

        row = lax.broadcasted_iota(jnp.int32, (8, CB), 0)
        dw = jnp.zeros((8, CB), f32)
        for k in range(4):
            dw = jnp.where(row == k, jnp.sum(acc_scr[8 + 8 * k:16 + 8 * k, :], axis=0, keepdims=True), dw)
        _accum(dw_ref, dw, i == 0)
        _accum(db_ref, jnp.sum(acc_scr[0:8, :], axis=0, keepdims=True), i == 0)

    nh = t // HALO
    return pl.pallas_call(
        body, name=name, grid=(XBC // CB, t // tb),
        in_specs=[pl.BlockSpec((tb, CB), lambda j, i: (i, XBC_BLK0 + j)),
                  pl.BlockSpec((HALO, CB), lambda j, i: (jnp.maximum(i * (tb // HALO) - 1, 0), XBC_BLK0 + j)),
                  pl.BlockSpec((HALO, CB), lambda j, i: (jnp.minimum((i + 1) * (tb // HALO), nh - 1), XBC_BLK0 + j)),
                  pl.BlockSpec((tb, CB), lambda j, i: (i, j)),
                  pl.BlockSpec((HALO, CB), lambda j, i: (jnp.minimum((i + 1) * (tb // HALO), nh - 1), j)),
                  pl.BlockSpec((8, CB), lambda j, i: (0, j)), pl.BlockSpec((1, CB), lambda j, i: (0, j)),
                  pl.BlockSpec(memory_space=pl.ANY)],
        out_specs=[pl.BlockSpec((tb, CB), lambda j, i: (i, XBC_BLK0 + j)), pl.BlockSpec((8, CB), lambda j, i: (0, j)),
                   pl.BlockSpec((1, CB), lambda j, i: (0, j))],
        out_shape=[_sds((t, PROJ), bf16), _sds((8, XBC), f32), _sds((1, XBC), f32)],
        input_output_aliases={7: 0},
        scratch_shapes=[pltpu.VMEM((tb + 2 * HALO, CB), f32), pltpu.VMEM((tb + HALO, CB), f32),
                        pltpu.VMEM((40, CB), f32)],
        compiler_params=_cparams(2))(proj, proj, proj, dxs, dxs, ws, bs, dproj)


def place_columns(buf, part, col_block, tb, name):
    t, wdt = part.shape

    def body(p_ref, buf_ref, o_ref):
        o_ref[...] = p_ref[...]

    return pl.pallas_call(
        body, name=name, grid=(t // tb,),
        in_specs=[pl.BlockSpec((tb, wdt), lambda i: (i, 0)), pl.BlockSpec(memory_space=pl.ANY)],
        out_specs=pl.BlockSpec((tb, wdt), lambda i: (i, col_block)), out_shape=_sds(buf.shape, buf.dtype),
        input_output_aliases={1: 0}, compiler_params=_cparams(1))(part, buf)


GW = D // NG
EXPAND_TERMS = 2
REDUCE_TERMS = 1


def _ssd_consts():
    head_of_lane = jnp.arange(D) // HP
    expand = (jnp.arange(CH)[:, None] == head_of_lane[None, :]).astype(bf16)
    tri = (jnp.arange(CH)[:, None] >= jnp.arange(CH)[None, :]).astype(f32)
    return expand, tri


def _ssd_common(par_ref, dtr_ref, e_ref, tri_ref):
    par = par_ref[...]
    dtb, alog, dsk = par[0:1], par[1:2], par[2:3]
    lane = lax.broadcasted_iota(jnp.int32, (CH, CH), 1)
    a = -jnp.exp(alog)
    dtr = dtr_ref[...] + dtb
    sp = jnp.maximum(dtr, 0.0) + jnp.log(1.0 + jnp.exp(-jnp.abs(dtr)))
    dt = jnp.where(lane < NH, sp, 0.0)
    cs = jnp.dot(tri_ref[...], dt * a, precision=lax.Precision.HIGHEST, preferred_element_type=f32)
    cs_last = cs[CH - 1:CH, :]
    dte = jnp.exp(cs_last - cs)
    ecs = jnp.exp(cs)
    ecl = jnp.exp(cs_last)
    e = e_ref[...]
    row8 = lax.broadcasted_iota(jnp.int32, (8, CH), 0)
    r8 = _split_dot(jnp.where(row8 == 0, ecl, jnp.where(row8 == 1, dsk, 0.0)), e, 3)
    return dict(a=a, dtr=dtr, dt=dt, cs=cs, cst=cs.T, dte=dte, ecs=ecs, ecl=ecl, e=e, lane=lane,
                dt_x=_split_dot(dt, e, EXPAND_TERMS), dte_x=_split_dot(dte, e, EXPAND_TERMS),
                ecs_x=_split_dot(ecs, e, EXPAND_TERMS),
                ecl_x=r8[0:1], dsk_x=r8[1:2])


def _decay_matrix(c, h):
    li = lax.broadcasted_iota(jnp.int32, (CH, CH), 0)
    seg = c["cs"][:, h:h + 1] - c["cst"][h:h + 1, :]
    return jnp.exp(jnp.where(li >= c["lane"], seg, -jnp.inf))


def _gate_norm_fwd(y, z, gs):
    zg = z * _sigmoid(z)
    yg = y * zg
    return jnp.concatenate([_rms_fwd(yg[:, k * GW:(k + 1) * GW], gs[:, k * GW:(k + 1) * GW]) for k in range(NG)], axis=1)


def ssd_fwd(xbcs, proj, par, gs, cat, seq, name):
    t = xbcs.shape[0]
    nc = seq // CH
    expand, tri = _ssd_consts()

    def body(xs_ref, b_ref, c_ref, dtr_ref, z_ref, par_ref, e_ref, tri_ref, gs_ref, cat_ref, yn_ref, y_ref, st_ref,
             p_scr, yd_scr):
        @pl.when(pl.program_id(0) % nc == 0)
        def _():
            p_scr[...] = jnp.zeros_like(p_scr)

        c = _ssd_common(par_ref, dtr_ref, e_ref, tri_ref)
        xs = xs_ref[...]
        xdt = xs * c["dt_x"]
        xdt_b = xdt.astype(bf16)
        xdte_b = (xdt * c["dte_x"]).astype(bf16)
        p = p_scr[...]
        st_ref[0] = p
        p_b = p.astype(bf16)
        lo = c["lane"] < HP
        for g in range(NG):
            bg = b_ref[:, g * NS:(g + 1) * NS].astype(bf16)
            cg = c_ref[:, g * NS:(g + 1) * NS].astype(bf16)
            gmat = _dot_nt(cg, bg)
            for q in range(GW // CH):
                col = g * GW + q * CH
                xp = xdt_b[:, col:col + CH]
                h0 = col // HP
                m0 = (gmat * _decay_matrix(c, h0)).astype(bf16)
                m1 = (gmat * _decay_matrix(c, h0 + 1)).astype(bf16)
                stacked = jnp.concatenate([jnp.where(lo, xp, jnp.zeros_like(xp)),
                                           jnp.where(lo, jnp.zeros_like(xp), xp)], axis=0)
                yd_scr[:, col:col + CH] = _dot(jnp.concatenate([m0, m1], axis=1), stacked)
            gsl = slice(g * GW, (g + 1) * GW)
            yoff = _dot(cg, p_b[:, gsl]) * c["ecs_x"][:, gsl]
            yd_scr[:, gsl] = yd_scr[:, gsl] + yoff
            p_scr[:, gsl] = p[:, gsl] * c["ecl_x"][:, gsl] + _dot_tn(bg, xdte_b[:, gsl])
        y = yd_scr[...] + c["dsk_x"] * xs
        y_ref[...] = y
        yn_ref[...] = _gate_norm_fwd(y, z_ref[...], gs_ref[...]).astype(bf16)

    nb = t // CH
    return pl.pallas_call(
        body, name=name, grid=(nb,),
        in_specs=[pl.BlockSpec((CH, D), lambda i: (i, 0)),
                  pl.BlockSpec((CH, NG * NS), lambda i: (i, D // (NG * NS))),
                  pl.BlockSpec((CH, NG * NS), lambda i: (i, D // (NG * NS) + 1)),
                  pl.BlockSpec((CH, CH), lambda i: (i, COL_DT // CH)),
                  pl.BlockSpec((CH, D), lambda i: (i, COL_Z // D)),
                  pl.BlockSpec((8, CH), lambda i: (0, 0)), pl.BlockSpec((CH, D), lambda i: (0, 0)),
                  pl.BlockSpec((CH, CH), lambda i: (0, 0)), pl.BlockSpec((1, D), lambda i: (0, 0)),
                  pl.BlockSpec(memory_space=pl.ANY)],
        out_specs=[pl.BlockSpec((CH, D), lambda i: (i, 1)), pl.BlockSpec((CH, D), lambda i: (i, 0)),
                   pl.BlockSpec((1, NS, D), lambda i: (i, 0, 0))],
        out_shape=[_sds((t, 2 * D), bf16), _sds((t, D), f32), _sds((nb, NS, D), f32)],
        input_output_aliases={9: 0},
        scratch_shapes=[pltpu.VMEM((NS, D), f32), pltpu.VMEM((CH, D), f32)],
        compiler_params=_cparams(1))(xbcs, xbcs, xbcs, proj, proj, par, expand, tri, gs, cat)


def ssd_bwd(xbcs, proj, ypre, states, dcat, par, gs, dproj, seq, name):
    t = xbcs.shape[0]
    nc = seq // CH
    expand, tri = _ssd_consts()

    def body(xs_ref, b_ref, c_ref, dtr_ref, z_ref, y_ref, st_ref, dyn_ref, par_ref, e_ref, tri_ref, gs_ref, dproj_ref,
             dx_ref, dz_ref, ddt_ref, dpar_ref, dgs_ref, dp_scr, dxdt_scr):
        i = pl.program_id(0)

        @pl.when(i % nc == 0)
        def _():
            dp_scr[...] = jnp.zeros_like(dp_scr)

        c = _ssd_common(par_ref, dtr_ref, e_ref, tri_ref)
        e = c["e"]
        lane = c["lane"]
        sub = lax.broadcasted_iota(jnp.int32, (CH, CH), 0)
        xs = xs_ref[...]
        xdt = xs * c["dt_x"]
        xdt_b = xdt.astype(bf16)
        xdte_b = (xdt * c["dte_x"]).astype(bf16)
        p = st_ref[0]
        p_b = p.astype(bf16)
        dpn = dp_scr[...]
        dpn_b = dpn.astype(bf16)

        y, z, gs_v = y_ref[...], z_ref[...], gs_ref[...]
        zs = _sigmoid(z)
        zg = z * zs
        yg = y * zg
        parts, gparts = [], []
        for k in range(NG):
            sl = slice(k * GW, (k + 1) * GW)
            dxk, dgk = _rms_bwd(yg[:, sl], gs_v[:, sl], dyn_ref[:, sl])
            parts.append(dxk)
            gparts.append(dgk)
        dyg = jnp.concatenate(parts, axis=1)
        dgs_rows = jnp.concatenate(gparts, axis=1)
        dy = dyg * zg
        dz_ref[...] = (dyg * y * (zs * (1.0 + z * (1.0 - zs)))).astype(bf16)
        dy_b = dy.astype(bf16)
        dq_b = (dy * c["ecs_x"]).astype(bf16)

        lo = lane < HP
        dcs = jnp.zeros((CH, CH), f32)
        dcst = jnp.zeros((CH, CH), f32)
        for g in range(NG):
            gsl = slice(g * GW, (g + 1) * GW)
            bg = b_ref[:, g * NS:(g + 1) * NS].astype(bf16)
            cg = c_ref[:, g * NS:(g + 1) * NS].astype(bf16)
            gmat = _dot_nt(cg, bg)
            dgm = jnp.zeros((CH, CH), f32)
            for q in range(GW // CH):
                col = g * GW + q * CH
                xp = xdt_b[:, col:col + CH]
                dyp = dy_b[:, col:col + CH]
                zero = jnp.zeros_like(dyp)
                xp2 = jnp.concatenate([jnp.where(lo, xp, zero), jnp.where(lo, zero, xp)], axis=0)
                dy2 = jnp.concatenate([jnp.where(lo, dyp, zero), jnp.where(lo, zero, dyp)], axis=0)
                dm2 = _dot_nt(dyp, xp2)
                ms = []
                for hh in range(2):
                    h = col // HP + hh
                    dec = _decay_matrix(c, h)
                    m = gmat * dec
                    dm = dm2[:, hh * CH:(hh + 1) * CH]
                    dseg = dm * m
                    dcs = dcs + jnp.where(lane == h, jnp.sum(dseg, axis=1, keepdims=True), 0.0)
                    dcst = dcst + jnp.where(sub == h, jnp.sum(dseg, axis=0, keepdims=True), 0.0)
                    dgm = dgm + dm * dec
                    ms.append(m.astype(bf16))
                dxdt_scr[:, col:col + CH] = _dot_tn(jnp.concatenate(ms, axis=0), dy2)
            dgm_b = dgm.astype(bf16)
            bds = _dot(bg, dpn_b[:, gsl])
            dxdt_scr[:, gsl] = dxdt_scr[:, gsl] + c["dte_x"][:, gsl] * bds
            dc_g = _dot(dgm_b, bg) + _dot_nt(dq_b[:, gsl], p_b[:, gsl])
            db_g = _dot_tn(dgm_b, cg) + _dot_nt(xdte_b[:, gsl], dpn_b[:, gsl])
            dx_ref[:, D + g * NS:D + (g + 1) * NS] = db_g
            dx_ref[:, D + NG * NS + g * NS:D + NG * NS + (g + 1) * NS] = dc_g
            dp_scr[:, gsl] = dpn[:, gsl] * c["ecl_x"][:, gsl] + _dot_tn(cg, dq_b[:, gsl])
            q_g = _dot(cg, p_b[:, gsl])
            e_g = e[:, gsl]
            dcs = dcs + c["ecs"] * _split_dot(dy[:, gsl] * q_g, e_g, REDUCE_TERMS, nt=True)
            ddte = _split_dot(xdt[:, gsl] * bds, e_g, REDUCE_TERMS, nt=True) * c["dte"]
            dcs = dcs - ddte
            dcs = dcs + jnp.where(sub == CH - 1, jnp.sum(ddte, axis=0, keepdims=True), 0.0)

        decl = _split_dot(jnp.broadcast_to(jnp.sum(dpn * p, axis=0, keepdims=True), (8, D)), e, 2, nt=True)[0:1]
        dcs = dcs + jnp.where(sub == CH - 1, c["ecl"] * decl, 0.0)
        dcs = dcs - dcst.T
        dadt = lax.dot_general(tri_ref[...], dcs, (((0,), (0,)), ((), ())), precision=lax.Precision.HIGHEST,
                               preferred_element_type=f32)
        dxdt = dxdt_scr[...]
        ddt = dadt * c["a"] + _split_dot(dxdt * xs, e, REDUCE_TERMS, nt=True)
        ddtr = jnp.where(lane < NH, ddt * _sigmoid(c["dtr"]), 0.0)
        ddt_ref[...] = ddtr.astype(bf16)
        dx_ref[:, 0:D] = dxdt * c["dt_x"] + c["dsk_x"] * dy
        dsk = _split_dot(jnp.broadcast_to(jnp.sum(dy * xs, axis=0, keepdims=True), (8, D)), e, 2, nt=True)[0:1]
        dalog = jnp.sum(dadt * c["dt"], axis=0, keepdims=True) * c["a"]
        row8 = lax.broadcasted_iota(jnp.int32, (8, CH), 0)
        dpar = jnp.where(row8 == 0, jnp.sum(ddtr, axis=0, keepdims=True),
                         jnp.where(row8 == 1, dalog, jnp.where(row8 == 2, dsk, 0.0)))
        dpar = jnp.where(lax.broadcasted_iota(jnp.int32, (8, CH), 1) < NH, dpar, 0.0)
        _accum(dpar_ref, dpar, i == 0)
        _accum(dgs_ref, jnp.sum(dgs_rows, axis=0, keepdims=True), i == 0)

    nb = t // CH
    rev = lambda i: (i // nc) * nc + (nc - 1 - i % nc)
    return pl.pallas_call(
        body, name=name, grid=(nb,),
        in_specs=[pl.BlockSpec((CH, D), lambda i: (rev(i), 0)),
                  pl.BlockSpec((CH, NG * NS), lambda i: (rev(i), D // (NG * NS))),
                  pl.BlockSpec((CH, NG * NS), lambda i: (rev(i), D // (NG * NS) + 1)),
                  pl.BlockSpec((CH, CH), lambda i: (rev(i), COL_DT // CH)),
                  pl.BlockSpec((CH, D), lambda i: (rev(i), COL_Z // D)),
                  pl.BlockSpec((CH, D), lambda i: (rev(i), 0)),
                  pl.BlockSpec((1, NS, D), lambda i: (rev(i), 0, 0)),
                  pl.BlockSpec((CH, D), lambda i: (rev(i), 1)),
                  pl.BlockSpec((8, CH), lambda i: (0, 0)), pl.BlockSpec((CH, D), lambda i: (0, 0)),
                  pl.BlockSpec((CH, CH), lambda i: (0, 0)), pl.BlockSpec((1, D), lambda i: (0, 0)),
                  pl.BlockSpec(memory_space=pl.ANY)],
        out_specs=[pl.BlockSpec((CH, XBC), lambda i: (rev(i), 0)), pl.BlockSpec((CH, D), lambda i: (rev(i), COL_Z // D)),
                   pl.BlockSpec((CH, CH), lambda i: (rev(i), 0)),
                   pl.BlockSpec((8, CH), lambda i: (0, 0)), pl.BlockSpec((1, D), lambda i: (0, 0))],
        out_shape=[_sds((t, XBC), f32), _sds((t, PROJ), bf16), _sds((t, CH), bf16), _sds((8, CH), f32), _sds((1, D), f32)],
        input_output_aliases={12: 1},
        scratch_shapes=[pltpu.VMEM((NS, D), f32), pltpu.VMEM((CH, D), f32)],
        compiler_params=_cparams(1))(xbcs, xbcs, xbcs, proj, proj, ypre, states, dcat, par, expand, tri, gs, dproj)


def loss_head(y, target, tb, name):
    t = y.shape[0]

    def body(y_ref, t_ref, s_ref, dy_ref):
        err = y_ref[...] - t_ref[...]
        dy_ref[...] = err * (1.0 / D)
        _accum(s_ref, jnp.zeros((8, CH), f32) + jnp.sum(err * err), pl.program_id(0) == 0)

    return pl.pallas_call(
        body, name=name, grid=(t // tb,),
        in_specs=[pl.BlockSpec((tb, D), lambda i: (i, 0)), pl.BlockSpec((tb, D), lambda i: (i, 0))],
        out_specs=[pl.BlockSpec((8, CH), lambda i: (0, 0)), pl.BlockSpec((tb, D), lambda i: (i, 0))],
        out_shape=[_sds((8, CH), f32), _sds((t, D), f32)],
        compiler_params=_cparams(1))(y, target)


def _tiles(t, seq):
    tm = min(512, t)
    return dict(tm=tm, tm_small=min(256, t), tm_large=min(1024, t), tb=min(512, seq))


def local_step(x, target, depth, weights_of, seq, grads_done=None):
    t = x.shape[0]
    ts = _tiles(t, seq)
    tm, tl, tb = ts["tm"], ts["tm_large"], ts["tb"]
    saved, ws = [], []
    for l in range(depth):
        w = weights_of(l, x)
        ws.append(w)
        proj, h1 = norm_matmul(x, w["g1"], w["win"], tl, 1152, f32, "in_proj", token=w.get("token"))
        cat = group_a_fwd(proj, w["wa"], w["ga"], seq, tb, "group_a_fwd")
        xbcs = conv_b_fwd(proj, w["ws"], w["bs"], seq, tb, "conv_b_fwd")
        cat, ypre, states = ssd_fwd(xbcs, proj, w["par"], w["gs"], cat, seq, "ssd_fwd")
        if "late" in w:
            w.update(w.pop("late")(cat))
        mix, x2 = matmul_postnorm(cat, w["wo"], x, w["g2"], tl, False, "out_proj")
        fp, h2 = norm_matmul(x2, w["g3"], w["wu"], tl, 1024, bf16, "mlp_up")
        o, x3 = matmul_postnorm(fp, w["wd"], x2, w["g4"], tm, True, "mlp_down")
        saved.append(dict(x=x, proj=proj, h1=h1, xbcs=xbcs, ypre=ypre, states=states, cat=cat, mix=mix, x2=x2,
                          fp=fp, h2=h2, o=o))
        x = x3
    sse, dx = loss_head(x, target, tm, "loss_head")
    grads = [None] * depth
    for l in reversed(range(depth)):
        s, w = saved[l], ws[l]
        do, dg4, dfp = postnorm_bwd_matmul(s["o"], w["g4"], dx, w["wd"], s["fp"], tl, 1024, bf16, "mlp_down_bwd")
        dwd = matmul_tn(s["fp"], do, 512, 1024, True, "mlp_down_dw")
        dx2, dg3 = matmul_prenorm_bwd(dfp, w["wu"], s["x2"], w["g3"], dx, tm, "mlp_up_bwd")
        dwu = matmul_tn(s["h2"], dfp, 512, 1024, False, "mlp_up_dw", col_blocks=True)
        dmix, dg2, dcat = postnorm_bwd_matmul(s["mix"], w["g2"], dx2, w["wo"], None, tl, 1024, f32, "out_proj_bwd")
        dwo = matmul_tn(s["cat"], dmix, 512, 1024, False, "out_proj_dw")
        token = None if grads_done is None else grads_done(l, dict(wo=dwo, wu=dwu, wd=dwd), False)
        dproj, dwa, dga = group_a_bwd(s["proj"], dcat, w["wa"], w["ga"], seq, tb, "group_a_bwd", token=token)
        dxbcs, dproj, ddt, dpar, dgs = ssd_bwd(s["xbcs"], s["proj"], s["ypre"], s["states"], dcat, w["par"], w["gs"],
                                               dproj, seq, "ssd_bwd")
        dproj, dws, dbs = conv_b_bwd(s["proj"], dxbcs, w["ws"], w["bs"], dproj, seq, tb, "conv_b_bwd")
        dproj = place_columns(dproj, ddt, COL_DT // CH, tm, "place_ddt")
        dwin = matmul_tn(s["h1"], dproj, 512, 1152, False, "in_proj_dw")
        token = None if grads_done is None else grads_done(l, dict(win=dwin), True)
        dx, dg1 = matmul_prenorm_bwd(dproj, w["win"], s["x"], w["g1"], dx2, ts["tm_small"], "in_proj_bwd", token=token)
        grads[l] = dict(win=dwin, wo=dwo, wu=dwu, wd=dwd, wa=dwa, ws=dws, bs=dbs, par=dpar,
                        g1=dg1, ga=dga, gs=dgs, g2=dg2, g3=dg3, g4=dg4)
    return sse, dx, grads


GROUPS = {
    "chips": [(1, 0, 0), (0, 1, 0), (1, 1, 0)],
    "pair": [(0, 0, 1)],
    "all": [(1, 0, 0), (0, 1, 0), (1, 1, 0), (0, 0, 1), (1, 0, 1), (0, 1, 1), (1, 1, 1)],
}


def _group_index(group, x, y, c):
    return {"chips": 2 * x + y, "pair": c, "all": 4 * x + 2 * y + c}[group]


def _chunk_indices(shape, pieces):
    if len(shape) < 3:
        return [()]
    lead = [()]
    for n in shape[:-2]:
        lead = [i + (k,) for i in lead for k in range(n)]
    rows = shape[-2]
    split = max(1, pieces // len(lead))
    while split > 1 and (rows % split or (rows // split) % 16):
        split -= 1
    step = rows // split
    return [i + (pl.ds(s * step, step),) for i in lead for s in range(split)]


def _exchange(arrays, out_shapes, group, src_view, dst_view, view_shape, name, own, pieces=16):
    masks = GROUPS[group]
    na, nm = len(arrays), len(masks)
    cuts = [_chunk_indices(view_shape(a), pieces) for a in range(na)]

    def body(*refs):
        ins, outs = refs[:na], refs[na:2 * na]
        send_sems, recv_sems = refs[2 * na:2 * na + 2]
        local_sems = refs[2 * na + 2] if own else None
        x, y, c = lax.axis_index("x"), lax.axis_index("y"), lax.axis_index("c")
        me = _group_index(group, x, y, c)
        peers = []
        for mx, my, mc in masks:
            px, py, pc = (1 - x if mx else x), (1 - y if my else y), (1 - c if mc else c)
            peers.append(((px, py, pc), _group_index(group, px, py, pc)))

        def part(ref, idx):
            return ref.at[idx] if idx else ref

        if own:
            for a in range(na):
                for idx in cuts[a]:
                    pltpu.make_async_copy(part(src_view(ins[a], a, me), idx), part(dst_view(outs[a], a, me), idx),
                                          local_sems.at[a]).start()
        for a in range(na):
            for j, (dev, pidx) in enumerate(peers):
                for idx in cuts[a]:
                    pltpu.make_async_remote_copy(
                        src_ref=part(src_view(ins[a], a, pidx), idx), dst_ref=part(dst_view(outs[a], a, me), idx),
                        send_sem=send_sems.at[a * nm + j], recv_sem=recv_sems.at[a * nm + j],
                        device_id=dev, device_id_type=MESH).start()
        whole = []
        for a in range(na):
            for j, (dev, pidx) in enumerate(peers):
                whole.append(pltpu.make_async_remote_copy(
                    src_ref=src_view(ins[a], a, pidx), dst_ref=dst_view(outs[a], a, pidx),
                    send_sem=send_sems.at[a * nm + j], recv_sem=recv_sems.at[a * nm + j],
                    device_id=dev, device_id_type=MESH))
        for cp in whole:
            cp.wait_recv()
        for cp in whole:
            cp.wait_send()
        if own:
            for a in range(na):
                pltpu.make_async_copy(src_view(ins[a], a, me), dst_view(outs[a], a, me), local_sems.at[a]).wait()

    hbm = pl.BlockSpec(memory_space=pltpu.HBM)
    sems = [pltpu.SemaphoreType.DMA((na * nm,)), pltpu.SemaphoreType.DMA((na * nm,))]
    return pl.pallas_call(
        body, name=name, in_specs=[hbm] * na, out_specs=[hbm] * na,
        out_shape=[_sds(s, a.dtype) for s, a in zip(out_shapes, arrays)],
        scratch_shapes=sems + ([pltpu.SemaphoreType.DMA((na,))] if own else []))(*arrays)


def all_gather(arrays, group, name, slot_axis=0, own=True):
    n = len(GROUPS[group]) + 1
    shapes = [a.shape[:slot_axis] + (n,) + a.shape[slot_axis:] for a in arrays]
    lead = (slice(None),) * slot_axis
    return _exchange(arrays, shapes, group, lambda r, a, i: r, lambda r, a, i: r.at[lead + (i,)],
                     lambda a: arrays[a].shape, name, own)


HBM_SPEC = pl.BlockSpec(memory_space=pltpu.HBM)
SEM_SPEC = pl.BlockSpec(memory_space=pltpu.SEMAPHORE)
DATAFLOW = pltpu.SideEffectType.DATAFLOW_SIDE_EFFECTING
N_CHIPS = 4


def _chip_peers(x, y, c):
    out = []
    for mx, my, _ in GROUPS["chips"]:
        px, py = (1 - x if mx else x), (1 - y if my else y)
        out.append(((px, py, c), 2 * px + py))
    return out


def _weight_views(shards):
    half = [s.shape[0] // 2 for s in shards]
    return dict(src=lambda ref, a, c, to_chip: ref.at[pl.ds(c * half[a], half[a])],
                dst=lambda ref, a, c, from_chip: ref.at[from_chip, pl.ds(c * half[a], half[a])],
                rows=lambda a: half[a])


def _grad_views(sums):
    return dict(src=lambda ref, a, c, to_chip: ref.at[to_chip], dst=lambda ref, a, c, from_chip: ref.at[from_chip],
                rows=lambda a: sums[a].shape[1])


def chips_start(sources, zones, views, name, pieces=4, after=None):
    na, nm = len(sources), N_CHIPS - 1

    def body(*refs):
        ins, lands = refs[:na], refs[na:2 * na]
        n_in = 2 * na + len(_token_arg(after))
        send_sems, recv_sems, token = refs[n_in], refs[n_in + 1], refs[-1]
        x, y, c = lax.axis_index("x"), lax.axis_index("y"), lax.axis_index("c")
        chip = 2 * x + y
        for a in range(na):
            step = views["rows"](a) // pieces
            for j, (dev, to_chip) in enumerate(_chip_peers(x, y, c)):
                for q in range(pieces):
                    rows = pl.ds(q * step, step)
                    pltpu.make_async_remote_copy(
                        src_ref=views["src"](ins[a], a, c, to_chip).at[rows],
                        dst_ref=views["dst"](lands[a], a, c, chip).at[rows],
                        send_sem=send_sems.at[a * nm + j], recv_sem=recv_sems.at[a * nm + j],
                        device_id=dev, device_id_type=MESH).start()
        token[...] = jnp.zeros_like(token)

    both = list(sources) + list(zones)
    outs = pl.pallas_call(
        body, name=name,
        out_shape=(pltpu.SemaphoreType.DMA((na * nm,)), pltpu.SemaphoreType.DMA((na * nm,)),
                   *[pltpu.HBM(b.shape, b.dtype) for b in both], _sds((8, CH), f32)),
        in_specs=[HBM_SPEC] * (2 * na) + _token_spec(after),
        out_specs=(SEM_SPEC, SEM_SPEC, *[HBM_SPEC] * (2 * na), pl.BlockSpec(memory_space=pltpu.VMEM)),
        input_output_aliases={i: 2 + i for i in range(2 * na)},
        compiler_params=pltpu.CompilerParams(has_side_effects=DATAFLOW))(
            *[pltpu.with_memory_space_constraint(b, pltpu.HBM) for b in both], *_token_arg(after))
    return dict(send=outs[0], recv=outs[1], sources=list(outs[2:2 + na]), zones=list(outs[2 + na:2 + 2 * na]),
                token=outs[-1], views=views)


def chips_wait(started, after, name):
    sources, zones, views = started["sources"], started["zones"], started["views"]
    na, nm = len(sources), N_CHIPS - 1

    def body(*refs):
        ins, lands = refs[:na], refs[na:2 * na]
        send_sems, recv_sems = refs[2 * na], refs[2 * na + 1]
        x, y, c = lax.axis_index("x"), lax.axis_index("y"), lax.axis_index("c")
        for a in range(na):
            for j, (dev, peer_chip) in enumerate(_chip_peers(x, y, c)):
                cp = pltpu.make_async_remote_copy(
                    src_ref=views["src"](ins[a], a, c, peer_chip), dst_ref=views["dst"](lands[a], a, c, peer_chip),
                    send_sem=send_sems.at[a * nm + j], recv_sem=recv_sems.at[a * nm + j],
                    device_id=dev, device_id_type=MESH)
                cp.wait_send()
                cp.wait_recv()

    both = list(sources) + list(zones)
    outs = pl.pallas_call(
        body, name=name, out_shape=tuple(pltpu.HBM(b.shape, b.dtype) for b in both),
        in_specs=[HBM_SPEC] * (2 * na) + [SEM_SPEC, SEM_SPEC, pl.BlockSpec(memory_space=pl.ANY)],
        out_specs=tuple([HBM_SPEC] * (2 * na)), input_output_aliases={i: i for i in range(2 * na)},
        compiler_params=pltpu.CompilerParams(has_side_effects=DATAFLOW))(*both, started["send"], started["recv"], after)
    return list(outs[:na]), list(outs[na:])


def weights_share(zones, name):
    na, nm = len(zones), N_CHIPS - 1

    def body(*refs):
        lands = refs[na:2 * na]
        send_sems, recv_sems = refs[2 * na:]
        x, y, c = lax.axis_index("x"), lax.axis_index("y"), lax.axis_index("c")
        chip = 2 * x + y
        sibling = (x, y, 1 - c)
        sends = []
        for a in range(na):
            half = zones[a].shape[1] // 2
            for m in range(1, N_CHIPS):
                mine = lands[a].at[chip ^ m, pl.ds(c * half, half)]
                sends.append(pltpu.make_async_remote_copy(
                    src_ref=mine, dst_ref=mine, send_sem=send_sems.at[a * nm + m - 1],
                    recv_sem=recv_sems.at[a * nm + m - 1], device_id=sibling, device_id_type=MESH))
        for cp in sends:
            cp.start()
        for a in range(na):
            half = zones[a].shape[1] // 2
            for m in range(1, N_CHIPS):
                theirs = lands[a].at[chip ^ m, pl.ds((1 - c) * half, half)]
                pltpu.make_async_remote_copy(
                    src_ref=theirs, dst_ref=theirs, send_sem=send_sems.at[a * nm + m - 1],
                    recv_sem=recv_sems.at[a * nm + m - 1], device_id=sibling, device_id_type=MESH).wait_recv()
        for cp in sends:
            cp.wait_send()

    return pl.pallas_call(
        body, name=name, in_specs=[HBM_SPEC] * na, out_specs=[HBM_SPEC] * na,
        out_shape=[_sds(z.shape, z.dtype) for z in zones], input_output_aliases={i: i for i in range(na)},
        scratch_shapes=[pltpu.SemaphoreType.DMA((na * nm,)), pltpu.SemaphoreType.DMA((na * nm,))])(*zones)


def pair_send_halves(grads, name):
    half = [g.shape[1] // 2 for g in grads]
    shapes = [(g.shape[0], h, g.shape[2]) for g, h in zip(grads, half)]
    return _exchange(grads, shapes, "pair", lambda r, a, i: r.at[:, pl.ds(i * half[a], half[a])],
                     lambda r, a, i: r, lambda a: shapes[a], name, False)


def sum_pair_half(g, recv, core, name, tb=256, by_chip=None):
    nk, r, c = g.shape
    tb = min(tb, r // 2)
    nb = r // 2 // tb

    def body(core_ref, g_ref, r_ref, o_ref):
        s = g_ref[...] + r_ref[...]
        if by_chip is None:
            o_ref[...] = s.astype(bf16)
        else:
            for k in range(by_chip[0]):
                o_ref[k] = s[:, k * by_chip[1]:(k + 1) * by_chip[1]].astype(bf16)

    if by_chip is None:
        out_spec = pl.BlockSpec((None, tb, c), lambda k, i, core_ref: (k, i, 0))
        out_shape = _sds((nk, r // 2, c), bf16)
    else:
        assert nk == 1
        out_spec = pl.BlockSpec((by_chip[0], tb, by_chip[1]), lambda k, i, core_ref: (0, i, 0))
        out_shape = _sds((by_chip[0], r // 2, by_chip[1]), bf16)
    return pl.pallas_call(
        body, name=name,
        grid_spec=pltpu.PrefetchScalarGridSpec(
            num_scalar_prefetch=1, grid=(nk, nb),
            in_specs=[pl.BlockSpec((None, tb, c), lambda k, i, core_ref: (k, core_ref[0] * nb + i, 0)),
                      pl.BlockSpec((None, tb, c), lambda k, i, core_ref: (k, i, 0))],
            out_specs=out_spec),
        out_shape=out_shape, compiler_params=_cparams(2))(jnp.reshape(core, (1,)).astype(jnp.int32), g, recv)


def assemble_columns(blocks, width, name, tb=256):
    n, r, c = blocks.shape

    def body(b_ref, o_ref):
        for k in range(n):
            o_ref[:, k * c:(k + 1) * c] = b_ref[k]
        o_ref[:, n * c:] = jnp.zeros((tb, width - n * c), blocks.dtype)

    return pl.pallas_call(
        body, name=name, grid=(r // tb,), in_specs=[pl.BlockSpec((n, tb, c), lambda i: (0, i, 0))],
        out_specs=pl.BlockSpec((tb, width), lambda i: (i, 0)), out_shape=_sds((r, width), blocks.dtype),
        compiler_params=_cparams(1))(blocks)


def chip_sum_into(acc, layer, own, others, chip, name, tb=256):
    n, r, c = own.shape
    tb = min(tb, r)

    def body(chip_ref, x_ref, y1_ref, y2_ref, y3_ref, acc_ref, o_ref):
        o_ref[...] = ((x_ref[...].astype(f32) + y1_ref[...].astype(f32)) + y2_ref[...].astype(f32)) + y3_ref[...].astype(f32)

    def slot(k):
        return pl.BlockSpec((None, tb, c), lambda i, chip_ref: (chip_ref[0] ^ k, i, 0))

    return pl.pallas_call(
        body, name=name,
        grid_spec=pltpu.PrefetchScalarGridSpec(
            num_scalar_prefetch=1, grid=(r // tb,),
            in_specs=[slot(k) for k in range(n)] + [pl.BlockSpec(memory_space=pl.ANY)],
            out_specs=pl.BlockSpec((None, tb, c), lambda i, chip_ref: (layer, i, 0))),
        out_shape=_sds(acc.shape, f32), input_output_aliases={n + 1: 0}, compiler_params=_cparams(1))(
            jnp.reshape(chip, (1,)).astype(jnp.int32), own, *([others] * (n - 1)), acc)


def adamw_halves(w, g_own, g_recv, m, v, core, name, tb=256):
    depth, r, c = w.shape
    tb = min(tb, r // 2)
    nb = r // 2 // tb

    def body(core_ref, w_ref, go_ref, gr_ref, m_ref, v_ref, g_ref, d_ref, mo_ref, vo_ref):
        gv = jnp.where(pl.program_id(1) == core_ref[0], go_ref[...], gr_ref[...])
        m2 = B1 * m_ref[...] + (1.0 - B1) * gv
        v2 = B2 * v_ref[...] + (1.0 - B2) * (gv * gv)
        m_hat = m2 / (1.0 - B1 ** STEP)
        v_hat = v2 / (1.0 - B2 ** STEP)
        g_ref[...] = gv
        d_ref[...] = -LR * (m_hat / (jnp.sqrt(v_hat) + AEPS) + WD * w_ref[...])
        mo_ref[...] = m2
        vo_ref[...] = v2

    whole = pl.BlockSpec((None, tb, c), lambda l, h, i, core_ref: (l, h * nb + i, 0))
    part = pl.BlockSpec((None, tb, c), lambda l, h, i, core_ref: (l, i, 0))
    return pl.pallas_call(
        body, name=name,
        grid_spec=pltpu.PrefetchScalarGridSpec(num_scalar_prefetch=1, grid=(depth, 2, nb),
                                               in_specs=[whole, part, part, whole, whole], out_specs=[whole] * 4),
        out_shape=[_sds(w.shape, f32)] * 4, compiler_params=_cparams(3))(
            jnp.reshape(core, (1,)).astype(jnp.int32), w, g_own, g_recv, m, v)


def sum_slots(y, out_dtype, name, tb=256):
    n, r, c = y.shape
    tb = min(tb, r)

    def body(y_ref, o_ref):
        acc = y_ref[0].astype(f32)
        for i in range(1, n):
            acc = acc + y_ref[i].astype(f32)
        o_ref[...] = acc.astype(out_dtype)

    return pl.pallas_call(
        body, name=name, grid=(r // tb,),
        in_specs=[pl.BlockSpec((n, tb, c), lambda i: (0, i, 0))], out_specs=pl.BlockSpec((tb, c), lambda i: (i, 0)),
        out_shape=_sds((r, c), out_dtype), compiler_params=_cparams(1))(y)


def adamw(w, g, m, v, name, tb=256):
    r, c = w.shape
    tb = min(tb, r)

    def body(w_ref, g_ref, m_ref, v_ref, d_ref, mo_ref, vo_ref):
        gv = g_ref[...]
        m2 = B1 * m_ref[...] + (1.0 - B1) * gv
        v2 = B2 * v_ref[...] + (1.0 - B2) * (gv * gv)
        m_hat = m2 / (1.0 - B1 ** STEP)
        v_hat = v2 / (1.0 - B2 ** STEP)
        d_ref[...] = -LR * (m_hat / (jnp.sqrt(v_hat) + AEPS) + WD * w_ref[...])
        mo_ref[...] = m2
        vo_ref[...] = v2

    spec = pl.BlockSpec((tb, c), lambda i: (i, 0))
    return pl.pallas_call(
        body, name=name, grid=(r // tb,), in_specs=[spec] * 4, out_specs=[spec] * 3,
        out_shape=[_sds((r, c), f32)] * 3, compiler_params=_cparams(1))(w, g, m, v)


def adamw_leading(w, g, m, v, name, tc=64):
    c, l, r = w.shape
    main = c // tc
    tail = c - main * tc

    def body(w_ref, g_ref, m_ref, v_ref, *rest):
        d_ref, mo_ref, vo_ref = rest[-3:]
        gv = g_ref[...]
        m2 = B1 * m_ref[...] + (1.0 - B1) * gv
        v2 = B2 * v_ref[...] + (1.0 - B2) * (gv * gv)
        m_hat = m2 / (1.0 - B1 ** STEP)
        v_hat = v2 / (1.0 - B2 ** STEP)
        d_ref[...] = -LR * (m_hat / (jnp.sqrt(v_hat) + AEPS) + WD * w_ref[...])
        mo_ref[...] = m2
        vo_ref[...] = v2

    spec = pl.BlockSpec((tc, l, r), lambda i: (i, 0, 0))
    outs = pl.pallas_call(
        functools.partial(body), name=name, grid=(main,), in_specs=[spec] * 4, out_specs=[spec] * 3,
        out_shape=[_sds(w.shape, f32)] * 3, compiler_params=_cparams(1))(w, g, m, v)
    if tail:
        assert (main * tc) % tail == 0
        last = pl.BlockSpec((tail, l, r), lambda i: (main * tc // tail, 0, 0))
        outs = pl.pallas_call(
            functools.partial(body), name=name + "_tail", grid=(1,),
            in_specs=[last] * 4 + [pl.BlockSpec(memory_space=pl.ANY)] * 3, out_specs=[last] * 3,
            out_shape=[_sds(w.shape, f32)] * 3, input_output_aliases={4: 0, 5: 1, 6: 2},
            compiler_params=_cparams(1))(w, g, m, v, *outs)
    return outs


SMALL_ROW = 1024
SMALL_GAINS = ("g1", "ga", "gs", "g2", "g3", "g4")
SMALL_LAYER_ROWS = 8 + 8 + 16 + 8


def _pack_small(grads):
    wide = lambda a: jnp.pad(a, ((0, 0), (0, 2 * SMALL_ROW - a.shape[1]))).reshape(-1, SMALL_ROW)
    row = lax.broadcasted_iota(jnp.int32, (8, SMALL_ROW), 0)
    parts = []
    for g in grads:
        singles = [g[k] for k in SMALL_GAINS] + [g["bs"][:, :SMALL_ROW],
                                                 jnp.pad(g["bs"][:, SMALL_ROW:], ((0, 0), (0, 2 * SMALL_ROW - XBC)))]
        first = sum(jnp.where(row == k, s, 0.0) for k, s in enumerate(singles))
        parts += [first, g["wa"], wide(g["ws"]), jnp.pad(g["par"], ((0, 0), (0, SMALL_ROW - CH)))]
    return jnp.concatenate(parts, axis=0)


def _unpack_small(packed, depth):
    rows = packed.reshape(depth, SMALL_LAYER_ROWS, SMALL_ROW)
    out = {k: rows[:, i] for i, k in enumerate(SMALL_GAINS)}
    out["bs"] = rows[:, 6:8].reshape(depth, 2 * SMALL_ROW)[:, :XBC]
    out["wa"] = rows[:, 8:11]
    out["ws"] = rows[:, 16:32].reshape(depth, 8, 2 * SMALL_ROW)[:, :4, :XBC]
    out["par"] = rows[:, 32:35, :CH]
    return out


def kernel(x, norm_mix_pre, w_in, conv_a_w, ssm_conv_w, ssm_conv_b, dt_bias, a_log, d_skip, conv_out_norm, ssm_out_norm, w_out, norm_mix_post, norm_mlp_pre, w_up, w_down, norm_mlp_post, loss_target, m_norm_mix_pre, m_w_in, m_conv_a_w, m_ssm_conv_w, m_ssm_conv_b, m_dt_bias, m_a_log, m_d_skip, m_conv_out_norm, m_ssm_out_norm, m_w_out, m_norm_mix_post, m_norm_mlp_pre, m_w_up, m_w_down, m_norm_mlp_post, v_norm_mix_pre, v_w_in, v_conv_a_w, v_ssm_conv_w, v_ssm_conv_b, v_dt_bias, v_a_log, v_d_skip, v_conv_out_norm, v_ssm_out_norm, v_w_out, v_norm_mix_post, v_norm_mlp_pre, v_w_up, v_w_down, v_norm_mlp_post):
    nb, seq, _ = x.shape
    t = nb * seq
    depth = w_in.shape[0]
    ncol = w_in.shape[2]
    chip = 2 * lax.axis_index("x") + lax.axis_index("y")

    taps = [conv_a_w, ssm_conv_w]
    taps_g = all_gather(taps, "chips", "gather_taps", slot_axis=1, own=False)
    wa_g, ws_g = [lax.dynamic_update_index_in_dim(g, s, chip, 1) for g, s in zip(taps_g, taps)]
    wa_full = jnp.transpose(wa_g, (0, 2, 1, 3)).reshape(depth, 3, D)
    ws_full = jnp.transpose(ws_g, (0, 2, 1, 3)).reshape(depth, 4, XBC)
    lane_pad = lambda a: jnp.pad(a, ((0, 0), (0, CH - a.shape[1])))
    par = jnp.stack([lane_pad(dt_bias), lane_pad(a_log), lane_pad(d_skip)], axis=1)
    par = jnp.pad(par, ((0, 0), (0, 5), (0, 0)))

    layer_shards = lambda l: [w_in[l].astype(bf16), w_out[l].astype(bf16), w_up[l].astype(bf16), w_down[l].astype(bf16)]
    issued = []

    def start(shards, name):
        zones = [lax.empty((N_CHIPS,) + s.shape, s.dtype) for s in shards]
        issued.append(chips_start(shards, zones, _weight_views(shards), name,
                                  after=issued[-1]["token"] if issued else None))
        return issued[-1]

    def finish(started, after, name):
        shards, zones = chips_wait(started, after, name)
        zones = weights_share(zones, "weights_share")
        return [lax.dynamic_update_index_in_dim(z, s, chip, 0) for z, s in zip(zones, shards)]

    def shaped(mats):
        wo_z, wu_z, wd_z = mats
        return wo_z.reshape(2 * D, D), wu_z, wd_z.reshape(DFF, D)

    first = layer_shards(0)
    travelling = {0: start(first[:1], "weights_start_0")}
    rest = start(first[1:], "weights_start_0_rest")
    for l in range(1, depth):
        travelling[l] = start(layer_shards(l), f"weights_start_{l}")

    def weights_of(l, x_in):
        mats = finish(travelling.pop(l), x_in, f"weights_wait_{l}")
        w = dict(win=assemble_columns(mats[0], PROJ, "assemble_w_in"), wa=jnp.pad(wa_full[l], ((0, 5), (0, 0))), ws=jnp.pad(ws_full[l], ((0, 4), (0, 0))),
                 bs=ssm_conv_b[l][None], par=par[l], g1=norm_mix_pre[l][None], ga=conv_out_norm[l][None],
                 gs=ssm_out_norm[l][None], g2=norm_mix_post[l][None], g3=norm_mlp_pre[l][None],
                 g4=norm_mlp_post[l][None])
        if l == 0:
            w["late"] = lambda after: dict(zip(("wo", "wu", "wd"), shaped(finish(rest, after, "weights_wait_0_rest"))))
        else:
            w.update(zip(("wo", "wu", "wd"), shaped(mats[1:])))
        return w

    core = lax.axis_index("c")
    grads_travelling = {}

    chip_major = dict(win=lambda a: a[None], wo=lambda a: a.reshape(N_CHIPS, 2 * D // N_CHIPS, D), wu=lambda a: a,
                      wd=lambda a: a.reshape(N_CHIPS, DFF // N_CHIPS, D))
    held = {}

    def grads_done(l, g, last):
        if l > 0 and not last:
            held[l] = g
            return None
        g = {**held.pop(l, {}), **g}
        keys = [k for k in ("win", "wo", "wu", "wd") if k in g]
        mats = [chip_major[k](g[k]) for k in keys]
        received = pair_send_halves(mats, "grads_to_pair")
        sums = [sum_pair_half(m_, r_, core, "pair_sum", by_chip=(N_CHIPS, ncol) if k == "win" else None)
                for k, m_, r_ in zip(keys, mats, received)]
        zones = [lax.empty(s.shape, s.dtype) for s in sums]
        started = chips_start(sums, zones, _grad_views(sums), f"grads_start_{l}_{len(grads_travelling)}")
        grads_travelling[(l, keys[0])] = (keys, started)
        return started["token"]

    sse, dx, grads = local_step(x.reshape(t, D), loss_target.reshape(t, D), depth, weights_of, seq, grads_done)
    loss = lax.psum(0.5 / D * sse[0, 0], ("x", "y", "c"))

    small_all = all_gather([_pack_small(grads)], "all", "gather_small")[0]
    small_sum = sum_slots(small_all, f32, "small_sum", tb=8)
    small = _unpack_small(small_sum, depth)

    big_w = dict(win=w_in, wo=w_out, wu=w_up, wd=w_down)
    acc = {k: lax.empty((depth, bw.shape[1] // 2, bw.shape[2]), f32) for k, bw in big_w.items()}
    for n, ((l, _), (keys, started)) in enumerate(grads_travelling.items()):
        sums, zones = chips_wait(started, small_sum, f"grads_wait_{l}_{n}")
        for k, s, z in zip(keys, sums, zones):
            acc[k] = chip_sum_into(acc[k], l, s, z, chip, "chip_sum")
    acc = [acc[k] for k in ("win", "wo", "wu", "wd")]
    from_sibling = _exchange(acc, [a.shape for a in acc], "pair", lambda r, a, i: r, lambda r, a, i: r,
                             lambda a: acc[a].shape, "grads_from_pair", False)

    wa_cols, ws_cols = conv_a_w.shape[2], ssm_conv_w.shape[2]
    par_g = small["par"].reshape(depth, 3, CH)
    g_small = dict(
        norm_mix_pre=small["g1"], conv_out_norm=small["ga"], ssm_out_norm=small["gs"], norm_mix_post=small["g2"],
        norm_mlp_pre=small["g3"], norm_mlp_post=small["g4"], ssm_conv_b=small["bs"],
        conv_a_w=lax.dynamic_slice_in_dim(small["wa"].reshape(depth, 3, D), chip * wa_cols, wa_cols, axis=2),
        ssm_conv_w=lax.dynamic_slice_in_dim(small["ws"].reshape(depth, 4, XBC), chip * ws_cols, ws_cols, axis=2),
        dt_bias=par_g[:, 0, :NH], a_log=par_g[:, 1, :NH], d_skip=par_g[:, 2, :NH])

    given = dict(norm_mix_pre=(norm_mix_pre, m_norm_mix_pre, v_norm_mix_pre), w_in=(w_in, m_w_in, v_w_in),
                 conv_a_w=(conv_a_w, m_conv_a_w, v_conv_a_w), ssm_conv_w=(ssm_conv_w, m_ssm_conv_w, v_ssm_conv_w),
                 ssm_conv_b=(ssm_conv_b, m_ssm_conv_b, v_ssm_conv_b), dt_bias=(dt_bias, m_dt_bias, v_dt_bias),
                 a_log=(a_log, m_a_log, v_a_log), d_skip=(d_skip, m_d_skip, v_d_skip),
                 conv_out_norm=(conv_out_norm, m_conv_out_norm, v_conv_out_norm),
                 ssm_out_norm=(ssm_out_norm, m_ssm_out_norm, v_ssm_out_norm), w_out=(w_out, m_w_out, v_w_out),
                 norm_mix_post=(norm_mix_post, m_norm_mix_post, v_norm_mix_post),
                 norm_mlp_pre=(norm_mlp_pre, m_norm_mlp_pre, v_norm_mlp_pre), w_up=(w_up, m_w_up, v_w_up),
                 w_down=(w_down, m_w_down, v_w_down), norm_mlp_post=(norm_mlp_post, m_norm_mlp_post, v_norm_mlp_post))
    halves = dict(zip(["w_in", "w_out", "w_up", "w_down"], zip(acc, from_sibling)))
    order = ["norm_mix_pre", "w_in", "conv_a_w", "ssm_conv_w", "ssm_conv_b", "dt_bias", "a_log", "d_skip",
             "conv_out_norm", "ssm_out_norm", "w_out", "norm_mix_post", "norm_mlp_pre", "w_up", "w_down",
             "norm_mlp_post"]
    g_out, d_out, m_out, v_out = [], [], [], []
    for n in order:
        wv, mv, vv = given[n]
        if n in halves and wv.shape[-1] % CH:
            own, recv = halves[n]
            gv = jnp.concatenate([jnp.where(core == 0, own, recv), jnp.where(core == 0, recv, own)], axis=1)
            to_cols, to_rows = (lambda a: jnp.transpose(a, (2, 0, 1))), (lambda a: jnp.transpose(a, (1, 2, 0)))
            dlt, m2, v2 = [to_rows(o) for o in adamw_leading(to_cols(wv), to_cols(gv), to_cols(mv), to_cols(vv),
                                                             "adamw_cols")]
        elif n in halves:
            gv, dlt, m2, v2 = adamw_halves(wv, *halves[n], mv, vv, core, "adamw_matrix")
        else:
            gv = g_small[n].reshape(wv.shape)
            two_d = lambda a: a.reshape(-1, a.shape[-1])
            dlt, m2, v2 = adamw(two_d(wv), two_d(gv), two_d(mv), two_d(vv), "adamw")
        g_out.append(gv)
        d_out.append(dlt.reshape(wv.shape))
        m_out.append(m2.reshape(wv.shape))
        v_out.append(v2.reshape(wv.shape))
    return (loss, dx.reshape(nb, seq, D), *g_out, *d_out, *m_out, *v_out)
```

```python
import functools

import jax
import jax.numpy as jnp
from jax import lax
from jax.experimental import pallas as pl
from jax.experimental.pallas import tpu as pltpu

f32, bf16 = jnp.float32, jnp.bfloat16

D = 1024
NH, HP = 16, 64
NG, NS = 2, 128
CH = 128
XBC = D + 2 * NG * NS
DFF = 4 * D
IN_COLS = 3 * D + D + XBC + NH
PROJ = 5760
COL_Z, COL_XBC, COL_DT = 3 * D, 4 * D, 4 * D + XBC
EPS = 1e-6
HALO = 8
VMEM_LIMIT = 56 * 2**20
MESH = pl.DeviceIdType.MESH

LR, B1, B2, AEPS, WD, STEP = 0.001, 0.9, 0.999, 1e-08, 0.01, 10


def _cparams(n_axes):
    return pltpu.CompilerParams(dimension_semantics=("arbitrary",) * n_axes, vmem_limit_bytes=VMEM_LIMIT)


def _sds(shape, dtype):
    return jax.ShapeDtypeStruct(tuple(shape), dtype)


def _token_spec(token):
    return [] if token is None else [pl.BlockSpec(memory_space=pl.ANY)]


def _token_arg(token):
    return [] if token is None else [token]


def _rms_fwd(x, g):
    r = lax.rsqrt(jnp.mean(x * x, axis=-1, keepdims=True) + EPS)
    return x * r * g


def _rms_bwd(x, g, dy):
    r = lax.rsqrt(jnp.mean(x * x, axis=-1, keepdims=True) + EPS)
    xh = x * r
    gdy = dy * g
    dx = r * (gdy - xh * jnp.mean(xh * gdy, axis=-1, keepdims=True))
    return dx, dy * xh


def _accum(ref, part, first):
    @pl.when(first)
    def _():
        ref[...] = part

    @pl.when(jnp.logical_not(first))
    def _():
        ref[...] += part


def _dot_nt(a, b):
    return lax.dot_general(a, b, (((1,), (1,)), ((), ())), preferred_element_type=f32)


def _dot_tn(a, b):
    return lax.dot_general(a, b, (((0,), (0,)), ((), ())), preferred_element_type=f32)


def _dot(a, b):
    return jnp.dot(a, b, preferred_element_type=f32)


def _split_dot(x, e_bf, n_split, nt=False):
    acc = None
    rem = x
    for s in range(n_split):
        hi = rem.astype(bf16)
        term = _dot_nt(hi, e_bf) if nt else _dot(hi, e_bf)
        acc = term if acc is None else acc + term
        if s + 1 < n_split:
            rem = rem - hi.astype(f32)
    return acc


def _sigmoid(x):
    return 0.5 * jnp.tanh(0.5 * x) + 0.5


def norm_matmul(x, g, w, tm, tn, out_dtype, name, token=None):
    t = x.shape[0]
    if w.ndim == 3:
        assert w.shape[2] == tn
        n = w.shape[0] * tn
        w_spec = pl.BlockSpec((None, D, tn), lambda i, j: (j, 0, 0))
    else:
        n = w.shape[1]
        w_spec = pl.BlockSpec((D, tn), lambda i, j: (0, j))

    def body(x_ref, g_ref, w_ref, *rest):
        o_ref, h_ref = rest[-2:]

        @pl.when(pl.program_id(1) == 0)
        def _():
            h_ref[...] = _rms_fwd(x_ref[...], g_ref[...]).astype(bf16)

        o_ref[...] = _dot(h_ref[...], w_ref[...]).astype(out_dtype)

    return pl.pallas_call(
        body, name=name, grid=(t // tm, n // tn),
        in_specs=[pl.BlockSpec((tm, D), lambda i, j: (i, 0)), pl.BlockSpec((1, D), lambda i, j: (0, 0)), w_spec]
        + _token_spec(token),
        out_specs=[pl.BlockSpec((tm, tn), lambda i, j: (i, j)), pl.BlockSpec((tm, D), lambda i, j: (i, 0))],
        out_shape=[_sds((t, n), out_dtype), _sds((t, D), bf16)],
        compiler_params=_cparams(2))(x, g, w, *_token_arg(token))


def matmul_postnorm(a, w, xres, g, tm, relu2, name):
    t, k = a.shape

    def body(a_ref, w_ref, xr_ref, g_ref, y_ref, xo_ref):
        av = a_ref[...]
        if relu2:
            af = jnp.maximum(av.astype(f32), 0.0)
            av = (af * af).astype(bf16)
        y = _dot(av, w_ref[...])
        y_ref[...] = y
        xo_ref[...] = xr_ref[...] + _rms_fwd(y, g_ref[...])

    return pl.pallas_call(
        body, name=name, grid=(t // tm,),
        in_specs=[pl.BlockSpec((tm, k), lambda i: (i, 0)), pl.BlockSpec((k, D), lambda i: (0, 0)),
                  pl.BlockSpec((tm, D), lambda i: (i, 0)), pl.BlockSpec((1, D), lambda i: (0, 0))],
        out_specs=[pl.BlockSpec((tm, D), lambda i: (i, 0)), pl.BlockSpec((tm, D), lambda i: (i, 0))],
        out_shape=[_sds((t, D), f32), _sds((t, D), f32)],
        compiler_params=_cparams(1))(a, w, xres, g)


def postnorm_bwd_matmul(y, g, dxo, w, fp, tm, tn, out_dtype, name, token=None):
    t, n = y.shape[0], w.shape[0]
    relu = fp is not None

    def body(*refs):
        y_ref, g_ref, dxo_ref, w_ref = refs[:4]
        fp_ref = refs[4] if relu else None
        dy_ref, dg_ref, da_ref = refs[-3:]
        i, j = pl.program_id(0), pl.program_id(1)

        @pl.when(j == 0)
        def _():
            dx, dgc = _rms_bwd(y_ref[...], g_ref[...], dxo_ref[...])
            dy_ref[...] = dx.astype(bf16)
            _accum(dg_ref, jnp.sum(dgc, axis=0, keepdims=True), i == 0)

        da = _dot_nt(dy_ref[...], w_ref[...])
        if relu:
            da = da * (2.0 * jnp.maximum(fp_ref[...].astype(f32), 0.0))
        da_ref[...] = da.astype(out_dtype)

    in_specs = [pl.BlockSpec((tm, D), lambda i, j: (i, 0)), pl.BlockSpec((1, D), lambda i, j: (0, 0)),
                pl.BlockSpec((tm, D), lambda i, j: (i, 0)), pl.BlockSpec((tn, D), lambda i, j: (j, 0))]
    args = [y, g, dxo, w]
    if relu:
        in_specs.append(pl.BlockSpec((tm, tn), lambda i, j: (i, j)))
        args.append(fp)
    in_specs += _token_spec(token)
    args += _token_arg(token)
    return pl.pallas_call(
        body, name=name, grid=(t // tm, n // tn), in_specs=in_specs,
        out_specs=[pl.BlockSpec((tm, D), lambda i, j: (i, 0)), pl.BlockSpec((1, D), lambda i, j: (0, 0)),
                   pl.BlockSpec((tm, tn), lambda i, j: (i, j))],
        out_shape=[_sds((t, D), bf16), _sds((1, D), f32), _sds((t, n), out_dtype)],
        compiler_params=_cparams(2))(*args)


def matmul_prenorm_bwd(da, w, x, g, dxo, tm, name, token=None):
    t, k = da.shape
    blocked = w.ndim == 3

    def body(da_ref, w_ref, x_ref, g_ref, dxo_ref, *rest):
        dx_ref, dg_ref = rest[-2:]
        if blocked:
            kc = w.shape[2]
            dh = _dot_nt(da_ref[:, 0:kc], w_ref[0])
            for q in range(1, w.shape[0]):
                dh = dh + _dot_nt(da_ref[:, q * kc:(q + 1) * kc], w_ref[q])
        else:
            dh = _dot_nt(da_ref[...], w_ref[...])
        dxn, dgc = _rms_bwd(x_ref[...], g_ref[...], dh)
        dx_ref[...] = dxo_ref[...] + dxn
        _accum(dg_ref, jnp.sum(dgc, axis=0, keepdims=True), pl.program_id(0) == 0)

    w_spec = pl.BlockSpec(w.shape, (lambda i: (0, 0, 0)) if blocked else (lambda i: (0, 0)))
    return pl.pallas_call(
        body, name=name, grid=(t // tm,),
        in_specs=[pl.BlockSpec((tm, k), lambda i: (i, 0)), w_spec,
                  pl.BlockSpec((tm, D), lambda i: (i, 0)), pl.BlockSpec((1, D), lambda i: (0, 0)),
                  pl.BlockSpec((tm, D), lambda i: (i, 0))] + _token_spec(token),
        out_specs=[pl.BlockSpec((tm, D), lambda i: (i, 0)), pl.BlockSpec((1, D), lambda i: (0, 0))],
        out_shape=[_sds((t, D), f32), _sds((1, D), f32)],
        compiler_params=_cparams(1))(da, w, x, g, dxo, *_token_arg(token))


def matmul_tn(a, b, tm, tn, relu2, name, col_blocks=False):
    t, m = a.shape
    n = b.shape[1]
    if col_blocks:
        out_spec, out_shape = pl.BlockSpec((None, tm, tn), lambda i, j: (j, i, 0)), _sds((n // tn, m, tn), f32)
    else:
        out_spec, out_shape = pl.BlockSpec((tm, tn), lambda i, j: (i, j)), _sds((m, n), f32)

    def body(a_ref, b_ref, o_ref, at_ref):
        @pl.when(pl.program_id(1) == 0)
        def _():
            av = a_ref[...]
            if relu2:
                af = jnp.maximum(av.astype(f32), 0.0)
                av = (af * af).astype(bf16)
            at_ref[...] = av.T

        o_ref[...] = _dot(at_ref[...], b_ref[...])

    return pl.pallas_call(
        body, name=name, grid=(m // tm, n // tn),
        in_specs=[pl.BlockSpec((t, tm), lambda i, j: (0, i)), pl.BlockSpec((t, tn), lambda i, j: (0, j))],
        out_specs=out_spec, out_shape=out_shape,
        scratch_shapes=[pltpu.VMEM((tm, t), bf16)],
        compiler_params=_cparams(2))(a, b)


ROWS_A = 16
ROWS_B = 32
UNROLL = 4


def _past(win, s):
    return (win if s == 0 else pltpu.roll(win, s, 0))[HALO:]


def _future(win, s):
    n = win.shape[0]
    return (win if s == 0 else pltpu.roll(win, n - s, 0))[:n - HALO]


def _fold8(v):
    return v.reshape(v.shape[0] // 8, 8, v.shape[1]).sum(axis=0)


def _halo_prev(tb, col):
    return lambda i: (jnp.maximum(i * (tb // HALO) - 1, 0), col)


def _halo_next(tb, col, t):
    return lambda i: (jnp.minimum((i + 1) * (tb // HALO), t // HALO - 1), col)


def group_a_fwd(proj, wa, g, seq, tb, name):
    t = proj.shape[0]
    bps = seq // tb

    def body(xa_ref, ca_ref, ba_ref, xah_ref, cah_ref, wa_ref, g_ref, o_ref, u_scr):
        first = (pl.program_id(0) % bps) == 0
        u_scr[0:HALO, :] = jnp.where(first, 0.0, cah_ref[...] * xah_ref[...])
        w, gv = wa_ref[...], g_ref[...]

        def chunk(i, carry):
            r = pl.multiple_of(i * ROWS_A, ROWS_A)
            rows = pl.ds(r, ROWS_A)
            u_scr[pl.ds(pl.multiple_of(HALO + r, HALO), ROWS_A), :] = ca_ref[rows, :] * xa_ref[rows, :]
            win = u_scr[pl.ds(r, ROWS_A + HALO), :]
            cv = w[2:3] * _past(win, 0) + w[1:2] * _past(win, 1) + w[0:1] * _past(win, 2)
            o_ref[rows, :] = _rms_fwd(ba_ref[rows, :] * cv, gv).astype(bf16)
            return carry

        lax.fori_loop(0, tb // ROWS_A, chunk, 0, unroll=UNROLL)

    blk = lambda c: pl.BlockSpec((tb, D), lambda i: (i, c))
    return pl.pallas_call(
        body, name=name, grid=(t // tb,),
        in_specs=[blk(0), blk(1), blk(2),
                  pl.BlockSpec((HALO, D), _halo_prev(tb, 0)), pl.BlockSpec((HALO, D), _halo_prev(tb, 1)),
                  pl.BlockSpec((8, D), lambda i: (0, 0)), pl.BlockSpec((1, D), lambda i: (0, 0))],
        out_specs=pl.BlockSpec((tb, D), lambda i: (i, 0)),
        out_shape=_sds((t, 2 * D), bf16),
        scratch_shapes=[pltpu.VMEM((tb + HALO, D), f32)],
        compiler_params=_cparams(1))(proj, proj, proj, proj, proj, wa, g)


def group_a_bwd(proj, dcat, wa, g, seq, tb, name, token=None):
    t = proj.shape[0]
    bps = seq // tb

    def body(xa_ref, ca_ref, ba_ref, dy_ref, xap_ref, cap_ref, xan_ref, can_ref, ban_ref, dyn_ref, wa_ref, g_ref,
             *rest):
        dp_ref, dwa_ref, dg_ref, u_scr, d_scr, acc_scr = rest[-6:]
        i = pl.program_id(0)
        first = (i % bps) == 0
        last = (i % bps) == bps - 1
        w = wa_ref[...]
        gv = g_ref[...]
        u_scr[0:HALO, :] = jnp.where(first, 0.0, cap_ref[...] * xap_ref[...])
        u_scr[HALO + tb:2 * HALO + tb, :] = can_ref[...] * xan_ref[...]
        acc_scr[...] = jnp.zeros_like(acc_scr)

        def forward_part(n, carry):
            r = pl.multiple_of(n * ROWS_A, ROWS_A)
            rows = pl.ds(r, ROWS_A)
            ba = ba_ref[rows, :]
            u_scr[pl.ds(pl.multiple_of(HALO + r, HALO), ROWS_A), :] = ca_ref[rows, :] * xa_ref[rows, :]
            win = u_scr[pl.ds(r, ROWS_A + HALO), :]
            u = [_past(win, s) for s in range(3)]
            cv = w[2:3] * u[0] + w[1:2] * u[1] + w[0:1] * u[2]
            dya, dgc = _rms_bwd(ba * cv, gv, dy_ref[rows, :])
            dcv = dya * ba
            d_scr[rows, :] = dcv
            dp_ref[rows, 2 * D:3 * D] = (dya * cv).astype(bf16)
            acc_scr[0:8, :] += _fold8(dgc)
            for k in range(3):
                acc_scr[8 + 8 * k:16 + 8 * k, :] += _fold8(dcv * u[2 - k])
            return carry

        lax.fori_loop(0, tb // ROWS_A, forward_part, 0, unroll=UNROLL)

        start = HALO + tb
        cvn = (w[2:3] * u_scr[pl.ds(start, HALO), :] + w[1:2] * u_scr[pl.ds(start - 1, HALO), :]
               + w[0:1] * u_scr[pl.ds(start - 2, HALO), :])
        ban = ban_ref[...]
        dyan, _ = _rms_bwd(ban * cvn, gv, dyn_ref[...])
        d_scr[tb:tb + HALO, :] = jnp.where(last, 0.0, dyan * ban)

        def backward_part(n, carry):
            r = pl.multiple_of(n * ROWS_A, ROWS_A)
            rows = pl.ds(r, ROWS_A)
            win = d_scr[pl.ds(r, ROWS_A + HALO), :]
            du = w[2:3] * _future(win, 0) + w[1:2] * _future(win, 1) + w[0:1] * _future(win, 2)
            dp_ref[rows, 0:D] = (du * ca_ref[rows, :]).astype(bf16)
            dp_ref[rows, D:2 * D] = (du * xa_ref[rows, :]).astype(bf16)
            return carry

        lax.fori_loop(0, tb // ROWS_A, backward_part, 0, unroll=UNROLL)

        row = lax.broadcasted_iota(jnp.int32, (8, D), 0)
        dw = jnp.zeros((8, D), f32)
        for k in range(3):
            dw = jnp.where(row == k, jnp.sum(acc_scr[8 + 8 * k:16 + 8 * k, :], axis=0, keepdims=True), dw)
        _accum(dwa_ref, dw, i == 0)
        _accum(dg_ref, jnp.sum(acc_scr[0:8, :], axis=0, keepdims=True), i == 0)

    blk = lambda c: pl.BlockSpec((tb, D), lambda i: (i, c))
    prv = lambda c: pl.BlockSpec((HALO, D), _halo_prev(tb, c))
    nxt = lambda c: pl.BlockSpec((HALO, D), _halo_next(tb, c, t))
    return pl.pallas_call(
        body, name=name, grid=(t // tb,),
        in_specs=[blk(0), blk(1), blk(2), blk(0), prv(0), prv(1), nxt(0), nxt(1), nxt(2), nxt(0),
                  pl.BlockSpec((8, D), lambda i: (0, 0)), pl.BlockSpec((1, D), lambda i: (0, 0))] + _token_spec(token),
        out_specs=[pl.BlockSpec((tb, 3 * D), lambda i: (i, 0)), pl.BlockSpec((8, D), lambda i: (0, 0)),
                   pl.BlockSpec((1, D), lambda i: (0, 0))],
        out_shape=[_sds((t, PROJ), bf16), _sds((8, D), f32), _sds((1, D), f32)],
        scratch_shapes=[pltpu.VMEM((tb + 2 * HALO, D), f32), pltpu.VMEM((tb + HALO, D), f32), pltpu.VMEM((32, D), f32)],
        compiler_params=_cparams(1))(proj, proj, proj, dcat, proj, proj, proj, proj, proj, dcat, wa, g,
                                     *_token_arg(token))


CB = 512
XBC_BLK0 = COL_XBC // CB


def conv_b_fwd(proj, ws, bs, seq, tb, name):
    t = proj.shape[0]
    bps = seq // tb

    def body(x_ref, xp_ref, w_ref, b_ref, o_ref, x_scr):
        first = (pl.program_id(1) % bps) == 0
        x_scr[0:HALO, :] = jnp.where(first, 0.0, xp_ref[...])
        w, bias = w_ref[...], b_ref[...]

        def chunk(n, carry):
            r = pl.multiple_of(n * ROWS_B, ROWS_B)
            rows = pl.ds(r, ROWS_B)
            x_scr[pl.ds(pl.multiple_of(HALO + r, HALO), ROWS_B), :] = x_ref[rows, :]
            win = x_scr[pl.ds(r, ROWS_B + HALO), :]
            xc = bias + w[3:4] * _past(win, 0)
            for k in range(3):
                xc = xc + w[k:k + 1] * _past(win, 3 - k)
            o_ref[rows, :] = xc * _sigmoid(xc)
            return carry

        lax.fori_loop(0, tb // ROWS_B, chunk, 0, unroll=UNROLL)

    return pl.pallas_call(
        body, name=name, grid=(XBC // CB, t // tb),
        in_specs=[pl.BlockSpec((tb, CB), lambda j, i: (i, XBC_BLK0 + j)),
                  pl.BlockSpec((HALO, CB), lambda j, i: (jnp.maximum(i * (tb // HALO) - 1, 0), XBC_BLK0 + j)),
                  pl.BlockSpec((8, CB), lambda j, i: (0, j)), pl.BlockSpec((1, CB), lambda j, i: (0, j))],
        out_specs=pl.BlockSpec((tb, CB), lambda j, i: (i, j)),
        out_shape=_sds((t, XBC), f32),
        scratch_shapes=[pltpu.VMEM((tb + HALO, CB), f32)],
        compiler_params=_cparams(2))(proj, proj, ws, bs)


def conv_b_bwd(proj, dxs, ws, bs, dproj, seq, tb, name):
    t = proj.shape[0]
    bps = seq // tb

    def body(x_ref, xp_ref, xn_ref, d_ref, dn_ref, w_ref, b_ref, dproj_ref, dx_ref, dw_ref, db_ref, x_scr, d_scr,
             acc_scr):
        i = pl.program_id(1)
        first = (i % bps) == 0
        last = (i % bps) == bps - 1
        w = w_ref[...]
        bias = b_ref[...]
        x_scr[0:HALO, :] = jnp.where(first, 0.0, xp_ref[...])
        x_scr[HALO + tb:2 * HALO + tb, :] = xn_ref[...]
        acc_scr[...] = jnp.zeros_like(acc_scr)

        def dsilu(xc, d):
            sg = _sigmoid(xc)
            return d * (sg * (1.0 + xc * (1.0 - sg)))

        def forward_part(n, carry):
            r = pl.multiple_of(n * ROWS_B, ROWS_B)
            rows = pl.ds(r, ROWS_B)
            x_scr[pl.ds(pl.multiple_of(HALO + r, HALO), ROWS_B), :] = x_ref[rows, :]
            win = x_scr[pl.ds(r, ROWS_B + HALO), :]
            xs = [_past(win, s) for s in range(4)]
            xc = bias + w[3:4] * xs[0]
            for k in range(3):
                xc = xc + w[k:k + 1] * xs[3 - k]
            dxc = dsilu(xc, d_ref[rows, :])
            d_scr[rows, :] = dxc
            acc_scr[0:8, :] += _fold8(dxc)
            for k in range(4):
                acc_scr[8 + 8 * k:16 + 8 * k, :] += _fold8(dxc * xs[3 - k])
            return carry

        lax.fori_loop(0, tb // ROWS_B, forward_part, 0, unroll=UNROLL)

        start = HALO + tb
        xcn = bias + w[3:4] * x_scr[pl.ds(start, HALO), :]
        for k in range(3):
            xcn = xcn + w[k:k + 1] * x_scr[pl.ds(start - 3 + k, HALO), :]
        d_scr[tb:tb + HALO, :] = jnp.where(last, 0.0, dsilu(xcn, dn_ref[...]))

        def backward_part(n, carry):
            r = pl.multiple_of(n * ROWS_B, ROWS_B)
            win = d_scr[pl.ds(r, ROWS_B + HALO), :]
            dx = w[3:4] * _future(win, 0)
            for k in range(3):
                dx = dx + w[k:k + 1] * _future(win, 3 - k)
            dx_ref[pl.ds(r, ROWS_B), :] = dx.astype(bf16)
            return carry

        lax.fori_loop(0, tb // ROWS_B, backward_part, 0, unroll=UNROLL)

        row = lax.broadcasted_iota(jnp.int32, (8, CB), 0)
        dw = jnp.zeros((8, CB), f32)
        for k in range(4):
            dw = jnp.where(row == k, jnp.sum(acc_scr[8 + 8 * k:16 + 8 * k, :], axis=0, keepdims=True), dw)
        _accum(dw_ref, dw, i == 0)
        _accum(db_ref, jnp.sum(acc_scr[0:8, :], axis=0, keepdims=True), i == 0)

    nh = t // HALO
    return pl.pallas_call(
        body, name=name, grid=(XBC // CB, t // tb),
        in_specs=[pl.BlockSpec((tb, CB), lambda j, i: (i, XBC_BLK0 + j)),
                  pl.BlockSpec((HALO, CB), lambda j, i: (jnp.maximum(i * (tb // HALO) - 1, 0), XBC_BLK0 + j)),
                  pl.BlockSpec((HALO, CB), lambda j, i: (jnp.minimum((i + 1) * (tb // HALO), nh - 1), XBC_BLK0 + j)),
                  pl.BlockSpec((tb, CB), lambda j, i: (i, j)),
                  pl.BlockSpec((HALO, CB), lambda j, i: (jnp.minimum((i + 1) * (tb // HALO), nh - 1), j)),
                  pl.BlockSpec((8, CB), lambda j, i: (0, j)), pl.BlockSpec((1, CB), lambda j, i: (0, j)),
                  pl.BlockSpec(memory_space=pl.ANY)],
        out_specs=[pl.BlockSpec((tb, CB), lambda j, i: (i, XBC_BLK0 + j)), pl.BlockSpec((8, CB), lambda j, i: (0, j)),
                   pl.BlockSpec((1, CB), lambda j, i: (0, j))],
        out_shape=[_sds((t, PROJ), bf16), _sds((8, XBC), f32), _sds((1, XBC), f32)],
        input_output_aliases={7: 0},
        scratch_shapes=[pltpu.VMEM((tb + 2 * HALO, CB), f32), pltpu.VMEM((tb + HALO, CB), f32),
                        pltpu.VMEM((40, CB), f32)],
        compiler_params=_cparams(2))(proj, proj, proj, dxs, dxs, ws, bs, dproj)


def place_columns(buf, part, col_block, tb, name):
    t, wdt = part.shape

    def body(p_ref, buf_ref, o_ref):
        o_ref[...] = p_ref[...]

    return pl.pallas_call(
        body, name=name, grid=(t // tb,),
        in_specs=[pl.BlockSpec((tb, wdt), lambda i: (i, 0)), pl.BlockSpec(memory_space=pl.ANY)],
        out_specs=pl.BlockSpec((tb, wdt), lambda i: (i, col_block)), out_shape=_sds(buf.shape, buf.dtype),
        input_output_aliases={1: 0}, compiler_params=_cparams(1))(part, buf)


GW = D // NG
EXPAND_TERMS = 2
REDUCE_TERMS = 1


def _ssd_consts():
    head_of_lane = jnp.arange(D) // HP
    expand = (jnp.arange(CH)[:, None] == head_of_lane[None, :]).astype(bf16)
    tri = (jnp.arange(CH)[:, None] >= jnp.arange(CH)[None, :]).astype(f32)
    return expand, tri


def _ssd_common(par_ref, dtr_ref, e_ref, tri_ref):
    par = par_ref[...]
    dtb, alog, dsk = par[0:1], par[1:2], par[2:3]
    lane = lax.broadcasted_iota(jnp.int32, (CH, CH), 1)
    a = -jnp.exp(alog)
    dtr = dtr_ref[...] + dtb
    sp = jnp.maximum(dtr, 0.0) + jnp.log(1.0 + jnp.exp(-jnp.abs(dtr)))
    dt = jnp.where(lane < NH, sp, 0.0)
    cs = jnp.dot(tri_ref[...], dt * a, precision=lax.Precision.HIGHEST, preferred_element_type=f32)
    cs_last = cs[CH - 1:CH, :]
    dte = jnp.exp(cs_last - cs)
    ecs = jnp.exp(cs)
    ecl = jnp.exp(cs_last)
    e = e_ref[...]
    row8 = lax.broadcasted_iota(jnp.int32, (8, CH), 0)
    r8 = _split_dot(jnp.where(row8 == 0, ecl, jnp.where(row8 == 1, dsk, 0.0)), e, 3)
    return dict(a=a, dtr=dtr, dt=dt, cs=cs, cst=cs.T, dte=dte, ecs=ecs, ecl=ecl, e=e, lane=lane,
                dt_x=_split_dot(dt, e, EXPAND_TERMS), dte_x=_split_dot(dte, e, EXPAND_TERMS),
                ecs_x=_split_dot(ecs, e, EXPAND_TERMS),
                ecl_x=r8[0:1], dsk_x=r8[1:2])


def _decay_matrix(c, h):
    li = lax.broadcasted_iota(jnp.int32, (CH, CH), 0)
    seg = c["cs"][:, h:h + 1] - c["cst"][h:h + 1, :]
    return jnp.exp(jnp.where(li >= c["lane"], seg, -jnp.inf))


def _gate_norm_fwd(y, z, gs):
    zg = z * _sigmoid(z)
    yg = y * zg
    return jnp.concatenate([_rms_fwd(yg[:, k * GW:(k + 1) * GW], gs[:, k * GW:(k + 1) * GW]) for k in range(NG)], axis=1)


def ssd_fwd(xbcs, proj, par, gs, cat, seq, name):
    t = xbcs.shape[0]
    nc = seq // CH
    expand, tri = _ssd_consts()

    def body(xs_ref, b_ref, c_ref, dtr_ref, z_ref, par_ref, e_ref, tri_ref, gs_ref, cat_ref, yn_ref, y_ref, st_ref,
             p_scr, yd_scr):
        @pl.when(pl.program_id(0) % nc == 0)
        def _():
            p_scr[...] = jnp.zeros_like(p_scr)

        c = _ssd_common(par_ref, dtr_ref, e_ref, tri_ref)
        xs = xs_ref[...]
        xdt = xs * c["dt_x"]
        xdt_b = xdt.astype(bf16)
        xdte_b = (xdt * c["dte_x"]).astype(bf16)
        p = p_scr[...]
        st_ref[0] = p
        p_b = p.astype(bf16)
        lo = c["lane"] < HP
        for g in range(NG):
            bg = b_ref[:, g * NS:(g + 1) * NS].astype(bf16)
            cg = c_ref[:, g * NS:(g + 1) * NS].astype(bf16)
            gmat = _dot_nt(cg, bg)
            for q in range(GW // CH):
                col = g * GW + q * CH
                xp = xdt_b[:, col:col + CH]
                h0 = col // HP
                m0 = (gmat * _decay_matrix(c, h0)).astype(bf16)
                m1 = (gmat * _decay_matrix(c, h0 + 1)).astype(bf16)
                stacked = jnp.concatenate([jnp.where(lo, xp, jnp.zeros_like(xp)),
                                           jnp.where(lo, jnp.zeros_like(xp), xp)], axis=0)
                yd_scr[:, col:col + CH] = _dot(jnp.concatenate([m0, m1], axis=1), stacked)
            gsl = slice(g * GW, (g + 1) * GW)
            yoff = _dot(cg, p_b[:, gsl]) * c["ecs_x"][:, gsl]
            yd_scr[:, gsl] = yd_scr[:, gsl] + yoff
            p_scr[:, gsl] = p[:, gsl] * c["ecl_x"][:, gsl] + _dot_tn(bg, xdte_b[:, gsl])
        y = yd_scr[...] + c["dsk_x"] * xs
        y_ref[...] = y
        yn_ref[...] = _gate_norm_fwd(y, z_ref[...], gs_ref[...]).astype(bf16)

    nb = t // CH
    return pl.pallas_call(
        body, name=name, grid=(nb,),
        in_specs=[pl.BlockSpec((CH, D), lambda i: (i, 0)),
                  pl.BlockSpec((CH, NG * NS), lambda i: (i, D // (NG * NS))),
                  pl.BlockSpec((CH, NG * NS), lambda i: (i, D // (NG * NS) + 1)),
                  pl.BlockSpec((CH, CH), lambda i: (i, COL_DT // CH)),
                  pl.BlockSpec((CH, D), lambda i: (i, COL_Z // D)),
                  pl.BlockSpec((8, CH), lambda i: (0, 0)), pl.BlockSpec((CH, D), lambda i: (0, 0)),
                  pl.BlockSpec((CH, CH), lambda i: (0, 0)), pl.BlockSpec((1, D), lambda i: (0, 0)),
                  pl.BlockSpec(memory_space=pl.ANY)],
        out_specs=[pl.BlockSpec((CH, D), lambda i: (i, 1)), pl.BlockSpec((CH, D), lambda i: (i, 0)),
                   pl.BlockSpec((1, NS, D), lambda i: (i, 0, 0))],
        out_shape=[_sds((t, 2 * D), bf16), _sds((t, D), f32), _sds((nb, NS, D), f32)],
        input_output_aliases={9: 0},
        scratch_shapes=[pltpu.VMEM((NS, D), f32), pltpu.VMEM((CH, D), f32)],
        compiler_params=_cparams(1))(xbcs, xbcs, xbcs, proj, proj, par, expand, tri, gs, cat)


def ssd_bwd(xbcs, proj, ypre, states, dcat, par, gs, dproj, seq, name):
    t = xbcs.shape[0]
    nc = seq // CH
    expand, tri = _ssd_consts()

    def body(xs_ref, b_ref, c_ref, dtr_ref, z_ref, y_ref, st_ref, dyn_ref, par_ref, e_ref, tri_ref, gs_ref, dproj_ref,
             dx_ref, dz_ref, ddt_ref, dpar_ref, dgs_ref, dp_scr, dxdt_scr):
        i = pl.program_id(0)

        @pl.when(i % nc == 0)
        def _():
            dp_scr[...] = jnp.zeros_like(dp_scr)

        c = _ssd_common(par_ref, dtr_ref, e_ref, tri_ref)
        e = c["e"]
        lane = c["lane"]
        sub = lax.broadcasted_iota(jnp.int32, (CH, CH), 0)
        xs = xs_ref[...]
        xdt = xs * c["dt_x"]
        xdt_b = xdt.astype(bf16)
        xdte_b = (xdt * c["dte_x"]).astype(bf16)
        p = st_ref[0]
        p_b = p.astype(bf16)
        dpn = dp_scr[...]
        dpn_b = dpn.astype(bf16)

        y, z, gs_v = y_ref[...], z_ref[...], gs_ref[...]
        zs = _sigmoid(z)
        zg = z * zs
        yg = y * zg
        parts, gparts = [], []
        for k in range(NG):
            sl = slice(k * GW, (k + 1) * GW)
            dxk, dgk = _rms_bwd(yg[:, sl], gs_v[:, sl], dyn_ref[:, sl])
            parts.append(dxk)
            gparts.append(dgk)
        dyg = jnp.concatenate(parts, axis=1)
        dgs_rows = jnp.concatenate(gparts, axis=1)
        dy = dyg * zg
        dz_ref[...] = (dyg * y * (zs * (1.0 + z * (1.0 - zs)))).astype(bf16)
        dy_b = dy.astype(bf16)
        dq_b = (dy * c["ecs_x"]).astype(bf16)

        lo = lane < HP
        dcs = jnp.zeros((CH, CH), f32)
        dcst = jnp.zeros((CH, CH), f32)
        for g in range(NG):
            gsl = slice(g * GW, (g + 1) * GW)
            bg = b_ref[:, g * NS:(g + 1) * NS].astype(bf16)
            cg = c_ref[:, g * NS:(g + 1) * NS].astype(bf16)
            gmat = _dot_nt(cg, bg)
            dgm = jnp.zeros((CH, CH), f32)
            for q in range(GW // CH):
                col = g * GW + q * CH
                xp = xdt_b[:, col:col + CH]
                dyp = dy_b[:, col:col + CH]
                zero = jnp.zeros_like(dyp)
                xp2 = jnp.concatenate([jnp.where(lo, xp, zero), jnp.where(lo, zero, xp)], axis=0)
                dy2 = jnp.concatenate([jnp.where(lo, dyp, zero), jnp.where(lo, zero, dyp)], axis=0)
                dm2 = _dot_nt(dyp, xp2)
                ms = []
                for hh in range(2):
                    h = col // HP + hh
                    dec = _decay_matrix(c, h)
                    m = gmat * dec
                    dm = dm2[:, hh * CH:(hh + 1) * CH]
                    dseg = dm * m
                    dcs = dcs + jnp.where(lane == h, jnp.sum(dseg, axis=1, keepdims=True), 0.0)
                    dcst = dcst + jnp.where(sub == h, jnp.sum(dseg, axis=0, keepdims=True), 0.0)
                    dgm = dgm + dm * dec
                    ms.append(m.astype(bf16))
                dxdt_scr[:, col:col + CH] = _dot_tn(jnp.concatenate(ms, axis=0), dy2)
            dgm_b = dgm.astype(bf16)
            bds = _dot(bg, dpn_b[:, gsl])
            dxdt_scr[:, gsl] = dxdt_scr[:, gsl] + c["dte_x"][:, gsl] * bds
            dc_g = _dot(dgm_b, bg) + _dot_nt(dq_b[:, gsl], p_b[:, gsl])
            db_g = _dot_tn(dgm_b, cg) + _dot_nt(xdte_b[:, gsl], dpn_b[:, gsl])
            dx_ref[:, D + g * NS:D + (g + 1) * NS] = db_g
            dx_ref[:, D + NG * NS + g * NS:D + NG * NS + (g + 1) * NS] = dc_g
            dp_scr[:, gsl] = dpn[:, gsl] * c["ecl_x"][:, gsl] + _dot_tn(cg, dq_b[:, gsl])
            q_g = _dot(cg, p_b[:, gsl])
            e_g = e[:, gsl]
            dcs = dcs + c["ecs"] * _split_dot(dy[:, gsl] * q_g, e_g, REDUCE_TERMS, nt=True)
            ddte = _split_dot(xdt[:, gsl] * bds, e_g, REDUCE_TERMS, nt=True) * c["dte"]
            dcs = dcs - ddte
            dcs = dcs + jnp.where(sub == CH - 1, jnp.sum(ddte, axis=0, keepdims=True), 0.0)

        decl = _split_dot(jnp.broadcast_to(jnp.sum(dpn * p, axis=0, keepdims=True), (8, D)), e, 2, nt=True)[0:1]
        dcs = dcs + jnp.where(sub == CH - 1, c["ecl"] * decl, 0.0)
        dcs = dcs - dcst.T
        dadt = lax.dot_general(tri_ref[...], dcs, (((0,), (0,)), ((), ())), precision=lax.Precision.HIGHEST,
                               preferred_element_type=f32)
        dxdt = dxdt_scr[...]
        ddt = dadt * c["a"] + _split_dot(dxdt * xs, e, REDUCE_TERMS, nt=True)
        ddtr = jnp.where(lane < NH, ddt * _sigmoid(c["dtr"]), 0.0)
        ddt_ref[...] = ddtr.astype(bf16)
        dx_ref[:, 0:D] = dxdt * c["dt_x"] + c["dsk_x"] * dy
        dsk = _split_dot(jnp.broadcast_to(jnp.sum(dy * xs, axis=0, keepdims=True), (8, D)), e, 2, nt=True)[0:1]
        dalog = jnp.sum(dadt * c["dt"], axis=0, keepdims=True) * c["a"]
        row8 = lax.broadcasted_iota(jnp.int32, (8, CH), 0)
        dpar = jnp.where(row8 == 0, jnp.sum(ddtr, axis=0, keepdims=True),
                         jnp.where(row8 == 1, dalog, jnp.where(row8 == 2, dsk, 0.0)))
        dpar = jnp.where(lax.broadcasted_iota(jnp.int32, (8, CH), 1) < NH, dpar, 0.0)
        _accum(dpar_ref, dpar, i == 0)
        _accum(dgs_ref, jnp.sum(dgs_rows, axis=0, keepdims=True), i == 0)

    nb = t // CH
    rev = lambda i: (i // nc) * nc + (nc - 1 - i % nc)
    return pl.pallas_call(
        body, name=name, grid=(nb,),
        in_specs=[pl.BlockSpec((CH, D), lambda i: (rev(i), 0)),
                  pl.BlockSpec((CH, NG * NS), lambda i: (rev(i), D // (NG * NS))),
                  pl.BlockSpec((CH, NG * NS), lambda i: (rev(i), D // (NG * NS) + 1)),
                  pl.BlockSpec((CH, CH), lambda i: (rev(i), COL_DT // CH)),
                  pl.BlockSpec((CH, D), lambda i: (rev(i), COL_Z // D)),
                  pl.BlockSpec((CH, D), lambda i: (rev(i), 0)),
                  pl.BlockSpec((1, NS, D), lambda i: (rev(i), 0, 0)),
                  pl.BlockSpec((CH, D), lambda i: (rev(i), 1)),
                  pl.BlockSpec((8, CH), lambda i: (0, 0)), pl.BlockSpec((CH, D), lambda i: (0, 0)),
                  pl.BlockSpec((CH, CH), lambda i: (0, 0)), pl.BlockSpec((1, D), lambda i: (0, 0)),
                  pl.BlockSpec(memory_space=pl.ANY)],
        out_specs=[pl.BlockSpec((CH, XBC), lambda i: (rev(i), 0)), pl.BlockSpec((CH, D), lambda i: (rev(i), COL_Z // D)),
                   pl.BlockSpec((CH, CH), lambda i: (rev(i), 0)),
                   pl.BlockSpec((8, CH), lambda i: (0, 0)), pl.BlockSpec((1, D), lambda i: (0, 0))],
        out_shape=[_sds((t, XBC), f32), _sds((t, PROJ), bf16), _sds((t, CH), bf16), _sds((8, CH), f32), _sds((1, D), f32)],
        input_output_aliases={12: 1},
        scratch_shapes=[pltpu.VMEM((NS, D), f32), pltpu.VMEM((CH, D), f32)],
        compiler_params=_cparams(1))(xbcs, xbcs, xbcs, proj, proj, ypre, states, dcat, par, expand, tri, gs, dproj)


def loss_head(y, target, tb, name):
    t = y.shape[0]

    def body(y_ref, t_ref, s_ref, dy_ref):
        err = y_ref[...] - t_ref[...]
        dy_ref[...] = err * (1.0 / D)
        _accum(s_ref, jnp.zeros((8, CH), f32) + jnp.sum(err * err), pl.program_id(0) == 0)

    return pl.pallas_call(
        body, name=name, grid=(t // tb,),
        in_specs=[pl.BlockSpec((tb, D), lambda i: (i, 0)), pl.BlockSpec((tb, D), lambda i: (i, 0))],
        out_specs=[pl.BlockSpec((8, CH), lambda i: (0, 0)), pl.BlockSpec((tb, D), lambda i: (i, 0))],
        out_shape=[_sds((8, CH), f32), _sds((t, D), f32)],
        compiler_params=_cparams(1))(y, target)


def _tiles(t, seq):
    tm = min(512, t)
    return dict(tm=tm, tm_small=min(256, t), tm_large=min(1024, t), tb=min(512, seq))


def local_step(x, target, depth, weights_of, seq, grads_done=None):
    t = x.shape[0]
    ts = _tiles(t, seq)
    tm, tl, tb = ts["tm"], ts["tm_large"], ts["tb"]
    saved, ws = [], []
    for l in range(depth):
        w = weights_of(l, x)
        ws.append(w)
        proj, h1 = norm_matmul(x, w["g1"], w["win"], tl, 1152, f32, "in_proj", token=w.get("token"))
        cat = group_a_fwd(proj, w["wa"], w["ga"], seq, tb, "group_a_fwd")
        xbcs = conv_b_fwd(proj, w["ws"], w["bs"], seq, tb, "conv_b_fwd")
        cat, ypre, states = ssd_fwd(xbcs, proj, w["par"], w["gs"], cat, seq, "ssd_fwd")
        if "late" in w:
            w.update(w.pop("late")(cat))
        mix, x2 = matmul_postnorm(cat, w["wo"], x, w["g2"], tl, False, "out_proj")
        fp, h2 = norm_matmul(x2, w["g3"], w["wu"], tl, 1024, bf16, "mlp_up")
        o, x3 = matmul_postnorm(fp, w["wd"], x2, w["g4"], tm, True, "mlp_down")
        saved.append(dict(x=x, proj=proj, h1=h1, xbcs=xbcs, ypre=ypre, states=states, cat=cat, mix=mix, x2=x2,
                          fp=fp, h2=h2, o=o))
        x = x3
    sse, dx = loss_head(x, target, tm, "loss_head")
    grads = [None] * depth
    for l in reversed(range(depth)):
        s, w = saved[l], ws[l]
        do, dg4, dfp = postnorm_bwd_matmul(s["o"], w["g4"], dx, w["wd"], s["fp"], tl, 1024, bf16, "mlp_down_bwd")
        dwd = matmul_tn(s["fp"], do, 512, 1024, True, "mlp_down_dw")
        dx2, dg3 = matmul_prenorm_bwd(dfp, w["wu"], s["x2"], w["g3"], dx, tm, "mlp_up_bwd")
        dwu = matmul_tn(s["h2"], dfp, 512, 1024, False, "mlp_up_dw", col_blocks=True)
        dmix, dg2, dcat = postnorm_bwd_matmul(s["mix"], w["g2"], dx2, w["wo"], None, tl, 1024, f32, "out_proj_bwd")
        dwo = matmul_tn(s["cat"], dmix, 512, 1024, False, "out_proj_dw")
        token = None if grads_done is None else grads_done(l, dict(wo=dwo, wu=dwu, wd=dwd), False)
        dproj, dwa, dga = group_a_bwd(s["proj"], dcat, w["wa"], w["ga"], seq, tb, "group_a_bwd", token=token)
        dxbcs, dproj, ddt, dpar, dgs = ssd_bwd(s["xbcs"], s["proj"], s["ypre"], s["states"], dcat, w["par"], w["gs"],
                                               dproj, seq, "ssd_bwd")
        dproj, dws, dbs = conv_b_bwd(s["proj"], dxbcs, w["ws"], w["bs"], dproj, seq, tb, "conv_b_bwd")
        dproj = place_columns(dproj, ddt, COL_DT // CH, tm, "place_ddt")
        dwin = matmul_tn(s["h1"], dproj, 512, 1152, False, "in_proj_dw")
        token = None if grads_done is None else grads_done(l, dict(win=dwin), True)
        dx, dg1 = matmul_prenorm_bwd(dproj, w["win"], s["x"], w["g1"], dx2, ts["tm_small"], "in_proj_bwd", token=token)
        grads[l] = dict(win=dwin, wo=dwo, wu=dwu, wd=dwd, wa=dwa, ws=dws, bs=dbs, par=dpar,
                        g1=dg1, ga=dga, gs=dgs, g2=dg2, g3=dg3, g4=dg4)
    return sse, dx, grads


GROUPS = {
    "chips": [(1, 0, 0), (0, 1, 0), (1, 1, 0)],
    "pair": [(0, 0, 1)],
    "all": [(1, 0, 0), (0, 1, 0), (1, 1, 0), (0, 0, 1), (1, 0, 1), (0, 1, 1), (1, 1, 1)],
}


def _group_index(group, x, y, c):
    return {"chips": 2 * x + y, "pair": c, "all": 4 * x + 2 * y + c}[group]


def _chunk_indices(shape, pieces):
    if len(shape) < 3:
        return [()]
    lead = [()]
    for n in shape[:-2]:
        lead = [i + (k,) for i in lead for k in range(n)]
    rows = shape[-2]
    split = max(1, pieces // len(lead))
    while split > 1 and (rows % split or (rows // split) % 16):
        split -= 1
    step = rows // split
    return [i + (pl.ds(s * step, step),) for i in lead for s in range(split)]


def _exchange(arrays, out_shapes, group, src_view, dst_view, view_shape, name, own, pieces=16):
    masks = GROUPS[group]
    na, nm = len(arrays), len(masks)
    cuts = [_chunk_indices(view_shape(a), pieces) for a in range(na)]

    def body(*refs):
        ins, outs = refs[:na], refs[na:2 * na]
        send_sems, recv_sems = refs[2 * na:2 * na + 2]
        local_sems = refs[2 * na + 2] if own else None
        x, y, c = lax.axis_index("x"), lax.axis_index("y"), lax.axis_index("c")
        me = _group_index(group, x, y, c)
        peers = []
        for mx, my, mc in masks:
            px, py, pc = (1 - x if mx else x), (1 - y if my else y), (1 - c if mc else c)
            peers.append(((px, py, pc), _group_index(group, px, py, pc)))

        def part(ref, idx):
            return ref.at[idx] if idx else ref

        if own:
            for a in range(na):
                for idx in cuts[a]:
                    pltpu.make_async_copy(part(src_view(ins[a], a, me), idx), part(dst_view(outs[a], a, me), idx),
                                          local_sems.at[a]).start()
        for a in range(na):
            for j, (dev, pidx) in enumerate(peers):
                for idx in cuts[a]:
                    pltpu.make_async_remote_copy(
                        src_ref=part(src_view(ins[a], a, pidx), idx), dst_ref=part(dst_view(outs[a], a, me), idx),
                        send_sem=send_sems.at[a * nm + j], recv_sem=recv_sems.at[a * nm + j],
                        device_id=dev, device_id_type=MESH).start()
        whole = []
        for a in range(na):
            for j, (dev, pidx) in enumerate(peers):
                whole.append(pltpu.make_async_remote_copy(
                    src_ref=src_view(ins[a], a, pidx), dst_ref=dst_view(outs[a], a, pidx),
                    send_sem=send_sems.at[a * nm + j], recv_sem=recv_sems.at[a * nm + j],
                    device_id=dev, device_id_type=MESH))
        for cp in whole:
            cp.wait_recv()
        for cp in whole:
            cp.wait_send()
        if own:
            for a in range(na):
                pltpu.make_async_copy(src_view(ins[a], a, me), dst_view(outs[a], a, me), local_sems.at[a]).wait()

    hbm = pl.BlockSpec(memory_space=pltpu.HBM)
    sems = [pltpu.SemaphoreType.DMA((na * nm,)), pltpu.SemaphoreType.DMA((na * nm,))]
    return pl.pallas_call(
        body, name=name, in_specs=[hbm] * na, out_specs=[hbm] * na,
        out_shape=[_sds(s, a.dtype) for s, a in zip(out_shapes, arrays)],
        scratch_shapes=sems + ([pltpu.SemaphoreType.DMA((na,))] if own else []))(*arrays)


def all_gather(arrays, group, name, slot_axis=0, own=True):
    n = len(GROUPS[group]) + 1
    shapes = [a.shape[:slot_axis] + (n,) + a.shape[slot_axis:] for a in arrays]
    lead = (slice(None),) * slot_axis
    return _exchange(arrays, shapes, group, lambda r, a, i: r, lambda r, a, i: r.at[lead + (i,)],
                     lambda a: arrays[a].shape, name, own)


HBM_SPEC = pl.BlockSpec(memory_space=pltpu.HBM)
SEM_SPEC = pl.BlockSpec(memory_space=pltpu.SEMAPHORE)
DATAFLOW = pltpu.SideEffectType.DATAFLOW_SIDE_EFFECTING
N_CHIPS = 4


def _chip_peers(x, y, c):
    out = []
    for mx, my, _ in GROUPS["chips"]:
        px, py = (1 - x if mx else x), (1 - y if my else y)
        out.append(((px, py, c), 2 * px + py))
    return out


def _weight_views(shards):
    half = [s.shape[0] // 2 for s in shards]
    return dict(src=lambda ref, a, c, to_chip: ref.at[pl.ds(c * half[a], half[a])],
                dst=lambda ref, a, c, from_chip: ref.at[from_chip, pl.ds(c * half[a], half[a])],
                rows=lambda a: half[a])


def _grad_views(sums):
    return dict(src=lambda ref, a, c, to_chip: ref.at[to_chip], dst=lambda ref, a, c, from_chip: ref.at[from_chip],
                rows=lambda a: sums[a].shape[1])


def chips_start(sources, zones, views, name, pieces=4, after=None):
    na, nm = len(sources), N_CHIPS - 1

    def body(*refs):
        ins, lands = refs[:na], refs[na:2 * na]
        n_in = 2 * na + len(_token_arg(after))
        send_sems, recv_sems, token = refs[n_in], refs[n_in + 1], refs[-1]
        x, y, c = lax.axis_index("x"), lax.axis_index("y"), lax.axis_index("c")
        chip = 2 * x + y
        for a in range(na):
            step = views["rows"](a) // pieces
            for j, (dev, to_chip) in enumerate(_chip_peers(x, y, c)):
                for q in range(pieces):
                    rows = pl.ds(q * step, step)
                    pltpu.make_async_remote_copy(
                        src_ref=views["src"](ins[a], a, c, to_chip).at[rows],
                        dst_ref=views["dst"](lands[a], a, c, chip).at[rows],
                        send_sem=send_sems.at[a * nm + j], recv_sem=recv_sems.at[a * nm + j],
                        device_id=dev, device_id_type=MESH).start()
        token[...] = jnp.zeros_like(token)

    both = list(sources) + list(zones)
    outs = pl.pallas_call(
        body, name=name,
        out_shape=(pltpu.SemaphoreType.DMA((na * nm,)), pltpu.SemaphoreType.DMA((na * nm,)),
                   *[pltpu.HBM(b.shape, b.dtype) for b in both], _sds((8, CH), f32)),
        in_specs=[HBM_SPEC] * (2 * na) + _token_spec(after),
        out_specs=(SEM_SPEC, SEM_SPEC, *[HBM_SPEC] * (2 * na), pl.BlockSpec(memory_space=pltpu.VMEM)),
        input_output_aliases={i: 2 + i for i in range(2 * na)},
        compiler_params=pltpu.CompilerParams(has_side_effects=DATAFLOW))(
            *[pltpu.with_memory_space_constraint(b, pltpu.HBM) for b in both], *_token_arg(after))
    return dict(send=outs[0], recv=outs[1], sources=list(outs[2:2 + na]), zones=list(outs[2 + na:2 + 2 * na]),
                token=outs[-1], views=views)


def chips_wait(started, after, name):
    sources, zones, views = started["sources"], started["zones"], started["views"]
    na, nm = len(sources), N_CHIPS - 1

    def body(*refs):
        ins, lands = refs[:na], refs[na:2 * na]
        send_sems, recv_sems = refs[2 * na], refs[2 * na + 1]
        x, y, c = lax.axis_index("x"), lax.axis_index("y"), lax.axis_index("c")
        for a in range(na):
            for j, (dev, peer_chip) in enumerate(_chip_peers(x, y, c)):
                cp = pltpu.make_async_remote_copy(
                    src_ref=views["src"](ins[a], a, c, peer_chip), dst_ref=views["dst"](lands[a], a, c, peer_chip),
                    send_sem=send_sems.at[a * nm + j], recv_sem=recv_sems.at[a * nm + j],
                    device_id=dev, device_id_type=MESH)
                cp.wait_send()
                cp.wait_recv()

    both = list(sources) + list(zones)
    outs = pl.pallas_call(
        body, name=name, out_shape=tuple(pltpu.HBM(b.shape, b.dtype) for b in both),
        in_specs=[HBM_SPEC] * (2 * na) + [SEM_SPEC, SEM_SPEC, pl.BlockSpec(memory_space=pl.ANY)],
        out_specs=tuple([HBM_SPEC] * (2 * na)), input_output_aliases={i: i for i in range(2 * na)},
        compiler_params=pltpu.CompilerParams(has_side_effects=DATAFLOW))(*both, started["send"], started["recv"], after)
    return list(outs[:na]), list(outs[na:])


def weights_share(zones, name):
    na, nm = len(zones), N_CHIPS - 1

    def body(*refs):
        lands = refs[na:2 * na]
        send_sems, recv_sems = refs[2 * na:]
        x, y, c = lax.axis_index("x"), lax.axis_index("y"), lax.axis_index("c")
        chip = 2 * x + y
        sibling = (x, y, 1 - c)
        sends = []
        for a in range(na):
            half = zones[a].shape[1] // 2
            for m in range(1, N_CHIPS):
                mine = lands[a].at[chip ^ m, pl.ds(c * half, half)]
                sends.append(pltpu.make_async_remote_copy(
                    src_ref=mine, dst_ref=mine, send_sem=send_sems.at[a * nm + m - 1],
                    recv_sem=recv_sems.at[a * nm + m - 1], device_id=sibling, device_id_type=MESH))
        for cp in sends:
            cp.start()
        for a in range(na):
            half = zones[a].shape[1] // 2
            for m in range(1, N_CHIPS):
                theirs = lands[a].at[chip ^ m, pl.ds((1 - c) * half, half)]
                pltpu.make_async_remote_copy(
                    src_ref=theirs, dst_ref=theirs, send_sem=send_sems.at[a * nm + m - 1],
                    recv_sem=recv_sems.at[a * nm + m - 1], device_id=sibling, device_id_type=MESH).wait_recv()
        for cp in sends:
            cp.wait_send()

    return pl.pallas_call(
        body, name=name, in_specs=[HBM_SPEC] * na, out_specs=[HBM_SPEC] * na,
        out_shape=[_sds(z.shape, z.dtype) for z in zones], input_output_aliases={i: i for i in range(na)},
        scratch_shapes=[pltpu.SemaphoreType.DMA((na * nm,)), pltpu.SemaphoreType.DMA((na * nm,))])(*zones)


def pair_send_halves(grads, name):
    half = [g.shape[1] // 2 for g in grads]
    shapes = [(g.shape[0], h, g.shape[2]) for g, h in zip(grads, half)]
    return _exchange(grads, shapes, "pair", lambda r, a, i: r.at[:, pl.ds(i * half[a], half[a])],
                     lambda r, a, i: r, lambda a: shapes[a], name, False)


def sum_pair_half(g, recv, core, name, tb=256, by_chip=None):
    nk, r, c = g.shape
    tb = min(tb, r // 2)
    nb = r // 2 // tb

    def body(core_ref, g_ref, r_ref, o_ref):
        s = g_ref[...] + r_ref[...]
        if by_chip is None:
            o_ref[...] = s.astype(bf16)
        else:
            for k in range(by_chip[0]):
                o_ref[k] = s[:, k * by_chip[1]:(k + 1) * by_chip[1]].astype(bf16)

    if by_chip is None:
        out_spec = pl.BlockSpec((None, tb, c), lambda k, i, core_ref: (k, i, 0))
        out_shape = _sds((nk, r // 2, c), bf16)
    else:
        assert nk == 1
        out_spec = pl.BlockSpec((by_chip[0], tb, by_chip[1]), lambda k, i, core_ref: (0, i, 0))
        out_shape = _sds((by_chip[0], r // 2, by_chip[1]), bf16)
    return pl.pallas_call(
        body, name=name,
        grid_spec=pltpu.PrefetchScalarGridSpec(
            num_scalar_prefetch=1, grid=(nk, nb),
            in_specs=[pl.BlockSpec((None, tb, c), lambda k, i, core_ref: (k, core_ref[0] * nb + i, 0)),
                      pl.BlockSpec((None, tb, c), lambda k, i, core_ref: (k, i, 0))],
            out_specs=out_spec),
        out_shape=out_shape, compiler_params=_cparams(2))(jnp.reshape(core, (1,)).astype(jnp.int32), g, recv)


def assemble_columns(blocks, width, name, tb=256):
    n, r, c = blocks.shape

    def body(b_ref, o_ref):
        for k in range(n):
            o_ref[:, k * c:(k + 1) * c] = b_ref[k]
        o_ref[:, n * c:] = jnp.zeros((tb, width - n * c), blocks.dtype)

    return pl.pallas_call(
        body, name=name, grid=(r // tb,), in_specs=[pl.BlockSpec((n, tb, c), lambda i: (0, i, 0))],
        out_specs=pl.BlockSpec((tb, width), lambda i: (i, 0)), out_shape=_sds((r, width), blocks.dtype),
        compiler_params=_cparams(1))(blocks)


def chip_sum_into(acc, layer, own, others, chip, name, tb=256):
    n, r, c = own.shape
    tb = min(tb, r)

    def body(chip_ref, x_ref, y1_ref, y2_ref, y3_ref, acc_ref, o_ref):
        o_ref[...] = ((x_ref[...].astype(f32) + y1_ref[...].astype(f32)) + y2_ref[...].astype(f32)) + y3_ref[...].astype(f32)

    def slot(k):
        return pl.BlockSpec((None, tb, c), lambda i, chip_ref: (chip_ref[0] ^ k, i, 0))

    return pl.pallas_call(
        body, name=name,
        grid_spec=pltpu.PrefetchScalarGridSpec(
            num_scalar_prefetch=1, grid=(r // tb,),
            in_specs=[slot(k) for k in range(n)] + [pl.BlockSpec(memory_space=pl.ANY)],
            out_specs=pl.BlockSpec((None, tb, c), lambda i, chip_ref: (layer, i, 0))),
        out_shape=_sds(acc.shape, f32), input_output_aliases={n + 1: 0}, compiler_params=_cparams(1))(
            jnp.reshape(chip, (1,)).astype(jnp.int32), own, *([others] * (n - 1)), acc)


def adamw_halves(w, g_own, g_recv, m, v, core, name, tb=256):
    depth, r, c = w.shape
    tb = min(tb, r // 2)
    nb = r // 2 // tb

    def body(core_ref, w_ref, go_ref, gr_ref, m_ref, v_ref, g_ref, d_ref, mo_ref, vo_ref):
        gv = jnp.where(pl.program_id(1) == core_ref[0], go_ref[...], gr_ref[...])
        m2 = B1 * m_ref[...] + (1.0 - B1) * gv
        v2 = B2 * v_ref[...] + (1.0 - B2) * (gv * gv)
        m_hat = m2 / (1.0 - B1 ** STEP)
        v_hat = v2 / (1.0 - B2 ** STEP)
        g_ref[...] = gv
        d_ref[...] = -LR * (m_hat / (jnp.sqrt(v_hat) + AEPS) + WD * w_ref[...])
        mo_ref[...] = m2
        vo_ref[...] = v2

    whole = pl.BlockSpec((None, tb, c), lambda l, h, i, core_ref: (l, h * nb + i, 0))
    part = pl.BlockSpec((None, tb, c), lambda l, h, i, core_ref: (l, i, 0))
    return pl.pallas_call(
        body, name=name,
        grid_spec=pltpu.PrefetchScalarGridSpec(num_scalar_prefetch=1, grid=(depth, 2, nb),
                                               in_specs=[whole, part, part, whole, whole], out_specs=[whole] * 4),
        out_shape=[_sds(w.shape, f32)] * 4, compiler_params=_cparams(3))(
            jnp.reshape(core, (1,)).astype(jnp.int32), w, g_own, g_recv, m, v)


def sum_slots(y, out_dtype, name, tb=256):
    n, r, c = y.shape
    tb = min(tb, r)

    def body(y_ref, o_ref):
        acc = y_ref[0].astype(f32)
        for i in range(1, n):
            acc = acc + y_ref[i].astype(f32)
        o_ref[...] = acc.astype(out_dtype)

    return pl.pallas_call(
        body, name=name, grid=(r // tb,),
        in_specs=[pl.BlockSpec((n, tb, c), lambda i: (0, i, 0))], out_specs=pl.BlockSpec((tb, c), lambda i: (i, 0)),
        out_shape=_sds((r, c), out_dtype), compiler_params=_cparams(1))(y)


def adamw(w, g, m, v, name, tb=256):
    r, c = w.shape
    tb = min(tb, r)

    def body(w_ref, g_ref, m_ref, v_ref, d_ref, mo_ref, vo_ref):
        gv = g_ref[...]
        m2 = B1 * m_ref[...] + (1.0 - B1) * gv
        v2 = B2 * v_ref[...] + (1.0 - B2) * (gv * gv)
        m_hat = m2 / (1.0 - B1 ** STEP)
        v_hat = v2 / (1.0 - B2 ** STEP)
        d_ref[...] = -LR * (m_hat / (jnp.sqrt(v_hat) + AEPS) + WD * w_ref[...])
        mo_ref[...] = m2
        vo_ref[...] = v2

    spec = pl.BlockSpec((tb, c), lambda i: (i, 0))
    return pl.pallas_call(
        body, name=name, grid=(r // tb,), in_specs=[spec] * 4, out_specs=[spec] * 3,
        out_shape=[_sds((r, c), f32)] * 3, compiler_params=_cparams(1))(w, g, m, v)


def adamw_leading(w, g, m, v, name, tc=64):
    c, l, r = w.shape
    main = c // tc
    tail = c - main * tc

    def body(w_ref, g_ref, m_ref, v_ref, *rest):
        d_ref, mo_ref, vo_ref = rest[-3:]
        gv = g_ref[...]
        m2 = B1 * m_ref[...] + (1.0 - B1) * gv
        v2 = B2 * v_ref[...] + (1.0 - B2) * (gv * gv)
        m_hat = m2 / (1.0 - B1 ** STEP)
        v_hat = v2 / (1.0 - B2 ** STEP)
        d_ref[...] = -LR * (m_hat / (jnp.sqrt(v_hat) + AEPS) + WD * w_ref[...])
        mo_ref[...] = m2
        vo_ref[...] = v2

    spec = pl.BlockSpec((tc, l, r), lambda i: (i, 0, 0))
    outs = pl.pallas_call(
        functools.partial(body), name=name, grid=(main,), in_specs=[spec] * 4, out_specs=[spec] * 3,
        out_shape=[_sds(w.shape, f32)] * 3, compiler_params=_cparams(1))(w, g, m, v)
    if tail:
        assert (main * tc) % tail == 0
        last = pl.BlockSpec((tail, l, r), lambda i: (main * tc // tail, 0, 0))
        outs = pl.pallas_call(
            functools.partial(body), name=name + "_tail", grid=(1,),
            in_specs=[last] * 4 + [pl.BlockSpec(memory_space=pl.ANY)] * 3, out_specs=[last] * 3,
            out_shape=[_sds(w.shape, f32)] * 3, input_output_aliases={4: 0, 5: 1, 6: 2},
            compiler_params=_cparams(1))(w, g, m, v, *outs)
    return outs


SMALL_ROW = 1024
SMALL_GAINS = ("g1", "ga", "gs", "g2", "g3", "g4")
SMALL_LAYER_ROWS = 8 + 8 + 16 + 8


def _pack_small(grads):
    wide = lambda a: jnp.pad(a, ((0, 0), (0, 2 * SMALL_ROW - a.shape[1]))).reshape(-1, SMALL_ROW)
    row = lax.broadcasted_iota(jnp.int32, (8, SMALL_ROW), 0)
    parts = []
    for g in grads:
        singles = [g[k] for k in SMALL_GAINS] + [g["bs"][:, :SMALL_ROW],
                                                 jnp.pad(g["bs"][:, SMALL_ROW:], ((0, 0), (0, 2 * SMALL_ROW - XBC)))]
        first = sum(jnp.where(row == k, s, 0.0) for k, s in enumerate(singles))
        parts += [first, g["wa"], wide(g["ws"]), jnp.pad(g["par"], ((0, 0), (0, SMALL_ROW - CH)))]
    return jnp.concatenate(parts, axis=0)


def _unpack_small(packed, depth):
    rows = packed.reshape(depth, SMALL_LAYER_ROWS, SMALL_ROW)
    out = {k: rows[:, i] for i, k in enumerate(SMALL_GAINS)}
    out["bs"] = rows[:, 6:8].reshape(depth, 2 * SMALL_ROW)[:, :XBC]
    out["wa"] = rows[:, 8:11]
    out["ws"] = rows[:, 16:32].reshape(depth, 8, 2 * SMALL_ROW)[:, :4, :XBC]
    out["par"] = rows[:, 32:35, :CH]
    return out


def kernel(x, norm_mix_pre, w_in, conv_a_w, ssm_conv_w, ssm_conv_b, dt_bias, a_log, d_skip, conv_out_norm, ssm_out_norm, w_out, norm_mix_post, norm_mlp_pre, w_up, w_down, norm_mlp_post, loss_target, m_norm_mix_pre, m_w_in, m_conv_a_w, m_ssm_conv_w, m_ssm_conv_b, m_dt_bias, m_a_log, m_d_skip, m_conv_out_norm, m_ssm_out_norm, m_w_out, m_norm_mix_post, m_norm_mlp_pre, m_w_up, m_w_down, m_norm_mlp_post, v_norm_mix_pre, v_w_in, v_conv_a_w, v_ssm_conv_w, v_ssm_conv_b, v_dt_bias, v_a_log, v_d_skip, v_conv_out_norm, v_ssm_out_norm, v_w_out, v_norm_mix_post, v_norm_mlp_pre, v_w_up, v_w_down, v_norm_mlp_post):
    nb, seq, _ = x.shape
    t = nb * seq
    depth = w_in.shape[0]
    ncol = w_in.shape[2]
    chip = 2 * lax.axis_index("x") + lax.axis_index("y")

    taps = [conv_a_w, ssm_conv_w]
    taps_g = all_gather(taps, "chips", "gather_taps", slot_axis=1, own=False)
    wa_g, ws_g = [lax.dynamic_update_index_in_dim(g, s, chip, 1) for g, s in zip(taps_g, taps)]
    wa_full = jnp.transpose(wa_g, (0, 2, 1, 3)).reshape(depth, 3, D)
    ws_full = jnp.transpose(ws_g, (0, 2, 1, 3)).reshape(depth, 4, XBC)
    lane_pad = lambda a: jnp.pad(a, ((0, 0), (0, CH - a.shape[1])))
    par = jnp.stack([lane_pad(dt_bias), lane_pad(a_log), lane_pad(d_skip)], axis=1)
    par = jnp.pad(par, ((0, 0), (0, 5), (0, 0)))

    layer_shards = lambda l: [w_in[l].astype(bf16), w_out[l].astype(bf16), w_up[l].astype(bf16), w_down[l].astype(bf16)]
    issued = []

    def start(shards, name):
        zones = [lax.empty((N_CHIPS,) + s.shape, s.dtype) for s in shards]
        issued.append(chips_start(shards, zones, _weight_views(shards), name,
                                  after=issued[-1]["token"] if issued else None))
        return issued[-1]

    def finish(started, after, name):
        shards, zones = chips_wait(started, after, name)
        zones = weights_share(zones, "weights_share")
        return [lax.dynamic_update_index_in_dim(z, s, chip, 0) for z, s in zip(zones, shards)]

    def shaped(mats):
        wo_z, wu_z, wd_z = mats
        return wo_z.reshape(2 * D, D), wu_z, wd_z.reshape(DFF, D)

    first = layer_shards(0)
    travelling = {0: start(first[:1], "weights_start_0")}
    rest = start(first[1:], "weights_start_0_rest")
    for l in range(1, depth):
        travelling[l] = start(layer_shards(l), f"weights_start_{l}")

    def weights_of(l, x_in):
        mats = finish(travelling.pop(l), x_in, f"weights_wait_{l}")
        w = dict(win=assemble_columns(mats[0], PROJ, "assemble_w_in"), wa=jnp.pad(wa_full[l], ((0, 5), (0, 0))),
                 ws=jnp.pad(ws_full[l], ((0, 4), (0, 0))), bs=ssm_conv_b[l][None], par=par[l],
                 g1=norm_mix_pre[l][None], ga=conv_out_norm[l][None], gs=ssm_out_norm[l][None],
                 g2=norm_mix_post[l][None], g3=norm_mlp_pre[l][None], g4=norm_mlp_post[l][None])
        if l == 0:
            w["token"] = issued[-1]["token"]
            w["late"] = lambda after: dict(zip(("wo", "wu", "wd"), shaped(finish(rest, after, "weights_wait_0_rest"))))
        else:
            w.update(zip(("wo", "wu", "wd"), shaped(mats[1:])))
        return w

    core = lax.axis_index("c")
    grads_travelling = {}

    chip_major = dict(win=lambda a: a[None], wo=lambda a: a.reshape(N_CHIPS, 2 * D // N_CHIPS, D), wu=lambda a: a,
                      wd=lambda a: a.reshape(N_CHIPS, DFF // N_CHIPS, D))
    held = {}

    def grads_done(l, g, last):
        if l > 0 and not last:
            held[l] = g
            return None
        g = {**held.pop(l, {}), **g}
        keys = [k for k in ("win", "wo", "wu", "wd") if k in g]
        mats = [chip_major[k](g[k]) for k in keys]
        received = pair_send_halves(mats, "grads_to_pair")
        sums = [sum_pair_half(m_, r_, core, "pair_sum", by_chip=(N_CHIPS, ncol) if k == "win" else None)
                for k, m_, r_ in zip(keys, mats, received)]
        zones = [lax.empty(s.shape, s.dtype) for s in sums]
        started = chips_start(sums, zones, _grad_views(sums), f"grads_start_{l}_{len(grads_travelling)}")
        grads_travelling[(l, keys[0])] = (keys, started)
        return started["token"]

    sse, dx, grads = local_step(x.reshape(t, D), loss_target.reshape(t, D), depth, weights_of, seq, grads_done)
    loss = lax.psum(0.5 / D * sse[0, 0], ("x", "y", "c"))

    small_all = all_gather([_pack_small(grads)], "all", "gather_small")[0]
    small_sum = sum_slots(small_all, f32, "small_sum", tb=8)
    small = _unpack_small(small_sum, depth)

    big_w = dict(win=w_in, wo=w_out, wu=w_up, wd=w_down)
    acc = {k: lax.empty((depth, bw.shape[1] // 2, bw.shape[2]), f32) for k, bw in big_w.items()}
    for n, ((l, _), (keys, started)) in enumerate(grads_travelling.items()):
        sums, zones = chips_wait(started, small_sum, f"grads_wait_{l}_{n}")
        for k, s, z in zip(keys, sums, zones):
            acc[k] = chip_sum_into(acc[k], l, s, z, chip, "chip_sum")
    acc = [acc[k] for k in ("win", "wo", "wu", "wd")]
    from_sibling = _exchange(acc, [a.shape for a in acc], "pair", lambda r, a, i: r, lambda r, a, i: r,
                             lambda a: acc[a].shape, "grads_from_pair", False)

    wa_cols, ws_cols = conv_a_w.shape[2], ssm_conv_w.shape[2]
    par_g = small["par"].reshape(depth, 3, CH)
    g_small = dict(
        norm_mix_pre=small["g1"], conv_out_norm=small["ga"], ssm_out_norm=small["gs"], norm_mix_post=small["g2"],
        norm_mlp_pre=small["g3"], norm_mlp_post=small["g4"], ssm_conv_b=small["bs"],
        conv_a_w=lax.dynamic_slice_in_dim(small["wa"].reshape(depth, 3, D), chip * wa_cols, wa_cols, axis=2),
        ssm_conv_w=lax.dynamic_slice_in_dim(small["ws"].reshape(depth, 4, XBC), chip * ws_cols, ws_cols, axis=2),
        dt_bias=par_g[:, 0, :NH], a_log=par_g[:, 1, :NH], d_skip=par_g[:, 2, :NH])

    given = dict(norm_mix_pre=(norm_mix_pre, m_norm_mix_pre, v_norm_mix_pre), w_in=(w_in, m_w_in, v_w_in),
                 conv_a_w=(conv_a_w, m_conv_a_w, v_conv_a_w), ssm_conv_w=(ssm_conv_w, m_ssm_conv_w, v_ssm_conv_w),
                 ssm_conv_b=(ssm_conv_b, m_ssm_conv_b, v_ssm_conv_b), dt_bias=(dt_bias, m_dt_bias, v_dt_bias),
                 a_log=(a_log, m_a_log, v_a_log), d_skip=(d_skip, m_d_skip, v_d_skip),
                 conv_out_norm=(conv_out_norm, m_conv_out_norm, v_conv_out_norm),
                 ssm_out_norm=(ssm_out_norm, m_ssm_out_norm, v_ssm_out_norm), w_out=(w_out, m_w_out, v_w_out),
                 norm_mix_post=(norm_mix_post, m_norm_mix_post, v_norm_mix_post),
                 norm_mlp_pre=(norm_mlp_pre, m_norm_mlp_pre, v_norm_mlp_pre), w_up=(w_up, m_w_up, v_w_up),
                 w_down=(w_down, m_w_down, v_w_down), norm_mlp_post=(norm_mlp_post, m_norm_mlp_post, v_norm_mlp_post))
    halves = dict(zip(["w_in", "w_out", "w_up", "w_down"], zip(acc, from_sibling)))
    order = ["norm_mix_pre", "w_in", "conv_a_w", "ssm_conv_w", "ssm_conv_b", "dt_bias", "a_log", "d_skip",
             "conv_out_norm", "ssm_out_norm", "w_out", "norm_mix_post", "norm_mlp_pre", "w_up", "w_down",
             "norm_mlp_post"]
    g_out, d_out, m_out, v_out = [], [], [], []
    for n in order:
        wv, mv, vv = given[n]
        if n in halves and wv.shape[-1] % CH:
            own, recv = halves[n]
            gv = jnp.concatenate([jnp.where(core == 0, own, recv), jnp.where(core == 0, recv, own)], axis=1)
            to_cols, to_rows = (lambda a: jnp.transpose(a, (2, 0, 1))), (lambda a: jnp.transpose(a, (1, 2, 0)))
            dlt, m2, v2 = [to_rows(o) for o in adamw_leading(to_cols(wv), to_cols(gv), to_cols(mv), to_cols(vv),
                                                             "adamw_cols")]
        elif n in halves:
            gv, dlt, m2, v2 = adamw_halves(wv, *halves[n], mv, vv, core, "adamw_matrix")
        else:
            gv = g_small[n].reshape(wv.shape)
            two_d = lambda a: a.reshape(-1, a.shape[-1])
            dlt, m2, v2 = adamw(two_d(wv), two_d(gv), two_d(mv), two_d(vv), "adamw")
        g_out.append(gv)
        d_out.append(dlt.reshape(wv.shape))
        m_out.append(m2.reshape(wv.shape))
        v_out.append(v2.reshape(wv.shape))
    return (loss, dx.reshape(nb, seq, D), *g_out, *d_out, *m_out, *v_out)
```

```python
import functools

import jax
import jax.numpy as jnp
from jax import lax
from jax.experimental import pallas as pl
from jax.experimental.pallas import tpu as pltpu

f32, bf16 = jnp.float32, jnp.bfloat16

D = 1024
NH, HP = 16, 64
NG, NS = 2, 128
CH = 128
XBC = D + 2 * NG * NS
DFF = 4 * D
IN_COLS = 3 * D + D + XBC + NH
PROJ = 5760
COL_Z, COL_XBC, COL_DT = 3 * D, 4 * D, 4 * D + XBC
EPS = 1e-6
HALO = 8
VMEM_LIMIT = 56 * 2**20
MESH = pl.DeviceIdType.MESH

LR, B1, B2, AEPS, WD, STEP = 0.001, 0.9, 0.999, 1e-08, 0.01, 10


def _cparams(n_axes):
    return pltpu.CompilerParams(dimension_semantics=("arbitrary",) * n_axes, vmem_limit_bytes=VMEM_LIMIT)


def _sds(shape, dtype):
    return jax.ShapeDtypeStruct(tuple(shape), dtype)


def _token_spec(token):
    return [] if token is None else [pl.BlockSpec(memory_space=pl.ANY)]


def _token_arg(token):
    return [] if token is None else [token]


def _rms_fwd(x, g):
    r = lax.rsqrt(jnp.mean(x * x, axis=-1, keepdims=True) + EPS)
    return x * r * g


def _rms_bwd(x, g, dy):
    r = lax.rsqrt(jnp.mean(x * x, axis=-1, keepdims=True) + EPS)
    xh = x * r
    gdy = dy * g
    dx = r * (gdy - xh * jnp.mean(xh * gdy, axis=-1, keepdims=True))
    return dx, dy * xh


def _accum(ref, part, first):
    @pl.when(first)
    def _():
        ref[...] = part

    @pl.when(jnp.logical_not(first))
    def _():
        ref[...] += part


def _dot_nt(a, b):
    return lax.dot_general(a, b, (((1,), (1,)), ((), ())), preferred_element_type=f32)


def _dot_tn(a, b):
    return lax.dot_general(a, b, (((0,), (0,)), ((), ())), preferred_element_type=f32)


def _dot(a, b):
    return jnp.dot(a, b, preferred_element_type=f32)


def _split_dot(x, e_bf, n_split, nt=False):
    acc = None
    rem = x
    for s in range(n_split):
        hi = rem.astype(bf16)
        term = _dot_nt(hi, e_bf) if nt else _dot(hi, e_bf)
        acc = term if acc is None else acc + term
        if s + 1 < n_split:
            rem = rem - hi.astype(f32)
    return acc


def _sigmoid(x):
    return 0.5 * jnp.tanh(0.5 * x) + 0.5


def norm_matmul(x, g, w, tm, tn, out_dtype, name, token=None):
    t = x.shape[0]
    if w.ndim == 3:
        assert w.shape[2] == tn
        n = w.shape[0] * tn
        w_spec = pl.BlockSpec((None, D, tn), lambda i, j: (j, 0, 0))
    else:
        n = w.shape[1]
        w_spec = pl.BlockSpec((D, tn), lambda i, j: (0, j))

    def body(x_ref, g_ref, w_ref, *rest):
        o_ref, h_ref = rest[-2:]

        @pl.when(pl.program_id(1) == 0)
        def _():
            h_ref[...] = _rms_fwd(x_ref[...], g_ref[...]).astype(bf16)

        o_ref[...] = _dot(h_ref[...], w_ref[...]).astype(out_dtype)

    return pl.pallas_call(
        body, name=name, grid=(t // tm, n // tn),
        in_specs=[pl.BlockSpec((tm, D), lambda i, j: (i, 0)), pl.BlockSpec((1, D), lambda i, j: (0, 0)), w_spec]
        + _token_spec(token),
        out_specs=[pl.BlockSpec((tm, tn), lambda i, j: (i, j)), pl.BlockSpec((tm, D), lambda i, j: (i, 0))],
        out_shape=[_sds((t, n), out_dtype), _sds((t, D), bf16)],
        compiler_params=_cparams(2))(x, g, w, *_token_arg(token))


def matmul_postnorm(a, w, xres, g, tm, relu2, name):
    t, k = a.shape

    def body(a_ref, w_ref, xr_ref, g_ref, y_ref, xo_ref):
        av = a_ref[...]
        if relu2:
            af = jnp.maximum(av.astype(f32), 0.0)
            av = (af * af).astype(bf16)
        y = _dot(av, w_ref[...])
        y_ref[...] = y
        xo_ref[...] = xr_ref[...] + _rms_fwd(y, g_ref[...])

    return pl.pallas_call(
        body, name=name, grid=(t // tm,),
        in_specs=[pl.BlockSpec((tm, k), lambda i: (i, 0)), pl.BlockSpec((k, D), lambda i: (0, 0)),
                  pl.BlockSpec((tm, D), lambda i: (i, 0)), pl.BlockSpec((1, D), lambda i: (0, 0))],
        out_specs=[pl.BlockSpec((tm, D), lambda i: (i, 0)), pl.BlockSpec((tm, D), lambda i: (i, 0))],
        out_shape=[_sds((t, D), f32), _sds((t, D), f32)],
        compiler_params=_cparams(1))(a, w, xres, g)


def postnorm_bwd_matmul(y, g, dxo, w, fp, tm, tn, out_dtype, name, token=None):
    t, n = y.shape[0], w.shape[0]
    relu = fp is not None

    def body(*refs):
        y_ref, g_ref, dxo_ref, w_ref = refs[:4]
        fp_ref = refs[4] if relu else None
        dy_ref, dg_ref, da_ref = refs[-3:]
        i, j = pl.program_id(0), pl.program_id(1)

        @pl.when(j == 0)
        def _():
            dx, dgc = _rms_bwd(y_ref[...], g_ref[...], dxo_ref[...])
            dy_ref[...] = dx.astype(bf16)
            _accum(dg_ref, jnp.sum(dgc, axis=0, keepdims=True), i == 0)

        da = _dot_nt(dy_ref[...], w_ref[...])
        if relu:
            da = da * (2.0 * jnp.maximum(fp_ref[...].astype(f32), 0.0))
        da_ref[...] = da.astype(out_dtype)

    in_specs = [pl.BlockSpec((tm, D), lambda i, j: (i, 0)), pl.BlockSpec((1, D), lambda i, j: (0, 0)),
                pl.BlockSpec((tm, D), lambda i, j: (i, 0)), pl.BlockSpec((tn, D), lambda i, j: (j, 0))]
    args = [y, g, dxo, w]
    if relu:
        in_specs.append(pl.BlockSpec((tm, tn), lambda i, j: (i, j)))
        args.append(fp)
    in_specs += _token_spec(token)
    args += _token_arg(token)
    return pl.pallas_call(
        body, name=name, grid=(t // tm, n // tn), in_specs=in_specs,
        out_specs=[pl.BlockSpec((tm, D), lambda i, j: (i, 0)), pl.BlockSpec((1, D), lambda i, j: (0, 0)),
                   pl.BlockSpec((tm, tn), lambda i, j: (i, j))],
        out_shape=[_sds((t, D), bf16), _sds((1, D), f32), _sds((t, n), out_dtype)],
        compiler_params=_cparams(2))(*args)


def matmul_prenorm_bwd(da, w, x, g, dxo, tm, name, token=None):
    t, k = da.shape
    blocked = w.ndim == 3

    def body(da_ref, w_ref, x_ref, g_ref, dxo_ref, *rest):
        dx_ref, dg_ref = rest[-2:]
        if blocked:
            kc = w.shape[2]
            dh = _dot_nt(da_ref[:, 0:kc], w_ref[0])
            for q in range(1, w.shape[0]):
                dh = dh + _dot_nt(da_ref[:, q * kc:(q + 1) * kc], w_ref[q])
        else:
            dh = _dot_nt(da_ref[...], w_ref[...])
        dxn, dgc = _rms_bwd(x_ref[...], g_ref[...], dh)
        dx_ref[...] = dxo_ref[...] + dxn
        _accum(dg_ref, jnp.sum(dgc, axis=0, keepdims=True), pl.program_id(0) == 0)

    w_spec = pl.BlockSpec(w.shape, (lambda i: (0, 0, 0)) if blocked else (lambda i: (0, 0)))
    return pl.pallas_call(
        body, name=name, grid=(t // tm,),
        in_specs=[pl.BlockSpec((tm, k), lambda i: (i, 0)), w_spec,
                  pl.BlockSpec((tm, D), lambda i: (i, 0)), pl.BlockSpec((1, D), lambda i: (0, 0)),
                  pl.BlockSpec((tm, D), lambda i: (i, 0))] + _token_spec(token),
        out_specs=[pl.BlockSpec((tm, D), lambda i: (i, 0)), pl.BlockSpec((1, D), lambda i: (0, 0))],
        out_shape=[_sds((t, D), f32), _sds((1, D), f32)],
        compiler_params=_cparams(1))(da, w, x, g, dxo, *_token_arg(token))


def matmul_tn(a, b, tm, tn, relu2, name, col_blocks=False):
    t, m = a.shape
    n = b.shape[1]
    if col_blocks:
        out_spec, out_shape = pl.BlockSpec((None, tm, tn), lambda i, j: (j, i, 0)), _sds((n // tn, m, tn), f32)
    else:
        out_spec, out_shape = pl.BlockSpec((tm, tn), lambda i, j: (i, j)), _sds((m, n), f32)

    def body(a_ref, b_ref, o_ref, at_ref):
        @pl.when(pl.program_id(1) == 0)
        def _():
            av = a_ref[...]
            if relu2:
                af = jnp.maximum(av.astype(f32), 0.0)
                av = (af * af).astype(bf16)
            at_ref[...] = av.T

        o_ref[...] = _dot(at_ref[...], b_ref[...])

    return pl.pallas_call(
        body, name=name, grid=(m // tm, n // tn),
        in_specs=[pl.BlockSpec((t, tm), lambda i, j: (0, i)), pl.BlockSpec((t, tn), lambda i, j: (0, j))],
        out_specs=out_spec, out_shape=out_shape,
        scratch_shapes=[pltpu.VMEM((tm, t), bf16)],
        compiler_params=_cparams(2))(a, b)


ROWS_A = 16
ROWS_B = 32
UNROLL = 4


def _past(win, s):
    return (win if s == 0 else pltpu.roll(win, s, 0))[HALO:]


def _future(win, s):
    n = win.shape[0]
    return (win if s == 0 else pltpu.roll(win, n - s, 0))[:n - HALO]


def _fold8(v):
    return v.reshape(v.shape[0] // 8, 8, v.shape[1]).sum(axis=0)


def _halo_prev(tb, col):
    return lambda i: (jnp.maximum(i * (tb // HALO) - 1, 0), col)


def _halo_next(tb, col, t):
    return lambda i: (jnp.minimum((i + 1) * (tb // HALO), t // HALO - 1), col)


def group_a_fwd(proj, wa, g, seq, tb, name):
    t = proj.shape[0]
    bps = seq // tb

    def body(xa_ref, ca_ref, ba_ref, xah_ref, cah_ref, wa_ref, g_ref, o_ref, u_scr):
        first = (pl.program_id(0) % bps) == 0
        u_scr[0:HALO, :] = jnp.where(first, 0.0, cah_ref[...] * xah_ref[...])
        w, gv = wa_ref[...], g_ref[...]

        def chunk(i, carry):
            r = pl.multiple_of(i * ROWS_A, ROWS_A)
            rows = pl.ds(r, ROWS_A)
            u_scr[pl.ds(pl.multiple_of(HALO + r, HALO), ROWS_A), :] = ca_ref[rows, :] * xa_ref[rows, :]
            win = u_scr[pl.ds(r, ROWS_A + HALO), :]
            cv = w[2:3] * _past(win, 0) + w[1:2] * _past(win, 1) + w[0:1] * _past(win, 2)
            o_ref[rows, :] = _rms_fwd(ba_ref[rows, :] * cv, gv).astype(bf16)
            return carry

        lax.fori_loop(0, tb // ROWS_A, chunk, 0, unroll=UNROLL)

    blk = lambda c: pl.BlockSpec((tb, D), lambda i: (i, c))
    return pl.pallas_call(
        body, name=name, grid=(t // tb,),
        in_specs=[blk(0), blk(1), blk(2),
                  pl.BlockSpec((HALO, D), _halo_prev(tb, 0)), pl.BlockSpec((HALO, D), _halo_prev(tb, 1)),
                  pl.BlockSpec((8, D), lambda i: (0, 0)), pl.BlockSpec((1, D), lambda i: (0, 0))],
        out_specs=pl.BlockSpec((tb, D), lambda i: (i, 0)),
        out_shape=_sds((t, 2 * D), bf16),
        scratch_shapes=[pltpu.VMEM((tb + HALO, D), f32)],
        compiler_params=_cparams(1))(proj, proj, proj, proj, proj, wa, g)


def group_a_bwd(proj, dcat, wa, g, seq, tb, name, token=None):
    t = proj.shape[0]
    bps = seq // tb

    def body(xa_ref, ca_ref, ba_ref, dy_ref, xap_ref, cap_ref, xan_ref, can_ref, ban_ref, dyn_ref, wa_ref, g_ref,
             *rest):
        dp_ref, dwa_ref, dg_ref, u_scr, d_scr, acc_scr = rest[-6:]
        i = pl.program_id(0)
        first = (i % bps) == 0
        last = (i % bps) == bps - 1
        w = wa_ref[...]
        gv = g_ref[...]
        u_scr[0:HALO, :] = jnp.where(first, 0.0, cap_ref[...] * xap_ref[...])
        u_scr[HALO + tb:2 * HALO + tb, :] = can_ref[...] * xan_ref[...]
        acc_scr[...] = jnp.zeros_like(acc_scr)

        def forward_part(n, carry):
            r = pl.multiple_of(n * ROWS_A, ROWS_A)
            rows = pl.ds(r, ROWS_A)
            ba = ba_ref[rows, :]
            u_scr[pl.ds(pl.multiple_of(HALO + r, HALO), ROWS_A), :] = ca_ref[rows, :] * xa_ref[rows, :]
            win = u_scr[pl.ds(r, ROWS_A + HALO), :]
            u = [_past(win, s) for s in range(3)]
            cv = w[2:3] * u[0] + w[1:2] * u[1] + w[0:1] * u[2]
            dya, dgc = _rms_bwd(ba * cv, gv, dy_ref[rows, :])
            dcv = dya * ba
            d_scr[rows, :] = dcv
            dp_ref[rows, 2 * D:3 * D] = (dya * cv).astype(bf16)
            acc_scr[0:8, :] += _fold8(dgc)
            for k in range(3):
                acc_scr[8 + 8 * k:16 + 8 * k, :] += _fold8(dcv * u[2 - k])
            return carry

        lax.fori_loop(0, tb // ROWS_A, forward_part, 0, unroll=UNROLL)

        start = HALO + tb
        cvn = (w[2:3] * u_scr[pl.ds(start, HALO), :] + w[1:2] * u_scr[pl.ds(start - 1, HALO), :]
               + w[0:1] * u_scr[pl.ds(start - 2, HALO), :])
        ban = ban_ref[...]
        dyan, _ = _rms_bwd(ban * cvn, gv, dyn_ref[...])
        d_scr[tb:tb + HALO, :] = jnp.where(last, 0.0, dyan * ban)

        def backward_part(n, carry):
            r = pl.multiple_of(n * ROWS_A, ROWS_A)
            rows = pl.ds(r, ROWS_A)
            win = d_scr[pl.ds(r, ROWS_A + HALO), :]
            du = w[2:3] * _future(win, 0) + w[1:2] * _future(win, 1) + w[0:1] * _future(win, 2)
            dp_ref[rows, 0:D] = (du * ca_ref[rows, :]).astype(bf16)
            dp_ref[rows, D:2 * D] = (du * xa_ref[rows, :]).astype(bf16)
            return carry

        lax.fori_loop(0, tb // ROWS_A, backward_part, 0, unroll=UNROLL)

        row = lax.broadcasted_iota(jnp.int32, (8, D), 0)
        dw = jnp.zeros((8, D), f32)
        for k in range(3):
            dw = jnp.where(row == k, jnp.sum(acc_scr[8 + 8 * k:16 + 8 * k, :], axis=0, keepdims=True), dw)
        _accum(dwa_ref, dw, i == 0)
        _accum(dg_ref, jnp.sum(acc_scr[0:8, :], axis=0, keepdims=True), i == 0)

    blk = lambda c: pl.BlockSpec((tb, D), lambda i: (i, c))
    prv = lambda c: pl.BlockSpec((HALO, D), _halo_prev(tb, c))
    nxt = lambda c: pl.BlockSpec((HALO, D), _halo_next(tb, c, t))
    return pl.pallas_call(
        body, name=name, grid=(t // tb,),
        in_specs=[blk(0), blk(1), blk(2), blk(0), prv(0), prv(1), nxt(0), nxt(1), nxt(2), nxt(0),
                  pl.BlockSpec((8, D), lambda i: (0, 0)), pl.BlockSpec((1, D), lambda i: (0, 0))] + _token_spec(token),
        out_specs=[pl.BlockSpec((tb, 3 * D), lambda i: (i, 0)), pl.BlockSpec((8, D), lambda i: (0, 0)),
                   pl.BlockSpec((1, D), lambda i: (0, 0))],
        out_shape=[_sds((t, PROJ), bf16), _sds((8, D), f32), _sds((1, D), f32)],
        scratch_shapes=[pltpu.VMEM((tb + 2 * HALO, D), f32), pltpu.VMEM((tb + HALO, D), f32), pltpu.VMEM((32, D), f32)],
        compiler_params=_cparams(1))(proj, proj, proj, dcat, proj, proj, proj, proj, proj, dcat, wa, g,
                                     *_token_arg(token))


CB = 512
XBC_BLK0 = COL_XBC // CB


def conv_b_fwd(proj, ws, bs, seq, tb, name):
    t = proj.shape[0]
    bps = seq // tb

    def body(x_ref, xp_ref, w_ref, b_ref, o_ref, x_scr):
        first = (pl.program_id(1) % bps) == 0
        x_scr[0:HALO, :] = jnp.where(first, 0.0, xp_ref[...])
        w, bias = w_ref[...], b_ref[...]

        def chunk(n, carry):
            r = pl.multiple_of(n * ROWS_B, ROWS_B)
            rows = pl.ds(r, ROWS_B)
            x_scr[pl.ds(pl.multiple_of(HALO + r, HALO), ROWS_B), :] = x_ref[rows, :]
            win = x_scr[pl.ds(r, ROWS_B + HALO), :]
            xc = bias + w[3:4] * _past(win, 0)
            for k in range(3):
                xc = xc + w[k:k + 1] * _past(win, 3 - k)
            o_ref[rows, :] = xc * _sigmoid(xc)
            return carry

        lax.fori_loop(0, tb // ROWS_B, chunk, 0, unroll=UNROLL)

    return pl.pallas_call(
        body, name=name, grid=(XBC // CB, t // tb),
        in_specs=[pl.BlockSpec((tb, CB), lambda j, i: (i, XBC_BLK0 + j)),
                  pl.BlockSpec((HALO, CB), lambda j, i: (jnp.maximum(i * (tb // HALO) - 1, 0), XBC_BLK0 + j)),
                  pl.BlockSpec((8, CB), lambda j, i: (0, j)), pl.BlockSpec((1, CB), lambda j, i: (0, j))],
        out_specs=pl.BlockSpec((tb, CB), lambda j, i: (i, j)),
        out_shape=_sds((t, XBC), f32),
        scratch_shapes=[pltpu.VMEM((tb + HALO, CB), f32)],
        compiler_params=_cparams(2))(proj, proj, ws, bs)


def conv_b_bwd(proj, dxs, ws, bs, dproj, seq, tb, name):
    t = proj.shape[0]
    bps = seq // tb

    def body(x_ref, xp_ref, xn_ref, d_ref, dn_ref, w_ref, b_ref, dproj_ref, dx_ref, dw_ref, db_ref, x_scr, d_scr,
             acc_scr):
        i = pl.program_id(1)
        first = (i % bps) == 0
        last = (i % bps) == bps - 1
        w = w_ref[...]
        bias = b_ref[...]
        x_scr[0:HALO, :] = jnp.where(first, 0.0, xp_ref[...])
        x_scr[HALO + tb:2 * HALO + tb, :] = xn_ref[...]
        acc_scr[...] = jnp.zeros_like(acc_scr)

        def dsilu(xc, d):
            sg = _sigmoid(xc)
            return d * (sg * (1.0 + xc * (1.0 - sg)))

        def forward_part(n, carry):
            r = pl.multiple_of(n * ROWS_B, ROWS_B)
            rows = pl.ds(r, ROWS_B)
            x_scr[pl.ds(pl.multiple_of(HALO + r, HALO), ROWS_B), :] = x_ref[rows, :]
            win = x_scr[pl.ds(r, ROWS_B + HALO), :]
            xs = [_past(win, s) for s in range(4)]
            xc = bias + w[3:4] * xs[0]
            for k in range(3):
                xc = xc + w[k:k + 1] * xs[3 - k]
            dxc = dsilu(xc, d_ref[rows, :])
            d_scr[rows, :] = dxc
            acc_scr[0:8, :] += _fold8(dxc)
            for k in range(4):
                acc_scr[8 + 8 * k:16 + 8 * k, :] += _fold8(dxc * xs[3 - k])
            return carry

        lax.fori_loop(0, tb // ROWS_B, forward_part, 0, unroll=UNROLL)

        start = HALO + tb
        xcn = bias + w[3:4] * x_scr[pl.ds(start, HALO), :]
        for k in range(3):
            xcn = xcn + w[k:k + 1] * x_scr[pl.ds(start - 3 + k, HALO), :]
        d_scr[tb:tb + HALO, :] = jnp.where(last, 0.0, dsilu(xcn, dn_ref[...]))

        def backward_part(n, carry):
            r = pl.multiple_of(n * ROWS_B, ROWS_B)
            win = d_scr[pl.ds(r, ROWS_B + HALO), :]
            dx = w[3:4] * _future(win, 0)
            for k in range(3):
                dx = dx + w[k:k + 1] * _future(win, 3 - k)
            dx_ref[pl.ds(r, ROWS_B), :] = dx.astype(bf16)
            return carry

        lax.fori_loop(0, tb // ROWS_B, backward_part, 0, unroll=UNROLL)

        row = lax.broadcasted_iota(jnp.int32, (8, CB), 0)
        dw = jnp.zeros((8, CB), f32)
        for k in range(4):
            dw = jnp.where(row == k, jnp.sum(acc_scr[8 + 8 * k:16 + 8 * k, :], axis=0, keepdims=True), dw)
        _accum(dw_ref, dw, i == 0)
        _accum(db_ref, jnp.sum(acc_scr[0:8, :], axis=0, keepdims=True), i == 0)

    nh = t // HALO
    return pl.pallas_call(
        body, name=name, grid=(XBC // CB, t // tb),
        in_specs=[pl.BlockSpec((tb, CB), lambda j, i: (i, XBC_BLK0 + j)),
                  pl.BlockSpec((HALO, CB), lambda j, i: (jnp.maximum(i * (tb // HALO) - 1, 0), XBC_BLK0 + j)),
                  pl.BlockSpec((HALO, CB), lambda j, i: (jnp.minimum((i + 1) * (tb // HALO), nh - 1), XBC_BLK0 + j)),
                  pl.BlockSpec((tb, CB), lambda j, i: (i, j)),
                  pl.BlockSpec((HALO, CB), lambda j, i: (jnp.minimum((i + 1) * (tb // HALO), nh - 1), j)),
                  pl.BlockSpec((8, CB), lambda j, i: (0, j)), pl.BlockSpec((1, CB), lambda j, i: (0, j)),
                  pl.BlockSpec(memory_space=pl.ANY)],
        out_specs=[pl.BlockSpec((tb, CB), lambda j, i: (i, XBC_BLK0 + j)), pl.BlockSpec((8, CB), lambda j, i: (0, j)),
                   pl.BlockSpec((1, CB), lambda j, i: (0, j))],
        out_shape=[_sds((t, PROJ), bf16), _sds((8, XBC), f32), _sds((1, XBC), f32)],
        input_output_aliases={7: 0},
        scratch_shapes=[pltpu.VMEM((tb + 2 * HALO, CB), f32), pltpu.VMEM((tb + HALO, CB), f32),
                        pltpu.VMEM((40, CB), f32)],
        compiler_params=_cparams(2))(proj, proj, proj, dxs, dxs, ws, bs, dproj)


def place_columns(buf, part, col_block, tb, name):
    t, wdt = part.shape

    def body(p_ref, buf_ref, o_ref):
        o_ref[...] = p_ref[...]

    return pl.pallas_call(
        body, name=name, grid=(t // tb,),
        in_specs=[pl.BlockSpec((tb, wdt), lambda i: (i, 0)), pl.BlockSpec(memory_space=pl.ANY)],
        out_specs=pl.BlockSpec((tb, wdt), lambda i: (i, col_block)), out_shape=_sds(buf.shape, buf.dtype),
        input_output_aliases={1: 0}, compiler_params=_cparams(1))(part, buf)


GW = D // NG
EXPAND_TERMS = 2
REDUCE_TERMS = 1


def _ssd_consts():
    head_of_lane = jnp.arange(D) // HP
    expand = (jnp.arange(CH)[:, None] == head_of_lane[None, :]).astype(bf16)
    tri = (jnp.arange(CH)[:, None] >= jnp.arange(CH)[None, :]).astype(f32)
    return expand, tri


def _ssd_common(par_ref, dtr_ref, e_ref, tri_ref):
    par = par_ref[...]
    dtb, alog, dsk = par[0:1], par[1:2], par[2:3]
    lane = lax.broadcasted_iota(jnp.int32, (CH, CH), 1)
    a = -jnp.exp(alog)
    dtr = dtr_ref[...] + dtb
    sp = jnp.maximum(dtr, 0.0) + jnp.log(1.0 + jnp.exp(-jnp.abs(dtr)))
    dt = jnp.where(lane < NH, sp, 0.0)
    cs = jnp.dot(tri_ref[...], dt * a, precision=lax.Precision.HIGHEST, preferred_element_type=f32)
    cs_last = cs[CH - 1:CH, :]
    dte = jnp.exp(cs_last - cs)
    ecs = jnp.exp(cs)
    ecl = jnp.exp(cs_last)
    e = e_ref[...]
    row8 = lax.broadcasted_iota(jnp.int32, (8, CH), 0)
    r8 = _split_dot(jnp.where(row8 == 0, ecl, jnp.where(row8 == 1, dsk, 0.0)), e, 3)
    return dict(a=a, dtr=dtr, dt=dt, cs=cs, cst=cs.T, dte=dte, ecs=ecs, ecl=ecl, e=e, lane=lane,
                dt_x=_split_dot(dt, e, EXPAND_TERMS), dte_x=_split_dot(dte, e, EXPAND_TERMS),
                ecs_x=_split_dot(ecs, e, EXPAND_TERMS),
                ecl_x=r8[0:1], dsk_x=r8[1:2])


def _decay_matrix(c, h):
    li = lax.broadcasted_iota(jnp.int32, (CH, CH), 0)
    seg = c["cs"][:, h:h + 1] - c["cst"][h:h + 1, :]
    return jnp.exp(jnp.where(li >= c["lane"], seg, -jnp.inf))


def _gate_norm_fwd(y, z, gs):
    zg = z * _sigmoid(z)
    yg = y * zg
    return jnp.concatenate([_rms_fwd(yg[:, k * GW:(k + 1) * GW], gs[:, k * GW:(k + 1) * GW]) for k in range(NG)], axis=1)


def ssd_fwd(xbcs, proj, par, gs, cat, seq, name):
    t = xbcs.shape[0]
    nc = seq // CH
    expand, tri = _ssd_consts()

    def body(xs_ref, b_ref, c_ref, dtr_ref, z_ref, par_ref, e_ref, tri_ref, gs_ref, cat_ref, yn_ref, y_ref, st_ref,
             p_scr, yd_scr):
        @pl.when(pl.program_id(0) % nc == 0)
        def _():
            p_scr[...] = jnp.zeros_like(p_scr)

        c = _ssd_common(par_ref, dtr_ref, e_ref, tri_ref)
        xs = xs_ref[...]
        xdt = xs * c["dt_x"]
        xdt_b = xdt.astype(bf16)
        xdte_b = (xdt * c["dte_x"]).astype(bf16)
        p = p_scr[...]
        st_ref[0] = p
        p_b = p.astype(bf16)
        lo = c["lane"] < HP
        for g in range(NG):
            bg = b_ref[:, g * NS:(g + 1) * NS].astype(bf16)
            cg = c_ref[:, g * NS:(g + 1) * NS].astype(bf16)
            gmat = _dot_nt(cg, bg)
            for q in range(GW // CH):
                col = g * GW + q * CH
                xp = xdt_b[:, col:col + CH]
                h0 = col // HP
                m0 = (gmat * _decay_matrix(c, h0)).astype(bf16)
                m1 = (gmat * _decay_matrix(c, h0 + 1)).astype(bf16)
                stacked = jnp.concatenate([jnp.where(lo, xp, jnp.zeros_like(xp)),
                                           jnp.where(lo, jnp.zeros_like(xp), xp)], axis=0)
                yd_scr[:, col:col + CH] = _dot(jnp.concatenate([m0, m1], axis=1), stacked)
            gsl = slice(g * GW, (g + 1) * GW)
            yoff = _dot(cg, p_b[:, gsl]) * c["ecs_x"][:, gsl]
            yd_scr[:, gsl] = yd_scr[:, gsl] + yoff
            p_scr[:, gsl] = p[:, gsl] * c["ecl_x"][:, gsl] + _dot_tn(bg, xdte_b[:, gsl])
        y = yd_scr[...] + c["dsk_x"] * xs
        y_ref[...] = y
        yn_ref[...] = _gate_norm_fwd(y, z_ref[...], gs_ref[...]).astype(bf16)

    nb = t // CH
    return pl.pallas_call(
        body, name=name, grid=(nb,),
        in_specs=[pl.BlockSpec((CH, D), lambda i: (i, 0)),
                  pl.BlockSpec((CH, NG * NS), lambda i: (i, D // (NG * NS))),
                  pl.BlockSpec((CH, NG * NS), lambda i: (i, D // (NG * NS) + 1)),
                  pl.BlockSpec((CH, CH), lambda i: (i, COL_DT // CH)),
                  pl.BlockSpec((CH, D), lambda i: (i, COL_Z // D)),
                  pl.BlockSpec((8, CH), lambda i: (0, 0)), pl.BlockSpec((CH, D), lambda i: (0, 0)),
                  pl.BlockSpec((CH, CH), lambda i: (0, 0)), pl.BlockSpec((1, D), lambda i: (0, 0)),
                  pl.BlockSpec(memory_space=pl.ANY)],
        out_specs=[pl.BlockSpec((CH, D), lambda i: (i, 1)), pl.BlockSpec((CH, D), lambda i: (i, 0)),
                   pl.BlockSpec((1, NS, D), lambda i: (i, 0, 0))],
        out_shape=[_sds((t, 2 * D), bf16), _sds((t, D), f32), _sds((nb, NS, D), f32)],
        input_output_aliases={9: 0},
        scratch_shapes=[pltpu.VMEM((NS, D), f32), pltpu.VMEM((CH, D), f32)],
        compiler_params=_cparams(1))(xbcs, xbcs, xbcs, proj, proj, par, expand, tri, gs, cat)


def ssd_bwd(xbcs, proj, ypre, states, dcat, par, gs, dproj, seq, name):
    t = xbcs.shape[0]
    nc = seq // CH
    expand, tri = _ssd_consts()

    def body(xs_ref, b_ref, c_ref, dtr_ref, z_ref, y_ref, st_ref, dyn_ref, par_ref, e_ref, tri_ref, gs_ref, dproj_ref,
             dx_ref, dz_ref, ddt_ref, dpar_ref, dgs_ref, dp_scr, dxdt_scr):
        i = pl.program_id(0)

        @pl.when(i % nc == 0)
        def _():
            dp_scr[...] = jnp.zeros_like(dp_scr)

        c = _ssd_common(par_ref, dtr_ref, e_ref, tri_ref)
        e = c["e"]
        lane = c["lane"]
        sub = lax.broadcasted_iota(jnp.int32, (CH, CH), 0)
        xs = xs_ref[...]
        xdt = xs * c["dt_x"]
        xdt_b = xdt.astype(bf16)
        xdte_b = (xdt * c["dte_x"]).astype(bf16)
        p = st_ref[0]
        p_b = p.astype(bf16)
        dpn = dp_scr[...]
        dpn_b = dpn.astype(bf16)

        y, z, gs_v = y_ref[...], z_ref[...], gs_ref[...]
        zs = _sigmoid(z)
        zg = z * zs
        yg = y * zg
        parts, gparts = [], []
        for k in range(NG):
            sl = slice(k * GW, (k + 1) * GW)
            dxk, dgk = _rms_bwd(yg[:, sl], gs_v[:, sl], dyn_ref[:, sl])
            parts.append(dxk)
            gparts.append(dgk)
        dyg = jnp.concatenate(parts, axis=1)
        dgs_rows = jnp.concatenate(gparts, axis=1)
        dy = dyg * zg
        dz_ref[...] = (dyg * y * (zs * (1.0 + z * (1.0 - zs)))).astype(bf16)
        dy_b = dy.astype(bf16)
        dq_b = (dy * c["ecs_x"]).astype(bf16)

        lo = lane < HP
        dcs = jnp.zeros((CH, CH), f32)
        dcst = jnp.zeros((CH, CH), f32)
        for g in range(NG):
            gsl = slice(g * GW, (g + 1) * GW)
            bg = b_ref[:, g * NS:(g + 1) * NS].astype(bf16)
            cg = c_ref[:, g * NS:(g + 1) * NS].astype(bf16)
            gmat = _dot_nt(cg, bg)
            dgm = jnp.zeros((CH, CH), f32)
            for q in range(GW // CH):
                col = g * GW + q * CH
                xp = xdt_b[:, col:col + CH]
                dyp = dy_b[:, col:col + CH]
                zero = jnp.zeros_like(dyp)
                xp2 = jnp.concatenate([jnp.where(lo, xp, zero), jnp.where(lo, zero, xp)], axis=0)
                dy2 = jnp.concatenate([jnp.where(lo, dyp, zero), jnp.where(lo, zero, dyp)], axis=0)
                dm2 = _dot_nt(dyp, xp2)
                ms = []
                for hh in range(2):
                    h = col // HP + hh
                    dec = _decay_matrix(c, h)
                    m = gmat * dec
                    dm = dm2[:, hh * CH:(hh + 1) * CH]
                    dseg = dm * m
                    dcs = dcs + jnp.where(lane == h, jnp.sum(dseg, axis=1, keepdims=True), 0.0)
                    dcst = dcst + jnp.where(sub == h, jnp.sum(dseg, axis=0, keepdims=True), 0.0)
                    dgm = dgm + dm * dec
                    ms.append(m.astype(bf16))
                dxdt_scr[:, col:col + CH] = _dot_tn(jnp.concatenate(ms, axis=0), dy2)
            dgm_b = dgm.astype(bf16)
            bds = _dot(bg, dpn_b[:, gsl])
            dxdt_scr[:, gsl] = dxdt_scr[:, gsl] + c["dte_x"][:, gsl] * bds
            dc_g = _dot(dgm_b, bg) + _dot_nt(dq_b[:, gsl], p_b[:, gsl])
            db_g = _dot_tn(dgm_b, cg) + _dot_nt(xdte_b[:, gsl], dpn_b[:, gsl])
            dx_ref[:, D + g * NS:D + (g + 1) * NS] = db_g
            dx_ref[:, D + NG * NS + g * NS:D + NG * NS + (g + 1) * NS] = dc_g
            dp_scr[:, gsl] = dpn[:, gsl] * c["ecl_x"][:, gsl] + _dot_tn(cg, dq_b[:, gsl])
            q_g = _dot(cg, p_b[:, gsl])
            e_g = e[:, gsl]
            dcs = dcs + c["ecs"] * _split_dot(dy[:, gsl] * q_g, e_g, REDUCE_TERMS, nt=True)
            ddte = _split_dot(xdt[:, gsl] * bds, e_g, REDUCE_TERMS, nt=True) * c["dte"]
            dcs = dcs - ddte
            dcs = dcs + jnp.where(sub == CH - 1, jnp.sum(ddte, axis=0, keepdims=True), 0.0)

        decl = _split_dot(jnp.broadcast_to(jnp.sum(dpn * p, axis=0, keepdims=True), (8, D)), e, 2, nt=True)[0:1]
        dcs = dcs + jnp.where(sub == CH - 1, c["ecl"] * decl, 0.0)
        dcs = dcs - dcst.T
        dadt = lax.dot_general(tri_ref[...], dcs, (((0,), (0,)), ((), ())), precision=lax.Precision.HIGHEST,
                               preferred_element_type=f32)
        dxdt = dxdt_scr[...]
        ddt = dadt * c["a"] + _split_dot(dxdt * xs, e, REDUCE_TERMS, nt=True)
        ddtr = jnp.where(lane < NH, ddt * _sigmoid(c["dtr"]), 0.0)
        ddt_ref[...] = ddtr.astype(bf16)
        dx_ref[:, 0:D] = dxdt * c["dt_x"] + c["dsk_x"] * dy
        dsk = _split_dot(jnp.broadcast_to(jnp.sum(dy * xs, axis=0, keepdims=True), (8, D)), e, 2, nt=True)[0:1]
        dalog = jnp.sum(dadt * c["dt"], axis=0, keepdims=True) * c["a"]
        row8 = lax.broadcasted_iota(jnp.int32, (8, CH), 0)
        dpar = jnp.where(row8 == 0, jnp.sum(ddtr, axis=0, keepdims=True),
                         jnp.where(row8 == 1, dalog, jnp.where(row8 == 2, dsk, 0.0)))
        dpar = jnp.where(lax.broadcasted_iota(jnp.int32, (8, CH), 1) < NH, dpar, 0.0)
        _accum(dpar_ref, dpar, i == 0)
        _accum(dgs_ref, jnp.sum(dgs_rows, axis=0, keepdims=True), i == 0)

    nb = t // CH
    rev = lambda i: (i // nc) * nc + (nc - 1 - i % nc)
    return pl.pallas_call(
        body, name=name, grid=(nb,),
        in_specs=[pl.BlockSpec((CH, D), lambda i: (rev(i), 0)),
                  pl.BlockSpec((CH, NG * NS), lambda i: (rev(i), D // (NG * NS))),
                  pl.BlockSpec((CH, NG * NS), lambda i: (rev(i), D // (NG * NS) + 1)),
                  pl.BlockSpec((CH, CH), lambda i: (rev(i), COL_DT // CH)),
                  pl.BlockSpec((CH, D), lambda i: (rev(i), COL_Z // D)),
                  pl.BlockSpec((CH, D), lambda i: (rev(i), 0)),
                  pl.BlockSpec((1, NS, D), lambda i: (rev(i), 0, 0)),
                  pl.BlockSpec((CH, D), lambda i: (rev(i), 1)),
                  pl.BlockSpec((8, CH), lambda i: (0, 0)), pl.BlockSpec((CH, D), lambda i: (0, 0)),
                  pl.BlockSpec((CH, CH), lambda i: (0, 0)), pl.BlockSpec((1, D), lambda i: (0, 0)),
                  pl.BlockSpec(memory_space=pl.ANY)],
        out_specs=[pl.BlockSpec((CH, XBC), lambda i: (rev(i), 0)), pl.BlockSpec((CH, D), lambda i: (rev(i), COL_Z // D)),
                   pl.BlockSpec((CH, CH), lambda i: (rev(i), 0)),
                   pl.BlockSpec((8, CH), lambda i: (0, 0)), pl.BlockSpec((1, D), lambda i: (0, 0))],
        out_shape=[_sds((t, XBC), f32), _sds((t, PROJ), bf16), _sds((t, CH), bf16), _sds((8, CH), f32), _sds((1, D), f32)],
        input_output_aliases={12: 1},
        scratch_shapes=[pltpu.VMEM((NS, D), f32), pltpu.VMEM((CH, D), f32)],
        compiler_params=_cparams(1))(xbcs, xbcs, xbcs, proj, proj, ypre, states, dcat, par, expand, tri, gs, dproj)


def loss_head(y, target, tb, name):
    t = y.shape[0]

    def body(y_ref, t_ref, s_ref, dy_ref):
        err = y_ref[...] - t_ref[...]
        dy_ref[...] = err * (1.0 / D)
        _accum(s_ref, jnp.zeros((8, CH), f32) + jnp.sum(err * err), pl.program_id(0) == 0)

    return pl.pallas_call(
        body, name=name, grid=(t // tb,),
        in_specs=[pl.BlockSpec((tb, D), lambda i: (i, 0)), pl.BlockSpec((tb, D), lambda i: (i, 0))],
        out_specs=[pl.BlockSpec((8, CH), lambda i: (0, 0)), pl.BlockSpec((tb, D), lambda i: (i, 0))],
        out_shape=[_sds((8, CH), f32), _sds((t, D), f32)],
        compiler_params=_cparams(1))(y, target)


def _tiles(t, seq):
    tm = min(512, t)
    return dict(tm=tm, tm_small=min(256, t), tm_large=min(1024, t), tb=min(512, seq))


def local_step(x, target, depth, weights_of, seq, grads_done=None):
    t = x.shape[0]
    ts = _tiles(t, seq)
    tm, tl, tb = ts["tm"], ts["tm_large"], ts["tb"]
    saved, ws = [], []
    for l in range(depth):
        w = weights_of(l, x)
        ws.append(w)
        proj, h1 = norm_matmul(x, w["g1"], w["win"], tl, 1152, f32, "in_proj", token=w.get("token"))
        cat = group_a_fwd(proj, w["wa"], w["ga"], seq, tb, "group_a_fwd")
        xbcs = conv_b_fwd(proj, w["ws"], w["bs"], seq, tb, "conv_b_fwd")
        cat, ypre, states = ssd_fwd(xbcs, proj, w["par"], w["gs"], cat, seq, "ssd_fwd")
        if "late" in w:
            w.update(w.pop("late")(cat))
        mix, x2 = matmul_postnorm(cat, w["wo"], x, w["g2"], tl, False, "out_proj")
        fp, h2 = norm_matmul(x2, w["g3"], w["wu"], tl, 1024, bf16, "mlp_up")
        o, x3 = matmul_postnorm(fp, w["wd"], x2, w["g4"], tm, True, "mlp_down")
        saved.append(dict(x=x, proj=proj, h1=h1, xbcs=xbcs, ypre=ypre, states=states, cat=cat, mix=mix, x2=x2,
                          fp=fp, h2=h2, o=o))
        x = x3
    sse, dx = loss_head(x, target, tm, "loss_head")
    grads = [None] * depth
    for l in reversed(range(depth)):
        s, w = saved[l], ws[l]
        do, dg4, dfp = postnorm_bwd_matmul(s["o"], w["g4"], dx, w["wd"], s["fp"], tl, 1024, bf16, "mlp_down_bwd")
        dwd = matmul_tn(s["fp"], do, 512, 1024, True, "mlp_down_dw")
        dx2, dg3 = matmul_prenorm_bwd(dfp, w["wu"], s["x2"], w["g3"], dx, tm, "mlp_up_bwd")
        dwu = matmul_tn(s["h2"], dfp, 512, 1024, False, "mlp_up_dw", col_blocks=True)
        dmix, dg2, dcat = postnorm_bwd_matmul(s["mix"], w["g2"], dx2, w["wo"], None, tl, 1024, f32, "out_proj_bwd")
        dwo = matmul_tn(s["cat"], dmix, 512, 1024, False, "out_proj_dw")
        token = None if grads_done is None else grads_done(l, dict(wo=dwo, wu=dwu, wd=dwd), False)
        dproj, dwa, dga = group_a_bwd(s["proj"], dcat, w["wa"], w["ga"], seq, tb, "group_a_bwd", token=token)
        dxbcs, dproj, ddt, dpar, dgs = ssd_bwd(s["xbcs"], s["proj"], s["ypre"], s["states"], dcat, w["par"], w["gs"],
                                               dproj, seq, "ssd_bwd")
        dproj, dws, dbs = conv_b_bwd(s["proj"], dxbcs, w["ws"], w["bs"], dproj, seq, tb, "conv_b_bwd")
        dproj = place_columns(dproj, ddt, COL_DT // CH, tm, "place_ddt")
        dwin = matmul_tn(s["h1"], dproj, 512, 1152, False, "in_proj_dw")
        token = None if grads_done is None else grads_done(l, dict(win=dwin), True)
        dx, dg1 = matmul_prenorm_bwd(dproj, w["win"], s["x"], w["g1"], dx2, ts["tm_small"], "in_proj_bwd", token=token)
        grads[l] = dict(win=dwin, wo=dwo, wu=dwu, wd=dwd, wa=dwa, ws=dws, bs=dbs, par=dpar,
                        g1=dg1, ga=dga, gs=dgs, g2=dg2, g3=dg3, g4=dg4)
    return sse, dx, grads


GROUPS = {
    "chips": [(1, 0, 0), (0, 1, 0), (1, 1, 0)],
    "pair": [(0, 0, 1)],
    "all": [(1, 0, 0), (0, 1, 0), (1, 1, 0), (0, 0, 1), (1, 0, 1), (0, 1, 1), (1, 1, 1)],
}


def _group_index(group, x, y, c):
    return {"chips": 2 * x + y, "pair": c, "all": 4 * x + 2 * y + c}[group]


def _chunk_indices(shape, pieces):
    if len(shape) < 3:
        return [()]
    lead = [()]
    for n in shape[:-2]:
        lead = [i + (k,) for i in lead for k in range(n)]
    rows = shape[-2]
    split = max(1, pieces // len(lead))
    while split > 1 and (rows % split or (rows // split) % 16):
        split -= 1
    step = rows // split
    return [i + (pl.ds(s * step, step),) for i in lead for s in range(split)]


def _exchange(arrays, out_shapes, group, src_view, dst_view, view_shape, name, own, pieces=16):
    masks = GROUPS[group]
    na, nm = len(arrays), len(masks)
    cuts = [_chunk_indices(view_shape(a), pieces) for a in range(na)]

    def body(*refs):
        ins, outs = refs[:na], refs[na:2 * na]
        send_sems, recv_sems = refs[2 * na:2 * na + 2]
        local_sems = refs[2 * na + 2] if own else None
        x, y, c = lax.axis_index("x"), lax.axis_index("y"), lax.axis_index("c")
        me = _group_index(group, x, y, c)
        peers = []
        for mx, my, mc in masks:
            px, py, pc = (1 - x if mx else x), (1 - y if my else y), (1 - c if mc else c)
            peers.append(((px, py, pc), _group_index(group, px, py, pc)))

        def part(ref, idx):
            return ref.at[idx] if idx else ref

        if own:
            for a in range(na):
                for idx in cuts[a]:
                    pltpu.make_async_copy(part(src_view(ins[a], a, me), idx), part(dst_view(outs[a], a, me), idx),
                                          local_sems.at[a]).start()
        for a in range(na):
            for j, (dev, pidx) in enumerate(peers):
                for idx in cuts[a]:
                    pltpu.make_async_remote_copy(
                        src_ref=part(src_view(ins[a], a, pidx), idx), dst_ref=part(dst_view(outs[a], a, me), idx),
                        send_sem=send_sems.at[a * nm + j], recv_sem=recv_sems.at[a * nm + j],
                        device_id=dev, device_id_type=MESH).start()
        whole = []
        for a in range(na):
            for j, (dev, pidx) in enumerate(peers):
                whole.append(pltpu.make_async_remote_copy(
                    src_ref=src_view(ins[a], a, pidx), dst_ref=dst_view(outs[a], a, pidx),
                    send_sem=send_sems.at[a * nm + j], recv_sem=recv_sems.at[a * nm + j],
                    device_id=dev, device_id_type=MESH))
        for cp in whole:
            cp.wait_recv()
        for cp in whole:
            cp.wait_send()
        if own:
            for a in range(na):
                pltpu.make_async_copy(src_view(ins[a], a, me), dst_view(outs[a], a, me), local_sems.at[a]).wait()

    hbm = pl.BlockSpec(memory_space=pltpu.HBM)
    sems = [pltpu.SemaphoreType.DMA((na * nm,)), pltpu.SemaphoreType.DMA((na * nm,))]
    return pl.pallas_call(
        body, name=name, in_specs=[hbm] * na, out_specs=[hbm] * na,
        out_shape=[_sds(s, a.dtype) for s, a in zip(out_shapes, arrays)],
        scratch_shapes=sems + ([pltpu.SemaphoreType.DMA((na,))] if own else []))(*arrays)


def all_gather(arrays, group, name, slot_axis=0, own=True):
    n = len(GROUPS[group]) + 1
    shapes = [a.shape[:slot_axis] + (n,) + a.shape[slot_axis:] for a in arrays]
    lead = (slice(None),) * slot_axis
    return _exchange(arrays, shapes, group, lambda r, a, i: r, lambda r, a, i: r.at[lead + (i,)],
                     lambda a: arrays[a].shape, name, own)


HBM_SPEC = pl.BlockSpec(memory_space=pltpu.HBM)
SEM_SPEC = pl.BlockSpec(memory_space=pltpu.SEMAPHORE)
DATAFLOW = pltpu.SideEffectType.DATAFLOW_SIDE_EFFECTING
N_CHIPS = 4


def _chip_peers(x, y, c):
    out = []
    for mx, my, _ in GROUPS["chips"]:
        px, py = (1 - x if mx else x), (1 - y if my else y)
        out.append(((px, py, c), 2 * px + py))
    return out


def _weight_views(shards):
    half = [s.shape[0] // 2 for s in shards]
    return dict(src=lambda ref, a, c, to_chip: ref.at[pl.ds(c * half[a], half[a])],
                dst=lambda ref, a, c, from_chip: ref.at[from_chip, pl.ds(c * half[a], half[a])],
                rows=lambda a: half[a])


def _grad_views(sums):
    return dict(src=lambda ref, a, c, to_chip: ref.at[to_chip], dst=lambda ref, a, c, from_chip: ref.at[from_chip],
                rows=lambda a: sums[a].shape[1])


def chips_start(sources, zones, views, name, pieces=4, after=None):
    na, nm = len(sources), N_CHIPS - 1

    def body(*refs):
        ins, lands = refs[:na], refs[na:2 * na]
        n_in = 2 * na + len(_token_arg(after))
        send_sems, recv_sems, token = refs[n_in], refs[n_in + 1], refs[-1]
        x, y, c = lax.axis_index("x"), lax.axis_index("y"), lax.axis_index("c")
        chip = 2 * x + y
        for a in range(na):
            step = views["rows"](a) // pieces
            for j, (dev, to_chip) in enumerate(_chip_peers(x, y, c)):
                for q in range(pieces):
                    rows = pl.ds(q * step, step)
                    pltpu.make_async_remote_copy(
                        src_ref=views["src"](ins[a], a, c, to_chip).at[rows],
                        dst_ref=views["dst"](lands[a], a, c, chip).at[rows],
                        send_sem=send_sems.at[a * nm + j], recv_sem=recv_sems.at[a * nm + j],
                        device_id=dev, device_id_type=MESH).start()
        token[...] = jnp.zeros_like(token)

    both = list(sources) + list(zones)
    outs = pl.pallas_call(
        body, name=name,
        out_shape=(pltpu.SemaphoreType.DMA((na * nm,)), pltpu.SemaphoreType.DMA((na * nm,)),
                   *[pltpu.HBM(b.shape, b.dtype) for b in both], _sds((8, CH), f32)),
        in_specs=[HBM_SPEC] * (2 * na) + _token_spec(after),
        out_specs=(SEM_SPEC, SEM_SPEC, *[HBM_SPEC] * (2 * na), pl.BlockSpec(memory_space=pltpu.VMEM)),
        input_output_aliases={i: 2 + i for i in range(2 * na)},
        compiler_params=pltpu.CompilerParams(has_side_effects=DATAFLOW))(
            *[pltpu.with_memory_space_constraint(b, pltpu.HBM) for b in both], *_token_arg(after))
    return dict(send=outs[0], recv=outs[1], sources=list(outs[2:2 + na]), zones=list(outs[2 + na:2 + 2 * na]),
                token=outs[-1], views=views)


def chips_wait(started, after, name):
    sources, zones, views = started["sources"], started["zones"], started["views"]
    na, nm = len(sources), N_CHIPS - 1

    def body(*refs):
        ins, lands = refs[:na], refs[na:2 * na]
        send_sems, recv_sems = refs[2 * na], refs[2 * na + 1]
        x, y, c = lax.axis_index("x"), lax.axis_index("y"), lax.axis_index("c")
        for a in range(na):
            for j, (dev, peer_chip) in enumerate(_chip_peers(x, y, c)):
                cp = pltpu.make_async_remote_copy(
                    src_ref=views["src"](ins[a], a, c, peer_chip), dst_ref=views["dst"](lands[a], a, c, peer_chip),
                    send_sem=send_sems.at[a * nm + j], recv_sem=recv_sems.at[a * nm + j],
                    device_id=dev, device_id_type=MESH)
                cp.wait_send()
                cp.wait_recv()

    both = list(sources) + list(zones)
    outs = pl.pallas_call(
        body, name=name, out_shape=tuple(pltpu.HBM(b.shape, b.dtype) for b in both),
        in_specs=[HBM_SPEC] * (2 * na) + [SEM_SPEC, SEM_SPEC, pl.BlockSpec(memory_space=pl.ANY)],
        out_specs=tuple([HBM_SPEC] * (2 * na)), input_output_aliases={i: i for i in range(2 * na)},
        compiler_params=pltpu.CompilerParams(has_side_effects=DATAFLOW))(*both, started["send"], started["recv"], after)
    return list(outs[:na]), list(outs[na:])


def weights_share(zones, name):
    na, nm = len(zones), N_CHIPS - 1

    def body(*refs):
        lands = refs[na:2 * na]
        send_sems, recv_sems = refs[2 * na:]
        x, y, c = lax.axis_index("x"), lax.axis_index("y"), lax.axis_index("c")
        chip = 2 * x + y
        sibling = (x, y, 1 - c)
        sends = []
        for a in range(na):
            half = zones[a].shape[1] // 2
            for m in range(1, N_CHIPS):
                mine = lands[a].at[chip ^ m, pl.ds(c * half, half)]
                sends.append(pltpu.make_async_remote_copy(
                    src_ref=mine, dst_ref=mine, send_sem=send_sems.at[a * nm + m - 1],
                    recv_sem=recv_sems.at[a * nm + m - 1], device_id=sibling, device_id_type=MESH))
        for cp in sends:
            cp.start()
        for a in range(na):
            half = zones[a].shape[1] // 2
            for m in range(1, N_CHIPS):
                theirs = lands[a].at[chip ^ m, pl.ds((1 - c) * half, half)]
                pltpu.make_async_remote_copy(
                    src_ref=theirs, dst_ref=theirs, send_sem=send_sems.at[a * nm + m - 1],
                    recv_sem=recv_sems.at[a * nm + m - 1], device_id=sibling, device_id_type=MESH).wait_recv()
        for cp in sends:
            cp.wait_send()

    return pl.pallas_call(
        body, name=name, in_specs=[HBM_SPEC] * na, out_specs=[HBM_SPEC] * na,
        out_shape=[_sds(z.shape, z.dtype) for z in zones], input_output_aliases={i: i for i in range(na)},
        scratch_shapes=[pltpu.SemaphoreType.DMA((na * nm,)), pltpu.SemaphoreType.DMA((na * nm,))])(*zones)


def pair_send_halves(grads, name):
    half = [g.shape[1] // 2 for g in grads]
    shapes = [(g.shape[0], h, g.shape[2]) for g, h in zip(grads, half)]
    return _exchange(grads, shapes, "pair", lambda r, a, i: r.at[:, pl.ds(i * half[a], half[a])],
                     lambda r, a, i: r, lambda a: shapes[a], name, False)


def sum_pair_half(g, recv, core, name, tb=256, by_chip=None):
    nk, r, c = g.shape
    tb = min(tb, r // 2)
    nb = r // 2 // tb

    def body(core_ref, g_ref, r_ref, o_ref):
        s = g_ref[...] + r_ref[...]
        if by_chip is None:
            o_ref[...] = s.astype(bf16)
        else:
            for k in range(by_chip[0]):
                o_ref[k] = s[:, k * by_chip[1]:(k + 1) * by_chip[1]].astype(bf16)

    if by_chip is None:
        out_spec = pl.BlockSpec((None, tb, c), lambda k, i, core_ref: (k, i, 0))
        out_shape = _sds((nk, r // 2, c), bf16)
    else:
        assert nk == 1
        out_spec = pl.BlockSpec((by_chip[0], tb, by_chip[1]), lambda k, i, core_ref: (0, i, 0))
        out_shape = _sds((by_chip[0], r // 2, by_chip[1]), bf16)
    return pl.pallas_call(
        body, name=name,
        grid_spec=pltpu.PrefetchScalarGridSpec(
            num_scalar_prefetch=1, grid=(nk, nb),
            in_specs=[pl.BlockSpec((None, tb, c), lambda k, i, core_ref: (k, core_ref[0] * nb + i, 0)),
                      pl.BlockSpec((None, tb, c), lambda k, i, core_ref: (k, i, 0))],
            out_specs=out_spec),
        out_shape=out_shape, compiler_params=_cparams(2))(jnp.reshape(core, (1,)).astype(jnp.int32), g, recv)


def assemble_columns(blocks, width, name, tb=256):
    n, r, c = blocks.shape

    def body(b_ref, o_ref):
        for k in range(n):
            o_ref[:, k * c:(k + 1) * c] = b_ref[k]
        o_ref[:, n * c:] = jnp.zeros((tb, width - n * c), blocks.dtype)

    return pl.pallas_call(
        body, name=name, grid=(r // tb,), in_specs=[pl.BlockSpec((n, tb, c), lambda i: (0, i, 0))],
        out_specs=pl.BlockSpec((tb, width), lambda i: (i, 0)), out_shape=_sds((r, width), blocks.dtype),
        compiler_params=_cparams(1))(blocks)


def chip_sum_into(acc, layer, own, others, chip, name, tb=256):
    n, r, c = own.shape
    tb = min(tb, r)

    def body(chip_ref, x_ref, y1_ref, y2_ref, y3_ref, acc_ref, o_ref):
        o_ref[...] = ((x_ref[...].astype(f32) + y1_ref[...].astype(f32)) + y2_ref[...].astype(f32)) + y3_ref[...].astype(f32)

    def slot(k):
        return pl.BlockSpec((None, tb, c), lambda i, chip_ref: (chip_ref[0] ^ k, i, 0))

    return pl.pallas_call(
        body, name=name,
        grid_spec=pltpu.PrefetchScalarGridSpec(
            num_scalar_prefetch=1, grid=(r // tb,),
            in_specs=[slot(k) for k in range(n)] + [pl.BlockSpec(memory_space=pl.ANY)],
            out_specs=pl.BlockSpec((None, tb, c), lambda i, chip_ref: (layer, i, 0))),
        out_shape=_sds(acc.shape, f32), input_output_aliases={n + 1: 0}, compiler_params=_cparams(1))(
            jnp.reshape(chip, (1,)).astype(jnp.int32), own, *([others] * (n - 1)), acc)


def adamw_halves(w, g_own, g_recv, m, v, core, name, tb=256):
    depth, r, c = w.shape
    tb = min(tb, r // 2)
    nb = r // 2 // tb

    def body(core_ref, w_ref, go_ref, gr_ref, m_ref, v_ref, g_ref, d_ref, mo_ref, vo_ref):
        gv = jnp.where(pl.program_id(1) == core_ref[0], go_ref[...], gr_ref[...])
        m2 = B1 * m_ref[...] + (1.0 - B1) * gv
        v2 = B2 * v_ref[...] + (1.0 - B2) * (gv * gv)
        m_hat = m2 / (1.0 - B1 ** STEP)
        v_hat = v2 / (1.0 - B2 ** STEP)
        g_ref[...] = gv
        d_ref[...] = -LR * (m_hat / (jnp.sqrt(v_hat) + AEPS) + WD * w_ref[...])
        mo_ref[...] = m2
        vo_ref[...] = v2

    whole = pl.BlockSpec((None, tb, c), lambda l, h, i, core_ref: (l, h * nb + i, 0))
    part = pl.BlockSpec((None, tb, c), lambda l, h, i, core_ref: (l, i, 0))
    return pl.pallas_call(
        body, name=name,
        grid_spec=pltpu.PrefetchScalarGridSpec(num_scalar_prefetch=1, grid=(depth, 2, nb),
                                               in_specs=[whole, part, part, whole, whole], out_specs=[whole] * 4),
        out_shape=[_sds(w.shape, f32)] * 4, compiler_params=_cparams(3))(
            jnp.reshape(core, (1,)).astype(jnp.int32), w, g_own, g_recv, m, v)


def sum_slots(y, out_dtype, name, tb=256):
    n, r, c = y.shape
    tb = min(tb, r)

    def body(y_ref, o_ref):
        acc = y_ref[0].astype(f32)
        for i in range(1, n):
            acc = acc + y_ref[i].astype(f32)
        o_ref[...] = acc.astype(out_dtype)

    return pl.pallas_call(
        body, name=name, grid=(r // tb,),
        in_specs=[pl.BlockSpec((n, tb, c), lambda i: (0, i, 0))], out_specs=pl.BlockSpec((tb, c), lambda i: (i, 0)),
        out_shape=_sds((r, c), out_dtype), compiler_params=_cparams(1))(y)


def adamw(w, g, m, v, name, tb=256):
    r, c = w.shape
    tb = min(tb, r)

    def body(w_ref, g_ref, m_ref, v_ref, d_ref, mo_ref, vo_ref):
        gv = g_ref[...]
        m2 = B1 * m_ref[...] + (1.0 - B1) * gv
        v2 = B2 * v_ref[...] + (1.0 - B2) * (gv * gv)
        m_hat = m2 / (1.0 - B1 ** STEP)
        v_hat = v2 / (1.0 - B2 ** STEP)
        d_ref[...] = -LR * (m_hat / (jnp.sqrt(v_hat) + AEPS) + WD * w_ref[...])
        mo_ref[...] = m2
        vo_ref[...] = v2

    spec = pl.BlockSpec((tb, c), lambda i: (i, 0))
    return pl.pallas_call(
        body, name=name, grid=(r // tb,), in_specs=[spec] * 4, out_specs=[spec] * 3,
        out_shape=[_sds((r, c), f32)] * 3, compiler_params=_cparams(1))(w, g, m, v)


def adamw_leading(w, g, m, v, name, tc=64):
    c, l, r = w.shape
    main = c // tc
    tail = c - main * tc

    def body(w_ref, g_ref, m_ref, v_ref, *rest):
        d_ref, mo_ref, vo_ref = rest[-3:]
        gv = g_ref[...]
        m2 = B1 * m_ref[...] + (1.0 - B1) * gv
        v2 = B2 * v_ref[...] + (1.0 - B2) * (gv * gv)
        m_hat = m2 / (1.0 - B1 ** STEP)
        v_hat = v2 / (1.0 - B2 ** STEP)
        d_ref[...] = -LR * (m_hat / (jnp.sqrt(v_hat) + AEPS) + WD * w_ref[...])
        mo_ref[...] = m2
        vo_ref[...] = v2

    spec = pl.BlockSpec((tc, l, r), lambda i: (i, 0, 0))
    outs = pl.pallas_call(
        functools.partial(body), name=name, grid=(main,), in_specs=[spec] * 4, out_specs=[spec] * 3,
        out_shape=[_sds(w.shape, f32)] * 3, compiler_params=_cparams(1))(w, g, m, v)
    if tail:
        assert (main * tc) % tail == 0
        last = pl.BlockSpec((tail, l, r), lambda i: (main * tc // tail, 0, 0))
        outs = pl.pallas_call(
            functools.partial(body), name=name + "_tail", grid=(1,),
            in_specs=[last] * 4 + [pl.BlockSpec(memory_space=pl.ANY)] * 3, out_specs=[last] * 3,
            out_shape=[_sds(w.shape, f32)] * 3, input_output_aliases={4: 0, 5: 1, 6: 2},
            compiler_params=_cparams(1))(w, g, m, v, *outs)
    return outs


SMALL_ROW = 1024
SMALL_GAINS = ("g1", "ga", "gs", "g2", "g3", "g4")
SMALL_LAYER_ROWS = 8 + 8 + 16 + 8


def _pack_small(grads):
    wide = lambda a: jnp.pad(a, ((0, 0), (0, 2 * SMALL_ROW - a.shape[1]))).reshape(-1, SMALL_ROW)
    row = lax.broadcasted_iota(jnp.int32, (8, SMALL_ROW), 0)
    parts = []
    for g in grads:
        singles = [g[k] for k in SMALL_GAINS] + [g["bs"][:, :SMALL_ROW],
                                                 jnp.pad(g["bs"][:, SMALL_ROW:], ((0, 0), (0, 2 * SMALL_ROW - XBC)))]
        first = sum(jnp.where(row == k, s, 0.0) for k, s in enumerate(singles))
        parts += [first, g["wa"], wide(g["ws"]), jnp.pad(g["par"], ((0, 0), (0, SMALL_ROW - CH)))]
    return jnp.concatenate(parts, axis=0)


def _unpack_small(packed, depth):
    rows = packed.reshape(depth, SMALL_LAYER_ROWS, SMALL_ROW)
    out = {k: rows[:, i] for i, k in enumerate(SMALL_GAINS)}
    out["bs"] = rows[:, 6:8].reshape(depth, 2 * SMALL_ROW)[:, :XBC]
    out["wa"] = rows[:, 8:11]
    out["ws"] = rows[:, 16:32].reshape(depth, 8, 2 * SMALL_ROW)[:, :4, :XBC]
    out["par"] = rows[:, 32:35, :CH]
    return out


def kernel(x, norm_mix_pre, w_in, conv_a_w, ssm_conv_w, ssm_conv_b, dt_bias, a_log, d_skip, conv_out_norm, ssm_out_norm, w_out, norm_mix_post, norm_mlp_pre, w_up, w_down, norm_mlp_post, loss_target, m_norm_mix_pre, m_w_in, m_conv_a_w, m_ssm_conv_w, m_ssm_conv_b, m_dt_bias, m_a_log, m_d_skip, m_conv_out_norm, m_ssm_out_norm, m_w_out, m_norm_mix_post, m_norm_mlp_pre, m_w_up, m_w_down, m_norm_mlp_post, v_norm_mix_pre, v_w_in, v_conv_a_w, v_ssm_conv_w, v_ssm_conv_b, v_dt_bias, v_a_log, v_d_skip, v_conv_out_norm, v_ssm_out_norm, v_w_out, v_norm_mix_post, v_norm_mlp_pre, v_w_up, v_w_down, v_norm_mlp_post):
    nb, seq, _ = x.shape
    t = nb * seq
    depth = w_in.shape[0]
    ncol = w_in.shape[2]
    chip = 2 * lax.axis_index("x") + lax.axis_index("y")

    taps = [conv_a_w, ssm_conv_w]
    taps_g = all_gather(taps, "chips", "gather_taps", slot_axis=1, own=False)
    wa_g, ws_g = [lax.dynamic_update_index_in_dim(g, s, chip, 1) for g, s in zip(taps_g, taps)]
    wa_full = jnp.transpose(wa_g, (0, 2, 1, 3)).reshape(depth, 3, D)
    ws_full = jnp.transpose(ws_g, (0, 2, 1, 3)).reshape(depth, 4, XBC)
    lane_pad = lambda a: jnp.pad(a, ((0, 0), (0, CH - a.shape[1])))
    par = jnp.stack([lane_pad(dt_bias), lane_pad(a_log), lane_pad(d_skip)], axis=1)
    par = jnp.pad(par, ((0, 0), (0, 5), (0, 0)))

    layer_shards = lambda l: [w_in[l].astype(bf16), w_out[l].astype(bf16), w_up[l].astype(bf16), w_down[l].astype(bf16)]
    issued = []

    def start(shards, name):
        zones = [lax.empty((N_CHIPS,) + s.shape, s.dtype) for s in shards]
        issued.append(chips_start(shards, zones, _weight_views(shards), name,
                                  after=issued[-1]["token"] if issued else taps_g[0]))
        return issued[-1]

    def finish(started, after, name):
        shards, zones = chips_wait(started, after, name)
        zones = weights_share(zones, "weights_share")
        return [lax.dynamic_update_index_in_dim(z, s, chip, 0) for z, s in zip(zones, shards)]

    def shaped(mats):
        wo_z, wu_z, wd_z = mats
        return wo_z.reshape(2 * D, D), wu_z, wd_z.reshape(DFF, D)

    first = layer_shards(0)
    travelling = {0: start(first[:1], "weights_start_0")}
    rest = start(first[1:], "weights_start_0_rest")
    for l in range(1, depth):
        travelling[l] = start(layer_shards(l), f"weights_start_{l}")

    def weights_of(l, x_in):
        mats = finish(travelling.pop(l), x_in, f"weights_wait_{l}")
        w = dict(win=assemble_columns(mats[0], PROJ, "assemble_w_in"), wa=jnp.pad(wa_full[l], ((0, 5), (0, 0))),
                 ws=jnp.pad(ws_full[l], ((0, 4), (0, 0))), bs=ssm_conv_b[l][None], par=par[l],
                 g1=norm_mix_pre[l][None], ga=conv_out_norm[l][None], gs=ssm_out_norm[l][None],
                 g2=norm_mix_post[l][None], g3=norm_mlp_pre[l][None], g4=norm_mlp_post[l][None])
        if l == 0:
            w["token"] = issued[-1]["token"]
            w["late"] = lambda after: dict(zip(("wo", "wu", "wd"), shaped(finish(rest, after, "weights_wait_0_rest"))))
        else:
            w.update(zip(("wo", "wu", "wd"), shaped(mats[1:])))
        return w

    core = lax.axis_index("c")
    grads_travelling = {}

    chip_major = dict(win=lambda a: a[None], wo=lambda a: a.reshape(N_CHIPS, 2 * D // N_CHIPS, D), wu=lambda a: a,
                      wd=lambda a: a.reshape(N_CHIPS, DFF // N_CHIPS, D))
    held = {}

    def grads_done(l, g, last):
        if l > 0 and not last:
            held[l] = g
            return None
        g = {**held.pop(l, {}), **g}
        keys = [k for k in ("win", "wo", "wu", "wd") if k in g]
        mats = [chip_major[k](g[k]) for k in keys]
        received = pair_send_halves(mats, "grads_to_pair")
        sums = [sum_pair_half(m_, r_, core, "pair_sum", by_chip=(N_CHIPS, ncol) if k == "win" else None)
                for k, m_, r_ in zip(keys, mats, received)]
        zones = [lax.empty(s.shape, s.dtype) for s in sums]
        started = chips_start(sums, zones, _grad_views(sums), f"grads_start_{l}_{len(grads_travelling)}")
        grads_travelling[(l, keys[0])] = (keys, started)
        return started["token"]

    sse, dx, grads = local_step(x.reshape(t, D), loss_target.reshape(t, D), depth, weights_of, seq, grads_done)
    loss = lax.psum(0.5 / D * sse[0, 0], ("x", "y", "c"))

    small_all = all_gather([_pack_small(grads)], "all", "gather_small")[0]
    small_sum = sum_slots(small_all, f32, "small_sum", tb=8)
    small = _unpack_small(small_sum, depth)

    big_w = dict(win=w_in, wo=w_out, wu=w_up, wd=w_down)
    acc = {k: lax.empty((depth, bw.shape[1] // 2, bw.shape[2]), f32) for k, bw in big_w.items()}
    for n, ((l, _), (keys, started)) in enumerate(grads_travelling.items()):
        sums, zones = chips_wait(started, small_sum, f"grads_wait_{l}_{n}")
        for k, s, z in zip(keys, sums, zones):
            acc[k] = chip_sum_into(acc[k], l, s, z, chip, "chip_sum")
    acc = [acc[k] for k in ("win", "wo", "wu", "wd")]
    from_sibling = _exchange(acc, [a.shape for a in acc], "pair", lambda r, a, i: r, lambda r, a, i: r,
                             lambda a: acc[a].shape, "grads_from_pair", False)

    wa_cols, ws_cols = conv_a_w.shape[2], ssm_conv_w.shape[2]
    par_g = small["par"].reshape(depth, 3, CH)
    g_small = dict(
        norm_mix_pre=small["g1"], conv_out_norm=small["ga"], ssm_out_norm=small["gs"], norm_mix_post=small["g2"],
        norm_mlp_pre=small["g3"], norm_mlp_post=small["g4"], ssm_conv_b=small["bs"],
        conv_a_w=lax.dynamic_slice_in_dim(small["wa"].reshape(depth, 3, D), chip * wa_cols, wa_cols, axis=2),
        ssm_conv_w=lax.dynamic_slice_in_dim(small["ws"].reshape(depth, 4, XBC), chip * ws_cols, ws_cols, axis=2),
        dt_bias=par_g[:, 0, :NH], a_log=par_g[:, 1, :NH], d_skip=par_g[:, 2, :NH])

    given = dict(norm_mix_pre=(norm_mix_pre, m_norm_mix_pre, v_norm_mix_pre), w_in=(w_in, m_w_in, v_w_in),
                 conv_a_w=(conv_a_w, m_conv_a_w, v_conv_a_w), ssm_conv_w=(ssm_conv_w, m_ssm_conv_w, v_ssm_conv_w),
                 ssm_conv_b=(ssm_conv_b, m_ssm_conv_b, v_ssm_conv_b), dt_bias=(dt_bias, m_dt_bias, v_dt_bias),
                 a_log=(a_log, m_a_log, v_a_log), d_skip=(d_skip, m_d_skip, v_d_skip),
                 conv_out_norm=(conv_out_norm, m_conv_out_norm, v_conv_out_norm),
                 ssm_out_norm=(ssm_out_norm, m_ssm_out_norm, v_ssm_out_norm), w_out=(w_out, m_w_out, v_w_out),
                 norm_mix_post=(norm_mix_post, m_norm_mix_post, v_norm_mix_post),
                 norm_mlp_pre=(norm_mlp_pre, m_norm_mlp_pre, v_norm_mlp_pre), w_up=(w_up, m_w_up, v_w_up),
                 w_down=(w_down, m_w_down, v_w_down), norm_mlp_post=(norm_mlp_post, m_norm_mlp_post, v_norm_mlp_post))
    halves = dict(zip(["w_in", "w_out", "w_up", "w_down"], zip(acc, from_sibling)))
    order = ["norm_mix_pre", "w_in", "conv_a_w", "ssm_conv_w", "ssm_conv_b", "dt_bias", "a_log", "d_skip",
             "conv_out_norm", "ssm_out_norm", "w_out", "norm_mix_post", "norm_mlp_pre", "w_up", "w_down",
             "norm_mlp_post"]
    g_out, d_out, m_out, v_out = [], [], [], []
    for n in order:
        wv, mv, vv = given[n]
        if n in halves and wv.shape[-1] % CH:
            own, recv = halves[n]
            gv = jnp.concatenate([jnp.where(core == 0, own, recv), jnp.where(core == 0, recv, own)], axis=1)
            to_cols, to_rows = (lambda a: jnp.transpose(a, (2, 0, 1))), (lambda a: jnp.transpose(a, (1, 2, 0)))
            dlt, m2, v2 = [to_rows(o) for o in adamw_leading(to_cols(wv), to_cols(gv), to_cols(mv), to_cols(vv),
                                                             "adamw_cols")]
        elif n in halves:
            gv, dlt, m2, v2 = adamw_halves(wv, *halves[n], mv, vv, core, "adamw_matrix")
        else:
            gv = g_small[n].reshape(wv.shape)
            two_d = lambda a: a.reshape(-1, a.shape[-1])
            dlt, m2, v2 = adamw(two_d(wv), two_d(gv), two_d(mv), two_d(vv), "adamw")
        g_out.append(gv)
        d_out.append(dlt.reshape(wv.shape))
        m_out.append(m2.reshape(wv.shape))
        v_out.append(v2.reshape(wv.shape))
    return (loss, dx.reshape(nb, seq, D), *g_out, *d_out, *m_out, *v_out)
```

```python
import functools

import jax
import jax.numpy as jnp
from jax import lax
from jax.experimental import pallas as pl
from jax.experimental.pallas import tpu as pltpu

f32, bf16 = jnp.float32, jnp.bfloat16

D = 1024
NH, HP = 16, 64
NG, NS = 2, 128
CH = 128
XBC = D + 2 * NG * NS
DFF = 4 * D
IN_COLS = 3 * D + D + XBC + NH
PROJ = 5760
COL_Z, COL_XBC, COL_DT = 3 * D, 4 * D, 4 * D + XBC
EPS = 1e-6
HALO = 8
VMEM_LIMIT = 56 * 2**20
MESH = pl.DeviceIdType.MESH

LR, B1, B2, AEPS, WD, STEP = 0.001, 0.9, 0.999, 1e-08, 0.01, 10


def _cparams(n_axes):
    return pltpu.CompilerParams(dimension_semantics=("arbitrary",) * n_axes, vmem_limit_bytes=VMEM_LIMIT)


def _sds(shape, dtype):
    return jax.ShapeDtypeStruct(tuple(shape), dtype)


def _token_spec(token):
    return [] if token is None else [pl.BlockSpec(memory_space=pl.ANY)]


def _token_arg(token):
    return [] if token is None else [token]


def _rms_fwd(x, g):
    r = lax.rsqrt(jnp.mean(x * x, axis=-1, keepdims=True) + EPS)
    return x * r * g


def _rms_bwd(x, g, dy):
    r = lax.rsqrt(jnp.mean(x * x, axis=-1, keepdims=True) + EPS)
    xh = x * r
    gdy = dy * g
    dx = r * (gdy - xh * jnp.mean(xh * gdy, axis=-1, keepdims=True))
    return dx, dy * xh


def _accum(ref, part, first):
    @pl.when(first)
    def _():
        ref[...] = part

    @pl.when(jnp.logical_not(first))
    def _():
        ref[...] += part


def _dot_nt(a, b):
    return lax.dot_general(a, b, (((1,), (1,)), ((), ())), preferred_element_type=f32)


def _dot_tn(a, b):
    return lax.dot_general(a, b, (((0,), (0,)), ((), ())), preferred_element_type=f32)


def _dot(a, b):
    return jnp.dot(a, b, preferred_element_type=f32)


def _split_dot(x, e_bf, n_split, nt=False):
    acc = None
    rem = x
    for s in range(n_split):
        hi = rem.astype(bf16)
        term = _dot_nt(hi, e_bf) if nt else _dot(hi, e_bf)
        acc = term if acc is None else acc + term
        if s + 1 < n_split:
            rem = rem - hi.astype(f32)
    return acc


def _sigmoid(x):
    return 0.5 * jnp.tanh(0.5 * x) + 0.5


def norm_matmul(x, g, w, tm, tn, out_dtype, name, token=None):
    t = x.shape[0]
    if w.ndim == 3:
        assert w.shape[2] == tn
        n = w.shape[0] * tn
        w_spec = pl.BlockSpec((None, D, tn), lambda i, j: (j, 0, 0))
    else:
        n = w.shape[1]
        w_spec = pl.BlockSpec((D, tn), lambda i, j: (0, j))

    def body(x_ref, g_ref, w_ref, *rest):
        o_ref, h_ref = rest[-2:]

        @pl.when(pl.program_id(1) == 0)
        def _():
            h_ref[...] = _rms_fwd(x_ref[...], g_ref[...]).astype(bf16)

        o_ref[...] = _dot(h_ref[...], w_ref[...]).astype(out_dtype)

    return pl.pallas_call(
        body, name=name, grid=(t // tm, n // tn),
        in_specs=[pl.BlockSpec((tm, D), lambda i, j: (i, 0)), pl.BlockSpec((1, D), lambda i, j: (0, 0)), w_spec]
        + _token_spec(token),
        out_specs=[pl.BlockSpec((tm, tn), lambda i, j: (i, j)), pl.BlockSpec((tm, D), lambda i, j: (i, 0))],
        out_shape=[_sds((t, n), out_dtype), _sds((t, D), bf16)],
        compiler_params=_cparams(2))(x, g, w, *_token_arg(token))


def matmul_postnorm(a, w, xres, g, tm, relu2, name):
    t, k = a.shape

    def body(a_ref, w_ref, xr_ref, g_ref, y_ref, xo_ref):
        av = a_ref[...]
        if relu2:
            af = jnp.maximum(av.astype(f32), 0.0)
            av = (af * af).astype(bf16)
        y = _dot(av, w_ref[...])
        y_ref[...] = y.astype(bf16)
        xo_ref[...] = xr_ref[...] + _rms_fwd(y, g_ref[...])

    return pl.pallas_call(
        body, name=name, grid=(t // tm,),
        in_specs=[pl.BlockSpec((tm, k), lambda i: (i, 0)), pl.BlockSpec((k, D), lambda i: (0, 0)),
                  pl.BlockSpec((tm, D), lambda i: (i, 0)), pl.BlockSpec((1, D), lambda i: (0, 0))],
        out_specs=[pl.BlockSpec((tm, D), lambda i: (i, 0)), pl.BlockSpec((tm, D), lambda i: (i, 0))],
        out_shape=[_sds((t, D), bf16), _sds((t, D), f32)],
        compiler_params=_cparams(1))(a, w, xres, g)


def postnorm_bwd_matmul(y, g, dxo, w, fp, tm, tn, out_dtype, name, token=None):
    t, n = y.shape[0], w.shape[0]
    relu = fp is not None

    def body(*refs):
        y_ref, g_ref, dxo_ref, w_ref = refs[:4]
        fp_ref = refs[4] if relu else None
        dy_ref, dg_ref, da_ref = refs[-3:]
        i, j = pl.program_id(0), pl.program_id(1)

        @pl.when(j == 0)
        def _():
            dx, dgc = _rms_bwd(y_ref[...].astype(f32), g_ref[...], dxo_ref[...])
            dy_ref[...] = dx.astype(bf16)
            _accum(dg_ref, jnp.sum(dgc, axis=0, keepdims=True), i == 0)

        da = _dot_nt(dy_ref[...], w_ref[...])
        if relu:
            da = da * (2.0 * jnp.maximum(fp_ref[...].astype(f32), 0.0))
        da_ref[...] = da.astype(out_dtype)

    in_specs = [pl.BlockSpec((tm, D), lambda i, j: (i, 0)), pl.BlockSpec((1, D), lambda i, j: (0, 0)),
                pl.BlockSpec((tm, D), lambda i, j: (i, 0)), pl.BlockSpec((tn, D), lambda i, j: (j, 0))]
    args = [y, g, dxo, w]
    if relu:
        in_specs.append(pl.BlockSpec((tm, tn), lambda i, j: (i, j)))
        args.append(fp)
    in_specs += _token_spec(token)
    args += _token_arg(token)
    return pl.pallas_call(
        body, name=name, grid=(t // tm, n // tn), in_specs=in_specs,
        out_specs=[pl.BlockSpec((tm, D), lambda i, j: (i, 0)), pl.BlockSpec((1, D), lambda i, j: (0, 0)),
                   pl.BlockSpec((tm, tn), lambda i, j: (i, j))],
        out_shape=[_sds((t, D), bf16), _sds((1, D), f32), _sds((t, n), out_dtype)],
        compiler_params=_cparams(2))(*args)


def matmul_prenorm_bwd(da, w, x, g, dxo, tm, name, token=None):
    t, k = da.shape
    blocked = w.ndim == 3

    def body(da_ref, w_ref, x_ref, g_ref, dxo_ref, *rest):
        dx_ref, dg_ref = rest[-2:]
        if blocked:
            kc = w.shape[2]
            dh = _dot_nt(da_ref[:, 0:kc], w_ref[0])
            for q in range(1, w.shape[0]):
                dh = dh + _dot_nt(da_ref[:, q * kc:(q + 1) * kc], w_ref[q])
        else:
            dh = _dot_nt(da_ref[...], w_ref[...])
        dxn, dgc = _rms_bwd(x_ref[...], g_ref[...], dh)
        dx_ref[...] = dxo_ref[...] + dxn
        _accum(dg_ref, jnp.sum(dgc, axis=0, keepdims=True), pl.program_id(0) == 0)

    w_spec = pl.BlockSpec(w.shape, (lambda i: (0, 0, 0)) if blocked else (lambda i: (0, 0)))
    return pl.pallas_call(
        body, name=name, grid=(t // tm,),
        in_specs=[pl.BlockSpec((tm, k), lambda i: (i, 0)), w_spec,
                  pl.BlockSpec((tm, D), lambda i: (i, 0)), pl.BlockSpec((1, D), lambda i: (0, 0)),
                  pl.BlockSpec((tm, D), lambda i: (i, 0))] + _token_spec(token),
        out_specs=[pl.BlockSpec((tm, D), lambda i: (i, 0)), pl.BlockSpec((1, D), lambda i: (0, 0))],
        out_shape=[_sds((t, D), f32), _sds((1, D), f32)],
        compiler_params=_cparams(1))(da, w, x, g, dxo, *_token_arg(token))


def matmul_tn(a, b, tm, tn, relu2, name, col_blocks=False):
    t, m = a.shape
    n = b.shape[1]
    if col_blocks:
        out_spec, out_shape = pl.BlockSpec((None, tm, tn), lambda i, j: (j, i, 0)), _sds((n // tn, m, tn), bf16)
    else:
        out_spec, out_shape = pl.BlockSpec((tm, tn), lambda i, j: (i, j)), _sds((m, n), bf16)

    def body(a_ref, b_ref, o_ref, at_ref):
        @pl.when(pl.program_id(1) == 0)
        def _():
            av = a_ref[...]
            if relu2:
                af = jnp.maximum(av.astype(f32), 0.0)
                av = (af * af).astype(bf16)
            at_ref[...] = av.T

        o_ref[...] = _dot(at_ref[...], b_ref[...]).astype(bf16)

    return pl.pallas_call(
        body, name=name, grid=(m // tm, n // tn),
        in_specs=[pl.BlockSpec((t, tm), lambda i, j: (0, i)), pl.BlockSpec((t, tn), lambda i, j: (0, j))],
        out_specs=out_spec, out_shape=out_shape,
        scratch_shapes=[pltpu.VMEM((tm, t), bf16)],
        compiler_params=_cparams(2))(a, b)


ROWS_A = 16
ROWS_B = 32
UNROLL = 4


def _past(win, s):
    return (win if s == 0 else pltpu.roll(win, s, 0))[HALO:]


def _future(win, s):
    n = win.shape[0]
    return (win if s == 0 else pltpu.roll(win, n - s, 0))[:n - HALO]


def _fold8(v):
    return v.reshape(v.shape[0] // 8, 8, v.shape[1]).sum(axis=0)


def _halo_prev(tb, col):
    return lambda i: (jnp.maximum(i * (tb // HALO) - 1, 0), col)


def _halo_next(tb, col, t):
    return lambda i: (jnp.minimum((i + 1) * (tb // HALO), t // HALO - 1), col)


def group_a_fwd(proj, wa, g, seq, tb, name):
    t = proj.shape[0]
    bps = seq // tb

    def body(xa_ref, ca_ref, ba_ref, xah_ref, cah_ref, wa_ref, g_ref, o_ref, u_scr):
        first = (pl.program_id(0) % bps) == 0
        u_scr[0:HALO, :] = jnp.where(first, 0.0, cah_ref[...] * xah_ref[...])
        w, gv = wa_ref[...], g_ref[...]

        def chunk(i, carry):
            r = pl.multiple_of(i * ROWS_A, ROWS_A)
            rows = pl.ds(r, ROWS_A)
            u_scr[pl.ds(pl.multiple_of(HALO + r, HALO), ROWS_A), :] = ca_ref[rows, :] * xa_ref[rows, :]
            win = u_scr[pl.ds(r, ROWS_A + HALO), :]
            cv = w[2:3] * _past(win, 0) + w[1:2] * _past(win, 1) + w[0:1] * _past(win, 2)
            o_ref[rows, :] = _rms_fwd(ba_ref[rows, :] * cv, gv).astype(bf16)
            return carry

        lax.fori_loop(0, tb // ROWS_A, chunk, 0, unroll=UNROLL)

    blk = lambda c: pl.BlockSpec((tb, D), lambda i: (i, c))
    return pl.pallas_call(
        body, name=name, grid=(t // tb,),
        in_specs=[blk(0), blk(1), blk(2),
                  pl.BlockSpec((HALO, D), _halo_prev(tb, 0)), pl.BlockSpec((HALO, D), _halo_prev(tb, 1)),
                  pl.BlockSpec((8, D), lambda i: (0, 0)), pl.BlockSpec((1, D), lambda i: (0, 0))],
        out_specs=pl.BlockSpec((tb, D), lambda i: (i, 0)),
        out_shape=_sds((t, 2 * D), bf16),
        scratch_shapes=[pltpu.VMEM((tb + HALO, D), f32)],
        compiler_params=_cparams(1))(proj, proj, proj, proj, proj, wa, g)


def group_a_bwd(proj, dcat, wa, g, seq, tb, name, token=None):
    t = proj.shape[0]
    bps = seq // tb

    def body(xa_ref, ca_ref, ba_ref, dy_ref, xap_ref, cap_ref, xan_ref, can_ref, ban_ref, dyn_ref, wa_ref, g_ref,
             *rest):
        dp_ref, dwa_ref, dg_ref, u_scr, d_scr, acc_scr = rest[-6:]
        i = pl.program_id(0)
        first = (i % bps) == 0
        last = (i % bps) == bps - 1
        w = wa_ref[...]
        gv = g_ref[...]
        u_scr[0:HALO, :] = jnp.where(first, 0.0, cap_ref[...] * xap_ref[...])
        u_scr[HALO + tb:2 * HALO + tb, :] = can_ref[...] * xan_ref[...]
        acc_scr[...] = jnp.zeros_like(acc_scr)

        def forward_part(n, carry):
            r = pl.multiple_of(n * ROWS_A, ROWS_A)
            rows = pl.ds(r, ROWS_A)
            ba = ba_ref[rows, :]
            u_scr[pl.ds(pl.multiple_of(HALO + r, HALO), ROWS_A), :] = ca_ref[rows, :] * xa_ref[rows, :]
            win = u_scr[pl.ds(r, ROWS_A + HALO), :]
            u = [_past(win, s) for s in range(3)]
            cv = w[2:3] * u[0] + w[1:2] * u[1] + w[0:1] * u[2]
            dya, dgc = _rms_bwd(ba * cv, gv, dy_ref[rows, :])
            dcv = dya * ba
            d_scr[rows, :] = dcv
            dp_ref[rows, 2 * D:3 * D] = (dya * cv).astype(bf16)
            acc_scr[0:8, :] += _fold8(dgc)
            for k in range(3):
                acc_scr[8 + 8 * k:16 + 8 * k, :] += _fold8(dcv * u[2 - k])
            return carry

        lax.fori_loop(0, tb // ROWS_A, forward_part, 0, unroll=UNROLL)

        start = HALO + tb
        cvn = (w[2:3] * u_scr[pl.ds(start, HALO), :] + w[1:2] * u_scr[pl.ds(start - 1, HALO), :]
               + w[0:1] * u_scr[pl.ds(start - 2, HALO), :])
        ban = ban_ref[...]
        dyan, _ = _rms_bwd(ban * cvn, gv, dyn_ref[...])
        d_scr[tb:tb + HALO, :] = jnp.where(last, 0.0, dyan * ban)

        def backward_part(n, carry):
            r = pl.multiple_of(n * ROWS_A, ROWS_A)
            rows = pl.ds(r, ROWS_A)
            win = d_scr[pl.ds(r, ROWS_A + HALO), :]
            du = w[2:3] * _future(win, 0) + w[1:2] * _future(win, 1) + w[0:1] * _future(win, 2)
            dp_ref[rows, 0:D] = (du * ca_ref[rows, :]).astype(bf16)
            dp_ref[rows, D:2 * D] = (du * xa_ref[rows, :]).astype(bf16)
            return carry

        lax.fori_loop(0, tb // ROWS_A, backward_part, 0, unroll=UNROLL)

        row = lax.broadcasted_iota(jnp.int32, (8, D), 0)
        dw = jnp.zeros((8, D), f32)
        for k in range(3):
            dw = jnp.where(row == k, jnp.sum(acc_scr[8 + 8 * k:16 + 8 * k, :], axis=0, keepdims=True), dw)
        _accum(dwa_ref, dw, i == 0)
        _accum(dg_ref, jnp.sum(acc_scr[0:8, :], axis=0, keepdims=True), i == 0)

    blk = lambda c: pl.BlockSpec((tb, D), lambda i: (i, c))
    prv = lambda c: pl.BlockSpec((HALO, D), _halo_prev(tb, c))
    nxt = lambda c: pl.BlockSpec((HALO, D), _halo_next(tb, c, t))
    return pl.pallas_call(
        body, name=name, grid=(t // tb,),
        in_specs=[blk(0), blk(1), blk(2), blk(0), prv(0), prv(1), nxt(0), nxt(1), nxt(2), nxt(0),
                  pl.BlockSpec((8, D), lambda i: (0, 0)), pl.BlockSpec((1, D), lambda i: (0, 0))] + _token_spec(token),
        out_specs=[pl.BlockSpec((tb, 3 * D), lambda i: (i, 0)), pl.BlockSpec((8, D), lambda i: (0, 0)),
                   pl.BlockSpec((1, D), lambda i: (0, 0))],
        out_shape=[_sds((t, PROJ), bf16), _sds((8, D), f32), _sds((1, D), f32)],
        scratch_shapes=[pltpu.VMEM((tb + 2 * HALO, D), f32), pltpu.VMEM((tb + HALO, D), f32), pltpu.VMEM((32, D), f32)],
        compiler_params=_cparams(1))(proj, proj, proj, dcat, proj, proj, proj, proj, proj, dcat, wa, g,
                                     *_token_arg(token))


CB = 512
XBC_BLK0 = COL_XBC // CB


def conv_b_fwd(proj, ws, bs, seq, tb, name):
    t = proj.shape[0]
    bps = seq // tb

    def body(x_ref, xp_ref, w_ref, b_ref, o_ref, x_scr):
        first = (pl.program_id(1) % bps) == 0
        x_scr[0:HALO, :] = jnp.where(first, 0.0, xp_ref[...])
        w, bias = w_ref[...], b_ref[...]

        def chunk(n, carry):
            r = pl.multiple_of(n * ROWS_B, ROWS_B)
            rows = pl.ds(r, ROWS_B)
            x_scr[pl.ds(pl.multiple_of(HALO + r, HALO), ROWS_B), :] = x_ref[rows, :]
            win = x_scr[pl.ds(r, ROWS_B + HALO), :]
            xc = bias + w[3:4] * _past(win, 0)
            for k in range(3):
                xc = xc + w[k:k + 1] * _past(win, 3 - k)
            o_ref[rows, :] = xc * _sigmoid(xc)
            return carry

        lax.fori_loop(0, tb // ROWS_B, chunk, 0, unroll=UNROLL)

    return pl.pallas_call(
        body, name=name, grid=(XBC // CB, t // tb),
        in_specs=[pl.BlockSpec((tb, CB), lambda j, i: (i, XBC_BLK0 + j)),
                  pl.BlockSpec((HALO, CB), lambda j, i: (jnp.maximum(i * (tb // HALO) - 1, 0), XBC_BLK0 + j)),
                  pl.BlockSpec((8, CB), lambda j, i: (0, j)), pl.BlockSpec((1, CB), lambda j, i: (0, j))],
        out_specs=pl.BlockSpec((tb, CB), lambda j, i: (i, j)),
        out_shape=_sds((t, XBC), f32),
        scratch_shapes=[pltpu.VMEM((tb + HALO, CB), f32)],
        compiler_params=_cparams(2))(proj, proj, ws, bs)


def conv_b_bwd(proj, dxs, ws, bs, dproj, seq, tb, name):
    t = proj.shape[0]
    bps = seq // tb

    def body(x_ref, xp_ref, xn_ref, d_ref, dn_ref, w_ref, b_ref, dproj_ref, dx_ref, dw_ref, db_ref, x_scr, d_scr,
             acc_scr):
        i = pl.program_id(1)
        first = (i % bps) == 0
        last = (i % bps) == bps - 1
        w = w_ref[...]
        bias = b_ref[...]
        x_scr[0:HALO, :] = jnp.where(first, 0.0, xp_ref[...])
        x_scr[HALO + tb:2 * HALO + tb, :] = xn_ref[...]
        acc_scr[...] = jnp.zeros_like(acc_scr)

        def dsilu(xc, d):
            sg = _sigmoid(xc)
            return d * (sg * (1.0 + xc * (1.0 - sg)))

        def forward_part(n, carry):
            r = pl.multiple_of(n * ROWS_B, ROWS_B)
            rows = pl.ds(r, ROWS_B)
            x_scr[pl.ds(pl.multiple_of(HALO + r, HALO), ROWS_B), :] = x_ref[rows, :]
            win = x_scr[pl.ds(r, ROWS_B + HALO), :]
            xs = [_past(win, s) for s in range(4)]
            xc = bias + w[3:4] * xs[0]
            for k in range(3):
                xc = xc + w[k:k + 1] * xs[3 - k]
            dxc = dsilu(xc, d_ref[rows, :])
            d_scr[rows, :] = dxc
            acc_scr[0:8, :] += _fold8(dxc)
            for k in range(4):
                acc_scr[8 + 8 * k:16 + 8 * k, :] += _fold8(dxc * xs[3 - k])
            return carry

        lax.fori_loop(0, tb // ROWS_B, forward_part, 0, unroll=UNROLL)

        start = HALO + tb
        xcn = bias + w[3:4] * x_scr[pl.ds(start, HALO), :]
        for k in range(3):
            xcn = xcn + w[k:k + 1] * x_scr[pl.ds(start - 3 + k, HALO), :]
        d_scr[tb:tb + HALO, :] = jnp.where(last, 0.0, dsilu(xcn, dn_ref[...]))

        def backward_part(n, carry):
            r = pl.multiple_of(n * ROWS_B, ROWS_B)
            win = d_scr[pl.ds(r, ROWS_B + HALO), :]
            dx = w[3:4] * _future(win, 0)
            for k in range(3):
                dx = dx + w[k:k + 1] * _future(win, 3 - k)
            dx_ref[pl.ds(r, ROWS_B), :] = dx.astype(bf16)
            return carry

        lax.fori_loop(0, tb // ROWS_B, backward_part, 0, unroll=UNROLL)

        row = lax.broadcasted_iota(jnp.int32, (8, CB), 0)
        dw = jnp.zeros((8, CB), f32)
        for k in range(4):
            dw = jnp.where(row == k, jnp.sum(acc_scr[8 + 8 * k:16 + 8 * k, :], axis=0, keepdims=True), dw)
        _accum(dw_ref, dw, i == 0)
        _accum(db_ref, jnp.sum(acc_scr[0:8, :], axis=0, keepdims=True), i == 0)

    nh = t // HALO
    return pl.pallas_call(
        body, name=name, grid=(XBC // CB, t // tb),
        in_specs=[pl.BlockSpec((tb, CB), lambda j, i: (i, XBC_BLK0 + j)),
                  pl.BlockSpec((HALO, CB), lambda j, i: (jnp.maximum(i * (tb // HALO) - 1, 0), XBC_BLK0 + j)),
                  pl.BlockSpec((HALO, CB), lambda j, i: (jnp.minimum((i + 1) * (tb // HALO), nh - 1), XBC_BLK0 + j)),
                  pl.BlockSpec((tb, CB), lambda j, i: (i, j)),
                  pl.BlockSpec((HALO, CB), lambda j, i: (jnp.minimum((i + 1) * (tb // HALO), nh - 1), j)),
                  pl.BlockSpec((8, CB), lambda j, i: (0, j)), pl.BlockSpec((1, CB), lambda j, i: (0, j)),
                  pl.BlockSpec(memory_space=pl.ANY)],
        out_specs=[pl.BlockSpec((tb, CB), lambda j, i: (i, XBC_BLK0 + j)), pl.BlockSpec((8, CB), lambda j, i: (0, j)),
                   pl.BlockSpec((1, CB), lambda j, i: (0, j))],
        out_shape=[_sds((t, PROJ), bf16), _sds((8, XBC), f32), _sds((1, XBC), f32)],
        input_output_aliases={7: 0},
        scratch_shapes=[pltpu.VMEM((tb + 2 * HALO, CB), f32), pltpu.VMEM((tb + HALO, CB), f32),
                        pltpu.VMEM((40, CB), f32)],
        compiler_params=_cparams(2))(proj, proj, proj, dxs, dxs, ws, bs, dproj)


def place_columns(buf, part, col_block, tb, name):
    t, wdt = part.shape

    def body(p_ref, buf_ref, o_ref):
        o_ref[...] = p_ref[...]

    return pl.pallas_call(
        body, name=name, grid=(t // tb,),
        in_specs=[pl.BlockSpec((tb, wdt), lambda i: (i, 0)), pl.BlockSpec(memory_space=pl.ANY)],
        out_specs=pl.BlockSpec((tb, wdt), lambda i: (i, col_block)), out_shape=_sds(buf.shape, buf.dtype),
        input_output_aliases={1: 0}, compiler_params=_cparams(1))(part, buf)


GW = D // NG
EXPAND_TERMS = 2
REDUCE_TERMS = 1


def _ssd_consts():
    head_of_lane = jnp.arange(D) // HP
    expand = (jnp.arange(CH)[:, None] == head_of_lane[None, :]).astype(bf16)
    tri = (jnp.arange(CH)[:, None] >= jnp.arange(CH)[None, :]).astype(f32)
    return expand, tri


def _ssd_common(par_ref, dtr_ref, e_ref, tri_ref):
    par = par_ref[...]
    dtb, alog, dsk = par[0:1], par[1:2], par[2:3]
    lane = lax.broadcasted_iota(jnp.int32, (CH, CH), 1)
    a = -jnp.exp(alog)
    dtr = dtr_ref[...] + dtb
    sp = jnp.maximum(dtr, 0.0) + jnp.log(1.0 + jnp.exp(-jnp.abs(dtr)))
    dt = jnp.where(lane < NH, sp, 0.0)
    cs = jnp.dot(tri_ref[...], dt * a, precision=lax.Precision.HIGHEST, preferred_element_type=f32)
    cs_last = cs[CH - 1:CH, :]
    dte = jnp.exp(cs_last - cs)
    ecs = jnp.exp(cs)
    ecl = jnp.exp(cs_last)
    e = e_ref[...]
    row8 = lax.broadcasted_iota(jnp.int32, (8, CH), 0)
    r8 = _split_dot(jnp.where(row8 == 0, ecl, jnp.where(row8 == 1, dsk, 0.0)), e, 3)
    return dict(a=a, dtr=dtr, dt=dt, cs=cs, cst=cs.T, dte=dte, ecs=ecs, ecl=ecl, e=e, lane=lane,
                dt_x=_split_dot(dt, e, EXPAND_TERMS), dte_x=_split_dot(dte, e, EXPAND_TERMS),
                ecs_x=_split_dot(ecs, e, EXPAND_TERMS),
                ecl_x=r8[0:1], dsk_x=r8[1:2])


def _decay_matrix(c, h):
    li = lax.broadcasted_iota(jnp.int32, (CH, CH), 0)
    seg = c["cs"][:, h:h + 1] - c["cst"][h:h + 1, :]
    return jnp.exp(jnp.where(li >= c["lane"], seg, -jnp.inf))


def _gate_norm_fwd(y, z, gs):
    zg = z * _sigmoid(z)
    yg = y * zg
    return jnp.concatenate([_rms_fwd(yg[:, k * GW:(k + 1) * GW], gs[:, k * GW:(k + 1) * GW]) for k in range(NG)], axis=1)


def ssd_fwd(xbcs, proj, par, gs, cat, seq, name):
    t = xbcs.shape[0]
    nc = seq // CH
    expand, tri = _ssd_consts()

    def body(xs_ref, b_ref, c_ref, dtr_ref, z_ref, par_ref, e_ref, tri_ref, gs_ref, cat_ref, yn_ref, y_ref, st_ref,
             p_scr, yd_scr):
        @pl.when(pl.program_id(0) % nc == 0)
        def _():
            p_scr[...] = jnp.zeros_like(p_scr)

        c = _ssd_common(par_ref, dtr_ref, e_ref, tri_ref)
        xs = xs_ref[...]
        xdt = xs * c["dt_x"]
        xdt_b = xdt.astype(bf16)
        xdte_b = (xdt * c["dte_x"]).astype(bf16)
        p = p_scr[...]
        st_ref[0] = p
        p_b = p.astype(bf16)
        lo = c["lane"] < HP
        for g in range(NG):
            bg = b_ref[:, g * NS:(g + 1) * NS].astype(bf16)
            cg = c_ref[:, g * NS:(g + 1) * NS].astype(bf16)
            gmat = _dot_nt(cg, bg)
            for q in range(GW // CH):
                col = g * GW + q * CH
                xp = xdt_b[:, col:col + CH]
                h0 = col // HP
                m0 = (gmat * _decay_matrix(c, h0)).astype(bf16)
                m1 = (gmat * _decay_matrix(c, h0 + 1)).astype(bf16)
                stacked = jnp.concatenate([jnp.where(lo, xp, jnp.zeros_like(xp)),
                                           jnp.where(lo, jnp.zeros_like(xp), xp)], axis=0)
                yd_scr[:, col:col + CH] = _dot(jnp.concatenate([m0, m1], axis=1), stacked)
            gsl = slice(g * GW, (g + 1) * GW)
            yoff = _dot(cg, p_b[:, gsl]) * c["ecs_x"][:, gsl]
            yd_scr[:, gsl] = yd_scr[:, gsl] + yoff
            p_scr[:, gsl] = p[:, gsl] * c["ecl_x"][:, gsl] + _dot_tn(bg, xdte_b[:, gsl])
        y = yd_scr[...] + c["dsk_x"] * xs
        y_ref[...] = y
        yn_ref[...] = _gate_norm_fwd(y, z_ref[...], gs_ref[...]).astype(bf16)

    nb = t // CH
    return pl.pallas_call(
        body, name=name, grid=(nb,),
        in_specs=[pl.BlockSpec((CH, D), lambda i: (i, 0)),
                  pl.BlockSpec((CH, NG * NS), lambda i: (i, D // (NG * NS))),
                  pl.BlockSpec((CH, NG * NS), lambda i: (i, D // (NG * NS) + 1)),
                  pl.BlockSpec((CH, CH), lambda i: (i, COL_DT // CH)),
                  pl.BlockSpec((CH, D), lambda i: (i, COL_Z // D)),
                  pl.BlockSpec((8, CH), lambda i: (0, 0)), pl.BlockSpec((CH, D), lambda i: (0, 0)),
                  pl.BlockSpec((CH, CH), lambda i: (0, 0)), pl.BlockSpec((1, D), lambda i: (0, 0)),
                  pl.BlockSpec(memory_space=pl.ANY)],
        out_specs=[pl.BlockSpec((CH, D), lambda i: (i, 1)), pl.BlockSpec((CH, D), lambda i: (i, 0)),
                   pl.BlockSpec((1, NS, D), lambda i: (i, 0, 0))],
        out_shape=[_sds((t, 2 * D), bf16), _sds((t, D), f32), _sds((nb, NS, D), f32)],
        input_output_aliases={9: 0},
        scratch_shapes=[pltpu.VMEM((NS, D), f32), pltpu.VMEM((CH, D), f32)],
        compiler_params=_cparams(1))(xbcs, xbcs, xbcs, proj, proj, par, expand, tri, gs, cat)


def ssd_bwd(xbcs, proj, ypre, states, dcat, par, gs, dproj, seq, name):
    t = xbcs.shape[0]
    nc = seq // CH
    expand, tri = _ssd_consts()

    def body(xs_ref, b_ref, c_ref, dtr_ref, z_ref, y_ref, st_ref, dyn_ref, par_ref, e_ref, tri_ref, gs_ref, dproj_ref,
             dx_ref, dz_ref, ddt_ref, dpar_ref, dgs_ref, dp_scr, dxdt_scr):
        i = pl.program_id(0)

        @pl.when(i % nc == 0)
        def _():
            dp_scr[...] = jnp.zeros_like(dp_scr)

        c = _ssd_common(par_ref, dtr_ref, e_ref, tri_ref)
        e = c["e"]
        lane = c["lane"]
        sub = lax.broadcasted_iota(jnp.int32, (CH, CH), 0)
        xs = xs_ref[...]
        xdt = xs * c["dt_x"]
        xdt_b = xdt.astype(bf16)
        xdte_b = (xdt * c["dte_x"]).astype(bf16)
        p = st_ref[0]
        p_b = p.astype(bf16)
        dpn = dp_scr[...]
        dpn_b = dpn.astype(bf16)

        y, z, gs_v = y_ref[...], z_ref[...], gs_ref[...]
        zs = _sigmoid(z)
        zg = z * zs
        yg = y * zg
        parts, gparts = [], []
        for k in range(NG):
            sl = slice(k * GW, (k + 1) * GW)
            dxk, dgk = _rms_bwd(yg[:, sl], gs_v[:, sl], dyn_ref[:, sl])
            parts.append(dxk)
            gparts.append(dgk)
        dyg = jnp.concatenate(parts, axis=1)
        dgs_rows = jnp.concatenate(gparts, axis=1)
        dy = dyg * zg
        dz_ref[...] = (dyg * y * (zs * (1.0 + z * (1.0 - zs)))).astype(bf16)
        dy_b = dy.astype(bf16)
        dq_b = (dy * c["ecs_x"]).astype(bf16)

        lo = lane < HP
        dcs = jnp.zeros((CH, CH), f32)
        dcst = jnp.zeros((CH, CH), f32)
        for g in range(NG):
            gsl = slice(g * GW, (g + 1) * GW)
            bg = b_ref[:, g * NS:(g + 1) * NS].astype(bf16)
            cg = c_ref[:, g * NS:(g + 1) * NS].astype(bf16)
            gmat = _dot_nt(cg, bg)
            dgm = jnp.zeros((CH, CH), f32)
            for q in range(GW // CH):
                col = g * GW + q * CH
                xp = xdt_b[:, col:col + CH]
                dyp = dy_b[:, col:col + CH]
                zero = jnp.zeros_like(dyp)
                xp2 = jnp.concatenate([jnp.where(lo, xp, zero), jnp.where(lo, zero, xp)], axis=0)
                dy2 = jnp.concatenate([jnp.where(lo, dyp, zero), jnp.where(lo, zero, dyp)], axis=0)
                dm2 = _dot_nt(dyp, xp2)
                ms = []
                for hh in range(2):
                    h = col // HP + hh
                    dec = _decay_matrix(c, h)
                    m = gmat * dec
                    dm = dm2[:, hh * CH:(hh + 1) * CH]
                    dseg = dm * m
                    dcs = dcs + jnp.where(lane == h, jnp.sum(dseg, axis=1, keepdims=True), 0.0)
                    dcst = dcst + jnp.where(sub == h, jnp.sum(dseg, axis=0, keepdims=True), 0.0)
                    dgm = dgm + dm * dec
                    ms.append(m.astype(bf16))
                dxdt_scr[:, col:col + CH] = _dot_tn(jnp.concatenate(ms, axis=0), dy2)
            dgm_b = dgm.astype(bf16)
            bds = _dot(bg, dpn_b[:, gsl])
            dxdt_scr[:, gsl] = dxdt_scr[:, gsl] + c["dte_x"][:, gsl] * bds
            dc_g = _dot(dgm_b, bg) + _dot_nt(dq_b[:, gsl], p_b[:, gsl])
            db_g = _dot_tn(dgm_b, cg) + _dot_nt(xdte_b[:, gsl], dpn_b[:, gsl])
            dx_ref[:, D + g * NS:D + (g + 1) * NS] = db_g
            dx_ref[:, D + NG * NS + g * NS:D + NG * NS + (g + 1) * NS] = dc_g
            dp_scr[:, gsl] = dpn[:, gsl] * c["ecl_x"][:, gsl] + _dot_tn(cg, dq_b[:, gsl])
            q_g = _dot(cg, p_b[:, gsl])
            e_g = e[:, gsl]
            dcs = dcs + c["ecs"] * _split_dot(dy[:, gsl] * q_g, e_g, REDUCE_TERMS, nt=True)
            ddte = _split_dot(xdt[:, gsl] * bds, e_g, REDUCE_TERMS, nt=True) * c["dte"]
            dcs = dcs - ddte
            dcs = dcs + jnp.where(sub == CH - 1, jnp.sum(ddte, axis=0, keepdims=True), 0.0)

        decl = _split_dot(jnp.broadcast_to(jnp.sum(dpn * p, axis=0, keepdims=True), (8, D)), e, 2, nt=True)[0:1]
        dcs = dcs + jnp.where(sub == CH - 1, c["ecl"] * decl, 0.0)
        dcs = dcs - dcst.T
        dadt = lax.dot_general(tri_ref[...], dcs, (((0,), (0,)), ((), ())), precision=lax.Precision.HIGHEST,
                               preferred_element_type=f32)
        dxdt = dxdt_scr[...]
        ddt = dadt * c["a"] + _split_dot(dxdt * xs, e, REDUCE_TERMS, nt=True)
        ddtr = jnp.where(lane < NH, ddt * _sigmoid(c["dtr"]), 0.0)
        ddt_ref[...] = ddtr.astype(bf16)
        dx_ref[:, 0:D] = dxdt * c["dt_x"] + c["dsk_x"] * dy
        dsk = _split_dot(jnp.broadcast_to(jnp.sum(dy * xs, axis=0, keepdims=True), (8, D)), e, 2, nt=True)[0:1]
        dalog = jnp.sum(dadt * c["dt"], axis=0, keepdims=True) * c["a"]
        row8 = lax.broadcasted_iota(jnp.int32, (8, CH), 0)
        dpar = jnp.where(row8 == 0, jnp.sum(ddtr, axis=0, keepdims=True),
                         jnp.where(row8 == 1, dalog, jnp.where(row8 == 2, dsk, 0.0)))
        dpar = jnp.where(lax.broadcasted_iota(jnp.int32, (8, CH), 1) < NH, dpar, 0.0)
        _accum(dpar_ref, dpar, i == 0)
        _accum(dgs_ref, jnp.sum(dgs_rows, axis=0, keepdims=True), i == 0)

    nb = t // CH
    rev = lambda i: (i // nc) * nc + (nc - 1 - i % nc)
    return pl.pallas_call(
        body, name=name, grid=(nb,),
        in_specs=[pl.BlockSpec((CH, D), lambda i: (rev(i), 0)),
                  pl.BlockSpec((CH, NG * NS), lambda i: (rev(i), D // (NG * NS))),
                  pl.BlockSpec((CH, NG * NS), lambda i: (rev(i), D // (NG * NS) + 1)),
                  pl.BlockSpec((CH, CH), lambda i: (rev(i), COL_DT // CH)),
                  pl.BlockSpec((CH, D), lambda i: (rev(i), COL_Z // D)),
                  pl.BlockSpec((CH, D), lambda i: (rev(i), 0)),
                  pl.BlockSpec((1, NS, D), lambda i: (rev(i), 0, 0)),
                  pl.BlockSpec((CH, D), lambda i: (rev(i), 1)),
                  pl.BlockSpec((8, CH), lambda i: (0, 0)), pl.BlockSpec((CH, D), lambda i: (0, 0)),
                  pl.BlockSpec((CH, CH), lambda i: (0, 0)), pl.BlockSpec((1, D), lambda i: (0, 0)),
                  pl.BlockSpec(memory_space=pl.ANY)],
        out_specs=[pl.BlockSpec((CH, XBC), lambda i: (rev(i), 0)), pl.BlockSpec((CH, D), lambda i: (rev(i), COL_Z // D)),
                   pl.BlockSpec((CH, CH), lambda i: (rev(i), 0)),
                   pl.BlockSpec((8, CH), lambda i: (0, 0)), pl.BlockSpec((1, D), lambda i: (0, 0))],
        out_shape=[_sds((t, XBC), f32), _sds((t, PROJ), bf16), _sds((t, CH), bf16), _sds((8, CH), f32), _sds((1, D), f32)],
        input_output_aliases={12: 1},
        scratch_shapes=[pltpu.VMEM((NS, D), f32), pltpu.VMEM((CH, D), f32)],
        compiler_params=_cparams(1))(xbcs, xbcs, xbcs, proj, proj, ypre, states, dcat, par, expand, tri, gs, dproj)


def loss_head(y, target, tb, name):
    t = y.shape[0]

    def body(y_ref, t_ref, s_ref, dy_ref):
        err = y_ref[...] - t_ref[...]
        dy_ref[...] = err * (1.0 / D)
        _accum(s_ref, jnp.zeros((8, CH), f32) + jnp.sum(err * err), pl.program_id(0) == 0)

    return pl.pallas_call(
        body, name=name, grid=(t // tb,),
        in_specs=[pl.BlockSpec((tb, D), lambda i: (i, 0)), pl.BlockSpec((tb, D), lambda i: (i, 0))],
        out_specs=[pl.BlockSpec((8, CH), lambda i: (0, 0)), pl.BlockSpec((tb, D), lambda i: (i, 0))],
        out_shape=[_sds((8, CH), f32), _sds((t, D), f32)],
        compiler_params=_cparams(1))(y, target)


def _tiles(t, seq):
    tm = min(512, t)
    return dict(tm=tm, tm_small=min(256, t), tm_large=min(1024, t), tm_huge=min(2048, t), tb=min(512, seq))


def local_step(x, target, depth, weights_of, seq, grads_done=None):
    t = x.shape[0]
    ts = _tiles(t, seq)
    tm, tl, th, tb = ts["tm"], ts["tm_large"], ts["tm_huge"], ts["tb"]
    saved, ws = [], []
    for l in range(depth):
        w = weights_of(l, x)
        ws.append(w)
        proj, h1 = norm_matmul(x, w["g1"], w["win"], tl, 1152, f32, "in_proj", token=w.get("token"))
        cat = group_a_fwd(proj, w["wa"], w["ga"], seq, tb, "group_a_fwd")
        xbcs = conv_b_fwd(proj, w["ws"], w["bs"], seq, tb, "conv_b_fwd")
        cat, ypre, states = ssd_fwd(xbcs, proj, w["par"], w["gs"], cat, seq, "ssd_fwd")
        if "late" in w:
            w.update(w.pop("late")(cat))
        mix, x2 = matmul_postnorm(cat, w["wo"], x, w["g2"], tl, False, "out_proj")
        fp, h2 = norm_matmul(x2, w["g3"], w["wu"], th, 1024, bf16, "mlp_up")
        o, x3 = matmul_postnorm(fp, w["wd"], x2, w["g4"], tm, True, "mlp_down")
        saved.append(dict(x=x, proj=proj, h1=h1, xbcs=xbcs, ypre=ypre, states=states, cat=cat, mix=mix, x2=x2,
                          fp=fp, h2=h2, o=o))
        x = x3
    sse, dx = loss_head(x, target, tm, "loss_head")
    grads = [None] * depth
    for l in reversed(range(depth)):
        s, w = saved[l], ws[l]
        do, dg4, dfp = postnorm_bwd_matmul(s["o"], w["g4"], dx, w["wd"], s["fp"], tl, 1024, bf16, "mlp_down_bwd")
        dwd = matmul_tn(s["fp"], do, 512, 1024, True, "mlp_down_dw")
        dx2, dg3 = matmul_prenorm_bwd(dfp, w["wu"], s["x2"], w["g3"], dx, tm, "mlp_up_bwd")
        dwu = matmul_tn(s["h2"], dfp, 512, 1024, False, "mlp_up_dw", col_blocks=True)
        dmix, dg2, dcat = postnorm_bwd_matmul(s["mix"], w["g2"], dx2, w["wo"], None, tl, 1024, f32, "out_proj_bwd")
        dwo = matmul_tn(s["cat"], dmix, 512, 1024, False, "out_proj_dw")
        token = None if grads_done is None else grads_done(l, dict(wo=dwo, wu=dwu, wd=dwd), False)
        dproj, dwa, dga = group_a_bwd(s["proj"], dcat, w["wa"], w["ga"], seq, tb, "group_a_bwd", token=token)
        dxbcs, dproj, ddt, dpar, dgs = ssd_bwd(s["xbcs"], s["proj"], s["ypre"], s["states"], dcat, w["par"], w["gs"],
                                               dproj, seq, "ssd_bwd")
        dproj, dws, dbs = conv_b_bwd(s["proj"], dxbcs, w["ws"], w["bs"], dproj, seq, tb, "conv_b_bwd")
        dproj = place_columns(dproj, ddt, COL_DT // CH, tm, "place_ddt")
        dwin = matmul_tn(s["h1"], dproj, 512, 1152, False, "in_proj_dw")
        token = None if grads_done is None else grads_done(l, dict(win=dwin), True)
        dx, dg1 = matmul_prenorm_bwd(dproj, w["win"], s["x"], w["g1"], dx2, ts["tm_small"], "in_proj_bwd", token=token)
        grads[l] = dict(win=dwin, wo=dwo, wu=dwu, wd=dwd, wa=dwa, ws=dws, bs=dbs, par=dpar,
                        g1=dg1, ga=dga, gs=dgs, g2=dg2, g3=dg3, g4=dg4)
    return sse, dx, grads


GROUPS = {
    "chips": [(1, 0, 0), (0, 1, 0), (1, 1, 0)],
    "pair": [(0, 0, 1)],
    "all": [(1, 0, 0), (0, 1, 0), (1, 1, 0), (0, 0, 1), (1, 0, 1), (0, 1, 1), (1, 1, 1)],
}


def _group_index(group, x, y, c):
    return {"chips": 2 * x + y, "pair": c, "all": 4 * x + 2 * y + c}[group]


def _chunk_indices(shape, pieces):
    if len(shape) < 3:
        return [()]
    lead = [()]
    for n in shape[:-2]:
        lead = [i + (k,) for i in lead for k in range(n)]
    rows = shape[-2]
    split = max(1, pieces // len(lead))
    while split > 1 and (rows % split or (rows // split) % 16):
        split -= 1
    step = rows // split
    return [i + (pl.ds(s * step, step),) for i in lead for s in range(split)]


def _exchange(arrays, out_shapes, group, src_view, dst_view, view_shape, name, own, pieces=16):
    masks = GROUPS[group]
    na, nm = len(arrays), len(masks)
    cuts = [_chunk_indices(view_shape(a), pieces) for a in range(na)]

    def body(*refs):
        ins, outs = refs[:na], refs[na:2 * na]
        send_sems, recv_sems = refs[2 * na:2 * na + 2]
        local_sems = refs[2 * na + 2] if own else None
        x, y, c = lax.axis_index("x"), lax.axis_index("y"), lax.axis_index("c")
        me = _group_index(group, x, y, c)
        peers = []
        for mx, my, mc in masks:
            px, py, pc = (1 - x if mx else x), (1 - y if my else y), (1 - c if mc else c)
            peers.append(((px, py, pc), _group_index(group, px, py, pc)))

        def part(ref, idx):
            return ref.at[idx] if idx else ref

        if own:
            for a in range(na):
                for idx in cuts[a]:
                    pltpu.make_async_copy(part(src_view(ins[a], a, me), idx), part(dst_view(outs[a], a, me), idx),
                                          local_sems.at[a]).start()
        for a in range(na):
            for j, (dev, pidx) in enumerate(peers):
                for idx in cuts[a]:
                    pltpu.make_async_remote_copy(
                        src_ref=part(src_view(ins[a], a, pidx), idx), dst_ref=part(dst_view(outs[a], a, me), idx),
                        send_sem=send_sems.at[a * nm + j], recv_sem=recv_sems.at[a * nm + j],
                        device_id=dev, device_id_type=MESH).start()
        whole = []
        for a in range(na):
            for j, (dev, pidx) in enumerate(peers):
                whole.append(pltpu.make_async_remote_copy(
                    src_ref=src_view(ins[a], a, pidx), dst_ref=dst_view(outs[a], a, pidx),
                    send_sem=send_sems.at[a * nm + j], recv_sem=recv_sems.at[a * nm + j],
                    device_id=dev, device_id_type=MESH))
        for cp in whole:
            cp.wait_recv()
        for cp in whole:
            cp.wait_send()
        if own:
            for a in range(na):
                pltpu.make_async_copy(src_view(ins[a], a, me), dst_view(outs[a], a, me), local_sems.at[a]).wait()

    hbm = pl.BlockSpec(memory_space=pltpu.HBM)
    sems = [pltpu.SemaphoreType.DMA((na * nm,)), pltpu.SemaphoreType.DMA((na * nm,))]
    return pl.pallas_call(
        body, name=name, in_specs=[hbm] * na, out_specs=[hbm] * na,
        out_shape=[_sds(s, a.dtype) for s, a in zip(out_shapes, arrays)],
        scratch_shapes=sems + ([pltpu.SemaphoreType.DMA((na,))] if own else []))(*arrays)


def all_gather(arrays, group, name, slot_axis=0, own=True):
    n = len(GROUPS[group]) + 1
    shapes = [a.shape[:slot_axis] + (n,) + a.shape[slot_axis:] for a in arrays]
    lead = (slice(None),) * slot_axis
    return _exchange(arrays, shapes, group, lambda r, a, i: r, lambda r, a, i: r.at[lead + (i,)],
                     lambda a: arrays[a].shape, name, own)


HBM_SPEC = pl.BlockSpec(memory_space=pltpu.HBM)
SEM_SPEC = pl.BlockSpec(memory_space=pltpu.SEMAPHORE)
DATAFLOW = pltpu.SideEffectType.DATAFLOW_SIDE_EFFECTING
N_CHIPS = 4


def _chip_peers(x, y, c):
    out = []
    for mx, my, _ in GROUPS["chips"]:
        px, py = (1 - x if mx else x), (1 - y if my else y)
        out.append(((px, py, c), 2 * px + py))
    return out


def _weight_views(shards):
    half = [s.shape[0] // 2 for s in shards]
    return dict(src=lambda ref, a, c, to_chip: ref.at[pl.ds(c * half[a], half[a])],
                dst=lambda ref, a, c, from_chip: ref.at[from_chip, pl.ds(c * half[a], half[a])],
                rows=lambda a: half[a])


def _grad_views(sums):
    return dict(src=lambda ref, a, c, to_chip: ref.at[to_chip], dst=lambda ref, a, c, from_chip: ref.at[from_chip],
                rows=lambda a: sums[a].shape[1])


def chips_start(sources, zones, views, name, pieces=4, after=None):
    na, nm = len(sources), N_CHIPS - 1

    def body(*refs):
        ins, lands = refs[:na], refs[na:2 * na]
        n_in = 2 * na + len(_token_arg(after))
        send_sems, recv_sems, token = refs[n_in], refs[n_in + 1], refs[-1]
        x, y, c = lax.axis_index("x"), lax.axis_index("y"), lax.axis_index("c")
        chip = 2 * x + y
        for a in range(na):
            step = views["rows"](a) // pieces
            for j, (dev, to_chip) in enumerate(_chip_peers(x, y, c)):
                for q in range(pieces):
                    rows = pl.ds(q * step, step)
                    pltpu.make_async_remote_copy(
                        src_ref=views["src"](ins[a], a, c, to_chip).at[rows],
                        dst_ref=views["dst"](lands[a], a, c, chip).at[rows],
                        send_sem=send_sems.at[a * nm + j], recv_sem=recv_sems.at[a * nm + j],
                        device_id=dev, device_id_type=MESH).start()
        token[...] = jnp.zeros_like(token)

    both = list(sources) + list(zones)
    outs = pl.pallas_call(
        body, name=name,
        out_shape=(pltpu.SemaphoreType.DMA((na * nm,)), pltpu.SemaphoreType.DMA((na * nm,)),
                   *[pltpu.HBM(b.shape, b.dtype) for b in both], _sds((8, CH), f32)),
        in_specs=[HBM_SPEC] * (2 * na) + _token_spec(after),
        out_specs=(SEM_SPEC, SEM_SPEC, *[HBM_SPEC] * (2 * na), pl.BlockSpec(memory_space=pltpu.VMEM)),
        input_output_aliases={i: 2 + i for i in range(2 * na)},
        compiler_params=pltpu.CompilerParams(has_side_effects=DATAFLOW))(
            *[pltpu.with_memory_space_constraint(b, pltpu.HBM) for b in both], *_token_arg(after))
    return dict(send=outs[0], recv=outs[1], sources=list(outs[2:2 + na]), zones=list(outs[2 + na:2 + 2 * na]),
                token=outs[-1], views=views)


def chips_wait(started, after, name):
    sources, zones, views = started["sources"], started["zones"], started["views"]
    na, nm = len(sources), N_CHIPS - 1

    def body(*refs):
        ins, lands = refs[:na], refs[na:2 * na]
        send_sems, recv_sems = refs[2 * na], refs[2 * na + 1]
        x, y, c = lax.axis_index("x"), lax.axis_index("y"), lax.axis_index("c")
        for a in range(na):
            for j, (dev, peer_chip) in enumerate(_chip_peers(x, y, c)):
                cp = pltpu.make_async_remote_copy(
                    src_ref=views["src"](ins[a], a, c, peer_chip), dst_ref=views["dst"](lands[a], a, c, peer_chip),
                    send_sem=send_sems.at[a * nm + j], recv_sem=recv_sems.at[a * nm + j],
                    device_id=dev, device_id_type=MESH)
                cp.wait_send()
                cp.wait_recv()

    both = list(sources) + list(zones)
    outs = pl.pallas_call(
        body, name=name, out_shape=tuple(pltpu.HBM(b.shape, b.dtype) for b in both),
        in_specs=[HBM_SPEC] * (2 * na) + [SEM_SPEC, SEM_SPEC, pl.BlockSpec(memory_space=pl.ANY)],
        out_specs=tuple([HBM_SPEC] * (2 * na)), input_output_aliases={i: i for i in range(2 * na)},
        compiler_params=pltpu.CompilerParams(has_side_effects=DATAFLOW))(*both, started["send"], started["recv"], after)
    return list(outs[:na]), list(outs[na:])


def weights_share(zones, name):
    na, nm = len(zones), N_CHIPS - 1

    def body(*refs):
        lands = refs[na:2 * na]
        send_sems, recv_sems = refs[2 * na:]
        x, y, c = lax.axis_index("x"), lax.axis_index("y"), lax.axis_index("c")
        chip = 2 * x + y
        sibling = (x, y, 1 - c)
        sends = []
        for a in range(na):
            half = zones[a].shape[1] // 2
            for m in range(1, N_CHIPS):
                mine = lands[a].at[chip ^ m, pl.ds(c * half, half)]
                sends.append(pltpu.make_async_remote_copy(
                    src_ref=mine, dst_ref=mine, send_sem=send_sems.at[a * nm + m - 1],
                    recv_sem=recv_sems.at[a * nm + m - 1], device_id=sibling, device_id_type=MESH))
        for cp in sends:
            cp.start()
        for a in range(na):
            half = zones[a].shape[1] // 2
            for m in range(1, N_CHIPS):
                theirs = lands[a].at[chip ^ m, pl.ds((1 - c) * half, half)]
                pltpu.make_async_remote_copy(
                    src_ref=theirs, dst_ref=theirs, send_sem=send_sems.at[a * nm + m - 1],
                    recv_sem=recv_sems.at[a * nm + m - 1], device_id=sibling, device_id_type=MESH).wait_recv()
        for cp in sends:
            cp.wait_send()

    return pl.pallas_call(
        body, name=name, in_specs=[HBM_SPEC] * na, out_specs=[HBM_SPEC] * na,
        out_shape=[_sds(z.shape, z.dtype) for z in zones], input_output_aliases={i: i for i in range(na)},
        scratch_shapes=[pltpu.SemaphoreType.DMA((na * nm,)), pltpu.SemaphoreType.DMA((na * nm,))])(*zones)


def pair_send_halves(grads, name):
    half = [g.shape[1] // 2 for g in grads]
    shapes = [(g.shape[0], h, g.shape[2]) for g, h in zip(grads, half)]
    return _exchange(grads, shapes, "pair", lambda r, a, i: r.at[:, pl.ds(i * half[a], half[a])],
                     lambda r, a, i: r, lambda a: shapes[a], name, False)


def sum_pair_half(g, recv, core, name, tb=256, by_chip=None):
    nk, r, c = g.shape
    tb = min(tb, r // 2)
    nb = r // 2 // tb

    def body(core_ref, g_ref, r_ref, o_ref):
        s = g_ref[...].astype(f32) + r_ref[...].astype(f32)
        if by_chip is None:
            o_ref[...] = s.astype(bf16)
        else:
            for k in range(by_chip[0]):
                o_ref[k] = s[:, k * by_chip[1]:(k + 1) * by_chip[1]].astype(bf16)

    if by_chip is None:
        out_spec = pl.BlockSpec((None, tb, c), lambda k, i, core_ref: (k, i, 0))
        out_shape = _sds((nk, r // 2, c), bf16)
    else:
        assert nk == 1
        out_spec = pl.BlockSpec((by_chip[0], tb, by_chip[1]), lambda k, i, core_ref: (0, i, 0))
        out_shape = _sds((by_chip[0], r // 2, by_chip[1]), bf16)
    return pl.pallas_call(
        body, name=name,
        grid_spec=pltpu.PrefetchScalarGridSpec(
            num_scalar_prefetch=1, grid=(nk, nb),
            in_specs=[pl.BlockSpec((None, tb, c), lambda k, i, core_ref: (k, core_ref[0] * nb + i, 0)),
                      pl.BlockSpec((None, tb, c), lambda k, i, core_ref: (k, i, 0))],
            out_specs=out_spec),
        out_shape=out_shape, compiler_params=_cparams(2))(jnp.reshape(core, (1,)).astype(jnp.int32), g, recv)


def assemble_columns(blocks, width, name, tb=256):
    n, r, c = blocks.shape

    def body(b_ref, o_ref):
        for k in range(n):
            o_ref[:, k * c:(k + 1) * c] = b_ref[k]
        o_ref[:, n * c:] = jnp.zeros((tb, width - n * c), blocks.dtype)

    return pl.pallas_call(
        body, name=name, grid=(r // tb,), in_specs=[pl.BlockSpec((n, tb, c), lambda i: (0, i, 0))],
        out_specs=pl.BlockSpec((tb, width), lambda i: (i, 0)), out_shape=_sds((r, width), blocks.dtype),
        compiler_params=_cparams(1))(blocks)


def chip_sum_into(acc, layer, own, others, chip, name, tb=256):
    n, r, c = own.shape
    tb = min(tb, r)

    def body(chip_ref, x_ref, y1_ref, y2_ref, y3_ref, acc_ref, o_ref):
        o_ref[...] = ((x_ref[...].astype(f32) + y1_ref[...].astype(f32)) + y2_ref[...].astype(f32)) + y3_ref[...].astype(f32)

    def slot(k):
        return pl.BlockSpec((None, tb, c), lambda i, chip_ref: (chip_ref[0] ^ k, i, 0))

    return pl.pallas_call(
        body, name=name,
        grid_spec=pltpu.PrefetchScalarGridSpec(
            num_scalar_prefetch=1, grid=(r // tb,),
            in_specs=[slot(k) for k in range(n)] + [pl.BlockSpec(memory_space=pl.ANY)],
            out_specs=pl.BlockSpec((None, tb, c), lambda i, chip_ref: (layer, i, 0))),
        out_shape=_sds(acc.shape, f32), input_output_aliases={n + 1: 0}, compiler_params=_cparams(1))(
            jnp.reshape(chip, (1,)).astype(jnp.int32), own, *([others] * (n - 1)), acc)


def adamw_halves(w, g_own, g_recv, m, v, core, name, tb=256):
    depth, r, c = w.shape
    tb = min(tb, r // 2)
    nb = r // 2 // tb

    def body(core_ref, w_ref, go_ref, gr_ref, m_ref, v_ref, g_ref, d_ref, mo_ref, vo_ref):
        gv = jnp.where(pl.program_id(1) == core_ref[0], go_ref[...], gr_ref[...])
        m2 = B1 * m_ref[...] + (1.0 - B1) * gv
        v2 = B2 * v_ref[...] + (1.0 - B2) * (gv * gv)
        m_hat = m2 / (1.0 - B1 ** STEP)
        v_hat = v2 / (1.0 - B2 ** STEP)
        g_ref[...] = gv
        d_ref[...] = -LR * (m_hat / (jnp.sqrt(v_hat) + AEPS) + WD * w_ref[...])
        mo_ref[...] = m2
        vo_ref[...] = v2

    whole = pl.BlockSpec((None, tb, c), lambda l, h, i, core_ref: (l, h * nb + i, 0))
    part = pl.BlockSpec((None, tb, c), lambda l, h, i, core_ref: (l, i, 0))
    return pl.pallas_call(
        body, name=name,
        grid_spec=pltpu.PrefetchScalarGridSpec(num_scalar_prefetch=1, grid=(depth, 2, nb),
                                               in_specs=[whole, part, part, whole, whole], out_specs=[whole] * 4),
        out_shape=[_sds(w.shape, f32)] * 4, compiler_params=_cparams(3))(
            jnp.reshape(core, (1,)).astype(jnp.int32), w, g_own, g_recv, m, v)


def sum_slots(y, out_dtype, name, tb=256):
    n, r, c = y.shape
    tb = min(tb, r)

    def body(y_ref, o_ref):
        acc = y_ref[0].astype(f32)
        for i in range(1, n):
            acc = acc + y_ref[i].astype(f32)
        o_ref[...] = acc.astype(out_dtype)

    return pl.pallas_call(
        body, name=name, grid=(r // tb,),
        in_specs=[pl.BlockSpec((n, tb, c), lambda i: (0, i, 0))], out_specs=pl.BlockSpec((tb, c), lambda i: (i, 0)),
        out_shape=_sds((r, c), out_dtype), compiler_params=_cparams(1))(y)


def adamw(w, g, m, v, name, tb=256):
    r, c = w.shape
    tb = min(tb, r)

    def body(w_ref, g_ref, m_ref, v_ref, d_ref, mo_ref, vo_ref):
        gv = g_ref[...]
        m2 = B1 * m_ref[...] + (1.0 - B1) * gv
        v2 = B2 * v_ref[...] + (1.0 - B2) * (gv * gv)
        m_hat = m2 / (1.0 - B1 ** STEP)
        v_hat = v2 / (1.0 - B2 ** STEP)
        d_ref[...] = -LR * (m_hat / (jnp.sqrt(v_hat) + AEPS) + WD * w_ref[...])
        mo_ref[...] = m2
        vo_ref[...] = v2

    spec = pl.BlockSpec((tb, c), lambda i: (i, 0))
    return pl.pallas_call(
        body, name=name, grid=(r // tb,), in_specs=[spec] * 4, out_specs=[spec] * 3,
        out_shape=[_sds((r, c), f32)] * 3, compiler_params=_cparams(1))(w, g, m, v)


def adamw_leading(w, g, m, v, name, tc=64):
    c, l, r = w.shape
    main = c // tc
    tail = c - main * tc

    def body(w_ref, g_ref, m_ref, v_ref, *rest):
        d_ref, mo_ref, vo_ref = rest[-3:]
        gv = g_ref[...]
        m2 = B1 * m_ref[...] + (1.0 - B1) * gv
        v2 = B2 * v_ref[...] + (1.0 - B2) * (gv * gv)
        m_hat = m2 / (1.0 - B1 ** STEP)
        v_hat = v2 / (1.0 - B2 ** STEP)
        d_ref[...] = -LR * (m_hat / (jnp.sqrt(v_hat) + AEPS) + WD * w_ref[...])
        mo_ref[...] = m2
        vo_ref[...] = v2

    spec = pl.BlockSpec((tc, l, r), lambda i: (i, 0, 0))
    outs = pl.pallas_call(
        functools.partial(body), name=name, grid=(main,), in_specs=[spec] * 4, out_specs=[spec] * 3,
        out_shape=[_sds(w.shape, f32)] * 3, compiler_params=_cparams(1))(w, g, m, v)
    if tail:
        assert (main * tc) % tail == 0
        last = pl.BlockSpec((tail, l, r), lambda i: (main * tc // tail, 0, 0))
        outs = pl.pallas_call(
            functools.partial(body), name=name + "_tail", grid=(1,),
            in_specs=[last] * 4 + [pl.BlockSpec(memory_space=pl.ANY)] * 3, out_specs=[last] * 3,
            out_shape=[_sds(w.shape, f32)] * 3, input_output_aliases={4: 0, 5: 1, 6: 2},
            compiler_params=_cparams(1))(w, g, m, v, *outs)
    return outs


SMALL_ROW = 1024
SMALL_GAINS = ("g1", "ga", "gs", "g2", "g3", "g4")
SMALL_LAYER_ROWS = 8 + 8 + 16 + 8


def _pack_small(grads):
    wide = lambda a: jnp.pad(a, ((0, 0), (0, 2 * SMALL_ROW - a.shape[1]))).reshape(-1, SMALL_ROW)
    row = lax.broadcasted_iota(jnp.int32, (8, SMALL_ROW), 0)
    parts = []
    for g in grads:
        singles = [g[k] for k in SMALL_GAINS] + [g["bs"][:, :SMALL_ROW],
                                                 jnp.pad(g["bs"][:, SMALL_ROW:], ((0, 0), (0, 2 * SMALL_ROW - XBC)))]
        first = sum(jnp.where(row == k, s, 0.0) for k, s in enumerate(singles))
        parts += [first, g["wa"], wide(g["ws"]), jnp.pad(g["par"], ((0, 0), (0, SMALL_ROW - CH)))]
    return jnp.concatenate(parts, axis=0)


def _unpack_small(packed, depth):
    rows = packed.reshape(depth, SMALL_LAYER_ROWS, SMALL_ROW)
    out = {k: rows[:, i] for i, k in enumerate(SMALL_GAINS)}
    out["bs"] = rows[:, 6:8].reshape(depth, 2 * SMALL_ROW)[:, :XBC]
    out["wa"] = rows[:, 8:11]
    out["ws"] = rows[:, 16:32].reshape(depth, 8, 2 * SMALL_ROW)[:, :4, :XBC]
    out["par"] = rows[:, 32:35, :CH]
    return out


def kernel(x, norm_mix_pre, w_in, conv_a_w, ssm_conv_w, ssm_conv_b, dt_bias, a_log, d_skip, conv_out_norm, ssm_out_norm, w_out, norm_mix_post, norm_mlp_pre, w_up, w_down, norm_mlp_post, loss_target, m_norm_mix_pre, m_w_in, m_conv_a_w, m_ssm_conv_w, m_ssm_conv_b, m_dt_bias, m_a_log, m_d_skip, m_conv_out_norm, m_ssm_out_norm, m_w_out, m_norm_mix_post, m_norm_mlp_pre, m_w_up, m_w_down, m_norm_mlp_post, v_norm_mix_pre, v_w_in, v_conv_a_w, v_ssm_conv_w, v_ssm_conv_b, v_dt_bias, v_a_log, v_d_skip, v_conv_out_norm, v_ssm_out_norm, v_w_out, v_norm_mix_post, v_norm_mlp_pre, v_w_up, v_w_down, v_norm_mlp_post):
    nb, seq, _ = x.shape
    t = nb * seq
    depth = w_in.shape[0]
    ncol = w_in.shape[2]
    chip = 2 * lax.axis_index("x") + lax.axis_index("y")

    taps = [conv_a_w, ssm_conv_w]
    taps_g = all_gather(taps, "chips", "gather_taps", slot_axis=1, own=False)
    wa_g, ws_g = [lax.dynamic_update_index_in_dim(g, s, chip, 1) for g, s in zip(taps_g, taps)]
    wa_full = jnp.transpose(wa_g, (0, 2, 1, 3)).reshape(depth, 3, D)
    ws_full = jnp.transpose(ws_g, (0, 2, 1, 3)).reshape(depth, 4, XBC)
    lane_pad = lambda a: jnp.pad(a, ((0, 0), (0, CH - a.shape[1])))
    par = jnp.stack([lane_pad(dt_bias), lane_pad(a_log), lane_pad(d_skip)], axis=1)
    par = jnp.pad(par, ((0, 0), (0, 5), (0, 0)))

    layer_shards = lambda l: [w_in[l].astype(bf16), w_out[l].astype(bf16), w_up[l].astype(bf16), w_down[l].astype(bf16)]
    issued = []

    def start(shards, name):
        zones = [lax.empty((N_CHIPS,) + s.shape, s.dtype) for s in shards]
        issued.append(chips_start(shards, zones, _weight_views(shards), name,
                                  after=issued[-1]["token"] if issued else taps_g[0]))
        return issued[-1]

    def finish(started, after, name):
        shards, zones = chips_wait(started, after, name)
        zones = weights_share(zones, "weights_share")
        return [lax.dynamic_update_index_in_dim(z, s, chip, 0) for z, s in zip(zones, shards)]

    def shaped(mats):
        wo_z, wu_z, wd_z = mats
        return wo_z.reshape(2 * D, D), wu_z, wd_z.reshape(DFF, D)

    first = layer_shards(0)
    travelling = {0: start(first[:1], "weights_start_0")}
    rest = start(first[1:], "weights_start_0_rest")
    for l in range(1, depth):
        travelling[l] = start(layer_shards(l), f"weights_start_{l}")

    def weights_of(l, x_in):
        mats = finish(travelling.pop(l), x_in, f"weights_wait_{l}")
        w = dict(win=assemble_columns(mats[0], PROJ, "assemble_w_in"), wa=jnp.pad(wa_full[l], ((0, 5), (0, 0))),
                 ws=jnp.pad(ws_full[l], ((0, 4), (0, 0))), bs=ssm_conv_b[l][None], par=par[l],
                 g1=norm_mix_pre[l][None], ga=conv_out_norm[l][None], gs=ssm_out_norm[l][None],
                 g2=norm_mix_post[l][None], g3=norm_mlp_pre[l][None], g4=norm_mlp_post[l][None])
        if l == 0:
            w["token"] = issued[-1]["token"]
            w["late"] = lambda after: dict(zip(("wo", "wu", "wd"), shaped(finish(rest, after, "weights_wait_0_rest"))))
        else:
            w.update(zip(("wo", "wu", "wd"), shaped(mats[1:])))
        return w

    core = lax.axis_index("c")
    grads_travelling = {}

    chip_major = dict(win=lambda a: a[None], wo=lambda a: a.reshape(N_CHIPS, 2 * D // N_CHIPS, D), wu=lambda a: a,
                      wd=lambda a: a.reshape(N_CHIPS, DFF // N_CHIPS, D))
    held = {}

    def grads_done(l, g, last):
        if l > 0 and not last:
            held[l] = g
            return None
        g = {**held.pop(l, {}), **g}
        keys = [k for k in ("win", "wo", "wu", "wd") if k in g]
        mats = [chip_major[k](g[k]) for k in keys]
        received = pair_send_halves(mats, "grads_to_pair")
        sums = [sum_pair_half(m_, r_, core, "pair_sum", by_chip=(N_CHIPS, ncol) if k == "win" else None)
                for k, m_, r_ in zip(keys, mats, received)]
        zones = [lax.empty(s.shape, s.dtype) for s in sums]
        started = chips_start(sums, zones, _grad_views(sums), f"grads_start_{l}_{len(grads_travelling)}")
        grads_travelling[(l, keys[0])] = (keys, started)
        return started["token"]

    sse, dx, grads = local_step(x.reshape(t, D), loss_target.reshape(t, D), depth, weights_of, seq, grads_done)
    loss = lax.psum(0.5 / D * sse[0, 0], ("x", "y", "c"))

    small_all = all_gather([_pack_small(grads)], "all", "gather_small")[0]
    small_sum = sum_slots(small_all, f32, "small_sum", tb=8)
    small = _unpack_small(small_sum, depth)

    big_w = dict(win=w_in, wo=w_out, wu=w_up, wd=w_down)
    acc = {k: lax.empty((depth, bw.shape[1] // 2, bw.shape[2]), f32) for k, bw in big_w.items()}
    for n, ((l, _), (keys, started)) in enumerate(grads_travelling.items()):
        sums, zones = chips_wait(started, small_sum, f"grads_wait_{l}_{n}")
        for k, s, z in zip(keys, sums, zones):
            acc[k] = chip_sum_into(acc[k], l, s, z, chip, "chip_sum")
    acc = [acc[k] for k in ("win", "wo", "wu", "wd")]
    from_sibling = _exchange(acc, [a.shape for a in acc], "pair", lambda r, a, i: r, lambda r, a, i: r,
                             lambda a: acc[a].shape, "grads_from_pair", False)

    wa_cols, ws_cols = conv_a_w.shape[2], ssm_conv_w.shape[2]
    par_g = small["par"].reshape(depth, 3, CH)
    g_small = dict(
        norm_mix_pre=small["g1"], conv_out_norm=small["ga"], ssm_out_norm=small["gs"], norm_mix_post=small["g2"],
        norm_mlp_pre=small["g3"], norm_mlp_post=small["g4"], ssm_conv_b=small["bs"],
        conv_a_w=lax.dynamic_slice_in_dim(small["wa"].reshape(depth, 3, D), chip * wa_cols, wa_cols, axis=2),
        ssm_conv_w=lax.dynamic_slice_in_dim(small["ws"].reshape(depth, 4, XBC), chip * ws_cols, ws_cols, axis=2),
        dt_bias=par_g[:, 0, :NH], a_log=par_g[:, 1, :NH], d_skip=par_g[:, 2, :NH])

    given = dict(norm_mix_pre=(norm_mix_pre, m_norm_mix_pre, v_norm_mix_pre), w_in=(w_in, m_w_in, v_w_in),
                 conv_a_w=(conv_a_w, m_conv_a_w, v_conv_a_w), ssm_conv_w=(ssm_conv_w, m_ssm_conv_w, v_ssm_conv_w),
                 ssm_conv_b=(ssm_conv_b, m_ssm_conv_b, v_ssm_conv_b), dt_bias=(dt_bias, m_dt_bias, v_dt_bias),
                 a_log=(a_log, m_a_log, v_a_log), d_skip=(d_skip, m_d_skip, v_d_skip),
                 conv_out_norm=(conv_out_norm, m_conv_out_norm, v_conv_out_norm),
                 ssm_out_norm=(ssm_out_norm, m_ssm_out_norm, v_ssm_out_norm), w_out=(w_out, m_w_out, v_w_out),
                 norm_mix_post=(norm_mix_post, m_norm_mix_post, v_norm_mix_post),
                 norm_mlp_pre=(norm_mlp_pre, m_norm_mlp_pre, v_norm_mlp_pre), w_up=(w_up, m_w_up, v_w_up),
                 w_down=(w_down, m_w_down, v_w_down), norm_mlp_post=(norm_mlp_post, m_norm_mlp_post, v_norm_mlp_post))
    halves = dict(zip(["w_in", "w_out", "w_up", "w_down"], zip(acc, from_sibling)))
    order = ["norm_mix_pre", "w_in", "conv_a_w", "ssm_conv_w", "ssm_conv_b", "dt_bias", "a_log", "d_skip",
             "conv_out_norm", "ssm_out_norm", "w_out", "norm_mix_post", "norm_mlp_pre", "w_up", "w_down",
             "norm_mlp_post"]
    g_out, d_out, m_out, v_out = [], [], [], []
    for n in order:
        wv, mv, vv = given[n]
        if n in halves and wv.shape[-1] % CH:
            own, recv = halves[n]
            gv = jnp.concatenate([jnp.where(core == 0, own, recv), jnp.where(core == 0, recv, own)], axis=1)
            to_cols, to_rows = (lambda a: jnp.transpose(a, (2, 0, 1))), (lambda a: jnp.transpose(a, (1, 2, 0)))
            dlt, m2, v2 = [to_rows(o) for o in adamw_leading(to_cols(wv), to_cols(gv), to_cols(mv), to_cols(vv),
                                                             "adamw_cols")]
        elif n in halves:
            gv, dlt, m2, v2 = adamw_halves(wv, *halves[n], mv, vv, core, "adamw_matrix")
        else:
            gv = g_small[n].reshape(wv.shape)
            two_d = lambda a: a.reshape(-1, a.shape[-1])
            dlt, m2, v2 = adamw(two_d(wv), two_d(gv), two_d(mv), two_d(vv), "adamw")
        g_out.append(gv)
        d_out.append(dlt.reshape(wv.shape))
        m_out.append(m2.reshape(wv.shape))
        v_out.append(v2.reshape(wv.shape))
    return (loss, dx.reshape(nb, seq, D), *g_out, *d_out, *m_out, *v_out)
```

```python
import functools

import jax
import jax.numpy as jnp
from jax import lax
from jax.experimental import pallas as pl
from jax.experimental.pallas import tpu as pltpu

f32, bf16 = jnp.float32, jnp.bfloat16

D = 1024
NH, HP = 16, 64
NG, NS = 2, 128
CH = 128
XBC = D + 2 * NG * NS
DFF = 4 * D
IN_COLS = 3 * D + D + XBC + NH
PROJ = 5760
COL_Z, COL_XBC, COL_DT = 3 * D, 4 * D, 4 * D + XBC
EPS = 1e-6
HALO = 8
HBLK = 16
VMEM_LIMIT = 56 * 2**20
MESH = pl.DeviceIdType.MESH

LR, B1, B2, AEPS, WD, STEP = 0.001, 0.9, 0.999, 1e-08, 0.01, 10


def _cparams(n_axes):
    return pltpu.CompilerParams(dimension_semantics=("arbitrary",) * n_axes, vmem_limit_bytes=VMEM_LIMIT)


def _sds(shape, dtype):
    return jax.ShapeDtypeStruct(tuple(shape), dtype)


def _token_spec(token):
    return [] if token is None else [pl.BlockSpec(memory_space=pl.ANY)]


def _token_arg(token):
    return [] if token is None else [token]


def _rms_fwd(x, g):
    r = lax.rsqrt(jnp.mean(x * x, axis=-1, keepdims=True) + EPS)
    return x * r * g


def _rms_bwd(x, g, dy):
    r = lax.rsqrt(jnp.mean(x * x, axis=-1, keepdims=True) + EPS)
    xh = x * r
    gdy = dy * g
    dx = r * (gdy - xh * jnp.mean(xh * gdy, axis=-1, keepdims=True))
    return dx, dy * xh


def _accum(ref, part, first):
    @pl.when(first)
    def _():
        ref[...] = part

    @pl.when(jnp.logical_not(first))
    def _():
        ref[...] += part


def _dot_nt(a, b):
    return lax.dot_general(a, b, (((1,), (1,)), ((), ())), preferred_element_type=f32)


def _dot_tn(a, b):
    return lax.dot_general(a, b, (((0,), (0,)), ((), ())), preferred_element_type=f32)


def _dot(a, b):
    return jnp.dot(a, b, preferred_element_type=f32)


def _split_dot(x, e_bf, n_split, nt=False):
    acc = None
    rem = x
    for s in range(n_split):
        hi = rem.astype(bf16)
        term = _dot_nt(hi, e_bf) if nt else _dot(hi, e_bf)
        acc = term if acc is None else acc + term
        if s + 1 < n_split:
            rem = rem - hi.astype(f32)
    return acc


def _sigmoid(x):
    return 0.5 * jnp.tanh(0.5 * x) + 0.5


def norm_matmul(x, g, w, tm, tn, out_dtype, name, token=None):
    t = x.shape[0]
    if w.ndim == 3:
        assert w.shape[2] == tn
        n = w.shape[0] * tn
        w_spec = pl.BlockSpec((None, D, tn), lambda i, j: (j, 0, 0))
    else:
        n = w.shape[1]
        w_spec = pl.BlockSpec((D, tn), lambda i, j: (0, j))

    def body(x_ref, g_ref, w_ref, *rest):
        o_ref, h_ref = rest[-2:]

        @pl.when(pl.program_id(1) == 0)
        def _():
            h_ref[...] = _rms_fwd(x_ref[...], g_ref[...]).astype(bf16)

        o_ref[...] = _dot(h_ref[...], w_ref[...]).astype(out_dtype)

    return pl.pallas_call(
        body, name=name, grid=(t // tm, n // tn),
        in_specs=[pl.BlockSpec((tm, D), lambda i, j: (i, 0)), pl.BlockSpec((1, D), lambda i, j: (0, 0)), w_spec]
        + _token_spec(token),
        out_specs=[pl.BlockSpec((tm, tn), lambda i, j: (i, j)), pl.BlockSpec((tm, D), lambda i, j: (i, 0))],
        out_shape=[_sds((t, n), out_dtype), _sds((t, D), bf16)],
        compiler_params=_cparams(2))(x, g, w, *_token_arg(token))


def matmul_postnorm(a, w, xres, g, tm, relu2, name):
    t, k = a.shape

    def body(a_ref, w_ref, xr_ref, g_ref, y_ref, xo_ref):
        av = a_ref[...]
        if relu2:
            af = jnp.maximum(av.astype(f32), 0.0)
            av = (af * af).astype(bf16)
        y = _dot(av, w_ref[...])
        y_ref[...] = y.astype(bf16)
        xo_ref[...] = xr_ref[...] + _rms_fwd(y, g_ref[...])

    return pl.pallas_call(
        body, name=name, grid=(t // tm,),
        in_specs=[pl.BlockSpec((tm, k), lambda i: (i, 0)), pl.BlockSpec((k, D), lambda i: (0, 0)),
                  pl.BlockSpec((tm, D), lambda i: (i, 0)), pl.BlockSpec((1, D), lambda i: (0, 0))],
        out_specs=[pl.BlockSpec((tm, D), lambda i: (i, 0)), pl.BlockSpec((tm, D), lambda i: (i, 0))],
        out_shape=[_sds((t, D), bf16), _sds((t, D), f32)],
        compiler_params=_cparams(1))(a, w, xres, g)


def postnorm_bwd_matmul(y, g, dxo, w, fp, tm, tn, out_dtype, name, token=None):
    t, n = y.shape[0], w.shape[0]
    relu = fp is not None

    def body(*refs):
        y_ref, g_ref, dxo_ref, w_ref = refs[:4]
        fp_ref = refs[4] if relu else None
        dy_ref, dg_ref, da_ref = refs[-3:]
        i, j = pl.program_id(0), pl.program_id(1)

        @pl.when(j == 0)
        def _():
            dx, dgc = _rms_bwd(y_ref[...].astype(f32), g_ref[...], dxo_ref[...])
            dy_ref[...] = dx.astype(bf16)
            _accum(dg_ref, jnp.sum(dgc, axis=0, keepdims=True), i == 0)

        da = _dot_nt(dy_ref[...], w_ref[...])
        if relu:
            da = da * (2.0 * jnp.maximum(fp_ref[...].astype(f32), 0.0))
        da_ref[...] = da.astype(out_dtype)

    in_specs = [pl.BlockSpec((tm, D), lambda i, j: (i, 0)), pl.BlockSpec((1, D), lambda i, j: (0, 0)),
                pl.BlockSpec((tm, D), lambda i, j: (i, 0)), pl.BlockSpec((tn, D), lambda i, j: (j, 0))]
    args = [y, g, dxo, w]
    if relu:
        in_specs.append(pl.BlockSpec((tm, tn), lambda i, j: (i, j)))
        args.append(fp)
    in_specs += _token_spec(token)
    args += _token_arg(token)
    return pl.pallas_call(
        body, name=name, grid=(t // tm, n // tn), in_specs=in_specs,
        out_specs=[pl.BlockSpec((tm, D), lambda i, j: (i, 0)), pl.BlockSpec((1, D), lambda i, j: (0, 0)),
                   pl.BlockSpec((tm, tn), lambda i, j: (i, j))],
        out_shape=[_sds((t, D), bf16), _sds((1, D), f32), _sds((t, n), out_dtype)],
        compiler_params=_cparams(2))(*args)


def matmul_prenorm_bwd(da, w, x, g, dxo, tm, name, token=None):
    t, k = da.shape
    blocked = w.ndim == 3

    def body(da_ref, w_ref, x_ref, g_ref, dxo_ref, *rest):
        dx_ref, dg_ref = rest[-2:]
        if blocked:
            kc = w.shape[2]
            dh = _dot_nt(da_ref[:, 0:kc], w_ref[0])
            for q in range(1, w.shape[0]):
                dh = dh + _dot_nt(da_ref[:, q * kc:(q + 1) * kc], w_ref[q])
        else:
            dh = _dot_nt(da_ref[...], w_ref[...])
        dxn, dgc = _rms_bwd(x_ref[...], g_ref[...], dh)
        dx_ref[...] = dxo_ref[...] + dxn
        _accum(dg_ref, jnp.sum(dgc, axis=0, keepdims=True), pl.program_id(0) == 0)

    w_spec = pl.BlockSpec(w.shape, (lambda i: (0, 0, 0)) if blocked else (lambda i: (0, 0)))
    return pl.pallas_call(
        body, name=name, grid=(t // tm,),
        in_specs=[pl.BlockSpec((tm, k), lambda i: (i, 0)), w_spec,
                  pl.BlockSpec((tm, D), lambda i: (i, 0)), pl.BlockSpec((1, D), lambda i: (0, 0)),
                  pl.BlockSpec((tm, D), lambda i: (i, 0))] + _token_spec(token),
        out_specs=[pl.BlockSpec((tm, D), lambda i: (i, 0)), pl.BlockSpec((1, D), lambda i: (0, 0))],
        out_shape=[_sds((t, D), f32), _sds((1, D), f32)],
        compiler_params=_cparams(1))(da, w, x, g, dxo, *_token_arg(token))


def matmul_tn(a, b, tm, tn, relu2, name, col_blocks=False):
    t, m = a.shape
    n = b.shape[1]
    if col_blocks:
        out_spec, out_shape = pl.BlockSpec((None, tm, tn), lambda i, j: (j, i, 0)), _sds((n // tn, m, tn), bf16)
    else:
        out_spec, out_shape = pl.BlockSpec((tm, tn), lambda i, j: (i, j)), _sds((m, n), bf16)

    def body(a_ref, b_ref, o_ref, at_ref):
        @pl.when(pl.program_id(1) == 0)
        def _():
            av = a_ref[...]
            if relu2:
                af = jnp.maximum(av.astype(f32), 0.0)
                av = (af * af).astype(bf16)
            at_ref[...] = av.T

        o_ref[...] = _dot(at_ref[...], b_ref[...]).astype(bf16)

    return pl.pallas_call(
        body, name=name, grid=(m // tm, n // tn),
        in_specs=[pl.BlockSpec((t, tm), lambda i, j: (0, i)), pl.BlockSpec((t, tn), lambda i, j: (0, j))],
        out_specs=out_spec, out_shape=out_shape,
        scratch_shapes=[pltpu.VMEM((tm, t), bf16)],
        compiler_params=_cparams(2))(a, b)


ROWS_A = 16
ROWS_B = 32
UNROLL = 4


def _past(win, s):
    return (win if s == 0 else pltpu.roll(win, s, 0))[HALO:]


def _future(win, s):
    n = win.shape[0]
    return (win if s == 0 else pltpu.roll(win, n - s, 0))[:n - HALO]


def _fold8(v):
    return v.reshape(v.shape[0] // 8, 8, v.shape[1]).sum(axis=0)


def _last8(ref):
    return ref[...].astype(f32)[HBLK - HALO:]


def _first8(ref):
    return ref[...].astype(f32)[:HALO]


def _rd(ref, rows):
    return ref[rows, :].astype(f32)


def _halo_prev(tb, col):
    return lambda i: (jnp.maximum(i * (tb // HBLK) - 1, 0), col)


def _halo_next(tb, col, t):
    return lambda i: (jnp.minimum((i + 1) * (tb // HBLK), t // HBLK - 1), col)


def group_a_fwd(proj, wa, g, seq, tb, name):
    t = proj.shape[0]
    bps = seq // tb

    def body(xa_ref, ca_ref, ba_ref, xah_ref, cah_ref, wa_ref, g_ref, o_ref, u_scr):
        first = (pl.program_id(0) % bps) == 0
        u_scr[0:HALO, :] = jnp.where(first, 0.0, _last8(cah_ref) * _last8(xah_ref))
        w, gv = wa_ref[...], g_ref[...]

        def chunk(i, carry):
            r = pl.multiple_of(i * ROWS_A, ROWS_A)
            rows = pl.ds(r, ROWS_A)
            u_scr[pl.ds(pl.multiple_of(HALO + r, HALO), ROWS_A), :] = _rd(ca_ref, rows) * _rd(xa_ref, rows)
            win = u_scr[pl.ds(r, ROWS_A + HALO), :]
            cv = w[2:3] * _past(win, 0) + w[1:2] * _past(win, 1) + w[0:1] * _past(win, 2)
            o_ref[rows, :] = _rms_fwd(_rd(ba_ref, rows) * cv, gv).astype(bf16)
            return carry

        lax.fori_loop(0, tb // ROWS_A, chunk, 0, unroll=UNROLL)

    blk = lambda c: pl.BlockSpec((tb, D), lambda i: (i, c))
    return pl.pallas_call(
        body, name=name, grid=(t // tb,),
        in_specs=[blk(0), blk(1), blk(2),
                  pl.BlockSpec((HBLK, D), _halo_prev(tb, 0)), pl.BlockSpec((HBLK, D), _halo_prev(tb, 1)),
                  pl.BlockSpec((8, D), lambda i: (0, 0)), pl.BlockSpec((1, D), lambda i: (0, 0))],
        out_specs=pl.BlockSpec((tb, D), lambda i: (i, 0)),
        out_shape=_sds((t, 2 * D), bf16),
        scratch_shapes=[pltpu.VMEM((tb + HALO, D), f32)],
        compiler_params=_cparams(1))(proj, proj, proj, proj, proj, wa, g)


def group_a_bwd(proj, dcat, wa, g, seq, tb, name, token=None):
    t = proj.shape[0]
    bps = seq // tb

    def body(xa_ref, ca_ref, ba_ref, dy_ref, xap_ref, cap_ref, xan_ref, can_ref, ban_ref, dyn_ref, wa_ref, g_ref,
             *rest):
        dp_ref, dwa_ref, dg_ref, u_scr, d_scr, acc_scr = rest[-6:]
        i = pl.program_id(0)
        first = (i % bps) == 0
        last = (i % bps) == bps - 1
        w = wa_ref[...]
        gv = g_ref[...]
        u_scr[0:HALO, :] = jnp.where(first, 0.0, _last8(cap_ref) * _last8(xap_ref))
        u_scr[HALO + tb:2 * HALO + tb, :] = _first8(can_ref) * _first8(xan_ref)
        acc_scr[...] = jnp.zeros_like(acc_scr)

        def forward_part(n, carry):
            r = pl.multiple_of(n * ROWS_A, ROWS_A)
            rows = pl.ds(r, ROWS_A)
            ba = _rd(ba_ref, rows)
            u_scr[pl.ds(pl.multiple_of(HALO + r, HALO), ROWS_A), :] = _rd(ca_ref, rows) * _rd(xa_ref, rows)
            win = u_scr[pl.ds(r, ROWS_A + HALO), :]
            u = [_past(win, s) for s in range(3)]
            cv = w[2:3] * u[0] + w[1:2] * u[1] + w[0:1] * u[2]
            dya, dgc = _rms_bwd(ba * cv, gv, _rd(dy_ref, rows))
            dcv = dya * ba
            d_scr[rows, :] = dcv
            dp_ref[rows, 2 * D:3 * D] = (dya * cv).astype(bf16)
            acc_scr[0:8, :] += _fold8(dgc)
            for k in range(3):
                acc_scr[8 + 8 * k:16 + 8 * k, :] += _fold8(dcv * u[2 - k])
            return carry

        lax.fori_loop(0, tb // ROWS_A, forward_part, 0, unroll=UNROLL)

        start = HALO + tb
        cvn = (w[2:3] * u_scr[pl.ds(start, HALO), :] + w[1:2] * u_scr[pl.ds(start - 1, HALO), :]
               + w[0:1] * u_scr[pl.ds(start - 2, HALO), :])
        ban = _first8(ban_ref)
        dyan, _ = _rms_bwd(ban * cvn, gv, _first8(dyn_ref))
        d_scr[tb:tb + HALO, :] = jnp.where(last, 0.0, dyan * ban)

        def backward_part(n, carry):
            r = pl.multiple_of(n * ROWS_A, ROWS_A)
            rows = pl.ds(r, ROWS_A)
            win = d_scr[pl.ds(r, ROWS_A + HALO), :]
            du = w[2:3] * _future(win, 0) + w[1:2] * _future(win, 1) + w[0:1] * _future(win, 2)
            dp_ref[rows, 0:D] = (du * _rd(ca_ref, rows)).astype(bf16)
            dp_ref[rows, D:2 * D] = (du * _rd(xa_ref, rows)).astype(bf16)
            return carry

        lax.fori_loop(0, tb // ROWS_A, backward_part, 0, unroll=UNROLL)

        row = lax.broadcasted_iota(jnp.int32, (8, D), 0)
        dw = jnp.zeros((8, D), f32)
        for k in range(3):
            dw = jnp.where(row == k, jnp.sum(acc_scr[8 + 8 * k:16 + 8 * k, :], axis=0, keepdims=True), dw)
        _accum(dwa_ref, dw, i == 0)
        _accum(dg_ref, jnp.sum(acc_scr[0:8, :], axis=0, keepdims=True), i == 0)

    blk = lambda c: pl.BlockSpec((tb, D), lambda i: (i, c))
    prv = lambda c: pl.BlockSpec((HBLK, D), _halo_prev(tb, c))
    nxt = lambda c: pl.BlockSpec((HBLK, D), _halo_next(tb, c, t))
    return pl.pallas_call(
        body, name=name, grid=(t // tb,),
        in_specs=[blk(0), blk(1), blk(2), blk(0), prv(0), prv(1), nxt(0), nxt(1), nxt(2), nxt(0),
                  pl.BlockSpec((8, D), lambda i: (0, 0)), pl.BlockSpec((1, D), lambda i: (0, 0))] + _token_spec(token),
        out_specs=[pl.BlockSpec((tb, 3 * D), lambda i: (i, 0)), pl.BlockSpec((8, D), lambda i: (0, 0)),
                   pl.BlockSpec((1, D), lambda i: (0, 0))],
        out_shape=[_sds((t, PROJ), bf16), _sds((8, D), f32), _sds((1, D), f32)],
        scratch_shapes=[pltpu.VMEM((tb + 2 * HALO, D), f32), pltpu.VMEM((tb + HALO, D), f32), pltpu.VMEM((32, D), f32)],
        compiler_params=_cparams(1))(proj, proj, proj, dcat, proj, proj, proj, proj, proj, dcat, wa, g,
                                     *_token_arg(token))


CB = 512
XBC_BLK0 = COL_XBC // CB


def conv_b_fwd(proj, ws, bs, seq, tb, name):
    t = proj.shape[0]
    bps = seq // tb

    def body(x_ref, xp_ref, w_ref, b_ref, o_ref, x_scr):
        first = (pl.program_id(1) % bps) == 0
        x_scr[0:HALO, :] = jnp.where(first, 0.0, _last8(xp_ref))
        w, bias = w_ref[...], b_ref[...]

        def chunk(n, carry):
            r = pl.multiple_of(n * ROWS_B, ROWS_B)
            rows = pl.ds(r, ROWS_B)
            x_scr[pl.ds(pl.multiple_of(HALO + r, HALO), ROWS_B), :] = _rd(x_ref, rows)
            win = x_scr[pl.ds(r, ROWS_B + HALO), :]
            xc = bias + w[3:4] * _past(win, 0)
            for k in range(3):
                xc = xc + w[k:k + 1] * _past(win, 3 - k)
            o_ref[rows, :] = xc * _sigmoid(xc)
            return carry

        lax.fori_loop(0, tb // ROWS_B, chunk, 0, unroll=UNROLL)

    return pl.pallas_call(
        body, name=name, grid=(XBC // CB, t // tb),
        in_specs=[pl.BlockSpec((tb, CB), lambda j, i: (i, XBC_BLK0 + j)),
                  pl.BlockSpec((HBLK, CB), lambda j, i: (jnp.maximum(i * (tb // HBLK) - 1, 0), XBC_BLK0 + j)),
                  pl.BlockSpec((8, CB), lambda j, i: (0, j)), pl.BlockSpec((1, CB), lambda j, i: (0, j))],
        out_specs=pl.BlockSpec((tb, CB), lambda j, i: (i, j)),
        out_shape=_sds((t, XBC), f32),
        scratch_shapes=[pltpu.VMEM((tb + HALO, CB), f32)],
        compiler_params=_cparams(2))(proj, proj, ws, bs)


def conv_b_bwd(proj, dxs, ws, bs, dproj, seq, tb, name):
    t = proj.shape[0]
    bps = seq // tb

    def body(x_ref, xp_ref, xn_ref, d_ref, dn_ref, w_ref, b_ref, dproj_ref, dx_ref, dw_ref, db_ref, x_scr, d_scr,
             acc_scr):
        i = pl.program_id(1)
        first = (i % bps) == 0
        last = (i % bps) == bps - 1
        w = w_ref[...]
        bias = b_ref[...]
        x_scr[0:HALO, :] = jnp.where(first, 0.0, _last8(xp_ref))
        x_scr[HALO + tb:2 * HALO + tb, :] = _first8(xn_ref)
        acc_scr[...] = jnp.zeros_like(acc_scr)

        def dsilu(xc, d):
            sg = _sigmoid(xc)
            return d * (sg * (1.0 + xc * (1.0 - sg)))

        def forward_part(n, carry):
            r = pl.multiple_of(n * ROWS_B, ROWS_B)
            rows = pl.ds(r, ROWS_B)
            x_scr[pl.ds(pl.multiple_of(HALO + r, HALO), ROWS_B), :] = _rd(x_ref, rows)
            win = x_scr[pl.ds(r, ROWS_B + HALO), :]
            xs = [_past(win, s) for s in range(4)]
            xc = bias + w[3:4] * xs[0]
            for k in range(3):
                xc = xc + w[k:k + 1] * xs[3 - k]
            dxc = dsilu(xc, _rd(d_ref, rows))
            d_scr[rows, :] = dxc
            acc_scr[0:8, :] += _fold8(dxc)
            for k in range(4):
                acc_scr[8 + 8 * k:16 + 8 * k, :] += _fold8(dxc * xs[3 - k])
            return carry

        lax.fori_loop(0, tb // ROWS_B, forward_part, 0, unroll=UNROLL)

        start = HALO + tb
        xcn = bias + w[3:4] * x_scr[pl.ds(start, HALO), :]
        for k in range(3):
            xcn = xcn + w[k:k + 1] * x_scr[pl.ds(start - 3 + k, HALO), :]
        d_scr[tb:tb + HALO, :] = jnp.where(last, 0.0, dsilu(xcn, _first8(dn_ref)))

        def backward_part(n, carry):
            r = pl.multiple_of(n * ROWS_B, ROWS_B)
            win = d_scr[pl.ds(r, ROWS_B + HALO), :]
            dx = w[3:4] * _future(win, 0)
            for k in range(3):
                dx = dx + w[k:k + 1] * _future(win, 3 - k)
            dx_ref[pl.ds(r, ROWS_B), :] = dx.astype(bf16)
            return carry

        lax.fori_loop(0, tb // ROWS_B, backward_part, 0, unroll=UNROLL)

        row = lax.broadcasted_iota(jnp.int32, (8, CB), 0)
        dw = jnp.zeros((8, CB), f32)
        for k in range(4):
            dw = jnp.where(row == k, jnp.sum(acc_scr[8 + 8 * k:16 + 8 * k, :], axis=0, keepdims=True), dw)
        _accum(dw_ref, dw, i == 0)
        _accum(db_ref, jnp.sum(acc_scr[0:8, :], axis=0, keepdims=True), i == 0)

    nh = t // HBLK
    return pl.pallas_call(
        body, name=name, grid=(XBC // CB, t // tb),
        in_specs=[pl.BlockSpec((tb, CB), lambda j, i: (i, XBC_BLK0 + j)),
                  pl.BlockSpec((HBLK, CB), lambda j, i: (jnp.maximum(i * (tb // HBLK) - 1, 0), XBC_BLK0 + j)),
                  pl.BlockSpec((HBLK, CB), lambda j, i: (jnp.minimum((i + 1) * (tb // HBLK), nh - 1), XBC_BLK0 + j)),
                  pl.BlockSpec((tb, CB), lambda j, i: (i, j)),
                  pl.BlockSpec((HBLK, CB), lambda j, i: (jnp.minimum((i + 1) * (tb // HBLK), nh - 1), j)),
                  pl.BlockSpec((8, CB), lambda j, i: (0, j)), pl.BlockSpec((1, CB), lambda j, i: (0, j)),
                  pl.BlockSpec(memory_space=pl.ANY)],
        out_specs=[pl.BlockSpec((tb, CB), lambda j, i: (i, XBC_BLK0 + j)), pl.BlockSpec((8, CB), lambda j, i: (0, j)),
                   pl.BlockSpec((1, CB), lambda j, i: (0, j))],
        out_shape=[_sds((t, PROJ), bf16), _sds((8, XBC), f32), _sds((1, XBC), f32)],
        input_output_aliases={7: 0},
        scratch_shapes=[pltpu.VMEM((tb + 2 * HALO, CB), f32), pltpu.VMEM((tb + HALO, CB), f32),
                        pltpu.VMEM((40, CB), f32)],
        compiler_params=_cparams(2))(proj, proj, proj, dxs, dxs, ws, bs, dproj)


def place_columns(buf, part, col_block, tb, name):
    t, wdt = part.shape

    def body(p_ref, buf_ref, o_ref):
        o_ref[...] = p_ref[...]

    return pl.pallas_call(
        body, name=name, grid=(t // tb,),
        in_specs=[pl.BlockSpec((tb, wdt), lambda i: (i, 0)), pl.BlockSpec(memory_space=pl.ANY)],
        out_specs=pl.BlockSpec((tb, wdt), lambda i: (i, col_block)), out_shape=_sds(buf.shape, buf.dtype),
        input_output_aliases={1: 0}, compiler_params=_cparams(1))(part, buf)


GW = D // NG
EXPAND_TERMS = 2
REDUCE_TERMS = 1


def _ssd_consts():
    head_of_lane = jnp.arange(D) // HP
    expand = (jnp.arange(CH)[:, None] == head_of_lane[None, :]).astype(bf16)
    tri = (jnp.arange(CH)[:, None] >= jnp.arange(CH)[None, :]).astype(f32)
    return expand, tri


def _ssd_common(par_ref, dtr_ref, e_ref, tri_ref):
    par = par_ref[...]
    dtb, alog, dsk = par[0:1], par[1:2], par[2:3]
    lane = lax.broadcasted_iota(jnp.int32, (CH, CH), 1)
    a = -jnp.exp(alog)
    dtr = dtr_ref[...].astype(f32) + dtb
    sp = jnp.maximum(dtr, 0.0) + jnp.log(1.0 + jnp.exp(-jnp.abs(dtr)))
    dt = jnp.where(lane < NH, sp, 0.0)
    cs = jnp.dot(tri_ref[...], dt * a, precision=lax.Precision.HIGHEST, preferred_element_type=f32)
    cs_last = cs[CH - 1:CH, :]
    dte = jnp.exp(cs_last - cs)
    ecs = jnp.exp(cs)
    ecl = jnp.exp(cs_last)
    e = e_ref[...]
    row8 = lax.broadcasted_iota(jnp.int32, (8, CH), 0)
    r8 = _split_dot(jnp.where(row8 == 0, ecl, jnp.where(row8 == 1, dsk, 0.0)), e, 3)
    return dict(a=a, dtr=dtr, dt=dt, cs=cs, cst=cs.T, dte=dte, ecs=ecs, ecl=ecl, e=e, lane=lane,
                dt_x=_split_dot(dt, e, EXPAND_TERMS), dte_x=_split_dot(dte, e, EXPAND_TERMS),
                ecs_x=_split_dot(ecs, e, EXPAND_TERMS),
                ecl_x=r8[0:1], dsk_x=r8[1:2])


def _decay_matrix(c, h):
    li = lax.broadcasted_iota(jnp.int32, (CH, CH), 0)
    seg = c["cs"][:, h:h + 1] - c["cst"][h:h + 1, :]
    return jnp.exp(jnp.where(li >= c["lane"], seg, -jnp.inf))


def _gate_norm_fwd(y, z, gs):
    zg = z * _sigmoid(z)
    yg = y * zg
    return jnp.concatenate([_rms_fwd(yg[:, k * GW:(k + 1) * GW], gs[:, k * GW:(k + 1) * GW]) for k in range(NG)], axis=1)


def ssd_fwd(xbcs, proj, par, gs, cat, seq, name):
    t = xbcs.shape[0]
    nc = seq // CH
    expand, tri = _ssd_consts()

    def body(xs_ref, b_ref, c_ref, dtr_ref, z_ref, par_ref, e_ref, tri_ref, gs_ref, cat_ref, yn_ref, y_ref, st_ref,
             p_scr, yd_scr):
        @pl.when(pl.program_id(0) % nc == 0)
        def _():
            p_scr[...] = jnp.zeros_like(p_scr)

        c = _ssd_common(par_ref, dtr_ref, e_ref, tri_ref)
        xs = xs_ref[...]
        xdt = xs * c["dt_x"]
        xdt_b = xdt.astype(bf16)
        xdte_b = (xdt * c["dte_x"]).astype(bf16)
        p = p_scr[...]
        st_ref[0] = p
        p_b = p.astype(bf16)
        lo = c["lane"] < HP
        for g in range(NG):
            bg = b_ref[:, g * NS:(g + 1) * NS].astype(bf16)
            cg = c_ref[:, g * NS:(g + 1) * NS].astype(bf16)
            gmat = _dot_nt(cg, bg)
            for q in range(GW // CH):
                col = g * GW + q * CH
                xp = xdt_b[:, col:col + CH]
                h0 = col // HP
                m0 = (gmat * _decay_matrix(c, h0)).astype(bf16)
                m1 = (gmat * _decay_matrix(c, h0 + 1)).astype(bf16)
                stacked = jnp.concatenate([jnp.where(lo, xp, jnp.zeros_like(xp)),
                                           jnp.where(lo, jnp.zeros_like(xp), xp)], axis=0)
                yd_scr[:, col:col + CH] = _dot(jnp.concatenate([m0, m1], axis=1), stacked)
            gsl = slice(g * GW, (g + 1) * GW)
            yoff = _dot(cg, p_b[:, gsl]) * c["ecs_x"][:, gsl]
            yd_scr[:, gsl] = yd_scr[:, gsl] + yoff
            p_scr[:, gsl] = p[:, gsl] * c["ecl_x"][:, gsl] + _dot_tn(bg, xdte_b[:, gsl])
        y = yd_scr[...] + c["dsk_x"] * xs
        y_ref[...] = y
        yn_ref[...] = _gate_norm_fwd(y, z_ref[...].astype(f32), gs_ref[...]).astype(bf16)

    nb = t // CH
    return pl.pallas_call(
        body, name=name, grid=(nb,),
        in_specs=[pl.BlockSpec((CH, D), lambda i: (i, 0)),
                  pl.BlockSpec((CH, NG * NS), lambda i: (i, D // (NG * NS))),
                  pl.BlockSpec((CH, NG * NS), lambda i: (i, D // (NG * NS) + 1)),
                  pl.BlockSpec((CH, CH), lambda i: (i, COL_DT // CH)),
                  pl.BlockSpec((CH, D), lambda i: (i, COL_Z // D)),
                  pl.BlockSpec((8, CH), lambda i: (0, 0)), pl.BlockSpec((CH, D), lambda i: (0, 0)),
                  pl.BlockSpec((CH, CH), lambda i: (0, 0)), pl.BlockSpec((1, D), lambda i: (0, 0)),
                  pl.BlockSpec(memory_space=pl.ANY)],
        out_specs=[pl.BlockSpec((CH, D), lambda i: (i, 1)), pl.BlockSpec((CH, D), lambda i: (i, 0)),
                   pl.BlockSpec((1, NS, D), lambda i: (i, 0, 0))],
        out_shape=[_sds((t, 2 * D), bf16), _sds((t, D), f32), _sds((nb, NS, D), f32)],
        input_output_aliases={9: 0},
        scratch_shapes=[pltpu.VMEM((NS, D), f32), pltpu.VMEM((CH, D), f32)],
        compiler_params=_cparams(1))(xbcs, xbcs, xbcs, proj, proj, par, expand, tri, gs, cat)


def ssd_bwd(xbcs, proj, ypre, states, dcat, par, gs, dproj, seq, name):
    t = xbcs.shape[0]
    nc = seq // CH
    expand, tri = _ssd_consts()

    def body(xs_ref, b_ref, c_ref, dtr_ref, z_ref, y_ref, st_ref, dyn_ref, par_ref, e_ref, tri_ref, gs_ref, dproj_ref,
             dx_ref, dz_ref, ddt_ref, dpar_ref, dgs_ref, dp_scr, dxdt_scr):
        i = pl.program_id(0)

        @pl.when(i % nc == 0)
        def _():
            dp_scr[...] = jnp.zeros_like(dp_scr)

        c = _ssd_common(par_ref, dtr_ref, e_ref, tri_ref)
        e = c["e"]
        lane = c["lane"]
        sub = lax.broadcasted_iota(jnp.int32, (CH, CH), 0)
        xs = xs_ref[...]
        xdt = xs * c["dt_x"]
        xdt_b = xdt.astype(bf16)
        xdte_b = (xdt * c["dte_x"]).astype(bf16)
        p = st_ref[0]
        p_b = p.astype(bf16)
        dpn = dp_scr[...]
        dpn_b = dpn.astype(bf16)

        y, z, gs_v = y_ref[...], z_ref[...].astype(f32), gs_ref[...]
        zs = _sigmoid(z)
        zg = z * zs
        yg = y * zg
        parts, gparts = [], []
        for k in range(NG):
            sl = slice(k * GW, (k + 1) * GW)
            dxk, dgk = _rms_bwd(yg[:, sl], gs_v[:, sl], dyn_ref[:, sl].astype(f32))
            parts.append(dxk)
            gparts.append(dgk)
        dyg = jnp.concatenate(parts, axis=1)
        dgs_rows = jnp.concatenate(gparts, axis=1)
        dy = dyg * zg
        dz_ref[...] = (dyg * y * (zs * (1.0 + z * (1.0 - zs)))).astype(bf16)
        dy_b = dy.astype(bf16)
        dq_b = (dy * c["ecs_x"]).astype(bf16)

        lo = lane < HP
        dcs = jnp.zeros((CH, CH), f32)
        dcst = jnp.zeros((CH, CH), f32)
        for g in range(NG):
            gsl = slice(g * GW, (g + 1) * GW)
            bg = b_ref[:, g * NS:(g + 1) * NS].astype(bf16)
            cg = c_ref[:, g * NS:(g + 1) * NS].astype(bf16)
            gmat = _dot_nt(cg, bg)
            dgm = jnp.zeros((CH, CH), f32)
            for q in range(GW // CH):
                col = g * GW + q * CH
                xp = xdt_b[:, col:col + CH]
                dyp = dy_b[:, col:col + CH]
                zero = jnp.zeros_like(dyp)
                xp2 = jnp.concatenate([jnp.where(lo, xp, zero), jnp.where(lo, zero, xp)], axis=0)
                dy2 = jnp.concatenate([jnp.where(lo, dyp, zero), jnp.where(lo, zero, dyp)], axis=0)
                dm2 = _dot_nt(dyp, xp2)
                ms = []
                for hh in range(2):
                    h = col // HP + hh
                    dec = _decay_matrix(c, h)
                    m = gmat * dec
                    dm = dm2[:, hh * CH:(hh + 1) * CH]
                    dseg = dm * m
                    dcs = dcs + jnp.where(lane == h, jnp.sum(dseg, axis=1, keepdims=True), 0.0)
                    dcst = dcst + jnp.where(sub == h, jnp.sum(dseg, axis=0, keepdims=True), 0.0)
                    dgm = dgm + dm * dec
                    ms.append(m.astype(bf16))
                dxdt_scr[:, col:col + CH] = _dot_tn(jnp.concatenate(ms, axis=0), dy2)
            dgm_b = dgm.astype(bf16)
            bds = _dot(bg, dpn_b[:, gsl])
            dxdt_scr[:, gsl] = dxdt_scr[:, gsl] + c["dte_x"][:, gsl] * bds
            dc_g = _dot(dgm_b, bg) + _dot_nt(dq_b[:, gsl], p_b[:, gsl])
            db_g = _dot_tn(dgm_b, cg) + _dot_nt(xdte_b[:, gsl], dpn_b[:, gsl])
            dx_ref[:, D + g * NS:D + (g + 1) * NS] = db_g
            dx_ref[:, D + NG * NS + g * NS:D + NG * NS + (g + 1) * NS] = dc_g
            dp_scr[:, gsl] = dpn[:, gsl] * c["ecl_x"][:, gsl] + _dot_tn(cg, dq_b[:, gsl])
            q_g = _dot(cg, p_b[:, gsl])
            e_g = e[:, gsl]
            dcs = dcs + c["ecs"] * _split_dot(dy[:, gsl] * q_g, e_g, REDUCE_TERMS, nt=True)
            ddte = _split_dot(xdt[:, gsl] * bds, e_g, REDUCE_TERMS, nt=True) * c["dte"]
            dcs = dcs - ddte
            dcs = dcs + jnp.where(sub == CH - 1, jnp.sum(ddte, axis=0, keepdims=True), 0.0)

        decl = _split_dot(jnp.broadcast_to(jnp.sum(dpn * p, axis=0, keepdims=True), (8, D)), e, 2, nt=True)[0:1]
        dcs = dcs + jnp.where(sub == CH - 1, c["ecl"] * decl, 0.0)
        dcs = dcs - dcst.T
        dadt = lax.dot_general(tri_ref[...], dcs, (((0,), (0,)), ((), ())), precision=lax.Precision.HIGHEST,
                               preferred_element_type=f32)
        dxdt = dxdt_scr[...]
        ddt = dadt * c["a"] + _split_dot(dxdt * xs, e, REDUCE_TERMS, nt=True)
        ddtr = jnp.where(lane < NH, ddt * _sigmoid(c["dtr"]), 0.0)
        ddt_ref[...] = ddtr.astype(bf16)
        dx_ref[:, 0:D] = dxdt * c["dt_x"] + c["dsk_x"] * dy
        dsk = _split_dot(jnp.broadcast_to(jnp.sum(dy * xs, axis=0, keepdims=True), (8, D)), e, 2, nt=True)[0:1]
        dalog = jnp.sum(dadt * c["dt"], axis=0, keepdims=True) * c["a"]
        row8 = lax.broadcasted_iota(jnp.int32, (8, CH), 0)
        dpar = jnp.where(row8 == 0, jnp.sum(ddtr, axis=0, keepdims=True),
                         jnp.where(row8 == 1, dalog, jnp.where(row8 == 2, dsk, 0.0)))
        dpar = jnp.where(lax.broadcasted_iota(jnp.int32, (8, CH), 1) < NH, dpar, 0.0)
        _accum(dpar_ref, dpar, i == 0)
        _accum(dgs_ref, jnp.sum(dgs_rows, axis=0, keepdims=True), i == 0)

    nb = t // CH
    rev = lambda i: (i // nc) * nc + (nc - 1 - i % nc)
    return pl.pallas_call(
        body, name=name, grid=(nb,),
        in_specs=[pl.BlockSpec((CH, D), lambda i: (rev(i), 0)),
                  pl.BlockSpec((CH, NG * NS), lambda i: (rev(i), D // (NG * NS))),
                  pl.BlockSpec((CH, NG * NS), lambda i: (rev(i), D // (NG * NS) + 1)),
                  pl.BlockSpec((CH, CH), lambda i: (rev(i), COL_DT // CH)),
                  pl.BlockSpec((CH, D), lambda i: (rev(i), COL_Z // D)),
                  pl.BlockSpec((CH, D), lambda i: (rev(i), 0)),
                  pl.BlockSpec((1, NS, D), lambda i: (rev(i), 0, 0)),
                  pl.BlockSpec((CH, D), lambda i: (rev(i), 1)),
                  pl.BlockSpec((8, CH), lambda i: (0, 0)), pl.BlockSpec((CH, D), lambda i: (0, 0)),
                  pl.BlockSpec((CH, CH), lambda i: (0, 0)), pl.BlockSpec((1, D), lambda i: (0, 0)),
                  pl.BlockSpec(memory_space=pl.ANY)],
        out_specs=[pl.BlockSpec((CH, XBC), lambda i: (rev(i), 0)), pl.BlockSpec((CH, D), lambda i: (rev(i), COL_Z // D)),
                   pl.BlockSpec((CH, CH), lambda i: (rev(i), 0)),
                   pl.BlockSpec((8, CH), lambda i: (0, 0)), pl.BlockSpec((1, D), lambda i: (0, 0))],
        out_shape=[_sds((t, XBC), f32), _sds((t, PROJ), bf16), _sds((t, CH), bf16), _sds((8, CH), f32), _sds((1, D), f32)],
        input_output_aliases={12: 1},
        scratch_shapes=[pltpu.VMEM((NS, D), f32), pltpu.VMEM((CH, D), f32)],
        compiler_params=_cparams(1))(xbcs, xbcs, xbcs, proj, proj, ypre, states, dcat, par, expand, tri, gs, dproj)


def loss_head(y, target, tb, name):
    t = y.shape[0]

    def body(y_ref, t_ref, s_ref, dy_ref):
        err = y_ref[...] - t_ref[...]
        dy_ref[...] = err * (1.0 / D)
        _accum(s_ref, jnp.zeros((8, CH), f32) + jnp.sum(err * err), pl.program_id(0) == 0)

    return pl.pallas_call(
        body, name=name, grid=(t // tb,),
        in_specs=[pl.BlockSpec((tb, D), lambda i: (i, 0)), pl.BlockSpec((tb, D), lambda i: (i, 0))],
        out_specs=[pl.BlockSpec((8, CH), lambda i: (0, 0)), pl.BlockSpec((tb, D), lambda i: (i, 0))],
        out_shape=[_sds((8, CH), f32), _sds((t, D), f32)],
        compiler_params=_cparams(1))(y, target)


def _tiles(t, seq):
    tm = min(512, t)
    return dict(tm=tm, tm_small=min(256, t), tm_large=min(1024, t), tm_huge=min(2048, t), tb=min(512, seq))


def local_step(x, target, depth, weights_of, seq, grads_done=None):
    t = x.shape[0]
    ts = _tiles(t, seq)
    tm, tl, th, tb = ts["tm"], ts["tm_large"], ts["tm_huge"], ts["tb"]
    saved, ws = [], []
    for l in range(depth):
        w = weights_of(l, x)
        ws.append(w)
        proj, h1 = norm_matmul(x, w["g1"], w["win"], th, 1152, bf16, "in_proj", token=w.get("token"))
        cat = group_a_fwd(proj, w["wa"], w["ga"], seq, tb, "group_a_fwd")
        xbcs = conv_b_fwd(proj, w["ws"], w["bs"], seq, tb, "conv_b_fwd")
        cat, ypre, states = ssd_fwd(xbcs, proj, w["par"], w["gs"], cat, seq, "ssd_fwd")
        if "late" in w:
            w.update(w.pop("late")(cat))
        mix, x2 = matmul_postnorm(cat, w["wo"], x, w["g2"], tl, False, "out_proj")
        fp, h2 = norm_matmul(x2, w["g3"], w["wu"], th, 1024, bf16, "mlp_up")
        o, x3 = matmul_postnorm(fp, w["wd"], x2, w["g4"], tm, True, "mlp_down")
        saved.append(dict(x=x, proj=proj, h1=h1, xbcs=xbcs, ypre=ypre, states=states, cat=cat, mix=mix, x2=x2,
                          fp=fp, h2=h2, o=o))
        x = x3
    sse, dx = loss_head(x, target, tm, "loss_head")
    grads = [None] * depth
    for l in reversed(range(depth)):
        s, w = saved[l], ws[l]
        do, dg4, dfp = postnorm_bwd_matmul(s["o"], w["g4"], dx, w["wd"], s["fp"], tl, 1024, bf16, "mlp_down_bwd")
        dwd = matmul_tn(s["fp"], do, 512, 1024, True, "mlp_down_dw")
        dx2, dg3 = matmul_prenorm_bwd(dfp, w["wu"], s["x2"], w["g3"], dx, tm, "mlp_up_bwd")
        dwu = matmul_tn(s["h2"], dfp, 512, 1024, False, "mlp_up_dw", col_blocks=True)
        dmix, dg2, dcat = postnorm_bwd_matmul(s["mix"], w["g2"], dx2, w["wo"], None, tl, 1024, bf16, "out_proj_bwd")
        dwo = matmul_tn(s["cat"], dmix, 512, 1024, False, "out_proj_dw")
        token = None if grads_done is None else grads_done(l, dict(wo=dwo, wu=dwu, wd=dwd), False)
        dproj, dwa, dga = group_a_bwd(s["proj"], dcat, w["wa"], w["ga"], seq, tb, "group_a_bwd", token=token)
        dxbcs, dproj, ddt, dpar, dgs = ssd_bwd(s["xbcs"], s["proj"], s["ypre"], s["states"], dcat, w["par"], w["gs"],
                                               dproj, seq, "ssd_bwd")
        dproj, dws, dbs = conv_b_bwd(s["proj"], dxbcs, w["ws"], w["bs"], dproj, seq, tb, "conv_b_bwd")
        dproj = place_columns(dproj, ddt, COL_DT // CH, tm, "place_ddt")
        dwin = matmul_tn(s["h1"], dproj, 512, 1152, False, "in_proj_dw")
        token = None if grads_done is None else grads_done(l, dict(win=dwin), True)
        dx, dg1 = matmul_prenorm_bwd(dproj, w["win"], s["x"], w["g1"], dx2, ts["tm_small"], "in_proj_bwd", token=token)
        grads[l] = dict(win=dwin, wo=dwo, wu=dwu, wd=dwd, wa=dwa, ws=dws, bs=dbs, par=dpar,
                        g1=dg1, ga=dga, gs=dgs, g2=dg2, g3=dg3, g4=dg4)
    return sse, dx, grads


GROUPS = {
    "chips": [(1, 0, 0), (0, 1, 0), (1, 1, 0)],
    "pair": [(0, 0, 1)],
    "all": [(1, 0, 0), (0, 1, 0), (1, 1, 0), (0, 0, 1), (1, 0, 1), (0, 1, 1), (1, 1, 1)],
}


def _group_index(group, x, y, c):
    return {"chips": 2 * x + y, "pair": c, "all": 4 * x + 2 * y + c}[group]


def _chunk_indices(shape, pieces):
    if len(shape) < 3:
        return [()]
    lead = [()]
    for n in shape[:-2]:
        lead = [i + (k,) for i in lead for k in range(n)]
    rows = shape[-2]
    split = max(1, pieces // len(lead))
    while split > 1 and (rows % split or (rows // split) % 16):
        split -= 1
    step = rows // split
    return [i + (pl.ds(s * step, step),) for i in lead for s in range(split)]


def _exchange(arrays, out_shapes, group, src_view, dst_view, view_shape, name, own, pieces=16):
    masks = GROUPS[group]
    na, nm = len(arrays), len(masks)
    cuts = [_chunk_indices(view_shape(a), pieces) for a in range(na)]

    def body(*refs):
        ins, outs = refs[:na], refs[na:2 * na]
        send_sems, recv_sems = refs[2 * na:2 * na + 2]
        local_sems = refs[2 * na + 2] if own else None
        x, y, c = lax.axis_index("x"), lax.axis_index("y"), lax.axis_index("c")
        me = _group_index(group, x, y, c)
        peers = []
        for mx, my, mc in masks:
            px, py, pc = (1 - x if mx else x), (1 - y if my else y), (1 - c if mc else c)
            peers.append(((px, py, pc), _group_index(group, px, py, pc)))

        def part(ref, idx):
            return ref.at[idx] if idx else ref

        if own:
            for a in range(na):
                for idx in cuts[a]:
                    pltpu.make_async_copy(part(src_view(ins[a], a, me), idx), part(dst_view(outs[a], a, me), idx),
                                          local_sems.at[a]).start()
        for a in range(na):
            for j, (dev, pidx) in enumerate(peers):
                for idx in cuts[a]:
                    pltpu.make_async_remote_copy(
                        src_ref=part(src_view(ins[a], a, pidx), idx), dst_ref=part(dst_view(outs[a], a, me), idx),
                        send_sem=send_sems.at[a * nm + j], recv_sem=recv_sems.at[a * nm + j],
                        device_id=dev, device_id_type=MESH).start()
        whole = []
        for a in range(na):
            for j, (dev, pidx) in enumerate(peers):
                whole.append(pltpu.make_async_remote_copy(
                    src_ref=src_view(ins[a], a, pidx), dst_ref=dst_view(outs[a], a, pidx),
                    send_sem=send_sems.at[a * nm + j], recv_sem=recv_sems.at[a * nm + j],
                    device_id=dev, device_id_type=MESH))
        for cp in whole:
            cp.wait_recv()
        for cp in whole:
            cp.wait_send()
        if own:
            for a in range(na):
                pltpu.make_async_copy(src_view(ins[a], a, me), dst_view(outs[a], a, me), local_sems.at[a]).wait()

    hbm = pl.BlockSpec(memory_space=pltpu.HBM)
    sems = [pltpu.SemaphoreType.DMA((na * nm,)), pltpu.SemaphoreType.DMA((na * nm,))]
    return pl.pallas_call(
        body, name=name, in_specs=[hbm] * na, out_specs=[hbm] * na,
        out_shape=[_sds(s, a.dtype) for s, a in zip(out_shapes, arrays)],
        scratch_shapes=sems + ([pltpu.SemaphoreType.DMA((na,))] if own else []))(*arrays)


def all_gather(arrays, group, name, slot_axis=0, own=True):
    n = len(GROUPS[group]) + 1
    shapes = [a.shape[:slot_axis] + (n,) + a.shape[slot_axis:] for a in arrays]
    lead = (slice(None),) * slot_axis
    return _exchange(arrays, shapes, group, lambda r, a, i: r, lambda r, a, i: r.at[lead + (i,)],
                     lambda a: arrays[a].shape, name, own)


HBM_SPEC = pl.BlockSpec(memory_space=pltpu.HBM)
SEM_SPEC = pl.BlockSpec(memory_space=pltpu.SEMAPHORE)
DATAFLOW = pltpu.SideEffectType.DATAFLOW_SIDE_EFFECTING
N_CHIPS = 4


def _chip_peers(x, y, c):
    out = []
    for mx, my, _ in GROUPS["chips"]:
        px, py = (1 - x if mx else x), (1 - y if my else y)
        out.append(((px, py, c), 2 * px + py))
    return out


def _weight_views(shards):
    half = [s.shape[0] // 2 for s in shards]
    return dict(src=lambda ref, a, c, to_chip: ref.at[pl.ds(c * half[a], half[a])],
                dst=lambda ref, a, c, from_chip: ref.at[from_chip, pl.ds(c * half[a], half[a])],
                rows=lambda a: half[a])


def _grad_views(sums):
    return dict(src=lambda ref, a, c, to_chip: ref.at[to_chip], dst=lambda ref, a, c, from_chip: ref.at[from_chip],
                rows=lambda a: sums[a].shape[1])


def chips_start(sources, zones, views, name, pieces=4, after=None):
    na, nm = len(sources), N_CHIPS - 1

    def body(*refs):
        ins, lands = refs[:na], refs[na:2 * na]
        n_in = 2 * na + len(_token_arg(after))
        send_sems, recv_sems, token = refs[n_in], refs[n_in + 1], refs[-1]
        x, y, c = lax.axis_index("x"), lax.axis_index("y"), lax.axis_index("c")
        chip = 2 * x + y
        for a in range(na):
            step = views["rows"](a) // pieces
            for j, (dev, to_chip) in enumerate(_chip_peers(x, y, c)):
                for q in range(pieces):
                    rows = pl.ds(q * step, step)
                    pltpu.make_async_remote_copy(
                        src_ref=views["src"](ins[a], a, c, to_chip).at[rows],
                        dst_ref=views["dst"](lands[a], a, c, chip).at[rows],
                        send_sem=send_sems.at[a * nm + j], recv_sem=recv_sems.at[a * nm + j],
                        device_id=dev, device_id_type=MESH).start()
        token[...] = jnp.zeros_like(token)

    both = list(sources) + list(zones)
    outs = pl.pallas_call(
        body, name=name,
        out_shape=(pltpu.SemaphoreType.DMA((na * nm,)), pltpu.SemaphoreType.DMA((na * nm,)),
                   *[pltpu.HBM(b.shape, b.dtype) for b in both], _sds((8, CH), f32)),
        in_specs=[HBM_SPEC] * (2 * na) + _token_spec(after),
        out_specs=(SEM_SPEC, SEM_SPEC, *[HBM_SPEC] * (2 * na), pl.BlockSpec(memory_space=pltpu.VMEM)),
        input_output_aliases={i: 2 + i for i in range(2 * na)},
        compiler_params=pltpu.CompilerParams(has_side_effects=DATAFLOW))(
            *[pltpu.with_memory_space_constraint(b, pltpu.HBM) for b in both], *_token_arg(after))
    return dict(send=outs[0], recv=outs[1], sources=list(outs[2:2 + na]), zones=list(outs[2 + na:2 + 2 * na]),
                token=outs[-1], views=views)


def chips_wait(started, after, name):
    sources, zones, views = started["sources"], started["zones"], started["views"]
    na, nm = len(sources), N_CHIPS - 1

    def body(*refs):
        ins, lands = refs[:na], refs[na:2 * na]
        send_sems, recv_sems = refs[2 * na], refs[2 * na + 1]
        x, y, c = lax.axis_index("x"), lax.axis_index("y"), lax.axis_index("c")
        for a in range(na):
            for j, (dev, peer_chip) in enumerate(_chip_peers(x, y, c)):
                cp = pltpu.make_async_remote_copy(
                    src_ref=views["src"](ins[a], a, c, peer_chip), dst_ref=views["dst"](lands[a], a, c, peer_chip),
                    send_sem=send_sems.at[a * nm + j], recv_sem=recv_sems.at[a * nm + j],
                    device_id=dev, device_id_type=MESH)
                cp.wait_send()
                cp.wait_recv()

    both = list(sources) + list(zones)
    outs = pl.pallas_call(
        body, name=name, out_shape=tuple(pltpu.HBM(b.shape, b.dtype) for b in both),
        in_specs=[HBM_SPEC] * (2 * na) + [SEM_SPEC, SEM_SPEC, pl.BlockSpec(memory_space=pl.ANY)],
        out_specs=tuple([HBM_SPEC] * (2 * na)), input_output_aliases={i: i for i in range(2 * na)},
        compiler_params=pltpu.CompilerParams(has_side_effects=DATAFLOW))(*both, started["send"], started["recv"], after)
    return list(outs[:na]), list(outs[na:])


def weights_share(zones, name):
    na, nm = len(zones), N_CHIPS - 1

    def body(*refs):
        lands = refs[na:2 * na]
        send_sems, recv_sems = refs[2 * na:]
        x, y, c = lax.axis_index("x"), lax.axis_index("y"), lax.axis_index("c")
        chip = 2 * x + y
        sibling = (x, y, 1 - c)
        sends = []
        for a in range(na):
            half = zones[a].shape[1] // 2
            for m in range(1, N_CHIPS):
                mine = lands[a].at[chip ^ m, pl.ds(c * half, half)]
                sends.append(pltpu.make_async_remote_copy(
                    src_ref=mine, dst_ref=mine, send_sem=send_sems.at[a * nm + m - 1],
                    recv_sem=recv_sems.at[a * nm + m - 1], device_id=sibling, device_id_type=MESH))
        for cp in sends:
            cp.start()
        for a in range(na):
            half = zones[a].shape[1] // 2
            for m in range(1, N_CHIPS):
                theirs = lands[a].at[chip ^ m, pl.ds((1 - c) * half, half)]
                pltpu.make_async_remote_copy(
                    src_ref=theirs, dst_ref=theirs, send_sem=send_sems.at[a * nm + m - 1],
                    recv_sem=recv_sems.at[a * nm + m - 1], device_id=sibling, device_id_type=MESH).wait_recv()
        for cp in sends:
            cp.wait_send()

    return pl.pallas_call(
        body, name=name, in_specs=[HBM_SPEC] * na, out_specs=[HBM_SPEC] * na,
        out_shape=[_sds(z.shape, z.dtype) for z in zones], input_output_aliases={i: i for i in range(na)},
        scratch_shapes=[pltpu.SemaphoreType.DMA((na * nm,)), pltpu.SemaphoreType.DMA((na * nm,))])(*zones)


def pair_send_halves(grads, name):
    half = [g.shape[1] // 2 for g in grads]
    shapes = [(g.shape[0], h, g.shape[2]) for g, h in zip(grads, half)]
    return _exchange(grads, shapes, "pair", lambda r, a, i: r.at[:, pl.ds(i * half[a], half[a])],
                     lambda r, a, i: r, lambda a: shapes[a], name, False)


def sum_pair_half(g, recv, core, name, tb=256, by_chip=None):
    nk, r, c = g.shape
    tb = min(tb, r // 2)
    nb = r // 2 // tb

    def body(core_ref, g_ref, r_ref, o_ref):
        s = g_ref[...].astype(f32) + r_ref[...].astype(f32)
        if by_chip is None:
            o_ref[...] = s.astype(bf16)
        else:
            for k in range(by_chip[0]):
                o_ref[k] = s[:, k * by_chip[1]:(k + 1) * by_chip[1]].astype(bf16)

    if by_chip is None:
        out_spec = pl.BlockSpec((None, tb, c), lambda k, i, core_ref: (k, i, 0))
        out_shape = _sds((nk, r // 2, c), bf16)
    else:
        assert nk == 1
        out_spec = pl.BlockSpec((by_chip[0], tb, by_chip[1]), lambda k, i, core_ref: (0, i, 0))
        out_shape = _sds((by_chip[0], r // 2, by_chip[1]), bf16)
    return pl.pallas_call(
        body, name=name,
        grid_spec=pltpu.PrefetchScalarGridSpec(
            num_scalar_prefetch=1, grid=(nk, nb),
            in_specs=[pl.BlockSpec((None, tb, c), lambda k, i, core_ref: (k, core_ref[0] * nb + i, 0)),
                      pl.BlockSpec((None, tb, c), lambda k, i, core_ref: (k, i, 0))],
            out_specs=out_spec),
        out_shape=out_shape, compiler_params=_cparams(2))(jnp.reshape(core, (1,)).astype(jnp.int32), g, recv)


def assemble_columns(blocks, width, name, tb=256):
    n, r, c = blocks.shape

    def body(b_ref, o_ref):
        for k in range(n):
            o_ref[:, k * c:(k + 1) * c] = b_ref[k]
        o_ref[:, n * c:] = jnp.zeros((tb, width - n * c), blocks.dtype)

    return pl.pallas_call(
        body, name=name, grid=(r // tb,), in_specs=[pl.BlockSpec((n, tb, c), lambda i: (0, i, 0))],
        out_specs=pl.BlockSpec((tb, width), lambda i: (i, 0)), out_shape=_sds((r, width), blocks.dtype),
        compiler_params=_cparams(1))(blocks)


def chip_sum_into(acc, layer, own, others, chip, name, tb=256):
    n, r, c = own.shape
    tb = min(tb, r)

    def body(chip_ref, x_ref, y1_ref, y2_ref, y3_ref, acc_ref, o_ref):
        o_ref[...] = ((x_ref[...].astype(f32) + y1_ref[...].astype(f32)) + y2_ref[...].astype(f32)) + y3_ref[...].astype(f32)

    def slot(k):
        return pl.BlockSpec((None, tb, c), lambda i, chip_ref: (chip_ref[0] ^ k, i, 0))

    return pl.pallas_call(
        body, name=name,
        grid_spec=pltpu.PrefetchScalarGridSpec(
            num_scalar_prefetch=1, grid=(r // tb,),
            in_specs=[slot(k) for k in range(n)] + [pl.BlockSpec(memory_space=pl.ANY)],
            out_specs=pl.BlockSpec((None, tb, c), lambda i, chip_ref: (layer, i, 0))),
        out_shape=_sds(acc.shape, f32), input_output_aliases={n + 1: 0}, compiler_params=_cparams(1))(
            jnp.reshape(chip, (1,)).astype(jnp.int32), own, *([others] * (n - 1)), acc)


def adamw_halves(w, g_own, g_recv, m, v, core, name, tb=256):
    depth, r, c = w.shape
    tb = min(tb, r // 2)
    nb = r // 2 // tb

    def body(core_ref, w_ref, go_ref, gr_ref, m_ref, v_ref, g_ref, d_ref, mo_ref, vo_ref):
        gv = jnp.where(pl.program_id(1) == core_ref[0], go_ref[...], gr_ref[...])
        m2 = B1 * m_ref[...] + (1.0 - B1) * gv
        v2 = B2 * v_ref[...] + (1.0 - B2) * (gv * gv)
        m_hat = m2 / (1.0 - B1 ** STEP)
        v_hat = v2 / (1.0 - B2 ** STEP)
        g_ref[...] = gv
        d_ref[...] = -LR * (m_hat / (jnp.sqrt(v_hat) + AEPS) + WD * w_ref[...])
        mo_ref[...] = m2
        vo_ref[...] = v2

    whole = pl.BlockSpec((None, tb, c), lambda l, h, i, core_ref: (l, h * nb + i, 0))
    part = pl.BlockSpec((None, tb, c), lambda l, h, i, core_ref: (l, i, 0))
    return pl.pallas_call(
        body, name=name,
        grid_spec=pltpu.PrefetchScalarGridSpec(num_scalar_prefetch=1, grid=(depth, 2, nb),
                                               in_specs=[whole, part, part, whole, whole], out_specs=[whole] * 4),
        out_shape=[_sds(w.shape, f32)] * 4, compiler_params=_cparams(3))(
            jnp.reshape(core, (1,)).astype(jnp.int32), w, g_own, g_recv, m, v)


def sum_slots(y, out_dtype, name, tb=256):
    n, r, c = y.shape
    tb = min(tb, r)

    def body(y_ref, o_ref):
        acc = y_ref[0].astype(f32)
        for i in range(1, n):
            acc = acc + y_ref[i].astype(f32)
        o_ref[...] = acc.astype(out_dtype)

    return pl.pallas_call(
        body, name=name, grid=(r // tb,),
        in_specs=[pl.BlockSpec((n, tb, c), lambda i: (0, i, 0))], out_specs=pl.BlockSpec((tb, c), lambda i: (i, 0)),
        out_shape=_sds((r, c), out_dtype), compiler_params=_cparams(1))(y)


def adamw(w, g, m, v, name, tb=256):
    r, c = w.shape
    tb = min(tb, r)

    def body(w_ref, g_ref, m_ref, v_ref, d_ref, mo_ref, vo_ref):
        gv = g_ref[...]
        m2 = B1 * m_ref[...] + (1.0 - B1) * gv
        v2 = B2 * v_ref[...] + (1.0 - B2) * (gv * gv)
        m_hat = m2 / (1.0 - B1 ** STEP)
        v_hat = v2 / (1.0 - B2 ** STEP)
        d_ref[...] = -LR * (m_hat / (jnp.sqrt(v_hat) + AEPS) + WD * w_ref[...])
        mo_ref[...] = m2
        vo_ref[...] = v2

    spec = pl.BlockSpec((tb, c), lambda i: (i, 0))
    return pl.pallas_call(
        body, name=name, grid=(r // tb,), in_specs=[spec] * 4, out_specs=[spec] * 3,
        out_shape=[_sds((r, c), f32)] * 3, compiler_params=_cparams(1))(w, g, m, v)


def adamw_leading(w, g, m, v, name, tc=64):
    c, l, r = w.shape
    main = c // tc
    tail = c - main * tc

    def body(w_ref, g_ref, m_ref, v_ref, *rest):
        d_ref, mo_ref, vo_ref = rest[-3:]
        gv = g_ref[...]
        m2 = B1 * m_ref[...] + (1.0 - B1) * gv
        v2 = B2 * v_ref[...] + (1.0 - B2) * (gv * gv)
        m_hat = m2 / (1.0 - B1 ** STEP)
        v_hat = v2 / (1.0 - B2 ** STEP)
        d_ref[...] = -LR * (m_hat / (jnp.sqrt(v_hat) + AEPS) + WD * w_ref[...])
        mo_ref[...] = m2
        vo_ref[...] = v2

    spec = pl.BlockSpec((tc, l, r), lambda i: (i, 0, 0))
    outs = pl.pallas_call(
        functools.partial(body), name=name, grid=(main,), in_specs=[spec] * 4, out_specs=[spec] * 3,
        out_shape=[_sds(w.shape, f32)] * 3, compiler_params=_cparams(1))(w, g, m, v)
    if tail:
        assert (main * tc) % tail == 0
        last = pl.BlockSpec((tail, l, r), lambda i: (main * tc // tail, 0, 0))
        outs = pl.pallas_call(
            functools.partial(body), name=name + "_tail", grid=(1,),
            in_specs=[last] * 4 + [pl.BlockSpec(memory_space=pl.ANY)] * 3, out_specs=[last] * 3,
            out_shape=[_sds(w.shape, f32)] * 3, input_output_aliases={4: 0, 5: 1, 6: 2},
            compiler_params=_cparams(1))(w, g, m, v, *outs)
    return outs


SMALL_ROW = 1024
SMALL_GAINS = ("g1", "ga", "gs", "g2", "g3", "g4")
SMALL_LAYER_ROWS = 8 + 8 + 16 + 8


def _pack_small(grads):
    wide = lambda a: jnp.pad(a, ((0, 0), (0, 2 * SMALL_ROW - a.shape[1]))).reshape(-1, SMALL_ROW)
    row = lax.broadcasted_iota(jnp.int32, (8, SMALL_ROW), 0)
    parts = []
    for g in grads:
        singles = [g[k] for k in SMALL_GAINS] + [g["bs"][:, :SMALL_ROW],
                                                 jnp.pad(g["bs"][:, SMALL_ROW:], ((0, 0), (0, 2 * SMALL_ROW - XBC)))]
        first = sum(jnp.where(row == k, s, 0.0) for k, s in enumerate(singles))
        parts += [first, g["wa"], wide(g["ws"]), jnp.pad(g["par"], ((0, 0), (0, SMALL_ROW - CH)))]
    return jnp.concatenate(parts, axis=0)


def _unpack_small(packed, depth):
    rows = packed.reshape(depth, SMALL_LAYER_ROWS, SMALL_ROW)
    out = {k: rows[:, i] for i, k in enumerate(SMALL_GAINS)}
    out["bs"] = rows[:, 6:8].reshape(depth, 2 * SMALL_ROW)[:, :XBC]
    out["wa"] = rows[:, 8:11]
    out["ws"] = rows[:, 16:32].reshape(depth, 8, 2 * SMALL_ROW)[:, :4, :XBC]
    out["par"] = rows[:, 32:35, :CH]
    return out


def kernel(x, norm_mix_pre, w_in, conv_a_w, ssm_conv_w, ssm_conv_b, dt_bias, a_log, d_skip, conv_out_norm, ssm_out_norm, w_out, norm_mix_post, norm_mlp_pre, w_up, w_down, norm_mlp_post, loss_target, m_norm_mix_pre, m_w_in, m_conv_a_w, m_ssm_conv_w, m_ssm_conv_b, m_dt_bias, m_a_log, m_d_skip, m_conv_out_norm, m_ssm_out_norm, m_w_out, m_norm_mix_post, m_norm_mlp_pre, m_w_up, m_w_down, m_norm_mlp_post, v_norm_mix_pre, v_w_in, v_conv_a_w, v_ssm_conv_w, v_ssm_conv_b, v_dt_bias, v_a_log, v_d_skip, v_conv_out_norm, v_ssm_out_norm, v_w_out, v_norm_mix_post, v_norm_mlp_pre, v_w_up, v_w_down, v_norm_mlp_post):
    nb, seq, _ = x.shape
    t = nb * seq
    depth = w_in.shape[0]
    ncol = w_in.shape[2]
    chip = 2 * lax.axis_index("x") + lax.axis_index("y")

    taps = [conv_a_w, ssm_conv_w]
    taps_g = all_gather(taps, "chips", "gather_taps", slot_axis=1, own=False)
    wa_g, ws_g = [lax.dynamic_update_index_in_dim(g, s, chip, 1) for g, s in zip(taps_g, taps)]
    wa_full = jnp.transpose(wa_g, (0, 2, 1, 3)).reshape(depth, 3, D)
    ws_full = jnp.transpose(ws_g, (0, 2, 1, 3)).reshape(depth, 4, XBC)
    lane_pad = lambda a: jnp.pad(a, ((0, 0), (0, CH - a.shape[1])))
    par = jnp.stack([lane_pad(dt_bias), lane_pad(a_log), lane_pad(d_skip)], axis=1)
    par = jnp.pad(par, ((0, 0), (0, 5), (0, 0)))

    layer_shards = lambda l: [w_in[l].astype(bf16), w_out[l].astype(bf16), w_up[l].astype(bf16), w_down[l].astype(bf16)]
    issued = []

    def start(shards, name):
        zones = [lax.empty((N_CHIPS,) + s.shape, s.dtype) for s in shards]
        issued.append(chips_start(shards, zones, _weight_views(shards), name,
                                  after=issued[-1]["token"] if issued else taps_g[0]))
        return issued[-1]

    def finish(started, after, name):
        shards, zones = chips_wait(started, after, name)
        zones = weights_share(zones, "weights_share")
        return [lax.dynamic_update_index_in_dim(z, s, chip, 0) for z, s in zip(zones, shards)]

    def shaped(mats):
        wo_z, wu_z, wd_z = mats
        return wo_z.reshape(2 * D, D), wu_z, wd_z.reshape(DFF, D)

    first = layer_shards(0)
    travelling = {0: start(first[:1], "weights_start_0")}
    rest = start(first[1:], "weights_start_0_rest")
    for l in range(1, depth):
        travelling[l] = start(layer_shards(l), f"weights_start_{l}")

    def weights_of(l, x_in):
        mats = finish(travelling.pop(l), x_in, f"weights_wait_{l}")
        w = dict(win=assemble_columns(mats[0], PROJ, "assemble_w_in"), wa=jnp.pad(wa_full[l], ((0, 5), (0, 0))),
                 ws=jnp.pad(ws_full[l], ((0, 4), (0, 0))), bs=ssm_conv_b[l][None], par=par[l],
                 g1=norm_mix_pre[l][None], ga=conv_out_norm[l][None], gs=ssm_out_norm[l][None],
                 g2=norm_mix_post[l][None], g3=norm_mlp_pre[l][None], g4=norm_mlp_post[l][None])
        if l == 0:
            w["token"] = issued[-1]["token"]
            w["late"] = lambda after: dict(zip(("wo", "wu", "wd"), shaped(finish(rest, after, "weights_wait_0_rest"))))
        else:
            w.update(zip(("wo", "wu", "wd"), shaped(mats[1:])))
        return w

    core = lax.axis_index("c")
    grads_travelling = {}

    chip_major = dict(win=lambda a: a[None], wo=lambda a: a.reshape(N_CHIPS, 2 * D // N_CHIPS, D), wu=lambda a: a,
                      wd=lambda a: a.reshape(N_CHIPS, DFF // N_CHIPS, D))
    held = {}

    def grads_done(l, g, last):
        if l > 0 and not last:
            held[l] = g
            return None
        g = {**held.pop(l, {}), **g}
        keys = [k for k in ("win", "wo", "wu", "wd") if k in g]
        mats = [chip_major[k](g[k]) for k in keys]
        received = pair_send_halves(mats, "grads_to_pair")
        sums = [sum_pair_half(m_, r_, core, "pair_sum", by_chip=(N_CHIPS, ncol) if k == "win" else None)
                for k, m_, r_ in zip(keys, mats, received)]
        zones = [lax.empty(s.shape, s.dtype) for s in sums]
        started = chips_start(sums, zones, _grad_views(sums), f"grads_start_{l}_{len(grads_travelling)}")
        grads_travelling[(l, keys[0])] = (keys, started)
        return started["token"]

    sse, dx, grads = local_step(x.reshape(t, D), loss_target.reshape(t, D), depth, weights_of, seq, grads_done)
    loss = lax.psum(0.5 / D * sse[0, 0], ("x", "y", "c"))

    small_all = all_gather([_pack_small(grads)], "all", "gather_small")[0]
    small_sum = sum_slots(small_all, f32, "small_sum", tb=8)
    small = _unpack_small(small_sum, depth)

    big_w = dict(win=w_in, wo=w_out, wu=w_up, wd=w_down)
    acc = {k: lax.empty((depth, bw.shape[1] // 2, bw.shape[2]), f32) for k, bw in big_w.items()}
    for n, ((l, _), (keys, started)) in enumerate(grads_travelling.items()):
        sums, zones = chips_wait(started, small_sum, f"grads_wait_{l}_{n}")
        for k, s, z in zip(keys, sums, zones):
            acc[k] = chip_sum_into(acc[k], l, s, z, chip, "chip_sum")
    acc = [acc[k] for k in ("win", "wo", "wu", "wd")]
    from_sibling = _exchange(acc, [a.shape for a in acc], "pair", lambda r, a, i: r, lambda r, a, i: r,
                             lambda a: acc[a].shape, "grads_from_pair", False)

    wa_cols, ws_cols = conv_a_w.shape[2], ssm_conv_w.shape[2]
    par_g = small["par"].reshape(depth, 3, CH)
    g_small = dict(
        norm_mix_pre=small["g1"], conv_out_norm=small["ga"], ssm_out_norm=small["gs"], norm_mix_post=small["g2"],
        norm_mlp_pre=small["g3"], norm_mlp_post=small["g4"], ssm_conv_b=small["bs"],
        conv_a_w=lax.dynamic_slice_in_dim(small["wa"].reshape(depth, 3, D), chip * wa_cols, wa_cols, axis=2),
        ssm_conv_w=lax.dynamic_slice_in_dim(small["ws"].reshape(depth, 4, XBC), chip * ws_cols, ws_cols, axis=2),
        dt_bias=par_g[:, 0, :NH], a_log=par_g[:, 1, :NH], d_skip=par_g[:, 2, :NH])

    given = dict(norm_mix_pre=(norm_mix_pre, m_norm_mix_pre, v_norm_mix_pre), w_in=(w_in, m_w_in, v_w_in),
                 conv_a_w=(conv_a_w, m_conv_a_w, v_conv_a_w), ssm_conv_w=(ssm_conv_w, m_ssm_conv_w, v_ssm_conv_w),
                 ssm_conv_b=(ssm_conv_b, m_ssm_conv_b, v_ssm_conv_b), dt_bias=(dt_bias, m_dt_bias, v_dt_bias),
                 a_log=(a_log, m_a_log, v_a_log), d_skip=(d_skip, m_d_skip, v_d_skip),
                 conv_out_norm=(conv_out_norm, m_conv_out_norm, v_conv_out_norm),
                 ssm_out_norm=(ssm_out_norm, m_ssm_out_norm, v_ssm_out_norm), w_out=(w_out, m_w_out, v_w_out),
                 norm_mix_post=(norm_mix_post, m_norm_mix_post, v_norm_mix_post),
                 norm_mlp_pre=(norm_mlp_pre, m_norm_mlp_pre, v_norm_mlp_pre), w_up=(w_up, m_w_up, v_w_up),
                 w_down=(w_down, m_w_down, v_w_down), norm_mlp_post=(norm_mlp_post, m_norm_mlp_post, v_norm_mlp_post))
    halves = dict(zip(["w_in", "w_out", "w_up", "w_down"], zip(acc, from_sibling)))
    order = ["norm_mix_pre", "w_in", "conv_a_w", "ssm_conv_w", "ssm_conv_b", "dt_bias", "a_log", "d_skip",
             "conv_out_norm", "ssm_out_norm", "w_out", "norm_mix_post", "norm_mlp_pre", "w_up", "w_down",
             "norm_mlp_post"]
    g_out, d_out, m_out, v_out = [], [], [], []
    for n in order:
        wv, mv, vv = given[n]
        if n in halves and wv.shape[-1] % CH:
            own, recv = halves[n]
            gv = jnp.concatenate([jnp.where(core == 0, own, recv), jnp.where(core == 0, recv, own)], axis=1)
            to_cols, to_rows = (lambda a: jnp.transpose(a, (2, 0, 1))), (lambda a: jnp.transpose(a, (1, 2, 0)))
            dlt, m2, v2 = [to_rows(o) for o in adamw_leading(to_cols(wv), to_cols(gv), to_cols(mv), to_cols(vv),
                                                             "adamw_cols")]
        elif n in halves:
            gv, dlt, m2, v2 = adamw_halves(wv, *halves[n], mv, vv, core, "adamw_matrix")
        else:
            gv = g_small[n].reshape(wv.shape)
            two_d = lambda a: a.reshape(-1, a.shape[-1])
            dlt, m2, v2 = adamw(two_d(wv), two_d(gv), two_d(mv), two_d(vv), "adamw")
        g_out.append(gv)
        d_out.append(dlt.reshape(wv.shape))
        m_out.append(m2.reshape(wv.shape))
        v_out.append(v2.reshape(wv.shape))
    return (loss, dx.reshape(nb, seq, D), *g_out, *d_out, *m_out, *v_out)
```

```python
import functools

import jax
import jax.numpy as jnp
from jax import lax
from jax.experimental import pallas as pl
from jax.experimental.pallas import tpu as pltpu

f32, bf16 = jnp.float32, jnp.bfloat16

D = 1024
NH, HP = 16, 64
NG, NS = 2, 128
CH = 128
XBC = D + 2 * NG * NS
DFF = 4 * D
IN_COLS = 3 * D + D + XBC + NH
PROJ = 5760
COL_Z, COL_XBC, COL_DT = 3 * D, 4 * D, 4 * D + XBC
EPS = 1e-6
HALO = 8
HBLK = 16
VMEM_LIMIT = 56 * 2**20
MESH = pl.DeviceIdType.MESH

LR, B1, B2, AEPS, WD, STEP = 0.001, 0.9, 0.999, 1e-08, 0.01, 10


def _cparams(n_axes):
    return pltpu.CompilerParams(dimension_semantics=("arbitrary",) * n_axes, vmem_limit_bytes=VMEM_LIMIT)


def _sds(shape, dtype):
    return jax.ShapeDtypeStruct(tuple(shape), dtype)


def _token_spec(token):
    return [] if token is None else [pl.BlockSpec(memory_space=pl.ANY)]


def _token_arg(token):
    return [] if token is None else [token]


def _rms_fwd(x, g):
    r = lax.rsqrt(jnp.mean(x * x, axis=-1, keepdims=True) + EPS)
    return x * r * g


def _rms_bwd(x, g, dy):
    r = lax.rsqrt(jnp.mean(x * x, axis=-1, keepdims=True) + EPS)
    xh = x * r
    gdy = dy * g
    dx = r * (gdy - xh * jnp.mean(xh * gdy, axis=-1, keepdims=True))
    return dx, dy * xh


def _accum(ref, part, first):
    @pl.when(first)
    def _():
        ref[...] = part

    @pl.when(jnp.logical_not(first))
    def _():
        ref[...] += part


def _dot_nt(a, b):
    return lax.dot_general(a, b, (((1,), (1,)), ((), ())), preferred_element_type=f32)


def _dot_tn(a, b):
    return lax.dot_general(a, b, (((0,), (0,)), ((), ())), preferred_element_type=f32)


def _dot(a, b):
    return jnp.dot(a, b, preferred_element_type=f32)


def _split_dot(x, e_bf, n_split, nt=False):
    acc = None
    rem = x
    for s in range(n_split):
        hi = rem.astype(bf16)
        term = _dot_nt(hi, e_bf) if nt else _dot(hi, e_bf)
        acc = term if acc is None else acc + term
        if s + 1 < n_split:
            rem = rem - hi.astype(f32)
    return acc


def _sigmoid(x):
    return 0.5 * jnp.tanh(0.5 * x) + 0.5


def norm_matmul(x, g, w, tm, tn, out_dtype, name, token=None):
    t = x.shape[0]
    if w.ndim == 3:
        assert w.shape[2] == tn
        n = w.shape[0] * tn
        w_spec = pl.BlockSpec((None, D, tn), lambda i, j: (j, 0, 0))
    else:
        n = w.shape[1]
        w_spec = pl.BlockSpec((D, tn), lambda i, j: (0, j))

    def body(x_ref, g_ref, w_ref, *rest):
        o_ref, h_ref = rest[-2:]

        @pl.when(pl.program_id(1) == 0)
        def _():
            h_ref[...] = _rms_fwd(x_ref[...], g_ref[...]).astype(bf16)

        o_ref[...] = _dot(h_ref[...], w_ref[...]).astype(out_dtype)

    return pl.pallas_call(
        body, name=name, grid=(t // tm, n // tn),
        in_specs=[pl.BlockSpec((tm, D), lambda i, j: (i, 0)), pl.BlockSpec((1, D), lambda i, j: (0, 0)), w_spec]
        + _token_spec(token),
        out_specs=[pl.BlockSpec((tm, tn), lambda i, j: (i, j)), pl.BlockSpec((tm, D), lambda i, j: (i, 0))],
        out_shape=[_sds((t, n), out_dtype), _sds((t, D), bf16)],
        compiler_params=_cparams(2))(x, g, w, *_token_arg(token))


def matmul_postnorm(a, w, xres, g, tm, relu2, name):
    t, k = a.shape

    def body(a_ref, w_ref, xr_ref, g_ref, y_ref, xo_ref):
        av = a_ref[...]
        if relu2:
            af = jnp.maximum(av.astype(f32), 0.0)
            av = (af * af).astype(bf16)
        y = _dot(av, w_ref[...])
        y_ref[...] = y.astype(bf16)
        xo_ref[...] = xr_ref[...] + _rms_fwd(y, g_ref[...])

    return pl.pallas_call(
        body, name=name, grid=(t // tm,),
        in_specs=[pl.BlockSpec((tm, k), lambda i: (i, 0)), pl.BlockSpec((k, D), lambda i: (0, 0)),
                  pl.BlockSpec((tm, D), lambda i: (i, 0)), pl.BlockSpec((1, D), lambda i: (0, 0))],
        out_specs=[pl.BlockSpec((tm, D), lambda i: (i, 0)), pl.BlockSpec((tm, D), lambda i: (i, 0))],
        out_shape=[_sds((t, D), bf16), _sds((t, D), f32)],
        compiler_params=_cparams(1))(a, w, xres, g)


def postnorm_bwd_matmul(y, g, dxo, w, fp, tm, tn, out_dtype, name, token=None):
    t, n = y.shape[0], w.shape[0]
    relu = fp is not None

    def body(*refs):
        y_ref, g_ref, dxo_ref, w_ref = refs[:4]
        fp_ref = refs[4] if relu else None
        dy_ref, dg_ref, da_ref = refs[-3:]
        i, j = pl.program_id(0), pl.program_id(1)

        @pl.when(j == 0)
        def _():
            dx, dgc = _rms_bwd(y_ref[...].astype(f32), g_ref[...], dxo_ref[...])
            dy_ref[...] = dx.astype(bf16)
            _accum(dg_ref, jnp.sum(dgc, axis=0, keepdims=True), i == 0)

        da = _dot_nt(dy_ref[...], w_ref[...])
        if relu:
            da = da * (2.0 * jnp.maximum(fp_ref[...].astype(f32), 0.0))
        da_ref[...] = da.astype(out_dtype)

    in_specs = [pl.BlockSpec((tm, D), lambda i, j: (i, 0)), pl.BlockSpec((1, D), lambda i, j: (0, 0)),
                pl.BlockSpec((tm, D), lambda i, j: (i, 0)), pl.BlockSpec((tn, D), lambda i, j: (j, 0))]
    args = [y, g, dxo, w]
    if relu:
        in_specs.append(pl.BlockSpec((tm, tn), lambda i, j: (i, j)))
        args.append(fp)
    in_specs += _token_spec(token)
    args += _token_arg(token)
    return pl.pallas_call(
        body, name=name, grid=(t // tm, n // tn), in_specs=in_specs,
        out_specs=[pl.BlockSpec((tm, D), lambda i, j: (i, 0)), pl.BlockSpec((1, D), lambda i, j: (0, 0)),
                   pl.BlockSpec((tm, tn), lambda i, j: (i, j))],
        out_shape=[_sds((t, D), bf16), _sds((1, D), f32), _sds((t, n), out_dtype)],
        compiler_params=_cparams(2))(*args)


def matmul_prenorm_bwd(da, w, x, g, dxo, tm, name, token=None):
    t, k = da.shape
    blocked = w.ndim == 3

    def body(da_ref, w_ref, x_ref, g_ref, dxo_ref, *rest):
        dx_ref, dg_ref = rest[-2:]
        if blocked:
            kc = w.shape[2]
            dh = _dot_nt(da_ref[:, 0:kc], w_ref[0])
            for q in range(1, w.shape[0]):
                dh = dh + _dot_nt(da_ref[:, q * kc:(q + 1) * kc], w_ref[q])
        else:
            dh = _dot_nt(da_ref[...], w_ref[...])
        dxn, dgc = _rms_bwd(x_ref[...], g_ref[...], dh)
        dx_ref[...] = dxo_ref[...] + dxn
        _accum(dg_ref, jnp.sum(dgc, axis=0, keepdims=True), pl.program_id(0) == 0)

    w_spec = pl.BlockSpec(w.shape, (lambda i: (0, 0, 0)) if blocked else (lambda i: (0, 0)))
    return pl.pallas_call(
        body, name=name, grid=(t // tm,),
        in_specs=[pl.BlockSpec((tm, k), lambda i: (i, 0)), w_spec,
                  pl.BlockSpec((tm, D), lambda i: (i, 0)), pl.BlockSpec((1, D), lambda i: (0, 0)),
                  pl.BlockSpec((tm, D), lambda i: (i, 0))] + _token_spec(token),
        out_specs=[pl.BlockSpec((tm, D), lambda i: (i, 0)), pl.BlockSpec((1, D), lambda i: (0, 0))],
        out_shape=[_sds((t, D), f32), _sds((1, D), f32)],
        compiler_params=_cparams(1))(da, w, x, g, dxo, *_token_arg(token))


def _resident(shape):
    return pl.BlockSpec(shape, lambda i: (0,) * len(shape), pipeline_mode=pl.Buffered(1))


def mlp_fwd(x, g_pre, wu, wd, g_post, tm, name):
    t = x.shape[0]
    nq, _, fc = wu.shape

    def body(x_ref, gp_ref, wu_ref, wd_ref, gq_ref, fp_ref, h_ref, o_ref, xo_ref):
        xv = x_ref[...]
        h = _rms_fwd(xv, gp_ref[...]).astype(bf16)
        h_ref[...] = h
        o = None
        for q in range(nq):
            fq = _dot(h, wu_ref[q])
            fp_ref[:, q * fc:(q + 1) * fc] = fq.astype(bf16)
            r = jnp.maximum(fq, 0.0)
            part = _dot((r * r).astype(bf16), wd_ref[q * fc:(q + 1) * fc, :])
            o = part if o is None else o + part
        o_ref[...] = o.astype(bf16)
        xo_ref[...] = xv + _rms_fwd(o, gq_ref[...])

    row = lambda c: pl.BlockSpec((tm, c), lambda i: (i, 0))
    vec = pl.BlockSpec((1, D), lambda i: (0, 0))
    return pl.pallas_call(
        body, name=name, grid=(t // tm,),
        in_specs=[row(D), vec, _resident(wu.shape), _resident(wd.shape), vec],
        out_specs=[row(nq * fc), row(D), row(D), row(D)],
        out_shape=[_sds((t, nq * fc), bf16), _sds((t, D), bf16), _sds((t, D), bf16), _sds((t, D), f32)],
        compiler_params=_cparams(1))(x, g_pre, wu, wd, g_post)


def mlp_bwd(o, g_post, dxo, wd, fp, wu, x, g_pre, tm, name):
    t = x.shape[0]
    nq, _, fc = wu.shape

    def body(o_ref, gq_ref, dxo_ref, wd_ref, fp_ref, wu_ref, x_ref, gp_ref, do_ref, dfp_ref, dx_ref, dgq_ref, dgp_ref):
        i = pl.program_id(0)
        dxo_v = dxo_ref[...]
        do, dgq = _rms_bwd(o_ref[...].astype(f32), gq_ref[...], dxo_v)
        do_b = do.astype(bf16)
        do_ref[...] = do_b
        dh = None
        for q in range(nq):
            cols = slice(q * fc, (q + 1) * fc)
            dq = _dot_nt(do_b, wd_ref[cols, :]) * (2.0 * jnp.maximum(fp_ref[:, cols].astype(f32), 0.0))
            dq_b = dq.astype(bf16)
            dfp_ref[:, cols] = dq_b
            part = _dot_nt(dq_b, wu_ref[q])
            dh = part if dh is None else dh + part
        dxn, dgp = _rms_bwd(x_ref[...], gp_ref[...], dh)
        dx_ref[...] = dxo_v + dxn
        _accum(dgq_ref, jnp.sum(dgq, axis=0, keepdims=True), i == 0)
        _accum(dgp_ref, jnp.sum(dgp, axis=0, keepdims=True), i == 0)

    row = lambda c: pl.BlockSpec((tm, c), lambda i: (i, 0))
    vec = pl.BlockSpec((1, D), lambda i: (0, 0))
    return pl.pallas_call(
        body, name=name, grid=(t // tm,),
        in_specs=[row(D), vec, row(D), _resident(wd.shape), row(nq * fc), _resident(wu.shape), row(D), vec],
        out_specs=[row(D), row(nq * fc), row(D), vec, vec],
        out_shape=[_sds((t, D), bf16), _sds((t, nq * fc), bf16), _sds((t, D), f32), _sds((1, D), f32), _sds((1, D), f32)],
        compiler_params=_cparams(1))(o, g_post, dxo, wd, fp, wu, x, g_pre)


def matmul_tn(a, b, tm, tn, relu2, name, col_blocks=False):
    t, m = a.shape
    n = b.shape[1]
    if col_blocks:
        out_spec, out_shape = pl.BlockSpec((None, tm, tn), lambda i, j: (j, i, 0)), _sds((n // tn, m, tn), bf16)
    else:
        out_spec, out_shape = pl.BlockSpec((tm, tn), lambda i, j: (i, j)), _sds((m, n), bf16)

    def body(a_ref, b_ref, o_ref, at_ref):
        @pl.when(pl.program_id(1) == 0)
        def _():
            av = a_ref[...]
            if relu2:
                af = jnp.maximum(av.astype(f32), 0.0)
                av = (af * af).astype(bf16)
            at_ref[...] = av.T

        o_ref[...] = _dot(at_ref[...], b_ref[...]).astype(bf16)

    return pl.pallas_call(
        body, name=name, grid=(m // tm, n // tn),
        in_specs=[pl.BlockSpec((t, tm), lambda i, j: (0, i)), pl.BlockSpec((t, tn), lambda i, j: (0, j))],
        out_specs=out_spec, out_shape=out_shape,
        scratch_shapes=[pltpu.VMEM((tm, t), bf16)],
        compiler_params=_cparams(2))(a, b)


ROWS_A = 16
ROWS_B = 32
UNROLL = 4


def _past(win, s):
    return (win if s == 0 else pltpu.roll(win, s, 0))[HALO:]


def _future(win, s):
    n = win.shape[0]
    return (win if s == 0 else pltpu.roll(win, n - s, 0))[:n - HALO]


def _fold8(v):
    return v.reshape(v.shape[0] // 8, 8, v.shape[1]).sum(axis=0)


def _last8(ref):
    return ref[...].astype(f32)[HBLK - HALO:]


def _first8(ref):
    return ref[...].astype(f32)[:HALO]


def _rd(ref, rows):
    return ref[rows, :].astype(f32)


def _halo_prev(tb, col):
    return lambda i: (jnp.maximum(i * (tb // HBLK) - 1, 0), col)


def _halo_next(tb, col, t):
    return lambda i: (jnp.minimum((i + 1) * (tb // HBLK), t // HBLK - 1), col)


def group_a_fwd(proj, wa, g, seq, tb, name):
    t = proj.shape[0]
    bps = seq // tb

    def body(xa_ref, ca_ref, ba_ref, xah_ref, cah_ref, wa_ref, g_ref, o_ref, u_scr):
        first = (pl.program_id(0) % bps) == 0
        u_scr[0:HALO, :] = jnp.where(first, 0.0, _last8(cah_ref) * _last8(xah_ref))
        w, gv = wa_ref[...], g_ref[...]

        def chunk(i, carry):
            r = pl.multiple_of(i * ROWS_A, ROWS_A)
            rows = pl.ds(r, ROWS_A)
            u_scr[pl.ds(pl.multiple_of(HALO + r, HALO), ROWS_A), :] = _rd(ca_ref, rows) * _rd(xa_ref, rows)
            win = u_scr[pl.ds(r, ROWS_A + HALO), :]
            cv = w[2:3] * _past(win, 0) + w[1:2] * _past(win, 1) + w[0:1] * _past(win, 2)
            o_ref[rows, :] = _rms_fwd(_rd(ba_ref, rows) * cv, gv).astype(bf16)
            return carry

        lax.fori_loop(0, tb // ROWS_A, chunk, 0, unroll=UNROLL)

    blk = lambda c: pl.BlockSpec((tb, D), lambda i: (i, c))
    return pl.pallas_call(
        body, name=name, grid=(t // tb,),
        in_specs=[blk(0), blk(1), blk(2),
                  pl.BlockSpec((HBLK, D), _halo_prev(tb, 0)), pl.BlockSpec((HBLK, D), _halo_prev(tb, 1)),
                  pl.BlockSpec((8, D), lambda i: (0, 0)), pl.BlockSpec((1, D), lambda i: (0, 0))],
        out_specs=pl.BlockSpec((tb, D), lambda i: (i, 0)),
        out_shape=_sds((t, 2 * D), bf16),
        scratch_shapes=[pltpu.VMEM((tb + HALO, D), f32)],
        compiler_params=_cparams(1))(proj, proj, proj, proj, proj, wa, g)


def group_a_bwd(proj, dcat, wa, g, seq, tb, name, token=None):
    t = proj.shape[0]
    bps = seq // tb

    def body(xa_ref, ca_ref, ba_ref, dy_ref, xap_ref, cap_ref, xan_ref, can_ref, ban_ref, dyn_ref, wa_ref, g_ref,
             *rest):
        dp_ref, dwa_ref, dg_ref, u_scr, d_scr, acc_scr = rest[-6:]
        i = pl.program_id(0)
        first = (i % bps) == 0
        last = (i % bps) == bps - 1
        w = wa_ref[...]
        gv = g_ref[...]
        u_scr[0:HALO, :] = jnp.where(first, 0.0, _last8(cap_ref) * _last8(xap_ref))
        u_scr[HALO + tb:2 * HALO + tb, :] = _first8(can_ref) * _first8(xan_ref)
        acc_scr[...] = jnp.zeros_like(acc_scr)

        def forward_part(n, carry):
            r = pl.multiple_of(n * ROWS_A, ROWS_A)
            rows = pl.ds(r, ROWS_A)
            ba = _rd(ba_ref, rows)
            u_scr[pl.ds(pl.multiple_of(HALO + r, HALO), ROWS_A), :] = _rd(ca_ref, rows) * _rd(xa_ref, rows)
            win = u_scr[pl.ds(r, ROWS_A + HALO), :]
            u = [_past(win, s) for s in range(3)]
            cv = w[2:3] * u[0] + w[1:2] * u[1] + w[0:1] * u[2]
            dya, dgc = _rms_bwd(ba * cv, gv, _rd(dy_ref, rows))
            dcv = dya * ba
            d_scr[rows, :] = dcv
            dp_ref[rows, 2 * D:3 * D] = (dya * cv).astype(bf16)
            acc_scr[0:8, :] += _fold8(dgc)
            for k in range(3):
                acc_scr[8 + 8 * k:16 + 8 * k, :] += _fold8(dcv * u[2 - k])
            return carry

        lax.fori_loop(0, tb // ROWS_A, forward_part, 0, unroll=UNROLL)

        start = HALO + tb
        cvn = (w[2:3] * u_scr[pl.ds(start, HALO), :] + w[1:2] * u_scr[pl.ds(start - 1, HALO), :]
               + w[0:1] * u_scr[pl.ds(start - 2, HALO), :])
        ban = _first8(ban_ref)
        dyan, _ = _rms_bwd(ban * cvn, gv, _first8(dyn_ref))
        d_scr[tb:tb + HALO, :] = jnp.where(last, 0.0, dyan * ban)

        def backward_part(n, carry):
            r = pl.multiple_of(n * ROWS_A, ROWS_A)
            rows = pl.ds(r, ROWS_A)
            win = d_scr[pl.ds(r, ROWS_A + HALO), :]
            du = w[2:3] * _future(win, 0) + w[1:2] * _future(win, 1) + w[0:1] * _future(win, 2)
            dp_ref[rows, 0:D] = (du * _rd(ca_ref, rows)).astype(bf16)
            dp_ref[rows, D:2 * D] = (du * _rd(xa_ref, rows)).astype(bf16)
            return carry

        lax.fori_loop(0, tb // ROWS_A, backward_part, 0, unroll=UNROLL)

        row = lax.broadcasted_iota(jnp.int32, (8, D), 0)
        dw = jnp.zeros((8, D), f32)
        for k in range(3):
            dw = jnp.where(row == k, jnp.sum(acc_scr[8 + 8 * k:16 + 8 * k, :], axis=0, keepdims=True), dw)
        _accum(dwa_ref, dw, i == 0)
        _accum(dg_ref, jnp.sum(acc_scr[0:8, :], axis=0, keepdims=True), i == 0)

    blk = lambda c: pl.BlockSpec((tb, D), lambda i: (i, c))
    prv = lambda c: pl.BlockSpec((HBLK, D), _halo_prev(tb, c))
    nxt = lambda c: pl.BlockSpec((HBLK, D), _halo_next(tb, c, t))
    return pl.pallas_call(
        body, name=name, grid=(t // tb,),
        in_specs=[blk(0), blk(1), blk(2), blk(0), prv(0), prv(1), nxt(0), nxt(1), nxt(2), nxt(0),
                  pl.BlockSpec((8, D), lambda i: (0, 0)), pl.BlockSpec((1, D), lambda i: (0, 0))] + _token_spec(token),
        out_specs=[pl.BlockSpec((tb, 3 * D), lambda i: (i, 0)), pl.BlockSpec((8, D), lambda i: (0, 0)),
                   pl.BlockSpec((1, D), lambda i: (0, 0))],
        out_shape=[_sds((t, PROJ), bf16), _sds((8, D), f32), _sds((1, D), f32)],
        scratch_shapes=[pltpu.VMEM((tb + 2 * HALO, D), f32), pltpu.VMEM((tb + HALO, D), f32), pltpu.VMEM((32, D), f32)],
        compiler_params=_cparams(1))(proj, proj, proj, dcat, proj, proj, proj, proj, proj, dcat, wa, g,
                                     *_token_arg(token))


CB = 512
XBC_BLK0 = COL_XBC // CB


def conv_b_fwd(proj, ws, bs, seq, tb, name):
    t = proj.shape[0]
    bps = seq // tb

    def body(x_ref, xp_ref, w_ref, b_ref, o_ref, x_scr):
        first = (pl.program_id(1) % bps) == 0
        x_scr[0:HALO, :] = jnp.where(first, 0.0, _last8(xp_ref))
        w, bias = w_ref[...], b_ref[...]

        def chunk(n, carry):
            r = pl.multiple_of(n * ROWS_B, ROWS_B)
            rows = pl.ds(r, ROWS_B)
            x_scr[pl.ds(pl.multiple_of(HALO + r, HALO), ROWS_B), :] = _rd(x_ref, rows)
            win = x_scr[pl.ds(r, ROWS_B + HALO), :]
            xc = bias + w[3:4] * _past(win, 0)
            for k in range(3):
                xc = xc + w[k:k + 1] * _past(win, 3 - k)
            o_ref[rows, :] = xc * _sigmoid(xc)
            return carry

        lax.fori_loop(0, tb // ROWS_B, chunk, 0, unroll=UNROLL)

    return pl.pallas_call(
        body, name=name, grid=(XBC // CB, t // tb),
        in_specs=[pl.BlockSpec((tb, CB), lambda j, i: (i, XBC_BLK0 + j)),
                  pl.BlockSpec((HBLK, CB), lambda j, i: (jnp.maximum(i * (tb // HBLK) - 1, 0), XBC_BLK0 + j)),
                  pl.BlockSpec((8, CB), lambda j, i: (0, j)), pl.BlockSpec((1, CB), lambda j, i: (0, j))],
        out_specs=pl.BlockSpec((tb, CB), lambda j, i: (i, j)),
        out_shape=_sds((t, XBC), f32),
        scratch_shapes=[pltpu.VMEM((tb + HALO, CB), f32)],
        compiler_params=_cparams(2))(proj, proj, ws, bs)


def conv_b_bwd(proj, dxs, ws, bs, dproj, seq, tb, name):
    t = proj.shape[0]
    bps = seq // tb

    def body(x_ref, xp_ref, xn_ref, d_ref, dn_ref, w_ref, b_ref, dproj_ref, dx_ref, dw_ref, db_ref, x_scr, d_scr,
             acc_scr):
        i = pl.program_id(1)
        first = (i % bps) == 0
        last = (i % bps) == bps - 1
        w = w_ref[...]
        bias = b_ref[...]
        x_scr[0:HALO, :] = jnp.where(first, 0.0, _last8(xp_ref))
        x_scr[HALO + tb:2 * HALO + tb, :] = _first8(xn_ref)
        acc_scr[...] = jnp.zeros_like(acc_scr)

        def dsilu(xc, d):
            sg = _sigmoid(xc)
            return d * (sg * (1.0 + xc * (1.0 - sg)))

        def forward_part(n, carry):
            r = pl.multiple_of(n * ROWS_B, ROWS_B)
            rows = pl.ds(r, ROWS_B)
            x_scr[pl.ds(pl.multiple_of(HALO + r, HALO), ROWS_B), :] = _rd(x_ref, rows)
            win = x_scr[pl.ds(r, ROWS_B + HALO), :]
            xs = [_past(win, s) for s in range(4)]
            xc = bias + w[3:4] * xs[0]
            for k in range(3):
                xc = xc + w[k:k + 1] * xs[3 - k]
            dxc = dsilu(xc, _rd(d_ref, rows))
            d_scr[rows, :] = dxc
            acc_scr[0:8, :] += _fold8(dxc)
            for k in range(4):
                acc_scr[8 + 8 * k:16 + 8 * k, :] += _fold8(dxc * xs[3 - k])
            return carry

        lax.fori_loop(0, tb // ROWS_B, forward_part, 0, unroll=UNROLL)

        start = HALO + tb
        xcn = bias + w[3:4] * x_scr[pl.ds(start, HALO), :]
        for k in range(3):
            xcn = xcn + w[k:k + 1] * x_scr[pl.ds(start - 3 + k, HALO), :]
        d_scr[tb:tb + HALO, :] = jnp.where(last, 0.0, dsilu(xcn, _first8(dn_ref)))

        def backward_part(n, carry):
            r = pl.multiple_of(n * ROWS_B, ROWS_B)
            win = d_scr[pl.ds(r, ROWS_B + HALO), :]
            dx = w[3:4] * _future(win, 0)
            for k in range(3):
                dx = dx + w[k:k + 1] * _future(win, 3 - k)
            dx_ref[pl.ds(r, ROWS_B), :] = dx.astype(bf16)
            return carry

        lax.fori_loop(0, tb // ROWS_B, backward_part, 0, unroll=UNROLL)

        row = lax.broadcasted_iota(jnp.int32, (8, CB), 0)
        dw = jnp.zeros((8, CB), f32)
        for k in range(4):
            dw = jnp.where(row == k, jnp.sum(acc_scr[8 + 8 * k:16 + 8 * k, :], axis=0, keepdims=True), dw)
        _accum(dw_ref, dw, i == 0)
        _accum(db_ref, jnp.sum(acc_scr[0:8, :], axis=0, keepdims=True), i == 0)

    nh = t // HBLK
    return pl.pallas_call(
        body, name=name, grid=(XBC // CB, t // tb),
        in_specs=[pl.BlockSpec((tb, CB), lambda j, i: (i, XBC_BLK0 + j)),
                  pl.BlockSpec((HBLK, CB), lambda j, i: (jnp.maximum(i * (tb // HBLK) - 1, 0), XBC_BLK0 + j)),
                  pl.BlockSpec((HBLK, CB), lambda j, i: (jnp.minimum((i + 1) * (tb // HBLK), nh - 1), XBC_BLK0 + j)),
                  pl.BlockSpec((tb, CB), lambda j, i: (i, j)),
                  pl.BlockSpec((HBLK, CB), lambda j, i: (jnp.minimum((i + 1) * (tb // HBLK), nh - 1), j)),
                  pl.BlockSpec((8, CB), lambda j, i: (0, j)), pl.BlockSpec((1, CB), lambda j, i: (0, j)),
                  pl.BlockSpec(memory_space=pl.ANY)],
        out_specs=[pl.BlockSpec((tb, CB), lambda j, i: (i, XBC_BLK0 + j)), pl.BlockSpec((8, CB), lambda j, i: (0, j)),
                   pl.BlockSpec((1, CB), lambda j, i: (0, j))],
        out_shape=[_sds((t, PROJ), bf16), _sds((8, XBC), f32), _sds((1, XBC), f32)],
        input_output_aliases={7: 0},
        scratch_shapes=[pltpu.VMEM((tb + 2 * HALO, CB), f32), pltpu.VMEM((tb + HALO, CB), f32),
                        pltpu.VMEM((40, CB), f32)],
        compiler_params=_cparams(2))(proj, proj, proj, dxs, dxs, ws, bs, dproj)


def place_columns(buf, part, col_block, tb, name):
    t, wdt = part.shape

    def body(p_ref, buf_ref, o_ref):
        o_ref[...] = p_ref[...]

    return pl.pallas_call(
        body, name=name, grid=(t // tb,),
        in_specs=[pl.BlockSpec((tb, wdt), lambda i: (i, 0)), pl.BlockSpec(memory_space=pl.ANY)],
        out_specs=pl.BlockSpec((tb, wdt), lambda i: (i, col_block)), out_shape=_sds(buf.shape, buf.dtype),
        input_output_aliases={1: 0}, compiler_params=_cparams(1))(part, buf)


GW = D // NG
EXPAND_TERMS = 2
REDUCE_TERMS = 1


def _ssd_consts():
    head_of_lane = jnp.arange(D) // HP
    expand = (jnp.arange(CH)[:, None] == head_of_lane[None, :]).astype(bf16)
    tri = (jnp.arange(CH)[:, None] >= jnp.arange(CH)[None, :]).astype(f32)
    return expand, tri


def _ssd_common(par_ref, dtr_ref, e_ref, tri_ref):
    par = par_ref[...]
    dtb, alog, dsk = par[0:1], par[1:2], par[2:3]
    lane = lax.broadcasted_iota(jnp.int32, (CH, CH), 1)
    a = -jnp.exp(alog)
    dtr = dtr_ref[...].astype(f32) + dtb
    sp = jnp.maximum(dtr, 0.0) + jnp.log(1.0 + jnp.exp(-jnp.abs(dtr)))
    dt = jnp.where(lane < NH, sp, 0.0)
    cs = jnp.dot(tri_ref[...], dt * a, precision=lax.Precision.HIGHEST, preferred_element_type=f32)
    cs_last = cs[CH - 1:CH, :]
    dte = jnp.exp(cs_last - cs)
    ecs = jnp.exp(cs)
    ecl = jnp.exp(cs_last)
    e = e_ref[...]
    row8 = lax.broadcasted_iota(jnp.int32, (8, CH), 0)
    r8 = _split_dot(jnp.where(row8 == 0, ecl, jnp.where(row8 == 1, dsk, 0.0)), e, 3)
    return dict(a=a, dtr=dtr, dt=dt, cs=cs, cst=cs.T, dte=dte, ecs=ecs, ecl=ecl, e=e, lane=lane,
                dt_x=_split_dot(dt, e, EXPAND_TERMS), dte_x=_split_dot(dte, e, EXPAND_TERMS),
                ecs_x=_split_dot(ecs, e, EXPAND_TERMS),
                ecl_x=r8[0:1], dsk_x=r8[1:2])


def _decay_matrix(c, h):
    li = lax.broadcasted_iota(jnp.int32, (CH, CH), 0)
    seg = c["cs"][:, h:h + 1] - c["cst"][h:h + 1, :]
    return jnp.exp(jnp.where(li >= c["lane"], seg, -jnp.inf))


def _gate_norm_fwd(y, z, gs):
    zg = z * _sigmoid(z)
    yg = y * zg
    return jnp.concatenate([_rms_fwd(yg[:, k * GW:(k + 1) * GW], gs[:, k * GW:(k + 1) * GW]) for k in range(NG)], axis=1)


def ssd_fwd(xbcs, proj, par, gs, cat, seq, name):
    t = xbcs.shape[0]
    nc = seq // CH
    expand, tri = _ssd_consts()

    def body(xs_ref, b_ref, c_ref, dtr_ref, z_ref, par_ref, e_ref, tri_ref, gs_ref, cat_ref, yn_ref, y_ref, st_ref,
             p_scr, yd_scr):
        @pl.when(pl.program_id(0) % nc == 0)
        def _():
            p_scr[...] = jnp.zeros_like(p_scr)

        c = _ssd_common(par_ref, dtr_ref, e_ref, tri_ref)
        xs = xs_ref[...]
        xdt = xs * c["dt_x"]
        xdt_b = xdt.astype(bf16)
        xdte_b = (xdt * c["dte_x"]).astype(bf16)
        p = p_scr[...]
        st_ref[0] = p
        p_b = p.astype(bf16)
        lo = c["lane"] < HP
        for g in range(NG):
            bg = b_ref[:, g * NS:(g + 1) * NS].astype(bf16)
            cg = c_ref[:, g * NS:(g + 1) * NS].astype(bf16)
            gmat = _dot_nt(cg, bg)
            for q in range(GW // CH):
                col = g * GW + q * CH
                xp = xdt_b[:, col:col + CH]
                h0 = col // HP
                m0 = (gmat * _decay_matrix(c, h0)).astype(bf16)
                m1 = (gmat * _decay_matrix(c, h0 + 1)).astype(bf16)
                stacked = jnp.concatenate([jnp.where(lo, xp, jnp.zeros_like(xp)),
                                           jnp.where(lo, jnp.zeros_like(xp), xp)], axis=0)
                yd_scr[:, col:col + CH] = _dot(jnp.concatenate([m0, m1], axis=1), stacked)
            gsl = slice(g * GW, (g + 1) * GW)
            yoff = _dot(cg, p_b[:, gsl]) * c["ecs_x"][:, gsl]
            yd_scr[:, gsl] = yd_scr[:, gsl] + yoff
            p_scr[:, gsl] = p[:, gsl] * c["ecl_x"][:, gsl] + _dot_tn(bg, xdte_b[:, gsl])
        y = yd_scr[...] + c["dsk_x"] * xs
        y_ref[...] = y
        yn_ref[...] = _gate_norm_fwd(y, z_ref[...].astype(f32), gs_ref[...]).astype(bf16)

    nb = t // CH
    return pl.pallas_call(
        body, name=name, grid=(nb,),
        in_specs=[pl.BlockSpec((CH, D), lambda i: (i, 0)),
                  pl.BlockSpec((CH, NG * NS), lambda i: (i, D // (NG * NS))),
                  pl.BlockSpec((CH, NG * NS), lambda i: (i, D // (NG * NS) + 1)),
                  pl.BlockSpec((CH, CH), lambda i: (i, COL_DT // CH)),
                  pl.BlockSpec((CH, D), lambda i: (i, COL_Z // D)),
                  pl.BlockSpec((8, CH), lambda i: (0, 0)), pl.BlockSpec((CH, D), lambda i: (0, 0)),
                  pl.BlockSpec((CH, CH), lambda i: (0, 0)), pl.BlockSpec((1, D), lambda i: (0, 0)),
                  pl.BlockSpec(memory_space=pl.ANY)],
        out_specs=[pl.BlockSpec((CH, D), lambda i: (i, 1)), pl.BlockSpec((CH, D), lambda i: (i, 0)),
                   pl.BlockSpec((1, NS, D), lambda i: (i, 0, 0))],
        out_shape=[_sds((t, 2 * D), bf16), _sds((t, D), f32), _sds((nb, NS, D), f32)],
        input_output_aliases={9: 0},
        scratch_shapes=[pltpu.VMEM((NS, D), f32), pltpu.VMEM((CH, D), f32)],
        compiler_params=_cparams(1))(xbcs, xbcs, xbcs, proj, proj, par, expand, tri, gs, cat)


def ssd_bwd(xbcs, proj, ypre, states, dcat, par, gs, dproj, seq, name):
    t = xbcs.shape[0]
    nc = seq // CH
    expand, tri = _ssd_consts()

    def body(xs_ref, b_ref, c_ref, dtr_ref, z_ref, y_ref, st_ref, dyn_ref, par_ref, e_ref, tri_ref, gs_ref, dproj_ref,
             dx_ref, dz_ref, ddt_ref, dpar_ref, dgs_ref, dp_scr, dxdt_scr):
        i = pl.program_id(0)

        @pl.when(i % nc == 0)
        def _():
            dp_scr[...] = jnp.zeros_like(dp_scr)

        c = _ssd_common(par_ref, dtr_ref, e_ref, tri_ref)
        e = c["e"]
        lane = c["lane"]
        sub = lax.broadcasted_iota(jnp.int32, (CH, CH), 0)
        xs = xs_ref[...]
        xdt = xs * c["dt_x"]
        xdt_b = xdt.astype(bf16)
        xdte_b = (xdt * c["dte_x"]).astype(bf16)
        p = st_ref[0]
        p_b = p.astype(bf16)
        dpn = dp_scr[...]
        dpn_b = dpn.astype(bf16)

        y, z, gs_v = y_ref[...], z_ref[...].astype(f32), gs_ref[...]
        zs = _sigmoid(z)
        zg = z * zs
        yg = y * zg
        parts, gparts = [], []
        for k in range(NG):
            sl = slice(k * GW, (k + 1) * GW)
            dxk, dgk = _rms_bwd(yg[:, sl], gs_v[:, sl], dyn_ref[:, sl].astype(f32))
            parts.append(dxk)
            gparts.append(dgk)
        dyg = jnp.concatenate(parts, axis=1)
        dgs_rows = jnp.concatenate(gparts, axis=1)
        dy = dyg * zg
        dz_ref[...] = (dyg * y * (zs * (1.0 + z * (1.0 - zs)))).astype(bf16)
        dy_b = dy.astype(bf16)
        dq_b = (dy * c["ecs_x"]).astype(bf16)

        lo = lane < HP
        dcs = jnp.zeros((CH, CH), f32)
        dcst = jnp.zeros((CH, CH), f32)
        for g in range(NG):
            gsl = slice(g * GW, (g + 1) * GW)
            bg = b_ref[:, g * NS:(g + 1) * NS].astype(bf16)
            cg = c_ref[:, g * NS:(g + 1) * NS].astype(bf16)
            gmat = _dot_nt(cg, bg)
            dgm = jnp.zeros((CH, CH), f32)
            for q in range(GW // CH):
                col = g * GW + q * CH
                xp = xdt_b[:, col:col + CH]
                dyp = dy_b[:, col:col + CH]
                zero = jnp.zeros_like(dyp)
                xp2 = jnp.concatenate([jnp.where(lo, xp, zero), jnp.where(lo, zero, xp)], axis=0)
                dy2 = jnp.concatenate([jnp.where(lo, dyp, zero), jnp.where(lo, zero, dyp)], axis=0)
                dm2 = _dot_nt(dyp, xp2)
                ms = []
                for hh in range(2):
                    h = col // HP + hh
                    dec = _decay_matrix(c, h)
                    m = gmat * dec
                    dm = dm2[:, hh * CH:(hh + 1) * CH]
                    dseg = dm * m
                    dcs = dcs + jnp.where(lane == h, jnp.sum(dseg, axis=1, keepdims=True), 0.0)
                    dcst = dcst + jnp.where(sub == h, jnp.sum(dseg, axis=0, keepdims=True), 0.0)
                    dgm = dgm + dm * dec
                    ms.append(m.astype(bf16))
                dxdt_scr[:, col:col + CH] = _dot_tn(jnp.concatenate(ms, axis=0), dy2)
            dgm_b = dgm.astype(bf16)
            bds = _dot(bg, dpn_b[:, gsl])
            dxdt_scr[:, gsl] = dxdt_scr[:, gsl] + c["dte_x"][:, gsl] * bds
            dc_g = _dot(dgm_b, bg) + _dot_nt(dq_b[:, gsl], p_b[:, gsl])
            db_g = _dot_tn(dgm_b, cg) + _dot_nt(xdte_b[:, gsl], dpn_b[:, gsl])
            dx_ref[:, D + g * NS:D + (g + 1) * NS] = db_g
            dx_ref[:, D + NG * NS + g * NS:D + NG * NS + (g + 1) * NS] = dc_g
            dp_scr[:, gsl] = dpn[:, gsl] * c["ecl_x"][:, gsl] + _dot_tn(cg, dq_b[:, gsl])
            q_g = _dot(cg, p_b[:, gsl])
            e_g = e[:, gsl]
            dcs = dcs + c["ecs"] * _split_dot(dy[:, gsl] * q_g, e_g, REDUCE_TERMS, nt=True)
            ddte = _split_dot(xdt[:, gsl] * bds, e_g, REDUCE_TERMS, nt=True) * c["dte"]
            dcs = dcs - ddte
            dcs = dcs + jnp.where(sub == CH - 1, jnp.sum(ddte, axis=0, keepdims=True), 0.0)

        decl = _split_dot(jnp.broadcast_to(jnp.sum(dpn * p, axis=0, keepdims=True), (8, D)), e, 2, nt=True)[0:1]
        dcs = dcs + jnp.where(sub == CH - 1, c["ecl"] * decl, 0.0)
        dcs = dcs - dcst.T
        dadt = lax.dot_general(tri_ref[...], dcs, (((0,), (0,)), ((), ())), precision=lax.Precision.HIGHEST,
                               preferred_element_type=f32)
        dxdt = dxdt_scr[...]
        ddt = dadt * c["a"] + _split_dot(dxdt * xs, e, REDUCE_TERMS, nt=True)
        ddtr = jnp.where(lane < NH, ddt * _sigmoid(c["dtr"]), 0.0)
        ddt_ref[...] = ddtr.astype(bf16)
        dx_ref[:, 0:D] = dxdt * c["dt_x"] + c["dsk_x"] * dy
        dsk = _split_dot(jnp.broadcast_to(jnp.sum(dy * xs, axis=0, keepdims=True), (8, D)), e, 2, nt=True)[0:1]
        dalog = jnp.sum(dadt * c["dt"], axis=0, keepdims=True) * c["a"]
        row8 = lax.broadcasted_iota(jnp.int32, (8, CH), 0)
        dpar = jnp.where(row8 == 0, jnp.sum(ddtr, axis=0, keepdims=True),
                         jnp.where(row8 == 1, dalog, jnp.where(row8 == 2, dsk, 0.0)))
        dpar = jnp.where(lax.broadcasted_iota(jnp.int32, (8, CH), 1) < NH, dpar, 0.0)
        _accum(dpar_ref, dpar, i == 0)
        _accum(dgs_ref, jnp.sum(dgs_rows, axis=0, keepdims=True), i == 0)

    nb = t // CH
    rev = lambda i: (i // nc) * nc + (nc - 1 - i % nc)
    return pl.pallas_call(
        body, name=name, grid=(nb,),
        in_specs=[pl.BlockSpec((CH, D), lambda i: (rev(i), 0)),
                  pl.BlockSpec((CH, NG * NS), lambda i: (rev(i), D // (NG * NS))),
                  pl.BlockSpec((CH, NG * NS), lambda i: (rev(i), D // (NG * NS) + 1)),
                  pl.BlockSpec((CH, CH), lambda i: (rev(i), COL_DT // CH)),
                  pl.BlockSpec((CH, D), lambda i: (rev(i), COL_Z // D)),
                  pl.BlockSpec((CH, D), lambda i: (rev(i), 0)),
                  pl.BlockSpec((1, NS, D), lambda i: (rev(i), 0, 0)),
                  pl.BlockSpec((CH, D), lambda i: (rev(i), 1)),
                  pl.BlockSpec((8, CH), lambda i: (0, 0)), pl.BlockSpec((CH, D), lambda i: (0, 0)),
                  pl.BlockSpec((CH, CH), lambda i: (0, 0)), pl.BlockSpec((1, D), lambda i: (0, 0)),
                  pl.BlockSpec(memory_space=pl.ANY)],
        out_specs=[pl.BlockSpec((CH, XBC), lambda i: (rev(i), 0)), pl.BlockSpec((CH, D), lambda i: (rev(i), COL_Z // D)),
                   pl.BlockSpec((CH, CH), lambda i: (rev(i), 0)),
                   pl.BlockSpec((8, CH), lambda i: (0, 0)), pl.BlockSpec((1, D), lambda i: (0, 0))],
        out_shape=[_sds((t, XBC), f32), _sds((t, PROJ), bf16), _sds((t, CH), bf16), _sds((8, CH), f32), _sds((1, D), f32)],
        input_output_aliases={12: 1},
        scratch_shapes=[pltpu.VMEM((NS, D), f32), pltpu.VMEM((CH, D), f32)],
        compiler_params=_cparams(1))(xbcs, xbcs, xbcs, proj, proj, ypre, states, dcat, par, expand, tri, gs, dproj)


def loss_head(y, target, tb, name):
    t = y.shape[0]

    def body(y_ref, t_ref, s_ref, dy_ref):
        err = y_ref[...] - t_ref[...]
        dy_ref[...] = err * (1.0 / D)
        _accum(s_ref, jnp.zeros((8, CH), f32) + jnp.sum(err * err), pl.program_id(0) == 0)

    return pl.pallas_call(
        body, name=name, grid=(t // tb,),
        in_specs=[pl.BlockSpec((tb, D), lambda i: (i, 0)), pl.BlockSpec((tb, D), lambda i: (i, 0))],
        out_specs=[pl.BlockSpec((8, CH), lambda i: (0, 0)), pl.BlockSpec((tb, D), lambda i: (i, 0))],
        out_shape=[_sds((8, CH), f32), _sds((t, D), f32)],
        compiler_params=_cparams(1))(y, target)


def _tiles(t, seq):
    tm = min(512, t)
    return dict(tm=tm, tm_small=min(256, t), tm_large=min(1024, t), tm_huge=min(2048, t), tb=min(512, seq))


def local_step(x, target, depth, weights_of, seq, grads_done=None):
    t = x.shape[0]
    ts = _tiles(t, seq)
    tm, tl, th, tb = ts["tm"], ts["tm_large"], ts["tm_huge"], ts["tb"]
    saved, ws = [], []
    for l in range(depth):
        w = weights_of(l, x)
        ws.append(w)
        proj, h1 = norm_matmul(x, w["g1"], w["win"], th, 1152, bf16, "in_proj", token=w.get("token"))
        cat = group_a_fwd(proj, w["wa"], w["ga"], seq, tb, "group_a_fwd")
        xbcs = conv_b_fwd(proj, w["ws"], w["bs"], seq, tb, "conv_b_fwd")
        cat, ypre, states = ssd_fwd(xbcs, proj, w["par"], w["gs"], cat, seq, "ssd_fwd")
        if "late" in w:
            w.update(w.pop("late")(cat))
        mix, x2 = matmul_postnorm(cat, w["wo"], x, w["g2"], tl, False, "out_proj")
        fp, h2, o, x3 = mlp_fwd(x2, w["g3"], w["wu"], w["wd"], w["g4"], tm, "mlp_fwd")
        saved.append(dict(x=x, proj=proj, h1=h1, xbcs=xbcs, ypre=ypre, states=states, cat=cat, mix=mix, x2=x2,
                          fp=fp, h2=h2, o=o))
        x = x3
    sse, dx = loss_head(x, target, tm, "loss_head")
    grads = [None] * depth
    for l in reversed(range(depth)):
        s, w = saved[l], ws[l]
        do, dfp, dx2, dg4, dg3 = mlp_bwd(s["o"], w["g4"], dx, w["wd"], s["fp"], w["wu"], s["x2"], w["g3"],
                                         ts["tm_small"], "mlp_bwd")
        dwd = matmul_tn(s["fp"], do, 512, 1024, True, "mlp_down_dw")
        dwu = matmul_tn(s["h2"], dfp, 512, 1024, False, "mlp_up_dw", col_blocks=True)
        dmix, dg2, dcat = postnorm_bwd_matmul(s["mix"], w["g2"], dx2, w["wo"], None, tl, 1024, bf16, "out_proj_bwd")
        dwo = matmul_tn(s["cat"], dmix, 512, 1024, False, "out_proj_dw")
        token = None if grads_done is None else grads_done(l, dict(wo=dwo, wu=dwu, wd=dwd), False)
        dproj, dwa, dga = group_a_bwd(s["proj"], dcat, w["wa"], w["ga"], seq, tb, "group_a_bwd", token=token)
        dxbcs, dproj, ddt, dpar, dgs = ssd_bwd(s["xbcs"], s["proj"], s["ypre"], s["states"], dcat, w["par"], w["gs"],
                                               dproj, seq, "ssd_bwd")
        dproj, dws, dbs = conv_b_bwd(s["proj"], dxbcs, w["ws"], w["bs"], dproj, seq, tb, "conv_b_bwd")
        dproj = place_columns(dproj, ddt, COL_DT // CH, tm, "place_ddt")
        dwin = matmul_tn(s["h1"], dproj, 512, 1152, False, "in_proj_dw")
        token = None if grads_done is None else grads_done(l, dict(win=dwin), True)
        dx, dg1 = matmul_prenorm_bwd(dproj, w["win"], s["x"], w["g1"], dx2, ts["tm_small"], "in_proj_bwd", token=token)
        grads[l] = dict(win=dwin, wo=dwo, wu=dwu, wd=dwd, wa=dwa, ws=dws, bs=dbs, par=dpar,
                        g1=dg1, ga=dga, gs=dgs, g2=dg2, g3=dg3, g4=dg4)
    return sse, dx, grads


GROUPS = {
    "chips": [(1, 0, 0), (0, 1, 0), (1, 1, 0)],
    "pair": [(0, 0, 1)],
    "all": [(1, 0, 0), (0, 1, 0), (1, 1, 0), (0, 0, 1), (1, 0, 1), (0, 1, 1), (1, 1, 1)],
}


def _group_index(group, x, y, c):
    return {"chips": 2 * x + y, "pair": c, "all": 4 * x + 2 * y + c}[group]


def _chunk_indices(shape, pieces):
    if len(shape) < 3:
        return [()]
    lead = [()]
    for n in shape[:-2]:
        lead = [i + (k,) for i in lead for k in range(n)]
    rows = shape[-2]
    split = max(1, pieces // len(lead))
    while split > 1 and (rows % split or (rows // split) % 16):
        split -= 1
    step = rows // split
    return [i + (pl.ds(s * step, step),) for i in lead for s in range(split)]


def _exchange(arrays, out_shapes, group, src_view, dst_view, view_shape, name, own, pieces=16):
    masks = GROUPS[group]
    na, nm = len(arrays), len(masks)
    cuts = [_chunk_indices(view_shape(a), pieces) for a in range(na)]

    def body(*refs):
        ins, outs = refs[:na], refs[na:2 * na]
        send_sems, recv_sems = refs[2 * na:2 * na + 2]
        local_sems = refs[2 * na + 2] if own else None
        x, y, c = lax.axis_index("x"), lax.axis_index("y"), lax.axis_index("c")
        me = _group_index(group, x, y, c)
        peers = []
        for mx, my, mc in masks:
            px, py, pc = (1 - x if mx else x), (1 - y if my else y), (1 - c if mc else c)
            peers.append(((px, py, pc), _group_index(group, px, py, pc)))

        def part(ref, idx):
            return ref.at[idx] if idx else ref

        if own:
            for a in range(na):
                for idx in cuts[a]:
                    pltpu.make_async_copy(part(src_view(ins[a], a, me), idx), part(dst_view(outs[a], a, me), idx),
                                          local_sems.at[a]).start()
        for a in range(na):
            for j, (dev, pidx) in enumerate(peers):
                for idx in cuts[a]:
                    pltpu.make_async_remote_copy(
                        src_ref=part(src_view(ins[a], a, pidx), idx), dst_ref=part(dst_view(outs[a], a, me), idx),
                        send_sem=send_sems.at[a * nm + j], recv_sem=recv_sems.at[a * nm + j],
                        device_id=dev, device_id_type=MESH).start()
        whole = []
        for a in range(na):
            for j, (dev, pidx) in enumerate(peers):
                whole.append(pltpu.make_async_remote_copy(
                    src_ref=src_view(ins[a], a, pidx), dst_ref=dst_view(outs[a], a, pidx),
                    send_sem=send_sems.at[a * nm + j], recv_sem=recv_sems.at[a * nm + j],
                    device_id=dev, device_id_type=MESH))
        for cp in whole:
            cp.wait_recv()
        for cp in whole:
            cp.wait_send()
        if own:
            for a in range(na):
                pltpu.make_async_copy(src_view(ins[a], a, me), dst_view(outs[a], a, me), local_sems.at[a]).wait()

    hbm = pl.BlockSpec(memory_space=pltpu.HBM)
    sems = [pltpu.SemaphoreType.DMA((na * nm,)), pltpu.SemaphoreType.DMA((na * nm,))]
    return pl.pallas_call(
        body, name=name, in_specs=[hbm] * na, out_specs=[hbm] * na,
        out_shape=[_sds(s, a.dtype) for s, a in zip(out_shapes, arrays)],
        scratch_shapes=sems + ([pltpu.SemaphoreType.DMA((na,))] if own else []))(*arrays)


def all_gather(arrays, group, name, slot_axis=0, own=True):
    n = len(GROUPS[group]) + 1
    shapes = [a.shape[:slot_axis] + (n,) + a.shape[slot_axis:] for a in arrays]
    lead = (slice(None),) * slot_axis
    return _exchange(arrays, shapes, group, lambda r, a, i: r, lambda r, a, i: r.at[lead + (i,)],
                     lambda a: arrays[a].shape, name, own)


HBM_SPEC = pl.BlockSpec(memory_space=pltpu.HBM)
SEM_SPEC = pl.BlockSpec(memory_space=pltpu.SEMAPHORE)
DATAFLOW = pltpu.SideEffectType.DATAFLOW_SIDE_EFFECTING
N_CHIPS = 4


def _chip_peers(x, y, c):
    out = []
    for mx, my, _ in GROUPS["chips"]:
        px, py = (1 - x if mx else x), (1 - y if my else y)
        out.append(((px, py, c), 2 * px + py))
    return out


def _weight_views(shards):
    half = [s.shape[0] // 2 for s in shards]
    return dict(src=lambda ref, a, c, to_chip: ref.at[pl.ds(c * half[a], half[a])],
                dst=lambda ref, a, c, from_chip: ref.at[from_chip, pl.ds(c * half[a], half[a])],
                rows=lambda a: half[a])


def _grad_views(sums):
    return dict(src=lambda ref, a, c, to_chip: ref.at[to_chip], dst=lambda ref, a, c, from_chip: ref.at[from_chip],
                rows=lambda a: sums[a].shape[1])


def chips_start(sources, zones, views, name, pieces=4, after=None):
    na, nm = len(sources), N_CHIPS - 1

    def body(*refs):
        ins, lands = refs[:na], refs[na:2 * na]
        n_in = 2 * na + len(_token_arg(after))
        send_sems, recv_sems, token = refs[n_in], refs[n_in + 1], refs[-1]
        x, y, c = lax.axis_index("x"), lax.axis_index("y"), lax.axis_index("c")
        chip = 2 * x + y
        for a in range(na):
            step = views["rows"](a) // pieces
            for j, (dev, to_chip) in enumerate(_chip_peers(x, y, c)):
                for q in range(pieces):
                    rows = pl.ds(q * step, step)
                    pltpu.make_async_remote_copy(
                        src_ref=views["src"](ins[a], a, c, to_chip).at[rows],
                        dst_ref=views["dst"](lands[a], a, c, chip).at[rows],
                        send_sem=send_sems.at[a * nm + j], recv_sem=recv_sems.at[a * nm + j],
                        device_id=dev, device_id_type=MESH).start()
        token[...] = jnp.zeros_like(token)

    both = list(sources) + list(zones)
    outs = pl.pallas_call(
        body, name=name,
        out_shape=(pltpu.SemaphoreType.DMA((na * nm,)), pltpu.SemaphoreType.DMA((na * nm,)),
                   *[pltpu.HBM(b.shape, b.dtype) for b in both], _sds((8, CH), f32)),
        in_specs=[HBM_SPEC] * (2 * na) + _token_spec(after),
        out_specs=(SEM_SPEC, SEM_SPEC, *[HBM_SPEC] * (2 * na), pl.BlockSpec(memory_space=pltpu.VMEM)),
        input_output_aliases={i: 2 + i for i in range(2 * na)},
        compiler_params=pltpu.CompilerParams(has_side_effects=DATAFLOW))(
            *[pltpu.with_memory_space_constraint(b, pltpu.HBM) for b in both], *_token_arg(after))
    return dict(send=outs[0], recv=outs[1], sources=list(outs[2:2 + na]), zones=list(outs[2 + na:2 + 2 * na]),
                token=outs[-1], views=views)


def chips_wait(started, after, name):
    sources, zones, views = started["sources"], started["zones"], started["views"]
    na, nm = len(sources), N_CHIPS - 1

    def body(*refs):
        ins, lands = refs[:na], refs[na:2 * na]
        send_sems, recv_sems = refs[2 * na], refs[2 * na + 1]
        x, y, c = lax.axis_index("x"), lax.axis_index("y"), lax.axis_index("c")
        for a in range(na):
            for j, (dev, peer_chip) in enumerate(_chip_peers(x, y, c)):
                cp = pltpu.make_async_remote_copy(
                    src_ref=views["src"](ins[a], a, c, peer_chip), dst_ref=views["dst"](lands[a], a, c, peer_chip),
                    send_sem=send_sems.at[a * nm + j], recv_sem=recv_sems.at[a * nm + j],
                    device_id=dev, device_id_type=MESH)
                cp.wait_send()
                cp.wait_recv()

    both = list(sources) + list(zones)
    outs = pl.pallas_call(
        body, name=name, out_shape=tuple(pltpu.HBM(b.shape, b.dtype) for b in both),
        in_specs=[HBM_SPEC] * (2 * na) + [SEM_SPEC, SEM_SPEC, pl.BlockSpec(memory_space=pl.ANY)],
        out_specs=tuple([HBM_SPEC] * (2 * na)), input_output_aliases={i: i for i in range(2 * na)},
        compiler_params=pltpu.CompilerParams(has_side_effects=DATAFLOW))(*both, started["send"], started["recv"], after)
    return list(outs[:na]), list(outs[na:])


def weights_share(zones, name):
    na, nm = len(zones), N_CHIPS - 1

    def body(*refs):
        lands = refs[na:2 * na]
        send_sems, recv_sems = refs[2 * na:]
        x, y, c = lax.axis_index("x"), lax.axis_index("y"), lax.axis_index("c")
        chip = 2 * x + y
        sibling = (x, y, 1 - c)
        sends = []
        for a in range(na):
            half = zones[a].shape[1] // 2
            for m in range(1, N_CHIPS):
                mine = lands[a].at[chip ^ m, pl.ds(c * half, half)]
                sends.append(pltpu.make_async_remote_copy(
                    src_ref=mine, dst_ref=mine, send_sem=send_sems.at[a * nm + m - 1],
                    recv_sem=recv_sems.at[a * nm + m - 1], device_id=sibling, device_id_type=MESH))
        for cp in sends:
            cp.start()
        for a in range(na):
            half = zones[a].shape[1] // 2
            for m in range(1, N_CHIPS):
                theirs = lands[a].at[chip ^ m, pl.ds((1 - c) * half, half)]
                pltpu.make_async_remote_copy(
                    src_ref=theirs, dst_ref=theirs, send_sem=send_sems.at[a * nm + m - 1],
                    recv_sem=recv_sems.at[a * nm + m - 1], device_id=sibling, device_id_type=MESH).wait_recv()
        for cp in sends:
            cp.wait_send()

    return pl.pallas_call(
        body, name=name, in_specs=[HBM_SPEC] * na, out_specs=[HBM_SPEC] * na,
        out_shape=[_sds(z.shape, z.dtype) for z in zones], input_output_aliases={i: i for i in range(na)},
        scratch_shapes=[pltpu.SemaphoreType.DMA((na * nm,)), pltpu.SemaphoreType.DMA((na * nm,))])(*zones)


def pair_send_halves(grads, name):
    half = [g.shape[1] // 2 for g in grads]
    shapes = [(g.shape[0], h, g.shape[2]) for g, h in zip(grads, half)]
    return _exchange(grads, shapes, "pair", lambda r, a, i: r.at[:, pl.ds(i * half[a], half[a])],
                     lambda r, a, i: r, lambda a: shapes[a], name, False)


def sum_pair_half(g, recv, core, name, tb=256, by_chip=None):
    nk, r, c = g.shape
    tb = min(tb, r // 2)
    nb = r // 2 // tb

    def body(core_ref, g_ref, r_ref, o_ref):
        s = g_ref[...].astype(f32) + r_ref[...].astype(f32)
        if by_chip is None:
            o_ref[...] = s.astype(bf16)
        else:
            for k in range(by_chip[0]):
                o_ref[k] = s[:, k * by_chip[1]:(k + 1) * by_chip[1]].astype(bf16)

    if by_chip is None:
        out_spec = pl.BlockSpec((None, tb, c), lambda k, i, core_ref: (k, i, 0))
        out_shape = _sds((nk, r // 2, c), bf16)
    else:
        assert nk == 1
        out_spec = pl.BlockSpec((by_chip[0], tb, by_chip[1]), lambda k, i, core_ref: (0, i, 0))
        out_shape = _sds((by_chip[0], r // 2, by_chip[1]), bf16)
    return pl.pallas_call(
        body, name=name,
        grid_spec=pltpu.PrefetchScalarGridSpec(
            num_scalar_prefetch=1, grid=(nk, nb),
            in_specs=[pl.BlockSpec((None, tb, c), lambda k, i, core_ref: (k, core_ref[0] * nb + i, 0)),
                      pl.BlockSpec((None, tb, c), lambda k, i, core_ref: (k, i, 0))],
            out_specs=out_spec),
        out_shape=out_shape, compiler_params=_cparams(2))(jnp.reshape(core, (1,)).astype(jnp.int32), g, recv)


def assemble_columns(blocks, width, name, tb=256):
    n, r, c = blocks.shape

    def body(b_ref, o_ref):
        for k in range(n):
            o_ref[:, k * c:(k + 1) * c] = b_ref[k]
        o_ref[:, n * c:] = jnp.zeros((tb, width - n * c), blocks.dtype)

    return pl.pallas_call(
        body, name=name, grid=(r // tb,), in_specs=[pl.BlockSpec((n, tb, c), lambda i: (0, i, 0))],
        out_specs=pl.BlockSpec((tb, width), lambda i: (i, 0)), out_shape=_sds((r, width), blocks.dtype),
        compiler_params=_cparams(1))(blocks)


def chip_sum_into(acc, layer, own, others, chip, name, tb=256):
    n, r, c = own.shape
    tb = min(tb, r)

    def body(chip_ref, x_ref, y1_ref, y2_ref, y3_ref, acc_ref, o_ref):
        o_ref[...] = ((x_ref[...].astype(f32) + y1_ref[...].astype(f32)) + y2_ref[...].astype(f32)) + y3_ref[...].astype(f32)

    def slot(k):
        return pl.BlockSpec((None, tb, c), lambda i, chip_ref: (chip_ref[0] ^ k, i, 0))

    return pl.pallas_call(
        body, name=name,
        grid_spec=pltpu.PrefetchScalarGridSpec(
            num_scalar_prefetch=1, grid=(r // tb,),
            in_specs=[slot(k) for k in range(n)] + [pl.BlockSpec(memory_space=pl.ANY)],
            out_specs=pl.BlockSpec((None, tb, c), lambda i, chip_ref: (layer, i, 0))),
        out_shape=_sds(acc.shape, f32), input_output_aliases={n + 1: 0}, compiler_params=_cparams(1))(
            jnp.reshape(chip, (1,)).astype(jnp.int32), own, *([others] * (n - 1)), acc)


def adamw_halves(w, g_own, g_recv, m, v, core, name, tb=256):
    depth, r, c = w.shape
    tb = min(tb, r // 2)
    nb = r // 2 // tb

    def body(core_ref, w_ref, go_ref, gr_ref, m_ref, v_ref, g_ref, d_ref, mo_ref, vo_ref):
        gv = jnp.where(pl.program_id(1) == core_ref[0], go_ref[...], gr_ref[...])
        m2 = B1 * m_ref[...] + (1.0 - B1) * gv
        v2 = B2 * v_ref[...] + (1.0 - B2) * (gv * gv)
        m_hat = m2 / (1.0 - B1 ** STEP)
        v_hat = v2 / (1.0 - B2 ** STEP)
        g_ref[...] = gv
        d_ref[...] = -LR * (m_hat / (jnp.sqrt(v_hat) + AEPS) + WD * w_ref[...])
        mo_ref[...] = m2
        vo_ref[...] = v2

    whole = pl.BlockSpec((None, tb, c), lambda l, h, i, core_ref: (l, h * nb + i, 0))
    own = pl.BlockSpec((None, tb, c), lambda l, h, i, core_ref: (l, jnp.where(h == core_ref[0], i, 0), 0))
    other = pl.BlockSpec((None, tb, c), lambda l, h, i, core_ref: (l, jnp.where(h == core_ref[0], 0, i), 0))
    return pl.pallas_call(
        body, name=name,
        grid_spec=pltpu.PrefetchScalarGridSpec(num_scalar_prefetch=1, grid=(depth, 2, nb),
                                               in_specs=[whole, own, other, whole, whole], out_specs=[whole] * 4),
        out_shape=[_sds(w.shape, f32)] * 4, compiler_params=_cparams(3))(
            jnp.reshape(core, (1,)).astype(jnp.int32), w, g_own, g_recv, m, v)


def sum_slots(y, out_dtype, name, tb=256):
    n, r, c = y.shape
    tb = min(tb, r)

    def body(y_ref, o_ref):
        acc = y_ref[0].astype(f32)
        for i in range(1, n):
            acc = acc + y_ref[i].astype(f32)
        o_ref[...] = acc.astype(out_dtype)

    return pl.pallas_call(
        body, name=name, grid=(r // tb,),
        in_specs=[pl.BlockSpec((n, tb, c), lambda i: (0, i, 0))], out_specs=pl.BlockSpec((tb, c), lambda i: (i, 0)),
        out_shape=_sds((r, c), out_dtype), compiler_params=_cparams(1))(y)


def adamw(w, g, m, v, name, tb=256):
    r, c = w.shape
    tb = min(tb, r)

    def body(w_ref, g_ref, m_ref, v_ref, d_ref, mo_ref, vo_ref):
        gv = g_ref[...]
        m2 = B1 * m_ref[...] + (1.0 - B1) * gv
        v2 = B2 * v_ref[...] + (1.0 - B2) * (gv * gv)
        m_hat = m2 / (1.0 - B1 ** STEP)
        v_hat = v2 / (1.0 - B2 ** STEP)
        d_ref[...] = -LR * (m_hat / (jnp.sqrt(v_hat) + AEPS) + WD * w_ref[...])
        mo_ref[...] = m2
        vo_ref[...] = v2

    spec = pl.BlockSpec((tb, c), lambda i: (i, 0))
    return pl.pallas_call(
        body, name=name, grid=(r // tb,), in_specs=[spec] * 4, out_specs=[spec] * 3,
        out_shape=[_sds((r, c), f32)] * 3, compiler_params=_cparams(1))(w, g, m, v)


def adamw_leading(w, g, m, v, name, tc=64):
    c, l, r = w.shape
    main = c // tc
    tail = c - main * tc

    def body(w_ref, g_ref, m_ref, v_ref, *rest):
        d_ref, mo_ref, vo_ref = rest[-3:]
        gv = g_ref[...]
        m2 = B1 * m_ref[...] + (1.0 - B1) * gv
        v2 = B2 * v_ref[...] + (1.0 - B2) * (gv * gv)
        m_hat = m2 / (1.0 - B1 ** STEP)
        v_hat = v2 / (1.0 - B2 ** STEP)
        d_ref[...] = -LR * (m_hat / (jnp.sqrt(v_hat) + AEPS) + WD * w_ref[...])
        mo_ref[...] = m2
        vo_ref[...] = v2

    spec = pl.BlockSpec((tc, l, r), lambda i: (i, 0, 0))
    outs = pl.pallas_call(
        functools.partial(body), name=name, grid=(main,), in_specs=[spec] * 4, out_specs=[spec] * 3,
        out_shape=[_sds(w.shape, f32)] * 3, compiler_params=_cparams(1))(w, g, m, v)
    if tail:
        assert (main * tc) % tail == 0
        last = pl.BlockSpec((tail, l, r), lambda i: (main * tc // tail, 0, 0))
        outs = pl.pallas_call(
            functools.partial(body), name=name + "_tail", grid=(1,),
            in_specs=[last] * 4 + [pl.BlockSpec(memory_space=pl.ANY)] * 3, out_specs=[last] * 3,
            out_shape=[_sds(w.shape, f32)] * 3, input_output_aliases={4: 0, 5: 1, 6: 2},
            compiler_params=_cparams(1))(w, g, m, v, *outs)
    return outs


SMALL_ROW = 1024
SMALL_GAINS = ("g1", "ga", "gs", "g2", "g3", "g4")
SMALL_LAYER_ROWS = 8 + 8 + 16 + 8


def _pack_small(grads):
    wide = lambda a: jnp.pad(a, ((0, 0), (0, 2 * SMALL_ROW - a.shape[1]))).reshape(-1, SMALL_ROW)
    row = lax.broadcasted_iota(jnp.int32, (8, SMALL_ROW), 0)
    parts = []
    for g in grads:
        singles = [g[k] for k in SMALL_GAINS] + [g["bs"][:, :SMALL_ROW],
                                                 jnp.pad(g["bs"][:, SMALL_ROW:], ((0, 0), (0, 2 * SMALL_ROW - XBC)))]
        first = sum(jnp.where(row == k, s, 0.0) for k, s in enumerate(singles))
        parts += [first, g["wa"], wide(g["ws"]), jnp.pad(g["par"], ((0, 0), (0, SMALL_ROW - CH)))]
    return jnp.concatenate(parts, axis=0)


def _unpack_small(packed, depth):
    rows = packed.reshape(depth, SMALL_LAYER_ROWS, SMALL_ROW)
    out = {k: rows[:, i] for i, k in enumerate(SMALL_GAINS)}
    out["bs"] = rows[:, 6:8].reshape(depth, 2 * SMALL_ROW)[:, :XBC]
    out["wa"] = rows[:, 8:11]
    out["ws"] = rows[:, 16:32].reshape(depth, 8, 2 * SMALL_ROW)[:, :4, :XBC]
    out["par"] = rows[:, 32:35, :CH]
    return out


def kernel(x, norm_mix_pre, w_in, conv_a_w, ssm_conv_w, ssm_conv_b, dt_bias, a_log, d_skip, conv_out_norm, ssm_out_norm, w_out, norm_mix_post, norm_mlp_pre, w_up, w_down, norm_mlp_post, loss_target, m_norm_mix_pre, m_w_in, m_conv_a_w, m_ssm_conv_w, m_ssm_conv_b, m_dt_bias, m_a_log, m_d_skip, m_conv_out_norm, m_ssm_out_norm, m_w_out, m_norm_mix_post, m_norm_mlp_pre, m_w_up, m_w_down, m_norm_mlp_post, v_norm_mix_pre, v_w_in, v_conv_a_w, v_ssm_conv_w, v_ssm_conv_b, v_dt_bias, v_a_log, v_d_skip, v_conv_out_norm, v_ssm_out_norm, v_w_out, v_norm_mix_post, v_norm_mlp_pre, v_w_up, v_w_down, v_norm_mlp_post):
    nb, seq, _ = x.shape
    t = nb * seq
    depth = w_in.shape[0]
    ncol = w_in.shape[2]
    chip = 2 * lax.axis_index("x") + lax.axis_index("y")

    taps = [conv_a_w, ssm_conv_w]
    taps_g = all_gather(taps, "chips", "gather_taps", slot_axis=1, own=False)
    wa_g, ws_g = [lax.dynamic_update_index_in_dim(g, s, chip, 1) for g, s in zip(taps_g, taps)]
    wa_full = jnp.transpose(wa_g, (0, 2, 1, 3)).reshape(depth, 3, D)
    ws_full = jnp.transpose(ws_g, (0, 2, 1, 3)).reshape(depth, 4, XBC)
    lane_pad = lambda a: jnp.pad(a, ((0, 0), (0, CH - a.shape[1])))
    par = jnp.stack([lane_pad(dt_bias), lane_pad(a_log), lane_pad(d_skip)], axis=1)
    par = jnp.pad(par, ((0, 0), (0, 5), (0, 0)))

    layer_shards = lambda l: [w_in[l].astype(bf16), w_out[l].astype(bf16), w_up[l].astype(bf16), w_down[l].astype(bf16)]
    issued = []

    def start(shards, name):
        zones = [lax.empty((N_CHIPS,) + s.shape, s.dtype) for s in shards]
        issued.append(chips_start(shards, zones, _weight_views(shards), name,
                                  after=issued[-1]["token"] if issued else taps_g[0]))
        return issued[-1]

    def finish(started, after, name):
        shards, zones = chips_wait(started, after, name)
        zones = weights_share(zones, "weights_share")
        return [lax.dynamic_update_index_in_dim(z, s, chip, 0) for z, s in zip(zones, shards)]

    def shaped(mats):
        wo_z, wu_z, wd_z = mats
        return wo_z.reshape(2 * D, D), wu_z, wd_z.reshape(DFF, D)

    first = layer_shards(0)
    travelling = {0: start(first[:1], "weights_start_0")}
    rest = start(first[1:], "weights_start_0_rest")
    for l in range(1, depth):
        travelling[l] = start(layer_shards(l), f"weights_start_{l}")

    def weights_of(l, x_in):
        mats = finish(travelling.pop(l), x_in, f"weights_wait_{l}")
        w = dict(win=assemble_columns(mats[0], PROJ, "assemble_w_in"), wa=jnp.pad(wa_full[l], ((0, 5), (0, 0))),
                 ws=jnp.pad(ws_full[l], ((0, 4), (0, 0))), bs=ssm_conv_b[l][None], par=par[l],
                 g1=norm_mix_pre[l][None], ga=conv_out_norm[l][None], gs=ssm_out_norm[l][None],
                 g2=norm_mix_post[l][None], g3=norm_mlp_pre[l][None], g4=norm_mlp_post[l][None])
        if l == 0:
            w["token"] = issued[-1]["token"]
            w["late"] = lambda after: dict(zip(("wo", "wu", "wd"), shaped(finish(rest, after, "weights_wait_0_rest"))))
        else:
            w.update(zip(("wo", "wu", "wd"), shaped(mats[1:])))
        return w

    core = lax.axis_index("c")
    grads_travelling = {}

    chip_major = dict(win=lambda a: a[None], wo=lambda a: a.reshape(N_CHIPS, 2 * D // N_CHIPS, D), wu=lambda a: a,
                      wd=lambda a: a.reshape(N_CHIPS, DFF // N_CHIPS, D))
    held = {}

    def grads_done(l, g, last):
        if l > 0 and not last:
            held[l] = g
            return None
        g = {**held.pop(l, {}), **g}
        keys = [k for k in ("win", "wo", "wu", "wd") if k in g]
        mats = [chip_major[k](g[k]) for k in keys]
        received = pair_send_halves(mats, "grads_to_pair")
        sums = [sum_pair_half(m_, r_, core, "pair_sum", by_chip=(N_CHIPS, ncol) if k == "win" else None)
                for k, m_, r_ in zip(keys, mats, received)]
        zones = [lax.empty(s.shape, s.dtype) for s in sums]
        started = chips_start(sums, zones, _grad_views(sums), f"grads_start_{l}_{len(grads_travelling)}")
        grads_travelling[(l, keys[0])] = (keys, started)
        return started["token"]

    sse, dx, grads = local_step(x.reshape(t, D), loss_target.reshape(t, D), depth, weights_of, seq, grads_done)
    loss = lax.psum(0.5 / D * sse[0, 0], ("x", "y", "c"))

    small_all = all_gather([_pack_small(grads)], "all", "gather_small")[0]
    small_sum = sum_slots(small_all, f32, "small_sum", tb=8)
    small = _unpack_small(small_sum, depth)

    big_w = dict(win=w_in, wo=w_out, wu=w_up, wd=w_down)
    acc = {k: lax.empty((depth, bw.shape[1] // 2, bw.shape[2]), f32) for k, bw in big_w.items()}
    for n, ((l, _), (keys, started)) in enumerate(grads_travelling.items()):
        sums, zones = chips_wait(started, small_sum, f"grads_wait_{l}_{n}")
        for k, s, z in zip(keys, sums, zones):
            acc[k] = chip_sum_into(acc[k], l, s, z, chip, "chip_sum")
    acc = [acc[k] for k in ("win", "wo", "wu", "wd")]
    from_sibling = _exchange(acc, [a.shape for a in acc], "pair", lambda r, a, i: r, lambda r, a, i: r,
                             lambda a: acc[a].shape, "grads_from_pair", False)

    wa_cols, ws_cols = conv_a_w.shape[2], ssm_conv_w.shape[2]
    par_g = small["par"].reshape(depth, 3, CH)
    g_small = dict(
        norm_mix_pre=small["g1"], conv_out_norm=small["ga"], ssm_out_norm=small["gs"], norm_mix_post=small["g2"],
        norm_mlp_pre=small["g3"], norm_mlp_post=small["g4"], ssm_conv_b=small["bs"],
        conv_a_w=lax.dynamic_slice_in_dim(small["wa"].reshape(depth, 3, D), chip * wa_cols, wa_cols, axis=2),
        ssm_conv_w=lax.dynamic_slice_in_dim(small["ws"].reshape(depth, 4, XBC), chip * ws_cols, ws_cols, axis=2),
        dt_bias=par_g[:, 0, :NH], a_log=par_g[:, 1, :NH], d_skip=par_g[:, 2, :NH])

    given = dict(norm_mix_pre=(norm_mix_pre, m_norm_mix_pre, v_norm_mix_pre), w_in=(w_in, m_w_in, v_w_in),
                 conv_a_w=(conv_a_w, m_conv_a_w, v_conv_a_w), ssm_conv_w=(ssm_conv_w, m_ssm_conv_w, v_ssm_conv_w),
                 ssm_conv_b=(ssm_conv_b, m_ssm_conv_b, v_ssm_conv_b), dt_bias=(dt_bias, m_dt_bias, v_dt_bias),
                 a_log=(a_log, m_a_log, v_a_log), d_skip=(d_skip, m_d_skip, v_d_skip),
                 conv_out_norm=(conv_out_norm, m_conv_out_norm, v_conv_out_norm),
                 ssm_out_norm=(ssm_out_norm, m_ssm_out_norm, v_ssm_out_norm), w_out=(w_out, m_w_out, v_w_out),
                 norm_mix_post=(norm_mix_post, m_norm_mix_post, v_norm_mix_post),
                 norm_mlp_pre=(norm_mlp_pre, m_norm_mlp_pre, v_norm_mlp_pre), w_up=(w_up, m_w_up, v_w_up),
                 w_down=(w_down, m_w_down, v_w_down), norm_mlp_post=(norm_mlp_post, m_norm_mlp_post, v_norm_mlp_post))
    halves = dict(zip(["w_in", "w_out", "w_up", "w_down"], zip(acc, from_sibling)))
    order = ["norm_mix_pre", "w_in", "conv_a_w", "ssm_conv_w", "ssm_conv_b", "dt_bias", "a_log", "d_skip",
             "conv_out_norm", "ssm_out_norm", "w_out", "norm_mix_post", "norm_mlp_pre", "w_up", "w_down",
             "norm_mlp_post"]
    g_out, d_out, m_out, v_out = [], [], [], []
    for n in order:
        wv, mv, vv = given[n]
        if n in halves and wv.shape[-1] % CH:
            own, recv = halves[n]
            gv = jnp.concatenate([jnp.where(core == 0, own, recv), jnp.where(core == 0, recv, own)], axis=1)
            to_cols, to_rows = (lambda a: jnp.transpose(a, (2, 0, 1))), (lambda a: jnp.transpose(a, (1, 2, 0)))
            dlt, m2, v2 = [to_rows(o) for o in adamw_leading(to_cols(wv), to_cols(gv), to_cols(mv), to_cols(vv),
                                                             "adamw_cols")]
        elif n in halves:
            gv, dlt, m2, v2 = adamw_halves(wv, *halves[n], mv, vv, core, "adamw_matrix")
        else:
            gv = g_small[n].reshape(wv.shape)
            two_d = lambda a: a.reshape(-1, a.shape[-1])
            dlt, m2, v2 = adamw(two_d(wv), two_d(gv), two_d(mv), two_d(vv), "adamw")
        g_out.append(gv)
        d_out.append(dlt.reshape(wv.shape))
        m_out.append(m2.reshape(wv.shape))
        v_out.append(v2.reshape(wv.shape))
    return (loss, dx.reshape(nb, seq, D), *g_out, *d_out, *m_out, *v_out)
```

```python
import functools

import jax
import jax.numpy as jnp
from jax import lax
from jax.experimental import pallas as pl
from jax.experimental.pallas import tpu as pltpu

f32, bf16 = jnp.float32, jnp.bfloat16

D = 1024
NH, HP = 16, 64
NG, NS = 2, 128
CH = 128
XBC = D + 2 * NG * NS
DFF = 4 * D
IN_COLS = 3 * D + D + XBC + NH
PROJ = 5760
COL_Z, COL_XBC, COL_DT = 3 * D, 4 * D, 4 * D + XBC
EPS = 1e-6
HALO = 8
HBLK = 16
VMEM_LIMIT = 56 * 2**20
MESH = pl.DeviceIdType.MESH

LR, B1, B2, AEPS, WD, STEP = 0.001, 0.9, 0.999, 1e-08, 0.01, 10


def _cparams(n_axes):
    return pltpu.CompilerParams(dimension_semantics=("arbitrary",) * n_axes, vmem_limit_bytes=VMEM_LIMIT)


def _sds(shape, dtype):
    return jax.ShapeDtypeStruct(tuple(shape), dtype)


def _token_spec(token):
    return [] if token is None else [pl.BlockSpec(memory_space=pl.ANY)]


def _token_arg(token):
    return [] if token is None else [token]


def _rms_fwd(x, g):
    r = lax.rsqrt(jnp.mean(x * x, axis=-1, keepdims=True) + EPS)
    return x * r * g


def _rms_bwd(x, g, dy):
    r = lax.rsqrt(jnp.mean(x * x, axis=-1, keepdims=True) + EPS)
    xh = x * r
    gdy = dy * g
    dx = r * (gdy - xh * jnp.mean(xh * gdy, axis=-1, keepdims=True))
    return dx, dy * xh


def _accum(ref, part, first):
    @pl.when(first)
    def _():
        ref[...] = part

    @pl.when(jnp.logical_not(first))
    def _():
        ref[...] += part


def _dot_nt(a, b):
    return lax.dot_general(a, b, (((1,), (1,)), ((), ())), preferred_element_type=f32)


def _dot_tn(a, b):
    return lax.dot_general(a, b, (((0,), (0,)), ((), ())), preferred_element_type=f32)


def _dot(a, b):
    return jnp.dot(a, b, preferred_element_type=f32)


def _split_dot(x, e_bf, n_split, nt=False):
    acc = None
    rem = x
    for s in range(n_split):
        hi = rem.astype(bf16)
        term = _dot_nt(hi, e_bf) if nt else _dot(hi, e_bf)
        acc = term if acc is None else acc + term
        if s + 1 < n_split:
            rem = rem - hi.astype(f32)
    return acc


def _sigmoid(x):
    return 0.5 * jnp.tanh(0.5 * x) + 0.5


def norm_matmul(x, g, w, tm, tn, out_dtype, name, token=None):
    t = x.shape[0]
    if w.ndim == 3:
        assert w.shape[2] == tn
        n = w.shape[0] * tn
        w_spec = pl.BlockSpec((None, D, tn), lambda i, j: (j, 0, 0))
    else:
        n = w.shape[1]
        w_spec = pl.BlockSpec((D, tn), lambda i, j: (0, j))

    def body(x_ref, g_ref, w_ref, *rest):
        o_ref, h_ref = rest[-2:]

        @pl.when(pl.program_id(1) == 0)
        def _():
            h_ref[...] = _rms_fwd(x_ref[...], g_ref[...]).astype(bf16)

        o_ref[...] = _dot(h_ref[...], w_ref[...]).astype(out_dtype)

    return pl.pallas_call(
        body, name=name, grid=(t // tm, n // tn),
        in_specs=[pl.BlockSpec((tm, D), lambda i, j: (i, 0)), pl.BlockSpec((1, D), lambda i, j: (0, 0)), w_spec]
        + _token_spec(token),
        out_specs=[pl.BlockSpec((tm, tn), lambda i, j: (i, j)), pl.BlockSpec((tm, D), lambda i, j: (i, 0))],
        out_shape=[_sds((t, n), out_dtype), _sds((t, D), bf16)],
        compiler_params=_cparams(2))(x, g, w, *_token_arg(token))


def matmul_postnorm(a, w, xres, g, tm, relu2, name):
    t, k = a.shape

    def body(a_ref, w_ref, xr_ref, g_ref, y_ref, xo_ref):
        av = a_ref[...]
        if relu2:
            af = jnp.maximum(av.astype(f32), 0.0)
            av = (af * af).astype(bf16)
        y = _dot(av, w_ref[...])
        y_ref[...] = y.astype(bf16)
        xo_ref[...] = xr_ref[...] + _rms_fwd(y, g_ref[...])

    return pl.pallas_call(
        body, name=name, grid=(t // tm,),
        in_specs=[pl.BlockSpec((tm, k), lambda i: (i, 0)), pl.BlockSpec((k, D), lambda i: (0, 0)),
                  pl.BlockSpec((tm, D), lambda i: (i, 0)), pl.BlockSpec((1, D), lambda i: (0, 0))],
        out_specs=[pl.BlockSpec((tm, D), lambda i: (i, 0)), pl.BlockSpec((tm, D), lambda i: (i, 0))],
        out_shape=[_sds((t, D), bf16), _sds((t, D), f32)],
        compiler_params=_cparams(1))(a, w, xres, g)


def postnorm_bwd_matmul(y, g, dxo, w, fp, tm, tn, out_dtype, name, token=None):
    t, n = y.shape[0], w.shape[0]
    relu = fp is not None

    def body(*refs):
        y_ref, g_ref, dxo_ref, w_ref = refs[:4]
        fp_ref = refs[4] if relu else None
        dy_ref, dg_ref, da_ref = refs[-3:]
        i, j = pl.program_id(0), pl.program_id(1)

        @pl.when(j == 0)
        def _():
            dx, dgc = _rms_bwd(y_ref[...].astype(f32), g_ref[...], dxo_ref[...])
            dy_ref[...] = dx.astype(bf16)
            _accum(dg_ref, jnp.sum(dgc, axis=0, keepdims=True), i == 0)

        da = _dot_nt(dy_ref[...], w_ref[...])
        if relu:
            da = da * (2.0 * jnp.maximum(fp_ref[...].astype(f32), 0.0))
        da_ref[...] = da.astype(out_dtype)

    in_specs = [pl.BlockSpec((tm, D), lambda i, j: (i, 0)), pl.BlockSpec((1, D), lambda i, j: (0, 0)),
                pl.BlockSpec((tm, D), lambda i, j: (i, 0)), pl.BlockSpec((tn, D), lambda i, j: (j, 0))]
    args = [y, g, dxo, w]
    if relu:
        in_specs.append(pl.BlockSpec((tm, tn), lambda i, j: (i, j)))
        args.append(fp)
    in_specs += _token_spec(token)
    args += _token_arg(token)
    return pl.pallas_call(
        body, name=name, grid=(t // tm, n // tn), in_specs=in_specs,
        out_specs=[pl.BlockSpec((tm, D), lambda i, j: (i, 0)), pl.BlockSpec((1, D), lambda i, j: (0, 0)),
                   pl.BlockSpec((tm, tn), lambda i, j: (i, j))],
        out_shape=[_sds((t, D), bf16), _sds((1, D), f32), _sds((t, n), out_dtype)],
        compiler_params=_cparams(2))(*args)


def matmul_prenorm_bwd(da, w, x, g, dxo, tm, name, token=None):
    t, k = da.shape
    blocked = w.ndim == 3

    def body(da_ref, w_ref, x_ref, g_ref, dxo_ref, *rest):
        dx_ref, dg_ref = rest[-2:]
        if blocked:
            kc = w.shape[2]
            dh = _dot_nt(da_ref[:, 0:kc], w_ref[0])
            for q in range(1, w.shape[0]):
                dh = dh + _dot_nt(da_ref[:, q * kc:(q + 1) * kc], w_ref[q])
        else:
            dh = _dot_nt(da_ref[...], w_ref[...])
        dxn, dgc = _rms_bwd(x_ref[...], g_ref[...], dh)
        dx_ref[...] = dxo_ref[...] + dxn
        _accum(dg_ref, jnp.sum(dgc, axis=0, keepdims=True), pl.program_id(0) == 0)

    w_spec = pl.BlockSpec(w.shape, (lambda i: (0, 0, 0)) if blocked else (lambda i: (0, 0)))
    return pl.pallas_call(
        body, name=name, grid=(t // tm,),
        in_specs=[pl.BlockSpec((tm, k), lambda i: (i, 0)), w_spec,
                  pl.BlockSpec((tm, D), lambda i: (i, 0)), pl.BlockSpec((1, D), lambda i: (0, 0)),
                  pl.BlockSpec((tm, D), lambda i: (i, 0))] + _token_spec(token),
        out_specs=[pl.BlockSpec((tm, D), lambda i: (i, 0)), pl.BlockSpec((1, D), lambda i: (0, 0))],
        out_shape=[_sds((t, D), f32), _sds((1, D), f32)],
        compiler_params=_cparams(1))(da, w, x, g, dxo, *_token_arg(token))


def _resident(shape):
    return pl.BlockSpec(shape, lambda i: (0,) * len(shape), pipeline_mode=pl.Buffered(1))


def mlp_fwd(x, g_pre, wu, wd, g_post, tm, name):
    t = x.shape[0]
    nq, _, fc = wu.shape

    def body(x_ref, gp_ref, wu_ref, wd_ref, gq_ref, fp_ref, h_ref, o_ref, xo_ref):
        xv = x_ref[...]
        h = _rms_fwd(xv, gp_ref[...]).astype(bf16)
        h_ref[...] = h
        o = None
        for q in range(nq):
            fq = _dot(h, wu_ref[q])
            fp_ref[:, q * fc:(q + 1) * fc] = fq.astype(bf16)
            r = jnp.maximum(fq, 0.0)
            part = _dot((r * r).astype(bf16), wd_ref[q * fc:(q + 1) * fc, :])
            o = part if o is None else o + part
        o_ref[...] = o.astype(bf16)
        xo_ref[...] = xv + _rms_fwd(o, gq_ref[...])

    row = lambda c: pl.BlockSpec((tm, c), lambda i: (i, 0))
    vec = pl.BlockSpec((1, D), lambda i: (0, 0))
    return pl.pallas_call(
        body, name=name, grid=(t // tm,),
        in_specs=[row(D), vec, _resident(wu.shape), _resident(wd.shape), vec],
        out_specs=[row(nq * fc), row(D), row(D), row(D)],
        out_shape=[_sds((t, nq * fc), bf16), _sds((t, D), bf16), _sds((t, D), bf16), _sds((t, D), f32)],
        compiler_params=_cparams(1))(x, g_pre, wu, wd, g_post)


def mlp_bwd(o, g_post, dxo, wd, fp, wu, x, g_pre, tm, name):
    t = x.shape[0]
    nq, _, fc = wu.shape

    def body(o_ref, gq_ref, dxo_ref, wd_ref, fp_ref, wu_ref, x_ref, gp_ref, do_ref, dfp_ref, dx_ref, dgq_ref, dgp_ref):
        i = pl.program_id(0)
        dxo_v = dxo_ref[...]
        do, dgq = _rms_bwd(o_ref[...].astype(f32), gq_ref[...], dxo_v)
        do_b = do.astype(bf16)
        do_ref[...] = do_b
        dh = None
        for q in range(nq):
            cols = slice(q * fc, (q + 1) * fc)
            dq = _dot_nt(do_b, wd_ref[cols, :]) * (2.0 * jnp.maximum(fp_ref[:, cols].astype(f32), 0.0))
            dq_b = dq.astype(bf16)
            dfp_ref[:, cols] = dq_b
            part = _dot_nt(dq_b, wu_ref[q])
            dh = part if dh is None else dh + part
        dxn, dgp = _rms_bwd(x_ref[...], gp_ref[...], dh)
        dx_ref[...] = dxo_v + dxn
        _accum(dgq_ref, jnp.sum(dgq, axis=0, keepdims=True), i == 0)
        _accum(dgp_ref, jnp.sum(dgp, axis=0, keepdims=True), i == 0)

    row = lambda c: pl.BlockSpec((tm, c), lambda i: (i, 0))
    vec = pl.BlockSpec((1, D), lambda i: (0, 0))
    return pl.pallas_call(
        body, name=name, grid=(t // tm,),
        in_specs=[row(D), vec, row(D), _resident(wd.shape), row(nq * fc), _resident(wu.shape), row(D), vec],
        out_specs=[row(D), row(nq * fc), row(D), vec, vec],
        out_shape=[_sds((t, D), bf16), _sds((t, nq * fc), bf16), _sds((t, D), f32), _sds((1, D), f32), _sds((1, D), f32)],
        compiler_params=_cparams(1))(o, g_post, dxo, wd, fp, wu, x, g_pre)


def matmul_tn(a, b, tm, tn, relu2, name, col_blocks=False):
    t, m = a.shape
    n = b.shape[1]
    if col_blocks:
        out_spec, out_shape = pl.BlockSpec((None, tm, tn), lambda i, j: (j, i, 0)), _sds((n // tn, m, tn), bf16)
    else:
        out_spec, out_shape = pl.BlockSpec((tm, tn), lambda i, j: (i, j)), _sds((m, n), bf16)

    def body(a_ref, b_ref, o_ref, at_ref):
        @pl.when(pl.program_id(1) == 0)
        def _():
            av = a_ref[...]
            if relu2:
                af = jnp.maximum(av.astype(f32), 0.0)
                av = (af * af).astype(bf16)
            at_ref[...] = av.T

        o_ref[...] = _dot(at_ref[...], b_ref[...]).astype(bf16)

    return pl.pallas_call(
        body, name=name, grid=(m // tm, n // tn),
        in_specs=[pl.BlockSpec((t, tm), lambda i, j: (0, i)), pl.BlockSpec((t, tn), lambda i, j: (0, j))],
        out_specs=out_spec, out_shape=out_shape,
        scratch_shapes=[pltpu.VMEM((tm, t), bf16)],
        compiler_params=_cparams(2))(a, b)


ROWS_A = 16
ROWS_B = 32
UNROLL = 4


def _past(win, s):
    return (win if s == 0 else pltpu.roll(win, s, 0))[HALO:]


def _future(win, s):
    n = win.shape[0]
    return (win if s == 0 else pltpu.roll(win, n - s, 0))[:n - HALO]


def _fold8(v):
    return v.reshape(v.shape[0] // 8, 8, v.shape[1]).sum(axis=0)


def _last8(ref):
    return ref[...].astype(f32)[HBLK - HALO:]


def _first8(ref):
    return ref[...].astype(f32)[:HALO]


def _rd(ref, rows):
    return ref[rows, :].astype(f32)


def _halo_prev(tb, col):
    return lambda i: (jnp.maximum(i * (tb // HBLK) - 1, 0), col)


def _halo_next(tb, col, t):
    return lambda i: (jnp.minimum((i + 1) * (tb // HBLK), t // HBLK - 1), col)


def group_a_fwd(proj, wa, g, seq, tb, name):
    t = proj.shape[0]
    bps = seq // tb

    def body(xa_ref, ca_ref, ba_ref, xah_ref, cah_ref, wa_ref, g_ref, o_ref, u_scr):
        first = (pl.program_id(0) % bps) == 0
        u_scr[0:HALO, :] = jnp.where(first, 0.0, _last8(cah_ref) * _last8(xah_ref))
        w, gv = wa_ref[...], g_ref[...]

        def chunk(i, carry):
            r = pl.multiple_of(i * ROWS_A, ROWS_A)
            rows = pl.ds(r, ROWS_A)
            u_scr[pl.ds(pl.multiple_of(HALO + r, HALO), ROWS_A), :] = _rd(ca_ref, rows) * _rd(xa_ref, rows)
            win = u_scr[pl.ds(r, ROWS_A + HALO), :]
            cv = w[2:3] * _past(win, 0) + w[1:2] * _past(win, 1) + w[0:1] * _past(win, 2)
            o_ref[rows, :] = _rms_fwd(_rd(ba_ref, rows) * cv, gv).astype(bf16)
            return carry

        lax.fori_loop(0, tb // ROWS_A, chunk, 0, unroll=UNROLL)

    blk = lambda c: pl.BlockSpec((tb, D), lambda i: (i, c))
    return pl.pallas_call(
        body, name=name, grid=(t // tb,),
        in_specs=[blk(0), blk(1), blk(2),
                  pl.BlockSpec((HBLK, D), _halo_prev(tb, 0)), pl.BlockSpec((HBLK, D), _halo_prev(tb, 1)),
                  pl.BlockSpec((8, D), lambda i: (0, 0)), pl.BlockSpec((1, D), lambda i: (0, 0))],
        out_specs=pl.BlockSpec((tb, D), lambda i: (i, 0)),
        out_shape=_sds((t, 2 * D), bf16),
        scratch_shapes=[pltpu.VMEM((tb + HALO, D), f32)],
        compiler_params=_cparams(1))(proj, proj, proj, proj, proj, wa, g)


def group_a_bwd(proj, dcat, wa, g, seq, tb, name, token=None):
    t = proj.shape[0]
    bps = seq // tb

    def body(xa_ref, ca_ref, ba_ref, dy_ref, xap_ref, cap_ref, xan_ref, can_ref, ban_ref, dyn_ref, wa_ref, g_ref,
             *rest):
        dp_ref, dwa_ref, dg_ref, u_scr, d_scr, acc_scr = rest[-6:]
        i = pl.program_id(0)
        first = (i % bps) == 0
        last = (i % bps) == bps - 1
        w = wa_ref[...]
        gv = g_ref[...]
        u_scr[0:HALO, :] = jnp.where(first, 0.0, _last8(cap_ref) * _last8(xap_ref))
        u_scr[HALO + tb:2 * HALO + tb, :] = _first8(can_ref) * _first8(xan_ref)
        acc_scr[...] = jnp.zeros_like(acc_scr)

        def forward_part(n, carry):
            r = pl.multiple_of(n * ROWS_A, ROWS_A)
            rows = pl.ds(r, ROWS_A)
            ba = _rd(ba_ref, rows)
            u_scr[pl.ds(pl.multiple_of(HALO + r, HALO), ROWS_A), :] = _rd(ca_ref, rows) * _rd(xa_ref, rows)
            win = u_scr[pl.ds(r, ROWS_A + HALO), :]
            u = [_past(win, s) for s in range(3)]
            cv = w[2:3] * u[0] + w[1:2] * u[1] + w[0:1] * u[2]
            dya, dgc = _rms_bwd(ba * cv, gv, _rd(dy_ref, rows))
            dcv = dya * ba
            d_scr[rows, :] = dcv
            dp_ref[rows, 2 * D:3 * D] = (dya * cv).astype(bf16)
            acc_scr[0:8, :] += _fold8(dgc)
            for k in range(3):
                acc_scr[8 + 8 * k:16 + 8 * k, :] += _fold8(dcv * u[2 - k])
            return carry

        lax.fori_loop(0, tb // ROWS_A, forward_part, 0, unroll=UNROLL)

        start = HALO + tb
        cvn = (w[2:3] * u_scr[pl.ds(start, HALO), :] + w[1:2] * u_scr[pl.ds(start - 1, HALO), :]
               + w[0:1] * u_scr[pl.ds(start - 2, HALO), :])
        ban = _first8(ban_ref)
        dyan, _ = _rms_bwd(ban * cvn, gv, _first8(dyn_ref))
        d_scr[tb:tb + HALO, :] = jnp.where(last, 0.0, dyan * ban)

        def backward_part(n, carry):
            r = pl.multiple_of(n * ROWS_A, ROWS_A)
            rows = pl.ds(r, ROWS_A)
            win = d_scr[pl.ds(r, ROWS_A + HALO), :]
            du = w[2:3] * _future(win, 0) + w[1:2] * _future(win, 1) + w[0:1] * _future(win, 2)
            dp_ref[rows, 0:D] = (du * _rd(ca_ref, rows)).astype(bf16)
            dp_ref[rows, D:2 * D] = (du * _rd(xa_ref, rows)).astype(bf16)
            return carry

        lax.fori_loop(0, tb // ROWS_A, backward_part, 0, unroll=UNROLL)

        row = lax.broadcasted_iota(jnp.int32, (8, D), 0)
        dw = jnp.zeros((8, D), f32)
        for k in range(3):
            dw = jnp.where(row == k, jnp.sum(acc_scr[8 + 8 * k:16 + 8 * k, :], axis=0, keepdims=True), dw)
        _accum(dwa_ref, dw, i == 0)
        _accum(dg_ref, jnp.sum(acc_scr[0:8, :], axis=0, keepdims=True), i == 0)

    blk = lambda c: pl.BlockSpec((tb, D), lambda i: (i, c))
    prv = lambda c: pl.BlockSpec((HBLK, D), _halo_prev(tb, c))
    nxt = lambda c: pl.BlockSpec((HBLK, D), _halo_next(tb, c, t))
    return pl.pallas_call(
        body, name=name, grid=(t // tb,),
        in_specs=[blk(0), blk(1), blk(2), blk(0), prv(0), prv(1), nxt(0), nxt(1), nxt(2), nxt(0),
                  pl.BlockSpec((8, D), lambda i: (0, 0)), pl.BlockSpec((1, D), lambda i: (0, 0))] + _token_spec(token),
        out_specs=[pl.BlockSpec((tb, 3 * D), lambda i: (i, 0)), pl.BlockSpec((8, D), lambda i: (0, 0)),
                   pl.BlockSpec((1, D), lambda i: (0, 0))],
        out_shape=[_sds((t, PROJ), bf16), _sds((8, D), f32), _sds((1, D), f32)],
        scratch_shapes=[pltpu.VMEM((tb + 2 * HALO, D), f32), pltpu.VMEM((tb + HALO, D), f32), pltpu.VMEM((32, D), f32)],
        compiler_params=_cparams(1))(proj, proj, proj, dcat, proj, proj, proj, proj, proj, dcat, wa, g,
                                     *_token_arg(token))


CB = 512
XBC_BLK0 = COL_XBC // CB


def conv_b_fwd(proj, ws, bs, seq, tb, name):
    t = proj.shape[0]
    bps = seq // tb

    def body(x_ref, xp_ref, w_ref, b_ref, o_ref, x_scr):
        first = (pl.program_id(1) % bps) == 0
        x_scr[0:HALO, :] = jnp.where(first, 0.0, _last8(xp_ref))
        w, bias = w_ref[...], b_ref[...]

        def chunk(n, carry):
            r = pl.multiple_of(n * ROWS_B, ROWS_B)
            rows = pl.ds(r, ROWS_B)
            x_scr[pl.ds(pl.multiple_of(HALO + r, HALO), ROWS_B), :] = _rd(x_ref, rows)
            win = x_scr[pl.ds(r, ROWS_B + HALO), :]
            xc = bias + w[3:4] * _past(win, 0)
            for k in range(3):
                xc = xc + w[k:k + 1] * _past(win, 3 - k)
            o_ref[rows, :] = xc * _sigmoid(xc)
            return carry

        lax.fori_loop(0, tb // ROWS_B, chunk, 0, unroll=UNROLL)

    return pl.pallas_call(
        body, name=name, grid=(XBC // CB, t // tb),
        in_specs=[pl.BlockSpec((tb, CB), lambda j, i: (i, XBC_BLK0 + j)),
                  pl.BlockSpec((HBLK, CB), lambda j, i: (jnp.maximum(i * (tb // HBLK) - 1, 0), XBC_BLK0 + j)),
                  pl.BlockSpec((8, CB), lambda j, i: (0, j)), pl.BlockSpec((1, CB), lambda j, i: (0, j))],
        out_specs=pl.BlockSpec((tb, CB), lambda j, i: (i, j)),
        out_shape=_sds((t, XBC), f32),
        scratch_shapes=[pltpu.VMEM((tb + HALO, CB), f32)],
        compiler_params=_cparams(2))(proj, proj, ws, bs)


def conv_b_bwd(proj, dxs, ws, bs, dproj, seq, tb, name):
    t = proj.shape[0]
    bps = seq // tb

    def body(x_ref, xp_ref, xn_ref, d_ref, dn_ref, w_ref, b_ref, dproj_ref, dx_ref, dw_ref, db_ref, x_scr, d_scr,
             acc_scr):
        i = pl.program_id(1)
        first = (i % bps) == 0
        last = (i % bps) == bps - 1
        w = w_ref[...]
        bias = b_ref[...]
        x_scr[0:HALO, :] = jnp.where(first, 0.0, _last8(xp_ref))
        x_scr[HALO + tb:2 * HALO + tb, :] = _first8(xn_ref)
        acc_scr[...] = jnp.zeros_like(acc_scr)

        def dsilu(xc, d):
            sg = _sigmoid(xc)
            return d * (sg * (1.0 + xc * (1.0 - sg)))

        def forward_part(n, carry):
            r = pl.multiple_of(n * ROWS_B, ROWS_B)
            rows = pl.ds(r, ROWS_B)
            x_scr[pl.ds(pl.multiple_of(HALO + r, HALO), ROWS_B), :] = _rd(x_ref, rows)
            win = x_scr[pl.ds(r, ROWS_B + HALO), :]
            xs = [_past(win, s) for s in range(4)]
            xc = bias + w[3:4] * xs[0]
            for k in range(3):
                xc = xc + w[k:k + 1] * xs[3 - k]
            dxc = dsilu(xc, _rd(d_ref, rows))
            d_scr[rows, :] = dxc
            acc_scr[0:8, :] += _fold8(dxc)
            for k in range(4):
                acc_scr[8 + 8 * k:16 + 8 * k, :] += _fold8(dxc * xs[3 - k])
            return carry

        lax.fori_loop(0, tb // ROWS_B, forward_part, 0, unroll=UNROLL)

        start = HALO + tb
        xcn = bias + w[3:4] * x_scr[pl.ds(start, HALO), :]
        for k in range(3):
            xcn = xcn + w[k:k + 1] * x_scr[pl.ds(start - 3 + k, HALO), :]
        d_scr[tb:tb + HALO, :] = jnp.where(last, 0.0, dsilu(xcn, _first8(dn_ref)))

        def backward_part(n, carry):
            r = pl.multiple_of(n * ROWS_B, ROWS_B)
            win = d_scr[pl.ds(r, ROWS_B + HALO), :]
            dx = w[3:4] * _future(win, 0)
            for k in range(3):
                dx = dx + w[k:k + 1] * _future(win, 3 - k)
            dx_ref[pl.ds(r, ROWS_B), :] = dx.astype(bf16)
            return carry

        lax.fori_loop(0, tb // ROWS_B, backward_part, 0, unroll=UNROLL)

        row = lax.broadcasted_iota(jnp.int32, (8, CB), 0)
        dw = jnp.zeros((8, CB), f32)
        for k in range(4):
            dw = jnp.where(row == k, jnp.sum(acc_scr[8 + 8 * k:16 + 8 * k, :], axis=0, keepdims=True), dw)
        _accum(dw_ref, dw, i == 0)
        _accum(db_ref, jnp.sum(acc_scr[0:8, :], axis=0, keepdims=True), i == 0)

    nh = t // HBLK
    return pl.pallas_call(
        body, name=name, grid=(XBC // CB, t // tb),
        in_specs=[pl.BlockSpec((tb, CB), lambda j, i: (i, XBC_BLK0 + j)),
                  pl.BlockSpec((HBLK, CB), lambda j, i: (jnp.maximum(i * (tb // HBLK) - 1, 0), XBC_BLK0 + j)),
                  pl.BlockSpec((HBLK, CB), lambda j, i: (jnp.minimum((i + 1) * (tb // HBLK), nh - 1), XBC_BLK0 + j)),
                  pl.BlockSpec((tb, CB), lambda j, i: (i, j)),
                  pl.BlockSpec((HBLK, CB), lambda j, i: (jnp.minimum((i + 1) * (tb // HBLK), nh - 1), j)),
                  pl.BlockSpec((8, CB), lambda j, i: (0, j)), pl.BlockSpec((1, CB), lambda j, i: (0, j)),
                  pl.BlockSpec(memory_space=pl.ANY)],
        out_specs=[pl.BlockSpec((tb, CB), lambda j, i: (i, XBC_BLK0 + j)), pl.BlockSpec((8, CB), lambda j, i: (0, j)),
                   pl.BlockSpec((1, CB), lambda j, i: (0, j))],
        out_shape=[_sds((t, PROJ), bf16), _sds((8, XBC), f32), _sds((1, XBC), f32)],
        input_output_aliases={7: 0},
        scratch_shapes=[pltpu.VMEM((tb + 2 * HALO, CB), f32), pltpu.VMEM((tb + HALO, CB), f32),
                        pltpu.VMEM((40, CB), f32)],
        compiler_params=_cparams(2))(proj, proj, proj, dxs, dxs, ws, bs, dproj)


def place_columns(buf, part, col_block, tb, name):
    t, wdt = part.shape

    def body(p_ref, buf_ref, o_ref):
        o_ref[...] = p_ref[...]

    return pl.pallas_call(
        body, name=name, grid=(t // tb,),
        in_specs=[pl.BlockSpec((tb, wdt), lambda i: (i, 0)), pl.BlockSpec(memory_space=pl.ANY)],
        out_specs=pl.BlockSpec((tb, wdt), lambda i: (i, col_block)), out_shape=_sds(buf.shape, buf.dtype),
        input_output_aliases={1: 0}, compiler_params=_cparams(1))(part, buf)


GW = D // NG
EXPAND_TERMS = 2
REDUCE_TERMS = 1


def _ssd_consts():
    head_of_lane = jnp.arange(D) // HP
    expand = (jnp.arange(CH)[:, None] == head_of_lane[None, :]).astype(bf16)
    tri = (jnp.arange(CH)[:, None] >= jnp.arange(CH)[None, :]).astype(f32)
    return expand, tri


def _ssd_common(par_ref, dtr_ref, e_ref, tri_ref):
    par = par_ref[...]
    dtb, alog, dsk = par[0:1], par[1:2], par[2:3]
    lane = lax.broadcasted_iota(jnp.int32, (CH, CH), 1)
    a = -jnp.exp(alog)
    dtr = dtr_ref[...].astype(f32) + dtb
    sp = jnp.maximum(dtr, 0.0) + jnp.log(1.0 + jnp.exp(-jnp.abs(dtr)))
    dt = jnp.where(lane < NH, sp, 0.0)
    cs = jnp.dot(tri_ref[...], dt * a, precision=lax.Precision.HIGHEST, preferred_element_type=f32)
    cs_last = cs[CH - 1:CH, :]
    dte = jnp.exp(cs_last - cs)
    ecs = jnp.exp(cs)
    ecl = jnp.exp(cs_last)
    e = e_ref[...]
    row8 = lax.broadcasted_iota(jnp.int32, (8, CH), 0)
    r8 = _split_dot(jnp.where(row8 == 0, ecl, jnp.where(row8 == 1, dsk, 0.0)), e, 3)
    return dict(a=a, dtr=dtr, dt=dt, cs=cs, cst=cs.T, dte=dte, ecs=ecs, ecl=ecl, e=e, lane=lane,
                dt_x=_split_dot(dt, e, EXPAND_TERMS), dte_x=_split_dot(dte, e, EXPAND_TERMS),
                ecs_x=_split_dot(ecs, e, EXPAND_TERMS),
                ecl_x=r8[0:1], dsk_x=r8[1:2])


def _decay_matrix(c, h):
    li = lax.broadcasted_iota(jnp.int32, (CH, CH), 0)
    seg = c["cs"][:, h:h + 1] - c["cst"][h:h + 1, :]
    return jnp.exp(jnp.where(li >= c["lane"], seg, -jnp.inf))


def _gate_norm_fwd(y, z, gs):
    zg = z * _sigmoid(z)
    yg = y * zg
    return jnp.concatenate([_rms_fwd(yg[:, k * GW:(k + 1) * GW], gs[:, k * GW:(k + 1) * GW]) for k in range(NG)], axis=1)


def ssd_fwd(xbcs, proj, par, gs, cat, seq, name):
    t = xbcs.shape[0]
    nc = seq // CH
    expand, tri = _ssd_consts()

    def body(xs_ref, b_ref, c_ref, dtr_ref, z_ref, par_ref, e_ref, tri_ref, gs_ref, cat_ref, yn_ref, y_ref, st_ref,
             p_scr, yd_scr):
        @pl.when(pl.program_id(0) % nc == 0)
        def _():
            p_scr[...] = jnp.zeros_like(p_scr)

        c = _ssd_common(par_ref, dtr_ref, e_ref, tri_ref)
        xs = xs_ref[...]
        xdt = xs * c["dt_x"]
        xdt_b = xdt.astype(bf16)
        xdte_b = (xdt * c["dte_x"]).astype(bf16)
        p = p_scr[...]
        st_ref[0] = p
        p_b = p.astype(bf16)
        lo = c["lane"] < HP
        for g in range(NG):
            bg = b_ref[:, g * NS:(g + 1) * NS].astype(bf16)
            cg = c_ref[:, g * NS:(g + 1) * NS].astype(bf16)
            gmat = _dot_nt(cg, bg)
            for q in range(GW // CH):
                col = g * GW + q * CH
                xp = xdt_b[:, col:col + CH]
                h0 = col // HP
                m0 = (gmat * _decay_matrix(c, h0)).astype(bf16)
                m1 = (gmat * _decay_matrix(c, h0 + 1)).astype(bf16)
                stacked = jnp.concatenate([jnp.where(lo, xp, jnp.zeros_like(xp)),
                                           jnp.where(lo, jnp.zeros_like(xp), xp)], axis=0)
                yd_scr[:, col:col + CH] = _dot(jnp.concatenate([m0, m1], axis=1), stacked)
            gsl = slice(g * GW, (g + 1) * GW)
            yoff = _dot(cg, p_b[:, gsl]) * c["ecs_x"][:, gsl]
            yd_scr[:, gsl] = yd_scr[:, gsl] + yoff
            p_scr[:, gsl] = p[:, gsl] * c["ecl_x"][:, gsl] + _dot_tn(bg, xdte_b[:, gsl])
        y = yd_scr[...] + c["dsk_x"] * xs
        y_ref[...] = y
        yn_ref[...] = _gate_norm_fwd(y, z_ref[...].astype(f32), gs_ref[...]).astype(bf16)

    nb = t // CH
    return pl.pallas_call(
        body, name=name, grid=(nb,),
        in_specs=[pl.BlockSpec((CH, D), lambda i: (i, 0)),
                  pl.BlockSpec((CH, NG * NS), lambda i: (i, D // (NG * NS))),
                  pl.BlockSpec((CH, NG * NS), lambda i: (i, D // (NG * NS) + 1)),
                  pl.BlockSpec((CH, CH), lambda i: (i, COL_DT // CH)),
                  pl.BlockSpec((CH, D), lambda i: (i, COL_Z // D)),
                  pl.BlockSpec((8, CH), lambda i: (0, 0)), pl.BlockSpec((CH, D), lambda i: (0, 0)),
                  pl.BlockSpec((CH, CH), lambda i: (0, 0)), pl.BlockSpec((1, D), lambda i: (0, 0)),
                  pl.BlockSpec(memory_space=pl.ANY)],
        out_specs=[pl.BlockSpec((CH, D), lambda i: (i, 1)), pl.BlockSpec((CH, D), lambda i: (i, 0)),
                   pl.BlockSpec((1, NS, D), lambda i: (i, 0, 0))],
        out_shape=[_sds((t, 2 * D), bf16), _sds((t, D), f32), _sds((nb, NS, D), f32)],
        input_output_aliases={9: 0},
        scratch_shapes=[pltpu.VMEM((NS, D), f32), pltpu.VMEM((CH, D), f32)],
        compiler_params=_cparams(1))(xbcs, xbcs, xbcs, proj, proj, par, expand, tri, gs, cat)


def ssd_bwd(xbcs, proj, ypre, states, dcat, par, gs, dproj, seq, name):
    t = xbcs.shape[0]
    nc = seq // CH
    expand, tri = _ssd_consts()

    def body(xs_ref, b_ref, c_ref, dtr_ref, z_ref, y_ref, st_ref, dyn_ref, par_ref, e_ref, tri_ref, gs_ref, dproj_ref,
             dx_ref, dz_ref, ddt_ref, dpar_ref, dgs_ref, dp_scr, dxdt_scr):
        i = pl.program_id(0)

        @pl.when(i % nc == 0)
        def _():
            dp_scr[...] = jnp.zeros_like(dp_scr)

        c = _ssd_common(par_ref, dtr_ref, e_ref, tri_ref)
        e = c["e"]
        lane = c["lane"]
        sub = lax.broadcasted_iota(jnp.int32, (CH, CH), 0)
        xs = xs_ref[...]
        xdt = xs * c["dt_x"]
        xdt_b = xdt.astype(bf16)
        xdte_b = (xdt * c["dte_x"]).astype(bf16)
        p = st_ref[0]
        p_b = p.astype(bf16)
        dpn = dp_scr[...]
        dpn_b = dpn.astype(bf16)

        y, z, gs_v = y_ref[...], z_ref[...].astype(f32), gs_ref[...]
        zs = _sigmoid(z)
        zg = z * zs
        yg = y * zg
        parts, gparts = [], []
        for k in range(NG):
            sl = slice(k * GW, (k + 1) * GW)
            dxk, dgk = _rms_bwd(yg[:, sl], gs_v[:, sl], dyn_ref[:, sl].astype(f32))
            parts.append(dxk)
            gparts.append(dgk)
        dyg = jnp.concatenate(parts, axis=1)
        dgs_rows = jnp.concatenate(gparts, axis=1)
        dy = dyg * zg
        dz_ref[...] = (dyg * y * (zs * (1.0 + z * (1.0 - zs)))).astype(bf16)
        dy_b = dy.astype(bf16)
        dq_b = (dy * c["ecs_x"]).astype(bf16)

        lo = lane < HP
        dcs = jnp.zeros((CH, CH), f32)
        dcst = jnp.zeros((CH, CH), f32)
        for g in range(NG):
            gsl = slice(g * GW, (g + 1) * GW)
            bg = b_ref[:, g * NS:(g + 1) * NS].astype(bf16)
            cg = c_ref[:, g * NS:(g + 1) * NS].astype(bf16)
            gmat = _dot_nt(cg, bg)
            dgm = jnp.zeros((CH, CH), f32)
            for q in range(GW // CH):
                col = g * GW + q * CH
                xp = xdt_b[:, col:col + CH]
                dyp = dy_b[:, col:col + CH]
                zero = jnp.zeros_like(dyp)
                xp2 = jnp.concatenate([jnp.where(lo, xp, zero), jnp.where(lo, zero, xp)], axis=0)
                dy2 = jnp.concatenate([jnp.where(lo, dyp, zero), jnp.where(lo, zero, dyp)], axis=0)
                dm2 = _dot_nt(dyp, xp2)
                ms = []
                for hh in range(2):
                    h = col // HP + hh
                    dec = _decay_matrix(c, h)
                    m = gmat * dec
                    dm = dm2[:, hh * CH:(hh + 1) * CH]
                    dseg = dm * m
                    dcs = dcs + jnp.where(lane == h, jnp.sum(dseg, axis=1, keepdims=True), 0.0)
                    dcst = dcst + jnp.where(sub == h, jnp.sum(dseg, axis=0, keepdims=True), 0.0)
                    dgm = dgm + dm * dec
                    ms.append(m.astype(bf16))
                dxdt_scr[:, col:col + CH] = _dot_tn(jnp.concatenate(ms, axis=0), dy2)
            dgm_b = dgm.astype(bf16)
            bds = _dot(bg, dpn_b[:, gsl])
            dxdt_scr[:, gsl] = dxdt_scr[:, gsl] + c["dte_x"][:, gsl] * bds
            dc_g = _dot(dgm_b, bg) + _dot_nt(dq_b[:, gsl], p_b[:, gsl])
            db_g = _dot_tn(dgm_b, cg) + _dot_nt(xdte_b[:, gsl], dpn_b[:, gsl])
            dx_ref[:, D + g * NS:D + (g + 1) * NS] = db_g
            dx_ref[:, D + NG * NS + g * NS:D + NG * NS + (g + 1) * NS] = dc_g
            dp_scr[:, gsl] = dpn[:, gsl] * c["ecl_x"][:, gsl] + _dot_tn(cg, dq_b[:, gsl])
            q_g = _dot(cg, p_b[:, gsl])
            e_g = e[:, gsl]
            dcs = dcs + c["ecs"] * _split_dot(dy[:, gsl] * q_g, e_g, REDUCE_TERMS, nt=True)
            ddte = _split_dot(xdt[:, gsl] * bds, e_g, REDUCE_TERMS, nt=True) * c["dte"]
            dcs = dcs - ddte
            dcs = dcs + jnp.where(sub == CH - 1, jnp.sum(ddte, axis=0, keepdims=True), 0.0)

        decl = _split_dot(jnp.broadcast_to(jnp.sum(dpn * p, axis=0, keepdims=True), (8, D)), e, 2, nt=True)[0:1]
        dcs = dcs + jnp.where(sub == CH - 1, c["ecl"] * decl, 0.0)
        dcs = dcs - dcst.T
        dadt = lax.dot_general(tri_ref[...], dcs, (((0,), (0,)), ((), ())), precision=lax.Precision.HIGHEST,
                               preferred_element_type=f32)
        dxdt = dxdt_scr[...]
        ddt = dadt * c["a"] + _split_dot(dxdt * xs, e, REDUCE_TERMS, nt=True)
        ddtr = jnp.where(lane < NH, ddt * _sigmoid(c["dtr"]), 0.0)
        ddt_ref[...] = ddtr.astype(bf16)
        dx_ref[:, 0:D] = dxdt * c["dt_x"] + c["dsk_x"] * dy
        dsk = _split_dot(jnp.broadcast_to(jnp.sum(dy * xs, axis=0, keepdims=True), (8, D)), e, 2, nt=True)[0:1]
        dalog = jnp.sum(dadt * c["dt"], axis=0, keepdims=True) * c["a"]
        row8 = lax.broadcasted_iota(jnp.int32, (8, CH), 0)
        dpar = jnp.where(row8 == 0, jnp.sum(ddtr, axis=0, keepdims=True),
                         jnp.where(row8 == 1, dalog, jnp.where(row8 == 2, dsk, 0.0)))
        dpar = jnp.where(lax.broadcasted_iota(jnp.int32, (8, CH), 1) < NH, dpar, 0.0)
        _accum(dpar_ref, dpar, i == 0)
        _accum(dgs_ref, jnp.sum(dgs_rows, axis=0, keepdims=True), i == 0)

    nb = t // CH
    rev = lambda i: (i // nc) * nc + (nc - 1 - i % nc)
    return pl.pallas_call(
        body, name=name, grid=(nb,),
        in_specs=[pl.BlockSpec((CH, D), lambda i: (rev(i), 0)),
                  pl.BlockSpec((CH, NG * NS), lambda i: (rev(i), D // (NG * NS))),
                  pl.BlockSpec((CH, NG * NS), lambda i: (rev(i), D // (NG * NS) + 1)),
                  pl.BlockSpec((CH, CH), lambda i: (rev(i), COL_DT // CH)),
                  pl.BlockSpec((CH, D), lambda i: (rev(i), COL_Z // D)),
                  pl.BlockSpec((CH, D), lambda i: (rev(i), 0)),
                  pl.BlockSpec((1, NS, D), lambda i: (rev(i), 0, 0)),
                  pl.BlockSpec((CH, D), lambda i: (rev(i), 1)),
                  pl.BlockSpec((8, CH), lambda i: (0, 0)), pl.BlockSpec((CH, D), lambda i: (0, 0)),
                  pl.BlockSpec((CH, CH), lambda i: (0, 0)), pl.BlockSpec((1, D), lambda i: (0, 0)),
                  pl.BlockSpec(memory_space=pl.ANY)],
        out_specs=[pl.BlockSpec((CH, XBC), lambda i: (rev(i), 0)), pl.BlockSpec((CH, D), lambda i: (rev(i), COL_Z // D)),
                   pl.BlockSpec((CH, CH), lambda i: (rev(i), 0)),
                   pl.BlockSpec((8, CH), lambda i: (0, 0)), pl.BlockSpec((1, D), lambda i: (0, 0))],
        out_shape=[_sds((t, XBC), f32), _sds((t, PROJ), bf16), _sds((t, CH), bf16), _sds((8, CH), f32), _sds((1, D), f32)],
        input_output_aliases={12: 1},
        scratch_shapes=[pltpu.VMEM((NS, D), f32), pltpu.VMEM((CH, D), f32)],
        compiler_params=_cparams(1))(xbcs, xbcs, xbcs, proj, proj, ypre, states, dcat, par, expand, tri, gs, dproj)


def loss_head(y, target, tb, name):
    t = y.shape[0]

    def body(y_ref, t_ref, s_ref, dy_ref):
        err = y_ref[...] - t_ref[...]
        dy_ref[...] = err * (1.0 / D)
        _accum(s_ref, jnp.zeros((8, CH), f32) + jnp.sum(err * err), pl.program_id(0) == 0)

    return pl.pallas_call(
        body, name=name, grid=(t // tb,),
        in_specs=[pl.BlockSpec((tb, D), lambda i: (i, 0)), pl.BlockSpec((tb, D), lambda i: (i, 0))],
        out_specs=[pl.BlockSpec((8, CH), lambda i: (0, 0)), pl.BlockSpec((tb, D), lambda i: (i, 0))],
        out_shape=[_sds((8, CH), f32), _sds((t, D), f32)],
        compiler_params=_cparams(1))(y, target)


def _tiles(t, seq):
    tm = min(512, t)
    return dict(tm=tm, tm_small=min(256, t), tm_large=min(1024, t), tm_huge=min(2048, t), tb=min(512, seq))


def local_step(x, target, depth, weights_of, seq, grads_done=None):
    t = x.shape[0]
    ts = _tiles(t, seq)
    tm, tl, th, tb = ts["tm"], ts["tm_large"], ts["tm_huge"], ts["tb"]
    saved, ws = [], []
    for l in range(depth):
        w = weights_of(l, x)
        ws.append(w)
        proj, h1 = norm_matmul(x, w["g1"], w["win"], th, 1152, bf16, "in_proj", token=w.get("token"))
        cat = group_a_fwd(proj, w["wa"], w["ga"], seq, tb, "group_a_fwd")
        xbcs = conv_b_fwd(proj, w["ws"], w["bs"], seq, tb, "conv_b_fwd")
        cat, ypre, states = ssd_fwd(xbcs, proj, w["par"], w["gs"], cat, seq, "ssd_fwd")
        if "late" in w:
            w.update(w.pop("late")(cat))
        mix, x2 = matmul_postnorm(cat, w["wo"], x, w["g2"], tl, False, "out_proj")
        fp, h2, o, x3 = mlp_fwd(x2, w["g3"], w["wu"], w["wd"], w["g4"], tm, "mlp_fwd")
        saved.append(dict(x=x, proj=proj, h1=h1, xbcs=xbcs, ypre=ypre, states=states, cat=cat, mix=mix, x2=x2,
                          fp=fp, h2=h2, o=o))
        x = x3
    sse, dx = loss_head(x, target, tm, "loss_head")
    grads = [None] * depth
    for l in reversed(range(depth)):
        s, w = saved[l], ws[l]
        do, dfp, dx2, dg4, dg3 = mlp_bwd(s["o"], w["g4"], dx, w["wd"], s["fp"], w["wu"], s["x2"], w["g3"],
                                         ts["tm_small"], "mlp_bwd")
        dwd = matmul_tn(s["fp"], do, 512, 1024, True, "mlp_down_dw")
        dwu = matmul_tn(s["h2"], dfp, 512, 1024, False, "mlp_up_dw", col_blocks=True)
        dmix, dg2, dcat = postnorm_bwd_matmul(s["mix"], w["g2"], dx2, w["wo"], None, tl, 1024, bf16, "out_proj_bwd")
        dwo = matmul_tn(s["cat"], dmix, 512, 1024, False, "out_proj_dw")
        token = None if grads_done is None else grads_done(l, dict(wo=dwo, wu=dwu, wd=dwd), False)
        dproj, dwa, dga = group_a_bwd(s["proj"], dcat, w["wa"], w["ga"], seq, tb, "group_a_bwd", token=token)
        dxbcs, dproj, ddt, dpar, dgs = ssd_bwd(s["xbcs"], s["proj"], s["ypre"], s["states"], dcat, w["par"], w["gs"],
                                               dproj, seq, "ssd_bwd")
        dproj, dws, dbs = conv_b_bwd(s["proj"], dxbcs, w["ws"], w["bs"], dproj, seq, tb, "conv_b_bwd")
        dproj = place_columns(dproj, ddt, COL_DT // CH, tm, "place_ddt")
        dwin = matmul_tn(s["h1"], dproj, 512, 1152, False, "in_proj_dw")
        token = None if grads_done is None else grads_done(l, dict(win=dwin), True)
        dx, dg1 = matmul_prenorm_bwd(dproj, w["win"], s["x"], w["g1"], dx2, ts["tm_small"], "in_proj_bwd", token=token)
        grads[l] = dict(win=dwin, wo=dwo, wu=dwu, wd=dwd, wa=dwa, ws=dws, bs=dbs, par=dpar,
                        g1=dg1, ga=dga, gs=dgs, g2=dg2, g3=dg3, g4=dg4)
    return sse, dx, grads


GROUPS = {
    "chips": [(1, 0, 0), (0, 1, 0), (1, 1, 0)],
    "pair": [(0, 0, 1)],
    "all": [(1, 0, 0), (0, 1, 0), (1, 1, 0), (0, 0, 1), (1, 0, 1), (0, 1, 1), (1, 1, 1)],
}


def _group_index(group, x, y, c):
    return {"chips": 2 * x + y, "pair": c, "all": 4 * x + 2 * y + c}[group]


def _chunk_indices(shape, pieces):
    if len(shape) < 3:
        return [()]
    lead = [()]
    for n in shape[:-2]:
        lead = [i + (k,) for i in lead for k in range(n)]
    rows = shape[-2]
    split = max(1, pieces // len(lead))
    while split > 1 and (rows % split or (rows // split) % 16):
        split -= 1
    step = rows // split
    return [i + (pl.ds(s * step, step),) for i in lead for s in range(split)]


def _exchange(arrays, out_shapes, group, src_view, dst_view, view_shape, name, own, pieces=16):
    masks = GROUPS[group]
    na, nm = len(arrays), len(masks)
    cuts = [_chunk_indices(view_shape(a), pieces) for a in range(na)]

    def body(*refs):
        ins, outs = refs[:na], refs[na:2 * na]
        send_sems, recv_sems = refs[2 * na:2 * na + 2]
        local_sems = refs[2 * na + 2] if own else None
        x, y, c = lax.axis_index("x"), lax.axis_index("y"), lax.axis_index("c")
        me = _group_index(group, x, y, c)
        peers = []
        for mx, my, mc in masks:
            px, py, pc = (1 - x if mx else x), (1 - y if my else y), (1 - c if mc else c)
            peers.append(((px, py, pc), _group_index(group, px, py, pc)))

        def part(ref, idx):
            return ref.at[idx] if idx else ref

        if own:
            for a in range(na):
                for idx in cuts[a]:
                    pltpu.make_async_copy(part(src_view(ins[a], a, me), idx), part(dst_view(outs[a], a, me), idx),
                                          local_sems.at[a]).start()
        for a in range(na):
            for j, (dev, pidx) in enumerate(peers):
                for idx in cuts[a]:
                    pltpu.make_async_remote_copy(
                        src_ref=part(src_view(ins[a], a, pidx), idx), dst_ref=part(dst_view(outs[a], a, me), idx),
                        send_sem=send_sems.at[a * nm + j], recv_sem=recv_sems.at[a * nm + j],
                        device_id=dev, device_id_type=MESH).start()
        whole = []
        for a in range(na):
            for j, (dev, pidx) in enumerate(peers):
                whole.append(pltpu.make_async_remote_copy(
                    src_ref=src_view(ins[a], a, pidx), dst_ref=dst_view(outs[a], a, pidx),
                    send_sem=send_sems.at[a * nm + j], recv_sem=recv_sems.at[a * nm + j],
                    device_id=dev, device_id_type=MESH))
        for cp in whole:
            cp.wait_recv()
        for cp in whole:
            cp.wait_send()
        if own:
            for a in range(na):
                pltpu.make_async_copy(src_view(ins[a], a, me), dst_view(outs[a], a, me), local_sems.at[a]).wait()

    hbm = pl.BlockSpec(memory_space=pltpu.HBM)
    sems = [pltpu.SemaphoreType.DMA((na * nm,)), pltpu.SemaphoreType.DMA((na * nm,))]
    return pl.pallas_call(
        body, name=name, in_specs=[hbm] * na, out_specs=[hbm] * na,
        out_shape=[_sds(s, a.dtype) for s, a in zip(out_shapes, arrays)],
        scratch_shapes=sems + ([pltpu.SemaphoreType.DMA((na,))] if own else []))(*arrays)


def all_gather(arrays, group, name, slot_axis=0, own=True):
    n = len(GROUPS[group]) + 1
    shapes = [a.shape[:slot_axis] + (n,) + a.shape[slot_axis:] for a in arrays]
    lead = (slice(None),) * slot_axis
    return _exchange(arrays, shapes, group, lambda r, a, i: r, lambda r, a, i: r.at[lead + (i,)],
                     lambda a: arrays[a].shape, name, own)


HBM_SPEC = pl.BlockSpec(memory_space=pltpu.HBM)
SEM_SPEC = pl.BlockSpec(memory_space=pltpu.SEMAPHORE)
DATAFLOW = pltpu.SideEffectType.DATAFLOW_SIDE_EFFECTING
N_CHIPS = 4


def _peers(group, x, y, c):
    out = []
    for mx, my, mc in GROUPS[group]:
        px, py, pc = (1 - x if mx else x), (1 - y if my else y), (1 - c if mc else c)
        out.append(((px, py, pc), _group_index(group, px, py, pc)))
    return out


def _whole_views(sources):
    return dict(src=lambda ref, a, c, to: ref, dst=lambda ref, a, c, sender: ref.at[sender],
                rows=lambda a: sources[a].shape[0])


def _weight_views(shards):
    half = [s.shape[0] // 2 for s in shards]
    return dict(src=lambda ref, a, c, to_chip: ref.at[pl.ds(c * half[a], half[a])],
                dst=lambda ref, a, c, from_chip: ref.at[from_chip, pl.ds(c * half[a], half[a])],
                rows=lambda a: half[a])


def _grad_views(sums):
    return dict(src=lambda ref, a, c, to_chip: ref.at[to_chip], dst=lambda ref, a, c, from_chip: ref.at[from_chip],
                rows=lambda a: sums[a].shape[1])


def chips_start(sources, zones, views, name, pieces=4, after=None, group="chips"):
    na, nm = len(sources), len(GROUPS[group])

    def body(*refs):
        ins, lands = refs[:na], refs[na:2 * na]
        n_in = 2 * na + len(_token_arg(after))
        send_sems, recv_sems, token = refs[n_in], refs[n_in + 1], refs[-1]
        x, y, c = lax.axis_index("x"), lax.axis_index("y"), lax.axis_index("c")
        me = _group_index(group, x, y, c)
        for a in range(na):
            step = views["rows"](a) // pieces
            for j, (dev, to) in enumerate(_peers(group, x, y, c)):
                for q in range(pieces):
                    rows = pl.ds(q * step, step)
                    pltpu.make_async_remote_copy(
                        src_ref=views["src"](ins[a], a, c, to).at[rows],
                        dst_ref=views["dst"](lands[a], a, c, me).at[rows],
                        send_sem=send_sems.at[a * nm + j], recv_sem=recv_sems.at[a * nm + j],
                        device_id=dev, device_id_type=MESH).start()
        token[...] = jnp.zeros_like(token)

    both = list(sources) + list(zones)
    outs = pl.pallas_call(
        body, name=name,
        out_shape=(pltpu.SemaphoreType.DMA((na * nm,)), pltpu.SemaphoreType.DMA((na * nm,)),
                   *[pltpu.HBM(b.shape, b.dtype) for b in both], _sds((8, CH), f32)),
        in_specs=[HBM_SPEC] * (2 * na) + _token_spec(after),
        out_specs=(SEM_SPEC, SEM_SPEC, *[HBM_SPEC] * (2 * na), pl.BlockSpec(memory_space=pltpu.VMEM)),
        input_output_aliases={i: 2 + i for i in range(2 * na)},
        compiler_params=pltpu.CompilerParams(has_side_effects=DATAFLOW))(
            *[pltpu.with_memory_space_constraint(b, pltpu.HBM) for b in both], *_token_arg(after))
    return dict(send=outs[0], recv=outs[1], sources=list(outs[2:2 + na]), zones=list(outs[2 + na:2 + 2 * na]),
                token=outs[-1], views=views, group=group)


def chips_wait(started, after, name):
    sources, zones, views, group = started["sources"], started["zones"], started["views"], started["group"]
    na, nm = len(sources), len(GROUPS[group])

    def body(*refs):
        ins, lands = refs[:na], refs[na:2 * na]
        send_sems, recv_sems = refs[2 * na], refs[2 * na + 1]
        x, y, c = lax.axis_index("x"), lax.axis_index("y"), lax.axis_index("c")
        for a in range(na):
            for j, (dev, peer) in enumerate(_peers(group, x, y, c)):
                cp = pltpu.make_async_remote_copy(
                    src_ref=views["src"](ins[a], a, c, peer), dst_ref=views["dst"](lands[a], a, c, peer),
                    send_sem=send_sems.at[a * nm + j], recv_sem=recv_sems.at[a * nm + j],
                    device_id=dev, device_id_type=MESH)
                cp.wait_send()
                cp.wait_recv()

    both = list(sources) + list(zones)
    outs = pl.pallas_call(
        body, name=name, out_shape=tuple(pltpu.HBM(b.shape, b.dtype) for b in both),
        in_specs=[HBM_SPEC] * (2 * na) + [SEM_SPEC, SEM_SPEC, pl.BlockSpec(memory_space=pl.ANY)],
        out_specs=tuple([HBM_SPEC] * (2 * na)), input_output_aliases={i: i for i in range(2 * na)},
        compiler_params=pltpu.CompilerParams(has_side_effects=DATAFLOW))(*both, started["send"], started["recv"], after)
    return list(outs[:na]), list(outs[na:])


def weights_share(zones, name):
    na, nm = len(zones), N_CHIPS - 1

    def body(*refs):
        lands = refs[na:2 * na]
        send_sems, recv_sems = refs[2 * na:]
        x, y, c = lax.axis_index("x"), lax.axis_index("y"), lax.axis_index("c")
        chip = 2 * x + y
        sibling = (x, y, 1 - c)
        sends = []
        for a in range(na):
            half = zones[a].shape[1] // 2
            for m in range(1, N_CHIPS):
                mine = lands[a].at[chip ^ m, pl.ds(c * half, half)]
                sends.append(pltpu.make_async_remote_copy(
                    src_ref=mine, dst_ref=mine, send_sem=send_sems.at[a * nm + m - 1],
                    recv_sem=recv_sems.at[a * nm + m - 1], device_id=sibling, device_id_type=MESH))
        for cp in sends:
            cp.start()
        for a in range(na):
            half = zones[a].shape[1] // 2
            for m in range(1, N_CHIPS):
                theirs = lands[a].at[chip ^ m, pl.ds((1 - c) * half, half)]
                pltpu.make_async_remote_copy(
                    src_ref=theirs, dst_ref=theirs, send_sem=send_sems.at[a * nm + m - 1],
                    recv_sem=recv_sems.at[a * nm + m - 1], device_id=sibling, device_id_type=MESH).wait_recv()
        for cp in sends:
            cp.wait_send()

    return pl.pallas_call(
        body, name=name, in_specs=[HBM_SPEC] * na, out_specs=[HBM_SPEC] * na,
        out_shape=[_sds(z.shape, z.dtype) for z in zones], input_output_aliases={i: i for i in range(na)},
        scratch_shapes=[pltpu.SemaphoreType.DMA((na * nm,)), pltpu.SemaphoreType.DMA((na * nm,))])(*zones)


def pair_send_halves(grads, name):
    half = [g.shape[1] // 2 for g in grads]
    shapes = [(g.shape[0], h, g.shape[2]) for g, h in zip(grads, half)]
    return _exchange(grads, shapes, "pair", lambda r, a, i: r.at[:, pl.ds(i * half[a], half[a])],
                     lambda r, a, i: r, lambda a: shapes[a], name, False)


def sum_pair_half(g, recv, core, name, tb=256, by_chip=None):
    nk, r, c = g.shape
    tb = min(tb, r // 2)
    nb = r // 2 // tb

    def body(core_ref, g_ref, r_ref, o_ref):
        s = g_ref[...].astype(f32) + r_ref[...].astype(f32)
        if by_chip is None:
            o_ref[...] = s.astype(bf16)
        else:
            for k in range(by_chip[0]):
                o_ref[k] = s[:, k * by_chip[1]:(k + 1) * by_chip[1]].astype(bf16)

    if by_chip is None:
        out_spec = pl.BlockSpec((None, tb, c), lambda k, i, core_ref: (k, i, 0))
        out_shape = _sds((nk, r // 2, c), bf16)
    else:
        assert nk == 1
        out_spec = pl.BlockSpec((by_chip[0], tb, by_chip[1]), lambda k, i, core_ref: (0, i, 0))
        out_shape = _sds((by_chip[0], r // 2, by_chip[1]), bf16)
    return pl.pallas_call(
        body, name=name,
        grid_spec=pltpu.PrefetchScalarGridSpec(
            num_scalar_prefetch=1, grid=(nk, nb),
            in_specs=[pl.BlockSpec((None, tb, c), lambda k, i, core_ref: (k, core_ref[0] * nb + i, 0)),
                      pl.BlockSpec((None, tb, c), lambda k, i, core_ref: (k, i, 0))],
            out_specs=out_spec),
        out_shape=out_shape, compiler_params=_cparams(2))(jnp.reshape(core, (1,)).astype(jnp.int32), g, recv)


def assemble_columns(blocks, width, name, tb=256):
    n, r, c = blocks.shape

    def body(b_ref, o_ref):
        for k in range(n):
            o_ref[:, k * c:(k + 1) * c] = b_ref[k]
        o_ref[:, n * c:] = jnp.zeros((tb, width - n * c), blocks.dtype)

    return pl.pallas_call(
        body, name=name, grid=(r // tb,), in_specs=[pl.BlockSpec((n, tb, c), lambda i: (0, i, 0))],
        out_specs=pl.BlockSpec((tb, width), lambda i: (i, 0)), out_shape=_sds((r, width), blocks.dtype),
        compiler_params=_cparams(1))(blocks)


def chip_sum_into(acc, layer, own, others, chip, name, tb=256):
    n, r, c = own.shape
    tb = min(tb, r)

    def body(chip_ref, x_ref, y1_ref, y2_ref, y3_ref, acc_ref, o_ref):
        o_ref[...] = ((x_ref[...].astype(f32) + y1_ref[...].astype(f32)) + y2_ref[...].astype(f32)) + y3_ref[...].astype(f32)

    def slot(k):
        return pl.BlockSpec((None, tb, c), lambda i, chip_ref: (chip_ref[0] ^ k, i, 0))

    return pl.pallas_call(
        body, name=name,
        grid_spec=pltpu.PrefetchScalarGridSpec(
            num_scalar_prefetch=1, grid=(r // tb,),
            in_specs=[slot(k) for k in range(n)] + [pl.BlockSpec(memory_space=pl.ANY)],
            out_specs=pl.BlockSpec((None, tb, c), lambda i, chip_ref: (layer, i, 0))),
        out_shape=_sds(acc.shape, f32), input_output_aliases={n + 1: 0}, compiler_params=_cparams(1))(
            jnp.reshape(chip, (1,)).astype(jnp.int32), own, *([others] * (n - 1)), acc)


def adamw_half(w, g_half, m, v, half, name, before=None, token=None, tb=256):
    depth, r, c = w.shape
    tb = min(tb, r // 2)
    nb = r // 2 // tb
    n_extra = (0 if before is None else 4) + len(_token_arg(token))

    def body(half_ref, w_ref, gh_ref, m_ref, v_ref, *rest):
        g_ref, d_ref, mo_ref, vo_ref = rest[n_extra:]
        gv = gh_ref[...]
        m2 = B1 * m_ref[...] + (1.0 - B1) * gv
        v2 = B2 * v_ref[...] + (1.0 - B2) * (gv * gv)
        m_hat = m2 / (1.0 - B1 ** STEP)
        v_hat = v2 / (1.0 - B2 ** STEP)
        g_ref[...] = gv
        d_ref[...] = -LR * (m_hat / (jnp.sqrt(v_hat) + AEPS) + WD * w_ref[...])
        mo_ref[...] = m2
        vo_ref[...] = v2

    whole = pl.BlockSpec((None, tb, c), lambda l, i, half_ref: (l, half_ref[0] * nb + i, 0))
    part = pl.BlockSpec((None, tb, c), lambda l, i, half_ref: (l, i, 0))
    extra = ([] if before is None else list(before)) + _token_arg(token)
    return pl.pallas_call(
        body, name=name,
        grid_spec=pltpu.PrefetchScalarGridSpec(
            num_scalar_prefetch=1, grid=(depth, nb),
            in_specs=[whole, part, whole, whole] + [pl.BlockSpec(memory_space=pl.ANY)] * n_extra, out_specs=[whole] * 4),
        out_shape=[_sds(w.shape, f32)] * 4,
        input_output_aliases={} if before is None else {5 + k: k for k in range(4)},
        compiler_params=_cparams(2))(jnp.reshape(half, (1,)).astype(jnp.int32), w, g_half, m, v, *extra)


def sum_slots(y, out_dtype, name, tb=256):
    n, r, c = y.shape
    tb = min(tb, r)

    def body(y_ref, o_ref):
        acc = y_ref[0].astype(f32)
        for i in range(1, n):
            acc = acc + y_ref[i].astype(f32)
        o_ref[...] = acc.astype(out_dtype)

    return pl.pallas_call(
        body, name=name, grid=(r // tb,),
        in_specs=[pl.BlockSpec((n, tb, c), lambda i: (0, i, 0))], out_specs=pl.BlockSpec((tb, c), lambda i: (i, 0)),
        out_shape=_sds((r, c), out_dtype), compiler_params=_cparams(1))(y)


def adamw(w, g, m, v, name, tb=256):
    r, c = w.shape
    tb = min(tb, r)

    def body(w_ref, g_ref, m_ref, v_ref, d_ref, mo_ref, vo_ref):
        gv = g_ref[...]
        m2 = B1 * m_ref[...] + (1.0 - B1) * gv
        v2 = B2 * v_ref[...] + (1.0 - B2) * (gv * gv)
        m_hat = m2 / (1.0 - B1 ** STEP)
        v_hat = v2 / (1.0 - B2 ** STEP)
        d_ref[...] = -LR * (m_hat / (jnp.sqrt(v_hat) + AEPS) + WD * w_ref[...])
        mo_ref[...] = m2
        vo_ref[...] = v2

    spec = pl.BlockSpec((tb, c), lambda i: (i, 0))
    return pl.pallas_call(
        body, name=name, grid=(r // tb,), in_specs=[spec] * 4, out_specs=[spec] * 3,
        out_shape=[_sds((r, c), f32)] * 3, compiler_params=_cparams(1))(w, g, m, v)


def adamw_leading(w, g, m, v, name, tc=64):
    c, l, r = w.shape
    main = c // tc
    tail = c - main * tc

    def body(w_ref, g_ref, m_ref, v_ref, *rest):
        d_ref, mo_ref, vo_ref = rest[-3:]
        gv = g_ref[...]
        m2 = B1 * m_ref[...] + (1.0 - B1) * gv
        v2 = B2 * v_ref[...] + (1.0 - B2) * (gv * gv)
        m_hat = m2 / (1.0 - B1 ** STEP)
        v_hat = v2 / (1.0 - B2 ** STEP)
        d_ref[...] = -LR * (m_hat / (jnp.sqrt(v_hat) + AEPS) + WD * w_ref[...])
        mo_ref[...] = m2
        vo_ref[...] = v2

    spec = pl.BlockSpec((tc, l, r), lambda i: (i, 0, 0))
    outs = pl.pallas_call(
        functools.partial(body), name=name, grid=(main,), in_specs=[spec] * 4, out_specs=[spec] * 3,
        out_shape=[_sds(w.shape, f32)] * 3, compiler_params=_cparams(1))(w, g, m, v)
    if tail:
        assert (main * tc) % tail == 0
        last = pl.BlockSpec((tail, l, r), lambda i: (main * tc // tail, 0, 0))
        outs = pl.pallas_call(
            functools.partial(body), name=name + "_tail", grid=(1,),
            in_specs=[last] * 4 + [pl.BlockSpec(memory_space=pl.ANY)] * 3, out_specs=[last] * 3,
            out_shape=[_sds(w.shape, f32)] * 3, input_output_aliases={4: 0, 5: 1, 6: 2},
            compiler_params=_cparams(1))(w, g, m, v, *outs)
    return outs


SMALL_ROW = 1024
SMALL_GAINS = ("g1", "ga", "gs", "g2", "g3", "g4")
SMALL_LAYER_ROWS = 8 + 8 + 16 + 8


def _pack_small(grads):
    wide = lambda a: jnp.pad(a, ((0, 0), (0, 2 * SMALL_ROW - a.shape[1]))).reshape(-1, SMALL_ROW)
    row = lax.broadcasted_iota(jnp.int32, (8, SMALL_ROW), 0)
    parts = []
    for g in grads:
        singles = [g[k] for k in SMALL_GAINS] + [g["bs"][:, :SMALL_ROW],
                                                 jnp.pad(g["bs"][:, SMALL_ROW:], ((0, 0), (0, 2 * SMALL_ROW - XBC)))]
        first = sum(jnp.where(row == k, s, 0.0) for k, s in enumerate(singles))
        parts += [first, g["wa"], wide(g["ws"]), jnp.pad(g["par"], ((0, 0), (0, SMALL_ROW - CH)))]
    return jnp.concatenate(parts, axis=0)


def _unpack_small(packed, depth):
    rows = packed.reshape(depth, SMALL_LAYER_ROWS, SMALL_ROW)
    out = {k: rows[:, i] for i, k in enumerate(SMALL_GAINS)}
    out["bs"] = rows[:, 6:8].reshape(depth, 2 * SMALL_ROW)[:, :XBC]
    out["wa"] = rows[:, 8:11]
    out["ws"] = rows[:, 16:32].reshape(depth, 8, 2 * SMALL_ROW)[:, :4, :XBC]
    out["par"] = rows[:, 32:35, :CH]
    return out


def kernel(x, norm_mix_pre, w_in, conv_a_w, ssm_conv_w, ssm_conv_b, dt_bias, a_log, d_skip, conv_out_norm, ssm_out_norm, w_out, norm_mix_post, norm_mlp_pre, w_up, w_down, norm_mlp_post, loss_target, m_norm_mix_pre, m_w_in, m_conv_a_w, m_ssm_conv_w, m_ssm_conv_b, m_dt_bias, m_a_log, m_d_skip, m_conv_out_norm, m_ssm_out_norm, m_w_out, m_norm_mix_post, m_norm_mlp_pre, m_w_up, m_w_down, m_norm_mlp_post, v_norm_mix_pre, v_w_in, v_conv_a_w, v_ssm_conv_w, v_ssm_conv_b, v_dt_bias, v_a_log, v_d_skip, v_conv_out_norm, v_ssm_out_norm, v_w_out, v_norm_mix_post, v_norm_mlp_pre, v_w_up, v_w_down, v_norm_mlp_post):
    nb, seq, _ = x.shape
    t = nb * seq
    depth = w_in.shape[0]
    ncol = w_in.shape[2]
    chip = 2 * lax.axis_index("x") + lax.axis_index("y")

    taps = [conv_a_w, ssm_conv_w]
    taps_g = all_gather(taps, "chips", "gather_taps", slot_axis=1, own=False)
    wa_g, ws_g = [lax.dynamic_update_index_in_dim(g, s, chip, 1) for g, s in zip(taps_g, taps)]
    wa_full = jnp.transpose(wa_g, (0, 2, 1, 3)).reshape(depth, 3, D)
    ws_full = jnp.transpose(ws_g, (0, 2, 1, 3)).reshape(depth, 4, XBC)
    lane_pad = lambda a: jnp.pad(a, ((0, 0), (0, CH - a.shape[1])))
    par = jnp.stack([lane_pad(dt_bias), lane_pad(a_log), lane_pad(d_skip)], axis=1)
    par = jnp.pad(par, ((0, 0), (0, 5), (0, 0)))

    layer_shards = lambda l: [w_in[l].astype(bf16), w_out[l].astype(bf16), w_up[l].astype(bf16), w_down[l].astype(bf16)]
    issued = []

    def start(shards, name):
        zones = [lax.empty((N_CHIPS,) + s.shape, s.dtype) for s in shards]
        issued.append(chips_start(shards, zones, _weight_views(shards), name,
                                  after=issued[-1]["token"] if issued else taps_g[0]))
        return issued[-1]

    def finish(started, after, name):
        shards, zones = chips_wait(started, after, name)
        zones = weights_share(zones, "weights_share")
        return [lax.dynamic_update_index_in_dim(z, s, chip, 0) for z, s in zip(zones, shards)]

    def shaped(mats):
        wo_z, wu_z, wd_z = mats
        return wo_z.reshape(2 * D, D), wu_z, wd_z.reshape(DFF, D)

    first = layer_shards(0)
    travelling = {0: start(first[:1], "weights_start_0")}
    rest = start(first[1:], "weights_start_0_rest")
    for l in range(1, depth):
        travelling[l] = start(layer_shards(l), f"weights_start_{l}")

    def weights_of(l, x_in):
        mats = finish(travelling.pop(l), x_in, f"weights_wait_{l}")
        w = dict(win=assemble_columns(mats[0], PROJ, "assemble_w_in"), wa=jnp.pad(wa_full[l], ((0, 5), (0, 0))),
                 ws=jnp.pad(ws_full[l], ((0, 4), (0, 0))), bs=ssm_conv_b[l][None], par=par[l],
                 g1=norm_mix_pre[l][None], ga=conv_out_norm[l][None], gs=ssm_out_norm[l][None],
                 g2=norm_mix_post[l][None], g3=norm_mlp_pre[l][None], g4=norm_mlp_post[l][None])
        if l == 0:
            w["token"] = issued[-1]["token"]
            w["late"] = lambda after: dict(zip(("wo", "wu", "wd"), shaped(finish(rest, after, "weights_wait_0_rest"))))
        else:
            w.update(zip(("wo", "wu", "wd"), shaped(mats[1:])))
        return w

    core = lax.axis_index("c")
    grads_travelling = {}
    given_m = dict(win=m_w_in, wo=m_w_out, wu=m_w_up, wd=m_w_down)
    given_v = dict(win=v_w_in, wo=v_w_out, wu=v_w_up, wd=v_w_down)

    chip_major = dict(win=lambda a: a[None], wo=lambda a: a.reshape(N_CHIPS, 2 * D // N_CHIPS, D), wu=lambda a: a,
                      wd=lambda a: a.reshape(N_CHIPS, DFF // N_CHIPS, D))
    held = {}

    def grads_done(l, g, last):
        if l > 0 and not last:
            held[l] = g
            return None
        g = {**held.pop(l, {}), **g}
        keys = [k for k in ("win", "wo", "wu", "wd") if k in g]
        mats = [chip_major[k](g[k]) for k in keys]
        received = pair_send_halves(mats, "grads_to_pair")
        sums = [sum_pair_half(m_, r_, core, "pair_sum", by_chip=(N_CHIPS, ncol) if k == "win" else None)
                for k, m_, r_ in zip(keys, mats, received)]
        zones = [lax.empty(s.shape, s.dtype) for s in sums]
        started = chips_start(sums, zones, _grad_views(sums), f"grads_start_{l}_{len(grads_travelling)}")
        grads_travelling[(l, keys[0])] = (keys, started)
        return started["token"]

    sse, dx, grads = local_step(x.reshape(t, D), loss_target.reshape(t, D), depth, weights_of, seq, grads_done)
    loss = lax.psum(0.5 / D * sse[0, 0], ("x", "y", "c"))

    packed = _pack_small(grads)
    small_travelling = chips_start([packed], [lax.empty((8,) + packed.shape, f32)], _whole_views([packed]),
                                   "small_start", group="all")

    big_w = dict(win=w_in, wo=w_out, wu=w_up, wd=w_down)
    acc = {k: lax.empty((depth, bw.shape[1] // 2, bw.shape[2]), f32) for k, bw in big_w.items()}
    for n, ((l, _), (keys, started)) in enumerate(grads_travelling.items()):
        sums, zones = chips_wait(started, small_travelling["token"], f"grads_wait_{l}_{n}")
        for k, s, z in zip(keys, sums, zones):
            acc[k] = chip_sum_into(acc[k], l, s, z, chip, "chip_sum")
    names = ("win", "wo", "wu", "wd")
    acc = [acc[k] for k in names]
    pair_views = dict(src=lambda ref, a, c, to: ref, dst=lambda ref, a, c, sender: ref, rows=lambda a: depth)
    to_sibling = chips_start(acc, [lax.empty(a.shape, f32) for a in acc], pair_views, "grads_from_pair_start",
                             pieces=depth, group="pair")
    own_done = {}
    for k, a in zip(names, to_sibling["sources"]):
        if big_w[k].shape[-1] % CH == 0:
            own_done[k] = adamw_half(big_w[k], a, given_m[k], given_v[k], core, "adamw_matrix",
                                     token=to_sibling["token"])

    (packed,), (small_all,) = chips_wait(small_travelling, own_done["wd"][1], "small_wait")
    small_all = lax.dynamic_update_index_in_dim(small_all, packed, 4 * lax.axis_index("x") + 2 * lax.axis_index("y") + core, 0)
    small = _unpack_small(sum_slots(small_all, f32, "small_sum", tb=8), depth)
    wa_cols, ws_cols = conv_a_w.shape[2], ssm_conv_w.shape[2]
    par_g = small["par"].reshape(depth, 3, CH)
    g_small = dict(
        norm_mix_pre=small["g1"], conv_out_norm=small["ga"], ssm_out_norm=small["gs"], norm_mix_post=small["g2"],
        norm_mlp_pre=small["g3"], norm_mlp_post=small["g4"], ssm_conv_b=small["bs"],
        conv_a_w=lax.dynamic_slice_in_dim(small["wa"].reshape(depth, 3, D), chip * wa_cols, wa_cols, axis=2),
        ssm_conv_w=lax.dynamic_slice_in_dim(small["ws"].reshape(depth, 4, XBC), chip * ws_cols, ws_cols, axis=2),
        dt_bias=par_g[:, 0, :NH], a_log=par_g[:, 1, :NH], d_skip=par_g[:, 2, :NH])

    given = dict(norm_mix_pre=(norm_mix_pre, m_norm_mix_pre, v_norm_mix_pre), w_in=(w_in, m_w_in, v_w_in),
                 conv_a_w=(conv_a_w, m_conv_a_w, v_conv_a_w), ssm_conv_w=(ssm_conv_w, m_ssm_conv_w, v_ssm_conv_w),
                 ssm_conv_b=(ssm_conv_b, m_ssm_conv_b, v_ssm_conv_b), dt_bias=(dt_bias, m_dt_bias, v_dt_bias),
                 a_log=(a_log, m_a_log, v_a_log), d_skip=(d_skip, m_d_skip, v_d_skip),
                 conv_out_norm=(conv_out_norm, m_conv_out_norm, v_conv_out_norm),
                 ssm_out_norm=(ssm_out_norm, m_ssm_out_norm, v_ssm_out_norm), w_out=(w_out, m_w_out, v_w_out),
                 norm_mix_post=(norm_mix_post, m_norm_mix_post, v_norm_mix_post),
                 norm_mlp_pre=(norm_mlp_pre, m_norm_mlp_pre, v_norm_mlp_pre), w_up=(w_up, m_w_up, v_w_up),
                 w_down=(w_down, m_w_down, v_w_down), norm_mlp_post=(norm_mlp_post, m_norm_mlp_post, v_norm_mlp_post))
    order = ["norm_mix_pre", "w_in", "conv_a_w", "ssm_conv_w", "ssm_conv_b", "dt_bias", "a_log", "d_skip",
             "conv_out_norm", "ssm_out_norm", "w_out", "norm_mix_post", "norm_mlp_pre", "w_up", "w_down",
             "norm_mlp_post"]
    short = dict(w_in="win", w_out="wo", w_up="wu", w_down="wd")
    results = {}
    for n in order:
        if n in short:
            continue
        wv, mv, vv = given[n]
        gv = g_small[n].reshape(wv.shape)
        two_d = lambda a: a.reshape(-1, a.shape[-1])
        results[n] = (gv,) + tuple(adamw(two_d(wv), two_d(gv), two_d(mv), two_d(vv), "adamw"))

    acc, from_sibling = chips_wait(to_sibling, results["norm_mlp_post"][1], "grads_from_pair_wait")
    for n, k in short.items():
        wv, mv, vv = given[n]
        own, recv = acc[names.index(k)], from_sibling[names.index(k)]
        if k in own_done:
            results[n] = adamw_half(wv, recv, mv, vv, 1 - core, "adamw_matrix", before=own_done[k])
        else:
            gv = jnp.concatenate([jnp.where(core == 0, own, recv), jnp.where(core == 0, recv, own)], axis=1)
            to_cols, to_rows = (lambda a: jnp.transpose(a, (2, 0, 1))), (lambda a: jnp.transpose(a, (1, 2, 0)))
            results[n] = (gv,) + tuple(to_rows(o) for o in adamw_leading(to_cols(wv), to_cols(gv), to_cols(mv),
                                                                         to_cols(vv), "adamw_cols"))
    g_out, d_out, m_out, v_out = [], [], [], []
    for n in order:
        wv = given[n][0]
        gv, dlt, m2, v2 = results[n]
        g_out.append(gv.reshape(wv.shape))
        d_out.append(dlt.reshape(wv.shape))
        m_out.append(m2.reshape(wv.shape))
        v_out.append(v2.reshape(wv.shape))
    return (loss, dx.reshape(nb, seq, D), *g_out, *d_out, *m_out, *v_out)
```

```python
import functools

import jax
import jax.numpy as jnp
from jax import lax
from jax.experimental import pallas as pl
from jax.experimental.pallas import tpu as pltpu

f32, bf16 = jnp.float32, jnp.bfloat16

D = 1024
NH, HP = 16, 64
NG, NS = 2, 128
CH = 128
XBC = D + 2 * NG * NS
DFF = 4 * D
IN_COLS = 3 * D + D + XBC + NH
PROJ = 5760
COL_Z, COL_XBC, COL_DT = 3 * D, 4 * D, 4 * D + XBC
EPS = 1e-6
HALO = 8
HBLK = 16
VMEM_LIMIT = 56 * 2**20
MESH = pl.DeviceIdType.MESH

LR, B1, B2, AEPS, WD, STEP = 0.001, 0.9, 0.999, 1e-08, 0.01, 10


def _cparams(n_axes):
    return pltpu.CompilerParams(dimension_semantics=("arbitrary",) * n_axes, vmem_limit_bytes=VMEM_LIMIT)


def _sds(shape, dtype):
    return jax.ShapeDtypeStruct(tuple(shape), dtype)


def _token_spec(token):
    return [] if token is None else [pl.BlockSpec(memory_space=pl.ANY)]


def _token_arg(token):
    return [] if token is None else [token]


def _rms_fwd(x, g):
    r = lax.rsqrt(jnp.mean(x * x, axis=-1, keepdims=True) + EPS)
    return x * r * g


def _rms_bwd(x, g, dy):
    r = lax.rsqrt(jnp.mean(x * x, axis=-1, keepdims=True) + EPS)
    xh = x * r
    gdy = dy * g
    dx = r * (gdy - xh * jnp.mean(xh * gdy, axis=-1, keepdims=True))
    return dx, dy * xh


def _accum(ref, part, first):
    @pl.when(first)
    def _():
        ref[...] = part

    @pl.when(jnp.logical_not(first))
    def _():
        ref[...] += part


def _dot_nt(a, b):
    return lax.dot_general(a, b, (((1,), (1,)), ((), ())), preferred_element_type=f32)


def _dot_tn(a, b):
    return lax.dot_general(a, b, (((0,), (0,)), ((), ())), preferred_element_type=f32)


def _dot(a, b):
    return jnp.dot(a, b, preferred_element_type=f32)


def _split_dot(x, e_bf, n_split, nt=False):
    acc = None
    rem = x
    for s in range(n_split):
        hi = rem.astype(bf16)
        term = _dot_nt(hi, e_bf) if nt else _dot(hi, e_bf)
        acc = term if acc is None else acc + term
        if s + 1 < n_split:
            rem = rem - hi.astype(f32)
    return acc


def _sigmoid(x):
    return 0.5 * jnp.tanh(0.5 * x) + 0.5


def norm_matmul(x, g, w, tm, tn, out_dtype, name, token=None):
    t = x.shape[0]
    if w.ndim == 3:
        assert w.shape[2] == tn
        n = w.shape[0] * tn
        w_spec = pl.BlockSpec((None, D, tn), lambda i, j: (j, 0, 0))
    else:
        n = w.shape[1]
        w_spec = pl.BlockSpec((D, tn), lambda i, j: (0, j))

    def body(x_ref, g_ref, w_ref, *rest):
        o_ref, h_ref = rest[-2:]

        @pl.when(pl.program_id(1) == 0)
        def _():
            h_ref[...] = _rms_fwd(x_ref[...], g_ref[...]).astype(bf16)

        o_ref[...] = _dot(h_ref[...], w_ref[...]).astype(out_dtype)

    return pl.pallas_call(
        body, name=name, grid=(t // tm, n // tn),
        in_specs=[pl.BlockSpec((tm, D), lambda i, j: (i, 0)), pl.BlockSpec((1, D), lambda i, j: (0, 0)), w_spec]
        + _token_spec(token),
        out_specs=[pl.BlockSpec((tm, tn), lambda i, j: (i, j)), pl.BlockSpec((tm, D), lambda i, j: (i, 0))],
        out_shape=[_sds((t, n), out_dtype), _sds((t, D), bf16)],
        compiler_params=_cparams(2))(x, g, w, *_token_arg(token))


def matmul_postnorm(a, w, xres, g, tm, relu2, name):
    t, k = a.shape

    def body(a_ref, w_ref, xr_ref, g_ref, y_ref, xo_ref):
        av = a_ref[...]
        if relu2:
            af = jnp.maximum(av.astype(f32), 0.0)
            av = (af * af).astype(bf16)
        y = _dot(av, w_ref[...])
        y_ref[...] = y.astype(bf16)
        xo_ref[...] = xr_ref[...] + _rms_fwd(y, g_ref[...])

    return pl.pallas_call(
        body, name=name, grid=(t // tm,),
        in_specs=[pl.BlockSpec((tm, k), lambda i: (i, 0)), pl.BlockSpec((k, D), lambda i: (0, 0)),
                  pl.BlockSpec((tm, D), lambda i: (i, 0)), pl.BlockSpec((1, D), lambda i: (0, 0))],
        out_specs=[pl.BlockSpec((tm, D), lambda i: (i, 0)), pl.BlockSpec((tm, D), lambda i: (i, 0))],
        out_shape=[_sds((t, D), bf16), _sds((t, D), f32)],
        compiler_params=_cparams(1))(a, w, xres, g)


def postnorm_bwd_matmul(y, g, dxo, w, fp, tm, tn, out_dtype, name, token=None):
    t, n = y.shape[0], w.shape[0]
    relu = fp is not None

    def body(*refs):
        y_ref, g_ref, dxo_ref, w_ref = refs[:4]
        fp_ref = refs[4] if relu else None
        dy_ref, dg_ref, da_ref = refs[-3:]
        i, j = pl.program_id(0), pl.program_id(1)

        @pl.when(j == 0)
        def _():
            dx, dgc = _rms_bwd(y_ref[...].astype(f32), g_ref[...], dxo_ref[...])
            dy_ref[...] = dx.astype(bf16)
            _accum(dg_ref, jnp.sum(dgc, axis=0, keepdims=True), i == 0)

        da = _dot_nt(dy_ref[...], w_ref[...])
        if relu:
            da = da * (2.0 * jnp.maximum(fp_ref[...].astype(f32), 0.0))
        da_ref[...] = da.astype(out_dtype)

    in_specs = [pl.BlockSpec((tm, D), lambda i, j: (i, 0)), pl.BlockSpec((1, D), lambda i, j: (0, 0)),
                pl.BlockSpec((tm, D), lambda i, j: (i, 0)), pl.BlockSpec((tn, D), lambda i, j: (j, 0))]
    args = [y, g, dxo, w]
    if relu:
        in_specs.append(pl.BlockSpec((tm, tn), lambda i, j: (i, j)))
        args.append(fp)
    in_specs += _token_spec(token)
    args += _token_arg(token)
    return pl.pallas_call(
        body, name=name, grid=(t // tm, n // tn), in_specs=in_specs,
        out_specs=[pl.BlockSpec((tm, D), lambda i, j: (i, 0)), pl.BlockSpec((1, D), lambda i, j: (0, 0)),
                   pl.BlockSpec((tm, tn), lambda i, j: (i, j))],
        out_shape=[_sds((t, D), bf16), _sds((1, D), f32), _sds((t, n), out_dtype)],
        compiler_params=_cparams(2))(*args)


def matmul_prenorm_bwd(da, w, x, g, dxo, tm, name, token=None):
    t, k = da.shape
    blocked = w.ndim == 3

    def body(da_ref, w_ref, x_ref, g_ref, dxo_ref, *rest):
        dx_ref, dg_ref = rest[-2:]
        if blocked:
            kc = w.shape[2]
            dh = _dot_nt(da_ref[:, 0:kc], w_ref[0])
            for q in range(1, w.shape[0]):
                dh = dh + _dot_nt(da_ref[:, q * kc:(q + 1) * kc], w_ref[q])
        else:
            dh = _dot_nt(da_ref[...], w_ref[...])
        dxn, dgc = _rms_bwd(x_ref[...], g_ref[...], dh)
        dx_ref[...] = dxo_ref[...] + dxn
        _accum(dg_ref, jnp.sum(dgc, axis=0, keepdims=True), pl.program_id(0) == 0)

    w_spec = pl.BlockSpec(w.shape, lambda i: (0,) * w.ndim, pipeline_mode=pl.Buffered(1))
    return pl.pallas_call(
        body, name=name, grid=(t // tm,),
        in_specs=[pl.BlockSpec((tm, k), lambda i: (i, 0)), w_spec,
                  pl.BlockSpec((tm, D), lambda i: (i, 0)), pl.BlockSpec((1, D), lambda i: (0, 0)),
                  pl.BlockSpec((tm, D), lambda i: (i, 0))] + _token_spec(token),
        out_specs=[pl.BlockSpec((tm, D), lambda i: (i, 0)), pl.BlockSpec((1, D), lambda i: (0, 0))],
        out_shape=[_sds((t, D), f32), _sds((1, D), f32)],
        compiler_params=_cparams(1))(da, w, x, g, dxo, *_token_arg(token))


def _resident(shape):
    return pl.BlockSpec(shape, lambda i: (0,) * len(shape), pipeline_mode=pl.Buffered(1))


def mlp_fwd(x, g_pre, wu, wd, g_post, tm, name):
    t = x.shape[0]
    nq, _, fc = wu.shape

    def body(x_ref, gp_ref, wu_ref, wd_ref, gq_ref, fp_ref, h_ref, o_ref, xo_ref):
        xv = x_ref[...]
        h = _rms_fwd(xv, gp_ref[...]).astype(bf16)
        h_ref[...] = h
        o = None
        for q in range(nq):
            fq = _dot(h, wu_ref[q])
            fp_ref[:, q * fc:(q + 1) * fc] = fq.astype(bf16)
            r = jnp.maximum(fq, 0.0)
            part = _dot((r * r).astype(bf16), wd_ref[q * fc:(q + 1) * fc, :])
            o = part if o is None else o + part
        o_ref[...] = o.astype(bf16)
        xo_ref[...] = xv + _rms_fwd(o, gq_ref[...])

    row = lambda c: pl.BlockSpec((tm, c), lambda i: (i, 0))
    vec = pl.BlockSpec((1, D), lambda i: (0, 0))
    return pl.pallas_call(
        body, name=name, grid=(t // tm,),
        in_specs=[row(D), vec, _resident(wu.shape), _resident(wd.shape), vec],
        out_specs=[row(nq * fc), row(D), row(D), row(D)],
        out_shape=[_sds((t, nq * fc), bf16), _sds((t, D), bf16), _sds((t, D), bf16), _sds((t, D), f32)],
        compiler_params=_cparams(1))(x, g_pre, wu, wd, g_post)


def mlp_bwd(o, g_post, dxo, wd, fp, wu, x, g_pre, tm, name):
    t = x.shape[0]
    nq, _, fc = wu.shape

    def body(o_ref, gq_ref, dxo_ref, wd_ref, fp_ref, wu_ref, x_ref, gp_ref, do_ref, dfp_ref, dx_ref, dgq_ref, dgp_ref):
        i = pl.program_id(0)
        dxo_v = dxo_ref[...]
        do, dgq = _rms_bwd(o_ref[...].astype(f32), gq_ref[...], dxo_v)
        do_b = do.astype(bf16)
        do_ref[...] = do_b
        dh = None
        for q in range(nq):
            cols = slice(q * fc, (q + 1) * fc)
            dq = _dot_nt(do_b, wd_ref[cols, :]) * (2.0 * jnp.maximum(fp_ref[:, cols].astype(f32), 0.0))
            dq_b = dq.astype(bf16)
            dfp_ref[:, cols] = dq_b
            part = _dot_nt(dq_b, wu_ref[q])
            dh = part if dh is None else dh + part
        dxn, dgp = _rms_bwd(x_ref[...], gp_ref[...], dh)
        dx_ref[...] = dxo_v + dxn
        _accum(dgq_ref, jnp.sum(dgq, axis=0, keepdims=True), i == 0)
        _accum(dgp_ref, jnp.sum(dgp, axis=0, keepdims=True), i == 0)

    row = lambda c: pl.BlockSpec((tm, c), lambda i: (i, 0))
    vec = pl.BlockSpec((1, D), lambda i: (0, 0))
    return pl.pallas_call(
        body, name=name, grid=(t // tm,),
        in_specs=[row(D), vec, row(D), _resident(wd.shape), row(nq * fc), _resident(wu.shape), row(D), vec],
        out_specs=[row(D), row(nq * fc), row(D), vec, vec],
        out_shape=[_sds((t, D), bf16), _sds((t, nq * fc), bf16), _sds((t, D), f32), _sds((1, D), f32), _sds((1, D), f32)],
        compiler_params=_cparams(1))(o, g_post, dxo, wd, fp, wu, x, g_pre)


def matmul_tn(a, b, tm, tn, relu2, name, col_blocks=False):
    t, m = a.shape
    n = b.shape[1]
    if col_blocks:
        out_spec, out_shape = pl.BlockSpec((None, tm, tn), lambda i, j: (j, i, 0)), _sds((n // tn, m, tn), bf16)
    else:
        out_spec, out_shape = pl.BlockSpec((tm, tn), lambda i, j: (i, j)), _sds((m, n), bf16)

    def body(a_ref, b_ref, o_ref, at_ref):
        @pl.when(pl.program_id(1) == 0)
        def _():
            av = a_ref[...]
            if relu2:
                af = jnp.maximum(av.astype(f32), 0.0)
                av = (af * af).astype(bf16)
            at_ref[...] = av.T

        o_ref[...] = _dot(at_ref[...], b_ref[...]).astype(bf16)

    return pl.pallas_call(
        body, name=name, grid=(m // tm, n // tn),
        in_specs=[pl.BlockSpec((t, tm), lambda i, j: (0, i)), pl.BlockSpec((t, tn), lambda i, j: (0, j))],
        out_specs=out_spec, out_shape=out_shape,
        scratch_shapes=[pltpu.VMEM((tm, t), bf16)],
        compiler_params=_cparams(2))(a, b)


ROWS_A = 16
ROWS_B = 32
UNROLL = 4


def _past(win, s):
    return (win if s == 0 else pltpu.roll(win, s, 0))[HALO:]


def _future(win, s):
    n = win.shape[0]
    return (win if s == 0 else pltpu.roll(win, n - s, 0))[:n - HALO]


def _fold8(v):
    return v.reshape(v.shape[0] // 8, 8, v.shape[1]).sum(axis=0)


def _last8(ref):
    return ref[...].astype(f32)[HBLK - HALO:]


def _first8(ref):
    return ref[...].astype(f32)[:HALO]


def _rd(ref, rows):
    return ref[rows, :].astype(f32)


def _halo_prev(tb, col):
    return lambda i: (jnp.maximum(i * (tb // HBLK) - 1, 0), col)


def _halo_next(tb, col, t):
    return lambda i: (jnp.minimum((i + 1) * (tb // HBLK), t // HBLK - 1), col)


def group_a_fwd(proj, wa, g, seq, tb, name):
    t = proj.shape[0]
    bps = seq // tb

    def body(xa_ref, ca_ref, ba_ref, xah_ref, cah_ref, wa_ref, g_ref, o_ref, u_scr):
        first = (pl.program_id(0) % bps) == 0
        u_scr[0:HALO, :] = jnp.where(first, 0.0, _last8(cah_ref) * _last8(xah_ref))
        w, gv = wa_ref[...], g_ref[...]

        def chunk(i, carry):
            r = pl.multiple_of(i * ROWS_A, ROWS_A)
            rows = pl.ds(r, ROWS_A)
            u_scr[pl.ds(pl.multiple_of(HALO + r, HALO), ROWS_A), :] = _rd(ca_ref, rows) * _rd(xa_ref, rows)
            win = u_scr[pl.ds(r, ROWS_A + HALO), :]
            cv = w[2:3] * _past(win, 0) + w[1:2] * _past(win, 1) + w[0:1] * _past(win, 2)
            o_ref[rows, :] = _rms_fwd(_rd(ba_ref, rows) * cv, gv).astype(bf16)
            return carry

        lax.fori_loop(0, tb // ROWS_A, chunk, 0, unroll=UNROLL)

    blk = lambda c: pl.BlockSpec((tb, D), lambda i: (i, c))
    return pl.pallas_call(
        body, name=name, grid=(t // tb,),
        in_specs=[blk(0), blk(1), blk(2),
                  pl.BlockSpec((HBLK, D), _halo_prev(tb, 0)), pl.BlockSpec((HBLK, D), _halo_prev(tb, 1)),
                  pl.BlockSpec((8, D), lambda i: (0, 0)), pl.BlockSpec((1, D), lambda i: (0, 0))],
        out_specs=pl.BlockSpec((tb, D), lambda i: (i, 0)),
        out_shape=_sds((t, 2 * D), bf16),
        scratch_shapes=[pltpu.VMEM((tb + HALO, D), f32)],
        compiler_params=_cparams(1))(proj, proj, proj, proj, proj, wa, g)


def group_a_bwd(proj, dcat, wa, g, seq, tb, name, token=None):
    t = proj.shape[0]
    bps = seq // tb

    def body(xa_ref, ca_ref, ba_ref, dy_ref, xap_ref, cap_ref, xan_ref, can_ref, ban_ref, dyn_ref, wa_ref, g_ref,
             *rest):
        dp_ref, dwa_ref, dg_ref, u_scr, d_scr, acc_scr = rest[-6:]
        i = pl.program_id(0)
        first = (i % bps) == 0
        last = (i % bps) == bps - 1
        w = wa_ref[...]
        gv = g_ref[...]
        u_scr[0:HALO, :] = jnp.where(first, 0.0, _last8(cap_ref) * _last8(xap_ref))
        u_scr[HALO + tb:2 * HALO + tb, :] = _first8(can_ref) * _first8(xan_ref)
        acc_scr[...] = jnp.zeros_like(acc_scr)

        def forward_part(n, carry):
            r = pl.multiple_of(n * ROWS_A, ROWS_A)
            rows = pl.ds(r, ROWS_A)
            ba = _rd(ba_ref, rows)
            u_scr[pl.ds(pl.multiple_of(HALO + r, HALO), ROWS_A), :] = _rd(ca_ref, rows) * _rd(xa_ref, rows)
            win = u_scr[pl.ds(r, ROWS_A + HALO), :]
            u = [_past(win, s) for s in range(3)]
            cv = w[2:3] * u[0] + w[1:2] * u[1] + w[0:1] * u[2]
            dya, dgc = _rms_bwd(ba * cv, gv, _rd(dy_ref, rows))
            dcv = dya * ba
            d_scr[rows, :] = dcv
            dp_ref[rows, 2 * D:3 * D] = (dya * cv).astype(bf16)
            acc_scr[0:8, :] += _fold8(dgc)
            for k in range(3):
                acc_scr[8 + 8 * k:16 + 8 * k, :] += _fold8(dcv * u[2 - k])
            return carry

        lax.fori_loop(0, tb // ROWS_A, forward_part, 0, unroll=UNROLL)

        start = HALO + tb
        cvn = (w[2:3] * u_scr[pl.ds(start, HALO), :] + w[1:2] * u_scr[pl.ds(start - 1, HALO), :]
               + w[0:1] * u_scr[pl.ds(start - 2, HALO), :])
        ban = _first8(ban_ref)
        dyan, _ = _rms_bwd(ban * cvn, gv, _first8(dyn_ref))
        d_scr[tb:tb + HALO, :] = jnp.where(last, 0.0, dyan * ban)

        def backward_part(n, carry):
            r = pl.multiple_of(n * ROWS_A, ROWS_A)
            rows = pl.ds(r, ROWS_A)
            win = d_scr[pl.ds(r, ROWS_A + HALO), :]
            du = w[2:3] * _future(win, 0) + w[1:2] * _future(win, 1) + w[0:1] * _future(win, 2)
            dp_ref[rows, 0:D] = (du * _rd(ca_ref, rows)).astype(bf16)
            dp_ref[rows, D:2 * D] = (du * _rd(xa_ref, rows)).astype(bf16)
            return carry

        lax.fori_loop(0, tb // ROWS_A, backward_part, 0, unroll=UNROLL)

        row = lax.broadcasted_iota(jnp.int32, (8, D), 0)
        dw = jnp.zeros((8, D), f32)
        for k in range(3):
            dw = jnp.where(row == k, jnp.sum(acc_scr[8 + 8 * k:16 + 8 * k, :], axis=0, keepdims=True), dw)
        _accum(dwa_ref, dw, i == 0)
        _accum(dg_ref, jnp.sum(acc_scr[0:8, :], axis=0, keepdims=True), i == 0)

    blk = lambda c: pl.BlockSpec((tb, D), lambda i: (i, c))
    prv = lambda c: pl.BlockSpec((HBLK, D), _halo_prev(tb, c))
    nxt = lambda c: pl.BlockSpec((HBLK, D), _halo_next(tb, c, t))
    return pl.pallas_call(
        body, name=name, grid=(t // tb,),
        in_specs=[blk(0), blk(1), blk(2), blk(0), prv(0), prv(1), nxt(0), nxt(1), nxt(2), nxt(0),
                  pl.BlockSpec((8, D), lambda i: (0, 0)), pl.BlockSpec((1, D), lambda i: (0, 0))] + _token_spec(token),
        out_specs=[pl.BlockSpec((tb, 3 * D), lambda i: (i, 0)), pl.BlockSpec((8, D), lambda i: (0, 0)),
                   pl.BlockSpec((1, D), lambda i: (0, 0))],
        out_shape=[_sds((t, PROJ), bf16), _sds((8, D), f32), _sds((1, D), f32)],
        scratch_shapes=[pltpu.VMEM((tb + 2 * HALO, D), f32), pltpu.VMEM((tb + HALO, D), f32), pltpu.VMEM((32, D), f32)],
        compiler_params=_cparams(1))(proj, proj, proj, dcat, proj, proj, proj, proj, proj, dcat, wa, g,
                                     *_token_arg(token))


CB = 512
XBC_BLK0 = COL_XBC // CB


def conv_b_fwd(proj, ws, bs, seq, tb, name):
    t = proj.shape[0]
    bps = seq // tb

    def body(x_ref, xp_ref, w_ref, b_ref, o_ref, x_scr):
        first = (pl.program_id(1) % bps) == 0
        x_scr[0:HALO, :] = jnp.where(first, 0.0, _last8(xp_ref))
        w, bias = w_ref[...], b_ref[...]

        def chunk(n, carry):
            r = pl.multiple_of(n * ROWS_B, ROWS_B)
            rows = pl.ds(r, ROWS_B)
            x_scr[pl.ds(pl.multiple_of(HALO + r, HALO), ROWS_B), :] = _rd(x_ref, rows)
            win = x_scr[pl.ds(r, ROWS_B + HALO), :]
            xc = bias + w[3:4] * _past(win, 0)
            for k in range(3):
                xc = xc + w[k:k + 1] * _past(win, 3 - k)
            o_ref[rows, :] = xc * _sigmoid(xc)
            return carry

        lax.fori_loop(0, tb // ROWS_B, chunk, 0, unroll=UNROLL)

    return pl.pallas_call(
        body, name=name, grid=(XBC // CB, t // tb),
        in_specs=[pl.BlockSpec((tb, CB), lambda j, i: (i, XBC_BLK0 + j)),
                  pl.BlockSpec((HBLK, CB), lambda j, i: (jnp.maximum(i * (tb // HBLK) - 1, 0), XBC_BLK0 + j)),
                  pl.BlockSpec((8, CB), lambda j, i: (0, j)), pl.BlockSpec((1, CB), lambda j, i: (0, j))],
        out_specs=pl.BlockSpec((tb, CB), lambda j, i: (i, j)),
        out_shape=_sds((t, XBC), f32),
        scratch_shapes=[pltpu.VMEM((tb + HALO, CB), f32)],
        compiler_params=_cparams(2))(proj, proj, ws, bs)


def conv_b_bwd(proj, dxs, ws, bs, dproj, seq, tb, name):
    t = proj.shape[0]
    bps = seq // tb

    def body(x_ref, xp_ref, xn_ref, d_ref, dn_ref, w_ref, b_ref, dproj_ref, dx_ref, dw_ref, db_ref, x_scr, d_scr,
             acc_scr):
        i = pl.program_id(1)
        first = (i % bps) == 0
        last = (i % bps) == bps - 1
        w = w_ref[...]
        bias = b_ref[...]
        x_scr[0:HALO, :] = jnp.where(first, 0.0, _last8(xp_ref))
        x_scr[HALO + tb:2 * HALO + tb, :] = _first8(xn_ref)
        acc_scr[...] = jnp.zeros_like(acc_scr)

        def dsilu(xc, d):
            sg = _sigmoid(xc)
            return d * (sg * (1.0 + xc * (1.0 - sg)))

        def forward_part(n, carry):
            r = pl.multiple_of(n * ROWS_B, ROWS_B)
            rows = pl.ds(r, ROWS_B)
            x_scr[pl.ds(pl.multiple_of(HALO + r, HALO), ROWS_B), :] = _rd(x_ref, rows)
            win = x_scr[pl.ds(r, ROWS_B + HALO), :]
            xs = [_past(win, s) for s in range(4)]
            xc = bias + w[3:4] * xs[0]
            for k in range(3):
                xc = xc + w[k:k + 1] * xs[3 - k]
            dxc = dsilu(xc, _rd(d_ref, rows))
            d_scr[rows, :] = dxc
            acc_scr[0:8, :] += _fold8(dxc)
            for k in range(4):
                acc_scr[8 + 8 * k:16 + 8 * k, :] += _fold8(dxc * xs[3 - k])
            return carry

        lax.fori_loop(0, tb // ROWS_B, forward_part, 0, unroll=UNROLL)

        start = HALO + tb
        xcn = bias + w[3:4] * x_scr[pl.ds(start, HALO), :]
        for k in range(3):
            xcn = xcn + w[k:k + 1] * x_scr[pl.ds(start - 3 + k, HALO), :]
        d_scr[tb:tb + HALO, :] = jnp.where(last, 0.0, dsilu(xcn, _first8(dn_ref)))

        def backward_part(n, carry):
            r = pl.multiple_of(n * ROWS_B, ROWS_B)
            win = d_scr[pl.ds(r, ROWS_B + HALO), :]
            dx = w[3:4] * _future(win, 0)
            for k in range(3):
                dx = dx + w[k:k + 1] * _future(win, 3 - k)
            dx_ref[pl.ds(r, ROWS_B), :] = dx.astype(bf16)
            return carry

        lax.fori_loop(0, tb // ROWS_B, backward_part, 0, unroll=UNROLL)

        row = lax.broadcasted_iota(jnp.int32, (8, CB), 0)
        dw = jnp.zeros((8, CB), f32)
        for k in range(4):
            dw = jnp.where(row == k, jnp.sum(acc_scr[8 + 8 * k:16 + 8 * k, :], axis=0, keepdims=True), dw)
        _accum(dw_ref, dw, i == 0)
        _accum(db_ref, jnp.sum(acc_scr[0:8, :], axis=0, keepdims=True), i == 0)

    nh = t // HBLK
    return pl.pallas_call(
        body, name=name, grid=(XBC // CB, t // tb),
        in_specs=[pl.BlockSpec((tb, CB), lambda j, i: (i, XBC_BLK0 + j)),
                  pl.BlockSpec((HBLK, CB), lambda j, i: (jnp.maximum(i * (tb // HBLK) - 1, 0), XBC_BLK0 + j)),
                  pl.BlockSpec((HBLK, CB), lambda j, i: (jnp.minimum((i + 1) * (tb // HBLK), nh - 1), XBC_BLK0 + j)),
                  pl.BlockSpec((tb, CB), lambda j, i: (i, j)),
                  pl.BlockSpec((HBLK, CB), lambda j, i: (jnp.minimum((i + 1) * (tb // HBLK), nh - 1), j)),
                  pl.BlockSpec((8, CB), lambda j, i: (0, j)), pl.BlockSpec((1, CB), lambda j, i: (0, j)),
                  pl.BlockSpec(memory_space=pl.ANY)],
        out_specs=[pl.BlockSpec((tb, CB), lambda j, i: (i, XBC_BLK0 + j)), pl.BlockSpec((8, CB), lambda j, i: (0, j)),
                   pl.BlockSpec((1, CB), lambda j, i: (0, j))],
        out_shape=[_sds((t, PROJ), bf16), _sds((8, XBC), f32), _sds((1, XBC), f32)],
        input_output_aliases={7: 0},
        scratch_shapes=[pltpu.VMEM((tb + 2 * HALO, CB), f32), pltpu.VMEM((tb + HALO, CB), f32),
                        pltpu.VMEM((40, CB), f32)],
        compiler_params=_cparams(2))(proj, proj, proj, dxs, dxs, ws, bs, dproj)


def place_columns(buf, part, col_block, tb, name):
    t, wdt = part.shape

    def body(p_ref, buf_ref, o_ref):
        o_ref[...] = p_ref[...]

    return pl.pallas_call(
        body, name=name, grid=(t // tb,),
        in_specs=[pl.BlockSpec((tb, wdt), lambda i: (i, 0)), pl.BlockSpec(memory_space=pl.ANY)],
        out_specs=pl.BlockSpec((tb, wdt), lambda i: (i, col_block)), out_shape=_sds(buf.shape, buf.dtype),
        input_output_aliases={1: 0}, compiler_params=_cparams(1))(part, buf)


GW = D // NG
EXPAND_TERMS = 2
REDUCE_TERMS = 1


def _ssd_consts():
    head_of_lane = jnp.arange(D) // HP
    expand = (jnp.arange(CH)[:, None] == head_of_lane[None, :]).astype(bf16)
    tri = (jnp.arange(CH)[:, None] >= jnp.arange(CH)[None, :]).astype(f32)
    return expand, tri


def _ssd_common(par_ref, dtr_ref, e_ref, tri_ref):
    par = par_ref[...]
    dtb, alog, dsk = par[0:1], par[1:2], par[2:3]
    lane = lax.broadcasted_iota(jnp.int32, (CH, CH), 1)
    a = -jnp.exp(alog)
    dtr = dtr_ref[...].astype(f32) + dtb
    sp = jnp.maximum(dtr, 0.0) + jnp.log(1.0 + jnp.exp(-jnp.abs(dtr)))
    dt = jnp.where(lane < NH, sp, 0.0)
    cs = jnp.dot(tri_ref[...], dt * a, precision=lax.Precision.HIGHEST, preferred_element_type=f32)
    cs_last = cs[CH - 1:CH, :]
    dte = jnp.exp(cs_last - cs)
    ecs = jnp.exp(cs)
    ecl = jnp.exp(cs_last)
    e = e_ref[...]
    row8 = lax.broadcasted_iota(jnp.int32, (8, CH), 0)
    r8 = _split_dot(jnp.where(row8 == 0, ecl, jnp.where(row8 == 1, dsk, 0.0)), e, 3)
    return dict(a=a, dtr=dtr, dt=dt, cs=cs, cst=cs.T, dte=dte, ecs=ecs, ecl=ecl, e=e, lane=lane,
                dt_x=_split_dot(dt, e, EXPAND_TERMS), dte_x=_split_dot(dte, e, EXPAND_TERMS),
                ecs_x=_split_dot(ecs, e, EXPAND_TERMS),
                ecl_x=r8[0:1], dsk_x=r8[1:2])


def _decay_matrix(c, h):
    li = lax.broadcasted_iota(jnp.int32, (CH, CH), 0)
    seg = c["cs"][:, h:h + 1] - c["cst"][h:h + 1, :]
    return jnp.exp(jnp.where(li >= c["lane"], seg, -jnp.inf))


def _gate_norm_fwd(y, z, gs):
    zg = z * _sigmoid(z)
    yg = y * zg
    return jnp.concatenate([_rms_fwd(yg[:, k * GW:(k + 1) * GW], gs[:, k * GW:(k + 1) * GW]) for k in range(NG)], axis=1)


def ssd_fwd(xbcs, proj, par, gs, cat, seq, name):
    t = xbcs.shape[0]
    nc = seq // CH
    expand, tri = _ssd_consts()

    def body(xs_ref, b_ref, c_ref, dtr_ref, z_ref, par_ref, e_ref, tri_ref, gs_ref, cat_ref, yn_ref, y_ref, st_ref,
             p_scr, yd_scr):
        @pl.when(pl.program_id(0) % nc == 0)
        def _():
            p_scr[...] = jnp.zeros_like(p_scr)

        c = _ssd_common(par_ref, dtr_ref, e_ref, tri_ref)
        xs = xs_ref[...]
        xdt = xs * c["dt_x"]
        xdt_b = xdt.astype(bf16)
        xdte_b = (xdt * c["dte_x"]).astype(bf16)
        p = p_scr[...]
        st_ref[0] = p
        p_b = p.astype(bf16)
        lo = c["lane"] < HP
        for g in range(NG):
            bg = b_ref[:, g * NS:(g + 1) * NS].astype(bf16)
            cg = c_ref[:, g * NS:(g + 1) * NS].astype(bf16)
            gmat = _dot_nt(cg, bg)
            for q in range(GW // CH):
                col = g * GW + q * CH
                xp = xdt_b[:, col:col + CH]
                h0 = col // HP
                m0 = (gmat * _decay_matrix(c, h0)).astype(bf16)
                m1 = (gmat * _decay_matrix(c, h0 + 1)).astype(bf16)
                stacked = jnp.concatenate([jnp.where(lo, xp, jnp.zeros_like(xp)),
                                           jnp.where(lo, jnp.zeros_like(xp), xp)], axis=0)
                yd_scr[:, col:col + CH] = _dot(jnp.concatenate([m0, m1], axis=1), stacked)
            gsl = slice(g * GW, (g + 1) * GW)
            yoff = _dot(cg, p_b[:, gsl]) * c["ecs_x"][:, gsl]
            yd_scr[:, gsl] = yd_scr[:, gsl] + yoff
            p_scr[:, gsl] = p[:, gsl] * c["ecl_x"][:, gsl] + _dot_tn(bg, xdte_b[:, gsl])
        y = yd_scr[...] + c["dsk_x"] * xs
        y_ref[...] = y
        yn_ref[...] = _gate_norm_fwd(y, z_ref[...].astype(f32), gs_ref[...]).astype(bf16)

    nb = t // CH
    return pl.pallas_call(
        body, name=name, grid=(nb,),
        in_specs=[pl.BlockSpec((CH, D), lambda i: (i, 0)),
                  pl.BlockSpec((CH, NG * NS), lambda i: (i, D // (NG * NS))),
                  pl.BlockSpec((CH, NG * NS), lambda i: (i, D // (NG * NS) + 1)),
                  pl.BlockSpec((CH, CH), lambda i: (i, COL_DT // CH)),
                  pl.BlockSpec((CH, D), lambda i: (i, COL_Z // D)),
                  pl.BlockSpec((8, CH), lambda i: (0, 0)), pl.BlockSpec((CH, D), lambda i: (0, 0)),
                  pl.BlockSpec((CH, CH), lambda i: (0, 0)), pl.BlockSpec((1, D), lambda i: (0, 0)),
                  pl.BlockSpec(memory_space=pl.ANY)],
        out_specs=[pl.BlockSpec((CH, D), lambda i: (i, 1)), pl.BlockSpec((CH, D), lambda i: (i, 0)),
                   pl.BlockSpec((1, NS, D), lambda i: (i, 0, 0))],
        out_shape=[_sds((t, 2 * D), bf16), _sds((t, D), f32), _sds((nb, NS, D), f32)],
        input_output_aliases={9: 0},
        scratch_shapes=[pltpu.VMEM((NS, D), f32), pltpu.VMEM((CH, D), f32)],
        compiler_params=_cparams(1))(xbcs, xbcs, xbcs, proj, proj, par, expand, tri, gs, cat)


def ssd_bwd(xbcs, proj, ypre, states, dcat, par, gs, dproj, seq, name):
    t = xbcs.shape[0]
    nc = seq // CH
    expand, tri = _ssd_consts()

    def body(xs_ref, b_ref, c_ref, dtr_ref, z_ref, y_ref, st_ref, dyn_ref, par_ref, e_ref, tri_ref, gs_ref, dproj_ref,
             dx_ref, dz_ref, ddt_ref, dpar_ref, dgs_ref, dp_scr, dxdt_scr):
        i = pl.program_id(0)

        @pl.when(i % nc == 0)
        def _():
            dp_scr[...] = jnp.zeros_like(dp_scr)

        c = _ssd_common(par_ref, dtr_ref, e_ref, tri_ref)
        e = c["e"]
        lane = c["lane"]
        sub = lax.broadcasted_iota(jnp.int32, (CH, CH), 0)
        xs = xs_ref[...]
        xdt = xs * c["dt_x"]
        xdt_b = xdt.astype(bf16)
        xdte_b = (xdt * c["dte_x"]).astype(bf16)
        p = st_ref[0]
        p_b = p.astype(bf16)
        dpn = dp_scr[...]
        dpn_b = dpn.astype(bf16)

        y, z, gs_v = y_ref[...], z_ref[...].astype(f32), gs_ref[...]
        zs = _sigmoid(z)
        zg = z * zs
        yg = y * zg
        parts, gparts = [], []
        for k in range(NG):
            sl = slice(k * GW, (k + 1) * GW)
            dxk, dgk = _rms_bwd(yg[:, sl], gs_v[:, sl], dyn_ref[:, sl].astype(f32))
            parts.append(dxk)
            gparts.append(dgk)
        dyg = jnp.concatenate(parts, axis=1)
        dgs_rows = jnp.concatenate(gparts, axis=1)
        dy = dyg * zg
        dz_ref[...] = (dyg * y * (zs * (1.0 + z * (1.0 - zs)))).astype(bf16)
        dy_b = dy.astype(bf16)
        dq_b = (dy * c["ecs_x"]).astype(bf16)

        lo = lane < HP
        dcs = jnp.zeros((CH, CH), f32)
        dcst = jnp.zeros((CH, CH), f32)
        for g in range(NG):
            gsl = slice(g * GW, (g + 1) * GW)
            bg = b_ref[:, g * NS:(g + 1) * NS].astype(bf16)
            cg = c_ref[:, g * NS:(g + 1) * NS].astype(bf16)
            gmat = _dot_nt(cg, bg)
            dgm = jnp.zeros((CH, CH), f32)
            for q in range(GW // CH):
                col = g * GW + q * CH
                xp = xdt_b[:, col:col + CH]
                dyp = dy_b[:, col:col + CH]
                zero = jnp.zeros_like(dyp)
                xp2 = jnp.concatenate([jnp.where(lo, xp, zero), jnp.where(lo, zero, xp)], axis=0)
                dy2 = jnp.concatenate([jnp.where(lo, dyp, zero), jnp.where(lo, zero, dyp)], axis=0)
                dm2 = _dot_nt(dyp, xp2)
                ms = []
                for hh in range(2):
                    h = col // HP + hh
                    dec = _decay_matrix(c, h)
                    m = gmat * dec
                    dm = dm2[:, hh * CH:(hh + 1) * CH]
                    dseg = dm * m
                    dcs = dcs + jnp.where(lane == h, jnp.sum(dseg, axis=1, keepdims=True), 0.0)
                    dcst = dcst + jnp.where(sub == h, jnp.sum(dseg, axis=0, keepdims=True), 0.0)
                    dgm = dgm + dm * dec
                    ms.append(m.astype(bf16))
                dxdt_scr[:, col:col + CH] = _dot_tn(jnp.concatenate(ms, axis=0), dy2)
            dgm_b = dgm.astype(bf16)
            bds = _dot(bg, dpn_b[:, gsl])
            dxdt_scr[:, gsl] = dxdt_scr[:, gsl] + c["dte_x"][:, gsl] * bds
            dc_g = _dot(dgm_b, bg) + _dot_nt(dq_b[:, gsl], p_b[:, gsl])
            db_g = _dot_tn(dgm_b, cg) + _dot_nt(xdte_b[:, gsl], dpn_b[:, gsl])
            dx_ref[:, D + g * NS:D + (g + 1) * NS] = db_g
            dx_ref[:, D + NG * NS + g * NS:D + NG * NS + (g + 1) * NS] = dc_g
            dp_scr[:, gsl] = dpn[:, gsl] * c["ecl_x"][:, gsl] + _dot_tn(cg, dq_b[:, gsl])
            q_g = _dot(cg, p_b[:, gsl])
            e_g = e[:, gsl]
            dcs = dcs + c["ecs"] * _split_dot(dy[:, gsl] * q_g, e_g, REDUCE_TERMS, nt=True)
            ddte = _split_dot(xdt[:, gsl] * bds, e_g, REDUCE_TERMS, nt=True) * c["dte"]
            dcs = dcs - ddte
            dcs = dcs + jnp.where(sub == CH - 1, jnp.sum(ddte, axis=0, keepdims=True), 0.0)

        decl = _split_dot(jnp.broadcast_to(jnp.sum(dpn * p, axis=0, keepdims=True), (8, D)), e, 2, nt=True)[0:1]
        dcs = dcs + jnp.where(sub == CH - 1, c["ecl"] * decl, 0.0)
        dcs = dcs - dcst.T
        dadt = lax.dot_general(tri_ref[...], dcs, (((0,), (0,)), ((), ())), precision=lax.Precision.HIGHEST,
                               preferred_element_type=f32)
        dxdt = dxdt_scr[...]
        ddt = dadt * c["a"] + _split_dot(dxdt * xs, e, REDUCE_TERMS, nt=True)
        ddtr = jnp.where(lane < NH, ddt * _sigmoid(c["dtr"]), 0.0)
        ddt_ref[...] = ddtr.astype(bf16)
        dx_ref[:, 0:D] = dxdt * c["dt_x"] + c["dsk_x"] * dy
        dsk = _split_dot(jnp.broadcast_to(jnp.sum(dy * xs, axis=0, keepdims=True), (8, D)), e, 2, nt=True)[0:1]
        dalog = jnp.sum(dadt * c["dt"], axis=0, keepdims=True) * c["a"]
        row8 = lax.broadcasted_iota(jnp.int32, (8, CH), 0)
        dpar = jnp.where(row8 == 0, jnp.sum(ddtr, axis=0, keepdims=True),
                         jnp.where(row8 == 1, dalog, jnp.where(row8 == 2, dsk, 0.0)))
        dpar = jnp.where(lax.broadcasted_iota(jnp.int32, (8, CH), 1) < NH, dpar, 0.0)
        _accum(dpar_ref, dpar, i == 0)
        _accum(dgs_ref, jnp.sum(dgs_rows, axis=0, keepdims=True), i == 0)

    nb = t // CH
    rev = lambda i: (i // nc) * nc + (nc - 1 - i % nc)
    return pl.pallas_call(
        body, name=name, grid=(nb,),
        in_specs=[pl.BlockSpec((CH, D), lambda i: (rev(i), 0)),
                  pl.BlockSpec((CH, NG * NS), lambda i: (rev(i), D // (NG * NS))),
                  pl.BlockSpec((CH, NG * NS), lambda i: (rev(i), D // (NG * NS) + 1)),
                  pl.BlockSpec((CH, CH), lambda i: (rev(i), COL_DT // CH)),
                  pl.BlockSpec((CH, D), lambda i: (rev(i), COL_Z // D)),
                  pl.BlockSpec((CH, D), lambda i: (rev(i), 0)),
                  pl.BlockSpec((1, NS, D), lambda i: (rev(i), 0, 0)),
                  pl.BlockSpec((CH, D), lambda i: (rev(i), 1)),
                  pl.BlockSpec((8, CH), lambda i: (0, 0)), pl.BlockSpec((CH, D), lambda i: (0, 0)),
                  pl.BlockSpec((CH, CH), lambda i: (0, 0)), pl.BlockSpec((1, D), lambda i: (0, 0)),
                  pl.BlockSpec(memory_space=pl.ANY)],
        out_specs=[pl.BlockSpec((CH, XBC), lambda i: (rev(i), 0)), pl.BlockSpec((CH, D), lambda i: (rev(i), COL_Z // D)),
                   pl.BlockSpec((CH, CH), lambda i: (rev(i), 0)),
                   pl.BlockSpec((8, CH), lambda i: (0, 0)), pl.BlockSpec((1, D), lambda i: (0, 0))],
        out_shape=[_sds((t, XBC), f32), _sds((t, PROJ), bf16), _sds((t, CH), bf16), _sds((8, CH), f32), _sds((1, D), f32)],
        input_output_aliases={12: 1},
        scratch_shapes=[pltpu.VMEM((NS, D), f32), pltpu.VMEM((CH, D), f32)],
        compiler_params=_cparams(1))(xbcs, xbcs, xbcs, proj, proj, ypre, states, dcat, par, expand, tri, gs, dproj)


def loss_head(y, target, tb, name):
    t = y.shape[0]

    def body(y_ref, t_ref, s_ref, dy_ref):
        err = y_ref[...] - t_ref[...]
        dy_ref[...] = err * (1.0 / D)
        _accum(s_ref, jnp.zeros((8, CH), f32) + jnp.sum(err * err), pl.program_id(0) == 0)

    return pl.pallas_call(
        body, name=name, grid=(t // tb,),
        in_specs=[pl.BlockSpec((tb, D), lambda i: (i, 0)), pl.BlockSpec((tb, D), lambda i: (i, 0))],
        out_specs=[pl.BlockSpec((8, CH), lambda i: (0, 0)), pl.BlockSpec((tb, D), lambda i: (i, 0))],
        out_shape=[_sds((8, CH), f32), _sds((t, D), f32)],
        compiler_params=_cparams(1))(y, target)


def _tiles(t, seq):
    tm = min(512, t)
    return dict(tm=tm, tm_small=min(256, t), tm_large=min(1024, t), tm_huge=min(2048, t), tb=min(512, seq))


def local_step(x, target, depth, weights_of, seq, grads_done=None):
    t = x.shape[0]
    ts = _tiles(t, seq)
    tm, tl, th, tb = ts["tm"], ts["tm_large"], ts["tm_huge"], ts["tb"]
    saved, ws = [], []
    for l in range(depth):
        w = weights_of(l, x)
        ws.append(w)
        proj, h1 = norm_matmul(x, w["g1"], w["win"], th, 1152, bf16, "in_proj", token=w.get("token"))
        cat = group_a_fwd(proj, w["wa"], w["ga"], seq, tb, "group_a_fwd")
        xbcs = conv_b_fwd(proj, w["ws"], w["bs"], seq, tb, "conv_b_fwd")
        cat, ypre, states = ssd_fwd(xbcs, proj, w["par"], w["gs"], cat, seq, "ssd_fwd")
        if "late" in w:
            w.update(w.pop("late")(cat))
        mix, x2 = matmul_postnorm(cat, w["wo"], x, w["g2"], tl, False, "out_proj")
        fp, h2, o, x3 = mlp_fwd(x2, w["g3"], w["wu"], w["wd"], w["g4"], tm, "mlp_fwd")
        saved.append(dict(x=x, proj=proj, h1=h1, xbcs=xbcs, ypre=ypre, states=states, cat=cat, mix=mix, x2=x2,
                          fp=fp, h2=h2, o=o))
        x = x3
    sse, dx = loss_head(x, target, tm, "loss_head")
    grads = [None] * depth
    for l in reversed(range(depth)):
        s, w = saved[l], ws[l]
        do, dfp, dx2, dg4, dg3 = mlp_bwd(s["o"], w["g4"], dx, w["wd"], s["fp"], w["wu"], s["x2"], w["g3"],
                                         ts["tm_small"], "mlp_bwd")
        dwd = matmul_tn(s["fp"], do, 512, 1024, True, "mlp_down_dw")
        dwu = matmul_tn(s["h2"], dfp, tl, 1024, False, "mlp_up_dw", col_blocks=True)
        dmix, dg2, dcat = postnorm_bwd_matmul(s["mix"], w["g2"], dx2, w["wo"], None, tl, 1024, bf16, "out_proj_bwd")
        dwo = matmul_tn(s["cat"], dmix, 512, 1024, False, "out_proj_dw")
        token = None if grads_done is None else grads_done(l, dict(wo=dwo, wu=dwu, wd=dwd), False)
        dproj, dwa, dga = group_a_bwd(s["proj"], dcat, w["wa"], w["ga"], seq, tb, "group_a_bwd", token=token)
        dxbcs, dproj, ddt, dpar, dgs = ssd_bwd(s["xbcs"], s["proj"], s["ypre"], s["states"], dcat, w["par"], w["gs"],
                                               dproj, seq, "ssd_bwd")
        dproj, dws, dbs = conv_b_bwd(s["proj"], dxbcs, w["ws"], w["bs"], dproj, seq, tb, "conv_b_bwd")
        dproj = place_columns(dproj, ddt, COL_DT // CH, tm, "place_ddt")
        dwin = matmul_tn(s["h1"], dproj, tl, 1152, False, "in_proj_dw")
        token = None if grads_done is None else grads_done(l, dict(win=dwin), True)
        dx, dg1 = matmul_prenorm_bwd(dproj, w["win"], s["x"], w["g1"], dx2, tm, "in_proj_bwd", token=token)
        grads[l] = dict(win=dwin, wo=dwo, wu=dwu, wd=dwd, wa=dwa, ws=dws, bs=dbs, par=dpar,
                        g1=dg1, ga=dga, gs=dgs, g2=dg2, g3=dg3, g4=dg4)
    return sse, dx, grads


GROUPS = {
    "chips": [(1, 0, 0), (0, 1, 0), (1, 1, 0)],
    "pair": [(0, 0, 1)],
    "all": [(1, 0, 0), (0, 1, 0), (1, 1, 0), (0, 0, 1), (1, 0, 1), (0, 1, 1), (1, 1, 1)],
}


def _group_index(group, x, y, c):
    return {"chips": 2 * x + y, "pair": c, "all": 4 * x + 2 * y + c}[group]


def _chunk_indices(shape, pieces):
    if len(shape) < 3:
        return [()]
    lead = [()]
    for n in shape[:-2]:
        lead = [i + (k,) for i in lead for k in range(n)]
    rows = shape[-2]
    split = max(1, pieces // len(lead))
    while split > 1 and (rows % split or (rows // split) % 16):
        split -= 1
    step = rows // split
    return [i + (pl.ds(s * step, step),) for i in lead for s in range(split)]


def _exchange(arrays, out_shapes, group, src_view, dst_view, view_shape, name, own, pieces=16):
    masks = GROUPS[group]
    na, nm = len(arrays), len(masks)
    cuts = [_chunk_indices(view_shape(a), pieces) for a in range(na)]

    def body(*refs):
        ins, outs = refs[:na], refs[na:2 * na]
        send_sems, recv_sems = refs[2 * na:2 * na + 2]
        local_sems = refs[2 * na + 2] if own else None
        x, y, c = lax.axis_index("x"), lax.axis_index("y"), lax.axis_index("c")
        me = _group_index(group, x, y, c)
        peers = []
        for mx, my, mc in masks:
            px, py, pc = (1 - x if mx else x), (1 - y if my else y), (1 - c if mc else c)
            peers.append(((px, py, pc), _group_index(group, px, py, pc)))

        def part(ref, idx):
            return ref.at[idx] if idx else ref

        if own:
            for a in range(na):
                for idx in cuts[a]:
                    pltpu.make_async_copy(part(src_view(ins[a], a, me), idx), part(dst_view(outs[a], a, me), idx),
                                          local_sems.at[a]).start()
        for a in range(na):
            for j, (dev, pidx) in enumerate(peers):
                for idx in cuts[a]:
                    pltpu.make_async_remote_copy(
                        src_ref=part(src_view(ins[a], a, pidx), idx), dst_ref=part(dst_view(outs[a], a, me), idx),
                        send_sem=send_sems.at[a * nm + j], recv_sem=recv_sems.at[a * nm + j],
                        device_id=dev, device_id_type=MESH).start()
        whole = []
        for a in range(na):
            for j, (dev, pidx) in enumerate(peers):
                whole.append(pltpu.make_async_remote_copy(
                    src_ref=src_view(ins[a], a, pidx), dst_ref=dst_view(outs[a], a, pidx),
                    send_sem=send_sems.at[a * nm + j], recv_sem=recv_sems.at[a * nm + j],
                    device_id=dev, device_id_type=MESH))
        for cp in whole:
            cp.wait_recv()
        for cp in whole:
            cp.wait_send()
        if own:
            for a in range(na):
                pltpu.make_async_copy(src_view(ins[a], a, me), dst_view(outs[a], a, me), local_sems.at[a]).wait()

    hbm = pl.BlockSpec(memory_space=pltpu.HBM)
    sems = [pltpu.SemaphoreType.DMA((na * nm,)), pltpu.SemaphoreType.DMA((na * nm,))]
    return pl.pallas_call(
        body, name=name, in_specs=[hbm] * na, out_specs=[hbm] * na,
        out_shape=[_sds(s, a.dtype) for s, a in zip(out_shapes, arrays)],
        scratch_shapes=sems + ([pltpu.SemaphoreType.DMA((na,))] if own else []))(*arrays)


def all_gather(arrays, group, name, slot_axis=0, own=True):
    n = len(GROUPS[group]) + 1
    shapes = [a.shape[:slot_axis] + (n,) + a.shape[slot_axis:] for a in arrays]
    lead = (slice(None),) * slot_axis
    return _exchange(arrays, shapes, group, lambda r, a, i: r, lambda r, a, i: r.at[lead + (i,)],
                     lambda a: arrays[a].shape, name, own)


HBM_SPEC = pl.BlockSpec(memory_space=pltpu.HBM)
SEM_SPEC = pl.BlockSpec(memory_space=pltpu.SEMAPHORE)
DATAFLOW = pltpu.SideEffectType.DATAFLOW_SIDE_EFFECTING
N_CHIPS = 4


def _peers(group, x, y, c):
    out = []
    for mx, my, mc in GROUPS[group]:
        px, py, pc = (1 - x if mx else x), (1 - y if my else y), (1 - c if mc else c)
        out.append(((px, py, pc), _group_index(group, px, py, pc)))
    return out


def _whole_views(sources):
    return dict(src=lambda ref, a, c, to: ref, dst=lambda ref, a, c, sender: ref.at[sender],
                rows=lambda a: sources[a].shape[0])


def _weight_views(shards):
    half = [s.shape[0] // 2 for s in shards]
    return dict(src=lambda ref, a, c, to_chip: ref.at[pl.ds(c * half[a], half[a])],
                dst=lambda ref, a, c, from_chip: ref.at[from_chip, pl.ds(c * half[a], half[a])],
                rows=lambda a: half[a])


def _grad_views(sums):
    return dict(src=lambda ref, a, c, to_chip: ref.at[to_chip], dst=lambda ref, a, c, from_chip: ref.at[from_chip],
                rows=lambda a: sums[a].shape[1])


def chips_start(sources, zones, views, name, pieces=4, after=None, group="chips"):
    na, nm = len(sources), len(GROUPS[group])

    def body(*refs):
        ins, lands = refs[:na], refs[na:2 * na]
        n_in = 2 * na + len(_token_arg(after))
        send_sems, recv_sems, token = refs[n_in], refs[n_in + 1], refs[-1]
        x, y, c = lax.axis_index("x"), lax.axis_index("y"), lax.axis_index("c")
        me = _group_index(group, x, y, c)
        for a in range(na):
            step = views["rows"](a) // pieces
            for j, (dev, to) in enumerate(_peers(group, x, y, c)):
                for q in range(pieces):
                    rows = pl.ds(q * step, step)
                    pltpu.make_async_remote_copy(
                        src_ref=views["src"](ins[a], a, c, to).at[rows],
                        dst_ref=views["dst"](lands[a], a, c, me).at[rows],
                        send_sem=send_sems.at[a * nm + j], recv_sem=recv_sems.at[a * nm + j],
                        device_id=dev, device_id_type=MESH).start()
        token[...] = jnp.zeros_like(token)

    both = list(sources) + list(zones)
    outs = pl.pallas_call(
        body, name=name,
        out_shape=(pltpu.SemaphoreType.DMA((na * nm,)), pltpu.SemaphoreType.DMA((na * nm,)),
                   *[pltpu.HBM(b.shape, b.dtype) for b in both], _sds((8, CH), f32)),
        in_specs=[HBM_SPEC] * (2 * na) + _token_spec(after),
        out_specs=(SEM_SPEC, SEM_SPEC, *[HBM_SPEC] * (2 * na), pl.BlockSpec(memory_space=pltpu.VMEM)),
        input_output_aliases={i: 2 + i for i in range(2 * na)},
        compiler_params=pltpu.CompilerParams(has_side_effects=DATAFLOW))(
            *[pltpu.with_memory_space_constraint(b, pltpu.HBM) for b in both], *_token_arg(after))
    return dict(send=outs[0], recv=outs[1], sources=list(outs[2:2 + na]), zones=list(outs[2 + na:2 + 2 * na]),
                token=outs[-1], views=views, group=group)


def chips_wait(started, after, name):
    sources, zones, views, group = started["sources"], started["zones"], started["views"], started["group"]
    na, nm = len(sources), len(GROUPS[group])

    def body(*refs):
        ins, lands = refs[:na], refs[na:2 * na]
        send_sems, recv_sems = refs[2 * na], refs[2 * na + 1]
        x, y, c = lax.axis_index("x"), lax.axis_index("y"), lax.axis_index("c")
        for a in range(na):
            for j, (dev, peer) in enumerate(_peers(group, x, y, c)):
                cp = pltpu.make_async_remote_copy(
                    src_ref=views["src"](ins[a], a, c, peer), dst_ref=views["dst"](lands[a], a, c, peer),
                    send_sem=send_sems.at[a * nm + j], recv_sem=recv_sems.at[a * nm + j],
                    device_id=dev, device_id_type=MESH)
                cp.wait_send()
                cp.wait_recv()

    both = list(sources) + list(zones)
    outs = pl.pallas_call(
        body, name=name, out_shape=tuple(pltpu.HBM(b.shape, b.dtype) for b in both),
        in_specs=[HBM_SPEC] * (2 * na) + [SEM_SPEC, SEM_SPEC, pl.BlockSpec(memory_space=pl.ANY)],
        out_specs=tuple([HBM_SPEC] * (2 * na)), input_output_aliases={i: i for i in range(2 * na)},
        compiler_params=pltpu.CompilerParams(has_side_effects=DATAFLOW))(*both, started["send"], started["recv"], after)
    return list(outs[:na]), list(outs[na:])


def weights_share(zones, name):
    na, nm = len(zones), N_CHIPS - 1

    def body(*refs):
        lands = refs[na:2 * na]
        send_sems, recv_sems = refs[2 * na:]
        x, y, c = lax.axis_index("x"), lax.axis_index("y"), lax.axis_index("c")
        chip = 2 * x + y
        sibling = (x, y, 1 - c)
        sends = []
        for a in range(na):
            half = zones[a].shape[1] // 2
            for m in range(1, N_CHIPS):
                mine = lands[a].at[chip ^ m, pl.ds(c * half, half)]
                sends.append(pltpu.make_async_remote_copy(
                    src_ref=mine, dst_ref=mine, send_sem=send_sems.at[a * nm + m - 1],
                    recv_sem=recv_sems.at[a * nm + m - 1], device_id=sibling, device_id_type=MESH))
        for cp in sends:
            cp.start()
        for a in range(na):
            half = zones[a].shape[1] // 2
            for m in range(1, N_CHIPS):
                theirs = lands[a].at[chip ^ m, pl.ds((1 - c) * half, half)]
                pltpu.make_async_remote_copy(
                    src_ref=theirs, dst_ref=theirs, send_sem=send_sems.at[a * nm + m - 1],
                    recv_sem=recv_sems.at[a * nm + m - 1], device_id=sibling, device_id_type=MESH).wait_recv()
        for cp in sends:
            cp.wait_send()

    return pl.pallas_call(
        body, name=name, in_specs=[HBM_SPEC] * na, out_specs=[HBM_SPEC] * na,
        out_shape=[_sds(z.shape, z.dtype) for z in zones], input_output_aliases={i: i for i in range(na)},
        scratch_shapes=[pltpu.SemaphoreType.DMA((na * nm,)), pltpu.SemaphoreType.DMA((na * nm,))])(*zones)


def pair_send_halves(grads, name):
    half = [g.shape[1] // 2 for g in grads]
    shapes = [(g.shape[0], h, g.shape[2]) for g, h in zip(grads, half)]
    return _exchange(grads, shapes, "pair", lambda r, a, i: r.at[:, pl.ds(i * half[a], half[a])],
                     lambda r, a, i: r, lambda a: shapes[a], name, False)


def sum_pair_half(g, recv, core, name, tb=256, by_chip=None):
    nk, r, c = g.shape
    tb = min(tb, r // 2)
    nb = r // 2 // tb

    def body(core_ref, g_ref, r_ref, o_ref):
        s = g_ref[...].astype(f32) + r_ref[...].astype(f32)
        if by_chip is None:
            o_ref[...] = s.astype(bf16)
        else:
            for k in range(by_chip[0]):
                o_ref[k] = s[:, k * by_chip[1]:(k + 1) * by_chip[1]].astype(bf16)

    if by_chip is None:
        out_spec = pl.BlockSpec((None, tb, c), lambda k, i, core_ref: (k, i, 0))
        out_shape = _sds((nk, r // 2, c), bf16)
    else:
        assert nk == 1
        out_spec = pl.BlockSpec((by_chip[0], tb, by_chip[1]), lambda k, i, core_ref: (0, i, 0))
        out_shape = _sds((by_chip[0], r // 2, by_chip[1]), bf16)
    return pl.pallas_call(
        body, name=name,
        grid_spec=pltpu.PrefetchScalarGridSpec(
            num_scalar_prefetch=1, grid=(nk, nb),
            in_specs=[pl.BlockSpec((None, tb, c), lambda k, i, core_ref: (k, core_ref[0] * nb + i, 0)),
                      pl.BlockSpec((None, tb, c), lambda k, i, core_ref: (k, i, 0))],
            out_specs=out_spec),
        out_shape=out_shape, compiler_params=_cparams(2))(jnp.reshape(core, (1,)).astype(jnp.int32), g, recv)


def assemble_columns(blocks, width, name, tb=256):
    n, r, c = blocks.shape

    def body(b_ref, o_ref):
        for k in range(n):
            o_ref[:, k * c:(k + 1) * c] = b_ref[k]
        o_ref[:, n * c:] = jnp.zeros((tb, width - n * c), blocks.dtype)

    return pl.pallas_call(
        body, name=name, grid=(r // tb,), in_specs=[pl.BlockSpec((n, tb, c), lambda i: (0, i, 0))],
        out_specs=pl.BlockSpec((tb, width), lambda i: (i, 0)), out_shape=_sds((r, width), blocks.dtype),
        compiler_params=_cparams(1))(blocks)


def chip_sum_into(acc, layer, own, others, chip, name, tb=256):
    n, r, c = own.shape
    tb = min(tb, r)

    def body(chip_ref, x_ref, y1_ref, y2_ref, y3_ref, acc_ref, o_ref):
        o_ref[...] = ((x_ref[...].astype(f32) + y1_ref[...].astype(f32)) + y2_ref[...].astype(f32)) + y3_ref[...].astype(f32)

    def slot(k):
        return pl.BlockSpec((None, tb, c), lambda i, chip_ref: (chip_ref[0] ^ k, i, 0))

    return pl.pallas_call(
        body, name=name,
        grid_spec=pltpu.PrefetchScalarGridSpec(
            num_scalar_prefetch=1, grid=(r // tb,),
            in_specs=[slot(k) for k in range(n)] + [pl.BlockSpec(memory_space=pl.ANY)],
            out_specs=pl.BlockSpec((None, tb, c), lambda i, chip_ref: (layer, i, 0))),
        out_shape=_sds(acc.shape, f32), input_output_aliases={n + 1: 0}, compiler_params=_cparams(1))(
            jnp.reshape(chip, (1,)).astype(jnp.int32), own, *([others] * (n - 1)), acc)


def adamw_half(w, g_half, m, v, half, name, before=None, token=None, tb=256):
    depth, r, c = w.shape
    tb = min(tb, r // 2)
    nb = r // 2 // tb
    n_extra = (0 if before is None else 4) + len(_token_arg(token))

    def body(half_ref, w_ref, gh_ref, m_ref, v_ref, *rest):
        g_ref, d_ref, mo_ref, vo_ref = rest[n_extra:]
        gv = gh_ref[...]
        m2 = B1 * m_ref[...] + (1.0 - B1) * gv
        v2 = B2 * v_ref[...] + (1.0 - B2) * (gv * gv)
        m_hat = m2 / (1.0 - B1 ** STEP)
        v_hat = v2 / (1.0 - B2 ** STEP)
        g_ref[...] = gv
        d_ref[...] = -LR * (m_hat / (jnp.sqrt(v_hat) + AEPS) + WD * w_ref[...])
        mo_ref[...] = m2
        vo_ref[...] = v2

    whole = pl.BlockSpec((None, tb, c), lambda l, i, half_ref: (l, half_ref[0] * nb + i, 0))
    part = pl.BlockSpec((None, tb, c), lambda l, i, half_ref: (l, i, 0))
    extra = ([] if before is None else list(before)) + _token_arg(token)
    return pl.pallas_call(
        body, name=name,
        grid_spec=pltpu.PrefetchScalarGridSpec(
            num_scalar_prefetch=1, grid=(depth, nb),
            in_specs=[whole, part, whole, whole] + [pl.BlockSpec(memory_space=pl.ANY)] * n_extra, out_specs=[whole] * 4),
        out_shape=[_sds(w.shape, f32)] * 4,
        input_output_aliases={} if before is None else {5 + k: k for k in range(4)},
        compiler_params=_cparams(2))(jnp.reshape(half, (1,)).astype(jnp.int32), w, g_half, m, v, *extra)


def sum_slots(y, out_dtype, name, tb=256):
    n, r, c = y.shape
    tb = min(tb, r)

    def body(y_ref, o_ref):
        acc = y_ref[0].astype(f32)
        for i in range(1, n):
            acc = acc + y_ref[i].astype(f32)
        o_ref[...] = acc.astype(out_dtype)

    return pl.pallas_call(
        body, name=name, grid=(r // tb,),
        in_specs=[pl.BlockSpec((n, tb, c), lambda i: (0, i, 0))], out_specs=pl.BlockSpec((tb, c), lambda i: (i, 0)),
        out_shape=_sds((r, c), out_dtype), compiler_params=_cparams(1))(y)


def adamw(w, g, m, v, name, tb=256):
    r, c = w.shape
    tb = min(tb, r)

    def body(w_ref, g_ref, m_ref, v_ref, d_ref, mo_ref, vo_ref):
        gv = g_ref[...]
        m2 = B1 * m_ref[...] + (1.0 - B1) * gv
        v2 = B2 * v_ref[...] + (1.0 - B2) * (gv * gv)
        m_hat = m2 / (1.0 - B1 ** STEP)
        v_hat = v2 / (1.0 - B2 ** STEP)
        d_ref[...] = -LR * (m_hat / (jnp.sqrt(v_hat) + AEPS) + WD * w_ref[...])
        mo_ref[...] = m2
        vo_ref[...] = v2

    spec = pl.BlockSpec((tb, c), lambda i: (i, 0))
    return pl.pallas_call(
        body, name=name, grid=(r // tb,), in_specs=[spec] * 4, out_specs=[spec] * 3,
        out_shape=[_sds((r, c), f32)] * 3, compiler_params=_cparams(1))(w, g, m, v)


def adamw_leading(w, g, m, v, name, tc=64):
    c, l, r = w.shape
    main = c // tc
    tail = c - main * tc

    def body(w_ref, g_ref, m_ref, v_ref, *rest):
        d_ref, mo_ref, vo_ref = rest[-3:]
        gv = g_ref[...]
        m2 = B1 * m_ref[...] + (1.0 - B1) * gv
        v2 = B2 * v_ref[...] + (1.0 - B2) * (gv * gv)
        m_hat = m2 / (1.0 - B1 ** STEP)
        v_hat = v2 / (1.0 - B2 ** STEP)
        d_ref[...] = -LR * (m_hat / (jnp.sqrt(v_hat) + AEPS) + WD * w_ref[...])
        mo_ref[...] = m2
        vo_ref[...] = v2

    spec = pl.BlockSpec((tc, l, r), lambda i: (i, 0, 0))
    outs = pl.pallas_call(
        functools.partial(body), name=name, grid=(main,), in_specs=[spec] * 4, out_specs=[spec] * 3,
        out_shape=[_sds(w.shape, f32)] * 3, compiler_params=_cparams(1))(w, g, m, v)
    if tail:
        assert (main * tc) % tail == 0
        last = pl.BlockSpec((tail, l, r), lambda i: (main * tc // tail, 0, 0))
        outs = pl.pallas_call(
            functools.partial(body), name=name + "_tail", grid=(1,),
            in_specs=[last] * 4 + [pl.BlockSpec(memory_space=pl.ANY)] * 3, out_specs=[last] * 3,
            out_shape=[_sds(w.shape, f32)] * 3, input_output_aliases={4: 0, 5: 1, 6: 2},
            compiler_params=_cparams(1))(w, g, m, v, *outs)
    return outs


SMALL_ROW = 1024
SMALL_GAINS = ("g1", "ga", "gs", "g2", "g3", "g4")
SMALL_LAYER_ROWS = 8 + 8 + 16 + 8


def _pack_small(grads):
    wide = lambda a: jnp.pad(a, ((0, 0), (0, 2 * SMALL_ROW - a.shape[1]))).reshape(-1, SMALL_ROW)
    row = lax.broadcasted_iota(jnp.int32, (8, SMALL_ROW), 0)
    parts = []
    for g in grads:
        singles = [g[k] for k in SMALL_GAINS] + [g["bs"][:, :SMALL_ROW],
                                                 jnp.pad(g["bs"][:, SMALL_ROW:], ((0, 0), (0, 2 * SMALL_ROW - XBC)))]
        first = sum(jnp.where(row == k, s, 0.0) for k, s in enumerate(singles))
        parts += [first, g["wa"], wide(g["ws"]), jnp.pad(g["par"], ((0, 0), (0, SMALL_ROW - CH)))]
    return jnp.concatenate(parts, axis=0)


def _unpack_small(packed, depth):
    rows = packed.reshape(depth, SMALL_LAYER_ROWS, SMALL_ROW)
    out = {k: rows[:, i] for i, k in enumerate(SMALL_GAINS)}
    out["bs"] = rows[:, 6:8].reshape(depth, 2 * SMALL_ROW)[:, :XBC]
    out["wa"] = rows[:, 8:11]
    out["ws"] = rows[:, 16:32].reshape(depth, 8, 2 * SMALL_ROW)[:, :4, :XBC]
    out["par"] = rows[:, 32:35, :CH]
    return out


def kernel(x, norm_mix_pre, w_in, conv_a_w, ssm_conv_w, ssm_conv_b, dt_bias, a_log, d_skip, conv_out_norm, ssm_out_norm, w_out, norm_mix_post, norm_mlp_pre, w_up, w_down, norm_mlp_post, loss_target, m_norm_mix_pre, m_w_in, m_conv_a_w, m_ssm_conv_w, m_ssm_conv_b, m_dt_bias, m_a_log, m_d_skip, m_conv_out_norm, m_ssm_out_norm, m_w_out, m_norm_mix_post, m_norm_mlp_pre, m_w_up, m_w_down, m_norm_mlp_post, v_norm_mix_pre, v_w_in, v_conv_a_w, v_ssm_conv_w, v_ssm_conv_b, v_dt_bias, v_a_log, v_d_skip, v_conv_out_norm, v_ssm_out_norm, v_w_out, v_norm_mix_post, v_norm_mlp_pre, v_w_up, v_w_down, v_norm_mlp_post):
    nb, seq, _ = x.shape
    t = nb * seq
    depth = w_in.shape[0]
    ncol = w_in.shape[2]
    chip = 2 * lax.axis_index("x") + lax.axis_index("y")

    taps = [conv_a_w, ssm_conv_w]
    taps_g = all_gather(taps, "chips", "gather_taps", slot_axis=1, own=False)
    wa_g, ws_g = [lax.dynamic_update_index_in_dim(g, s, chip, 1) for g, s in zip(taps_g, taps)]
    wa_full = jnp.transpose(wa_g, (0, 2, 1, 3)).reshape(depth, 3, D)
    ws_full = jnp.transpose(ws_g, (0, 2, 1, 3)).reshape(depth, 4, XBC)
    lane_pad = lambda a: jnp.pad(a, ((0, 0), (0, CH - a.shape[1])))
    par = jnp.stack([lane_pad(dt_bias), lane_pad(a_log), lane_pad(d_skip)], axis=1)
    par = jnp.pad(par, ((0, 0), (0, 5), (0, 0)))

    layer_shards = lambda l: [w_in[l].astype(bf16), w_out[l].astype(bf16), w_up[l].astype(bf16), w_down[l].astype(bf16)]
    issued = []

    def start(shards, name):
        zones = [lax.empty((N_CHIPS,) + s.shape, s.dtype) for s in shards]
        issued.append(chips_start(shards, zones, _weight_views(shards), name,
                                  after=issued[-1]["token"] if issued else taps_g[0]))
        return issued[-1]

    def finish(started, after, name):
        shards, zones = chips_wait(started, after, name)
        zones = weights_share(zones, "weights_share")
        return [lax.dynamic_update_index_in_dim(z, s, chip, 0) for z, s in zip(zones, shards)]

    def shaped(mats):
        wo_z, wu_z, wd_z = mats
        return wo_z.reshape(2 * D, D), wu_z, wd_z.reshape(DFF, D)

    first = layer_shards(0)
    travelling = {0: start(first[:1], "weights_start_0")}
    rest = start(first[1:], "weights_start_0_rest")
    for l in range(1, depth):
        travelling[l] = start(layer_shards(l), f"weights_start_{l}")

    def weights_of(l, x_in):
        mats = finish(travelling.pop(l), x_in, f"weights_wait_{l}")
        w = dict(win=assemble_columns(mats[0], PROJ, "assemble_w_in"), wa=jnp.pad(wa_full[l], ((0, 5), (0, 0))),
                 ws=jnp.pad(ws_full[l], ((0, 4), (0, 0))), bs=ssm_conv_b[l][None], par=par[l],
                 g1=norm_mix_pre[l][None], ga=conv_out_norm[l][None], gs=ssm_out_norm[l][None],
                 g2=norm_mix_post[l][None], g3=norm_mlp_pre[l][None], g4=norm_mlp_post[l][None])
        if l == 0:
            w["token"] = issued[-1]["token"]
            w["late"] = lambda after: dict(zip(("wo", "wu", "wd"), shaped(finish(rest, after, "weights_wait_0_rest"))))
        else:
            w.update(zip(("wo", "wu", "wd"), shaped(mats[1:])))
        return w

    core = lax.axis_index("c")
    grads_travelling = {}
    given_m = dict(win=m_w_in, wo=m_w_out, wu=m_w_up, wd=m_w_down)
    given_v = dict(win=v_w_in, wo=v_w_out, wu=v_w_up, wd=v_w_down)

    chip_major = dict(win=lambda a: a[None], wo=lambda a: a.reshape(N_CHIPS, 2 * D // N_CHIPS, D), wu=lambda a: a,
                      wd=lambda a: a.reshape(N_CHIPS, DFF // N_CHIPS, D))
    held = {}

    def grads_done(l, g, last):
        if l > 0 and not last:
            held[l] = g
            return None
        g = {**held.pop(l, {}), **g}
        keys = [k for k in ("win", "wo", "wu", "wd") if k in g]
        mats = [chip_major[k](g[k]) for k in keys]
        received = pair_send_halves(mats, "grads_to_pair")
        sums = [sum_pair_half(m_, r_, core, "pair_sum", by_chip=(N_CHIPS, ncol) if k == "win" else None)
                for k, m_, r_ in zip(keys, mats, received)]
        zones = [lax.empty(s.shape, s.dtype) for s in sums]
        started = chips_start(sums, zones, _grad_views(sums), f"grads_start_{l}_{len(grads_travelling)}")
        grads_travelling[(l, keys[0])] = (keys, started)
        return started["token"]

    sse, dx, grads = local_step(x.reshape(t, D), loss_target.reshape(t, D), depth, weights_of, seq, grads_done)
    loss = lax.psum(0.5 / D * sse[0, 0], ("x", "y", "c"))

    packed = _pack_small(grads)
    small_travelling = chips_start([packed], [lax.empty((8,) + packed.shape, f32)], _whole_views([packed]),
                                   "small_start", group="all")

    big_w = dict(win=w_in, wo=w_out, wu=w_up, wd=w_down)
    acc = {k: lax.empty((depth, bw.shape[1] // 2, bw.shape[2]), f32) for k, bw in big_w.items()}
    for n, ((l, _), (keys, started)) in enumerate(grads_travelling.items()):
        sums, zones = chips_wait(started, small_travelling["token"], f"grads_wait_{l}_{n}")
        for k, s, z in zip(keys, sums, zones):
            acc[k] = chip_sum_into(acc[k], l, s, z, chip, "chip_sum")
    names = ("win", "wo", "wu", "wd")
    acc = [acc[k] for k in names]
    pair_views = dict(src=lambda ref, a, c, to: ref, dst=lambda ref, a, c, sender: ref, rows=lambda a: depth)
    to_sibling = chips_start(acc, [lax.empty(a.shape, f32) for a in acc], pair_views, "grads_from_pair_start",
                             pieces=depth, group="pair")
    own_done = {}
    for k, a in zip(names, to_sibling["sources"]):
        if big_w[k].shape[-1] % CH == 0:
            own_done[k] = adamw_half(big_w[k], a, given_m[k], given_v[k], core, "adamw_matrix",
                                     token=to_sibling["token"])

    (packed,), (small_all,) = chips_wait(small_travelling, own_done["wd"][1], "small_wait")
    small_all = lax.dynamic_update_index_in_dim(small_all, packed, 4 * lax.axis_index("x") + 2 * lax.axis_index("y") + core, 0)
    small = _unpack_small(sum_slots(small_all, f32, "small_sum", tb=8), depth)
    wa_cols, ws_cols = conv_a_w.shape[2], ssm_conv_w.shape[2]
    par_g = small["par"].reshape(depth, 3, CH)
    g_small = dict(
        norm_mix_pre=small["g1"], conv_out_norm=small["ga"], ssm_out_norm=small["gs"], norm_mix_post=small["g2"],
        norm_mlp_pre=small["g3"], norm_mlp_post=small["g4"], ssm_conv_b=small["bs"],
        conv_a_w=lax.dynamic_slice_in_dim(small["wa"].reshape(depth, 3, D), chip * wa_cols, wa_cols, axis=2),
        ssm_conv_w=lax.dynamic_slice_in_dim(small["ws"].reshape(depth, 4, XBC), chip * ws_cols, ws_cols, axis=2),
        dt_bias=par_g[:, 0, :NH], a_log=par_g[:, 1, :NH], d_skip=par_g[:, 2, :NH])

    given = dict(norm_mix_pre=(norm_mix_pre, m_norm_mix_pre, v_norm_mix_pre), w_in=(w_in, m_w_in, v_w_in),
                 conv_a_w=(conv_a_w, m_conv_a_w, v_conv_a_w), ssm_conv_w=(ssm_conv_w, m_ssm_conv_w, v_ssm_conv_w),
                 ssm_conv_b=(ssm_conv_b, m_ssm_conv_b, v_ssm_conv_b), dt_bias=(dt_bias, m_dt_bias, v_dt_bias),
                 a_log=(a_log, m_a_log, v_a_log), d_skip=(d_skip, m_d_skip, v_d_skip),
                 conv_out_norm=(conv_out_norm, m_conv_out_norm, v_conv_out_norm),
                 ssm_out_norm=(ssm_out_norm, m_ssm_out_norm, v_ssm_out_norm), w_out=(w_out, m_w_out, v_w_out),
                 norm_mix_post=(norm_mix_post, m_norm_mix_post, v_norm_mix_post),
                 norm_mlp_pre=(norm_mlp_pre, m_norm_mlp_pre, v_norm_mlp_pre), w_up=(w_up, m_w_up, v_w_up),
                 w_down=(w_down, m_w_down, v_w_down), norm_mlp_post=(norm_mlp_post, m_norm_mlp_post, v_norm_mlp_post))
    order = ["norm_mix_pre", "w_in", "conv_a_w", "ssm_conv_w", "ssm_conv_b", "dt_bias", "a_log", "d_skip",
             "conv_out_norm", "ssm_out_norm", "w_out", "norm_mix_post", "norm_mlp_pre", "w_up", "w_down",
             "norm_mlp_post"]
    short = dict(w_in="win", w_out="wo", w_up="wu", w_down="wd")
    results = {}
    for n in order:
        if n in short:
            continue
        wv, mv, vv = given[n]
        gv = g_small[n].reshape(wv.shape)
        two_d = lambda a: a.reshape(-1, a.shape[-1])
        results[n] = (gv,) + tuple(adamw(two_d(wv), two_d(gv), two_d(mv), two_d(vv), "adamw"))

    acc, from_sibling = chips_wait(to_sibling, results["norm_mlp_post"][1], "grads_from_pair_wait")
    for n, k in short.items():
        wv, mv, vv = given[n]
        own, recv = acc[names.index(k)], from_sibling[names.index(k)]
        if k in own_done:
            results[n] = adamw_half(wv, recv, mv, vv, 1 - core, "adamw_matrix", before=own_done[k])
        else:
            gv = jnp.concatenate([jnp.where(core == 0, own, recv), jnp.where(core == 0, recv, own)], axis=1)
            to_cols, to_rows = (lambda a: jnp.transpose(a, (2, 0, 1))), (lambda a: jnp.transpose(a, (1, 2, 0)))
            results[n] = (gv,) + tuple(to_rows(o) for o in adamw_leading(to_cols(wv), to_cols(gv), to_cols(mv),
                                                                         to_cols(vv), "adamw_cols"))
    g_out, d_out, m_out, v_out = [], [], [], []
    for n in order:
        wv = given[n][0]
        gv, dlt, m2, v2 = results[n]
        g_out.append(gv.reshape(wv.shape))
        d_out.append(dlt.reshape(wv.shape))
        m_out.append(m2.reshape(wv.shape))
        v_out.append(v2.reshape(wv.shape))
    return (loss, dx.reshape(nb, seq, D), *g_out, *d_out, *m_out, *v_out)
```

```python
import functools

import jax
import jax.numpy as jnp
from jax import lax
from jax.experimental import pallas as pl
from jax.experimental.pallas import tpu as pltpu

f32, bf16 = jnp.float32, jnp.bfloat16

D = 1024
NH, HP = 16, 64
NG, NS = 2, 128
CH = 128
XBC = D + 2 * NG * NS
DFF = 4 * D
IN_COLS = 3 * D + D + XBC + NH
PROJ = 5760
COL_Z, COL_XBC, COL_DT = 3 * D, 4 * D, 4 * D + XBC
EPS = 1e-6
HALO = 8
HBLK = 16
VMEM_LIMIT = 56 * 2**20
MESH = pl.DeviceIdType.MESH

LR, B1, B2, AEPS, WD, STEP = 0.001, 0.9, 0.999, 1e-08, 0.01, 10


def _cparams(n_axes):
    return pltpu.CompilerParams(dimension_semantics=("arbitrary",) * n_axes, vmem_limit_bytes=VMEM_LIMIT)


def _sds(shape, dtype):
    return jax.ShapeDtypeStruct(tuple(shape), dtype)


def _token_spec(token):
    return [] if token is None else [pl.BlockSpec(memory_space=pl.ANY)]


def _token_arg(token):
    return [] if token is None else [token]


def _resident(shape):
    return pl.BlockSpec(shape, lambda i: (0,) * len(shape), pipeline_mode=pl.Buffered(1))


def _rms_fwd(x, g):
    r = lax.rsqrt(jnp.mean(x * x, axis=-1, keepdims=True) + EPS)
    return x * r * g


def _rms_bwd(x, g, dy):
    r = lax.rsqrt(jnp.mean(x * x, axis=-1, keepdims=True) + EPS)
    xh = x * r
    gdy = dy * g
    dx = r * (gdy - xh * jnp.mean(xh * gdy, axis=-1, keepdims=True))
    return dx, dy * xh


def _accum(ref, part, first):
    @pl.when(first)
    def _():
        ref[...] = part

    @pl.when(jnp.logical_not(first))
    def _():
        ref[...] += part


def _dot_nt(a, b):
    return lax.dot_general(a, b, (((1,), (1,)), ((), ())), preferred_element_type=f32)


def _dot_tn(a, b):
    return lax.dot_general(a, b, (((0,), (0,)), ((), ())), preferred_element_type=f32)


def _dot(a, b):
    return jnp.dot(a, b, preferred_element_type=f32)


def _split_dot(x, e_bf, n_split, nt=False):
    acc = None
    rem = x
    for s in range(n_split):
        hi = rem.astype(bf16)
        term = _dot_nt(hi, e_bf) if nt else _dot(hi, e_bf)
        acc = term if acc is None else acc + term
        if s + 1 < n_split:
            rem = rem - hi.astype(f32)
    return acc


def _sigmoid(x):
    return 0.5 * jnp.tanh(0.5 * x) + 0.5


def norm_matmul(x, g, w, tm, tn, out_dtype, name, token=None):
    t, n = x.shape[0], w.shape[1]
    w_spec = pl.BlockSpec((D, tn), lambda i, j: (0, j))

    def body(x_ref, g_ref, w_ref, *rest):
        o_ref, h_ref = rest[-2:]

        @pl.when(pl.program_id(1) == 0)
        def _():
            h_ref[...] = _rms_fwd(x_ref[...], g_ref[...]).astype(bf16)

        o_ref[...] = _dot(h_ref[...], w_ref[...]).astype(out_dtype)

    return pl.pallas_call(
        body, name=name, grid=(t // tm, n // tn),
        in_specs=[pl.BlockSpec((tm, D), lambda i, j: (i, 0)), pl.BlockSpec((1, D), lambda i, j: (0, 0)), w_spec]
        + _token_spec(token),
        out_specs=[pl.BlockSpec((tm, tn), lambda i, j: (i, j)), pl.BlockSpec((tm, D), lambda i, j: (i, 0))],
        out_shape=[_sds((t, n), out_dtype), _sds((t, D), bf16)],
        compiler_params=_cparams(2))(x, g, w, *_token_arg(token))


def matmul_postnorm(a, w, xres, g, tm, name):
    t, k = a.shape

    def body(a_ref, w_ref, xr_ref, g_ref, y_ref, xo_ref):
        y = _dot(a_ref[...], w_ref[...])
        y_ref[...] = y.astype(bf16)
        xo_ref[...] = xr_ref[...] + _rms_fwd(y, g_ref[...])

    return pl.pallas_call(
        body, name=name, grid=(t // tm,),
        in_specs=[pl.BlockSpec((tm, k), lambda i: (i, 0)), _resident(w.shape),
                  pl.BlockSpec((tm, D), lambda i: (i, 0)), pl.BlockSpec((1, D), lambda i: (0, 0))],
        out_specs=[pl.BlockSpec((tm, D), lambda i: (i, 0)), pl.BlockSpec((tm, D), lambda i: (i, 0))],
        out_shape=[_sds((t, D), bf16), _sds((t, D), f32)],
        compiler_params=_cparams(1))(a, w, xres, g)


def postnorm_bwd_matmul(y, g, dxo, w, tm, tn, name):
    t, n = y.shape[0], w.shape[0]

    def body(y_ref, g_ref, dxo_ref, w_ref, dy_ref, dg_ref, da_ref):
        i, j = pl.program_id(0), pl.program_id(1)

        @pl.when(j == 0)
        def _():
            dx, dgc = _rms_bwd(y_ref[...].astype(f32), g_ref[...], dxo_ref[...])
            dy_ref[...] = dx.astype(bf16)
            _accum(dg_ref, jnp.sum(dgc, axis=0, keepdims=True), i == 0)

        da_ref[...] = _dot_nt(dy_ref[...], w_ref[...]).astype(bf16)

    return pl.pallas_call(
        body, name=name, grid=(t // tm, n // tn),
        in_specs=[pl.BlockSpec((tm, D), lambda i, j: (i, 0)), pl.BlockSpec((1, D), lambda i, j: (0, 0)),
                  pl.BlockSpec((tm, D), lambda i, j: (i, 0)), pl.BlockSpec((tn, D), lambda i, j: (j, 0))],
        out_specs=[pl.BlockSpec((tm, D), lambda i, j: (i, 0)), pl.BlockSpec((1, D), lambda i, j: (0, 0)),
                   pl.BlockSpec((tm, tn), lambda i, j: (i, j))],
        out_shape=[_sds((t, D), bf16), _sds((1, D), f32), _sds((t, n), bf16)],
        compiler_params=_cparams(2))(y, g, dxo, w)


def matmul_prenorm_bwd(da, w, x, g, dxo, tm, name, token=None):
    t, k = da.shape

    def body(da_ref, w_ref, x_ref, g_ref, dxo_ref, *rest):
        dx_ref, dg_ref = rest[-2:]
        dh = _dot_nt(da_ref[...], w_ref[...])
        dxn, dgc = _rms_bwd(x_ref[...], g_ref[...], dh)
        dx_ref[...] = dxo_ref[...] + dxn
        _accum(dg_ref, jnp.sum(dgc, axis=0, keepdims=True), pl.program_id(0) == 0)

    return pl.pallas_call(
        body, name=name, grid=(t // tm,),
        in_specs=[pl.BlockSpec((tm, k), lambda i: (i, 0)), _resident(w.shape),
                  pl.BlockSpec((tm, D), lambda i: (i, 0)), pl.BlockSpec((1, D), lambda i: (0, 0)),
                  pl.BlockSpec((tm, D), lambda i: (i, 0))] + _token_spec(token),
        out_specs=[pl.BlockSpec((tm, D), lambda i: (i, 0)), pl.BlockSpec((1, D), lambda i: (0, 0))],
        out_shape=[_sds((t, D), f32), _sds((1, D), f32)],
        compiler_params=_cparams(1))(da, w, x, g, dxo, *_token_arg(token))


def mlp_fwd(x, g_pre, wu, wd, g_post, tm, name):
    t = x.shape[0]
    nq, _, fc = wu.shape

    def body(x_ref, gp_ref, wu_ref, wd_ref, gq_ref, fp_ref, h_ref, o_ref, xo_ref):
        xv = x_ref[...]
        h = _rms_fwd(xv, gp_ref[...]).astype(bf16)
        h_ref[...] = h
        o = None
        for q in range(nq):
            fq = _dot(h, wu_ref[q])
            fp_ref[:, q * fc:(q + 1) * fc] = fq.astype(bf16)
            r = jnp.maximum(fq, 0.0)
            part = _dot((r * r).astype(bf16), wd_ref[q * fc:(q + 1) * fc, :])
            o = part if o is None else o + part
        o_ref[...] = o.astype(bf16)
        xo_ref[...] = xv + _rms_fwd(o, gq_ref[...])

    row = lambda c: pl.BlockSpec((tm, c), lambda i: (i, 0))
    vec = pl.BlockSpec((1, D), lambda i: (0, 0))
    return pl.pallas_call(
        body, name=name, grid=(t // tm,),
        in_specs=[row(D), vec, _resident(wu.shape), _resident(wd.shape), vec],
        out_specs=[row(nq * fc), row(D), row(D), row(D)],
        out_shape=[_sds((t, nq * fc), bf16), _sds((t, D), bf16), _sds((t, D), bf16), _sds((t, D), f32)],
        compiler_params=_cparams(1))(x, g_pre, wu, wd, g_post)


def mlp_bwd(o, g_post, dxo, wd, fp, wu, x, g_pre, tm, name):
    t = x.shape[0]
    nq, _, fc = wu.shape

    def body(o_ref, gq_ref, dxo_ref, wd_ref, fp_ref, wu_ref, x_ref, gp_ref, do_ref, dfp_ref, dx_ref, dgq_ref, dgp_ref):
        i = pl.program_id(0)
        dxo_v = dxo_ref[...]
        do, dgq = _rms_bwd(o_ref[...].astype(f32), gq_ref[...], dxo_v)
        do_b = do.astype(bf16)
        do_ref[...] = do_b
        dh = None
        for q in range(nq):
            cols = slice(q * fc, (q + 1) * fc)
            dq = _dot_nt(do_b, wd_ref[cols, :]) * (2.0 * jnp.maximum(fp_ref[:, cols].astype(f32), 0.0))
            dq_b = dq.astype(bf16)
            dfp_ref[:, cols] = dq_b
            part = _dot_nt(dq_b, wu_ref[q])
            dh = part if dh is None else dh + part
        dxn, dgp = _rms_bwd(x_ref[...], gp_ref[...], dh)
        dx_ref[...] = dxo_v + dxn
        _accum(dgq_ref, jnp.sum(dgq, axis=0, keepdims=True), i == 0)
        _accum(dgp_ref, jnp.sum(dgp, axis=0, keepdims=True), i == 0)

    row = lambda c: pl.BlockSpec((tm, c), lambda i: (i, 0))
    vec = pl.BlockSpec((1, D), lambda i: (0, 0))
    return pl.pallas_call(
        body, name=name, grid=(t // tm,),
        in_specs=[row(D), vec, row(D), _resident(wd.shape), row(nq * fc), _resident(wu.shape), row(D), vec],
        out_specs=[row(D), row(nq * fc), row(D), vec, vec],
        out_shape=[_sds((t, D), bf16), _sds((t, nq * fc), bf16), _sds((t, D), f32), _sds((1, D), f32), _sds((1, D), f32)],
        compiler_params=_cparams(1))(o, g_post, dxo, wd, fp, wu, x, g_pre)


def matmul_tn(a, b, tm, tn, relu2, name, col_blocks=False):
    t, m = a.shape
    n = b.shape[1]
    if col_blocks:
        out_spec, out_shape = pl.BlockSpec((None, tm, tn), lambda i, j: (j, i, 0)), _sds((n // tn, m, tn), bf16)
    else:
        out_spec, out_shape = pl.BlockSpec((tm, tn), lambda i, j: (i, j)), _sds((m, n), bf16)

    def body(a_ref, b_ref, o_ref, at_ref):
        @pl.when(pl.program_id(1) == 0)
        def _():
            av = a_ref[...]
            if relu2:
                af = jnp.maximum(av.astype(f32), 0.0)
                av = (af * af).astype(bf16)
            at_ref[...] = av.T

        o_ref[...] = _dot(at_ref[...], b_ref[...]).astype(bf16)

    return pl.pallas_call(
        body, name=name, grid=(m // tm, n // tn),
        in_specs=[pl.BlockSpec((t, tm), lambda i, j: (0, i)), pl.BlockSpec((t, tn), lambda i, j: (0, j))],
        out_specs=out_spec, out_shape=out_shape,
        scratch_shapes=[pltpu.VMEM((tm, t), bf16)],
        compiler_params=_cparams(2))(a, b)


ROWS_A = 16
ROWS_B = 32
UNROLL = 4


def _past(win, s):
    return (win if s == 0 else pltpu.roll(win, s, 0))[HALO:]


def _future(win, s):
    n = win.shape[0]
    return (win if s == 0 else pltpu.roll(win, n - s, 0))[:n - HALO]


def _fold8(v):
    return v.reshape(v.shape[0] // 8, 8, v.shape[1]).sum(axis=0)


def _last8(ref):
    return ref[...].astype(f32)[HBLK - HALO:]


def _first8(ref):
    return ref[...].astype(f32)[:HALO]


def _rd(ref, rows):
    return ref[rows, :].astype(f32)


def _halo_prev(tb, col):
    return lambda i: (jnp.maximum(i * (tb // HBLK) - 1, 0), col)


def _halo_next(tb, col, t):
    return lambda i: (jnp.minimum((i + 1) * (tb // HBLK), t // HBLK - 1), col)


def group_a_fwd(proj, wa, g, seq, tb, name):
    t = proj.shape[0]
    bps = seq // tb

    def body(xa_ref, ca_ref, ba_ref, xah_ref, cah_ref, wa_ref, g_ref, o_ref, u_scr):
        first = (pl.program_id(0) % bps) == 0
        u_scr[0:HALO, :] = jnp.where(first, 0.0, _last8(cah_ref) * _last8(xah_ref))
        w, gv = wa_ref[...], g_ref[...]

        def chunk(i, carry):
            r = pl.multiple_of(i * ROWS_A, ROWS_A)
            rows = pl.ds(r, ROWS_A)
            u_scr[pl.ds(pl.multiple_of(HALO + r, HALO), ROWS_A), :] = _rd(ca_ref, rows) * _rd(xa_ref, rows)
            win = u_scr[pl.ds(r, ROWS_A + HALO), :]
            cv = w[2:3] * _past(win, 0) + w[1:2] * _past(win, 1) + w[0:1] * _past(win, 2)
            o_ref[rows, :] = _rms_fwd(_rd(ba_ref, rows) * cv, gv).astype(bf16)
            return carry

        lax.fori_loop(0, tb // ROWS_A, chunk, 0, unroll=UNROLL)

    blk = lambda c: pl.BlockSpec((tb, D), lambda i: (i, c))
    return pl.pallas_call(
        body, name=name, grid=(t // tb,),
        in_specs=[blk(0), blk(1), blk(2),
                  pl.BlockSpec((HBLK, D), _halo_prev(tb, 0)), pl.BlockSpec((HBLK, D), _halo_prev(tb, 1)),
                  pl.BlockSpec((8, D), lambda i: (0, 0)), pl.BlockSpec((1, D), lambda i: (0, 0))],
        out_specs=pl.BlockSpec((tb, D), lambda i: (i, 0)),
        out_shape=_sds((t, 2 * D), bf16),
        scratch_shapes=[pltpu.VMEM((tb + HALO, D), f32)],
        compiler_params=_cparams(1))(proj, proj, proj, proj, proj, wa, g)


def group_a_bwd(proj, dcat, wa, g, seq, tb, name, token=None):
    t = proj.shape[0]
    bps = seq // tb

    def body(xa_ref, ca_ref, ba_ref, dy_ref, xap_ref, cap_ref, xan_ref, can_ref, ban_ref, dyn_ref, wa_ref, g_ref,
             *rest):
        dp_ref, dwa_ref, dg_ref, u_scr, d_scr, acc_scr = rest[-6:]
        i = pl.program_id(0)
        first = (i % bps) == 0
        last = (i % bps) == bps - 1
        w = wa_ref[...]
        gv = g_ref[...]
        u_scr[0:HALO, :] = jnp.where(first, 0.0, _last8(cap_ref) * _last8(xap_ref))
        u_scr[HALO + tb:2 * HALO + tb, :] = _first8(can_ref) * _first8(xan_ref)
        acc_scr[...] = jnp.zeros_like(acc_scr)

        def forward_part(n, carry):
            r = pl.multiple_of(n * ROWS_A, ROWS_A)
            rows = pl.ds(r, ROWS_A)
            ba = _rd(ba_ref, rows)
            u_scr[pl.ds(pl.multiple_of(HALO + r, HALO), ROWS_A), :] = _rd(ca_ref, rows) * _rd(xa_ref, rows)
            win = u_scr[pl.ds(r, ROWS_A + HALO), :]
            u = [_past(win, s) for s in range(3)]
            cv = w[2:3] * u[0] + w[1:2] * u[1] + w[0:1] * u[2]
            dya, dgc = _rms_bwd(ba * cv, gv, _rd(dy_ref, rows))
            dcv = dya * ba
            d_scr[rows, :] = dcv
            dp_ref[rows, 2 * D:3 * D] = (dya * cv).astype(bf16)
            acc_scr[0:8, :] += _fold8(dgc)
            for k in range(3):
                acc_scr[8 + 8 * k:16 + 8 * k, :] += _fold8(dcv * u[2 - k])
            return carry

        lax.fori_loop(0, tb // ROWS_A, forward_part, 0, unroll=UNROLL)

        start = HALO + tb
        cvn = (w[2:3] * u_scr[pl.ds(start, HALO), :] + w[1:2] * u_scr[pl.ds(start - 1, HALO), :]
               + w[0:1] * u_scr[pl.ds(start - 2, HALO), :])
        ban = _first8(ban_ref)
        dyan, _ = _rms_bwd(ban * cvn, gv, _first8(dyn_ref))
        d_scr[tb:tb + HALO, :] = jnp.where(last, 0.0, dyan * ban)

        def backward_part(n, carry):
            r = pl.multiple_of(n * ROWS_A, ROWS_A)
            rows = pl.ds(r, ROWS_A)
            win = d_scr[pl.ds(r, ROWS_A + HALO), :]
            du = w[2:3] * _future(win, 0) + w[1:2] * _future(win, 1) + w[0:1] * _future(win, 2)
            dp_ref[rows, 0:D] = (du * _rd(ca_ref, rows)).astype(bf16)
            dp_ref[rows, D:2 * D] = (du * _rd(xa_ref, rows)).astype(bf16)
            return carry

        lax.fori_loop(0, tb // ROWS_A, backward_part, 0, unroll=UNROLL)

        row = lax.broadcasted_iota(jnp.int32, (8, D), 0)
        dw = jnp.zeros((8, D), f32)
        for k in range(3):
            dw = jnp.where(row == k, jnp.sum(acc_scr[8 + 8 * k:16 + 8 * k, :], axis=0, keepdims=True), dw)
        _accum(dwa_ref, dw, i == 0)
        _accum(dg_ref, jnp.sum(acc_scr[0:8, :], axis=0, keepdims=True), i == 0)

    blk = lambda c: pl.BlockSpec((tb, D), lambda i: (i, c))
    prv = lambda c: pl.BlockSpec((HBLK, D), _halo_prev(tb, c))
    nxt = lambda c: pl.BlockSpec((HBLK, D), _halo_next(tb, c, t))
    return pl.pallas_call(
        body, name=name, grid=(t // tb,),
        in_specs=[blk(0), blk(1), blk(2), blk(0), prv(0), prv(1), nxt(0), nxt(1), nxt(2), nxt(0),
                  pl.BlockSpec((8, D), lambda i: (0, 0)), pl.BlockSpec((1, D), lambda i: (0, 0))] + _token_spec(token),
        out_specs=[pl.BlockSpec((tb, 3 * D), lambda i: (i, 0)), pl.BlockSpec((8, D), lambda i: (0, 0)),
                   pl.BlockSpec((1, D), lambda i: (0, 0))],
        out_shape=[_sds((t, PROJ), bf16), _sds((8, D), f32), _sds((1, D), f32)],
        scratch_shapes=[pltpu.VMEM((tb + 2 * HALO, D), f32), pltpu.VMEM((tb + HALO, D), f32), pltpu.VMEM((32, D), f32)],
        compiler_params=_cparams(1))(proj, proj, proj, dcat, proj, proj, proj, proj, proj, dcat, wa, g,
                                     *_token_arg(token))


CB = 512
XBC_BLK0 = COL_XBC // CB


def conv_b_fwd(proj, ws, bs, seq, tb, name):
    t = proj.shape[0]
    bps = seq // tb

    def body(x_ref, xp_ref, w_ref, b_ref, o_ref, da_ref, x_scr):
        first = (pl.program_id(1) % bps) == 0
        x_scr[0:HALO, :] = jnp.where(first, 0.0, _last8(xp_ref))
        w, bias = w_ref[...], b_ref[...]

        def chunk(n, carry):
            r = pl.multiple_of(n * ROWS_B, ROWS_B)
            rows = pl.ds(r, ROWS_B)
            x_scr[pl.ds(pl.multiple_of(HALO + r, HALO), ROWS_B), :] = _rd(x_ref, rows)
            win = x_scr[pl.ds(r, ROWS_B + HALO), :]
            xc = bias + w[3:4] * _past(win, 0)
            for k in range(3):
                xc = xc + w[k:k + 1] * _past(win, 3 - k)
            sg = _sigmoid(xc)
            o_ref[rows, :] = xc * sg
            da_ref[rows, :] = (sg * (1.0 + xc * (1.0 - sg))).astype(bf16)
            return carry

        lax.fori_loop(0, tb // ROWS_B, chunk, 0, unroll=UNROLL)

    return pl.pallas_call(
        body, name=name, grid=(XBC // CB, t // tb),
        in_specs=[pl.BlockSpec((tb, CB), lambda j, i: (i, XBC_BLK0 + j)),
                  pl.BlockSpec((HBLK, CB), lambda j, i: (jnp.maximum(i * (tb // HBLK) - 1, 0), XBC_BLK0 + j)),
                  pl.BlockSpec((8, CB), lambda j, i: (0, j)), pl.BlockSpec((1, CB), lambda j, i: (0, j))],
        out_specs=[pl.BlockSpec((tb, CB), lambda j, i: (i, j)), pl.BlockSpec((tb, CB), lambda j, i: (i, j))],
        out_shape=[_sds((t, XBC), f32), _sds((t, XBC), bf16)],
        scratch_shapes=[pltpu.VMEM((tb + HALO, CB), f32)],
        compiler_params=_cparams(2))(proj, proj, ws, bs)


def conv_b_bwd(proj, dxs, dact, ws, dproj, seq, tb, name):
    t = proj.shape[0]
    bps = seq // tb

    def body(x_ref, xp_ref, d_ref, dn_ref, a_ref, an_ref, w_ref, dproj_ref, dx_ref, dw_ref, db_ref, x_scr, d_scr,
             acc_scr):
        i = pl.program_id(1)
        first = (i % bps) == 0
        last = (i % bps) == bps - 1
        w = w_ref[...]
        x_scr[0:HALO, :] = jnp.where(first, 0.0, _last8(xp_ref))
        acc_scr[...] = jnp.zeros_like(acc_scr)

        def forward_part(n, carry):
            r = pl.multiple_of(n * ROWS_B, ROWS_B)
            rows = pl.ds(r, ROWS_B)
            x_scr[pl.ds(pl.multiple_of(HALO + r, HALO), ROWS_B), :] = _rd(x_ref, rows)
            win = x_scr[pl.ds(r, ROWS_B + HALO), :]
            dxc = _rd(d_ref, rows) * _rd(a_ref, rows)
            d_scr[rows, :] = dxc
            acc_scr[0:8, :] += _fold8(dxc)
            for k in range(4):
                acc_scr[8 + 8 * k:16 + 8 * k, :] += _fold8(dxc * _past(win, 3 - k))
            return carry

        lax.fori_loop(0, tb // ROWS_B, forward_part, 0, unroll=UNROLL)
        d_scr[tb:tb + HALO, :] = jnp.where(last, 0.0, _first8(dn_ref) * _first8(an_ref))

        def backward_part(n, carry):
            r = pl.multiple_of(n * ROWS_B, ROWS_B)
            win = d_scr[pl.ds(r, ROWS_B + HALO), :]
            dx = w[3:4] * _future(win, 0)
            for k in range(3):
                dx = dx + w[k:k + 1] * _future(win, 3 - k)
            dx_ref[pl.ds(r, ROWS_B), :] = dx.astype(bf16)
            return carry

        lax.fori_loop(0, tb // ROWS_B, backward_part, 0, unroll=UNROLL)

        row = lax.broadcasted_iota(jnp.int32, (8, CB), 0)
        dw = jnp.zeros((8, CB), f32)
        for k in range(4):
            dw = jnp.where(row == k, jnp.sum(acc_scr[8 + 8 * k:16 + 8 * k, :], axis=0, keepdims=True), dw)
        _accum(dw_ref, dw, i == 0)
        _accum(db_ref, jnp.sum(acc_scr[0:8, :], axis=0, keepdims=True), i == 0)

    nh = t // HBLK
    nxt = pl.BlockSpec((HBLK, CB), lambda j, i: (jnp.minimum((i + 1) * (tb // HBLK), nh - 1), j))
    cur = pl.BlockSpec((tb, CB), lambda j, i: (i, j))
    return pl.pallas_call(
        body, name=name, grid=(XBC // CB, t // tb),
        in_specs=[pl.BlockSpec((tb, CB), lambda j, i: (i, XBC_BLK0 + j)),
                  pl.BlockSpec((HBLK, CB), lambda j, i: (jnp.maximum(i * (tb // HBLK) - 1, 0), XBC_BLK0 + j)),
                  cur, nxt, cur, nxt, pl.BlockSpec((8, CB), lambda j, i: (0, j)), pl.BlockSpec(memory_space=pl.ANY)],
        out_specs=[pl.BlockSpec((tb, CB), lambda j, i: (i, XBC_BLK0 + j)), pl.BlockSpec((8, CB), lambda j, i: (0, j)),
                   pl.BlockSpec((1, CB), lambda j, i: (0, j))],
        out_shape=[_sds((t, PROJ), bf16), _sds((8, XBC), f32), _sds((1, XBC), f32)],
        input_output_aliases={7: 0},
        scratch_shapes=[pltpu.VMEM((tb + HALO, CB), f32), pltpu.VMEM((tb + HALO, CB), f32), pltpu.VMEM((40, CB), f32)],
        compiler_params=_cparams(2))(proj, proj, dxs, dxs, dact, dact, ws, dproj)


def place_columns(buf, part, col_block, tb, name):
    t, wdt = part.shape

    def body(p_ref, buf_ref, o_ref):
        o_ref[...] = p_ref[...]

    return pl.pallas_call(
        body, name=name, grid=(t // tb,),
        in_specs=[pl.BlockSpec((tb, wdt), lambda i: (i, 0)), pl.BlockSpec(memory_space=pl.ANY)],
        out_specs=pl.BlockSpec((tb, wdt), lambda i: (i, col_block)), out_shape=_sds(buf.shape, buf.dtype),
        input_output_aliases={1: 0}, compiler_params=_cparams(1))(part, buf)


GW = D // NG
EXPAND_TERMS = 2
REDUCE_TERMS = 1


def _ssd_consts():
    head_of_lane = jnp.arange(D) // HP
    expand = (jnp.arange(CH)[:, None] == head_of_lane[None, :]).astype(bf16)
    tri = (jnp.arange(CH)[:, None] >= jnp.arange(CH)[None, :]).astype(f32)
    return expand, tri


def _ssd_common(par_ref, dtr_ref, e_ref, tri_ref):
    par = par_ref[...]
    dtb, alog, dsk = par[0:1], par[1:2], par[2:3]
    lane = lax.broadcasted_iota(jnp.int32, (CH, CH), 1)
    a = -jnp.exp(alog)
    dtr = dtr_ref[...].astype(f32) + dtb
    sp = jnp.maximum(dtr, 0.0) + jnp.log(1.0 + jnp.exp(-jnp.abs(dtr)))
    dt = jnp.where(lane < NH, sp, 0.0)
    cs = jnp.dot(tri_ref[...], dt * a, precision=lax.Precision.HIGHEST, preferred_element_type=f32)
    cs_last = cs[CH - 1:CH, :]
    dte = jnp.exp(cs_last - cs)
    ecs = jnp.exp(cs)
    ecl = jnp.exp(cs_last)
    e = e_ref[...]
    row8 = lax.broadcasted_iota(jnp.int32, (8, CH), 0)
    r8 = _split_dot(jnp.where(row8 == 0, ecl, jnp.where(row8 == 1, dsk, 0.0)), e, 3)
    return dict(a=a, dtr=dtr, dt=dt, cs=cs, cst=cs.T, dte=dte, ecs=ecs, ecl=ecl, e=e, lane=lane,
                dt_x=_split_dot(dt, e, EXPAND_TERMS), dte_x=_split_dot(dte, e, EXPAND_TERMS),
                ecs_x=_split_dot(ecs, e, EXPAND_TERMS),
                ecl_x=r8[0:1], dsk_x=r8[1:2])


def _decay_matrix(c, h):
    li = lax.broadcasted_iota(jnp.int32, (CH, CH), 0)
    seg = c["cs"][:, h:h + 1] - c["cst"][h:h + 1, :]
    return jnp.exp(jnp.where(li >= c["lane"], seg, -jnp.inf))


def _gate_norm_fwd(y, z, gs):
    zg = z * _sigmoid(z)
    yg = y * zg
    return jnp.concatenate([_rms_fwd(yg[:, k * GW:(k + 1) * GW], gs[:, k * GW:(k + 1) * GW]) for k in range(NG)], axis=1)


def ssd_fwd(xbcs, proj, par, gs, cat, seq, name):
    t = xbcs.shape[0]
    nc = seq // CH
    expand, tri = _ssd_consts()

    def body(xs_ref, b_ref, c_ref, dtr_ref, z_ref, par_ref, e_ref, tri_ref, gs_ref, cat_ref, yn_ref, y_ref, st_ref,
             p_scr, yd_scr):
        @pl.when(pl.program_id(0) % nc == 0)
        def _():
            p_scr[...] = jnp.zeros_like(p_scr)

        c = _ssd_common(par_ref, dtr_ref, e_ref, tri_ref)
        xs = xs_ref[...]
        xdt = xs * c["dt_x"]
        xdt_b = xdt.astype(bf16)
        xdte_b = (xdt * c["dte_x"]).astype(bf16)
        p = p_scr[...]
        st_ref[0] = p
        p_b = p.astype(bf16)
        lo = c["lane"] < HP
        for g in range(NG):
            bg = b_ref[:, g * NS:(g + 1) * NS].astype(bf16)
            cg = c_ref[:, g * NS:(g + 1) * NS].astype(bf16)
            gmat = _dot_nt(cg, bg)
            for q in range(GW // CH):
                col = g * GW + q * CH
                xp = xdt_b[:, col:col + CH]
                h0 = col // HP
                m0 = (gmat * _decay_matrix(c, h0)).astype(bf16)
                m1 = (gmat * _decay_matrix(c, h0 + 1)).astype(bf16)
                stacked = jnp.concatenate([jnp.where(lo, xp, jnp.zeros_like(xp)),
                                           jnp.where(lo, jnp.zeros_like(xp), xp)], axis=0)
                yd_scr[:, col:col + CH] = _dot(jnp.concatenate([m0, m1], axis=1), stacked)
            gsl = slice(g * GW, (g + 1) * GW)
            yoff = _dot(cg, p_b[:, gsl]) * c["ecs_x"][:, gsl]
            yd_scr[:, gsl] = yd_scr[:, gsl] + yoff
            p_scr[:, gsl] = p[:, gsl] * c["ecl_x"][:, gsl] + _dot_tn(bg, xdte_b[:, gsl])
        y = yd_scr[...] + c["dsk_x"] * xs
        y_ref[...] = y
        yn_ref[...] = _gate_norm_fwd(y, z_ref[...].astype(f32), gs_ref[...]).astype(bf16)

    nb = t // CH
    return pl.pallas_call(
        body, name=name, grid=(nb,),
        in_specs=[pl.BlockSpec((CH, D), lambda i: (i, 0)),
                  pl.BlockSpec((CH, NG * NS), lambda i: (i, D // (NG * NS))),
                  pl.BlockSpec((CH, NG * NS), lambda i: (i, D // (NG * NS) + 1)),
                  pl.BlockSpec((CH, CH), lambda i: (i, COL_DT // CH)),
                  pl.BlockSpec((CH, D), lambda i: (i, COL_Z // D)),
                  pl.BlockSpec((8, CH), lambda i: (0, 0)), pl.BlockSpec((CH, D), lambda i: (0, 0)),
                  pl.BlockSpec((CH, CH), lambda i: (0, 0)), pl.BlockSpec((1, D), lambda i: (0, 0)),
                  pl.BlockSpec(memory_space=pl.ANY)],
        out_specs=[pl.BlockSpec((CH, D), lambda i: (i, 1)), pl.BlockSpec((CH, D), lambda i: (i, 0)),
                   pl.BlockSpec((1, NS, D), lambda i: (i, 0, 0))],
        out_shape=[_sds((t, 2 * D), bf16), _sds((t, D), f32), _sds((nb, NS, D), f32)],
        input_output_aliases={9: 0},
        scratch_shapes=[pltpu.VMEM((NS, D), f32), pltpu.VMEM((CH, D), f32)],
        compiler_params=_cparams(1))(xbcs, xbcs, xbcs, proj, proj, par, expand, tri, gs, cat)


def ssd_bwd(xbcs, proj, ypre, states, dcat, par, gs, dproj, seq, name):
    t = xbcs.shape[0]
    nc = seq // CH
    expand, tri = _ssd_consts()

    def body(xs_ref, b_ref, c_ref, dtr_ref, z_ref, y_ref, st_ref, dyn_ref, par_ref, e_ref, tri_ref, gs_ref, dproj_ref,
             dx_ref, dz_ref, ddt_ref, dpar_ref, dgs_ref, dp_scr, dxdt_scr):
        i = pl.program_id(0)

        @pl.when(i % nc == 0)
        def _():
            dp_scr[...] = jnp.zeros_like(dp_scr)

        c = _ssd_common(par_ref, dtr_ref, e_ref, tri_ref)
        e = c["e"]
        lane = c["lane"]
        sub = lax.broadcasted_iota(jnp.int32, (CH, CH), 0)
        xs = xs_ref[...]
        xdt = xs * c["dt_x"]
        xdt_b = xdt.astype(bf16)
        xdte_b = (xdt * c["dte_x"]).astype(bf16)
        p = st_ref[0]
        p_b = p.astype(bf16)
        dpn = dp_scr[...]
        dpn_b = dpn.astype(bf16)

        y, z, gs_v = y_ref[...], z_ref[...].astype(f32), gs_ref[...]
        zs = _sigmoid(z)
        zg = z * zs
        yg = y * zg
        parts, gparts = [], []
        for k in range(NG):
            sl = slice(k * GW, (k + 1) * GW)
            dxk, dgk = _rms_bwd(yg[:, sl], gs_v[:, sl], dyn_ref[:, sl].astype(f32))
            parts.append(dxk)
            gparts.append(dgk)
        dyg = jnp.concatenate(parts, axis=1)
        dgs_rows = jnp.concatenate(gparts, axis=1)
        dy = dyg * zg
        dz_ref[...] = (dyg * y * (zs * (1.0 + z * (1.0 - zs)))).astype(bf16)
        dy_b = dy.astype(bf16)
        dq_b = (dy * c["ecs_x"]).astype(bf16)

        lo = lane < HP
        dcs = jnp.zeros((CH, CH), f32)
        dcst = jnp.zeros((CH, CH), f32)
        for g in range(NG):
            gsl = slice(g * GW, (g + 1) * GW)
            bg = b_ref[:, g * NS:(g + 1) * NS].astype(bf16)
            cg = c_ref[:, g * NS:(g + 1) * NS].astype(bf16)
            gmat = _dot_nt(cg, bg)
            dgm = jnp.zeros((CH, CH), f32)
            for q in range(GW // CH):
                col = g * GW + q * CH
                xp = xdt_b[:, col:col + CH]
                dyp = dy_b[:, col:col + CH]
                zero = jnp.zeros_like(dyp)
                xp2 = jnp.concatenate([jnp.where(lo, xp, zero), jnp.where(lo, zero, xp)], axis=0)
                dy2 = jnp.concatenate([jnp.where(lo, dyp, zero), jnp.where(lo, zero, dyp)], axis=0)
                dm2 = _dot_nt(dyp, xp2)
                ms = []
                for hh in range(2):
                    h = col // HP + hh
                    dec = _decay_matrix(c, h)
                    m = gmat * dec
                    dm = dm2[:, hh * CH:(hh + 1) * CH]
                    dseg = dm * m
                    dcs = dcs + jnp.where(lane == h, jnp.sum(dseg, axis=1, keepdims=True), 0.0)
                    dcst = dcst + jnp.where(sub == h, jnp.sum(dseg, axis=0, keepdims=True), 0.0)
                    dgm = dgm + dm * dec
                    ms.append(m.astype(bf16))
                dxdt_scr[:, col:col + CH] = _dot_tn(jnp.concatenate(ms, axis=0), dy2)
            dgm_b = dgm.astype(bf16)
            bds = _dot(bg, dpn_b[:, gsl])
            dxdt_scr[:, gsl] = dxdt_scr[:, gsl] + c["dte_x"][:, gsl] * bds
            dc_g = _dot(dgm_b, bg) + _dot_nt(dq_b[:, gsl], p_b[:, gsl])
            db_g = _dot_tn(dgm_b, cg) + _dot_nt(xdte_b[:, gsl], dpn_b[:, gsl])
            dx_ref[:, D + g * NS:D + (g + 1) * NS] = db_g
            dx_ref[:, D + NG * NS + g * NS:D + NG * NS + (g + 1) * NS] = dc_g
            dp_scr[:, gsl] = dpn[:, gsl] * c["ecl_x"][:, gsl] + _dot_tn(cg, dq_b[:, gsl])
            q_g = _dot(cg, p_b[:, gsl])
            e_g = e[:, gsl]
            dcs = dcs + c["ecs"] * _split_dot(dy[:, gsl] * q_g, e_g, REDUCE_TERMS, nt=True)
            ddte = _split_dot(xdt[:, gsl] * bds, e_g, REDUCE_TERMS, nt=True) * c["dte"]
            dcs = dcs - ddte
            dcs = dcs + jnp.where(sub == CH - 1, jnp.sum(ddte, axis=0, keepdims=True), 0.0)

        decl = _split_dot(jnp.broadcast_to(jnp.sum(dpn * p, axis=0, keepdims=True), (8, D)), e, 2, nt=True)[0:1]
        dcs = dcs + jnp.where(sub == CH - 1, c["ecl"] * decl, 0.0)
        dcs = dcs - dcst.T
        dadt = lax.dot_general(tri_ref[...], dcs, (((0,), (0,)), ((), ())), precision=lax.Precision.HIGHEST,
                               preferred_element_type=f32)
        dxdt = dxdt_scr[...]
        ddt = dadt * c["a"] + _split_dot(dxdt * xs, e, REDUCE_TERMS, nt=True)
        ddtr = jnp.where(lane < NH, ddt * _sigmoid(c["dtr"]), 0.0)
        ddt_ref[...] = ddtr.astype(bf16)
        dx_ref[:, 0:D] = dxdt * c["dt_x"] + c["dsk_x"] * dy
        dsk = _split_dot(jnp.broadcast_to(jnp.sum(dy * xs, axis=0, keepdims=True), (8, D)), e, 2, nt=True)[0:1]
        dalog = jnp.sum(dadt * c["dt"], axis=0, keepdims=True) * c["a"]
        row8 = lax.broadcasted_iota(jnp.int32, (8, CH), 0)
        dpar = jnp.where(row8 == 0, jnp.sum(ddtr, axis=0, keepdims=True),
                         jnp.where(row8 == 1, dalog, jnp.where(row8 == 2, dsk, 0.0)))
        dpar = jnp.where(lax.broadcasted_iota(jnp.int32, (8, CH), 1) < NH, dpar, 0.0)
        _accum(dpar_ref, dpar, i == 0)
        _accum(dgs_ref, jnp.sum(dgs_rows, axis=0, keepdims=True), i == 0)

    nb = t // CH
    rev = lambda i: (i // nc) * nc + (nc - 1 - i % nc)
    return pl.pallas_call(
        body, name=name, grid=(nb,),
        in_specs=[pl.BlockSpec((CH, D), lambda i: (rev(i), 0)),
                  pl.BlockSpec((CH, NG * NS), lambda i: (rev(i), D // (NG * NS))),
                  pl.BlockSpec((CH, NG * NS), lambda i: (rev(i), D // (NG * NS) + 1)),
                  pl.BlockSpec((CH, CH), lambda i: (rev(i), COL_DT // CH)),
                  pl.BlockSpec((CH, D), lambda i: (rev(i), COL_Z // D)),
                  pl.BlockSpec((CH, D), lambda i: (rev(i), 0)),
                  pl.BlockSpec((1, NS, D), lambda i: (rev(i), 0, 0)),
                  pl.BlockSpec((CH, D), lambda i: (rev(i), 1)),
                  pl.BlockSpec((8, CH), lambda i: (0, 0)), pl.BlockSpec((CH, D), lambda i: (0, 0)),
                  pl.BlockSpec((CH, CH), lambda i: (0, 0)), pl.BlockSpec((1, D), lambda i: (0, 0)),
                  pl.BlockSpec(memory_space=pl.ANY)],
        out_specs=[pl.BlockSpec((CH, XBC), lambda i: (rev(i), 0)), pl.BlockSpec((CH, D), lambda i: (rev(i), COL_Z // D)),
                   pl.BlockSpec((CH, CH), lambda i: (rev(i), 0)),
                   pl.BlockSpec((8, CH), lambda i: (0, 0)), pl.BlockSpec((1, D), lambda i: (0, 0))],
        out_shape=[_sds((t, XBC), f32), _sds((t, PROJ), bf16), _sds((t, CH), bf16), _sds((8, CH), f32), _sds((1, D), f32)],
        input_output_aliases={12: 1},
        scratch_shapes=[pltpu.VMEM((NS, D), f32), pltpu.VMEM((CH, D), f32)],
        compiler_params=_cparams(1))(xbcs, xbcs, xbcs, proj, proj, ypre, states, dcat, par, expand, tri, gs, dproj)


def loss_head(y, target, tb, name):
    t = y.shape[0]

    def body(y_ref, t_ref, s_ref, dy_ref):
        err = y_ref[...] - t_ref[...]
        dy_ref[...] = err * (1.0 / D)
        _accum(s_ref, jnp.zeros((8, CH), f32) + jnp.sum(err * err), pl.program_id(0) == 0)

    return pl.pallas_call(
        body, name=name, grid=(t // tb,),
        in_specs=[pl.BlockSpec((tb, D), lambda i: (i, 0)), pl.BlockSpec((tb, D), lambda i: (i, 0))],
        out_specs=[pl.BlockSpec((8, CH), lambda i: (0, 0)), pl.BlockSpec((tb, D), lambda i: (i, 0))],
        out_shape=[_sds((8, CH), f32), _sds((t, D), f32)],
        compiler_params=_cparams(1))(y, target)


def _tiles(t, seq):
    tm = min(512, t)
    return dict(tm=tm, tm_small=min(256, t), tm_large=min(1024, t), tm_huge=min(2048, t), tb=min(512, seq))


def local_step(x, target, depth, weights_of, seq, grads_done=None):
    t = x.shape[0]
    ts = _tiles(t, seq)
    tm, tl, th, tb = ts["tm"], ts["tm_large"], ts["tm_huge"], ts["tb"]
    saved, ws = [], []
    for l in range(depth):
        w = weights_of(l, x)
        ws.append(w)
        proj, h1 = norm_matmul(x, w["g1"], w["win"], th, 1152, bf16, "in_proj", token=w.get("token"))
        cat = group_a_fwd(proj, w["wa"], w["ga"], seq, tb, "group_a_fwd")
        xbcs, dact = conv_b_fwd(proj, w["ws"], w["bs"], seq, tb, "conv_b_fwd")
        cat, ypre, states = ssd_fwd(xbcs, proj, w["par"], w["gs"], cat, seq, "ssd_fwd")
        if "late" in w:
            w.update(w.pop("late")(cat))
        mix, x2 = matmul_postnorm(cat, w["wo"], x, w["g2"], tl, "out_proj")
        fp, h2, o, x3 = mlp_fwd(x2, w["g3"], w["wu"], w["wd"], w["g4"], tm, "mlp_fwd")
        saved.append(dict(x=x, proj=proj, h1=h1, xbcs=xbcs, dact=dact, ypre=ypre, states=states, cat=cat, mix=mix, x2=x2,
                          fp=fp, h2=h2, o=o))
        x = x3
    sse, dx = loss_head(x, target, tm, "loss_head")
    grads = [None] * depth
    for l in reversed(range(depth)):
        s, w = saved[l], ws[l]
        do, dfp, dx2, dg4, dg3 = mlp_bwd(s["o"], w["g4"], dx, w["wd"], s["fp"], w["wu"], s["x2"], w["g3"],
                                         ts["tm_small"], "mlp_bwd")
        dwd = matmul_tn(s["fp"], do, 512, 1024, True, "mlp_down_dw")
        dwu = matmul_tn(s["h2"], dfp, tl, 1024, False, "mlp_up_dw", col_blocks=True)
        dmix, dg2, dcat = postnorm_bwd_matmul(s["mix"], w["g2"], dx2, w["wo"], tl, 1024, "out_proj_bwd")
        dwo = matmul_tn(s["cat"], dmix, 512, 1024, False, "out_proj_dw")
        token = None if grads_done is None else grads_done(l, dict(wo=dwo, wu=dwu, wd=dwd), False)
        dproj, dwa, dga = group_a_bwd(s["proj"], dcat, w["wa"], w["ga"], seq, tb, "group_a_bwd", token=token)
        dxbcs, dproj, ddt, dpar, dgs = ssd_bwd(s["xbcs"], s["proj"], s["ypre"], s["states"], dcat, w["par"], w["gs"],
                                               dproj, seq, "ssd_bwd")
        dproj, dws, dbs = conv_b_bwd(s["proj"], dxbcs, s["dact"], w["ws"], dproj, seq, tb, "conv_b_bwd")
        dproj = place_columns(dproj, ddt, COL_DT // CH, tm, "place_ddt")
        dwin = matmul_tn(s["h1"], dproj, tl, 1152, False, "in_proj_dw")
        token = None if grads_done is None else grads_done(l, dict(win=dwin), True)
        dx, dg1 = matmul_prenorm_bwd(dproj, w["win"], s["x"], w["g1"], dx2, tm, "in_proj_bwd", token=token)
        grads[l] = dict(win=dwin, wo=dwo, wu=dwu, wd=dwd, wa=dwa, ws=dws, bs=dbs, par=dpar,
                        g1=dg1, ga=dga, gs=dgs, g2=dg2, g3=dg3, g4=dg4)
    return sse, dx, grads


GROUPS = {
    "chips": [(1, 0, 0), (0, 1, 0), (1, 1, 0)],
    "pair": [(0, 0, 1)],
    "all": [(1, 0, 0), (0, 1, 0), (1, 1, 0), (0, 0, 1), (1, 0, 1), (0, 1, 1), (1, 1, 1)],
}


def _group_index(group, x, y, c):
    return {"chips": 2 * x + y, "pair": c, "all": 4 * x + 2 * y + c}[group]


def _chunk_indices(shape, pieces):
    if len(shape) < 3:
        return [()]
    lead = [()]
    for n in shape[:-2]:
        lead = [i + (k,) for i in lead for k in range(n)]
    rows = shape[-2]
    split = max(1, pieces // len(lead))
    while split > 1 and (rows % split or (rows // split) % 16):
        split -= 1
    step = rows // split
    return [i + (pl.ds(s * step, step),) for i in lead for s in range(split)]


def _exchange(arrays, out_shapes, group, src_view, dst_view, view_shape, name, own, pieces=16):
    masks = GROUPS[group]
    na, nm = len(arrays), len(masks)
    cuts = [_chunk_indices(view_shape(a), pieces) for a in range(na)]

    def body(*refs):
        ins, outs = refs[:na], refs[na:2 * na]
        send_sems, recv_sems = refs[2 * na:2 * na + 2]
        local_sems = refs[2 * na + 2] if own else None
        x, y, c = lax.axis_index("x"), lax.axis_index("y"), lax.axis_index("c")
        me = _group_index(group, x, y, c)
        peers = []
        for mx, my, mc in masks:
            px, py, pc = (1 - x if mx else x), (1 - y if my else y), (1 - c if mc else c)
            peers.append(((px, py, pc), _group_index(group, px, py, pc)))

        def part(ref, idx):
            return ref.at[idx] if idx else ref

        if own:
            for a in range(na):
                for idx in cuts[a]:
                    pltpu.make_async_copy(part(src_view(ins[a], a, me), idx), part(dst_view(outs[a], a, me), idx),
                                          local_sems.at[a]).start()
        for a in range(na):
            for j, (dev, pidx) in enumerate(peers):
                for idx in cuts[a]:
                    pltpu.make_async_remote_copy(
                        src_ref=part(src_view(ins[a], a, pidx), idx), dst_ref=part(dst_view(outs[a], a, me), idx),
                        send_sem=send_sems.at[a * nm + j], recv_sem=recv_sems.at[a * nm + j],
                        device_id=dev, device_id_type=MESH).start()
        whole = []
        for a in range(na):
            for j, (dev, pidx) in enumerate(peers):
                whole.append(pltpu.make_async_remote_copy(
                    src_ref=src_view(ins[a], a, pidx), dst_ref=dst_view(outs[a], a, pidx),
                    send_sem=send_sems.at[a * nm + j], recv_sem=recv_sems.at[a * nm + j],
                    device_id=dev, device_id_type=MESH))
        for cp in whole:
            cp.wait_recv()
        for cp in whole:
            cp.wait_send()
        if own:
            for a in range(na):
                pltpu.make_async_copy(src_view(ins[a], a, me), dst_view(outs[a], a, me), local_sems.at[a]).wait()

    hbm = pl.BlockSpec(memory_space=pltpu.HBM)
    sems = [pltpu.SemaphoreType.DMA((na * nm,)), pltpu.SemaphoreType.DMA((na * nm,))]
    return pl.pallas_call(
        body, name=name, in_specs=[hbm] * na, out_specs=[hbm] * na,
        out_shape=[_sds(s, a.dtype) for s, a in zip(out_shapes, arrays)],
        scratch_shapes=sems + ([pltpu.SemaphoreType.DMA((na,))] if own else []))(*arrays)


def all_gather(arrays, group, name, slot_axis=0, own=True):
    n = len(GROUPS[group]) + 1
    shapes = [a.shape[:slot_axis] + (n,) + a.shape[slot_axis:] for a in arrays]
    lead = (slice(None),) * slot_axis
    return _exchange(arrays, shapes, group, lambda r, a, i: r, lambda r, a, i: r.at[lead + (i,)],
                     lambda a: arrays[a].shape, name, own)


HBM_SPEC = pl.BlockSpec(memory_space=pltpu.HBM)
SEM_SPEC = pl.BlockSpec(memory_space=pltpu.SEMAPHORE)
DATAFLOW = pltpu.SideEffectType.DATAFLOW_SIDE_EFFECTING
N_CHIPS = 4


def _peers(group, x, y, c):
    out = []
    for mx, my, mc in GROUPS[group]:
        px, py, pc = (1 - x if mx else x), (1 - y if my else y), (1 - c if mc else c)
        out.append(((px, py, pc), _group_index(group, px, py, pc)))
    return out


def _whole_views(sources):
    return dict(src=lambda ref, a, c, to: ref, dst=lambda ref, a, c, sender: ref.at[sender],
                rows=lambda a: sources[a].shape[0])


def _weight_views(shards):
    half = [s.shape[0] // 2 for s in shards]
    return dict(src=lambda ref, a, c, to_chip: ref.at[pl.ds(c * half[a], half[a])],
                dst=lambda ref, a, c, from_chip: ref.at[from_chip, pl.ds(c * half[a], half[a])],
                rows=lambda a: half[a])


def _grad_views(sums):
    return dict(src=lambda ref, a, c, to_chip: ref.at[to_chip], dst=lambda ref, a, c, from_chip: ref.at[from_chip],
                rows=lambda a: sums[a].shape[1])


def chips_start(sources, zones, views, name, pieces=4, after=None, group="chips"):
    na, nm = len(sources), len(GROUPS[group])

    def body(*refs):
        ins, lands = refs[:na], refs[na:2 * na]
        n_in = 2 * na + len(_token_arg(after))
        send_sems, recv_sems, token = refs[n_in], refs[n_in + 1], refs[-1]
        x, y, c = lax.axis_index("x"), lax.axis_index("y"), lax.axis_index("c")
        me = _group_index(group, x, y, c)
        for a in range(na):
            step = views["rows"](a) // pieces
            for j, (dev, to) in enumerate(_peers(group, x, y, c)):
                for q in range(pieces):
                    rows = pl.ds(q * step, step)
                    pltpu.make_async_remote_copy(
                        src_ref=views["src"](ins[a], a, c, to).at[rows],
                        dst_ref=views["dst"](lands[a], a, c, me).at[rows],
                        send_sem=send_sems.at[a * nm + j], recv_sem=recv_sems.at[a * nm + j],
                        device_id=dev, device_id_type=MESH).start()
        token[...] = jnp.zeros_like(token)

    both = list(sources) + list(zones)
    outs = pl.pallas_call(
        body, name=name,
        out_shape=(pltpu.SemaphoreType.DMA((na * nm,)), pltpu.SemaphoreType.DMA((na * nm,)),
                   *[pltpu.HBM(b.shape, b.dtype) for b in both], _sds((8, CH), f32)),
        in_specs=[HBM_SPEC] * (2 * na) + _token_spec(after),
        out_specs=(SEM_SPEC, SEM_SPEC, *[HBM_SPEC] * (2 * na), pl.BlockSpec(memory_space=pltpu.VMEM)),
        input_output_aliases={i: 2 + i for i in range(2 * na)},
        compiler_params=pltpu.CompilerParams(has_side_effects=DATAFLOW))(
            *[pltpu.with_memory_space_constraint(b, pltpu.HBM) for b in both], *_token_arg(after))
    return dict(send=outs[0], recv=outs[1], sources=list(outs[2:2 + na]), zones=list(outs[2 + na:2 + 2 * na]),
                token=outs[-1], views=views, group=group)


def chips_wait(started, after, name):
    sources, zones, views, group = started["sources"], started["zones"], started["views"], started["group"]
    na, nm = len(sources), len(GROUPS[group])

    def body(*refs):
        ins, lands = refs[:na], refs[na:2 * na]
        send_sems, recv_sems = refs[2 * na], refs[2 * na + 1]
        x, y, c = lax.axis_index("x"), lax.axis_index("y"), lax.axis_index("c")
        for a in range(na):
            for j, (dev, peer) in enumerate(_peers(group, x, y, c)):
                cp = pltpu.make_async_remote_copy(
                    src_ref=views["src"](ins[a], a, c, peer), dst_ref=views["dst"](lands[a], a, c, peer),
                    send_sem=send_sems.at[a * nm + j], recv_sem=recv_sems.at[a * nm + j],
                    device_id=dev, device_id_type=MESH)
                cp.wait_send()
                cp.wait_recv()

    both = list(sources) + list(zones)
    outs = pl.pallas_call(
        body, name=name, out_shape=tuple(pltpu.HBM(b.shape, b.dtype) for b in both),
        in_specs=[HBM_SPEC] * (2 * na) + [SEM_SPEC, SEM_SPEC, pl.BlockSpec(memory_space=pl.ANY)],
        out_specs=tuple([HBM_SPEC] * (2 * na)), input_output_aliases={i: i for i in range(2 * na)},
        compiler_params=pltpu.CompilerParams(has_side_effects=DATAFLOW))(*both, started["send"], started["recv"], after)
    return list(outs[:na]), list(outs[na:])


def weights_share(zones, name):
    na, nm = len(zones), N_CHIPS - 1

    def body(*refs):
        lands = refs[na:2 * na]
        send_sems, recv_sems = refs[2 * na:]
        x, y, c = lax.axis_index("x"), lax.axis_index("y"), lax.axis_index("c")
        chip = 2 * x + y
        sibling = (x, y, 1 - c)
        sends = []
        for a in range(na):
            half = zones[a].shape[1] // 2
            for m in range(1, N_CHIPS):
                mine = lands[a].at[chip ^ m, pl.ds(c * half, half)]
                sends.append(pltpu.make_async_remote_copy(
                    src_ref=mine, dst_ref=mine, send_sem=send_sems.at[a * nm + m - 1],
                    recv_sem=recv_sems.at[a * nm + m - 1], device_id=sibling, device_id_type=MESH))
        for cp in sends:
            cp.start()
        for a in range(na):
            half = zones[a].shape[1] // 2
            for m in range(1, N_CHIPS):
                theirs = lands[a].at[chip ^ m, pl.ds((1 - c) * half, half)]
                pltpu.make_async_remote_copy(
                    src_ref=theirs, dst_ref=theirs, send_sem=send_sems.at[a * nm + m - 1],
                    recv_sem=recv_sems.at[a * nm + m - 1], device_id=sibling, device_id_type=MESH).wait_recv()
        for cp in sends:
            cp.wait_send()

    return pl.pallas_call(
        body, name=name, in_specs=[HBM_SPEC] * na, out_specs=[HBM_SPEC] * na,
        out_shape=[_sds(z.shape, z.dtype) for z in zones], input_output_aliases={i: i for i in range(na)},
        scratch_shapes=[pltpu.SemaphoreType.DMA((na * nm,)), pltpu.SemaphoreType.DMA((na * nm,))])(*zones)


def pair_send_halves(grads, name):
    half = [g.shape[1] // 2 for g in grads]
    shapes = [(g.shape[0], h, g.shape[2]) for g, h in zip(grads, half)]
    return _exchange(grads, shapes, "pair", lambda r, a, i: r.at[:, pl.ds(i * half[a], half[a])],
                     lambda r, a, i: r, lambda a: shapes[a], name, False)


def sum_pair_half(g, recv, core, name, tb=256, by_chip=None):
    nk, r, c = g.shape
    tb = min(tb, r // 2)
    nb = r // 2 // tb

    def body(core_ref, g_ref, r_ref, o_ref):
        s = g_ref[...].astype(f32) + r_ref[...].astype(f32)
        if by_chip is None:
            o_ref[...] = s.astype(bf16)
        else:
            for k in range(by_chip[0]):
                o_ref[k] = s[:, k * by_chip[1]:(k + 1) * by_chip[1]].astype(bf16)

    if by_chip is None:
        out_spec = pl.BlockSpec((None, tb, c), lambda k, i, core_ref: (k, i, 0))
        out_shape = _sds((nk, r // 2, c), bf16)
    else:
        assert nk == 1
        out_spec = pl.BlockSpec((by_chip[0], tb, by_chip[1]), lambda k, i, core_ref: (0, i, 0))
        out_shape = _sds((by_chip[0], r // 2, by_chip[1]), bf16)
    return pl.pallas_call(
        body, name=name,
        grid_spec=pltpu.PrefetchScalarGridSpec(
            num_scalar_prefetch=1, grid=(nk, nb),
            in_specs=[pl.BlockSpec((None, tb, c), lambda k, i, core_ref: (k, core_ref[0] * nb + i, 0)),
                      pl.BlockSpec((None, tb, c), lambda k, i, core_ref: (k, i, 0))],
            out_specs=out_spec),
        out_shape=out_shape, compiler_params=_cparams(2))(jnp.reshape(core, (1,)).astype(jnp.int32), g, recv)


def assemble_columns(blocks, width, name, tb=256):
    n, r, c = blocks.shape

    def body(b_ref, o_ref):
        for k in range(n):
            o_ref[:, k * c:(k + 1) * c] = b_ref[k]
        o_ref[:, n * c:] = jnp.zeros((tb, width - n * c), blocks.dtype)

    return pl.pallas_call(
        body, name=name, grid=(r // tb,), in_specs=[pl.BlockSpec((n, tb, c), lambda i: (0, i, 0))],
        out_specs=pl.BlockSpec((tb, width), lambda i: (i, 0)), out_shape=_sds((r, width), blocks.dtype),
        compiler_params=_cparams(1))(blocks)


def chip_sum_into(acc, layer, own, others, chip, name, tb=256):
    n, r, c = own.shape
    tb = min(tb, r)

    def body(chip_ref, x_ref, y1_ref, y2_ref, y3_ref, acc_ref, o_ref):
        o_ref[...] = ((x_ref[...].astype(f32) + y1_ref[...].astype(f32)) + y2_ref[...].astype(f32)) + y3_ref[...].astype(f32)

    def slot(k):
        return pl.BlockSpec((None, tb, c), lambda i, chip_ref: (chip_ref[0] ^ k, i, 0))

    return pl.pallas_call(
        body, name=name,
        grid_spec=pltpu.PrefetchScalarGridSpec(
            num_scalar_prefetch=1, grid=(r // tb,),
            in_specs=[slot(k) for k in range(n)] + [pl.BlockSpec(memory_space=pl.ANY)],
            out_specs=pl.BlockSpec((None, tb, c), lambda i, chip_ref: (layer, i, 0))),
        out_shape=_sds(acc.shape, f32), input_output_aliases={n + 1: 0}, compiler_params=_cparams(1))(
            jnp.reshape(chip, (1,)).astype(jnp.int32), own, *([others] * (n - 1)), acc)


def adamw_half(w, g_half, m, v, half, name, before=None, token=None, tb=256):
    depth, r, c = w.shape
    tb = min(tb, r // 2)
    nb = r // 2 // tb
    n_extra = (0 if before is None else 4) + len(_token_arg(token))

    def body(half_ref, w_ref, gh_ref, m_ref, v_ref, *rest):
        g_ref, d_ref, mo_ref, vo_ref = rest[n_extra:]
        gv = gh_ref[...]
        m2 = B1 * m_ref[...] + (1.0 - B1) * gv
        v2 = B2 * v_ref[...] + (1.0 - B2) * (gv * gv)
        m_hat = m2 / (1.0 - B1 ** STEP)
        v_hat = v2 / (1.0 - B2 ** STEP)
        g_ref[...] = gv
        d_ref[...] = -LR * (m_hat / (jnp.sqrt(v_hat) + AEPS) + WD * w_ref[...])
        mo_ref[...] = m2
        vo_ref[...] = v2

    whole = pl.BlockSpec((None, tb, c), lambda l, i, half_ref: (l, half_ref[0] * nb + i, 0))
    part = pl.BlockSpec((None, tb, c), lambda l, i, half_ref: (l, i, 0))
    extra = ([] if before is None else list(before)) + _token_arg(token)
    return pl.pallas_call(
        body, name=name,
        grid_spec=pltpu.PrefetchScalarGridSpec(
            num_scalar_prefetch=1, grid=(depth, nb),
            in_specs=[whole, part, whole, whole] + [pl.BlockSpec(memory_space=pl.ANY)] * n_extra, out_specs=[whole] * 4),
        out_shape=[_sds(w.shape, f32)] * 4,
        input_output_aliases={} if before is None else {5 + k: k for k in range(4)},
        compiler_params=_cparams(2))(jnp.reshape(half, (1,)).astype(jnp.int32), w, g_half, m, v, *extra)


def sum_slots(y, out_dtype, name, tb=256):
    n, r, c = y.shape
    tb = min(tb, r)

    def body(y_ref, o_ref):
        acc = y_ref[0].astype(f32)
        for i in range(1, n):
            acc = acc + y_ref[i].astype(f32)
        o_ref[...] = acc.astype(out_dtype)

    return pl.pallas_call(
        body, name=name, grid=(r // tb,),
        in_specs=[pl.BlockSpec((n, tb, c), lambda i: (0, i, 0))], out_specs=pl.BlockSpec((tb, c), lambda i: (i, 0)),
        out_shape=_sds((r, c), out_dtype), compiler_params=_cparams(1))(y)


def adamw(w, g, m, v, name, tb=256):
    r, c = w.shape
    tb = min(tb, r)

    def body(w_ref, g_ref, m_ref, v_ref, d_ref, mo_ref, vo_ref):
        gv = g_ref[...]
        m2 = B1 * m_ref[...] + (1.0 - B1) * gv
        v2 = B2 * v_ref[...] + (1.0 - B2) * (gv * gv)
        m_hat = m2 / (1.0 - B1 ** STEP)
        v_hat = v2 / (1.0 - B2 ** STEP)
        d_ref[...] = -LR * (m_hat / (jnp.sqrt(v_hat) + AEPS) + WD * w_ref[...])
        mo_ref[...] = m2
        vo_ref[...] = v2

    spec = pl.BlockSpec((tb, c), lambda i: (i, 0))
    return pl.pallas_call(
        body, name=name, grid=(r // tb,), in_specs=[spec] * 4, out_specs=[spec] * 3,
        out_shape=[_sds((r, c), f32)] * 3, compiler_params=_cparams(1))(w, g, m, v)


def adamw_leading(w, g, m, v, name, tc=64):
    c, l, r = w.shape
    main = c // tc
    tail = c - main * tc

    def body(w_ref, g_ref, m_ref, v_ref, *rest):
        d_ref, mo_ref, vo_ref = rest[-3:]
        gv = g_ref[...]
        m2 = B1 * m_ref[...] + (1.0 - B1) * gv
        v2 = B2 * v_ref[...] + (1.0 - B2) * (gv * gv)
        m_hat = m2 / (1.0 - B1 ** STEP)
        v_hat = v2 / (1.0 - B2 ** STEP)
        d_ref[...] = -LR * (m_hat / (jnp.sqrt(v_hat) + AEPS) + WD * w_ref[...])
        mo_ref[...] = m2
        vo_ref[...] = v2

    spec = pl.BlockSpec((tc, l, r), lambda i: (i, 0, 0))
    outs = pl.pallas_call(
        functools.partial(body), name=name, grid=(main,), in_specs=[spec] * 4, out_specs=[spec] * 3,
        out_shape=[_sds(w.shape, f32)] * 3, compiler_params=_cparams(1))(w, g, m, v)
    if tail:
        assert (main * tc) % tail == 0
        last = pl.BlockSpec((tail, l, r), lambda i: (main * tc // tail, 0, 0))
        outs = pl.pallas_call(
            functools.partial(body), name=name + "_tail", grid=(1,),
            in_specs=[last] * 4 + [pl.BlockSpec(memory_space=pl.ANY)] * 3, out_specs=[last] * 3,
            out_shape=[_sds(w.shape, f32)] * 3, input_output_aliases={4: 0, 5: 1, 6: 2},
            compiler_params=_cparams(1))(w, g, m, v, *outs)
    return outs


SMALL_ROW = 1024
SMALL_GAINS = ("g1", "ga", "gs", "g2", "g3", "g4")
SMALL_LAYER_ROWS = 8 + 8 + 16 + 8


def _pack_small(grads):
    wide = lambda a: jnp.pad(a, ((0, 0), (0, 2 * SMALL_ROW - a.shape[1]))).reshape(-1, SMALL_ROW)
    row = lax.broadcasted_iota(jnp.int32, (8, SMALL_ROW), 0)
    parts = []
    for g in grads:
        singles = [g[k] for k in SMALL_GAINS] + [g["bs"][:, :SMALL_ROW],
                                                 jnp.pad(g["bs"][:, SMALL_ROW:], ((0, 0), (0, 2 * SMALL_ROW - XBC)))]
        first = sum(jnp.where(row == k, s, 0.0) for k, s in enumerate(singles))
        parts += [first, g["wa"], wide(g["ws"]), jnp.pad(g["par"], ((0, 0), (0, SMALL_ROW - CH)))]
    return jnp.concatenate(parts, axis=0)


def _unpack_small(packed, depth):
    rows = packed.reshape(depth, SMALL_LAYER_ROWS, SMALL_ROW)
    out = {k: rows[:, i] for i, k in enumerate(SMALL_GAINS)}
    out["bs"] = rows[:, 6:8].reshape(depth, 2 * SMALL_ROW)[:, :XBC]
    out["wa"] = rows[:, 8:11]
    out["ws"] = rows[:, 16:32].reshape(depth, 8, 2 * SMALL_ROW)[:, :4, :XBC]
    out["par"] = rows[:, 32:35, :CH]
    return out


def kernel(x, norm_mix_pre, w_in, conv_a_w, ssm_conv_w, ssm_conv_b, dt_bias, a_log, d_skip, conv_out_norm, ssm_out_norm, w_out, norm_mix_post, norm_mlp_pre, w_up, w_down, norm_mlp_post, loss_target, m_norm_mix_pre, m_w_in, m_conv_a_w, m_ssm_conv_w, m_ssm_conv_b, m_dt_bias, m_a_log, m_d_skip, m_conv_out_norm, m_ssm_out_norm, m_w_out, m_norm_mix_post, m_norm_mlp_pre, m_w_up, m_w_down, m_norm_mlp_post, v_norm_mix_pre, v_w_in, v_conv_a_w, v_ssm_conv_w, v_ssm_conv_b, v_dt_bias, v_a_log, v_d_skip, v_conv_out_norm, v_ssm_out_norm, v_w_out, v_norm_mix_post, v_norm_mlp_pre, v_w_up, v_w_down, v_norm_mlp_post):
    nb, seq, _ = x.shape
    t = nb * seq
    depth = w_in.shape[0]
    ncol = w_in.shape[2]
    chip = 2 * lax.axis_index("x") + lax.axis_index("y")

    taps = [conv_a_w, ssm_conv_w]
    taps_g = all_gather(taps, "chips", "gather_taps", slot_axis=1, own=False)
    wa_g, ws_g = [lax.dynamic_update_index_in_dim(g, s, chip, 1) for g, s in zip(taps_g, taps)]
    wa_full = jnp.transpose(wa_g, (0, 2, 1, 3)).reshape(depth, 3, D)
    ws_full = jnp.transpose(ws_g, (0, 2, 1, 3)).reshape(depth, 4, XBC)
    lane_pad = lambda a: jnp.pad(a, ((0, 0), (0, CH - a.shape[1])))
    par = jnp.stack([lane_pad(dt_bias), lane_pad(a_log), lane_pad(d_skip)], axis=1)
    par = jnp.pad(par, ((0, 0), (0, 5), (0, 0)))

    layer_shards = lambda l: [w_in[l].astype(bf16), w_out[l].astype(bf16), w_up[l].astype(bf16), w_down[l].astype(bf16)]
    issued = []

    def start(shards, name):
        zones = [lax.empty((N_CHIPS,) + s.shape, s.dtype) for s in shards]
        issued.append(chips_start(shards, zones, _weight_views(shards), name,
                                  after=issued[-1]["token"] if issued else taps_g[0]))
        return issued[-1]

    def finish(started, after, name):
        shards, zones = chips_wait(started, after, name)
        zones = weights_share(zones, "weights_share")
        return [lax.dynamic_update_index_in_dim(z, s, chip, 0) for z, s in zip(zones, shards)]

    def shaped(mats):
        wo_z, wu_z, wd_z = mats
        return wo_z.reshape(2 * D, D), wu_z, wd_z.reshape(DFF, D)

    first = layer_shards(0)
    travelling = {0: start(first[:1], "weights_start_0")}
    rest = start(first[1:], "weights_start_0_rest")
    for l in range(1, depth):
        travelling[l] = start(layer_shards(l), f"weights_start_{l}")

    def weights_of(l, x_in):
        mats = finish(travelling.pop(l), x_in, f"weights_wait_{l}")
        w = dict(win=assemble_columns(mats[0], PROJ, "assemble_w_in"), wa=jnp.pad(wa_full[l], ((0, 5), (0, 0))),
                 ws=jnp.pad(ws_full[l], ((0, 4), (0, 0))), bs=ssm_conv_b[l][None], par=par[l],
                 g1=norm_mix_pre[l][None], ga=conv_out_norm[l][None], gs=ssm_out_norm[l][None],
                 g2=norm_mix_post[l][None], g3=norm_mlp_pre[l][None], g4=norm_mlp_post[l][None])
        if l == 0:
            w["token"] = issued[-1]["token"]
            w["late"] = lambda after: dict(zip(("wo", "wu", "wd"), shaped(finish(rest, after, "weights_wait_0_rest"))))
        else:
            w.update(zip(("wo", "wu", "wd"), shaped(mats[1:])))
        return w

    core = lax.axis_index("c")
    grads_travelling = {}
    given_m = dict(win=m_w_in, wo=m_w_out, wu=m_w_up, wd=m_w_down)
    given_v = dict(win=v_w_in, wo=v_w_out, wu=v_w_up, wd=v_w_down)

    chip_major = dict(win=lambda a: a[None], wo=lambda a: a.reshape(N_CHIPS, 2 * D // N_CHIPS, D), wu=lambda a: a,
                      wd=lambda a: a.reshape(N_CHIPS, DFF // N_CHIPS, D))
    held = {}

    def grads_done(l, g, last):
        if l > 0 and not last:
            held[l] = g
            return None
        g = {**held.pop(l, {}), **g}
        keys = [k for k in ("win", "wo", "wu", "wd") if k in g]
        mats = [chip_major[k](g[k]) for k in keys]
        received = pair_send_halves(mats, "grads_to_pair")
        sums = [sum_pair_half(m_, r_, core, "pair_sum", by_chip=(N_CHIPS, ncol) if k == "win" else None)
                for k, m_, r_ in zip(keys, mats, received)]
        zones = [lax.empty(s.shape, s.dtype) for s in sums]
        started = chips_start(sums, zones, _grad_views(sums), f"grads_start_{l}_{len(grads_travelling)}")
        grads_travelling[(l, keys[0])] = (keys, started)
        return started["token"]

    sse, dx, grads = local_step(x.reshape(t, D), loss_target.reshape(t, D), depth, weights_of, seq, grads_done)
    loss = lax.psum(0.5 / D * sse[0, 0], ("x", "y", "c"))

    packed = _pack_small(grads)
    small_travelling = chips_start([packed], [lax.empty((8,) + packed.shape, f32)], _whole_views([packed]),
                                   "small_start", group="all")

    big_w = dict(win=w_in, wo=w_out, wu=w_up, wd=w_down)
    acc = {k: lax.empty((depth, bw.shape[1] // 2, bw.shape[2]), f32) for k, bw in big_w.items()}
    for n, ((l, _), (keys, started)) in enumerate(grads_travelling.items()):
        sums, zones = chips_wait(started, small_travelling["token"], f"grads_wait_{l}_{n}")
        for k, s, z in zip(keys, sums, zones):
            acc[k] = chip_sum_into(acc[k], l, s, z, chip, "chip_sum")
    names = ("win", "wo", "wu", "wd")
    acc = [acc[k] for k in names]
    pair_views = dict(src=lambda ref, a, c, to: ref, dst=lambda ref, a, c, sender: ref, rows=lambda a: depth)
    to_sibling = chips_start(acc, [lax.empty(a.shape, f32) for a in acc], pair_views, "grads_from_pair_start",
                             pieces=depth, group="pair")
    own_done = {}
    for k, a in zip(names, to_sibling["sources"]):
        if big_w[k].shape[-1] % CH == 0:
            own_done[k] = adamw_half(big_w[k], a, given_m[k], given_v[k], core, "adamw_matrix",
                                     token=to_sibling["token"])

    (packed,), (small_all,) = chips_wait(small_travelling, own_done["wd"][1], "small_wait")
    small_all = lax.dynamic_update_index_in_dim(small_all, packed, 4 * lax.axis_index("x") + 2 * lax.axis_index("y") + core, 0)
    small = _unpack_small(sum_slots(small_all, f32, "small_sum", tb=8), depth)
    wa_cols, ws_cols = conv_a_w.shape[2], ssm_conv_w.shape[2]
    par_g = small["par"].reshape(depth, 3, CH)
    g_small = dict(
        norm_mix_pre=small["g1"], conv_out_norm=small["ga"], ssm_out_norm=small["gs"], norm_mix_post=small["g2"],
        norm_mlp_pre=small["g3"], norm_mlp_post=small["g4"], ssm_conv_b=small["bs"],
        conv_a_w=lax.dynamic_slice_in_dim(small["wa"].reshape(depth, 3, D), chip * wa_cols, wa_cols, axis=2),
        ssm_conv_w=lax.dynamic_slice_in_dim(small["ws"].reshape(depth, 4, XBC), chip * ws_cols, ws_cols, axis=2),
        dt_bias=par_g[:, 0, :NH], a_log=par_g[:, 1, :NH], d_skip=par_g[:, 2, :NH])

    given = dict(norm_mix_pre=(norm_mix_pre, m_norm_mix_pre, v_norm_mix_pre), w_in=(w_in, m_w_in, v_w_in),
                 conv_a_w=(conv_a_w, m_conv_a_w, v_conv_a_w), ssm_conv_w=(ssm_conv_w, m_ssm_conv_w, v_ssm_conv_w),
                 ssm_conv_b=(ssm_conv_b, m_ssm_conv_b, v_ssm_conv_b), dt_bias=(dt_bias, m_dt_bias, v_dt_bias),
                 a_log=(a_log, m_a_log, v_a_log), d_skip=(d_skip, m_d_skip, v_d_skip),
                 conv_out_norm=(conv_out_norm, m_conv_out_norm, v_conv_out_norm),
                 ssm_out_norm=(ssm_out_norm, m_ssm_out_norm, v_ssm_out_norm), w_out=(w_out, m_w_out, v_w_out),
                 norm_mix_post=(norm_mix_post, m_norm_mix_post, v_norm_mix_post),
                 norm_mlp_pre=(norm_mlp_pre, m_norm_mlp_pre, v_norm_mlp_pre), w_up=(w_up, m_w_up, v_w_up),
                 w_down=(w_down, m_w_down, v_w_down), norm_mlp_post=(norm_mlp_post, m_norm_mlp_post, v_norm_mlp_post))
    order = ["norm_mix_pre", "w_in", "conv_a_w", "ssm_conv_w", "ssm_conv_b", "dt_bias", "a_log", "d_skip",
             "conv_out_norm", "ssm_out_norm", "w_out", "norm_mix_post", "norm_mlp_pre", "w_up", "w_down",
             "norm_mlp_post"]
    short = dict(w_in="win", w_out="wo", w_up="wu", w_down="wd")
    results = {}
    for n in order:
        if n in short:
            continue
        wv, mv, vv = given[n]
        gv = g_small[n].reshape(wv.shape)
        two_d = lambda a: a.reshape(-1, a.shape[-1])
        results[n] = (gv,) + tuple(adamw(two_d(wv), two_d(gv), two_d(mv), two_d(vv), "adamw"))

    acc, from_sibling = chips_wait(to_sibling, results["norm_mlp_post"][1], "grads_from_pair_wait")
    for n, k in short.items():
        wv, mv, vv = given[n]
        own, recv = acc[names.index(k)], from_sibling[names.index(k)]
        if k in own_done:
            results[n] = adamw_half(wv, recv, mv, vv, 1 - core, "adamw_matrix", before=own_done[k])
        else:
            to_cols, to_rows = (lambda a: jnp.transpose(a, (2, 0, 1))), (lambda a: jnp.transpose(a, (1, 2, 0)))
            own_c, recv_c = to_cols(own), to_cols(recv)
            g_cols = jnp.where(core == 0, jnp.concatenate([own_c, recv_c], axis=2),
                               jnp.concatenate([recv_c, own_c], axis=2))
            results[n] = tuple(to_rows(o) for o in (g_cols,) + tuple(adamw_leading(to_cols(wv), g_cols, to_cols(mv),
                                                                                   to_cols(vv), "adamw_cols")))
    g_out, d_out, m_out, v_out = [], [], [], []
    for n in order:
        wv = given[n][0]
        gv, dlt, m2, v2 = results[n]
        g_out.append(gv.reshape(wv.shape))
        d_out.append(dlt.reshape(wv.shape))
        m_out.append(m2.reshape(wv.shape))
        v_out.append(v2.reshape(wv.shape))
    return (loss, dx.reshape(nb, seq, D), *g_out, *d_out, *m_out, *v_out)
```

```python
import functools

import jax
import jax.numpy as jnp
from jax import lax
from jax.experimental import pallas as pl
from jax.experimental.pallas import tpu as pltpu

f32, bf16 = jnp.float32, jnp.bfloat16

D = 1024
NH, HP = 16, 64
NG, NS = 2, 128
CH = 128
XBC = D + 2 * NG * NS
DFF = 4 * D
IN_COLS = 3 * D + D + XBC + NH
PROJ = 5760
COL_Z, COL_XBC, COL_DT = 3 * D, 4 * D, 4 * D + XBC
EPS = 1e-6
HALO = 8
HBLK = 16
VMEM_LIMIT = 56 * 2**20
MESH = pl.DeviceIdType.MESH

LR, B1, B2, AEPS, WD, STEP = 0.001, 0.9, 0.999, 1e-08, 0.01, 10


def _cparams(n_axes):
    return pltpu.CompilerParams(dimension_semantics=("arbitrary",) * n_axes, vmem_limit_bytes=VMEM_LIMIT)


def _sds(shape, dtype):
    return jax.ShapeDtypeStruct(tuple(shape), dtype)


def _token_spec(token):
    return [] if token is None else [pl.BlockSpec(memory_space=pl.ANY)]


def _token_arg(token):
    return [] if token is None else [token]


def _resident(shape):
    return pl.BlockSpec(shape, lambda i: (0,) * len(shape), pipeline_mode=pl.Buffered(1))


def _rms_fwd(x, g):
    r = lax.rsqrt(jnp.mean(x * x, axis=-1, keepdims=True) + EPS)
    return x * r * g


def _rms_bwd(x, g, dy):
    r = lax.rsqrt(jnp.mean(x * x, axis=-1, keepdims=True) + EPS)
    xh = x * r
    gdy = dy * g
    dx = r * (gdy - xh * jnp.mean(xh * gdy, axis=-1, keepdims=True))
    return dx, dy * xh


def _accum(ref, part, first):
    @pl.when(first)
    def _():
        ref[...] = part

    @pl.when(jnp.logical_not(first))
    def _():
        ref[...] += part


def _dot_nt(a, b):
    return lax.dot_general(a, b, (((1,), (1,)), ((), ())), preferred_element_type=f32)


def _dot_tn(a, b):
    return lax.dot_general(a, b, (((0,), (0,)), ((), ())), preferred_element_type=f32)


def _dot(a, b):
    return jnp.dot(a, b, preferred_element_type=f32)


def _split_dot(x, e_bf, n_split, nt=False):
    acc = None
    rem = x
    for s in range(n_split):
        hi = rem.astype(bf16)
        term = _dot_nt(hi, e_bf) if nt else _dot(hi, e_bf)
        acc = term if acc is None else acc + term
        if s + 1 < n_split:
            rem = rem - hi.astype(f32)
    return acc


def _sigmoid(x):
    return 0.5 * jnp.tanh(0.5 * x) + 0.5


def norm_matmul(x, g, w, tm, tn, out_dtype, name, token=None):
    t, n = x.shape[0], w.shape[1]
    w_spec = pl.BlockSpec((D, tn), lambda i, j: (0, j))

    def body(x_ref, g_ref, w_ref, *rest):
        o_ref, h_ref = rest[-2:]

        @pl.when(pl.program_id(1) == 0)
        def _():
            h_ref[...] = _rms_fwd(x_ref[...], g_ref[...]).astype(bf16)

        o_ref[...] = _dot(h_ref[...], w_ref[...]).astype(out_dtype)

    return pl.pallas_call(
        body, name=name, grid=(t // tm, n // tn),
        in_specs=[pl.BlockSpec((tm, D), lambda i, j: (i, 0)), pl.BlockSpec((1, D), lambda i, j: (0, 0)), w_spec]
        + _token_spec(token),
        out_specs=[pl.BlockSpec((tm, tn), lambda i, j: (i, j)), pl.BlockSpec((tm, D), lambda i, j: (i, 0))],
        out_shape=[_sds((t, n), out_dtype), _sds((t, D), bf16)],
        compiler_params=_cparams(2))(x, g, w, *_token_arg(token))


def matmul_postnorm(a, w, xres, g, tm, name):
    t, k = a.shape

    def body(a_ref, w_ref, xr_ref, g_ref, y_ref, xo_ref):
        y = _dot(a_ref[...], w_ref[...])
        y_ref[...] = y.astype(bf16)
        xo_ref[...] = xr_ref[...] + _rms_fwd(y, g_ref[...])

    return pl.pallas_call(
        body, name=name, grid=(t // tm,),
        in_specs=[pl.BlockSpec((tm, k), lambda i: (i, 0)), _resident(w.shape),
                  pl.BlockSpec((tm, D), lambda i: (i, 0)), pl.BlockSpec((1, D), lambda i: (0, 0))],
        out_specs=[pl.BlockSpec((tm, D), lambda i: (i, 0)), pl.BlockSpec((tm, D), lambda i: (i, 0))],
        out_shape=[_sds((t, D), bf16), _sds((t, D), f32)],
        compiler_params=_cparams(1))(a, w, xres, g)


def postnorm_bwd_matmul(y, g, dxo, w, tm, tn, name):
    t, n = y.shape[0], w.shape[0]

    def body(y_ref, g_ref, dxo_ref, w_ref, dy_ref, dg_ref, da_ref):
        i, j = pl.program_id(0), pl.program_id(1)

        @pl.when(j == 0)
        def _():
            dx, dgc = _rms_bwd(y_ref[...].astype(f32), g_ref[...], dxo_ref[...])
            dy_ref[...] = dx.astype(bf16)
            _accum(dg_ref, jnp.sum(dgc, axis=0, keepdims=True), i == 0)

        da_ref[...] = _dot_nt(dy_ref[...], w_ref[...]).astype(bf16)

    return pl.pallas_call(
        body, name=name, grid=(t // tm, n // tn),
        in_specs=[pl.BlockSpec((tm, D), lambda i, j: (i, 0)), pl.BlockSpec((1, D), lambda i, j: (0, 0)),
                  pl.BlockSpec((tm, D), lambda i, j: (i, 0)), pl.BlockSpec((tn, D), lambda i, j: (j, 0))],
        out_specs=[pl.BlockSpec((tm, D), lambda i, j: (i, 0)), pl.BlockSpec((1, D), lambda i, j: (0, 0)),
                   pl.BlockSpec((tm, tn), lambda i, j: (i, j))],
        out_shape=[_sds((t, D), bf16), _sds((1, D), f32), _sds((t, n), bf16)],
        compiler_params=_cparams(2))(y, g, dxo, w)


def matmul_prenorm_bwd(da, w, x, g, dxo, tm, name, token=None):
    t, k = da.shape

    def body(da_ref, w_ref, x_ref, g_ref, dxo_ref, *rest):
        dx_ref, dg_ref = rest[-2:]
        dh = _dot_nt(da_ref[...], w_ref[...])
        dxn, dgc = _rms_bwd(x_ref[...], g_ref[...], dh)
        dx_ref[...] = dxo_ref[...] + dxn
        _accum(dg_ref, jnp.sum(dgc, axis=0, keepdims=True), pl.program_id(0) == 0)

    return pl.pallas_call(
        body, name=name, grid=(t // tm,),
        in_specs=[pl.BlockSpec((tm, k), lambda i: (i, 0)), _resident(w.shape),
                  pl.BlockSpec((tm, D), lambda i: (i, 0)), pl.BlockSpec((1, D), lambda i: (0, 0)),
                  pl.BlockSpec((tm, D), lambda i: (i, 0))] + _token_spec(token),
        out_specs=[pl.BlockSpec((tm, D), lambda i: (i, 0)), pl.BlockSpec((1, D), lambda i: (0, 0))],
        out_shape=[_sds((t, D), f32), _sds((1, D), f32)],
        compiler_params=_cparams(1))(da, w, x, g, dxo, *_token_arg(token))


def mlp_fwd(x, g_pre, wu, wd, g_post, tm, name):
    t = x.shape[0]
    nq, _, fc = wu.shape

    def body(x_ref, gp_ref, wu_ref, wd_ref, gq_ref, fp_ref, h_ref, o_ref, xo_ref):
        xv = x_ref[...]
        h = _rms_fwd(xv, gp_ref[...]).astype(bf16)
        h_ref[...] = h
        o = None
        for q in range(nq):
            fq = _dot(h, wu_ref[q])
            fp_ref[:, q * fc:(q + 1) * fc] = fq.astype(bf16)
            r = jnp.maximum(fq, 0.0)
            part = _dot((r * r).astype(bf16), wd_ref[q * fc:(q + 1) * fc, :])
            o = part if o is None else o + part
        o_ref[...] = o.astype(bf16)
        xo_ref[...] = xv + _rms_fwd(o, gq_ref[...])

    row = lambda c: pl.BlockSpec((tm, c), lambda i: (i, 0))
    vec = pl.BlockSpec((1, D), lambda i: (0, 0))
    return pl.pallas_call(
        body, name=name, grid=(t // tm,),
        in_specs=[row(D), vec, _resident(wu.shape), _resident(wd.shape), vec],
        out_specs=[row(nq * fc), row(D), row(D), row(D)],
        out_shape=[_sds((t, nq * fc), bf16), _sds((t, D), bf16), _sds((t, D), bf16), _sds((t, D), f32)],
        compiler_params=_cparams(1))(x, g_pre, wu, wd, g_post)


def mlp_bwd(o, g_post, dxo, wd, fp, wu, x, g_pre, tm, name):
    t = x.shape[0]
    nq, _, fc = wu.shape

    def body(o_ref, gq_ref, dxo_ref, wd_ref, fp_ref, wu_ref, x_ref, gp_ref, do_ref, dfp_ref, dx_ref, dgq_ref, dgp_ref):
        i = pl.program_id(0)
        dxo_v = dxo_ref[...]
        do, dgq = _rms_bwd(o_ref[...].astype(f32), gq_ref[...], dxo_v)
        do_b = do.astype(bf16)
        do_ref[...] = do_b
        dh = None
        for q in range(nq):
            cols = slice(q * fc, (q + 1) * fc)
            dq = _dot_nt(do_b, wd_ref[cols, :]) * (2.0 * jnp.maximum(fp_ref[:, cols].astype(f32), 0.0))
            dq_b = dq.astype(bf16)
            dfp_ref[:, cols] = dq_b
            part = _dot_nt(dq_b, wu_ref[q])
            dh = part if dh is None else dh + part
        dxn, dgp = _rms_bwd(x_ref[...], gp_ref[...], dh)
        dx_ref[...] = dxo_v + dxn
        _accum(dgq_ref, jnp.sum(dgq, axis=0, keepdims=True), i == 0)
        _accum(dgp_ref, jnp.sum(dgp, axis=0, keepdims=True), i == 0)

    row = lambda c: pl.BlockSpec((tm, c), lambda i: (i, 0))
    vec = pl.BlockSpec((1, D), lambda i: (0, 0))
    return pl.pallas_call(
        body, name=name, grid=(t // tm,),
        in_specs=[row(D), vec, row(D), _resident(wd.shape), row(nq * fc), _resident(wu.shape), row(D), vec],
        out_specs=[row(D), row(nq * fc), row(D), vec, vec],
        out_shape=[_sds((t, D), bf16), _sds((t, nq * fc), bf16), _sds((t, D), f32), _sds((1, D), f32), _sds((1, D), f32)],
        compiler_params=_cparams(1))(o, g_post, dxo, wd, fp, wu, x, g_pre)


def matmul_tn(a, b, tm, tn, relu2, name, col_blocks=False):
    t, m = a.shape
    n = b.shape[1]
    if col_blocks:
        out_spec, out_shape = pl.BlockSpec((None, tm, tn), lambda i, j: (j, i, 0)), _sds((n // tn, m, tn), bf16)
    else:
        out_spec, out_shape = pl.BlockSpec((tm, tn), lambda i, j: (i, j)), _sds((m, n), bf16)

    def body(a_ref, b_ref, o_ref, at_ref):
        @pl.when(pl.program_id(1) == 0)
        def _():
            av = a_ref[...]
            if relu2:
                af = jnp.maximum(av.astype(f32), 0.0)
                av = (af * af).astype(bf16)
            at_ref[...] = av.T

        o_ref[...] = _dot(at_ref[...], b_ref[...]).astype(bf16)

    return pl.pallas_call(
        body, name=name, grid=(m // tm, n // tn),
        in_specs=[pl.BlockSpec((t, tm), lambda i, j: (0, i)), pl.BlockSpec((t, tn), lambda i, j: (0, j))],
        out_specs=out_spec, out_shape=out_shape,
        scratch_shapes=[pltpu.VMEM((tm, t), bf16)],
        compiler_params=_cparams(2))(a, b)


ROWS_A = 16
ROWS_B = 32
UNROLL = 4


def _past(win, s):
    return (win if s == 0 else pltpu.roll(win, s, 0))[HALO:]


def _future(win, s):
    n = win.shape[0]
    return (win if s == 0 else pltpu.roll(win, n - s, 0))[:n - HALO]


def _fold8(v):
    return v.reshape(v.shape[0] // 8, 8, v.shape[1]).sum(axis=0)


def _last8(ref):
    return ref[...].astype(f32)[HBLK - HALO:]


def _first8(ref):
    return ref[...].astype(f32)[:HALO]


def _rd(ref, rows):
    return ref[rows, :].astype(f32)


def _halo_prev(tb, col):
    return lambda i: (jnp.maximum(i * (tb // HBLK) - 1, 0), col)


def _halo_next(tb, col, t):
    return lambda i: (jnp.minimum((i + 1) * (tb // HBLK), t // HBLK - 1), col)


def group_a_fwd(proj, wa, g, seq, tb, name):
    t = proj.shape[0]
    bps = seq // tb

    def body(xa_ref, ca_ref, ba_ref, xah_ref, cah_ref, wa_ref, g_ref, o_ref, u_scr):
        first = (pl.program_id(0) % bps) == 0
        u_scr[0:HALO, :] = jnp.where(first, 0.0, _last8(cah_ref) * _last8(xah_ref))
        w, gv = wa_ref[...], g_ref[...]

        def chunk(i, carry):
            r = pl.multiple_of(i * ROWS_A, ROWS_A)
            rows = pl.ds(r, ROWS_A)
            u_scr[pl.ds(pl.multiple_of(HALO + r, HALO), ROWS_A), :] = _rd(ca_ref, rows) * _rd(xa_ref, rows)
            win = u_scr[pl.ds(r, ROWS_A + HALO), :]
            cv = w[2:3] * _past(win, 0) + w[1:2] * _past(win, 1) + w[0:1] * _past(win, 2)
            o_ref[rows, :] = _rms_fwd(_rd(ba_ref, rows) * cv, gv).astype(bf16)
            return carry

        lax.fori_loop(0, tb // ROWS_A, chunk, 0, unroll=UNROLL)

    blk = lambda c: pl.BlockSpec((tb, D), lambda i: (i, c))
    return pl.pallas_call(
        body, name=name, grid=(t // tb,),
        in_specs=[blk(0), blk(1), blk(2),
                  pl.BlockSpec((HBLK, D), _halo_prev(tb, 0)), pl.BlockSpec((HBLK, D), _halo_prev(tb, 1)),
                  pl.BlockSpec((8, D), lambda i: (0, 0)), pl.BlockSpec((1, D), lambda i: (0, 0))],
        out_specs=pl.BlockSpec((tb, D), lambda i: (i, 0)),
        out_shape=_sds((t, 2 * D), bf16),
        scratch_shapes=[pltpu.VMEM((tb + HALO, D), f32)],
        compiler_params=_cparams(1))(proj, proj, proj, proj, proj, wa, g)


def group_a_bwd(proj, dcat, wa, g, seq, tb, name, token=None):
    t = proj.shape[0]
    bps = seq // tb

    def body(xa_ref, ca_ref, ba_ref, dy_ref, xap_ref, cap_ref, xan_ref, can_ref, ban_ref, dyn_ref, wa_ref, g_ref,
             *rest):
        dp_ref, dwa_ref, dg_ref, u_scr, d_scr, acc_scr = rest[-6:]
        i = pl.program_id(0)
        first = (i % bps) == 0
        last = (i % bps) == bps - 1
        w = wa_ref[...]
        gv = g_ref[...]
        u_scr[0:HALO, :] = jnp.where(first, 0.0, _last8(cap_ref) * _last8(xap_ref))
        u_scr[HALO + tb:2 * HALO + tb, :] = _first8(can_ref) * _first8(xan_ref)
        acc_scr[...] = jnp.zeros_like(acc_scr)

        def forward_part(n, carry):
            r = pl.multiple_of(n * ROWS_A, ROWS_A)
            rows = pl.ds(r, ROWS_A)
            ba = _rd(ba_ref, rows)
            u_scr[pl.ds(pl.multiple_of(HALO + r, HALO), ROWS_A), :] = _rd(ca_ref, rows) * _rd(xa_ref, rows)
            win = u_scr[pl.ds(r, ROWS_A + HALO), :]
            u = [_past(win, s) for s in range(3)]
            cv = w[2:3] * u[0] + w[1:2] * u[1] + w[0:1] * u[2]
            dya, dgc = _rms_bwd(ba * cv, gv, _rd(dy_ref, rows))
            dcv = dya * ba
            d_scr[rows, :] = dcv
            dp_ref[rows, 2 * D:3 * D] = (dya * cv).astype(bf16)
            acc_scr[0:8, :] += _fold8(dgc)
            for k in range(3):
                acc_scr[8 + 8 * k:16 + 8 * k, :] += _fold8(dcv * u[2 - k])
            return carry

        lax.fori_loop(0, tb // ROWS_A, forward_part, 0, unroll=UNROLL)

        start = HALO + tb
        cvn = (w[2:3] * u_scr[pl.ds(start, HALO), :] + w[1:2] * u_scr[pl.ds(start - 1, HALO), :]
               + w[0:1] * u_scr[pl.ds(start - 2, HALO), :])
        ban = _first8(ban_ref)
        dyan, _ = _rms_bwd(ban * cvn, gv, _first8(dyn_ref))
        d_scr[tb:tb + HALO, :] = jnp.where(last, 0.0, dyan * ban)

        def backward_part(n, carry):
            r = pl.multiple_of(n * ROWS_A, ROWS_A)
            rows = pl.ds(r, ROWS_A)
            win = d_scr[pl.ds(r, ROWS_A + HALO), :]
            du = w[2:3] * _future(win, 0) + w[1:2] * _future(win, 1) + w[0:1] * _future(win, 2)
            dp_ref[rows, 0:D] = (du * _rd(ca_ref, rows)).astype(bf16)
            dp_ref[rows, D:2 * D] = (du * _rd(xa_ref, rows)).astype(bf16)
            return carry

        lax.fori_loop(0, tb // ROWS_A, backward_part, 0, unroll=UNROLL)

        row = lax.broadcasted_iota(jnp.int32, (8, D), 0)
        dw = jnp.zeros((8, D), f32)
        for k in range(3):
            dw = jnp.where(row == k, jnp.sum(acc_scr[8 + 8 * k:16 + 8 * k, :], axis=0, keepdims=True), dw)
        _accum(dwa_ref, dw, i == 0)
        _accum(dg_ref, jnp.sum(acc_scr[0:8, :], axis=0, keepdims=True), i == 0)

    blk = lambda c: pl.BlockSpec((tb, D), lambda i: (i, c))
    prv = lambda c: pl.BlockSpec((HBLK, D), _halo_prev(tb, c))
    nxt = lambda c: pl.BlockSpec((HBLK, D), _halo_next(tb, c, t))
    return pl.pallas_call(
        body, name=name, grid=(t // tb,),
        in_specs=[blk(0), blk(1), blk(2), blk(0), prv(0), prv(1), nxt(0), nxt(1), nxt(2), nxt(0),
                  pl.BlockSpec((8, D), lambda i: (0, 0)), pl.BlockSpec((1, D), lambda i: (0, 0))] + _token_spec(token),
        out_specs=[pl.BlockSpec((tb, 3 * D), lambda i: (i, 0)), pl.BlockSpec((8, D), lambda i: (0, 0)),
                   pl.BlockSpec((1, D), lambda i: (0, 0))],
        out_shape=[_sds((t, PROJ), bf16), _sds((8, D), f32), _sds((1, D), f32)],
        scratch_shapes=[pltpu.VMEM((tb + 2 * HALO, D), f32), pltpu.VMEM((tb + HALO, D), f32), pltpu.VMEM((32, D), f32)],
        compiler_params=_cparams(1))(proj, proj, proj, dcat, proj, proj, proj, proj, proj, dcat, wa, g,
                                     *_token_arg(token))


CB = 512
XBC_BLK0 = COL_XBC // CB


def conv_b_fwd(proj, ws, bs, seq, tb, name):
    t = proj.shape[0]
    bps = seq // tb

    def body(x_ref, xp_ref, w_ref, b_ref, o_ref, da_ref, x_scr):
        first = (pl.program_id(1) % bps) == 0
        x_scr[0:HALO, :] = jnp.where(first, 0.0, _last8(xp_ref))
        w, bias = w_ref[...], b_ref[...]

        def chunk(n, carry):
            r = pl.multiple_of(n * ROWS_B, ROWS_B)
            rows = pl.ds(r, ROWS_B)
            x_scr[pl.ds(pl.multiple_of(HALO + r, HALO), ROWS_B), :] = _rd(x_ref, rows)
            win = x_scr[pl.ds(r, ROWS_B + HALO), :]
            xc = bias + w[3:4] * _past(win, 0)
            for k in range(3):
                xc = xc + w[k:k + 1] * _past(win, 3 - k)
            sg = _sigmoid(xc)
            o_ref[rows, :] = xc * sg
            da_ref[rows, :] = (sg * (1.0 + xc * (1.0 - sg))).astype(bf16)
            return carry

        lax.fori_loop(0, tb // ROWS_B, chunk, 0, unroll=UNROLL)

    return pl.pallas_call(
        body, name=name, grid=(XBC // CB, t // tb),
        in_specs=[pl.BlockSpec((tb, CB), lambda j, i: (i, XBC_BLK0 + j)),
                  pl.BlockSpec((HBLK, CB), lambda j, i: (jnp.maximum(i * (tb // HBLK) - 1, 0), XBC_BLK0 + j)),
                  pl.BlockSpec((8, CB), lambda j, i: (0, j)), pl.BlockSpec((1, CB), lambda j, i: (0, j))],
        out_specs=[pl.BlockSpec((tb, CB), lambda j, i: (i, j)), pl.BlockSpec((tb, CB), lambda j, i: (i, j))],
        out_shape=[_sds((t, XBC), f32), _sds((t, XBC), bf16)],
        scratch_shapes=[pltpu.VMEM((tb + HALO, CB), f32)],
        compiler_params=_cparams(2))(proj, proj, ws, bs)


def conv_b_bwd(proj, dxs, dact, ws, dproj, seq, tb, name):
    t = proj.shape[0]
    bps = seq // tb

    def body(x_ref, xp_ref, d_ref, dn_ref, a_ref, an_ref, w_ref, dproj_ref, dx_ref, dw_ref, db_ref, x_scr, d_scr,
             acc_scr):
        i = pl.program_id(1)
        first = (i % bps) == 0
        last = (i % bps) == bps - 1
        w = w_ref[...]
        x_scr[0:HALO, :] = jnp.where(first, 0.0, _last8(xp_ref))
        acc_scr[...] = jnp.zeros_like(acc_scr)

        def forward_part(n, carry):
            r = pl.multiple_of(n * ROWS_B, ROWS_B)
            rows = pl.ds(r, ROWS_B)
            x_scr[pl.ds(pl.multiple_of(HALO + r, HALO), ROWS_B), :] = _rd(x_ref, rows)
            win = x_scr[pl.ds(r, ROWS_B + HALO), :]
            dxc = _rd(d_ref, rows) * _rd(a_ref, rows)
            d_scr[rows, :] = dxc
            acc_scr[0:8, :] += _fold8(dxc)
            for k in range(4):
                acc_scr[8 + 8 * k:16 + 8 * k, :] += _fold8(dxc * _past(win, 3 - k))
            return carry

        lax.fori_loop(0, tb // ROWS_B, forward_part, 0, unroll=UNROLL)
        d_scr[tb:tb + HALO, :] = jnp.where(last, 0.0, _first8(dn_ref) * _first8(an_ref))

        def backward_part(n, carry):
            r = pl.multiple_of(n * ROWS_B, ROWS_B)
            win = d_scr[pl.ds(r, ROWS_B + HALO), :]
            dx = w[3:4] * _future(win, 0)
            for k in range(3):
                dx = dx + w[k:k + 1] * _future(win, 3 - k)
            dx_ref[pl.ds(r, ROWS_B), :] = dx.astype(bf16)
            return carry

        lax.fori_loop(0, tb // ROWS_B, backward_part, 0, unroll=UNROLL)

        row = lax.broadcasted_iota(jnp.int32, (8, CB), 0)
        dw = jnp.zeros((8, CB), f32)
        for k in range(4):
            dw = jnp.where(row == k, jnp.sum(acc_scr[8 + 8 * k:16 + 8 * k, :], axis=0, keepdims=True), dw)
        _accum(dw_ref, dw, i == 0)
        _accum(db_ref, jnp.sum(acc_scr[0:8, :], axis=0, keepdims=True), i == 0)

    nh = t // HBLK
    nxt = pl.BlockSpec((HBLK, CB), lambda j, i: (jnp.minimum((i + 1) * (tb // HBLK), nh - 1), j))
    cur = pl.BlockSpec((tb, CB), lambda j, i: (i, j))
    return pl.pallas_call(
        body, name=name, grid=(XBC // CB, t // tb),
        in_specs=[pl.BlockSpec((tb, CB), lambda j, i: (i, XBC_BLK0 + j)),
                  pl.BlockSpec((HBLK, CB), lambda j, i: (jnp.maximum(i * (tb // HBLK) - 1, 0), XBC_BLK0 + j)),
                  cur, nxt, cur, nxt, pl.BlockSpec((8, CB), lambda j, i: (0, j)), pl.BlockSpec(memory_space=pl.ANY)],
        out_specs=[pl.BlockSpec((tb, CB), lambda j, i: (i, XBC_BLK0 + j)), pl.BlockSpec((8, CB), lambda j, i: (0, j)),
                   pl.BlockSpec((1, CB), lambda j, i: (0, j))],
        out_shape=[_sds((t, PROJ), bf16), _sds((8, XBC), f32), _sds((1, XBC), f32)],
        input_output_aliases={7: 0},
        scratch_shapes=[pltpu.VMEM((tb + HALO, CB), f32), pltpu.VMEM((tb + HALO, CB), f32), pltpu.VMEM((40, CB), f32)],
        compiler_params=_cparams(2))(proj, proj, dxs, dxs, dact, dact, ws, dproj)


def place_columns(buf, part, col_block, tb, name):
    t, wdt = part.shape

    def body(p_ref, buf_ref, o_ref):
        o_ref[...] = p_ref[...]

    return pl.pallas_call(
        body, name=name, grid=(t // tb,),
        in_specs=[pl.BlockSpec((tb, wdt), lambda i: (i, 0)), pl.BlockSpec(memory_space=pl.ANY)],
        out_specs=pl.BlockSpec((tb, wdt), lambda i: (i, col_block)), out_shape=_sds(buf.shape, buf.dtype),
        input_output_aliases={1: 0}, compiler_params=_cparams(1))(part, buf)


GW = D // NG
EXPAND_TERMS = 2
REDUCE_TERMS = 1


def _ssd_consts():
    head_of_lane = jnp.arange(D) // HP
    expand = (jnp.arange(CH)[:, None] == head_of_lane[None, :]).astype(bf16)
    tri = (jnp.arange(CH)[:, None] >= jnp.arange(CH)[None, :]).astype(f32)
    return expand, tri


def _ssd_common(par_ref, dtr_ref, e_ref, tri_ref):
    par = par_ref[...]
    dtb, alog, dsk = par[0:1], par[1:2], par[2:3]
    lane = lax.broadcasted_iota(jnp.int32, (CH, CH), 1)
    a = -jnp.exp(alog)
    dtr = dtr_ref[...].astype(f32) + dtb
    sp = jnp.maximum(dtr, 0.0) + jnp.log(1.0 + jnp.exp(-jnp.abs(dtr)))
    dt = jnp.where(lane < NH, sp, 0.0)
    cs = jnp.dot(tri_ref[...], dt * a, precision=lax.Precision.HIGHEST, preferred_element_type=f32)
    cs_last = cs[CH - 1:CH, :]
    dte = jnp.exp(cs_last - cs)
    ecs = jnp.exp(cs)
    ecl = jnp.exp(cs_last)
    e = e_ref[...]
    row8 = lax.broadcasted_iota(jnp.int32, (8, CH), 0)
    r8 = _split_dot(jnp.where(row8 == 0, ecl, jnp.where(row8 == 1, dsk, 0.0)), e, 3)
    return dict(a=a, dtr=dtr, dt=dt, cs=cs, cst=cs.T, dte=dte, ecs=ecs, ecl=ecl, e=e, lane=lane,
                dt_x=_split_dot(dt, e, EXPAND_TERMS), dte_x=_split_dot(dte, e, EXPAND_TERMS),
                ecs_x=_split_dot(ecs, e, EXPAND_TERMS),
                ecl_x=r8[0:1], dsk_x=r8[1:2])


def _decay_matrix(c, h):
    li = lax.broadcasted_iota(jnp.int32, (CH, CH), 0)
    seg = c["cs"][:, h:h + 1] - c["cst"][h:h + 1, :]
    return jnp.exp(jnp.where(li >= c["lane"], seg, -jnp.inf))


def _gate_norm_fwd(y, z, gs):
    zg = z * _sigmoid(z)
    yg = y * zg
    return jnp.concatenate([_rms_fwd(yg[:, k * GW:(k + 1) * GW], gs[:, k * GW:(k + 1) * GW]) for k in range(NG)], axis=1)


def ssd_fwd(xbcs, proj, par, gs, cat, seq, name):
    t = xbcs.shape[0]
    nc = seq // CH
    expand, tri = _ssd_consts()

    def body(xs_ref, b_ref, c_ref, dtr_ref, z_ref, par_ref, e_ref, tri_ref, gs_ref, cat_ref, yn_ref, y_ref, st_ref,
             p_scr, yd_scr):
        @pl.when(pl.program_id(0) % nc == 0)
        def _():
            p_scr[...] = jnp.zeros_like(p_scr)

        c = _ssd_common(par_ref, dtr_ref, e_ref, tri_ref)
        xs = xs_ref[...]
        xdt = xs * c["dt_x"]
        xdt_b = xdt.astype(bf16)
        xdte_b = (xdt * c["dte_x"]).astype(bf16)
        p = p_scr[...]
        st_ref[0] = p
        p_b = p.astype(bf16)
        lo = c["lane"] < HP
        for g in range(NG):
            bg = b_ref[:, g * NS:(g + 1) * NS].astype(bf16)
            cg = c_ref[:, g * NS:(g + 1) * NS].astype(bf16)
            gmat = _dot_nt(cg, bg)
            for q in range(GW // CH):
                col = g * GW + q * CH
                xp = xdt_b[:, col:col + CH]
                h0 = col // HP
                m0 = (gmat * _decay_matrix(c, h0)).astype(bf16)
                m1 = (gmat * _decay_matrix(c, h0 + 1)).astype(bf16)
                stacked = jnp.concatenate([jnp.where(lo, xp, jnp.zeros_like(xp)),
                                           jnp.where(lo, jnp.zeros_like(xp), xp)], axis=0)
                yd_scr[:, col:col + CH] = _dot(jnp.concatenate([m0, m1], axis=1), stacked)
            gsl = slice(g * GW, (g + 1) * GW)
            yoff = _dot(cg, p_b[:, gsl]) * c["ecs_x"][:, gsl]
            yd_scr[:, gsl] = yd_scr[:, gsl] + yoff
            p_scr[:, gsl] = p[:, gsl] * c["ecl_x"][:, gsl] + _dot_tn(bg, xdte_b[:, gsl])
        y = yd_scr[...] + c["dsk_x"] * xs
        y_ref[...] = y
        yn_ref[...] = _gate_norm_fwd(y, z_ref[...].astype(f32), gs_ref[...]).astype(bf16)

    nb = t // CH
    return pl.pallas_call(
        body, name=name, grid=(nb,),
        in_specs=[pl.BlockSpec((CH, D), lambda i: (i, 0)),
                  pl.BlockSpec((CH, NG * NS), lambda i: (i, D // (NG * NS))),
                  pl.BlockSpec((CH, NG * NS), lambda i: (i, D // (NG * NS) + 1)),
                  pl.BlockSpec((CH, CH), lambda i: (i, COL_DT // CH)),
                  pl.BlockSpec((CH, D), lambda i: (i, COL_Z // D)),
                  pl.BlockSpec((8, CH), lambda i: (0, 0)), pl.BlockSpec((CH, D), lambda i: (0, 0)),
                  pl.BlockSpec((CH, CH), lambda i: (0, 0)), pl.BlockSpec((1, D), lambda i: (0, 0)),
                  pl.BlockSpec(memory_space=pl.ANY)],
        out_specs=[pl.BlockSpec((CH, D), lambda i: (i, 1)), pl.BlockSpec((CH, D), lambda i: (i, 0)),
                   pl.BlockSpec((1, NS, D), lambda i: (i, 0, 0))],
        out_shape=[_sds((t, 2 * D), bf16), _sds((t, D), f32), _sds((nb, NS, D), f32)],
        input_output_aliases={9: 0},
        scratch_shapes=[pltpu.VMEM((NS, D), f32), pltpu.VMEM((CH, D), f32)],
        compiler_params=_cparams(1))(xbcs, xbcs, xbcs, proj, proj, par, expand, tri, gs, cat)


def ssd_bwd(xbcs, proj, ypre, states, dcat, par, gs, dproj, seq, name):
    t = xbcs.shape[0]
    nc = seq // CH
    expand, tri = _ssd_consts()

    def body(xs_ref, b_ref, c_ref, dtr_ref, z_ref, y_ref, st_ref, dyn_ref, par_ref, e_ref, tri_ref, gs_ref, dproj_ref,
             dx_ref, dz_ref, ddt_ref, dpar_ref, dgs_ref, dp_scr, dxdt_scr):
        i = pl.program_id(0)

        @pl.when(i % nc == 0)
        def _():
            dp_scr[...] = jnp.zeros_like(dp_scr)

        c = _ssd_common(par_ref, dtr_ref, e_ref, tri_ref)
        e = c["e"]
        lane = c["lane"]
        sub = lax.broadcasted_iota(jnp.int32, (CH, CH), 0)
        xs = xs_ref[...]
        xdt = xs * c["dt_x"]
        xdt_b = xdt.astype(bf16)
        xdte_b = (xdt * c["dte_x"]).astype(bf16)
        p = st_ref[0]
        p_b = p.astype(bf16)
        dpn = dp_scr[...]
        dpn_b = dpn.astype(bf16)

        y, z, gs_v = y_ref[...], z_ref[...].astype(f32), gs_ref[...]
        zs = _sigmoid(z)
        zg = z * zs
        yg = y * zg
        parts, gparts = [], []
        for k in range(NG):
            sl = slice(k * GW, (k + 1) * GW)
            dxk, dgk = _rms_bwd(yg[:, sl], gs_v[:, sl], dyn_ref[:, sl].astype(f32))
            parts.append(dxk)
            gparts.append(dgk)
        dyg = jnp.concatenate(parts, axis=1)
        dgs_rows = jnp.concatenate(gparts, axis=1)
        dy = dyg * zg
        dz_ref[...] = (dyg * y * (zs * (1.0 + z * (1.0 - zs)))).astype(bf16)
        dy_b = dy.astype(bf16)
        dq_b = (dy * c["ecs_x"]).astype(bf16)

        lo = lane < HP
        dcs = jnp.zeros((CH, CH), f32)
        dcst = jnp.zeros((CH, CH), f32)
        for g in range(NG):
            gsl = slice(g * GW, (g + 1) * GW)
            bg = b_ref[:, g * NS:(g + 1) * NS].astype(bf16)
            cg = c_ref[:, g * NS:(g + 1) * NS].astype(bf16)
            gmat = _dot_nt(cg, bg)
            dgm = jnp.zeros((CH, CH), f32)
            for q in range(GW // CH):
                col = g * GW + q * CH
                xp = xdt_b[:, col:col + CH]
                dyp = dy_b[:, col:col + CH]
                zero = jnp.zeros_like(dyp)
                xp2 = jnp.concatenate([jnp.where(lo, xp, zero), jnp.where(lo, zero, xp)], axis=0)
                dy2 = jnp.concatenate([jnp.where(lo, dyp, zero), jnp.where(lo, zero, dyp)], axis=0)
                dm2 = _dot_nt(dyp, xp2)
                ms = []
                for hh in range(2):
                    h = col // HP + hh
                    dec = _decay_matrix(c, h)
                    m = gmat * dec
                    dm = dm2[:, hh * CH:(hh + 1) * CH]
                    dseg = dm * m
                    dcs = dcs + jnp.where(lane == h, jnp.sum(dseg, axis=1, keepdims=True), 0.0)
                    dcst = dcst + jnp.where(sub == h, jnp.sum(dseg, axis=0, keepdims=True), 0.0)
                    dgm = dgm + dm * dec
                    ms.append(m.astype(bf16))
                dxdt_scr[:, col:col + CH] = _dot_tn(jnp.concatenate(ms, axis=0), dy2)
            dgm_b = dgm.astype(bf16)
            bds = _dot(bg, dpn_b[:, gsl])
            dxdt_scr[:, gsl] = dxdt_scr[:, gsl] + c["dte_x"][:, gsl] * bds
            dc_g = _dot(dgm_b, bg) + _dot_nt(dq_b[:, gsl], p_b[:, gsl])
            db_g = _dot_tn(dgm_b, cg) + _dot_nt(xdte_b[:, gsl], dpn_b[:, gsl])
            dx_ref[:, D + g * NS:D + (g + 1) * NS] = db_g
            dx_ref[:, D + NG * NS + g * NS:D + NG * NS + (g + 1) * NS] = dc_g
            dp_scr[:, gsl] = dpn[:, gsl] * c["ecl_x"][:, gsl] + _dot_tn(cg, dq_b[:, gsl])
            q_g = _dot(cg, p_b[:, gsl])
            e_g = e[:, gsl]
            dcs = dcs + c["ecs"] * _split_dot(dy[:, gsl] * q_g, e_g, REDUCE_TERMS, nt=True)
            ddte = _split_dot(xdt[:, gsl] * bds, e_g, REDUCE_TERMS, nt=True) * c["dte"]
            dcs = dcs - ddte
            dcs = dcs + jnp.where(sub == CH - 1, jnp.sum(ddte, axis=0, keepdims=True), 0.0)

        decl = _split_dot(jnp.broadcast_to(jnp.sum(dpn * p, axis=0, keepdims=True), (8, D)), e, 2, nt=True)[0:1]
        dcs = dcs + jnp.where(sub == CH - 1, c["ecl"] * decl, 0.0)
        dcs = dcs - dcst.T
        dadt = lax.dot_general(tri_ref[...], dcs, (((0,), (0,)), ((), ())), precision=lax.Precision.HIGHEST,
                               preferred_element_type=f32)
        dxdt = dxdt_scr[...]
        ddt = dadt * c["a"] + _split_dot(dxdt * xs, e, REDUCE_TERMS, nt=True)
        ddtr = jnp.where(lane < NH, ddt * _sigmoid(c["dtr"]), 0.0)
        ddt_ref[...] = ddtr.astype(bf16)
        dx_ref[:, 0:D] = dxdt * c["dt_x"] + c["dsk_x"] * dy
        dsk = _split_dot(jnp.broadcast_to(jnp.sum(dy * xs, axis=0, keepdims=True), (8, D)), e, 2, nt=True)[0:1]
        dalog = jnp.sum(dadt * c["dt"], axis=0, keepdims=True) * c["a"]
        row8 = lax.broadcasted_iota(jnp.int32, (8, CH), 0)
        dpar = jnp.where(row8 == 0, jnp.sum(ddtr, axis=0, keepdims=True),
                         jnp.where(row8 == 1, dalog, jnp.where(row8 == 2, dsk, 0.0)))
        dpar = jnp.where(lax.broadcasted_iota(jnp.int32, (8, CH), 1) < NH, dpar, 0.0)
        _accum(dpar_ref, dpar, i == 0)
        _accum(dgs_ref, jnp.sum(dgs_rows, axis=0, keepdims=True), i == 0)

    nb = t // CH
    rev = lambda i: (i // nc) * nc + (nc - 1 - i % nc)
    return pl.pallas_call(
        body, name=name, grid=(nb,),
        in_specs=[pl.BlockSpec((CH, D), lambda i: (rev(i), 0)),
                  pl.BlockSpec((CH, NG * NS), lambda i: (rev(i), D // (NG * NS))),
                  pl.BlockSpec((CH, NG * NS), lambda i: (rev(i), D // (NG * NS) + 1)),
                  pl.BlockSpec((CH, CH), lambda i: (rev(i), COL_DT // CH)),
                  pl.BlockSpec((CH, D), lambda i: (rev(i), COL_Z // D)),
                  pl.BlockSpec((CH, D), lambda i: (rev(i), 0)),
                  pl.BlockSpec((1, NS, D), lambda i: (rev(i), 0, 0)),
                  pl.BlockSpec((CH, D), lambda i: (rev(i), 1)),
                  pl.BlockSpec((8, CH), lambda i: (0, 0)), pl.BlockSpec((CH, D), lambda i: (0, 0)),
                  pl.BlockSpec((CH, CH), lambda i: (0, 0)), pl.BlockSpec((1, D), lambda i: (0, 0)),
                  pl.BlockSpec(memory_space=pl.ANY)],
        out_specs=[pl.BlockSpec((CH, XBC), lambda i: (rev(i), 0)), pl.BlockSpec((CH, D), lambda i: (rev(i), COL_Z // D)),
                   pl.BlockSpec((CH, CH), lambda i: (rev(i), 0)),
                   pl.BlockSpec((8, CH), lambda i: (0, 0)), pl.BlockSpec((1, D), lambda i: (0, 0))],
        out_shape=[_sds((t, XBC), f32), _sds((t, PROJ), bf16), _sds((t, CH), bf16), _sds((8, CH), f32), _sds((1, D), f32)],
        input_output_aliases={12: 1},
        scratch_shapes=[pltpu.VMEM((NS, D), f32), pltpu.VMEM((CH, D), f32)],
        compiler_params=_cparams(1))(xbcs, xbcs, xbcs, proj, proj, ypre, states, dcat, par, expand, tri, gs, dproj)


def loss_head(y, target, tb, name):
    t = y.shape[0]

    def body(y_ref, t_ref, s_ref, dy_ref):
        err = y_ref[...] - t_ref[...]
        dy_ref[...] = err * (1.0 / D)
        _accum(s_ref, jnp.zeros((8, CH), f32) + jnp.sum(err * err), pl.program_id(0) == 0)

    return pl.pallas_call(
        body, name=name, grid=(t // tb,),
        in_specs=[pl.BlockSpec((tb, D), lambda i: (i, 0)), pl.BlockSpec((tb, D), lambda i: (i, 0))],
        out_specs=[pl.BlockSpec((8, CH), lambda i: (0, 0)), pl.BlockSpec((tb, D), lambda i: (i, 0))],
        out_shape=[_sds((8, CH), f32), _sds((t, D), f32)],
        compiler_params=_cparams(1))(y, target)


def _tiles(t, seq):
    tm = min(512, t)
    return dict(tm=tm, tm_small=min(256, t), tm_large=min(1024, t), tm_huge=min(2048, t), tb=min(512, seq))


def local_step(x, target, depth, weights_of, seq, grads_done=None):
    t = x.shape[0]
    ts = _tiles(t, seq)
    tm, tl, th, tb = ts["tm"], ts["tm_large"], ts["tm_huge"], ts["tb"]
    saved, ws = [], []
    for l in range(depth):
        w = weights_of(l, x)
        ws.append(w)
        proj, h1 = norm_matmul(x, w["g1"], w["win"], th, 1920, bf16, "in_proj", token=w.get("token"))
        cat = group_a_fwd(proj, w["wa"], w["ga"], seq, tb, "group_a_fwd")
        xbcs, dact = conv_b_fwd(proj, w["ws"], w["bs"], seq, tb, "conv_b_fwd")
        cat, ypre, states = ssd_fwd(xbcs, proj, w["par"], w["gs"], cat, seq, "ssd_fwd")
        if "late" in w:
            w.update(w.pop("late")(cat))
        mix, x2 = matmul_postnorm(cat, w["wo"], x, w["g2"], tl, "out_proj")
        fp, h2, o, x3 = mlp_fwd(x2, w["g3"], w["wu"], w["wd"], w["g4"], tm, "mlp_fwd")
        saved.append(dict(x=x, proj=proj, h1=h1, xbcs=xbcs, dact=dact, ypre=ypre, states=states, cat=cat, mix=mix, x2=x2,
                          fp=fp, h2=h2, o=o))
        x = x3
    sse, dx = loss_head(x, target, tm, "loss_head")
    grads = [None] * depth
    for l in reversed(range(depth)):
        s, w = saved[l], ws[l]
        do, dfp, dx2, dg4, dg3 = mlp_bwd(s["o"], w["g4"], dx, w["wd"], s["fp"], w["wu"], s["x2"], w["g3"],
                                         tm, "mlp_bwd")
        dwd = matmul_tn(s["fp"], do, 512, 1024, True, "mlp_down_dw")
        dwu = matmul_tn(s["h2"], dfp, tl, 1024, False, "mlp_up_dw", col_blocks=True)
        dmix, dg2, dcat = postnorm_bwd_matmul(s["mix"], w["g2"], dx2, w["wo"], tl, 2 * D, "out_proj_bwd")
        dwo = matmul_tn(s["cat"], dmix, 512, 1024, False, "out_proj_dw")
        token = None if grads_done is None else grads_done(l, dict(wo=dwo, wu=dwu, wd=dwd), False)
        dproj, dwa, dga = group_a_bwd(s["proj"], dcat, w["wa"], w["ga"], seq, tb, "group_a_bwd", token=token)
        dxbcs, dproj, ddt, dpar, dgs = ssd_bwd(s["xbcs"], s["proj"], s["ypre"], s["states"], dcat, w["par"], w["gs"],
                                               dproj, seq, "ssd_bwd")
        dproj, dws, dbs = conv_b_bwd(s["proj"], dxbcs, s["dact"], w["ws"], dproj, seq, tb, "conv_b_bwd")
        dproj = place_columns(dproj, ddt, COL_DT // CH, tm, "place_ddt")
        dwin = matmul_tn(s["h1"], dproj, tl, 1152, False, "in_proj_dw")
        token = None if grads_done is None else grads_done(l, dict(win=dwin), True)
        dx, dg1 = matmul_prenorm_bwd(dproj, w["win"], s["x"], w["g1"], dx2, tm, "in_proj_bwd", token=token)
        grads[l] = dict(win=dwin, wo=dwo, wu=dwu, wd=dwd, wa=dwa, ws=dws, bs=dbs, par=dpar,
                        g1=dg1, ga=dga, gs=dgs, g2=dg2, g3=dg3, g4=dg4)
    return sse, dx, grads


GROUPS = {
    "chips": [(1, 0, 0), (0, 1, 0), (1, 1, 0)],
    "pair": [(0, 0, 1)],
    "all": [(1, 0, 0), (0, 1, 0), (1, 1, 0), (0, 0, 1), (1, 0, 1), (0, 1, 1), (1, 1, 1)],
}


def _group_index(group, x, y, c):
    return {"chips": 2 * x + y, "pair": c, "all": 4 * x + 2 * y + c}[group]


def _chunk_indices(shape, pieces):
    if len(shape) < 3:
        return [()]
    lead = [()]
    for n in shape[:-2]:
        lead = [i + (k,) for i in lead for k in range(n)]
    rows = shape[-2]
    split = max(1, pieces // len(lead))
    while split > 1 and (rows % split or (rows // split) % 16):
        split -= 1
    step = rows // split
    return [i + (pl.ds(s * step, step),) for i in lead for s in range(split)]


def _exchange(arrays, out_shapes, group, src_view, dst_view, view_shape, name, own, pieces=16):
    masks = GROUPS[group]
    na, nm = len(arrays), len(masks)
    cuts = [_chunk_indices(view_shape(a), pieces) for a in range(na)]

    def body(*refs):
        ins, outs = refs[:na], refs[na:2 * na]
        send_sems, recv_sems = refs[2 * na:2 * na + 2]
        local_sems = refs[2 * na + 2] if own else None
        x, y, c = lax.axis_index("x"), lax.axis_index("y"), lax.axis_index("c")
        me = _group_index(group, x, y, c)
        peers = []
        for mx, my, mc in masks:
            px, py, pc = (1 - x if mx else x), (1 - y if my else y), (1 - c if mc else c)
            peers.append(((px, py, pc), _group_index(group, px, py, pc)))

        def part(ref, idx):
            return ref.at[idx] if idx else ref

        if own:
            for a in range(na):
                for idx in cuts[a]:
                    pltpu.make_async_copy(part(src_view(ins[a], a, me), idx), part(dst_view(outs[a], a, me), idx),
                                          local_sems.at[a]).start()
        for a in range(na):
            for j, (dev, pidx) in enumerate(peers):
                for idx in cuts[a]:
                    pltpu.make_async_remote_copy(
                        src_ref=part(src_view(ins[a], a, pidx), idx), dst_ref=part(dst_view(outs[a], a, me), idx),
                        send_sem=send_sems.at[a * nm + j], recv_sem=recv_sems.at[a * nm + j],
                        device_id=dev, device_id_type=MESH).start()
        whole = []
        for a in range(na):
            for j, (dev, pidx) in enumerate(peers):
                whole.append(pltpu.make_async_remote_copy(
                    src_ref=src_view(ins[a], a, pidx), dst_ref=dst_view(outs[a], a, pidx),
                    send_sem=send_sems.at[a * nm + j], recv_sem=recv_sems.at[a * nm + j],
                    device_id=dev, device_id_type=MESH))
        for cp in whole:
            cp.wait_recv()
        for cp in whole:
            cp.wait_send()
        if own:
            for a in range(na):
                pltpu.make_async_copy(src_view(ins[a], a, me), dst_view(outs[a], a, me), local_sems.at[a]).wait()

    hbm = pl.BlockSpec(memory_space=pltpu.HBM)
    sems = [pltpu.SemaphoreType.DMA((na * nm,)), pltpu.SemaphoreType.DMA((na * nm,))]
    return pl.pallas_call(
        body, name=name, in_specs=[hbm] * na, out_specs=[hbm] * na,
        out_shape=[_sds(s, a.dtype) for s, a in zip(out_shapes, arrays)],
        scratch_shapes=sems + ([pltpu.SemaphoreType.DMA((na,))] if own else []))(*arrays)


def all_gather(arrays, group, name, slot_axis=0, own=True):
    n = len(GROUPS[group]) + 1
    shapes = [a.shape[:slot_axis] + (n,) + a.shape[slot_axis:] for a in arrays]
    lead = (slice(None),) * slot_axis
    return _exchange(arrays, shapes, group, lambda r, a, i: r, lambda r, a, i: r.at[lead + (i,)],
                     lambda a: arrays[a].shape, name, own)


HBM_SPEC = pl.BlockSpec(memory_space=pltpu.HBM)
SEM_SPEC = pl.BlockSpec(memory_space=pltpu.SEMAPHORE)
DATAFLOW = pltpu.SideEffectType.DATAFLOW_SIDE_EFFECTING
N_CHIPS = 4


def _peers(group, x, y, c):
    out = []
    for mx, my, mc in GROUPS[group]:
        px, py, pc = (1 - x if mx else x), (1 - y if my else y), (1 - c if mc else c)
        out.append(((px, py, pc), _group_index(group, px, py, pc)))
    return out


def _whole_views(sources):
    return dict(src=lambda ref, a, c, to: ref, dst=lambda ref, a, c, sender: ref.at[sender],
                rows=lambda a: sources[a].shape[0])


def _weight_views(shards):
    half = [s.shape[0] // 2 for s in shards]
    return dict(src=lambda ref, a, c, to_chip: ref.at[pl.ds(c * half[a], half[a])],
                dst=lambda ref, a, c, from_chip: ref.at[from_chip, pl.ds(c * half[a], half[a])],
                rows=lambda a: half[a])


def _grad_views(sums):
    return dict(src=lambda ref, a, c, to_chip: ref.at[to_chip], dst=lambda ref, a, c, from_chip: ref.at[from_chip],
                rows=lambda a: sums[a].shape[1])


def chips_start(sources, zones, views, name, pieces=4, after=None, group="chips"):
    na, nm = len(sources), len(GROUPS[group])

    def body(*refs):
        ins, lands = refs[:na], refs[na:2 * na]
        n_in = 2 * na + len(_token_arg(after))
        send_sems, recv_sems, token = refs[n_in], refs[n_in + 1], refs[-1]
        x, y, c = lax.axis_index("x"), lax.axis_index("y"), lax.axis_index("c")
        me = _group_index(group, x, y, c)
        for a in range(na):
            step = views["rows"](a) // pieces
            for j, (dev, to) in enumerate(_peers(group, x, y, c)):
                for q in range(pieces):
                    rows = pl.ds(q * step, step)
                    pltpu.make_async_remote_copy(
                        src_ref=views["src"](ins[a], a, c, to).at[rows],
                        dst_ref=views["dst"](lands[a], a, c, me).at[rows],
                        send_sem=send_sems.at[a * nm + j], recv_sem=recv_sems.at[a * nm + j],
                        device_id=dev, device_id_type=MESH).start()
        token[...] = jnp.zeros_like(token)

    both = list(sources) + list(zones)
    outs = pl.pallas_call(
        body, name=name,
        out_shape=(pltpu.SemaphoreType.DMA((na * nm,)), pltpu.SemaphoreType.DMA((na * nm,)),
                   *[pltpu.HBM(b.shape, b.dtype) for b in both], _sds((8, CH), f32)),
        in_specs=[HBM_SPEC] * (2 * na) + _token_spec(after),
        out_specs=(SEM_SPEC, SEM_SPEC, *[HBM_SPEC] * (2 * na), pl.BlockSpec(memory_space=pltpu.VMEM)),
        input_output_aliases={i: 2 + i for i in range(2 * na)},
        compiler_params=pltpu.CompilerParams(has_side_effects=DATAFLOW))(
            *[pltpu.with_memory_space_constraint(b, pltpu.HBM) for b in both], *_token_arg(after))
    return dict(send=outs[0], recv=outs[1], sources=list(outs[2:2 + na]), zones=list(outs[2 + na:2 + 2 * na]),
                token=outs[-1], views=views, group=group)


def chips_wait(started, after, name):
    sources, zones, views, group = started["sources"], started["zones"], started["views"], started["group"]
    na, nm = len(sources), len(GROUPS[group])

    def body(*refs):
        ins, lands = refs[:na], refs[na:2 * na]
        send_sems, recv_sems = refs[2 * na], refs[2 * na + 1]
        x, y, c = lax.axis_index("x"), lax.axis_index("y"), lax.axis_index("c")
        for a in range(na):
            for j, (dev, peer) in enumerate(_peers(group, x, y, c)):
                cp = pltpu.make_async_remote_copy(
                    src_ref=views["src"](ins[a], a, c, peer), dst_ref=views["dst"](lands[a], a, c, peer),
                    send_sem=send_sems.at[a * nm + j], recv_sem=recv_sems.at[a * nm + j],
                    device_id=dev, device_id_type=MESH)
                cp.wait_send()
                cp.wait_recv()

    both = list(sources) + list(zones)
    outs = pl.pallas_call(
        body, name=name, out_shape=tuple(pltpu.HBM(b.shape, b.dtype) for b in both),
        in_specs=[HBM_SPEC] * (2 * na) + [SEM_SPEC, SEM_SPEC, pl.BlockSpec(memory_space=pl.ANY)],
        out_specs=tuple([HBM_SPEC] * (2 * na)), input_output_aliases={i: i for i in range(2 * na)},
        compiler_params=pltpu.CompilerParams(has_side_effects=DATAFLOW))(*both, started["send"], started["recv"], after)
    return list(outs[:na]), list(outs[na:])


def weights_share(zones, name):
    na, nm = len(zones), N_CHIPS - 1

    def body(*refs):
        lands = refs[na:2 * na]
        send_sems, recv_sems = refs[2 * na:]
        x, y, c = lax.axis_index("x"), lax.axis_index("y"), lax.axis_index("c")
        chip = 2 * x + y
        sibling = (x, y, 1 - c)
        sends = []
        for a in range(na):
            half = zones[a].shape[1] // 2
            for m in range(1, N_CHIPS):
                mine = lands[a].at[chip ^ m, pl.ds(c * half, half)]
                sends.append(pltpu.make_async_remote_copy(
                    src_ref=mine, dst_ref=mine, send_sem=send_sems.at[a * nm + m - 1],
                    recv_sem=recv_sems.at[a * nm + m - 1], device_id=sibling, device_id_type=MESH))
        for cp in sends:
            cp.start()
        for a in range(na):
            half = zones[a].shape[1] // 2
            for m in range(1, N_CHIPS):
                theirs = lands[a].at[chip ^ m, pl.ds((1 - c) * half, half)]
                pltpu.make_async_remote_copy(
                    src_ref=theirs, dst_ref=theirs, send_sem=send_sems.at[a * nm + m - 1],
                    recv_sem=recv_sems.at[a * nm + m - 1], device_id=sibling, device_id_type=MESH).wait_recv()
        for cp in sends:
            cp.wait_send()

    return pl.pallas_call(
        body, name=name, in_specs=[HBM_SPEC] * na, out_specs=[HBM_SPEC] * na,
        out_shape=[_sds(z.shape, z.dtype) for z in zones], input_output_aliases={i: i for i in range(na)},
        scratch_shapes=[pltpu.SemaphoreType.DMA((na * nm,)), pltpu.SemaphoreType.DMA((na * nm,))])(*zones)


def pair_send_halves(grads, name):
    half = [g.shape[1] // 2 for g in grads]
    shapes = [(g.shape[0], h, g.shape[2]) for g, h in zip(grads, half)]
    return _exchange(grads, shapes, "pair", lambda r, a, i: r.at[:, pl.ds(i * half[a], half[a])],
                     lambda r, a, i: r, lambda a: shapes[a], name, False)


def sum_pair_half(g, recv, core, name, tb=256, by_chip=None):
    nk, r, c = g.shape
    tb = min(tb, r // 2)
    nb = r // 2 // tb

    def body(core_ref, g_ref, r_ref, o_ref):
        s = g_ref[...].astype(f32) + r_ref[...].astype(f32)
        if by_chip is None:
            o_ref[...] = s.astype(bf16)
        else:
            for k in range(by_chip[0]):
                o_ref[k] = s[:, k * by_chip[1]:(k + 1) * by_chip[1]].astype(bf16)

    if by_chip is None:
        out_spec = pl.BlockSpec((None, tb, c), lambda k, i, core_ref: (k, i, 0))
        out_shape = _sds((nk, r // 2, c), bf16)
    else:
        assert nk == 1
        out_spec = pl.BlockSpec((by_chip[0], tb, by_chip[1]), lambda k, i, core_ref: (0, i, 0))
        out_shape = _sds((by_chip[0], r // 2, by_chip[1]), bf16)
    return pl.pallas_call(
        body, name=name,
        grid_spec=pltpu.PrefetchScalarGridSpec(
            num_scalar_prefetch=1, grid=(nk, nb),
            in_specs=[pl.BlockSpec((None, tb, c), lambda k, i, core_ref: (k, core_ref[0] * nb + i, 0)),
                      pl.BlockSpec((None, tb, c), lambda k, i, core_ref: (k, i, 0))],
            out_specs=out_spec),
        out_shape=out_shape, compiler_params=_cparams(2))(jnp.reshape(core, (1,)).astype(jnp.int32), g, recv)


def assemble_columns(blocks, width, name, tb=256):
    n, r, c = blocks.shape

    def body(b_ref, o_ref):
        for k in range(n):
            o_ref[:, k * c:(k + 1) * c] = b_ref[k]
        o_ref[:, n * c:] = jnp.zeros((tb, width - n * c), blocks.dtype)

    return pl.pallas_call(
        body, name=name, grid=(r // tb,), in_specs=[pl.BlockSpec((n, tb, c), lambda i: (0, i, 0))],
        out_specs=pl.BlockSpec((tb, width), lambda i: (i, 0)), out_shape=_sds((r, width), blocks.dtype),
        compiler_params=_cparams(1))(blocks)


def chip_sum_into(acc, layer, own, others, chip, name, tb=256):
    n, r, c = own.shape
    tb = min(tb, r)

    def body(chip_ref, x_ref, y1_ref, y2_ref, y3_ref, acc_ref, o_ref):
        o_ref[...] = ((x_ref[...].astype(f32) + y1_ref[...].astype(f32)) + y2_ref[...].astype(f32)) + y3_ref[...].astype(f32)

    def slot(k):
        return pl.BlockSpec((None, tb, c), lambda i, chip_ref: (chip_ref[0] ^ k, i, 0))

    return pl.pallas_call(
        body, name=name,
        grid_spec=pltpu.PrefetchScalarGridSpec(
            num_scalar_prefetch=1, grid=(r // tb,),
            in_specs=[slot(k) for k in range(n)] + [pl.BlockSpec(memory_space=pl.ANY)],
            out_specs=pl.BlockSpec((None, tb, c), lambda i, chip_ref: (layer, i, 0))),
        out_shape=_sds(acc.shape, f32), input_output_aliases={n + 1: 0}, compiler_params=_cparams(1))(
            jnp.reshape(chip, (1,)).astype(jnp.int32), own, *([others] * (n - 1)), acc)


def adamw_half(w, g_half, m, v, half, name, before=None, token=None, tb=256):
    depth, r, c = w.shape
    tb = min(tb, r // 2)
    nb = r // 2 // tb
    n_extra = (0 if before is None else 4) + len(_token_arg(token))

    def body(half_ref, w_ref, gh_ref, m_ref, v_ref, *rest):
        g_ref, d_ref, mo_ref, vo_ref = rest[n_extra:]
        gv = gh_ref[...]
        m2 = B1 * m_ref[...] + (1.0 - B1) * gv
        v2 = B2 * v_ref[...] + (1.0 - B2) * (gv * gv)
        m_hat = m2 / (1.0 - B1 ** STEP)
        v_hat = v2 / (1.0 - B2 ** STEP)
        g_ref[...] = gv
        d_ref[...] = -LR * (m_hat / (jnp.sqrt(v_hat) + AEPS) + WD * w_ref[...])
        mo_ref[...] = m2
        vo_ref[...] = v2

    whole = pl.BlockSpec((None, tb, c), lambda l, i, half_ref: (l, half_ref[0] * nb + i, 0))
    part = pl.BlockSpec((None, tb, c), lambda l, i, half_ref: (l, i, 0))
    extra = ([] if before is None else list(before)) + _token_arg(token)
    return pl.pallas_call(
        body, name=name,
        grid_spec=pltpu.PrefetchScalarGridSpec(
            num_scalar_prefetch=1, grid=(depth, nb),
            in_specs=[whole, part, whole, whole] + [pl.BlockSpec(memory_space=pl.ANY)] * n_extra, out_specs=[whole] * 4),
        out_shape=[_sds(w.shape, f32)] * 4,
        input_output_aliases={} if before is None else {5 + k: k for k in range(4)},
        compiler_params=_cparams(2))(jnp.reshape(half, (1,)).astype(jnp.int32), w, g_half, m, v, *extra)


def sum_slots(y, out_dtype, name, tb=256):
    n, r, c = y.shape
    tb = min(tb, r)

    def body(y_ref, o_ref):
        acc = y_ref[0].astype(f32)
        for i in range(1, n):
            acc = acc + y_ref[i].astype(f32)
        o_ref[...] = acc.astype(out_dtype)

    return pl.pallas_call(
        body, name=name, grid=(r // tb,),
        in_specs=[pl.BlockSpec((n, tb, c), lambda i: (0, i, 0))], out_specs=pl.BlockSpec((tb, c), lambda i: (i, 0)),
        out_shape=_sds((r, c), out_dtype), compiler_params=_cparams(1))(y)


def adamw(w, g, m, v, name, tb=256):
    r, c = w.shape
    tb = min(tb, r)

    def body(w_ref, g_ref, m_ref, v_ref, d_ref, mo_ref, vo_ref):
        gv = g_ref[...]
        m2 = B1 * m_ref[...] + (1.0 - B1) * gv
        v2 = B2 * v_ref[...] + (1.0 - B2) * (gv * gv)
        m_hat = m2 / (1.0 - B1 ** STEP)
        v_hat = v2 / (1.0 - B2 ** STEP)
        d_ref[...] = -LR * (m_hat / (jnp.sqrt(v_hat) + AEPS) + WD * w_ref[...])
        mo_ref[...] = m2
        vo_ref[...] = v2

    spec = pl.BlockSpec((tb, c), lambda i: (i, 0))
    return pl.pallas_call(
        body, name=name, grid=(r // tb,), in_specs=[spec] * 4, out_specs=[spec] * 3,
        out_shape=[_sds((r, c), f32)] * 3, compiler_params=_cparams(1))(w, g, m, v)


def adamw_leading(w, g, m, v, name, tc=64):
    c, l, r = w.shape
    main = c // tc
    tail = c - main * tc

    def body(w_ref, g_ref, m_ref, v_ref, *rest):
        d_ref, mo_ref, vo_ref = rest[-3:]
        gv = g_ref[...]
        m2 = B1 * m_ref[...] + (1.0 - B1) * gv
        v2 = B2 * v_ref[...] + (1.0 - B2) * (gv * gv)
        m_hat = m2 / (1.0 - B1 ** STEP)
        v_hat = v2 / (1.0 - B2 ** STEP)
        d_ref[...] = -LR * (m_hat / (jnp.sqrt(v_hat) + AEPS) + WD * w_ref[...])
        mo_ref[...] = m2
        vo_ref[...] = v2

    spec = pl.BlockSpec((tc, l, r), lambda i: (i, 0, 0))
    outs = pl.pallas_call(
        functools.partial(body), name=name, grid=(main,), in_specs=[spec] * 4, out_specs=[spec] * 3,
        out_shape=[_sds(w.shape, f32)] * 3, compiler_params=_cparams(1))(w, g, m, v)
    if tail:
        assert (main * tc) % tail == 0
        last = pl.BlockSpec((tail, l, r), lambda i: (main * tc // tail, 0, 0))
        outs = pl.pallas_call(
            functools.partial(body), name=name + "_tail", grid=(1,),
            in_specs=[last] * 4 + [pl.BlockSpec(memory_space=pl.ANY)] * 3, out_specs=[last] * 3,
            out_shape=[_sds(w.shape, f32)] * 3, input_output_aliases={4: 0, 5: 1, 6: 2},
            compiler_params=_cparams(1))(w, g, m, v, *outs)
    return outs


SMALL_ROW = 1024
SMALL_GAINS = ("g1", "ga", "gs", "g2", "g3", "g4")
SMALL_LAYER_ROWS = 8 + 8 + 16 + 8


def _pack_small(grads):
    wide = lambda a: jnp.pad(a, ((0, 0), (0, 2 * SMALL_ROW - a.shape[1]))).reshape(-1, SMALL_ROW)
    row = lax.broadcasted_iota(jnp.int32, (8, SMALL_ROW), 0)
    parts = []
    for g in grads:
        singles = [g[k] for k in SMALL_GAINS] + [g["bs"][:, :SMALL_ROW],
                                                 jnp.pad(g["bs"][:, SMALL_ROW:], ((0, 0), (0, 2 * SMALL_ROW - XBC)))]
        first = sum(jnp.where(row == k, s, 0.0) for k, s in enumerate(singles))
        parts += [first, g["wa"], wide(g["ws"]), jnp.pad(g["par"], ((0, 0), (0, SMALL_ROW - CH)))]
    return jnp.concatenate(parts, axis=0)


def _unpack_small(packed, depth):
    rows = packed.reshape(depth, SMALL_LAYER_ROWS, SMALL_ROW)
    out = {k: rows[:, i] for i, k in enumerate(SMALL_GAINS)}
    out["bs"] = rows[:, 6:8].reshape(depth, 2 * SMALL_ROW)[:, :XBC]
    out["wa"] = rows[:, 8:11]
    out["ws"] = rows[:, 16:32].reshape(depth, 8, 2 * SMALL_ROW)[:, :4, :XBC]
    out["par"] = rows[:, 32:35, :CH]
    return out


def kernel(x, norm_mix_pre, w_in, conv_a_w, ssm_conv_w, ssm_conv_b, dt_bias, a_log, d_skip, conv_out_norm, ssm_out_norm, w_out, norm_mix_post, norm_mlp_pre, w_up, w_down, norm_mlp_post, loss_target, m_norm_mix_pre, m_w_in, m_conv_a_w, m_ssm_conv_w, m_ssm_conv_b, m_dt_bias, m_a_log, m_d_skip, m_conv_out_norm, m_ssm_out_norm, m_w_out, m_norm_mix_post, m_norm_mlp_pre, m_w_up, m_w_down, m_norm_mlp_post, v_norm_mix_pre, v_w_in, v_conv_a_w, v_ssm_conv_w, v_ssm_conv_b, v_dt_bias, v_a_log, v_d_skip, v_conv_out_norm, v_ssm_out_norm, v_w_out, v_norm_mix_post, v_norm_mlp_pre, v_w_up, v_w_down, v_norm_mlp_post):
    nb, seq, _ = x.shape
    t = nb * seq
    depth = w_in.shape[0]
    ncol = w_in.shape[2]
    chip = 2 * lax.axis_index("x") + lax.axis_index("y")

    taps = [conv_a_w, ssm_conv_w]
    taps_g = all_gather(taps, "chips", "gather_taps", slot_axis=1, own=False)
    wa_g, ws_g = [lax.dynamic_update_index_in_dim(g, s, chip, 1) for g, s in zip(taps_g, taps)]
    wa_full = jnp.transpose(wa_g, (0, 2, 1, 3)).reshape(depth, 3, D)
    ws_full = jnp.transpose(ws_g, (0, 2, 1, 3)).reshape(depth, 4, XBC)
    lane_pad = lambda a: jnp.pad(a, ((0, 0), (0, CH - a.shape[1])))
    par = jnp.stack([lane_pad(dt_bias), lane_pad(a_log), lane_pad(d_skip)], axis=1)
    par = jnp.pad(par, ((0, 0), (0, 5), (0, 0)))

    layer_shards = lambda l: [w_in[l].astype(bf16), w_out[l].astype(bf16), w_up[l].astype(bf16), w_down[l].astype(bf16)]
    issued = []

    def start(shards, name):
        zones = [lax.empty((N_CHIPS,) + s.shape, s.dtype) for s in shards]
        issued.append(chips_start(shards, zones, _weight_views(shards), name,
                                  after=issued[-1]["token"] if issued else taps_g[0]))
        return issued[-1]

    def finish(started, after, name):
        shards, zones = chips_wait(started, after, name)
        zones = weights_share(zones, "weights_share")
        return [lax.dynamic_update_index_in_dim(z, s, chip, 0) for z, s in zip(zones, shards)]

    def shaped(mats):
        wo_z, wu_z, wd_z = mats
        return wo_z.reshape(2 * D, D), wu_z, wd_z.reshape(DFF, D)

    first = layer_shards(0)
    travelling = {0: start(first[:1], "weights_start_0")}
    rest = start(first[1:], "weights_start_0_rest")
    for l in range(1, depth):
        travelling[l] = start(layer_shards(l), f"weights_start_{l}")

    def weights_of(l, x_in):
        mats = finish(travelling.pop(l), x_in, f"weights_wait_{l}")
        w = dict(win=assemble_columns(mats[0], PROJ, "assemble_w_in"), wa=jnp.pad(wa_full[l], ((0, 5), (0, 0))),
                 ws=jnp.pad(ws_full[l], ((0, 4), (0, 0))), bs=ssm_conv_b[l][None], par=par[l],
                 g1=norm_mix_pre[l][None], ga=conv_out_norm[l][None], gs=ssm_out_norm[l][None],
                 g2=norm_mix_post[l][None], g3=norm_mlp_pre[l][None], g4=norm_mlp_post[l][None])
        if l == 0:
            w["token"] = issued[-1]["token"]
            w["late"] = lambda after: dict(zip(("wo", "wu", "wd"), shaped(finish(rest, after, "weights_wait_0_rest"))))
        else:
            w.update(zip(("wo", "wu", "wd"), shaped(mats[1:])))
        return w

    core = lax.axis_index("c")
    grads_travelling = {}
    given_m = dict(win=m_w_in, wo=m_w_out, wu=m_w_up, wd=m_w_down)
    given_v = dict(win=v_w_in, wo=v_w_out, wu=v_w_up, wd=v_w_down)

    chip_major = dict(win=lambda a: a[None], wo=lambda a: a.reshape(N_CHIPS, 2 * D // N_CHIPS, D), wu=lambda a: a,
                      wd=lambda a: a.reshape(N_CHIPS, DFF // N_CHIPS, D))
    held = {}

    def grads_done(l, g, last):
        if l > 0 and not last:
            held[l] = g
            return None
        g = {**held.pop(l, {}), **g}
        keys = [k for k in ("win", "wo", "wu", "wd") if k in g]
        mats = [chip_major[k](g[k]) for k in keys]
        received = pair_send_halves(mats, "grads_to_pair")
        sums = [sum_pair_half(m_, r_, core, "pair_sum", by_chip=(N_CHIPS, ncol) if k == "win" else None)
                for k, m_, r_ in zip(keys, mats, received)]
        zones = [lax.empty(s.shape, s.dtype) for s in sums]
        started = chips_start(sums, zones, _grad_views(sums), f"grads_start_{l}_{len(grads_travelling)}")
        grads_travelling[(l, keys[0])] = (keys, started)
        return started["token"]

    sse, dx, grads = local_step(x.reshape(t, D), loss_target.reshape(t, D), depth, weights_of, seq, grads_done)
    loss = lax.psum(0.5 / D * sse[0, 0], ("x", "y", "c"))

    packed = _pack_small(grads)
    small_travelling = chips_start([packed], [lax.empty((8,) + packed.shape, f32)], _whole_views([packed]),
                                   "small_start", group="all")

    big_w = dict(win=w_in, wo=w_out, wu=w_up, wd=w_down)
    acc = {k: lax.empty((depth, bw.shape[1] // 2, bw.shape[2]), f32) for k, bw in big_w.items()}
    for n, ((l, _), (keys, started)) in enumerate(grads_travelling.items()):
        sums, zones = chips_wait(started, small_travelling["token"], f"grads_wait_{l}_{n}")
        for k, s, z in zip(keys, sums, zones):
            acc[k] = chip_sum_into(acc[k], l, s, z, chip, "chip_sum")
    names = ("win", "wo", "wu", "wd")
    acc = [acc[k] for k in names]
    pair_views = dict(src=lambda ref, a, c, to: ref, dst=lambda ref, a, c, sender: ref, rows=lambda a: depth)
    to_sibling = chips_start(acc, [lax.empty(a.shape, f32) for a in acc], pair_views, "grads_from_pair_start",
                             pieces=depth, group="pair")
    own_done = {}
    for k, a in zip(names, to_sibling["sources"]):
        if big_w[k].shape[-1] % CH == 0:
            own_done[k] = adamw_half(big_w[k], a, given_m[k], given_v[k], core, "adamw_matrix",
                                     token=to_sibling["token"])

    (packed,), (small_all,) = chips_wait(small_travelling, own_done["wd"][1], "small_wait")
    small_all = lax.dynamic_update_index_in_dim(small_all, packed, 4 * lax.axis_index("x") + 2 * lax.axis_index("y") + core, 0)
    small = _unpack_small(sum_slots(small_all, f32, "small_sum", tb=8), depth)
    wa_cols, ws_cols = conv_a_w.shape[2], ssm_conv_w.shape[2]
    par_g = small["par"].reshape(depth, 3, CH)
    g_small = dict(
        norm_mix_pre=small["g1"], conv_out_norm=small["ga"], ssm_out_norm=small["gs"], norm_mix_post=small["g2"],
        norm_mlp_pre=small["g3"], norm_mlp_post=small["g4"], ssm_conv_b=small["bs"],
        conv_a_w=lax.dynamic_slice_in_dim(small["wa"].reshape(depth, 3, D), chip * wa_cols, wa_cols, axis=2),
        ssm_conv_w=lax.dynamic_slice_in_dim(small["ws"].reshape(depth, 4, XBC), chip * ws_cols, ws_cols, axis=2),
        dt_bias=par_g[:, 0, :NH], a_log=par_g[:, 1, :NH], d_skip=par_g[:, 2, :NH])

    given = dict(norm_mix_pre=(norm_mix_pre, m_norm_mix_pre, v_norm_mix_pre), w_in=(w_in, m_w_in, v_w_in),
                 conv_a_w=(conv_a_w, m_conv_a_w, v_conv_a_w), ssm_conv_w=(ssm_conv_w, m_ssm_conv_w, v_ssm_conv_w),
                 ssm_conv_b=(ssm_conv_b, m_ssm_conv_b, v_ssm_conv_b), dt_bias=(dt_bias, m_dt_bias, v_dt_bias),
                 a_log=(a_log, m_a_log, v_a_log), d_skip=(d_skip, m_d_skip, v_d_skip),
                 conv_out_norm=(conv_out_norm, m_conv_out_norm, v_conv_out_norm),
                 ssm_out_norm=(ssm_out_norm, m_ssm_out_norm, v_ssm_out_norm), w_out=(w_out, m_w_out, v_w_out),
                 norm_mix_post=(norm_mix_post, m_norm_mix_post, v_norm_mix_post),
                 norm_mlp_pre=(norm_mlp_pre, m_norm_mlp_pre, v_norm_mlp_pre), w_up=(w_up, m_w_up, v_w_up),
                 w_down=(w_down, m_w_down, v_w_down), norm_mlp_post=(norm_mlp_post, m_norm_mlp_post, v_norm_mlp_post))
    order = ["norm_mix_pre", "w_in", "conv_a_w", "ssm_conv_w", "ssm_conv_b", "dt_bias", "a_log", "d_skip",
             "conv_out_norm", "ssm_out_norm", "w_out", "norm_mix_post", "norm_mlp_pre", "w_up", "w_down",
             "norm_mlp_post"]
    short = dict(w_in="win", w_out="wo", w_up="wu", w_down="wd")
    results = {}
    for n in order:
        if n in short:
            continue
        wv, mv, vv = given[n]
        gv = g_small[n].reshape(wv.shape)
        two_d = lambda a: a.reshape(-1, a.shape[-1])
        results[n] = (gv,) + tuple(adamw(two_d(wv), two_d(gv), two_d(mv), two_d(vv), "adamw"))

    acc, from_sibling = chips_wait(to_sibling, results["norm_mlp_post"][1], "grads_from_pair_wait")
    for n, k in short.items():
        wv, mv, vv = given[n]
        own, recv = acc[names.index(k)], from_sibling[names.index(k)]
        if k in own_done:
            results[n] = adamw_half(wv, recv, mv, vv, 1 - core, "adamw_matrix", before=own_done[k])
        else:
            to_cols, to_rows = (lambda a: jnp.transpose(a, (2, 0, 1))), (lambda a: jnp.transpose(a, (1, 2, 0)))
            own_c, recv_c = to_cols(own), to_cols(recv)
            g_cols = jnp.where(core == 0, jnp.concatenate([own_c, recv_c], axis=2),
                               jnp.concatenate([recv_c, own_c], axis=2))
            results[n] = tuple(to_rows(o) for o in (g_cols,) + tuple(adamw_leading(to_cols(wv), g_cols, to_cols(mv),
                                                                                   to_cols(vv), "adamw_cols")))
    g_out, d_out, m_out, v_out = [], [], [], []
    for n in order:
        wv = given[n][0]
        gv, dlt, m2, v2 = results[n]
        g_out.append(gv.reshape(wv.shape))
        d_out.append(dlt.reshape(wv.shape))
        m_out.append(m2.reshape(wv.shape))
        v_out.append(v2.reshape(wv.shape))
    return (loss, dx.reshape(nb, seq, D), *g_out, *d_out, *m_out, *v_out)
```

```python
import functools

import jax
import jax.numpy as jnp
from jax import lax
from jax.experimental import pallas as pl
from jax.experimental.pallas import tpu as pltpu

f32, bf16 = jnp.float32, jnp.bfloat16

D = 1024
NH, HP = 16, 64
NG, NS = 2, 128
CH = 128
XBC = D + 2 * NG * NS
DFF = 4 * D
IN_COLS = 3 * D + D + XBC + NH
PROJ = 5760
COL_Z, COL_XBC, COL_DT = 3 * D, 4 * D, 4 * D + XBC
EPS = 1e-6
HALO = 8
HBLK = 16
VMEM_LIMIT = 56 * 2**20
MESH = pl.DeviceIdType.MESH

LR, B1, B2, AEPS, WD, STEP = 0.001, 0.9, 0.999, 1e-08, 0.01, 10


def _cparams(n_axes):
    return pltpu.CompilerParams(dimension_semantics=("arbitrary",) * n_axes, vmem_limit_bytes=VMEM_LIMIT)


def _sds(shape, dtype):
    return jax.ShapeDtypeStruct(tuple(shape), dtype)


def _token_spec(token):
    return [] if token is None else [pl.BlockSpec(memory_space=pl.ANY)]


def _token_arg(token):
    return [] if token is None else [token]


def _resident(shape):
    return pl.BlockSpec(shape, lambda i: (0,) * len(shape), pipeline_mode=pl.Buffered(1))


def _rms_fwd(x, g):
    r = lax.rsqrt(jnp.mean(x * x, axis=-1, keepdims=True) + EPS)
    return x * r * g


def _rms_bwd(x, g, dy):
    r = lax.rsqrt(jnp.mean(x * x, axis=-1, keepdims=True) + EPS)
    xh = x * r
    gdy = dy * g
    dx = r * (gdy - xh * jnp.mean(xh * gdy, axis=-1, keepdims=True))
    return dx, dy * xh


def _accum(ref, part, first):
    @pl.when(first)
    def _():
        ref[...] = part

    @pl.when(jnp.logical_not(first))
    def _():
        ref[...] += part


def _dot_nt(a, b):
    return lax.dot_general(a, b, (((1,), (1,)), ((), ())), preferred_element_type=f32)


def _dot_tn(a, b):
    return lax.dot_general(a, b, (((0,), (0,)), ((), ())), preferred_element_type=f32)


def _dot(a, b):
    return jnp.dot(a, b, preferred_element_type=f32)


def _split_dot(x, e_bf, n_split, nt=False):
    acc = None
    rem = x
    for s in range(n_split):
        hi = rem.astype(bf16)
        term = _dot_nt(hi, e_bf) if nt else _dot(hi, e_bf)
        acc = term if acc is None else acc + term
        if s + 1 < n_split:
            rem = rem - hi.astype(f32)
    return acc


def _sigmoid(x):
    return 0.5 * jnp.tanh(0.5 * x) + 0.5


def norm_matmul(x, g, w, tm, tn, out_dtype, name, token=None):
    t, n = x.shape[0], w.shape[1]
    w_spec = pl.BlockSpec((D, tn), lambda i, j: (0, j))

    def body(x_ref, g_ref, w_ref, *rest):
        o_ref, h_ref = rest[-2:]

        @pl.when(pl.program_id(1) == 0)
        def _():
            h_ref[...] = _rms_fwd(x_ref[...], g_ref[...]).astype(bf16)

        o_ref[...] = _dot(h_ref[...], w_ref[...]).astype(out_dtype)

    return pl.pallas_call(
        body, name=name, grid=(t // tm, n // tn),
        in_specs=[pl.BlockSpec((tm, D), lambda i, j: (i, 0)), pl.BlockSpec((1, D), lambda i, j: (0, 0)), w_spec]
        + _token_spec(token),
        out_specs=[pl.BlockSpec((tm, tn), lambda i, j: (i, j)), pl.BlockSpec((tm, D), lambda i, j: (i, 0))],
        out_shape=[_sds((t, n), out_dtype), _sds((t, D), bf16)],
        compiler_params=_cparams(2))(x, g, w, *_token_arg(token))


def matmul_postnorm(a, w, xres, g, tm, name):
    t, k = a.shape

    def body(a_ref, w_ref, xr_ref, g_ref, y_ref, xo_ref):
        y = _dot(a_ref[...], w_ref[...])
        y_ref[...] = y.astype(bf16)
        xo_ref[...] = xr_ref[...] + _rms_fwd(y, g_ref[...])

    return pl.pallas_call(
        body, name=name, grid=(t // tm,),
        in_specs=[pl.BlockSpec((tm, k), lambda i: (i, 0)), _resident(w.shape),
                  pl.BlockSpec((tm, D), lambda i: (i, 0)), pl.BlockSpec((1, D), lambda i: (0, 0))],
        out_specs=[pl.BlockSpec((tm, D), lambda i: (i, 0)), pl.BlockSpec((tm, D), lambda i: (i, 0))],
        out_shape=[_sds((t, D), bf16), _sds((t, D), f32)],
        compiler_params=_cparams(1))(a, w, xres, g)


def postnorm_bwd_matmul(y, g, dxo, w, tm, tn, name):
    t, n = y.shape[0], w.shape[0]

    def body(y_ref, g_ref, dxo_ref, w_ref, dy_ref, dg_ref, da_ref):
        i, j = pl.program_id(0), pl.program_id(1)

        @pl.when(j == 0)
        def _():
            dx, dgc = _rms_bwd(y_ref[...].astype(f32), g_ref[...], dxo_ref[...])
            dy_ref[...] = dx.astype(bf16)
            _accum(dg_ref, jnp.sum(dgc, axis=0, keepdims=True), i == 0)

        da_ref[...] = _dot_nt(dy_ref[...], w_ref[...]).astype(bf16)

    return pl.pallas_call(
        body, name=name, grid=(t // tm, n // tn),
        in_specs=[pl.BlockSpec((tm, D), lambda i, j: (i, 0)), pl.BlockSpec((1, D), lambda i, j: (0, 0)),
                  pl.BlockSpec((tm, D), lambda i, j: (i, 0)), pl.BlockSpec((tn, D), lambda i, j: (j, 0))],
        out_specs=[pl.BlockSpec((tm, D), lambda i, j: (i, 0)), pl.BlockSpec((1, D), lambda i, j: (0, 0)),
                   pl.BlockSpec((tm, tn), lambda i, j: (i, j))],
        out_shape=[_sds((t, D), bf16), _sds((1, D), f32), _sds((t, n), bf16)],
        compiler_params=_cparams(2))(y, g, dxo, w)


def matmul_prenorm_bwd(da, w, x, g, dxo, tm, name, token=None):
    t, k = da.shape

    def body(da_ref, w_ref, x_ref, g_ref, dxo_ref, *rest):
        dx_ref, dg_ref = rest[-2:]
        dh = _dot_nt(da_ref[...], w_ref[...])
        dxn, dgc = _rms_bwd(x_ref[...], g_ref[...], dh)
        dx_ref[...] = dxo_ref[...] + dxn
        _accum(dg_ref, jnp.sum(dgc, axis=0, keepdims=True), pl.program_id(0) == 0)

    return pl.pallas_call(
        body, name=name, grid=(t // tm,),
        in_specs=[pl.BlockSpec((tm, k), lambda i: (i, 0)), _resident(w.shape),
                  pl.BlockSpec((tm, D), lambda i: (i, 0)), pl.BlockSpec((1, D), lambda i: (0, 0)),
                  pl.BlockSpec((tm, D), lambda i: (i, 0))] + _token_spec(token),
        out_specs=[pl.BlockSpec((tm, D), lambda i: (i, 0)), pl.BlockSpec((1, D), lambda i: (0, 0))],
        out_shape=[_sds((t, D), f32), _sds((1, D), f32)],
        compiler_params=_cparams(1))(da, w, x, g, dxo, *_token_arg(token))


def mlp_fwd(x, g_pre, wu, wd, g_post, tm, name):
    t = x.shape[0]
    nq, _, fc = wu.shape

    def body(x_ref, gp_ref, wu_ref, wd_ref, gq_ref, fp_ref, h_ref, o_ref, xo_ref):
        xv = x_ref[...]
        h = _rms_fwd(xv, gp_ref[...]).astype(bf16)
        h_ref[...] = h
        o = None
        for q in range(nq):
            fq = _dot(h, wu_ref[q])
            fp_ref[:, q * fc:(q + 1) * fc] = fq.astype(bf16)
            r = jnp.maximum(fq, 0.0)
            part = _dot((r * r).astype(bf16), wd_ref[q * fc:(q + 1) * fc, :])
            o = part if o is None else o + part
        o_ref[...] = o.astype(bf16)
        xo_ref[...] = xv + _rms_fwd(o, gq_ref[...])

    row = lambda c: pl.BlockSpec((tm, c), lambda i: (i, 0))
    vec = pl.BlockSpec((1, D), lambda i: (0, 0))
    return pl.pallas_call(
        body, name=name, grid=(t // tm,),
        in_specs=[row(D), vec, _resident(wu.shape), _resident(wd.shape), vec],
        out_specs=[row(nq * fc), row(D), row(D), row(D)],
        out_shape=[_sds((t, nq * fc), bf16), _sds((t, D), bf16), _sds((t, D), bf16), _sds((t, D), f32)],
        compiler_params=_cparams(1))(x, g_pre, wu, wd, g_post)


def mlp_bwd(o, g_post, dxo, wd, fp, wu, x, g_pre, tm, name):
    t = x.shape[0]
    nq, _, fc = wu.shape

    def body(o_ref, gq_ref, dxo_ref, wd_ref, fp_ref, wu_ref, x_ref, gp_ref, do_ref, dfp_ref, dx_ref, dgq_ref, dgp_ref):
        i = pl.program_id(0)
        dxo_v = dxo_ref[...]
        do, dgq = _rms_bwd(o_ref[...].astype(f32), gq_ref[...], dxo_v)
        do_b = do.astype(bf16)
        do_ref[...] = do_b
        dh = None
        for q in range(nq):
            cols = slice(q * fc, (q + 1) * fc)
            dq = _dot_nt(do_b, wd_ref[cols, :]) * (2.0 * jnp.maximum(fp_ref[:, cols].astype(f32), 0.0))
            dq_b = dq.astype(bf16)
            dfp_ref[:, cols] = dq_b
            part = _dot_nt(dq_b, wu_ref[q])
            dh = part if dh is None else dh + part
        dxn, dgp = _rms_bwd(x_ref[...], gp_ref[...], dh)
        dx_ref[...] = dxo_v + dxn
        _accum(dgq_ref, jnp.sum(dgq, axis=0, keepdims=True), i == 0)
        _accum(dgp_ref, jnp.sum(dgp, axis=0, keepdims=True), i == 0)

    row = lambda c: pl.BlockSpec((tm, c), lambda i: (i, 0))
    vec = pl.BlockSpec((1, D), lambda i: (0, 0))
    return pl.pallas_call(
        body, name=name, grid=(t // tm,),
        in_specs=[row(D), vec, row(D), _resident(wd.shape), row(nq * fc), _resident(wu.shape), row(D), vec],
        out_specs=[row(D), row(nq * fc), row(D), vec, vec],
        out_shape=[_sds((t, D), bf16), _sds((t, nq * fc), bf16), _sds((t, D), f32), _sds((1, D), f32), _sds((1, D), f32)],
        compiler_params=_cparams(1))(o, g_post, dxo, wd, fp, wu, x, g_pre)


def matmul_tn(a, b, tm, tn, relu2, name, col_blocks=False):
    t, m = a.shape
    n = b.shape[1]
    if col_blocks:
        out_spec, out_shape = pl.BlockSpec((None, tm, tn), lambda i, j: (j, i, 0)), _sds((n // tn, m, tn), bf16)
    else:
        out_spec, out_shape = pl.BlockSpec((tm, tn), lambda i, j: (i, j)), _sds((m, n), bf16)

    def body(a_ref, b_ref, o_ref, at_ref):
        @pl.when(pl.program_id(1) == 0)
        def _():
            av = a_ref[...]
            if relu2:
                af = jnp.maximum(av.astype(f32), 0.0)
                av = (af * af).astype(bf16)
            at_ref[...] = av.T

        o_ref[...] = _dot(at_ref[...], b_ref[...]).astype(bf16)

    return pl.pallas_call(
        body, name=name, grid=(m // tm, n // tn),
        in_specs=[pl.BlockSpec((t, tm), lambda i, j: (0, i)), pl.BlockSpec((t, tn), lambda i, j: (0, j))],
        out_specs=out_spec, out_shape=out_shape,
        scratch_shapes=[pltpu.VMEM((tm, t), bf16)],
        compiler_params=_cparams(2))(a, b)


ROWS_A = 16
ROWS_B = 32
UNROLL = 4


def _past(win, s):
    return (win if s == 0 else pltpu.roll(win, s, 0))[HALO:]


def _future(win, s):
    n = win.shape[0]
    return (win if s == 0 else pltpu.roll(win, n - s, 0))[:n - HALO]


def _fold8(v):
    return v.reshape(v.shape[0] // 8, 8, v.shape[1]).sum(axis=0)


def _last8(ref):
    return ref[...].astype(f32)[HBLK - HALO:]


def _first8(ref):
    return ref[...].astype(f32)[:HALO]


def _rd(ref, rows):
    return ref[rows, :].astype(f32)


def _halo_prev(tb, col):
    return lambda i: (jnp.maximum(i * (tb // HBLK) - 1, 0), col)


def _halo_next(tb, col, t):
    return lambda i: (jnp.minimum((i + 1) * (tb // HBLK), t // HBLK - 1), col)


def group_a_fwd(proj, wa, g, seq, tb, name):
    t = proj.shape[0]
    bps = seq // tb

    def body(xa_ref, ca_ref, ba_ref, xah_ref, cah_ref, wa_ref, g_ref, o_ref, u_scr):
        first = (pl.program_id(0) % bps) == 0
        u_scr[0:HALO, :] = jnp.where(first, 0.0, _last8(cah_ref) * _last8(xah_ref))
        w, gv = wa_ref[...], g_ref[...]

        def chunk(i, carry):
            r = pl.multiple_of(i * ROWS_A, ROWS_A)
            rows = pl.ds(r, ROWS_A)
            u_scr[pl.ds(pl.multiple_of(HALO + r, HALO), ROWS_A), :] = _rd(ca_ref, rows) * _rd(xa_ref, rows)
            win = u_scr[pl.ds(r, ROWS_A + HALO), :]
            cv = w[2:3] * _past(win, 0) + w[1:2] * _past(win, 1) + w[0:1] * _past(win, 2)
            o_ref[rows, :] = _rms_fwd(_rd(ba_ref, rows) * cv, gv).astype(bf16)
            return carry

        lax.fori_loop(0, tb // ROWS_A, chunk, 0, unroll=UNROLL)

    blk = lambda c: pl.BlockSpec((tb, D), lambda i: (i, c))
    return pl.pallas_call(
        body, name=name, grid=(t // tb,),
        in_specs=[blk(0), blk(1), blk(2),
                  pl.BlockSpec((HBLK, D), _halo_prev(tb, 0)), pl.BlockSpec((HBLK, D), _halo_prev(tb, 1)),
                  pl.BlockSpec((8, D), lambda i: (0, 0)), pl.BlockSpec((1, D), lambda i: (0, 0))],
        out_specs=pl.BlockSpec((tb, D), lambda i: (i, 0)),
        out_shape=_sds((t, 2 * D), bf16),
        scratch_shapes=[pltpu.VMEM((tb + HALO, D), f32)],
        compiler_params=_cparams(1))(proj, proj, proj, proj, proj, wa, g)


def group_a_bwd(proj, dcat, wa, g, seq, tb, name, token=None):
    t = proj.shape[0]
    bps = seq // tb

    def body(xa_ref, ca_ref, ba_ref, dy_ref, xap_ref, cap_ref, xan_ref, can_ref, ban_ref, dyn_ref, wa_ref, g_ref,
             *rest):
        dp_ref, dwa_ref, dg_ref, u_scr, d_scr, acc_scr = rest[-6:]
        i = pl.program_id(0)
        first = (i % bps) == 0
        last = (i % bps) == bps - 1
        w = wa_ref[...]
        gv = g_ref[...]
        u_scr[0:HALO, :] = jnp.where(first, 0.0, _last8(cap_ref) * _last8(xap_ref))
        u_scr[HALO + tb:2 * HALO + tb, :] = _first8(can_ref) * _first8(xan_ref)
        acc_scr[...] = jnp.zeros_like(acc_scr)

        def forward_part(n, carry):
            r = pl.multiple_of(n * ROWS_A, ROWS_A)
            rows = pl.ds(r, ROWS_A)
            ba = _rd(ba_ref, rows)
            u_scr[pl.ds(pl.multiple_of(HALO + r, HALO), ROWS_A), :] = _rd(ca_ref, rows) * _rd(xa_ref, rows)
            win = u_scr[pl.ds(r, ROWS_A + HALO), :]
            u = [_past(win, s) for s in range(3)]
            cv = w[2:3] * u[0] + w[1:2] * u[1] + w[0:1] * u[2]
            dya, dgc = _rms_bwd(ba * cv, gv, _rd(dy_ref, rows))
            dcv = dya * ba
            d_scr[rows, :] = dcv
            dp_ref[rows, 2 * D:3 * D] = (dya * cv).astype(bf16)
            acc_scr[0:8, :] += _fold8(dgc)
            for k in range(3):
                acc_scr[8 + 8 * k:16 + 8 * k, :] += _fold8(dcv * u[2 - k])
            return carry

        lax.fori_loop(0, tb // ROWS_A, forward_part, 0, unroll=UNROLL)

        start = HALO + tb
        cvn = (w[2:3] * u_scr[pl.ds(start, HALO), :] + w[1:2] * u_scr[pl.ds(start - 1, HALO), :]
               + w[0:1] * u_scr[pl.ds(start - 2, HALO), :])
        ban = _first8(ban_ref)
        dyan, _ = _rms_bwd(ban * cvn, gv, _first8(dyn_ref))
        d_scr[tb:tb + HALO, :] = jnp.where(last, 0.0, dyan * ban)

        def backward_part(n, carry):
            r = pl.multiple_of(n * ROWS_A, ROWS_A)
            rows = pl.ds(r, ROWS_A)
            win = d_scr[pl.ds(r, ROWS_A + HALO), :]
            du = w[2:3] * _future(win, 0) + w[1:2] * _future(win, 1) + w[0:1] * _future(win, 2)
            dp_ref[rows, 0:D] = (du * _rd(ca_ref, rows)).astype(bf16)
            dp_ref[rows, D:2 * D] = (du * _rd(xa_ref, rows)).astype(bf16)
            return carry

        lax.fori_loop(0, tb // ROWS_A, backward_part, 0, unroll=UNROLL)

        row = lax.broadcasted_iota(jnp.int32, (8, D), 0)
        dw = jnp.zeros((8, D), f32)
        for k in range(3):
            dw = jnp.where(row == k, jnp.sum(acc_scr[8 + 8 * k:16 + 8 * k, :], axis=0, keepdims=True), dw)
        _accum(dwa_ref, dw, i == 0)
        _accum(dg_ref, jnp.sum(acc_scr[0:8, :], axis=0, keepdims=True), i == 0)

    blk = lambda c: pl.BlockSpec((tb, D), lambda i: (i, c))
    prv = lambda c: pl.BlockSpec((HBLK, D), _halo_prev(tb, c))
    nxt = lambda c: pl.BlockSpec((HBLK, D), _halo_next(tb, c, t))
    return pl.pallas_call(
        body, name=name, grid=(t // tb,),
        in_specs=[blk(0), blk(1), blk(2), blk(0), prv(0), prv(1), nxt(0), nxt(1), nxt(2), nxt(0),
                  pl.BlockSpec((8, D), lambda i: (0, 0)), pl.BlockSpec((1, D), lambda i: (0, 0))] + _token_spec(token),
        out_specs=[pl.BlockSpec((tb, 3 * D), lambda i: (i, 0)), pl.BlockSpec((8, D), lambda i: (0, 0)),
                   pl.BlockSpec((1, D), lambda i: (0, 0))],
        out_shape=[_sds((t, PROJ), bf16), _sds((8, D), f32), _sds((1, D), f32)],
        scratch_shapes=[pltpu.VMEM((tb + 2 * HALO, D), f32), pltpu.VMEM((tb + HALO, D), f32), pltpu.VMEM((32, D), f32)],
        compiler_params=_cparams(1))(proj, proj, proj, dcat, proj, proj, proj, proj, proj, dcat, wa, g,
                                     *_token_arg(token))


CB = 512
XBC_BLK0 = COL_XBC // CB


def conv_b_fwd(proj, ws, bs, seq, tb, name):
    t = proj.shape[0]
    bps = seq // tb

    def body(x_ref, xp_ref, w_ref, b_ref, o_ref, da_ref, x_scr):
        first = (pl.program_id(1) % bps) == 0
        x_scr[0:HALO, :] = jnp.where(first, 0.0, _last8(xp_ref))
        w, bias = w_ref[...], b_ref[...]

        def chunk(n, carry):
            r = pl.multiple_of(n * ROWS_B, ROWS_B)
            rows = pl.ds(r, ROWS_B)
            x_scr[pl.ds(pl.multiple_of(HALO + r, HALO), ROWS_B), :] = _rd(x_ref, rows)
            win = x_scr[pl.ds(r, ROWS_B + HALO), :]
            xc = bias + w[3:4] * _past(win, 0)
            for k in range(3):
                xc = xc + w[k:k + 1] * _past(win, 3 - k)
            sg = _sigmoid(xc)
            o_ref[rows, :] = xc * sg
            da_ref[rows, :] = (sg * (1.0 + xc * (1.0 - sg))).astype(bf16)
            return carry

        lax.fori_loop(0, tb // ROWS_B, chunk, 0, unroll=UNROLL)

    return pl.pallas_call(
        body, name=name, grid=(XBC // CB, t // tb),
        in_specs=[pl.BlockSpec((tb, CB), lambda j, i: (i, XBC_BLK0 + j)),
                  pl.BlockSpec((HBLK, CB), lambda j, i: (jnp.maximum(i * (tb // HBLK) - 1, 0), XBC_BLK0 + j)),
                  pl.BlockSpec((8, CB), lambda j, i: (0, j)), pl.BlockSpec((1, CB), lambda j, i: (0, j))],
        out_specs=[pl.BlockSpec((tb, CB), lambda j, i: (i, j)), pl.BlockSpec((tb, CB), lambda j, i: (i, j))],
        out_shape=[_sds((t, XBC), f32), _sds((t, XBC), bf16)],
        scratch_shapes=[pltpu.VMEM((tb + HALO, CB), f32)],
        compiler_params=_cparams(2))(proj, proj, ws, bs)


def conv_b_bwd(proj, dxs, dact, ws, dproj, seq, tb, name):
    t = proj.shape[0]
    bps = seq // tb

    def body(x_ref, xp_ref, d_ref, dn_ref, a_ref, an_ref, w_ref, dproj_ref, dx_ref, dw_ref, db_ref, x_scr, d_scr,
             acc_scr):
        i = pl.program_id(1)
        first = (i % bps) == 0
        last = (i % bps) == bps - 1
        w = w_ref[...]
        x_scr[0:HALO, :] = jnp.where(first, 0.0, _last8(xp_ref))
        acc_scr[...] = jnp.zeros_like(acc_scr)

        def forward_part(n, carry):
            r = pl.multiple_of(n * ROWS_B, ROWS_B)
            rows = pl.ds(r, ROWS_B)
            x_scr[pl.ds(pl.multiple_of(HALO + r, HALO), ROWS_B), :] = _rd(x_ref, rows)
            win = x_scr[pl.ds(r, ROWS_B + HALO), :]
            dxc = _rd(d_ref, rows) * _rd(a_ref, rows)
            d_scr[rows, :] = dxc
            acc_scr[0:8, :] += _fold8(dxc)
            for k in range(4):
                acc_scr[8 + 8 * k:16 + 8 * k, :] += _fold8(dxc * _past(win, 3 - k))
            return carry

        lax.fori_loop(0, tb // ROWS_B, forward_part, 0, unroll=UNROLL)
        d_scr[tb:tb + HALO, :] = jnp.where(last, 0.0, _first8(dn_ref) * _first8(an_ref))

        def backward_part(n, carry):
            r = pl.multiple_of(n * ROWS_B, ROWS_B)
            win = d_scr[pl.ds(r, ROWS_B + HALO), :]
            dx = w[3:4] * _future(win, 0)
            for k in range(3):
                dx = dx + w[k:k + 1] * _future(win, 3 - k)
            dx_ref[pl.ds(r, ROWS_B), :] = dx.astype(bf16)
            return carry

        lax.fori_loop(0, tb // ROWS_B, backward_part, 0, unroll=UNROLL)

        row = lax.broadcasted_iota(jnp.int32, (8, CB), 0)
        dw = jnp.zeros((8, CB), f32)
        for k in range(4):
            dw = jnp.where(row == k, jnp.sum(acc_scr[8 + 8 * k:16 + 8 * k, :], axis=0, keepdims=True), dw)
        _accum(dw_ref, dw, i == 0)
        _accum(db_ref, jnp.sum(acc_scr[0:8, :], axis=0, keepdims=True), i == 0)

    nh = t // HBLK
    nxt = pl.BlockSpec((HBLK, CB), lambda j, i: (jnp.minimum((i + 1) * (tb // HBLK), nh - 1), j))
    cur = pl.BlockSpec((tb, CB), lambda j, i: (i, j))
    return pl.pallas_call(
        body, name=name, grid=(XBC // CB, t // tb),
        in_specs=[pl.BlockSpec((tb, CB), lambda j, i: (i, XBC_BLK0 + j)),
                  pl.BlockSpec((HBLK, CB), lambda j, i: (jnp.maximum(i * (tb // HBLK) - 1, 0), XBC_BLK0 + j)),
                  cur, nxt, cur, nxt, pl.BlockSpec((8, CB), lambda j, i: (0, j)), pl.BlockSpec(memory_space=pl.ANY)],
        out_specs=[pl.BlockSpec((tb, CB), lambda j, i: (i, XBC_BLK0 + j)), pl.BlockSpec((8, CB), lambda j, i: (0, j)),
                   pl.BlockSpec((1, CB), lambda j, i: (0, j))],
        out_shape=[_sds((t, PROJ), bf16), _sds((8, XBC), f32), _sds((1, XBC), f32)],
        input_output_aliases={7: 0},
        scratch_shapes=[pltpu.VMEM((tb + HALO, CB), f32), pltpu.VMEM((tb + HALO, CB), f32), pltpu.VMEM((40, CB), f32)],
        compiler_params=_cparams(2))(proj, proj, dxs, dxs, dact, dact, ws, dproj)


def place_columns(buf, part, col_block, tb, name):
    t, wdt = part.shape

    def body(p_ref, buf_ref, o_ref):
        o_ref[...] = p_ref[...]

    return pl.pallas_call(
        body, name=name, grid=(t // tb,),
        in_specs=[pl.BlockSpec((tb, wdt), lambda i: (i, 0)), pl.BlockSpec(memory_space=pl.ANY)],
        out_specs=pl.BlockSpec((tb, wdt), lambda i: (i, col_block)), out_shape=_sds(buf.shape, buf.dtype),
        input_output_aliases={1: 0}, compiler_params=_cparams(1))(part, buf)


GW = D // NG
EXPAND_TERMS = 2
REDUCE_TERMS = 1


def _ssd_consts():
    head_of_lane = jnp.arange(D) // HP
    expand = (jnp.arange(CH)[:, None] == head_of_lane[None, :]).astype(bf16)
    tri = (jnp.arange(CH)[:, None] >= jnp.arange(CH)[None, :]).astype(f32)
    return expand, tri


def _ssd_common(par_ref, dtr_ref, e_ref, tri_ref):
    par = par_ref[...]
    dtb, alog, dsk = par[0:1], par[1:2], par[2:3]
    lane = lax.broadcasted_iota(jnp.int32, (CH, CH), 1)
    a = -jnp.exp(alog)
    dtr = dtr_ref[...].astype(f32) + dtb
    sp = jnp.maximum(dtr, 0.0) + jnp.log(1.0 + jnp.exp(-jnp.abs(dtr)))
    dt = jnp.where(lane < NH, sp, 0.0)
    cs = jnp.dot(tri_ref[...], dt * a, precision=lax.Precision.HIGHEST, preferred_element_type=f32)
    cs_last = cs[CH - 1:CH, :]
    dte = jnp.exp(cs_last - cs)
    ecs = jnp.exp(cs)
    ecl = jnp.exp(cs_last)
    e = e_ref[...]
    row8 = lax.broadcasted_iota(jnp.int32, (8, CH), 0)
    r8 = _split_dot(jnp.where(row8 == 0, ecl, jnp.where(row8 == 1, dsk, 0.0)), e, 3)
    return dict(a=a, dtr=dtr, dt=dt, cs=cs, cst=cs.T, dte=dte, ecs=ecs, ecl=ecl, e=e, lane=lane,
                dt_x=_split_dot(dt, e, EXPAND_TERMS), dte_x=_split_dot(dte, e, EXPAND_TERMS),
                ecs_x=_split_dot(ecs, e, EXPAND_TERMS),
                ecl_x=r8[0:1], dsk_x=r8[1:2])


def _decay_matrix(c, h):
    li = lax.broadcasted_iota(jnp.int32, (CH, CH), 0)
    seg = c["cs"][:, h:h + 1] - c["cst"][h:h + 1, :]
    return jnp.exp(jnp.where(li >= c["lane"], seg, -jnp.inf))


def _gate_norm_fwd(y, z, gs):
    zg = z * _sigmoid(z)
    yg = y * zg
    return jnp.concatenate([_rms_fwd(yg[:, k * GW:(k + 1) * GW], gs[:, k * GW:(k + 1) * GW]) for k in range(NG)], axis=1)


def ssd_fwd(xbcs, proj, par, gs, cat, seq, name):
    t = xbcs.shape[0]
    nc = seq // CH
    expand, tri = _ssd_consts()

    def body(xs_ref, b_ref, c_ref, dtr_ref, z_ref, par_ref, e_ref, tri_ref, gs_ref, cat_ref, yn_ref, y_ref, st_ref,
             p_scr, yd_scr):
        @pl.when(pl.program_id(0) % nc == 0)
        def _():
            p_scr[...] = jnp.zeros_like(p_scr)

        c = _ssd_common(par_ref, dtr_ref, e_ref, tri_ref)
        xs = xs_ref[...]
        xdt = xs * c["dt_x"]
        xdt_b = xdt.astype(bf16)
        xdte_b = (xdt * c["dte_x"]).astype(bf16)
        p = p_scr[...]
        st_ref[0] = p
        p_b = p.astype(bf16)
        lo = c["lane"] < HP
        for g in range(NG):
            bg = b_ref[:, g * NS:(g + 1) * NS].astype(bf16)
            cg = c_ref[:, g * NS:(g + 1) * NS].astype(bf16)
            gmat = _dot_nt(cg, bg)
            for q in range(GW // CH):
                col = g * GW + q * CH
                xp = xdt_b[:, col:col + CH]
                h0 = col // HP
                m0 = (gmat * _decay_matrix(c, h0)).astype(bf16)
                m1 = (gmat * _decay_matrix(c, h0 + 1)).astype(bf16)
                stacked = jnp.concatenate([jnp.where(lo, xp, jnp.zeros_like(xp)),
                                           jnp.where(lo, jnp.zeros_like(xp), xp)], axis=0)
                yd_scr[:, col:col + CH] = _dot(jnp.concatenate([m0, m1], axis=1), stacked)
            gsl = slice(g * GW, (g + 1) * GW)
            yoff = _dot(cg, p_b[:, gsl]) * c["ecs_x"][:, gsl]
            yd_scr[:, gsl] = yd_scr[:, gsl] + yoff
            p_scr[:, gsl] = p[:, gsl] * c["ecl_x"][:, gsl] + _dot_tn(bg, xdte_b[:, gsl])
        y = yd_scr[...] + c["dsk_x"] * xs
        y_ref[...] = y
        yn_ref[...] = _gate_norm_fwd(y, z_ref[...].astype(f32), gs_ref[...]).astype(bf16)

    nb = t // CH
    return pl.pallas_call(
        body, name=name, grid=(nb,),
        in_specs=[pl.BlockSpec((CH, D), lambda i: (i, 0)),
                  pl.BlockSpec((CH, NG * NS), lambda i: (i, D // (NG * NS))),
                  pl.BlockSpec((CH, NG * NS), lambda i: (i, D // (NG * NS) + 1)),
                  pl.BlockSpec((CH, CH), lambda i: (i, COL_DT // CH)),
                  pl.BlockSpec((CH, D), lambda i: (i, COL_Z // D)),
                  pl.BlockSpec((8, CH), lambda i: (0, 0)), pl.BlockSpec((CH, D), lambda i: (0, 0)),
                  pl.BlockSpec((CH, CH), lambda i: (0, 0)), pl.BlockSpec((1, D), lambda i: (0, 0)),
                  pl.BlockSpec(memory_space=pl.ANY)],
        out_specs=[pl.BlockSpec((CH, D), lambda i: (i, 1)), pl.BlockSpec((CH, D), lambda i: (i, 0)),
                   pl.BlockSpec((1, NS, D), lambda i: (i, 0, 0))],
        out_shape=[_sds((t, 2 * D), bf16), _sds((t, D), f32), _sds((nb, NS, D), f32)],
        input_output_aliases={9: 0},
        scratch_shapes=[pltpu.VMEM((NS, D), f32), pltpu.VMEM((CH, D), f32)],
        compiler_params=_cparams(1))(xbcs, xbcs, xbcs, proj, proj, par, expand, tri, gs, cat)


def ssd_bwd(xbcs, proj, ypre, states, dcat, par, gs, dproj, seq, name):
    t = xbcs.shape[0]
    nc = seq // CH
    expand, tri = _ssd_consts()

    def body(xs_ref, b_ref, c_ref, dtr_ref, z_ref, y_ref, st_ref, dyn_ref, par_ref, e_ref, tri_ref, gs_ref, dproj_ref,
             dx_ref, dz_ref, ddt_ref, dpar_ref, dgs_ref, dp_scr, dxdt_scr):
        i = pl.program_id(0)

        @pl.when(i % nc == 0)
        def _():
            dp_scr[...] = jnp.zeros_like(dp_scr)

        c = _ssd_common(par_ref, dtr_ref, e_ref, tri_ref)
        e = c["e"]
        lane = c["lane"]
        sub = lax.broadcasted_iota(jnp.int32, (CH, CH), 0)
        xs = xs_ref[...]
        xdt = xs * c["dt_x"]
        xdt_b = xdt.astype(bf16)
        xdte_b = (xdt * c["dte_x"]).astype(bf16)
        p = st_ref[0]
        p_b = p.astype(bf16)
        dpn = dp_scr[...]
        dpn_b = dpn.astype(bf16)

        y, z, gs_v = y_ref[...], z_ref[...].astype(f32), gs_ref[...]
        zs = _sigmoid(z)
        zg = z * zs
        yg = y * zg
        parts, gparts = [], []
        for k in range(NG):
            sl = slice(k * GW, (k + 1) * GW)
            dxk, dgk = _rms_bwd(yg[:, sl], gs_v[:, sl], dyn_ref[:, sl].astype(f32))
            parts.append(dxk)
            gparts.append(dgk)
        dyg = jnp.concatenate(parts, axis=1)
        dgs_rows = jnp.concatenate(gparts, axis=1)
        dy = dyg * zg
        dz_ref[...] = (dyg * y * (zs * (1.0 + z * (1.0 - zs)))).astype(bf16)
        dy_b = dy.astype(bf16)
        dq_b = (dy * c["ecs_x"]).astype(bf16)

        lo = lane < HP
        dcs = jnp.zeros((CH, CH), f32)
        dcst = jnp.zeros((CH, CH), f32)
        for g in range(NG):
            gsl = slice(g * GW, (g + 1) * GW)
            bg = b_ref[:, g * NS:(g + 1) * NS].astype(bf16)
            cg = c_ref[:, g * NS:(g + 1) * NS].astype(bf16)
            gmat = _dot_nt(cg, bg)
            dgm = jnp.zeros((CH, CH), f32)
            for q in range(GW // CH):
                col = g * GW + q * CH
                xp = xdt_b[:, col:col + CH]
                dyp = dy_b[:, col:col + CH]
                zero = jnp.zeros_like(dyp)
                xp2 = jnp.concatenate([jnp.where(lo, xp, zero), jnp.where(lo, zero, xp)], axis=0)
                dy2 = jnp.concatenate([jnp.where(lo, dyp, zero), jnp.where(lo, zero, dyp)], axis=0)
                dm2 = _dot_nt(dyp, xp2)
                ms = []
                for hh in range(2):
                    h = col // HP + hh
                    dec = _decay_matrix(c, h)
                    m = gmat * dec
                    dm = dm2[:, hh * CH:(hh + 1) * CH]
                    dseg = dm * m
                    dcs = dcs + jnp.where(lane == h, jnp.sum(dseg, axis=1, keepdims=True), 0.0)
                    dcst = dcst + jnp.where(sub == h, jnp.sum(dseg, axis=0, keepdims=True), 0.0)
                    dgm = dgm + dm * dec
                    ms.append(m.astype(bf16))
                dxdt_scr[:, col:col + CH] = _dot_tn(jnp.concatenate(ms, axis=0), dy2)
            dgm_b = dgm.astype(bf16)
            bds = _dot(bg, dpn_b[:, gsl])
            dxdt_scr[:, gsl] = dxdt_scr[:, gsl] + c["dte_x"][:, gsl] * bds
            dc_g = _dot(dgm_b, bg) + _dot_nt(dq_b[:, gsl], p_b[:, gsl])
            db_g = _dot_tn(dgm_b, cg) + _dot_nt(xdte_b[:, gsl], dpn_b[:, gsl])
            dx_ref[:, D + g * NS:D + (g + 1) * NS] = db_g
            dx_ref[:, D + NG * NS + g * NS:D + NG * NS + (g + 1) * NS] = dc_g
            dp_scr[:, gsl] = dpn[:, gsl] * c["ecl_x"][:, gsl] + _dot_tn(cg, dq_b[:, gsl])
            q_g = _dot(cg, p_b[:, gsl])
            e_g = e[:, gsl]
            dcs = dcs + c["ecs"] * _split_dot(dy[:, gsl] * q_g, e_g, REDUCE_TERMS, nt=True)
            ddte = _split_dot(xdt[:, gsl] * bds, e_g, REDUCE_TERMS, nt=True) * c["dte"]
            dcs = dcs - ddte
            dcs = dcs + jnp.where(sub == CH - 1, jnp.sum(ddte, axis=0, keepdims=True), 0.0)

        decl = _split_dot(jnp.broadcast_to(jnp.sum(dpn * p, axis=0, keepdims=True), (8, D)), e, 2, nt=True)[0:1]
        dcs = dcs + jnp.where(sub == CH - 1, c["ecl"] * decl, 0.0)
        dcs = dcs - dcst.T
        dadt = lax.dot_general(tri_ref[...], dcs, (((0,), (0,)), ((), ())), precision=lax.Precision.HIGHEST,
                               preferred_element_type=f32)
        dxdt = dxdt_scr[...]
        ddt = dadt * c["a"] + _split_dot(dxdt * xs, e, REDUCE_TERMS, nt=True)
        ddtr = jnp.where(lane < NH, ddt * _sigmoid(c["dtr"]), 0.0)
        ddt_ref[...] = ddtr.astype(bf16)
        dx_ref[:, 0:D] = dxdt * c["dt_x"] + c["dsk_x"] * dy
        dsk = _split_dot(jnp.broadcast_to(jnp.sum(dy * xs, axis=0, keepdims=True), (8, D)), e, 2, nt=True)[0:1]
        dalog = jnp.sum(dadt * c["dt"], axis=0, keepdims=True) * c["a"]
        row8 = lax.broadcasted_iota(jnp.int32, (8, CH), 0)
        dpar = jnp.where(row8 == 0, jnp.sum(ddtr, axis=0, keepdims=True),
                         jnp.where(row8 == 1, dalog, jnp.where(row8 == 2, dsk, 0.0)))
        dpar = jnp.where(lax.broadcasted_iota(jnp.int32, (8, CH), 1) < NH, dpar, 0.0)
        _accum(dpar_ref, dpar, i == 0)
        _accum(dgs_ref, jnp.sum(dgs_rows, axis=0, keepdims=True), i == 0)

    nb = t // CH
    rev = lambda i: (i // nc) * nc + (nc - 1 - i % nc)
    return pl.pallas_call(
        body, name=name, grid=(nb,),
        in_specs=[pl.BlockSpec((CH, D), lambda i: (rev(i), 0)),
                  pl.BlockSpec((CH, NG * NS), lambda i: (rev(i), D // (NG * NS))),
                  pl.BlockSpec((CH, NG * NS), lambda i: (rev(i), D // (NG * NS) + 1)),
                  pl.BlockSpec((CH, CH), lambda i: (rev(i), COL_DT // CH)),
                  pl.BlockSpec((CH, D), lambda i: (rev(i), COL_Z // D)),
                  pl.BlockSpec((CH, D), lambda i: (rev(i), 0)),
                  pl.BlockSpec((1, NS, D), lambda i: (rev(i), 0, 0)),
                  pl.BlockSpec((CH, D), lambda i: (rev(i), 1)),
                  pl.BlockSpec((8, CH), lambda i: (0, 0)), pl.BlockSpec((CH, D), lambda i: (0, 0)),
                  pl.BlockSpec((CH, CH), lambda i: (0, 0)), pl.BlockSpec((1, D), lambda i: (0, 0)),
                  pl.BlockSpec(memory_space=pl.ANY)],
        out_specs=[pl.BlockSpec((CH, XBC), lambda i: (rev(i), 0)), pl.BlockSpec((CH, D), lambda i: (rev(i), COL_Z // D)),
                   pl.BlockSpec((CH, CH), lambda i: (rev(i), 0)),
                   pl.BlockSpec((8, CH), lambda i: (0, 0)), pl.BlockSpec((1, D), lambda i: (0, 0))],
        out_shape=[_sds((t, XBC), f32), _sds((t, PROJ), bf16), _sds((t, CH), bf16), _sds((8, CH), f32), _sds((1, D), f32)],
        input_output_aliases={12: 1},
        scratch_shapes=[pltpu.VMEM((NS, D), f32), pltpu.VMEM((CH, D), f32)],
        compiler_params=_cparams(1))(xbcs, xbcs, xbcs, proj, proj, ypre, states, dcat, par, expand, tri, gs, dproj)


def loss_head(y, target, tb, name):
    t = y.shape[0]

    def body(y_ref, t_ref, s_ref, dy_ref):
        err = y_ref[...] - t_ref[...]
        dy_ref[...] = err * (1.0 / D)
        _accum(s_ref, jnp.zeros((8, CH), f32) + jnp.sum(err * err), pl.program_id(0) == 0)

    return pl.pallas_call(
        body, name=name, grid=(t // tb,),
        in_specs=[pl.BlockSpec((tb, D), lambda i: (i, 0)), pl.BlockSpec((tb, D), lambda i: (i, 0))],
        out_specs=[pl.BlockSpec((8, CH), lambda i: (0, 0)), pl.BlockSpec((tb, D), lambda i: (i, 0))],
        out_shape=[_sds((8, CH), f32), _sds((t, D), f32)],
        compiler_params=_cparams(1))(y, target)


def _tiles(t, seq):
    tm = min(512, t)
    return dict(tm=tm, tm_small=min(256, t), tm_large=min(1024, t), tm_huge=min(2048, t), tb=min(512, seq))


def local_step(x, target, depth, weights_of, seq, grads_done=None):
    t = x.shape[0]
    ts = _tiles(t, seq)
    tm, tl, th, tb = ts["tm"], ts["tm_large"], ts["tm_huge"], ts["tb"]
    saved, ws = [], []
    for l in range(depth):
        w = weights_of(l, x)
        ws.append(w)
        proj, h1 = norm_matmul(x, w["g1"], w["win"], th, 1920, bf16, "in_proj", token=w.get("token"))
        cat = group_a_fwd(proj, w["wa"], w["ga"], seq, tb, "group_a_fwd")
        xbcs, dact = conv_b_fwd(proj, w["ws"], w["bs"], seq, tb, "conv_b_fwd")
        cat, ypre, states = ssd_fwd(xbcs, proj, w["par"], w["gs"], cat, seq, "ssd_fwd")
        if "late" in w:
            w.update(w.pop("late")(cat))
        mix, x2 = matmul_postnorm(cat, w["wo"], x, w["g2"], tl, "out_proj")
        fp, h2, o, x3 = mlp_fwd(x2, w["g3"], w["wu"], w["wd"], w["g4"], tm, "mlp_fwd")
        saved.append(dict(x=x, proj=proj, h1=h1, xbcs=xbcs, dact=dact, ypre=ypre, states=states, cat=cat, mix=mix, x2=x2,
                          fp=fp, h2=h2, o=o))
        x = x3
    sse, dx = loss_head(x, target, tm, "loss_head")
    grads = [None] * depth
    for l in reversed(range(depth)):
        s, w = saved[l], ws[l]
        do, dfp, dx2, dg4, dg3 = mlp_bwd(s["o"], w["g4"], dx, w["wd"], s["fp"], w["wu"], s["x2"], w["g3"],
                                         tm, "mlp_bwd")
        dwd = matmul_tn(s["fp"], do, 512, 1024, True, "mlp_down_dw")
        dwu = matmul_tn(s["h2"], dfp, tl, 1024, False, "mlp_up_dw", col_blocks=True)
        dmix, dg2, dcat = postnorm_bwd_matmul(s["mix"], w["g2"], dx2, w["wo"], tl, 2 * D, "out_proj_bwd")
        dwo = matmul_tn(s["cat"], dmix, 512, 1024, False, "out_proj_dw")
        token = None if grads_done is None else grads_done(l, dict(wo=dwo, wu=dwu, wd=dwd), False)
        dproj, dwa, dga = group_a_bwd(s["proj"], dcat, w["wa"], w["ga"], seq, tb, "group_a_bwd", token=token)
        dxbcs, dproj, ddt, dpar, dgs = ssd_bwd(s["xbcs"], s["proj"], s["ypre"], s["states"], dcat, w["par"], w["gs"],
                                               dproj, seq, "ssd_bwd")
        dproj, dws, dbs = conv_b_bwd(s["proj"], dxbcs, s["dact"], w["ws"], dproj, seq, tb, "conv_b_bwd")
        dproj = place_columns(dproj, ddt, COL_DT // CH, tm, "place_ddt")
        dwin = matmul_tn(s["h1"], dproj, tl, 1152, False, "in_proj_dw")
        token = None if grads_done is None else grads_done(l, dict(win=dwin), True)
        dx, dg1 = matmul_prenorm_bwd(dproj, w["win"], s["x"], w["g1"], dx2, tm, "in_proj_bwd", token=token)
        grads[l] = dict(win=dwin, wo=dwo, wu=dwu, wd=dwd, wa=dwa, ws=dws, bs=dbs, par=dpar,
                        g1=dg1, ga=dga, gs=dgs, g2=dg2, g3=dg3, g4=dg4)
    return sse, dx, grads


GROUPS = {
    "chips": [(1, 0, 0), (0, 1, 0), (1, 1, 0)],
    "pair": [(0, 0, 1)],
    "all": [(1, 0, 0), (0, 1, 0), (1, 1, 0), (0, 0, 1), (1, 0, 1), (0, 1, 1), (1, 1, 1)],
}


def _group_index(group, x, y, c):
    return {"chips": 2 * x + y, "pair": c, "all": 4 * x + 2 * y + c}[group]


def _chunk_indices(shape, pieces):
    if len(shape) < 3:
        return [()]
    lead = [()]
    for n in shape[:-2]:
        lead = [i + (k,) for i in lead for k in range(n)]
    rows = shape[-2]
    split = max(1, pieces // len(lead))
    while split > 1 and (rows % split or (rows // split) % 16):
        split -= 1
    step = rows // split
    return [i + (pl.ds(s * step, step),) for i in lead for s in range(split)]


def _exchange(arrays, out_shapes, group, src_view, dst_view, view_shape, name, own, pieces=16):
    masks = GROUPS[group]
    na, nm = len(arrays), len(masks)
    cuts = [_chunk_indices(view_shape(a), pieces) for a in range(na)]

    def body(*refs):
        ins, outs = refs[:na], refs[na:2 * na]
        send_sems, recv_sems = refs[2 * na:2 * na + 2]
        local_sems = refs[2 * na + 2] if own else None
        x, y, c = lax.axis_index("x"), lax.axis_index("y"), lax.axis_index("c")
        me = _group_index(group, x, y, c)
        peers = []
        for mx, my, mc in masks:
            px, py, pc = (1 - x if mx else x), (1 - y if my else y), (1 - c if mc else c)
            peers.append(((px, py, pc), _group_index(group, px, py, pc)))

        def part(ref, idx):
            return ref.at[idx] if idx else ref

        if own:
            for a in range(na):
                for idx in cuts[a]:
                    pltpu.make_async_copy(part(src_view(ins[a], a, me), idx), part(dst_view(outs[a], a, me), idx),
                                          local_sems.at[a]).start()
        for a in range(na):
            for j, (dev, pidx) in enumerate(peers):
                for idx in cuts[a]:
                    pltpu.make_async_remote_copy(
                        src_ref=part(src_view(ins[a], a, pidx), idx), dst_ref=part(dst_view(outs[a], a, me), idx),
                        send_sem=send_sems.at[a * nm + j], recv_sem=recv_sems.at[a * nm + j],
                        device_id=dev, device_id_type=MESH).start()
        whole = []
        for a in range(na):
            for j, (dev, pidx) in enumerate(peers):
                whole.append(pltpu.make_async_remote_copy(
                    src_ref=src_view(ins[a], a, pidx), dst_ref=dst_view(outs[a], a, pidx),
                    send_sem=send_sems.at[a * nm + j], recv_sem=recv_sems.at[a * nm + j],
                    device_id=dev, device_id_type=MESH))
        for cp in whole:
            cp.wait_recv()
        for cp in whole:
            cp.wait_send()
        if own:
            for a in range(na):
                pltpu.make_async_copy(src_view(ins[a], a, me), dst_view(outs[a], a, me), local_sems.at[a]).wait()

    hbm = pl.BlockSpec(memory_space=pltpu.HBM)
    sems = [pltpu.SemaphoreType.DMA((na * nm,)), pltpu.SemaphoreType.DMA((na * nm,))]
    return pl.pallas_call(
        body, name=name, in_specs=[hbm] * na, out_specs=[hbm] * na,
        out_shape=[_sds(s, a.dtype) for s, a in zip(out_shapes, arrays)],
        scratch_shapes=sems + ([pltpu.SemaphoreType.DMA((na,))] if own else []))(*arrays)


def all_gather(arrays, group, name, slot_axis=0, own=True):
    n = len(GROUPS[group]) + 1
    shapes = [a.shape[:slot_axis] + (n,) + a.shape[slot_axis:] for a in arrays]
    lead = (slice(None),) * slot_axis
    return _exchange(arrays, shapes, group, lambda r, a, i: r, lambda r, a, i: r.at[lead + (i,)],
                     lambda a: arrays[a].shape, name, own)


HBM_SPEC = pl.BlockSpec(memory_space=pltpu.HBM)
SEM_SPEC = pl.BlockSpec(memory_space=pltpu.SEMAPHORE)
DATAFLOW = pltpu.SideEffectType.DATAFLOW_SIDE_EFFECTING
N_CHIPS = 4


def _peers(group, x, y, c):
    out = []
    for mx, my, mc in GROUPS[group]:
        px, py, pc = (1 - x if mx else x), (1 - y if my else y), (1 - c if mc else c)
        out.append(((px, py, pc), _group_index(group, px, py, pc)))
    return out


def _whole_views(sources):
    return dict(src=lambda ref, a, c, to: ref, dst=lambda ref, a, c, sender: ref.at[sender],
                rows=lambda a: sources[a].shape[0])


def _weight_views(shards):
    half = [s.shape[0] // 2 for s in shards]
    return dict(src=lambda ref, a, c, to_chip: ref.at[pl.ds(c * half[a], half[a])],
                dst=lambda ref, a, c, from_chip: ref.at[from_chip, pl.ds(c * half[a], half[a])],
                rows=lambda a: half[a])


def _grad_views(sums):
    return dict(src=lambda ref, a, c, to_chip: ref.at[to_chip], dst=lambda ref, a, c, from_chip: ref.at[from_chip],
                rows=lambda a: sums[a].shape[1])


def chips_start(sources, zones, views, name, pieces=4, after=None, group="chips"):
    na, nm = len(sources), len(GROUPS[group])

    def body(*refs):
        ins, lands = refs[:na], refs[na:2 * na]
        n_in = 2 * na + len(_token_arg(after))
        send_sems, recv_sems, token = refs[n_in], refs[n_in + 1], refs[-1]
        x, y, c = lax.axis_index("x"), lax.axis_index("y"), lax.axis_index("c")
        me = _group_index(group, x, y, c)
        for a in range(na):
            step = views["rows"](a) // pieces
            for j, (dev, to) in enumerate(_peers(group, x, y, c)):
                for q in range(pieces):
                    rows = pl.ds(q * step, step)
                    pltpu.make_async_remote_copy(
                        src_ref=views["src"](ins[a], a, c, to).at[rows],
                        dst_ref=views["dst"](lands[a], a, c, me).at[rows],
                        send_sem=send_sems.at[a * nm + j], recv_sem=recv_sems.at[a * nm + j],
                        device_id=dev, device_id_type=MESH).start()
        token[...] = jnp.zeros_like(token)

    both = list(sources) + list(zones)
    outs = pl.pallas_call(
        body, name=name,
        out_shape=(pltpu.SemaphoreType.DMA((na * nm,)), pltpu.SemaphoreType.DMA((na * nm,)),
                   *[pltpu.HBM(b.shape, b.dtype) for b in both], _sds((8, CH), f32)),
        in_specs=[HBM_SPEC] * (2 * na) + _token_spec(after),
        out_specs=(SEM_SPEC, SEM_SPEC, *[HBM_SPEC] * (2 * na), pl.BlockSpec(memory_space=pltpu.VMEM)),
        input_output_aliases={i: 2 + i for i in range(2 * na)},
        compiler_params=pltpu.CompilerParams(has_side_effects=DATAFLOW))(
            *[pltpu.with_memory_space_constraint(b, pltpu.HBM) for b in both], *_token_arg(after))
    return dict(send=outs[0], recv=outs[1], sources=list(outs[2:2 + na]), zones=list(outs[2 + na:2 + 2 * na]),
                token=outs[-1], views=views, group=group)


def chips_wait(started, after, name):
    sources, zones, views, group = started["sources"], started["zones"], started["views"], started["group"]
    na, nm = len(sources), len(GROUPS[group])

    def body(*refs):
        ins, lands = refs[:na], refs[na:2 * na]
        send_sems, recv_sems = refs[2 * na], refs[2 * na + 1]
        x, y, c = lax.axis_index("x"), lax.axis_index("y"), lax.axis_index("c")
        for a in range(na):
            for j, (dev, peer) in enumerate(_peers(group, x, y, c)):
                cp = pltpu.make_async_remote_copy(
                    src_ref=views["src"](ins[a], a, c, peer), dst_ref=views["dst"](lands[a], a, c, peer),
                    send_sem=send_sems.at[a * nm + j], recv_sem=recv_sems.at[a * nm + j],
                    device_id=dev, device_id_type=MESH)
                cp.wait_send()
                cp.wait_recv()

    both = list(sources) + list(zones)
    outs = pl.pallas_call(
        body, name=name, out_shape=tuple(pltpu.HBM(b.shape, b.dtype) for b in both),
        in_specs=[HBM_SPEC] * (2 * na) + [SEM_SPEC, SEM_SPEC, pl.BlockSpec(memory_space=pl.ANY)],
        out_specs=tuple([HBM_SPEC] * (2 * na)), input_output_aliases={i: i for i in range(2 * na)},
        compiler_params=pltpu.CompilerParams(has_side_effects=DATAFLOW))(*both, started["send"], started["recv"], after)
    return list(outs[:na]), list(outs[na:])


def weights_share(zones, name):
    na, nm = len(zones), N_CHIPS - 1

    def body(*refs):
        lands = refs[na:2 * na]
        send_sems, recv_sems = refs[2 * na:]
        x, y, c = lax.axis_index("x"), lax.axis_index("y"), lax.axis_index("c")
        chip = 2 * x + y
        sibling = (x, y, 1 - c)
        sends = []
        for a in range(na):
            half = zones[a].shape[1] // 2
            for m in range(1, N_CHIPS):
                mine = lands[a].at[chip ^ m, pl.ds(c * half, half)]
                sends.append(pltpu.make_async_remote_copy(
                    src_ref=mine, dst_ref=mine, send_sem=send_sems.at[a * nm + m - 1],
                    recv_sem=recv_sems.at[a * nm + m - 1], device_id=sibling, device_id_type=MESH))
        for cp in sends:
            cp.start()
        for a in range(na):
            half = zones[a].shape[1] // 2
            for m in range(1, N_CHIPS):
                theirs = lands[a].at[chip ^ m, pl.ds((1 - c) * half, half)]
                pltpu.make_async_remote_copy(
                    src_ref=theirs, dst_ref=theirs, send_sem=send_sems.at[a * nm + m - 1],
                    recv_sem=recv_sems.at[a * nm + m - 1], device_id=sibling, device_id_type=MESH).wait_recv()
        for cp in sends:
            cp.wait_send()

    return pl.pallas_call(
        body, name=name, in_specs=[HBM_SPEC] * na, out_specs=[HBM_SPEC] * na,
        out_shape=[_sds(z.shape, z.dtype) for z in zones], input_output_aliases={i: i for i in range(na)},
        scratch_shapes=[pltpu.SemaphoreType.DMA((na * nm,)), pltpu.SemaphoreType.DMA((na * nm,))])(*zones)


def pair_send_halves(grads, name):
    half = [g.shape[1] // 2 for g in grads]
    shapes = [(g.shape[0], h, g.shape[2]) for g, h in zip(grads, half)]
    return _exchange(grads, shapes, "pair", lambda r, a, i: r.at[:, pl.ds(i * half[a], half[a])],
                     lambda r, a, i: r, lambda a: shapes[a], name, False)


def sum_pair_half(g, recv, core, name, tb=256, by_chip=None):
    nk, r, c = g.shape
    tb = min(tb, r // 2)
    nb = r // 2 // tb

    def body(core_ref, g_ref, r_ref, o_ref):
        s = g_ref[...].astype(f32) + r_ref[...].astype(f32)
        if by_chip is None:
            o_ref[...] = s.astype(bf16)
        else:
            for k in range(by_chip[0]):
                o_ref[k] = s[:, k * by_chip[1]:(k + 1) * by_chip[1]].astype(bf16)

    if by_chip is None:
        out_spec = pl.BlockSpec((None, tb, c), lambda k, i, core_ref: (k, i, 0))
        out_shape = _sds((nk, r // 2, c), bf16)
    else:
        assert nk == 1
        out_spec = pl.BlockSpec((by_chip[0], tb, by_chip[1]), lambda k, i, core_ref: (0, i, 0))
        out_shape = _sds((by_chip[0], r // 2, by_chip[1]), bf16)
    return pl.pallas_call(
        body, name=name,
        grid_spec=pltpu.PrefetchScalarGridSpec(
            num_scalar_prefetch=1, grid=(nk, nb),
            in_specs=[pl.BlockSpec((None, tb, c), lambda k, i, core_ref: (k, core_ref[0] * nb + i, 0)),
                      pl.BlockSpec((None, tb, c), lambda k, i, core_ref: (k, i, 0))],
            out_specs=out_spec),
        out_shape=out_shape, compiler_params=_cparams(2))(jnp.reshape(core, (1,)).astype(jnp.int32), g, recv)


def assemble_columns(blocks, width, name, tb=256):
    n, r, c = blocks.shape

    def body(b_ref, o_ref):
        for k in range(n):
            o_ref[:, k * c:(k + 1) * c] = b_ref[k]
        o_ref[:, n * c:] = jnp.zeros((tb, width - n * c), blocks.dtype)

    return pl.pallas_call(
        body, name=name, grid=(r // tb,), in_specs=[pl.BlockSpec((n, tb, c), lambda i: (0, i, 0))],
        out_specs=pl.BlockSpec((tb, width), lambda i: (i, 0)), out_shape=_sds((r, width), blocks.dtype),
        compiler_params=_cparams(1))(blocks)


def chip_sum_into(acc, layer, own, others, chip, name, tb=256):
    n, r, c = own.shape
    tb = min(tb, r)

    def body(chip_ref, x_ref, y1_ref, y2_ref, y3_ref, acc_ref, o_ref):
        o_ref[...] = ((x_ref[...].astype(f32) + y1_ref[...].astype(f32)) + y2_ref[...].astype(f32)) + y3_ref[...].astype(f32)

    def slot(k):
        return pl.BlockSpec((None, tb, c), lambda i, chip_ref: (chip_ref[0] ^ k, i, 0))

    return pl.pallas_call(
        body, name=name,
        grid_spec=pltpu.PrefetchScalarGridSpec(
            num_scalar_prefetch=1, grid=(r // tb,),
            in_specs=[slot(k) for k in range(n)] + [pl.BlockSpec(memory_space=pl.ANY)],
            out_specs=pl.BlockSpec((None, tb, c), lambda i, chip_ref: (layer, i, 0))),
        out_shape=_sds(acc.shape, f32), input_output_aliases={n + 1: 0}, compiler_params=_cparams(1))(
            jnp.reshape(chip, (1,)).astype(jnp.int32), own, *([others] * (n - 1)), acc)


def adamw_half(w, g_half, m, v, half, name, before=None, token=None, tb=256):
    depth, r, c = w.shape
    tb = min(tb, r // 2)
    nb = r // 2 // tb
    n_extra = (0 if before is None else 4) + len(_token_arg(token))

    def body(half_ref, w_ref, gh_ref, m_ref, v_ref, *rest):
        g_ref, d_ref, mo_ref, vo_ref = rest[n_extra:]
        gv = gh_ref[...]
        m2 = B1 * m_ref[...] + (1.0 - B1) * gv
        v2 = B2 * v_ref[...] + (1.0 - B2) * (gv * gv)
        m_hat = m2 / (1.0 - B1 ** STEP)
        v_hat = v2 / (1.0 - B2 ** STEP)
        g_ref[...] = gv
        d_ref[...] = -LR * (m_hat / (jnp.sqrt(v_hat) + AEPS) + WD * w_ref[...])
        mo_ref[...] = m2
        vo_ref[...] = v2

    whole = pl.BlockSpec((None, tb, c), lambda l, i, half_ref: (l, half_ref[0] * nb + i, 0))
    part = pl.BlockSpec((None, tb, c), lambda l, i, half_ref: (l, i, 0))
    extra = ([] if before is None else list(before)) + _token_arg(token)
    return pl.pallas_call(
        body, name=name,
        grid_spec=pltpu.PrefetchScalarGridSpec(
            num_scalar_prefetch=1, grid=(depth, nb),
            in_specs=[whole, part, whole, whole] + [pl.BlockSpec(memory_space=pl.ANY)] * n_extra, out_specs=[whole] * 4),
        out_shape=[_sds(w.shape, f32)] * 4,
        input_output_aliases={} if before is None else {5 + k: k for k in range(4)},
        compiler_params=_cparams(2))(jnp.reshape(half, (1,)).astype(jnp.int32), w, g_half, m, v, *extra)


def sum_slots(y, out_dtype, name, tb=256):
    n, r, c = y.shape
    tb = min(tb, r)

    def body(y_ref, o_ref):
        acc = y_ref[0].astype(f32)
        for i in range(1, n):
            acc = acc + y_ref[i].astype(f32)
        o_ref[...] = acc.astype(out_dtype)

    return pl.pallas_call(
        body, name=name, grid=(r // tb,),
        in_specs=[pl.BlockSpec((n, tb, c), lambda i: (0, i, 0))], out_specs=pl.BlockSpec((tb, c), lambda i: (i, 0)),
        out_shape=_sds((r, c), out_dtype), compiler_params=_cparams(1))(y)


def adamw(w, g, m, v, name, tb=256):
    r, c = w.shape
    tb = min(tb, r)

    def body(w_ref, g_ref, m_ref, v_ref, d_ref, mo_ref, vo_ref):
        gv = g_ref[...]
        m2 = B1 * m_ref[...] + (1.0 - B1) * gv
        v2 = B2 * v_ref[...] + (1.0 - B2) * (gv * gv)
        m_hat = m2 / (1.0 - B1 ** STEP)
        v_hat = v2 / (1.0 - B2 ** STEP)
        d_ref[...] = -LR * (m_hat / (jnp.sqrt(v_hat) + AEPS) + WD * w_ref[...])
        mo_ref[...] = m2
        vo_ref[...] = v2

    spec = pl.BlockSpec((tb, c), lambda i: (i, 0))
    return pl.pallas_call(
        body, name=name, grid=(r // tb,), in_specs=[spec] * 4, out_specs=[spec] * 3,
        out_shape=[_sds((r, c), f32)] * 3, compiler_params=_cparams(1))(w, g, m, v)


def adamw_leading(w, g, m, v, name, tc=64):
    c, l, r = w.shape
    main = c // tc
    tail = c - main * tc

    def body(w_ref, g_ref, m_ref, v_ref, *rest):
        d_ref, mo_ref, vo_ref = rest[-3:]
        gv = g_ref[...]
        m2 = B1 * m_ref[...] + (1.0 - B1) * gv
        v2 = B2 * v_ref[...] + (1.0 - B2) * (gv * gv)
        m_hat = m2 / (1.0 - B1 ** STEP)
        v_hat = v2 / (1.0 - B2 ** STEP)
        d_ref[...] = -LR * (m_hat / (jnp.sqrt(v_hat) + AEPS) + WD * w_ref[...])
        mo_ref[...] = m2
        vo_ref[...] = v2

    spec = pl.BlockSpec((tc, l, r), lambda i: (i, 0, 0))
    outs = pl.pallas_call(
        functools.partial(body), name=name, grid=(main,), in_specs=[spec] * 4, out_specs=[spec] * 3,
        out_shape=[_sds(w.shape, f32)] * 3, compiler_params=_cparams(1))(w, g, m, v)
    if tail:
        assert (main * tc) % tail == 0
        last = pl.BlockSpec((tail, l, r), lambda i: (main * tc // tail, 0, 0))
        outs = pl.pallas_call(
            functools.partial(body), name=name + "_tail", grid=(1,),
            in_specs=[last] * 4 + [pl.BlockSpec(memory_space=pl.ANY)] * 3, out_specs=[last] * 3,
            out_shape=[_sds(w.shape, f32)] * 3, input_output_aliases={4: 0, 5: 1, 6: 2},
            compiler_params=_cparams(1))(w, g, m, v, *outs)
    return outs


SMALL_ROW = 1024
SMALL_GAINS = ("g1", "ga", "gs", "g2", "g3", "g4")
SMALL_LAYER_ROWS = 8 + 8 + 16 + 8


def _pack_small(grads):
    wide = lambda a: jnp.pad(a, ((0, 0), (0, 2 * SMALL_ROW - a.shape[1]))).reshape(-1, SMALL_ROW)
    row = lax.broadcasted_iota(jnp.int32, (8, SMALL_ROW), 0)
    parts = []
    for g in grads:
        singles = [g[k] for k in SMALL_GAINS] + [g["bs"][:, :SMALL_ROW],
                                                 jnp.pad(g["bs"][:, SMALL_ROW:], ((0, 0), (0, 2 * SMALL_ROW - XBC)))]
        first = sum(jnp.where(row == k, s, 0.0) for k, s in enumerate(singles))
        parts += [first, g["wa"], wide(g["ws"]), jnp.pad(g["par"], ((0, 0), (0, SMALL_ROW - CH)))]
    return jnp.concatenate(parts, axis=0)


def _unpack_small(packed, depth):
    rows = packed.reshape(depth, SMALL_LAYER_ROWS, SMALL_ROW)
    out = {k: rows[:, i] for i, k in enumerate(SMALL_GAINS)}
    out["bs"] = rows[:, 6:8].reshape(depth, 2 * SMALL_ROW)[:, :XBC]
    out["wa"] = rows[:, 8:11]
    out["ws"] = rows[:, 16:32].reshape(depth, 8, 2 * SMALL_ROW)[:, :4, :XBC]
    out["par"] = rows[:, 32:35, :CH]
    return out


def kernel(x, norm_mix_pre, w_in, conv_a_w, ssm_conv_w, ssm_conv_b, dt_bias, a_log, d_skip, conv_out_norm, ssm_out_norm, w_out, norm_mix_post, norm_mlp_pre, w_up, w_down, norm_mlp_post, loss_target, m_norm_mix_pre, m_w_in, m_conv_a_w, m_ssm_conv_w, m_ssm_conv_b, m_dt_bias, m_a_log, m_d_skip, m_conv_out_norm, m_ssm_out_norm, m_w_out, m_norm_mix_post, m_norm_mlp_pre, m_w_up, m_w_down, m_norm_mlp_post, v_norm_mix_pre, v_w_in, v_conv_a_w, v_ssm_conv_w, v_ssm_conv_b, v_dt_bias, v_a_log, v_d_skip, v_conv_out_norm, v_ssm_out_norm, v_w_out, v_norm_mix_post, v_norm_mlp_pre, v_w_up, v_w_down, v_norm_mlp_post):
    nb, seq, _ = x.shape
    t = nb * seq
    depth = w_in.shape[0]
    ncol = w_in.shape[2]
    chip = 2 * lax.axis_index("x") + lax.axis_index("y")

    taps = [conv_a_w, ssm_conv_w]
    taps_g = all_gather(taps, "chips", "gather_taps", slot_axis=1, own=False)
    wa_g, ws_g = [lax.dynamic_update_index_in_dim(g, s, chip, 1) for g, s in zip(taps_g, taps)]
    wa_full = jnp.transpose(wa_g, (0, 2, 1, 3)).reshape(depth, 3, D)
    ws_full = jnp.transpose(ws_g, (0, 2, 1, 3)).reshape(depth, 4, XBC)
    lane_pad = lambda a: jnp.pad(a, ((0, 0), (0, CH - a.shape[1])))
    par = jnp.stack([lane_pad(dt_bias), lane_pad(a_log), lane_pad(d_skip)], axis=1)
    par = jnp.pad(par, ((0, 0), (0, 5), (0, 0)))

    layer_shards = lambda l: [w_in[l].astype(bf16), w_out[l].astype(bf16), w_up[l].astype(bf16), w_down[l].astype(bf16)]
    issued = []

    def start(shards, name):
        zones = [lax.empty((N_CHIPS,) + s.shape, s.dtype) for s in shards]
        issued.append(chips_start(shards, zones, _weight_views(shards), name,
                                  after=issued[-1]["token"] if issued else taps_g[0]))
        return issued[-1]

    def finish(started, after, name):
        shards, zones = chips_wait(started, after, name)
        zones = weights_share(zones, "weights_share")
        return [lax.dynamic_update_index_in_dim(z, s, chip, 0) for z, s in zip(zones, shards)]

    def shaped(mats):
        wo_z, wu_z, wd_z = mats
        return wo_z.reshape(2 * D, D), wu_z, wd_z.reshape(DFF, D)

    first = layer_shards(0)
    travelling = {0: start(first[:1], "weights_start_0")}
    rest = start(first[1:], "weights_start_0_rest")
    for l in range(1, depth):
        travelling[l] = start(layer_shards(l), f"weights_start_{l}")

    def weights_of(l, x_in):
        mats = finish(travelling.pop(l), x_in, f"weights_wait_{l}")
        w = dict(win=assemble_columns(mats[0], PROJ, "assemble_w_in"), wa=jnp.pad(wa_full[l], ((0, 5), (0, 0))),
                 ws=jnp.pad(ws_full[l], ((0, 4), (0, 0))), bs=ssm_conv_b[l][None], par=par[l],
                 g1=norm_mix_pre[l][None], ga=conv_out_norm[l][None], gs=ssm_out_norm[l][None],
                 g2=norm_mix_post[l][None], g3=norm_mlp_pre[l][None], g4=norm_mlp_post[l][None])
        if l == 0:
            w["token"] = issued[-1]["token"]
            w["late"] = lambda after: dict(zip(("wo", "wu", "wd"), shaped(finish(rest, after, "weights_wait_0_rest"))))
        else:
            w.update(zip(("wo", "wu", "wd"), shaped(mats[1:])))
        return w

    core = lax.axis_index("c")
    grads_travelling = {}
    given_m = dict(win=m_w_in, wo=m_w_out, wu=m_w_up, wd=m_w_down)
    given_v = dict(win=v_w_in, wo=v_w_out, wu=v_w_up, wd=v_w_down)

    chip_major = dict(win=lambda a: a[None], wo=lambda a: a.reshape(N_CHIPS, 2 * D // N_CHIPS, D), wu=lambda a: a,
                      wd=lambda a: a.reshape(N_CHIPS, DFF // N_CHIPS, D))
    held = {}

    to_pair = {}

    def pair_sums_to_chips(l, keys, mats, received, after=None):
        sums = [sum_pair_half(m_, r_, core, "pair_sum", by_chip=(N_CHIPS, ncol) if k == "win" else None)
                for k, m_, r_ in zip(keys, mats, received)]
        zones = [lax.empty(s.shape, s.dtype) for s in sums]
        started = chips_start(sums, zones, _grad_views(sums), f"grads_start_{l}_{len(grads_travelling)}", after=after)
        grads_travelling[(l, keys[0])] = (keys, started)
        return started["token"]

    def grads_done(l, g, last):
        token = None
        if not last and l + 1 in to_pair:
            keys, started = to_pair.pop(l + 1)
            mats, received = chips_wait(started, g["wo"], f"grads_to_pair_wait_{l + 1}")
            token = pair_sums_to_chips(l + 1, keys, mats, received)
        if l > 0 and not last:
            held[l] = g
            return token
        g = {**held.pop(l, {}), **g}
        keys = [k for k in ("win", "wo", "wu", "wd") if k in g]
        mats = [chip_major[k](g[k]) for k in keys]
        if l > 0:
            half = [m_.shape[1] // 2 for m_ in mats]
            views = dict(src=lambda ref, a, c, to: ref.at[:, pl.ds(to * half[a], half[a])],
                         dst=lambda ref, a, c, sender: ref, rows=lambda a: mats[a].shape[0])
            zones = [lax.empty((m_.shape[0], h, m_.shape[2]), m_.dtype) for m_, h in zip(mats, half)]
            to_pair[l] = (keys, chips_start(mats, zones, views, f"grads_to_pair_start_{l}", pieces=1, group="pair"))
            return to_pair[l][1]["token"]
        return pair_sums_to_chips(l, keys, mats, pair_send_halves(mats, "grads_to_pair"), after=token)

    sse, dx, grads = local_step(x.reshape(t, D), loss_target.reshape(t, D), depth, weights_of, seq, grads_done)
    loss = lax.psum(0.5 / D * sse[0, 0], ("x", "y", "c"))

    packed = _pack_small(grads)
    small_travelling = chips_start([packed], [lax.empty((8,) + packed.shape, f32)], _whole_views([packed]),
                                   "small_start", group="all")

    big_w = dict(win=w_in, wo=w_out, wu=w_up, wd=w_down)
    acc = {k: lax.empty((depth, bw.shape[1] // 2, bw.shape[2]), f32) for k, bw in big_w.items()}
    for n, ((l, _), (keys, started)) in enumerate(grads_travelling.items()):
        sums, zones = chips_wait(started, small_travelling["token"], f"grads_wait_{l}_{n}")
        for k, s, z in zip(keys, sums, zones):
            acc[k] = chip_sum_into(acc[k], l, s, z, chip, "chip_sum")
    names = ("win", "wo", "wu", "wd")
    acc = [acc[k] for k in names]
    pair_views = dict(src=lambda ref, a, c, to: ref, dst=lambda ref, a, c, sender: ref, rows=lambda a: depth)
    to_sibling = chips_start(acc, [lax.empty(a.shape, f32) for a in acc], pair_views, "grads_from_pair_start",
                             pieces=depth, group="pair")
    own_done = {}
    for k, a in zip(names, to_sibling["sources"]):
        if big_w[k].shape[-1] % CH == 0:
            own_done[k] = adamw_half(big_w[k], a, given_m[k], given_v[k], core, "adamw_matrix",
                                     token=to_sibling["token"])

    (packed,), (small_all,) = chips_wait(small_travelling, own_done["wd"][1], "small_wait")
    small_all = lax.dynamic_update_index_in_dim(small_all, packed, 4 * lax.axis_index("x") + 2 * lax.axis_index("y") + core, 0)
    small = _unpack_small(sum_slots(small_all, f32, "small_sum", tb=8), depth)
    wa_cols, ws_cols = conv_a_w.shape[2], ssm_conv_w.shape[2]
    par_g = small["par"].reshape(depth, 3, CH)
    g_small = dict(
        norm_mix_pre=small["g1"], conv_out_norm=small["ga"], ssm_out_norm=small["gs"], norm_mix_post=small["g2"],
        norm_mlp_pre=small["g3"], norm_mlp_post=small["g4"], ssm_conv_b=small["bs"],
        conv_a_w=lax.dynamic_slice_in_dim(small["wa"].reshape(depth, 3, D), chip * wa_cols, wa_cols, axis=2),
        ssm_conv_w=lax.dynamic_slice_in_dim(small["ws"].reshape(depth, 4, XBC), chip * ws_cols, ws_cols, axis=2),
        dt_bias=par_g[:, 0, :NH], a_log=par_g[:, 1, :NH], d_skip=par_g[:, 2, :NH])

    given = dict(norm_mix_pre=(norm_mix_pre, m_norm_mix_pre, v_norm_mix_pre), w_in=(w_in, m_w_in, v_w_in),
                 conv_a_w=(conv_a_w, m_conv_a_w, v_conv_a_w), ssm_conv_w=(ssm_conv_w, m_ssm_conv_w, v_ssm_conv_w),
                 ssm_conv_b=(ssm_conv_b, m_ssm_conv_b, v_ssm_conv_b), dt_bias=(dt_bias, m_dt_bias, v_dt_bias),
                 a_log=(a_log, m_a_log, v_a_log), d_skip=(d_skip, m_d_skip, v_d_skip),
                 conv_out_norm=(conv_out_norm, m_conv_out_norm, v_conv_out_norm),
                 ssm_out_norm=(ssm_out_norm, m_ssm_out_norm, v_ssm_out_norm), w_out=(w_out, m_w_out, v_w_out),
                 norm_mix_post=(norm_mix_post, m_norm_mix_post, v_norm_mix_post),
                 norm_mlp_pre=(norm_mlp_pre, m_norm_mlp_pre, v_norm_mlp_pre), w_up=(w_up, m_w_up, v_w_up),
                 w_down=(w_down, m_w_down, v_w_down), norm_mlp_post=(norm_mlp_post, m_norm_mlp_post, v_norm_mlp_post))
    order = ["norm_mix_pre", "w_in", "conv_a_w", "ssm_conv_w", "ssm_conv_b", "dt_bias", "a_log", "d_skip",
             "conv_out_norm", "ssm_out_norm", "w_out", "norm_mix_post", "norm_mlp_pre", "w_up", "w_down",
             "norm_mlp_post"]
    short = dict(w_in="win", w_out="wo", w_up="wu", w_down="wd")
    results = {}
    for n in order:
        if n in short:
            continue
        wv, mv, vv = given[n]
        gv = g_small[n].reshape(wv.shape)
        two_d = lambda a: a.reshape(-1, a.shape[-1])
        results[n] = (gv,) + tuple(adamw(two_d(wv), two_d(gv), two_d(mv), two_d(vv), "adamw"))

    acc, from_sibling = chips_wait(to_sibling, results["norm_mlp_post"][1], "grads_from_pair_wait")
    for n, k in short.items():
        wv, mv, vv = given[n]
        own, recv = acc[names.index(k)], from_sibling[names.index(k)]
        if k in own_done:
            results[n] = adamw_half(wv, recv, mv, vv, 1 - core, "adamw_matrix", before=own_done[k])
        else:
            to_cols, to_rows = (lambda a: jnp.transpose(a, (2, 0, 1))), (lambda a: jnp.transpose(a, (1, 2, 0)))
            own_c, recv_c = to_cols(own), to_cols(recv)
            g_cols = jnp.where(core == 0, jnp.concatenate([own_c, recv_c], axis=2),
                               jnp.concatenate([recv_c, own_c], axis=2))
            results[n] = tuple(to_rows(o) for o in (g_cols,) + tuple(adamw_leading(to_cols(wv), g_cols, to_cols(mv),
                                                                                   to_cols(vv), "adamw_cols")))
    g_out, d_out, m_out, v_out = [], [], [], []
    for n in order:
        wv = given[n][0]
        gv, dlt, m2, v2 = results[n]
        g_out.append(gv.reshape(wv.shape))
        d_out.append(dlt.reshape(wv.shape))
        m_out.append(m2.reshape(wv.shape))
        v_out.append(v2.reshape(wv.shape))
    return (loss, dx.reshape(nb, seq, D), *g_out, *d_out, *m_out, *v_out)
```

```python
import functools

import jax
import jax.numpy as jnp
from jax import lax
from jax.experimental import pallas as pl
from jax.experimental.pallas import tpu as pltpu

f32, bf16 = jnp.float32, jnp.bfloat16

D = 1024
NH, HP = 16, 64
NG, NS = 2, 128
CH = 128
XBC = D + 2 * NG * NS
DFF = 4 * D
IN_COLS = 3 * D + D + XBC + NH
PROJ = 5760
COL_Z, COL_XBC, COL_DT = 3 * D, 4 * D, 4 * D + XBC
EPS = 1e-6
HALO = 8
HBLK = 16
VMEM_LIMIT = 56 * 2**20
MESH = pl.DeviceIdType.MESH

LR, B1, B2, AEPS, WD, STEP = 0.001, 0.9, 0.999, 1e-08, 0.01, 10


def _cparams(n_axes):
    return pltpu.CompilerParams(dimension_semantics=("arbitrary",) * n_axes, vmem_limit_bytes=VMEM_LIMIT)


def _sds(shape, dtype):
    return jax.ShapeDtypeStruct(tuple(shape), dtype)


def _token_spec(token):
    return [] if token is None else [pl.BlockSpec(memory_space=pl.ANY)]


def _token_arg(token):
    return [] if token is None else [token]


def _resident(shape):
    return pl.BlockSpec(shape, lambda i: (0,) * len(shape), pipeline_mode=pl.Buffered(1))


def _rms_fwd(x, g):
    r = lax.rsqrt(jnp.mean(x * x, axis=-1, keepdims=True) + EPS)
    return x * r * g


def _rms_bwd(x, g, dy):
    r = lax.rsqrt(jnp.mean(x * x, axis=-1, keepdims=True) + EPS)
    xh = x * r
    gdy = dy * g
    dx = r * (gdy - xh * jnp.mean(xh * gdy, axis=-1, keepdims=True))
    return dx, dy * xh


def _accum(ref, part, first):
    @pl.when(first)
    def _():
        ref[...] = part

    @pl.when(jnp.logical_not(first))
    def _():
        ref[...] += part


def _dot_nt(a, b):
    return lax.dot_general(a, b, (((1,), (1,)), ((), ())), preferred_element_type=f32)


def _dot_tn(a, b):
    return lax.dot_general(a, b, (((0,), (0,)), ((), ())), preferred_element_type=f32)


def _dot(a, b):
    return jnp.dot(a, b, preferred_element_type=f32)


def _split_dot(x, e_bf, n_split, nt=False):
    acc = None
    rem = x
    for s in range(n_split):
        hi = rem.astype(bf16)
        term = _dot_nt(hi, e_bf) if nt else _dot(hi, e_bf)
        acc = term if acc is None else acc + term
        if s + 1 < n_split:
            rem = rem - hi.astype(f32)
    return acc


def _sigmoid(x):
    return 0.5 * jnp.tanh(0.5 * x) + 0.5


def norm_matmul(x, g, w, tm, tn, out_dtype, name, token=None):
    t, n = x.shape[0], w.shape[1]
    w_spec = pl.BlockSpec((D, tn), lambda i, j: (0, j))

    def body(x_ref, g_ref, w_ref, *rest):
        o_ref, h_ref = rest[-2:]

        @pl.when(pl.program_id(1) == 0)
        def _():
            h_ref[...] = _rms_fwd(x_ref[...], g_ref[...]).astype(bf16)

        o_ref[...] = _dot(h_ref[...], w_ref[...]).astype(out_dtype)

    return pl.pallas_call(
        body, name=name, grid=(t // tm, n // tn),
        in_specs=[pl.BlockSpec((tm, D), lambda i, j: (i, 0)), pl.BlockSpec((1, D), lambda i, j: (0, 0)), w_spec]
        + _token_spec(token),
        out_specs=[pl.BlockSpec((tm, tn), lambda i, j: (i, j)), pl.BlockSpec((tm, D), lambda i, j: (i, 0))],
        out_shape=[_sds((t, n), out_dtype), _sds((t, D), bf16)],
        compiler_params=_cparams(2))(x, g, w, *_token_arg(token))


def matmul_postnorm(a, w, xres, g, tm, name):
    t, k = a.shape

    def body(a_ref, w_ref, xr_ref, g_ref, y_ref, xo_ref):
        y = _dot(a_ref[...], w_ref[...])
        y_ref[...] = y.astype(bf16)
        xo_ref[...] = xr_ref[...] + _rms_fwd(y, g_ref[...])

    return pl.pallas_call(
        body, name=name, grid=(t // tm,),
        in_specs=[pl.BlockSpec((tm, k), lambda i: (i, 0)), _resident(w.shape),
                  pl.BlockSpec((tm, D), lambda i: (i, 0)), pl.BlockSpec((1, D), lambda i: (0, 0))],
        out_specs=[pl.BlockSpec((tm, D), lambda i: (i, 0)), pl.BlockSpec((tm, D), lambda i: (i, 0))],
        out_shape=[_sds((t, D), bf16), _sds((t, D), f32)],
        compiler_params=_cparams(1))(a, w, xres, g)


def postnorm_bwd_matmul(y, g, dxo, w, tm, tn, name):
    t, n = y.shape[0], w.shape[0]

    def body(y_ref, g_ref, dxo_ref, w_ref, dy_ref, dg_ref, da_ref):
        i, j = pl.program_id(0), pl.program_id(1)

        @pl.when(j == 0)
        def _():
            dx, dgc = _rms_bwd(y_ref[...].astype(f32), g_ref[...], dxo_ref[...])
            dy_ref[...] = dx.astype(bf16)
            _accum(dg_ref, jnp.sum(dgc, axis=0, keepdims=True), i == 0)

        da_ref[...] = _dot_nt(dy_ref[...], w_ref[...]).astype(bf16)

    return pl.pallas_call(
        body, name=name, grid=(t // tm, n // tn),
        in_specs=[pl.BlockSpec((tm, D), lambda i, j: (i, 0)), pl.BlockSpec((1, D), lambda i, j: (0, 0)),
                  pl.BlockSpec((tm, D), lambda i, j: (i, 0)), pl.BlockSpec((tn, D), lambda i, j: (j, 0))],
        out_specs=[pl.BlockSpec((tm, D), lambda i, j: (i, 0)), pl.BlockSpec((1, D), lambda i, j: (0, 0)),
                   pl.BlockSpec((tm, tn), lambda i, j: (i, j))],
        out_shape=[_sds((t, D), bf16), _sds((1, D), f32), _sds((t, n), bf16)],
        compiler_params=_cparams(2))(y, g, dxo, w)


def matmul_prenorm_bwd(da, w, x, g, dxo, tm, name, token=None):
    t, k = da.shape

    def body(da_ref, w_ref, x_ref, g_ref, dxo_ref, *rest):
        dx_ref, dg_ref = rest[-2:]
        dh = _dot_nt(da_ref[...], w_ref[...])
        dxn, dgc = _rms_bwd(x_ref[...], g_ref[...], dh)
        dx_ref[...] = dxo_ref[...] + dxn
        _accum(dg_ref, jnp.sum(dgc, axis=0, keepdims=True), pl.program_id(0) == 0)

    return pl.pallas_call(
        body, name=name, grid=(t // tm,),
        in_specs=[pl.BlockSpec((tm, k), lambda i: (i, 0)), _resident(w.shape),
                  pl.BlockSpec((tm, D), lambda i: (i, 0)), pl.BlockSpec((1, D), lambda i: (0, 0)),
                  pl.BlockSpec((tm, D), lambda i: (i, 0))] + _token_spec(token),
        out_specs=[pl.BlockSpec((tm, D), lambda i: (i, 0)), pl.BlockSpec((1, D), lambda i: (0, 0))],
        out_shape=[_sds((t, D), f32), _sds((1, D), f32)],
        compiler_params=_cparams(1))(da, w, x, g, dxo, *_token_arg(token))


def mlp_fwd(x, g_pre, wu, wd, g_post, tm, name):
    t = x.shape[0]
    nq, _, fc = wu.shape

    def body(x_ref, gp_ref, wu_ref, wd_ref, gq_ref, fp_ref, h_ref, o_ref, xo_ref):
        xv = x_ref[...]
        h = _rms_fwd(xv, gp_ref[...]).astype(bf16)
        h_ref[...] = h
        o = None
        for q in range(nq):
            fq = _dot(h, wu_ref[q])
            fp_ref[:, q * fc:(q + 1) * fc] = fq.astype(bf16)
            r = jnp.maximum(fq, 0.0)
            part = _dot((r * r).astype(bf16), wd_ref[q * fc:(q + 1) * fc, :])
            o = part if o is None else o + part
        o_ref[...] = o.astype(bf16)
        xo_ref[...] = xv + _rms_fwd(o, gq_ref[...])

    row = lambda c: pl.BlockSpec((tm, c), lambda i: (i, 0))
    vec = pl.BlockSpec((1, D), lambda i: (0, 0))
    return pl.pallas_call(
        body, name=name, grid=(t // tm,),
        in_specs=[row(D), vec, _resident(wu.shape), _resident(wd.shape), vec],
        out_specs=[row(nq * fc), row(D), row(D), row(D)],
        out_shape=[_sds((t, nq * fc), bf16), _sds((t, D), bf16), _sds((t, D), bf16), _sds((t, D), f32)],
        compiler_params=_cparams(1))(x, g_pre, wu, wd, g_post)


def mlp_bwd(o, g_post, dxo, wd, fp, wu, x, g_pre, tm, name):
    t = x.shape[0]
    nq, _, fc = wu.shape

    def body(o_ref, gq_ref, dxo_ref, wd_ref, fp_ref, wu_ref, x_ref, gp_ref, do_ref, dfp_ref, dx_ref, dgq_ref, dgp_ref):
        i = pl.program_id(0)
        dxo_v = dxo_ref[...]
        do, dgq = _rms_bwd(o_ref[...].astype(f32), gq_ref[...], dxo_v)
        do_b = do.astype(bf16)
        do_ref[...] = do_b
        dh = None
        for q in range(nq):
            cols = slice(q * fc, (q + 1) * fc)
            dq = _dot_nt(do_b, wd_ref[cols, :]) * (2.0 * jnp.maximum(fp_ref[:, cols].astype(f32), 0.0))
            dq_b = dq.astype(bf16)
            dfp_ref[:, cols] = dq_b
            part = _dot_nt(dq_b, wu_ref[q])
            dh = part if dh is None else dh + part
        dxn, dgp = _rms_bwd(x_ref[...], gp_ref[...], dh)
        dx_ref[...] = dxo_v + dxn
        _accum(dgq_ref, jnp.sum(dgq, axis=0, keepdims=True), i == 0)
        _accum(dgp_ref, jnp.sum(dgp, axis=0, keepdims=True), i == 0)

    row = lambda c: pl.BlockSpec((tm, c), lambda i: (i, 0))
    vec = pl.BlockSpec((1, D), lambda i: (0, 0))
    return pl.pallas_call(
        body, name=name, grid=(t // tm,),
        in_specs=[row(D), vec, row(D), _resident(wd.shape), row(nq * fc), _resident(wu.shape), row(D), vec],
        out_specs=[row(D), row(nq * fc), row(D), vec, vec],
        out_shape=[_sds((t, D), bf16), _sds((t, nq * fc), bf16), _sds((t, D), f32), _sds((1, D), f32), _sds((1, D), f32)],
        compiler_params=_cparams(1))(o, g_post, dxo, wd, fp, wu, x, g_pre)


def matmul_tn(a, b, tm, tn, relu2, name, col_blocks=False):
    t, m = a.shape
    n = b.shape[1]
    if col_blocks:
        out_spec, out_shape = pl.BlockSpec((None, tm, tn), lambda i, j: (j, i, 0)), _sds((n // tn, m, tn), bf16)
    else:
        out_spec, out_shape = pl.BlockSpec((tm, tn), lambda i, j: (i, j)), _sds((m, n), bf16)

    def body(a_ref, b_ref, o_ref, at_ref):
        @pl.when(pl.program_id(1) == 0)
        def _():
            av = a_ref[...]
            if relu2:
                af = jnp.maximum(av.astype(f32), 0.0)
                av = (af * af).astype(bf16)
            at_ref[...] = av.T

        o_ref[...] = _dot(at_ref[...], b_ref[...]).astype(bf16)

    return pl.pallas_call(
        body, name=name, grid=(m // tm, n // tn),
        in_specs=[pl.BlockSpec((t, tm), lambda i, j: (0, i)), pl.BlockSpec((t, tn), lambda i, j: (0, j))],
        out_specs=out_spec, out_shape=out_shape,
        scratch_shapes=[pltpu.VMEM((tm, t), bf16)],
        compiler_params=_cparams(2))(a, b)


ROWS_A = 16
ROWS_B = 32
UNROLL = 4


def _past(win, s):
    return (win if s == 0 else pltpu.roll(win, s, 0))[HALO:]


def _future(win, s):
    n = win.shape[0]
    return (win if s == 0 else pltpu.roll(win, n - s, 0))[:n - HALO]


def _fold8(v):
    return v.reshape(v.shape[0] // 8, 8, v.shape[1]).sum(axis=0)


def _last8(ref):
    return ref[...].astype(f32)[HBLK - HALO:]


def _first8(ref):
    return ref[...].astype(f32)[:HALO]


def _rd(ref, rows):
    return ref[rows, :].astype(f32)


def _halo_prev(tb, col):
    return lambda i: (jnp.maximum(i * (tb // HBLK) - 1, 0), col)


def _halo_next(tb, col, t):
    return lambda i: (jnp.minimum((i + 1) * (tb // HBLK), t // HBLK - 1), col)


def group_a_fwd(proj, wa, g, seq, tb, name):
    t = proj.shape[0]
    bps = seq // tb

    def body(xa_ref, ca_ref, ba_ref, xah_ref, cah_ref, wa_ref, g_ref, o_ref, u_scr):
        first = (pl.program_id(0) % bps) == 0
        u_scr[0:HALO, :] = jnp.where(first, 0.0, _last8(cah_ref) * _last8(xah_ref))
        w, gv = wa_ref[...], g_ref[...]

        def chunk(i, carry):
            r = pl.multiple_of(i * ROWS_A, ROWS_A)
            rows = pl.ds(r, ROWS_A)
            u_scr[pl.ds(pl.multiple_of(HALO + r, HALO), ROWS_A), :] = _rd(ca_ref, rows) * _rd(xa_ref, rows)
            win = u_scr[pl.ds(r, ROWS_A + HALO), :]
            cv = w[2:3] * _past(win, 0) + w[1:2] * _past(win, 1) + w[0:1] * _past(win, 2)
            o_ref[rows, :] = _rms_fwd(_rd(ba_ref, rows) * cv, gv).astype(bf16)
            return carry

        lax.fori_loop(0, tb // ROWS_A, chunk, 0, unroll=UNROLL)

    blk = lambda c: pl.BlockSpec((tb, D), lambda i: (i, c))
    return pl.pallas_call(
        body, name=name, grid=(t // tb,),
        in_specs=[blk(0), blk(1), blk(2),
                  pl.BlockSpec((HBLK, D), _halo_prev(tb, 0)), pl.BlockSpec((HBLK, D), _halo_prev(tb, 1)),
                  pl.BlockSpec((8, D), lambda i: (0, 0)), pl.BlockSpec((1, D), lambda i: (0, 0))],
        out_specs=pl.BlockSpec((tb, D), lambda i: (i, 0)),
        out_shape=_sds((t, 2 * D), bf16),
        scratch_shapes=[pltpu.VMEM((tb + HALO, D), f32)],
        compiler_params=_cparams(1))(proj, proj, proj, proj, proj, wa, g)


def group_a_bwd(proj, dcat, wa, g, seq, tb, name, token=None):
    t = proj.shape[0]
    bps = seq // tb

    def body(xa_ref, ca_ref, ba_ref, dy_ref, xap_ref, cap_ref, xan_ref, can_ref, ban_ref, dyn_ref, wa_ref, g_ref,
             *rest):
        dp_ref, dwa_ref, dg_ref, u_scr, d_scr, acc_scr = rest[-6:]
        i = pl.program_id(0)
        first = (i % bps) == 0
        last = (i % bps) == bps - 1
        w = wa_ref[...]
        gv = g_ref[...]
        u_scr[0:HALO, :] = jnp.where(first, 0.0, _last8(cap_ref) * _last8(xap_ref))
        u_scr[HALO + tb:2 * HALO + tb, :] = _first8(can_ref) * _first8(xan_ref)
        acc_scr[...] = jnp.zeros_like(acc_scr)

        def forward_part(n, carry):
            r = pl.multiple_of(n * ROWS_A, ROWS_A)
            rows = pl.ds(r, ROWS_A)
            ba = _rd(ba_ref, rows)
            u_scr[pl.ds(pl.multiple_of(HALO + r, HALO), ROWS_A), :] = _rd(ca_ref, rows) * _rd(xa_ref, rows)
            win = u_scr[pl.ds(r, ROWS_A + HALO), :]
            u = [_past(win, s) for s in range(3)]
            cv = w[2:3] * u[0] + w[1:2] * u[1] + w[0:1] * u[2]
            dya, dgc = _rms_bwd(ba * cv, gv, _rd(dy_ref, rows))
            dcv = dya * ba
            d_scr[rows, :] = dcv
            dp_ref[rows, 2 * D:3 * D] = (dya * cv).astype(bf16)
            acc_scr[0:8, :] += _fold8(dgc)
            for k in range(3):
                acc_scr[8 + 8 * k:16 + 8 * k, :] += _fold8(dcv * u[2 - k])
            return carry

        lax.fori_loop(0, tb // ROWS_A, forward_part, 0, unroll=UNROLL)

        start = HALO + tb
        cvn = (w[2:3] * u_scr[pl.ds(start, HALO), :] + w[1:2] * u_scr[pl.ds(start - 1, HALO), :]
               + w[0:1] * u_scr[pl.ds(start - 2, HALO), :])
        ban = _first8(ban_ref)
        dyan, _ = _rms_bwd(ban * cvn, gv, _first8(dyn_ref))
        d_scr[tb:tb + HALO, :] = jnp.where(last, 0.0, dyan * ban)

        def backward_part(n, carry):
            r = pl.multiple_of(n * ROWS_A, ROWS_A)
            rows = pl.ds(r, ROWS_A)
            win = d_scr[pl.ds(r, ROWS_A + HALO), :]
            du = w[2:3] * _future(win, 0) + w[1:2] * _future(win, 1) + w[0:1] * _future(win, 2)
            dp_ref[rows, 0:D] = (du * _rd(ca_ref, rows)).astype(bf16)
            dp_ref[rows, D:2 * D] = (du * _rd(xa_ref, rows)).astype(bf16)
            return carry

        lax.fori_loop(0, tb // ROWS_A, backward_part, 0, unroll=UNROLL)

        row = lax.broadcasted_iota(jnp.int32, (8, D), 0)
        dw = jnp.zeros((8, D), f32)
        for k in range(3):
            dw = jnp.where(row == k, jnp.sum(acc_scr[8 + 8 * k:16 + 8 * k, :], axis=0, keepdims=True), dw)
        _accum(dwa_ref, dw, i == 0)
        _accum(dg_ref, jnp.sum(acc_scr[0:8, :], axis=0, keepdims=True), i == 0)

    blk = lambda c: pl.BlockSpec((tb, D), lambda i: (i, c))
    prv = lambda c: pl.BlockSpec((HBLK, D), _halo_prev(tb, c))
    nxt = lambda c: pl.BlockSpec((HBLK, D), _halo_next(tb, c, t))
    return pl.pallas_call(
        body, name=name, grid=(t // tb,),
        in_specs=[blk(0), blk(1), blk(2), blk(0), prv(0), prv(1), nxt(0), nxt(1), nxt(2), nxt(0),
                  pl.BlockSpec((8, D), lambda i: (0, 0)), pl.BlockSpec((1, D), lambda i: (0, 0))] + _token_spec(token),
        out_specs=[pl.BlockSpec((tb, 3 * D), lambda i: (i, 0)), pl.BlockSpec((8, D), lambda i: (0, 0)),
                   pl.BlockSpec((1, D), lambda i: (0, 0))],
        out_shape=[_sds((t, PROJ), bf16), _sds((8, D), f32), _sds((1, D), f32)],
        scratch_shapes=[pltpu.VMEM((tb + 2 * HALO, D), f32), pltpu.VMEM((tb + HALO, D), f32), pltpu.VMEM((32, D), f32)],
        compiler_params=_cparams(1))(proj, proj, proj, dcat, proj, proj, proj, proj, proj, dcat, wa, g,
                                     *_token_arg(token))


CB = 512
XBC_BLK0 = COL_XBC // CB


def conv_b_fwd(proj, ws, bs, seq, tb, name):
    t = proj.shape[0]
    bps = seq // tb

    def body(x_ref, xp_ref, w_ref, b_ref, o_ref, da_ref, x_scr):
        first = (pl.program_id(1) % bps) == 0
        x_scr[0:HALO, :] = jnp.where(first, 0.0, _last8(xp_ref))
        w, bias = w_ref[...], b_ref[...]

        def chunk(n, carry):
            r = pl.multiple_of(n * ROWS_B, ROWS_B)
            rows = pl.ds(r, ROWS_B)
            x_scr[pl.ds(pl.multiple_of(HALO + r, HALO), ROWS_B), :] = _rd(x_ref, rows)
            win = x_scr[pl.ds(r, ROWS_B + HALO), :]
            xc = bias + w[3:4] * _past(win, 0)
            for k in range(3):
                xc = xc + w[k:k + 1] * _past(win, 3 - k)
            sg = _sigmoid(xc)
            o_ref[rows, :] = xc * sg
            da_ref[rows, :] = (sg * (1.0 + xc * (1.0 - sg))).astype(bf16)
            return carry

        lax.fori_loop(0, tb // ROWS_B, chunk, 0, unroll=UNROLL)

    return pl.pallas_call(
        body, name=name, grid=(XBC // CB, t // tb),
        in_specs=[pl.BlockSpec((tb, CB), lambda j, i: (i, XBC_BLK0 + j)),
                  pl.BlockSpec((HBLK, CB), lambda j, i: (jnp.maximum(i * (tb // HBLK) - 1, 0), XBC_BLK0 + j)),
                  pl.BlockSpec((8, CB), lambda j, i: (0, j)), pl.BlockSpec((1, CB), lambda j, i: (0, j))],
        out_specs=[pl.BlockSpec((tb, CB), lambda j, i: (i, j)), pl.BlockSpec((tb, CB), lambda j, i: (i, j))],
        out_shape=[_sds((t, XBC), f32), _sds((t, XBC), bf16)],
        scratch_shapes=[pltpu.VMEM((tb + HALO, CB), f32)],
        compiler_params=_cparams(2))(proj, proj, ws, bs)


def conv_b_bwd(proj, dxs, dact, ws, dproj, seq, tb, name):
    t = proj.shape[0]
    bps = seq // tb

    def body(x_ref, xp_ref, d_ref, dn_ref, a_ref, an_ref, w_ref, dproj_ref, dx_ref, dw_ref, db_ref, x_scr, d_scr,
             acc_scr):
        i = pl.program_id(1)
        first = (i % bps) == 0
        last = (i % bps) == bps - 1
        w = w_ref[...]
        x_scr[0:HALO, :] = jnp.where(first, 0.0, _last8(xp_ref))
        acc_scr[...] = jnp.zeros_like(acc_scr)

        def forward_part(n, carry):
            r = pl.multiple_of(n * ROWS_B, ROWS_B)
            rows = pl.ds(r, ROWS_B)
            x_scr[pl.ds(pl.multiple_of(HALO + r, HALO), ROWS_B), :] = _rd(x_ref, rows)
            win = x_scr[pl.ds(r, ROWS_B + HALO), :]
            dxc = _rd(d_ref, rows) * _rd(a_ref, rows)
            d_scr[rows, :] = dxc
            acc_scr[0:8, :] += _fold8(dxc)
            for k in range(4):
                acc_scr[8 + 8 * k:16 + 8 * k, :] += _fold8(dxc * _past(win, 3 - k))
            return carry

        lax.fori_loop(0, tb // ROWS_B, forward_part, 0, unroll=UNROLL)
        d_scr[tb:tb + HALO, :] = jnp.where(last, 0.0, _first8(dn_ref) * _first8(an_ref))

        def backward_part(n, carry):
            r = pl.multiple_of(n * ROWS_B, ROWS_B)
            win = d_scr[pl.ds(r, ROWS_B + HALO), :]
            dx = w[3:4] * _future(win, 0)
            for k in range(3):
                dx = dx + w[k:k + 1] * _future(win, 3 - k)
            dx_ref[pl.ds(r, ROWS_B), :] = dx.astype(bf16)
            return carry

        lax.fori_loop(0, tb // ROWS_B, backward_part, 0, unroll=UNROLL)

        row = lax.broadcasted_iota(jnp.int32, (8, CB), 0)
        dw = jnp.zeros((8, CB), f32)
        for k in range(4):
            dw = jnp.where(row == k, jnp.sum(acc_scr[8 + 8 * k:16 + 8 * k, :], axis=0, keepdims=True), dw)
        _accum(dw_ref, dw, i == 0)
        _accum(db_ref, jnp.sum(acc_scr[0:8, :], axis=0, keepdims=True), i == 0)

    nh = t // HBLK
    nxt = pl.BlockSpec((HBLK, CB), lambda j, i: (jnp.minimum((i + 1) * (tb // HBLK), nh - 1), j))
    cur = pl.BlockSpec((tb, CB), lambda j, i: (i, j))
    return pl.pallas_call(
        body, name=name, grid=(XBC // CB, t // tb),
        in_specs=[pl.BlockSpec((tb, CB), lambda j, i: (i, XBC_BLK0 + j)),
                  pl.BlockSpec((HBLK, CB), lambda j, i: (jnp.maximum(i * (tb // HBLK) - 1, 0), XBC_BLK0 + j)),
                  cur, nxt, cur, nxt, pl.BlockSpec((8, CB), lambda j, i: (0, j)), pl.BlockSpec(memory_space=pl.ANY)],
        out_specs=[pl.BlockSpec((tb, CB), lambda j, i: (i, XBC_BLK0 + j)), pl.BlockSpec((8, CB), lambda j, i: (0, j)),
                   pl.BlockSpec((1, CB), lambda j, i: (0, j))],
        out_shape=[_sds((t, PROJ), bf16), _sds((8, XBC), f32), _sds((1, XBC), f32)],
        input_output_aliases={7: 0},
        scratch_shapes=[pltpu.VMEM((tb + HALO, CB), f32), pltpu.VMEM((tb + HALO, CB), f32), pltpu.VMEM((40, CB), f32)],
        compiler_params=_cparams(2))(proj, proj, dxs, dxs, dact, dact, ws, dproj)


def place_columns(buf, part, col_block, tb, name):
    t, wdt = part.shape

    def body(p_ref, buf_ref, o_ref):
        o_ref[...] = p_ref[...]

    return pl.pallas_call(
        body, name=name, grid=(t // tb,),
        in_specs=[pl.BlockSpec((tb, wdt), lambda i: (i, 0)), pl.BlockSpec(memory_space=pl.ANY)],
        out_specs=pl.BlockSpec((tb, wdt), lambda i: (i, col_block)), out_shape=_sds(buf.shape, buf.dtype),
        input_output_aliases={1: 0}, compiler_params=_cparams(1))(part, buf)


GW = D // NG
EXPAND_TERMS = 2
REDUCE_TERMS = 1


def _ssd_consts():
    head_of_lane = jnp.arange(D) // HP
    expand = (jnp.arange(CH)[:, None] == head_of_lane[None, :]).astype(bf16)
    tri = (jnp.arange(CH)[:, None] >= jnp.arange(CH)[None, :]).astype(f32)
    return expand, tri


def _ssd_common(par_ref, dtr_ref, e_ref, tri_ref):
    par = par_ref[...]
    dtb, alog, dsk = par[0:1], par[1:2], par[2:3]
    lane = lax.broadcasted_iota(jnp.int32, (CH, CH), 1)
    a = -jnp.exp(alog)
    dtr = dtr_ref[...].astype(f32) + dtb
    sp = jnp.maximum(dtr, 0.0) + jnp.log(1.0 + jnp.exp(-jnp.abs(dtr)))
    dt = jnp.where(lane < NH, sp, 0.0)
    cs = jnp.dot(tri_ref[...], dt * a, precision=lax.Precision.HIGHEST, preferred_element_type=f32)
    cs_last = cs[CH - 1:CH, :]
    dte = jnp.exp(cs_last - cs)
    ecs = jnp.exp(cs)
    ecl = jnp.exp(cs_last)
    e = e_ref[...]
    row8 = lax.broadcasted_iota(jnp.int32, (8, CH), 0)
    r8 = _split_dot(jnp.where(row8 == 0, ecl, jnp.where(row8 == 1, dsk, 0.0)), e, 3)
    return dict(a=a, dtr=dtr, dt=dt, cs=cs, cst=cs.T, dte=dte, ecs=ecs, ecl=ecl, e=e, lane=lane,
                dt_x=_split_dot(dt, e, EXPAND_TERMS), dte_x=_split_dot(dte, e, EXPAND_TERMS),
                ecs_x=_split_dot(ecs, e, EXPAND_TERMS),
                ecl_x=r8[0:1], dsk_x=r8[1:2])


def _decay_matrix(c, h):
    li = lax.broadcasted_iota(jnp.int32, (CH, CH), 0)
    seg = c["cs"][:, h:h + 1] - c["cst"][h:h + 1, :]
    return jnp.exp(jnp.where(li >= c["lane"], seg, -jnp.inf))


def _gate_norm_fwd(y, z, gs):
    zg = z * _sigmoid(z)
    yg = y * zg
    return jnp.concatenate([_rms_fwd(yg[:, k * GW:(k + 1) * GW], gs[:, k * GW:(k + 1) * GW]) for k in range(NG)], axis=1)


def ssd_fwd(xbcs, proj, par, gs, cat, seq, name):
    t = xbcs.shape[0]
    nc = seq // CH
    expand, tri = _ssd_consts()

    def body(xs_ref, b_ref, c_ref, dtr_ref, z_ref, par_ref, e_ref, tri_ref, gs_ref, cat_ref, yn_ref, y_ref, st_ref,
             p_scr, yd_scr):
        @pl.when(pl.program_id(0) % nc == 0)
        def _():
            p_scr[...] = jnp.zeros_like(p_scr)

        c = _ssd_common(par_ref, dtr_ref, e_ref, tri_ref)
        xs = xs_ref[...]
        xdt = xs * c["dt_x"]
        xdt_b = xdt.astype(bf16)
        xdte_b = (xdt * c["dte_x"]).astype(bf16)
        p = p_scr[...]
        st_ref[0] = p
        p_b = p.astype(bf16)
        lo = c["lane"] < HP
        for g in range(NG):
            bg = b_ref[:, g * NS:(g + 1) * NS].astype(bf16)
            cg = c_ref[:, g * NS:(g + 1) * NS].astype(bf16)
            gmat = _dot_nt(cg, bg)
            for q in range(GW // CH):
                col = g * GW + q * CH
                xp = xdt_b[:, col:col + CH]
                h0 = col // HP
                m0 = (gmat * _decay_matrix(c, h0)).astype(bf16)
                m1 = (gmat * _decay_matrix(c, h0 + 1)).astype(bf16)
                stacked = jnp.concatenate([jnp.where(lo, xp, jnp.zeros_like(xp)),
                                           jnp.where(lo, jnp.zeros_like(xp), xp)], axis=0)
                yd_scr[:, col:col + CH] = _dot(jnp.concatenate([m0, m1], axis=1), stacked)
            gsl = slice(g * GW, (g + 1) * GW)
            yoff = _dot(cg, p_b[:, gsl]) * c["ecs_x"][:, gsl]
            yd_scr[:, gsl] = yd_scr[:, gsl] + yoff
            p_scr[:, gsl] = p[:, gsl] * c["ecl_x"][:, gsl] + _dot_tn(bg, xdte_b[:, gsl])
        y = yd_scr[...] + c["dsk_x"] * xs
        y_ref[...] = y
        yn_ref[...] = _gate_norm_fwd(y, z_ref[...].astype(f32), gs_ref[...]).astype(bf16)

    nb = t // CH
    return pl.pallas_call(
        body, name=name, grid=(nb,),
        in_specs=[pl.BlockSpec((CH, D), lambda i: (i, 0)),
                  pl.BlockSpec((CH, NG * NS), lambda i: (i, D // (NG * NS))),
                  pl.BlockSpec((CH, NG * NS), lambda i: (i, D // (NG * NS) + 1)),
                  pl.BlockSpec((CH, CH), lambda i: (i, COL_DT // CH)),
                  pl.BlockSpec((CH, D), lambda i: (i, COL_Z // D)),
                  pl.BlockSpec((8, CH), lambda i: (0, 0)), pl.BlockSpec((CH, D), lambda i: (0, 0)),
                  pl.BlockSpec((CH, CH), lambda i: (0, 0)), pl.BlockSpec((1, D), lambda i: (0, 0)),
                  pl.BlockSpec(memory_space=pl.ANY)],
        out_specs=[pl.BlockSpec((CH, D), lambda i: (i, 1)), pl.BlockSpec((CH, D), lambda i: (i, 0)),
                   pl.BlockSpec((1, NS, D), lambda i: (i, 0, 0))],
        out_shape=[_sds((t, 2 * D), bf16), _sds((t, D), f32), _sds((nb, NS, D), f32)],
        input_output_aliases={9: 0},
        scratch_shapes=[pltpu.VMEM((NS, D), f32), pltpu.VMEM((CH, D), f32)],
        compiler_params=_cparams(1))(xbcs, xbcs, xbcs, proj, proj, par, expand, tri, gs, cat)


def ssd_bwd(xbcs, proj, ypre, states, dcat, par, gs, dproj, seq, name):
    t = xbcs.shape[0]
    nc = seq // CH
    expand, tri = _ssd_consts()

    def body(xs_ref, b_ref, c_ref, dtr_ref, z_ref, y_ref, st_ref, dyn_ref, par_ref, e_ref, tri_ref, gs_ref, dproj_ref,
             dx_ref, dz_ref, ddt_ref, dpar_ref, dgs_ref, dp_scr, dxdt_scr):
        i = pl.program_id(0)

        @pl.when(i % nc == 0)
        def _():
            dp_scr[...] = jnp.zeros_like(dp_scr)

        c = _ssd_common(par_ref, dtr_ref, e_ref, tri_ref)
        e = c["e"]
        lane = c["lane"]
        sub = lax.broadcasted_iota(jnp.int32, (CH, CH), 0)
        xs = xs_ref[...]
        xdt = xs * c["dt_x"]
        xdt_b = xdt.astype(bf16)
        xdte_b = (xdt * c["dte_x"]).astype(bf16)
        p = st_ref[0]
        p_b = p.astype(bf16)
        dpn = dp_scr[...]
        dpn_b = dpn.astype(bf16)

        y, z, gs_v = y_ref[...], z_ref[...].astype(f32), gs_ref[...]
        zs = _sigmoid(z)
        zg = z * zs
        yg = y * zg
        parts, gparts = [], []
        for k in range(NG):
            sl = slice(k * GW, (k + 1) * GW)
            dxk, dgk = _rms_bwd(yg[:, sl], gs_v[:, sl], dyn_ref[:, sl].astype(f32))
            parts.append(dxk)
            gparts.append(dgk)
        dyg = jnp.concatenate(parts, axis=1)
        dgs_rows = jnp.concatenate(gparts, axis=1)
        dy = dyg * zg
        dz_ref[...] = (dyg * y * (zs * (1.0 + z * (1.0 - zs)))).astype(bf16)
        dy_b = dy.astype(bf16)
        dq_b = (dy * c["ecs_x"]).astype(bf16)

        lo = lane < HP
        dcs = jnp.zeros((CH, CH), f32)
        dcst = jnp.zeros((CH, CH), f32)
        for g in range(NG):
            gsl = slice(g * GW, (g + 1) * GW)
            bg = b_ref[:, g * NS:(g + 1) * NS].astype(bf16)
            cg = c_ref[:, g * NS:(g + 1) * NS].astype(bf16)
            gmat = _dot_nt(cg, bg)
            dgm = jnp.zeros((CH, CH), f32)
            for q in range(GW // CH):
                col = g * GW + q * CH
                xp = xdt_b[:, col:col + CH]
                dyp = dy_b[:, col:col + CH]
                zero = jnp.zeros_like(dyp)
                xp2 = jnp.concatenate([jnp.where(lo, xp, zero), jnp.where(lo, zero, xp)], axis=0)
                dy2 = jnp.concatenate([jnp.where(lo, dyp, zero), jnp.where(lo, zero, dyp)], axis=0)
                dm2 = _dot_nt(dyp, xp2)
                ms = []
                for hh in range(2):
                    h = col // HP + hh
                    dec = _decay_matrix(c, h)
                    m = gmat * dec
                    dm = dm2[:, hh * CH:(hh + 1) * CH]
                    dseg = dm * m
                    dcs = dcs + jnp.where(lane == h, jnp.sum(dseg, axis=1, keepdims=True), 0.0)
                    dcst = dcst + jnp.where(sub == h, jnp.sum(dseg, axis=0, keepdims=True), 0.0)
                    dgm = dgm + dm * dec
                    ms.append(m.astype(bf16))
                dxdt_scr[:, col:col + CH] = _dot_tn(jnp.concatenate(ms, axis=0), dy2)
            dgm_b = dgm.astype(bf16)
            bds = _dot(bg, dpn_b[:, gsl])
            dxdt_scr[:, gsl] = dxdt_scr[:, gsl] + c["dte_x"][:, gsl] * bds
            dc_g = _dot(dgm_b, bg) + _dot_nt(dq_b[:, gsl], p_b[:, gsl])
            db_g = _dot_tn(dgm_b, cg) + _dot_nt(xdte_b[:, gsl], dpn_b[:, gsl])
            dx_ref[:, D + g * NS:D + (g + 1) * NS] = db_g
            dx_ref[:, D + NG * NS + g * NS:D + NG * NS + (g + 1) * NS] = dc_g
            dp_scr[:, gsl] = dpn[:, gsl] * c["ecl_x"][:, gsl] + _dot_tn(cg, dq_b[:, gsl])
            q_g = _dot(cg, p_b[:, gsl])
            e_g = e[:, gsl]
            dcs = dcs + c["ecs"] * _split_dot(dy[:, gsl] * q_g, e_g, REDUCE_TERMS, nt=True)
            ddte = _split_dot(xdt[:, gsl] * bds, e_g, REDUCE_TERMS, nt=True) * c["dte"]
            dcs = dcs - ddte
            dcs = dcs + jnp.where(sub == CH - 1, jnp.sum(ddte, axis=0, keepdims=True), 0.0)

        decl = _split_dot(jnp.broadcast_to(jnp.sum(dpn * p, axis=0, keepdims=True), (8, D)), e, 2, nt=True)[0:1]
        dcs = dcs + jnp.where(sub == CH - 1, c["ecl"] * decl, 0.0)
        dcs = dcs - dcst.T
        dadt = lax.dot_general(tri_ref[...], dcs, (((0,), (0,)), ((), ())), precision=lax.Precision.HIGHEST,
                               preferred_element_type=f32)
        dxdt = dxdt_scr[...]
        ddt = dadt * c["a"] + _split_dot(dxdt * xs, e, REDUCE_TERMS, nt=True)
        ddtr = jnp.where(lane < NH, ddt * _sigmoid(c["dtr"]), 0.0)
        ddt_ref[...] = ddtr.astype(bf16)
        dx_ref[:, 0:D] = dxdt * c["dt_x"] + c["dsk_x"] * dy
        dsk = _split_dot(jnp.broadcast_to(jnp.sum(dy * xs, axis=0, keepdims=True), (8, D)), e, 2, nt=True)[0:1]
        dalog = jnp.sum(dadt * c["dt"], axis=0, keepdims=True) * c["a"]
        row8 = lax.broadcasted_iota(jnp.int32, (8, CH), 0)
        dpar = jnp.where(row8 == 0, jnp.sum(ddtr, axis=0, keepdims=True),
                         jnp.where(row8 == 1, dalog, jnp.where(row8 == 2, dsk, 0.0)))
        dpar = jnp.where(lax.broadcasted_iota(jnp.int32, (8, CH), 1) < NH, dpar, 0.0)
        _accum(dpar_ref, dpar, i == 0)
        _accum(dgs_ref, jnp.sum(dgs_rows, axis=0, keepdims=True), i == 0)

    nb = t // CH
    rev = lambda i: (i // nc) * nc + (nc - 1 - i % nc)
    return pl.pallas_call(
        body, name=name, grid=(nb,),
        in_specs=[pl.BlockSpec((CH, D), lambda i: (rev(i), 0)),
                  pl.BlockSpec((CH, NG * NS), lambda i: (rev(i), D // (NG * NS))),
                  pl.BlockSpec((CH, NG * NS), lambda i: (rev(i), D // (NG * NS) + 1)),
                  pl.BlockSpec((CH, CH), lambda i: (rev(i), COL_DT // CH)),
                  pl.BlockSpec((CH, D), lambda i: (rev(i), COL_Z // D)),
                  pl.BlockSpec((CH, D), lambda i: (rev(i), 0)),
                  pl.BlockSpec((1, NS, D), lambda i: (rev(i), 0, 0)),
                  pl.BlockSpec((CH, D), lambda i: (rev(i), 1)),
                  pl.BlockSpec((8, CH), lambda i: (0, 0)), pl.BlockSpec((CH, D), lambda i: (0, 0)),
                  pl.BlockSpec((CH, CH), lambda i: (0, 0)), pl.BlockSpec((1, D), lambda i: (0, 0)),
                  pl.BlockSpec(memory_space=pl.ANY)],
        out_specs=[pl.BlockSpec((CH, XBC), lambda i: (rev(i), 0)), pl.BlockSpec((CH, D), lambda i: (rev(i), COL_Z // D)),
                   pl.BlockSpec((CH, CH), lambda i: (rev(i), 0)),
                   pl.BlockSpec((8, CH), lambda i: (0, 0)), pl.BlockSpec((1, D), lambda i: (0, 0))],
        out_shape=[_sds((t, XBC), f32), _sds((t, PROJ), bf16), _sds((t, CH), bf16), _sds((8, CH), f32), _sds((1, D), f32)],
        input_output_aliases={12: 1},
        scratch_shapes=[pltpu.VMEM((NS, D), f32), pltpu.VMEM((CH, D), f32)],
        compiler_params=_cparams(1))(xbcs, xbcs, xbcs, proj, proj, ypre, states, dcat, par, expand, tri, gs, dproj)


def loss_head(y, target, tb, name):
    t = y.shape[0]

    def body(y_ref, t_ref, s_ref, dy_ref):
        err = y_ref[...] - t_ref[...]
        dy_ref[...] = err * (1.0 / D)
        _accum(s_ref, jnp.zeros((8, CH), f32) + jnp.sum(err * err), pl.program_id(0) == 0)

    return pl.pallas_call(
        body, name=name, grid=(t // tb,),
        in_specs=[pl.BlockSpec((tb, D), lambda i: (i, 0)), pl.BlockSpec((tb, D), lambda i: (i, 0))],
        out_specs=[pl.BlockSpec((8, CH), lambda i: (0, 0)), pl.BlockSpec((tb, D), lambda i: (i, 0))],
        out_shape=[_sds((8, CH), f32), _sds((t, D), f32)],
        compiler_params=_cparams(1))(y, target)


def _tiles(t, seq):
    tm = min(512, t)
    return dict(tm=tm, tm_small=min(256, t), tm_large=min(1024, t), tm_huge=min(2048, t), tb=min(512, seq))


def local_step(x, target, depth, weights_of, seq, grads_done=None):
    t = x.shape[0]
    ts = _tiles(t, seq)
    tm, tl, th, tb = ts["tm"], ts["tm_large"], ts["tm_huge"], ts["tb"]
    saved, ws = [], []
    for l in range(depth):
        w = weights_of(l, x)
        ws.append(w)
        proj, h1 = norm_matmul(x, w["g1"], w["win"], th, 1920, bf16, "in_proj", token=w.get("token"))
        cat = group_a_fwd(proj, w["wa"], w["ga"], seq, tb, "group_a_fwd")
        xbcs, dact = conv_b_fwd(proj, w["ws"], w["bs"], seq, tb, "conv_b_fwd")
        cat, ypre, states = ssd_fwd(xbcs, proj, w["par"], w["gs"], cat, seq, "ssd_fwd")
        if "late" in w:
            w.update(w.pop("late")(cat))
        mix, x2 = matmul_postnorm(cat, w["wo"], x, w["g2"], tl, "out_proj")
        fp, h2, o, x3 = mlp_fwd(x2, w["g3"], w["wu"], w["wd"], w["g4"], tm, "mlp_fwd")
        saved.append(dict(x=x, proj=proj, h1=h1, xbcs=xbcs, dact=dact, ypre=ypre, states=states, cat=cat, mix=mix, x2=x2,
                          fp=fp, h2=h2, o=o))
        x = x3
    sse, dx = loss_head(x, target, tm, "loss_head")
    grads = [None] * depth
    for l in reversed(range(depth)):
        s, w = saved[l], ws[l]
        do, dfp, dx2, dg4, dg3 = mlp_bwd(s["o"], w["g4"], dx, w["wd"], s["fp"], w["wu"], s["x2"], w["g3"],
                                         tm, "mlp_bwd")
        dwd = matmul_tn(s["fp"], do, 512, 1024, True, "mlp_down_dw")
        dwu = matmul_tn(s["h2"], dfp, tl, 1024, False, "mlp_up_dw", col_blocks=True)
        dmix, dg2, dcat = postnorm_bwd_matmul(s["mix"], w["g2"], dx2, w["wo"], tl, 2 * D, "out_proj_bwd")
        dwo = matmul_tn(s["cat"], dmix, 512, 1024, False, "out_proj_dw")
        token = None if grads_done is None else grads_done(l, dict(wo=dwo, wu=dwu, wd=dwd), False)
        dproj, dwa, dga = group_a_bwd(s["proj"], dcat, w["wa"], w["ga"], seq, tb, "group_a_bwd", token=token)
        dxbcs, dproj, ddt, dpar, dgs = ssd_bwd(s["xbcs"], s["proj"], s["ypre"], s["states"], dcat, w["par"], w["gs"],
                                               dproj, seq, "ssd_bwd")
        dproj, dws, dbs = conv_b_bwd(s["proj"], dxbcs, s["dact"], w["ws"], dproj, seq, tb, "conv_b_bwd")
        dproj = place_columns(dproj, ddt, COL_DT // CH, tm, "place_ddt")
        dwin = matmul_tn(s["h1"], dproj, tl, 1152, False, "in_proj_dw")
        token = None if grads_done is None else grads_done(l, dict(win=dwin), True)
        dx, dg1 = matmul_prenorm_bwd(dproj, w["win"], s["x"], w["g1"], dx2, tm, "in_proj_bwd", token=token)
        grads[l] = dict(win=dwin, wo=dwo, wu=dwu, wd=dwd, wa=dwa, ws=dws, bs=dbs, par=dpar,
                        g1=dg1, ga=dga, gs=dgs, g2=dg2, g3=dg3, g4=dg4)
    return sse, dx, grads


GROUPS = {
    "chips": [(1, 0, 0), (0, 1, 0), (1, 1, 0)],
    "pair": [(0, 0, 1)],
    "all": [(1, 0, 0), (0, 1, 0), (1, 1, 0), (0, 0, 1), (1, 0, 1), (0, 1, 1), (1, 1, 1)],
}


def _group_index(group, x, y, c):
    return {"chips": 2 * x + y, "pair": c, "all": 4 * x + 2 * y + c}[group]


def _chunk_indices(shape, pieces):
    if len(shape) < 3:
        return [()]
    lead = [()]
    for n in shape[:-2]:
        lead = [i + (k,) for i in lead for k in range(n)]
    rows = shape[-2]
    split = max(1, pieces // len(lead))
    while split > 1 and (rows % split or (rows // split) % 16):
        split -= 1
    step = rows // split
    return [i + (pl.ds(s * step, step),) for i in lead for s in range(split)]


def _exchange(arrays, out_shapes, group, src_view, dst_view, view_shape, name, own, pieces=16):
    masks = GROUPS[group]
    na, nm = len(arrays), len(masks)
    cuts = [_chunk_indices(view_shape(a), pieces) for a in range(na)]

    def body(*refs):
        ins, outs = refs[:na], refs[na:2 * na]
        send_sems, recv_sems = refs[2 * na:2 * na + 2]
        local_sems = refs[2 * na + 2] if own else None
        x, y, c = lax.axis_index("x"), lax.axis_index("y"), lax.axis_index("c")
        me = _group_index(group, x, y, c)
        peers = []
        for mx, my, mc in masks:
            px, py, pc = (1 - x if mx else x), (1 - y if my else y), (1 - c if mc else c)
            peers.append(((px, py, pc), _group_index(group, px, py, pc)))

        def part(ref, idx):
            return ref.at[idx] if idx else ref

        if own:
            for a in range(na):
                for idx in cuts[a]:
                    pltpu.make_async_copy(part(src_view(ins[a], a, me), idx), part(dst_view(outs[a], a, me), idx),
                                          local_sems.at[a]).start()
        for a in range(na):
            for j, (dev, pidx) in enumerate(peers):
                for idx in cuts[a]:
                    pltpu.make_async_remote_copy(
                        src_ref=part(src_view(ins[a], a, pidx), idx), dst_ref=part(dst_view(outs[a], a, me), idx),
                        send_sem=send_sems.at[a * nm + j], recv_sem=recv_sems.at[a * nm + j],
                        device_id=dev, device_id_type=MESH).start()
        whole = []
        for a in range(na):
            for j, (dev, pidx) in enumerate(peers):
                whole.append(pltpu.make_async_remote_copy(
                    src_ref=src_view(ins[a], a, pidx), dst_ref=dst_view(outs[a], a, pidx),
                    send_sem=send_sems.at[a * nm + j], recv_sem=recv_sems.at[a * nm + j],
                    device_id=dev, device_id_type=MESH))
        for cp in whole:
            cp.wait_recv()
        for cp in whole:
            cp.wait_send()
        if own:
            for a in range(na):
                pltpu.make_async_copy(src_view(ins[a], a, me), dst_view(outs[a], a, me), local_sems.at[a]).wait()

    hbm = pl.BlockSpec(memory_space=pltpu.HBM)
    sems = [pltpu.SemaphoreType.DMA((na * nm,)), pltpu.SemaphoreType.DMA((na * nm,))]
    return pl.pallas_call(
        body, name=name, in_specs=[hbm] * na, out_specs=[hbm] * na,
        out_shape=[_sds(s, a.dtype) for s, a in zip(out_shapes, arrays)],
        scratch_shapes=sems + ([pltpu.SemaphoreType.DMA((na,))] if own else []))(*arrays)


def all_gather(arrays, group, name, slot_axis=0, own=True):
    n = len(GROUPS[group]) + 1
    shapes = [a.shape[:slot_axis] + (n,) + a.shape[slot_axis:] for a in arrays]
    lead = (slice(None),) * slot_axis
    return _exchange(arrays, shapes, group, lambda r, a, i: r, lambda r, a, i: r.at[lead + (i,)],
                     lambda a: arrays[a].shape, name, own)


HBM_SPEC = pl.BlockSpec(memory_space=pltpu.HBM)
SEM_SPEC = pl.BlockSpec(memory_space=pltpu.SEMAPHORE)
DATAFLOW = pltpu.SideEffectType.DATAFLOW_SIDE_EFFECTING
N_CHIPS = 4


def _peers(group, x, y, c):
    out = []
    for mx, my, mc in GROUPS[group]:
        px, py, pc = (1 - x if mx else x), (1 - y if my else y), (1 - c if mc else c)
        out.append(((px, py, pc), _group_index(group, px, py, pc)))
    return out


def _whole_views(sources):
    return dict(src=lambda ref, a, c, to: ref, dst=lambda ref, a, c, sender: ref.at[sender],
                rows=lambda a: sources[a].shape[0])


def _weight_views(shards):
    half = [s.shape[0] // 2 for s in shards]
    return dict(src=lambda ref, a, c, to_chip: ref.at[pl.ds(c * half[a], half[a])],
                dst=lambda ref, a, c, from_chip: ref.at[from_chip, pl.ds(c * half[a], half[a])],
                rows=lambda a: half[a])


def _grad_views(sums):
    return dict(src=lambda ref, a, c, to_chip: ref.at[to_chip], dst=lambda ref, a, c, from_chip: ref.at[from_chip],
                rows=lambda a: sums[a].shape[1])


def chips_start(sources, zones, views, name, pieces=4, after=None, group="chips"):
    na, nm = len(sources), len(GROUPS[group])

    def body(*refs):
        ins, lands = refs[:na], refs[na:2 * na]
        n_in = 2 * na + len(_token_arg(after))
        send_sems, recv_sems, token = refs[n_in], refs[n_in + 1], refs[-1]
        x, y, c = lax.axis_index("x"), lax.axis_index("y"), lax.axis_index("c")
        me = _group_index(group, x, y, c)
        for a in range(na):
            step = views["rows"](a) // pieces
            for j, (dev, to) in enumerate(_peers(group, x, y, c)):
                for q in range(pieces):
                    rows = pl.ds(q * step, step)
                    pltpu.make_async_remote_copy(
                        src_ref=views["src"](ins[a], a, c, to).at[rows],
                        dst_ref=views["dst"](lands[a], a, c, me).at[rows],
                        send_sem=send_sems.at[a * nm + j], recv_sem=recv_sems.at[a * nm + j],
                        device_id=dev, device_id_type=MESH).start()
        token[...] = jnp.zeros_like(token)

    both = list(sources) + list(zones)
    outs = pl.pallas_call(
        body, name=name,
        out_shape=(pltpu.SemaphoreType.DMA((na * nm,)), pltpu.SemaphoreType.DMA((na * nm,)),
                   *[pltpu.HBM(b.shape, b.dtype) for b in both], _sds((8, CH), f32)),
        in_specs=[HBM_SPEC] * (2 * na) + _token_spec(after),
        out_specs=(SEM_SPEC, SEM_SPEC, *[HBM_SPEC] * (2 * na), pl.BlockSpec(memory_space=pltpu.VMEM)),
        input_output_aliases={i: 2 + i for i in range(2 * na)},
        compiler_params=pltpu.CompilerParams(has_side_effects=DATAFLOW))(
            *[pltpu.with_memory_space_constraint(b, pltpu.HBM) for b in both], *_token_arg(after))
    return dict(send=outs[0], recv=outs[1], sources=list(outs[2:2 + na]), zones=list(outs[2 + na:2 + 2 * na]),
                token=outs[-1], views=views, group=group)


def chips_wait(started, after, name):
    sources, zones, views, group = started["sources"], started["zones"], started["views"], started["group"]
    na, nm = len(sources), len(GROUPS[group])

    def body(*refs):
        ins, lands = refs[:na], refs[na:2 * na]
        send_sems, recv_sems = refs[2 * na], refs[2 * na + 1]
        x, y, c = lax.axis_index("x"), lax.axis_index("y"), lax.axis_index("c")
        for a in range(na):
            for j, (dev, peer) in enumerate(_peers(group, x, y, c)):
                cp = pltpu.make_async_remote_copy(
                    src_ref=views["src"](ins[a], a, c, peer), dst_ref=views["dst"](lands[a], a, c, peer),
                    send_sem=send_sems.at[a * nm + j], recv_sem=recv_sems.at[a * nm + j],
                    device_id=dev, device_id_type=MESH)
                cp.wait_send()
                cp.wait_recv()

    both = list(sources) + list(zones)
    outs = pl.pallas_call(
        body, name=name, out_shape=tuple(pltpu.HBM(b.shape, b.dtype) for b in both),
        in_specs=[HBM_SPEC] * (2 * na) + [SEM_SPEC, SEM_SPEC, pl.BlockSpec(memory_space=pl.ANY)],
        out_specs=tuple([HBM_SPEC] * (2 * na)), input_output_aliases={i: i for i in range(2 * na)},
        compiler_params=pltpu.CompilerParams(has_side_effects=DATAFLOW))(*both, started["send"], started["recv"], after)
    return list(outs[:na]), list(outs[na:])


def weights_share(zones, name):
    na, nm = len(zones), N_CHIPS - 1

    def body(*refs):
        lands = refs[na:2 * na]
        send_sems, recv_sems = refs[2 * na:]
        x, y, c = lax.axis_index("x"), lax.axis_index("y"), lax.axis_index("c")
        chip = 2 * x + y
        sibling = (x, y, 1 - c)
        sends = []
        for a in range(na):
            half = zones[a].shape[1] // 2
            for m in range(1, N_CHIPS):
                mine = lands[a].at[chip ^ m, pl.ds(c * half, half)]
                sends.append(pltpu.make_async_remote_copy(
                    src_ref=mine, dst_ref=mine, send_sem=send_sems.at[a * nm + m - 1],
                    recv_sem=recv_sems.at[a * nm + m - 1], device_id=sibling, device_id_type=MESH))
        for cp in sends:
            cp.start()
        for a in range(na):
            half = zones[a].shape[1] // 2
            for m in range(1, N_CHIPS):
                theirs = lands[a].at[chip ^ m, pl.ds((1 - c) * half, half)]
                pltpu.make_async_remote_copy(
                    src_ref=theirs, dst_ref=theirs, send_sem=send_sems.at[a * nm + m - 1],
                    recv_sem=recv_sems.at[a * nm + m - 1], device_id=sibling, device_id_type=MESH).wait_recv()
        for cp in sends:
            cp.wait_send()

    return pl.pallas_call(
        body, name=name, in_specs=[HBM_SPEC] * na, out_specs=[HBM_SPEC] * na,
        out_shape=[_sds(z.shape, z.dtype) for z in zones], input_output_aliases={i: i for i in range(na)},
        scratch_shapes=[pltpu.SemaphoreType.DMA((na * nm,)), pltpu.SemaphoreType.DMA((na * nm,))])(*zones)


def pair_send_halves(grads, name):
    half = [g.shape[1] // 2 for g in grads]
    shapes = [(g.shape[0], h, g.shape[2]) for g, h in zip(grads, half)]
    return _exchange(grads, shapes, "pair", lambda r, a, i: r.at[:, pl.ds(i * half[a], half[a])],
                     lambda r, a, i: r, lambda a: shapes[a], name, False)


def sum_pair_half(g, recv, core, name, tb=256, by_chip=None):
    nk, r, c = g.shape
    tb = min(tb, r // 2)
    nb = r // 2 // tb

    def body(core_ref, g_ref, r_ref, o_ref):
        s = g_ref[...].astype(f32) + r_ref[...].astype(f32)
        if by_chip is None:
            o_ref[...] = s.astype(bf16)
        else:
            for k in range(by_chip[0]):
                o_ref[k] = s[:, k * by_chip[1]:(k + 1) * by_chip[1]].astype(bf16)

    if by_chip is None:
        out_spec = pl.BlockSpec((None, tb, c), lambda k, i, core_ref: (k, i, 0))
        out_shape = _sds((nk, r // 2, c), bf16)
    else:
        assert nk == 1
        out_spec = pl.BlockSpec((by_chip[0], tb, by_chip[1]), lambda k, i, core_ref: (0, i, 0))
        out_shape = _sds((by_chip[0], r // 2, by_chip[1]), bf16)
    return pl.pallas_call(
        body, name=name,
        grid_spec=pltpu.PrefetchScalarGridSpec(
            num_scalar_prefetch=1, grid=(nk, nb),
            in_specs=[pl.BlockSpec((None, tb, c), lambda k, i, core_ref: (k, core_ref[0] * nb + i, 0)),
                      pl.BlockSpec((None, tb, c), lambda k, i, core_ref: (k, i, 0))],
            out_specs=out_spec),
        out_shape=out_shape, compiler_params=_cparams(2))(jnp.reshape(core, (1,)).astype(jnp.int32), g, recv)


def assemble_columns(blocks, width, name, tb=256):
    n, r, c = blocks.shape

    def body(b_ref, o_ref):
        for k in range(n):
            o_ref[:, k * c:(k + 1) * c] = b_ref[k]
        o_ref[:, n * c:] = jnp.zeros((tb, width - n * c), blocks.dtype)

    return pl.pallas_call(
        body, name=name, grid=(r // tb,), in_specs=[pl.BlockSpec((n, tb, c), lambda i: (0, i, 0))],
        out_specs=pl.BlockSpec((tb, width), lambda i: (i, 0)), out_shape=_sds((r, width), blocks.dtype),
        compiler_params=_cparams(1))(blocks)


def chip_sum_into(acc, layer, own, others, chip, name, tb=256):
    n, r, c = own.shape
    tb = min(tb, r)

    def body(chip_ref, x_ref, y1_ref, y2_ref, y3_ref, acc_ref, o_ref):
        o_ref[...] = ((x_ref[...].astype(f32) + y1_ref[...].astype(f32)) + y2_ref[...].astype(f32)) + y3_ref[...].astype(f32)

    def slot(k):
        return pl.BlockSpec((None, tb, c), lambda i, chip_ref: (chip_ref[0] ^ k, i, 0))

    return pl.pallas_call(
        body, name=name,
        grid_spec=pltpu.PrefetchScalarGridSpec(
            num_scalar_prefetch=1, grid=(r // tb,),
            in_specs=[slot(k) for k in range(n)] + [pl.BlockSpec(memory_space=pl.ANY)],
            out_specs=pl.BlockSpec((None, tb, c), lambda i, chip_ref: (layer, i, 0))),
        out_shape=_sds(acc.shape, f32), input_output_aliases={n + 1: 0}, compiler_params=_cparams(1))(
            jnp.reshape(chip, (1,)).astype(jnp.int32), own, *([others] * (n - 1)), acc)


def adamw_half(w, g_half, m, v, half, name, before=None, token=None, tb=256):
    depth, r, c = w.shape
    tb = min(tb, r // 2)
    nb = r // 2 // tb
    n_extra = (0 if before is None else 4) + len(_token_arg(token))

    def body(half_ref, w_ref, gh_ref, m_ref, v_ref, *rest):
        g_ref, d_ref, mo_ref, vo_ref = rest[n_extra:]
        gv = gh_ref[...]
        m2 = B1 * m_ref[...] + (1.0 - B1) * gv
        v2 = B2 * v_ref[...] + (1.0 - B2) * (gv * gv)
        m_hat = m2 / (1.0 - B1 ** STEP)
        v_hat = v2 / (1.0 - B2 ** STEP)
        g_ref[...] = gv
        d_ref[...] = -LR * (m_hat / (jnp.sqrt(v_hat) + AEPS) + WD * w_ref[...])
        mo_ref[...] = m2
        vo_ref[...] = v2

    whole = pl.BlockSpec((None, tb, c), lambda l, i, half_ref: (l, half_ref[0] * nb + i, 0))
    part = pl.BlockSpec((None, tb, c), lambda l, i, half_ref: (l, i, 0))
    extra = ([] if before is None else list(before)) + _token_arg(token)
    return pl.pallas_call(
        body, name=name,
        grid_spec=pltpu.PrefetchScalarGridSpec(
            num_scalar_prefetch=1, grid=(depth, nb),
            in_specs=[whole, part, whole, whole] + [pl.BlockSpec(memory_space=pl.ANY)] * n_extra, out_specs=[whole] * 4),
        out_shape=[_sds(w.shape, f32)] * 4,
        input_output_aliases={} if before is None else {5 + k: k for k in range(4)},
        compiler_params=_cparams(2))(jnp.reshape(half, (1,)).astype(jnp.int32), w, g_half, m, v, *extra)


def sum_slots(y, out_dtype, name, tb=256):
    n, r, c = y.shape
    tb = min(tb, r)

    def body(y_ref, o_ref):
        acc = y_ref[0].astype(f32)
        for i in range(1, n):
            acc = acc + y_ref[i].astype(f32)
        o_ref[...] = acc.astype(out_dtype)

    return pl.pallas_call(
        body, name=name, grid=(r // tb,),
        in_specs=[pl.BlockSpec((n, tb, c), lambda i: (0, i, 0))], out_specs=pl.BlockSpec((tb, c), lambda i: (i, 0)),
        out_shape=_sds((r, c), out_dtype), compiler_params=_cparams(1))(y)


def adamw(w, g, m, v, name, tb=256):
    r, c = w.shape
    tb = min(tb, r)

    def body(w_ref, g_ref, m_ref, v_ref, d_ref, mo_ref, vo_ref):
        gv = g_ref[...]
        m2 = B1 * m_ref[...] + (1.0 - B1) * gv
        v2 = B2 * v_ref[...] + (1.0 - B2) * (gv * gv)
        m_hat = m2 / (1.0 - B1 ** STEP)
        v_hat = v2 / (1.0 - B2 ** STEP)
        d_ref[...] = -LR * (m_hat / (jnp.sqrt(v_hat) + AEPS) + WD * w_ref[...])
        mo_ref[...] = m2
        vo_ref[...] = v2

    spec = pl.BlockSpec((tb, c), lambda i: (i, 0))
    return pl.pallas_call(
        body, name=name, grid=(r // tb,), in_specs=[spec] * 4, out_specs=[spec] * 3,
        out_shape=[_sds((r, c), f32)] * 3, compiler_params=_cparams(1))(w, g, m, v)


def adamw_leading(w, g, m, v, name, tc=64):
    c, l, r = w.shape
    main = c // tc
    tail = c - main * tc

    def body(w_ref, g_ref, m_ref, v_ref, *rest):
        d_ref, mo_ref, vo_ref = rest[-3:]
        gv = g_ref[...]
        m2 = B1 * m_ref[...] + (1.0 - B1) * gv
        v2 = B2 * v_ref[...] + (1.0 - B2) * (gv * gv)
        m_hat = m2 / (1.0 - B1 ** STEP)
        v_hat = v2 / (1.0 - B2 ** STEP)
        d_ref[...] = -LR * (m_hat / (jnp.sqrt(v_hat) + AEPS) + WD * w_ref[...])
        mo_ref[...] = m2
        vo_ref[...] = v2

    spec = pl.BlockSpec((tc, l, r), lambda i: (i, 0, 0))
    outs = pl.pallas_call(
        functools.partial(body), name=name, grid=(main,), in_specs=[spec] * 4, out_specs=[spec] * 3,
        out_shape=[_sds(w.shape, f32)] * 3, compiler_params=_cparams(1))(w, g, m, v)
    if tail:
        assert (main * tc) % tail == 0
        last = pl.BlockSpec((tail, l, r), lambda i: (main * tc // tail, 0, 0))
        outs = pl.pallas_call(
            functools.partial(body), name=name + "_tail", grid=(1,),
            in_specs=[last] * 4 + [pl.BlockSpec(memory_space=pl.ANY)] * 3, out_specs=[last] * 3,
            out_shape=[_sds(w.shape, f32)] * 3, input_output_aliases={4: 0, 5: 1, 6: 2},
            compiler_params=_cparams(1))(w, g, m, v, *outs)
    return outs


SMALL_ROW = 1024
SMALL_GAINS = ("g1", "ga", "gs", "g2", "g3", "g4")
SMALL_LAYER_ROWS = 8 + 8 + 16 + 8


def _pack_small(grads):
    wide = lambda a: jnp.pad(a, ((0, 0), (0, 2 * SMALL_ROW - a.shape[1]))).reshape(-1, SMALL_ROW)
    row = lax.broadcasted_iota(jnp.int32, (8, SMALL_ROW), 0)
    parts = []
    for g in grads:
        singles = [g[k] for k in SMALL_GAINS] + [g["bs"][:, :SMALL_ROW],
                                                 jnp.pad(g["bs"][:, SMALL_ROW:], ((0, 0), (0, 2 * SMALL_ROW - XBC)))]
        first = sum(jnp.where(row == k, s, 0.0) for k, s in enumerate(singles))
        parts += [first, g["wa"], wide(g["ws"]), jnp.pad(g["par"], ((0, 0), (0, SMALL_ROW - CH)))]
    return jnp.concatenate(parts, axis=0)


def _unpack_small(packed, depth):
    rows = packed.reshape(depth, SMALL_LAYER_ROWS, SMALL_ROW)
    out = {k: rows[:, i] for i, k in enumerate(SMALL_GAINS)}
    out["bs"] = rows[:, 6:8].reshape(depth, 2 * SMALL_ROW)[:, :XBC]
    out["wa"] = rows[:, 8:11]
    out["ws"] = rows[:, 16:32].reshape(depth, 8, 2 * SMALL_ROW)[:, :4, :XBC]
    out["par"] = rows[:, 32:35, :CH]
    return out


def kernel(x, norm_mix_pre, w_in, conv_a_w, ssm_conv_w, ssm_conv_b, dt_bias, a_log, d_skip, conv_out_norm, ssm_out_norm, w_out, norm_mix_post, norm_mlp_pre, w_up, w_down, norm_mlp_post, loss_target, m_norm_mix_pre, m_w_in, m_conv_a_w, m_ssm_conv_w, m_ssm_conv_b, m_dt_bias, m_a_log, m_d_skip, m_conv_out_norm, m_ssm_out_norm, m_w_out, m_norm_mix_post, m_norm_mlp_pre, m_w_up, m_w_down, m_norm_mlp_post, v_norm_mix_pre, v_w_in, v_conv_a_w, v_ssm_conv_w, v_ssm_conv_b, v_dt_bias, v_a_log, v_d_skip, v_conv_out_norm, v_ssm_out_norm, v_w_out, v_norm_mix_post, v_norm_mlp_pre, v_w_up, v_w_down, v_norm_mlp_post):
    nb, seq, _ = x.shape
    t = nb * seq
    depth = w_in.shape[0]
    ncol = w_in.shape[2]
    chip = 2 * lax.axis_index("x") + lax.axis_index("y")

    taps = [conv_a_w, ssm_conv_w]
    taps_g = all_gather(taps, "chips", "gather_taps", slot_axis=1, own=False)
    wa_g, ws_g = [lax.dynamic_update_index_in_dim(g, s, chip, 1) for g, s in zip(taps_g, taps)]
    wa_full = jnp.transpose(wa_g, (0, 2, 1, 3)).reshape(depth, 3, D)
    ws_full = jnp.transpose(ws_g, (0, 2, 1, 3)).reshape(depth, 4, XBC)
    lane_pad = lambda a: jnp.pad(a, ((0, 0), (0, CH - a.shape[1])))
    par = jnp.stack([lane_pad(dt_bias), lane_pad(a_log), lane_pad(d_skip)], axis=1)
    par = jnp.pad(par, ((0, 0), (0, 5), (0, 0)))

    layer_shards = lambda l: [w_in[l].astype(bf16), w_out[l].astype(bf16), w_up[l].astype(bf16), w_down[l].astype(bf16)]
    issued = []

    def start(shards, name, whole=False):
        zones = [lax.empty((N_CHIPS,) + s.shape, s.dtype) for s in shards]
        issued.append(chips_start(shards, zones, _whole_views(shards) if whole else _weight_views(shards), name,
                                  after=issued[-1]["token"] if issued else taps_g[0]))
        issued[-1]["whole"] = whole
        return issued[-1]

    def finish(started, after, name):
        shards, zones = chips_wait(started, after, name)
        if not started["whole"]:
            zones = weights_share(zones, "weights_share")
        return [lax.dynamic_update_index_in_dim(z, s, chip, 0) for z, s in zip(zones, shards)]

    def shaped(mats):
        wo_z, wu_z, wd_z = mats
        return wo_z.reshape(2 * D, D), wu_z, wd_z.reshape(DFF, D)

    first = layer_shards(0)
    travelling = {0: start(first[:1], "weights_start_0")}
    rest = start(first[1:], "weights_start_0_rest")
    for l in range(1, depth):
        travelling[l] = start(layer_shards(l), f"weights_start_{l}", whole=l >= 2)

    def weights_of(l, x_in):
        mats = finish(travelling.pop(l), x_in, f"weights_wait_{l}")
        w = dict(win=assemble_columns(mats[0], PROJ, "assemble_w_in"), wa=jnp.pad(wa_full[l], ((0, 5), (0, 0))),
                 ws=jnp.pad(ws_full[l], ((0, 4), (0, 0))), bs=ssm_conv_b[l][None], par=par[l],
                 g1=norm_mix_pre[l][None], ga=conv_out_norm[l][None], gs=ssm_out_norm[l][None],
                 g2=norm_mix_post[l][None], g3=norm_mlp_pre[l][None], g4=norm_mlp_post[l][None])
        if l == 0:
            w["token"] = issued[-1]["token"]
            w["late"] = lambda after: dict(zip(("wo", "wu", "wd"), shaped(finish(rest, after, "weights_wait_0_rest"))))
        else:
            w.update(zip(("wo", "wu", "wd"), shaped(mats[1:])))
        return w

    core = lax.axis_index("c")
    grads_travelling = {}
    given_m = dict(win=m_w_in, wo=m_w_out, wu=m_w_up, wd=m_w_down)
    given_v = dict(win=v_w_in, wo=v_w_out, wu=v_w_up, wd=v_w_down)

    chip_major = dict(win=lambda a: a[None], wo=lambda a: a.reshape(N_CHIPS, 2 * D // N_CHIPS, D), wu=lambda a: a,
                      wd=lambda a: a.reshape(N_CHIPS, DFF // N_CHIPS, D))
    held = {}

    to_pair = {}

    def pair_sums_to_chips(l, keys, mats, received, after=None):
        sums = [sum_pair_half(m_, r_, core, "pair_sum", by_chip=(N_CHIPS, ncol) if k == "win" else None)
                for k, m_, r_ in zip(keys, mats, received)]
        zones = [lax.empty(s.shape, s.dtype) for s in sums]
        started = chips_start(sums, zones, _grad_views(sums), f"grads_start_{l}_{len(grads_travelling)}", after=after)
        grads_travelling[(l, keys[0])] = (keys, started)
        return started["token"]

    def grads_done(l, g, last):
        token = None
        if not last and l + 1 in to_pair:
            keys, started = to_pair.pop(l + 1)
            mats, received = chips_wait(started, g["wo"], f"grads_to_pair_wait_{l + 1}")
            token = pair_sums_to_chips(l + 1, keys, mats, received)
        if l > 0 and not last:
            held[l] = g
            return token
        g = {**held.pop(l, {}), **g}
        keys = [k for k in ("win", "wo", "wu", "wd") if k in g]
        mats = [chip_major[k](g[k]) for k in keys]
        if l > 0:
            half = [m_.shape[1] // 2 for m_ in mats]
            views = dict(src=lambda ref, a, c, to: ref.at[:, pl.ds(to * half[a], half[a])],
                         dst=lambda ref, a, c, sender: ref, rows=lambda a: mats[a].shape[0])
            zones = [lax.empty((m_.shape[0], h, m_.shape[2]), m_.dtype) for m_, h in zip(mats, half)]
            to_pair[l] = (keys, chips_start(mats, zones, views, f"grads_to_pair_start_{l}", pieces=1, group="pair"))
            return to_pair[l][1]["token"]
        return pair_sums_to_chips(l, keys, mats, pair_send_halves(mats, "grads_to_pair"), after=token)

    sse, dx, grads = local_step(x.reshape(t, D), loss_target.reshape(t, D), depth, weights_of, seq, grads_done)
    loss = lax.psum(0.5 / D * sse[0, 0], ("x", "y", "c"))

    packed = _pack_small(grads)
    small_travelling = chips_start([packed], [lax.empty((8,) + packed.shape, f32)], _whole_views([packed]),
                                   "small_start", group="all")

    big_w = dict(win=w_in, wo=w_out, wu=w_up, wd=w_down)
    acc = {k: lax.empty((depth, bw.shape[1] // 2, bw.shape[2]), f32) for k, bw in big_w.items()}
    for n, ((l, _), (keys, started)) in enumerate(grads_travelling.items()):
        sums, zones = chips_wait(started, small_travelling["token"], f"grads_wait_{l}_{n}")
        for k, s, z in zip(keys, sums, zones):
            acc[k] = chip_sum_into(acc[k], l, s, z, chip, "chip_sum")
    names = ("win", "wo", "wu", "wd")
    acc = [acc[k] for k in names]
    pair_views = dict(src=lambda ref, a, c, to: ref, dst=lambda ref, a, c, sender: ref, rows=lambda a: depth)
    to_sibling = chips_start(acc, [lax.empty(a.shape, f32) for a in acc], pair_views, "grads_from_pair_start",
                             pieces=depth, group="pair")
    own_done = {}
    for k, a in zip(names, to_sibling["sources"]):
        if big_w[k].shape[-1] % CH == 0:
            own_done[k] = adamw_half(big_w[k], a, given_m[k], given_v[k], core, "adamw_matrix",
                                     token=to_sibling["token"])

    (packed,), (small_all,) = chips_wait(small_travelling, own_done["wd"][1], "small_wait")
    small_all = lax.dynamic_update_index_in_dim(small_all, packed, 4 * lax.axis_index("x") + 2 * lax.axis_index("y") + core, 0)
    small = _unpack_small(sum_slots(small_all, f32, "small_sum", tb=8), depth)
    wa_cols, ws_cols = conv_a_w.shape[2], ssm_conv_w.shape[2]
    par_g = small["par"].reshape(depth, 3, CH)
    g_small = dict(
        norm_mix_pre=small["g1"], conv_out_norm=small["ga"], ssm_out_norm=small["gs"], norm_mix_post=small["g2"],
        norm_mlp_pre=small["g3"], norm_mlp_post=small["g4"], ssm_conv_b=small["bs"],
        conv_a_w=lax.dynamic_slice_in_dim(small["wa"].reshape(depth, 3, D), chip * wa_cols, wa_cols, axis=2),
        ssm_conv_w=lax.dynamic_slice_in_dim(small["ws"].reshape(depth, 4, XBC), chip * ws_cols, ws_cols, axis=2),
        dt_bias=par_g[:, 0, :NH], a_log=par_g[:, 1, :NH], d_skip=par_g[:, 2, :NH])

    given = dict(norm_mix_pre=(norm_mix_pre, m_norm_mix_pre, v_norm_mix_pre), w_in=(w_in, m_w_in, v_w_in),
                 conv_a_w=(conv_a_w, m_conv_a_w, v_conv_a_w), ssm_conv_w=(ssm_conv_w, m_ssm_conv_w, v_ssm_conv_w),
                 ssm_conv_b=(ssm_conv_b, m_ssm_conv_b, v_ssm_conv_b), dt_bias=(dt_bias, m_dt_bias, v_dt_bias),
                 a_log=(a_log, m_a_log, v_a_log), d_skip=(d_skip, m_d_skip, v_d_skip),
                 conv_out_norm=(conv_out_norm, m_conv_out_norm, v_conv_out_norm),
                 ssm_out_norm=(ssm_out_norm, m_ssm_out_norm, v_ssm_out_norm), w_out=(w_out, m_w_out, v_w_out),
                 norm_mix_post=(norm_mix_post, m_norm_mix_post, v_norm_mix_post),
                 norm_mlp_pre=(norm_mlp_pre, m_norm_mlp_pre, v_norm_mlp_pre), w_up=(w_up, m_w_up, v_w_up),
                 w_down=(w_down, m_w_down, v_w_down), norm_mlp_post=(norm_mlp_post, m_norm_mlp_post, v_norm_mlp_post))
    order = ["norm_mix_pre", "w_in", "conv_a_w", "ssm_conv_w", "ssm_conv_b", "dt_bias", "a_log", "d_skip",
             "conv_out_norm", "ssm_out_norm", "w_out", "norm_mix_post", "norm_mlp_pre", "w_up", "w_down",
             "norm_mlp_post"]
    short = dict(w_in="win", w_out="wo", w_up="wu", w_down="wd")
    results = {}
    for n in order:
        if n in short:
            continue
        wv, mv, vv = given[n]
        gv = g_small[n].reshape(wv.shape)
        two_d = lambda a: a.reshape(-1, a.shape[-1])
        results[n] = (gv,) + tuple(adamw(two_d(wv), two_d(gv), two_d(mv), two_d(vv), "adamw"))

    acc, from_sibling = chips_wait(to_sibling, results["norm_mlp_post"][1], "grads_from_pair_wait")
    for n, k in short.items():
        wv, mv, vv = given[n]
        own, recv = acc[names.index(k)], from_sibling[names.index(k)]
        if k in own_done:
            results[n] = adamw_half(wv, recv, mv, vv, 1 - core, "adamw_matrix", before=own_done[k])
        else:
            to_cols, to_rows = (lambda a: jnp.transpose(a, (2, 0, 1))), (lambda a: jnp.transpose(a, (1, 2, 0)))
            own_c, recv_c = to_cols(own), to_cols(recv)
            g_cols = jnp.where(core == 0, jnp.concatenate([own_c, recv_c], axis=2),
                               jnp.concatenate([recv_c, own_c], axis=2))
            results[n] = tuple(to_rows(o) for o in (g_cols,) + tuple(adamw_leading(to_cols(wv), g_cols, to_cols(mv),
                                                                                   to_cols(vv), "adamw_cols")))
    g_out, d_out, m_out, v_out = [], [], [], []
    for n in order:
        wv = given[n][0]
        gv, dlt, m2, v2 = results[n]
        g_out.append(gv.reshape(wv.shape))
        d_out.append(dlt.reshape(wv.shape))
        m_out.append(m2.reshape(wv.shape))
        v_out.append(v2.reshape(wv.shape))
    return (loss, dx.reshape(nb, seq, D), *g_out, *d_out, *m_out, *v_out)
```

```python
import functools

import jax
import jax.numpy as jnp
from jax import lax
from jax.experimental import pallas as pl
from jax.experimental.pallas import tpu as pltpu

f32, bf16 = jnp.float32, jnp.bfloat16

D = 1024
NH, HP = 16, 64
NG, NS = 2, 128
CH = 128
XBC = D + 2 * NG * NS
DFF = 4 * D
IN_COLS = 3 * D + D + XBC + NH
PROJ = 5760
COL_Z, COL_XBC, COL_DT = 3 * D, 4 * D, 4 * D + XBC
EPS = 1e-6
HALO = 8
HBLK = 16
VMEM_LIMIT = 56 * 2**20
MESH = pl.DeviceIdType.MESH

LR, B1, B2, AEPS, WD, STEP = 0.001, 0.9, 0.999, 1e-08, 0.01, 10


def _cparams(n_axes):
    return pltpu.CompilerParams(dimension_semantics=("arbitrary",) * n_axes, vmem_limit_bytes=VMEM_LIMIT)


def _sds(shape, dtype):
    return jax.ShapeDtypeStruct(tuple(shape), dtype)


def _token_spec(token):
    return [] if token is None else [pl.BlockSpec(memory_space=pl.ANY)]


def _token_arg(token):
    return [] if token is None else [token]


def _resident(shape):
    return pl.BlockSpec(shape, lambda i: (0,) * len(shape), pipeline_mode=pl.Buffered(1))


def _rms_fwd(x, g):
    r = lax.rsqrt(jnp.mean(x * x, axis=-1, keepdims=True) + EPS)
    return x * r * g


def _rms_bwd(x, g, dy):
    r = lax.rsqrt(jnp.mean(x * x, axis=-1, keepdims=True) + EPS)
    xh = x * r
    gdy = dy * g
    dx = r * (gdy - xh * jnp.mean(xh * gdy, axis=-1, keepdims=True))
    return dx, dy * xh


def _accum(ref, part, first):
    @pl.when(first)
    def _():
        ref[...] = part

    @pl.when(jnp.logical_not(first))
    def _():
        ref[...] += part


def _dot_nt(a, b):
    return lax.dot_general(a, b, (((1,), (1,)), ((), ())), preferred_element_type=f32)


def _dot_tn(a, b):
    return lax.dot_general(a, b, (((0,), (0,)), ((), ())), preferred_element_type=f32)


def _dot(a, b):
    return jnp.dot(a, b, preferred_element_type=f32)


def _split_dot(x, e_bf, n_split, nt=False):
    acc = None
    rem = x
    for s in range(n_split):
        hi = rem.astype(bf16)
        term = _dot_nt(hi, e_bf) if nt else _dot(hi, e_bf)
        acc = term if acc is None else acc + term
        if s + 1 < n_split:
            rem = rem - hi.astype(f32)
    return acc


def _sigmoid(x):
    return 0.5 * jnp.tanh(0.5 * x) + 0.5


def norm_matmul(x, g, w, tm, tn, out_dtype, name, token=None):
    t, n = x.shape[0], w.shape[1]
    w_spec = pl.BlockSpec((D, tn), lambda i, j: (0, j))

    def body(x_ref, g_ref, w_ref, *rest):
        o_ref, h_ref = rest[-2:]

        @pl.when(pl.program_id(1) == 0)
        def _():
            h_ref[...] = _rms_fwd(x_ref[...], g_ref[...]).astype(bf16)

        o_ref[...] = _dot(h_ref[...], w_ref[...]).astype(out_dtype)

    return pl.pallas_call(
        body, name=name, grid=(t // tm, n // tn),
        in_specs=[pl.BlockSpec((tm, D), lambda i, j: (i, 0)), pl.BlockSpec((1, D), lambda i, j: (0, 0)), w_spec]
        + _token_spec(token),
        out_specs=[pl.BlockSpec((tm, tn), lambda i, j: (i, j)), pl.BlockSpec((tm, D), lambda i, j: (i, 0))],
        out_shape=[_sds((t, n), out_dtype), _sds((t, D), bf16)],
        compiler_params=_cparams(2))(x, g, w, *_token_arg(token))


def matmul_postnorm(a, w, xres, g, tm, name):
    t, k = a.shape

    def body(a_ref, w_ref, xr_ref, g_ref, y_ref, xo_ref):
        y = _dot(a_ref[...], w_ref[...])
        y_ref[...] = y.astype(bf16)
        xo_ref[...] = xr_ref[...] + _rms_fwd(y, g_ref[...])

    return pl.pallas_call(
        body, name=name, grid=(t // tm,),
        in_specs=[pl.BlockSpec((tm, k), lambda i: (i, 0)), _resident(w.shape),
                  pl.BlockSpec((tm, D), lambda i: (i, 0)), pl.BlockSpec((1, D), lambda i: (0, 0))],
        out_specs=[pl.BlockSpec((tm, D), lambda i: (i, 0)), pl.BlockSpec((tm, D), lambda i: (i, 0))],
        out_shape=[_sds((t, D), bf16), _sds((t, D), f32)],
        compiler_params=_cparams(1))(a, w, xres, g)


def postnorm_bwd_matmul(y, g, dxo, w, tm, tn, name):
    t, n = y.shape[0], w.shape[0]

    def body(y_ref, g_ref, dxo_ref, w_ref, dy_ref, dg_ref, da_ref):
        i, j = pl.program_id(0), pl.program_id(1)

        @pl.when(j == 0)
        def _():
            dx, dgc = _rms_bwd(y_ref[...].astype(f32), g_ref[...], dxo_ref[...])
            dy_ref[...] = dx.astype(bf16)
            _accum(dg_ref, jnp.sum(dgc, axis=0, keepdims=True), i == 0)

        da_ref[...] = _dot_nt(dy_ref[...], w_ref[...]).astype(bf16)

    return pl.pallas_call(
        body, name=name, grid=(t // tm, n // tn),
        in_specs=[pl.BlockSpec((tm, D), lambda i, j: (i, 0)), pl.BlockSpec((1, D), lambda i, j: (0, 0)),
                  pl.BlockSpec((tm, D), lambda i, j: (i, 0)), pl.BlockSpec((tn, D), lambda i, j: (j, 0))],
        out_specs=[pl.BlockSpec((tm, D), lambda i, j: (i, 0)), pl.BlockSpec((1, D), lambda i, j: (0, 0)),
                   pl.BlockSpec((tm, tn), lambda i, j: (i, j))],
        out_shape=[_sds((t, D), bf16), _sds((1, D), f32), _sds((t, n), bf16)],
        compiler_params=_cparams(2))(y, g, dxo, w)


def matmul_prenorm_bwd(da, w, x, g, dxo, tm, name, token=None):
    t, k = da.shape

    def body(da_ref, w_ref, x_ref, g_ref, dxo_ref, *rest):
        dx_ref, dg_ref = rest[-2:]
        dh = _dot_nt(da_ref[...], w_ref[...])
        dxn, dgc = _rms_bwd(x_ref[...], g_ref[...], dh)
        dx_ref[...] = dxo_ref[...] + dxn
        _accum(dg_ref, jnp.sum(dgc, axis=0, keepdims=True), pl.program_id(0) == 0)

    return pl.pallas_call(
        body, name=name, grid=(t // tm,),
        in_specs=[pl.BlockSpec((tm, k), lambda i: (i, 0)), _resident(w.shape),
                  pl.BlockSpec((tm, D), lambda i: (i, 0)), pl.BlockSpec((1, D), lambda i: (0, 0)),
                  pl.BlockSpec((tm, D), lambda i: (i, 0))] + _token_spec(token),
        out_specs=[pl.BlockSpec((tm, D), lambda i: (i, 0)), pl.BlockSpec((1, D), lambda i: (0, 0))],
        out_shape=[_sds((t, D), f32), _sds((1, D), f32)],
        compiler_params=_cparams(1))(da, w, x, g, dxo, *_token_arg(token))


def mlp_fwd(x, g_pre, wu, wd, g_post, tm, name):
    t = x.shape[0]
    nq, _, fc = wu.shape

    def body(x_ref, gp_ref, wu_ref, wd_ref, gq_ref, fp_ref, h_ref, o_ref, xo_ref):
        xv = x_ref[...]
        h = _rms_fwd(xv, gp_ref[...]).astype(bf16)
        h_ref[...] = h
        o = None
        for q in range(nq):
            fq = _dot(h, wu_ref[q])
            fp_ref[:, q * fc:(q + 1) * fc] = fq.astype(bf16)
            r = jnp.maximum(fq, 0.0)
            part = _dot((r * r).astype(bf16), wd_ref[q * fc:(q + 1) * fc, :])
            o = part if o is None else o + part
        o_ref[...] = o.astype(bf16)
        xo_ref[...] = xv + _rms_fwd(o, gq_ref[...])

    row = lambda c: pl.BlockSpec((tm, c), lambda i: (i, 0))
    vec = pl.BlockSpec((1, D), lambda i: (0, 0))
    return pl.pallas_call(
        body, name=name, grid=(t // tm,),
        in_specs=[row(D), vec, _resident(wu.shape), _resident(wd.shape), vec],
        out_specs=[row(nq * fc), row(D), row(D), row(D)],
        out_shape=[_sds((t, nq * fc), bf16), _sds((t, D), bf16), _sds((t, D), bf16), _sds((t, D), f32)],
        compiler_params=_cparams(1))(x, g_pre, wu, wd, g_post)


def mlp_bwd(o, g_post, dxo, wd, fp, wu, x, g_pre, tm, name):
    t = x.shape[0]
    nq, _, fc = wu.shape

    def body(o_ref, gq_ref, dxo_ref, wd_ref, fp_ref, wu_ref, x_ref, gp_ref, do_ref, dfp_ref, dx_ref, dgq_ref, dgp_ref):
        i = pl.program_id(0)
        dxo_v = dxo_ref[...]
        do, dgq = _rms_bwd(o_ref[...].astype(f32), gq_ref[...], dxo_v)
        do_b = do.astype(bf16)
        do_ref[...] = do_b
        dh = None
        for q in range(nq):
            cols = slice(q * fc, (q + 1) * fc)
            dq = _dot_nt(do_b, wd_ref[cols, :]) * (2.0 * jnp.maximum(fp_ref[:, cols].astype(f32), 0.0))
            dq_b = dq.astype(bf16)
            dfp_ref[:, cols] = dq_b
            part = _dot_nt(dq_b, wu_ref[q])
            dh = part if dh is None else dh + part
        dxn, dgp = _rms_bwd(x_ref[...], gp_ref[...], dh)
        dx_ref[...] = dxo_v + dxn
        _accum(dgq_ref, jnp.sum(dgq, axis=0, keepdims=True), i == 0)
        _accum(dgp_ref, jnp.sum(dgp, axis=0, keepdims=True), i == 0)

    row = lambda c: pl.BlockSpec((tm, c), lambda i: (i, 0))
    vec = pl.BlockSpec((1, D), lambda i: (0, 0))
    return pl.pallas_call(
        body, name=name, grid=(t // tm,),
        in_specs=[row(D), vec, row(D), _resident(wd.shape), row(nq * fc), _resident(wu.shape), row(D), vec],
        out_specs=[row(D), row(nq * fc), row(D), vec, vec],
        out_shape=[_sds((t, D), bf16), _sds((t, nq * fc), bf16), _sds((t, D), f32), _sds((1, D), f32), _sds((1, D), f32)],
        compiler_params=_cparams(1))(o, g_post, dxo, wd, fp, wu, x, g_pre)


def matmul_tn(a, b, tm, tn, relu2, name, col_blocks=False):
    t, m = a.shape
    n = b.shape[1]
    if col_blocks:
        out_spec, out_shape = pl.BlockSpec((None, tm, tn), lambda i, j: (j, i, 0)), _sds((n // tn, m, tn), bf16)
    else:
        out_spec, out_shape = pl.BlockSpec((tm, tn), lambda i, j: (i, j)), _sds((m, n), bf16)

    def body(a_ref, b_ref, o_ref, at_ref):
        @pl.when(pl.program_id(1) == 0)
        def _():
            av = a_ref[...]
            if relu2:
                af = jnp.maximum(av.astype(f32), 0.0)
                av = (af * af).astype(bf16)
            at_ref[...] = av.T

        o_ref[...] = _dot(at_ref[...], b_ref[...]).astype(bf16)

    return pl.pallas_call(
        body, name=name, grid=(m // tm, n // tn),
        in_specs=[pl.BlockSpec((t, tm), lambda i, j: (0, i)), pl.BlockSpec((t, tn), lambda i, j: (0, j))],
        out_specs=out_spec, out_shape=out_shape,
        scratch_shapes=[pltpu.VMEM((tm, t), bf16)],
        compiler_params=_cparams(2))(a, b)


ROWS_A = 16
ROWS_B = 32
UNROLL = 4
UNROLL_NORM = 8


def _past(win, s):
    return (win if s == 0 else pltpu.roll(win, s, 0))[HALO:]


def _future(win, s):
    n = win.shape[0]
    return (win if s == 0 else pltpu.roll(win, n - s, 0))[:n - HALO]


def _fold8(v):
    return v.reshape(v.shape[0] // 8, 8, v.shape[1]).sum(axis=0)


def _last8(ref):
    return ref[...].astype(f32)[HBLK - HALO:]


def _first8(ref):
    return ref[...].astype(f32)[:HALO]


def _rd(ref, rows):
    return ref[rows, :].astype(f32)


def _halo_prev(tb, col):
    return lambda i: (jnp.maximum(i * (tb // HBLK) - 1, 0), col)


def _halo_next(tb, col, t):
    return lambda i: (jnp.minimum((i + 1) * (tb // HBLK), t // HBLK - 1), col)


def group_a_fwd(proj, wa, g, seq, tb, name):
    t = proj.shape[0]
    bps = seq // tb

    def body(xa_ref, ca_ref, ba_ref, xah_ref, cah_ref, wa_ref, g_ref, o_ref, u_scr):
        first = (pl.program_id(0) % bps) == 0
        u_scr[0:HALO, :] = jnp.where(first, 0.0, _last8(cah_ref) * _last8(xah_ref))
        w, gv = wa_ref[...], g_ref[...]

        def chunk(i, carry):
            r = pl.multiple_of(i * ROWS_A, ROWS_A)
            rows = pl.ds(r, ROWS_A)
            u_scr[pl.ds(pl.multiple_of(HALO + r, HALO), ROWS_A), :] = _rd(ca_ref, rows) * _rd(xa_ref, rows)
            win = u_scr[pl.ds(r, ROWS_A + HALO), :]
            cv = w[2:3] * _past(win, 0) + w[1:2] * _past(win, 1) + w[0:1] * _past(win, 2)
            o_ref[rows, :] = _rms_fwd(_rd(ba_ref, rows) * cv, gv).astype(bf16)
            return carry

        lax.fori_loop(0, tb // ROWS_A, chunk, 0, unroll=UNROLL_NORM)

    blk = lambda c: pl.BlockSpec((tb, D), lambda i: (i, c))
    return pl.pallas_call(
        body, name=name, grid=(t // tb,),
        in_specs=[blk(0), blk(1), blk(2),
                  pl.BlockSpec((HBLK, D), _halo_prev(tb, 0)), pl.BlockSpec((HBLK, D), _halo_prev(tb, 1)),
                  pl.BlockSpec((8, D), lambda i: (0, 0)), pl.BlockSpec((1, D), lambda i: (0, 0))],
        out_specs=pl.BlockSpec((tb, D), lambda i: (i, 0)),
        out_shape=_sds((t, 2 * D), bf16),
        scratch_shapes=[pltpu.VMEM((tb + HALO, D), f32)],
        compiler_params=_cparams(1))(proj, proj, proj, proj, proj, wa, g)


def group_a_bwd(proj, dcat, wa, g, seq, tb, name, token=None):
    t = proj.shape[0]
    bps = seq // tb

    def body(xa_ref, ca_ref, ba_ref, dy_ref, xap_ref, cap_ref, xan_ref, can_ref, ban_ref, dyn_ref, wa_ref, g_ref,
             *rest):
        dp_ref, dwa_ref, dg_ref, u_scr, d_scr, acc_scr = rest[-6:]
        i = pl.program_id(0)
        first = (i % bps) == 0
        last = (i % bps) == bps - 1
        w = wa_ref[...]
        gv = g_ref[...]
        u_scr[0:HALO, :] = jnp.where(first, 0.0, _last8(cap_ref) * _last8(xap_ref))
        u_scr[HALO + tb:2 * HALO + tb, :] = _first8(can_ref) * _first8(xan_ref)
        acc_scr[...] = jnp.zeros_like(acc_scr)

        def forward_part(n, carry):
            r = pl.multiple_of(n * ROWS_A, ROWS_A)
            rows = pl.ds(r, ROWS_A)
            ba = _rd(ba_ref, rows)
            u_scr[pl.ds(pl.multiple_of(HALO + r, HALO), ROWS_A), :] = _rd(ca_ref, rows) * _rd(xa_ref, rows)
            win = u_scr[pl.ds(r, ROWS_A + HALO), :]
            u = [_past(win, s) for s in range(3)]
            cv = w[2:3] * u[0] + w[1:2] * u[1] + w[0:1] * u[2]
            dya, dgc = _rms_bwd(ba * cv, gv, _rd(dy_ref, rows))
            dcv = dya * ba
            d_scr[rows, :] = dcv
            dp_ref[rows, 2 * D:3 * D] = (dya * cv).astype(bf16)
            acc_scr[0:8, :] += _fold8(dgc)
            for k in range(3):
                acc_scr[8 + 8 * k:16 + 8 * k, :] += _fold8(dcv * u[2 - k])
            return carry

        lax.fori_loop(0, tb // ROWS_A, forward_part, 0, unroll=UNROLL_NORM)

        start = HALO + tb
        cvn = (w[2:3] * u_scr[pl.ds(start, HALO), :] + w[1:2] * u_scr[pl.ds(start - 1, HALO), :]
               + w[0:1] * u_scr[pl.ds(start - 2, HALO), :])
        ban = _first8(ban_ref)
        dyan, _ = _rms_bwd(ban * cvn, gv, _first8(dyn_ref))
        d_scr[tb:tb + HALO, :] = jnp.where(last, 0.0, dyan * ban)

        def backward_part(n, carry):
            r = pl.multiple_of(n * ROWS_A, ROWS_A)
            rows = pl.ds(r, ROWS_A)
            win = d_scr[pl.ds(r, ROWS_A + HALO), :]
            du = w[2:3] * _future(win, 0) + w[1:2] * _future(win, 1) + w[0:1] * _future(win, 2)
            dp_ref[rows, 0:D] = (du * _rd(ca_ref, rows)).astype(bf16)
            dp_ref[rows, D:2 * D] = (du * _rd(xa_ref, rows)).astype(bf16)
            return carry

        lax.fori_loop(0, tb // ROWS_A, backward_part, 0, unroll=UNROLL)

        row = lax.broadcasted_iota(jnp.int32, (8, D), 0)
        dw = jnp.zeros((8, D), f32)
        for k in range(3):
            dw = jnp.where(row == k, jnp.sum(acc_scr[8 + 8 * k:16 + 8 * k, :], axis=0, keepdims=True), dw)
        _accum(dwa_ref, dw, i == 0)
        _accum(dg_ref, jnp.sum(acc_scr[0:8, :], axis=0, keepdims=True), i == 0)

    blk = lambda c: pl.BlockSpec((tb, D), lambda i: (i, c))
    prv = lambda c: pl.BlockSpec((HBLK, D), _halo_prev(tb, c))
    nxt = lambda c: pl.BlockSpec((HBLK, D), _halo_next(tb, c, t))
    return pl.pallas_call(
        body, name=name, grid=(t // tb,),
        in_specs=[blk(0), blk(1), blk(2), blk(0), prv(0), prv(1), nxt(0), nxt(1), nxt(2), nxt(0),
                  pl.BlockSpec((8, D), lambda i: (0, 0)), pl.BlockSpec((1, D), lambda i: (0, 0))] + _token_spec(token),
        out_specs=[pl.BlockSpec((tb, 3 * D), lambda i: (i, 0)), pl.BlockSpec((8, D), lambda i: (0, 0)),
                   pl.BlockSpec((1, D), lambda i: (0, 0))],
        out_shape=[_sds((t, PROJ), bf16), _sds((8, D), f32), _sds((1, D), f32)],
        scratch_shapes=[pltpu.VMEM((tb + 2 * HALO, D), f32), pltpu.VMEM((tb + HALO, D), f32), pltpu.VMEM((32, D), f32)],
        compiler_params=_cparams(1))(proj, proj, proj, dcat, proj, proj, proj, proj, proj, dcat, wa, g,
                                     *_token_arg(token))


CB = 512
XBC_BLK0 = COL_XBC // CB


def conv_b_fwd(proj, ws, bs, seq, tb, name):
    t = proj.shape[0]
    bps = seq // tb

    def body(x_ref, xp_ref, w_ref, b_ref, o_ref, da_ref, x_scr):
        first = (pl.program_id(1) % bps) == 0
        x_scr[0:HALO, :] = jnp.where(first, 0.0, _last8(xp_ref))
        w, bias = w_ref[...], b_ref[...]

        def chunk(n, carry):
            r = pl.multiple_of(n * ROWS_B, ROWS_B)
            rows = pl.ds(r, ROWS_B)
            x_scr[pl.ds(pl.multiple_of(HALO + r, HALO), ROWS_B), :] = _rd(x_ref, rows)
            win = x_scr[pl.ds(r, ROWS_B + HALO), :]
            xc = bias + w[3:4] * _past(win, 0)
            for k in range(3):
                xc = xc + w[k:k + 1] * _past(win, 3 - k)
            sg = _sigmoid(xc)
            o_ref[rows, :] = xc * sg
            da_ref[rows, :] = (sg * (1.0 + xc * (1.0 - sg))).astype(bf16)
            return carry

        lax.fori_loop(0, tb // ROWS_B, chunk, 0, unroll=UNROLL)

    return pl.pallas_call(
        body, name=name, grid=(XBC // CB, t // tb),
        in_specs=[pl.BlockSpec((tb, CB), lambda j, i: (i, XBC_BLK0 + j)),
                  pl.BlockSpec((HBLK, CB), lambda j, i: (jnp.maximum(i * (tb // HBLK) - 1, 0), XBC_BLK0 + j)),
                  pl.BlockSpec((8, CB), lambda j, i: (0, j)), pl.BlockSpec((1, CB), lambda j, i: (0, j))],
        out_specs=[pl.BlockSpec((tb, CB), lambda j, i: (i, j)), pl.BlockSpec((tb, CB), lambda j, i: (i, j))],
        out_shape=[_sds((t, XBC), f32), _sds((t, XBC), bf16)],
        scratch_shapes=[pltpu.VMEM((tb + HALO, CB), f32)],
        compiler_params=_cparams(2))(proj, proj, ws, bs)


def conv_b_bwd(proj, dxs, dact, ws, dproj, seq, tb, name):
    t = proj.shape[0]
    bps = seq // tb

    def body(x_ref, xp_ref, d_ref, dn_ref, a_ref, an_ref, w_ref, dproj_ref, dx_ref, dw_ref, db_ref, x_scr, d_scr,
             acc_scr):
        i = pl.program_id(1)
        first = (i % bps) == 0
        last = (i % bps) == bps - 1
        w = w_ref[...]
        x_scr[0:HALO, :] = jnp.where(first, 0.0, _last8(xp_ref))
        acc_scr[...] = jnp.zeros_like(acc_scr)

        def forward_part(n, carry):
            r = pl.multiple_of(n * ROWS_B, ROWS_B)
            rows = pl.ds(r, ROWS_B)
            x_scr[pl.ds(pl.multiple_of(HALO + r, HALO), ROWS_B), :] = _rd(x_ref, rows)
            win = x_scr[pl.ds(r, ROWS_B + HALO), :]
            dxc = _rd(d_ref, rows) * _rd(a_ref, rows)
            d_scr[rows, :] = dxc
            acc_scr[0:8, :] += _fold8(dxc)
            for k in range(4):
                acc_scr[8 + 8 * k:16 + 8 * k, :] += _fold8(dxc * _past(win, 3 - k))
            return carry

        lax.fori_loop(0, tb // ROWS_B, forward_part, 0, unroll=UNROLL)
        d_scr[tb:tb + HALO, :] = jnp.where(last, 0.0, _first8(dn_ref) * _first8(an_ref))

        def backward_part(n, carry):
            r = pl.multiple_of(n * ROWS_B, ROWS_B)
            win = d_scr[pl.ds(r, ROWS_B + HALO), :]
            dx = w[3:4] * _future(win, 0)
            for k in range(3):
                dx = dx + w[k:k + 1] * _future(win, 3 - k)
            dx_ref[pl.ds(r, ROWS_B), :] = dx.astype(bf16)
            return carry

        lax.fori_loop(0, tb // ROWS_B, backward_part, 0, unroll=UNROLL)

        row = lax.broadcasted_iota(jnp.int32, (8, CB), 0)
        dw = jnp.zeros((8, CB), f32)
        for k in range(4):
            dw = jnp.where(row == k, jnp.sum(acc_scr[8 + 8 * k:16 + 8 * k, :], axis=0, keepdims=True), dw)
        _accum(dw_ref, dw, i == 0)
        _accum(db_ref, jnp.sum(acc_scr[0:8, :], axis=0, keepdims=True), i == 0)

    nh = t // HBLK
    nxt = pl.BlockSpec((HBLK, CB), lambda j, i: (jnp.minimum((i + 1) * (tb // HBLK), nh - 1), j))
    cur = pl.BlockSpec((tb, CB), lambda j, i: (i, j))
    return pl.pallas_call(
        body, name=name, grid=(XBC // CB, t // tb),
        in_specs=[pl.BlockSpec((tb, CB), lambda j, i: (i, XBC_BLK0 + j)),
                  pl.BlockSpec((HBLK, CB), lambda j, i: (jnp.maximum(i * (tb // HBLK) - 1, 0), XBC_BLK0 + j)),
                  cur, nxt, cur, nxt, pl.BlockSpec((8, CB), lambda j, i: (0, j)), pl.BlockSpec(memory_space=pl.ANY)],
        out_specs=[pl.BlockSpec((tb, CB), lambda j, i: (i, XBC_BLK0 + j)), pl.BlockSpec((8, CB), lambda j, i: (0, j)),
                   pl.BlockSpec((1, CB), lambda j, i: (0, j))],
        out_shape=[_sds((t, PROJ), bf16), _sds((8, XBC), f32), _sds((1, XBC), f32)],
        input_output_aliases={7: 0},
        scratch_shapes=[pltpu.VMEM((tb + HALO, CB), f32), pltpu.VMEM((tb + HALO, CB), f32), pltpu.VMEM((40, CB), f32)],
        compiler_params=_cparams(2))(proj, proj, dxs, dxs, dact, dact, ws, dproj)


def place_columns(buf, part, col_block, tb, name):
    t, wdt = part.shape

    def body(p_ref, buf_ref, o_ref):
        o_ref[...] = p_ref[...]

    return pl.pallas_call(
        body, name=name, grid=(t // tb,),
        in_specs=[pl.BlockSpec((tb, wdt), lambda i: (i, 0)), pl.BlockSpec(memory_space=pl.ANY)],
        out_specs=pl.BlockSpec((tb, wdt), lambda i: (i, col_block)), out_shape=_sds(buf.shape, buf.dtype),
        input_output_aliases={1: 0}, compiler_params=_cparams(1))(part, buf)


GW = D // NG
EXPAND_TERMS = 2
REDUCE_TERMS = 1


def _ssd_consts():
    head_of_lane = jnp.arange(D) // HP
    expand = (jnp.arange(CH)[:, None] == head_of_lane[None, :]).astype(bf16)
    tri = (jnp.arange(CH)[:, None] >= jnp.arange(CH)[None, :]).astype(f32)
    return expand, tri


def _ssd_common(par_ref, dtr_ref, e_ref, tri_ref):
    par = par_ref[...]
    dtb, alog, dsk = par[0:1], par[1:2], par[2:3]
    lane = lax.broadcasted_iota(jnp.int32, (CH, CH), 1)
    a = -jnp.exp(alog)
    dtr = dtr_ref[...].astype(f32) + dtb
    sp = jnp.maximum(dtr, 0.0) + jnp.log(1.0 + jnp.exp(-jnp.abs(dtr)))
    dt = jnp.where(lane < NH, sp, 0.0)
    cs = jnp.dot(tri_ref[...], dt * a, precision=lax.Precision.HIGHEST, preferred_element_type=f32)
    cs_last = cs[CH - 1:CH, :]
    dte = jnp.exp(cs_last - cs)
    ecs = jnp.exp(cs)
    ecl = jnp.exp(cs_last)
    e = e_ref[...]
    row8 = lax.broadcasted_iota(jnp.int32, (8, CH), 0)
    r8 = _split_dot(jnp.where(row8 == 0, ecl, jnp.where(row8 == 1, dsk, 0.0)), e, 3)
    return dict(a=a, dtr=dtr, dt=dt, cs=cs, cst=cs.T, dte=dte, ecs=ecs, ecl=ecl, e=e, lane=lane,
                dt_x=_split_dot(dt, e, EXPAND_TERMS), dte_x=_split_dot(dte, e, EXPAND_TERMS),
                ecs_x=_split_dot(ecs, e, EXPAND_TERMS),
                ecl_x=r8[0:1], dsk_x=r8[1:2])


def _decay_matrix(c, h):
    li = lax.broadcasted_iota(jnp.int32, (CH, CH), 0)
    seg = c["cs"][:, h:h + 1] - c["cst"][h:h + 1, :]
    return jnp.exp(jnp.where(li >= c["lane"], seg, -jnp.inf))


def _gate_norm_fwd(y, z, gs):
    zg = z * _sigmoid(z)
    yg = y * zg
    return jnp.concatenate([_rms_fwd(yg[:, k * GW:(k + 1) * GW], gs[:, k * GW:(k + 1) * GW]) for k in range(NG)], axis=1)


def ssd_fwd(xbcs, proj, par, gs, cat, seq, name):
    t = xbcs.shape[0]
    nc = seq // CH
    expand, tri = _ssd_consts()

    def body(xs_ref, b_ref, c_ref, dtr_ref, z_ref, par_ref, e_ref, tri_ref, gs_ref, cat_ref, yn_ref, y_ref, st_ref,
             p_scr, yd_scr):
        @pl.when(pl.program_id(0) % nc == 0)
        def _():
            p_scr[...] = jnp.zeros_like(p_scr)

        c = _ssd_common(par_ref, dtr_ref, e_ref, tri_ref)
        xs = xs_ref[...]
        xdt = xs * c["dt_x"]
        xdt_b = xdt.astype(bf16)
        xdte_b = (xdt * c["dte_x"]).astype(bf16)
        p = p_scr[...]
        st_ref[0] = p
        p_b = p.astype(bf16)
        lo = c["lane"] < HP
        for g in range(NG):
            bg = b_ref[:, g * NS:(g + 1) * NS].astype(bf16)
            cg = c_ref[:, g * NS:(g + 1) * NS].astype(bf16)
            gmat = _dot_nt(cg, bg)
            for q in range(GW // CH):
                col = g * GW + q * CH
                xp = xdt_b[:, col:col + CH]
                h0 = col // HP
                m0 = (gmat * _decay_matrix(c, h0)).astype(bf16)
                m1 = (gmat * _decay_matrix(c, h0 + 1)).astype(bf16)
                stacked = jnp.concatenate([jnp.where(lo, xp, jnp.zeros_like(xp)),
                                           jnp.where(lo, jnp.zeros_like(xp), xp)], axis=0)
                yd_scr[:, col:col + CH] = _dot(jnp.concatenate([m0, m1], axis=1), stacked)
            gsl = slice(g * GW, (g + 1) * GW)
            yoff = _dot(cg, p_b[:, gsl]) * c["ecs_x"][:, gsl]
            yd_scr[:, gsl] = yd_scr[:, gsl] + yoff
            p_scr[:, gsl] = p[:, gsl] * c["ecl_x"][:, gsl] + _dot_tn(bg, xdte_b[:, gsl])
        y = yd_scr[...] + c["dsk_x"] * xs
        y_ref[...] = y
        yn_ref[...] = _gate_norm_fwd(y, z_ref[...].astype(f32), gs_ref[...]).astype(bf16)

    nb = t // CH
    return pl.pallas_call(
        body, name=name, grid=(nb,),
        in_specs=[pl.BlockSpec((CH, D), lambda i: (i, 0)),
                  pl.BlockSpec((CH, NG * NS), lambda i: (i, D // (NG * NS))),
                  pl.BlockSpec((CH, NG * NS), lambda i: (i, D // (NG * NS) + 1)),
                  pl.BlockSpec((CH, CH), lambda i: (i, COL_DT // CH)),
                  pl.BlockSpec((CH, D), lambda i: (i, COL_Z // D)),
                  pl.BlockSpec((8, CH), lambda i: (0, 0)), pl.BlockSpec((CH, D), lambda i: (0, 0)),
                  pl.BlockSpec((CH, CH), lambda i: (0, 0)), pl.BlockSpec((1, D), lambda i: (0, 0)),
                  pl.BlockSpec(memory_space=pl.ANY)],
        out_specs=[pl.BlockSpec((CH, D), lambda i: (i, 1)), pl.BlockSpec((CH, D), lambda i: (i, 0)),
                   pl.BlockSpec((1, NS, D), lambda i: (i, 0, 0))],
        out_shape=[_sds((t, 2 * D), bf16), _sds((t, D), f32), _sds((nb, NS, D), f32)],
        input_output_aliases={9: 0},
        scratch_shapes=[pltpu.VMEM((NS, D), f32), pltpu.VMEM((CH, D), f32)],
        compiler_params=_cparams(1))(xbcs, xbcs, xbcs, proj, proj, par, expand, tri, gs, cat)


def ssd_bwd(xbcs, proj, ypre, states, dcat, par, gs, dproj, seq, name):
    t = xbcs.shape[0]
    nc = seq // CH
    expand, tri = _ssd_consts()

    def body(xs_ref, b_ref, c_ref, dtr_ref, z_ref, y_ref, st_ref, dyn_ref, par_ref, e_ref, tri_ref, gs_ref, dproj_ref,
             dx_ref, dz_ref, ddt_ref, dpar_ref, dgs_ref, dp_scr, dxdt_scr):
        i = pl.program_id(0)

        @pl.when(i % nc == 0)
        def _():
            dp_scr[...] = jnp.zeros_like(dp_scr)

        c = _ssd_common(par_ref, dtr_ref, e_ref, tri_ref)
        e = c["e"]
        lane = c["lane"]
        sub = lax.broadcasted_iota(jnp.int32, (CH, CH), 0)
        xs = xs_ref[...]
        xdt = xs * c["dt_x"]
        xdt_b = xdt.astype(bf16)
        xdte_b = (xdt * c["dte_x"]).astype(bf16)
        p = st_ref[0]
        p_b = p.astype(bf16)
        dpn = dp_scr[...]
        dpn_b = dpn.astype(bf16)

        y, z, gs_v = y_ref[...], z_ref[...].astype(f32), gs_ref[...]
        zs = _sigmoid(z)
        zg = z * zs
        yg = y * zg
        parts, gparts = [], []
        for k in range(NG):
            sl = slice(k * GW, (k + 1) * GW)
            dxk, dgk = _rms_bwd(yg[:, sl], gs_v[:, sl], dyn_ref[:, sl].astype(f32))
            parts.append(dxk)
            gparts.append(dgk)
        dyg = jnp.concatenate(parts, axis=1)
        dgs_rows = jnp.concatenate(gparts, axis=1)
        dy = dyg * zg
        dz_ref[...] = (dyg * y * (zs * (1.0 + z * (1.0 - zs)))).astype(bf16)
        dy_b = dy.astype(bf16)
        dq_b = (dy * c["ecs_x"]).astype(bf16)

        lo = lane < HP
        dcs = jnp.zeros((CH, CH), f32)
        dcst = jnp.zeros((CH, CH), f32)
        for g in range(NG):
            gsl = slice(g * GW, (g + 1) * GW)
            bg = b_ref[:, g * NS:(g + 1) * NS].astype(bf16)
            cg = c_ref[:, g * NS:(g + 1) * NS].astype(bf16)
            gmat = _dot_nt(cg, bg)
            dgm = jnp.zeros((CH, CH), f32)
            for q in range(GW // CH):
                col = g * GW + q * CH
                xp = xdt_b[:, col:col + CH]
                dyp = dy_b[:, col:col + CH]
                zero = jnp.zeros_like(dyp)
                xp2 = jnp.concatenate([jnp.where(lo, xp, zero), jnp.where(lo, zero, xp)], axis=0)
                dy2 = jnp.concatenate([jnp.where(lo, dyp, zero), jnp.where(lo, zero, dyp)], axis=0)
                dm2 = _dot_nt(dyp, xp2)
                ms = []
                for hh in range(2):
                    h = col // HP + hh
                    dec = _decay_matrix(c, h)
                    m = gmat * dec
                    dm = dm2[:, hh * CH:(hh + 1) * CH]
                    dseg = dm * m
                    dcs = dcs + jnp.where(lane == h, jnp.sum(dseg, axis=1, keepdims=True), 0.0)
                    dcst = dcst + jnp.where(sub == h, jnp.sum(dseg, axis=0, keepdims=True), 0.0)
                    dgm = dgm + dm * dec
                    ms.append(m.astype(bf16))
                dxdt_scr[:, col:col + CH] = _dot_tn(jnp.concatenate(ms, axis=0), dy2)
            dgm_b = dgm.astype(bf16)
            bds = _dot(bg, dpn_b[:, gsl])
            dxdt_scr[:, gsl] = dxdt_scr[:, gsl] + c["dte_x"][:, gsl] * bds
            dc_g = _dot(dgm_b, bg) + _dot_nt(dq_b[:, gsl], p_b[:, gsl])
            db_g = _dot_tn(dgm_b, cg) + _dot_nt(xdte_b[:, gsl], dpn_b[:, gsl])
            dx_ref[:, D + g * NS:D + (g + 1) * NS] = db_g
            dx_ref[:, D + NG * NS + g * NS:D + NG * NS + (g + 1) * NS] = dc_g
            dp_scr[:, gsl] = dpn[:, gsl] * c["ecl_x"][:, gsl] + _dot_tn(cg, dq_b[:, gsl])
            q_g = _dot(cg, p_b[:, gsl])
            e_g = e[:, gsl]
            dcs = dcs + c["ecs"] * _split_dot(dy[:, gsl] * q_g, e_g, REDUCE_TERMS, nt=True)
            ddte = _split_dot(xdt[:, gsl] * bds, e_g, REDUCE_TERMS, nt=True) * c["dte"]
            dcs = dcs - ddte
            dcs = dcs + jnp.where(sub == CH - 1, jnp.sum(ddte, axis=0, keepdims=True), 0.0)

        decl = _split_dot(jnp.broadcast_to(jnp.sum(dpn * p, axis=0, keepdims=True), (8, D)), e, 2, nt=True)[0:1]
        dcs = dcs + jnp.where(sub == CH - 1, c["ecl"] * decl, 0.0)
        dcs = dcs - dcst.T
        dadt = lax.dot_general(tri_ref[...], dcs, (((0,), (0,)), ((), ())), precision=lax.Precision.HIGHEST,
                               preferred_element_type=f32)
        dxdt = dxdt_scr[...]
        ddt = dadt * c["a"] + _split_dot(dxdt * xs, e, REDUCE_TERMS, nt=True)
        ddtr = jnp.where(lane < NH, ddt * _sigmoid(c["dtr"]), 0.0)
        ddt_ref[...] = ddtr.astype(bf16)
        dx_ref[:, 0:D] = dxdt * c["dt_x"] + c["dsk_x"] * dy
        dsk = _split_dot(jnp.broadcast_to(jnp.sum(dy * xs, axis=0, keepdims=True), (8, D)), e, 2, nt=True)[0:1]
        dalog = jnp.sum(dadt * c["dt"], axis=0, keepdims=True) * c["a"]
        row8 = lax.broadcasted_iota(jnp.int32, (8, CH), 0)
        dpar = jnp.where(row8 == 0, jnp.sum(ddtr, axis=0, keepdims=True),
                         jnp.where(row8 == 1, dalog, jnp.where(row8 == 2, dsk, 0.0)))
        dpar = jnp.where(lax.broadcasted_iota(jnp.int32, (8, CH), 1) < NH, dpar, 0.0)
        _accum(dpar_ref, dpar, i == 0)
        _accum(dgs_ref, jnp.sum(dgs_rows, axis=0, keepdims=True), i == 0)

    nb = t // CH
    rev = lambda i: (i // nc) * nc + (nc - 1 - i % nc)
    return pl.pallas_call(
        body, name=name, grid=(nb,),
        in_specs=[pl.BlockSpec((CH, D), lambda i: (rev(i), 0)),
                  pl.BlockSpec((CH, NG * NS), lambda i: (rev(i), D // (NG * NS))),
                  pl.BlockSpec((CH, NG * NS), lambda i: (rev(i), D // (NG * NS) + 1)),
                  pl.BlockSpec((CH, CH), lambda i: (rev(i), COL_DT // CH)),
                  pl.BlockSpec((CH, D), lambda i: (rev(i), COL_Z // D)),
                  pl.BlockSpec((CH, D), lambda i: (rev(i), 0)),
                  pl.BlockSpec((1, NS, D), lambda i: (rev(i), 0, 0)),
                  pl.BlockSpec((CH, D), lambda i: (rev(i), 1)),
                  pl.BlockSpec((8, CH), lambda i: (0, 0)), pl.BlockSpec((CH, D), lambda i: (0, 0)),
                  pl.BlockSpec((CH, CH), lambda i: (0, 0)), pl.BlockSpec((1, D), lambda i: (0, 0)),
                  pl.BlockSpec(memory_space=pl.ANY)],
        out_specs=[pl.BlockSpec((CH, XBC), lambda i: (rev(i), 0)), pl.BlockSpec((CH, D), lambda i: (rev(i), COL_Z // D)),
                   pl.BlockSpec((CH, CH), lambda i: (rev(i), 0)),
                   pl.BlockSpec((8, CH), lambda i: (0, 0)), pl.BlockSpec((1, D), lambda i: (0, 0))],
        out_shape=[_sds((t, XBC), f32), _sds((t, PROJ), bf16), _sds((t, CH), bf16), _sds((8, CH), f32), _sds((1, D), f32)],
        input_output_aliases={12: 1},
        scratch_shapes=[pltpu.VMEM((NS, D), f32), pltpu.VMEM((CH, D), f32)],
        compiler_params=_cparams(1))(xbcs, xbcs, xbcs, proj, proj, ypre, states, dcat, par, expand, tri, gs, dproj)


def loss_head(y, target, tb, name):
    t = y.shape[0]

    def body(y_ref, t_ref, s_ref, dy_ref):
        err = y_ref[...] - t_ref[...]
        dy_ref[...] = err * (1.0 / D)
        _accum(s_ref, jnp.zeros((8, CH), f32) + jnp.sum(err * err), pl.program_id(0) == 0)

    return pl.pallas_call(
        body, name=name, grid=(t // tb,),
        in_specs=[pl.BlockSpec((tb, D), lambda i: (i, 0)), pl.BlockSpec((tb, D), lambda i: (i, 0))],
        out_specs=[pl.BlockSpec((8, CH), lambda i: (0, 0)), pl.BlockSpec((tb, D), lambda i: (i, 0))],
        out_shape=[_sds((8, CH), f32), _sds((t, D), f32)],
        compiler_params=_cparams(1))(y, target)


def _tiles(t, seq):
    tm = min(512, t)
    return dict(tm=tm, tm_small=min(256, t), tm_large=min(1024, t), tm_huge=min(2048, t), tb=min(512, seq))


def local_step(x, target, depth, weights_of, seq, grads_done=None):
    t = x.shape[0]
    ts = _tiles(t, seq)
    tm, tl, th, tb = ts["tm"], ts["tm_large"], ts["tm_huge"], ts["tb"]
    saved, ws = [], []
    for l in range(depth):
        w = weights_of(l, x)
        ws.append(w)
        proj, h1 = norm_matmul(x, w["g1"], w["win"], th, 1920, bf16, "in_proj", token=w.get("token"))
        cat = group_a_fwd(proj, w["wa"], w["ga"], seq, tb, "group_a_fwd")
        xbcs, dact = conv_b_fwd(proj, w["ws"], w["bs"], seq, tb, "conv_b_fwd")
        cat, ypre, states = ssd_fwd(xbcs, proj, w["par"], w["gs"], cat, seq, "ssd_fwd")
        if "late" in w:
            w.update(w.pop("late")(cat))
        mix, x2 = matmul_postnorm(cat, w["wo"], x, w["g2"], tl, "out_proj")
        fp, h2, o, x3 = mlp_fwd(x2, w["g3"], w["wu"], w["wd"], w["g4"], tm, "mlp_fwd")
        saved.append(dict(x=x, proj=proj, h1=h1, xbcs=xbcs, dact=dact, ypre=ypre, states=states, cat=cat, mix=mix, x2=x2,
                          fp=fp, h2=h2, o=o))
        x = x3
    sse, dx = loss_head(x, target, tm, "loss_head")
    grads = [None] * depth
    for l in reversed(range(depth)):
        s, w = saved[l], ws[l]
        do, dfp, dx2, dg4, dg3 = mlp_bwd(s["o"], w["g4"], dx, w["wd"], s["fp"], w["wu"], s["x2"], w["g3"],
                                         tm, "mlp_bwd")
        dwd = matmul_tn(s["fp"], do, 512, 1024, True, "mlp_down_dw")
        dwu = matmul_tn(s["h2"], dfp, tl, 1024, False, "mlp_up_dw", col_blocks=True)
        dmix, dg2, dcat = postnorm_bwd_matmul(s["mix"], w["g2"], dx2, w["wo"], tl, 2 * D, "out_proj_bwd")
        dwo = matmul_tn(s["cat"], dmix, 512, 1024, False, "out_proj_dw")
        token = None if grads_done is None else grads_done(l, dict(wo=dwo, wu=dwu, wd=dwd), False)
        dproj, dwa, dga = group_a_bwd(s["proj"], dcat, w["wa"], w["ga"], seq, tb, "group_a_bwd", token=token)
        dxbcs, dproj, ddt, dpar, dgs = ssd_bwd(s["xbcs"], s["proj"], s["ypre"], s["states"], dcat, w["par"], w["gs"],
                                               dproj, seq, "ssd_bwd")
        dproj, dws, dbs = conv_b_bwd(s["proj"], dxbcs, s["dact"], w["ws"], dproj, seq, tb, "conv_b_bwd")
        dproj = place_columns(dproj, ddt, COL_DT // CH, tm, "place_ddt")
        dwin = matmul_tn(s["h1"], dproj, tl, 1152, False, "in_proj_dw")
        token = None if grads_done is None else grads_done(l, dict(win=dwin), True)
        dx, dg1 = matmul_prenorm_bwd(dproj, w["win"], s["x"], w["g1"], dx2, tm, "in_proj_bwd", token=token)
        grads[l] = dict(win=dwin, wo=dwo, wu=dwu, wd=dwd, wa=dwa, ws=dws, bs=dbs, par=dpar,
                        g1=dg1, ga=dga, gs=dgs, g2=dg2, g3=dg3, g4=dg4)
    return sse, dx, grads


GROUPS = {
    "chips": [(1, 0, 0), (0, 1, 0), (1, 1, 0)],
    "pair": [(0, 0, 1)],
    "all": [(1, 0, 0), (0, 1, 0), (1, 1, 0), (0, 0, 1), (1, 0, 1), (0, 1, 1), (1, 1, 1)],
}


def _group_index(group, x, y, c):
    return {"chips": 2 * x + y, "pair": c, "all": 4 * x + 2 * y + c}[group]


def _chunk_indices(shape, pieces):
    if len(shape) < 3:
        return [()]
    lead = [()]
    for n in shape[:-2]:
        lead = [i + (k,) for i in lead for k in range(n)]
    rows = shape[-2]
    split = max(1, pieces // len(lead))
    while split > 1 and (rows % split or (rows // split) % 16):
        split -= 1
    step = rows // split
    return [i + (pl.ds(s * step, step),) for i in lead for s in range(split)]


def _exchange(arrays, out_shapes, group, src_view, dst_view, view_shape, name, own, pieces=16):
    masks = GROUPS[group]
    na, nm = len(arrays), len(masks)
    cuts = [_chunk_indices(view_shape(a), pieces) for a in range(na)]

    def body(*refs):
        ins, outs = refs[:na], refs[na:2 * na]
        send_sems, recv_sems = refs[2 * na:2 * na + 2]
        local_sems = refs[2 * na + 2] if own else None
        x, y, c = lax.axis_index("x"), lax.axis_index("y"), lax.axis_index("c")
        me = _group_index(group, x, y, c)
        peers = []
        for mx, my, mc in masks:
            px, py, pc = (1 - x if mx else x), (1 - y if my else y), (1 - c if mc else c)
            peers.append(((px, py, pc), _group_index(group, px, py, pc)))

        def part(ref, idx):
            return ref.at[idx] if idx else ref

        if own:
            for a in range(na):
                for idx in cuts[a]:
                    pltpu.make_async_copy(part(src_view(ins[a], a, me), idx), part(dst_view(outs[a], a, me), idx),
                                          local_sems.at[a]).start()
        for a in range(na):
            for j, (dev, pidx) in enumerate(peers):
                for idx in cuts[a]:
                    pltpu.make_async_remote_copy(
                        src_ref=part(src_view(ins[a], a, pidx), idx), dst_ref=part(dst_view(outs[a], a, me), idx),
                        send_sem=send_sems.at[a * nm + j], recv_sem=recv_sems.at[a * nm + j],
                        device_id=dev, device_id_type=MESH).start()
        whole = []
        for a in range(na):
            for j, (dev, pidx) in enumerate(peers):
                whole.append(pltpu.make_async_remote_copy(
                    src_ref=src_view(ins[a], a, pidx), dst_ref=dst_view(outs[a], a, pidx),
                    send_sem=send_sems.at[a * nm + j], recv_sem=recv_sems.at[a * nm + j],
                    device_id=dev, device_id_type=MESH))
        for cp in whole:
            cp.wait_recv()
        for cp in whole:
            cp.wait_send()
        if own:
            for a in range(na):
                pltpu.make_async_copy(src_view(ins[a], a, me), dst_view(outs[a], a, me), local_sems.at[a]).wait()

    hbm = pl.BlockSpec(memory_space=pltpu.HBM)
    sems = [pltpu.SemaphoreType.DMA((na * nm,)), pltpu.SemaphoreType.DMA((na * nm,))]
    return pl.pallas_call(
        body, name=name, in_specs=[hbm] * na, out_specs=[hbm] * na,
        out_shape=[_sds(s, a.dtype) for s, a in zip(out_shapes, arrays)],
        scratch_shapes=sems + ([pltpu.SemaphoreType.DMA((na,))] if own else []))(*arrays)


def all_gather(arrays, group, name, slot_axis=0, own=True):
    n = len(GROUPS[group]) + 1
    shapes = [a.shape[:slot_axis] + (n,) + a.shape[slot_axis:] for a in arrays]
    lead = (slice(None),) * slot_axis
    return _exchange(arrays, shapes, group, lambda r, a, i: r, lambda r, a, i: r.at[lead + (i,)],
                     lambda a: arrays[a].shape, name, own)


HBM_SPEC = pl.BlockSpec(memory_space=pltpu.HBM)
SEM_SPEC = pl.BlockSpec(memory_space=pltpu.SEMAPHORE)
DATAFLOW = pltpu.SideEffectType.DATAFLOW_SIDE_EFFECTING
N_CHIPS = 4


def _peers(group, x, y, c):
    out = []
    for mx, my, mc in GROUPS[group]:
        px, py, pc = (1 - x if mx else x), (1 - y if my else y), (1 - c if mc else c)
        out.append(((px, py, pc), _group_index(group, px, py, pc)))
    return out


def _whole_views(sources):
    return dict(src=lambda ref, a, c, to: ref, dst=lambda ref, a, c, sender: ref.at[sender],
                rows=lambda a: sources[a].shape[0])


def _weight_views(shards):
    half = [s.shape[0] // 2 for s in shards]
    return dict(src=lambda ref, a, c, to_chip: ref.at[pl.ds(c * half[a], half[a])],
                dst=lambda ref, a, c, from_chip: ref.at[from_chip, pl.ds(c * half[a], half[a])],
                rows=lambda a: half[a])


def _grad_views(sums):
    return dict(src=lambda ref, a, c, to_chip: ref.at[to_chip], dst=lambda ref, a, c, from_chip: ref.at[from_chip],
                rows=lambda a: sums[a].shape[1])


def chips_start(sources, zones, views, name, pieces=4, after=None, group="chips"):
    na, nm = len(sources), len(GROUPS[group])

    def body(*refs):
        ins, lands = refs[:na], refs[na:2 * na]
        n_in = 2 * na + len(_token_arg(after))
        send_sems, recv_sems, token = refs[n_in], refs[n_in + 1], refs[-1]
        x, y, c = lax.axis_index("x"), lax.axis_index("y"), lax.axis_index("c")
        me = _group_index(group, x, y, c)
        for a in range(na):
            step = views["rows"](a) // pieces
            for j, (dev, to) in enumerate(_peers(group, x, y, c)):
                for q in range(pieces):
                    rows = pl.ds(q * step, step)
                    pltpu.make_async_remote_copy(
                        src_ref=views["src"](ins[a], a, c, to).at[rows],
                        dst_ref=views["dst"](lands[a], a, c, me).at[rows],
                        send_sem=send_sems.at[a * nm + j], recv_sem=recv_sems.at[a * nm + j],
                        device_id=dev, device_id_type=MESH).start()
        token[...] = jnp.zeros_like(token)

    both = list(sources) + list(zones)
    outs = pl.pallas_call(
        body, name=name,
        out_shape=(pltpu.SemaphoreType.DMA((na * nm,)), pltpu.SemaphoreType.DMA((na * nm,)),
                   *[pltpu.HBM(b.shape, b.dtype) for b in both], _sds((8, CH), f32)),
        in_specs=[HBM_SPEC] * (2 * na) + _token_spec(after),
        out_specs=(SEM_SPEC, SEM_SPEC, *[HBM_SPEC] * (2 * na), pl.BlockSpec(memory_space=pltpu.VMEM)),
        input_output_aliases={i: 2 + i for i in range(2 * na)},
        compiler_params=pltpu.CompilerParams(has_side_effects=DATAFLOW))(
            *[pltpu.with_memory_space_constraint(b, pltpu.HBM) for b in both], *_token_arg(after))
    return dict(send=outs[0], recv=outs[1], sources=list(outs[2:2 + na]), zones=list(outs[2 + na:2 + 2 * na]),
                token=outs[-1], views=views, group=group)


def chips_wait(started, after, name):
    sources, zones, views, group = started["sources"], started["zones"], started["views"], started["group"]
    na, nm = len(sources), len(GROUPS[group])

    def body(*refs):
        ins, lands = refs[:na], refs[na:2 * na]
        send_sems, recv_sems = refs[2 * na], refs[2 * na + 1]
        x, y, c = lax.axis_index("x"), lax.axis_index("y"), lax.axis_index("c")
        for a in range(na):
            for j, (dev, peer) in enumerate(_peers(group, x, y, c)):
                cp = pltpu.make_async_remote_copy(
                    src_ref=views["src"](ins[a], a, c, peer), dst_ref=views["dst"](lands[a], a, c, peer),
                    send_sem=send_sems.at[a * nm + j], recv_sem=recv_sems.at[a * nm + j],
                    device_id=dev, device_id_type=MESH)
                cp.wait_send()
                cp.wait_recv()

    both = list(sources) + list(zones)
    outs = pl.pallas_call(
        body, name=name, out_shape=tuple(pltpu.HBM(b.shape, b.dtype) for b in both),
        in_specs=[HBM_SPEC] * (2 * na) + [SEM_SPEC, SEM_SPEC, pl.BlockSpec(memory_space=pl.ANY)],
        out_specs=tuple([HBM_SPEC] * (2 * na)), input_output_aliases={i: i for i in range(2 * na)},
        compiler_params=pltpu.CompilerParams(has_side_effects=DATAFLOW))(*both, started["send"], started["recv"], after)
    return list(outs[:na]), list(outs[na:])


def weights_share(zones, name):
    na, nm = len(zones), N_CHIPS - 1

    def body(*refs):
        lands = refs[na:2 * na]
        send_sems, recv_sems = refs[2 * na:]
        x, y, c = lax.axis_index("x"), lax.axis_index("y"), lax.axis_index("c")
        chip = 2 * x + y
        sibling = (x, y, 1 - c)
        sends = []
        for a in range(na):
            half = zones[a].shape[1] // 2
            for m in range(1, N_CHIPS):
                mine = lands[a].at[chip ^ m, pl.ds(c * half, half)]
                sends.append(pltpu.make_async_remote_copy(
                    src_ref=mine, dst_ref=mine, send_sem=send_sems.at[a * nm + m - 1],
                    recv_sem=recv_sems.at[a * nm + m - 1], device_id=sibling, device_id_type=MESH))
        for cp in sends:
            cp.start()
        for a in range(na):
            half = zones[a].shape[1] // 2
            for m in range(1, N_CHIPS):
                theirs = lands[a].at[chip ^ m, pl.ds((1 - c) * half, half)]
                pltpu.make_async_remote_copy(
                    src_ref=theirs, dst_ref=theirs, send_sem=send_sems.at[a * nm + m - 1],
                    recv_sem=recv_sems.at[a * nm + m - 1], device_id=sibling, device_id_type=MESH).wait_recv()
        for cp in sends:
            cp.wait_send()

    return pl.pallas_call(
        body, name=name, in_specs=[HBM_SPEC] * na, out_specs=[HBM_SPEC] * na,
        out_shape=[_sds(z.shape, z.dtype) for z in zones], input_output_aliases={i: i for i in range(na)},
        scratch_shapes=[pltpu.SemaphoreType.DMA((na * nm,)), pltpu.SemaphoreType.DMA((na * nm,))])(*zones)


def pair_send_halves(grads, name):
    half = [g.shape[1] // 2 for g in grads]
    shapes = [(g.shape[0], h, g.shape[2]) for g, h in zip(grads, half)]
    return _exchange(grads, shapes, "pair", lambda r, a, i: r.at[:, pl.ds(i * half[a], half[a])],
                     lambda r, a, i: r, lambda a: shapes[a], name, False)


def sum_pair_half(g, recv, core, name, tb=256, by_chip=None):
    nk, r, c = g.shape
    tb = min(tb, r // 2)
    nb = r // 2 // tb

    def body(core_ref, g_ref, r_ref, o_ref):
        s = g_ref[...].astype(f32) + r_ref[...].astype(f32)
        if by_chip is None:
            o_ref[...] = s.astype(bf16)
        else:
            for k in range(by_chip[0]):
                o_ref[k] = s[:, k * by_chip[1]:(k + 1) * by_chip[1]].astype(bf16)

    if by_chip is None:
        out_spec = pl.BlockSpec((None, tb, c), lambda k, i, core_ref: (k, i, 0))
        out_shape = _sds((nk, r // 2, c), bf16)
    else:
        assert nk == 1
        out_spec = pl.BlockSpec((by_chip[0], tb, by_chip[1]), lambda k, i, core_ref: (0, i, 0))
        out_shape = _sds((by_chip[0], r // 2, by_chip[1]), bf16)
    return pl.pallas_call(
        body, name=name,
        grid_spec=pltpu.PrefetchScalarGridSpec(
            num_scalar_prefetch=1, grid=(nk, nb),
            in_specs=[pl.BlockSpec((None, tb, c), lambda k, i, core_ref: (k, core_ref[0] * nb + i, 0)),
                      pl.BlockSpec((None, tb, c), lambda k, i, core_ref: (k, i, 0))],
            out_specs=out_spec),
        out_shape=out_shape, compiler_params=_cparams(2))(jnp.reshape(core, (1,)).astype(jnp.int32), g, recv)


def assemble_columns(blocks, width, name, tb=256):
    n, r, c = blocks.shape

    def body(b_ref, o_ref):
        for k in range(n):
            o_ref[:, k * c:(k + 1) * c] = b_ref[k]
        o_ref[:, n * c:] = jnp.zeros((tb, width - n * c), blocks.dtype)

    return pl.pallas_call(
        body, name=name, grid=(r // tb,), in_specs=[pl.BlockSpec((n, tb, c), lambda i: (0, i, 0))],
        out_specs=pl.BlockSpec((tb, width), lambda i: (i, 0)), out_shape=_sds((r, width), blocks.dtype),
        compiler_params=_cparams(1))(blocks)


def chip_sum_into(acc, layer, own, others, chip, name, tb=256):
    n, r, c = own.shape
    tb = min(tb, r)

    def body(chip_ref, x_ref, y1_ref, y2_ref, y3_ref, acc_ref, o_ref):
        o_ref[...] = ((x_ref[...].astype(f32) + y1_ref[...].astype(f32)) + y2_ref[...].astype(f32)) + y3_ref[...].astype(f32)

    def slot(k):
        return pl.BlockSpec((None, tb, c), lambda i, chip_ref: (chip_ref[0] ^ k, i, 0))

    return pl.pallas_call(
        body, name=name,
        grid_spec=pltpu.PrefetchScalarGridSpec(
            num_scalar_prefetch=1, grid=(r // tb,),
            in_specs=[slot(k) for k in range(n)] + [pl.BlockSpec(memory_space=pl.ANY)],
            out_specs=pl.BlockSpec((None, tb, c), lambda i, chip_ref: (layer, i, 0))),
        out_shape=_sds(acc.shape, f32), input_output_aliases={n + 1: 0}, compiler_params=_cparams(1))(
            jnp.reshape(chip, (1,)).astype(jnp.int32), own, *([others] * (n - 1)), acc)


def adamw_half(w, g_half, m, v, half, name, before=None, token=None, tb=256):
    depth, r, c = w.shape
    tb = min(tb, r // 2)
    nb = r // 2 // tb
    n_extra = (0 if before is None else 4) + len(_token_arg(token))

    def body(half_ref, w_ref, gh_ref, m_ref, v_ref, *rest):
        g_ref, d_ref, mo_ref, vo_ref = rest[n_extra:]
        gv = gh_ref[...]
        m2 = B1 * m_ref[...] + (1.0 - B1) * gv
        v2 = B2 * v_ref[...] + (1.0 - B2) * (gv * gv)
        m_hat = m2 / (1.0 - B1 ** STEP)
        v_hat = v2 / (1.0 - B2 ** STEP)
        g_ref[...] = gv
        d_ref[...] = -LR * (m_hat / (jnp.sqrt(v_hat) + AEPS) + WD * w_ref[...])
        mo_ref[...] = m2
        vo_ref[...] = v2

    whole = pl.BlockSpec((None, tb, c), lambda l, i, half_ref: (l, half_ref[0] * nb + i, 0))
    part = pl.BlockSpec((None, tb, c), lambda l, i, half_ref: (l, i, 0))
    extra = ([] if before is None else list(before)) + _token_arg(token)
    return pl.pallas_call(
        body, name=name,
        grid_spec=pltpu.PrefetchScalarGridSpec(
            num_scalar_prefetch=1, grid=(depth, nb),
            in_specs=[whole, part, whole, whole] + [pl.BlockSpec(memory_space=pl.ANY)] * n_extra, out_specs=[whole] * 4),
        out_shape=[_sds(w.shape, f32)] * 4,
        input_output_aliases={} if before is None else {5 + k: k for k in range(4)},
        compiler_params=_cparams(2))(jnp.reshape(half, (1,)).astype(jnp.int32), w, g_half, m, v, *extra)


def sum_slots(y, out_dtype, name, tb=256):
    n, r, c = y.shape
    tb = min(tb, r)

    def body(y_ref, o_ref):
        acc = y_ref[0].astype(f32)
        for i in range(1, n):
            acc = acc + y_ref[i].astype(f32)
        o_ref[...] = acc.astype(out_dtype)

    return pl.pallas_call(
        body, name=name, grid=(r // tb,),
        in_specs=[pl.BlockSpec((n, tb, c), lambda i: (0, i, 0))], out_specs=pl.BlockSpec((tb, c), lambda i: (i, 0)),
        out_shape=_sds((r, c), out_dtype), compiler_params=_cparams(1))(y)


def adamw(w, g, m, v, name, tb=256):
    r, c = w.shape
    tb = min(tb, r)

    def body(w_ref, g_ref, m_ref, v_ref, d_ref, mo_ref, vo_ref):
        gv = g_ref[...]
        m2 = B1 * m_ref[...] + (1.0 - B1) * gv
        v2 = B2 * v_ref[...] + (1.0 - B2) * (gv * gv)
        m_hat = m2 / (1.0 - B1 ** STEP)
        v_hat = v2 / (1.0 - B2 ** STEP)
        d_ref[...] = -LR * (m_hat / (jnp.sqrt(v_hat) + AEPS) + WD * w_ref[...])
        mo_ref[...] = m2
        vo_ref[...] = v2

    spec = pl.BlockSpec((tb, c), lambda i: (i, 0))
    return pl.pallas_call(
        body, name=name, grid=(r // tb,), in_specs=[spec] * 4, out_specs=[spec] * 3,
        out_shape=[_sds((r, c), f32)] * 3, compiler_params=_cparams(1))(w, g, m, v)


def adamw_leading(w, g, m, v, name, tc=64):
    c, l, r = w.shape
    main = c // tc
    tail = c - main * tc

    def body(w_ref, g_ref, m_ref, v_ref, *rest):
        d_ref, mo_ref, vo_ref = rest[-3:]
        gv = g_ref[...]
        m2 = B1 * m_ref[...] + (1.0 - B1) * gv
        v2 = B2 * v_ref[...] + (1.0 - B2) * (gv * gv)
        m_hat = m2 / (1.0 - B1 ** STEP)
        v_hat = v2 / (1.0 - B2 ** STEP)
        d_ref[...] = -LR * (m_hat / (jnp.sqrt(v_hat) + AEPS) + WD * w_ref[...])
        mo_ref[...] = m2
        vo_ref[...] = v2

    spec = pl.BlockSpec((tc, l, r), lambda i: (i, 0, 0))
    outs = pl.pallas_call(
        functools.partial(body), name=name, grid=(main,), in_specs=[spec] * 4, out_specs=[spec] * 3,
        out_shape=[_sds(w.shape, f32)] * 3, compiler_params=_cparams(1))(w, g, m, v)
    if tail:
        assert (main * tc) % tail == 0
        last = pl.BlockSpec((tail, l, r), lambda i: (main * tc // tail, 0, 0))
        outs = pl.pallas_call(
            functools.partial(body), name=name + "_tail", grid=(1,),
            in_specs=[last] * 4 + [pl.BlockSpec(memory_space=pl.ANY)] * 3, out_specs=[last] * 3,
            out_shape=[_sds(w.shape, f32)] * 3, input_output_aliases={4: 0, 5: 1, 6: 2},
            compiler_params=_cparams(1))(w, g, m, v, *outs)
    return outs


SMALL_ROW = 1024
SMALL_GAINS = ("g1", "ga", "gs", "g2", "g3", "g4")
SMALL_LAYER_ROWS = 8 + 8 + 16 + 8


def _pack_small(grads):
    wide = lambda a: jnp.pad(a, ((0, 0), (0, 2 * SMALL_ROW - a.shape[1]))).reshape(-1, SMALL_ROW)
    row = lax.broadcasted_iota(jnp.int32, (8, SMALL_ROW), 0)
    parts = []
    for g in grads:
        singles = [g[k] for k in SMALL_GAINS] + [g["bs"][:, :SMALL_ROW],
                                                 jnp.pad(g["bs"][:, SMALL_ROW:], ((0, 0), (0, 2 * SMALL_ROW - XBC)))]
        first = sum(jnp.where(row == k, s, 0.0) for k, s in enumerate(singles))
        parts += [first, g["wa"], wide(g["ws"]), jnp.pad(g["par"], ((0, 0), (0, SMALL_ROW - CH)))]
    return jnp.concatenate(parts, axis=0)


def _unpack_small(packed, depth):
    rows = packed.reshape(depth, SMALL_LAYER_ROWS, SMALL_ROW)
    out = {k: rows[:, i] for i, k in enumerate(SMALL_GAINS)}
    out["bs"] = rows[:, 6:8].reshape(depth, 2 * SMALL_ROW)[:, :XBC]
    out["wa"] = rows[:, 8:11]
    out["ws"] = rows[:, 16:32].reshape(depth, 8, 2 * SMALL_ROW)[:, :4, :XBC]
    out["par"] = rows[:, 32:35, :CH]
    return out


def kernel(x, norm_mix_pre, w_in, conv_a_w, ssm_conv_w, ssm_conv_b, dt_bias, a_log, d_skip, conv_out_norm, ssm_out_norm, w_out, norm_mix_post, norm_mlp_pre, w_up, w_down, norm_mlp_post, loss_target, m_norm_mix_pre, m_w_in, m_conv_a_w, m_ssm_conv_w, m_ssm_conv_b, m_dt_bias, m_a_log, m_d_skip, m_conv_out_norm, m_ssm_out_norm, m_w_out, m_norm_mix_post, m_norm_mlp_pre, m_w_up, m_w_down, m_norm_mlp_post, v_norm_mix_pre, v_w_in, v_conv_a_w, v_ssm_conv_w, v_ssm_conv_b, v_dt_bias, v_a_log, v_d_skip, v_conv_out_norm, v_ssm_out_norm, v_w_out, v_norm_mix_post, v_norm_mlp_pre, v_w_up, v_w_down, v_norm_mlp_post):
    nb, seq, _ = x.shape
    t = nb * seq
    depth = w_in.shape[0]
    ncol = w_in.shape[2]
    chip = 2 * lax.axis_index("x") + lax.axis_index("y")

    taps = [conv_a_w, ssm_conv_w]
    taps_g = all_gather(taps, "chips", "gather_taps", slot_axis=1, own=False)
    wa_g, ws_g = [lax.dynamic_update_index_in_dim(g, s, chip, 1) for g, s in zip(taps_g, taps)]
    wa_full = jnp.transpose(wa_g, (0, 2, 1, 3)).reshape(depth, 3, D)
    ws_full = jnp.transpose(ws_g, (0, 2, 1, 3)).reshape(depth, 4, XBC)
    lane_pad = lambda a: jnp.pad(a, ((0, 0), (0, CH - a.shape[1])))
    par = jnp.stack([lane_pad(dt_bias), lane_pad(a_log), lane_pad(d_skip)], axis=1)
    par = jnp.pad(par, ((0, 0), (0, 5), (0, 0)))

    layer_shards = lambda l: [w_in[l].astype(bf16), w_out[l].astype(bf16), w_up[l].astype(bf16), w_down[l].astype(bf16)]
    issued = []

    def start(shards, name, whole=False):
        zones = [lax.empty((N_CHIPS,) + s.shape, s.dtype) for s in shards]
        issued.append(chips_start(shards, zones, _whole_views(shards) if whole else _weight_views(shards), name,
                                  after=issued[-1]["token"] if issued else taps_g[0]))
        issued[-1]["whole"] = whole
        return issued[-1]

    def finish(started, after, name):
        shards, zones = chips_wait(started, after, name)
        if not started["whole"]:
            zones = weights_share(zones, "weights_share")
        return [lax.dynamic_update_index_in_dim(z, s, chip, 0) for z, s in zip(zones, shards)]

    def shaped(mats):
        wo_z, wu_z, wd_z = mats
        return wo_z.reshape(2 * D, D), wu_z, wd_z.reshape(DFF, D)

    first = layer_shards(0)
    travelling = {0: start(first[:1], "weights_start_0")}
    rest = start(first[1:], "weights_start_0_rest")
    for l in range(1, depth):
        travelling[l] = start(layer_shards(l), f"weights_start_{l}", whole=l >= 2)

    def weights_of(l, x_in):
        mats = finish(travelling.pop(l), x_in, f"weights_wait_{l}")
        w = dict(win=assemble_columns(mats[0], PROJ, "assemble_w_in"), wa=jnp.pad(wa_full[l], ((0, 5), (0, 0))),
                 ws=jnp.pad(ws_full[l], ((0, 4), (0, 0))), bs=ssm_conv_b[l][None], par=par[l],
                 g1=norm_mix_pre[l][None], ga=conv_out_norm[l][None], gs=ssm_out_norm[l][None],
                 g2=norm_mix_post[l][None], g3=norm_mlp_pre[l][None], g4=norm_mlp_post[l][None])
        if l == 0:
            w["token"] = issued[-1]["token"]
            w["late"] = lambda after: dict(zip(("wo", "wu", "wd"), shaped(finish(rest, after, "weights_wait_0_rest"))))
        else:
            w.update(zip(("wo", "wu", "wd"), shaped(mats[1:])))
        return w

    core = lax.axis_index("c")
    grads_travelling = {}
    given_m = dict(win=m_w_in, wo=m_w_out, wu=m_w_up, wd=m_w_down)
    given_v = dict(win=v_w_in, wo=v_w_out, wu=v_w_up, wd=v_w_down)

    chip_major = dict(win=lambda a: a[None], wo=lambda a: a.reshape(N_CHIPS, 2 * D // N_CHIPS, D), wu=lambda a: a,
                      wd=lambda a: a.reshape(N_CHIPS, DFF // N_CHIPS, D))
    held = {}

    to_pair = {}

    def pair_sums_to_chips(l, keys, mats, received, after=None):
        sums = [sum_pair_half(m_, r_, core, "pair_sum", by_chip=(N_CHIPS, ncol) if k == "win" else None)
                for k, m_, r_ in zip(keys, mats, received)]
        zones = [lax.empty(s.shape, s.dtype) for s in sums]
        started = chips_start(sums, zones, _grad_views(sums), f"grads_start_{l}_{len(grads_travelling)}", after=after)
        grads_travelling[(l, keys[0])] = (keys, started)
        return started["token"]

    def grads_done(l, g, last):
        token = None
        if not last and l + 1 in to_pair:
            keys, started = to_pair.pop(l + 1)
            mats, received = chips_wait(started, g["wo"], f"grads_to_pair_wait_{l + 1}")
            token = pair_sums_to_chips(l + 1, keys, mats, received)
        if l > 0 and not last:
            held[l] = g
            return token
        g = {**held.pop(l, {}), **g}
        keys = [k for k in ("win", "wo", "wu", "wd") if k in g]
        mats = [chip_major[k](g[k]) for k in keys]
        if l > 0:
            half = [m_.shape[1] // 2 for m_ in mats]
            views = dict(src=lambda ref, a, c, to: ref.at[:, pl.ds(to * half[a], half[a])],
                         dst=lambda ref, a, c, sender: ref, rows=lambda a: mats[a].shape[0])
            zones = [lax.empty((m_.shape[0], h, m_.shape[2]), m_.dtype) for m_, h in zip(mats, half)]
            to_pair[l] = (keys, chips_start(mats, zones, views, f"grads_to_pair_start_{l}", pieces=1, group="pair"))
            return to_pair[l][1]["token"]
        return pair_sums_to_chips(l, keys, mats, pair_send_halves(mats, "grads_to_pair"), after=token)

    sse, dx, grads = local_step(x.reshape(t, D), loss_target.reshape(t, D), depth, weights_of, seq, grads_done)
    loss = lax.psum(0.5 / D * sse[0, 0], ("x", "y", "c"))

    packed = _pack_small(grads)
    small_travelling = chips_start([packed], [lax.empty((8,) + packed.shape, f32)], _whole_views([packed]),
                                   "small_start", group="all")

    big_w = dict(win=w_in, wo=w_out, wu=w_up, wd=w_down)
    acc = {k: lax.empty((depth, bw.shape[1] // 2, bw.shape[2]), f32) for k, bw in big_w.items()}
    for n, ((l, _), (keys, started)) in enumerate(grads_travelling.items()):
        sums, zones = chips_wait(started, small_travelling["token"], f"grads_wait_{l}_{n}")
        for k, s, z in zip(keys, sums, zones):
            acc[k] = chip_sum_into(acc[k], l, s, z, chip, "chip_sum")
    names = ("win", "wo", "wu", "wd")
    acc = [acc[k] for k in names]
    pair_views = dict(src=lambda ref, a, c, to: ref, dst=lambda ref, a, c, sender: ref, rows=lambda a: depth)
    to_sibling = chips_start(acc, [lax.empty(a.shape, f32) for a in acc], pair_views, "grads_from_pair_start",
                             pieces=depth, group="pair")
    own_done = {}
    for k, a in zip(names, to_sibling["sources"]):
        if big_w[k].shape[-1] % CH == 0:
            own_done[k] = adamw_half(big_w[k], a, given_m[k], given_v[k], core, "adamw_matrix",
                                     token=to_sibling["token"])

    (packed,), (small_all,) = chips_wait(small_travelling, own_done["wd"][1], "small_wait")
    small_all = lax.dynamic_update_index_in_dim(small_all, packed, 4 * lax.axis_index("x") + 2 * lax.axis_index("y") + core, 0)
    small = _unpack_small(sum_slots(small_all, f32, "small_sum", tb=8), depth)
    wa_cols, ws_cols = conv_a_w.shape[2], ssm_conv_w.shape[2]
    par_g = small["par"].reshape(depth, 3, CH)
    g_small = dict(
        norm_mix_pre=small["g1"], conv_out_norm=small["ga"], ssm_out_norm=small["gs"], norm_mix_post=small["g2"],
        norm_mlp_pre=small["g3"], norm_mlp_post=small["g4"], ssm_conv_b=small["bs"],
        conv_a_w=lax.dynamic_slice_in_dim(small["wa"].reshape(depth, 3, D), chip * wa_cols, wa_cols, axis=2),
        ssm_conv_w=lax.dynamic_slice_in_dim(small["ws"].reshape(depth, 4, XBC), chip * ws_cols, ws_cols, axis=2),
        dt_bias=par_g[:, 0, :NH], a_log=par_g[:, 1, :NH], d_skip=par_g[:, 2, :NH])

    given = dict(norm_mix_pre=(norm_mix_pre, m_norm_mix_pre, v_norm_mix_pre), w_in=(w_in, m_w_in, v_w_in),
                 conv_a_w=(conv_a_w, m_conv_a_w, v_conv_a_w), ssm_conv_w=(ssm_conv_w, m_ssm_conv_w, v_ssm_conv_w),
                 ssm_conv_b=(ssm_conv_b, m_ssm_conv_b, v_ssm_conv_b), dt_bias=(dt_bias, m_dt_bias, v_dt_bias),
                 a_log=(a_log, m_a_log, v_a_log), d_skip=(d_skip, m_d_skip, v_d_skip),
                 conv_out_norm=(conv_out_norm, m_conv_out_norm, v_conv_out_norm),
                 ssm_out_norm=(ssm_out_norm, m_ssm_out_norm, v_ssm_out_norm), w_out=(w_out, m_w_out, v_w_out),
                 norm_mix_post=(norm_mix_post, m_norm_mix_post, v_norm_mix_post),
                 norm_mlp_pre=(norm_mlp_pre, m_norm_mlp_pre, v_norm_mlp_pre), w_up=(w_up, m_w_up, v_w_up),
                 w_down=(w_down, m_w_down, v_w_down), norm_mlp_post=(norm_mlp_post, m_norm_mlp_post, v_norm_mlp_post))
    order = ["norm_mix_pre", "w_in", "conv_a_w", "ssm_conv_w", "ssm_conv_b", "dt_bias", "a_log", "d_skip",
             "conv_out_norm", "ssm_out_norm", "w_out", "norm_mix_post", "norm_mlp_pre", "w_up", "w_down",
             "norm_mlp_post"]
    short = dict(w_in="win", w_out="wo", w_up="wu", w_down="wd")
    results = {}
    for n in order:
        if n in short:
            continue
        wv, mv, vv = given[n]
        gv = g_small[n].reshape(wv.shape)
        two_d = lambda a: a.reshape(-1, a.shape[-1])
        results[n] = (gv,) + tuple(adamw(two_d(wv), two_d(gv), two_d(mv), two_d(vv), "adamw"))

    acc, from_sibling = chips_wait(to_sibling, results["norm_mlp_post"][1], "grads_from_pair_wait")
    for n, k in short.items():
        wv, mv, vv = given[n]
        own, recv = acc[names.index(k)], from_sibling[names.index(k)]
        if k in own_done:
            results[n] = adamw_half(wv, recv, mv, vv, 1 - core, "adamw_matrix", before=own_done[k])
        else:
            to_cols, to_rows = (lambda a: jnp.transpose(a, (2, 0, 1))), (lambda a: jnp.transpose(a, (1, 2, 0)))
            own_c, recv_c = to_cols(own), to_cols(recv)
            g_cols = jnp.where(core == 0, jnp.concatenate([own_c, recv_c], axis=2),
                               jnp.concatenate([recv_c, own_c], axis=2))
            results[n] = tuple(to_rows(o) for o in (g_cols,) + tuple(adamw_leading(to_cols(wv), g_cols, to_cols(mv),
                                                                                   to_cols(vv), "adamw_cols")))
    g_out, d_out, m_out, v_out = [], [], [], []
    for n in order:
        wv = given[n][0]
        gv, dlt, m2, v2 = results[n]
        g_out.append(gv.reshape(wv.shape))
        d_out.append(dlt.reshape(wv.shape))
        m_out.append(m2.reshape(wv.shape))
        v_out.append(v2.reshape(wv.shape))
    return (loss, dx.reshape(nb, seq, D), *g_out, *d_out, *m_out, *v_out)
```

```python
import functools

import jax
import jax.numpy as jnp
from jax import lax
from jax.experimental import pallas as pl
from jax.experimental.pallas import tpu as pltpu

f32, bf16 = jnp.float32, jnp.bfloat16

D = 1024
NH, HP = 16, 64
NG, NS = 2, 128
CH = 128
XBC = D + 2 * NG * NS
DFF = 4 * D
IN_COLS = 3 * D + D + XBC + NH
PROJ = 5760
COL_Z, COL_XBC, COL_DT = 3 * D, 4 * D, 4 * D + XBC
EPS = 1e-6
HALO = 8
HBLK = 16
VMEM_LIMIT = 56 * 2**20
MESH = pl.DeviceIdType.MESH

LR, B1, B2, AEPS, WD, STEP = 0.001, 0.9, 0.999, 1e-08, 0.01, 10


def _cparams(n_axes):
    return pltpu.CompilerParams(dimension_semantics=("arbitrary",) * n_axes, vmem_limit_bytes=VMEM_LIMIT)


def _sds(shape, dtype):
    return jax.ShapeDtypeStruct(tuple(shape), dtype)


def _token_spec(token):
    return [] if token is None else [pl.BlockSpec(memory_space=pl.ANY)]


def _token_arg(token):
    return [] if token is None else [token]


def _resident(shape):
    return pl.BlockSpec(shape, lambda i: (0,) * len(shape), pipeline_mode=pl.Buffered(1))


def _rms_fwd(x, g):
    r = lax.rsqrt(jnp.mean(x * x, axis=-1, keepdims=True) + EPS)
    return x * r * g


def _rms_bwd(x, g, dy):
    r = lax.rsqrt(jnp.mean(x * x, axis=-1, keepdims=True) + EPS)
    xh = x * r
    gdy = dy * g
    dx = r * (gdy - xh * jnp.mean(xh * gdy, axis=-1, keepdims=True))
    return dx, dy * xh


def _accum(ref, part, first):
    @pl.when(first)
    def _():
        ref[...] = part

    @pl.when(jnp.logical_not(first))
    def _():
        ref[...] += part


def _dot_nt(a, b):
    return lax.dot_general(a, b, (((1,), (1,)), ((), ())), preferred_element_type=f32)


def _dot_tn(a, b):
    return lax.dot_general(a, b, (((0,), (0,)), ((), ())), preferred_element_type=f32)


def _dot(a, b):
    return jnp.dot(a, b, preferred_element_type=f32)


def _split_dot(x, e_bf, n_split, nt=False):
    acc = None
    rem = x
    for s in range(n_split):
        hi = rem.astype(bf16)
        term = _dot_nt(hi, e_bf) if nt else _dot(hi, e_bf)
        acc = term if acc is None else acc + term
        if s + 1 < n_split:
            rem = rem - hi.astype(f32)
    return acc


def _sigmoid(x):
    return 0.5 * jnp.tanh(0.5 * x) + 0.5


def norm_matmul(x, g, w, tm, tn, out_dtype, name, token=None):
    t, n = x.shape[0], w.shape[1]
    w_spec = pl.BlockSpec((D, tn), lambda i, j: (0, j))

    def body(x_ref, g_ref, w_ref, *rest):
        o_ref, h_ref = rest[-2:]

        @pl.when(pl.program_id(1) == 0)
        def _():
            h_ref[...] = _rms_fwd(x_ref[...], g_ref[...]).astype(bf16)

        o_ref[...] = _dot(h_ref[...], w_ref[...]).astype(out_dtype)

    return pl.pallas_call(
        body, name=name, grid=(t // tm, n // tn),
        in_specs=[pl.BlockSpec((tm, D), lambda i, j: (i, 0)), pl.BlockSpec((1, D), lambda i, j: (0, 0)), w_spec]
        + _token_spec(token),
        out_specs=[pl.BlockSpec((tm, tn), lambda i, j: (i, j)), pl.BlockSpec((tm, D), lambda i, j: (i, 0))],
        out_shape=[_sds((t, n), out_dtype), _sds((t, D), bf16)],
        compiler_params=_cparams(2))(x, g, w, *_token_arg(token))


def matmul_postnorm(a, w, xres, g, tm, name):
    t, k = a.shape

    def body(a_ref, w_ref, xr_ref, g_ref, y_ref, xo_ref):
        y = _dot(a_ref[...], w_ref[...])
        y_ref[...] = y.astype(bf16)
        xo_ref[...] = xr_ref[...] + _rms_fwd(y, g_ref[...])

    return pl.pallas_call(
        body, name=name, grid=(t // tm,),
        in_specs=[pl.BlockSpec((tm, k), lambda i: (i, 0)), _resident(w.shape),
                  pl.BlockSpec((tm, D), lambda i: (i, 0)), pl.BlockSpec((1, D), lambda i: (0, 0))],
        out_specs=[pl.BlockSpec((tm, D), lambda i: (i, 0)), pl.BlockSpec((tm, D), lambda i: (i, 0))],
        out_shape=[_sds((t, D), bf16), _sds((t, D), f32)],
        compiler_params=_cparams(1))(a, w, xres, g)


def postnorm_bwd_matmul(y, g, dxo, w, tm, tn, name):
    t, n = y.shape[0], w.shape[0]

    def body(y_ref, g_ref, dxo_ref, w_ref, dy_ref, dg_ref, da_ref):
        i, j = pl.program_id(0), pl.program_id(1)

        @pl.when(j == 0)
        def _():
            dx, dgc = _rms_bwd(y_ref[...].astype(f32), g_ref[...], dxo_ref[...])
            dy_ref[...] = dx.astype(bf16)
            _accum(dg_ref, jnp.sum(dgc, axis=0, keepdims=True), i == 0)

        da_ref[...] = _dot_nt(dy_ref[...], w_ref[...]).astype(bf16)

    return pl.pallas_call(
        body, name=name, grid=(t // tm, n // tn),
        in_specs=[pl.BlockSpec((tm, D), lambda i, j: (i, 0)), pl.BlockSpec((1, D), lambda i, j: (0, 0)),
                  pl.BlockSpec((tm, D), lambda i, j: (i, 0)), pl.BlockSpec((tn, D), lambda i, j: (j, 0))],
        out_specs=[pl.BlockSpec((tm, D), lambda i, j: (i, 0)), pl.BlockSpec((1, D), lambda i, j: (0, 0)),
                   pl.BlockSpec((tm, tn), lambda i, j: (i, j))],
        out_shape=[_sds((t, D), bf16), _sds((1, D), f32), _sds((t, n), bf16)],
        compiler_params=_cparams(2))(y, g, dxo, w)


def matmul_prenorm_bwd(da, w, x, g, dxo, tm, name, token=None):
    t, k = da.shape

    def body(da_ref, w_ref, x_ref, g_ref, dxo_ref, *rest):
        dx_ref, dg_ref = rest[-2:]
        dh = _dot_nt(da_ref[...], w_ref[...])
        dxn, dgc = _rms_bwd(x_ref[...], g_ref[...], dh)
        dx_ref[...] = dxo_ref[...] + dxn
        _accum(dg_ref, jnp.sum(dgc, axis=0, keepdims=True), pl.program_id(0) == 0)

    return pl.pallas_call(
        body, name=name, grid=(t // tm,),
        in_specs=[pl.BlockSpec((tm, k), lambda i: (i, 0)), _resident(w.shape),
                  pl.BlockSpec((tm, D), lambda i: (i, 0)), pl.BlockSpec((1, D), lambda i: (0, 0)),
                  pl.BlockSpec((tm, D), lambda i: (i, 0))] + _token_spec(token),
        out_specs=[pl.BlockSpec((tm, D), lambda i: (i, 0)), pl.BlockSpec((1, D), lambda i: (0, 0))],
        out_shape=[_sds((t, D), f32), _sds((1, D), f32)],
        compiler_params=_cparams(1))(da, w, x, g, dxo, *_token_arg(token))


def mlp_fwd(x, g_pre, wu, wd, g_post, tm, name):
    t = x.shape[0]
    nq, _, fc = wu.shape

    def body(x_ref, gp_ref, wu_ref, wd_ref, gq_ref, fp_ref, h_ref, o_ref, xo_ref):
        xv = x_ref[...]
        h = _rms_fwd(xv, gp_ref[...]).astype(bf16)
        h_ref[...] = h
        o = None
        for q in range(nq):
            fq = _dot(h, wu_ref[q])
            fp_ref[:, q * fc:(q + 1) * fc] = fq.astype(bf16)
            r = jnp.maximum(fq, 0.0)
            part = _dot((r * r).astype(bf16), wd_ref[q * fc:(q + 1) * fc, :])
            o = part if o is None else o + part
        o_ref[...] = o.astype(bf16)
        xo_ref[...] = xv + _rms_fwd(o, gq_ref[...])

    row = lambda c: pl.BlockSpec((tm, c), lambda i: (i, 0))
    vec = pl.BlockSpec((1, D), lambda i: (0, 0))
    return pl.pallas_call(
        body, name=name, grid=(t // tm,),
        in_specs=[row(D), vec, _resident(wu.shape), _resident(wd.shape), vec],
        out_specs=[row(nq * fc), row(D), row(D), row(D)],
        out_shape=[_sds((t, nq * fc), bf16), _sds((t, D), bf16), _sds((t, D), bf16), _sds((t, D), f32)],
        compiler_params=_cparams(1))(x, g_pre, wu, wd, g_post)


def mlp_bwd(o, g_post, dxo, wd, fp, wu, x, g_pre, tm, name):
    t = x.shape[0]
    nq, _, fc = wu.shape

    def body(o_ref, gq_ref, dxo_ref, wd_ref, fp_ref, wu_ref, x_ref, gp_ref, do_ref, dfp_ref, dx_ref, dgq_ref, dgp_ref):
        i = pl.program_id(0)
        dxo_v = dxo_ref[...]
        do, dgq = _rms_bwd(o_ref[...].astype(f32), gq_ref[...], dxo_v)
        do_b = do.astype(bf16)
        do_ref[...] = do_b
        dh = None
        for q in range(nq):
            cols = slice(q * fc, (q + 1) * fc)
            dq = _dot_nt(do_b, wd_ref[cols, :]) * (2.0 * jnp.maximum(fp_ref[:, cols].astype(f32), 0.0))
            dq_b = dq.astype(bf16)
            dfp_ref[:, cols] = dq_b
            part = _dot_nt(dq_b, wu_ref[q])
            dh = part if dh is None else dh + part
        dxn, dgp = _rms_bwd(x_ref[...], gp_ref[...], dh)
        dx_ref[...] = dxo_v + dxn
        _accum(dgq_ref, jnp.sum(dgq, axis=0, keepdims=True), i == 0)
        _accum(dgp_ref, jnp.sum(dgp, axis=0, keepdims=True), i == 0)

    row = lambda c: pl.BlockSpec((tm, c), lambda i: (i, 0))
    vec = pl.BlockSpec((1, D), lambda i: (0, 0))
    return pl.pallas_call(
        body, name=name, grid=(t // tm,),
        in_specs=[row(D), vec, row(D), _resident(wd.shape), row(nq * fc), _resident(wu.shape), row(D), vec],
        out_specs=[row(D), row(nq * fc), row(D), vec, vec],
        out_shape=[_sds((t, D), bf16), _sds((t, nq * fc), bf16), _sds((t, D), f32), _sds((1, D), f32), _sds((1, D), f32)],
        compiler_params=_cparams(1))(o, g_post, dxo, wd, fp, wu, x, g_pre)


def matmul_tn(a, b, tm, tn, relu2, name, col_blocks=False):
    t, m = a.shape
    n = b.shape[1]
    if col_blocks:
        out_spec, out_shape = pl.BlockSpec((None, tm, tn), lambda i, j: (j, i, 0)), _sds((n // tn, m, tn), bf16)
    else:
        out_spec, out_shape = pl.BlockSpec((tm, tn), lambda i, j: (i, j)), _sds((m, n), bf16)

    def body(a_ref, b_ref, o_ref, at_ref):
        @pl.when(pl.program_id(1) == 0)
        def _():
            av = a_ref[...]
            if relu2:
                af = jnp.maximum(av.astype(f32), 0.0)
                av = (af * af).astype(bf16)
            at_ref[...] = av.T

        o_ref[...] = _dot(at_ref[...], b_ref[...]).astype(bf16)

    return pl.pallas_call(
        body, name=name, grid=(m // tm, n // tn),
        in_specs=[pl.BlockSpec((t, tm), lambda i, j: (0, i)), pl.BlockSpec((t, tn), lambda i, j: (0, j))],
        out_specs=out_spec, out_shape=out_shape,
        scratch_shapes=[pltpu.VMEM((tm, t), bf16)],
        compiler_params=_cparams(2))(a, b)


ROWS_A = 16
ROWS_B = 32
UNROLL = 4
UNROLL_NORM = 8


def _past(win, s):
    return (win if s == 0 else pltpu.roll(win, s, 0))[HALO:]


def _future(win, s):
    n = win.shape[0]
    return (win if s == 0 else pltpu.roll(win, n - s, 0))[:n - HALO]


def _fold8(v):
    return v.reshape(v.shape[0] // 8, 8, v.shape[1]).sum(axis=0)


def _last8(ref):
    return ref[...].astype(f32)[HBLK - HALO:]


def _first8(ref):
    return ref[...].astype(f32)[:HALO]


def _rd(ref, rows):
    return ref[rows, :].astype(f32)


def _halo_prev(tb, col):
    return lambda i: (jnp.maximum(i * (tb // HBLK) - 1, 0), col)


def _halo_next(tb, col, t):
    return lambda i: (jnp.minimum((i + 1) * (tb // HBLK), t // HBLK - 1), col)


def group_a_fwd(proj, wa, g, seq, tb, name):
    t = proj.shape[0]
    bps = seq // tb

    def body(xa_ref, ca_ref, ba_ref, xah_ref, cah_ref, wa_ref, g_ref, o_ref, u_scr):
        first = (pl.program_id(0) % bps) == 0
        u_scr[0:HALO, :] = jnp.where(first, 0.0, _last8(cah_ref) * _last8(xah_ref))
        w, gv = wa_ref[...], g_ref[...]

        def chunk(i, carry):
            r = pl.multiple_of(i * ROWS_A, ROWS_A)
            rows = pl.ds(r, ROWS_A)
            u_scr[pl.ds(pl.multiple_of(HALO + r, HALO), ROWS_A), :] = _rd(ca_ref, rows) * _rd(xa_ref, rows)
            win = u_scr[pl.ds(r, ROWS_A + HALO), :]
            cv = w[2:3] * _past(win, 0) + w[1:2] * _past(win, 1) + w[0:1] * _past(win, 2)
            o_ref[rows, :] = _rms_fwd(_rd(ba_ref, rows) * cv, gv).astype(bf16)
            return carry

        lax.fori_loop(0, tb // ROWS_A, chunk, 0, unroll=UNROLL_NORM)

    blk = lambda c: pl.BlockSpec((tb, D), lambda i: (i, c))
    return pl.pallas_call(
        body, name=name, grid=(t // tb,),
        in_specs=[blk(0), blk(1), blk(2),
                  pl.BlockSpec((HBLK, D), _halo_prev(tb, 0)), pl.BlockSpec((HBLK, D), _halo_prev(tb, 1)),
                  pl.BlockSpec((8, D), lambda i: (0, 0)), pl.BlockSpec((1, D), lambda i: (0, 0))],
        out_specs=pl.BlockSpec((tb, D), lambda i: (i, 0)),
        out_shape=_sds((t, 2 * D), bf16),
        scratch_shapes=[pltpu.VMEM((tb + HALO, D), f32)],
        compiler_params=_cparams(1))(proj, proj, proj, proj, proj, wa, g)


def group_a_bwd(proj, dcat, wa, g, seq, tb, name, token=None):
    t = proj.shape[0]
    bps = seq // tb

    def body(xa_ref, ca_ref, ba_ref, dy_ref, xap_ref, cap_ref, xan_ref, can_ref, ban_ref, dyn_ref, wa_ref, g_ref,
             *rest):
        dp_ref, dwa_ref, dg_ref, u_scr, d_scr, acc_scr = rest[-6:]
        i = pl.program_id(0)
        first = (i % bps) == 0
        last = (i % bps) == bps - 1
        w = wa_ref[...]
        gv = g_ref[...]
        u_scr[0:HALO, :] = jnp.where(first, 0.0, _last8(cap_ref) * _last8(xap_ref))
        u_scr[HALO + tb:2 * HALO + tb, :] = _first8(can_ref) * _first8(xan_ref)
        acc_scr[...] = jnp.zeros_like(acc_scr)

        def forward_part(n, carry):
            r = pl.multiple_of(n * ROWS_A, ROWS_A)
            rows = pl.ds(r, ROWS_A)
            ba = _rd(ba_ref, rows)
            u_scr[pl.ds(pl.multiple_of(HALO + r, HALO), ROWS_A), :] = _rd(ca_ref, rows) * _rd(xa_ref, rows)
            win = u_scr[pl.ds(r, ROWS_A + HALO), :]
            u = [_past(win, s) for s in range(3)]
            cv = w[2:3] * u[0] + w[1:2] * u[1] + w[0:1] * u[2]
            dya, dgc = _rms_bwd(ba * cv, gv, _rd(dy_ref, rows))
            dcv = dya * ba
            d_scr[rows, :] = dcv
            dp_ref[rows, 2 * D:3 * D] = (dya * cv).astype(bf16)
            acc_scr[0:8, :] += _fold8(dgc)
            for k in range(3):
                acc_scr[8 + 8 * k:16 + 8 * k, :] += _fold8(dcv * u[2 - k])
            return carry

        lax.fori_loop(0, tb // ROWS_A, forward_part, 0, unroll=UNROLL_NORM)

        start = HALO + tb
        cvn = (w[2:3] * u_scr[pl.ds(start, HALO), :] + w[1:2] * u_scr[pl.ds(start - 1, HALO), :]
               + w[0:1] * u_scr[pl.ds(start - 2, HALO), :])
        ban = _first8(ban_ref)
        dyan, _ = _rms_bwd(ban * cvn, gv, _first8(dyn_ref))
        d_scr[tb:tb + HALO, :] = jnp.where(last, 0.0, dyan * ban)

        def backward_part(n, carry):
            r = pl.multiple_of(n * ROWS_A, ROWS_A)
            rows = pl.ds(r, ROWS_A)
            win = d_scr[pl.ds(r, ROWS_A + HALO), :]
            du = w[2:3] * _future(win, 0) + w[1:2] * _future(win, 1) + w[0:1] * _future(win, 2)
            dp_ref[rows, 0:D] = (du * _rd(ca_ref, rows)).astype(bf16)
            dp_ref[rows, D:2 * D] = (du * _rd(xa_ref, rows)).astype(bf16)
            return carry

        lax.fori_loop(0, tb // ROWS_A, backward_part, 0, unroll=UNROLL)

        row = lax.broadcasted_iota(jnp.int32, (8, D), 0)
        dw = jnp.zeros((8, D), f32)
        for k in range(3):
            dw = jnp.where(row == k, jnp.sum(acc_scr[8 + 8 * k:16 + 8 * k, :], axis=0, keepdims=True), dw)
        _accum(dwa_ref, dw, i == 0)
        _accum(dg_ref, jnp.sum(acc_scr[0:8, :], axis=0, keepdims=True), i == 0)

    blk = lambda c: pl.BlockSpec((tb, D), lambda i: (i, c))
    prv = lambda c: pl.BlockSpec((HBLK, D), _halo_prev(tb, c))
    nxt = lambda c: pl.BlockSpec((HBLK, D), _halo_next(tb, c, t))
    return pl.pallas_call(
        body, name=name, grid=(t // tb,),
        in_specs=[blk(0), blk(1), blk(2), blk(0), prv(0), prv(1), nxt(0), nxt(1), nxt(2), nxt(0),
                  pl.BlockSpec((8, D), lambda i: (0, 0)), pl.BlockSpec((1, D), lambda i: (0, 0))] + _token_spec(token),
        out_specs=[pl.BlockSpec((tb, 3 * D), lambda i: (i, 0)), pl.BlockSpec((8, D), lambda i: (0, 0)),
                   pl.BlockSpec((1, D), lambda i: (0, 0))],
        out_shape=[_sds((t, PROJ), bf16), _sds((8, D), f32), _sds((1, D), f32)],
        scratch_shapes=[pltpu.VMEM((tb + 2 * HALO, D), f32), pltpu.VMEM((tb + HALO, D), f32), pltpu.VMEM((32, D), f32)],
        compiler_params=_cparams(1))(proj, proj, proj, dcat, proj, proj, proj, proj, proj, dcat, wa, g,
                                     *_token_arg(token))


CB = 512
XBC_BLK0 = COL_XBC // CB


def conv_b_fwd(proj, ws, bs, seq, tb, name):
    t = proj.shape[0]
    bps = seq // tb

    def body(x_ref, xp_ref, w_ref, b_ref, o_ref, da_ref, x_scr):
        first = (pl.program_id(1) % bps) == 0
        x_scr[0:HALO, :] = jnp.where(first, 0.0, _last8(xp_ref))
        w, bias = w_ref[...], b_ref[...]

        def chunk(n, carry):
            r = pl.multiple_of(n * ROWS_B, ROWS_B)
            rows = pl.ds(r, ROWS_B)
            x_scr[pl.ds(pl.multiple_of(HALO + r, HALO), ROWS_B), :] = _rd(x_ref, rows)
            win = x_scr[pl.ds(r, ROWS_B + HALO), :]
            xc = bias + w[3:4] * _past(win, 0)
            for k in range(3):
                xc = xc + w[k:k + 1] * _past(win, 3 - k)
            sg = _sigmoid(xc)
            o_ref[rows, :] = xc * sg
            da_ref[rows, :] = (sg * (1.0 + xc * (1.0 - sg))).astype(bf16)
            return carry

        lax.fori_loop(0, tb // ROWS_B, chunk, 0, unroll=UNROLL)

    return pl.pallas_call(
        body, name=name, grid=(XBC // CB, t // tb),
        in_specs=[pl.BlockSpec((tb, CB), lambda j, i: (i, XBC_BLK0 + j)),
                  pl.BlockSpec((HBLK, CB), lambda j, i: (jnp.maximum(i * (tb // HBLK) - 1, 0), XBC_BLK0 + j)),
                  pl.BlockSpec((8, CB), lambda j, i: (0, j)), pl.BlockSpec((1, CB), lambda j, i: (0, j))],
        out_specs=[pl.BlockSpec((tb, CB), lambda j, i: (i, j)), pl.BlockSpec((tb, CB), lambda j, i: (i, j))],
        out_shape=[_sds((t, XBC), f32), _sds((t, XBC), bf16)],
        scratch_shapes=[pltpu.VMEM((tb + HALO, CB), f32)],
        compiler_params=_cparams(2))(proj, proj, ws, bs)


def conv_b_bwd(proj, dxs, dact, ws, dproj, seq, tb, name):
    t = proj.shape[0]
    bps = seq // tb

    def body(x_ref, xp_ref, d_ref, dn_ref, a_ref, an_ref, w_ref, dproj_ref, dx_ref, dw_ref, db_ref, x_scr, d_scr,
             acc_scr):
        i = pl.program_id(1)
        first = (i % bps) == 0
        last = (i % bps) == bps - 1
        w = w_ref[...]
        x_scr[0:HALO, :] = jnp.where(first, 0.0, _last8(xp_ref))
        acc_scr[...] = jnp.zeros_like(acc_scr)

        def forward_part(n, carry):
            r = pl.multiple_of(n * ROWS_B, ROWS_B)
            rows = pl.ds(r, ROWS_B)
            x_scr[pl.ds(pl.multiple_of(HALO + r, HALO), ROWS_B), :] = _rd(x_ref, rows)
            win = x_scr[pl.ds(r, ROWS_B + HALO), :]
            dxc = _rd(d_ref, rows) * _rd(a_ref, rows)
            d_scr[rows, :] = dxc
            acc_scr[0:8, :] += _fold8(dxc)
            for k in range(4):
                acc_scr[8 + 8 * k:16 + 8 * k, :] += _fold8(dxc * _past(win, 3 - k))
            return carry

        lax.fori_loop(0, tb // ROWS_B, forward_part, 0, unroll=UNROLL)
        d_scr[tb:tb + HALO, :] = jnp.where(last, 0.0, _first8(dn_ref) * _first8(an_ref))

        def backward_part(n, carry):
            r = pl.multiple_of(n * ROWS_B, ROWS_B)
            win = d_scr[pl.ds(r, ROWS_B + HALO), :]
            dx = w[3:4] * _future(win, 0)
            for k in range(3):
                dx = dx + w[k:k + 1] * _future(win, 3 - k)
            dx_ref[pl.ds(r, ROWS_B), :] = dx.astype(bf16)
            return carry

        lax.fori_loop(0, tb // ROWS_B, backward_part, 0, unroll=UNROLL)

        row = lax.broadcasted_iota(jnp.int32, (8, CB), 0)
        dw = jnp.zeros((8, CB), f32)
        for k in range(4):
            dw = jnp.where(row == k, jnp.sum(acc_scr[8 + 8 * k:16 + 8 * k, :], axis=0, keepdims=True), dw)
        _accum(dw_ref, dw, i == 0)
        _accum(db_ref, jnp.sum(acc_scr[0:8, :], axis=0, keepdims=True), i == 0)

    nh = t // HBLK
    nxt = pl.BlockSpec((HBLK, CB), lambda j, i: (jnp.minimum((i + 1) * (tb // HBLK), nh - 1), j))
    cur = pl.BlockSpec((tb, CB), lambda j, i: (i, j))
    return pl.pallas_call(
        body, name=name, grid=(XBC // CB, t // tb),
        in_specs=[pl.BlockSpec((tb, CB), lambda j, i: (i, XBC_BLK0 + j)),
                  pl.BlockSpec((HBLK, CB), lambda j, i: (jnp.maximum(i * (tb // HBLK) - 1, 0), XBC_BLK0 + j)),
                  cur, nxt, cur, nxt, pl.BlockSpec((8, CB), lambda j, i: (0, j)), pl.BlockSpec(memory_space=pl.ANY)],
        out_specs=[pl.BlockSpec((tb, CB), lambda j, i: (i, XBC_BLK0 + j)), pl.BlockSpec((8, CB), lambda j, i: (0, j)),
                   pl.BlockSpec((1, CB), lambda j, i: (0, j))],
        out_shape=[_sds((t, PROJ), bf16), _sds((8, XBC), f32), _sds((1, XBC), f32)],
        input_output_aliases={7: 0},
        scratch_shapes=[pltpu.VMEM((tb + HALO, CB), f32), pltpu.VMEM((tb + HALO, CB), f32), pltpu.VMEM((40, CB), f32)],
        compiler_params=_cparams(2))(proj, proj, dxs, dxs, dact, dact, ws, dproj)


def place_columns(buf, part, col_block, tb, name):
    t, wdt = part.shape

    def body(p_ref, buf_ref, o_ref):
        o_ref[...] = p_ref[...]

    return pl.pallas_call(
        body, name=name, grid=(t // tb,),
        in_specs=[pl.BlockSpec((tb, wdt), lambda i: (i, 0)), pl.BlockSpec(memory_space=pl.ANY)],
        out_specs=pl.BlockSpec((tb, wdt), lambda i: (i, col_block)), out_shape=_sds(buf.shape, buf.dtype),
        input_output_aliases={1: 0}, compiler_params=_cparams(1))(part, buf)


GW = D // NG
EXPAND_TERMS = 1
REDUCE_TERMS = 1


def _ssd_consts():
    head_of_lane = jnp.arange(D) // HP
    expand = (jnp.arange(CH)[:, None] == head_of_lane[None, :]).astype(bf16)
    tri = (jnp.arange(CH)[:, None] >= jnp.arange(CH)[None, :]).astype(f32)
    return expand, tri


def _ssd_common(par_ref, dtr_ref, e_ref, tri_ref):
    par = par_ref[...]
    dtb, alog, dsk = par[0:1], par[1:2], par[2:3]
    lane = lax.broadcasted_iota(jnp.int32, (CH, CH), 1)
    a = -jnp.exp(alog)
    dtr = dtr_ref[...].astype(f32) + dtb
    sp = jnp.maximum(dtr, 0.0) + jnp.log(1.0 + jnp.exp(-jnp.abs(dtr)))
    dt = jnp.where(lane < NH, sp, 0.0)
    cs = jnp.dot(tri_ref[...], dt * a, precision=lax.Precision.HIGHEST, preferred_element_type=f32)
    cs_last = cs[CH - 1:CH, :]
    dte = jnp.exp(cs_last - cs)
    ecs = jnp.exp(cs)
    ecl = jnp.exp(cs_last)
    e = e_ref[...]
    row8 = lax.broadcasted_iota(jnp.int32, (8, CH), 0)
    r8 = _split_dot(jnp.where(row8 == 0, ecl, jnp.where(row8 == 1, dsk, 0.0)), e, 3)
    return dict(a=a, dtr=dtr, dt=dt, cs=cs, cst=cs.T, dte=dte, ecs=ecs, ecl=ecl, e=e, lane=lane,
                dt_x=_split_dot(dt, e, EXPAND_TERMS), dte_x=_split_dot(dte, e, EXPAND_TERMS),
                ecs_x=_split_dot(ecs, e, EXPAND_TERMS),
                ecl_x=r8[0:1], dsk_x=r8[1:2])


def _decay_matrix(c, h):
    li = lax.broadcasted_iota(jnp.int32, (CH, CH), 0)
    seg = c["cs"][:, h:h + 1] - c["cst"][h:h + 1, :]
    return jnp.exp(jnp.where(li >= c["lane"], seg, -jnp.inf))


def _gate_norm_fwd(y, z, gs):
    zg = z * _sigmoid(z)
    yg = y * zg
    return jnp.concatenate([_rms_fwd(yg[:, k * GW:(k + 1) * GW], gs[:, k * GW:(k + 1) * GW]) for k in range(NG)], axis=1)


def ssd_fwd(xbcs, proj, par, gs, cat, seq, name):
    t = xbcs.shape[0]
    nc = seq // CH
    expand, tri = _ssd_consts()

    def body(xs_ref, b_ref, c_ref, dtr_ref, z_ref, par_ref, e_ref, tri_ref, gs_ref, cat_ref, yn_ref, y_ref, st_ref,
             p_scr, yd_scr):
        @pl.when(pl.program_id(0) % nc == 0)
        def _():
            p_scr[...] = jnp.zeros_like(p_scr)

        c = _ssd_common(par_ref, dtr_ref, e_ref, tri_ref)
        xs = xs_ref[...]
        xdt = xs * c["dt_x"]
        xdt_b = xdt.astype(bf16)
        xdte_b = (xdt * c["dte_x"]).astype(bf16)
        p = p_scr[...]
        st_ref[0] = p
        p_b = p.astype(bf16)
        lo = c["lane"] < HP
        for g in range(NG):
            bg = b_ref[:, g * NS:(g + 1) * NS].astype(bf16)
            cg = c_ref[:, g * NS:(g + 1) * NS].astype(bf16)
            gmat = _dot_nt(cg, bg)
            for q in range(GW // CH):
                col = g * GW + q * CH
                xp = xdt_b[:, col:col + CH]
                h0 = col // HP
                m0 = (gmat * _decay_matrix(c, h0)).astype(bf16)
                m1 = (gmat * _decay_matrix(c, h0 + 1)).astype(bf16)
                stacked = jnp.concatenate([jnp.where(lo, xp, jnp.zeros_like(xp)),
                                           jnp.where(lo, jnp.zeros_like(xp), xp)], axis=0)
                yd_scr[:, col:col + CH] = _dot(jnp.concatenate([m0, m1], axis=1), stacked)
            gsl = slice(g * GW, (g + 1) * GW)
            yoff = _dot(cg, p_b[:, gsl]) * c["ecs_x"][:, gsl]
            yd_scr[:, gsl] = yd_scr[:, gsl] + yoff
            p_scr[:, gsl] = p[:, gsl] * c["ecl_x"][:, gsl] + _dot_tn(bg, xdte_b[:, gsl])
        y = yd_scr[...] + c["dsk_x"] * xs
        y_ref[...] = y
        yn_ref[...] = _gate_norm_fwd(y, z_ref[...].astype(f32), gs_ref[...]).astype(bf16)

    nb = t // CH
    return pl.pallas_call(
        body, name=name, grid=(nb,),
        in_specs=[pl.BlockSpec((CH, D), lambda i: (i, 0)),
                  pl.BlockSpec((CH, NG * NS), lambda i: (i, D // (NG * NS))),
                  pl.BlockSpec((CH, NG * NS), lambda i: (i, D // (NG * NS) + 1)),
                  pl.BlockSpec((CH, CH), lambda i: (i, COL_DT // CH)),
                  pl.BlockSpec((CH, D), lambda i: (i, COL_Z // D)),
                  pl.BlockSpec((8, CH), lambda i: (0, 0)), pl.BlockSpec((CH, D), lambda i: (0, 0)),
                  pl.BlockSpec((CH, CH), lambda i: (0, 0)), pl.BlockSpec((1, D), lambda i: (0, 0)),
                  pl.BlockSpec(memory_space=pl.ANY)],
        out_specs=[pl.BlockSpec((CH, D), lambda i: (i, 1)), pl.BlockSpec((CH, D), lambda i: (i, 0)),
                   pl.BlockSpec((1, NS, D), lambda i: (i, 0, 0))],
        out_shape=[_sds((t, 2 * D), bf16), _sds((t, D), f32), _sds((nb, NS, D), f32)],
        input_output_aliases={9: 0},
        scratch_shapes=[pltpu.VMEM((NS, D), f32), pltpu.VMEM((CH, D), f32)],
        compiler_params=_cparams(1))(xbcs, xbcs, xbcs, proj, proj, par, expand, tri, gs, cat)


def ssd_bwd(xbcs, proj, ypre, states, dcat, par, gs, dproj, seq, name):
    t = xbcs.shape[0]
    nc = seq // CH
    expand, tri = _ssd_consts()

    def body(xs_ref, b_ref, c_ref, dtr_ref, z_ref, y_ref, st_ref, dyn_ref, par_ref, e_ref, tri_ref, gs_ref, dproj_ref,
             dx_ref, dz_ref, ddt_ref, dpar_ref, dgs_ref, dp_scr, dxdt_scr):
        i = pl.program_id(0)

        @pl.when(i % nc == 0)
        def _():
            dp_scr[...] = jnp.zeros_like(dp_scr)

        c = _ssd_common(par_ref, dtr_ref, e_ref, tri_ref)
        e = c["e"]
        lane = c["lane"]
        sub = lax.broadcasted_iota(jnp.int32, (CH, CH), 0)
        xs = xs_ref[...]
        xdt = xs * c["dt_x"]
        xdt_b = xdt.astype(bf16)
        xdte_b = (xdt * c["dte_x"]).astype(bf16)
        p = st_ref[0]
        p_b = p.astype(bf16)
        dpn = dp_scr[...]
        dpn_b = dpn.astype(bf16)

        y, z, gs_v = y_ref[...], z_ref[...].astype(f32), gs_ref[...]
        zs = _sigmoid(z)
        zg = z * zs
        yg = y * zg
        parts, gparts = [], []
        for k in range(NG):
            sl = slice(k * GW, (k + 1) * GW)
            dxk, dgk = _rms_bwd(yg[:, sl], gs_v[:, sl], dyn_ref[:, sl].astype(f32))
            parts.append(dxk)
            gparts.append(dgk)
        dyg = jnp.concatenate(parts, axis=1)
        dgs_rows = jnp.concatenate(gparts, axis=1)
        dy = dyg * zg
        dz_ref[...] = (dyg * y * (zs * (1.0 + z * (1.0 - zs)))).astype(bf16)
        dy_b = dy.astype(bf16)
        dq_b = (dy * c["ecs_x"]).astype(bf16)

        lo = lane < HP
        dcs = jnp.zeros((CH, CH), f32)
        dcst = jnp.zeros((CH, CH), f32)
        for g in range(NG):
            gsl = slice(g * GW, (g + 1) * GW)
            bg = b_ref[:, g * NS:(g + 1) * NS].astype(bf16)
            cg = c_ref[:, g * NS:(g + 1) * NS].astype(bf16)
            gmat = _dot_nt(cg, bg)
            dgm = jnp.zeros((CH, CH), f32)
            for q in range(GW // CH):
                col = g * GW + q * CH
                xp = xdt_b[:, col:col + CH]
                dyp = dy_b[:, col:col + CH]
                zero = jnp.zeros_like(dyp)
                xp2 = jnp.concatenate([jnp.where(lo, xp, zero), jnp.where(lo, zero, xp)], axis=0)
                dy2 = jnp.concatenate([jnp.where(lo, dyp, zero), jnp.where(lo, zero, dyp)], axis=0)
                dm2 = _dot_nt(dyp, xp2)
                ms = []
                for hh in range(2):
                    h = col // HP + hh
                    dec = _decay_matrix(c, h)
                    m = gmat * dec
                    dm = dm2[:, hh * CH:(hh + 1) * CH]
                    dseg = dm * m
                    dcs = dcs + jnp.where(lane == h, jnp.sum(dseg, axis=1, keepdims=True), 0.0)
                    dcst = dcst + jnp.where(sub == h, jnp.sum(dseg, axis=0, keepdims=True), 0.0)
                    dgm = dgm + dm * dec
                    ms.append(m.astype(bf16))
                dxdt_scr[:, col:col + CH] = _dot_tn(jnp.concatenate(ms, axis=0), dy2)
            dgm_b = dgm.astype(bf16)
            bds = _dot(bg, dpn_b[:, gsl])
            dxdt_scr[:, gsl] = dxdt_scr[:, gsl] + c["dte_x"][:, gsl] * bds
            dc_g = _dot(dgm_b, bg) + _dot_nt(dq_b[:, gsl], p_b[:, gsl])
            db_g = _dot_tn(dgm_b, cg) + _dot_nt(xdte_b[:, gsl], dpn_b[:, gsl])
            dx_ref[:, D + g * NS:D + (g + 1) * NS] = db_g
            dx_ref[:, D + NG * NS + g * NS:D + NG * NS + (g + 1) * NS] = dc_g
            dp_scr[:, gsl] = dpn[:, gsl] * c["ecl_x"][:, gsl] + _dot_tn(cg, dq_b[:, gsl])
            q_g = _dot(cg, p_b[:, gsl])
            e_g = e[:, gsl]
            dcs = dcs + c["ecs"] * _split_dot(dy[:, gsl] * q_g, e_g, REDUCE_TERMS, nt=True)
            ddte = _split_dot(xdt[:, gsl] * bds, e_g, REDUCE_TERMS, nt=True) * c["dte"]
            dcs = dcs - ddte
            dcs = dcs + jnp.where(sub == CH - 1, jnp.sum(ddte, axis=0, keepdims=True), 0.0)

        decl = _split_dot(jnp.broadcast_to(jnp.sum(dpn * p, axis=0, keepdims=True), (8, D)), e, 2, nt=True)[0:1]
        dcs = dcs + jnp.where(sub == CH - 1, c["ecl"] * decl, 0.0)
        dcs = dcs - dcst.T
        dadt = lax.dot_general(tri_ref[...], dcs, (((0,), (0,)), ((), ())), precision=lax.Precision.HIGHEST,
                               preferred_element_type=f32)
        dxdt = dxdt_scr[...]
        ddt = dadt * c["a"] + _split_dot(dxdt * xs, e, REDUCE_TERMS, nt=True)
        ddtr = jnp.where(lane < NH, ddt * _sigmoid(c["dtr"]), 0.0)
        ddt_ref[...] = ddtr.astype(bf16)
        dx_ref[:, 0:D] = dxdt * c["dt_x"] + c["dsk_x"] * dy
        dsk = _split_dot(jnp.broadcast_to(jnp.sum(dy * xs, axis=0, keepdims=True), (8, D)), e, 2, nt=True)[0:1]
        dalog = jnp.sum(dadt * c["dt"], axis=0, keepdims=True) * c["a"]
        row8 = lax.broadcasted_iota(jnp.int32, (8, CH), 0)
        dpar = jnp.where(row8 == 0, jnp.sum(ddtr, axis=0, keepdims=True),
                         jnp.where(row8 == 1, dalog, jnp.where(row8 == 2, dsk, 0.0)))
        dpar = jnp.where(lax.broadcasted_iota(jnp.int32, (8, CH), 1) < NH, dpar, 0.0)
        _accum(dpar_ref, dpar, i == 0)
        _accum(dgs_ref, jnp.sum(dgs_rows, axis=0, keepdims=True), i == 0)

    nb = t // CH
    rev = lambda i: (i // nc) * nc + (nc - 1 - i % nc)
    return pl.pallas_call(
        body, name=name, grid=(nb,),
        in_specs=[pl.BlockSpec((CH, D), lambda i: (rev(i), 0)),
                  pl.BlockSpec((CH, NG * NS), lambda i: (rev(i), D // (NG * NS))),
                  pl.BlockSpec((CH, NG * NS), lambda i: (rev(i), D // (NG * NS) + 1)),
                  pl.BlockSpec((CH, CH), lambda i: (rev(i), COL_DT // CH)),
                  pl.BlockSpec((CH, D), lambda i: (rev(i), COL_Z // D)),
                  pl.BlockSpec((CH, D), lambda i: (rev(i), 0)),
                  pl.BlockSpec((1, NS, D), lambda i: (rev(i), 0, 0)),
                  pl.BlockSpec((CH, D), lambda i: (rev(i), 1)),
                  pl.BlockSpec((8, CH), lambda i: (0, 0)), pl.BlockSpec((CH, D), lambda i: (0, 0)),
                  pl.BlockSpec((CH, CH), lambda i: (0, 0)), pl.BlockSpec((1, D), lambda i: (0, 0)),
                  pl.BlockSpec(memory_space=pl.ANY)],
        out_specs=[pl.BlockSpec((CH, XBC), lambda i: (rev(i), 0)), pl.BlockSpec((CH, D), lambda i: (rev(i), COL_Z // D)),
                   pl.BlockSpec((CH, CH), lambda i: (rev(i), 0)),
                   pl.BlockSpec((8, CH), lambda i: (0, 0)), pl.BlockSpec((1, D), lambda i: (0, 0))],
        out_shape=[_sds((t, XBC), f32), _sds((t, PROJ), bf16), _sds((t, CH), bf16), _sds((8, CH), f32), _sds((1, D), f32)],
        input_output_aliases={12: 1},
        scratch_shapes=[pltpu.VMEM((NS, D), f32), pltpu.VMEM((CH, D), f32)],
        compiler_params=_cparams(1))(xbcs, xbcs, xbcs, proj, proj, ypre, states, dcat, par, expand, tri, gs, dproj)


def loss_head(y, target, tb, name):
    t = y.shape[0]

    def body(y_ref, t_ref, s_ref, dy_ref):
        err = y_ref[...] - t_ref[...]
        dy_ref[...] = err * (1.0 / D)
        _accum(s_ref, jnp.zeros((8, CH), f32) + jnp.sum(err * err), pl.program_id(0) == 0)

    return pl.pallas_call(
        body, name=name, grid=(t // tb,),
        in_specs=[pl.BlockSpec((tb, D), lambda i: (i, 0)), pl.BlockSpec((tb, D), lambda i: (i, 0))],
        out_specs=[pl.BlockSpec((8, CH), lambda i: (0, 0)), pl.BlockSpec((tb, D), lambda i: (i, 0))],
        out_shape=[_sds((8, CH), f32), _sds((t, D), f32)],
        compiler_params=_cparams(1))(y, target)


def _tiles(t, seq):
    tm = min(512, t)
    return dict(tm=tm, tm_small=min(256, t), tm_large=min(1024, t), tm_huge=min(2048, t), tb=min(512, seq))


def local_step(x, target, depth, weights_of, seq, grads_done=None):
    t = x.shape[0]
    ts = _tiles(t, seq)
    tm, tl, th, tb = ts["tm"], ts["tm_large"], ts["tm_huge"], ts["tb"]
    saved, ws = [], []
    for l in range(depth):
        w = weights_of(l, x)
        ws.append(w)
        proj, h1 = norm_matmul(x, w["g1"], w["win"], th, 1920, bf16, "in_proj", token=w.get("token"))
        cat = group_a_fwd(proj, w["wa"], w["ga"], seq, tb, "group_a_fwd")
        xbcs, dact = conv_b_fwd(proj, w["ws"], w["bs"], seq, tb, "conv_b_fwd")
        cat, ypre, states = ssd_fwd(xbcs, proj, w["par"], w["gs"], cat, seq, "ssd_fwd")
        if "late" in w:
            w.update(w.pop("late")(cat))
        mix, x2 = matmul_postnorm(cat, w["wo"], x, w["g2"], tl, "out_proj")
        fp, h2, o, x3 = mlp_fwd(x2, w["g3"], w["wu"], w["wd"], w["g4"], tm, "mlp_fwd")
        saved.append(dict(x=x, proj=proj, h1=h1, xbcs=xbcs, dact=dact, ypre=ypre, states=states, cat=cat, mix=mix, x2=x2,
                          fp=fp, h2=h2, o=o))
        x = x3
    sse, dx = loss_head(x, target, tm, "loss_head")
    grads = [None] * depth
    for l in reversed(range(depth)):
        s, w = saved[l], ws[l]
        do, dfp, dx2, dg4, dg3 = mlp_bwd(s["o"], w["g4"], dx, w["wd"], s["fp"], w["wu"], s["x2"], w["g3"],
                                         tm, "mlp_bwd")
        dwd = matmul_tn(s["fp"], do, 512, 1024, True, "mlp_down_dw")
        dwu = matmul_tn(s["h2"], dfp, tl, 1024, False, "mlp_up_dw", col_blocks=True)
        dmix, dg2, dcat = postnorm_bwd_matmul(s["mix"], w["g2"], dx2, w["wo"], tl, 2 * D, "out_proj_bwd")
        dwo = matmul_tn(s["cat"], dmix, 512, 1024, False, "out_proj_dw")
        token = None if grads_done is None else grads_done(l, dict(wo=dwo, wu=dwu, wd=dwd), False)
        dproj, dwa, dga = group_a_bwd(s["proj"], dcat, w["wa"], w["ga"], seq, tb, "group_a_bwd", token=token)
        dxbcs, dproj, ddt, dpar, dgs = ssd_bwd(s["xbcs"], s["proj"], s["ypre"], s["states"], dcat, w["par"], w["gs"],
                                               dproj, seq, "ssd_bwd")
        dproj, dws, dbs = conv_b_bwd(s["proj"], dxbcs, s["dact"], w["ws"], dproj, seq, tb, "conv_b_bwd")
        dproj = place_columns(dproj, ddt, COL_DT // CH, tm, "place_ddt")
        dwin = matmul_tn(s["h1"], dproj, tl, 1152, False, "in_proj_dw")
        token = None if grads_done is None else grads_done(l, dict(win=dwin), True)
        dx, dg1 = matmul_prenorm_bwd(dproj, w["win"], s["x"], w["g1"], dx2, tm, "in_proj_bwd", token=token)
        grads[l] = dict(win=dwin, wo=dwo, wu=dwu, wd=dwd, wa=dwa, ws=dws, bs=dbs, par=dpar,
                        g1=dg1, ga=dga, gs=dgs, g2=dg2, g3=dg3, g4=dg4)
    return sse, dx, grads


GROUPS = {
    "chips": [(1, 0, 0), (0, 1, 0), (1, 1, 0)],
    "pair": [(0, 0, 1)],
    "all": [(1, 0, 0), (0, 1, 0), (1, 1, 0), (0, 0, 1), (1, 0, 1), (0, 1, 1), (1, 1, 1)],
}


def _group_index(group, x, y, c):
    return {"chips": 2 * x + y, "pair": c, "all": 4 * x + 2 * y + c}[group]


def _chunk_indices(shape, pieces):
    if len(shape) < 3:
        return [()]
    lead = [()]
    for n in shape[:-2]:
        lead = [i + (k,) for i in lead for k in range(n)]
    rows = shape[-2]
    split = max(1, pieces // len(lead))
    while split > 1 and (rows % split or (rows // split) % 16):
        split -= 1
    step = rows // split
    return [i + (pl.ds(s * step, step),) for i in lead for s in range(split)]


def _exchange(arrays, out_shapes, group, src_view, dst_view, view_shape, name, own, pieces=16):
    masks = GROUPS[group]
    na, nm = len(arrays), len(masks)
    cuts = [_chunk_indices(view_shape(a), pieces) for a in range(na)]

    def body(*refs):
        ins, outs = refs[:na], refs[na:2 * na]
        send_sems, recv_sems = refs[2 * na:2 * na + 2]
        local_sems = refs[2 * na + 2] if own else None
        x, y, c = lax.axis_index("x"), lax.axis_index("y"), lax.axis_index("c")
        me = _group_index(group, x, y, c)
        peers = []
        for mx, my, mc in masks:
            px, py, pc = (1 - x if mx else x), (1 - y if my else y), (1 - c if mc else c)
            peers.append(((px, py, pc), _group_index(group, px, py, pc)))

        def part(ref, idx):
            return ref.at[idx] if idx else ref

        if own:
            for a in range(na):
                for idx in cuts[a]:
                    pltpu.make_async_copy(part(src_view(ins[a], a, me), idx), part(dst_view(outs[a], a, me), idx),
                                          local_sems.at[a]).start()
        for a in range(na):
            for j, (dev, pidx) in enumerate(peers):
                for idx in cuts[a]:
                    pltpu.make_async_remote_copy(
                        src_ref=part(src_view(ins[a], a, pidx), idx), dst_ref=part(dst_view(outs[a], a, me), idx),
                        send_sem=send_sems.at[a * nm + j], recv_sem=recv_sems.at[a * nm + j],
                        device_id=dev, device_id_type=MESH).start()
        whole = []
        for a in range(na):
            for j, (dev, pidx) in enumerate(peers):
                whole.append(pltpu.make_async_remote_copy(
                    src_ref=src_view(ins[a], a, pidx), dst_ref=dst_view(outs[a], a, pidx),
                    send_sem=send_sems.at[a * nm + j], recv_sem=recv_sems.at[a * nm + j],
                    device_id=dev, device_id_type=MESH))
        for cp in whole:
            cp.wait_recv()
        for cp in whole:
            cp.wait_send()
        if own:
            for a in range(na):
                pltpu.make_async_copy(src_view(ins[a], a, me), dst_view(outs[a], a, me), local_sems.at[a]).wait()

    hbm = pl.BlockSpec(memory_space=pltpu.HBM)
    sems = [pltpu.SemaphoreType.DMA((na * nm,)), pltpu.SemaphoreType.DMA((na * nm,))]
    return pl.pallas_call(
        body, name=name, in_specs=[hbm] * na, out_specs=[hbm] * na,
        out_shape=[_sds(s, a.dtype) for s, a in zip(out_shapes, arrays)],
        scratch_shapes=sems + ([pltpu.SemaphoreType.DMA((na,))] if own else []))(*arrays)


def all_gather(arrays, group, name, slot_axis=0, own=True):
    n = len(GROUPS[group]) + 1
    shapes = [a.shape[:slot_axis] + (n,) + a.shape[slot_axis:] for a in arrays]
    lead = (slice(None),) * slot_axis
    return _exchange(arrays, shapes, group, lambda r, a, i: r, lambda r, a, i: r.at[lead + (i,)],
                     lambda a: arrays[a].shape, name, own)


HBM_SPEC = pl.BlockSpec(memory_space=pltpu.HBM)
SEM_SPEC = pl.BlockSpec(memory_space=pltpu.SEMAPHORE)
DATAFLOW = pltpu.SideEffectType.DATAFLOW_SIDE_EFFECTING
N_CHIPS = 4


def _peers(group, x, y, c):
    out = []
    for mx, my, mc in GROUPS[group]:
        px, py, pc = (1 - x if mx else x), (1 - y if my else y), (1 - c if mc else c)
        out.append(((px, py, pc), _group_index(group, px, py, pc)))
    return out


def _whole_views(sources):
    return dict(src=lambda ref, a, c, to: ref, dst=lambda ref, a, c, sender: ref.at[sender],
                rows=lambda a: sources[a].shape[0])


def _weight_views(shards):
    half = [s.shape[0] // 2 for s in shards]
    return dict(src=lambda ref, a, c, to_chip: ref.at[pl.ds(c * half[a], half[a])],
                dst=lambda ref, a, c, from_chip: ref.at[from_chip, pl.ds(c * half[a], half[a])],
                rows=lambda a: half[a])


def _grad_views(sums):
    return dict(src=lambda ref, a, c, to_chip: ref.at[to_chip], dst=lambda ref, a, c, from_chip: ref.at[from_chip],
                rows=lambda a: sums[a].shape[1])


def chips_start(sources, zones, views, name, pieces=4, after=None, group="chips"):
    na, nm = len(sources), len(GROUPS[group])

    def body(*refs):
        ins, lands = refs[:na], refs[na:2 * na]
        n_in = 2 * na + len(_token_arg(after))
        send_sems, recv_sems, token = refs[n_in], refs[n_in + 1], refs[-1]
        x, y, c = lax.axis_index("x"), lax.axis_index("y"), lax.axis_index("c")
        me = _group_index(group, x, y, c)
        for a in range(na):
            step = views["rows"](a) // pieces
            for j, (dev, to) in enumerate(_peers(group, x, y, c)):
                for q in range(pieces):
                    rows = pl.ds(q * step, step)
                    pltpu.make_async_remote_copy(
                        src_ref=views["src"](ins[a], a, c, to).at[rows],
                        dst_ref=views["dst"](lands[a], a, c, me).at[rows],
                        send_sem=send_sems.at[a * nm + j], recv_sem=recv_sems.at[a * nm + j],
                        device_id=dev, device_id_type=MESH).start()
        token[...] = jnp.zeros_like(token)

    both = list(sources) + list(zones)
    outs = pl.pallas_call(
        body, name=name,
        out_shape=(pltpu.SemaphoreType.DMA((na * nm,)), pltpu.SemaphoreType.DMA((na * nm,)),
                   *[pltpu.HBM(b.shape, b.dtype) for b in both], _sds((8, CH), f32)),
        in_specs=[HBM_SPEC] * (2 * na) + _token_spec(after),
        out_specs=(SEM_SPEC, SEM_SPEC, *[HBM_SPEC] * (2 * na), pl.BlockSpec(memory_space=pltpu.VMEM)),
        input_output_aliases={i: 2 + i for i in range(2 * na)},
        compiler_params=pltpu.CompilerParams(has_side_effects=DATAFLOW))(
            *[pltpu.with_memory_space_constraint(b, pltpu.HBM) for b in both], *_token_arg(after))
    return dict(send=outs[0], recv=outs[1], sources=list(outs[2:2 + na]), zones=list(outs[2 + na:2 + 2 * na]),
                token=outs[-1], views=views, group=group)


def chips_wait(started, after, name):
    sources, zones, views, group = started["sources"], started["zones"], started["views"], started["group"]
    na, nm = len(sources), len(GROUPS[group])

    def body(*refs):
        ins, lands = refs[:na], refs[na:2 * na]
        send_sems, recv_sems = refs[2 * na], refs[2 * na + 1]
        x, y, c = lax.axis_index("x"), lax.axis_index("y"), lax.axis_index("c")
        for a in range(na):
            for j, (dev, peer) in enumerate(_peers(group, x, y, c)):
                cp = pltpu.make_async_remote_copy(
                    src_ref=views["src"](ins[a], a, c, peer), dst_ref=views["dst"](lands[a], a, c, peer),
                    send_sem=send_sems.at[a * nm + j], recv_sem=recv_sems.at[a * nm + j],
                    device_id=dev, device_id_type=MESH)
                cp.wait_send()
                cp.wait_recv()

    both = list(sources) + list(zones)
    outs = pl.pallas_call(
        body, name=name, out_shape=tuple(pltpu.HBM(b.shape, b.dtype) for b in both),
        in_specs=[HBM_SPEC] * (2 * na) + [SEM_SPEC, SEM_SPEC, pl.BlockSpec(memory_space=pl.ANY)],
        out_specs=tuple([HBM_SPEC] * (2 * na)), input_output_aliases={i: i for i in range(2 * na)},
        compiler_params=pltpu.CompilerParams(has_side_effects=DATAFLOW))(*both, started["send"], started["recv"], after)
    return list(outs[:na]), list(outs[na:])


def weights_share(zones, name):
    na, nm = len(zones), N_CHIPS - 1

    def body(*refs):
        lands = refs[na:2 * na]
        send_sems, recv_sems = refs[2 * na:]
        x, y, c = lax.axis_index("x"), lax.axis_index("y"), lax.axis_index("c")
        chip = 2 * x + y
        sibling = (x, y, 1 - c)
        sends = []
        for a in range(na):
            half = zones[a].shape[1] // 2
            for m in range(1, N_CHIPS):
                mine = lands[a].at[chip ^ m, pl.ds(c * half, half)]
                sends.append(pltpu.make_async_remote_copy(
                    src_ref=mine, dst_ref=mine, send_sem=send_sems.at[a * nm + m - 1],
                    recv_sem=recv_sems.at[a * nm + m - 1], device_id=sibling, device_id_type=MESH))
        for cp in sends:
            cp.start()
        for a in range(na):
            half = zones[a].shape[1] // 2
            for m in range(1, N_CHIPS):
                theirs = lands[a].at[chip ^ m, pl.ds((1 - c) * half, half)]
                pltpu.make_async_remote_copy(
                    src_ref=theirs, dst_ref=theirs, send_sem=send_sems.at[a * nm + m - 1],
                    recv_sem=recv_sems.at[a * nm + m - 1], device_id=sibling, device_id_type=MESH).wait_recv()
        for cp in sends:
            cp.wait_send()

    return pl.pallas_call(
        body, name=name, in_specs=[HBM_SPEC] * na, out_specs=[HBM_SPEC] * na,
        out_shape=[_sds(z.shape, z.dtype) for z in zones], input_output_aliases={i: i for i in range(na)},
        scratch_shapes=[pltpu.SemaphoreType.DMA((na * nm,)), pltpu.SemaphoreType.DMA((na * nm,))])(*zones)


def pair_send_halves(grads, name):
    half = [g.shape[1] // 2 for g in grads]
    shapes = [(g.shape[0], h, g.shape[2]) for g, h in zip(grads, half)]
    return _exchange(grads, shapes, "pair", lambda r, a, i: r.at[:, pl.ds(i * half[a], half[a])],
                     lambda r, a, i: r, lambda a: shapes[a], name, False)


def sum_pair_half(g, recv, core, name, tb=256, by_chip=None):
    nk, r, c = g.shape
    tb = min(tb, r // 2)
    nb = r // 2 // tb

    def body(core_ref, g_ref, r_ref, o_ref):
        s = g_ref[...].astype(f32) + r_ref[...].astype(f32)
        if by_chip is None:
            o_ref[...] = s.astype(bf16)
        else:
            for k in range(by_chip[0]):
                o_ref[k] = s[:, k * by_chip[1]:(k + 1) * by_chip[1]].astype(bf16)

    if by_chip is None:
        out_spec = pl.BlockSpec((None, tb, c), lambda k, i, core_ref: (k, i, 0))
        out_shape = _sds((nk, r // 2, c), bf16)
    else:
        assert nk == 1
        out_spec = pl.BlockSpec((by_chip[0], tb, by_chip[1]), lambda k, i, core_ref: (0, i, 0))
        out_shape = _sds((by_chip[0], r // 2, by_chip[1]), bf16)
    return pl.pallas_call(
        body, name=name,
        grid_spec=pltpu.PrefetchScalarGridSpec(
            num_scalar_prefetch=1, grid=(nk, nb),
            in_specs=[pl.BlockSpec((None, tb, c), lambda k, i, core_ref: (k, core_ref[0] * nb + i, 0)),
                      pl.BlockSpec((None, tb, c), lambda k, i, core_ref: (k, i, 0))],
            out_specs=out_spec),
        out_shape=out_shape, compiler_params=_cparams(2))(jnp.reshape(core, (1,)).astype(jnp.int32), g, recv)


def assemble_columns(blocks, width, name, tb=256):
    n, r, c = blocks.shape

    def body(b_ref, o_ref):
        for k in range(n):
            o_ref[:, k * c:(k + 1) * c] = b_ref[k]
        o_ref[:, n * c:] = jnp.zeros((tb, width - n * c), blocks.dtype)

    return pl.pallas_call(
        body, name=name, grid=(r // tb,), in_specs=[pl.BlockSpec((n, tb, c), lambda i: (0, i, 0))],
        out_specs=pl.BlockSpec((tb, width), lambda i: (i, 0)), out_shape=_sds((r, width), blocks.dtype),
        compiler_params=_cparams(1))(blocks)


def chip_sum_into(acc, layer, own, others, chip, name, tb=256):
    n, r, c = own.shape
    tb = min(tb, r)

    def body(chip_ref, x_ref, y1_ref, y2_ref, y3_ref, acc_ref, o_ref):
        o_ref[...] = ((x_ref[...].astype(f32) + y1_ref[...].astype(f32)) + y2_ref[...].astype(f32)) + y3_ref[...].astype(f32)

    def slot(k):
        return pl.BlockSpec((None, tb, c), lambda i, chip_ref: (chip_ref[0] ^ k, i, 0))

    return pl.pallas_call(
        body, name=name,
        grid_spec=pltpu.PrefetchScalarGridSpec(
            num_scalar_prefetch=1, grid=(r // tb,),
            in_specs=[slot(k) for k in range(n)] + [pl.BlockSpec(memory_space=pl.ANY)],
            out_specs=pl.BlockSpec((None, tb, c), lambda i, chip_ref: (layer, i, 0))),
        out_shape=_sds(acc.shape, f32), input_output_aliases={n + 1: 0}, compiler_params=_cparams(1))(
            jnp.reshape(chip, (1,)).astype(jnp.int32), own, *([others] * (n - 1)), acc)


def adamw_half(w, g_half, m, v, half, name, before=None, token=None, tb=256):
    depth, r, c = w.shape
    tb = min(tb, r // 2)
    nb = r // 2 // tb
    n_extra = (0 if before is None else 4) + len(_token_arg(token))

    def body(half_ref, w_ref, gh_ref, m_ref, v_ref, *rest):
        g_ref, d_ref, mo_ref, vo_ref = rest[n_extra:]
        gv = gh_ref[...]
        m2 = B1 * m_ref[...] + (1.0 - B1) * gv
        v2 = B2 * v_ref[...] + (1.0 - B2) * (gv * gv)
        m_hat = m2 / (1.0 - B1 ** STEP)
        v_hat = v2 / (1.0 - B2 ** STEP)
        g_ref[...] = gv
        d_ref[...] = -LR * (m_hat / (jnp.sqrt(v_hat) + AEPS) + WD * w_ref[...])
        mo_ref[...] = m2
        vo_ref[...] = v2

    whole = pl.BlockSpec((None, tb, c), lambda l, i, half_ref: (l, half_ref[0] * nb + i, 0))
    part = pl.BlockSpec((None, tb, c), lambda l, i, half_ref: (l, i, 0))
    extra = ([] if before is None else list(before)) + _token_arg(token)
    return pl.pallas_call(
        body, name=name,
        grid_spec=pltpu.PrefetchScalarGridSpec(
            num_scalar_prefetch=1, grid=(depth, nb),
            in_specs=[whole, part, whole, whole] + [pl.BlockSpec(memory_space=pl.ANY)] * n_extra, out_specs=[whole] * 4),
        out_shape=[_sds(w.shape, f32)] * 4,
        input_output_aliases={} if before is None else {5 + k: k for k in range(4)},
        compiler_params=_cparams(2))(jnp.reshape(half, (1,)).astype(jnp.int32), w, g_half, m, v, *extra)


def sum_slots(y, out_dtype, name, tb=256):
    n, r, c = y.shape
    tb = min(tb, r)

    def body(y_ref, o_ref):
        acc = y_ref[0].astype(f32)
        for i in range(1, n):
            acc = acc + y_ref[i].astype(f32)
        o_ref[...] = acc.astype(out_dtype)

    return pl.pallas_call(
        body, name=name, grid=(r // tb,),
        in_specs=[pl.BlockSpec((n, tb, c), lambda i: (0, i, 0))], out_specs=pl.BlockSpec((tb, c), lambda i: (i, 0)),
        out_shape=_sds((r, c), out_dtype), compiler_params=_cparams(1))(y)


def adamw(w, g, m, v, name, tb=256):
    r, c = w.shape
    tb = min(tb, r)

    def body(w_ref, g_ref, m_ref, v_ref, d_ref, mo_ref, vo_ref):
        gv = g_ref[...]
        m2 = B1 * m_ref[...] + (1.0 - B1) * gv
        v2 = B2 * v_ref[...] + (1.0 - B2) * (gv * gv)
        m_hat = m2 / (1.0 - B1 ** STEP)
        v_hat = v2 / (1.0 - B2 ** STEP)
        d_ref[...] = -LR * (m_hat / (jnp.sqrt(v_hat) + AEPS) + WD * w_ref[...])
        mo_ref[...] = m2
        vo_ref[...] = v2

    spec = pl.BlockSpec((tb, c), lambda i: (i, 0))
    return pl.pallas_call(
        body, name=name, grid=(r // tb,), in_specs=[spec] * 4, out_specs=[spec] * 3,
        out_shape=[_sds((r, c), f32)] * 3, compiler_params=_cparams(1))(w, g, m, v)


def adamw_leading(w, g, m, v, name, tc=64):
    c, l, r = w.shape
    main = c // tc
    tail = c - main * tc

    def body(w_ref, g_ref, m_ref, v_ref, *rest):
        d_ref, mo_ref, vo_ref = rest[-3:]
        gv = g_ref[...]
        m2 = B1 * m_ref[...] + (1.0 - B1) * gv
        v2 = B2 * v_ref[...] + (1.0 - B2) * (gv * gv)
        m_hat = m2 / (1.0 - B1 ** STEP)
        v_hat = v2 / (1.0 - B2 ** STEP)
        d_ref[...] = -LR * (m_hat / (jnp.sqrt(v_hat) + AEPS) + WD * w_ref[...])
        mo_ref[...] = m2
        vo_ref[...] = v2

    spec = pl.BlockSpec((tc, l, r), lambda i: (i, 0, 0))
    outs = pl.pallas_call(
        functools.partial(body), name=name, grid=(main,), in_specs=[spec] * 4, out_specs=[spec] * 3,
        out_shape=[_sds(w.shape, f32)] * 3, compiler_params=_cparams(1))(w, g, m, v)
    if tail:
        assert (main * tc) % tail == 0
        last = pl.BlockSpec((tail, l, r), lambda i: (main * tc // tail, 0, 0))
        outs = pl.pallas_call(
            functools.partial(body), name=name + "_tail", grid=(1,),
            in_specs=[last] * 4 + [pl.BlockSpec(memory_space=pl.ANY)] * 3, out_specs=[last] * 3,
            out_shape=[_sds(w.shape, f32)] * 3, input_output_aliases={4: 0, 5: 1, 6: 2},
            compiler_params=_cparams(1))(w, g, m, v, *outs)
    return outs


SMALL_ROW = 1024
SMALL_GAINS = ("g1", "ga", "gs", "g2", "g3", "g4")
SMALL_LAYER_ROWS = 8 + 8 + 16 + 8


def _pack_small(grads):
    wide = lambda a: jnp.pad(a, ((0, 0), (0, 2 * SMALL_ROW - a.shape[1]))).reshape(-1, SMALL_ROW)
    row = lax.broadcasted_iota(jnp.int32, (8, SMALL_ROW), 0)
    parts = []
    for g in grads:
        singles = [g[k] for k in SMALL_GAINS] + [g["bs"][:, :SMALL_ROW],
                                                 jnp.pad(g["bs"][:, SMALL_ROW:], ((0, 0), (0, 2 * SMALL_ROW - XBC)))]
        first = sum(jnp.where(row == k, s, 0.0) for k, s in enumerate(singles))
        parts += [first, g["wa"], wide(g["ws"]), jnp.pad(g["par"], ((0, 0), (0, SMALL_ROW - CH)))]
    return jnp.concatenate(parts, axis=0)


def _unpack_small(packed, depth):
    rows = packed.reshape(depth, SMALL_LAYER_ROWS, SMALL_ROW)
    out = {k: rows[:, i] for i, k in enumerate(SMALL_GAINS)}
    out["bs"] = rows[:, 6:8].reshape(depth, 2 * SMALL_ROW)[:, :XBC]
    out["wa"] = rows[:, 8:11]
    out["ws"] = rows[:, 16:32].reshape(depth, 8, 2 * SMALL_ROW)[:, :4, :XBC]
    out["par"] = rows[:, 32:35, :CH]
    return out


def kernel(x, norm_mix_pre, w_in, conv_a_w, ssm_conv_w, ssm_conv_b, dt_bias, a_log, d_skip, conv_out_norm, ssm_out_norm, w_out, norm_mix_post, norm_mlp_pre, w_up, w_down, norm_mlp_post, loss_target, m_norm_mix_pre, m_w_in, m_conv_a_w, m_ssm_conv_w, m_ssm_conv_b, m_dt_bias, m_a_log, m_d_skip, m_conv_out_norm, m_ssm_out_norm, m_w_out, m_norm_mix_post, m_norm_mlp_pre, m_w_up, m_w_down, m_norm_mlp_post, v_norm_mix_pre, v_w_in, v_conv_a_w, v_ssm_conv_w, v_ssm_conv_b, v_dt_bias, v_a_log, v_d_skip, v_conv_out_norm, v_ssm_out_norm, v_w_out, v_norm_mix_post, v_norm_mlp_pre, v_w_up, v_w_down, v_norm_mlp_post):
    nb, seq, _ = x.shape
    t = nb * seq
    depth = w_in.shape[0]
    ncol = w_in.shape[2]
    chip = 2 * lax.axis_index("x") + lax.axis_index("y")

    taps = [conv_a_w, ssm_conv_w]
    taps_g = all_gather(taps, "chips", "gather_taps", slot_axis=1, own=False)
    wa_g, ws_g = [lax.dynamic_update_index_in_dim(g, s, chip, 1) for g, s in zip(taps_g, taps)]
    wa_full = jnp.transpose(wa_g, (0, 2, 1, 3)).reshape(depth, 3, D)
    ws_full = jnp.transpose(ws_g, (0, 2, 1, 3)).reshape(depth, 4, XBC)
    lane_pad = lambda a: jnp.pad(a, ((0, 0), (0, CH - a.shape[1])))
    par = jnp.stack([lane_pad(dt_bias), lane_pad(a_log), lane_pad(d_skip)], axis=1)
    par = jnp.pad(par, ((0, 0), (0, 5), (0, 0)))

    layer_shards = lambda l: [w_in[l].astype(bf16), w_out[l].astype(bf16), w_up[l].astype(bf16), w_down[l].astype(bf16)]
    issued = []

    def start(shards, name, whole=False):
        zones = [lax.empty((N_CHIPS,) + s.shape, s.dtype) for s in shards]
        issued.append(chips_start(shards, zones, _whole_views(shards) if whole else _weight_views(shards), name,
                                  after=issued[-1]["token"] if issued else taps_g[0]))
        issued[-1]["whole"] = whole
        return issued[-1]

    def finish(started, after, name):
        shards, zones = chips_wait(started, after, name)
        if not started["whole"]:
            zones = weights_share(zones, "weights_share")
        return [lax.dynamic_update_index_in_dim(z, s, chip, 0) for z, s in zip(zones, shards)]

    def shaped(mats):
        wo_z, wu_z, wd_z = mats
        return wo_z.reshape(2 * D, D), wu_z, wd_z.reshape(DFF, D)

    first = layer_shards(0)
    travelling = {0: start(first[:1], "weights_start_0")}
    rest = start(first[1:], "weights_start_0_rest")
    for l in range(1, depth):
        travelling[l] = start(layer_shards(l), f"weights_start_{l}", whole=l >= 2)

    def weights_of(l, x_in):
        mats = finish(travelling.pop(l), x_in, f"weights_wait_{l}")
        w = dict(win=assemble_columns(mats[0], PROJ, "assemble_w_in"), wa=jnp.pad(wa_full[l], ((0, 5), (0, 0))),
                 ws=jnp.pad(ws_full[l], ((0, 4), (0, 0))), bs=ssm_conv_b[l][None], par=par[l],
                 g1=norm_mix_pre[l][None], ga=conv_out_norm[l][None], gs=ssm_out_norm[l][None],
                 g2=norm_mix_post[l][None], g3=norm_mlp_pre[l][None], g4=norm_mlp_post[l][None])
        if l == 0:
            w["token"] = issued[-1]["token"]
            w["late"] = lambda after: dict(zip(("wo", "wu", "wd"), shaped(finish(rest, after, "weights_wait_0_rest"))))
        else:
            w.update(zip(("wo", "wu", "wd"), shaped(mats[1:])))
        return w

    core = lax.axis_index("c")
    grads_travelling = {}
    given_m = dict(win=m_w_in, wo=m_w_out, wu=m_w_up, wd=m_w_down)
    given_v = dict(win=v_w_in, wo=v_w_out, wu=v_w_up, wd=v_w_down)

    chip_major = dict(win=lambda a: a[None], wo=lambda a: a.reshape(N_CHIPS, 2 * D // N_CHIPS, D), wu=lambda a: a,
                      wd=lambda a: a.reshape(N_CHIPS, DFF // N_CHIPS, D))
    held = {}

    to_pair = {}

    def pair_sums_to_chips(l, keys, mats, received, after=None):
        sums = [sum_pair_half(m_, r_, core, "pair_sum", by_chip=(N_CHIPS, ncol) if k == "win" else None)
                for k, m_, r_ in zip(keys, mats, received)]
        zones = [lax.empty(s.shape, s.dtype) for s in sums]
        started = chips_start(sums, zones, _grad_views(sums), f"grads_start_{l}_{len(grads_travelling)}", after=after)
        grads_travelling[(l, keys[0])] = (keys, started)
        return started["token"]

    def grads_done(l, g, last):
        token = None
        if not last and l + 1 in to_pair:
            keys, started = to_pair.pop(l + 1)
            mats, received = chips_wait(started, g["wo"], f"grads_to_pair_wait_{l + 1}")
            token = pair_sums_to_chips(l + 1, keys, mats, received)
        if l > 0 and not last:
            held[l] = g
            return token
        g = {**held.pop(l, {}), **g}
        keys = [k for k in ("win", "wo", "wu", "wd") if k in g]
        mats = [chip_major[k](g[k]) for k in keys]
        if l > 0:
            half = [m_.shape[1] // 2 for m_ in mats]
            views = dict(src=lambda ref, a, c, to: ref.at[:, pl.ds(to * half[a], half[a])],
                         dst=lambda ref, a, c, sender: ref, rows=lambda a: mats[a].shape[0])
            zones = [lax.empty((m_.shape[0], h, m_.shape[2]), m_.dtype) for m_, h in zip(mats, half)]
            to_pair[l] = (keys, chips_start(mats, zones, views, f"grads_to_pair_start_{l}", pieces=1, group="pair"))
            return to_pair[l][1]["token"]
        return pair_sums_to_chips(l, keys, mats, pair_send_halves(mats, "grads_to_pair"), after=token)

    sse, dx, grads = local_step(x.reshape(t, D), loss_target.reshape(t, D), depth, weights_of, seq, grads_done)
    loss = lax.psum(0.5 / D * sse[0, 0], ("x", "y", "c"))

    packed = _pack_small(grads)
    small_travelling = chips_start([packed], [lax.empty((8,) + packed.shape, f32)], _whole_views([packed]),
                                   "small_start", group="all")

    big_w = dict(win=w_in, wo=w_out, wu=w_up, wd=w_down)
    acc = {k: lax.empty((depth, bw.shape[1] // 2, bw.shape[2]), f32) for k, bw in big_w.items()}
    for n, ((l, _), (keys, started)) in enumerate(grads_travelling.items()):
        sums, zones = chips_wait(started, small_travelling["token"], f"grads_wait_{l}_{n}")
        for k, s, z in zip(keys, sums, zones):
            acc[k] = chip_sum_into(acc[k], l, s, z, chip, "chip_sum")
    names = ("win", "wo", "wu", "wd")
    acc = [acc[k] for k in names]
    pair_views = dict(src=lambda ref, a, c, to: ref, dst=lambda ref, a, c, sender: ref, rows=lambda a: depth)
    to_sibling = chips_start(acc, [lax.empty(a.shape, f32) for a in acc], pair_views, "grads_from_pair_start",
                             pieces=depth, group="pair")
    own_done = {}
    for k, a in zip(names, to_sibling["sources"]):
        if big_w[k].shape[-1] % CH == 0:
            own_done[k] = adamw_half(big_w[k], a, given_m[k], given_v[k], core, "adamw_matrix",
                                     token=to_sibling["token"])

    (packed,), (small_all,) = chips_wait(small_travelling, own_done["wd"][1], "small_wait")
    small_all = lax.dynamic_update_index_in_dim(small_all, packed, 4 * lax.axis_index("x") + 2 * lax.axis_index("y") + core, 0)
    small = _unpack_small(sum_slots(small_all, f32, "small_sum", tb=8), depth)
    wa_cols, ws_cols = conv_a_w.shape[2], ssm_conv_w.shape[2]
    par_g = small["par"].reshape(depth, 3, CH)
    g_small = dict(
        norm_mix_pre=small["g1"], conv_out_norm=small["ga"], ssm_out_norm=small["gs"], norm_mix_post=small["g2"],
        norm_mlp_pre=small["g3"], norm_mlp_post=small["g4"], ssm_conv_b=small["bs"],
        conv_a_w=lax.dynamic_slice_in_dim(small["wa"].reshape(depth, 3, D), chip * wa_cols, wa_cols, axis=2),
        ssm_conv_w=lax.dynamic_slice_in_dim(small["ws"].reshape(depth, 4, XBC), chip * ws_cols, ws_cols, axis=2),
        dt_bias=par_g[:, 0, :NH], a_log=par_g[:, 1, :NH], d_skip=par_g[:, 2, :NH])

    given = dict(norm_mix_pre=(norm_mix_pre, m_norm_mix_pre, v_norm_mix_pre), w_in=(w_in, m_w_in, v_w_in),
                 conv_a_w=(conv_a_w, m_conv_a_w, v_conv_a_w), ssm_conv_w=(ssm_conv_w, m_ssm_conv_w, v_ssm_conv_w),
                 ssm_conv_b=(ssm_conv_b, m_ssm_conv_b, v_ssm_conv_b), dt_bias=(dt_bias, m_dt_bias, v_dt_bias),
                 a_log=(a_log, m_a_log, v_a_log), d_skip=(d_skip, m_d_skip, v_d_skip),
                 conv_out_norm=(conv_out_norm, m_conv_out_norm, v_conv_out_norm),
                 ssm_out_norm=(ssm_out_norm, m_ssm_out_norm, v_ssm_out_norm), w_out=(w_out, m_w_out, v_w_out),
                 norm_mix_post=(norm_mix_post, m_norm_mix_post, v_norm_mix_post),
                 norm_mlp_pre=(norm_mlp_pre, m_norm_mlp_pre, v_norm_mlp_pre), w_up=(w_up, m_w_up, v_w_up),
                 w_down=(w_down, m_w_down, v_w_down), norm_mlp_post=(norm_mlp_post, m_norm_mlp_post, v_norm_mlp_post))
    order = ["norm_mix_pre", "w_in", "conv_a_w", "ssm_conv_w", "ssm_conv_b", "dt_bias", "a_log", "d_skip",
             "conv_out_norm", "ssm_out_norm", "w_out", "norm_mix_post", "norm_mlp_pre", "w_up", "w_down",
             "norm_mlp_post"]
    short = dict(w_in="win", w_out="wo", w_up="wu", w_down="wd")
    results = {}
    for n in order:
        if n in short:
            continue
        wv, mv, vv = given[n]
        gv = g_small[n].reshape(wv.shape)
        two_d = lambda a: a.reshape(-1, a.shape[-1])
        results[n] = (gv,) + tuple(adamw(two_d(wv), two_d(gv), two_d(mv), two_d(vv), "adamw"))

    acc, from_sibling = chips_wait(to_sibling, results["norm_mlp_post"][1], "grads_from_pair_wait")
    for n, k in short.items():
        wv, mv, vv = given[n]
        own, recv = acc[names.index(k)], from_sibling[names.index(k)]
        if k in own_done:
            results[n] = adamw_half(wv, recv, mv, vv, 1 - core, "adamw_matrix", before=own_done[k])
        else:
            to_cols, to_rows = (lambda a: jnp.transpose(a, (2, 0, 1))), (lambda a: jnp.transpose(a, (1, 2, 0)))
            own_c, recv_c = to_cols(own), to_cols(recv)
            g_cols = jnp.where(core == 0, jnp.concatenate([own_c, recv_c], axis=2),
                               jnp.concatenate([recv_c, own_c], axis=2))
            results[n] = tuple(to_rows(o) for o in (g_cols,) + tuple(adamw_leading(to_cols(wv), g_cols, to_cols(mv),
                                                                                   to_cols(vv), "adamw_cols")))
    g_out, d_out, m_out, v_out = [], [], [], []
    for n in order:
        wv = given[n][0]
        gv, dlt, m2, v2 = results[n]
        g_out.append(gv.reshape(wv.shape))
        d_out.append(dlt.reshape(wv.shape))
        m_out.append(m2.reshape(wv.shape))
        v_out.append(v2.reshape(wv.shape))
    return (loss, dx.reshape(nb, seq, D), *g_out, *d_out, *m_out, *v_out)
```

```python
import functools

import jax
import jax.numpy as jnp
from jax import lax
from jax.experimental import pallas as pl
from jax.experimental.pallas import tpu as pltpu

f32, bf16 = jnp.float32, jnp.bfloat16

D = 1024
NH, HP = 16, 64
NG, NS = 2, 128
CH = 128
XBC = D + 2 * NG * NS
DFF = 4 * D
IN_COLS = 3 * D + D + XBC + NH
PROJ = 5760
COL_Z, COL_XBC, COL_DT = 3 * D, 4 * D, 4 * D + XBC
EPS = 1e-6
HALO = 8
HBLK = 16
VMEM_LIMIT = 56 * 2**20
MESH = pl.DeviceIdType.MESH

LR, B1, B2, AEPS, WD, STEP = 0.001, 0.9, 0.999, 1e-08, 0.01, 10


def _cparams(n_axes):
    return pltpu.CompilerParams(dimension_semantics=("arbitrary",) * n_axes, vmem_limit_bytes=VMEM_LIMIT)


def _sds(shape, dtype):
    return jax.ShapeDtypeStruct(tuple(shape), dtype)


def _token_spec(token):
    return [] if token is None else [pl.BlockSpec(memory_space=pl.ANY)]


def _token_arg(token):
    return [] if token is None else [token]


def _resident(shape):
    return pl.BlockSpec(shape, lambda i: (0,) * len(shape), pipeline_mode=pl.Buffered(1))


def _rms_fwd(x, g):
    r = lax.rsqrt(jnp.mean(x * x, axis=-1, keepdims=True) + EPS)
    return x * r * g


def _rms_bwd(x, g, dy):
    r = lax.rsqrt(jnp.mean(x * x, axis=-1, keepdims=True) + EPS)
    xh = x * r
    gdy = dy * g
    dx = r * (gdy - xh * jnp.mean(xh * gdy, axis=-1, keepdims=True))
    return dx, dy * xh


def _accum(ref, part, first):
    @pl.when(first)
    def _():
        ref[...] = part

    @pl.when(jnp.logical_not(first))
    def _():
        ref[...] += part


def _dot_nt(a, b):
    return lax.dot_general(a, b, (((1,), (1,)), ((), ())), preferred_element_type=f32)


def _dot_tn(a, b):
    return lax.dot_general(a, b, (((0,), (0,)), ((), ())), preferred_element_type=f32)


def _dot(a, b):
    return jnp.dot(a, b, preferred_element_type=f32)


def _split_dot(x, e_bf, n_split, nt=False):
    acc = None
    rem = x
    for s in range(n_split):
        hi = rem.astype(bf16)
        term = _dot_nt(hi, e_bf) if nt else _dot(hi, e_bf)
        acc = term if acc is None else acc + term
        if s + 1 < n_split:
            rem = rem - hi.astype(f32)
    return acc


def _sigmoid(x):
    return 0.5 * jnp.tanh(0.5 * x) + 0.5


def norm_matmul(x, g, w, tm, tn, out_dtype, name, token=None):
    t, n = x.shape[0], w.shape[1]
    w_spec = pl.BlockSpec((D, tn), lambda i, j: (0, j))

    def body(x_ref, g_ref, w_ref, *rest):
        o_ref, h_ref = rest[-2:]

        @pl.when(pl.program_id(1) == 0)
        def _():
            gv = g_ref[...]

            def chunk(n, carry):
                rows = pl.ds(pl.multiple_of(n * ROWS_A, ROWS_A), ROWS_A)
                h_ref[rows, :] = _rms_fwd(x_ref[rows, :], gv).astype(bf16)
                return carry

            lax.fori_loop(0, tm // ROWS_A, chunk, 0, unroll=UNROLL_NORM)

        o_ref[...] = _dot(h_ref[...], w_ref[...]).astype(out_dtype)

    return pl.pallas_call(
        body, name=name, grid=(t // tm, n // tn),
        in_specs=[pl.BlockSpec((tm, D), lambda i, j: (i, 0)), pl.BlockSpec((1, D), lambda i, j: (0, 0)), w_spec]
        + _token_spec(token),
        out_specs=[pl.BlockSpec((tm, tn), lambda i, j: (i, j)), pl.BlockSpec((tm, D), lambda i, j: (i, 0))],
        out_shape=[_sds((t, n), out_dtype), _sds((t, D), bf16)],
        compiler_params=_cparams(2))(x, g, w, *_token_arg(token))


def matmul_postnorm(a, w, xres, g, tm, name):
    t, k = a.shape

    def body(a_ref, w_ref, xr_ref, g_ref, y_ref, xo_ref):
        y = _dot(a_ref[...], w_ref[...])
        y_ref[...] = y.astype(bf16)
        xo_ref[...] = xr_ref[...] + _rms_fwd(y, g_ref[...])

    return pl.pallas_call(
        body, name=name, grid=(t // tm,),
        in_specs=[pl.BlockSpec((tm, k), lambda i: (i, 0)), _resident(w.shape),
                  pl.BlockSpec((tm, D), lambda i: (i, 0)), pl.BlockSpec((1, D), lambda i: (0, 0))],
        out_specs=[pl.BlockSpec((tm, D), lambda i: (i, 0)), pl.BlockSpec((tm, D), lambda i: (i, 0))],
        out_shape=[_sds((t, D), bf16), _sds((t, D), f32)],
        compiler_params=_cparams(1))(a, w, xres, g)


def postnorm_bwd_matmul(y, g, dxo, w, tm, tn, name):
    t, n = y.shape[0], w.shape[0]

    def body(y_ref, g_ref, dxo_ref, w_ref, dy_ref, dg_ref, da_ref):
        i, j = pl.program_id(0), pl.program_id(1)

        @pl.when(j == 0)
        def _():
            gv = g_ref[...]

            def chunk(n, acc):
                rows = pl.ds(pl.multiple_of(n * ROWS_A, ROWS_A), ROWS_A)
                dx, dgc = _rms_bwd(_rd(y_ref, rows), gv, dxo_ref[rows, :])
                dy_ref[rows, :] = dx.astype(bf16)
                return acc + _fold8(dgc)

            acc = lax.fori_loop(0, tm // ROWS_A, chunk, jnp.zeros((8, D), f32), unroll=UNROLL_NORM)
            _accum(dg_ref, jnp.sum(acc, axis=0, keepdims=True), i == 0)

        da_ref[...] = _dot_nt(dy_ref[...], w_ref[...]).astype(bf16)

    return pl.pallas_call(
        body, name=name, grid=(t // tm, n // tn),
        in_specs=[pl.BlockSpec((tm, D), lambda i, j: (i, 0)), pl.BlockSpec((1, D), lambda i, j: (0, 0)),
                  pl.BlockSpec((tm, D), lambda i, j: (i, 0)), pl.BlockSpec((tn, D), lambda i, j: (j, 0))],
        out_specs=[pl.BlockSpec((tm, D), lambda i, j: (i, 0)), pl.BlockSpec((1, D), lambda i, j: (0, 0)),
                   pl.BlockSpec((tm, tn), lambda i, j: (i, j))],
        out_shape=[_sds((t, D), bf16), _sds((1, D), f32), _sds((t, n), bf16)],
        compiler_params=_cparams(2))(y, g, dxo, w)


def matmul_prenorm_bwd(da, w, x, g, dxo, tm, name, token=None):
    t, k = da.shape

    def body(da_ref, w_ref, x_ref, g_ref, dxo_ref, *rest):
        dx_ref, dg_ref = rest[-2:]
        dh = _dot_nt(da_ref[...], w_ref[...])
        dxn, dgc = _rms_bwd(x_ref[...], g_ref[...], dh)
        dx_ref[...] = dxo_ref[...] + dxn
        _accum(dg_ref, jnp.sum(dgc, axis=0, keepdims=True), pl.program_id(0) == 0)

    return pl.pallas_call(
        body, name=name, grid=(t // tm,),
        in_specs=[pl.BlockSpec((tm, k), lambda i: (i, 0)), _resident(w.shape),
                  pl.BlockSpec((tm, D), lambda i: (i, 0)), pl.BlockSpec((1, D), lambda i: (0, 0)),
                  pl.BlockSpec((tm, D), lambda i: (i, 0))] + _token_spec(token),
        out_specs=[pl.BlockSpec((tm, D), lambda i: (i, 0)), pl.BlockSpec((1, D), lambda i: (0, 0))],
        out_shape=[_sds((t, D), f32), _sds((1, D), f32)],
        compiler_params=_cparams(1))(da, w, x, g, dxo, *_token_arg(token))


def mlp_fwd(x, g_pre, wu, wd, g_post, tm, name):
    t = x.shape[0]
    nq, _, fc = wu.shape

    def body(x_ref, gp_ref, wu_ref, wd_ref, gq_ref, fp_ref, h_ref, o_ref, xo_ref):
        xv = x_ref[...]
        h = _rms_fwd(xv, gp_ref[...]).astype(bf16)
        h_ref[...] = h
        o = None
        for q in range(nq):
            fq = _dot(h, wu_ref[q])
            fp_ref[:, q * fc:(q + 1) * fc] = fq.astype(bf16)
            r = jnp.maximum(fq, 0.0)
            part = _dot((r * r).astype(bf16), wd_ref[q * fc:(q + 1) * fc, :])
            o = part if o is None else o + part
        o_ref[...] = o.astype(bf16)
        xo_ref[...] = xv + _rms_fwd(o, gq_ref[...])

    row = lambda c: pl.BlockSpec((tm, c), lambda i: (i, 0))
    vec = pl.BlockSpec((1, D), lambda i: (0, 0))
    return pl.pallas_call(
        body, name=name, grid=(t // tm,),
        in_specs=[row(D), vec, _resident(wu.shape), _resident(wd.shape), vec],
        out_specs=[row(nq * fc), row(D), row(D), row(D)],
        out_shape=[_sds((t, nq * fc), bf16), _sds((t, D), bf16), _sds((t, D), bf16), _sds((t, D), f32)],
        compiler_params=_cparams(1))(x, g_pre, wu, wd, g_post)


def mlp_bwd(o, g_post, dxo, wd, fp, wu, x, g_pre, tm, name):
    t = x.shape[0]
    nq, _, fc = wu.shape

    def body(o_ref, gq_ref, dxo_ref, wd_ref, fp_ref, wu_ref, x_ref, gp_ref, do_ref, dfp_ref, dx_ref, dgq_ref, dgp_ref):
        i = pl.program_id(0)
        dxo_v = dxo_ref[...]
        do, dgq = _rms_bwd(o_ref[...].astype(f32), gq_ref[...], dxo_v)
        do_b = do.astype(bf16)
        do_ref[...] = do_b
        dh = None
        for q in range(nq):
            cols = slice(q * fc, (q + 1) * fc)
            dq = _dot_nt(do_b, wd_ref[cols, :]) * (2.0 * jnp.maximum(fp_ref[:, cols].astype(f32), 0.0))
            dq_b = dq.astype(bf16)
            dfp_ref[:, cols] = dq_b
            part = _dot_nt(dq_b, wu_ref[q])
            dh = part if dh is None else dh + part
        dxn, dgp = _rms_bwd(x_ref[...], gp_ref[...], dh)
        dx_ref[...] = dxo_v + dxn
        _accum(dgq_ref, jnp.sum(dgq, axis=0, keepdims=True), i == 0)
        _accum(dgp_ref, jnp.sum(dgp, axis=0, keepdims=True), i == 0)

    row = lambda c: pl.BlockSpec((tm, c), lambda i: (i, 0))
    vec = pl.BlockSpec((1, D), lambda i: (0, 0))
    return pl.pallas_call(
        body, name=name, grid=(t // tm,),
        in_specs=[row(D), vec, row(D), _resident(wd.shape), row(nq * fc), _resident(wu.shape), row(D), vec],
        out_specs=[row(D), row(nq * fc), row(D), vec, vec],
        out_shape=[_sds((t, D), bf16), _sds((t, nq * fc), bf16), _sds((t, D), f32), _sds((1, D), f32), _sds((1, D), f32)],
        compiler_params=_cparams(1))(o, g_post, dxo, wd, fp, wu, x, g_pre)


def matmul_tn(a, b, tm, tn, relu2, name, col_blocks=False):
    t, m = a.shape
    n = b.shape[1]
    if col_blocks:
        out_spec, out_shape = pl.BlockSpec((None, tm, tn), lambda i, j: (j, i, 0)), _sds((n // tn, m, tn), bf16)
    else:
        out_spec, out_shape = pl.BlockSpec((tm, tn), lambda i, j: (i, j)), _sds((m, n), bf16)

    def body(a_ref, b_ref, o_ref, at_ref):
        @pl.when(pl.program_id(1) == 0)
        def _():
            av = a_ref[...]
            if relu2:
                af = jnp.maximum(av.astype(f32), 0.0)
                av = (af * af).astype(bf16)
            at_ref[...] = av.T

        o_ref[...] = _dot(at_ref[...], b_ref[...]).astype(bf16)

    return pl.pallas_call(
        body, name=name, grid=(m // tm, n // tn),
        in_specs=[pl.BlockSpec((t, tm), lambda i, j: (0, i)), pl.BlockSpec((t, tn), lambda i, j: (0, j))],
        out_specs=out_spec, out_shape=out_shape,
        scratch_shapes=[pltpu.VMEM((tm, t), bf16)],
        compiler_params=_cparams(2))(a, b)


ROWS_A = 16
ROWS_B = 32
UNROLL = 4
UNROLL_NORM = 8


def _past(win, s):
    return (win if s == 0 else pltpu.roll(win, s, 0))[HALO:]


def _future(win, s):
    n = win.shape[0]
    return (win if s == 0 else pltpu.roll(win, n - s, 0))[:n - HALO]


def _fold8(v):
    return v.reshape(v.shape[0] // 8, 8, v.shape[1]).sum(axis=0)


def _last8(ref):
    return ref[...].astype(f32)[HBLK - HALO:]


def _first8(ref):
    return ref[...].astype(f32)[:HALO]


def _rd(ref, rows):
    return ref[rows, :].astype(f32)


def _halo_prev(tb, col):
    return lambda i: (jnp.maximum(i * (tb // HBLK) - 1, 0), col)


def _halo_next(tb, col, t):
    return lambda i: (jnp.minimum((i + 1) * (tb // HBLK), t // HBLK - 1), col)


def group_a_fwd(proj, wa, g, seq, tb, name):
    t = proj.shape[0]
    bps = seq // tb

    def body(xa_ref, ca_ref, ba_ref, xah_ref, cah_ref, wa_ref, g_ref, o_ref, u_scr):
        first = (pl.program_id(0) % bps) == 0
        u_scr[0:HALO, :] = jnp.where(first, 0.0, _last8(cah_ref) * _last8(xah_ref))
        w, gv = wa_ref[...], g_ref[...]

        def chunk(i, carry):
            r = pl.multiple_of(i * ROWS_A, ROWS_A)
            rows = pl.ds(r, ROWS_A)
            u_scr[pl.ds(pl.multiple_of(HALO + r, HALO), ROWS_A), :] = _rd(ca_ref, rows) * _rd(xa_ref, rows)
            win = u_scr[pl.ds(r, ROWS_A + HALO), :]
            cv = w[2:3] * _past(win, 0) + w[1:2] * _past(win, 1) + w[0:1] * _past(win, 2)
            o_ref[rows, :] = _rms_fwd(_rd(ba_ref, rows) * cv, gv).astype(bf16)
            return carry

        lax.fori_loop(0, tb // ROWS_A, chunk, 0, unroll=UNROLL_NORM)

    blk = lambda c: pl.BlockSpec((tb, D), lambda i: (i, c))
    return pl.pallas_call(
        body, name=name, grid=(t // tb,),
        in_specs=[blk(0), blk(1), blk(2),
                  pl.BlockSpec((HBLK, D), _halo_prev(tb, 0)), pl.BlockSpec((HBLK, D), _halo_prev(tb, 1)),
                  pl.BlockSpec((8, D), lambda i: (0, 0)), pl.BlockSpec((1, D), lambda i: (0, 0))],
        out_specs=pl.BlockSpec((tb, D), lambda i: (i, 0)),
        out_shape=_sds((t, 2 * D), bf16),
        scratch_shapes=[pltpu.VMEM((tb + HALO, D), f32)],
        compiler_params=_cparams(1))(proj, proj, proj, proj, proj, wa, g)


def group_a_bwd(proj, dcat, wa, g, seq, tb, name, token=None):
    t = proj.shape[0]
    bps = seq // tb

    def body(xa_ref, ca_ref, ba_ref, dy_ref, xap_ref, cap_ref, xan_ref, can_ref, ban_ref, dyn_ref, wa_ref, g_ref,
             *rest):
        dp_ref, dwa_ref, dg_ref, u_scr, d_scr, acc_scr = rest[-6:]
        i = pl.program_id(0)
        first = (i % bps) == 0
        last = (i % bps) == bps - 1
        w = wa_ref[...]
        gv = g_ref[...]
        u_scr[0:HALO, :] = jnp.where(first, 0.0, _last8(cap_ref) * _last8(xap_ref))
        u_scr[HALO + tb:2 * HALO + tb, :] = _first8(can_ref) * _first8(xan_ref)
        acc_scr[...] = jnp.zeros_like(acc_scr)

        def forward_part(n, carry):
            r = pl.multiple_of(n * ROWS_A, ROWS_A)
            rows = pl.ds(r, ROWS_A)
            ba = _rd(ba_ref, rows)
            u_scr[pl.ds(pl.multiple_of(HALO + r, HALO), ROWS_A), :] = _rd(ca_ref, rows) * _rd(xa_ref, rows)
            win = u_scr[pl.ds(r, ROWS_A + HALO), :]
            u = [_past(win, s) for s in range(3)]
            cv = w[2:3] * u[0] + w[1:2] * u[1] + w[0:1] * u[2]
            dya, dgc = _rms_bwd(ba * cv, gv, _rd(dy_ref, rows))
            dcv = dya * ba
            d_scr[rows, :] = dcv
            dp_ref[rows, 2 * D:3 * D] = (dya * cv).astype(bf16)
            acc_scr[0:8, :] += _fold8(dgc)
            for k in range(3):
                acc_scr[8 + 8 * k:16 + 8 * k, :] += _fold8(dcv * u[2 - k])
            return carry

        lax.fori_loop(0, tb // ROWS_A, forward_part, 0, unroll=UNROLL_NORM)

        start = HALO + tb
        cvn = (w[2:3] * u_scr[pl.ds(start, HALO), :] + w[1:2] * u_scr[pl.ds(start - 1, HALO), :]
               + w[0:1] * u_scr[pl.ds(start - 2, HALO), :])
        ban = _first8(ban_ref)
        dyan, _ = _rms_bwd(ban * cvn, gv, _first8(dyn_ref))
        d_scr[tb:tb + HALO, :] = jnp.where(last, 0.0, dyan * ban)

        def backward_part(n, carry):
            r = pl.multiple_of(n * ROWS_A, ROWS_A)
            rows = pl.ds(r, ROWS_A)
            win = d_scr[pl.ds(r, ROWS_A + HALO), :]
            du = w[2:3] * _future(win, 0) + w[1:2] * _future(win, 1) + w[0:1] * _future(win, 2)
            dp_ref[rows, 0:D] = (du * _rd(ca_ref, rows)).astype(bf16)
            dp_ref[rows, D:2 * D] = (du * _rd(xa_ref, rows)).astype(bf16)
            return carry

        lax.fori_loop(0, tb // ROWS_A, backward_part, 0, unroll=UNROLL)

        row = lax.broadcasted_iota(jnp.int32, (8, D), 0)
        dw = jnp.zeros((8, D), f32)
        for k in range(3):
            dw = jnp.where(row == k, jnp.sum(acc_scr[8 + 8 * k:16 + 8 * k, :], axis=0, keepdims=True), dw)
        _accum(dwa_ref, dw, i == 0)
        _accum(dg_ref, jnp.sum(acc_scr[0:8, :], axis=0, keepdims=True), i == 0)

    blk = lambda c: pl.BlockSpec((tb, D), lambda i: (i, c))
    prv = lambda c: pl.BlockSpec((HBLK, D), _halo_prev(tb, c))
    nxt = lambda c: pl.BlockSpec((HBLK, D), _halo_next(tb, c, t))
    return pl.pallas_call(
        body, name=name, grid=(t // tb,),
        in_specs=[blk(0), blk(1), blk(2), blk(0), prv(0), prv(1), nxt(0), nxt(1), nxt(2), nxt(0),
                  pl.BlockSpec((8, D), lambda i: (0, 0)), pl.BlockSpec((1, D), lambda i: (0, 0))] + _token_spec(token),
        out_specs=[pl.BlockSpec((tb, 3 * D), lambda i: (i, 0)), pl.BlockSpec((8, D), lambda i: (0, 0)),
                   pl.BlockSpec((1, D), lambda i: (0, 0))],
        out_shape=[_sds((t, PROJ), bf16), _sds((8, D), f32), _sds((1, D), f32)],
        scratch_shapes=[pltpu.VMEM((tb + 2 * HALO, D), f32), pltpu.VMEM((tb + HALO, D), f32), pltpu.VMEM((32, D), f32)],
        compiler_params=_cparams(1))(proj, proj, proj, dcat, proj, proj, proj, proj, proj, dcat, wa, g,
                                     *_token_arg(token))


CB = 512
XBC_BLK0 = COL_XBC // CB


def conv_b_fwd(proj, ws, bs, seq, tb, name):
    t = proj.shape[0]
    bps = seq // tb

    def body(x_ref, xp_ref, w_ref, b_ref, o_ref, da_ref, x_scr):
        first = (pl.program_id(1) % bps) == 0
        x_scr[0:HALO, :] = jnp.where(first, 0.0, _last8(xp_ref))
        w, bias = w_ref[...], b_ref[...]

        def chunk(n, carry):
            r = pl.multiple_of(n * ROWS_B, ROWS_B)
            rows = pl.ds(r, ROWS_B)
            x_scr[pl.ds(pl.multiple_of(HALO + r, HALO), ROWS_B), :] = _rd(x_ref, rows)
            win = x_scr[pl.ds(r, ROWS_B + HALO), :]
            xc = bias + w[3:4] * _past(win, 0)
            for k in range(3):
                xc = xc + w[k:k + 1] * _past(win, 3 - k)
            sg = _sigmoid(xc)
            o_ref[rows, :] = xc * sg
            da_ref[rows, :] = (sg * (1.0 + xc * (1.0 - sg))).astype(bf16)
            return carry

        lax.fori_loop(0, tb // ROWS_B, chunk, 0, unroll=UNROLL)

    return pl.pallas_call(
        body, name=name, grid=(XBC // CB, t // tb),
        in_specs=[pl.BlockSpec((tb, CB), lambda j, i: (i, XBC_BLK0 + j)),
                  pl.BlockSpec((HBLK, CB), lambda j, i: (jnp.maximum(i * (tb // HBLK) - 1, 0), XBC_BLK0 + j)),
                  pl.BlockSpec((8, CB), lambda j, i: (0, j)), pl.BlockSpec((1, CB), lambda j, i: (0, j))],
        out_specs=[pl.BlockSpec((tb, CB), lambda j, i: (i, j)), pl.BlockSpec((tb, CB), lambda j, i: (i, j))],
        out_shape=[_sds((t, XBC), f32), _sds((t, XBC), bf16)],
        scratch_shapes=[pltpu.VMEM((tb + HALO, CB), f32)],
        compiler_params=_cparams(2))(proj, proj, ws, bs)


def conv_b_bwd(proj, dxs, dact, ws, dproj, seq, tb, name):
    t = proj.shape[0]
    bps = seq // tb

    def body(x_ref, xp_ref, d_ref, dn_ref, a_ref, an_ref, w_ref, dproj_ref, dx_ref, dw_ref, db_ref, x_scr, d_scr,
             acc_scr):
        i = pl.program_id(1)
        first = (i % bps) == 0
        last = (i % bps) == bps - 1
        w = w_ref[...]
        x_scr[0:HALO, :] = jnp.where(first, 0.0, _last8(xp_ref))
        acc_scr[...] = jnp.zeros_like(acc_scr)

        def forward_part(n, carry):
            r = pl.multiple_of(n * ROWS_B, ROWS_B)
            rows = pl.ds(r, ROWS_B)
            x_scr[pl.ds(pl.multiple_of(HALO + r, HALO), ROWS_B), :] = _rd(x_ref, rows)
            win = x_scr[pl.ds(r, ROWS_B + HALO), :]
            dxc = _rd(d_ref, rows) * _rd(a_ref, rows)
            d_scr[rows, :] = dxc
            acc_scr[0:8, :] += _fold8(dxc)
            for k in range(4):
                acc_scr[8 + 8 * k:16 + 8 * k, :] += _fold8(dxc * _past(win, 3 - k))
            return carry

        lax.fori_loop(0, tb // ROWS_B, forward_part, 0, unroll=UNROLL)
        d_scr[tb:tb + HALO, :] = jnp.where(last, 0.0, _first8(dn_ref) * _first8(an_ref))

        def backward_part(n, carry):
            r = pl.multiple_of(n * ROWS_B, ROWS_B)
            win = d_scr[pl.ds(r, ROWS_B + HALO), :]
            dx = w[3:4] * _future(win, 0)
            for k in range(3):
                dx = dx + w[k:k + 1] * _future(win, 3 - k)
            dx_ref[pl.ds(r, ROWS_B), :] = dx.astype(bf16)
            return carry

        lax.fori_loop(0, tb // ROWS_B, backward_part, 0, unroll=UNROLL)

        row = lax.broadcasted_iota(jnp.int32, (8, CB), 0)
        dw = jnp.zeros((8, CB), f32)
        for k in range(4):
            dw = jnp.where(row == k, jnp.sum(acc_scr[8 + 8 * k:16 + 8 * k, :], axis=0, keepdims=True), dw)
        _accum(dw_ref, dw, i == 0)
        _accum(db_ref, jnp.sum(acc_scr[0:8, :], axis=0, keepdims=True), i == 0)

    nh = t // HBLK
    nxt = pl.BlockSpec((HBLK, CB), lambda j, i: (jnp.minimum((i + 1) * (tb // HBLK), nh - 1), j))
    cur = pl.BlockSpec((tb, CB), lambda j, i: (i, j))
    return pl.pallas_call(
        body, name=name, grid=(XBC // CB, t // tb),
        in_specs=[pl.BlockSpec((tb, CB), lambda j, i: (i, XBC_BLK0 + j)),
                  pl.BlockSpec((HBLK, CB), lambda j, i: (jnp.maximum(i * (tb // HBLK) - 1, 0), XBC_BLK0 + j)),
                  cur, nxt, cur, nxt, pl.BlockSpec((8, CB), lambda j, i: (0, j)), pl.BlockSpec(memory_space=pl.ANY)],
        out_specs=[pl.BlockSpec((tb, CB), lambda j, i: (i, XBC_BLK0 + j)), pl.BlockSpec((8, CB), lambda j, i: (0, j)),
                   pl.BlockSpec((1, CB), lambda j, i: (0, j))],
        out_shape=[_sds((t, PROJ), bf16), _sds((8, XBC), f32), _sds((1, XBC), f32)],
        input_output_aliases={7: 0},
        scratch_shapes=[pltpu.VMEM((tb + HALO, CB), f32), pltpu.VMEM((tb + HALO, CB), f32), pltpu.VMEM((40, CB), f32)],
        compiler_params=_cparams(2))(proj, proj, dxs, dxs, dact, dact, ws, dproj)


def place_columns(buf, part, col_block, tb, name):
    t, wdt = part.shape

    def body(p_ref, buf_ref, o_ref):
        o_ref[...] = p_ref[...]

    return pl.pallas_call(
        body, name=name, grid=(t // tb,),
        in_specs=[pl.BlockSpec((tb, wdt), lambda i: (i, 0)), pl.BlockSpec(memory_space=pl.ANY)],
        out_specs=pl.BlockSpec((tb, wdt), lambda i: (i, col_block)), out_shape=_sds(buf.shape, buf.dtype),
        input_output_aliases={1: 0}, compiler_params=_cparams(1))(part, buf)


GW = D // NG
EXPAND_TERMS = 1
REDUCE_TERMS = 1


def _ssd_consts():
    head_of_lane = jnp.arange(D) // HP
    expand = (jnp.arange(CH)[:, None] == head_of_lane[None, :]).astype(bf16)
    tri = (jnp.arange(CH)[:, None] >= jnp.arange(CH)[None, :]).astype(f32)
    return expand, tri


def _ssd_common(par_ref, dtr_ref, e_ref, tri_ref):
    par = par_ref[...]
    dtb, alog, dsk = par[0:1], par[1:2], par[2:3]
    lane = lax.broadcasted_iota(jnp.int32, (CH, CH), 1)
    a = -jnp.exp(alog)
    dtr = dtr_ref[...].astype(f32) + dtb
    sp = jnp.maximum(dtr, 0.0) + jnp.log(1.0 + jnp.exp(-jnp.abs(dtr)))
    dt = jnp.where(lane < NH, sp, 0.0)
    cs = jnp.dot(tri_ref[...], dt * a, precision=lax.Precision.HIGHEST, preferred_element_type=f32)
    cs_last = cs[CH - 1:CH, :]
    dte = jnp.exp(cs_last - cs)
    ecs = jnp.exp(cs)
    ecl = jnp.exp(cs_last)
    e = e_ref[...]
    row8 = lax.broadcasted_iota(jnp.int32, (8, CH), 0)
    r8 = _split_dot(jnp.where(row8 == 0, ecl, jnp.where(row8 == 1, dsk, 0.0)), e, 3)
    return dict(a=a, dtr=dtr, dt=dt, cs=cs, cst=cs.T, dte=dte, ecs=ecs, ecl=ecl, e=e, lane=lane,
                dt_x=_split_dot(dt, e, EXPAND_TERMS), dte_x=_split_dot(dte, e, EXPAND_TERMS),
                ecs_x=_split_dot(ecs, e, EXPAND_TERMS),
                ecl_x=r8[0:1], dsk_x=r8[1:2])


def _decay_matrix(c, h):
    li = lax.broadcasted_iota(jnp.int32, (CH, CH), 0)
    seg = c["cs"][:, h:h + 1] - c["cst"][h:h + 1, :]
    return jnp.exp(jnp.where(li >= c["lane"], seg, -jnp.inf))


def _gate_norm_fwd(y, z, gs):
    zg = z * _sigmoid(z)
    yg = y * zg
    return jnp.concatenate([_rms_fwd(yg[:, k * GW:(k + 1) * GW], gs[:, k * GW:(k + 1) * GW]) for k in range(NG)], axis=1)


def ssd_fwd(xbcs, proj, par, gs, cat, seq, name):
    t = xbcs.shape[0]
    nc = seq // CH
    expand, tri = _ssd_consts()

    def body(xs_ref, b_ref, c_ref, dtr_ref, z_ref, par_ref, e_ref, tri_ref, gs_ref, cat_ref, yn_ref, y_ref, st_ref,
             p_scr, yd_scr):
        @pl.when(pl.program_id(0) % nc == 0)
        def _():
            p_scr[...] = jnp.zeros_like(p_scr)

        c = _ssd_common(par_ref, dtr_ref, e_ref, tri_ref)
        xs = xs_ref[...]
        xdt = xs * c["dt_x"]
        xdt_b = xdt.astype(bf16)
        xdte_b = (xdt * c["dte_x"]).astype(bf16)
        p = p_scr[...]
        st_ref[0] = p
        p_b = p.astype(bf16)
        lo = c["lane"] < HP
        for g in range(NG):
            bg = b_ref[:, g * NS:(g + 1) * NS].astype(bf16)
            cg = c_ref[:, g * NS:(g + 1) * NS].astype(bf16)
            gmat = _dot_nt(cg, bg)
            for q in range(GW // CH):
                col = g * GW + q * CH
                xp = xdt_b[:, col:col + CH]
                h0 = col // HP
                m0 = (gmat * _decay_matrix(c, h0)).astype(bf16)
                m1 = (gmat * _decay_matrix(c, h0 + 1)).astype(bf16)
                stacked = jnp.concatenate([jnp.where(lo, xp, jnp.zeros_like(xp)),
                                           jnp.where(lo, jnp.zeros_like(xp), xp)], axis=0)
                yd_scr[:, col:col + CH] = _dot(jnp.concatenate([m0, m1], axis=1), stacked)
            gsl = slice(g * GW, (g + 1) * GW)
            yoff = _dot(cg, p_b[:, gsl]) * c["ecs_x"][:, gsl]
            yd_scr[:, gsl] = yd_scr[:, gsl] + yoff
            p_scr[:, gsl] = p[:, gsl] * c["ecl_x"][:, gsl] + _dot_tn(bg, xdte_b[:, gsl])
        y = yd_scr[...] + c["dsk_x"] * xs
        y_ref[...] = y
        yn_ref[...] = _gate_norm_fwd(y, z_ref[...].astype(f32), gs_ref[...]).astype(bf16)

    nb = t // CH
    return pl.pallas_call(
        body, name=name, grid=(nb,),
        in_specs=[pl.BlockSpec((CH, D), lambda i: (i, 0)),
                  pl.BlockSpec((CH, NG * NS), lambda i: (i, D // (NG * NS))),
                  pl.BlockSpec((CH, NG * NS), lambda i: (i, D // (NG * NS) + 1)),
                  pl.BlockSpec((CH, CH), lambda i: (i, COL_DT // CH)),
                  pl.BlockSpec((CH, D), lambda i: (i, COL_Z // D)),
                  pl.BlockSpec((8, CH), lambda i: (0, 0)), pl.BlockSpec((CH, D), lambda i: (0, 0)),
                  pl.BlockSpec((CH, CH), lambda i: (0, 0)), pl.BlockSpec((1, D), lambda i: (0, 0)),
                  pl.BlockSpec(memory_space=pl.ANY)],
        out_specs=[pl.BlockSpec((CH, D), lambda i: (i, 1)), pl.BlockSpec((CH, D), lambda i: (i, 0)),
                   pl.BlockSpec((1, NS, D), lambda i: (i, 0, 0))],
        out_shape=[_sds((t, 2 * D), bf16), _sds((t, D), f32), _sds((nb, NS, D), f32)],
        input_output_aliases={9: 0},
        scratch_shapes=[pltpu.VMEM((NS, D), f32), pltpu.VMEM((CH, D), f32)],
        compiler_params=_cparams(1))(xbcs, xbcs, xbcs, proj, proj, par, expand, tri, gs, cat)


def ssd_bwd(xbcs, proj, ypre, states, dcat, par, gs, dproj, seq, name):
    t = xbcs.shape[0]
    nc = seq // CH
    expand, tri = _ssd_consts()

    def body(xs_ref, b_ref, c_ref, dtr_ref, z_ref, y_ref, st_ref, dyn_ref, par_ref, e_ref, tri_ref, gs_ref, dproj_ref,
             dx_ref, dz_ref, ddt_ref, dpar_ref, dgs_ref, dp_scr, dxdt_scr):
        i = pl.program_id(0)

        @pl.when(i % nc == 0)
        def _():
            dp_scr[...] = jnp.zeros_like(dp_scr)

        c = _ssd_common(par_ref, dtr_ref, e_ref, tri_ref)
        e = c["e"]
        lane = c["lane"]
        sub = lax.broadcasted_iota(jnp.int32, (CH, CH), 0)
        xs = xs_ref[...]
        xdt = xs * c["dt_x"]
        xdt_b = xdt.astype(bf16)
        xdte_b = (xdt * c["dte_x"]).astype(bf16)
        p = st_ref[0]
        p_b = p.astype(bf16)
        dpn = dp_scr[...]
        dpn_b = dpn.astype(bf16)

        y, z, gs_v = y_ref[...], z_ref[...].astype(f32), gs_ref[...]
        zs = _sigmoid(z)
        zg = z * zs
        yg = y * zg
        parts, gparts = [], []
        for k in range(NG):
            sl = slice(k * GW, (k + 1) * GW)
            dxk, dgk = _rms_bwd(yg[:, sl], gs_v[:, sl], dyn_ref[:, sl].astype(f32))
            parts.append(dxk)
            gparts.append(dgk)
        dyg = jnp.concatenate(parts, axis=1)
        dgs_rows = jnp.concatenate(gparts, axis=1)
        dy = dyg * zg
        dz_ref[...] = (dyg * y * (zs * (1.0 + z * (1.0 - zs)))).astype(bf16)
        dy_b = dy.astype(bf16)
        dq_b = (dy * c["ecs_x"]).astype(bf16)

        lo = lane < HP
        dcs = jnp.zeros((CH, CH), f32)
        dcst = jnp.zeros((CH, CH), f32)
        for g in range(NG):
            gsl = slice(g * GW, (g + 1) * GW)
            bg = b_ref[:, g * NS:(g + 1) * NS].astype(bf16)
            cg = c_ref[:, g * NS:(g + 1) * NS].astype(bf16)
            gmat = _dot_nt(cg, bg)
            dgm = jnp.zeros((CH, CH), f32)
            for q in range(GW // CH):
                col = g * GW + q * CH
                xp = xdt_b[:, col:col + CH]
                dyp = dy_b[:, col:col + CH]
                zero = jnp.zeros_like(dyp)
                xp2 = jnp.concatenate([jnp.where(lo, xp, zero), jnp.where(lo, zero, xp)], axis=0)
                dy2 = jnp.concatenate([jnp.where(lo, dyp, zero), jnp.where(lo, zero, dyp)], axis=0)
                dm2 = _dot_nt(dyp, xp2)
                ms = []
                for hh in range(2):
                    h = col // HP + hh
                    dec = _decay_matrix(c, h)
                    m = gmat * dec
                    dm = dm2[:, hh * CH:(hh + 1) * CH]
                    dseg = dm * m
                    dcs = dcs + jnp.where(lane == h, jnp.sum(dseg, axis=1, keepdims=True), 0.0)
                    dcst = dcst + jnp.where(sub == h, jnp.sum(dseg, axis=0, keepdims=True), 0.0)
                    dgm = dgm + dm * dec
                    ms.append(m.astype(bf16))
                dxdt_scr[:, col:col + CH] = _dot_tn(jnp.concatenate(ms, axis=0), dy2)
            dgm_b = dgm.astype(bf16)
            bds = _dot(bg, dpn_b[:, gsl])
            dxdt_scr[:, gsl] = dxdt_scr[:, gsl] + c["dte_x"][:, gsl] * bds
            dc_g = _dot(dgm_b, bg) + _dot_nt(dq_b[:, gsl], p_b[:, gsl])
            db_g = _dot_tn(dgm_b, cg) + _dot_nt(xdte_b[:, gsl], dpn_b[:, gsl])
            dx_ref[:, D + g * NS:D + (g + 1) * NS] = db_g
            dx_ref[:, D + NG * NS + g * NS:D + NG * NS + (g + 1) * NS] = dc_g
            dp_scr[:, gsl] = dpn[:, gsl] * c["ecl_x"][:, gsl] + _dot_tn(cg, dq_b[:, gsl])
            q_g = _dot(cg, p_b[:, gsl])
            e_g = e[:, gsl]
            dcs = dcs + c["ecs"] * _split_dot(dy[:, gsl] * q_g, e_g, REDUCE_TERMS, nt=True)
            ddte = _split_dot(xdt[:, gsl] * bds, e_g, REDUCE_TERMS, nt=True) * c["dte"]
            dcs = dcs - ddte
            dcs = dcs + jnp.where(sub == CH - 1, jnp.sum(ddte, axis=0, keepdims=True), 0.0)

        decl = _split_dot(jnp.broadcast_to(jnp.sum(dpn * p, axis=0, keepdims=True), (8, D)), e, 2, nt=True)[0:1]
        dcs = dcs + jnp.where(sub == CH - 1, c["ecl"] * decl, 0.0)
        dcs = dcs - dcst.T
        dadt = lax.dot_general(tri_ref[...], dcs, (((0,), (0,)), ((), ())), precision=lax.Precision.HIGHEST,
                               preferred_element_type=f32)
        dxdt = dxdt_scr[...]
        ddt = dadt * c["a"] + _split_dot(dxdt * xs, e, REDUCE_TERMS, nt=True)
        ddtr = jnp.where(lane < NH, ddt * _sigmoid(c["dtr"]), 0.0)
        ddt_ref[...] = ddtr.astype(bf16)
        dx_ref[:, 0:D] = dxdt * c["dt_x"] + c["dsk_x"] * dy
        dsk = _split_dot(jnp.broadcast_to(jnp.sum(dy * xs, axis=0, keepdims=True), (8, D)), e, 2, nt=True)[0:1]
        dalog = jnp.sum(dadt * c["dt"], axis=0, keepdims=True) * c["a"]
        row8 = lax.broadcasted_iota(jnp.int32, (8, CH), 0)
        dpar = jnp.where(row8 == 0, jnp.sum(ddtr, axis=0, keepdims=True),
                         jnp.where(row8 == 1, dalog, jnp.where(row8 == 2, dsk, 0.0)))
        dpar = jnp.where(lax.broadcasted_iota(jnp.int32, (8, CH), 1) < NH, dpar, 0.0)
        _accum(dpar_ref, dpar, i == 0)
        _accum(dgs_ref, jnp.sum(dgs_rows, axis=0, keepdims=True), i == 0)

    nb = t // CH
    rev = lambda i: (i // nc) * nc + (nc - 1 - i % nc)
    return pl.pallas_call(
        body, name=name, grid=(nb,),
        in_specs=[pl.BlockSpec((CH, D), lambda i: (rev(i), 0)),
                  pl.BlockSpec((CH, NG * NS), lambda i: (rev(i), D // (NG * NS))),
                  pl.BlockSpec((CH, NG * NS), lambda i: (rev(i), D // (NG * NS) + 1)),
                  pl.BlockSpec((CH, CH), lambda i: (rev(i), COL_DT // CH)),
                  pl.BlockSpec((CH, D), lambda i: (rev(i), COL_Z // D)),
                  pl.BlockSpec((CH, D), lambda i: (rev(i), 0)),
                  pl.BlockSpec((1, NS, D), lambda i: (rev(i), 0, 0)),
                  pl.BlockSpec((CH, D), lambda i: (rev(i), 1)),
                  pl.BlockSpec((8, CH), lambda i: (0, 0)), pl.BlockSpec((CH, D), lambda i: (0, 0)),
                  pl.BlockSpec((CH, CH), lambda i: (0, 0)), pl.BlockSpec((1, D), lambda i: (0, 0)),
                  pl.BlockSpec(memory_space=pl.ANY)],
        out_specs=[pl.BlockSpec((CH, XBC), lambda i: (rev(i), 0)), pl.BlockSpec((CH, D), lambda i: (rev(i), COL_Z // D)),
                   pl.BlockSpec((CH, CH), lambda i: (rev(i), 0)),
                   pl.BlockSpec((8, CH), lambda i: (0, 0)), pl.BlockSpec((1, D), lambda i: (0, 0))],
        out_shape=[_sds((t, XBC), f32), _sds((t, PROJ), bf16), _sds((t, CH), bf16), _sds((8, CH), f32), _sds((1, D), f32)],
        input_output_aliases={12: 1},
        scratch_shapes=[pltpu.VMEM((NS, D), f32), pltpu.VMEM((CH, D), f32)],
        compiler_params=_cparams(1))(xbcs, xbcs, xbcs, proj, proj, ypre, states, dcat, par, expand, tri, gs, dproj)


def loss_head(y, target, tb, name):
    t = y.shape[0]

    def body(y_ref, t_ref, s_ref, dy_ref):
        err = y_ref[...] - t_ref[...]
        dy_ref[...] = err * (1.0 / D)
        _accum(s_ref, jnp.zeros((8, CH), f32) + jnp.sum(err * err), pl.program_id(0) == 0)

    return pl.pallas_call(
        body, name=name, grid=(t // tb,),
        in_specs=[pl.BlockSpec((tb, D), lambda i: (i, 0)), pl.BlockSpec((tb, D), lambda i: (i, 0))],
        out_specs=[pl.BlockSpec((8, CH), lambda i: (0, 0)), pl.BlockSpec((tb, D), lambda i: (i, 0))],
        out_shape=[_sds((8, CH), f32), _sds((t, D), f32)],
        compiler_params=_cparams(1))(y, target)


def _tiles(t, seq):
    tm = min(512, t)
    return dict(tm=tm, tm_small=min(256, t), tm_large=min(1024, t), tm_huge=min(2048, t), tb=min(512, seq))


def local_step(x, target, depth, weights_of, seq, grads_done=None):
    t = x.shape[0]
    ts = _tiles(t, seq)
    tm, tl, th, tb = ts["tm"], ts["tm_large"], ts["tm_huge"], ts["tb"]
    saved, ws = [], []
    for l in range(depth):
        w = weights_of(l, x)
        ws.append(w)
        proj, h1 = norm_matmul(x, w["g1"], w["win"], th, 1920, bf16, "in_proj", token=w.get("token"))
        cat = group_a_fwd(proj, w["wa"], w["ga"], seq, tb, "group_a_fwd")
        xbcs, dact = conv_b_fwd(proj, w["ws"], w["bs"], seq, tb, "conv_b_fwd")
        cat, ypre, states = ssd_fwd(xbcs, proj, w["par"], w["gs"], cat, seq, "ssd_fwd")
        if "late" in w:
            w.update(w.pop("late")(cat))
        mix, x2 = matmul_postnorm(cat, w["wo"], x, w["g2"], tl, "out_proj")
        fp, h2, o, x3 = mlp_fwd(x2, w["g3"], w["wu"], w["wd"], w["g4"], tm, "mlp_fwd")
        saved.append(dict(x=x, proj=proj, h1=h1, xbcs=xbcs, dact=dact, ypre=ypre, states=states, cat=cat, mix=mix, x2=x2,
                          fp=fp, h2=h2, o=o))
        x = x3
    sse, dx = loss_head(x, target, tm, "loss_head")
    grads = [None] * depth
    for l in reversed(range(depth)):
        s, w = saved[l], ws[l]
        do, dfp, dx2, dg4, dg3 = mlp_bwd(s["o"], w["g4"], dx, w["wd"], s["fp"], w["wu"], s["x2"], w["g3"],
                                         tm, "mlp_bwd")
        dwd = matmul_tn(s["fp"], do, 512, 1024, True, "mlp_down_dw")
        dwu = matmul_tn(s["h2"], dfp, tl, 1024, False, "mlp_up_dw", col_blocks=True)
        dmix, dg2, dcat = postnorm_bwd_matmul(s["mix"], w["g2"], dx2, w["wo"], tl, 2 * D, "out_proj_bwd")
        dwo = matmul_tn(s["cat"], dmix, 512, 1024, False, "out_proj_dw")
        token = None if grads_done is None else grads_done(l, dict(wo=dwo, wu=dwu, wd=dwd), False)
        dproj, dwa, dga = group_a_bwd(s["proj"], dcat, w["wa"], w["ga"], seq, tb, "group_a_bwd", token=token)
        dxbcs, dproj, ddt, dpar, dgs = ssd_bwd(s["xbcs"], s["proj"], s["ypre"], s["states"], dcat, w["par"], w["gs"],
                                               dproj, seq, "ssd_bwd")
        dproj, dws, dbs = conv_b_bwd(s["proj"], dxbcs, s["dact"], w["ws"], dproj, seq, tb, "conv_b_bwd")
        dproj = place_columns(dproj, ddt, COL_DT // CH, tm, "place_ddt")
        dwin = matmul_tn(s["h1"], dproj, tl, 1152, False, "in_proj_dw")
        token = None if grads_done is None else grads_done(l, dict(win=dwin), True)
        dx, dg1 = matmul_prenorm_bwd(dproj, w["win"], s["x"], w["g1"], dx2, tm, "in_proj_bwd", token=token)
        grads[l] = dict(win=dwin, wo=dwo, wu=dwu, wd=dwd, wa=dwa, ws=dws, bs=dbs, par=dpar,
                        g1=dg1, ga=dga, gs=dgs, g2=dg2, g3=dg3, g4=dg4)
    return sse, dx, grads


GROUPS = {
    "chips": [(1, 0, 0), (0, 1, 0), (1, 1, 0)],
    "pair": [(0, 0, 1)],
    "all": [(1, 0, 0), (0, 1, 0), (1, 1, 0), (0, 0, 1), (1, 0, 1), (0, 1, 1), (1, 1, 1)],
}


def _group_index(group, x, y, c):
    return {"chips": 2 * x + y, "pair": c, "all": 4 * x + 2 * y + c}[group]


def _chunk_indices(shape, pieces):
    if len(shape) < 3:
        return [()]
    lead = [()]
    for n in shape[:-2]:
        lead = [i + (k,) for i in lead for k in range(n)]
    rows = shape[-2]
    split = max(1, pieces // len(lead))
    while split > 1 and (rows % split or (rows // split) % 16):
        split -= 1
    step = rows // split
    return [i + (pl.ds(s * step, step),) for i in lead for s in range(split)]


def _exchange(arrays, out_shapes, group, src_view, dst_view, view_shape, name, own, pieces=16):
    masks = GROUPS[group]
    na, nm = len(arrays), len(masks)
    cuts = [_chunk_indices(view_shape(a), pieces) for a in range(na)]

    def body(*refs):
        ins, outs = refs[:na], refs[na:2 * na]
        send_sems, recv_sems = refs[2 * na:2 * na + 2]
        local_sems = refs[2 * na + 2] if own else None
        x, y, c = lax.axis_index("x"), lax.axis_index("y"), lax.axis_index("c")
        me = _group_index(group, x, y, c)
        peers = []
        for mx, my, mc in masks:
            px, py, pc = (1 - x if mx else x), (1 - y if my else y), (1 - c if mc else c)
            peers.append(((px, py, pc), _group_index(group, px, py, pc)))

        def part(ref, idx):
            return ref.at[idx] if idx else ref

        if own:
            for a in range(na):
                for idx in cuts[a]:
                    pltpu.make_async_copy(part(src_view(ins[a], a, me), idx), part(dst_view(outs[a], a, me), idx),
                                          local_sems.at[a]).start()
        for a in range(na):
            for j, (dev, pidx) in enumerate(peers):
                for idx in cuts[a]:
                    pltpu.make_async_remote_copy(
                        src_ref=part(src_view(ins[a], a, pidx), idx), dst_ref=part(dst_view(outs[a], a, me), idx),
                        send_sem=send_sems.at[a * nm + j], recv_sem=recv_sems.at[a * nm + j],
                        device_id=dev, device_id_type=MESH).start()
        whole = []
        for a in range(na):
            for j, (dev, pidx) in enumerate(peers):
                whole.append(pltpu.make_async_remote_copy(
                    src_ref=src_view(ins[a], a, pidx), dst_ref=dst_view(outs[a], a, pidx),
                    send_sem=send_sems.at[a * nm + j], recv_sem=recv_sems.at[a * nm + j],
                    device_id=dev, device_id_type=MESH))
        for cp in whole:
            cp.wait_recv()
        for cp in whole:
            cp.wait_send()
        if own:
            for a in range(na):
                pltpu.make_async_copy(src_view(ins[a], a, me), dst_view(outs[a], a, me), local_sems.at[a]).wait()

    hbm = pl.BlockSpec(memory_space=pltpu.HBM)
    sems = [pltpu.SemaphoreType.DMA((na * nm,)), pltpu.SemaphoreType.DMA((na * nm,))]
    return pl.pallas_call(
        body, name=name, in_specs=[hbm] * na, out_specs=[hbm] * na,
        out_shape=[_sds(s, a.dtype) for s, a in zip(out_shapes, arrays)],
        scratch_shapes=sems + ([pltpu.SemaphoreType.DMA((na,))] if own else []))(*arrays)


def all_gather(arrays, group, name, slot_axis=0, own=True):
    n = len(GROUPS[group]) + 1
    shapes = [a.shape[:slot_axis] + (n,) + a.shape[slot_axis:] for a in arrays]
    lead = (slice(None),) * slot_axis
    return _exchange(arrays, shapes, group, lambda r, a, i: r, lambda r, a, i: r.at[lead + (i,)],
                     lambda a: arrays[a].shape, name, own)


HBM_SPEC = pl.BlockSpec(memory_space=pltpu.HBM)
SEM_SPEC = pl.BlockSpec(memory_space=pltpu.SEMAPHORE)
DATAFLOW = pltpu.SideEffectType.DATAFLOW_SIDE_EFFECTING
N_CHIPS = 4


def _peers(group, x, y, c):
    out = []
    for mx, my, mc in GROUPS[group]:
        px, py, pc = (1 - x if mx else x), (1 - y if my else y), (1 - c if mc else c)
        out.append(((px, py, pc), _group_index(group, px, py, pc)))
    return out


def _whole_views(sources):
    return dict(src=lambda ref, a, c, to: ref, dst=lambda ref, a, c, sender: ref.at[sender],
                rows=lambda a: sources[a].shape[0])


def _weight_views(shards):
    half = [s.shape[0] // 2 for s in shards]
    return dict(src=lambda ref, a, c, to_chip: ref.at[pl.ds(c * half[a], half[a])],
                dst=lambda ref, a, c, from_chip: ref.at[from_chip, pl.ds(c * half[a], half[a])],
                rows=lambda a: half[a])


def _grad_views(sums):
    return dict(src=lambda ref, a, c, to_chip: ref.at[to_chip], dst=lambda ref, a, c, from_chip: ref.at[from_chip],
                rows=lambda a: sums[a].shape[1])


def chips_start(sources, zones, views, name, pieces=4, after=None, group="chips"):
    na, nm = len(sources), len(GROUPS[group])

    def body(*refs):
        ins, lands = refs[:na], refs[na:2 * na]
        n_in = 2 * na + len(_token_arg(after))
        send_sems, recv_sems, token = refs[n_in], refs[n_in + 1], refs[-1]
        x, y, c = lax.axis_index("x"), lax.axis_index("y"), lax.axis_index("c")
        me = _group_index(group, x, y, c)
        for a in range(na):
            step = views["rows"](a) // pieces
            for j, (dev, to) in enumerate(_peers(group, x, y, c)):
                for q in range(pieces):
                    rows = pl.ds(q * step, step)
                    pltpu.make_async_remote_copy(
                        src_ref=views["src"](ins[a], a, c, to).at[rows],
                        dst_ref=views["dst"](lands[a], a, c, me).at[rows],
                        send_sem=send_sems.at[a * nm + j], recv_sem=recv_sems.at[a * nm + j],
                        device_id=dev, device_id_type=MESH).start()
        token[...] = jnp.zeros_like(token)

    both = list(sources) + list(zones)
    outs = pl.pallas_call(
        body, name=name,
        out_shape=(pltpu.SemaphoreType.DMA((na * nm,)), pltpu.SemaphoreType.DMA((na * nm,)),
                   *[pltpu.HBM(b.shape, b.dtype) for b in both], _sds((8, CH), f32)),
        in_specs=[HBM_SPEC] * (2 * na) + _token_spec(after),
        out_specs=(SEM_SPEC, SEM_SPEC, *[HBM_SPEC] * (2 * na), pl.BlockSpec(memory_space=pltpu.VMEM)),
        input_output_aliases={i: 2 + i for i in range(2 * na)},
        compiler_params=pltpu.CompilerParams(has_side_effects=DATAFLOW))(
            *[pltpu.with_memory_space_constraint(b, pltpu.HBM) for b in both], *_token_arg(after))
    return dict(send=outs[0], recv=outs[1], sources=list(outs[2:2 + na]), zones=list(outs[2 + na:2 + 2 * na]),
                token=outs[-1], views=views, group=group)


def chips_wait(started, after, name):
    sources, zones, views, group = started["sources"], started["zones"], started["views"], started["group"]
    na, nm = len(sources), len(GROUPS[group])

    def body(*refs):
        ins, lands = refs[:na], refs[na:2 * na]
        send_sems, recv_sems = refs[2 * na], refs[2 * na + 1]
        x, y, c = lax.axis_index("x"), lax.axis_index("y"), lax.axis_index("c")
        for a in range(na):
            for j, (dev, peer) in enumerate(_peers(group, x, y, c)):
                cp = pltpu.make_async_remote_copy(
                    src_ref=views["src"](ins[a], a, c, peer), dst_ref=views["dst"](lands[a], a, c, peer),
                    send_sem=send_sems.at[a * nm + j], recv_sem=recv_sems.at[a * nm + j],
                    device_id=dev, device_id_type=MESH)
                cp.wait_send()
                cp.wait_recv()

    both = list(sources) + list(zones)
    outs = pl.pallas_call(
        body, name=name, out_shape=tuple(pltpu.HBM(b.shape, b.dtype) for b in both),
        in_specs=[HBM_SPEC] * (2 * na) + [SEM_SPEC, SEM_SPEC, pl.BlockSpec(memory_space=pl.ANY)],
        out_specs=tuple([HBM_SPEC] * (2 * na)), input_output_aliases={i: i for i in range(2 * na)},
        compiler_params=pltpu.CompilerParams(has_side_effects=DATAFLOW))(*both, started["send"], started["recv"], after)
    return list(outs[:na]), list(outs[na:])


def weights_share(zones, name):
    na, nm = len(zones), N_CHIPS - 1

    def body(*refs):
        lands = refs[na:2 * na]
        send_sems, recv_sems = refs[2 * na:]
        x, y, c = lax.axis_index("x"), lax.axis_index("y"), lax.axis_index("c")
        chip = 2 * x + y
        sibling = (x, y, 1 - c)
        sends = []
        for a in range(na):
            half = zones[a].shape[1] // 2
            for m in range(1, N_CHIPS):
                mine = lands[a].at[chip ^ m, pl.ds(c * half, half)]
                sends.append(pltpu.make_async_remote_copy(
                    src_ref=mine, dst_ref=mine, send_sem=send_sems.at[a * nm + m - 1],
                    recv_sem=recv_sems.at[a * nm + m - 1], device_id=sibling, device_id_type=MESH))
        for cp in sends:
            cp.start()
        for a in range(na):
            half = zones[a].shape[1] // 2
            for m in range(1, N_CHIPS):
                theirs = lands[a].at[chip ^ m, pl.ds((1 - c) * half, half)]
                pltpu.make_async_remote_copy(
                    src_ref=theirs, dst_ref=theirs, send_sem=send_sems.at[a * nm + m - 1],
                    recv_sem=recv_sems.at[a * nm + m - 1], device_id=sibling, device_id_type=MESH).wait_recv()
        for cp in sends:
            cp.wait_send()

    return pl.pallas_call(
        body, name=name, in_specs=[HBM_SPEC] * na, out_specs=[HBM_SPEC] * na,
        out_shape=[_sds(z.shape, z.dtype) for z in zones], input_output_aliases={i: i for i in range(na)},
        scratch_shapes=[pltpu.SemaphoreType.DMA((na * nm,)), pltpu.SemaphoreType.DMA((na * nm,))])(*zones)


def pair_send_halves(grads, name):
    half = [g.shape[1] // 2 for g in grads]
    shapes = [(g.shape[0], h, g.shape[2]) for g, h in zip(grads, half)]
    return _exchange(grads, shapes, "pair", lambda r, a, i: r.at[:, pl.ds(i * half[a], half[a])],
                     lambda r, a, i: r, lambda a: shapes[a], name, False)


def sum_pair_half(g, recv, core, name, tb=256, by_chip=None):
    nk, r, c = g.shape
    tb = min(tb, r // 2)
    nb = r // 2 // tb

    def body(core_ref, g_ref, r_ref, o_ref):
        s = g_ref[...].astype(f32) + r_ref[...].astype(f32)
        if by_chip is None:
            o_ref[...] = s.astype(bf16)
        else:
            for k in range(by_chip[0]):
                o_ref[k] = s[:, k * by_chip[1]:(k + 1) * by_chip[1]].astype(bf16)

    if by_chip is None:
        out_spec = pl.BlockSpec((None, tb, c), lambda k, i, core_ref: (k, i, 0))
        out_shape = _sds((nk, r // 2, c), bf16)
    else:
        assert nk == 1
        out_spec = pl.BlockSpec((by_chip[0], tb, by_chip[1]), lambda k, i, core_ref: (0, i, 0))
        out_shape = _sds((by_chip[0], r // 2, by_chip[1]), bf16)
    return pl.pallas_call(
        body, name=name,
        grid_spec=pltpu.PrefetchScalarGridSpec(
            num_scalar_prefetch=1, grid=(nk, nb),
            in_specs=[pl.BlockSpec((None, tb, c), lambda k, i, core_ref: (k, core_ref[0] * nb + i, 0)),
                      pl.BlockSpec((None, tb, c), lambda k, i, core_ref: (k, i, 0))],
            out_specs=out_spec),
        out_shape=out_shape, compiler_params=_cparams(2))(jnp.reshape(core, (1,)).astype(jnp.int32), g, recv)


def assemble_columns(blocks, width, name, tb=256):
    n, r, c = blocks.shape

    def body(b_ref, o_ref):
        for k in range(n):
            o_ref[:, k * c:(k + 1) * c] = b_ref[k]
        o_ref[:, n * c:] = jnp.zeros((tb, width - n * c), blocks.dtype)

    return pl.pallas_call(
        body, name=name, grid=(r // tb,), in_specs=[pl.BlockSpec((n, tb, c), lambda i: (0, i, 0))],
        out_specs=pl.BlockSpec((tb, width), lambda i: (i, 0)), out_shape=_sds((r, width), blocks.dtype),
        compiler_params=_cparams(1))(blocks)


def chip_sum_into(acc, layer, own, others, chip, name, tb=256):
    n, r, c = own.shape
    tb = min(tb, r)

    def body(chip_ref, x_ref, y1_ref, y2_ref, y3_ref, acc_ref, o_ref):
        o_ref[...] = ((x_ref[...].astype(f32) + y1_ref[...].astype(f32)) + y2_ref[...].astype(f32)) + y3_ref[...].astype(f32)

    def slot(k):
        return pl.BlockSpec((None, tb, c), lambda i, chip_ref: (chip_ref[0] ^ k, i, 0))

    return pl.pallas_call(
        body, name=name,
        grid_spec=pltpu.PrefetchScalarGridSpec(
            num_scalar_prefetch=1, grid=(r // tb,),
            in_specs=[slot(k) for k in range(n)] + [pl.BlockSpec(memory_space=pl.ANY)],
            out_specs=pl.BlockSpec((None, tb, c), lambda i, chip_ref: (layer, i, 0))),
        out_shape=_sds(acc.shape, f32), input_output_aliases={n + 1: 0}, compiler_params=_cparams(1))(
            jnp.reshape(chip, (1,)).astype(jnp.int32), own, *([others] * (n - 1)), acc)


def adamw_half(w, g_half, m, v, half, name, before=None, token=None, tb=256):
    depth, r, c = w.shape
    tb = min(tb, r // 2)
    nb = r // 2 // tb
    n_extra = (0 if before is None else 4) + len(_token_arg(token))

    def body(half_ref, w_ref, gh_ref, m_ref, v_ref, *rest):
        g_ref, d_ref, mo_ref, vo_ref = rest[n_extra:]
        gv = gh_ref[...]
        m2 = B1 * m_ref[...] + (1.0 - B1) * gv
        v2 = B2 * v_ref[...] + (1.0 - B2) * (gv * gv)
        m_hat = m2 / (1.0 - B1 ** STEP)
        v_hat = v2 / (1.0 - B2 ** STEP)
        g_ref[...] = gv
        d_ref[...] = -LR * (m_hat / (jnp.sqrt(v_hat) + AEPS) + WD * w_ref[...])
        mo_ref[...] = m2
        vo_ref[...] = v2

    whole = pl.BlockSpec((None, tb, c), lambda l, i, half_ref: (l, half_ref[0] * nb + i, 0))
    part = pl.BlockSpec((None, tb, c), lambda l, i, half_ref: (l, i, 0))
    extra = ([] if before is None else list(before)) + _token_arg(token)
    return pl.pallas_call(
        body, name=name,
        grid_spec=pltpu.PrefetchScalarGridSpec(
            num_scalar_prefetch=1, grid=(depth, nb),
            in_specs=[whole, part, whole, whole] + [pl.BlockSpec(memory_space=pl.ANY)] * n_extra, out_specs=[whole] * 4),
        out_shape=[_sds(w.shape, f32)] * 4,
        input_output_aliases={} if before is None else {5 + k: k for k in range(4)},
        compiler_params=_cparams(2))(jnp.reshape(half, (1,)).astype(jnp.int32), w, g_half, m, v, *extra)


def sum_slots(y, out_dtype, name, tb=256):
    n, r, c = y.shape
    tb = min(tb, r)

    def body(y_ref, o_ref):
        acc = y_ref[0].astype(f32)
        for i in range(1, n):
            acc = acc + y_ref[i].astype(f32)
        o_ref[...] = acc.astype(out_dtype)

    return pl.pallas_call(
        body, name=name, grid=(r // tb,),
        in_specs=[pl.BlockSpec((n, tb, c), lambda i: (0, i, 0))], out_specs=pl.BlockSpec((tb, c), lambda i: (i, 0)),
        out_shape=_sds((r, c), out_dtype), compiler_params=_cparams(1))(y)


def adamw(w, g, m, v, name, tb=256):
    r, c = w.shape
    tb = min(tb, r)

    def body(w_ref, g_ref, m_ref, v_ref, d_ref, mo_ref, vo_ref):
        gv = g_ref[...]
        m2 = B1 * m_ref[...] + (1.0 - B1) * gv
        v2 = B2 * v_ref[...] + (1.0 - B2) * (gv * gv)
        m_hat = m2 / (1.0 - B1 ** STEP)
        v_hat = v2 / (1.0 - B2 ** STEP)
        d_ref[...] = -LR * (m_hat / (jnp.sqrt(v_hat) + AEPS) + WD * w_ref[...])
        mo_ref[...] = m2
        vo_ref[...] = v2

    spec = pl.BlockSpec((tb, c), lambda i: (i, 0))
    return pl.pallas_call(
        body, name=name, grid=(r // tb,), in_specs=[spec] * 4, out_specs=[spec] * 3,
        out_shape=[_sds((r, c), f32)] * 3, compiler_params=_cparams(1))(w, g, m, v)


def adamw_leading(w, g, m, v, name, tc=64):
    c, l, r = w.shape
    main = c // tc
    tail = c - main * tc

    def body(w_ref, g_ref, m_ref, v_ref, *rest):
        d_ref, mo_ref, vo_ref = rest[-3:]
        gv = g_ref[...]
        m2 = B1 * m_ref[...] + (1.0 - B1) * gv
        v2 = B2 * v_ref[...] + (1.0 - B2) * (gv * gv)
        m_hat = m2 / (1.0 - B1 ** STEP)
        v_hat = v2 / (1.0 - B2 ** STEP)
        d_ref[...] = -LR * (m_hat / (jnp.sqrt(v_hat) + AEPS) + WD * w_ref[...])
        mo_ref[...] = m2
        vo_ref[...] = v2

    spec = pl.BlockSpec((tc, l, r), lambda i: (i, 0, 0))
    outs = pl.pallas_call(
        functools.partial(body), name=name, grid=(main,), in_specs=[spec] * 4, out_specs=[spec] * 3,
        out_shape=[_sds(w.shape, f32)] * 3, compiler_params=_cparams(1))(w, g, m, v)
    if tail:
        assert (main * tc) % tail == 0
        last = pl.BlockSpec((tail, l, r), lambda i: (main * tc // tail, 0, 0))
        outs = pl.pallas_call(
            functools.partial(body), name=name + "_tail", grid=(1,),
            in_specs=[last] * 4 + [pl.BlockSpec(memory_space=pl.ANY)] * 3, out_specs=[last] * 3,
            out_shape=[_sds(w.shape, f32)] * 3, input_output_aliases={4: 0, 5: 1, 6: 2},
            compiler_params=_cparams(1))(w, g, m, v, *outs)
    return outs


SMALL_ROW = 1024
SMALL_GAINS = ("g1", "ga", "gs", "g2", "g3", "g4")
SMALL_LAYER_ROWS = 8 + 8 + 16 + 8


def _pack_small(grads):
    wide = lambda a: jnp.pad(a, ((0, 0), (0, 2 * SMALL_ROW - a.shape[1]))).reshape(-1, SMALL_ROW)
    row = lax.broadcasted_iota(jnp.int32, (8, SMALL_ROW), 0)
    parts = []
    for g in grads:
        singles = [g[k] for k in SMALL_GAINS] + [g["bs"][:, :SMALL_ROW],
                                                 jnp.pad(g["bs"][:, SMALL_ROW:], ((0, 0), (0, 2 * SMALL_ROW - XBC)))]
        first = sum(jnp.where(row == k, s, 0.0) for k, s in enumerate(singles))
        parts += [first, g["wa"], wide(g["ws"]), jnp.pad(g["par"], ((0, 0), (0, SMALL_ROW - CH)))]
    return jnp.concatenate(parts, axis=0)


def _unpack_small(packed, depth):
    rows = packed.reshape(depth, SMALL_LAYER_ROWS, SMALL_ROW)
    out = {k: rows[:, i] for i, k in enumerate(SMALL_GAINS)}
    out["bs"] = rows[:, 6:8].reshape(depth, 2 * SMALL_ROW)[:, :XBC]
    out["wa"] = rows[:, 8:11]
    out["ws"] = rows[:, 16:32].reshape(depth, 8, 2 * SMALL_ROW)[:, :4, :XBC]
    out["par"] = rows[:, 32:35, :CH]
    return out


def kernel(x, norm_mix_pre, w_in, conv_a_w, ssm_conv_w, ssm_conv_b, dt_bias, a_log, d_skip, conv_out_norm, ssm_out_norm, w_out, norm_mix_post, norm_mlp_pre, w_up, w_down, norm_mlp_post, loss_target, m_norm_mix_pre, m_w_in, m_conv_a_w, m_ssm_conv_w, m_ssm_conv_b, m_dt_bias, m_a_log, m_d_skip, m_conv_out_norm, m_ssm_out_norm, m_w_out, m_norm_mix_post, m_norm_mlp_pre, m_w_up, m_w_down, m_norm_mlp_post, v_norm_mix_pre, v_w_in, v_conv_a_w, v_ssm_conv_w, v_ssm_conv_b, v_dt_bias, v_a_log, v_d_skip, v_conv_out_norm, v_ssm_out_norm, v_w_out, v_norm_mix_post, v_norm_mlp_pre, v_w_up, v_w_down, v_norm_mlp_post):
    nb, seq, _ = x.shape
    t = nb * seq
    depth = w_in.shape[0]
    ncol = w_in.shape[2]
    chip = 2 * lax.axis_index("x") + lax.axis_index("y")

    taps = [conv_a_w, ssm_conv_w]
    taps_g = all_gather(taps, "chips", "gather_taps", slot_axis=1, own=False)
    wa_g, ws_g = [lax.dynamic_update_index_in_dim(g, s, chip, 1) for g, s in zip(taps_g, taps)]
    wa_full = jnp.transpose(wa_g, (0, 2, 1, 3)).reshape(depth, 3, D)
    ws_full = jnp.transpose(ws_g, (0, 2, 1, 3)).reshape(depth, 4, XBC)
    lane_pad = lambda a: jnp.pad(a, ((0, 0), (0, CH - a.shape[1])))
    par = jnp.stack([lane_pad(dt_bias), lane_pad(a_log), lane_pad(d_skip)], axis=1)
    par = jnp.pad(par, ((0, 0), (0, 5), (0, 0)))

    layer_shards = lambda l: [w_in[l].astype(bf16), w_out[l].astype(bf16), w_up[l].astype(bf16), w_down[l].astype(bf16)]
    issued = []

    def start(shards, name, whole=False):
        zones = [lax.empty((N_CHIPS,) + s.shape, s.dtype) for s in shards]
        issued.append(chips_start(shards, zones, _whole_views(shards) if whole else _weight_views(shards), name,
                                  after=issued[-1]["token"] if issued else taps_g[0]))
        issued[-1]["whole"] = whole
        return issued[-1]

    def finish(started, after, name):
        shards, zones = chips_wait(started, after, name)
        if not started["whole"]:
            zones = weights_share(zones, "weights_share")
        return [lax.dynamic_update_index_in_dim(z, s, chip, 0) for z, s in zip(zones, shards)]

    def shaped(mats):
        wo_z, wu_z, wd_z = mats
        return wo_z.reshape(2 * D, D), wu_z, wd_z.reshape(DFF, D)

    first = layer_shards(0)
    travelling = {0: start(first[:1], "weights_start_0")}
    rest = start(first[1:], "weights_start_0_rest")
    for l in range(1, depth):
        travelling[l] = start(layer_shards(l), f"weights_start_{l}", whole=l >= 2)

    def weights_of(l, x_in):
        mats = finish(travelling.pop(l), x_in, f"weights_wait_{l}")
        w = dict(win=assemble_columns(mats[0], PROJ, "assemble_w_in"), wa=jnp.pad(wa_full[l], ((0, 5), (0, 0))),
                 ws=jnp.pad(ws_full[l], ((0, 4), (0, 0))), bs=ssm_conv_b[l][None], par=par[l],
                 g1=norm_mix_pre[l][None], ga=conv_out_norm[l][None], gs=ssm_out_norm[l][None],
                 g2=norm_mix_post[l][None], g3=norm_mlp_pre[l][None], g4=norm_mlp_post[l][None])
        if l == 0:
            w["token"] = issued[-1]["token"]
            w["late"] = lambda after: dict(zip(("wo", "wu", "wd"), shaped(finish(rest, after, "weights_wait_0_rest"))))
        else:
            w.update(zip(("wo", "wu", "wd"), shaped(mats[1:])))
        return w

    core = lax.axis_index("c")
    grads_travelling = {}
    given_m = dict(win=m_w_in, wo=m_w_out, wu=m_w_up, wd=m_w_down)
    given_v = dict(win=v_w_in, wo=v_w_out, wu=v_w_up, wd=v_w_down)

    chip_major = dict(win=lambda a: a[None], wo=lambda a: a.reshape(N_CHIPS, 2 * D // N_CHIPS, D), wu=lambda a: a,
                      wd=lambda a: a.reshape(N_CHIPS, DFF // N_CHIPS, D))
    held = {}

    to_pair = {}

    def pair_sums_to_chips(l, keys, mats, received, after=None):
        sums = [sum_pair_half(m_, r_, core, "pair_sum", by_chip=(N_CHIPS, ncol) if k == "win" else None)
                for k, m_, r_ in zip(keys, mats, received)]
        zones = [lax.empty(s.shape, s.dtype) for s in sums]
        started = chips_start(sums, zones, _grad_views(sums), f"grads_start_{l}_{len(grads_travelling)}", after=after)
        grads_travelling[(l, keys[0])] = (keys, started)
        return started["token"]

    def grads_done(l, g, last):
        token = None
        if not last and l + 1 in to_pair:
            keys, started = to_pair.pop(l + 1)
            mats, received = chips_wait(started, g["wo"], f"grads_to_pair_wait_{l + 1}")
            token = pair_sums_to_chips(l + 1, keys, mats, received)
        if l > 0 and not last:
            held[l] = g
            return token
        g = {**held.pop(l, {}), **g}
        keys = [k for k in ("win", "wo", "wu", "wd") if k in g]
        mats = [chip_major[k](g[k]) for k in keys]
        if l > 0:
            half = [m_.shape[1] // 2 for m_ in mats]
            views = dict(src=lambda ref, a, c, to: ref.at[:, pl.ds(to * half[a], half[a])],
                         dst=lambda ref, a, c, sender: ref, rows=lambda a: mats[a].shape[0])
            zones = [lax.empty((m_.shape[0], h, m_.shape[2]), m_.dtype) for m_, h in zip(mats, half)]
            to_pair[l] = (keys, chips_start(mats, zones, views, f"grads_to_pair_start_{l}", pieces=1, group="pair"))
            return to_pair[l][1]["token"]
        return pair_sums_to_chips(l, keys, mats, pair_send_halves(mats, "grads_to_pair"), after=token)

    sse, dx, grads = local_step(x.reshape(t, D), loss_target.reshape(t, D), depth, weights_of, seq, grads_done)
    loss = lax.psum(0.5 / D * sse[0, 0], ("x", "y", "c"))

    packed = _pack_small(grads)
    small_travelling = chips_start([packed], [lax.empty((8,) + packed.shape, f32)], _whole_views([packed]),
                                   "small_start", group="all")

    big_w = dict(win=w_in, wo=w_out, wu=w_up, wd=w_down)
    acc = {k: lax.empty((depth, bw.shape[1] // 2, bw.shape[2]), f32) for k, bw in big_w.items()}
    for n, ((l, _), (keys, started)) in enumerate(grads_travelling.items()):
        sums, zones = chips_wait(started, small_travelling["token"], f"grads_wait_{l}_{n}")
        for k, s, z in zip(keys, sums, zones):
            acc[k] = chip_sum_into(acc[k], l, s, z, chip, "chip_sum")
    names = ("win", "wo", "wu", "wd")
    acc = [acc[k] for k in names]
    pair_views = dict(src=lambda ref, a, c, to: ref, dst=lambda ref, a, c, sender: ref, rows=lambda a: depth)
    to_sibling = chips_start(acc, [lax.empty(a.shape, f32) for a in acc], pair_views, "grads_from_pair_start",
                             pieces=depth, group="pair")
    own_done = {}
    for k, a in zip(names, to_sibling["sources"]):
        if big_w[k].shape[-1] % CH == 0:
            own_done[k] = adamw_half(big_w[k], a, given_m[k], given_v[k], core, "adamw_matrix",
                                     token=to_sibling["token"])

    (packed,), (small_all,) = chips_wait(small_travelling, own_done["wd"][1], "small_wait")
    small_all = lax.dynamic_update_index_in_dim(small_all, packed, 4 * lax.axis_index("x") + 2 * lax.axis_index("y") + core, 0)
    small = _unpack_small(sum_slots(small_all, f32, "small_sum", tb=8), depth)
    wa_cols, ws_cols = conv_a_w.shape[2], ssm_conv_w.shape[2]
    par_g = small["par"].reshape(depth, 3, CH)
    g_small = dict(
        norm_mix_pre=small["g1"], conv_out_norm=small["ga"], ssm_out_norm=small["gs"], norm_mix_post=small["g2"],
        norm_mlp_pre=small["g3"], norm_mlp_post=small["g4"], ssm_conv_b=small["bs"],
        conv_a_w=lax.dynamic_slice_in_dim(small["wa"].reshape(depth, 3, D), chip * wa_cols, wa_cols, axis=2),
        ssm_conv_w=lax.dynamic_slice_in_dim(small["ws"].reshape(depth, 4, XBC), chip * ws_cols, ws_cols, axis=2),
        dt_bias=par_g[:, 0, :NH], a_log=par_g[:, 1, :NH], d_skip=par_g[:, 2, :NH])

    given = dict(norm_mix_pre=(norm_mix_pre, m_norm_mix_pre, v_norm_mix_pre), w_in=(w_in, m_w_in, v_w_in),
                 conv_a_w=(conv_a_w, m_conv_a_w, v_conv_a_w), ssm_conv_w=(ssm_conv_w, m_ssm_conv_w, v_ssm_conv_w),
                 ssm_conv_b=(ssm_conv_b, m_ssm_conv_b, v_ssm_conv_b), dt_bias=(dt_bias, m_dt_bias, v_dt_bias),
                 a_log=(a_log, m_a_log, v_a_log), d_skip=(d_skip, m_d_skip, v_d_skip),
                 conv_out_norm=(conv_out_norm, m_conv_out_norm, v_conv_out_norm),
                 ssm_out_norm=(ssm_out_norm, m_ssm_out_norm, v_ssm_out_norm), w_out=(w_out, m_w_out, v_w_out),
                 norm_mix_post=(norm_mix_post, m_norm_mix_post, v_norm_mix_post),
                 norm_mlp_pre=(norm_mlp_pre, m_norm_mlp_pre, v_norm_mlp_pre), w_up=(w_up, m_w_up, v_w_up),
                 w_down=(w_down, m_w_down, v_w_down), norm_mlp_post=(norm_mlp_post, m_norm_mlp_post, v_norm_mlp_post))
    order = ["norm_mix_pre", "w_in", "conv_a_w", "ssm_conv_w", "ssm_conv_b", "dt_bias", "a_log", "d_skip",
             "conv_out_norm", "ssm_out_norm", "w_out", "norm_mix_post", "norm_mlp_pre", "w_up", "w_down",
             "norm_mlp_post"]
    short = dict(w_in="win", w_out="wo", w_up="wu", w_down="wd")
    results = {}
    for n in order:
        if n in short:
            continue
        wv, mv, vv = given[n]
        gv = g_small[n].reshape(wv.shape)
        two_d = lambda a: a.reshape(-1, a.shape[-1])
        results[n] = (gv,) + tuple(adamw(two_d(wv), two_d(gv), two_d(mv), two_d(vv), "adamw"))

    acc, from_sibling = chips_wait(to_sibling, results["norm_mlp_post"][1], "grads_from_pair_wait")
    for n, k in short.items():
        wv, mv, vv = given[n]
        own, recv = acc[names.index(k)], from_sibling[names.index(k)]
        if k in own_done:
            results[n] = adamw_half(wv, recv, mv, vv, 1 - core, "adamw_matrix", before=own_done[k])
        else:
            to_cols, to_rows = (lambda a: jnp.transpose(a, (2, 0, 1))), (lambda a: jnp.transpose(a, (1, 2, 0)))
            own_c, recv_c = to_cols(own), to_cols(recv)
            g_cols = jnp.where(core == 0, jnp.concatenate([own_c, recv_c], axis=2),
                               jnp.concatenate([recv_c, own_c], axis=2))
            results[n] = tuple(to_rows(o) for o in (g_cols,) + tuple(adamw_leading(to_cols(wv), g_cols, to_cols(mv),
                                                                                   to_cols(vv), "adamw_cols")))
    g_out, d_out, m_out, v_out = [], [], [], []
    for n in order:
        wv = given[n][0]
        gv, dlt, m2, v2 = results[n]
        g_out.append(gv.reshape(wv.shape))
        d_out.append(dlt.reshape(wv.shape))
        m_out.append(m2.reshape(wv.shape))
        v_out.append(v2.reshape(wv.shape))
    return (loss, dx.reshape(nb, seq, D), *g_out, *d_out, *m_out, *v_out)
```
